```python
import jax, jax.numpy as jnp
from jax import lax
import numpy as np

D_MODEL = 2048
BATCH = 8
SEQ = 2048
DEPTH = 1

CHUNK = 64
HG_HEADS = 8
HG_DK = 128
HG_DV = 128
HG_WIDTH = HG_HEADS * HG_DV
AT_HEADS = 16
AT_DH = 64
AT_WIDTH = AT_HEADS * AT_DH
LEFT_CHUNKS = 8
BAND = (LEFT_CHUNKS + 1) * CHUNK
REL_CLIP = 256
N_REL = 2 * REL_CLIP + 1
D_FF = 4 * D_MODEL
N_BRANCH = 2
EPS = 1e-6
D_IN = 4 * HG_WIDTH + 3 * AT_WIDTH + N_BRANCH * D_MODEL
SPLIT_POINTS = (HG_WIDTH, 2 * HG_WIDTH, 3 * HG_WIDTH, 4 * HG_WIDTH,
                4 * HG_WIDTH + AT_WIDTH, 4 * HG_WIDTH + 2 * AT_WIDTH, 4 * HG_WIDTH + 3 * AT_WIDTH)

kernel_name = "hybrid_hgrn2_chunkattn_gated_block"


def rms_norm(x, w):
    xf = x.astype(jnp.float32)
    y = xf * lax.rsqrt(jnp.mean(xf * xf, axis=-1, keepdims=True) + EPS)
    return (y * w.astype(jnp.float32)).astype(x.dtype)


def hgrn2_scan(q, k, v, log_g):
    B, H, T, DK = q.shape
    DV = v.shape[-1]
    nc = T // CHUNK

    def to_chunks(a):
        return a.reshape(B, H, nc, CHUNK, a.shape[-1]).transpose(2, 0, 1, 3, 4)

    causal = jnp.tril(jnp.ones((CHUNK, CHUNK), dtype=bool))

    def step(S, inp):
        qc, kc, vc, gc = inp
        b = jnp.cumsum(gc, axis=2)
        o_inter = jnp.einsum('bhtk,bhkv->bhtv', qc * jnp.exp(b), S)
        diff = b[:, :, :, None, :] - b[:, :, None, :, :]
        decay = jnp.exp(jnp.where(causal[:, :, None], diff, -jnp.inf))
        scores = jnp.einsum('bhtk,bhtsk,bhsk->bhts', qc, decay, kc)
        o_intra = jnp.einsum('bhts,bhsv->bhtv', scores, vc)
        b_last = b[:, :, -1:, :]
        S_new = jnp.exp(b_last[:, :, 0, :])[..., None] * S + jnp.einsum(
            'bhsk,bhsv->bhkv', kc * jnp.exp(b_last - b), vc)
        return S_new, o_inter + o_intra

    S0 = jnp.zeros((B, H, DK, DV), jnp.float32)
    _, o = lax.scan(step, S0, (to_chunks(q), to_chunks(k), to_chunks(v), to_chunks(log_g)))
    return o.transpose(1, 2, 0, 3, 4).reshape(B, H, T, DV)


def chunk_band_attention(q, k, v, rel_bias):
    B, H, T, Dh = q.shape
    nc = T // CHUNK
    pad = LEFT_CHUNKS * CHUNK
    kp = jnp.pad(k, ((0, 0), (0, 0), (pad, 0), (0, 0)))
    vp = jnp.pad(v, ((0, 0), (0, 0), (pad, 0), (0, 0)))
    idx = (jnp.arange(nc) * CHUNK)[:, None] + jnp.arange(BAND)[None, :]
    kb = kp[:, :, idx, :]
    vb = vp[:, :, idx, :]
    qc = q.reshape(B, H, nc, CHUNK, Dh)
    valid = idx >= pad
    t = jnp.arange(CHUNK)
    j = jnp.arange(BAND)
    rel = t[:, None] + pad - j[None, :]
    rel_idx = jnp.clip(rel, -REL_CLIP, REL_CLIP) + REL_CLIP
    bias = rel_bias[:, rel_idx].astype(jnp.float32)
    s = jnp.einsum('bhnqd,bhnkd->bhnqk', qc, kb).astype(jnp.float32) * (Dh ** -0.5) + bias[:, None]
    s = jnp.where(valid[:, None, :], s, -jnp.inf)
    p = jax.nn.softmax(s, axis=-1).astype(v.dtype)
    o = jnp.einsum('bhnqk,bhnkd->bhnqd', p, vb)
    return o.reshape(B, H, T, Dh)


def mixer_block(u, w_in, lb, hg_norm_w, rel_bias, w_branch_a, w_branch_b, w_out):
    B, T, _ = u.shape
    z = u @ w_in
    hq, hf, hi, hg, aq, ak, av, gates = jnp.split(z, SPLIT_POINTS, axis=-1)

    def to_heads(a, h):
        return a.reshape(B, T, h, -1).transpose(0, 2, 1, 3)

    f = jax.nn.sigmoid(hf.astype(jnp.float32))
    g = lb + (1.0 - lb) * f
    log_g = jnp.log(g)
    kk = 1.0 - g
    q = jax.nn.silu(hq.astype(jnp.float32)) * (HG_DK ** -0.5)
    o = hgrn2_scan(to_heads(q, HG_HEADS), to_heads(kk, HG_HEADS),
                   to_heads(hi.astype(jnp.float32), HG_HEADS), to_heads(log_g, HG_HEADS))
    o = o.transpose(0, 2, 1, 3)
    o = rms_norm(o, hg_norm_w) * jax.nn.silu(hg.reshape(B, T, HG_HEADS, HG_DV).astype(jnp.float32))
    y_a = o.reshape(B, T, HG_WIDTH).astype(u.dtype)

    ya = chunk_band_attention(to_heads(aq, AT_HEADS), to_heads(ak, AT_HEADS),
                              to_heads(av, AT_HEADS), rel_bias)
    y_b = ya.transpose(0, 2, 1, 3).reshape(B, T, AT_WIDTH)

    gate_a, gate_b = jnp.split(jax.nn.sigmoid(gates), 2, axis=-1)
    merged = gate_a * (y_a @ w_branch_a) + gate_b * (y_b @ w_branch_b)
    return merged @ w_out


def _fwd_setup_inputs(seed: int = 0) -> dict:
    key = jax.random.key(seed)
    ks = jax.random.split(key, 13)
    f32 = jnp.float32
    x = jax.random.normal(ks[0], (BATCH, SEQ, D_MODEL), f32)
    w_in = jax.random.normal(ks[1], (DEPTH, D_MODEL, D_IN), f32) * D_MODEL ** -0.5
    lb_logits = jax.random.normal(ks[2], (DEPTH + 1, HG_WIDTH), f32)
    hg_norm_w = 1.0 + 0.02 * jax.random.normal(ks[3], (DEPTH, HG_DV), f32)
    rel_bias = 0.1 * jax.random.normal(ks[4], (DEPTH, AT_HEADS, N_REL), f32)
    w_branch_a = jax.random.normal(ks[5], (DEPTH, HG_WIDTH, D_MODEL), f32) * HG_WIDTH ** -0.5
    w_branch_b = jax.random.normal(ks[6], (DEPTH, AT_WIDTH, D_MODEL), f32) * AT_WIDTH ** -0.5
    w_out = jax.random.normal(ks[7], (DEPTH, D_MODEL, D_MODEL), f32) * D_MODEL ** -0.5
    norm_mix_w = 1.0 + 0.02 * jax.random.normal(ks[8], (DEPTH, D_MODEL), f32)
    norm_mlp_w = 1.0 + 0.02 * jax.random.normal(ks[9], (DEPTH, D_MODEL), f32)
    w_up = jax.random.normal(ks[10], (DEPTH, D_MODEL, D_FF), f32) * D_MODEL ** -0.5
    w_down = jax.random.normal(ks[11], (DEPTH, D_FF, D_MODEL), f32) * D_FF ** -0.5
    norm_final_w = 1.0 + 0.02 * jax.random.normal(ks[12], (D_MODEL,), f32)
    return {"x": x, "w_in": w_in, "lb_logits": lb_logits, "hg_norm_w": hg_norm_w,
            "rel_bias": rel_bias, "w_branch_a": w_branch_a, "w_branch_b": w_branch_b,
            "w_out": w_out, "norm_mix_w": norm_mix_w, "norm_mlp_w": norm_mlp_w,
            "w_up": w_up, "w_down": w_down, "norm_final_w": norm_final_w}


def _fwd_reference(x, w_in, lb_logits, hg_norm_w, rel_bias, w_branch_a, w_branch_b, w_out,
              norm_mix_w, norm_mlp_w, w_up, w_down, norm_final_w):
    lb_all = jnp.cumsum(jax.nn.softmax(lb_logits.astype(jnp.float32), axis=0), axis=0)
    h = x
    for l in range(DEPTH):
        h = h + mixer_block(rms_norm(h, norm_mix_w[l]), w_in[l], lb_all[l], hg_norm_w[l],
                            rel_bias[l], w_branch_a[l], w_branch_b[l], w_out[l])
        u = rms_norm(h, norm_mlp_w[l])
        h = h + jnp.square(jax.nn.relu(u @ w_up[l])) @ w_down[l]
    return rms_norm(h, norm_final_w)


import jax as _jax
import jax.numpy as _jnp

TWIN_FORMAT = 'train_step'
FWD_PARAMS = ['x', 'w_in', 'lb_logits', 'hg_norm_w', 'rel_bias', 'w_branch_a', 'w_branch_b', 'w_out', 'norm_mix_w', 'norm_mlp_w', 'w_up', 'w_down', 'norm_final_w']
TWIN_WEIGHTS = ['w_in', 'lb_logits', 'hg_norm_w', 'rel_bias', 'w_branch_a', 'w_branch_b', 'w_out', 'norm_mix_w', 'norm_mlp_w', 'w_up', 'w_down', 'norm_final_w']
TWIN_DIFF_INPUT = 'x'
TWIN_INPUTS = ['x', 'w_in', 'lb_logits', 'hg_norm_w', 'rel_bias', 'w_branch_a', 'w_branch_b', 'w_out', 'norm_mix_w', 'norm_mlp_w', 'w_up', 'w_down', 'norm_final_w', 'loss_target', 'm_w_in', 'm_lb_logits', 'm_hg_norm_w', 'm_rel_bias', 'm_w_branch_a', 'm_w_branch_b', 'm_w_out', 'm_norm_mix_w', 'm_norm_mlp_w', 'm_w_up', 'm_w_down', 'm_norm_final_w', 'v_w_in', 'v_lb_logits', 'v_hg_norm_w', 'v_rel_bias', 'v_w_branch_a', 'v_w_branch_b', 'v_w_out', 'v_norm_mix_w', 'v_norm_mlp_w', 'v_w_up', 'v_w_down', 'v_norm_final_w']
TWIN_OUTPUTS = ['loss', 'grad_x', 'grad_w_in', 'grad_lb_logits', 'grad_hg_norm_w', 'grad_rel_bias', 'grad_w_branch_a', 'grad_w_branch_b', 'grad_w_out', 'grad_norm_mix_w', 'grad_norm_mlp_w', 'grad_w_up', 'grad_w_down', 'grad_norm_final_w', 'delta_w_in', 'delta_lb_logits', 'delta_hg_norm_w', 'delta_rel_bias', 'delta_w_branch_a', 'delta_w_branch_b', 'delta_w_out', 'delta_norm_mix_w', 'delta_norm_mlp_w', 'delta_w_up', 'delta_w_down', 'delta_norm_final_w', 'new_m_w_in', 'new_m_lb_logits', 'new_m_hg_norm_w', 'new_m_rel_bias', 'new_m_w_branch_a', 'new_m_w_branch_b', 'new_m_w_out', 'new_m_norm_mix_w', 'new_m_norm_mlp_w', 'new_m_w_up', 'new_m_w_down', 'new_m_norm_final_w', 'new_v_w_in', 'new_v_lb_logits', 'new_v_hg_norm_w', 'new_v_rel_bias', 'new_v_w_branch_a', 'new_v_w_branch_b', 'new_v_w_out', 'new_v_norm_mix_w', 'new_v_norm_mlp_w', 'new_v_w_up', 'new_v_w_down', 'new_v_norm_final_w']
TWIN_LEAF_KINDS = {'loss': 'loss', 'grad_x': 'grad_x', 'grad_w_in': 'grad_w', 'grad_lb_logits': 'grad_w', 'grad_hg_norm_w': 'grad_w', 'grad_rel_bias': 'grad_w', 'grad_w_branch_a': 'grad_w', 'grad_w_branch_b': 'grad_w', 'grad_w_out': 'grad_w', 'grad_norm_mix_w': 'grad_w', 'grad_norm_mlp_w': 'grad_w', 'grad_w_up': 'grad_w', 'grad_w_down': 'grad_w', 'grad_norm_final_w': 'grad_w', 'delta_w_in': 'delta_w', 'delta_lb_logits': 'delta_w', 'delta_hg_norm_w': 'delta_w', 'delta_rel_bias': 'delta_w', 'delta_w_branch_a': 'delta_w', 'delta_w_branch_b': 'delta_w', 'delta_w_out': 'delta_w', 'delta_norm_mix_w': 'delta_w', 'delta_norm_mlp_w': 'delta_w', 'delta_w_up': 'delta_w', 'delta_w_down': 'delta_w', 'delta_norm_final_w': 'delta_w', 'new_m_w_in': 'new_m', 'new_m_lb_logits': 'new_m', 'new_m_hg_norm_w': 'new_m', 'new_m_rel_bias': 'new_m', 'new_m_w_branch_a': 'new_m', 'new_m_w_branch_b': 'new_m', 'new_m_w_out': 'new_m', 'new_m_norm_mix_w': 'new_m', 'new_m_norm_mlp_w': 'new_m', 'new_m_w_up': 'new_m', 'new_m_w_down': 'new_m', 'new_m_norm_final_w': 'new_m', 'new_v_w_in': 'new_v', 'new_v_lb_logits': 'new_v', 'new_v_hg_norm_w': 'new_v', 'new_v_rel_bias': 'new_v', 'new_v_w_branch_a': 'new_v', 'new_v_w_branch_b': 'new_v', 'new_v_w_out': 'new_v', 'new_v_norm_mix_w': 'new_v', 'new_v_norm_mlp_w': 'new_v', 'new_v_w_up': 'new_v', 'new_v_w_down': 'new_v', 'new_v_norm_final_w': 'new_v'}


def _forward(args):
    return _fwd_reference(*[args[k] for k in FWD_PARAMS])


def _output_shape():
    out = _jax.eval_shape(lambda: _forward(_fwd_setup_inputs(0)))
    return out.shape, out.dtype

N_MICROBATCH = 1
ADAM_LR = 0.001
ADAM_B1 = 0.9
ADAM_B2 = 0.999
ADAM_EPS = 1e-08
ADAM_WD = 0.01
ADAM_STEP = 10
PER_EXAMPLE_BATCH_AXIS = {'x': 0, 'loss_target': 0}
SHARED_INPUTS = []
_WEIGHT_DTYPES = {'w_in': _jnp.float32, 'lb_logits': _jnp.float32, 'hg_norm_w': _jnp.float32, 'rel_bias': _jnp.float32, 'w_branch_a': _jnp.float32, 'w_branch_b': _jnp.float32, 'w_out': _jnp.float32, 'norm_mix_w': _jnp.float32, 'norm_mlp_w': _jnp.float32, 'w_up': _jnp.float32, 'w_down': _jnp.float32, 'norm_final_w': _jnp.float32}
MOMENT_SCALE = {'w_in': 1.399531e-02, 'lb_logits': 2.265415e-03, 'hg_norm_w': 9.367881e-02, 'rel_bias': 2.482857e-03, 'w_branch_a': 2.092729e-02, 'w_branch_b': 5.187483e-03, 'w_out': 2.151684e-02, 'norm_mix_w': 3.261116e-02, 'norm_mlp_w': 5.474841e-02, 'w_up': 2.741104e-02, 'w_down': 5.176197e-02, 'norm_final_w': 8.075568e+00}


def _to_microbatches(a, axis):
    t = _jnp.moveaxis(a, axis, 0)
    t = t.reshape((N_MICROBATCH, t.shape[0] // N_MICROBATCH) + t.shape[1:])
    return _jnp.moveaxis(t, 1, axis + 1)


def setup_inputs(seed: int = 0) -> dict:
    inp = _fwd_setup_inputs(seed)
    key = _jax.random.fold_in(_jax.random.key(seed), 7919)
    shape, _ = _output_shape()
    out = dict(inp)
    out["loss_target"] = _jax.random.normal(_jax.random.fold_in(key, 0), shape, _jnp.float32)
    for i, name in enumerate(TWIN_WEIGHTS):
        w = inp[name].astype(_jnp.float32)
        if MOMENT_SCALE is None:
            s = _jnp.sqrt(_jnp.mean(_jnp.square(w)) + 1e-30)
        else:
            s = MOMENT_SCALE[name]
        km, kv = _jax.random.split(_jax.random.fold_in(key, i + 1))
        out[name] = w
        out["m_" + name] = s * _jax.random.normal(km, w.shape, _jnp.float32)
        out["v_" + name] = (s * s) * _jax.random.uniform(kv, w.shape, _jnp.float32, 0.5, 1.5)
    if N_MICROBATCH > 1:
        for name, axis in PER_EXAMPLE_BATCH_AXIS.items():
            out[name] = _to_microbatches(out[name], axis)
    return {'x': out['x'], 'w_in': out['w_in'], 'lb_logits': out['lb_logits'], 'hg_norm_w': out['hg_norm_w'], 'rel_bias': out['rel_bias'], 'w_branch_a': out['w_branch_a'], 'w_branch_b': out['w_branch_b'], 'w_out': out['w_out'], 'norm_mix_w': out['norm_mix_w'], 'norm_mlp_w': out['norm_mlp_w'], 'w_up': out['w_up'], 'w_down': out['w_down'], 'norm_final_w': out['norm_final_w'], 'loss_target': out['loss_target'], 'm_w_in': out['m_w_in'], 'm_lb_logits': out['m_lb_logits'], 'm_hg_norm_w': out['m_hg_norm_w'], 'm_rel_bias': out['m_rel_bias'], 'm_w_branch_a': out['m_w_branch_a'], 'm_w_branch_b': out['m_w_branch_b'], 'm_w_out': out['m_w_out'], 'm_norm_mix_w': out['m_norm_mix_w'], 'm_norm_mlp_w': out['m_norm_mlp_w'], 'm_w_up': out['m_w_up'], 'm_w_down': out['m_w_down'], 'm_norm_final_w': out['m_norm_final_w'], 'v_w_in': out['v_w_in'], 'v_lb_logits': out['v_lb_logits'], 'v_hg_norm_w': out['v_hg_norm_w'], 'v_rel_bias': out['v_rel_bias'], 'v_w_branch_a': out['v_w_branch_a'], 'v_w_branch_b': out['v_w_branch_b'], 'v_w_out': out['v_w_out'], 'v_norm_mix_w': out['v_norm_mix_w'], 'v_norm_mlp_w': out['v_norm_mlp_w'], 'v_w_up': out['v_w_up'], 'v_w_down': out['v_w_down'], 'v_norm_final_w': out['v_norm_final_w']}


def _loss(weights, diff, rest, loss_target):
    with _jax.named_scope("forward"):
        args = {**rest, TWIN_DIFF_INPUT: diff, **{k: w.astype(_WEIGHT_DTYPES[k]) for k, w in weights.items()}}
        y = _forward(args)
    with _jax.named_scope("loss_head"):
        err = _jnp.square(y.astype(_jnp.float32) - loss_target)
        return 0.5 * _jnp.sum(_jnp.mean(err, axis=-1)) if err.ndim else 0.5 * err


def _adamw(w, g, m, v):
    m = ADAM_B1 * m + (1.0 - ADAM_B1) * g
    v = ADAM_B2 * v + (1.0 - ADAM_B2) * _jnp.square(g)
    m_hat = m / (1.0 - ADAM_B1 ** ADAM_STEP)
    v_hat = v / (1.0 - ADAM_B2 ** ADAM_STEP)
    delta = -ADAM_LR * (m_hat / (_jnp.sqrt(v_hat) + ADAM_EPS) + ADAM_WD * w)
    return delta, m, v


def reference(x, w_in, lb_logits, hg_norm_w, rel_bias, w_branch_a, w_branch_b, w_out, norm_mix_w, norm_mlp_w, w_up, w_down, norm_final_w, loss_target, m_w_in, m_lb_logits, m_hg_norm_w, m_rel_bias, m_w_branch_a, m_w_branch_b, m_w_out, m_norm_mix_w, m_norm_mlp_w, m_w_up, m_w_down, m_norm_final_w, v_w_in, v_lb_logits, v_hg_norm_w, v_rel_bias, v_w_branch_a, v_w_branch_b, v_w_out, v_norm_mix_w, v_norm_mlp_w, v_w_up, v_w_down, v_norm_final_w):
    given = dict(x=x, w_in=w_in, lb_logits=lb_logits, hg_norm_w=hg_norm_w, rel_bias=rel_bias, w_branch_a=w_branch_a, w_branch_b=w_branch_b, w_out=w_out, norm_mix_w=norm_mix_w, norm_mlp_w=norm_mlp_w, w_up=w_up, w_down=w_down, norm_final_w=norm_final_w, loss_target=loss_target, m_w_in=m_w_in, m_lb_logits=m_lb_logits, m_hg_norm_w=m_hg_norm_w, m_rel_bias=m_rel_bias, m_w_branch_a=m_w_branch_a, m_w_branch_b=m_w_branch_b, m_w_out=m_w_out, m_norm_mix_w=m_norm_mix_w, m_norm_mlp_w=m_norm_mlp_w, m_w_up=m_w_up, m_w_down=m_w_down, m_norm_final_w=m_norm_final_w, v_w_in=v_w_in, v_lb_logits=v_lb_logits, v_hg_norm_w=v_hg_norm_w, v_rel_bias=v_rel_bias, v_w_branch_a=v_w_branch_a, v_w_branch_b=v_w_branch_b, v_w_out=v_w_out, v_norm_mix_w=v_norm_mix_w, v_norm_mlp_w=v_norm_mlp_w, v_w_up=v_w_up, v_w_down=v_w_down, v_norm_final_w=v_norm_final_w)
    weights = {n: given[n] for n in TWIN_WEIGHTS}
    shared = {n: given[n] for n in SHARED_INPUTS}
    per_example = {n: given[n] for n in ['x']}
    grad_fn = _jax.value_and_grad(_loss, argnums=(0, 1))

    def one_microbatch(ex, loss_target):
        ex = dict(ex)
        diff = ex.pop(TWIN_DIFF_INPUT)
        return grad_fn(weights, diff, {**shared, **ex}, loss_target)

    if N_MICROBATCH == 1:
        loss, (grad_w, grad_x) = one_microbatch(per_example, given["loss_target"])
    else:
        def body(carry, xs):
            loss_sum, grad_sum = carry
            l_k, (gw_k, gx_k) = one_microbatch(xs[0], xs[1])
            with _jax.named_scope("update"):
                return (loss_sum + l_k, _jax.tree.map(_jnp.add, grad_sum, gw_k)), gx_k

        init = (_jnp.zeros((), _jnp.float32), _jax.tree.map(_jnp.zeros_like, weights))
        (loss, grad_w), grad_x = _jax.lax.scan(body, init, (per_example, given["loss_target"]))
    with _jax.named_scope("update"):
        delta_w, new_m, new_v = {}, {}, {}
        for n in TWIN_WEIGHTS:
            delta_w[n], new_m[n], new_v[n] = _adamw(weights[n], grad_w[n], given["m_" + n], given["v_" + n])
    return (loss, grad_x, *[grad_w[n] for n in TWIN_WEIGHTS], *[delta_w[n] for n in TWIN_WEIGHTS],
            *[new_m[n] for n in TWIN_WEIGHTS], *[new_v[n] for n in TWIN_WEIGHTS])
```

```python
import functools

import jax
import jax.numpy as jnp
from jax import lax
from jax.experimental import pallas as pl
from jax.experimental.pallas import tpu as pltpu

F32 = jnp.float32
BF16 = jnp.bfloat16
HIGHEST = lax.Precision.HIGHEST
MESH = pl.DeviceIdType.MESH

D_MODEL = 2048
HG_HEADS = 8
HG_DK = 128
HG_WIDTH = 1024
AT_HEADS = 16
AT_DH = 64
AT_WIDTH = 1024
CHUNK = 64
LEFT_CHUNKS = 8
BAND = (LEFT_CHUNKS + 1) * CHUNK
PAD = LEFT_CHUNKS * CHUNK
REL_CLIP = 256
N_REL = 2 * REL_CLIP + 1
N_REL_PAD = 640
D_FF = 4 * D_MODEL
D_IN = 4 * HG_WIDTH + 3 * AT_WIDTH + 2 * D_MODEL
EPS = 1e-6
N_DEV = 8
N_CHIP = 4

ADAM_LR = 0.001
ADAM_B1 = 0.9
ADAM_B2 = 0.999
ADAM_EPS = 1e-08
ADAM_WD = 0.01
ADAM_STEP = 10

COL_HQ, COL_HF, COL_HI, COL_HG = 0, 8, 16, 24
COL_AQ, COL_AK, COL_AV = 32, 40, 48
COL_GATE_A, COL_GATE_B = 7, 9

VMEM_LIMIT = 56 * 1024 * 1024
SMALL_ROWS = 152


def _cparams(sem=None, **kw):
    if sem is not None:
        kw["dimension_semantics"] = sem
    return pltpu.CompilerParams(vmem_limit_bytes=VMEM_LIMIT, **kw)


def _pick(n, cands):
    for c in cands:
        if n % c == 0:
            return c
    return n


def _sigmoid(x):
    return 1.0 / (1.0 + jnp.exp(-x))


def _mm_nn(a, wb, out_dtype, name):
    M, K = a.shape
    NB, K2, Nb = wb.shape
    assert K == K2
    tm = min(M, 1024)
    tk = min(K, 2048)
    tn = _pick(Nb, (512, 1408, 256))
    nk = K // tk
    nn = Nb // tn

    def body(a_ref, b_ref, o_ref, *acc):
        part = jnp.dot(a_ref[...], b_ref[...], preferred_element_type=F32)
        if nk == 1:
            o_ref[...] = part.astype(out_dtype)
        else:
            acc_ref, = acc
            k = pl.program_id(3)

            @pl.when(k == 0)
            def _():
                acc_ref[...] = part

            @pl.when(k > 0)
            def _():
                acc_ref[...] += part

            @pl.when(k == nk - 1)
            def _():
                o_ref[...] = acc_ref[...].astype(out_dtype)

    return pl.pallas_call(
        body, name=name,
        grid=(M // tm, NB, nn, nk),
        in_specs=[pl.BlockSpec((tm, tk), lambda m, j, n, k: (m, k)),
                  pl.BlockSpec((None, tk, tn), lambda m, j, n, k: (j, k, n))],
        out_specs=pl.BlockSpec((tm, tn), lambda m, j, n, k: (m, j * nn + n)),
        out_shape=jax.ShapeDtypeStruct((M, NB * Nb), out_dtype),
        scratch_shapes=[] if nk == 1 else [pltpu.VMEM((tm, tn), F32)],
        compiler_params=_cparams(("parallel", "parallel", "parallel", "arbitrary")),
    )(a, wb)


def _mm_nt(a, wb, out_dtype, name):
    M, N = a.shape
    NB, K, Nb = wb.shape
    assert N == NB * Nb
    tm = min(M, 1024)
    tko = _pick(K, (1024,))
    tc = _pick(Nb, (1024, 1408, 256))
    nc = Nb // tc
    nsteps = NB * nc

    def body(a_ref, b_ref, o_ref, acc_ref):
        step = pl.program_id(2) * nc + pl.program_id(3)
        part = lax.dot_general(a_ref[...], b_ref[...], (((1,), (1,)), ((), ())), preferred_element_type=F32)

        @pl.when(step == 0)
        def _():
            acc_ref[...] = part

        @pl.when(step > 0)
        def _():
            acc_ref[...] += part

        @pl.when(step == nsteps - 1)
        def _():
            o_ref[...] = acc_ref[...].astype(out_dtype)

    return pl.pallas_call(
        body, name=name,
        grid=(M // tm, K // tko, NB, nc),
        in_specs=[pl.BlockSpec((tm, tc), lambda m, ko, j, c: (m, j * nc + c)),
                  pl.BlockSpec((None, tko, tc), lambda m, ko, j, c: (j, ko, c))],
        out_specs=pl.BlockSpec((tm, tko), lambda m, ko, j, c: (m, ko)),
        out_shape=jax.ShapeDtypeStruct((M, K), out_dtype),
        scratch_shapes=[pltpu.VMEM((tm, tko), F32)],
        compiler_params=_cparams(("parallel", "parallel", "arbitrary", "arbitrary")),
    )(a, wb)


def _mm_tn(a, g, nb, out_dtype, name):
    M, Ka = a.shape
    M2, N = g.shape
    assert M == M2 and N % nb == 0
    Nb = N // nb
    tka = _pick(Ka, (1024,))
    tn = _pick(Nb, (512, 1408, 256))
    nn = Nb // tn

    def body(a_ref, g_ref, o_ref):
        o_ref[...] = lax.dot_general(a_ref[...], g_ref[...], (((0,), (0,)), ((), ())),
                                     preferred_element_type=F32).astype(out_dtype)

    return pl.pallas_call(
        body, name=name,
        grid=(Ka // tka, nb, nn),
        in_specs=[pl.BlockSpec((M, tka), lambda ka, j, n: (0, ka)),
                  pl.BlockSpec((M, tn), lambda ka, j, n: (0, j * nn + n))],
        out_specs=pl.BlockSpec((None, tka, tn), lambda ka, j, n: (j, ka, n)),
        out_shape=jax.ShapeDtypeStruct((nb, Ka, Nb), out_dtype),
        compiler_params=_cparams(("parallel", "parallel", "parallel")),
    )(a, g)


ROW_TILE = 256


def _rms_fwd(x, w, name):
    T, Dm = x.shape

    def body(x_ref, w_ref, u_ref):
        xv = x_ref[...]
        r = lax.rsqrt(jnp.mean(xv * xv, axis=-1, keepdims=True) + EPS)
        u_ref[...] = (xv * r * w_ref[...]).astype(BF16)

    return pl.pallas_call(
        body, name=name, grid=(T // ROW_TILE,),
        in_specs=[pl.BlockSpec((ROW_TILE, Dm), lambda i: (i, 0)), pl.BlockSpec((1, Dm), lambda i: (0, 0))],
        out_specs=pl.BlockSpec((ROW_TILE, Dm), lambda i: (i, 0)),
        out_shape=jax.ShapeDtypeStruct((T, Dm), BF16),
        compiler_params=_cparams(("parallel",)),
    )(x, w)


def _resid_rms_fwd(x, mix, w, name):
    T, Dm = x.shape

    def body(x_ref, m_ref, w_ref, h_ref, u_ref):
        h = x_ref[...] + m_ref[...]
        h_ref[...] = h
        r = lax.rsqrt(jnp.mean(h * h, axis=-1, keepdims=True) + EPS)
        u_ref[...] = (h * r * w_ref[...]).astype(BF16)

    row = pl.BlockSpec((ROW_TILE, Dm), lambda i: (i, 0))
    return pl.pallas_call(
        body, name=name, grid=(T // ROW_TILE,),
        in_specs=[row, row, pl.BlockSpec((1, Dm), lambda i: (0, 0))],
        out_specs=[row, row],
        out_shape=[jax.ShapeDtypeStruct((T, Dm), F32), jax.ShapeDtypeStruct((T, Dm), BF16)],
        compiler_params=_cparams(("parallel",)),
    )(x, mix, w)


def _loss_head(h1, mlp, wf, target, name):
    T, Dm = h1.shape

    def body(h_ref, m_ref, w_ref, t_ref, loss_ref, dh_ref, dhb_ref, dw_ref):
        i = pl.program_id(0)
        h = h_ref[...] + m_ref[...]
        r = lax.rsqrt(jnp.mean(h * h, axis=-1, keepdims=True) + EPS)
        xh = h * r
        wv = w_ref[...]
        e = xh * wv - t_ref[...]
        part = 0.5 * jnp.sum(jnp.mean(e * e, axis=-1, keepdims=True), axis=0, keepdims=True)
        dy = e * (1.0 / Dm)
        dw = jnp.sum(dy * xh, axis=0, keepdims=True)
        gy = dy * wv
        dh = r * (gy - xh * jnp.mean(gy * xh, axis=-1, keepdims=True))
        dh_ref[...] = dh
        dhb_ref[...] = dh.astype(BF16)

        @pl.when(i == 0)
        def _():
            loss_ref[...] = jnp.zeros_like(loss_ref)
            dw_ref[...] = jnp.zeros_like(dw_ref)

        loss_ref[...] += jnp.broadcast_to(part, loss_ref.shape)
        dw_ref[...] += dw

    row = pl.BlockSpec((ROW_TILE, Dm), lambda i: (i, 0))
    vec = pl.BlockSpec((1, Dm), lambda i: (0, 0))
    return pl.pallas_call(
        body, name=name, grid=(T // ROW_TILE,),
        in_specs=[row, row, vec, row],
        out_specs=[pl.BlockSpec((8, 128), lambda i: (0, 0)), row, row, vec],
        out_shape=[jax.ShapeDtypeStruct((8, 128), F32), jax.ShapeDtypeStruct((T, Dm), F32),
                   jax.ShapeDtypeStruct((T, Dm), BF16), jax.ShapeDtypeStruct((1, Dm), F32)],
        compiler_params=_cparams(("arbitrary",)),
    )(h1, mlp, wf, target)


def _rms_bwd(dyn, x, w, dres, name):
    T, Dm = x.shape

    def body(g_ref, x_ref, w_ref, r_ref, dx_ref, dxb_ref, dw_ref):
        i = pl.program_id(0)
        xv = x_ref[...]
        r = lax.rsqrt(jnp.mean(xv * xv, axis=-1, keepdims=True) + EPS)
        xh = xv * r
        g = g_ref[...]
        dw = jnp.sum(g * xh, axis=0, keepdims=True)
        gy = g * w_ref[...]
        dx = r_ref[...] + r * (gy - xh * jnp.mean(gy * xh, axis=-1, keepdims=True))
        dx_ref[...] = dx
        dxb_ref[...] = dx.astype(BF16)

        @pl.when(i == 0)
        def _():
            dw_ref[...] = jnp.zeros_like(dw_ref)

        dw_ref[...] += dw

    row = pl.BlockSpec((ROW_TILE, Dm), lambda i: (i, 0))
    vec = pl.BlockSpec((1, Dm), lambda i: (0, 0))
    return pl.pallas_call(
        body, name=name, grid=(T // ROW_TILE,),
        in_specs=[row, row, vec, row],
        out_specs=[row, row, vec],
        out_shape=[jax.ShapeDtypeStruct((T, Dm), F32), jax.ShapeDtypeStruct((T, Dm), BF16),
                   jax.ShapeDtypeStruct((1, Dm), F32)],
        compiler_params=_cparams(("arbitrary",)),
    )(dyn, x, w, dres)


COL_TILE = 2048


def _relu2_fwd(a, name):
    T, N = a.shape

    def body(a_ref, r_ref):
        ra = jnp.maximum(a_ref[...], 0.0)
        r_ref[...] = (ra * ra).astype(BF16)

    blk = pl.BlockSpec((ROW_TILE, COL_TILE), lambda i, j: (i, j))
    return pl.pallas_call(
        body, name=name, grid=(T // ROW_TILE, N // COL_TILE), in_specs=[blk], out_specs=blk,
        out_shape=jax.ShapeDtypeStruct((T, N), BF16),
        compiler_params=_cparams(("parallel", "parallel")),
    )(a)


def _relu2_bwd(dr, a, name):
    T, N = a.shape

    def body(dr_ref, a_ref, da_ref):
        da_ref[...] = (dr_ref[...] * (2.0 * jnp.maximum(a_ref[...], 0.0))).astype(BF16)

    blk = pl.BlockSpec((ROW_TILE, COL_TILE), lambda i, j: (i, j))
    return pl.pallas_call(
        body, name=name, grid=(T // ROW_TILE, N // COL_TILE), in_specs=[blk, blk], out_specs=blk,
        out_shape=jax.ShapeDtypeStruct((T, N), BF16),
        compiler_params=_cparams(("parallel", "parallel")),
    )(dr, a)


GATE_TILE = 1024


def _merge_fwd(z, pa, pb, name):
    T, Dm = pa.shape

    def body(za_ref, zb_ref, pa_ref, pb_ref, m_ref):
        m_ref[...] = (_sigmoid(za_ref[...]) * pa_ref[...] + _sigmoid(zb_ref[...]) * pb_ref[...]).astype(BF16)

    blk = pl.BlockSpec((ROW_TILE, GATE_TILE), lambda i, j: (i, j))
    return pl.pallas_call(
        body, name=name, grid=(T // ROW_TILE, Dm // GATE_TILE),
        in_specs=[pl.BlockSpec((ROW_TILE, GATE_TILE), lambda i, j: (i, COL_GATE_A + j)),
                  pl.BlockSpec((ROW_TILE, GATE_TILE), lambda i, j: (i, COL_GATE_B + j)), blk, blk],
        out_specs=blk,
        out_shape=jax.ShapeDtypeStruct((T, Dm), BF16),
        compiler_params=_cparams(("parallel", "parallel")),
    )(z, z, pa, pb)


def _merge_bwd(dm, z, pa, pb, name):
    T, Dm = pa.shape

    def body(dm_ref, za_ref, zb_ref, pa_ref, pb_ref, dpa_ref, dpb_ref, dga_ref, dgb_ref):
        d = dm_ref[...]
        ga = _sigmoid(za_ref[...])
        gb = _sigmoid(zb_ref[...])
        dpa_ref[...] = (d * ga).astype(BF16)
        dpb_ref[...] = (d * gb).astype(BF16)
        dga_ref[...] = (d * pa_ref[...] * ga * (1.0 - ga)).astype(BF16)
        dgb_ref[...] = (d * pb_ref[...] * gb * (1.0 - gb)).astype(BF16)

    blk = pl.BlockSpec((ROW_TILE, GATE_TILE), lambda i, j: (i, j))
    out = jax.ShapeDtypeStruct((T, Dm), BF16)
    return pl.pallas_call(
        body, name=name, grid=(T // ROW_TILE, Dm // GATE_TILE),
        in_specs=[blk, pl.BlockSpec((ROW_TILE, GATE_TILE), lambda i, j: (i, COL_GATE_A + j)),
                  pl.BlockSpec((ROW_TILE, GATE_TILE), lambda i, j: (i, COL_GATE_B + j)), blk, blk],
        out_specs=[blk, blk, blk, blk],
        out_shape=[out, out, out, out],
        compiler_params=_cparams(("parallel", "parallel")),
    )(dm, z, z, pa, pb)


def _dot_hi(a, b, dims):
    return lax.dot_general(a, b, (dims, ((), ())), precision=HIGHEST, preferred_element_type=F32)


NN = ((1,), (0,))
NT = ((1,), (1,))
TN = ((0,), (0,))


def _hg_gates(hq, hf, lb):
    sq = _sigmoid(hq)
    q = hq * sq * (HG_DK ** -0.5)
    f = _sigmoid(hf)
    g = lb + (1.0 - lb) * f
    return q, sq, f, g, jnp.log(g), 1.0 - g


def _tri(lower):
    r = lax.broadcasted_iota(jnp.int32, (CHUNK, CHUNK), 0)
    c = lax.broadcasted_iota(jnp.int32, (CHUNK, CHUNK), 1)
    return jnp.where((r >= c) if lower else (r <= c), 1.0, 0.0).astype(F32)


def _hgrn2_fwd(z, lb_logits, hg_norm_w, name):
    T = z.shape[0]
    n_chunks = T // CHUNK

    def body(hq_ref, hf_ref, hi_ref, hg_ref, lbl_ref, nw_ref, o_ref, ya_ref, sall_ref, st_ref):
        lbl = lbl_ref[...]
        lb = 1.0 / (1.0 + jnp.exp(lbl[1:2, :] - lbl[0:1, :]))
        st_ref[...] = jnp.zeros_like(st_ref)
        tri = _tri(True)
        row8 = lax.broadcasted_iota(jnp.int32, (8, HG_DK), 0)

        def chunk(c, carry):
            rows = pl.ds(pl.multiple_of(c * CHUNK, CHUNK), CHUNK)
            q, _, _, _, lg, kk = _hg_gates(hq_ref[rows, :], hf_ref[rows, :], lb)
            v = hi_ref[rows, :]
            b = _dot_hi(tri, lg, NN)
            st = st_ref[...]
            sall_ref[c] = st
            o_inter = _dot_hi(q * jnp.exp(b), st, NT)
            for g8 in range(CHUNK // 8):
                n = 8 * (g8 + 1)
                bs, ks, vs = b[:n], kk[:n], v[:n]
                sidx = lax.broadcasted_iota(jnp.int32, (n, HG_DK), 0)
                blk = o_inter[8 * g8:n]
                for i in range(8):
                    t = 8 * g8 + i
                    e = jnp.where(sidx <= t, jnp.exp(b[t:t + 1] - bs), 0.0)
                    p = jnp.sum(e * ks * q[t:t + 1], axis=1, keepdims=True)
                    ot = jnp.sum(p * vs, axis=0, keepdims=True)
                    blk = blk + jnp.where(row8 == i, ot, 0.0)
                o_ref[pl.ds(pl.multiple_of(c * CHUNK + 8 * g8, 8), 8), :] = blk
            bl = b[CHUNK - 1:CHUNK]
            ke = kk * jnp.exp(bl - b)
            st_ref[...] = st * jnp.exp(bl) + _dot_hi(v, ke, TN)
            return carry

        lax.fori_loop(0, n_chunks, chunk, 0)
        o = o_ref[...]
        r = lax.rsqrt(jnp.mean(o * o, axis=-1, keepdims=True) + EPS)
        hg = hg_ref[...]
        ya_ref[...] = (o * r * nw_ref[...] * (hg * _sigmoid(hg))).astype(BF16)

    def col(base):
        return pl.BlockSpec((T, HG_DK), lambda h: (0, base + h))

    return pl.pallas_call(
        body, name=name, grid=(HG_HEADS,),
        in_specs=[col(COL_HQ), col(COL_HF), col(COL_HI), col(COL_HG),
                  pl.BlockSpec((2, HG_DK), lambda h: (0, h)), pl.BlockSpec((1, HG_DK), lambda h: (0, 0))],
        out_specs=[col(0), col(0), pl.BlockSpec((None, n_chunks, HG_DK, HG_DK), lambda h: (h, 0, 0, 0))],
        out_shape=[jax.ShapeDtypeStruct((T, HG_WIDTH), F32), jax.ShapeDtypeStruct((T, HG_WIDTH), BF16),
                   jax.ShapeDtypeStruct((HG_HEADS, n_chunks, HG_DK, HG_DK), F32)],
        scratch_shapes=[pltpu.VMEM((HG_DK, HG_DK), F32)],
        compiler_params=_cparams(("parallel",)),
    )(z, z, z, z, lb_logits, hg_norm_w)


def _hgrn2_bwd(z, lb_logits, hg_norm_w, o_raw, s_all, dya, name):
    T = z.shape[0]
    n_chunks = T // CHUNK

    def body(hq_ref, hf_ref, hi_ref, hg_ref, lbl_ref, nw_ref, o_ref, sall_ref, dya_ref,
             dhq_ref, dhf_ref, dhi_ref, dhg_ref, dlbl_ref, dnw_ref,
             do_ref, dst_ref, dq_ref, dk_ref, dv_ref, dlb_ref):
        h = pl.program_id(0)
        lbl = lbl_ref[...]
        lb = 1.0 / (1.0 + jnp.exp(lbl[1:2, :] - lbl[0:1, :]))

        o = o_ref[...]
        r = lax.rsqrt(jnp.mean(o * o, axis=-1, keepdims=True) + EPS)
        oh = o * r
        nw = nw_ref[...]
        hg = hg_ref[...]
        sg = _sigmoid(hg)
        dy = dya_ref[...]
        d_on = dy * (hg * sg)
        dhg_ref[...] = (dy * (oh * nw) * (sg * (1.0 + hg * (1.0 - sg)))).astype(BF16)
        dnw = jnp.sum(d_on * oh, axis=0, keepdims=True)
        gy = d_on * nw
        do_ref[...] = r * (gy - oh * jnp.mean(gy * oh, axis=-1, keepdims=True))

        @pl.when(h == 0)
        def _():
            dnw_ref[...] = jnp.zeros_like(dnw_ref)

        dnw_ref[...] += jnp.broadcast_to(dnw, dnw_ref.shape)

        dst_ref[...] = jnp.zeros_like(dst_ref)
        dlb_ref[...] = jnp.zeros_like(dlb_ref)
        tri = _tri(True)
        tri_t = _tri(False)
        row8 = lax.broadcasted_iota(jnp.int32, (8, HG_DK), 0)

        def chunk(ci, carry):
            c = n_chunks - 1 - ci
            rows = pl.ds(pl.multiple_of(c * CHUNK, CHUNK), CHUNK)
            hq = hq_ref[rows, :]
            q, sq, f, g, lg, kk = _hg_gates(hq, hf_ref[rows, :], lb)
            v = hi_ref[rows, :]
            do = do_ref[rows, :]
            b = _dot_hi(tri, lg, NN)
            eb = jnp.exp(b)
            bl = b[CHUNK - 1:CHUNK]
            ebl = jnp.exp(bl)
            ekb = jnp.exp(bl - b)
            qe = q * eb
            ke = kk * ekb
            st = sall_ref[c]
            dst = dst_ref[...]
            dqe = _dot_hi(do, st, NN)
            dke = _dot_hi(v, dst, NN)
            dv_inter = _dot_hi(ke, dst, NT)
            d_ebl = jnp.sum(st * dst, axis=0, keepdims=True)
            dst_ref[...] = dst * ebl + _dot_hi(do, qe, TN)

            dk_ref[...] = jnp.zeros_like(dk_ref)
            dv_ref[...] = jnp.zeros_like(dv_ref)
            for g8 in range(CHUNK // 8):
                n = 8 * (g8 + 1)
                bs, ks, vs = b[:n], kk[:n], v[:n]
                sidx = lax.broadcasted_iota(jnp.int32, (n, HG_DK), 0)
                blk = jnp.zeros((8, HG_DK), F32)
                for i in range(8):
                    t = 8 * g8 + i
                    qt = q[t:t + 1]
                    dot_ = do[t:t + 1]
                    e = jnp.where(sidx <= t, jnp.exp(b[t:t + 1] - bs), 0.0)
                    w = e * ks
                    p = jnp.sum(w * qt, axis=1, keepdims=True)
                    dsc = jnp.sum(vs * dot_, axis=1, keepdims=True)
                    dqt = jnp.sum(dsc * w, axis=0, keepdims=True)
                    blk = blk + jnp.where(row8 == i, dqt, 0.0)
                    dk_ref[0:n, :] += dsc * e * qt
                    dv_ref[0:n, :] += p * dot_
                dq_ref[8 * g8:n, :] = blk
            dq_i = dq_ref[...]
            dk_i = dk_ref[...]
            dke_ke = dke * ke
            db = q * dq_i - kk * dk_i + dqe * qe - dke_ke
            db_last = jnp.sum(dke_ke, axis=0, keepdims=True) + d_ebl * ebl
            dlg = _dot_hi(tri_t, db, NN) + db_last
            dq = dq_i + dqe * eb
            dkk = dk_i + dke * ekb
            dg = dlg / g - dkk
            dhq_ref[rows, :] = (dq * (HG_DK ** -0.5) * (sq * (1.0 + hq * (1.0 - sq)))).astype(BF16)
            dhf_ref[rows, :] = (dg * (1.0 - lb) * f * (1.0 - f)).astype(BF16)
            dhi_ref[rows, :] = (dv_ref[...] + dv_inter).astype(BF16)
            dlb_ref[...] += jnp.sum(dg * (1.0 - f), axis=0, keepdims=True)
            return carry

        lax.fori_loop(0, n_chunks, chunk, 0)
        dl0 = dlb_ref[...] * lb * (1.0 - lb)
        dlbl_ref[0:1, :] = dl0
        dlbl_ref[1:2, :] = -dl0

    def col(base):
        return pl.BlockSpec((T, HG_DK), lambda h: (0, base + h))

    outb = jax.ShapeDtypeStruct((T, HG_WIDTH), BF16)
    return pl.pallas_call(
        body, name=name, grid=(HG_HEADS,),
        in_specs=[col(COL_HQ), col(COL_HF), col(COL_HI), col(COL_HG),
                  pl.BlockSpec((2, HG_DK), lambda h: (0, h)), pl.BlockSpec((1, HG_DK), lambda h: (0, 0)),
                  col(0), pl.BlockSpec((None, n_chunks, HG_DK, HG_DK), lambda h: (h, 0, 0, 0)), col(0)],
        out_specs=[col(0), col(0), col(0), col(0), pl.BlockSpec((2, HG_DK), lambda h: (0, h)),
                   pl.BlockSpec((8, HG_DK), lambda h: (0, 0))],
        out_shape=[outb, outb, outb, outb, jax.ShapeDtypeStruct((2, HG_WIDTH), F32),
                   jax.ShapeDtypeStruct((8, HG_DK), F32)],
        scratch_shapes=[pltpu.VMEM((T, HG_DK), F32), pltpu.VMEM((HG_DK, HG_DK), F32),
                        pltpu.VMEM((CHUNK, HG_DK), F32), pltpu.VMEM((CHUNK, HG_DK), F32),
                        pltpu.VMEM((CHUNK, HG_DK), F32), pltpu.VMEM((1, HG_DK), F32)],
        compiler_params=_cparams(("arbitrary",)),
    )(z, z, z, z, lb_logits, hg_norm_w, o_raw, s_all, dya)


def _rel_onehot(t):
    r = lax.broadcasted_iota(jnp.int32, (N_REL_PAD, BAND), 0)
    j = lax.broadcasted_iota(jnp.int32, (N_REL_PAD, BAND), 1)
    idx = jnp.clip(t + PAD - j, -REL_CLIP, REL_CLIP) + REL_CLIP
    return jnp.where(r == idx, 1.0, 0.0).astype(F32)


def _bias_expand(rel, name):
    def body(rel_ref, out_ref):
        out_ref[...] = _dot_hi(rel_ref[...], _rel_onehot(pl.program_id(0)), NN)

    return pl.pallas_call(
        body, name=name, grid=(CHUNK,),
        in_specs=[pl.BlockSpec((AT_HEADS, N_REL_PAD), lambda t: (0, 0))],
        out_specs=pl.BlockSpec((None, AT_HEADS, BAND), lambda t: (t, 0, 0)),
        out_shape=jax.ShapeDtypeStruct((CHUNK, AT_HEADS, BAND), F32),
        compiler_params=_cparams(("parallel",)),
    )(rel)


def _bias_reduce(dbias_t, name):
    def body(db_ref, out_ref):
        t = pl.program_id(0)

        @pl.when(t == 0)
        def _():
            out_ref[...] = jnp.zeros_like(out_ref)

        out_ref[...] += _dot_hi(db_ref[...], _rel_onehot(t), NT)

    return pl.pallas_call(
        body, name=name, grid=(CHUNK,),
        in_specs=[pl.BlockSpec((None, AT_HEADS, BAND), lambda t: (t, 0, 0))],
        out_specs=pl.BlockSpec((AT_HEADS, N_REL_PAD), lambda t: (0, 0)),
        out_shape=jax.ShapeDtypeStruct((AT_HEADS, N_REL_PAD), F32),
        compiler_params=_cparams(("arbitrary",)),
    )(dbias_t)


def _band_probs(qc, kb, bias, c):
    s = lax.dot_general(qc, kb, (NT, ((), ())), preferred_element_type=F32) * (AT_DH ** -0.5) + bias
    j = lax.broadcasted_iota(jnp.int32, (CHUNK, BAND), 1)
    s = jnp.where(j + c * CHUNK >= PAD, s, -jnp.inf)
    p = jnp.exp(s - jnp.max(s, axis=-1, keepdims=True))
    return p / jnp.sum(p, axis=-1, keepdims=True)


def _attn_fwd(z, bias, name):
    T = z.shape[0]
    n_chunks = T // CHUNK

    def body(q_ref, k_ref, v_ref, bias_ref, y_ref, kp_ref, vp_ref):
        for hh in range(2):
            lanes = slice(hh * AT_DH, (hh + 1) * AT_DH)
            kp_ref[0:PAD, :] = jnp.zeros((PAD, AT_DH), BF16)
            vp_ref[0:PAD, :] = jnp.zeros((PAD, AT_DH), BF16)
            kp_ref[PAD:PAD + T, :] = k_ref[:, lanes].astype(BF16)
            vp_ref[PAD:PAD + T, :] = v_ref[:, lanes].astype(BF16)
            bias_h = bias_ref[hh]

            def chunk(c, carry):
                rows = pl.ds(pl.multiple_of(c * CHUNK, CHUNK), CHUNK)
                qc = q_ref[rows, lanes].astype(BF16)
                band = pl.ds(pl.multiple_of(c * CHUNK, CHUNK), BAND)
                kb = kp_ref[band, :]
                vb = vp_ref[band, :]
                p = _band_probs(qc, kb, bias_h, c)
                y_ref[rows, lanes] = jnp.dot(p.astype(BF16), vb, preferred_element_type=F32).astype(BF16)
                return carry

            lax.fori_loop(0, n_chunks, chunk, 0)

    def col(base):
        return pl.BlockSpec((T, 128), lambda h: (0, base + h))

    return pl.pallas_call(
        body, name=name, grid=(AT_HEADS // 2,),
        in_specs=[col(COL_AQ), col(COL_AK), col(COL_AV), pl.BlockSpec((2, CHUNK, BAND), lambda h: (h, 0, 0))],
        out_specs=col(0),
        out_shape=jax.ShapeDtypeStruct((T, AT_WIDTH), BF16),
        scratch_shapes=[pltpu.VMEM((PAD + T, AT_DH), BF16), pltpu.VMEM((PAD + T, AT_DH), BF16)],
        compiler_params=_cparams(("parallel",)),
    )(z, z, z, bias)


def _attn_bwd(z, bias, bias_kq, dyb, name):
    T = z.shape[0]
    n_chunks = T // CHUNK

    def body(q_ref, k_ref, v_ref, bias_ref, biast_ref, dy_ref, dq_ref, dk_ref, dv_ref, dbias_ref,
             kp_ref, vp_ref, dkp_ref, dvp_ref):
        for hh in range(2):
            lanes = slice(hh * AT_DH, (hh + 1) * AT_DH)
            kp_ref[0:PAD, :] = jnp.zeros((PAD, AT_DH), BF16)
            vp_ref[0:PAD, :] = jnp.zeros((PAD, AT_DH), BF16)
            kp_ref[PAD:PAD + T, :] = k_ref[:, lanes].astype(BF16)
            vp_ref[PAD:PAD + T, :] = v_ref[:, lanes].astype(BF16)
            dkp_ref[...] = jnp.zeros_like(dkp_ref)
            dvp_ref[...] = jnp.zeros_like(dvp_ref)
            dbias_ref[hh] = jnp.zeros((CHUNK, BAND), F32)
            bias_h = bias_ref[hh]
            biast_h = biast_ref[hh]

            def chunk(c, carry):
                rows = pl.ds(pl.multiple_of(c * CHUNK, CHUNK), CHUNK)
                band = pl.ds(pl.multiple_of(c * CHUNK, CHUNK), BAND)
                qc = q_ref[rows, lanes].astype(BF16)
                doc = dy_ref[rows, lanes].astype(BF16)
                kb = kp_ref[band, :]
                vb = vp_ref[band, :]
                p = _band_probs(qc, kb, bias_h, c)
                dp = lax.dot_general(doc, vb, (NT, ((), ())), preferred_element_type=F32)
                ds = p * (dp - jnp.sum(dp * p, axis=-1, keepdims=True))
                dbias_ref[hh] += ds
                dq_ref[rows, lanes] = (jnp.dot(ds.astype(BF16), kb, preferred_element_type=F32)
                                       * (AT_DH ** -0.5)).astype(BF16)
                st = lax.dot_general(kb, qc, (NT, ((), ())), preferred_element_type=F32) * (AT_DH ** -0.5) + biast_h
                j = lax.broadcasted_iota(jnp.int32, (BAND, CHUNK), 0)
                st = jnp.where(j + c * CHUNK >= PAD, st, -jnp.inf)
                pt = jnp.exp(st - jnp.max(st, axis=0, keepdims=True))
                pt = pt / jnp.sum(pt, axis=0, keepdims=True)
                dpt = lax.dot_general(vb, doc, (NT, ((), ())), preferred_element_type=F32)
                dst = pt * (dpt - jnp.sum(dpt * pt, axis=0, keepdims=True))
                dkp_ref[band, :] += jnp.dot(dst.astype(BF16), qc, preferred_element_type=F32) * (AT_DH ** -0.5)
                dvp_ref[band, :] += jnp.dot(pt.astype(BF16), doc, preferred_element_type=F32)
                return carry

            lax.fori_loop(0, n_chunks, chunk, 0)
            dk_ref[:, lanes] = dkp_ref[PAD:PAD + T, :].astype(BF16)
            dv_ref[:, lanes] = dvp_ref[PAD:PAD + T, :].astype(BF16)

    def col(base):
        return pl.BlockSpec((T, 128), lambda h: (0, base + h))

    outb = jax.ShapeDtypeStruct((T, AT_WIDTH), BF16)
    return pl.pallas_call(
        body, name=name, grid=(AT_HEADS // 2,),
        in_specs=[col(COL_AQ), col(COL_AK), col(COL_AV), pl.BlockSpec((2, CHUNK, BAND), lambda h: (h, 0, 0)),
                  pl.BlockSpec((2, BAND, CHUNK), lambda h: (h, 0, 0)), col(0)],
        out_specs=[col(0), col(0), col(0), pl.BlockSpec((2, CHUNK, BAND), lambda h: (h, 0, 0))],
        out_shape=[outb, outb, outb, jax.ShapeDtypeStruct((AT_HEADS, CHUNK, BAND), F32)],
        scratch_shapes=[pltpu.VMEM((PAD + T, AT_DH), BF16), pltpu.VMEM((PAD + T, AT_DH), BF16),
                        pltpu.VMEM((PAD + T, AT_DH), F32), pltpu.VMEM((PAD + T, AT_DH), F32)],
        compiler_params=_cparams(("parallel",)),
    )(z, z, z, bias, bias_kq, dyb)


def _local_step(x, target, lb_logits, hg_norm_w, rel_bias, norm_mix_w, norm_mlp_w, norm_final_w,
                w_in, w_a, w_b, w_out, w_up, w_down):
    w_out1 = w_out.reshape(1, D_MODEL, D_MODEL)
    w_down1 = w_down.reshape(1, D_FF, D_MODEL)
    rel = jnp.pad(rel_bias, ((0, 0), (0, N_REL_PAD - N_REL)))

    u = _rms_fwd(x, norm_mix_w, "rms_mix_fwd")
    z = _mm_nn(u, w_in, F32, "mm_in_fwd")
    o_raw, y_a, s_all = _hgrn2_fwd(z, lb_logits, hg_norm_w, "hgrn2_fwd")
    bias_t = _bias_expand(rel, "bias_expand")
    bias = jnp.transpose(bias_t, (1, 0, 2))
    bias_kq = jnp.transpose(bias_t, (1, 2, 0))
    y_b = _attn_fwd(z, bias, "attn_fwd")
    pa = _mm_nn(y_a, w_a, F32, "mm_a_fwd")
    pb = _mm_nn(y_b, w_b, F32, "mm_b_fwd")
    merged = _merge_fwd(z, pa, pb, "merge_fwd")
    mix = _mm_nn(merged, w_out1, F32, "mm_out_fwd")
    h1, u2 = _resid_rms_fwd(x, mix, norm_mlp_w, "rms_mlp_fwd")
    a = _mm_nn(u2, w_up, F32, "mm_up_fwd")
    r = _relu2_fwd(a, "relu2_fwd")
    mlp = _mm_nn(r, w_down1, F32, "mm_down_fwd")
    loss, dh2, dh2b, g_nf = _loss_head(h1, mlp, norm_final_w, target, "loss_head")

    g_down = _mm_tn(r, dh2b, 1, BF16, "mm_down_wgrad").reshape(N_DEV, D_FF // N_DEV, D_MODEL)
    dr = _mm_nt(dh2b, w_down1, F32, "mm_down_dgrad")
    da = _relu2_bwd(dr, a, "relu2_bwd")
    g_up = _mm_tn(u2, da, N_DEV, BF16, "mm_up_wgrad")
    du2 = _mm_nt(da, w_up, F32, "mm_up_dgrad")
    dh1, dh1b, g_nmlp = _rms_bwd(du2, h1, norm_mlp_w, dh2, "rms_mlp_bwd")

    g_out = _mm_tn(merged, dh1b, 1, BF16, "mm_out_wgrad").reshape(N_DEV, D_MODEL // N_DEV, D_MODEL)
    dmerged = _mm_nt(dh1b, w_out1, F32, "mm_out_dgrad")
    dpa, dpb, dga, dgb = _merge_bwd(dmerged, z, pa, pb, "merge_bwd")
    g_a = _mm_tn(y_a, dpa, N_DEV, BF16, "mm_a_wgrad")
    g_b = _mm_tn(y_b, dpb, N_DEV, BF16, "mm_b_wgrad")
    dya = _mm_nt(dpa, w_a, F32, "mm_a_dgrad")
    dyb = _mm_nt(dpb, w_b, F32, "mm_b_dgrad")
    dhq, dhf, dhi, dhg, g_lbl, g_hgw = _hgrn2_bwd(z, lb_logits, hg_norm_w, o_raw, s_all, dya, "hgrn2_bwd")
    daq, dak, dav, dbias = _attn_bwd(z, bias, bias_kq, dyb, "attn_bwd")
    g_rel = _bias_reduce(jnp.transpose(dbias, (1, 0, 2)), "bias_reduce")[:, :N_REL]
    dz = jnp.concatenate([dhq, dhf, dhi, dhg, daq, dak, dav, dga, dgb], axis=1)
    g_in = _mm_tn(u, dz, N_DEV, BF16, "mm_in_wgrad")
    du = _mm_nt(dz, w_in, F32, "mm_in_dgrad")
    grad_x, _, g_nmix = _rms_bwd(du, x, norm_mix_w, dh1, "rms_mix_bwd")

    small = dict(lb_logits=g_lbl, hg_norm_w=g_hgw[0:1], rel_bias=g_rel, norm_mix_w=g_nmix, norm_mlp_w=g_nmlp,
                 norm_final_w=g_nf)
    return loss, grad_x, (g_in, g_a, g_b, g_out, g_up, g_down), small


ANY = pl.BlockSpec(memory_space=pl.ANY)


def _position():
    return lax.axis_index("x"), lax.axis_index("y"), lax.axis_index("c")


def _all_gather(shards, name):
    n = len(shards)

    def body(*refs):
        ins, outs = refs[:n], refs[n:2 * n]
        send_sems, recv_sems, local_sems = refs[2 * n:]
        x, y, c = _position()
        me, sibling = (x, y, c), (x, y, 1 - c)
        chips = [(1 - x, y), (x, 1 - y), (1 - x, 1 - y)]

        def copy(w, k, block, to, src=None):
            dst = outs[w].at[4 * block[0] + 2 * block[1] + block[2]]
            return pltpu.make_async_remote_copy(
                src_ref=dst if src is None else src, dst_ref=dst,
                send_sem=send_sems.at[w, k], recv_sem=recv_sems.at[w, k], device_id=to, device_id_type=MESH)

        mine, first, passed = [], [], []
        for w in range(n):
            cp = pltpu.make_async_copy(ins[w], outs[w].at[4 * x + 2 * y + c], local_sems.at[w])
            cp.start()
            mine.append(cp)
        for w in range(n):
            first.append(copy(w, 0, me, sibling, src=ins[w]))
            first += [copy(w, 1 + j, me, (*chip, c), src=ins[w]) for j, chip in enumerate(chips)]
        for cp in first:
            cp.start()
        for w in range(n):
            for j, chip in enumerate(chips):
                copy(w, 1 + j, (*chip, c), me).wait_recv()
                cp = copy(w, 4 + j, (*chip, c), sibling)
                cp.start()
                passed.append(cp)
        for w in range(n):
            copy(w, 0, sibling, me).wait_recv()
            for j, chip in enumerate(chips):
                copy(w, 4 + j, (*chip, 1 - c), me).wait_recv()
        for cp in first + passed:
            cp.wait_send()
        for cp in mine:
            cp.wait()

    return pl.pallas_call(
        body, name=name,
        in_specs=[ANY] * n, out_specs=[ANY] * n,
        out_shape=[jax.ShapeDtypeStruct((N_DEV,) + s.shape, s.dtype) for s in shards],
        scratch_shapes=[pltpu.SemaphoreType.DMA((n, 7)), pltpu.SemaphoreType.DMA((n, 7)),
                        pltpu.SemaphoreType.DMA((n,))],
    )(*shards)


def _exchange_sibling(grads, name):
    n = len(grads)

    def body(*refs):
        ins, outs = refs[:n], refs[n:2 * n]
        send_sems, recv_sems = refs[2 * n:]
        x, y, c = _position()
        copies = []
        for w in range(n):
            for s in range(N_CHIP):
                cp = pltpu.make_async_remote_copy(
                    src_ref=ins[w].at[2 * s + (1 - c)], dst_ref=outs[w].at[s],
                    send_sem=send_sems.at[w, s], recv_sem=recv_sems.at[w, s],
                    device_id=(x, y, 1 - c), device_id_type=MESH)
                cp.start()
                copies.append(cp)
        for cp in copies:
            cp.wait()

    return pl.pallas_call(
        body, name=name,
        in_specs=[ANY] * n, out_specs=[ANY] * n,
        out_shape=[jax.ShapeDtypeStruct((N_CHIP,) + g.shape[1:], g.dtype) for g in grads],
        scratch_shapes=[pltpu.SemaphoreType.DMA((n, N_CHIP)), pltpu.SemaphoreType.DMA((n, N_CHIP))],
    )(*grads)


def _pair_sum(g, land, parity, name):
    _, R, C = g.shape
    tr = _pick(R, (512, 256))

    def body(par_ref, g_ref, l_ref, o_ref):
        o_ref[...] = (g_ref[...].astype(F32) + l_ref[...].astype(F32)).astype(BF16)

    return pl.pallas_call(
        body, name=name,
        grid_spec=pltpu.PrefetchScalarGridSpec(
            num_scalar_prefetch=1, grid=(N_CHIP, R // tr),
            in_specs=[pl.BlockSpec((None, tr, C), lambda s, i, par: (2 * s + par[0], i, 0)),
                      pl.BlockSpec((None, tr, C), lambda s, i, par: (s, i, 0))],
            out_specs=pl.BlockSpec((None, tr, C), lambda s, i, par: (s, i, 0))),
        out_shape=jax.ShapeDtypeStruct((N_CHIP, R, C), BF16),
        compiler_params=_cparams(("parallel", "parallel")),
    )(parity, g, land)


def _exchange_chips(partials, name):
    n = len(partials)

    def body(*refs):
        ins, outs = refs[:n], refs[n:2 * n]
        send_sems, recv_sems, local_sems = refs[2 * n:]
        x, y, c = _position()
        chips = [(1 - x, y), (x, 1 - y), (1 - x, 1 - y)]
        my_slot = 2 * x + y
        copies, local = [], []
        for w in range(n):
            cp = pltpu.make_async_copy(ins[w].at[my_slot], outs[w].at[my_slot], local_sems.at[w])
            cp.start()
            local.append(cp)
            for j, chip in enumerate(chips):
                cp = pltpu.make_async_remote_copy(
                    src_ref=ins[w].at[2 * chip[0] + chip[1]], dst_ref=outs[w].at[my_slot],
                    send_sem=send_sems.at[w, j], recv_sem=recv_sems.at[w, j],
                    device_id=(*chip, c), device_id_type=MESH)
                cp.start()
                copies.append(cp)
        for cp in copies:
            cp.wait()
        for cp in local:
            cp.wait()

    return pl.pallas_call(
        body, name=name,
        in_specs=[ANY] * n, out_specs=[ANY] * n,
        out_shape=[jax.ShapeDtypeStruct(p.shape, p.dtype) for p in partials],
        scratch_shapes=[pltpu.SemaphoreType.DMA((n, 3)), pltpu.SemaphoreType.DMA((n, 3)),
                        pltpu.SemaphoreType.DMA((n,))],
    )(*partials)


def _gather_small(packed, name):
    R = packed.shape[0]

    def body(x_ref, out_ref, send_sems, recv_sems):
        x, y, c = _position()
        me = 4 * x + 2 * y + c
        out_ref[me] = x_ref[...]
        copies = []
        for k in range(1, N_DEV):
            to = (x ^ ((k >> 2) & 1), y ^ ((k >> 1) & 1), c ^ (k & 1))
            cp = pltpu.make_async_remote_copy(
                src_ref=x_ref, dst_ref=out_ref.at[me],
                send_sem=send_sems.at[k], recv_sem=recv_sems.at[k], device_id=to, device_id_type=MESH)
            cp.start()
            copies.append((k, to, cp))
        for k, to, cp in copies:
            cp.wait_send()
            pltpu.make_async_remote_copy(
                src_ref=x_ref, dst_ref=out_ref.at[4 * to[0] + 2 * to[1] + to[2]],
                send_sem=send_sems.at[k], recv_sem=recv_sems.at[k], device_id=to, device_id_type=MESH).wait_recv()

    return pl.pallas_call(
        body, name=name,
        in_specs=[pl.BlockSpec(memory_space=pltpu.VMEM)], out_specs=pl.BlockSpec(memory_space=pltpu.VMEM),
        out_shape=jax.ShapeDtypeStruct((N_DEV, R, 128), F32),
        scratch_shapes=[pltpu.SemaphoreType.DMA((N_DEV,)), pltpu.SemaphoreType.DMA((N_DEV,))],
    )(packed)


def _adamw_math(w, g, m, v):
    m = ADAM_B1 * m + (1.0 - ADAM_B1) * g
    v = ADAM_B2 * v + (1.0 - ADAM_B2) * (g * g)
    m_hat = m / (1.0 - ADAM_B1 ** ADAM_STEP)
    v_hat = v / (1.0 - ADAM_B2 ** ADAM_STEP)
    delta = -ADAM_LR * (m_hat / (jnp.sqrt(v_hat) + ADAM_EPS) + ADAM_WD * w)
    return delta, m, v


def _adamw_big(w, m, v, parts, name):
    R, C = w.shape
    tr = _pick(R, (256,))

    def body(w_ref, m_ref, v_ref, p_ref, g_ref, d_ref, nm_ref, nv_ref):
        g = p_ref[0].astype(F32)
        for s in range(1, N_CHIP):
            g = g + p_ref[s].astype(F32)
        d, nm, nv = _adamw_math(w_ref[...], g, m_ref[...], v_ref[...])
        g_ref[...] = g
        d_ref[...] = d
        nm_ref[...] = nm
        nv_ref[...] = nv

    blk = pl.BlockSpec((tr, C), lambda i: (i, 0))
    out = jax.ShapeDtypeStruct((R, C), F32)
    return pl.pallas_call(
        body, name=name, grid=(R // tr,),
        in_specs=[blk, blk, blk, pl.BlockSpec((N_CHIP, tr, C), lambda i: (0, i, 0))],
        out_specs=[blk, blk, blk, blk], out_shape=[out, out, out, out],
        compiler_params=_cparams(("parallel",)),
    )(w, m, v, parts)


def _adamw_small(w, m, v, gathered, name):
    R = w.shape[0]

    def body(w_ref, m_ref, v_ref, p_ref, g_ref, d_ref, nm_ref, nv_ref):
        g = p_ref[0]
        for s in range(1, N_DEV):
            g = g + p_ref[s]
        d, nm, nv = _adamw_math(w_ref[...], g, m_ref[...], v_ref[...])
        g_ref[...] = g
        d_ref[...] = d
        nm_ref[...] = nm
        nv_ref[...] = nv

    out = jax.ShapeDtypeStruct((R, 128), F32)
    return pl.pallas_call(
        body, name=name, out_shape=[out, out, out, out],
    )(w, m, v, gathered)


SMALL_NAMES = ("lb_logits", "hg_norm_w", "rel_bias", "norm_mix_w", "norm_mlp_w", "norm_final_w")
SMALL_SHAPES = {"lb_logits": (2, HG_WIDTH), "hg_norm_w": (1, HG_DK), "rel_bias": (AT_HEADS, N_REL_PAD),
                "norm_mix_w": (1, D_MODEL), "norm_mlp_w": (1, D_MODEL), "norm_final_w": (1, D_MODEL)}


def _pack_small(parts):
    rows = []
    for nme in SMALL_NAMES:
        p = parts[nme]
        if nme == "rel_bias":
            p = jnp.pad(p, ((0, 0), (0, N_REL_PAD - N_REL)))
        rows.append(p.reshape(-1, 128))
    flat = jnp.concatenate(rows, axis=0)
    return jnp.pad(flat, ((0, SMALL_ROWS - flat.shape[0]), (0, 0)))


def _unpack_small(packed):
    out, at = {}, 0
    for nme in SMALL_NAMES:
        shp = SMALL_SHAPES[nme]
        nrow = shp[0] * shp[1] // 128
        p = packed[at:at + nrow].reshape(shp)
        at += nrow
        out[nme] = p[:, :N_REL] if nme == "rel_bias" else p
    return out


BIG_NAMES = ("w_in", "w_branch_a", "w_branch_b", "w_out", "w_up", "w_down")


def kernel(x, w_in, lb_logits, hg_norm_w, rel_bias, w_branch_a, w_branch_b, w_out, norm_mix_w, norm_mlp_w, w_up, w_down, norm_final_w, loss_target, m_w_in, m_lb_logits, m_hg_norm_w, m_rel_bias, m_w_branch_a, m_w_branch_b, m_w_out, m_norm_mix_w, m_norm_mlp_w, m_w_up, m_w_down, m_norm_final_w, v_w_in, v_lb_logits, v_hg_norm_w, v_rel_bias, v_w_branch_a, v_w_branch_b, v_w_out, v_norm_mix_w, v_norm_mlp_w, v_w_up, v_w_down, v_norm_final_w):
    big_w = [w_in[0], w_branch_a[0], w_branch_b[0], w_out[0], w_up[0], w_down[0]]
    big_m = [m_w_in[0], m_w_branch_a[0], m_w_branch_b[0], m_w_out[0], m_w_up[0], m_w_down[0]]
    big_v = [v_w_in[0], v_w_branch_a[0], v_w_branch_b[0], v_w_out[0], v_w_up[0], v_w_down[0]]

    full = _all_gather([w.astype(BF16) for w in big_w], "all_gather_weights")
    loss_part, grad_x, grads, small = _local_step(
        x[0], loss_target[0], lb_logits, hg_norm_w, rel_bias[0], norm_mix_w, norm_mlp_w,
        norm_final_w.reshape(1, D_MODEL), *full)
    loss = lax.psum(loss_part[0, 0], ("x", "y", "c"))

    parity = lax.axis_index("c").astype(jnp.int32).reshape(1)
    landed = _exchange_sibling(list(grads), "rs_sibling")
    partials = [_pair_sum(g, l, parity, "rs_pair_sum_" + nme) for g, l, nme in zip(grads, landed, BIG_NAMES)]
    chip_parts = _exchange_chips(partials, "rs_chips")
    big = [_adamw_big(w, m, v, p, "adamw_" + nme)
           for w, m, v, p, nme in zip(big_w, big_m, big_v, chip_parts, BIG_NAMES)]

    sw = dict(lb_logits=lb_logits, hg_norm_w=hg_norm_w, rel_bias=rel_bias[0], norm_mix_w=norm_mix_w,
              norm_mlp_w=norm_mlp_w, norm_final_w=norm_final_w.reshape(1, D_MODEL))
    sm = dict(lb_logits=m_lb_logits, hg_norm_w=m_hg_norm_w, rel_bias=m_rel_bias[0], norm_mix_w=m_norm_mix_w,
              norm_mlp_w=m_norm_mlp_w, norm_final_w=m_norm_final_w.reshape(1, D_MODEL))
    sv = dict(lb_logits=v_lb_logits, hg_norm_w=v_hg_norm_w, rel_bias=v_rel_bias[0], norm_mix_w=v_norm_mix_w,
              norm_mlp_w=v_norm_mlp_w, norm_final_w=v_norm_final_w.reshape(1, D_MODEL))
    gathered = _gather_small(_pack_small(small), "gather_small")
    small_out = [_unpack_small(p) for p in
                 _adamw_small(_pack_small(sw), _pack_small(sm), _pack_small(sv), gathered, "adamw_small")]

    def leaf(kind, nme):
        if nme in BIG_NAMES:
            return big[BIG_NAMES.index(nme)][kind][None]
        p = small_out[kind][nme]
        if nme == "rel_bias":
            return p[None]
        if nme == "norm_final_w":
            return p.reshape(D_MODEL)
        return p

    order = ("w_in", "lb_logits", "hg_norm_w", "rel_bias", "w_branch_a", "w_branch_b", "w_out", "norm_mix_w",
             "norm_mlp_w", "w_up", "w_down", "norm_final_w")
    outs = [loss, grad_x[None]]
    for kind in range(4):
        outs += [leaf(kind, nme) for nme in order]
    return tuple(outs)
```

```python
import functools

import jax
import jax.numpy as jnp
from jax import lax
from jax.experimental import pallas as pl
from jax.experimental.pallas import tpu as pltpu

F32 = jnp.float32
BF16 = jnp.bfloat16
HIGHEST = lax.Precision.HIGHEST
MESH = pl.DeviceIdType.MESH

D_MODEL = 2048
HG_HEADS = 8
HG_DK = 128
HG_WIDTH = 1024
AT_HEADS = 16
AT_DH = 64
AT_WIDTH = 1024
CHUNK = 64
LEFT_CHUNKS = 8
BAND = (LEFT_CHUNKS + 1) * CHUNK
PAD = LEFT_CHUNKS * CHUNK
REL_CLIP = 256
N_REL = 2 * REL_CLIP + 1
N_REL_PAD = 640
D_FF = 4 * D_MODEL
D_IN = 4 * HG_WIDTH + 3 * AT_WIDTH + 2 * D_MODEL
EPS = 1e-6
N_DEV = 8
N_CHIP = 4

ADAM_LR = 0.001
ADAM_B1 = 0.9
ADAM_B2 = 0.999
ADAM_EPS = 1e-08
ADAM_WD = 0.01
ADAM_STEP = 10

COL_HQ, COL_HF, COL_HI, COL_HG = 0, 8, 16, 24
COL_AQ, COL_AK, COL_AV = 32, 40, 48
COL_GATE_A, COL_GATE_B = 7, 9

VMEM_LIMIT = 56 * 1024 * 1024
SMALL_ROWS = 152


def _cparams(sem=None, **kw):
    if sem is not None:
        kw["dimension_semantics"] = sem
    return pltpu.CompilerParams(vmem_limit_bytes=VMEM_LIMIT, **kw)


def _pick(n, cands):
    for c in cands:
        if n % c == 0:
            return c
    return n


def _sigmoid(x):
    return 1.0 / (1.0 + jnp.exp(-x))


ANY = pl.BlockSpec(memory_space=pl.ANY)


def _position():
    return lax.axis_index("x"), lax.axis_index("y"), lax.axis_index("c")


class _Exchange:
    def __init__(self, arrays, out_shape, scratch, start, end, mid=None, mid_step=None):
        self.arrays, self.out_shape, self.scratch = list(arrays), list(out_shape), list(scratch)
        self.start, self.mid, self.end, self.mid_step = start, mid, end, mid_step


def _call(body, args, *, name, grid, in_specs, out_specs, out_shape, scratch_shapes=(), sem=None, comm=None):
    scratch_shapes = list(scratch_shapes)
    if comm is None:
        outs = pl.pallas_call(
            body, name=name, grid=grid, in_specs=in_specs, out_specs=out_specs, out_shape=out_shape,
            scratch_shapes=scratch_shapes, compiler_params=_cparams(sem))(*args)
        return list(outs), []

    n_in, n_out, n_sc = len(args), len(out_shape), len(scratch_shapes)
    n_ci, n_co = len(comm.arrays), len(comm.out_shape)
    n_steps = 1
    for g in grid:
        n_steps *= g

    def hosted(*refs):
        ins, refs = refs[:n_in], refs[n_in:]
        c_ins, refs = refs[:n_ci], refs[n_ci:]
        outs, refs = refs[:n_out], refs[n_out:]
        c_outs, refs = refs[:n_co], refs[n_co:]
        scratch, c_sems = refs[:n_sc], refs[n_sc:]
        step = 0
        for i, g in enumerate(grid):
            step = step * g + pl.program_id(i)

        @pl.when(step == 0)
        def _():
            comm.start(c_ins, c_outs, c_sems)

        body(*ins, *outs, *scratch)

        if comm.mid is not None:
            @pl.when(step == comm.mid_step)
            def _():
                comm.mid(c_ins, c_outs, c_sems)

        @pl.when(step == n_steps - 1)
        def _():
            comm.end(c_ins, c_outs, c_sems)

    outs = pl.pallas_call(
        hosted, name=name, grid=grid,
        in_specs=list(in_specs) + [ANY] * n_ci, out_specs=list(out_specs) + [ANY] * n_co,
        out_shape=list(out_shape) + comm.out_shape, scratch_shapes=scratch_shapes + comm.scratch,
        compiler_params=_cparams(("arbitrary",) * len(grid)))(*args, *comm.arrays)
    return list(outs[:n_out]), list(outs[n_out:])


def _mm_nn(a, wb, out_dtype, name):
    M, K = a.shape
    NB, K2, Nb = wb.shape
    assert K == K2
    tm = min(M, 1024)
    tk = min(K, 2048)
    tn = _pick(Nb, (512, 1408, 256))
    nk = K // tk
    nn = Nb // tn

    def body(a_ref, b_ref, o_ref, *acc):
        part = jnp.dot(a_ref[...], b_ref[...], preferred_element_type=F32)
        if nk == 1:
            o_ref[...] = part.astype(out_dtype)
        else:
            acc_ref, = acc
            k = pl.program_id(3)

            @pl.when(k == 0)
            def _():
                acc_ref[...] = part

            @pl.when(k > 0)
            def _():
                acc_ref[...] += part

            @pl.when(k == nk - 1)
            def _():
                o_ref[...] = acc_ref[...].astype(out_dtype)

    return pl.pallas_call(
        body, name=name,
        grid=(M // tm, NB, nn, nk),
        in_specs=[pl.BlockSpec((tm, tk), lambda m, j, n, k: (m, k)),
                  pl.BlockSpec((None, tk, tn), lambda m, j, n, k: (j, k, n))],
        out_specs=pl.BlockSpec((tm, tn), lambda m, j, n, k: (m, j * nn + n)),
        out_shape=jax.ShapeDtypeStruct((M, NB * Nb), out_dtype),
        scratch_shapes=[] if nk == 1 else [pltpu.VMEM((tm, tn), F32)],
        compiler_params=_cparams(("parallel", "parallel", "parallel", "arbitrary")),
    )(a, wb)


def _mm_nt(a, wb, out_dtype, name, comm=None):
    M, N = a.shape
    NB, K, Nb = wb.shape
    assert N == NB * Nb
    tm = min(M, 1024)
    tko = _pick(K, (1024,))
    tc = _pick(Nb, (1024, 1408, 256))
    nc = Nb // tc
    nsteps = NB * nc

    def body(a_ref, b_ref, o_ref, acc_ref):
        step = pl.program_id(2) * nc + pl.program_id(3)
        part = lax.dot_general(a_ref[...], b_ref[...], (((1,), (1,)), ((), ())), preferred_element_type=F32)

        @pl.when(step == 0)
        def _():
            acc_ref[...] = part

        @pl.when(step > 0)
        def _():
            acc_ref[...] += part

        @pl.when(step == nsteps - 1)
        def _():
            o_ref[...] = acc_ref[...].astype(out_dtype)

    (out,), landed = _call(
        body, (a, wb), name=name,
        grid=(M // tm, K // tko, NB, nc),
        in_specs=[pl.BlockSpec((tm, tc), lambda m, ko, j, c: (m, j * nc + c)),
                  pl.BlockSpec((None, tko, tc), lambda m, ko, j, c: (j, ko, c))],
        out_specs=[pl.BlockSpec((tm, tko), lambda m, ko, j, c: (m, ko))],
        out_shape=[jax.ShapeDtypeStruct((M, K), out_dtype)],
        scratch_shapes=[pltpu.VMEM((tm, tko), F32)],
        sem=("parallel", "parallel", "arbitrary", "arbitrary"), comm=comm)
    return out, landed


def _mm_tn(a, g, nb, out_dtype, name):
    M, Ka = a.shape
    M2, N = g.shape
    assert M == M2 and N % nb == 0
    Nb = N // nb
    tka = _pick(Ka, (1024,))
    tn = _pick(Nb, (512, 1408, 256))
    nn = Nb // tn

    def body(a_ref, g_ref, o_ref):
        o_ref[...] = lax.dot_general(a_ref[...], g_ref[...], (((0,), (0,)), ((), ())),
                                     preferred_element_type=F32).astype(out_dtype)

    return pl.pallas_call(
        body, name=name,
        grid=(Ka // tka, nb, nn),
        in_specs=[pl.BlockSpec((M, tka), lambda ka, j, n: (0, ka)),
                  pl.BlockSpec((M, tn), lambda ka, j, n: (0, j * nn + n))],
        out_specs=pl.BlockSpec((None, tka, tn), lambda ka, j, n: (j, ka, n)),
        out_shape=jax.ShapeDtypeStruct((nb, Ka, Nb), out_dtype),
        compiler_params=_cparams(("parallel", "parallel", "parallel")),
    )(a, g)


ROW_TILE = 256


def _rms_fwd(x, w, name):
    T, Dm = x.shape

    def body(x_ref, w_ref, u_ref):
        xv = x_ref[...]
        r = lax.rsqrt(jnp.mean(xv * xv, axis=-1, keepdims=True) + EPS)
        u_ref[...] = (xv * r * w_ref[...]).astype(BF16)

    return pl.pallas_call(
        body, name=name, grid=(T // ROW_TILE,),
        in_specs=[pl.BlockSpec((ROW_TILE, Dm), lambda i: (i, 0)), pl.BlockSpec((1, Dm), lambda i: (0, 0))],
        out_specs=pl.BlockSpec((ROW_TILE, Dm), lambda i: (i, 0)),
        out_shape=jax.ShapeDtypeStruct((T, Dm), BF16),
        compiler_params=_cparams(("parallel",)),
    )(x, w)


def _resid_rms_fwd(x, mix, w, name):
    T, Dm = x.shape

    def body(x_ref, m_ref, w_ref, h_ref, u_ref):
        h = x_ref[...] + m_ref[...]
        h_ref[...] = h
        r = lax.rsqrt(jnp.mean(h * h, axis=-1, keepdims=True) + EPS)
        u_ref[...] = (h * r * w_ref[...]).astype(BF16)

    row = pl.BlockSpec((ROW_TILE, Dm), lambda i: (i, 0))
    return pl.pallas_call(
        body, name=name, grid=(T // ROW_TILE,),
        in_specs=[row, row, pl.BlockSpec((1, Dm), lambda i: (0, 0))],
        out_specs=[row, row],
        out_shape=[jax.ShapeDtypeStruct((T, Dm), F32), jax.ShapeDtypeStruct((T, Dm), BF16)],
        compiler_params=_cparams(("parallel",)),
    )(x, mix, w)


def _loss_head(h1, mlp, wf, target, name):
    T, Dm = h1.shape

    def body(h_ref, m_ref, w_ref, t_ref, loss_ref, dh_ref, dhb_ref, dw_ref):
        i = pl.program_id(0)
        h = h_ref[...] + m_ref[...]
        r = lax.rsqrt(jnp.mean(h * h, axis=-1, keepdims=True) + EPS)
        xh = h * r
        wv = w_ref[...]
        e = xh * wv - t_ref[...]
        part = 0.5 * jnp.sum(jnp.mean(e * e, axis=-1, keepdims=True), axis=0, keepdims=True)
        dy = e * (1.0 / Dm)
        dw = jnp.sum(dy * xh, axis=0, keepdims=True)
        gy = dy * wv
        dh = r * (gy - xh * jnp.mean(gy * xh, axis=-1, keepdims=True))
        dh_ref[...] = dh
        dhb_ref[...] = dh.astype(BF16)

        @pl.when(i == 0)
        def _():
            loss_ref[...] = jnp.zeros_like(loss_ref)
            dw_ref[...] = jnp.zeros_like(dw_ref)

        loss_ref[...] += jnp.broadcast_to(part, loss_ref.shape)
        dw_ref[...] += dw

    row = pl.BlockSpec((ROW_TILE, Dm), lambda i: (i, 0))
    vec = pl.BlockSpec((1, Dm), lambda i: (0, 0))
    return pl.pallas_call(
        body, name=name, grid=(T // ROW_TILE,),
        in_specs=[row, row, vec, row],
        out_specs=[pl.BlockSpec((8, 128), lambda i: (0, 0)), row, row, vec],
        out_shape=[jax.ShapeDtypeStruct((8, 128), F32), jax.ShapeDtypeStruct((T, Dm), F32),
                   jax.ShapeDtypeStruct((T, Dm), BF16), jax.ShapeDtypeStruct((1, Dm), F32)],
        compiler_params=_cparams(("arbitrary",)),
    )(h1, mlp, wf, target)


def _rms_bwd(dyn, x, w, dres, name):
    T, Dm = x.shape

    def body(g_ref, x_ref, w_ref, r_ref, dx_ref, dxb_ref, dw_ref):
        i = pl.program_id(0)
        xv = x_ref[...]
        r = lax.rsqrt(jnp.mean(xv * xv, axis=-1, keepdims=True) + EPS)
        xh = xv * r
        g = g_ref[...]
        dw = jnp.sum(g * xh, axis=0, keepdims=True)
        gy = g * w_ref[...]
        dx = r_ref[...] + r * (gy - xh * jnp.mean(gy * xh, axis=-1, keepdims=True))
        dx_ref[...] = dx
        dxb_ref[...] = dx.astype(BF16)

        @pl.when(i == 0)
        def _():
            dw_ref[...] = jnp.zeros_like(dw_ref)

        dw_ref[...] += dw

    row = pl.BlockSpec((ROW_TILE, Dm), lambda i: (i, 0))
    vec = pl.BlockSpec((1, Dm), lambda i: (0, 0))
    return pl.pallas_call(
        body, name=name, grid=(T // ROW_TILE,),
        in_specs=[row, row, vec, row],
        out_specs=[row, row, vec],
        out_shape=[jax.ShapeDtypeStruct((T, Dm), F32), jax.ShapeDtypeStruct((T, Dm), BF16),
                   jax.ShapeDtypeStruct((1, Dm), F32)],
        compiler_params=_cparams(("arbitrary",)),
    )(dyn, x, w, dres)


COL_TILE = 2048


def _relu2_fwd(a, name):
    T, N = a.shape

    def body(a_ref, r_ref):
        ra = jnp.maximum(a_ref[...], 0.0)
        r_ref[...] = (ra * ra).astype(BF16)

    blk = pl.BlockSpec((ROW_TILE, COL_TILE), lambda i, j: (i, j))
    return pl.pallas_call(
        body, name=name, grid=(T // ROW_TILE, N // COL_TILE), in_specs=[blk], out_specs=blk,
        out_shape=jax.ShapeDtypeStruct((T, N), BF16),
        compiler_params=_cparams(("parallel", "parallel")),
    )(a)


def _relu2_bwd(dr, a, name):
    T, N = a.shape

    def body(dr_ref, a_ref, da_ref):
        da_ref[...] = (dr_ref[...] * (2.0 * jnp.maximum(a_ref[...], 0.0))).astype(BF16)

    blk = pl.BlockSpec((ROW_TILE, COL_TILE), lambda i, j: (i, j))
    return pl.pallas_call(
        body, name=name, grid=(T // ROW_TILE, N // COL_TILE), in_specs=[blk, blk], out_specs=blk,
        out_shape=jax.ShapeDtypeStruct((T, N), BF16),
        compiler_params=_cparams(("parallel", "parallel")),
    )(dr, a)


GATE_TILE = 1024


def _merge_fwd(z, pa, pb, name):
    T, Dm = pa.shape

    def body(za_ref, zb_ref, pa_ref, pb_ref, m_ref):
        m_ref[...] = (_sigmoid(za_ref[...]) * pa_ref[...] + _sigmoid(zb_ref[...]) * pb_ref[...]).astype(BF16)

    blk = pl.BlockSpec((ROW_TILE, GATE_TILE), lambda i, j: (i, j))
    return pl.pallas_call(
        body, name=name, grid=(T // ROW_TILE, Dm // GATE_TILE),
        in_specs=[pl.BlockSpec((ROW_TILE, GATE_TILE), lambda i, j: (i, COL_GATE_A + j)),
                  pl.BlockSpec((ROW_TILE, GATE_TILE), lambda i, j: (i, COL_GATE_B + j)), blk, blk],
        out_specs=blk,
        out_shape=jax.ShapeDtypeStruct((T, Dm), BF16),
        compiler_params=_cparams(("parallel", "parallel")),
    )(z, z, pa, pb)


def _merge_bwd(dm, z, pa, pb, name):
    T, Dm = pa.shape

    def body(dm_ref, za_ref, zb_ref, pa_ref, pb_ref, dpa_ref, dpb_ref, dga_ref, dgb_ref):
        d = dm_ref[...]
        ga = _sigmoid(za_ref[...])
        gb = _sigmoid(zb_ref[...])
        dpa_ref[...] = (d * ga).astype(BF16)
        dpb_ref[...] = (d * gb).astype(BF16)
        dga_ref[...] = (d * pa_ref[...] * ga * (1.0 - ga)).astype(BF16)
        dgb_ref[...] = (d * pb_ref[...] * gb * (1.0 - gb)).astype(BF16)

    blk = pl.BlockSpec((ROW_TILE, GATE_TILE), lambda i, j: (i, j))
    out = jax.ShapeDtypeStruct((T, Dm), BF16)
    return pl.pallas_call(
        body, name=name, grid=(T // ROW_TILE, Dm // GATE_TILE),
        in_specs=[blk, pl.BlockSpec((ROW_TILE, GATE_TILE), lambda i, j: (i, COL_GATE_A + j)),
                  pl.BlockSpec((ROW_TILE, GATE_TILE), lambda i, j: (i, COL_GATE_B + j)), blk, blk],
        out_specs=[blk, blk, blk, blk],
        out_shape=[out, out, out, out],
        compiler_params=_cparams(("parallel", "parallel")),
    )(dm, z, z, pa, pb)


def _dot_hi(a, b, dims):
    return lax.dot_general(a, b, (dims, ((), ())), precision=HIGHEST, preferred_element_type=F32)


NN = ((1,), (0,))
NT = ((1,), (1,))
TN = ((0,), (0,))


def _hg_gates(hq, hf, lb):
    sq = _sigmoid(hq)
    q = hq * sq * (HG_DK ** -0.5)
    f = _sigmoid(hf)
    g = lb + (1.0 - lb) * f
    return q, sq, f, g, jnp.log(g), 1.0 - g


def _tri(lower):
    r = lax.broadcasted_iota(jnp.int32, (CHUNK, CHUNK), 0)
    c = lax.broadcasted_iota(jnp.int32, (CHUNK, CHUNK), 1)
    return jnp.where((r >= c) if lower else (r <= c), 1.0, 0.0).astype(F32)


def _hgrn2_fwd(z, lb_logits, hg_norm_w, name, comm=None):
    T = z.shape[0]
    n_chunks = T // CHUNK

    def body(hq_ref, hf_ref, hi_ref, hg_ref, lbl_ref, nw_ref, o_ref, ya_ref, sall_ref, st_ref):
        lbl = lbl_ref[...]
        lb = 1.0 / (1.0 + jnp.exp(lbl[1:2, :] - lbl[0:1, :]))
        st_ref[...] = jnp.zeros_like(st_ref)
        tri = _tri(True)
        row8 = lax.broadcasted_iota(jnp.int32, (8, HG_DK), 0)

        def chunk(c, carry):
            rows = pl.ds(pl.multiple_of(c * CHUNK, CHUNK), CHUNK)
            q, _, _, _, lg, kk = _hg_gates(hq_ref[rows, :], hf_ref[rows, :], lb)
            v = hi_ref[rows, :]
            b = _dot_hi(tri, lg, NN)
            st = st_ref[...]
            sall_ref[c] = st
            o_inter = _dot_hi(q * jnp.exp(b), st, NT)
            for g8 in range(CHUNK // 8):
                n = 8 * (g8 + 1)
                bs, ks, vs = b[:n], kk[:n], v[:n]
                sidx = lax.broadcasted_iota(jnp.int32, (n, HG_DK), 0)
                blk = o_inter[8 * g8:n]
                for i in range(8):
                    t = 8 * g8 + i
                    e = jnp.where(sidx <= t, jnp.exp(b[t:t + 1] - bs), 0.0)
                    p = jnp.sum(e * ks * q[t:t + 1], axis=1, keepdims=True)
                    ot = jnp.sum(p * vs, axis=0, keepdims=True)
                    blk = blk + jnp.where(row8 == i, ot, 0.0)
                o_ref[pl.ds(pl.multiple_of(c * CHUNK + 8 * g8, 8), 8), :] = blk
            bl = b[CHUNK - 1:CHUNK]
            ke = kk * jnp.exp(bl - b)
            st_ref[...] = st * jnp.exp(bl) + _dot_hi(v, ke, TN)
            return carry

        lax.fori_loop(0, n_chunks, chunk, 0)
        o = o_ref[...]
        r = lax.rsqrt(jnp.mean(o * o, axis=-1, keepdims=True) + EPS)
        hg = hg_ref[...]
        ya_ref[...] = (o * r * nw_ref[...] * (hg * _sigmoid(hg))).astype(BF16)

    def col(base):
        return pl.BlockSpec((T, HG_DK), lambda h: (0, base + h))

    return _call(
        body, (z, z, z, z, lb_logits, hg_norm_w), name=name, grid=(HG_HEADS,),
        in_specs=[col(COL_HQ), col(COL_HF), col(COL_HI), col(COL_HG),
                  pl.BlockSpec((2, HG_DK), lambda h: (0, h)), pl.BlockSpec((1, HG_DK), lambda h: (0, 0))],
        out_specs=[col(0), col(0), pl.BlockSpec((None, n_chunks, HG_DK, HG_DK), lambda h: (h, 0, 0, 0))],
        out_shape=[jax.ShapeDtypeStruct((T, HG_WIDTH), F32), jax.ShapeDtypeStruct((T, HG_WIDTH), BF16),
                   jax.ShapeDtypeStruct((HG_HEADS, n_chunks, HG_DK, HG_DK), F32)],
        scratch_shapes=[pltpu.VMEM((HG_DK, HG_DK), F32)],
        sem=("parallel",), comm=comm)


def _hgrn2_bwd(z, lb_logits, hg_norm_w, o_raw, s_all, dya, name):
    T = z.shape[0]
    n_chunks = T // CHUNK

    def body(hq_ref, hf_ref, hi_ref, hg_ref, lbl_ref, nw_ref, o_ref, sall_ref, dya_ref,
             dhq_ref, dhf_ref, dhi_ref, dhg_ref, dlbl_ref, dnw_ref,
             do_ref, dst_ref, dq_ref, dk_ref, dv_ref, dlb_ref):
        h = pl.program_id(0)
        lbl = lbl_ref[...]
        lb = 1.0 / (1.0 + jnp.exp(lbl[1:2, :] - lbl[0:1, :]))

        o = o_ref[...]
        r = lax.rsqrt(jnp.mean(o * o, axis=-1, keepdims=True) + EPS)
        oh = o * r
        nw = nw_ref[...]
        hg = hg_ref[...]
        sg = _sigmoid(hg)
        dy = dya_ref[...]
        d_on = dy * (hg * sg)
        dhg_ref[...] = (dy * (oh * nw) * (sg * (1.0 + hg * (1.0 - sg)))).astype(BF16)
        dnw = jnp.sum(d_on * oh, axis=0, keepdims=True)
        gy = d_on * nw
        do_ref[...] = r * (gy - oh * jnp.mean(gy * oh, axis=-1, keepdims=True))

        @pl.when(h == 0)
        def _():
            dnw_ref[...] = jnp.zeros_like(dnw_ref)

        dnw_ref[...] += jnp.broadcast_to(dnw, dnw_ref.shape)

        dst_ref[...] = jnp.zeros_like(dst_ref)
        dlb_ref[...] = jnp.zeros_like(dlb_ref)
        tri = _tri(True)
        tri_t = _tri(False)
        row8 = lax.broadcasted_iota(jnp.int32, (8, HG_DK), 0)

        def chunk(ci, carry):
            c = n_chunks - 1 - ci
            rows = pl.ds(pl.multiple_of(c * CHUNK, CHUNK), CHUNK)
            hq = hq_ref[rows, :]
            q, sq, f, g, lg, kk = _hg_gates(hq, hf_ref[rows, :], lb)
            v = hi_ref[rows, :]
            do = do_ref[rows, :]
            b = _dot_hi(tri, lg, NN)
            eb = jnp.exp(b)
            bl = b[CHUNK - 1:CHUNK]
            ebl = jnp.exp(bl)
            ekb = jnp.exp(bl - b)
            qe = q * eb
            ke = kk * ekb
            st = sall_ref[c]
            dst = dst_ref[...]
            dqe = _dot_hi(do, st, NN)
            dke = _dot_hi(v, dst, NN)
            dv_inter = _dot_hi(ke, dst, NT)
            d_ebl = jnp.sum(st * dst, axis=0, keepdims=True)
            dst_ref[...] = dst * ebl + _dot_hi(do, qe, TN)

            dk_ref[...] = jnp.zeros_like(dk_ref)
            dv_ref[...] = jnp.zeros_like(dv_ref)
            for g8 in range(CHUNK // 8):
                n = 8 * (g8 + 1)
                bs, ks, vs = b[:n], kk[:n], v[:n]
                sidx = lax.broadcasted_iota(jnp.int32, (n, HG_DK), 0)
                blk = jnp.zeros((8, HG_DK), F32)
                for i in range(8):
                    t = 8 * g8 + i
                    qt = q[t:t + 1]
                    dot_ = do[t:t + 1]
                    e = jnp.where(sidx <= t, jnp.exp(b[t:t + 1] - bs), 0.0)
                    w = e * ks
                    p = jnp.sum(w * qt, axis=1, keepdims=True)
                    dsc = jnp.sum(vs * dot_, axis=1, keepdims=True)
                    dqt = jnp.sum(dsc * w, axis=0, keepdims=True)
                    blk = blk + jnp.where(row8 == i, dqt, 0.0)
                    dk_ref[0:n, :] += dsc * e * qt
                    dv_ref[0:n, :] += p * dot_
                dq_ref[8 * g8:n, :] = blk
            dq_i = dq_ref[...]
            dk_i = dk_ref[...]
            dke_ke = dke * ke
            db = q * dq_i - kk * dk_i + dqe * qe - dke_ke
            db_last = jnp.sum(dke_ke, axis=0, keepdims=True) + d_ebl * ebl
            dlg = _dot_hi(tri_t, db, NN) + db_last
            dq = dq_i + dqe * eb
            dkk = dk_i + dke * ekb
            dg = dlg / g - dkk
            dhq_ref[rows, :] = (dq * (HG_DK ** -0.5) * (sq * (1.0 + hq * (1.0 - sq)))).astype(BF16)
            dhf_ref[rows, :] = (dg * (1.0 - lb) * f * (1.0 - f)).astype(BF16)
            dhi_ref[rows, :] = (dv_ref[...] + dv_inter).astype(BF16)
            dlb_ref[...] += jnp.sum(dg * (1.0 - f), axis=0, keepdims=True)
            return carry

        lax.fori_loop(0, n_chunks, chunk, 0)
        dl0 = dlb_ref[...] * lb * (1.0 - lb)
        dlbl_ref[0:1, :] = dl0
        dlbl_ref[1:2, :] = -dl0

    def col(base):
        return pl.BlockSpec((T, HG_DK), lambda h: (0, base + h))

    outb = jax.ShapeDtypeStruct((T, HG_WIDTH), BF16)
    return pl.pallas_call(
        body, name=name, grid=(HG_HEADS,),
        in_specs=[col(COL_HQ), col(COL_HF), col(COL_HI), col(COL_HG),
                  pl.BlockSpec((2, HG_DK), lambda h: (0, h)), pl.BlockSpec((1, HG_DK), lambda h: (0, 0)),
                  col(0), pl.BlockSpec((None, n_chunks, HG_DK, HG_DK), lambda h: (h, 0, 0, 0)), col(0)],
        out_specs=[col(0), col(0), col(0), col(0), pl.BlockSpec((2, HG_DK), lambda h: (0, h)),
                   pl.BlockSpec((8, HG_DK), lambda h: (0, 0))],
        out_shape=[outb, outb, outb, outb, jax.ShapeDtypeStruct((2, HG_WIDTH), F32),
                   jax.ShapeDtypeStruct((8, HG_DK), F32)],
        scratch_shapes=[pltpu.VMEM((T, HG_DK), F32), pltpu.VMEM((HG_DK, HG_DK), F32),
                        pltpu.VMEM((CHUNK, HG_DK), F32), pltpu.VMEM((CHUNK, HG_DK), F32),
                        pltpu.VMEM((CHUNK, HG_DK), F32), pltpu.VMEM((1, HG_DK), F32)],
        compiler_params=_cparams(("arbitrary",)),
    )(z, z, z, z, lb_logits, hg_norm_w, o_raw, s_all, dya)


def _rel_onehot(t):
    r = lax.broadcasted_iota(jnp.int32, (N_REL_PAD, BAND), 0)
    j = lax.broadcasted_iota(jnp.int32, (N_REL_PAD, BAND), 1)
    idx = jnp.clip(t + PAD - j, -REL_CLIP, REL_CLIP) + REL_CLIP
    return jnp.where(r == idx, 1.0, 0.0).astype(F32)


def _bias_expand(rel, name):
    def body(rel_ref, out_ref):
        out_ref[...] = _dot_hi(rel_ref[...], _rel_onehot(pl.program_id(0)), NN)

    return pl.pallas_call(
        body, name=name, grid=(CHUNK,),
        in_specs=[pl.BlockSpec((AT_HEADS, N_REL_PAD), lambda t: (0, 0))],
        out_specs=pl.BlockSpec((None, AT_HEADS, BAND), lambda t: (t, 0, 0)),
        out_shape=jax.ShapeDtypeStruct((CHUNK, AT_HEADS, BAND), F32),
        compiler_params=_cparams(("parallel",)),
    )(rel)


def _bias_reduce(dbias_t, name):
    def body(db_ref, out_ref):
        t = pl.program_id(0)

        @pl.when(t == 0)
        def _():
            out_ref[...] = jnp.zeros_like(out_ref)

        out_ref[...] += _dot_hi(db_ref[...], _rel_onehot(t), NT)

    return pl.pallas_call(
        body, name=name, grid=(CHUNK,),
        in_specs=[pl.BlockSpec((None, AT_HEADS, BAND), lambda t: (t, 0, 0))],
        out_specs=pl.BlockSpec((AT_HEADS, N_REL_PAD), lambda t: (0, 0)),
        out_shape=jax.ShapeDtypeStruct((AT_HEADS, N_REL_PAD), F32),
        compiler_params=_cparams(("arbitrary",)),
    )(dbias_t)


def _band_probs(qc, kb, bias, c):
    s = lax.dot_general(qc, kb, (NT, ((), ())), preferred_element_type=F32) * (AT_DH ** -0.5) + bias
    j = lax.broadcasted_iota(jnp.int32, (CHUNK, BAND), 1)
    s = jnp.where(j + c * CHUNK >= PAD, s, -jnp.inf)
    p = jnp.exp(s - jnp.max(s, axis=-1, keepdims=True))
    return p / jnp.sum(p, axis=-1, keepdims=True)


def _fill_padded(dst_ref, src_ref, T):
    for hh in range(2):
        dst_ref[hh, 0:PAD, :] = jnp.zeros((PAD, AT_DH), BF16)
        dst_ref[hh, PAD:PAD + T, :] = src_ref[:, hh * AT_DH:(hh + 1) * AT_DH].astype(BF16)


def _attn_fwd(z, bias, name, comm=None):
    T = z.shape[0]
    n_chunks = T // CHUNK

    def body(q_ref, k_ref, v_ref, bias_ref, y_ref, kp_ref, vp_ref):
        _fill_padded(kp_ref, k_ref, T)
        _fill_padded(vp_ref, v_ref, T)

        def chunk(c, carry):
            rows = pl.ds(pl.multiple_of(c * CHUNK, CHUNK), CHUNK)
            band = pl.ds(pl.multiple_of(c * CHUNK, CHUNK), BAND)
            for hh in range(2):
                lanes = slice(hh * AT_DH, (hh + 1) * AT_DH)
                qc = q_ref[rows, lanes].astype(BF16)
                p = _band_probs(qc, kp_ref[hh, band, :], bias_ref[hh], c)
                y_ref[rows, lanes] = jnp.dot(p.astype(BF16), vp_ref[hh, band, :],
                                             preferred_element_type=F32).astype(BF16)
            return carry

        lax.fori_loop(0, n_chunks, chunk, 0, unroll=2)

    def col(base):
        return pl.BlockSpec((T, 128), lambda h: (0, base + h))

    return _call(
        body, (z, z, z, bias), name=name, grid=(AT_HEADS // 2,),
        in_specs=[col(COL_AQ), col(COL_AK), col(COL_AV), pl.BlockSpec((2, CHUNK, BAND), lambda h: (h, 0, 0))],
        out_specs=[col(0)],
        out_shape=[jax.ShapeDtypeStruct((T, AT_WIDTH), BF16)],
        scratch_shapes=[pltpu.VMEM((2, PAD + T, AT_DH), BF16), pltpu.VMEM((2, PAD + T, AT_DH), BF16)],
        sem=("parallel",), comm=comm)


def _attn_bwd(z, bias, bias_kq, dyb, name, comm=None):
    T = z.shape[0]
    n_chunks = T // CHUNK

    def body(q_ref, k_ref, v_ref, bias_ref, biast_ref, dy_ref, dq_ref, dk_ref, dv_ref, dbias_ref,
             kp_ref, vp_ref, dkp_ref, dvp_ref):
        _fill_padded(kp_ref, k_ref, T)
        _fill_padded(vp_ref, v_ref, T)
        dkp_ref[...] = jnp.zeros_like(dkp_ref)
        dvp_ref[...] = jnp.zeros_like(dvp_ref)
        dbias_ref[...] = jnp.zeros_like(dbias_ref)

        def chunk(c, carry):
            rows = pl.ds(pl.multiple_of(c * CHUNK, CHUNK), CHUNK)
            band = pl.ds(pl.multiple_of(c * CHUNK, CHUNK), BAND)
            for hh in range(2):
                lanes = slice(hh * AT_DH, (hh + 1) * AT_DH)
                qc = q_ref[rows, lanes].astype(BF16)
                doc = dy_ref[rows, lanes].astype(BF16)
                kb = kp_ref[hh, band, :]
                vb = vp_ref[hh, band, :]
                p = _band_probs(qc, kb, bias_ref[hh], c)
                dp = lax.dot_general(doc, vb, (NT, ((), ())), preferred_element_type=F32)
                ds = p * (dp - jnp.sum(dp * p, axis=-1, keepdims=True))
                dbias_ref[hh] += ds
                dq_ref[rows, lanes] = (jnp.dot(ds.astype(BF16), kb, preferred_element_type=F32)
                                       * (AT_DH ** -0.5)).astype(BF16)
                st = (lax.dot_general(kb, qc, (NT, ((), ())), preferred_element_type=F32) * (AT_DH ** -0.5)
                      + biast_ref[hh])
                j = lax.broadcasted_iota(jnp.int32, (BAND, CHUNK), 0)
                st = jnp.where(j + c * CHUNK >= PAD, st, -jnp.inf)
                pt = jnp.exp(st - jnp.max(st, axis=0, keepdims=True))
                pt = pt / jnp.sum(pt, axis=0, keepdims=True)
                dpt = lax.dot_general(vb, doc, (NT, ((), ())), preferred_element_type=F32)
                dst = pt * (dpt - jnp.sum(dpt * pt, axis=0, keepdims=True))
                dkp_ref[hh, band, :] += (jnp.dot(dst.astype(BF16), qc, preferred_element_type=F32)
                                         * (AT_DH ** -0.5))
                dvp_ref[hh, band, :] += jnp.dot(pt.astype(BF16), doc, preferred_element_type=F32)
            return carry

        lax.fori_loop(0, n_chunks, chunk, 0)
        for hh in range(2):
            lanes = slice(hh * AT_DH, (hh + 1) * AT_DH)
            dk_ref[:, lanes] = dkp_ref[hh, PAD:PAD + T, :].astype(BF16)
            dv_ref[:, lanes] = dvp_ref[hh, PAD:PAD + T, :].astype(BF16)

    def col(base):
        return pl.BlockSpec((T, 128), lambda h: (0, base + h))

    outb = jax.ShapeDtypeStruct((T, AT_WIDTH), BF16)
    return _call(
        body, (z, z, z, bias, bias_kq, dyb), name=name, grid=(AT_HEADS // 2,),
        in_specs=[col(COL_AQ), col(COL_AK), col(COL_AV), pl.BlockSpec((2, CHUNK, BAND), lambda h: (h, 0, 0)),
                  pl.BlockSpec((2, BAND, CHUNK), lambda h: (h, 0, 0)), col(0)],
        out_specs=[col(0), col(0), col(0), pl.BlockSpec((2, CHUNK, BAND), lambda h: (h, 0, 0))],
        out_shape=[outb, outb, outb, jax.ShapeDtypeStruct((AT_HEADS, CHUNK, BAND), F32)],
        scratch_shapes=[pltpu.VMEM((2, PAD + T, AT_DH), BF16), pltpu.VMEM((2, PAD + T, AT_DH), BF16),
                        pltpu.VMEM((2, PAD + T, AT_DH), F32), pltpu.VMEM((2, PAD + T, AT_DH), F32)],
        sem=("parallel",), comm=comm)


def _local_step(x, target, lb_logits, hg_norm_w, rel_bias, norm_mix_w, norm_mlp_w, norm_final_w,
                w_in, rest, exchanges=None):
    ex = exchanges
    rel = jnp.pad(rel_bias, ((0, 0), (0, N_REL_PAD - N_REL)))

    u = _rms_fwd(x, norm_mix_w, "rms_mix_fwd")
    z = _mm_nn(u, w_in, F32, "mm_in_fwd")
    (o_raw, y_a, s_all), got_a = _hgrn2_fwd(z, lb_logits, hg_norm_w, "hgrn2_fwd",
                                            comm=ex and ex.gather(rest[:4], mid_step=HG_HEADS - 2))
    bias_t = _bias_expand(rel, "bias_expand")
    bias = jnp.transpose(bias_t, (1, 0, 2))
    bias_kq = jnp.transpose(bias_t, (1, 2, 0))
    (y_b,), got_b = _attn_fwd(z, bias, "attn_fwd", comm=ex and ex.gather(rest[4:], mid_step=AT_HEADS // 2 - 2))
    w_a, w_b, w_out, w_up, w_down = (got_a + got_b) if ex else rest
    w_out1 = w_out.reshape(1, D_MODEL, D_MODEL)
    w_down1 = w_down.reshape(1, D_FF, D_MODEL)
    pa = _mm_nn(y_a, w_a, F32, "mm_a_fwd")
    pb = _mm_nn(y_b, w_b, F32, "mm_b_fwd")
    merged = _merge_fwd(z, pa, pb, "merge_fwd")
    mix = _mm_nn(merged, w_out1, F32, "mm_out_fwd")
    h1, u2 = _resid_rms_fwd(x, mix, norm_mlp_w, "rms_mlp_fwd")
    a = _mm_nn(u2, w_up, F32, "mm_up_fwd")
    r = _relu2_fwd(a, "relu2_fwd")
    mlp = _mm_nn(r, w_down1, F32, "mm_down_fwd")
    loss, dh2, dh2b, g_nf = _loss_head(h1, mlp, norm_final_w, target, "loss_head")

    g_down = _mm_tn(r, dh2b, 1, BF16, "mm_down_wgrad").reshape(N_DEV, D_FF // N_DEV, D_MODEL)
    dr, _ = _mm_nt(dh2b, w_down1, F32, "mm_down_dgrad")
    da = _relu2_bwd(dr, a, "relu2_bwd")
    g_up = _mm_tn(u2, da, N_DEV, BF16, "mm_up_wgrad")
    du2, _ = _mm_nt(da, w_up, F32, "mm_up_dgrad")
    dh1, dh1b, g_nmlp = _rms_bwd(du2, h1, norm_mlp_w, dh2, "rms_mlp_bwd")

    g_out = _mm_tn(merged, dh1b, 1, BF16, "mm_out_wgrad").reshape(N_DEV, D_MODEL // N_DEV, D_MODEL)
    dmerged, _ = _mm_nt(dh1b, w_out1, F32, "mm_out_dgrad")
    dpa, dpb, dga, dgb = _merge_bwd(dmerged, z, pa, pb, "merge_bwd")
    g_a = _mm_tn(y_a, dpa, N_DEV, BF16, "mm_a_wgrad")
    g_b = _mm_tn(y_b, dpb, N_DEV, BF16, "mm_b_wgrad")
    dya, _ = _mm_nt(dpa, w_a, F32, "mm_a_dgrad")
    dyb, _ = _mm_nt(dpb, w_b, F32, "mm_b_dgrad")
    early = [g_a, g_b, g_out, g_up, g_down]
    (daq, dak, dav, dbias), early_parts = _attn_bwd(
        z, bias, bias_kq, dyb, "attn_bwd", comm=ex and ex.scatter(ex.pair_sums(early, "early")))
    dhq, dhf, dhi, dhg, g_lbl, g_hgw = _hgrn2_bwd(z, lb_logits, hg_norm_w, o_raw, s_all, dya, "hgrn2_bwd")
    g_rel = _bias_reduce(jnp.transpose(dbias, (1, 0, 2)), "bias_reduce")[:, :N_REL]
    dz = jnp.concatenate([dhq, dhf, dhi, dhg, daq, dak, dav, dga, dgb], axis=1)
    g_in = _mm_tn(u, dz, N_DEV, BF16, "mm_in_wgrad")
    du, late_parts = _mm_nt(dz, w_in, F32, "mm_in_dgrad", comm=ex and ex.scatter(ex.pair_sums([g_in], "late")))
    grad_x, _, g_nmix = _rms_bwd(du, x, norm_mix_w, dh1, "rms_mix_bwd")

    small = dict(lb_logits=g_lbl, hg_norm_w=g_hgw[0:1], rel_bias=g_rel, norm_mix_w=g_nmix, norm_mlp_w=g_nmlp,
                 norm_final_w=g_nf)
    grads = (late_parts + early_parts) if ex else ([g_in] + early)
    return loss, grad_x, grads, small


def _gather_exchange(shards, mid_step=None):
    n = len(shards)

    def parts(ins, outs, sems):
        send_sems, recv_sems, local_sems = sems
        x, y, c = _position()
        chips = [(1 - x, y), (x, 1 - y), (1 - x, 1 - y)]

        def copy(w, k, block, to, src=None):
            dst = outs[w].at[4 * block[0] + 2 * block[1] + block[2]]
            return pltpu.make_async_remote_copy(
                src_ref=dst if src is None else src, dst_ref=dst,
                send_sem=send_sems.at[w, k], recv_sem=recv_sems.at[w, k], device_id=to, device_id_type=MESH)

        def local(w):
            return pltpu.make_async_copy(ins[w], outs[w].at[4 * x + 2 * y + c], local_sems.at[w])

        return (x, y, c), (x, y, 1 - c), chips, copy, local

    def start(ins, outs, sems):
        me, sibling, chips, copy, local = parts(ins, outs, sems)
        for w in range(n):
            local(w).start()
        for w in range(n):
            copy(w, 0, me, sibling, src=ins[w]).start()
            for j, chip in enumerate(chips):
                copy(w, 1 + j, me, (*chip, me[2]), src=ins[w]).start()

    def mid(ins, outs, sems):
        me, sibling, chips, copy, _ = parts(ins, outs, sems)
        for w in range(n):
            for j, chip in enumerate(chips):
                copy(w, 1 + j, (*chip, me[2]), me).wait_recv()
                copy(w, 4 + j, (*chip, me[2]), sibling).start()

    def end(ins, outs, sems):
        me, sibling, chips, copy, local = parts(ins, outs, sems)
        for w in range(n):
            copy(w, 0, sibling, me).wait_recv()
            for j, chip in enumerate(chips):
                copy(w, 4 + j, (*chip, sibling[2]), me).wait_recv()
        for w in range(n):
            for k in range(7):
                copy(w, k, me, sibling).wait_send()
            local(w).wait()

    return _Exchange(
        shards, [jax.ShapeDtypeStruct((N_DEV,) + s.shape, s.dtype) for s in shards],
        [pltpu.SemaphoreType.DMA((n, 7)), pltpu.SemaphoreType.DMA((n, 7)), pltpu.SemaphoreType.DMA((n,))],
        start, end, mid, mid_step)


def _run_exchange(comm, name):
    n_i, n_o = len(comm.arrays), len(comm.out_shape)

    def body(*refs):
        ins, outs, sems = refs[:n_i], refs[n_i:n_i + n_o], refs[n_i + n_o:]
        comm.start(ins, outs, sems)
        if comm.mid is not None:
            comm.mid(ins, outs, sems)
        comm.end(ins, outs, sems)

    return pl.pallas_call(
        body, name=name, in_specs=[ANY] * n_i, out_specs=[ANY] * n_o, out_shape=comm.out_shape,
        scratch_shapes=comm.scratch)(*comm.arrays)


def _exchange_sibling(grads, name):
    n = len(grads)

    def body(*refs):
        ins, outs = refs[:n], refs[n:2 * n]
        send_sems, recv_sems = refs[2 * n:]
        x, y, c = _position()
        copies = []
        for w in range(n):
            for s in range(N_CHIP):
                cp = pltpu.make_async_remote_copy(
                    src_ref=ins[w].at[2 * s + (1 - c)], dst_ref=outs[w].at[s],
                    send_sem=send_sems.at[w, s], recv_sem=recv_sems.at[w, s],
                    device_id=(x, y, 1 - c), device_id_type=MESH)
                cp.start()
                copies.append(cp)
        for cp in copies:
            cp.wait()

    return pl.pallas_call(
        body, name=name,
        in_specs=[ANY] * n, out_specs=[ANY] * n,
        out_shape=[jax.ShapeDtypeStruct((N_CHIP,) + g.shape[1:], g.dtype) for g in grads],
        scratch_shapes=[pltpu.SemaphoreType.DMA((n, N_CHIP)), pltpu.SemaphoreType.DMA((n, N_CHIP))],
    )(*grads)


def _pair_sum(g, land, parity, name):
    _, R, C = g.shape
    tr = _pick(R, (512, 256))

    def body(par_ref, g_ref, l_ref, o_ref):
        o_ref[...] = (g_ref[...].astype(F32) + l_ref[...].astype(F32)).astype(BF16)

    return pl.pallas_call(
        body, name=name,
        grid_spec=pltpu.PrefetchScalarGridSpec(
            num_scalar_prefetch=1, grid=(N_CHIP, R // tr),
            in_specs=[pl.BlockSpec((None, tr, C), lambda s, i, par: (2 * s + par[0], i, 0)),
                      pl.BlockSpec((None, tr, C), lambda s, i, par: (s, i, 0))],
            out_specs=pl.BlockSpec((None, tr, C), lambda s, i, par: (s, i, 0))),
        out_shape=jax.ShapeDtypeStruct((N_CHIP, R, C), BF16),
        compiler_params=_cparams(("parallel", "parallel")),
    )(parity, g, land)


def _scatter_exchange(partials):
    n = len(partials)

    def copies(ins, outs, sems):
        send_sems, recv_sems, local_sems = sems
        x, y, c = _position()
        chips = [(1 - x, y), (x, 1 - y), (1 - x, 1 - y)]
        my_slot = 2 * x + y
        local = [pltpu.make_async_copy(ins[w].at[my_slot], outs[w].at[my_slot], local_sems.at[w]) for w in range(n)]
        remote = [pltpu.make_async_remote_copy(
            src_ref=ins[w].at[2 * chip[0] + chip[1]], dst_ref=outs[w].at[my_slot],
            send_sem=send_sems.at[w, j], recv_sem=recv_sems.at[w, j], device_id=(*chip, c), device_id_type=MESH)
            for w in range(n) for j, chip in enumerate(chips)]
        return local, remote

    def start(ins, outs, sems):
        local, remote = copies(ins, outs, sems)
        for cp in local + remote:
            cp.start()

    def end(ins, outs, sems):
        local, remote = copies(ins, outs, sems)
        for cp in remote + local:
            cp.wait()

    return _Exchange(
        partials, [jax.ShapeDtypeStruct(p.shape, p.dtype) for p in partials],
        [pltpu.SemaphoreType.DMA((n, 3)), pltpu.SemaphoreType.DMA((n, 3)), pltpu.SemaphoreType.DMA((n,))],
        start, end)


class _Exchanges:
    def __init__(self, parity):
        self.parity = parity

    def gather(self, shards, mid_step):
        return _gather_exchange(list(shards), mid_step)

    def pair_sums(self, grads, tag):
        landed = _exchange_sibling(list(grads), "rs_sibling_" + tag)
        return [_pair_sum(g, l, self.parity, "rs_pair_sum_%s_%d" % (tag, i))
                for i, (g, l) in enumerate(zip(grads, landed))]

    def scatter(self, partials):
        return _scatter_exchange(partials)


def _gather_small(packed, name):
    R = packed.shape[0]

    def body(x_ref, out_ref, send_sems, recv_sems):
        x, y, c = _position()
        me = 4 * x + 2 * y + c
        out_ref[me] = x_ref[...]
        copies = []
        for k in range(1, N_DEV):
            to = (x ^ ((k >> 2) & 1), y ^ ((k >> 1) & 1), c ^ (k & 1))
            cp = pltpu.make_async_remote_copy(
                src_ref=x_ref, dst_ref=out_ref.at[me],
                send_sem=send_sems.at[k], recv_sem=recv_sems.at[k], device_id=to, device_id_type=MESH)
            cp.start()
            copies.append((k, to, cp))
        for k, to, cp in copies:
            cp.wait_send()
            pltpu.make_async_remote_copy(
                src_ref=x_ref, dst_ref=out_ref.at[4 * to[0] + 2 * to[1] + to[2]],
                send_sem=send_sems.at[k], recv_sem=recv_sems.at[k], device_id=to, device_id_type=MESH).wait_recv()

    return pl.pallas_call(
        body, name=name,
        in_specs=[pl.BlockSpec(memory_space=pltpu.VMEM)], out_specs=pl.BlockSpec(memory_space=pltpu.VMEM),
        out_shape=jax.ShapeDtypeStruct((N_DEV, R, 128), F32),
        scratch_shapes=[pltpu.SemaphoreType.DMA((N_DEV,)), pltpu.SemaphoreType.DMA((N_DEV,))],
    )(packed)


def _adamw_math(w, g, m, v):
    m = ADAM_B1 * m + (1.0 - ADAM_B1) * g
    v = ADAM_B2 * v + (1.0 - ADAM_B2) * (g * g)
    m_hat = m / (1.0 - ADAM_B1 ** ADAM_STEP)
    v_hat = v / (1.0 - ADAM_B2 ** ADAM_STEP)
    delta = -ADAM_LR * (m_hat / (jnp.sqrt(v_hat) + ADAM_EPS) + ADAM_WD * w)
    return delta, m, v


def _adamw_big(w, m, v, parts, name):
    R, C = w.shape
    tr = _pick(R, (256,))

    def body(w_ref, m_ref, v_ref, p_ref, g_ref, d_ref, nm_ref, nv_ref):
        g = p_ref[0].astype(F32)
        for s in range(1, N_CHIP):
            g = g + p_ref[s].astype(F32)
        d, nm, nv = _adamw_math(w_ref[...], g, m_ref[...], v_ref[...])
        g_ref[...] = g
        d_ref[...] = d
        nm_ref[...] = nm
        nv_ref[...] = nv

    blk = pl.BlockSpec((tr, C), lambda i: (i, 0))
    out = jax.ShapeDtypeStruct((R, C), F32)
    return pl.pallas_call(
        body, name=name, grid=(R // tr,),
        in_specs=[blk, blk, blk, pl.BlockSpec((N_CHIP, tr, C), lambda i: (0, i, 0))],
        out_specs=[blk, blk, blk, blk], out_shape=[out, out, out, out],
        compiler_params=_cparams(("parallel",)),
    )(w, m, v, parts)


def _adamw_small(w, m, v, gathered, name):
    R = w.shape[0]

    def body(w_ref, m_ref, v_ref, p_ref, g_ref, d_ref, nm_ref, nv_ref):
        g = p_ref[0]
        for s in range(1, N_DEV):
            g = g + p_ref[s]
        d, nm, nv = _adamw_math(w_ref[...], g, m_ref[...], v_ref[...])
        g_ref[...] = g
        d_ref[...] = d
        nm_ref[...] = nm
        nv_ref[...] = nv

    out = jax.ShapeDtypeStruct((R, 128), F32)
    return pl.pallas_call(
        body, name=name, out_shape=[out, out, out, out],
    )(w, m, v, gathered)


SMALL_NAMES = ("lb_logits", "hg_norm_w", "rel_bias", "norm_mix_w", "norm_mlp_w", "norm_final_w")
SMALL_SHAPES = {"lb_logits": (2, HG_WIDTH), "hg_norm_w": (1, HG_DK), "rel_bias": (AT_HEADS, N_REL_PAD),
                "norm_mix_w": (1, D_MODEL), "norm_mlp_w": (1, D_MODEL), "norm_final_w": (1, D_MODEL)}


def _pack_small(parts):
    rows = []
    for nme in SMALL_NAMES:
        p = parts[nme]
        if nme == "rel_bias":
            p = jnp.pad(p, ((0, 0), (0, N_REL_PAD - N_REL)))
        rows.append(p.reshape(-1, 128))
    flat = jnp.concatenate(rows, axis=0)
    return jnp.pad(flat, ((0, SMALL_ROWS - flat.shape[0]), (0, 0)))


def _unpack_small(packed):
    out, at = {}, 0
    for nme in SMALL_NAMES:
        shp = SMALL_SHAPES[nme]
        nrow = shp[0] * shp[1] // 128
        p = packed[at:at + nrow].reshape(shp)
        at += nrow
        out[nme] = p[:, :N_REL] if nme == "rel_bias" else p
    return out


BIG_NAMES = ("w_in", "w_branch_a", "w_branch_b", "w_out", "w_up", "w_down")


def kernel(x, w_in, lb_logits, hg_norm_w, rel_bias, w_branch_a, w_branch_b, w_out, norm_mix_w, norm_mlp_w, w_up, w_down, norm_final_w, loss_target, m_w_in, m_lb_logits, m_hg_norm_w, m_rel_bias, m_w_branch_a, m_w_branch_b, m_w_out, m_norm_mix_w, m_norm_mlp_w, m_w_up, m_w_down, m_norm_final_w, v_w_in, v_lb_logits, v_hg_norm_w, v_rel_bias, v_w_branch_a, v_w_branch_b, v_w_out, v_norm_mix_w, v_norm_mlp_w, v_w_up, v_w_down, v_norm_final_w):
    big_w = [w_in[0], w_branch_a[0], w_branch_b[0], w_out[0], w_up[0], w_down[0]]
    big_m = [m_w_in[0], m_w_branch_a[0], m_w_branch_b[0], m_w_out[0], m_w_up[0], m_w_down[0]]
    big_v = [v_w_in[0], v_w_branch_a[0], v_w_branch_b[0], v_w_out[0], v_w_up[0], v_w_down[0]]

    shards = [w.astype(BF16) for w in big_w]
    w_in_full, = _run_exchange(_gather_exchange(shards[:1]), "all_gather_w_in")
    parity = lax.axis_index("c").astype(jnp.int32).reshape(1)
    loss_part, grad_x, chip_parts, small = _local_step(
        x[0], loss_target[0], lb_logits, hg_norm_w, rel_bias[0], norm_mix_w, norm_mlp_w,
        norm_final_w.reshape(1, D_MODEL), w_in_full, shards[1:], _Exchanges(parity))
    loss = lax.psum(loss_part[0, 0], ("x", "y", "c"))
    big = [_adamw_big(w, m, v, p, "adamw_" + nme)
           for w, m, v, p, nme in zip(big_w, big_m, big_v, chip_parts, BIG_NAMES)]

    sw = dict(lb_logits=lb_logits, hg_norm_w=hg_norm_w, rel_bias=rel_bias[0], norm_mix_w=norm_mix_w,
              norm_mlp_w=norm_mlp_w, norm_final_w=norm_final_w.reshape(1, D_MODEL))
    sm = dict(lb_logits=m_lb_logits, hg_norm_w=m_hg_norm_w, rel_bias=m_rel_bias[0], norm_mix_w=m_norm_mix_w,
              norm_mlp_w=m_norm_mlp_w, norm_final_w=m_norm_final_w.reshape(1, D_MODEL))
    sv = dict(lb_logits=v_lb_logits, hg_norm_w=v_hg_norm_w, rel_bias=v_rel_bias[0], norm_mix_w=v_norm_mix_w,
              norm_mlp_w=v_norm_mlp_w, norm_final_w=v_norm_final_w.reshape(1, D_MODEL))
    gathered = _gather_small(_pack_small(small), "gather_small")
    small_out = [_unpack_small(p) for p in
                 _adamw_small(_pack_small(sw), _pack_small(sm), _pack_small(sv), gathered, "adamw_small")]

    def leaf(kind, nme):
        if nme in BIG_NAMES:
            return big[BIG_NAMES.index(nme)][kind][None]
        p = small_out[kind][nme]
        if nme == "rel_bias":
            return p[None]
        if nme == "norm_final_w":
            return p.reshape(D_MODEL)
        return p

    order = ("w_in", "lb_logits", "hg_norm_w", "rel_bias", "w_branch_a", "w_branch_b", "w_out", "norm_mix_w",
             "norm_mlp_w", "w_up", "w_down", "norm_final_w")
    outs = [loss, grad_x[None]]
    for kind in range(4):
        outs += [leaf(kind, nme) for nme in order]
    return tuple(outs)
```

```python
import functools

import jax
import jax.numpy as jnp
from jax import lax
from jax.experimental import pallas as pl
from jax.experimental.pallas import tpu as pltpu

F32 = jnp.float32
BF16 = jnp.bfloat16
HIGHEST = lax.Precision.HIGHEST
MESH = pl.DeviceIdType.MESH

D_MODEL = 2048
HG_HEADS = 8
HG_DK = 128
HG_WIDTH = 1024
AT_HEADS = 16
AT_DH = 64
AT_WIDTH = 1024
CHUNK = 64
LEFT_CHUNKS = 8
BAND = (LEFT_CHUNKS + 1) * CHUNK
PAD = LEFT_CHUNKS * CHUNK
REL_CLIP = 256
N_REL = 2 * REL_CLIP + 1
N_REL_PAD = 640
D_FF = 4 * D_MODEL
D_IN = 4 * HG_WIDTH + 3 * AT_WIDTH + 2 * D_MODEL
EPS = 1e-6
N_DEV = 8
N_CHIP = 4

ADAM_LR = 0.001
ADAM_B1 = 0.9
ADAM_B2 = 0.999
ADAM_EPS = 1e-08
ADAM_WD = 0.01
ADAM_STEP = 10

COL_HQ, COL_HF, COL_HI, COL_HG = 0, 8, 16, 24
COL_AQ, COL_AK, COL_AV = 32, 40, 48
COL_GATE_A, COL_GATE_B = 7, 9

VMEM_LIMIT = 56 * 1024 * 1024
SMALL_ROWS = 152


def _cparams(sem=None, **kw):
    if sem is not None:
        kw["dimension_semantics"] = sem
    return pltpu.CompilerParams(vmem_limit_bytes=VMEM_LIMIT, **kw)


def _pick(n, cands):
    for c in cands:
        if n % c == 0:
            return c
    return n


def _sigmoid(x):
    return 1.0 / (1.0 + jnp.exp(-x))


ANY = pl.BlockSpec(memory_space=pl.ANY)


def _position():
    return lax.axis_index("x"), lax.axis_index("y"), lax.axis_index("c")


class _Exchange:
    def __init__(self, arrays, out_shape, scratch, start, end, mid=None, mid_step=None):
        self.arrays, self.out_shape, self.scratch = list(arrays), list(out_shape), list(scratch)
        self.start, self.mid, self.end, self.mid_step = start, mid, end, mid_step


def _call(body, args, *, name, grid, in_specs, out_specs, out_shape, scratch_shapes=(), sem=None, comm=None):
    scratch_shapes = list(scratch_shapes)
    if comm is None:
        outs = pl.pallas_call(
            body, name=name, grid=grid, in_specs=in_specs, out_specs=out_specs, out_shape=out_shape,
            scratch_shapes=scratch_shapes, compiler_params=_cparams(sem))(*args)
        return list(outs), []

    n_in, n_out, n_sc = len(args), len(out_shape), len(scratch_shapes)
    n_ci, n_co = len(comm.arrays), len(comm.out_shape)
    n_steps = 1
    for g in grid:
        n_steps *= g

    def hosted(*refs):
        ins, refs = refs[:n_in], refs[n_in:]
        c_ins, refs = refs[:n_ci], refs[n_ci:]
        outs, refs = refs[:n_out], refs[n_out:]
        c_outs, refs = refs[:n_co], refs[n_co:]
        scratch, c_sems = refs[:n_sc], refs[n_sc:]
        step = 0
        for i, g in enumerate(grid):
            step = step * g + pl.program_id(i)

        @pl.when(step == 0)
        def _():
            comm.start(c_ins, c_outs, c_sems)

        body(*ins, *outs, *scratch)

        if comm.mid is not None:
            @pl.when(step == comm.mid_step)
            def _():
                comm.mid(c_ins, c_outs, c_sems)

        @pl.when(step == n_steps - 1)
        def _():
            comm.end(c_ins, c_outs, c_sems)

    outs = pl.pallas_call(
        hosted, name=name, grid=grid,
        in_specs=list(in_specs) + [ANY] * n_ci, out_specs=list(out_specs) + [ANY] * n_co,
        out_shape=list(out_shape) + comm.out_shape, scratch_shapes=scratch_shapes + comm.scratch,
        compiler_params=_cparams(("arbitrary",) * len(grid)))(*args, *comm.arrays)
    return list(outs[:n_out]), list(outs[n_out:])


def _mm_nn(a, wb, out_dtype, name, comm=None):
    M, K = a.shape
    NB, K2, Nb = wb.shape
    assert K == K2
    tm = min(M, 1024)
    tk = min(K, 2048)
    tn = _pick(Nb, (512, 1408, 256))
    nk = K // tk
    nn = Nb // tn

    def body(a_ref, b_ref, o_ref, *acc):
        part = jnp.dot(a_ref[...], b_ref[...], preferred_element_type=F32)
        if nk == 1:
            o_ref[...] = part.astype(out_dtype)
        else:
            acc_ref, = acc
            k = pl.program_id(3)

            @pl.when(k == 0)
            def _():
                acc_ref[...] = part

            @pl.when(k > 0)
            def _():
                acc_ref[...] += part

            @pl.when(k == nk - 1)
            def _():
                o_ref[...] = acc_ref[...].astype(out_dtype)

    grid = (M // tm, NB, nn, nk)
    if comm is not None and comm.mid is not None:
        comm.mid_step = (3 * grid[0] * grid[1] * grid[2] * grid[3]) // 4
    (out,), got = _call(
        body, (a, wb), name=name, grid=grid,
        in_specs=[pl.BlockSpec((tm, tk), lambda m, j, n, k: (m, k)),
                  pl.BlockSpec((None, tk, tn), lambda m, j, n, k: (j, k, n))],
        out_specs=[pl.BlockSpec((tm, tn), lambda m, j, n, k: (m, j * nn + n))],
        out_shape=[jax.ShapeDtypeStruct((M, NB * Nb), out_dtype)],
        scratch_shapes=[] if nk == 1 else [pltpu.VMEM((tm, tn), F32)],
        sem=("parallel", "parallel", "parallel", "arbitrary"), comm=comm)
    return (out, got) if comm is not None else out


def _mm_nt(a, wb, out_dtype, name, comm=None):
    M, N = a.shape
    NB, K, Nb = wb.shape
    assert N == NB * Nb
    tm = min(M, 1024)
    tko = _pick(K, (1024,))
    tc = _pick(Nb, (1024, 1408, 256))
    nc = Nb // tc
    nsteps = NB * nc

    def body(a_ref, b_ref, o_ref, acc_ref):
        step = pl.program_id(2) * nc + pl.program_id(3)
        part = lax.dot_general(a_ref[...], b_ref[...], (((1,), (1,)), ((), ())), preferred_element_type=F32)

        @pl.when(step == 0)
        def _():
            acc_ref[...] = part

        @pl.when(step > 0)
        def _():
            acc_ref[...] += part

        @pl.when(step == nsteps - 1)
        def _():
            o_ref[...] = acc_ref[...].astype(out_dtype)

    (out,), landed = _call(
        body, (a, wb), name=name,
        grid=(M // tm, K // tko, NB, nc),
        in_specs=[pl.BlockSpec((tm, tc), lambda m, ko, j, c: (m, j * nc + c)),
                  pl.BlockSpec((None, tko, tc), lambda m, ko, j, c: (j, ko, c))],
        out_specs=[pl.BlockSpec((tm, tko), lambda m, ko, j, c: (m, ko))],
        out_shape=[jax.ShapeDtypeStruct((M, K), out_dtype)],
        scratch_shapes=[pltpu.VMEM((tm, tko), F32)],
        sem=("parallel", "parallel", "arbitrary", "arbitrary"), comm=comm)
    return out, landed


def _mm_tn(a, g, nb, out_dtype, name):
    M, Ka = a.shape
    M2, N = g.shape
    assert M == M2 and N % nb == 0
    Nb = N // nb
    tka = _pick(Ka, (1024,))
    tn = _pick(Nb, (512, 1408, 256))
    nn = Nb // tn

    def body(a_ref, g_ref, o_ref):
        o_ref[...] = lax.dot_general(a_ref[...], g_ref[...], (((0,), (0,)), ((), ())),
                                     preferred_element_type=F32).astype(out_dtype)

    return pl.pallas_call(
        body, name=name,
        grid=(Ka // tka, nb, nn),
        in_specs=[pl.BlockSpec((M, tka), lambda ka, j, n: (0, ka)),
                  pl.BlockSpec((M, tn), lambda ka, j, n: (0, j * nn + n))],
        out_specs=pl.BlockSpec((None, tka, tn), lambda ka, j, n: (j, ka, n)),
        out_shape=jax.ShapeDtypeStruct((nb, Ka, Nb), out_dtype),
        compiler_params=_cparams(("parallel", "parallel", "parallel")),
    )(a, g)


ROW_TILE = 256


def _rms_fwd(x, w, name):
    T, Dm = x.shape

    def body(x_ref, w_ref, u_ref):
        xv = x_ref[...]
        r = lax.rsqrt(jnp.mean(xv * xv, axis=-1, keepdims=True) + EPS)
        u_ref[...] = (xv * r * w_ref[...]).astype(BF16)

    return pl.pallas_call(
        body, name=name, grid=(T // ROW_TILE,),
        in_specs=[pl.BlockSpec((ROW_TILE, Dm), lambda i: (i, 0)), pl.BlockSpec((1, Dm), lambda i: (0, 0))],
        out_specs=pl.BlockSpec((ROW_TILE, Dm), lambda i: (i, 0)),
        out_shape=jax.ShapeDtypeStruct((T, Dm), BF16),
        compiler_params=_cparams(("parallel",)),
    )(x, w)


def _resid_rms_fwd(x, mix, w, name):
    T, Dm = x.shape

    def body(x_ref, m_ref, w_ref, h_ref, u_ref):
        h = x_ref[...] + m_ref[...]
        h_ref[...] = h
        r = lax.rsqrt(jnp.mean(h * h, axis=-1, keepdims=True) + EPS)
        u_ref[...] = (h * r * w_ref[...]).astype(BF16)

    row = pl.BlockSpec((ROW_TILE, Dm), lambda i: (i, 0))
    return pl.pallas_call(
        body, name=name, grid=(T // ROW_TILE,),
        in_specs=[row, row, pl.BlockSpec((1, Dm), lambda i: (0, 0))],
        out_specs=[row, row],
        out_shape=[jax.ShapeDtypeStruct((T, Dm), F32), jax.ShapeDtypeStruct((T, Dm), BF16)],
        compiler_params=_cparams(("parallel",)),
    )(x, mix, w)


def _loss_head(h1, mlp, wf, target, name):
    T, Dm = h1.shape

    def body(h_ref, m_ref, w_ref, t_ref, loss_ref, dh_ref, dhb_ref, dw_ref):
        i = pl.program_id(0)
        h = h_ref[...] + m_ref[...]
        r = lax.rsqrt(jnp.mean(h * h, axis=-1, keepdims=True) + EPS)
        xh = h * r
        wv = w_ref[...]
        e = xh * wv - t_ref[...]
        part = 0.5 * jnp.sum(jnp.mean(e * e, axis=-1, keepdims=True), axis=0, keepdims=True)
        dy = e * (1.0 / Dm)
        dw = jnp.sum(dy * xh, axis=0, keepdims=True)
        gy = dy * wv
        dh = r * (gy - xh * jnp.mean(gy * xh, axis=-1, keepdims=True))
        dh_ref[...] = dh
        dhb_ref[...] = dh.astype(BF16)

        @pl.when(i == 0)
        def _():
            loss_ref[...] = jnp.zeros_like(loss_ref)
            dw_ref[...] = jnp.zeros_like(dw_ref)

        loss_ref[...] += jnp.broadcast_to(part, loss_ref.shape)
        dw_ref[...] += dw

    row = pl.BlockSpec((ROW_TILE, Dm), lambda i: (i, 0))
    vec = pl.BlockSpec((1, Dm), lambda i: (0, 0))
    return pl.pallas_call(
        body, name=name, grid=(T // ROW_TILE,),
        in_specs=[row, row, vec, row],
        out_specs=[pl.BlockSpec((8, 128), lambda i: (0, 0)), row, row, vec],
        out_shape=[jax.ShapeDtypeStruct((8, 128), F32), jax.ShapeDtypeStruct((T, Dm), F32),
                   jax.ShapeDtypeStruct((T, Dm), BF16), jax.ShapeDtypeStruct((1, Dm), F32)],
        compiler_params=_cparams(("arbitrary",)),
    )(h1, mlp, wf, target)


def _rms_bwd(dyn, x, w, dres, name):
    T, Dm = x.shape

    def body(g_ref, x_ref, w_ref, r_ref, dx_ref, dxb_ref, dw_ref):
        i = pl.program_id(0)
        xv = x_ref[...]
        r = lax.rsqrt(jnp.mean(xv * xv, axis=-1, keepdims=True) + EPS)
        xh = xv * r
        g = g_ref[...]
        dw = jnp.sum(g * xh, axis=0, keepdims=True)
        gy = g * w_ref[...]
        dx = r_ref[...] + r * (gy - xh * jnp.mean(gy * xh, axis=-1, keepdims=True))
        dx_ref[...] = dx
        dxb_ref[...] = dx.astype(BF16)

        @pl.when(i == 0)
        def _():
            dw_ref[...] = jnp.zeros_like(dw_ref)

        dw_ref[...] += dw

    row = pl.BlockSpec((ROW_TILE, Dm), lambda i: (i, 0))
    vec = pl.BlockSpec((1, Dm), lambda i: (0, 0))
    return pl.pallas_call(
        body, name=name, grid=(T // ROW_TILE,),
        in_specs=[row, row, vec, row],
        out_specs=[row, row, vec],
        out_shape=[jax.ShapeDtypeStruct((T, Dm), F32), jax.ShapeDtypeStruct((T, Dm), BF16),
                   jax.ShapeDtypeStruct((1, Dm), F32)],
        compiler_params=_cparams(("arbitrary",)),
    )(dyn, x, w, dres)


COL_TILE = 2048


def _relu2_fwd(a, name):
    T, N = a.shape

    def body(a_ref, r_ref):
        ra = jnp.maximum(a_ref[...], 0.0)
        r_ref[...] = (ra * ra).astype(BF16)

    blk = pl.BlockSpec((ROW_TILE, COL_TILE), lambda i, j: (i, j))
    return pl.pallas_call(
        body, name=name, grid=(T // ROW_TILE, N // COL_TILE), in_specs=[blk], out_specs=blk,
        out_shape=jax.ShapeDtypeStruct((T, N), BF16),
        compiler_params=_cparams(("parallel", "parallel")),
    )(a)


def _relu2_bwd(dr, a, name):
    T, N = a.shape

    def body(dr_ref, a_ref, da_ref):
        da_ref[...] = (dr_ref[...] * (2.0 * jnp.maximum(a_ref[...], 0.0))).astype(BF16)

    blk = pl.BlockSpec((ROW_TILE, COL_TILE), lambda i, j: (i, j))
    return pl.pallas_call(
        body, name=name, grid=(T // ROW_TILE, N // COL_TILE), in_specs=[blk, blk], out_specs=blk,
        out_shape=jax.ShapeDtypeStruct((T, N), BF16),
        compiler_params=_cparams(("parallel", "parallel")),
    )(dr, a)


GATE_TILE = 1024


def _merge_fwd(z, pa, pb, name):
    T, Dm = pa.shape

    def body(za_ref, zb_ref, pa_ref, pb_ref, m_ref):
        m_ref[...] = (_sigmoid(za_ref[...]) * pa_ref[...] + _sigmoid(zb_ref[...]) * pb_ref[...]).astype(BF16)

    blk = pl.BlockSpec((ROW_TILE, GATE_TILE), lambda i, j: (i, j))
    return pl.pallas_call(
        body, name=name, grid=(T // ROW_TILE, Dm // GATE_TILE),
        in_specs=[pl.BlockSpec((ROW_TILE, GATE_TILE), lambda i, j: (i, COL_GATE_A + j)),
                  pl.BlockSpec((ROW_TILE, GATE_TILE), lambda i, j: (i, COL_GATE_B + j)), blk, blk],
        out_specs=blk,
        out_shape=jax.ShapeDtypeStruct((T, Dm), BF16),
        compiler_params=_cparams(("parallel", "parallel")),
    )(z, z, pa, pb)


def _merge_bwd(dm, z, pa, pb, name):
    T, Dm = pa.shape

    def body(dm_ref, za_ref, zb_ref, pa_ref, pb_ref, dpa_ref, dpb_ref, dga_ref, dgb_ref):
        d = dm_ref[...]
        ga = _sigmoid(za_ref[...])
        gb = _sigmoid(zb_ref[...])
        dpa_ref[...] = (d * ga).astype(BF16)
        dpb_ref[...] = (d * gb).astype(BF16)
        dga_ref[...] = (d * pa_ref[...] * ga * (1.0 - ga)).astype(BF16)
        dgb_ref[...] = (d * pb_ref[...] * gb * (1.0 - gb)).astype(BF16)

    blk = pl.BlockSpec((ROW_TILE, GATE_TILE), lambda i, j: (i, j))
    out = jax.ShapeDtypeStruct((T, Dm), BF16)
    return pl.pallas_call(
        body, name=name, grid=(T // ROW_TILE, Dm // GATE_TILE),
        in_specs=[blk, pl.BlockSpec((ROW_TILE, GATE_TILE), lambda i, j: (i, COL_GATE_A + j)),
                  pl.BlockSpec((ROW_TILE, GATE_TILE), lambda i, j: (i, COL_GATE_B + j)), blk, blk],
        out_specs=[blk, blk, blk, blk],
        out_shape=[out, out, out, out],
        compiler_params=_cparams(("parallel", "parallel")),
    )(dm, z, z, pa, pb)


def _dot_hi(a, b, dims):
    return lax.dot_general(a, b, (dims, ((), ())), precision=HIGHEST, preferred_element_type=F32)


NN = ((1,), (0,))
NT = ((1,), (1,))
TN = ((0,), (0,))


def _hg_gates(hq, hf, lb):
    sq = _sigmoid(hq)
    q = hq * sq * (HG_DK ** -0.5)
    f = _sigmoid(hf)
    g = lb + (1.0 - lb) * f
    return q, sq, f, g, jnp.log(g), 1.0 - g


def _tri(lower):
    r = lax.broadcasted_iota(jnp.int32, (CHUNK, CHUNK), 0)
    c = lax.broadcasted_iota(jnp.int32, (CHUNK, CHUNK), 1)
    return jnp.where((r >= c) if lower else (r <= c), 1.0, 0.0).astype(F32)


def _hgrn2_fwd(z, lb_logits, hg_norm_w, name, comm=None):
    T = z.shape[0]
    n_chunks = T // CHUNK

    def body(hq_ref, hf_ref, hi_ref, hg_ref, lbl_ref, nw_ref, o_ref, ya_ref, sall_ref, st_ref):
        lbl = lbl_ref[...]
        lb = 1.0 / (1.0 + jnp.exp(lbl[1:2, :] - lbl[0:1, :]))
        st_ref[...] = jnp.zeros_like(st_ref)
        tri = _tri(True)
        row8 = lax.broadcasted_iota(jnp.int32, (8, HG_DK), 0)

        def chunk(c, carry):
            rows = pl.ds(pl.multiple_of(c * CHUNK, CHUNK), CHUNK)
            q, _, _, _, lg, kk = _hg_gates(hq_ref[rows, :], hf_ref[rows, :], lb)
            v = hi_ref[rows, :]
            b = _dot_hi(tri, lg, NN)
            st = st_ref[...]
            sall_ref[c] = st
            o_inter = _dot_hi(q * jnp.exp(b), st, NT)
            for g8 in range(CHUNK // 8):
                n = 8 * (g8 + 1)
                bs, ks, vs = b[:n], kk[:n], v[:n]
                sidx = lax.broadcasted_iota(jnp.int32, (n, HG_DK), 0)
                blk = o_inter[8 * g8:n]
                for i in range(8):
                    t = 8 * g8 + i
                    e = jnp.where(sidx <= t, jnp.exp(b[t:t + 1] - bs), 0.0)
                    p = jnp.sum(e * ks * q[t:t + 1], axis=1, keepdims=True)
                    ot = jnp.sum(p * vs, axis=0, keepdims=True)
                    blk = blk + jnp.where(row8 == i, ot, 0.0)
                o_ref[pl.ds(pl.multiple_of(c * CHUNK + 8 * g8, 8), 8), :] = blk
            bl = b[CHUNK - 1:CHUNK]
            ke = kk * jnp.exp(bl - b)
            st_ref[...] = st * jnp.exp(bl) + _dot_hi(v, ke, TN)
            return carry

        lax.fori_loop(0, n_chunks, chunk, 0)
        o = o_ref[...]
        r = lax.rsqrt(jnp.mean(o * o, axis=-1, keepdims=True) + EPS)
        hg = hg_ref[...]
        ya_ref[...] = (o * r * nw_ref[...] * (hg * _sigmoid(hg))).astype(BF16)

    def col(base):
        return pl.BlockSpec((T, HG_DK), lambda h: (0, base + h))

    return _call(
        body, (z, z, z, z, lb_logits, hg_norm_w), name=name, grid=(HG_HEADS,),
        in_specs=[col(COL_HQ), col(COL_HF), col(COL_HI), col(COL_HG),
                  pl.BlockSpec((2, HG_DK), lambda h: (0, h)), pl.BlockSpec((1, HG_DK), lambda h: (0, 0))],
        out_specs=[col(0), col(0), pl.BlockSpec((None, n_chunks, HG_DK, HG_DK), lambda h: (h, 0, 0, 0))],
        out_shape=[jax.ShapeDtypeStruct((T, HG_WIDTH), F32), jax.ShapeDtypeStruct((T, HG_WIDTH), BF16),
                   jax.ShapeDtypeStruct((HG_HEADS, n_chunks, HG_DK, HG_DK), F32)],
        scratch_shapes=[pltpu.VMEM((HG_DK, HG_DK), F32)],
        sem=("parallel",), comm=comm)


def _hgrn2_bwd(z, lb_logits, hg_norm_w, o_raw, s_all, dya, name, comm=None):
    T = z.shape[0]
    n_chunks = T // CHUNK

    def body(hq_ref, hf_ref, hi_ref, hg_ref, lbl_ref, nw_ref, o_ref, sall_ref, dya_ref,
             dhq_ref, dhf_ref, dhi_ref, dhg_ref, dlbl_ref, dnw_ref,
             do_ref, dst_ref, dq_ref, dk_ref, dv_ref, dlb_ref):
        h = pl.program_id(0)
        lbl = lbl_ref[...]
        lb = 1.0 / (1.0 + jnp.exp(lbl[1:2, :] - lbl[0:1, :]))

        o = o_ref[...]
        r = lax.rsqrt(jnp.mean(o * o, axis=-1, keepdims=True) + EPS)
        oh = o * r
        nw = nw_ref[...]
        hg = hg_ref[...]
        sg = _sigmoid(hg)
        dy = dya_ref[...]
        d_on = dy * (hg * sg)
        dhg_ref[...] = (dy * (oh * nw) * (sg * (1.0 + hg * (1.0 - sg)))).astype(BF16)
        dnw = jnp.sum(d_on * oh, axis=0, keepdims=True)
        gy = d_on * nw
        do_ref[...] = r * (gy - oh * jnp.mean(gy * oh, axis=-1, keepdims=True))

        @pl.when(h == 0)
        def _():
            dnw_ref[...] = jnp.zeros_like(dnw_ref)

        dnw_ref[...] += jnp.broadcast_to(dnw, dnw_ref.shape)

        dst_ref[...] = jnp.zeros_like(dst_ref)
        dlb_ref[...] = jnp.zeros_like(dlb_ref)
        tri = _tri(True)
        tri_t = _tri(False)
        row8 = lax.broadcasted_iota(jnp.int32, (8, HG_DK), 0)

        def chunk(ci, carry):
            c = n_chunks - 1 - ci
            rows = pl.ds(pl.multiple_of(c * CHUNK, CHUNK), CHUNK)
            hq = hq_ref[rows, :]
            q, sq, f, g, lg, kk = _hg_gates(hq, hf_ref[rows, :], lb)
            v = hi_ref[rows, :]
            do = do_ref[rows, :]
            b = _dot_hi(tri, lg, NN)
            eb = jnp.exp(b)
            bl = b[CHUNK - 1:CHUNK]
            ebl = jnp.exp(bl)
            ekb = jnp.exp(bl - b)
            qe = q * eb
            ke = kk * ekb
            st = sall_ref[c]
            dst = dst_ref[...]
            dqe = _dot_hi(do, st, NN)
            dke = _dot_hi(v, dst, NN)
            dv_inter = _dot_hi(ke, dst, NT)
            d_ebl = jnp.sum(st * dst, axis=0, keepdims=True)
            dst_ref[...] = dst * ebl + _dot_hi(do, qe, TN)

            dk_ref[...] = jnp.zeros_like(dk_ref)
            dv_ref[...] = jnp.zeros_like(dv_ref)
            for g8 in range(CHUNK // 8):
                n = 8 * (g8 + 1)
                bs, ks, vs = b[:n], kk[:n], v[:n]
                sidx = lax.broadcasted_iota(jnp.int32, (n, HG_DK), 0)
                blk = jnp.zeros((8, HG_DK), F32)
                for i in range(8):
                    t = 8 * g8 + i
                    qt = q[t:t + 1]
                    dot_ = do[t:t + 1]
                    e = jnp.where(sidx <= t, jnp.exp(b[t:t + 1] - bs), 0.0)
                    w = e * ks
                    p = jnp.sum(w * qt, axis=1, keepdims=True)
                    dsc = jnp.sum(vs * dot_, axis=1, keepdims=True)
                    dqt = jnp.sum(dsc * w, axis=0, keepdims=True)
                    blk = blk + jnp.where(row8 == i, dqt, 0.0)
                    dk_ref[0:n, :] += dsc * e * qt
                    dv_ref[0:n, :] += p * dot_
                dq_ref[8 * g8:n, :] = blk
            dq_i = dq_ref[...]
            dk_i = dk_ref[...]
            dke_ke = dke * ke
            db = q * dq_i - kk * dk_i + dqe * qe - dke_ke
            db_last = jnp.sum(dke_ke, axis=0, keepdims=True) + d_ebl * ebl
            dlg = _dot_hi(tri_t, db, NN) + db_last
            dq = dq_i + dqe * eb
            dkk = dk_i + dke * ekb
            dg = dlg / g - dkk
            dhq_ref[rows, :] = (dq * (HG_DK ** -0.5) * (sq * (1.0 + hq * (1.0 - sq)))).astype(BF16)
            dhf_ref[rows, :] = (dg * (1.0 - lb) * f * (1.0 - f)).astype(BF16)
            dhi_ref[rows, :] = (dv_ref[...] + dv_inter).astype(BF16)
            dlb_ref[...] += jnp.sum(dg * (1.0 - f), axis=0, keepdims=True)
            return carry

        lax.fori_loop(0, n_chunks, chunk, 0)
        dl0 = dlb_ref[...] * lb * (1.0 - lb)
        dlbl_ref[0:1, :] = dl0
        dlbl_ref[1:2, :] = -dl0

    def col(base):
        return pl.BlockSpec((T, HG_DK), lambda h: (0, base + h))

    outb = jax.ShapeDtypeStruct((T, HG_WIDTH), BF16)
    return _call(
        body, (z, z, z, z, lb_logits, hg_norm_w, o_raw, s_all, dya), name=name, grid=(HG_HEADS,),
        in_specs=[col(COL_HQ), col(COL_HF), col(COL_HI), col(COL_HG),
                  pl.BlockSpec((2, HG_DK), lambda h: (0, h)), pl.BlockSpec((1, HG_DK), lambda h: (0, 0)),
                  col(0), pl.BlockSpec((None, n_chunks, HG_DK, HG_DK), lambda h: (h, 0, 0, 0)), col(0)],
        out_specs=[col(0), col(0), col(0), col(0), pl.BlockSpec((2, HG_DK), lambda h: (0, h)),
                   pl.BlockSpec((8, HG_DK), lambda h: (0, 0))],
        out_shape=[outb, outb, outb, outb, jax.ShapeDtypeStruct((2, HG_WIDTH), F32),
                   jax.ShapeDtypeStruct((8, HG_DK), F32)],
        scratch_shapes=[pltpu.VMEM((T, HG_DK), F32), pltpu.VMEM((HG_DK, HG_DK), F32),
                        pltpu.VMEM((CHUNK, HG_DK), F32), pltpu.VMEM((CHUNK, HG_DK), F32),
                        pltpu.VMEM((CHUNK, HG_DK), F32), pltpu.VMEM((1, HG_DK), F32)],
        sem=("arbitrary",), comm=comm)


def _rel_onehot(t):
    r = lax.broadcasted_iota(jnp.int32, (N_REL_PAD, BAND), 0)
    j = lax.broadcasted_iota(jnp.int32, (N_REL_PAD, BAND), 1)
    idx = jnp.clip(t + PAD - j, -REL_CLIP, REL_CLIP) + REL_CLIP
    return jnp.where(r == idx, 1.0, 0.0).astype(F32)


def _bias_expand(rel, name):
    def body(rel_ref, out_ref):
        out_ref[...] = _dot_hi(rel_ref[...], _rel_onehot(pl.program_id(0)), NN)

    return pl.pallas_call(
        body, name=name, grid=(CHUNK,),
        in_specs=[pl.BlockSpec((AT_HEADS, N_REL_PAD), lambda t: (0, 0))],
        out_specs=pl.BlockSpec((None, AT_HEADS, BAND), lambda t: (t, 0, 0)),
        out_shape=jax.ShapeDtypeStruct((CHUNK, AT_HEADS, BAND), F32),
        compiler_params=_cparams(("parallel",)),
    )(rel)


def _bias_reduce(dbias_t, name):
    def body(db_ref, out_ref):
        t = pl.program_id(0)

        @pl.when(t == 0)
        def _():
            out_ref[...] = jnp.zeros_like(out_ref)

        out_ref[...] += _dot_hi(db_ref[...], _rel_onehot(t), NT)

    return pl.pallas_call(
        body, name=name, grid=(CHUNK,),
        in_specs=[pl.BlockSpec((None, AT_HEADS, BAND), lambda t: (t, 0, 0))],
        out_specs=pl.BlockSpec((AT_HEADS, N_REL_PAD), lambda t: (0, 0)),
        out_shape=jax.ShapeDtypeStruct((AT_HEADS, N_REL_PAD), F32),
        compiler_params=_cparams(("arbitrary",)),
    )(dbias_t)


def _pair_lanes():
    return lax.broadcasted_iota(jnp.int32, (CHUNK, 2 * AT_DH), 1) < AT_DH


def _block_diag(a):
    first = _pair_lanes()
    return jnp.concatenate([jnp.where(first, a, 0.0), jnp.where(first, 0.0, a)], axis=0).astype(BF16)


def _diag_blocks(a):
    return jnp.where(_pair_lanes(), a[:CHUNK], a[CHUNK:])


def _band_probs_t(kb, qbd, bias_t, c):
    s = lax.dot_general(kb, qbd, (NT, ((), ())), preferred_element_type=F32) * (AT_DH ** -0.5) + bias_t
    j = lax.broadcasted_iota(jnp.int32, (BAND, 2 * AT_DH), 0)
    s = jnp.where(j + c * CHUNK >= PAD, s, -jnp.inf)
    p = jnp.exp(s - jnp.max(s, axis=0, keepdims=True))
    return p / jnp.sum(p, axis=0, keepdims=True)


def _fill_padded(dst_ref, src_ref, T):
    dst_ref[0:PAD, :] = jnp.zeros((PAD, 2 * AT_DH), BF16)
    dst_ref[PAD:PAD + T, :] = src_ref[...].astype(BF16)


def _attn_fwd(z, bias_t, name, comm=None):
    T = z.shape[0]
    n_chunks = T // CHUNK

    def body(q_ref, k_ref, v_ref, bias_ref, y_ref, kp_ref, vp_ref):
        _fill_padded(kp_ref, k_ref, T)
        _fill_padded(vp_ref, v_ref, T)

        def chunk(c, carry):
            rows = pl.ds(pl.multiple_of(c * CHUNK, CHUNK), CHUNK)
            band = pl.ds(pl.multiple_of(c * CHUNK, CHUNK), BAND)
            p = _band_probs_t(kp_ref[band, :], _block_diag(q_ref[rows, :]), bias_ref[...], c)
            o2 = lax.dot_general(p.astype(BF16), vp_ref[band, :], (TN, ((), ())), preferred_element_type=F32)
            y_ref[rows, :] = _diag_blocks(o2).astype(BF16)
            return carry

        lax.fori_loop(0, n_chunks, chunk, 0, unroll=2)

    def col(base):
        return pl.BlockSpec((T, 128), lambda h: (0, base + h))

    return _call(
        body, (z, z, z, bias_t), name=name, grid=(AT_HEADS // 2,),
        in_specs=[col(COL_AQ), col(COL_AK), col(COL_AV), pl.BlockSpec((None, BAND, 128), lambda h: (h, 0, 0))],
        out_specs=[col(0)],
        out_shape=[jax.ShapeDtypeStruct((T, AT_WIDTH), BF16)],
        scratch_shapes=[pltpu.VMEM((PAD + T, 128), BF16), pltpu.VMEM((PAD + T, 128), BF16)],
        sem=("parallel",), comm=comm)


def _attn_bwd(z, bias_t, dyb, name, comm=None):
    T = z.shape[0]
    n_chunks = T // CHUNK

    def body(q_ref, k_ref, v_ref, bias_ref, dy_ref, dq_ref, dk_ref, dv_ref, dbias_ref,
             kp_ref, vp_ref, dkp_ref, dvp_ref):
        _fill_padded(kp_ref, k_ref, T)
        _fill_padded(vp_ref, v_ref, T)
        dkp_ref[...] = jnp.zeros_like(dkp_ref)
        dvp_ref[...] = jnp.zeros_like(dvp_ref)
        dbias_ref[...] = jnp.zeros_like(dbias_ref)

        def chunk(c, carry):
            rows = pl.ds(pl.multiple_of(c * CHUNK, CHUNK), CHUNK)
            band = pl.ds(pl.multiple_of(c * CHUNK, CHUNK), BAND)
            qbd = _block_diag(q_ref[rows, :])
            dobd = _block_diag(dy_ref[rows, :])
            kb = kp_ref[band, :]
            vb = vp_ref[band, :]
            p = _band_probs_t(kb, qbd, bias_ref[...], c)
            dp = lax.dot_general(vb, dobd, (NT, ((), ())), preferred_element_type=F32)
            ds = p * (dp - jnp.sum(dp * p, axis=0, keepdims=True))
            dbias_ref[...] += ds
            dsb = ds.astype(BF16)
            dq2 = lax.dot_general(dsb, kb, (TN, ((), ())), preferred_element_type=F32)
            dq_ref[rows, :] = (_diag_blocks(dq2) * (AT_DH ** -0.5)).astype(BF16)
            dkp_ref[band, :] += jnp.dot(dsb, qbd, preferred_element_type=F32) * (AT_DH ** -0.5)
            dvp_ref[band, :] += jnp.dot(p.astype(BF16), dobd, preferred_element_type=F32)
            return carry

        lax.fori_loop(0, n_chunks, chunk, 0)
        dk_ref[...] = dkp_ref[PAD:PAD + T, :].astype(BF16)
        dv_ref[...] = dvp_ref[PAD:PAD + T, :].astype(BF16)

    def col(base):
        return pl.BlockSpec((T, 128), lambda h: (0, base + h))

    pair = pl.BlockSpec((None, BAND, 128), lambda h: (h, 0, 0))
    outb = jax.ShapeDtypeStruct((T, AT_WIDTH), BF16)
    return _call(
        body, (z, z, z, bias_t, dyb), name=name, grid=(AT_HEADS // 2,),
        in_specs=[col(COL_AQ), col(COL_AK), col(COL_AV), pair, col(0)],
        out_specs=[col(0), col(0), col(0), pair],
        out_shape=[outb, outb, outb, jax.ShapeDtypeStruct((AT_HEADS // 2, BAND, 128), F32)],
        scratch_shapes=[pltpu.VMEM((PAD + T, 128), BF16), pltpu.VMEM((PAD + T, 128), BF16),
                        pltpu.VMEM((PAD + T, 128), F32), pltpu.VMEM((PAD + T, 128), F32)],
        sem=("parallel",), comm=comm)


def _local_step(x, target, lb_logits, hg_norm_w, rel_bias, norm_mix_w, norm_mlp_w, norm_final_w,
                w_in, rest, exchanges=None):
    ex = exchanges
    rel = jnp.pad(rel_bias, ((0, 0), (0, N_REL_PAD - N_REL)))

    u = _rms_fwd(x, norm_mix_w, "rms_mix_fwd")
    if ex:
        z, got_abo = _mm_nn(u, w_in, F32, "mm_in_fwd", comm=ex.gather(rest[:3], mid_step=0))
    else:
        z, got_abo = _mm_nn(u, w_in, F32, "mm_in_fwd"), []
    (o_raw, y_a, s_all), got_up = _hgrn2_fwd(z, lb_logits, hg_norm_w, "hgrn2_fwd",
                                             comm=ex and ex.gather(rest[3:4], mid_step=HG_HEADS - 2))
    bias_rows = _bias_expand(rel, "bias_expand")
    bias_t = jnp.transpose(bias_rows.reshape(CHUNK, AT_HEADS // 2, 2, BAND), (1, 3, 2, 0)).reshape(
        AT_HEADS // 2, BAND, 2 * CHUNK)
    (y_b,), got_down = _attn_fwd(z, bias_t, "attn_fwd",
                                 comm=ex and ex.gather(rest[4:], mid_step=AT_HEADS // 2 - 2))
    w_a, w_b, w_out, w_up, w_down = (got_abo + got_up + got_down) if ex else rest
    w_out1 = w_out.reshape(1, D_MODEL, D_MODEL)
    w_down1 = w_down.reshape(1, D_FF, D_MODEL)
    pa = _mm_nn(y_a, w_a, F32, "mm_a_fwd")
    pb = _mm_nn(y_b, w_b, F32, "mm_b_fwd")
    merged = _merge_fwd(z, pa, pb, "merge_fwd")
    mix = _mm_nn(merged, w_out1, F32, "mm_out_fwd")
    h1, u2 = _resid_rms_fwd(x, mix, norm_mlp_w, "rms_mlp_fwd")
    a = _mm_nn(u2, w_up, F32, "mm_up_fwd")
    r = _relu2_fwd(a, "relu2_fwd")
    mlp = _mm_nn(r, w_down1, F32, "mm_down_fwd")
    loss, dh2, dh2b, g_nf = _loss_head(h1, mlp, norm_final_w, target, "loss_head")

    g_down = _mm_tn(r, dh2b, 1, BF16, "mm_down_wgrad").reshape(N_DEV, D_FF // N_DEV, D_MODEL)
    dr, _ = _mm_nt(dh2b, w_down1, F32, "mm_down_dgrad")
    da = _relu2_bwd(dr, a, "relu2_bwd")
    g_up = _mm_tn(u2, da, N_DEV, BF16, "mm_up_wgrad")
    du2, _ = _mm_nt(da, w_up, F32, "mm_up_dgrad")
    dh1, dh1b, g_nmlp = _rms_bwd(du2, h1, norm_mlp_w, dh2, "rms_mlp_bwd")

    g_out = _mm_tn(merged, dh1b, 1, BF16, "mm_out_wgrad").reshape(N_DEV, D_MODEL // N_DEV, D_MODEL)
    dmerged, _ = _mm_nt(dh1b, w_out1, F32, "mm_out_dgrad")
    dpa, dpb, dga, dgb = _merge_bwd(dmerged, z, pa, pb, "merge_bwd")
    g_a = _mm_tn(y_a, dpa, N_DEV, BF16, "mm_a_wgrad")
    g_b = _mm_tn(y_b, dpb, N_DEV, BF16, "mm_b_wgrad")
    dya, _ = _mm_nt(dpa, w_a, F32, "mm_a_dgrad")
    dyb, _ = _mm_nt(dpb, w_b, F32, "mm_b_dgrad")
    early = [g_a, g_b, g_out, g_up, g_down]
    early_sums = ex.pair_sums(early, "early") if ex else None
    (daq, dak, dav, dbias_t), parts_down = _attn_bwd(
        z, bias_t, dyb, "attn_bwd", comm=ex and ex.scatter(early_sums[4:]))
    (dhq, dhf, dhi, dhg, g_lbl, g_hgw), parts_abou = _hgrn2_bwd(
        z, lb_logits, hg_norm_w, o_raw, s_all, dya, "hgrn2_bwd", comm=ex and ex.scatter(early_sums[:4]))
    early_parts = parts_abou + parts_down
    dbias_rows = jnp.transpose(dbias_t.reshape(AT_HEADS // 2, BAND, 2, CHUNK), (3, 0, 2, 1)).reshape(
        CHUNK, AT_HEADS, BAND)
    g_rel = _bias_reduce(dbias_rows, "bias_reduce")[:, :N_REL]
    dz = jnp.concatenate([dhq, dhf, dhi, dhg, daq, dak, dav, dga, dgb], axis=1)
    g_in = _mm_tn(u, dz, N_DEV, BF16, "mm_in_wgrad")
    du, late_parts = _mm_nt(dz, w_in, F32, "mm_in_dgrad", comm=ex and ex.scatter(ex.pair_sums([g_in], "late")))
    grad_x, _, g_nmix = _rms_bwd(du, x, norm_mix_w, dh1, "rms_mix_bwd")

    small = dict(lb_logits=g_lbl, hg_norm_w=g_hgw[0:1], rel_bias=g_rel, norm_mix_w=g_nmix, norm_mlp_w=g_nmlp,
                 norm_final_w=g_nf)
    grads = (late_parts + early_parts) if ex else ([g_in] + early)
    return loss, grad_x, grads, small


def _gather_exchange(shards, mid_step=None):
    n = len(shards)

    def parts(ins, outs, sems):
        send_sems, recv_sems, local_sems = sems
        x, y, c = _position()
        chips = [(1 - x, y), (x, 1 - y), (1 - x, 1 - y)]

        def copy(w, k, block, to, src=None):
            dst = outs[w].at[4 * block[0] + 2 * block[1] + block[2]]
            return pltpu.make_async_remote_copy(
                src_ref=dst if src is None else src, dst_ref=dst,
                send_sem=send_sems.at[w, k], recv_sem=recv_sems.at[w, k], device_id=to, device_id_type=MESH)

        def local(w):
            return pltpu.make_async_copy(ins[w], outs[w].at[4 * x + 2 * y + c], local_sems.at[w])

        return (x, y, c), (x, y, 1 - c), chips, copy, local

    def start(ins, outs, sems):
        me, sibling, chips, copy, local = parts(ins, outs, sems)
        for w in range(n):
            local(w).start()
        for w in range(n):
            copy(w, 0, me, sibling, src=ins[w]).start()
            for j, chip in enumerate(chips):
                copy(w, 1 + j, me, (*chip, me[2]), src=ins[w]).start()

    def mid(ins, outs, sems):
        me, sibling, chips, copy, _ = parts(ins, outs, sems)
        for w in range(n):
            for j, chip in enumerate(chips):
                copy(w, 1 + j, (*chip, me[2]), me).wait_recv()
                copy(w, 4 + j, (*chip, me[2]), sibling).start()

    def end(ins, outs, sems):
        me, sibling, chips, copy, local = parts(ins, outs, sems)
        for w in range(n):
            copy(w, 0, sibling, me).wait_recv()
            for j, chip in enumerate(chips):
                copy(w, 4 + j, (*chip, sibling[2]), me).wait_recv()
        for w in range(n):
            for k in range(7):
                copy(w, k, me, sibling).wait_send()
            local(w).wait()

    return _Exchange(
        shards, [jax.ShapeDtypeStruct((N_DEV,) + s.shape, s.dtype) for s in shards],
        [pltpu.SemaphoreType.DMA((n, 7)), pltpu.SemaphoreType.DMA((n, 7)), pltpu.SemaphoreType.DMA((n,))],
        start, end, mid, mid_step)


def _run_exchange(comm, name):
    n_i, n_o = len(comm.arrays), len(comm.out_shape)

    def body(*refs):
        ins, outs, sems = refs[:n_i], refs[n_i:n_i + n_o], refs[n_i + n_o:]
        comm.start(ins, outs, sems)
        if comm.mid is not None:
            comm.mid(ins, outs, sems)
        comm.end(ins, outs, sems)

    return pl.pallas_call(
        body, name=name, in_specs=[ANY] * n_i, out_specs=[ANY] * n_o, out_shape=comm.out_shape,
        scratch_shapes=comm.scratch)(*comm.arrays)


def _exchange_sibling(grads, name):
    n = len(grads)

    def body(*refs):
        ins, outs = refs[:n], refs[n:2 * n]
        send_sems, recv_sems = refs[2 * n:]
        x, y, c = _position()
        copies = []
        for w in range(n):
            for s in range(N_CHIP):
                cp = pltpu.make_async_remote_copy(
                    src_ref=ins[w].at[2 * s + (1 - c)], dst_ref=outs[w].at[s],
                    send_sem=send_sems.at[w, s], recv_sem=recv_sems.at[w, s],
                    device_id=(x, y, 1 - c), device_id_type=MESH)
                cp.start()
                copies.append(cp)
        for cp in copies:
            cp.wait()

    return pl.pallas_call(
        body, name=name,
        in_specs=[ANY] * n, out_specs=[ANY] * n,
        out_shape=[jax.ShapeDtypeStruct((N_CHIP,) + g.shape[1:], g.dtype) for g in grads],
        scratch_shapes=[pltpu.SemaphoreType.DMA((n, N_CHIP)), pltpu.SemaphoreType.DMA((n, N_CHIP))],
    )(*grads)


def _pair_sum(g, land, parity, name):
    _, R, C = g.shape
    tr = _pick(R, (512, 256))

    def body(par_ref, g_ref, l_ref, o_ref):
        o_ref[...] = (g_ref[...].astype(F32) + l_ref[...].astype(F32)).astype(BF16)

    return pl.pallas_call(
        body, name=name,
        grid_spec=pltpu.PrefetchScalarGridSpec(
            num_scalar_prefetch=1, grid=(N_CHIP, R // tr),
            in_specs=[pl.BlockSpec((None, tr, C), lambda s, i, par: (2 * s + par[0], i, 0)),
                      pl.BlockSpec((None, tr, C), lambda s, i, par: (s, i, 0))],
            out_specs=pl.BlockSpec((None, tr, C), lambda s, i, par: (s, i, 0))),
        out_shape=jax.ShapeDtypeStruct((N_CHIP, R, C), BF16),
        compiler_params=_cparams(("parallel", "parallel")),
    )(parity, g, land)


def _scatter_exchange(partials):
    n = len(partials)

    def copies(ins, outs, sems):
        send_sems, recv_sems, local_sems = sems
        x, y, c = _position()
        chips = [(1 - x, y), (x, 1 - y), (1 - x, 1 - y)]
        my_slot = 2 * x + y
        local = [pltpu.make_async_copy(ins[w].at[my_slot], outs[w].at[my_slot], local_sems.at[w]) for w in range(n)]
        remote = [pltpu.make_async_remote_copy(
            src_ref=ins[w].at[2 * chip[0] + chip[1]], dst_ref=outs[w].at[my_slot],
            send_sem=send_sems.at[w, j], recv_sem=recv_sems.at[w, j], device_id=(*chip, c), device_id_type=MESH)
            for w in range(n) for j, chip in enumerate(chips)]
        return local, remote

    def start(ins, outs, sems):
        local, remote = copies(ins, outs, sems)
        for cp in local + remote:
            cp.start()

    def end(ins, outs, sems):
        local, remote = copies(ins, outs, sems)
        for cp in remote + local:
            cp.wait()

    return _Exchange(
        partials, [jax.ShapeDtypeStruct(p.shape, p.dtype) for p in partials],
        [pltpu.SemaphoreType.DMA((n, 3)), pltpu.SemaphoreType.DMA((n, 3)), pltpu.SemaphoreType.DMA((n,))],
        start, end)


class _Exchanges:
    def __init__(self, parity):
        self.parity = parity

    def gather(self, shards, mid_step):
        return _gather_exchange(list(shards), mid_step)

    def pair_sums(self, grads, tag):
        landed = _exchange_sibling(list(grads), "rs_sibling_" + tag)
        return [_pair_sum(g, l, self.parity, "rs_pair_sum_%s_%d" % (tag, i))
                for i, (g, l) in enumerate(zip(grads, landed))]

    def scatter(self, partials):
        return _scatter_exchange(partials)


def _gather_small(packed, name):
    R = packed.shape[0]

    def body(x_ref, out_ref, send_sems, recv_sems):
        x, y, c = _position()
        me = 4 * x + 2 * y + c
        out_ref[me] = x_ref[...]
        copies = []
        for k in range(1, N_DEV):
            to = (x ^ ((k >> 2) & 1), y ^ ((k >> 1) & 1), c ^ (k & 1))
            cp = pltpu.make_async_remote_copy(
                src_ref=x_ref, dst_ref=out_ref.at[me],
                send_sem=send_sems.at[k], recv_sem=recv_sems.at[k], device_id=to, device_id_type=MESH)
            cp.start()
            copies.append((k, to, cp))
        for k, to, cp in copies:
            cp.wait_send()
            pltpu.make_async_remote_copy(
                src_ref=x_ref, dst_ref=out_ref.at[4 * to[0] + 2 * to[1] + to[2]],
                send_sem=send_sems.at[k], recv_sem=recv_sems.at[k], device_id=to, device_id_type=MESH).wait_recv()

    return pl.pallas_call(
        body, name=name,
        in_specs=[pl.BlockSpec(memory_space=pltpu.VMEM)], out_specs=pl.BlockSpec(memory_space=pltpu.VMEM),
        out_shape=jax.ShapeDtypeStruct((N_DEV, R, 128), F32),
        scratch_shapes=[pltpu.SemaphoreType.DMA((N_DEV,)), pltpu.SemaphoreType.DMA((N_DEV,))],
    )(packed)


def _adamw_math(w, g, m, v):
    m = ADAM_B1 * m + (1.0 - ADAM_B1) * g
    v = ADAM_B2 * v + (1.0 - ADAM_B2) * (g * g)
    m_hat = m / (1.0 - ADAM_B1 ** ADAM_STEP)
    v_hat = v / (1.0 - ADAM_B2 ** ADAM_STEP)
    delta = -ADAM_LR * (m_hat / (jnp.sqrt(v_hat) + ADAM_EPS) + ADAM_WD * w)
    return delta, m, v


def _adamw_big(w, m, v, parts, name):
    R, C = w.shape
    tr = _pick(R, (256,))

    def body(w_ref, m_ref, v_ref, p_ref, g_ref, d_ref, nm_ref, nv_ref):
        g = p_ref[0].astype(F32)
        for s in range(1, N_CHIP):
            g = g + p_ref[s].astype(F32)
        d, nm, nv = _adamw_math(w_ref[...], g, m_ref[...], v_ref[...])
        g_ref[...] = g
        d_ref[...] = d
        nm_ref[...] = nm
        nv_ref[...] = nv

    blk = pl.BlockSpec((tr, C), lambda i: (i, 0))
    out = jax.ShapeDtypeStruct((R, C), F32)
    return pl.pallas_call(
        body, name=name, grid=(R // tr,),
        in_specs=[blk, blk, blk, pl.BlockSpec((N_CHIP, tr, C), lambda i: (0, i, 0))],
        out_specs=[blk, blk, blk, blk], out_shape=[out, out, out, out],
        compiler_params=_cparams(("parallel",)),
    )(w, m, v, parts)


def _adamw_small(w, m, v, gathered, name):
    R = w.shape[0]

    def body(w_ref, m_ref, v_ref, p_ref, g_ref, d_ref, nm_ref, nv_ref):
        g = p_ref[0]
        for s in range(1, N_DEV):
            g = g + p_ref[s]
        d, nm, nv = _adamw_math(w_ref[...], g, m_ref[...], v_ref[...])
        g_ref[...] = g
        d_ref[...] = d
        nm_ref[...] = nm
        nv_ref[...] = nv

    out = jax.ShapeDtypeStruct((R, 128), F32)
    return pl.pallas_call(
        body, name=name, out_shape=[out, out, out, out],
    )(w, m, v, gathered)


SMALL_NAMES = ("lb_logits", "hg_norm_w", "rel_bias", "norm_mix_w", "norm_mlp_w", "norm_final_w")
SMALL_SHAPES = {"lb_logits": (2, HG_WIDTH), "hg_norm_w": (1, HG_DK), "rel_bias": (AT_HEADS, N_REL_PAD),
                "norm_mix_w": (1, D_MODEL), "norm_mlp_w": (1, D_MODEL), "norm_final_w": (1, D_MODEL)}


def _pack_small(parts):
    rows = []
    for nme in SMALL_NAMES:
        p = parts[nme]
        if nme == "rel_bias":
            p = jnp.pad(p, ((0, 0), (0, N_REL_PAD - N_REL)))
        rows.append(p.reshape(-1, 128))
    flat = jnp.concatenate(rows, axis=0)
    return jnp.pad(flat, ((0, SMALL_ROWS - flat.shape[0]), (0, 0)))


def _unpack_small(packed):
    out, at = {}, 0
    for nme in SMALL_NAMES:
        shp = SMALL_SHAPES[nme]
        nrow = shp[0] * shp[1] // 128
        p = packed[at:at + nrow].reshape(shp)
        at += nrow
        out[nme] = p[:, :N_REL] if nme == "rel_bias" else p
    return out


BIG_NAMES = ("w_in", "w_branch_a", "w_branch_b", "w_out", "w_up", "w_down")


def kernel(x, w_in, lb_logits, hg_norm_w, rel_bias, w_branch_a, w_branch_b, w_out, norm_mix_w, norm_mlp_w, w_up, w_down, norm_final_w, loss_target, m_w_in, m_lb_logits, m_hg_norm_w, m_rel_bias, m_w_branch_a, m_w_branch_b, m_w_out, m_norm_mix_w, m_norm_mlp_w, m_w_up, m_w_down, m_norm_final_w, v_w_in, v_lb_logits, v_hg_norm_w, v_rel_bias, v_w_branch_a, v_w_branch_b, v_w_out, v_norm_mix_w, v_norm_mlp_w, v_w_up, v_w_down, v_norm_final_w):
    big_w = [w_in[0], w_branch_a[0], w_branch_b[0], w_out[0], w_up[0], w_down[0]]
    big_m = [m_w_in[0], m_w_branch_a[0], m_w_branch_b[0], m_w_out[0], m_w_up[0], m_w_down[0]]
    big_v = [v_w_in[0], v_w_branch_a[0], v_w_branch_b[0], v_w_out[0], v_w_up[0], v_w_down[0]]

    shards = [w.astype(BF16) for w in big_w]
    w_in_full, = _run_exchange(_gather_exchange(shards[:1]), "all_gather_w_in")
    parity = lax.axis_index("c").astype(jnp.int32).reshape(1)
    loss_part, grad_x, chip_parts, small = _local_step(
        x[0], loss_target[0], lb_logits, hg_norm_w, rel_bias[0], norm_mix_w, norm_mlp_w,
        norm_final_w.reshape(1, D_MODEL), w_in_full, shards[1:], _Exchanges(parity))
    loss = lax.psum(loss_part[0, 0], ("x", "y", "c"))
    big = [_adamw_big(w, m, v, p, "adamw_" + nme)
           for w, m, v, p, nme in zip(big_w, big_m, big_v, chip_parts, BIG_NAMES)]

    sw = dict(lb_logits=lb_logits, hg_norm_w=hg_norm_w, rel_bias=rel_bias[0], norm_mix_w=norm_mix_w,
              norm_mlp_w=norm_mlp_w, norm_final_w=norm_final_w.reshape(1, D_MODEL))
    sm = dict(lb_logits=m_lb_logits, hg_norm_w=m_hg_norm_w, rel_bias=m_rel_bias[0], norm_mix_w=m_norm_mix_w,
              norm_mlp_w=m_norm_mlp_w, norm_final_w=m_norm_final_w.reshape(1, D_MODEL))
    sv = dict(lb_logits=v_lb_logits, hg_norm_w=v_hg_norm_w, rel_bias=v_rel_bias[0], norm_mix_w=v_norm_mix_w,
              norm_mlp_w=v_norm_mlp_w, norm_final_w=v_norm_final_w.reshape(1, D_MODEL))
    gathered = _gather_small(_pack_small(small), "gather_small")
    small_out = [_unpack_small(p) for p in
                 _adamw_small(_pack_small(sw), _pack_small(sm), _pack_small(sv), gathered, "adamw_small")]

    def leaf(kind, nme):
        if nme in BIG_NAMES:
            return big[BIG_NAMES.index(nme)][kind][None]
        p = small_out[kind][nme]
        if nme == "rel_bias":
            return p[None]
        if nme == "norm_final_w":
            return p.reshape(D_MODEL)
        return p

    order = ("w_in", "lb_logits", "hg_norm_w", "rel_bias", "w_branch_a", "w_branch_b", "w_out", "norm_mix_w",
             "norm_mlp_w", "w_up", "w_down", "norm_final_w")
    outs = [loss, grad_x[None]]
    for kind in range(4):
        outs += [leaf(kind, nme) for nme in order]
    return tuple(outs)
```

```python
import functools

import jax
import jax.numpy as jnp
from jax import lax
from jax.experimental import pallas as pl
from jax.experimental.pallas import tpu as pltpu

F32 = jnp.float32
BF16 = jnp.bfloat16
HIGHEST = lax.Precision.HIGHEST
MESH = pl.DeviceIdType.MESH

D_MODEL = 2048
HG_HEADS = 8
HG_DK = 128
HG_WIDTH = 1024
AT_HEADS = 16
AT_DH = 64
AT_WIDTH = 1024
CHUNK = 64
LEFT_CHUNKS = 8
BAND = (LEFT_CHUNKS + 1) * CHUNK
PAD = LEFT_CHUNKS * CHUNK
REL_CLIP = 256
N_REL = 2 * REL_CLIP + 1
N_REL_PAD = 640
D_FF = 4 * D_MODEL
D_IN = 4 * HG_WIDTH + 3 * AT_WIDTH + 2 * D_MODEL
EPS = 1e-6
N_DEV = 8
N_CHIP = 4

ADAM_LR = 0.001
ADAM_B1 = 0.9
ADAM_B2 = 0.999
ADAM_EPS = 1e-08
ADAM_WD = 0.01
ADAM_STEP = 10

COL_HQ, COL_HF, COL_HI, COL_HG = 0, 8, 16, 24
COL_AQ, COL_AK, COL_AV = 32, 40, 48
COL_GATE_A, COL_GATE_B = 7, 9

VMEM_LIMIT = 56 * 1024 * 1024
SMALL_ROWS = 152


def _cparams(sem=None, **kw):
    if sem is not None:
        kw["dimension_semantics"] = sem
    return pltpu.CompilerParams(vmem_limit_bytes=VMEM_LIMIT, **kw)


def _pick(n, cands):
    for c in cands:
        if n % c == 0:
            return c
    return n


def _sigmoid(x):
    return 1.0 / (1.0 + jnp.exp(-x))


ANY = pl.BlockSpec(memory_space=pl.ANY)


def _position():
    return lax.axis_index("x"), lax.axis_index("y"), lax.axis_index("c")


class _Exchange:
    def __init__(self, arrays, out_shape, scratch, start, end, mid=None, mid_step=None):
        self.arrays, self.out_shape, self.scratch = list(arrays), list(out_shape), list(scratch)
        self.start, self.mid, self.end, self.mid_step = start, mid, end, mid_step


def _call(body, args, *, name, grid, in_specs, out_specs, out_shape, scratch_shapes=(), sem=None, comm=None):
    scratch_shapes = list(scratch_shapes)
    if comm is None:
        outs = pl.pallas_call(
            body, name=name, grid=grid, in_specs=in_specs, out_specs=out_specs, out_shape=out_shape,
            scratch_shapes=scratch_shapes, compiler_params=_cparams(sem))(*args)
        return list(outs), []

    n_in, n_out, n_sc = len(args), len(out_shape), len(scratch_shapes)
    n_ci, n_co = len(comm.arrays), len(comm.out_shape)
    n_steps = 1
    for g in grid:
        n_steps *= g

    def hosted(*refs):
        ins, refs = refs[:n_in], refs[n_in:]
        c_ins, refs = refs[:n_ci], refs[n_ci:]
        outs, refs = refs[:n_out], refs[n_out:]
        c_outs, refs = refs[:n_co], refs[n_co:]
        scratch, c_sems = refs[:n_sc], refs[n_sc:]
        step = 0
        for i, g in enumerate(grid):
            step = step * g + pl.program_id(i)

        @pl.when(step == 0)
        def _():
            comm.start(c_ins, c_outs, c_sems)

        body(*ins, *outs, *scratch)

        if comm.mid is not None:
            @pl.when(step == comm.mid_step)
            def _():
                comm.mid(c_ins, c_outs, c_sems)

        @pl.when(step == n_steps - 1)
        def _():
            comm.end(c_ins, c_outs, c_sems)

    outs = pl.pallas_call(
        hosted, name=name, grid=grid,
        in_specs=list(in_specs) + [ANY] * n_ci, out_specs=list(out_specs) + [ANY] * n_co,
        out_shape=list(out_shape) + comm.out_shape, scratch_shapes=scratch_shapes + comm.scratch,
        compiler_params=_cparams(("arbitrary",) * len(grid)))(*args, *comm.arrays)
    return list(outs[:n_out]), list(outs[n_out:])


def _mm_nn(a, wb, out_dtype, name, comm=None):
    M, K = a.shape
    NB, K2, Nb = wb.shape
    assert K == K2
    tm = min(M, 1024)
    tk = min(K, 2048)
    tn = _pick(Nb, (512, 1408, 256))
    nk = K // tk
    nn = Nb // tn

    def body(a_ref, b_ref, o_ref, *acc):
        part = jnp.dot(a_ref[...], b_ref[...], preferred_element_type=F32)
        if nk == 1:
            o_ref[...] = part.astype(out_dtype)
        else:
            acc_ref, = acc
            k = pl.program_id(3)

            @pl.when(k == 0)
            def _():
                acc_ref[...] = part

            @pl.when(k > 0)
            def _():
                acc_ref[...] += part

            @pl.when(k == nk - 1)
            def _():
                o_ref[...] = acc_ref[...].astype(out_dtype)

    grid = (M // tm, NB, nn, nk)
    if comm is not None and comm.mid is not None:
        comm.mid_step = (3 * grid[0] * grid[1] * grid[2] * grid[3]) // 4
    (out,), got = _call(
        body, (a, wb), name=name, grid=grid,
        in_specs=[pl.BlockSpec((tm, tk), lambda m, j, n, k: (m, k)),
                  pl.BlockSpec((None, tk, tn), lambda m, j, n, k: (j, k, n))],
        out_specs=[pl.BlockSpec((tm, tn), lambda m, j, n, k: (m, j * nn + n))],
        out_shape=[jax.ShapeDtypeStruct((M, NB * Nb), out_dtype)],
        scratch_shapes=[] if nk == 1 else [pltpu.VMEM((tm, tn), F32)],
        sem=("parallel", "parallel", "parallel", "arbitrary"), comm=comm)
    return (out, got) if comm is not None else out


def _mm_nt(a, wb, out_dtype, name, comm=None):
    M, N = a.shape
    NB, K, Nb = wb.shape
    assert N == NB * Nb
    tm = min(M, 1024)
    tko = _pick(K, (1024,))
    tc = _pick(Nb, (1024, 1408, 256))
    nc = Nb // tc
    nsteps = NB * nc

    def body(a_ref, b_ref, o_ref, acc_ref):
        step = pl.program_id(2) * nc + pl.program_id(3)
        part = lax.dot_general(a_ref[...], b_ref[...], (((1,), (1,)), ((), ())), preferred_element_type=F32)

        @pl.when(step == 0)
        def _():
            acc_ref[...] = part

        @pl.when(step > 0)
        def _():
            acc_ref[...] += part

        @pl.when(step == nsteps - 1)
        def _():
            o_ref[...] = acc_ref[...].astype(out_dtype)

    (out,), landed = _call(
        body, (a, wb), name=name,
        grid=(M // tm, K // tko, NB, nc),
        in_specs=[pl.BlockSpec((tm, tc), lambda m, ko, j, c: (m, j * nc + c)),
                  pl.BlockSpec((None, tko, tc), lambda m, ko, j, c: (j, ko, c))],
        out_specs=[pl.BlockSpec((tm, tko), lambda m, ko, j, c: (m, ko))],
        out_shape=[jax.ShapeDtypeStruct((M, K), out_dtype)],
        scratch_shapes=[pltpu.VMEM((tm, tko), F32)],
        sem=("parallel", "parallel", "arbitrary", "arbitrary"), comm=comm)
    return out, landed


def _mm_tn(a, g, nb, out_dtype, name):
    M, Ka = a.shape
    M2, N = g.shape
    assert M == M2 and N % nb == 0
    Nb = N // nb
    tka = _pick(Ka, (1024,))
    tn = _pick(Nb, (512, 1408, 256))
    nn = Nb // tn

    def body(a_ref, g_ref, o_ref):
        o_ref[...] = lax.dot_general(a_ref[...], g_ref[...], (((0,), (0,)), ((), ())),
                                     preferred_element_type=F32).astype(out_dtype)

    return pl.pallas_call(
        body, name=name,
        grid=(Ka // tka, nb, nn),
        in_specs=[pl.BlockSpec((M, tka), lambda ka, j, n: (0, ka)),
                  pl.BlockSpec((M, tn), lambda ka, j, n: (0, j * nn + n))],
        out_specs=pl.BlockSpec((None, tka, tn), lambda ka, j, n: (j, ka, n)),
        out_shape=jax.ShapeDtypeStruct((nb, Ka, Nb), out_dtype),
        compiler_params=_cparams(("parallel", "parallel", "parallel")),
    )(a, g)


ROW_TILE = 256


def _rms_fwd(x, w, name):
    T, Dm = x.shape

    def body(x_ref, w_ref, u_ref):
        xv = x_ref[...]
        r = lax.rsqrt(jnp.mean(xv * xv, axis=-1, keepdims=True) + EPS)
        u_ref[...] = (xv * r * w_ref[...]).astype(BF16)

    return pl.pallas_call(
        body, name=name, grid=(T // ROW_TILE,),
        in_specs=[pl.BlockSpec((ROW_TILE, Dm), lambda i: (i, 0)), pl.BlockSpec((1, Dm), lambda i: (0, 0))],
        out_specs=pl.BlockSpec((ROW_TILE, Dm), lambda i: (i, 0)),
        out_shape=jax.ShapeDtypeStruct((T, Dm), BF16),
        compiler_params=_cparams(("parallel",)),
    )(x, w)


def _resid_rms_fwd(x, mix, w, name):
    T, Dm = x.shape

    def body(x_ref, m_ref, w_ref, h_ref, u_ref):
        h = x_ref[...] + m_ref[...]
        h_ref[...] = h
        r = lax.rsqrt(jnp.mean(h * h, axis=-1, keepdims=True) + EPS)
        u_ref[...] = (h * r * w_ref[...]).astype(BF16)

    row = pl.BlockSpec((ROW_TILE, Dm), lambda i: (i, 0))
    return pl.pallas_call(
        body, name=name, grid=(T // ROW_TILE,),
        in_specs=[row, row, pl.BlockSpec((1, Dm), lambda i: (0, 0))],
        out_specs=[row, row],
        out_shape=[jax.ShapeDtypeStruct((T, Dm), F32), jax.ShapeDtypeStruct((T, Dm), BF16)],
        compiler_params=_cparams(("parallel",)),
    )(x, mix, w)


def _loss_head(h1, mlp, wf, target, name):
    T, Dm = h1.shape

    def body(h_ref, m_ref, w_ref, t_ref, loss_ref, dh_ref, dhb_ref, dw_ref):
        i = pl.program_id(0)
        h = h_ref[...] + m_ref[...]
        r = lax.rsqrt(jnp.mean(h * h, axis=-1, keepdims=True) + EPS)
        xh = h * r
        wv = w_ref[...]
        e = xh * wv - t_ref[...]
        part = 0.5 * jnp.sum(jnp.mean(e * e, axis=-1, keepdims=True), axis=0, keepdims=True)
        dy = e * (1.0 / Dm)
        dw = jnp.sum(dy * xh, axis=0, keepdims=True)
        gy = dy * wv
        dh = r * (gy - xh * jnp.mean(gy * xh, axis=-1, keepdims=True))
        dh_ref[...] = dh
        dhb_ref[...] = dh.astype(BF16)

        @pl.when(i == 0)
        def _():
            loss_ref[...] = jnp.zeros_like(loss_ref)
            dw_ref[...] = jnp.zeros_like(dw_ref)

        loss_ref[...] += jnp.broadcast_to(part, loss_ref.shape)
        dw_ref[...] += dw

    row = pl.BlockSpec((ROW_TILE, Dm), lambda i: (i, 0))
    vec = pl.BlockSpec((1, Dm), lambda i: (0, 0))
    return pl.pallas_call(
        body, name=name, grid=(T // ROW_TILE,),
        in_specs=[row, row, vec, row],
        out_specs=[pl.BlockSpec((8, 128), lambda i: (0, 0)), row, row, vec],
        out_shape=[jax.ShapeDtypeStruct((8, 128), F32), jax.ShapeDtypeStruct((T, Dm), F32),
                   jax.ShapeDtypeStruct((T, Dm), BF16), jax.ShapeDtypeStruct((1, Dm), F32)],
        compiler_params=_cparams(("arbitrary",)),
    )(h1, mlp, wf, target)


def _rms_bwd(dyn, x, w, dres, name):
    T, Dm = x.shape

    def body(g_ref, x_ref, w_ref, r_ref, dx_ref, dxb_ref, dw_ref):
        i = pl.program_id(0)
        xv = x_ref[...]
        r = lax.rsqrt(jnp.mean(xv * xv, axis=-1, keepdims=True) + EPS)
        xh = xv * r
        g = g_ref[...]
        dw = jnp.sum(g * xh, axis=0, keepdims=True)
        gy = g * w_ref[...]
        dx = r_ref[...] + r * (gy - xh * jnp.mean(gy * xh, axis=-1, keepdims=True))
        dx_ref[...] = dx
        dxb_ref[...] = dx.astype(BF16)

        @pl.when(i == 0)
        def _():
            dw_ref[...] = jnp.zeros_like(dw_ref)

        dw_ref[...] += dw

    row = pl.BlockSpec((ROW_TILE, Dm), lambda i: (i, 0))
    vec = pl.BlockSpec((1, Dm), lambda i: (0, 0))
    return pl.pallas_call(
        body, name=name, grid=(T // ROW_TILE,),
        in_specs=[row, row, vec, row],
        out_specs=[row, row, vec],
        out_shape=[jax.ShapeDtypeStruct((T, Dm), F32), jax.ShapeDtypeStruct((T, Dm), BF16),
                   jax.ShapeDtypeStruct((1, Dm), F32)],
        compiler_params=_cparams(("arbitrary",)),
    )(dyn, x, w, dres)


COL_TILE = 2048


def _relu2_fwd(a, name):
    T, N = a.shape

    def body(a_ref, r_ref):
        ra = jnp.maximum(a_ref[...], 0.0)
        r_ref[...] = (ra * ra).astype(BF16)

    blk = pl.BlockSpec((ROW_TILE, COL_TILE), lambda i, j: (i, j))
    return pl.pallas_call(
        body, name=name, grid=(T // ROW_TILE, N // COL_TILE), in_specs=[blk], out_specs=blk,
        out_shape=jax.ShapeDtypeStruct((T, N), BF16),
        compiler_params=_cparams(("parallel", "parallel")),
    )(a)


def _relu2_bwd(dr, a, name):
    T, N = a.shape

    def body(dr_ref, a_ref, da_ref):
        da_ref[...] = (dr_ref[...] * (2.0 * jnp.maximum(a_ref[...], 0.0))).astype(BF16)

    blk = pl.BlockSpec((ROW_TILE, COL_TILE), lambda i, j: (i, j))
    return pl.pallas_call(
        body, name=name, grid=(T // ROW_TILE, N // COL_TILE), in_specs=[blk, blk], out_specs=blk,
        out_shape=jax.ShapeDtypeStruct((T, N), BF16),
        compiler_params=_cparams(("parallel", "parallel")),
    )(dr, a)


GATE_TILE = 1024


def _merge_fwd(z, pa, pb, name):
    T, Dm = pa.shape

    def body(za_ref, zb_ref, pa_ref, pb_ref, m_ref):
        m_ref[...] = (_sigmoid(za_ref[...]) * pa_ref[...] + _sigmoid(zb_ref[...]) * pb_ref[...]).astype(BF16)

    blk = pl.BlockSpec((ROW_TILE, GATE_TILE), lambda i, j: (i, j))
    return pl.pallas_call(
        body, name=name, grid=(T // ROW_TILE, Dm // GATE_TILE),
        in_specs=[pl.BlockSpec((ROW_TILE, GATE_TILE), lambda i, j: (i, COL_GATE_A + j)),
                  pl.BlockSpec((ROW_TILE, GATE_TILE), lambda i, j: (i, COL_GATE_B + j)), blk, blk],
        out_specs=blk,
        out_shape=jax.ShapeDtypeStruct((T, Dm), BF16),
        compiler_params=_cparams(("parallel", "parallel")),
    )(z, z, pa, pb)


def _merge_bwd(dm, z, pa, pb, name):
    T, Dm = pa.shape

    def body(dm_ref, za_ref, zb_ref, pa_ref, pb_ref, dpa_ref, dpb_ref, dga_ref, dgb_ref):
        d = dm_ref[...]
        ga = _sigmoid(za_ref[...])
        gb = _sigmoid(zb_ref[...])
        dpa_ref[...] = (d * ga).astype(BF16)
        dpb_ref[...] = (d * gb).astype(BF16)
        dga_ref[...] = (d * pa_ref[...] * ga * (1.0 - ga)).astype(BF16)
        dgb_ref[...] = (d * pb_ref[...] * gb * (1.0 - gb)).astype(BF16)

    blk = pl.BlockSpec((ROW_TILE, GATE_TILE), lambda i, j: (i, j))
    out = jax.ShapeDtypeStruct((T, Dm), BF16)
    return pl.pallas_call(
        body, name=name, grid=(T // ROW_TILE, Dm // GATE_TILE),
        in_specs=[blk, pl.BlockSpec((ROW_TILE, GATE_TILE), lambda i, j: (i, COL_GATE_A + j)),
                  pl.BlockSpec((ROW_TILE, GATE_TILE), lambda i, j: (i, COL_GATE_B + j)), blk, blk],
        out_specs=[blk, blk, blk, blk],
        out_shape=[out, out, out, out],
        compiler_params=_cparams(("parallel", "parallel")),
    )(dm, z, z, pa, pb)


def _dot_hi(a, b, dims):
    return lax.dot_general(a, b, (dims, ((), ())), precision=HIGHEST, preferred_element_type=F32)


NN = ((1,), (0,))
NT = ((1,), (1,))
TN = ((0,), (0,))


def _hg_gates(hq, hf, lb):
    sq = _sigmoid(hq)
    q = hq * sq * (HG_DK ** -0.5)
    f = _sigmoid(hf)
    g = lb + (1.0 - lb) * f
    return q, sq, f, g, jnp.log(g), 1.0 - g


def _tri(lower):
    r = lax.broadcasted_iota(jnp.int32, (CHUNK, CHUNK), 0)
    c = lax.broadcasted_iota(jnp.int32, (CHUNK, CHUNK), 1)
    return jnp.where((r >= c) if lower else (r <= c), 1.0, 0.0).astype(F32)


def _hgrn2_fwd(z, lb_logits, hg_norm_w, name, comm=None):
    T = z.shape[0]
    n_chunks = T // CHUNK

    def body(hq_ref, hf_ref, hi_ref, hg_ref, lbl_ref, nw_ref, o_ref, ya_ref, sall_ref, st_ref):
        lbl = lbl_ref[...]
        lb = 1.0 / (1.0 + jnp.exp(lbl[1:2, :] - lbl[0:1, :]))
        st_ref[...] = jnp.zeros_like(st_ref)
        tri = _tri(True)
        row8 = lax.broadcasted_iota(jnp.int32, (8, HG_DK), 0)

        def chunk(c, carry):
            rows = pl.ds(pl.multiple_of(c * CHUNK, CHUNK), CHUNK)
            q, _, _, _, lg, kk = _hg_gates(hq_ref[rows, :], hf_ref[rows, :], lb)
            v = hi_ref[rows, :]
            b = _dot_hi(tri, lg, NN)
            st = st_ref[...]
            sall_ref[c] = st
            o_inter = _dot_hi(q * jnp.exp(b), st, NT)
            for g8 in range(CHUNK // 8):
                n = 8 * (g8 + 1)
                bs, ks, vs = b[:n], kk[:n], v[:n]
                sidx = lax.broadcasted_iota(jnp.int32, (n, HG_DK), 0)
                blk = o_inter[8 * g8:n]
                for i in range(8):
                    t = 8 * g8 + i
                    e = jnp.where(sidx <= t, jnp.exp(b[t:t + 1] - bs), 0.0)
                    p = jnp.sum(e * ks * q[t:t + 1], axis=1, keepdims=True)
                    ot = jnp.sum(p * vs, axis=0, keepdims=True)
                    blk = blk + jnp.where(row8 == i, ot, 0.0)
                o_ref[pl.ds(pl.multiple_of(c * CHUNK + 8 * g8, 8), 8), :] = blk
            bl = b[CHUNK - 1:CHUNK]
            ke = kk * jnp.exp(bl - b)
            st_ref[...] = st * jnp.exp(bl) + _dot_hi(v, ke, TN)
            return carry

        lax.fori_loop(0, n_chunks, chunk, 0)
        o = o_ref[...]
        r = lax.rsqrt(jnp.mean(o * o, axis=-1, keepdims=True) + EPS)
        hg = hg_ref[...]
        ya_ref[...] = (o * r * nw_ref[...] * (hg * _sigmoid(hg))).astype(BF16)

    def col(base):
        return pl.BlockSpec((T, HG_DK), lambda h: (0, base + h))

    return _call(
        body, (z, z, z, z, lb_logits, hg_norm_w), name=name, grid=(HG_HEADS,),
        in_specs=[col(COL_HQ), col(COL_HF), col(COL_HI), col(COL_HG),
                  pl.BlockSpec((2, HG_DK), lambda h: (0, h)), pl.BlockSpec((1, HG_DK), lambda h: (0, 0))],
        out_specs=[col(0), col(0), pl.BlockSpec((None, n_chunks, HG_DK, HG_DK), lambda h: (h, 0, 0, 0))],
        out_shape=[jax.ShapeDtypeStruct((T, HG_WIDTH), F32), jax.ShapeDtypeStruct((T, HG_WIDTH), BF16),
                   jax.ShapeDtypeStruct((HG_HEADS, n_chunks, HG_DK, HG_DK), F32)],
        scratch_shapes=[pltpu.VMEM((HG_DK, HG_DK), F32)],
        sem=("parallel",), comm=comm)


def _hgrn2_bwd(z, lb_logits, hg_norm_w, o_raw, s_all, dya, name, comm=None):
    T = z.shape[0]
    n_chunks = T // CHUNK

    def body(hq_ref, hf_ref, hi_ref, hg_ref, lbl_ref, nw_ref, o_ref, sall_ref, dya_ref,
             dhq_ref, dhf_ref, dhi_ref, dhg_ref, dlbl_ref, dnw_ref,
             do_ref, dst_ref, dq_ref, dk_ref, dv_ref, dlb_ref):
        h = pl.program_id(0)
        lbl = lbl_ref[...]
        lb = 1.0 / (1.0 + jnp.exp(lbl[1:2, :] - lbl[0:1, :]))

        o = o_ref[...]
        r = lax.rsqrt(jnp.mean(o * o, axis=-1, keepdims=True) + EPS)
        oh = o * r
        nw = nw_ref[...]
        hg = hg_ref[...]
        sg = _sigmoid(hg)
        dy = dya_ref[...]
        d_on = dy * (hg * sg)
        dhg_ref[...] = (dy * (oh * nw) * (sg * (1.0 + hg * (1.0 - sg)))).astype(BF16)
        dnw = jnp.sum(d_on * oh, axis=0, keepdims=True)
        gy = d_on * nw
        do_ref[...] = r * (gy - oh * jnp.mean(gy * oh, axis=-1, keepdims=True))

        @pl.when(h == 0)
        def _():
            dnw_ref[...] = jnp.zeros_like(dnw_ref)

        dnw_ref[...] += jnp.broadcast_to(dnw, dnw_ref.shape)

        dst_ref[...] = jnp.zeros_like(dst_ref)
        dlb_ref[...] = jnp.zeros_like(dlb_ref)
        tri = _tri(True)
        tri_t = _tri(False)
        row8 = lax.broadcasted_iota(jnp.int32, (8, HG_DK), 0)

        def chunk(ci, carry):
            c = n_chunks - 1 - ci
            rows = pl.ds(pl.multiple_of(c * CHUNK, CHUNK), CHUNK)
            hq = hq_ref[rows, :]
            q, sq, f, g, lg, kk = _hg_gates(hq, hf_ref[rows, :], lb)
            v = hi_ref[rows, :]
            do = do_ref[rows, :]
            b = _dot_hi(tri, lg, NN)
            eb = jnp.exp(b)
            bl = b[CHUNK - 1:CHUNK]
            ebl = jnp.exp(bl)
            ekb = jnp.exp(bl - b)
            qe = q * eb
            ke = kk * ekb
            st = sall_ref[c]
            dst = dst_ref[...]
            dqe = _dot_hi(do, st, NN)
            dke = _dot_hi(v, dst, NN)
            dv_inter = _dot_hi(ke, dst, NT)
            d_ebl = jnp.sum(st * dst, axis=0, keepdims=True)
            dst_ref[...] = dst * ebl + _dot_hi(do, qe, TN)

            dk_ref[...] = jnp.zeros_like(dk_ref)
            dv_ref[...] = jnp.zeros_like(dv_ref)
            for g8 in range(CHUNK // 8):
                n = 8 * (g8 + 1)
                bs, ks, vs = b[:n], kk[:n], v[:n]
                sidx = lax.broadcasted_iota(jnp.int32, (n, HG_DK), 0)
                blk = jnp.zeros((8, HG_DK), F32)
                for i in range(8):
                    t = 8 * g8 + i
                    qt = q[t:t + 1]
                    dot_ = do[t:t + 1]
                    e = jnp.where(sidx <= t, jnp.exp(b[t:t + 1] - bs), 0.0)
                    w = e * ks
                    p = jnp.sum(w * qt, axis=1, keepdims=True)
                    dsc = jnp.sum(vs * dot_, axis=1, keepdims=True)
                    dqt = jnp.sum(dsc * w, axis=0, keepdims=True)
                    blk = blk + jnp.where(row8 == i, dqt, 0.0)
                    dk_ref[0:n, :] += dsc * e * qt
                    dv_ref[0:n, :] += p * dot_
                dq_ref[8 * g8:n, :] = blk
            dq_i = dq_ref[...]
            dk_i = dk_ref[...]
            dke_ke = dke * ke
            db = q * dq_i - kk * dk_i + dqe * qe - dke_ke
            db_last = jnp.sum(dke_ke, axis=0, keepdims=True) + d_ebl * ebl
            dlg = _dot_hi(tri_t, db, NN) + db_last
            dq = dq_i + dqe * eb
            dkk = dk_i + dke * ekb
            dg = dlg / g - dkk
            dhq_ref[rows, :] = (dq * (HG_DK ** -0.5) * (sq * (1.0 + hq * (1.0 - sq)))).astype(BF16)
            dhf_ref[rows, :] = (dg * (1.0 - lb) * f * (1.0 - f)).astype(BF16)
            dhi_ref[rows, :] = (dv_ref[...] + dv_inter).astype(BF16)
            dlb_ref[...] += jnp.sum(dg * (1.0 - f), axis=0, keepdims=True)
            return carry

        lax.fori_loop(0, n_chunks, chunk, 0)
        dl0 = dlb_ref[...] * lb * (1.0 - lb)
        dlbl_ref[0:1, :] = dl0
        dlbl_ref[1:2, :] = -dl0

    def col(base):
        return pl.BlockSpec((T, HG_DK), lambda h: (0, base + h))

    outb = jax.ShapeDtypeStruct((T, HG_WIDTH), BF16)
    return _call(
        body, (z, z, z, z, lb_logits, hg_norm_w, o_raw, s_all, dya), name=name, grid=(HG_HEADS,),
        in_specs=[col(COL_HQ), col(COL_HF), col(COL_HI), col(COL_HG),
                  pl.BlockSpec((2, HG_DK), lambda h: (0, h)), pl.BlockSpec((1, HG_DK), lambda h: (0, 0)),
                  col(0), pl.BlockSpec((None, n_chunks, HG_DK, HG_DK), lambda h: (h, 0, 0, 0)), col(0)],
        out_specs=[col(0), col(0), col(0), col(0), pl.BlockSpec((2, HG_DK), lambda h: (0, h)),
                   pl.BlockSpec((8, HG_DK), lambda h: (0, 0))],
        out_shape=[outb, outb, outb, outb, jax.ShapeDtypeStruct((2, HG_WIDTH), F32),
                   jax.ShapeDtypeStruct((8, HG_DK), F32)],
        scratch_shapes=[pltpu.VMEM((T, HG_DK), F32), pltpu.VMEM((HG_DK, HG_DK), F32),
                        pltpu.VMEM((CHUNK, HG_DK), F32), pltpu.VMEM((CHUNK, HG_DK), F32),
                        pltpu.VMEM((CHUNK, HG_DK), F32), pltpu.VMEM((1, HG_DK), F32)],
        sem=("arbitrary",), comm=comm)


CONST_KEYS = PAD - REL_CLIP
VAR_KEYS = BAND - CONST_KEYS
REL_LO = 128
REL_SPAN = N_REL_PAD - REL_LO


def _rel_onehot(t):
    r = lax.broadcasted_iota(jnp.int32, (REL_SPAN, VAR_KEYS), 0)
    j = lax.broadcasted_iota(jnp.int32, (REL_SPAN, VAR_KEYS), 1)
    idx = jnp.clip(t + PAD - CONST_KEYS - j, -REL_CLIP, REL_CLIP) + REL_CLIP - REL_LO
    return jnp.where(r == idx, 1.0, 0.0).astype(BF16)


def _split3(x):
    hi = x.astype(BF16)
    r1 = x - hi.astype(F32)
    mid = r1.astype(BF16)
    return hi, mid, (r1 - mid.astype(F32)).astype(BF16)


def _bias_expand(rel, name):
    def body(rel_ref, out_ref):
        tab = rel_ref[...]
        onehot = _rel_onehot(pl.program_id(0))
        out_ref[:, 0:CONST_KEYS] = jnp.broadcast_to(tab[:, 2 * REL_CLIP:2 * REL_CLIP + 1], (AT_HEADS, CONST_KEYS))
        out_ref[:, CONST_KEYS:BAND] = sum(
            jnp.dot(piece, onehot, preferred_element_type=F32) for piece in _split3(tab[:, REL_LO:N_REL_PAD]))

    return pl.pallas_call(
        body, name=name, grid=(CHUNK,),
        in_specs=[pl.BlockSpec((AT_HEADS, N_REL_PAD), lambda t: (0, 0))],
        out_specs=pl.BlockSpec((None, AT_HEADS, BAND), lambda t: (t, 0, 0)),
        out_shape=jax.ShapeDtypeStruct((CHUNK, AT_HEADS, BAND), F32),
        compiler_params=_cparams(("parallel",)),
    )(rel)


def _bias_reduce(dbias_t, name):
    def body(db_ref, out_ref):
        t = pl.program_id(0)

        @pl.when(t == 0)
        def _():
            out_ref[...] = jnp.zeros_like(out_ref)

        db = db_ref[...]
        onehot = _rel_onehot(t)
        acc = sum(lax.dot_general(piece, onehot, (NT, ((), ())), preferred_element_type=F32)
                  for piece in _split3(db[:, CONST_KEYS:BAND]))
        lane = lax.broadcasted_iota(jnp.int32, (AT_HEADS, REL_SPAN), 1)
        last = jnp.sum(db[:, 0:CONST_KEYS], axis=1, keepdims=True)
        out_ref[:, REL_LO:N_REL_PAD] += acc + jnp.where(lane == 2 * REL_CLIP - REL_LO, last, 0.0)

    return pl.pallas_call(
        body, name=name, grid=(CHUNK,),
        in_specs=[pl.BlockSpec((None, AT_HEADS, BAND), lambda t: (t, 0, 0))],
        out_specs=pl.BlockSpec((AT_HEADS, N_REL_PAD), lambda t: (0, 0)),
        out_shape=jax.ShapeDtypeStruct((AT_HEADS, N_REL_PAD), F32),
        compiler_params=_cparams(("arbitrary",)),
    )(dbias_t)


def _pair_lanes():
    return lax.broadcasted_iota(jnp.int32, (CHUNK, 2 * AT_DH), 1) < AT_DH


def _block_diag(a):
    first = _pair_lanes()
    return jnp.concatenate([jnp.where(first, a, 0.0), jnp.where(first, 0.0, a)], axis=0).astype(BF16)


def _diag_blocks(a):
    return jnp.where(_pair_lanes(), a[:CHUNK], a[CHUNK:])


def _band_probs_t(kb, qbd, bias_t, c):
    s = lax.dot_general(kb, qbd, (NT, ((), ())), preferred_element_type=F32) * (AT_DH ** -0.5) + bias_t
    j = lax.broadcasted_iota(jnp.int32, (BAND, 2 * AT_DH), 0)
    s = jnp.where(j + c * CHUNK >= PAD, s, -jnp.inf)
    p = jnp.exp(s - jnp.max(s, axis=0, keepdims=True))
    return p / jnp.sum(p, axis=0, keepdims=True)


def _fill_padded(dst_ref, src_ref, T):
    dst_ref[0:PAD, :] = jnp.zeros((PAD, 2 * AT_DH), BF16)
    dst_ref[PAD:PAD + T, :] = src_ref[...].astype(BF16)


def _attn_fwd(z, bias_t, name, comm=None):
    T = z.shape[0]
    n_chunks = T // CHUNK

    def body(q_ref, k_ref, v_ref, bias_ref, y_ref, kp_ref, vp_ref):
        _fill_padded(kp_ref, k_ref, T)
        _fill_padded(vp_ref, v_ref, T)

        def chunk(c, carry):
            rows = pl.ds(pl.multiple_of(c * CHUNK, CHUNK), CHUNK)
            band = pl.ds(pl.multiple_of(c * CHUNK, CHUNK), BAND)
            p = _band_probs_t(kp_ref[band, :], _block_diag(q_ref[rows, :]), bias_ref[...], c)
            o2 = lax.dot_general(p.astype(BF16), vp_ref[band, :], (TN, ((), ())), preferred_element_type=F32)
            y_ref[rows, :] = _diag_blocks(o2).astype(BF16)
            return carry

        lax.fori_loop(0, n_chunks, chunk, 0, unroll=2)

    def col(base):
        return pl.BlockSpec((T, 128), lambda h: (0, base + h))

    return _call(
        body, (z, z, z, bias_t), name=name, grid=(AT_HEADS // 2,),
        in_specs=[col(COL_AQ), col(COL_AK), col(COL_AV), pl.BlockSpec((None, BAND, 128), lambda h: (h, 0, 0))],
        out_specs=[col(0)],
        out_shape=[jax.ShapeDtypeStruct((T, AT_WIDTH), BF16)],
        scratch_shapes=[pltpu.VMEM((PAD + T, 128), BF16), pltpu.VMEM((PAD + T, 128), BF16)],
        sem=("parallel",), comm=comm)


def _attn_bwd(z, bias_t, dyb, name, comm=None):
    T = z.shape[0]
    n_chunks = T // CHUNK

    def body(q_ref, k_ref, v_ref, bias_ref, dy_ref, dq_ref, dk_ref, dv_ref, dbias_ref,
             kp_ref, vp_ref, dkp_ref, dvp_ref):
        _fill_padded(kp_ref, k_ref, T)
        _fill_padded(vp_ref, v_ref, T)
        dkp_ref[...] = jnp.zeros_like(dkp_ref)
        dvp_ref[...] = jnp.zeros_like(dvp_ref)
        dbias_ref[...] = jnp.zeros_like(dbias_ref)

        def chunk(c, carry):
            rows = pl.ds(pl.multiple_of(c * CHUNK, CHUNK), CHUNK)
            band = pl.ds(pl.multiple_of(c * CHUNK, CHUNK), BAND)
            qbd = _block_diag(q_ref[rows, :])
            dobd = _block_diag(dy_ref[rows, :])
            kb = kp_ref[band, :]
            vb = vp_ref[band, :]
            p = _band_probs_t(kb, qbd, bias_ref[...], c)
            dp = lax.dot_general(vb, dobd, (NT, ((), ())), preferred_element_type=F32)
            ds = p * (dp - jnp.sum(dp * p, axis=0, keepdims=True))
            dbias_ref[...] += ds
            dsb = ds.astype(BF16)
            dq2 = lax.dot_general(dsb, kb, (TN, ((), ())), preferred_element_type=F32)
            dq_ref[rows, :] = (_diag_blocks(dq2) * (AT_DH ** -0.5)).astype(BF16)
            dkp_ref[band, :] += jnp.dot(dsb, qbd, preferred_element_type=F32) * (AT_DH ** -0.5)
            dvp_ref[band, :] += jnp.dot(p.astype(BF16), dobd, preferred_element_type=F32)
            return carry

        lax.fori_loop(0, n_chunks, chunk, 0)
        dk_ref[...] = dkp_ref[PAD:PAD + T, :].astype(BF16)
        dv_ref[...] = dvp_ref[PAD:PAD + T, :].astype(BF16)

    def col(base):
        return pl.BlockSpec((T, 128), lambda h: (0, base + h))

    pair = pl.BlockSpec((None, BAND, 128), lambda h: (h, 0, 0))
    outb = jax.ShapeDtypeStruct((T, AT_WIDTH), BF16)
    return _call(
        body, (z, z, z, bias_t, dyb), name=name, grid=(AT_HEADS // 2,),
        in_specs=[col(COL_AQ), col(COL_AK), col(COL_AV), pair, col(0)],
        out_specs=[col(0), col(0), col(0), pair],
        out_shape=[outb, outb, outb, jax.ShapeDtypeStruct((AT_HEADS // 2, BAND, 128), F32)],
        scratch_shapes=[pltpu.VMEM((PAD + T, 128), BF16), pltpu.VMEM((PAD + T, 128), BF16),
                        pltpu.VMEM((PAD + T, 128), F32), pltpu.VMEM((PAD + T, 128), F32)],
        sem=("parallel",), comm=comm)


def _local_step(x, target, lb_logits, hg_norm_w, rel_bias, norm_mix_w, norm_mlp_w, norm_final_w,
                w_in, rest, exchanges=None):
    ex = exchanges
    rel = jnp.pad(rel_bias, ((0, 0), (0, N_REL_PAD - N_REL)))

    u = _rms_fwd(x, norm_mix_w, "rms_mix_fwd")
    if ex:
        z, w_in = _mm_gathered(u, w_in, ex.order, "mm_in_fwd")
    else:
        z = _mm_nn(u, w_in, F32, "mm_in_fwd")
    (o_raw, y_a, s_all), got_abou = _hgrn2_fwd(z, lb_logits, hg_norm_w, "hgrn2_fwd",
                                               comm=ex and ex.gather(rest[:4], mid_step=HG_HEADS - 2))
    bias_rows = _bias_expand(rel, "bias_expand")
    bias_t = jnp.transpose(bias_rows.reshape(CHUNK, AT_HEADS // 2, 2, BAND), (1, 3, 2, 0)).reshape(
        AT_HEADS // 2, BAND, 2 * CHUNK)
    (y_b,), got_down = _attn_fwd(z, bias_t, "attn_fwd",
                                 comm=ex and ex.gather(rest[4:], mid_step=AT_HEADS // 2 - 2))
    w_a, w_b, w_out, w_up, w_down = (got_abou + got_down) if ex else rest
    w_out1 = w_out.reshape(1, D_MODEL, D_MODEL)
    w_down1 = w_down.reshape(1, D_FF, D_MODEL)
    pa = _mm_nn(y_a, w_a, F32, "mm_a_fwd")
    pb = _mm_nn(y_b, w_b, F32, "mm_b_fwd")
    merged = _merge_fwd(z, pa, pb, "merge_fwd")
    mix = _mm_nn(merged, w_out1, F32, "mm_out_fwd")
    h1, u2 = _resid_rms_fwd(x, mix, norm_mlp_w, "rms_mlp_fwd")
    a = _mm_nn(u2, w_up, F32, "mm_up_fwd")
    r = _relu2_fwd(a, "relu2_fwd")
    mlp = _mm_nn(r, w_down1, F32, "mm_down_fwd")
    loss, dh2, dh2b, g_nf = _loss_head(h1, mlp, norm_final_w, target, "loss_head")

    g_down = _mm_tn(r, dh2b, 1, BF16, "mm_down_wgrad").reshape(N_DEV, D_FF // N_DEV, D_MODEL)
    dr, _ = _mm_nt(dh2b, w_down1, F32, "mm_down_dgrad")
    da = _relu2_bwd(dr, a, "relu2_bwd")
    g_up = _mm_tn(u2, da, N_DEV, BF16, "mm_up_wgrad")
    du2, _ = _mm_nt(da, w_up, F32, "mm_up_dgrad")
    dh1, dh1b, g_nmlp = _rms_bwd(du2, h1, norm_mlp_w, dh2, "rms_mlp_bwd")

    g_out = _mm_tn(merged, dh1b, 1, BF16, "mm_out_wgrad").reshape(N_DEV, D_MODEL // N_DEV, D_MODEL)
    dmerged, _ = _mm_nt(dh1b, w_out1, F32, "mm_out_dgrad")
    dpa, dpb, dga, dgb = _merge_bwd(dmerged, z, pa, pb, "merge_bwd")
    g_a = _mm_tn(y_a, dpa, N_DEV, BF16, "mm_a_wgrad")
    g_b = _mm_tn(y_b, dpb, N_DEV, BF16, "mm_b_wgrad")
    dya, _ = _mm_nt(dpa, w_a, F32, "mm_a_dgrad")
    dyb, _ = _mm_nt(dpb, w_b, F32, "mm_b_dgrad")
    early = [g_a, g_b, g_out, g_up, g_down]
    early_sums = ex.pair_sums(early, "early") if ex else None
    (daq, dak, dav, dbias_t), parts_down = _attn_bwd(
        z, bias_t, dyb, "attn_bwd", comm=ex and ex.scatter(early_sums[4:]))
    (dhq, dhf, dhi, dhg, g_lbl, g_hgw), parts_abou = _hgrn2_bwd(
        z, lb_logits, hg_norm_w, o_raw, s_all, dya, "hgrn2_bwd", comm=ex and ex.scatter(early_sums[:4]))
    early_parts = parts_abou + parts_down
    dbias_rows = jnp.transpose(dbias_t.reshape(AT_HEADS // 2, BAND, 2, CHUNK), (3, 0, 2, 1)).reshape(
        CHUNK, AT_HEADS, BAND)
    g_rel = _bias_reduce(dbias_rows, "bias_reduce")[:, :N_REL]
    dz = jnp.concatenate([dhq, dhf, dhi, dhg, daq, dak, dav, dga, dgb], axis=1)
    g_in = _mm_tn(u, dz, N_DEV, BF16, "mm_in_wgrad")
    du, late_parts = _mm_nt(dz, w_in, F32, "mm_in_dgrad", comm=ex and ex.scatter(ex.pair_sums([g_in], "late")))
    grad_x, _, g_nmix = _rms_bwd(du, x, norm_mix_w, dh1, "rms_mix_bwd")

    small = dict(lb_logits=g_lbl, hg_norm_w=g_hgw[0:1], rel_bias=g_rel, norm_mix_w=g_nmix, norm_mlp_w=g_nmlp,
                 norm_final_w=g_nf)
    grads = (late_parts + early_parts) if ex else ([g_in] + early)
    return loss, grad_x, grads, small


def _gather_exchange(shards, mid_step=None):
    n = len(shards)

    def parts(ins, outs, sems):
        send_sems, recv_sems, local_sems = sems
        x, y, c = _position()
        chips = [(1 - x, y), (x, 1 - y), (1 - x, 1 - y)]

        def copy(w, k, block, to, src=None):
            dst = outs[w].at[4 * block[0] + 2 * block[1] + block[2]]
            return pltpu.make_async_remote_copy(
                src_ref=dst if src is None else src, dst_ref=dst,
                send_sem=send_sems.at[w, k], recv_sem=recv_sems.at[w, k], device_id=to, device_id_type=MESH)

        def local(w):
            return pltpu.make_async_copy(ins[w], outs[w].at[4 * x + 2 * y + c], local_sems.at[w])

        return (x, y, c), (x, y, 1 - c), chips, copy, local

    def start(ins, outs, sems):
        me, sibling, chips, copy, local = parts(ins, outs, sems)
        for w in range(n):
            local(w).start()
        for w in range(n):
            copy(w, 0, me, sibling, src=ins[w]).start()
            for j, chip in enumerate(chips):
                copy(w, 1 + j, me, (*chip, me[2]), src=ins[w]).start()

    def mid(ins, outs, sems):
        me, sibling, chips, copy, _ = parts(ins, outs, sems)
        for w in range(n):
            for j, chip in enumerate(chips):
                copy(w, 1 + j, (*chip, me[2]), me).wait_recv()
                copy(w, 4 + j, (*chip, me[2]), sibling).start()

    def end(ins, outs, sems):
        me, sibling, chips, copy, local = parts(ins, outs, sems)
        for w in range(n):
            copy(w, 0, sibling, me).wait_recv()
            for j, chip in enumerate(chips):
                copy(w, 4 + j, (*chip, sibling[2]), me).wait_recv()
        for w in range(n):
            for k in range(7):
                copy(w, k, me, sibling).wait_send()
            local(w).wait()

    return _Exchange(
        shards, [jax.ShapeDtypeStruct((N_DEV,) + s.shape, s.dtype) for s in shards],
        [pltpu.SemaphoreType.DMA((n, 7)), pltpu.SemaphoreType.DMA((n, 7)), pltpu.SemaphoreType.DMA((n,))],
        start, end, mid, mid_step)


def _mm_gathered(u, shard, order, name):
    T, K = u.shape
    _, Nb = shard.shape

    def body(order_ref, u_ref, shard_ref, z_ref, full_ref, wbuf, load_sem, send_sems, recv_sems, local_sem):
        s = pl.program_id(0)
        x, y, c = _position()
        me, sibling = (x, y, c), (x, y, 1 - c)
        chips = [(1 - x, y), (x, 1 - y), (1 - x, 1 - y)]

        def copy(k, block, to, src=None):
            dst = full_ref.at[4 * block[0] + 2 * block[1] + block[2]]
            return pltpu.make_async_remote_copy(
                src_ref=dst if src is None else src, dst_ref=dst,
                send_sem=send_sems.at[k], recv_sem=recv_sems.at[k], device_id=to, device_id_type=MESH)

        @pl.when(s == 0)
        def _():
            local = pltpu.make_async_copy(shard_ref, full_ref.at[4 * x + 2 * y + c], local_sem)
            local.start()
            copy(0, me, sibling, src=shard_ref).start()
            for j, chip in enumerate(chips):
                copy(1 + j, me, (*chip, c), src=shard_ref).start()
            local.wait()

        @pl.when(s == 1)
        def _():
            copy(0, sibling, me).wait_recv()

        for j, chip in enumerate(chips):
            @pl.when(s == 2 + j)
            def _(j=j, chip=chip):
                copy(1 + j, (*chip, c), me).wait_recv()
                copy(4 + j, (*chip, c), sibling).start()

            @pl.when(s == 5 + j)
            def _(j=j, chip=chip):
                copy(4 + j, (*chip, 1 - c), me).wait_recv()

        load = pltpu.make_async_copy(full_ref.at[order_ref[s]], wbuf, load_sem)
        load.start()
        load.wait()
        z_ref[...] = jnp.dot(u_ref[...], wbuf[...], preferred_element_type=F32)

        @pl.when(s == N_DEV - 1)
        def _():
            for k in range(7):
                copy(k, me, sibling).wait_send()

    return pl.pallas_call(
        body, name=name,
        grid_spec=pltpu.PrefetchScalarGridSpec(
            num_scalar_prefetch=1, grid=(N_DEV,),
            in_specs=[pl.BlockSpec((T, K), lambda s, order: (0, 0)), ANY],
            out_specs=[pl.BlockSpec((T, Nb), lambda s, order: (0, order[s])), ANY],
            scratch_shapes=[pltpu.VMEM((K, Nb), BF16), pltpu.SemaphoreType.DMA,
                            pltpu.SemaphoreType.DMA((7,)), pltpu.SemaphoreType.DMA((7,)), pltpu.SemaphoreType.DMA]),
        out_shape=[jax.ShapeDtypeStruct((T, N_DEV * Nb), F32), jax.ShapeDtypeStruct((N_DEV, K, Nb), BF16)],
        compiler_params=_cparams(("arbitrary",)),
    )(order, u, shard)


def _gather_order():
    x, y, c = _position()
    chips = [(1 - x, y), (x, 1 - y), (1 - x, 1 - y)]
    ids = [4 * x + 2 * y + c, 4 * x + 2 * y + (1 - c)]
    ids += [4 * cx + 2 * cy + c for cx, cy in chips] + [4 * cx + 2 * cy + (1 - c) for cx, cy in chips]
    return jnp.stack(ids).astype(jnp.int32)


def _run_exchange(comm, name):
    n_i, n_o = len(comm.arrays), len(comm.out_shape)

    def body(*refs):
        ins, outs, sems = refs[:n_i], refs[n_i:n_i + n_o], refs[n_i + n_o:]
        comm.start(ins, outs, sems)
        if comm.mid is not None:
            comm.mid(ins, outs, sems)
        comm.end(ins, outs, sems)

    return pl.pallas_call(
        body, name=name, in_specs=[ANY] * n_i, out_specs=[ANY] * n_o, out_shape=comm.out_shape,
        scratch_shapes=comm.scratch)(*comm.arrays)


def _exchange_sibling(grads, name):
    n = len(grads)

    def body(*refs):
        ins, outs = refs[:n], refs[n:2 * n]
        send_sems, recv_sems = refs[2 * n:]
        x, y, c = _position()
        copies = []
        for w in range(n):
            for s in range(N_CHIP):
                cp = pltpu.make_async_remote_copy(
                    src_ref=ins[w].at[2 * s + (1 - c)], dst_ref=outs[w].at[s],
                    send_sem=send_sems.at[w, s], recv_sem=recv_sems.at[w, s],
                    device_id=(x, y, 1 - c), device_id_type=MESH)
                cp.start()
                copies.append(cp)
        for cp in copies:
            cp.wait()

    return pl.pallas_call(
        body, name=name,
        in_specs=[ANY] * n, out_specs=[ANY] * n,
        out_shape=[jax.ShapeDtypeStruct((N_CHIP,) + g.shape[1:], g.dtype) for g in grads],
        scratch_shapes=[pltpu.SemaphoreType.DMA((n, N_CHIP)), pltpu.SemaphoreType.DMA((n, N_CHIP))],
    )(*grads)


def _pair_sum(g, land, parity, name):
    _, R, C = g.shape
    tr = _pick(R, (512, 256))

    def body(par_ref, g_ref, l_ref, o_ref):
        o_ref[...] = (g_ref[...].astype(F32) + l_ref[...].astype(F32)).astype(BF16)

    return pl.pallas_call(
        body, name=name,
        grid_spec=pltpu.PrefetchScalarGridSpec(
            num_scalar_prefetch=1, grid=(N_CHIP, R // tr),
            in_specs=[pl.BlockSpec((None, tr, C), lambda s, i, par: (2 * s + par[0], i, 0)),
                      pl.BlockSpec((None, tr, C), lambda s, i, par: (s, i, 0))],
            out_specs=pl.BlockSpec((None, tr, C), lambda s, i, par: (s, i, 0))),
        out_shape=jax.ShapeDtypeStruct((N_CHIP, R, C), BF16),
        compiler_params=_cparams(("parallel", "parallel")),
    )(parity, g, land)


def _scatter_exchange(partials):
    n = len(partials)

    def copies(ins, outs, sems):
        send_sems, recv_sems, local_sems = sems
        x, y, c = _position()
        chips = [(1 - x, y), (x, 1 - y), (1 - x, 1 - y)]
        my_slot = 2 * x + y
        local = [pltpu.make_async_copy(ins[w].at[my_slot], outs[w].at[my_slot], local_sems.at[w]) for w in range(n)]
        remote = [pltpu.make_async_remote_copy(
            src_ref=ins[w].at[2 * chip[0] + chip[1]], dst_ref=outs[w].at[my_slot],
            send_sem=send_sems.at[w, j], recv_sem=recv_sems.at[w, j], device_id=(*chip, c), device_id_type=MESH)
            for w in range(n) for j, chip in enumerate(chips)]
        return local, remote

    def start(ins, outs, sems):
        local, remote = copies(ins, outs, sems)
        for cp in local + remote:
            cp.start()

    def end(ins, outs, sems):
        local, remote = copies(ins, outs, sems)
        for cp in remote + local:
            cp.wait()

    return _Exchange(
        partials, [jax.ShapeDtypeStruct(p.shape, p.dtype) for p in partials],
        [pltpu.SemaphoreType.DMA((n, 3)), pltpu.SemaphoreType.DMA((n, 3)), pltpu.SemaphoreType.DMA((n,))],
        start, end)


class _Exchanges:
    def __init__(self, parity, order):
        self.parity, self.order = parity, order

    def gather(self, shards, mid_step):
        return _gather_exchange(list(shards), mid_step)

    def pair_sums(self, grads, tag):
        landed = _exchange_sibling(list(grads), "rs_sibling_" + tag)
        return [_pair_sum(g, l, self.parity, "rs_pair_sum_%s_%d" % (tag, i))
                for i, (g, l) in enumerate(zip(grads, landed))]

    def scatter(self, partials):
        return _scatter_exchange(partials)


def _gather_small(packed, name):
    R = packed.shape[0]

    def body(x_ref, out_ref, send_sems, recv_sems):
        x, y, c = _position()
        me = 4 * x + 2 * y + c
        out_ref[me] = x_ref[...]
        copies = []
        for k in range(1, N_DEV):
            to = (x ^ ((k >> 2) & 1), y ^ ((k >> 1) & 1), c ^ (k & 1))
            cp = pltpu.make_async_remote_copy(
                src_ref=x_ref, dst_ref=out_ref.at[me],
                send_sem=send_sems.at[k], recv_sem=recv_sems.at[k], device_id=to, device_id_type=MESH)
            cp.start()
            copies.append((k, to, cp))
        for k, to, cp in copies:
            cp.wait_send()
            pltpu.make_async_remote_copy(
                src_ref=x_ref, dst_ref=out_ref.at[4 * to[0] + 2 * to[1] + to[2]],
                send_sem=send_sems.at[k], recv_sem=recv_sems.at[k], device_id=to, device_id_type=MESH).wait_recv()

    return pl.pallas_call(
        body, name=name,
        in_specs=[pl.BlockSpec(memory_space=pltpu.VMEM)], out_specs=pl.BlockSpec(memory_space=pltpu.VMEM),
        out_shape=jax.ShapeDtypeStruct((N_DEV, R, 128), F32),
        scratch_shapes=[pltpu.SemaphoreType.DMA((N_DEV,)), pltpu.SemaphoreType.DMA((N_DEV,))],
    )(packed)


def _adamw_math(w, g, m, v):
    m = ADAM_B1 * m + (1.0 - ADAM_B1) * g
    v = ADAM_B2 * v + (1.0 - ADAM_B2) * (g * g)
    m_hat = m / (1.0 - ADAM_B1 ** ADAM_STEP)
    v_hat = v / (1.0 - ADAM_B2 ** ADAM_STEP)
    delta = -ADAM_LR * (m_hat / (jnp.sqrt(v_hat) + ADAM_EPS) + ADAM_WD * w)
    return delta, m, v


def _adamw_big(w, m, v, parts, name):
    R, C = w.shape
    tr = _pick(R, (256,))

    def body(w_ref, m_ref, v_ref, p_ref, g_ref, d_ref, nm_ref, nv_ref):
        g = p_ref[0].astype(F32)
        for s in range(1, N_CHIP):
            g = g + p_ref[s].astype(F32)
        d, nm, nv = _adamw_math(w_ref[...], g, m_ref[...], v_ref[...])
        g_ref[...] = g
        d_ref[...] = d
        nm_ref[...] = nm
        nv_ref[...] = nv

    blk = pl.BlockSpec((tr, C), lambda i: (i, 0))
    out = jax.ShapeDtypeStruct((R, C), F32)
    return pl.pallas_call(
        body, name=name, grid=(R // tr,),
        in_specs=[blk, blk, blk, pl.BlockSpec((N_CHIP, tr, C), lambda i: (0, i, 0))],
        out_specs=[blk, blk, blk, blk], out_shape=[out, out, out, out],
        compiler_params=_cparams(("parallel",)),
    )(w, m, v, parts)


def _adamw_small(w, m, v, gathered, name):
    R = w.shape[0]

    def body(w_ref, m_ref, v_ref, p_ref, g_ref, d_ref, nm_ref, nv_ref):
        g = p_ref[0]
        for s in range(1, N_DEV):
            g = g + p_ref[s]
        d, nm, nv = _adamw_math(w_ref[...], g, m_ref[...], v_ref[...])
        g_ref[...] = g
        d_ref[...] = d
        nm_ref[...] = nm
        nv_ref[...] = nv

    out = jax.ShapeDtypeStruct((R, 128), F32)
    return pl.pallas_call(
        body, name=name, out_shape=[out, out, out, out],
    )(w, m, v, gathered)


SMALL_NAMES = ("lb_logits", "hg_norm_w", "rel_bias", "norm_mix_w", "norm_mlp_w", "norm_final_w")
SMALL_SHAPES = {"lb_logits": (2, HG_WIDTH), "hg_norm_w": (1, HG_DK), "rel_bias": (AT_HEADS, N_REL_PAD),
                "norm_mix_w": (1, D_MODEL), "norm_mlp_w": (1, D_MODEL), "norm_final_w": (1, D_MODEL)}


def _pack_small(parts):
    rows = []
    for nme in SMALL_NAMES:
        p = parts[nme]
        if nme == "rel_bias":
            p = jnp.pad(p, ((0, 0), (0, N_REL_PAD - N_REL)))
        rows.append(p.reshape(-1, 128))
    flat = jnp.concatenate(rows, axis=0)
    return jnp.pad(flat, ((0, SMALL_ROWS - flat.shape[0]), (0, 0)))


def _unpack_small(packed):
    out, at = {}, 0
    for nme in SMALL_NAMES:
        shp = SMALL_SHAPES[nme]
        nrow = shp[0] * shp[1] // 128
        p = packed[at:at + nrow].reshape(shp)
        at += nrow
        out[nme] = p[:, :N_REL] if nme == "rel_bias" else p
    return out


BIG_NAMES = ("w_in", "w_branch_a", "w_branch_b", "w_out", "w_up", "w_down")


def kernel(x, w_in, lb_logits, hg_norm_w, rel_bias, w_branch_a, w_branch_b, w_out, norm_mix_w, norm_mlp_w, w_up, w_down, norm_final_w, loss_target, m_w_in, m_lb_logits, m_hg_norm_w, m_rel_bias, m_w_branch_a, m_w_branch_b, m_w_out, m_norm_mix_w, m_norm_mlp_w, m_w_up, m_w_down, m_norm_final_w, v_w_in, v_lb_logits, v_hg_norm_w, v_rel_bias, v_w_branch_a, v_w_branch_b, v_w_out, v_norm_mix_w, v_norm_mlp_w, v_w_up, v_w_down, v_norm_final_w):
    big_w = [w_in[0], w_branch_a[0], w_branch_b[0], w_out[0], w_up[0], w_down[0]]
    big_m = [m_w_in[0], m_w_branch_a[0], m_w_branch_b[0], m_w_out[0], m_w_up[0], m_w_down[0]]
    big_v = [v_w_in[0], v_w_branch_a[0], v_w_branch_b[0], v_w_out[0], v_w_up[0], v_w_down[0]]

    shards = [w.astype(BF16) for w in big_w]
    parity = lax.axis_index("c").astype(jnp.int32).reshape(1)
    loss_part, grad_x, chip_parts, small = _local_step(
        x[0], loss_target[0], lb_logits, hg_norm_w, rel_bias[0], norm_mix_w, norm_mlp_w,
        norm_final_w.reshape(1, D_MODEL), shards[0], shards[1:], _Exchanges(parity, _gather_order()))
    loss = lax.psum(loss_part[0, 0], ("x", "y", "c"))
    big = [_adamw_big(w, m, v, p, "adamw_" + nme)
           for w, m, v, p, nme in zip(big_w, big_m, big_v, chip_parts, BIG_NAMES)]

    sw = dict(lb_logits=lb_logits, hg_norm_w=hg_norm_w, rel_bias=rel_bias[0], norm_mix_w=norm_mix_w,
              norm_mlp_w=norm_mlp_w, norm_final_w=norm_final_w.reshape(1, D_MODEL))
    sm = dict(lb_logits=m_lb_logits, hg_norm_w=m_hg_norm_w, rel_bias=m_rel_bias[0], norm_mix_w=m_norm_mix_w,
              norm_mlp_w=m_norm_mlp_w, norm_final_w=m_norm_final_w.reshape(1, D_MODEL))
    sv = dict(lb_logits=v_lb_logits, hg_norm_w=v_hg_norm_w, rel_bias=v_rel_bias[0], norm_mix_w=v_norm_mix_w,
              norm_mlp_w=v_norm_mlp_w, norm_final_w=v_norm_final_w.reshape(1, D_MODEL))
    gathered = _gather_small(_pack_small(small), "gather_small")
    small_out = [_unpack_small(p) for p in
                 _adamw_small(_pack_small(sw), _pack_small(sm), _pack_small(sv), gathered, "adamw_small")]

    def leaf(kind, nme):
        if nme in BIG_NAMES:
            return big[BIG_NAMES.index(nme)][kind][None]
        p = small_out[kind][nme]
        if nme == "rel_bias":
            return p[None]
        if nme == "norm_final_w":
            return p.reshape(D_MODEL)
        return p

    order = ("w_in", "lb_logits", "hg_norm_w", "rel_bias", "w_branch_a", "w_branch_b", "w_out", "norm_mix_w",
             "norm_mlp_w", "w_up", "w_down", "norm_final_w")
    outs = [loss, grad_x[None]]
    for kind in range(4):
        outs += [leaf(kind, nme) for nme in order]
    return tuple(outs)
```

```python
import functools

import jax
import jax.numpy as jnp
from jax import lax
from jax.experimental import pallas as pl
from jax.experimental.pallas import tpu as pltpu

F32 = jnp.float32
BF16 = jnp.bfloat16
HIGHEST = lax.Precision.HIGHEST
MESH = pl.DeviceIdType.MESH

D_MODEL = 2048
HG_HEADS = 8
HG_DK = 128
HG_WIDTH = 1024
AT_HEADS = 16
AT_DH = 64
AT_WIDTH = 1024
CHUNK = 64
LEFT_CHUNKS = 8
BAND = (LEFT_CHUNKS + 1) * CHUNK
PAD = LEFT_CHUNKS * CHUNK
REL_CLIP = 256
N_REL = 2 * REL_CLIP + 1
N_REL_PAD = 640
D_FF = 4 * D_MODEL
D_IN = 4 * HG_WIDTH + 3 * AT_WIDTH + 2 * D_MODEL
EPS = 1e-6
N_DEV = 8
N_CHIP = 4

ADAM_LR = 0.001
ADAM_B1 = 0.9
ADAM_B2 = 0.999
ADAM_EPS = 1e-08
ADAM_WD = 0.01
ADAM_STEP = 10

COL_HQ, COL_HF, COL_HI, COL_HG = 0, 8, 16, 24
COL_AQ, COL_AK, COL_AV = 32, 40, 48
COL_GATE_A, COL_GATE_B = 7, 9

VMEM_LIMIT = 56 * 1024 * 1024
SMALL_ROWS = 152


def _cparams(sem=None, **kw):
    if sem is not None:
        kw["dimension_semantics"] = sem
    return pltpu.CompilerParams(vmem_limit_bytes=VMEM_LIMIT, **kw)


def _pick(n, cands):
    for c in cands:
        if n % c == 0:
            return c
    return n


def _sigmoid(x):
    return 1.0 / (1.0 + jnp.exp(-x))


ANY = pl.BlockSpec(memory_space=pl.ANY)


def _position():
    return lax.axis_index("x"), lax.axis_index("y"), lax.axis_index("c")


class _Exchange:
    def __init__(self, arrays, out_shape, scratch, start, end, mid=None, mid_step=None):
        self.arrays, self.out_shape, self.scratch = list(arrays), list(out_shape), list(scratch)
        self.start, self.mid, self.end, self.mid_step = start, mid, end, mid_step


def _call(body, args, *, name, grid, in_specs, out_specs, out_shape, scratch_shapes=(), sem=None, comm=None):
    scratch_shapes = list(scratch_shapes)
    if comm is None:
        outs = pl.pallas_call(
            body, name=name, grid=grid, in_specs=in_specs, out_specs=out_specs, out_shape=out_shape,
            scratch_shapes=scratch_shapes, compiler_params=_cparams(sem))(*args)
        return list(outs), []

    n_in, n_out, n_sc = len(args), len(out_shape), len(scratch_shapes)
    n_ci, n_co = len(comm.arrays), len(comm.out_shape)
    n_steps = 1
    for g in grid:
        n_steps *= g

    def hosted(*refs):
        ins, refs = refs[:n_in], refs[n_in:]
        c_ins, refs = refs[:n_ci], refs[n_ci:]
        outs, refs = refs[:n_out], refs[n_out:]
        c_outs, refs = refs[:n_co], refs[n_co:]
        scratch, c_sems = refs[:n_sc], refs[n_sc:]
        step = 0
        for i, g in enumerate(grid):
            step = step * g + pl.program_id(i)

        @pl.when(step == 0)
        def _():
            comm.start(c_ins, c_outs, c_sems)

        body(*ins, *outs, *scratch)

        if comm.mid is not None:
            @pl.when(step == comm.mid_step)
            def _():
                comm.mid(c_ins, c_outs, c_sems)

        @pl.when(step == n_steps - 1)
        def _():
            comm.end(c_ins, c_outs, c_sems)

    outs = pl.pallas_call(
        hosted, name=name, grid=grid,
        in_specs=list(in_specs) + [ANY] * n_ci, out_specs=list(out_specs) + [ANY] * n_co,
        out_shape=list(out_shape) + comm.out_shape, scratch_shapes=scratch_shapes + comm.scratch,
        compiler_params=_cparams(("arbitrary",) * len(grid)))(*args, *comm.arrays)
    return list(outs[:n_out]), list(outs[n_out:])


def _mm_nn(a, wb, out_dtype, name, comm=None):
    M, K = a.shape
    NB, K2, Nb = wb.shape
    assert K == K2
    tm = min(M, 1024)
    tk = min(K, 2048)
    tn = _pick(Nb, (512, 1408, 256))
    nk = K // tk
    nn = Nb // tn

    def body(a_ref, b_ref, o_ref, *acc):
        part = jnp.dot(a_ref[...], b_ref[...], preferred_element_type=F32)
        if nk == 1:
            o_ref[...] = part.astype(out_dtype)
        else:
            acc_ref, = acc
            k = pl.program_id(3)

            @pl.when(k == 0)
            def _():
                acc_ref[...] = part

            @pl.when(k > 0)
            def _():
                acc_ref[...] += part

            @pl.when(k == nk - 1)
            def _():
                o_ref[...] = acc_ref[...].astype(out_dtype)

    grid = (M // tm, NB, nn, nk)
    if comm is not None and comm.mid is not None:
        comm.mid_step = (3 * grid[0] * grid[1] * grid[2] * grid[3]) // 4
    (out,), got = _call(
        body, (a, wb), name=name, grid=grid,
        in_specs=[pl.BlockSpec((tm, tk), lambda m, j, n, k: (m, k)),
                  pl.BlockSpec((None, tk, tn), lambda m, j, n, k: (j, k, n))],
        out_specs=[pl.BlockSpec((tm, tn), lambda m, j, n, k: (m, j * nn + n))],
        out_shape=[jax.ShapeDtypeStruct((M, NB * Nb), out_dtype)],
        scratch_shapes=[] if nk == 1 else [pltpu.VMEM((tm, tn), F32)],
        sem=("parallel", "parallel", "parallel", "arbitrary"), comm=comm)
    return (out, got) if comm is not None else out


def _mm_nt(a, wb, out_dtype, name, comm=None, token=None):
    M, N = a.shape
    NB, K, Nb = wb.shape
    assert N == NB * Nb
    tm = min(M, 1024)
    tko = _pick(K, (1024,))
    tc = _pick(Nb, (1024, 1408, 256))
    nc = Nb // tc
    nsteps = NB * nc

    def body(a_ref, b_ref, *refs):
        o_ref, acc_ref = refs[-2:]
        step = pl.program_id(2) * nc + pl.program_id(3)
        part = lax.dot_general(a_ref[...], b_ref[...], (((1,), (1,)), ((), ())), preferred_element_type=F32)

        @pl.when(step == 0)
        def _():
            acc_ref[...] = part

        @pl.when(step > 0)
        def _():
            acc_ref[...] += part

        @pl.when(step == nsteps - 1)
        def _():
            o_ref[...] = acc_ref[...].astype(out_dtype)

    order_after = [] if token is None else [token]
    (out,), landed = _call(
        body, (a, wb, *order_after), name=name,
        grid=(M // tm, K // tko, NB, nc),
        in_specs=[pl.BlockSpec((tm, tc), lambda m, ko, j, c: (m, j * nc + c)),
                  pl.BlockSpec((None, tko, tc), lambda m, ko, j, c: (j, ko, c))]
        + [pl.BlockSpec((8, 128), lambda m, ko, j, c: (0, 0))] * len(order_after),
        out_specs=[pl.BlockSpec((tm, tko), lambda m, ko, j, c: (m, ko))],
        out_shape=[jax.ShapeDtypeStruct((M, K), out_dtype)],
        scratch_shapes=[pltpu.VMEM((tm, tko), F32)],
        sem=("parallel", "parallel", "arbitrary", "arbitrary"), comm=comm)
    return out, landed


def _mm_tn(a, g, nb, out_dtype, name):
    M, Ka = a.shape
    M2, N = g.shape
    assert M == M2 and N % nb == 0
    Nb = N // nb
    tka = _pick(Ka, (1024,))
    tn = _pick(Nb, (512, 1408, 256))
    nn = Nb // tn

    def body(a_ref, g_ref, o_ref):
        o_ref[...] = lax.dot_general(a_ref[...], g_ref[...], (((0,), (0,)), ((), ())),
                                     preferred_element_type=F32).astype(out_dtype)

    return pl.pallas_call(
        body, name=name,
        grid=(Ka // tka, nb, nn),
        in_specs=[pl.BlockSpec((M, tka), lambda ka, j, n: (0, ka)),
                  pl.BlockSpec((M, tn), lambda ka, j, n: (0, j * nn + n))],
        out_specs=pl.BlockSpec((None, tka, tn), lambda ka, j, n: (j, ka, n)),
        out_shape=jax.ShapeDtypeStruct((nb, Ka, Nb), out_dtype),
        compiler_params=_cparams(("parallel", "parallel", "parallel")),
    )(a, g)


ROW_TILE = 256


def _rms_fwd(x, w, name):
    T, Dm = x.shape

    def body(x_ref, w_ref, u_ref):
        xv = x_ref[...]
        r = lax.rsqrt(jnp.mean(xv * xv, axis=-1, keepdims=True) + EPS)
        u_ref[...] = (xv * r * w_ref[...]).astype(BF16)

    return pl.pallas_call(
        body, name=name, grid=(T // ROW_TILE,),
        in_specs=[pl.BlockSpec((ROW_TILE, Dm), lambda i: (i, 0)), pl.BlockSpec((1, Dm), lambda i: (0, 0))],
        out_specs=pl.BlockSpec((ROW_TILE, Dm), lambda i: (i, 0)),
        out_shape=jax.ShapeDtypeStruct((T, Dm), BF16),
        compiler_params=_cparams(("parallel",)),
    )(x, w)


def _resid_rms_fwd(x, mix, w, name):
    T, Dm = x.shape

    def body(x_ref, m_ref, w_ref, h_ref, u_ref):
        h = x_ref[...] + m_ref[...]
        h_ref[...] = h
        r = lax.rsqrt(jnp.mean(h * h, axis=-1, keepdims=True) + EPS)
        u_ref[...] = (h * r * w_ref[...]).astype(BF16)

    row = pl.BlockSpec((ROW_TILE, Dm), lambda i: (i, 0))
    return pl.pallas_call(
        body, name=name, grid=(T // ROW_TILE,),
        in_specs=[row, row, pl.BlockSpec((1, Dm), lambda i: (0, 0))],
        out_specs=[row, row],
        out_shape=[jax.ShapeDtypeStruct((T, Dm), F32), jax.ShapeDtypeStruct((T, Dm), BF16)],
        compiler_params=_cparams(("parallel",)),
    )(x, mix, w)


def _loss_head(h1, mlp, wf, target, name):
    T, Dm = h1.shape

    def body(h_ref, m_ref, w_ref, t_ref, loss_ref, dh_ref, dhb_ref, dw_ref):
        i = pl.program_id(0)
        h = h_ref[...] + m_ref[...]
        r = lax.rsqrt(jnp.mean(h * h, axis=-1, keepdims=True) + EPS)
        xh = h * r
        wv = w_ref[...]
        e = xh * wv - t_ref[...]
        part = 0.5 * jnp.sum(jnp.mean(e * e, axis=-1, keepdims=True), axis=0, keepdims=True)
        dy = e * (1.0 / Dm)
        dw = jnp.sum(dy * xh, axis=0, keepdims=True)
        gy = dy * wv
        dh = r * (gy - xh * jnp.mean(gy * xh, axis=-1, keepdims=True))
        dh_ref[...] = dh
        dhb_ref[...] = dh.astype(BF16)

        @pl.when(i == 0)
        def _():
            loss_ref[...] = jnp.zeros_like(loss_ref)
            dw_ref[...] = jnp.zeros_like(dw_ref)

        loss_ref[...] += jnp.broadcast_to(part, loss_ref.shape)
        dw_ref[...] += dw

    row = pl.BlockSpec((ROW_TILE, Dm), lambda i: (i, 0))
    vec = pl.BlockSpec((1, Dm), lambda i: (0, 0))
    return pl.pallas_call(
        body, name=name, grid=(T // ROW_TILE,),
        in_specs=[row, row, vec, row],
        out_specs=[pl.BlockSpec((8, 128), lambda i: (0, 0)), row, row, vec],
        out_shape=[jax.ShapeDtypeStruct((8, 128), F32), jax.ShapeDtypeStruct((T, Dm), F32),
                   jax.ShapeDtypeStruct((T, Dm), BF16), jax.ShapeDtypeStruct((1, Dm), F32)],
        compiler_params=_cparams(("arbitrary",)),
    )(h1, mlp, wf, target)


def _rms_bwd(dyn, x, w, dres, name):
    T, Dm = x.shape

    def body(g_ref, x_ref, w_ref, r_ref, dx_ref, dxb_ref, dw_ref):
        i = pl.program_id(0)
        xv = x_ref[...]
        r = lax.rsqrt(jnp.mean(xv * xv, axis=-1, keepdims=True) + EPS)
        xh = xv * r
        g = g_ref[...]
        dw = jnp.sum(g * xh, axis=0, keepdims=True)
        gy = g * w_ref[...]
        dx = r_ref[...] + r * (gy - xh * jnp.mean(gy * xh, axis=-1, keepdims=True))
        dx_ref[...] = dx
        dxb_ref[...] = dx.astype(BF16)

        @pl.when(i == 0)
        def _():
            dw_ref[...] = jnp.zeros_like(dw_ref)

        dw_ref[...] += dw

    row = pl.BlockSpec((ROW_TILE, Dm), lambda i: (i, 0))
    vec = pl.BlockSpec((1, Dm), lambda i: (0, 0))
    return pl.pallas_call(
        body, name=name, grid=(T // ROW_TILE,),
        in_specs=[row, row, vec, row],
        out_specs=[row, row, vec],
        out_shape=[jax.ShapeDtypeStruct((T, Dm), F32), jax.ShapeDtypeStruct((T, Dm), BF16),
                   jax.ShapeDtypeStruct((1, Dm), F32)],
        compiler_params=_cparams(("arbitrary",)),
    )(dyn, x, w, dres)


COL_TILE = 2048


def _relu2_fwd(a, name):
    T, N = a.shape

    def body(a_ref, r_ref):
        ra = jnp.maximum(a_ref[...], 0.0)
        r_ref[...] = (ra * ra).astype(BF16)

    blk = pl.BlockSpec((ROW_TILE, COL_TILE), lambda i, j: (i, j))
    return pl.pallas_call(
        body, name=name, grid=(T // ROW_TILE, N // COL_TILE), in_specs=[blk], out_specs=blk,
        out_shape=jax.ShapeDtypeStruct((T, N), BF16),
        compiler_params=_cparams(("parallel", "parallel")),
    )(a)


def _relu2_bwd(dr, a, name):
    T, N = a.shape

    def body(dr_ref, a_ref, da_ref):
        da_ref[...] = (dr_ref[...] * (2.0 * jnp.maximum(a_ref[...], 0.0))).astype(BF16)

    blk = pl.BlockSpec((ROW_TILE, COL_TILE), lambda i, j: (i, j))
    return pl.pallas_call(
        body, name=name, grid=(T // ROW_TILE, N // COL_TILE), in_specs=[blk, blk], out_specs=blk,
        out_shape=jax.ShapeDtypeStruct((T, N), BF16),
        compiler_params=_cparams(("parallel", "parallel")),
    )(dr, a)


GATE_TILE = 1024


def _merge_fwd(z, pa, pb, name):
    T, Dm = pa.shape

    def body(za_ref, zb_ref, pa_ref, pb_ref, m_ref):
        m_ref[...] = (_sigmoid(za_ref[...]) * pa_ref[...] + _sigmoid(zb_ref[...]) * pb_ref[...]).astype(BF16)

    blk = pl.BlockSpec((ROW_TILE, GATE_TILE), lambda i, j: (i, j))
    return pl.pallas_call(
        body, name=name, grid=(T // ROW_TILE, Dm // GATE_TILE),
        in_specs=[pl.BlockSpec((ROW_TILE, GATE_TILE), lambda i, j: (i, COL_GATE_A + j)),
                  pl.BlockSpec((ROW_TILE, GATE_TILE), lambda i, j: (i, COL_GATE_B + j)), blk, blk],
        out_specs=blk,
        out_shape=jax.ShapeDtypeStruct((T, Dm), BF16),
        compiler_params=_cparams(("parallel", "parallel")),
    )(z, z, pa, pb)


def _merge_bwd(dm, z, pa, pb, name):
    T, Dm = pa.shape

    def body(dm_ref, za_ref, zb_ref, pa_ref, pb_ref, dpa_ref, dpb_ref, dga_ref, dgb_ref):
        d = dm_ref[...]
        ga = _sigmoid(za_ref[...])
        gb = _sigmoid(zb_ref[...])
        dpa_ref[...] = (d * ga).astype(BF16)
        dpb_ref[...] = (d * gb).astype(BF16)
        dga_ref[...] = (d * pa_ref[...] * ga * (1.0 - ga)).astype(BF16)
        dgb_ref[...] = (d * pb_ref[...] * gb * (1.0 - gb)).astype(BF16)

    blk = pl.BlockSpec((ROW_TILE, GATE_TILE), lambda i, j: (i, j))
    out = jax.ShapeDtypeStruct((T, Dm), BF16)
    return pl.pallas_call(
        body, name=name, grid=(T // ROW_TILE, Dm // GATE_TILE),
        in_specs=[blk, pl.BlockSpec((ROW_TILE, GATE_TILE), lambda i, j: (i, COL_GATE_A + j)),
                  pl.BlockSpec((ROW_TILE, GATE_TILE), lambda i, j: (i, COL_GATE_B + j)), blk, blk],
        out_specs=[blk, blk, blk, blk],
        out_shape=[out, out, out, out],
        compiler_params=_cparams(("parallel", "parallel")),
    )(dm, z, z, pa, pb)


def _dot_hi(a, b, dims):
    return lax.dot_general(a, b, (dims, ((), ())), precision=HIGHEST, preferred_element_type=F32)


NN = ((1,), (0,))
NT = ((1,), (1,))
TN = ((0,), (0,))


def _hg_gates(hq, hf, lb):
    sq = _sigmoid(hq)
    q = hq * sq * (HG_DK ** -0.5)
    f = _sigmoid(hf)
    g = lb + (1.0 - lb) * f
    return q, sq, f, g, jnp.log(g), 1.0 - g


def _tri(lower):
    r = lax.broadcasted_iota(jnp.int32, (CHUNK, CHUNK), 0)
    c = lax.broadcasted_iota(jnp.int32, (CHUNK, CHUNK), 1)
    return jnp.where((r >= c) if lower else (r <= c), 1.0, 0.0).astype(F32)


def _hgrn2_fwd(z, lb_logits, hg_norm_w, name, comm=None):
    T = z.shape[0]
    n_chunks = T // CHUNK

    def body(hq_ref, hf_ref, hi_ref, hg_ref, lbl_ref, nw_ref, o_ref, ya_ref, sall_ref, st_ref):
        lbl = lbl_ref[...]
        lb = 1.0 / (1.0 + jnp.exp(lbl[1:2, :] - lbl[0:1, :]))
        st_ref[...] = jnp.zeros_like(st_ref)
        tri = _tri(True)
        row8 = lax.broadcasted_iota(jnp.int32, (8, HG_DK), 0)

        def chunk(c, carry):
            rows = pl.ds(pl.multiple_of(c * CHUNK, CHUNK), CHUNK)
            q, _, _, _, lg, kk = _hg_gates(hq_ref[rows, :], hf_ref[rows, :], lb)
            v = hi_ref[rows, :]
            b = _dot_hi(tri, lg, NN)
            st = st_ref[...]
            sall_ref[c] = st
            o_inter = _dot_hi(q * jnp.exp(b), st, NT)
            for g8 in range(CHUNK // 8):
                n = 8 * (g8 + 1)
                bs, ks, vs = b[:n], kk[:n], v[:n]
                sidx = lax.broadcasted_iota(jnp.int32, (n, HG_DK), 0)
                blk = o_inter[8 * g8:n]
                for i in range(8):
                    t = 8 * g8 + i
                    e = jnp.where(sidx <= t, jnp.exp(b[t:t + 1] - bs), 0.0)
                    p = jnp.sum(e * ks * q[t:t + 1], axis=1, keepdims=True)
                    ot = jnp.sum(p * vs, axis=0, keepdims=True)
                    blk = blk + jnp.where(row8 == i, ot, 0.0)
                o_ref[pl.ds(pl.multiple_of(c * CHUNK + 8 * g8, 8), 8), :] = blk
            bl = b[CHUNK - 1:CHUNK]
            ke = kk * jnp.exp(bl - b)
            st_ref[...] = st * jnp.exp(bl) + _dot_hi(v, ke, TN)
            return carry

        lax.fori_loop(0, n_chunks, chunk, 0)
        o = o_ref[...]
        r = lax.rsqrt(jnp.mean(o * o, axis=-1, keepdims=True) + EPS)
        hg = hg_ref[...]
        ya_ref[...] = (o * r * nw_ref[...] * (hg * _sigmoid(hg))).astype(BF16)

    def col(base):
        return pl.BlockSpec((T, HG_DK), lambda h: (0, base + h))

    return _call(
        body, (z, z, z, z, lb_logits, hg_norm_w), name=name, grid=(HG_HEADS,),
        in_specs=[col(COL_HQ), col(COL_HF), col(COL_HI), col(COL_HG),
                  pl.BlockSpec((2, HG_DK), lambda h: (0, h)), pl.BlockSpec((1, HG_DK), lambda h: (0, 0))],
        out_specs=[col(0), col(0), pl.BlockSpec((None, n_chunks, HG_DK, HG_DK), lambda h: (h, 0, 0, 0))],
        out_shape=[jax.ShapeDtypeStruct((T, HG_WIDTH), F32), jax.ShapeDtypeStruct((T, HG_WIDTH), BF16),
                   jax.ShapeDtypeStruct((HG_HEADS, n_chunks, HG_DK, HG_DK), F32)],
        scratch_shapes=[pltpu.VMEM((HG_DK, HG_DK), F32)],
        sem=("parallel",), comm=comm)


def _hgrn2_bwd(z, lb_logits, hg_norm_w, o_raw, s_all, dya, name, comm=None):
    T = z.shape[0]
    n_chunks = T // CHUNK

    def body(hq_ref, hf_ref, hi_ref, hg_ref, lbl_ref, nw_ref, o_ref, sall_ref, dya_ref,
             dhq_ref, dhf_ref, dhi_ref, dhg_ref, dlbl_ref, dnw_ref,
             do_ref, dst_ref, dq_ref, dk_ref, dv_ref, dlb_ref):
        h = pl.program_id(0)
        lbl = lbl_ref[...]
        lb = 1.0 / (1.0 + jnp.exp(lbl[1:2, :] - lbl[0:1, :]))

        o = o_ref[...]
        r = lax.rsqrt(jnp.mean(o * o, axis=-1, keepdims=True) + EPS)
        oh = o * r
        nw = nw_ref[...]
        hg = hg_ref[...]
        sg = _sigmoid(hg)
        dy = dya_ref[...]
        d_on = dy * (hg * sg)
        dhg_ref[...] = (dy * (oh * nw) * (sg * (1.0 + hg * (1.0 - sg)))).astype(BF16)
        dnw = jnp.sum(d_on * oh, axis=0, keepdims=True)
        gy = d_on * nw
        do_ref[...] = r * (gy - oh * jnp.mean(gy * oh, axis=-1, keepdims=True))

        @pl.when(h == 0)
        def _():
            dnw_ref[...] = jnp.zeros_like(dnw_ref)

        dnw_ref[...] += jnp.broadcast_to(dnw, dnw_ref.shape)

        dst_ref[...] = jnp.zeros_like(dst_ref)
        dlb_ref[...] = jnp.zeros_like(dlb_ref)
        tri = _tri(True)
        tri_t = _tri(False)
        row8 = lax.broadcasted_iota(jnp.int32, (8, HG_DK), 0)

        def chunk(ci, carry):
            c = n_chunks - 1 - ci
            rows = pl.ds(pl.multiple_of(c * CHUNK, CHUNK), CHUNK)
            hq = hq_ref[rows, :]
            q, sq, f, g, lg, kk = _hg_gates(hq, hf_ref[rows, :], lb)
            v = hi_ref[rows, :]
            do = do_ref[rows, :]
            b = _dot_hi(tri, lg, NN)
            eb = jnp.exp(b)
            bl = b[CHUNK - 1:CHUNK]
            ebl = jnp.exp(bl)
            ekb = jnp.exp(bl - b)
            qe = q * eb
            ke = kk * ekb
            st = sall_ref[c]
            dst = dst_ref[...]
            dqe = _dot_hi(do, st, NN)
            dke = _dot_hi(v, dst, NN)
            dv_inter = _dot_hi(ke, dst, NT)
            d_ebl = jnp.sum(st * dst, axis=0, keepdims=True)
            dst_ref[...] = dst * ebl + _dot_hi(do, qe, TN)

            dk_ref[...] = jnp.zeros_like(dk_ref)
            dv_ref[...] = jnp.zeros_like(dv_ref)
            for g8 in range(CHUNK // 8):
                n = 8 * (g8 + 1)
                bs, ks, vs = b[:n], kk[:n], v[:n]
                sidx = lax.broadcasted_iota(jnp.int32, (n, HG_DK), 0)
                blk = jnp.zeros((8, HG_DK), F32)
                for i in range(8):
                    t = 8 * g8 + i
                    qt = q[t:t + 1]
                    dot_ = do[t:t + 1]
                    e = jnp.where(sidx <= t, jnp.exp(b[t:t + 1] - bs), 0.0)
                    w = e * ks
                    p = jnp.sum(w * qt, axis=1, keepdims=True)
                    dsc = jnp.sum(vs * dot_, axis=1, keepdims=True)
                    dqt = jnp.sum(dsc * w, axis=0, keepdims=True)
                    blk = blk + jnp.where(row8 == i, dqt, 0.0)
                    dk_ref[0:n, :] += dsc * e * qt
                    dv_ref[0:n, :] += p * dot_
                dq_ref[8 * g8:n, :] = blk
            dq_i = dq_ref[...]
            dk_i = dk_ref[...]
            dke_ke = dke * ke
            db = q * dq_i - kk * dk_i + dqe * qe - dke_ke
            db_last = jnp.sum(dke_ke, axis=0, keepdims=True) + d_ebl * ebl
            dlg = _dot_hi(tri_t, db, NN) + db_last
            dq = dq_i + dqe * eb
            dkk = dk_i + dke * ekb
            dg = dlg / g - dkk
            dhq_ref[rows, :] = (dq * (HG_DK ** -0.5) * (sq * (1.0 + hq * (1.0 - sq)))).astype(BF16)
            dhf_ref[rows, :] = (dg * (1.0 - lb) * f * (1.0 - f)).astype(BF16)
            dhi_ref[rows, :] = (dv_ref[...] + dv_inter).astype(BF16)
            dlb_ref[...] += jnp.sum(dg * (1.0 - f), axis=0, keepdims=True)
            return carry

        lax.fori_loop(0, n_chunks, chunk, 0)
        dl0 = dlb_ref[...] * lb * (1.0 - lb)
        dlbl_ref[0:1, :] = dl0
        dlbl_ref[1:2, :] = -dl0

    def col(base):
        return pl.BlockSpec((T, HG_DK), lambda h: (0, base + h))

    outb = jax.ShapeDtypeStruct((T, HG_WIDTH), BF16)
    return _call(
        body, (z, z, z, z, lb_logits, hg_norm_w, o_raw, s_all, dya), name=name, grid=(HG_HEADS,),
        in_specs=[col(COL_HQ), col(COL_HF), col(COL_HI), col(COL_HG),
                  pl.BlockSpec((2, HG_DK), lambda h: (0, h)), pl.BlockSpec((1, HG_DK), lambda h: (0, 0)),
                  col(0), pl.BlockSpec((None, n_chunks, HG_DK, HG_DK), lambda h: (h, 0, 0, 0)), col(0)],
        out_specs=[col(0), col(0), col(0), col(0), pl.BlockSpec((2, HG_DK), lambda h: (0, h)),
                   pl.BlockSpec((8, HG_DK), lambda h: (0, 0))],
        out_shape=[outb, outb, outb, outb, jax.ShapeDtypeStruct((2, HG_WIDTH), F32),
                   jax.ShapeDtypeStruct((8, HG_DK), F32)],
        scratch_shapes=[pltpu.VMEM((T, HG_DK), F32), pltpu.VMEM((HG_DK, HG_DK), F32),
                        pltpu.VMEM((CHUNK, HG_DK), F32), pltpu.VMEM((CHUNK, HG_DK), F32),
                        pltpu.VMEM((CHUNK, HG_DK), F32), pltpu.VMEM((1, HG_DK), F32)],
        sem=("arbitrary",), comm=comm)


CONST_KEYS = PAD - REL_CLIP
VAR_KEYS = BAND - CONST_KEYS
REL_LO = 128
REL_SPAN = N_REL_PAD - REL_LO


def _rel_onehot(t):
    r = lax.broadcasted_iota(jnp.int32, (REL_SPAN, VAR_KEYS), 0)
    j = lax.broadcasted_iota(jnp.int32, (REL_SPAN, VAR_KEYS), 1)
    idx = jnp.clip(t + PAD - CONST_KEYS - j, -REL_CLIP, REL_CLIP) + REL_CLIP - REL_LO
    return jnp.where(r == idx, 1.0, 0.0).astype(BF16)


def _split3(x):
    hi = x.astype(BF16)
    r1 = x - hi.astype(F32)
    mid = r1.astype(BF16)
    return hi, mid, (r1 - mid.astype(F32)).astype(BF16)


def _bias_expand(rel, name):
    def body(rel_ref, out_ref):
        tab = rel_ref[...]
        onehot = _rel_onehot(pl.program_id(0))
        out_ref[:, 0:CONST_KEYS] = jnp.broadcast_to(tab[:, 2 * REL_CLIP:2 * REL_CLIP + 1], (AT_HEADS, CONST_KEYS))
        out_ref[:, CONST_KEYS:BAND] = sum(
            jnp.dot(piece, onehot, preferred_element_type=F32) for piece in _split3(tab[:, REL_LO:N_REL_PAD]))

    return pl.pallas_call(
        body, name=name, grid=(CHUNK,),
        in_specs=[pl.BlockSpec((AT_HEADS, N_REL_PAD), lambda t: (0, 0))],
        out_specs=pl.BlockSpec((None, AT_HEADS, BAND), lambda t: (t, 0, 0)),
        out_shape=jax.ShapeDtypeStruct((CHUNK, AT_HEADS, BAND), F32),
        compiler_params=_cparams(("parallel",)),
    )(rel)


def _bias_reduce(dbias_t, name):
    def body(db_ref, out_ref):
        t = pl.program_id(0)

        @pl.when(t == 0)
        def _():
            out_ref[...] = jnp.zeros_like(out_ref)

        db = db_ref[...]
        onehot = _rel_onehot(t)
        acc = sum(lax.dot_general(piece, onehot, (NT, ((), ())), preferred_element_type=F32)
                  for piece in _split3(db[:, CONST_KEYS:BAND]))
        lane = lax.broadcasted_iota(jnp.int32, (AT_HEADS, REL_SPAN), 1)
        last = jnp.sum(db[:, 0:CONST_KEYS], axis=1, keepdims=True)
        out_ref[:, REL_LO:N_REL_PAD] += acc + jnp.where(lane == 2 * REL_CLIP - REL_LO, last, 0.0)

    return pl.pallas_call(
        body, name=name, grid=(CHUNK,),
        in_specs=[pl.BlockSpec((None, AT_HEADS, BAND), lambda t: (t, 0, 0))],
        out_specs=pl.BlockSpec((AT_HEADS, N_REL_PAD), lambda t: (0, 0)),
        out_shape=jax.ShapeDtypeStruct((AT_HEADS, N_REL_PAD), F32),
        compiler_params=_cparams(("arbitrary",)),
    )(dbias_t)


def _pair_lanes():
    return lax.broadcasted_iota(jnp.int32, (CHUNK, 2 * AT_DH), 1) < AT_DH


def _block_diag(a):
    first = _pair_lanes()
    return jnp.concatenate([jnp.where(first, a, 0.0), jnp.where(first, 0.0, a)], axis=0).astype(BF16)


def _diag_blocks(a):
    return jnp.where(_pair_lanes(), a[:CHUNK], a[CHUNK:])


def _band_probs_t(kb, qbd, bias_t, c):
    s = lax.dot_general(kb, qbd, (NT, ((), ())), preferred_element_type=F32) * (AT_DH ** -0.5) + bias_t
    j = lax.broadcasted_iota(jnp.int32, (BAND, 2 * AT_DH), 0)
    s = jnp.where(j + c * CHUNK >= PAD, s, -jnp.inf)
    p = jnp.exp(s - jnp.max(s, axis=0, keepdims=True))
    return p / jnp.sum(p, axis=0, keepdims=True)


def _fill_padded(dst_ref, src_ref, T):
    dst_ref[0:PAD, :] = jnp.zeros((PAD, 2 * AT_DH), BF16)
    dst_ref[PAD:PAD + T, :] = src_ref[...].astype(BF16)


def _attn_fwd(z, bias_t, name, comm=None):
    T = z.shape[0]
    n_chunks = T // CHUNK

    def body(q_ref, k_ref, v_ref, bias_ref, y_ref, kp_ref, vp_ref):
        _fill_padded(kp_ref, k_ref, T)
        _fill_padded(vp_ref, v_ref, T)

        def chunk(c, carry):
            rows = pl.ds(pl.multiple_of(c * CHUNK, CHUNK), CHUNK)
            band = pl.ds(pl.multiple_of(c * CHUNK, CHUNK), BAND)
            p = _band_probs_t(kp_ref[band, :], _block_diag(q_ref[rows, :]), bias_ref[...], c)
            o2 = lax.dot_general(p.astype(BF16), vp_ref[band, :], (TN, ((), ())), preferred_element_type=F32)
            y_ref[rows, :] = _diag_blocks(o2).astype(BF16)
            return carry

        lax.fori_loop(0, n_chunks, chunk, 0, unroll=2)

    def col(base):
        return pl.BlockSpec((T, 128), lambda h: (0, base + h))

    return _call(
        body, (z, z, z, bias_t), name=name, grid=(AT_HEADS // 2,),
        in_specs=[col(COL_AQ), col(COL_AK), col(COL_AV), pl.BlockSpec((None, BAND, 128), lambda h: (h, 0, 0))],
        out_specs=[col(0)],
        out_shape=[jax.ShapeDtypeStruct((T, AT_WIDTH), BF16)],
        scratch_shapes=[pltpu.VMEM((PAD + T, 128), BF16), pltpu.VMEM((PAD + T, 128), BF16)],
        sem=("parallel",), comm=comm)


def _attn_bwd(z, bias_t, dyb, name, comm=None):
    T = z.shape[0]
    n_chunks = T // CHUNK

    def body(q_ref, k_ref, v_ref, bias_ref, dy_ref, dq_ref, dk_ref, dv_ref, dbias_ref,
             kp_ref, vp_ref, dkp_ref, dvp_ref):
        _fill_padded(kp_ref, k_ref, T)
        _fill_padded(vp_ref, v_ref, T)
        dkp_ref[...] = jnp.zeros_like(dkp_ref)
        dvp_ref[...] = jnp.zeros_like(dvp_ref)
        dbias_ref[...] = jnp.zeros_like(dbias_ref)

        def chunk(c, carry):
            rows = pl.ds(pl.multiple_of(c * CHUNK, CHUNK), CHUNK)
            band = pl.ds(pl.multiple_of(c * CHUNK, CHUNK), BAND)
            qbd = _block_diag(q_ref[rows, :])
            dobd = _block_diag(dy_ref[rows, :])
            kb = kp_ref[band, :]
            vb = vp_ref[band, :]
            p = _band_probs_t(kb, qbd, bias_ref[...], c)
            dp = lax.dot_general(vb, dobd, (NT, ((), ())), preferred_element_type=F32)
            ds = p * (dp - jnp.sum(dp * p, axis=0, keepdims=True))
            dbias_ref[...] += ds
            dsb = ds.astype(BF16)
            dq2 = lax.dot_general(dsb, kb, (TN, ((), ())), preferred_element_type=F32)
            dq_ref[rows, :] = (_diag_blocks(dq2) * (AT_DH ** -0.5)).astype(BF16)
            dkp_ref[band, :] += jnp.dot(dsb, qbd, preferred_element_type=F32) * (AT_DH ** -0.5)
            dvp_ref[band, :] += jnp.dot(p.astype(BF16), dobd, preferred_element_type=F32)
            return carry

        lax.fori_loop(0, n_chunks, chunk, 0)
        dk_ref[...] = dkp_ref[PAD:PAD + T, :].astype(BF16)
        dv_ref[...] = dvp_ref[PAD:PAD + T, :].astype(BF16)

    def col(base):
        return pl.BlockSpec((T, 128), lambda h: (0, base + h))

    pair = pl.BlockSpec((None, BAND, 128), lambda h: (h, 0, 0))
    outb = jax.ShapeDtypeStruct((T, AT_WIDTH), BF16)
    return _call(
        body, (z, z, z, bias_t, dyb), name=name, grid=(AT_HEADS // 2,),
        in_specs=[col(COL_AQ), col(COL_AK), col(COL_AV), pair, col(0)],
        out_specs=[col(0), col(0), col(0), pair],
        out_shape=[outb, outb, outb, jax.ShapeDtypeStruct((AT_HEADS // 2, BAND, 128), F32)],
        scratch_shapes=[pltpu.VMEM((PAD + T, 128), BF16), pltpu.VMEM((PAD + T, 128), BF16),
                        pltpu.VMEM((PAD + T, 128), F32), pltpu.VMEM((PAD + T, 128), F32)],
        sem=("parallel",), comm=comm)


def _local_step(x, target, lb_logits, hg_norm_w, rel_bias, norm_mix_w, norm_mlp_w, norm_final_w,
                w_in, rest, exchanges=None):
    ex = exchanges
    rel = jnp.pad(rel_bias, ((0, 0), (0, N_REL_PAD - N_REL)))

    u = _rms_fwd(x, norm_mix_w, "rms_mix_fwd")
    if ex:
        z, w_in = _mm_gathered(u, w_in, ex.order, "mm_in_fwd")
    else:
        z = _mm_nn(u, w_in, F32, "mm_in_fwd")
    (o_raw, y_a, s_all), got_abou = _hgrn2_fwd(z, lb_logits, hg_norm_w, "hgrn2_fwd",
                                               comm=ex and ex.gather(rest[:4], mid_step=HG_HEADS - 2))
    bias_rows = _bias_expand(rel, "bias_expand")
    bias_t = jnp.transpose(bias_rows.reshape(CHUNK, AT_HEADS // 2, 2, BAND), (1, 3, 2, 0)).reshape(
        AT_HEADS // 2, BAND, 2 * CHUNK)
    (y_b,), got_down = _attn_fwd(z, bias_t, "attn_fwd",
                                 comm=ex and ex.gather(rest[4:], mid_step=AT_HEADS // 2 - 2))
    w_a, w_b, w_out, w_up, w_down = (got_abou + got_down) if ex else rest
    w_out1 = w_out.reshape(1, D_MODEL, D_MODEL)
    w_down1 = w_down.reshape(1, D_FF, D_MODEL)
    pa = _mm_nn(y_a, w_a, F32, "mm_a_fwd")
    pb = _mm_nn(y_b, w_b, F32, "mm_b_fwd")
    merged = _merge_fwd(z, pa, pb, "merge_fwd")
    mix = _mm_nn(merged, w_out1, F32, "mm_out_fwd")
    h1, u2 = _resid_rms_fwd(x, mix, norm_mlp_w, "rms_mlp_fwd")
    a = _mm_nn(u2, w_up, F32, "mm_up_fwd")
    r = _relu2_fwd(a, "relu2_fwd")
    mlp = _mm_nn(r, w_down1, F32, "mm_down_fwd")
    loss, dh2, dh2b, g_nf = _loss_head(h1, mlp, norm_final_w, target, "loss_head")

    g_down = _mm_tn(r, dh2b, 1, BF16, "mm_down_wgrad").reshape(N_DEV, D_FF // N_DEV, D_MODEL)
    dr, _ = _mm_nt(dh2b, w_down1, F32, "mm_down_dgrad")
    da = _relu2_bwd(dr, a, "relu2_bwd")
    g_up = _mm_tn(u2, da, N_DEV, BF16, "mm_up_wgrad")
    du2, _ = _mm_nt(da, w_up, F32, "mm_up_dgrad")
    dh1, dh1b, g_nmlp = _rms_bwd(du2, h1, norm_mlp_w, dh2, "rms_mlp_bwd")

    g_out = _mm_tn(merged, dh1b, 1, BF16, "mm_out_wgrad").reshape(N_DEV, D_MODEL // N_DEV, D_MODEL)
    dmerged, _ = _mm_nt(dh1b, w_out1, F32, "mm_out_dgrad")
    dpa, dpb, dga, dgb = _merge_bwd(dmerged, z, pa, pb, "merge_bwd")
    g_a = _mm_tn(y_a, dpa, N_DEV, BF16, "mm_a_wgrad")
    g_b = _mm_tn(y_b, dpb, N_DEV, BF16, "mm_b_wgrad")
    dya, _ = _mm_nt(dpa, w_a, F32, "mm_a_dgrad")
    dyb, _ = _mm_nt(dpb, w_b, F32, "mm_b_dgrad")
    early = [g_a, g_b, g_out, g_up, g_down]
    early_sums = ex.pair_sums(early, "early") if ex else None
    (daq, dak, dav, dbias_t), parts_down = _attn_bwd(
        z, bias_t, dyb, "attn_bwd", comm=ex and ex.scatter(early_sums[4:]))
    (dhq, dhf, dhi, dhg, g_lbl, g_hgw), parts_abou = _hgrn2_bwd(
        z, lb_logits, hg_norm_w, o_raw, s_all, dya, "hgrn2_bwd", comm=ex and ex.scatter(early_sums[:4]))
    early_parts = parts_abou + parts_down
    dbias_rows = jnp.transpose(dbias_t.reshape(AT_HEADS // 2, BAND, 2, CHUNK), (3, 0, 2, 1)).reshape(
        CHUNK, AT_HEADS, BAND)
    g_rel = _bias_reduce(dbias_rows, "bias_reduce")[:, :N_REL]
    dz = jnp.concatenate([dhq, dhf, dhi, dhg, daq, dak, dav, dga, dgb], axis=1)
    g_in = _mm_tn(u, dz, N_DEV, BF16, "mm_in_wgrad")
    if ex:
        late_parts, token = _scatter_start(ex.pair_sums([g_in], "late"), "rs_scatter_late")
        late_parts = [late_parts]
    else:
        token = None
    du, _ = _mm_nt(dz, w_in, F32, "mm_in_dgrad", token=token)
    grad_x, _, g_nmix = _rms_bwd(du, x, norm_mix_w, dh1, "rms_mix_bwd")

    small = dict(lb_logits=g_lbl, hg_norm_w=g_hgw[0:1], rel_bias=g_rel, norm_mix_w=g_nmix, norm_mlp_w=g_nmlp,
                 norm_final_w=g_nf)
    grads = (late_parts + early_parts) if ex else ([g_in] + early)
    return loss, grad_x, grads, small


def _gather_exchange(shards, mid_step=None):
    n = len(shards)

    def parts(ins, outs, sems):
        send_sems, recv_sems, local_sems = sems
        x, y, c = _position()
        chips = [(1 - x, y), (x, 1 - y), (1 - x, 1 - y)]

        def copy(w, k, block, to, src=None):
            dst = outs[w].at[4 * block[0] + 2 * block[1] + block[2]]
            return pltpu.make_async_remote_copy(
                src_ref=dst if src is None else src, dst_ref=dst,
                send_sem=send_sems.at[w, k], recv_sem=recv_sems.at[w, k], device_id=to, device_id_type=MESH)

        def local(w):
            return pltpu.make_async_copy(ins[w], outs[w].at[4 * x + 2 * y + c], local_sems.at[w])

        return (x, y, c), (x, y, 1 - c), chips, copy, local

    def start(ins, outs, sems):
        me, sibling, chips, copy, local = parts(ins, outs, sems)
        for w in range(n):
            local(w).start()
        for w in range(n):
            copy(w, 0, me, sibling, src=ins[w]).start()
            for j, chip in enumerate(chips):
                copy(w, 1 + j, me, (*chip, me[2]), src=ins[w]).start()

    def mid(ins, outs, sems):
        me, sibling, chips, copy, _ = parts(ins, outs, sems)
        for w in range(n):
            for j, chip in enumerate(chips):
                copy(w, 1 + j, (*chip, me[2]), me).wait_recv()
                copy(w, 4 + j, (*chip, me[2]), sibling).start()

    def end(ins, outs, sems):
        me, sibling, chips, copy, local = parts(ins, outs, sems)
        for w in range(n):
            copy(w, 0, sibling, me).wait_recv()
            for j, chip in enumerate(chips):
                copy(w, 4 + j, (*chip, sibling[2]), me).wait_recv()
        for w in range(n):
            for k in range(7):
                copy(w, k, me, sibling).wait_send()
            local(w).wait()

    return _Exchange(
        shards, [jax.ShapeDtypeStruct((N_DEV,) + s.shape, s.dtype) for s in shards],
        [pltpu.SemaphoreType.DMA((n, 7)), pltpu.SemaphoreType.DMA((n, 7)), pltpu.SemaphoreType.DMA((n,))],
        start, end, mid, mid_step)


def _mm_gathered(u, shard, order, name):
    T, K = u.shape
    _, Nb = shard.shape

    def body(order_ref, u_ref, shard_ref, z_ref, full_ref, wbuf, load_sem, send_sems, recv_sems, local_sem):
        s = pl.program_id(0)
        x, y, c = _position()
        me, sibling = (x, y, c), (x, y, 1 - c)
        chips = [(1 - x, y), (x, 1 - y), (1 - x, 1 - y)]

        def copy(k, block, to, src=None):
            dst = full_ref.at[4 * block[0] + 2 * block[1] + block[2]]
            return pltpu.make_async_remote_copy(
                src_ref=dst if src is None else src, dst_ref=dst,
                send_sem=send_sems.at[k], recv_sem=recv_sems.at[k], device_id=to, device_id_type=MESH)

        @pl.when(s == 0)
        def _():
            local = pltpu.make_async_copy(shard_ref, full_ref.at[4 * x + 2 * y + c], local_sem)
            local.start()
            copy(0, me, sibling, src=shard_ref).start()
            for j, chip in enumerate(chips):
                copy(1 + j, me, (*chip, c), src=shard_ref).start()
            local.wait()

        @pl.when(s == 1)
        def _():
            copy(0, sibling, me).wait_recv()

        for j, chip in enumerate(chips):
            direct, passed = ((2, 4), (3, 5), (6, 7))[j]

            @pl.when(s == direct)
            def _(j=j, chip=chip):
                copy(1 + j, (*chip, c), me).wait_recv()
                copy(4 + j, (*chip, c), sibling).start()

            @pl.when(s == passed)
            def _(j=j, chip=chip):
                copy(4 + j, (*chip, 1 - c), me).wait_recv()

        load = pltpu.make_async_copy(full_ref.at[order_ref[s]], wbuf, load_sem)
        load.start()
        load.wait()
        z_ref[...] = jnp.dot(u_ref[...], wbuf[...], preferred_element_type=F32)

        @pl.when(s == N_DEV - 1)
        def _():
            for k in range(7):
                copy(k, me, sibling).wait_send()

    return pl.pallas_call(
        body, name=name,
        grid_spec=pltpu.PrefetchScalarGridSpec(
            num_scalar_prefetch=1, grid=(N_DEV,),
            in_specs=[pl.BlockSpec((T, K), lambda s, order: (0, 0)), ANY],
            out_specs=[pl.BlockSpec((T, Nb), lambda s, order: (0, order[s])), ANY],
            scratch_shapes=[pltpu.VMEM((K, Nb), BF16), pltpu.SemaphoreType.DMA,
                            pltpu.SemaphoreType.DMA((7,)), pltpu.SemaphoreType.DMA((7,)), pltpu.SemaphoreType.DMA]),
        out_shape=[jax.ShapeDtypeStruct((T, N_DEV * Nb), F32), jax.ShapeDtypeStruct((N_DEV, K, Nb), BF16)],
        compiler_params=_cparams(("arbitrary",)),
    )(order, u, shard)


def _gather_order():
    x, y, c = _position()
    chips = [(1 - x, y), (x, 1 - y), (1 - x, 1 - y)]
    ids = [4 * x + 2 * y + c, 4 * x + 2 * y + (1 - c)]
    ids += [4 * cx + 2 * cy + c for cx, cy in chips[:2]] + [4 * cx + 2 * cy + (1 - c) for cx, cy in chips[:2]]
    ids += [4 * chips[2][0] + 2 * chips[2][1] + c, 4 * chips[2][0] + 2 * chips[2][1] + (1 - c)]
    return jnp.stack(ids).astype(jnp.int32)


def _run_exchange(comm, name):
    n_i, n_o = len(comm.arrays), len(comm.out_shape)

    def body(*refs):
        ins, outs, sems = refs[:n_i], refs[n_i:n_i + n_o], refs[n_i + n_o:]
        comm.start(ins, outs, sems)
        if comm.mid is not None:
            comm.mid(ins, outs, sems)
        comm.end(ins, outs, sems)

    return pl.pallas_call(
        body, name=name, in_specs=[ANY] * n_i, out_specs=[ANY] * n_o, out_shape=comm.out_shape,
        scratch_shapes=comm.scratch)(*comm.arrays)


def _exchange_sibling(grads, name):
    n = len(grads)

    def body(*refs):
        ins, outs = refs[:n], refs[n:2 * n]
        send_sems, recv_sems = refs[2 * n:]
        x, y, c = _position()
        copies = []
        for w in range(n):
            for s in range(N_CHIP):
                cp = pltpu.make_async_remote_copy(
                    src_ref=ins[w].at[2 * s + (1 - c)], dst_ref=outs[w].at[s],
                    send_sem=send_sems.at[w, s], recv_sem=recv_sems.at[w, s],
                    device_id=(x, y, 1 - c), device_id_type=MESH)
                cp.start()
                copies.append(cp)
        for cp in copies:
            cp.wait()

    return pl.pallas_call(
        body, name=name,
        in_specs=[ANY] * n, out_specs=[ANY] * n,
        out_shape=[jax.ShapeDtypeStruct((N_CHIP,) + g.shape[1:], g.dtype) for g in grads],
        scratch_shapes=[pltpu.SemaphoreType.DMA((n, N_CHIP)), pltpu.SemaphoreType.DMA((n, N_CHIP))],
    )(*grads)


def _pair_sum(g, land, parity, name):
    _, R, C = g.shape
    tr = _pick(R, (512, 256))

    def body(par_ref, g_ref, l_ref, o_ref):
        o_ref[...] = (g_ref[...].astype(F32) + l_ref[...].astype(F32)).astype(BF16)

    return pl.pallas_call(
        body, name=name,
        grid_spec=pltpu.PrefetchScalarGridSpec(
            num_scalar_prefetch=1, grid=(N_CHIP, R // tr),
            in_specs=[pl.BlockSpec((None, tr, C), lambda s, i, par: (2 * s + par[0], i, 0)),
                      pl.BlockSpec((None, tr, C), lambda s, i, par: (s, i, 0))],
            out_specs=pl.BlockSpec((None, tr, C), lambda s, i, par: (s, i, 0))),
        out_shape=jax.ShapeDtypeStruct((N_CHIP, R, C), BF16),
        compiler_params=_cparams(("parallel", "parallel")),
    )(parity, g, land)


def _scatter_exchange(partials):
    n = len(partials)

    def copies(ins, outs, sems):
        send_sems, recv_sems, local_sems = sems
        x, y, c = _position()
        chips = [(1 - x, y), (x, 1 - y), (1 - x, 1 - y)]
        my_slot = 2 * x + y
        local = [pltpu.make_async_copy(ins[w].at[my_slot], outs[w].at[my_slot], local_sems.at[w]) for w in range(n)]
        remote = [pltpu.make_async_remote_copy(
            src_ref=ins[w].at[2 * chip[0] + chip[1]], dst_ref=outs[w].at[my_slot],
            send_sem=send_sems.at[w, j], recv_sem=recv_sems.at[w, j], device_id=(*chip, c), device_id_type=MESH)
            for w in range(n) for j, chip in enumerate(chips)]
        return local, remote

    def start(ins, outs, sems):
        local, remote = copies(ins, outs, sems)
        for cp in local + remote:
            cp.start()

    def end(ins, outs, sems):
        local, remote = copies(ins, outs, sems)
        for cp in remote + local:
            cp.wait()

    return _Exchange(
        partials, [jax.ShapeDtypeStruct(p.shape, p.dtype) for p in partials],
        [pltpu.SemaphoreType.DMA((n, 3)), pltpu.SemaphoreType.DMA((n, 3)), pltpu.SemaphoreType.DMA((n,))],
        start, end)


HBM = pl.BlockSpec(memory_space=pltpu.HBM)
SEM = pl.BlockSpec(memory_space=pltpu.SEMAPHORE)
DATAFLOW = pltpu.SideEffectType.DATAFLOW_SIDE_EFFECTING


def _scatter_copies(ins, lands, send_sems, recv_sems):
    x, y, c = _position()
    chips = [(1 - x, y), (x, 1 - y), (1 - x, 1 - y)]
    return [pltpu.make_async_remote_copy(
        src_ref=ins[w].at[2 * chip[0] + chip[1]], dst_ref=lands[w].at[2 * x + y],
        send_sem=send_sems[3 * w + j], recv_sem=recv_sems[3 * w + j], device_id=(*chip, c), device_id_type=MESH)
        for w in range(len(ins)) for j, chip in enumerate(chips)]


def _scatter_start(partials, name):
    n = len(partials)

    def body(*refs):
        ins, lands = refs[:n], refs[n:2 * n]
        sems = refs[4 * n:10 * n]
        for cp in _scatter_copies(ins, lands, sems[:3 * n], sems[3 * n:]):
            cp.start()
        refs[-1][...] = jnp.zeros_like(refs[-1])

    def in_hbm(a):
        return pltpu.with_memory_space_constraint(a, pltpu.HBM)

    bufs = tuple(pltpu.HBM(p.shape, p.dtype) for p in partials)
    outs = pl.pallas_call(
        body, name=name,
        out_shape=bufs + bufs + (pltpu.SemaphoreType.DMA(()),) * (6 * n) + (jax.ShapeDtypeStruct((8, 128), F32),),
        in_specs=[HBM] * (2 * n),
        out_specs=(HBM,) * (2 * n) + (SEM,) * (6 * n) + (pl.BlockSpec(memory_space=pltpu.VMEM),),
        input_output_aliases={i: i for i in range(2 * n)},
        compiler_params=pltpu.CompilerParams(has_side_effects=DATAFLOW),
    )(*[in_hbm(p) for p in partials], *[in_hbm(lax.empty(p.shape, p.dtype)) for p in partials])
    return list(outs[:-1]), outs[-1]


def _scatter_wait(handle, after, name):
    n = len(handle) // 8
    bufs, sems = handle[:2 * n], handle[2 * n:]

    def body(*refs):
        ins, lands = refs[:n], refs[n:2 * n]
        sems = refs[2 * n:8 * n]
        for cp in _scatter_copies(ins, lands, sems[:3 * n], sems[3 * n:]):
            cp.wait_send()
            cp.wait_recv()

    outs = pl.pallas_call(
        body, name=name,
        out_shape=tuple(pltpu.HBM(b.shape, b.dtype) for b in bufs),
        in_specs=[HBM] * (2 * n) + [SEM] * (6 * n) + [ANY] * len(after), out_specs=(HBM,) * (2 * n),
        input_output_aliases={i: i for i in range(2 * n)},
        compiler_params=pltpu.CompilerParams(has_side_effects=DATAFLOW),
    )(*bufs, *sems, *after)
    return list(outs[:n]), list(outs[n:])


class _Exchanges:
    def __init__(self, parity, order):
        self.parity, self.order = parity, order

    def gather(self, shards, mid_step):
        return _gather_exchange(list(shards), mid_step)

    def pair_sums(self, grads, tag):
        landed = _exchange_sibling(list(grads), "rs_sibling_" + tag)
        return [_pair_sum(g, l, self.parity, "rs_pair_sum_%s_%d" % (tag, i))
                for i, (g, l) in enumerate(zip(grads, landed))]

    def scatter(self, partials):
        return _scatter_exchange(partials)


def _gather_small(packed, name):
    R = packed.shape[0]

    def body(x_ref, out_ref, send_sems, recv_sems):
        x, y, c = _position()
        me = 4 * x + 2 * y + c
        out_ref[me] = x_ref[...]
        copies = []
        for k in range(1, N_DEV):
            to = (x ^ ((k >> 2) & 1), y ^ ((k >> 1) & 1), c ^ (k & 1))
            cp = pltpu.make_async_remote_copy(
                src_ref=x_ref, dst_ref=out_ref.at[me],
                send_sem=send_sems.at[k], recv_sem=recv_sems.at[k], device_id=to, device_id_type=MESH)
            cp.start()
            copies.append((k, to, cp))
        for k, to, cp in copies:
            cp.wait_send()
            pltpu.make_async_remote_copy(
                src_ref=x_ref, dst_ref=out_ref.at[4 * to[0] + 2 * to[1] + to[2]],
                send_sem=send_sems.at[k], recv_sem=recv_sems.at[k], device_id=to, device_id_type=MESH).wait_recv()

    return pl.pallas_call(
        body, name=name,
        in_specs=[pl.BlockSpec(memory_space=pltpu.VMEM)], out_specs=pl.BlockSpec(memory_space=pltpu.VMEM),
        out_shape=jax.ShapeDtypeStruct((N_DEV, R, 128), F32),
        scratch_shapes=[pltpu.SemaphoreType.DMA((N_DEV,)), pltpu.SemaphoreType.DMA((N_DEV,))],
    )(packed)


def _adamw_math(w, g, m, v):
    m = ADAM_B1 * m + (1.0 - ADAM_B1) * g
    v = ADAM_B2 * v + (1.0 - ADAM_B2) * (g * g)
    m_hat = m / (1.0 - ADAM_B1 ** ADAM_STEP)
    v_hat = v / (1.0 - ADAM_B2 ** ADAM_STEP)
    delta = -ADAM_LR * (m_hat / (jnp.sqrt(v_hat) + ADAM_EPS) + ADAM_WD * w)
    return delta, m, v


def _adamw_big(w, m, v, parts, name):
    R, C = w.shape
    tr = _pick(R, (256,))

    def body(w_ref, m_ref, v_ref, p_ref, g_ref, d_ref, nm_ref, nv_ref):
        g = p_ref[0].astype(F32)
        for s in range(1, N_CHIP):
            g = g + p_ref[s].astype(F32)
        d, nm, nv = _adamw_math(w_ref[...], g, m_ref[...], v_ref[...])
        g_ref[...] = g
        d_ref[...] = d
        nm_ref[...] = nm
        nv_ref[...] = nv

    blk = pl.BlockSpec((tr, C), lambda i: (i, 0))
    out = jax.ShapeDtypeStruct((R, C), F32)
    return pl.pallas_call(
        body, name=name, grid=(R // tr,),
        in_specs=[blk, blk, blk, pl.BlockSpec((N_CHIP, tr, C), lambda i: (0, i, 0))],
        out_specs=[blk, blk, blk, blk], out_shape=[out, out, out, out],
        compiler_params=_cparams(("parallel",)),
    )(w, m, v, parts)


def _adamw_big_landed(w, m, v, parts, lands, slot, name):
    R, C = w.shape
    tr = _pick(R, (256,))

    def body(slot_ref, w_ref, m_ref, v_ref, own_ref, l1_ref, l2_ref, l3_ref, g_ref, d_ref, nm_ref, nv_ref):
        g = own_ref[...].astype(F32)
        for ref in (l1_ref, l2_ref, l3_ref):
            g = g + ref[...].astype(F32)
        d, nm, nv = _adamw_math(w_ref[...], g, m_ref[...], v_ref[...])
        g_ref[...] = g
        d_ref[...] = d
        nm_ref[...] = nm
        nv_ref[...] = nv

    blk = pl.BlockSpec((tr, C), lambda i, slot: (i, 0))

    def chip(k):
        return pl.BlockSpec((None, tr, C), lambda i, slot: ((slot[0] + k) % N_CHIP, i, 0))

    out = jax.ShapeDtypeStruct((R, C), F32)
    return pl.pallas_call(
        body, name=name,
        grid_spec=pltpu.PrefetchScalarGridSpec(
            num_scalar_prefetch=1, grid=(R // tr,),
            in_specs=[blk, blk, blk, chip(0), chip(1), chip(2), chip(3)],
            out_specs=[blk, blk, blk, blk]),
        out_shape=[out, out, out, out],
        compiler_params=_cparams(("parallel",)),
    )(slot, w, m, v, parts, lands, lands, lands)


def _adamw_small(w, m, v, gathered, name):
    R = w.shape[0]

    def body(w_ref, m_ref, v_ref, p_ref, g_ref, d_ref, nm_ref, nv_ref):
        g = p_ref[0]
        for s in range(1, N_DEV):
            g = g + p_ref[s]
        d, nm, nv = _adamw_math(w_ref[...], g, m_ref[...], v_ref[...])
        g_ref[...] = g
        d_ref[...] = d
        nm_ref[...] = nm
        nv_ref[...] = nv

    out = jax.ShapeDtypeStruct((R, 128), F32)
    return pl.pallas_call(
        body, name=name, out_shape=[out, out, out, out],
    )(w, m, v, gathered)


SMALL_NAMES = ("lb_logits", "hg_norm_w", "rel_bias", "norm_mix_w", "norm_mlp_w", "norm_final_w")
SMALL_SHAPES = {"lb_logits": (2, HG_WIDTH), "hg_norm_w": (1, HG_DK), "rel_bias": (AT_HEADS, N_REL_PAD),
                "norm_mix_w": (1, D_MODEL), "norm_mlp_w": (1, D_MODEL), "norm_final_w": (1, D_MODEL)}


def _pack_small(parts):
    rows = []
    for nme in SMALL_NAMES:
        p = parts[nme]
        if nme == "rel_bias":
            p = jnp.pad(p, ((0, 0), (0, N_REL_PAD - N_REL)))
        rows.append(p.reshape(-1, 128))
    flat = jnp.concatenate(rows, axis=0)
    return jnp.pad(flat, ((0, SMALL_ROWS - flat.shape[0]), (0, 0)))


def _unpack_small(packed):
    out, at = {}, 0
    for nme in SMALL_NAMES:
        shp = SMALL_SHAPES[nme]
        nrow = shp[0] * shp[1] // 128
        p = packed[at:at + nrow].reshape(shp)
        at += nrow
        out[nme] = p[:, :N_REL] if nme == "rel_bias" else p
    return out


BIG_NAMES = ("w_in", "w_branch_a", "w_branch_b", "w_out", "w_up", "w_down")


def kernel(x, w_in, lb_logits, hg_norm_w, rel_bias, w_branch_a, w_branch_b, w_out, norm_mix_w, norm_mlp_w, w_up, w_down, norm_final_w, loss_target, m_w_in, m_lb_logits, m_hg_norm_w, m_rel_bias, m_w_branch_a, m_w_branch_b, m_w_out, m_norm_mix_w, m_norm_mlp_w, m_w_up, m_w_down, m_norm_final_w, v_w_in, v_lb_logits, v_hg_norm_w, v_rel_bias, v_w_branch_a, v_w_branch_b, v_w_out, v_norm_mix_w, v_norm_mlp_w, v_w_up, v_w_down, v_norm_final_w):
    big_w = [w_in[0], w_branch_a[0], w_branch_b[0], w_out[0], w_up[0], w_down[0]]
    big_m = [m_w_in[0], m_w_branch_a[0], m_w_branch_b[0], m_w_out[0], m_w_up[0], m_w_down[0]]
    big_v = [v_w_in[0], v_w_branch_a[0], v_w_branch_b[0], v_w_out[0], v_w_up[0], v_w_down[0]]

    shards = [w.astype(BF16) for w in big_w]
    parity = lax.axis_index("c").astype(jnp.int32).reshape(1)
    loss_part, grad_x, chip_parts, small = _local_step(
        x[0], loss_target[0], lb_logits, hg_norm_w, rel_bias[0], norm_mix_w, norm_mlp_w,
        norm_final_w.reshape(1, D_MODEL), shards[0], shards[1:], _Exchanges(parity, _gather_order()))
    loss = lax.psum(loss_part[0, 0], ("x", "y", "c"))
    big = [None] + [_adamw_big(w, m, v, p, "adamw_" + nme)
                    for w, m, v, p, nme in list(zip(big_w, big_m, big_v, chip_parts, BIG_NAMES))[1:]]

    sw = dict(lb_logits=lb_logits, hg_norm_w=hg_norm_w, rel_bias=rel_bias[0], norm_mix_w=norm_mix_w,
              norm_mlp_w=norm_mlp_w, norm_final_w=norm_final_w.reshape(1, D_MODEL))
    sm = dict(lb_logits=m_lb_logits, hg_norm_w=m_hg_norm_w, rel_bias=m_rel_bias[0], norm_mix_w=m_norm_mix_w,
              norm_mlp_w=m_norm_mlp_w, norm_final_w=m_norm_final_w.reshape(1, D_MODEL))
    sv = dict(lb_logits=v_lb_logits, hg_norm_w=v_hg_norm_w, rel_bias=v_rel_bias[0], norm_mix_w=v_norm_mix_w,
              norm_mlp_w=v_norm_mlp_w, norm_final_w=v_norm_final_w.reshape(1, D_MODEL))
    gathered = _gather_small(_pack_small(small), "gather_small")
    small_packed = _adamw_small(_pack_small(sw), _pack_small(sm), _pack_small(sv), gathered, "adamw_small")
    small_out = [_unpack_small(p) for p in small_packed]

    (own_part,), (landed,) = _scatter_wait(chip_parts[0], [grad_x, small_packed[0]] + [b[1] for b in big[1:]],
                                           "rs_scatter_late_wait")
    slot = (2 * lax.axis_index("x") + lax.axis_index("y")).astype(jnp.int32).reshape(1)
    big[0] = _adamw_big_landed(big_w[0], big_m[0], big_v[0], own_part, landed, slot, "adamw_w_in")

    def leaf(kind, nme):
        if nme in BIG_NAMES:
            return big[BIG_NAMES.index(nme)][kind][None]
        p = small_out[kind][nme]
        if nme == "rel_bias":
            return p[None]
        if nme == "norm_final_w":
            return p.reshape(D_MODEL)
        return p

    order = ("w_in", "lb_logits", "hg_norm_w", "rel_bias", "w_branch_a", "w_branch_b", "w_out", "norm_mix_w",
             "norm_mlp_w", "w_up", "w_down", "norm_final_w")
    outs = [loss, grad_x[None]]
    for kind in range(4):
        outs += [leaf(kind, nme) for nme in order]
    return tuple(outs)
```

```python
import functools

import jax
import jax.numpy as jnp
from jax import lax
from jax.experimental import pallas as pl
from jax.experimental.pallas import tpu as pltpu

F32 = jnp.float32
BF16 = jnp.bfloat16
HIGHEST = lax.Precision.HIGHEST
MESH = pl.DeviceIdType.MESH

D_MODEL = 2048
HG_HEADS = 8
HG_DK = 128
HG_WIDTH = 1024
AT_HEADS = 16
AT_DH = 64
AT_WIDTH = 1024
CHUNK = 64
LEFT_CHUNKS = 8
BAND = (LEFT_CHUNKS + 1) * CHUNK
PAD = LEFT_CHUNKS * CHUNK
REL_CLIP = 256
N_REL = 2 * REL_CLIP + 1
N_REL_PAD = 640
D_FF = 4 * D_MODEL
D_IN = 4 * HG_WIDTH + 3 * AT_WIDTH + 2 * D_MODEL
EPS = 1e-6
N_DEV = 8
N_CHIP = 4

ADAM_LR = 0.001
ADAM_B1 = 0.9
ADAM_B2 = 0.999
ADAM_EPS = 1e-08
ADAM_WD = 0.01
ADAM_STEP = 10

COL_HQ, COL_HF, COL_HI, COL_HG = 0, 8, 16, 24
COL_AQ, COL_AK, COL_AV = 32, 40, 48
COL_GATE_A, COL_GATE_B = 7, 9

VMEM_LIMIT = 56 * 1024 * 1024
SMALL_ROWS = 152


def _cparams(sem=None, **kw):
    if sem is not None:
        kw["dimension_semantics"] = sem
    return pltpu.CompilerParams(vmem_limit_bytes=VMEM_LIMIT, **kw)


def _pick(n, cands):
    for c in cands:
        if n % c == 0:
            return c
    return n


def _sigmoid(x):
    return 1.0 / (1.0 + jnp.exp(-x))


ANY = pl.BlockSpec(memory_space=pl.ANY)


def _position():
    return lax.axis_index("x"), lax.axis_index("y"), lax.axis_index("c")


def _call(body, args, *, name, grid, in_specs, out_specs, out_shape, scratch_shapes=(), sem=None, after=()):
    n_in = len(args)

    def ordered(*refs):
        body(*refs[:n_in], *refs[n_in + len(after):])

    return list(pl.pallas_call(
        ordered if after else body, name=name, grid=grid, in_specs=list(in_specs) + [ANY] * len(after),
        out_specs=out_specs, out_shape=out_shape, scratch_shapes=list(scratch_shapes),
        compiler_params=_cparams(sem))(*args, *after))


def _mm_nn(a, wb, out_dtype, name, after=()):
    M, K = a.shape
    NB, K2, Nb = wb.shape
    assert K == K2
    tm = min(M, 1024)
    tk = min(K, 2048)
    tn = _pick(Nb, (512, 1408, 256))
    nk = K // tk
    nn = Nb // tn

    def body(a_ref, b_ref, o_ref, *acc):
        part = jnp.dot(a_ref[...], b_ref[...], preferred_element_type=F32)
        if nk == 1:
            o_ref[...] = part.astype(out_dtype)
        else:
            acc_ref, = acc
            k = pl.program_id(3)

            @pl.when(k == 0)
            def _():
                acc_ref[...] = part

            @pl.when(k > 0)
            def _():
                acc_ref[...] += part

            @pl.when(k == nk - 1)
            def _():
                o_ref[...] = acc_ref[...].astype(out_dtype)

    out, = _call(
        body, (a, wb), name=name, grid=(M // tm, NB, nn, nk),
        in_specs=[pl.BlockSpec((tm, tk), lambda m, j, n, k: (m, k)),
                  pl.BlockSpec((None, tk, tn), lambda m, j, n, k: (j, k, n))],
        out_specs=[pl.BlockSpec((tm, tn), lambda m, j, n, k: (m, j * nn + n))],
        out_shape=[jax.ShapeDtypeStruct((M, NB * Nb), out_dtype)],
        scratch_shapes=[] if nk == 1 else [pltpu.VMEM((tm, tn), F32)],
        sem=("parallel", "parallel", "parallel", "arbitrary"), after=after)
    return out


def _mm_nt(a, wb, out_dtype, name, after=()):
    M, N = a.shape
    NB, K, Nb = wb.shape
    assert N == NB * Nb
    tm = min(M, 1024)
    tko = _pick(K, (1024,))
    tc = _pick(Nb, (1024, 1408, 256))
    nc = Nb // tc
    nsteps = NB * nc

    def body(a_ref, b_ref, o_ref, acc_ref):
        step = pl.program_id(2) * nc + pl.program_id(3)
        part = lax.dot_general(a_ref[...], b_ref[...], (((1,), (1,)), ((), ())), preferred_element_type=F32)

        @pl.when(step == 0)
        def _():
            acc_ref[...] = part

        @pl.when(step > 0)
        def _():
            acc_ref[...] += part

        @pl.when(step == nsteps - 1)
        def _():
            o_ref[...] = acc_ref[...].astype(out_dtype)

    out, = _call(
        body, (a, wb), name=name,
        grid=(M // tm, K // tko, NB, nc),
        in_specs=[pl.BlockSpec((tm, tc), lambda m, ko, j, c: (m, j * nc + c)),
                  pl.BlockSpec((None, tko, tc), lambda m, ko, j, c: (j, ko, c))],
        out_specs=[pl.BlockSpec((tm, tko), lambda m, ko, j, c: (m, ko))],
        out_shape=[jax.ShapeDtypeStruct((M, K), out_dtype)],
        scratch_shapes=[pltpu.VMEM((tm, tko), F32)],
        sem=("parallel", "parallel", "arbitrary", "arbitrary"), after=after)
    return out


def _mm_tn(a, g, nb, out_dtype, name):
    M, Ka = a.shape
    M2, N = g.shape
    assert M == M2 and N % nb == 0
    Nb = N // nb
    tka = _pick(Ka, (1024,))
    tn = _pick(Nb, (512, 1408, 256))
    nn = Nb // tn

    def body(a_ref, g_ref, o_ref):
        o_ref[...] = lax.dot_general(a_ref[...], g_ref[...], (((0,), (0,)), ((), ())),
                                     preferred_element_type=F32).astype(out_dtype)

    return pl.pallas_call(
        body, name=name,
        grid=(Ka // tka, nb, nn),
        in_specs=[pl.BlockSpec((M, tka), lambda ka, j, n: (0, ka)),
                  pl.BlockSpec((M, tn), lambda ka, j, n: (0, j * nn + n))],
        out_specs=pl.BlockSpec((None, tka, tn), lambda ka, j, n: (j, ka, n)),
        out_shape=jax.ShapeDtypeStruct((nb, Ka, Nb), out_dtype),
        compiler_params=_cparams(("parallel", "parallel", "parallel")),
    )(a, g)


ROW_TILE = 256


def _rms_fwd(x, w, name):
    T, Dm = x.shape

    def body(x_ref, w_ref, u_ref):
        xv = x_ref[...]
        r = lax.rsqrt(jnp.mean(xv * xv, axis=-1, keepdims=True) + EPS)
        u_ref[...] = (xv * r * w_ref[...]).astype(BF16)

    return pl.pallas_call(
        body, name=name, grid=(T // ROW_TILE,),
        in_specs=[pl.BlockSpec((ROW_TILE, Dm), lambda i: (i, 0)), pl.BlockSpec((1, Dm), lambda i: (0, 0))],
        out_specs=pl.BlockSpec((ROW_TILE, Dm), lambda i: (i, 0)),
        out_shape=jax.ShapeDtypeStruct((T, Dm), BF16),
        compiler_params=_cparams(("parallel",)),
    )(x, w)


def _resid_rms_fwd(x, mix, w, name):
    T, Dm = x.shape

    def body(x_ref, m_ref, w_ref, h_ref, u_ref):
        h = x_ref[...] + m_ref[...]
        h_ref[...] = h
        r = lax.rsqrt(jnp.mean(h * h, axis=-1, keepdims=True) + EPS)
        u_ref[...] = (h * r * w_ref[...]).astype(BF16)

    row = pl.BlockSpec((ROW_TILE, Dm), lambda i: (i, 0))
    return pl.pallas_call(
        body, name=name, grid=(T // ROW_TILE,),
        in_specs=[row, row, pl.BlockSpec((1, Dm), lambda i: (0, 0))],
        out_specs=[row, row],
        out_shape=[jax.ShapeDtypeStruct((T, Dm), F32), jax.ShapeDtypeStruct((T, Dm), BF16)],
        compiler_params=_cparams(("parallel",)),
    )(x, mix, w)


def _loss_head(h1, mlp, wf, target, name):
    T, Dm = h1.shape

    def body(h_ref, m_ref, w_ref, t_ref, loss_ref, dh_ref, dhb_ref, dw_ref):
        i = pl.program_id(0)
        h = h_ref[...] + m_ref[...]
        r = lax.rsqrt(jnp.mean(h * h, axis=-1, keepdims=True) + EPS)
        xh = h * r
        wv = w_ref[...]
        e = xh * wv - t_ref[...]
        part = 0.5 * jnp.sum(jnp.mean(e * e, axis=-1, keepdims=True), axis=0, keepdims=True)
        dy = e * (1.0 / Dm)
        dw = jnp.sum(dy * xh, axis=0, keepdims=True)
        gy = dy * wv
        dh = r * (gy - xh * jnp.mean(gy * xh, axis=-1, keepdims=True))
        dh_ref[...] = dh
        dhb_ref[...] = dh.astype(BF16)

        @pl.when(i == 0)
        def _():
            loss_ref[...] = jnp.zeros_like(loss_ref)
            dw_ref[...] = jnp.zeros_like(dw_ref)

        loss_ref[...] += jnp.broadcast_to(part, loss_ref.shape)
        dw_ref[...] += dw

    row = pl.BlockSpec((ROW_TILE, Dm), lambda i: (i, 0))
    vec = pl.BlockSpec((1, Dm), lambda i: (0, 0))
    return pl.pallas_call(
        body, name=name, grid=(T // ROW_TILE,),
        in_specs=[row, row, vec, row],
        out_specs=[pl.BlockSpec((8, 128), lambda i: (0, 0)), row, row, vec],
        out_shape=[jax.ShapeDtypeStruct((8, 128), F32), jax.ShapeDtypeStruct((T, Dm), F32),
                   jax.ShapeDtypeStruct((T, Dm), BF16), jax.ShapeDtypeStruct((1, Dm), F32)],
        compiler_params=_cparams(("arbitrary",)),
    )(h1, mlp, wf, target)


def _rms_bwd(dyn, x, w, dres, name):
    T, Dm = x.shape

    def body(g_ref, x_ref, w_ref, r_ref, dx_ref, dxb_ref, dw_ref):
        i = pl.program_id(0)
        xv = x_ref[...]
        r = lax.rsqrt(jnp.mean(xv * xv, axis=-1, keepdims=True) + EPS)
        xh = xv * r
        g = g_ref[...]
        dw = jnp.sum(g * xh, axis=0, keepdims=True)
        gy = g * w_ref[...]
        dx = r_ref[...] + r * (gy - xh * jnp.mean(gy * xh, axis=-1, keepdims=True))
        dx_ref[...] = dx
        dxb_ref[...] = dx.astype(BF16)

        @pl.when(i == 0)
        def _():
            dw_ref[...] = jnp.zeros_like(dw_ref)

        dw_ref[...] += dw

    row = pl.BlockSpec((ROW_TILE, Dm), lambda i: (i, 0))
    vec = pl.BlockSpec((1, Dm), lambda i: (0, 0))
    return pl.pallas_call(
        body, name=name, grid=(T // ROW_TILE,),
        in_specs=[row, row, vec, row],
        out_specs=[row, row, vec],
        out_shape=[jax.ShapeDtypeStruct((T, Dm), F32), jax.ShapeDtypeStruct((T, Dm), BF16),
                   jax.ShapeDtypeStruct((1, Dm), F32)],
        compiler_params=_cparams(("arbitrary",)),
    )(dyn, x, w, dres)


COL_TILE = 2048


def _relu2_fwd(a, name):
    T, N = a.shape

    def body(a_ref, r_ref):
        ra = jnp.maximum(a_ref[...], 0.0)
        r_ref[...] = (ra * ra).astype(BF16)

    blk = pl.BlockSpec((ROW_TILE, COL_TILE), lambda i, j: (i, j))
    return pl.pallas_call(
        body, name=name, grid=(T // ROW_TILE, N // COL_TILE), in_specs=[blk], out_specs=blk,
        out_shape=jax.ShapeDtypeStruct((T, N), BF16),
        compiler_params=_cparams(("parallel", "parallel")),
    )(a)


def _relu2_bwd(dr, a, name):
    T, N = a.shape

    def body(dr_ref, a_ref, da_ref):
        da_ref[...] = (dr_ref[...] * (2.0 * jnp.maximum(a_ref[...], 0.0))).astype(BF16)

    blk = pl.BlockSpec((ROW_TILE, COL_TILE), lambda i, j: (i, j))
    return pl.pallas_call(
        body, name=name, grid=(T // ROW_TILE, N // COL_TILE), in_specs=[blk, blk], out_specs=blk,
        out_shape=jax.ShapeDtypeStruct((T, N), BF16),
        compiler_params=_cparams(("parallel", "parallel")),
    )(dr, a)


GATE_TILE = 1024


def _merge_fwd(z, pa, pb, name):
    T, Dm = pa.shape

    def body(za_ref, zb_ref, pa_ref, pb_ref, m_ref):
        m_ref[...] = (_sigmoid(za_ref[...]) * pa_ref[...] + _sigmoid(zb_ref[...]) * pb_ref[...]).astype(BF16)

    blk = pl.BlockSpec((ROW_TILE, GATE_TILE), lambda i, j: (i, j))
    return pl.pallas_call(
        body, name=name, grid=(T // ROW_TILE, Dm // GATE_TILE),
        in_specs=[pl.BlockSpec((ROW_TILE, GATE_TILE), lambda i, j: (i, COL_GATE_A + j)),
                  pl.BlockSpec((ROW_TILE, GATE_TILE), lambda i, j: (i, COL_GATE_B + j)), blk, blk],
        out_specs=blk,
        out_shape=jax.ShapeDtypeStruct((T, Dm), BF16),
        compiler_params=_cparams(("parallel", "parallel")),
    )(z, z, pa, pb)


def _merge_bwd(dm, z, pa, pb, name):
    T, Dm = pa.shape

    def body(dm_ref, za_ref, zb_ref, pa_ref, pb_ref, dpa_ref, dpb_ref, dga_ref, dgb_ref):
        d = dm_ref[...]
        ga = _sigmoid(za_ref[...])
        gb = _sigmoid(zb_ref[...])
        dpa_ref[...] = (d * ga).astype(BF16)
        dpb_ref[...] = (d * gb).astype(BF16)
        dga_ref[...] = (d * pa_ref[...] * ga * (1.0 - ga)).astype(BF16)
        dgb_ref[...] = (d * pb_ref[...] * gb * (1.0 - gb)).astype(BF16)

    blk = pl.BlockSpec((ROW_TILE, GATE_TILE), lambda i, j: (i, j))
    out = jax.ShapeDtypeStruct((T, Dm), BF16)
    return pl.pallas_call(
        body, name=name, grid=(T // ROW_TILE, Dm // GATE_TILE),
        in_specs=[blk, pl.BlockSpec((ROW_TILE, GATE_TILE), lambda i, j: (i, COL_GATE_A + j)),
                  pl.BlockSpec((ROW_TILE, GATE_TILE), lambda i, j: (i, COL_GATE_B + j)), blk, blk],
        out_specs=[blk, blk, blk, blk],
        out_shape=[out, out, out, out],
        compiler_params=_cparams(("parallel", "parallel")),
    )(dm, z, z, pa, pb)


def _dot_hi(a, b, dims):
    return lax.dot_general(a, b, (dims, ((), ())), precision=HIGHEST, preferred_element_type=F32)


NN = ((1,), (0,))
NT = ((1,), (1,))
TN = ((0,), (0,))


def _hg_gates(hq, hf, lb):
    sq = _sigmoid(hq)
    q = hq * sq * (HG_DK ** -0.5)
    f = _sigmoid(hf)
    g = lb + (1.0 - lb) * f
    return q, sq, f, g, jnp.log(g), 1.0 - g


def _tri(lower):
    r = lax.broadcasted_iota(jnp.int32, (CHUNK, CHUNK), 0)
    c = lax.broadcasted_iota(jnp.int32, (CHUNK, CHUNK), 1)
    return jnp.where((r >= c) if lower else (r <= c), 1.0, 0.0).astype(F32)


def _hgrn2_fwd(z, lb_logits, hg_norm_w, name, after=()):
    T = z.shape[0]
    n_chunks = T // CHUNK

    def body(hq_ref, hf_ref, hi_ref, hg_ref, lbl_ref, nw_ref, o_ref, ya_ref, sall_ref, st_ref):
        lbl = lbl_ref[...]
        lb = 1.0 / (1.0 + jnp.exp(lbl[1:2, :] - lbl[0:1, :]))
        st_ref[...] = jnp.zeros_like(st_ref)
        tri = _tri(True)
        row8 = lax.broadcasted_iota(jnp.int32, (8, HG_DK), 0)

        def chunk(c, carry):
            rows = pl.ds(pl.multiple_of(c * CHUNK, CHUNK), CHUNK)
            q, _, _, _, lg, kk = _hg_gates(hq_ref[rows, :], hf_ref[rows, :], lb)
            v = hi_ref[rows, :]
            b = _dot_hi(tri, lg, NN)
            st = st_ref[...]
            sall_ref[c] = st
            o_inter = _dot_hi(q * jnp.exp(b), st, NT)
            for g8 in range(CHUNK // 8):
                n = 8 * (g8 + 1)
                bs, ks, vs = b[:n], kk[:n], v[:n]
                sidx = lax.broadcasted_iota(jnp.int32, (n, HG_DK), 0)
                blk = o_inter[8 * g8:n]
                for i in range(8):
                    t = 8 * g8 + i
                    e = jnp.where(sidx <= t, jnp.exp(b[t:t + 1] - bs), 0.0)
                    p = jnp.sum(e * ks * q[t:t + 1], axis=1, keepdims=True)
                    ot = jnp.sum(p * vs, axis=0, keepdims=True)
                    blk = blk + jnp.where(row8 == i, ot, 0.0)
                o_ref[pl.ds(pl.multiple_of(c * CHUNK + 8 * g8, 8), 8), :] = blk
            bl = b[CHUNK - 1:CHUNK]
            ke = kk * jnp.exp(bl - b)
            st_ref[...] = st * jnp.exp(bl) + _dot_hi(v, ke, TN)
            return carry

        lax.fori_loop(0, n_chunks, chunk, 0)
        o = o_ref[...]
        r = lax.rsqrt(jnp.mean(o * o, axis=-1, keepdims=True) + EPS)
        hg = hg_ref[...]
        ya_ref[...] = (o * r * nw_ref[...] * (hg * _sigmoid(hg))).astype(BF16)

    def col(base):
        return pl.BlockSpec((T, HG_DK), lambda h: (0, base + h))

    return _call(
        body, (z, z, z, z, lb_logits, hg_norm_w), name=name, grid=(HG_HEADS,),
        in_specs=[col(COL_HQ), col(COL_HF), col(COL_HI), col(COL_HG),
                  pl.BlockSpec((2, HG_DK), lambda h: (0, h)), pl.BlockSpec((1, HG_DK), lambda h: (0, 0))],
        out_specs=[col(0), col(0), pl.BlockSpec((None, n_chunks, HG_DK, HG_DK), lambda h: (h, 0, 0, 0))],
        out_shape=[jax.ShapeDtypeStruct((T, HG_WIDTH), F32), jax.ShapeDtypeStruct((T, HG_WIDTH), BF16),
                   jax.ShapeDtypeStruct((HG_HEADS, n_chunks, HG_DK, HG_DK), F32)],
        scratch_shapes=[pltpu.VMEM((HG_DK, HG_DK), F32)],
        sem=("parallel",), after=after)


def _hgrn2_bwd(z, lb_logits, hg_norm_w, o_raw, s_all, dya, name, after=()):
    T = z.shape[0]
    n_chunks = T // CHUNK

    def body(hq_ref, hf_ref, hi_ref, hg_ref, lbl_ref, nw_ref, o_ref, sall_ref, dya_ref,
             dhq_ref, dhf_ref, dhi_ref, dhg_ref, dlbl_ref, dnw_ref,
             do_ref, dst_ref, dq_ref, dk_ref, dv_ref, dlb_ref):
        h = pl.program_id(0)
        lbl = lbl_ref[...]
        lb = 1.0 / (1.0 + jnp.exp(lbl[1:2, :] - lbl[0:1, :]))

        o = o_ref[...]
        r = lax.rsqrt(jnp.mean(o * o, axis=-1, keepdims=True) + EPS)
        oh = o * r
        nw = nw_ref[...]
        hg = hg_ref[...]
        sg = _sigmoid(hg)
        dy = dya_ref[...]
        d_on = dy * (hg * sg)
        dhg_ref[...] = (dy * (oh * nw) * (sg * (1.0 + hg * (1.0 - sg)))).astype(BF16)
        dnw = jnp.sum(d_on * oh, axis=0, keepdims=True)
        gy = d_on * nw
        do_ref[...] = r * (gy - oh * jnp.mean(gy * oh, axis=-1, keepdims=True))

        @pl.when(h == 0)
        def _():
            dnw_ref[...] = jnp.zeros_like(dnw_ref)

        dnw_ref[...] += jnp.broadcast_to(dnw, dnw_ref.shape)

        dst_ref[...] = jnp.zeros_like(dst_ref)
        dlb_ref[...] = jnp.zeros_like(dlb_ref)
        tri = _tri(True)
        tri_t = _tri(False)
        row8 = lax.broadcasted_iota(jnp.int32, (8, HG_DK), 0)

        def chunk(ci, carry):
            c = n_chunks - 1 - ci
            rows = pl.ds(pl.multiple_of(c * CHUNK, CHUNK), CHUNK)
            hq = hq_ref[rows, :]
            q, sq, f, g, lg, kk = _hg_gates(hq, hf_ref[rows, :], lb)
            v = hi_ref[rows, :]
            do = do_ref[rows, :]
            b = _dot_hi(tri, lg, NN)
            eb = jnp.exp(b)
            bl = b[CHUNK - 1:CHUNK]
            ebl = jnp.exp(bl)
            ekb = jnp.exp(bl - b)
            qe = q * eb
            ke = kk * ekb
            st = sall_ref[c]
            dst = dst_ref[...]
            dqe = _dot_hi(do, st, NN)
            dke = _dot_hi(v, dst, NN)
            dv_inter = _dot_hi(ke, dst, NT)
            d_ebl = jnp.sum(st * dst, axis=0, keepdims=True)
            dst_ref[...] = dst * ebl + _dot_hi(do, qe, TN)

            dk_ref[...] = jnp.zeros_like(dk_ref)
            dv_ref[...] = jnp.zeros_like(dv_ref)
            for g8 in range(CHUNK // 8):
                n = 8 * (g8 + 1)
                bs, ks, vs = b[:n], kk[:n], v[:n]
                sidx = lax.broadcasted_iota(jnp.int32, (n, HG_DK), 0)
                blk = jnp.zeros((8, HG_DK), F32)
                for i in range(8):
                    t = 8 * g8 + i
                    qt = q[t:t + 1]
                    dot_ = do[t:t + 1]
                    e = jnp.where(sidx <= t, jnp.exp(b[t:t + 1] - bs), 0.0)
                    w = e * ks
                    p = jnp.sum(w * qt, axis=1, keepdims=True)
                    dsc = jnp.sum(vs * dot_, axis=1, keepdims=True)
                    dqt = jnp.sum(dsc * w, axis=0, keepdims=True)
                    blk = blk + jnp.where(row8 == i, dqt, 0.0)
                    dk_ref[0:n, :] += dsc * e * qt
                    dv_ref[0:n, :] += p * dot_
                dq_ref[8 * g8:n, :] = blk
            dq_i = dq_ref[...]
            dk_i = dk_ref[...]
            dke_ke = dke * ke
            db = q * dq_i - kk * dk_i + dqe * qe - dke_ke
            db_last = jnp.sum(dke_ke, axis=0, keepdims=True) + d_ebl * ebl
            dlg = _dot_hi(tri_t, db, NN) + db_last
            dq = dq_i + dqe * eb
            dkk = dk_i + dke * ekb
            dg = dlg / g - dkk
            dhq_ref[rows, :] = (dq * (HG_DK ** -0.5) * (sq * (1.0 + hq * (1.0 - sq)))).astype(BF16)
            dhf_ref[rows, :] = (dg * (1.0 - lb) * f * (1.0 - f)).astype(BF16)
            dhi_ref[rows, :] = (dv_ref[...] + dv_inter).astype(BF16)
            dlb_ref[...] += jnp.sum(dg * (1.0 - f), axis=0, keepdims=True)
            return carry

        lax.fori_loop(0, n_chunks, chunk, 0)
        dl0 = dlb_ref[...] * lb * (1.0 - lb)
        dlbl_ref[0:1, :] = dl0
        dlbl_ref[1:2, :] = -dl0

    def col(base):
        return pl.BlockSpec((T, HG_DK), lambda h: (0, base + h))

    outb = jax.ShapeDtypeStruct((T, HG_WIDTH), BF16)
    return _call(
        body, (z, z, z, z, lb_logits, hg_norm_w, o_raw, s_all, dya), name=name, grid=(HG_HEADS,),
        in_specs=[col(COL_HQ), col(COL_HF), col(COL_HI), col(COL_HG),
                  pl.BlockSpec((2, HG_DK), lambda h: (0, h)), pl.BlockSpec((1, HG_DK), lambda h: (0, 0)),
                  col(0), pl.BlockSpec((None, n_chunks, HG_DK, HG_DK), lambda h: (h, 0, 0, 0)), col(0)],
        out_specs=[col(0), col(0), col(0), col(0), pl.BlockSpec((2, HG_DK), lambda h: (0, h)),
                   pl.BlockSpec((8, HG_DK), lambda h: (0, 0))],
        out_shape=[outb, outb, outb, outb, jax.ShapeDtypeStruct((2, HG_WIDTH), F32),
                   jax.ShapeDtypeStruct((8, HG_DK), F32)],
        scratch_shapes=[pltpu.VMEM((T, HG_DK), F32), pltpu.VMEM((HG_DK, HG_DK), F32),
                        pltpu.VMEM((CHUNK, HG_DK), F32), pltpu.VMEM((CHUNK, HG_DK), F32),
                        pltpu.VMEM((CHUNK, HG_DK), F32), pltpu.VMEM((1, HG_DK), F32)],
        sem=("arbitrary",), after=after)


CONST_KEYS = PAD - REL_CLIP
VAR_KEYS = BAND - CONST_KEYS
REL_LO = 128
REL_SPAN = N_REL_PAD - REL_LO


def _rel_onehot(t):
    r = lax.broadcasted_iota(jnp.int32, (REL_SPAN, VAR_KEYS), 0)
    j = lax.broadcasted_iota(jnp.int32, (REL_SPAN, VAR_KEYS), 1)
    idx = jnp.clip(t + PAD - CONST_KEYS - j, -REL_CLIP, REL_CLIP) + REL_CLIP - REL_LO
    return jnp.where(r == idx, 1.0, 0.0).astype(BF16)


def _split3(x):
    hi = x.astype(BF16)
    r1 = x - hi.astype(F32)
    mid = r1.astype(BF16)
    return hi, mid, (r1 - mid.astype(F32)).astype(BF16)


def _bias_expand(rel, name):
    def body(rel_ref, out_ref):
        tab = rel_ref[...]
        onehot = _rel_onehot(pl.program_id(0))
        out_ref[:, 0:CONST_KEYS] = jnp.broadcast_to(tab[:, 2 * REL_CLIP:2 * REL_CLIP + 1], (AT_HEADS, CONST_KEYS))
        out_ref[:, CONST_KEYS:BAND] = sum(
            jnp.dot(piece, onehot, preferred_element_type=F32) for piece in _split3(tab[:, REL_LO:N_REL_PAD]))

    return pl.pallas_call(
        body, name=name, grid=(CHUNK,),
        in_specs=[pl.BlockSpec((AT_HEADS, N_REL_PAD), lambda t: (0, 0))],
        out_specs=pl.BlockSpec((None, AT_HEADS, BAND), lambda t: (t, 0, 0)),
        out_shape=jax.ShapeDtypeStruct((CHUNK, AT_HEADS, BAND), F32),
        compiler_params=_cparams(("parallel",)),
    )(rel)


def _bias_reduce(dbias_t, name):
    def body(db_ref, out_ref):
        t = pl.program_id(0)

        @pl.when(t == 0)
        def _():
            out_ref[...] = jnp.zeros_like(out_ref)

        db = db_ref[...]
        onehot = _rel_onehot(t)
        acc = sum(lax.dot_general(piece, onehot, (NT, ((), ())), preferred_element_type=F32)
                  for piece in _split3(db[:, CONST_KEYS:BAND]))
        lane = lax.broadcasted_iota(jnp.int32, (AT_HEADS, REL_SPAN), 1)
        last = jnp.sum(db[:, 0:CONST_KEYS], axis=1, keepdims=True)
        out_ref[:, REL_LO:N_REL_PAD] += acc + jnp.where(lane == 2 * REL_CLIP - REL_LO, last, 0.0)

    return pl.pallas_call(
        body, name=name, grid=(CHUNK,),
        in_specs=[pl.BlockSpec((None, AT_HEADS, BAND), lambda t: (t, 0, 0))],
        out_specs=pl.BlockSpec((AT_HEADS, N_REL_PAD), lambda t: (0, 0)),
        out_shape=jax.ShapeDtypeStruct((AT_HEADS, N_REL_PAD), F32),
        compiler_params=_cparams(("arbitrary",)),
    )(dbias_t)


def _pair_lanes():
    return lax.broadcasted_iota(jnp.int32, (CHUNK, 2 * AT_DH), 1) < AT_DH


def _block_diag(a):
    first = _pair_lanes()
    return jnp.concatenate([jnp.where(first, a, 0.0), jnp.where(first, 0.0, a)], axis=0).astype(BF16)


def _diag_blocks(a):
    return jnp.where(_pair_lanes(), a[:CHUNK], a[CHUNK:])


def _band_probs_t(kb, qbd, bias_t, c):
    s = lax.dot_general(kb, qbd, (NT, ((), ())), preferred_element_type=F32) * (AT_DH ** -0.5) + bias_t
    j = lax.broadcasted_iota(jnp.int32, (BAND, 2 * AT_DH), 0)
    s = jnp.where(j + c * CHUNK >= PAD, s, -jnp.inf)
    p = jnp.exp(s - jnp.max(s, axis=0, keepdims=True))
    return p / jnp.sum(p, axis=0, keepdims=True)


def _fill_padded(dst_ref, src_ref, T):
    dst_ref[0:PAD, :] = jnp.zeros((PAD, 2 * AT_DH), BF16)
    dst_ref[PAD:PAD + T, :] = src_ref[...].astype(BF16)


def _attn_fwd(z, bias_t, name, after=()):
    T = z.shape[0]
    n_chunks = T // CHUNK

    def body(q_ref, k_ref, v_ref, bias_ref, y_ref, kp_ref, vp_ref):
        _fill_padded(kp_ref, k_ref, T)
        _fill_padded(vp_ref, v_ref, T)

        def chunk(c, carry):
            rows = pl.ds(pl.multiple_of(c * CHUNK, CHUNK), CHUNK)
            band = pl.ds(pl.multiple_of(c * CHUNK, CHUNK), BAND)
            p = _band_probs_t(kp_ref[band, :], _block_diag(q_ref[rows, :]), bias_ref[...], c)
            o2 = lax.dot_general(p.astype(BF16), vp_ref[band, :], (TN, ((), ())), preferred_element_type=F32)
            y_ref[rows, :] = _diag_blocks(o2).astype(BF16)
            return carry

        lax.fori_loop(0, n_chunks, chunk, 0, unroll=2)

    def col(base):
        return pl.BlockSpec((T, 128), lambda h: (0, base + h))

    return _call(
        body, (z, z, z, bias_t), name=name, grid=(AT_HEADS // 2,),
        in_specs=[col(COL_AQ), col(COL_AK), col(COL_AV), pl.BlockSpec((None, BAND, 128), lambda h: (h, 0, 0))],
        out_specs=[col(0)],
        out_shape=[jax.ShapeDtypeStruct((T, AT_WIDTH), BF16)],
        scratch_shapes=[pltpu.VMEM((PAD + T, 128), BF16), pltpu.VMEM((PAD + T, 128), BF16)],
        sem=("parallel",), after=after)


def _attn_bwd(z, bias_t, dyb, name, after=()):
    T = z.shape[0]
    n_chunks = T // CHUNK

    def body(q_ref, k_ref, v_ref, bias_ref, dy_ref, dq_ref, dk_ref, dv_ref, dbias_ref,
             kp_ref, vp_ref, dkp_ref, dvp_ref):
        _fill_padded(kp_ref, k_ref, T)
        _fill_padded(vp_ref, v_ref, T)
        dkp_ref[...] = jnp.zeros_like(dkp_ref)
        dvp_ref[...] = jnp.zeros_like(dvp_ref)
        dbias_ref[...] = jnp.zeros_like(dbias_ref)

        def chunk(c, carry):
            rows = pl.ds(pl.multiple_of(c * CHUNK, CHUNK), CHUNK)
            band = pl.ds(pl.multiple_of(c * CHUNK, CHUNK), BAND)
            qbd = _block_diag(q_ref[rows, :])
            dobd = _block_diag(dy_ref[rows, :])
            kb = kp_ref[band, :]
            vb = vp_ref[band, :]
            p = _band_probs_t(kb, qbd, bias_ref[...], c)
            dp = lax.dot_general(vb, dobd, (NT, ((), ())), preferred_element_type=F32)
            ds = p * (dp - jnp.sum(dp * p, axis=0, keepdims=True))
            dbias_ref[...] += ds
            dsb = ds.astype(BF16)
            dq2 = lax.dot_general(dsb, kb, (TN, ((), ())), preferred_element_type=F32)
            dq_ref[rows, :] = (_diag_blocks(dq2) * (AT_DH ** -0.5)).astype(BF16)
            dkp_ref[band, :] += jnp.dot(dsb, qbd, preferred_element_type=F32) * (AT_DH ** -0.5)
            dvp_ref[band, :] += jnp.dot(p.astype(BF16), dobd, preferred_element_type=F32)
            return carry

        lax.fori_loop(0, n_chunks, chunk, 0)
        dk_ref[...] = dkp_ref[PAD:PAD + T, :].astype(BF16)
        dv_ref[...] = dvp_ref[PAD:PAD + T, :].astype(BF16)

    def col(base):
        return pl.BlockSpec((T, 128), lambda h: (0, base + h))

    pair = pl.BlockSpec((None, BAND, 128), lambda h: (h, 0, 0))
    outb = jax.ShapeDtypeStruct((T, AT_WIDTH), BF16)
    return _call(
        body, (z, z, z, bias_t, dyb), name=name, grid=(AT_HEADS // 2,),
        in_specs=[col(COL_AQ), col(COL_AK), col(COL_AV), pair, col(0)],
        out_specs=[col(0), col(0), col(0), pair],
        out_shape=[outb, outb, outb, jax.ShapeDtypeStruct((AT_HEADS // 2, BAND, 128), F32)],
        scratch_shapes=[pltpu.VMEM((PAD + T, 128), BF16), pltpu.VMEM((PAD + T, 128), BF16),
                        pltpu.VMEM((PAD + T, 128), F32), pltpu.VMEM((PAD + T, 128), F32)],
        sem=("parallel",), after=after)


def _local_step(x, target, lb_logits, hg_norm_w, rel_bias, norm_mix_w, norm_mlp_w, norm_final_w,
                w_in, rest, exchanges=None):
    ex = exchanges
    rel = jnp.pad(rel_bias, ((0, 0), (0, N_REL_PAD - N_REL)))

    u = _rms_fwd(x, norm_mix_w, "rms_mix_fwd")
    if ex:
        z, w_in = _mm_gathered(u, w_in, ex.order, "mm_in_fwd")
        gather = _Gather(rest, [z], "ag")
        tok = [gather.token]
    else:
        z = _mm_nn(u, w_in, F32, "mm_in_fwd")
        w_a, w_b, w_out, w_up, w_down = rest
        tok = []
    o_raw, y_a, s_all = _hgrn2_fwd(z, lb_logits, hg_norm_w, "hgrn2_fwd", after=tok)
    if ex:
        tok = [gather.pass_on([0, 1, 2], [o_raw], "abo")]
    bias_rows = _bias_expand(rel, "bias_expand")
    bias_t = jnp.transpose(bias_rows.reshape(CHUNK, AT_HEADS // 2, 2, BAND), (1, 3, 2, 0)).reshape(
        AT_HEADS // 2, BAND, 2 * CHUNK)
    y_b, = _attn_fwd(z, bias_t, "attn_fwd", after=tok)
    if ex:
        tok = [gather.pass_on([3], [y_b], "up")]
        w_a, w_b, w_out = gather.finish([0, 1, 2], tok, "abo")
    pa = _mm_nn(y_a, w_a, F32, "mm_a_fwd")
    pb = _mm_nn(y_b, w_b, F32, "mm_b_fwd")
    merged = _merge_fwd(z, pa, pb, "merge_fwd")
    w_out1 = w_out.reshape(1, D_MODEL, D_MODEL)
    mix = _mm_nn(merged, w_out1, F32, "mm_out_fwd")
    h1, u2 = _resid_rms_fwd(x, mix, norm_mlp_w, "rms_mlp_fwd")
    if ex:
        tok = [gather.pass_on([4], [u2], "down")]
        w_up, = gather.finish([3], tok, "up")
    a = _mm_nn(u2, w_up, F32, "mm_up_fwd")
    r = _relu2_fwd(a, "relu2_fwd")
    if ex:
        w_down, = gather.finish([4], [r], "down")
    w_down1 = w_down.reshape(1, D_FF, D_MODEL)
    mlp = _mm_nn(r, w_down1, F32, "mm_down_fwd")
    loss, dh2, dh2b, g_nf = _loss_head(h1, mlp, norm_final_w, target, "loss_head")

    def reduce_scatter(grads, name):
        rs = _ReduceScatter(grads, ex.parity, name) if ex else None
        return rs, ([rs.token] if ex else [])

    g_down = _mm_tn(r, dh2b, 1, BF16, "mm_down_wgrad").reshape(N_DEV, D_FF // N_DEV, D_MODEL)
    rs_down, tok = reduce_scatter([g_down], "rs_down")
    dr = _mm_nt(dh2b, w_down1, F32, "mm_down_dgrad", after=tok)
    if ex:
        rs_down.pair_sums([dr])
    da = _relu2_bwd(dr, a, "relu2_bwd")
    g_up = _mm_tn(u2, da, N_DEV, BF16, "mm_up_wgrad")
    rs_up, tok = reduce_scatter([g_up], "rs_up")
    du2 = _mm_nt(da, w_up, F32, "mm_up_dgrad", after=tok)
    if ex:
        rs_up.pair_sums([du2])
    dh1, dh1b, g_nmlp = _rms_bwd(du2, h1, norm_mlp_w, dh2, "rms_mlp_bwd")

    g_out = _mm_tn(merged, dh1b, 1, BF16, "mm_out_wgrad").reshape(N_DEV, D_MODEL // N_DEV, D_MODEL)
    dmerged = _mm_nt(dh1b, w_out1, F32, "mm_out_dgrad")
    dpa, dpb, dga, dgb = _merge_bwd(dmerged, z, pa, pb, "merge_bwd")
    g_a = _mm_tn(y_a, dpa, N_DEV, BF16, "mm_a_wgrad")
    g_b = _mm_tn(y_b, dpb, N_DEV, BF16, "mm_b_wgrad")
    rs_mix, tok = reduce_scatter([g_a, g_b, g_out], "rs_mix")
    dya = _mm_nt(dpa, w_a, F32, "mm_a_dgrad", after=tok)
    dyb = _mm_nt(dpb, w_b, F32, "mm_b_dgrad", after=tok)
    if ex:
        rs_mix.pair_sums([dya, dyb])
    daq, dak, dav, dbias_t = _attn_bwd(z, bias_t, dyb, "attn_bwd")
    dhq, dhf, dhi, dhg, g_lbl, g_hgw = _hgrn2_bwd(z, lb_logits, hg_norm_w, o_raw, s_all, dya, "hgrn2_bwd")
    dbias_rows = jnp.transpose(dbias_t.reshape(AT_HEADS // 2, BAND, 2, CHUNK), (3, 0, 2, 1)).reshape(
        CHUNK, AT_HEADS, BAND)
    g_rel = _bias_reduce(dbias_rows, "bias_reduce")[:, :N_REL]
    dz = jnp.concatenate([dhq, dhf, dhi, dhg, daq, dak, dav, dga, dgb], axis=1)
    g_in = _mm_tn(u, dz, N_DEV, BF16, "mm_in_wgrad")
    rs_in, tok = reduce_scatter([g_in], "rs_in")
    du = _mm_nt(dz, w_in, F32, "mm_in_dgrad", after=tok)
    if ex:
        rs_in.pair_sums([du])
    grad_x, _, g_nmix = _rms_bwd(du, x, norm_mix_w, dh1, "rms_mix_bwd")

    small = dict(lb_logits=g_lbl, hg_norm_w=g_hgw[0:1], rel_bias=g_rel, norm_mix_w=g_nmix, norm_mlp_w=g_nmlp,
                 norm_final_w=g_nf)
    grads = [rs_in, rs_mix, rs_up, rs_down] if ex else [g_in, g_a, g_b, g_out, g_up, g_down]
    return loss, grad_x, grads, small


def _gather_exchange(shards, mid_step=None):
    n = len(shards)

    def parts(ins, outs, sems):
        send_sems, recv_sems, local_sems = sems
        x, y, c = _position()
        chips = [(1 - x, y), (x, 1 - y), (1 - x, 1 - y)]

        def copy(w, k, block, to, src=None):
            dst = outs[w].at[4 * block[0] + 2 * block[1] + block[2]]
            return pltpu.make_async_remote_copy(
                src_ref=dst if src is None else src, dst_ref=dst,
                send_sem=send_sems.at[w, k], recv_sem=recv_sems.at[w, k], device_id=to, device_id_type=MESH)

        def local(w):
            return pltpu.make_async_copy(ins[w], outs[w].at[4 * x + 2 * y + c], local_sems.at[w])

        return (x, y, c), (x, y, 1 - c), chips, copy, local

    def start(ins, outs, sems):
        me, sibling, chips, copy, local = parts(ins, outs, sems)
        for w in range(n):
            local(w).start()
        for w in range(n):
            copy(w, 0, me, sibling, src=ins[w]).start()
            for j, chip in enumerate(chips):
                copy(w, 1 + j, me, (*chip, me[2]), src=ins[w]).start()

    def mid(ins, outs, sems):
        me, sibling, chips, copy, _ = parts(ins, outs, sems)
        for w in range(n):
            for j, chip in enumerate(chips):
                copy(w, 1 + j, (*chip, me[2]), me).wait_recv()
                copy(w, 4 + j, (*chip, me[2]), sibling).start()

    def end(ins, outs, sems):
        me, sibling, chips, copy, local = parts(ins, outs, sems)
        for w in range(n):
            copy(w, 0, sibling, me).wait_recv()
            for j, chip in enumerate(chips):
                copy(w, 4 + j, (*chip, sibling[2]), me).wait_recv()
        for w in range(n):
            for k in range(7):
                copy(w, k, me, sibling).wait_send()
            local(w).wait()

    return _Exchange(
        shards, [jax.ShapeDtypeStruct((N_DEV,) + s.shape, s.dtype) for s in shards],
        [pltpu.SemaphoreType.DMA((n, 7)), pltpu.SemaphoreType.DMA((n, 7)), pltpu.SemaphoreType.DMA((n,))],
        start, end, mid, mid_step)


def _mm_gathered(u, shard, order, name):
    T, K = u.shape
    _, Nb = shard.shape

    def body(order_ref, u_ref, shard_ref, z_ref, full_ref, wbuf, load_sem, send_sems, recv_sems, local_sem):
        s = pl.program_id(0)
        x, y, c = _position()
        me, sibling = (x, y, c), (x, y, 1 - c)
        chips = [(1 - x, y), (x, 1 - y), (1 - x, 1 - y)]

        def copy(k, block, to, src=None):
            dst = full_ref.at[4 * block[0] + 2 * block[1] + block[2]]
            return pltpu.make_async_remote_copy(
                src_ref=dst if src is None else src, dst_ref=dst,
                send_sem=send_sems.at[k], recv_sem=recv_sems.at[k], device_id=to, device_id_type=MESH)

        @pl.when(s == 0)
        def _():
            local = pltpu.make_async_copy(shard_ref, full_ref.at[4 * x + 2 * y + c], local_sem)
            local.start()
            copy(0, me, sibling, src=shard_ref).start()
            for j, chip in enumerate(chips):
                copy(1 + j, me, (*chip, c), src=shard_ref).start()
            local.wait()

        @pl.when(s == 1)
        def _():
            copy(0, sibling, me).wait_recv()

        for j, chip in enumerate(chips):
            direct, passed = ((2, 4), (3, 5), (6, 7))[j]

            @pl.when(s == direct)
            def _(j=j, chip=chip):
                copy(1 + j, (*chip, c), me).wait_recv()
                copy(4 + j, (*chip, c), sibling).start()

            @pl.when(s == passed)
            def _(j=j, chip=chip):
                copy(4 + j, (*chip, 1 - c), me).wait_recv()

        load = pltpu.make_async_copy(full_ref.at[order_ref[s]], wbuf, load_sem)
        load.start()
        load.wait()
        z_ref[...] = jnp.dot(u_ref[...], wbuf[...], preferred_element_type=F32)

        @pl.when(s == N_DEV - 1)
        def _():
            for k in range(7):
                copy(k, me, sibling).wait_send()

    return pl.pallas_call(
        body, name=name,
        grid_spec=pltpu.PrefetchScalarGridSpec(
            num_scalar_prefetch=1, grid=(N_DEV,),
            in_specs=[pl.BlockSpec((T, K), lambda s, order: (0, 0)), ANY],
            out_specs=[pl.BlockSpec((T, Nb), lambda s, order: (0, order[s])), ANY],
            scratch_shapes=[pltpu.VMEM((K, Nb), BF16), pltpu.SemaphoreType.DMA,
                            pltpu.SemaphoreType.DMA((7,)), pltpu.SemaphoreType.DMA((7,)), pltpu.SemaphoreType.DMA]),
        out_shape=[jax.ShapeDtypeStruct((T, N_DEV * Nb), F32), jax.ShapeDtypeStruct((N_DEV, K, Nb), BF16)],
        compiler_params=_cparams(("arbitrary",)),
    )(order, u, shard)


def _gather_order():
    x, y, c = _position()
    chips = [(1 - x, y), (x, 1 - y), (1 - x, 1 - y)]
    ids = [4 * x + 2 * y + c, 4 * x + 2 * y + (1 - c)]
    ids += [4 * cx + 2 * cy + c for cx, cy in chips[:2]] + [4 * cx + 2 * cy + (1 - c) for cx, cy in chips[:2]]
    ids += [4 * chips[2][0] + 2 * chips[2][1] + c, 4 * chips[2][0] + 2 * chips[2][1] + (1 - c)]
    return jnp.stack(ids).astype(jnp.int32)


def _run_exchange(comm, name):
    n_i, n_o = len(comm.arrays), len(comm.out_shape)

    def body(*refs):
        ins, outs, sems = refs[:n_i], refs[n_i:n_i + n_o], refs[n_i + n_o:]
        comm.start(ins, outs, sems)
        if comm.mid is not None:
            comm.mid(ins, outs, sems)
        comm.end(ins, outs, sems)

    return pl.pallas_call(
        body, name=name, in_specs=[ANY] * n_i, out_specs=[ANY] * n_o, out_shape=comm.out_shape,
        scratch_shapes=comm.scratch)(*comm.arrays)


def _exchange_sibling(grads, name):
    n = len(grads)

    def body(*refs):
        ins, outs = refs[:n], refs[n:2 * n]
        send_sems, recv_sems = refs[2 * n:]
        x, y, c = _position()
        copies = []
        for w in range(n):
            for s in range(N_CHIP):
                cp = pltpu.make_async_remote_copy(
                    src_ref=ins[w].at[2 * s + (1 - c)], dst_ref=outs[w].at[s],
                    send_sem=send_sems.at[w, s], recv_sem=recv_sems.at[w, s],
                    device_id=(x, y, 1 - c), device_id_type=MESH)
                cp.start()
                copies.append(cp)
        for cp in copies:
            cp.wait()

    return pl.pallas_call(
        body, name=name,
        in_specs=[ANY] * n, out_specs=[ANY] * n,
        out_shape=[jax.ShapeDtypeStruct((N_CHIP,) + g.shape[1:], g.dtype) for g in grads],
        scratch_shapes=[pltpu.SemaphoreType.DMA((n, N_CHIP)), pltpu.SemaphoreType.DMA((n, N_CHIP))],
    )(*grads)


def _pair_sum(g, land, parity, name):
    _, R, C = g.shape
    tr = _pick(R, (512, 256))

    def body(par_ref, g_ref, l_ref, o_ref):
        o_ref[...] = (g_ref[...].astype(F32) + l_ref[...].astype(F32)).astype(BF16)

    return pl.pallas_call(
        body, name=name,
        grid_spec=pltpu.PrefetchScalarGridSpec(
            num_scalar_prefetch=1, grid=(N_CHIP, R // tr),
            in_specs=[pl.BlockSpec((None, tr, C), lambda s, i, par: (2 * s + par[0], i, 0)),
                      pl.BlockSpec((None, tr, C), lambda s, i, par: (s, i, 0))],
            out_specs=pl.BlockSpec((None, tr, C), lambda s, i, par: (s, i, 0))),
        out_shape=jax.ShapeDtypeStruct((N_CHIP, R, C), BF16),
        compiler_params=_cparams(("parallel", "parallel")),
    )(parity, g, land)


def _scatter_exchange(partials):
    n = len(partials)

    def copies(ins, outs, sems):
        send_sems, recv_sems, local_sems = sems
        x, y, c = _position()
        chips = [(1 - x, y), (x, 1 - y), (1 - x, 1 - y)]
        my_slot = 2 * x + y
        local = [pltpu.make_async_copy(ins[w].at[my_slot], outs[w].at[my_slot], local_sems.at[w]) for w in range(n)]
        remote = [pltpu.make_async_remote_copy(
            src_ref=ins[w].at[2 * chip[0] + chip[1]], dst_ref=outs[w].at[my_slot],
            send_sem=send_sems.at[w, j], recv_sem=recv_sems.at[w, j], device_id=(*chip, c), device_id_type=MESH)
            for w in range(n) for j, chip in enumerate(chips)]
        return local, remote

    def start(ins, outs, sems):
        local, remote = copies(ins, outs, sems)
        for cp in local + remote:
            cp.start()

    def end(ins, outs, sems):
        local, remote = copies(ins, outs, sems)
        for cp in remote + local:
            cp.wait()

    return _Exchange(
        partials, [jax.ShapeDtypeStruct(p.shape, p.dtype) for p in partials],
        [pltpu.SemaphoreType.DMA((n, 3)), pltpu.SemaphoreType.DMA((n, 3)), pltpu.SemaphoreType.DMA((n,))],
        start, end)


HBM = pl.BlockSpec(memory_space=pltpu.HBM)
SEM = pl.BlockSpec(memory_space=pltpu.SEMAPHORE)
DATAFLOW = pltpu.SideEffectType.DATAFLOW_SIDE_EFFECTING


def _scatter_copies(ins, lands, send_sems, recv_sems):
    x, y, c = _position()
    chips = [(1 - x, y), (x, 1 - y), (1 - x, 1 - y)]
    return [pltpu.make_async_remote_copy(
        src_ref=ins[w].at[2 * chip[0] + chip[1]], dst_ref=lands[w].at[2 * x + y],
        send_sem=send_sems[3 * w + j], recv_sem=recv_sems[3 * w + j], device_id=(*chip, c), device_id_type=MESH)
        for w in range(len(ins)) for j, chip in enumerate(chips)]


def _scatter_start(partials, name):
    n = len(partials)

    def body(*refs):
        ins, lands = refs[:n], refs[n:2 * n]
        sems = refs[4 * n:10 * n]
        for cp in _scatter_copies(ins, lands, sems[:3 * n], sems[3 * n:]):
            cp.start()
        refs[-1][...] = jnp.zeros_like(refs[-1])

    def in_hbm(a):
        return pltpu.with_memory_space_constraint(a, pltpu.HBM)

    bufs = tuple(pltpu.HBM(p.shape, p.dtype) for p in partials)
    outs = pl.pallas_call(
        body, name=name,
        out_shape=bufs + bufs + (pltpu.SemaphoreType.DMA(()),) * (6 * n) + (jax.ShapeDtypeStruct((8, 128), F32),),
        in_specs=[HBM] * (2 * n),
        out_specs=(HBM,) * (2 * n) + (SEM,) * (6 * n) + (pl.BlockSpec(memory_space=pltpu.VMEM),),
        input_output_aliases={i: i for i in range(2 * n)},
        compiler_params=pltpu.CompilerParams(has_side_effects=DATAFLOW),
    )(*[in_hbm(p) for p in partials], *[in_hbm(lax.empty(p.shape, p.dtype)) for p in partials])
    return list(outs[:-1]), outs[-1]


def _scatter_wait(handle, after, name):
    n = len(handle) // 8
    bufs, sems = handle[:2 * n], handle[2 * n:]

    def body(*refs):
        ins, lands = refs[:n], refs[n:2 * n]
        sems = refs[2 * n:8 * n]
        for cp in _scatter_copies(ins, lands, sems[:3 * n], sems[3 * n:]):
            cp.wait_send()
            cp.wait_recv()

    outs = pl.pallas_call(
        body, name=name,
        out_shape=tuple(pltpu.HBM(b.shape, b.dtype) for b in bufs),
        in_specs=[HBM] * (2 * n) + [SEM] * (6 * n) + [ANY] * len(after), out_specs=(HBM,) * (2 * n),
        input_output_aliases={i: i for i in range(2 * n)},
        compiler_params=pltpu.CompilerParams(has_side_effects=DATAFLOW),
    )(*bufs, *sems, *after)
    return list(outs[:n]), list(outs[n:])


def _split_call(name, bufs, waits=(), starts=None, after=()):
    nb = len(bufs)
    n_new = starts[1] if starts else 0
    wait_sems = [s for w in waits for s in (*w[1], *w[2])]

    def body(*refs):
        b, pos = refs[:nb], nb
        for plan, ss, _, send_idx, recv_idx in waits:
            k = len(ss)
            copies = plan(b, refs[pos:pos + k], refs[pos + k:pos + 2 * k])
            pos += 2 * k
            for i in recv_idx:
                copies[i].wait_recv()
            for i in send_idx:
                copies[i].wait_send()
        outs = refs[pos + len(after):]
        if starts:
            for cp in starts[0](b, outs[nb:nb + n_new], outs[nb + n_new:nb + 2 * n_new]):
                cp.start()
        outs[-1][...] = jnp.zeros_like(outs[-1])

    res = pl.pallas_call(
        body, name=name,
        out_shape=tuple(pltpu.HBM(a.shape, a.dtype) for a in bufs) + (pltpu.SemaphoreType.DMA(()),) * (2 * n_new)
        + (jax.ShapeDtypeStruct((8, 128), F32),),
        in_specs=[HBM] * nb + [SEM] * len(wait_sems) + [ANY] * len(after),
        out_specs=(HBM,) * nb + (SEM,) * (2 * n_new) + (pl.BlockSpec(memory_space=pltpu.VMEM),),
        input_output_aliases={i: i for i in range(nb)},
        compiler_params=pltpu.CompilerParams(has_side_effects=DATAFLOW),
    )(*bufs, *wait_sems, *after)
    return list(res[:nb]), list(res[nb:nb + n_new]), list(res[nb + n_new:nb + 2 * n_new]), res[-1]


def _in_hbm(a):
    return pltpu.with_memory_space_constraint(a, pltpu.HBM)


def _remote(src, dst, send_sem, recv_sem, to):
    return pltpu.make_async_remote_copy(src_ref=src, dst_ref=dst, send_sem=send_sem, recv_sem=recv_sem,
                                        device_id=to, device_id_type=MESH)


def _other_chips():
    x, y, _ = _position()
    return [(1 - x, y), (x, 1 - y), (1 - x, 1 - y)]


def _plan_gather_first(n):
    def plan(b, ss, rs):
        x, y, c = _position()
        to = [(x, y, 1 - c)] + [(*chip, c) for chip in _other_chips()]
        return [_remote(b[w], b[n + w].at[4 * x + 2 * y + c], ss[4 * w + k], rs[4 * w + k], to[k])
                for w in range(n) for k in range(4)]
    return plan, 4 * n


def _plan_gather_pass(n):
    def plan(b, ss, rs):
        x, y, c = _position()
        copies = []
        for w in range(n):
            for j, chip in enumerate(_other_chips()):
                blk = b[n + w].at[4 * chip[0] + 2 * chip[1] + c]
                copies.append(_remote(blk, blk, ss[3 * w + j], rs[3 * w + j], (x, y, 1 - c)))
        return copies
    return plan, 3 * n


def _plan_sibling(n):
    def plan(b, ss, rs):
        x, y, c = _position()
        return [_remote(b[w].at[2 * s + (1 - c)], b[n + w].at[s], ss[4 * w + s], rs[4 * w + s], (x, y, 1 - c))
                for w in range(n) for s in range(N_CHIP)]
    return plan, 4 * n


def _plan_scatter(n):
    def plan(b, ss, rs):
        x, y, c = _position()
        return [_remote(b[w].at[2 * chip[0] + chip[1]], b[n + w].at[2 * x + y], ss[3 * w + j], rs[3 * w + j],
                        (*chip, c))
                for w in range(n) for j, chip in enumerate(_other_chips())]
    return plan, 3 * n


class _Gather:
    def __init__(self, shards, after, name):
        self.n, self.name = len(shards), name
        x, y, c = _position()
        placed = [lax.dynamic_update_index_in_dim(lax.empty((N_DEV,) + s.shape, s.dtype), s, 4 * x + 2 * y + c, 0)
                  for s in shards]
        bufs, self.ss, self.rs, self.token = _split_call(
            name + "_start", [_in_hbm(a) for a in list(shards) + placed], starts=_plan_gather_first(self.n),
            after=after)
        self.shards, self.fulls = bufs[:self.n], bufs[self.n:]
        self.passed = {}

    def _sub(self, ids, sems, per):
        return [sems[per * w + k] for w in ids for k in range(per)]

    def pass_on(self, ids, after, tag):
        m = len(ids)
        first = (_plan_gather_first(m)[0], self._sub(ids, self.ss, 4), self._sub(ids, self.rs, 4),
                 [], [4 * i + k for i in range(m) for k in (1, 2, 3)])
        bufs, ss, rs, token = _split_call(
            "%s_pass_%s" % (self.name, tag), [self.shards[w] for w in ids] + [self.fulls[w] for w in ids],
            waits=[first], starts=_plan_gather_pass(m), after=after)
        for i, w in enumerate(ids):
            self.shards[w], self.fulls[w] = bufs[i], bufs[m + i]
        self.passed[tuple(ids)] = (ss, rs)
        return token

    def finish(self, ids, after, tag):
        m = len(ids)
        ss2, rs2 = self.passed[tuple(ids)]
        first = (_plan_gather_first(m)[0], self._sub(ids, self.ss, 4), self._sub(ids, self.rs, 4),
                 list(range(4 * m)), [4 * i for i in range(m)])
        passed = (_plan_gather_pass(m)[0], ss2, rs2, list(range(3 * m)), list(range(3 * m)))
        bufs, _, _, _ = _split_call(
            "%s_finish_%s" % (self.name, tag), [self.shards[w] for w in ids] + [self.fulls[w] for w in ids],
            waits=[first, passed], after=after)
        return bufs[m:]


class _ReduceScatter:
    def __init__(self, grads, parity, name):
        self.n, self.name, self.parity = len(grads), name, parity
        lands = [lax.empty((N_CHIP,) + g.shape[1:], g.dtype) for g in grads]
        self.bufs, self.ss, self.rs, self.token = _split_call(
            name + "_sibling_start", [_in_hbm(a) for a in list(grads) + lands], starts=_plan_sibling(self.n))

    def pair_sums(self, after):
        n = self.n
        bufs, _, _, _ = _split_call(
            self.name + "_sibling_wait", self.bufs,
            waits=[(_plan_sibling(n)[0], self.ss, self.rs, list(range(4 * n)), list(range(4 * n)))], after=after)
        sums = [_pair_sum(bufs[w], bufs[n + w], self.parity, "%s_pair_sum_%d" % (self.name, w)) for w in range(n)]
        lands = [lax.empty(s.shape, s.dtype) for s in sums]
        self.bufs, self.ss, self.rs, token = _split_call(
            self.name + "_scatter_start", [_in_hbm(a) for a in sums + lands], starts=_plan_scatter(n))
        return token

    def finish(self, after):
        n = self.n
        bufs, _, _, _ = _split_call(
            self.name + "_scatter_wait", self.bufs,
            waits=[(_plan_scatter(n)[0], self.ss, self.rs, list(range(3 * n)), list(range(3 * n)))], after=after)
        return bufs[:n], bufs[n:]


class _Exchanges:
    def __init__(self, parity, order):
        self.parity, self.order = parity, order


def _gather_small(packed, name):
    R = packed.shape[0]

    def body(x_ref, out_ref, send_sems, recv_sems):
        x, y, c = _position()
        me = 4 * x + 2 * y + c
        out_ref[me] = x_ref[...]
        copies = []
        for k in range(1, N_DEV):
            to = (x ^ ((k >> 2) & 1), y ^ ((k >> 1) & 1), c ^ (k & 1))
            cp = pltpu.make_async_remote_copy(
                src_ref=x_ref, dst_ref=out_ref.at[me],
                send_sem=send_sems.at[k], recv_sem=recv_sems.at[k], device_id=to, device_id_type=MESH)
            cp.start()
            copies.append((k, to, cp))
        for k, to, cp in copies:
            cp.wait_send()
            pltpu.make_async_remote_copy(
                src_ref=x_ref, dst_ref=out_ref.at[4 * to[0] + 2 * to[1] + to[2]],
                send_sem=send_sems.at[k], recv_sem=recv_sems.at[k], device_id=to, device_id_type=MESH).wait_recv()

    return pl.pallas_call(
        body, name=name,
        in_specs=[pl.BlockSpec(memory_space=pltpu.VMEM)], out_specs=pl.BlockSpec(memory_space=pltpu.VMEM),
        out_shape=jax.ShapeDtypeStruct((N_DEV, R, 128), F32),
        scratch_shapes=[pltpu.SemaphoreType.DMA((N_DEV,)), pltpu.SemaphoreType.DMA((N_DEV,))],
    )(packed)


def _adamw_math(w, g, m, v):
    m = ADAM_B1 * m + (1.0 - ADAM_B1) * g
    v = ADAM_B2 * v + (1.0 - ADAM_B2) * (g * g)
    m_hat = m / (1.0 - ADAM_B1 ** ADAM_STEP)
    v_hat = v / (1.0 - ADAM_B2 ** ADAM_STEP)
    delta = -ADAM_LR * (m_hat / (jnp.sqrt(v_hat) + ADAM_EPS) + ADAM_WD * w)
    return delta, m, v


def _adamw_big(w, m, v, parts, name):
    R, C = w.shape
    tr = _pick(R, (256,))

    def body(w_ref, m_ref, v_ref, p_ref, g_ref, d_ref, nm_ref, nv_ref):
        g = p_ref[0].astype(F32)
        for s in range(1, N_CHIP):
            g = g + p_ref[s].astype(F32)
        d, nm, nv = _adamw_math(w_ref[...], g, m_ref[...], v_ref[...])
        g_ref[...] = g
        d_ref[...] = d
        nm_ref[...] = nm
        nv_ref[...] = nv

    blk = pl.BlockSpec((tr, C), lambda i: (i, 0))
    out = jax.ShapeDtypeStruct((R, C), F32)
    return pl.pallas_call(
        body, name=name, grid=(R // tr,),
        in_specs=[blk, blk, blk, pl.BlockSpec((N_CHIP, tr, C), lambda i: (0, i, 0))],
        out_specs=[blk, blk, blk, blk], out_shape=[out, out, out, out],
        compiler_params=_cparams(("parallel",)),
    )(w, m, v, parts)


def _adamw_big_landed(w, m, v, parts, lands, slot, name):
    R, C = w.shape
    tr = _pick(R, (256,))

    def body(slot_ref, w_ref, m_ref, v_ref, own_ref, l1_ref, l2_ref, l3_ref, g_ref, d_ref, nm_ref, nv_ref):
        g = own_ref[...].astype(F32)
        for ref in (l1_ref, l2_ref, l3_ref):
            g = g + ref[...].astype(F32)
        d, nm, nv = _adamw_math(w_ref[...], g, m_ref[...], v_ref[...])
        g_ref[...] = g
        d_ref[...] = d
        nm_ref[...] = nm
        nv_ref[...] = nv

    blk = pl.BlockSpec((tr, C), lambda i, slot: (i, 0))

    def chip(k):
        return pl.BlockSpec((None, tr, C), lambda i, slot: ((slot[0] + k) % N_CHIP, i, 0))

    out = jax.ShapeDtypeStruct((R, C), F32)
    return pl.pallas_call(
        body, name=name,
        grid_spec=pltpu.PrefetchScalarGridSpec(
            num_scalar_prefetch=1, grid=(R // tr,),
            in_specs=[blk, blk, blk, chip(0), chip(1), chip(2), chip(3)],
            out_specs=[blk, blk, blk, blk]),
        out_shape=[out, out, out, out],
        compiler_params=_cparams(("parallel",)),
    )(slot, w, m, v, parts, lands, lands, lands)


def _adamw_small(w, m, v, gathered, name):
    R = w.shape[0]

    def body(w_ref, m_ref, v_ref, p_ref, g_ref, d_ref, nm_ref, nv_ref):
        g = p_ref[0]
        for s in range(1, N_DEV):
            g = g + p_ref[s]
        d, nm, nv = _adamw_math(w_ref[...], g, m_ref[...], v_ref[...])
        g_ref[...] = g
        d_ref[...] = d
        nm_ref[...] = nm
        nv_ref[...] = nv

    out = jax.ShapeDtypeStruct((R, 128), F32)
    return pl.pallas_call(
        body, name=name, out_shape=[out, out, out, out],
    )(w, m, v, gathered)


SMALL_NAMES = ("lb_logits", "hg_norm_w", "rel_bias", "norm_mix_w", "norm_mlp_w", "norm_final_w")
SMALL_SHAPES = {"lb_logits": (2, HG_WIDTH), "hg_norm_w": (1, HG_DK), "rel_bias": (AT_HEADS, N_REL_PAD),
                "norm_mix_w": (1, D_MODEL), "norm_mlp_w": (1, D_MODEL), "norm_final_w": (1, D_MODEL)}


def _pack_small(parts):
    rows = []
    for nme in SMALL_NAMES:
        p = parts[nme]
        if nme == "rel_bias":
            p = jnp.pad(p, ((0, 0), (0, N_REL_PAD - N_REL)))
        rows.append(p.reshape(-1, 128))
    flat = jnp.concatenate(rows, axis=0)
    return jnp.pad(flat, ((0, SMALL_ROWS - flat.shape[0]), (0, 0)))


def _unpack_small(packed):
    out, at = {}, 0
    for nme in SMALL_NAMES:
        shp = SMALL_SHAPES[nme]
        nrow = shp[0] * shp[1] // 128
        p = packed[at:at + nrow].reshape(shp)
        at += nrow
        out[nme] = p[:, :N_REL] if nme == "rel_bias" else p
    return out


BIG_NAMES = ("w_in", "w_branch_a", "w_branch_b", "w_out", "w_up", "w_down")


def kernel(x, w_in, lb_logits, hg_norm_w, rel_bias, w_branch_a, w_branch_b, w_out, norm_mix_w, norm_mlp_w, w_up, w_down, norm_final_w, loss_target, m_w_in, m_lb_logits, m_hg_norm_w, m_rel_bias, m_w_branch_a, m_w_branch_b, m_w_out, m_norm_mix_w, m_norm_mlp_w, m_w_up, m_w_down, m_norm_final_w, v_w_in, v_lb_logits, v_hg_norm_w, v_rel_bias, v_w_branch_a, v_w_branch_b, v_w_out, v_norm_mix_w, v_norm_mlp_w, v_w_up, v_w_down, v_norm_final_w):
    big_w = [w_in[0], w_branch_a[0], w_branch_b[0], w_out[0], w_up[0], w_down[0]]
    big_m = [m_w_in[0], m_w_branch_a[0], m_w_branch_b[0], m_w_out[0], m_w_up[0], m_w_down[0]]
    big_v = [v_w_in[0], v_w_branch_a[0], v_w_branch_b[0], v_w_out[0], v_w_up[0], v_w_down[0]]

    shards = [w.astype(BF16) for w in big_w]
    parity = lax.axis_index("c").astype(jnp.int32).reshape(1)
    loss_part, grad_x, chip_parts, small = _local_step(
        x[0], loss_target[0], lb_logits, hg_norm_w, rel_bias[0], norm_mix_w, norm_mlp_w,
        norm_final_w.reshape(1, D_MODEL), shards[0], shards[1:], _Exchanges(parity, _gather_order()))
    loss = lax.psum(loss_part[0, 0], ("x", "y", "c"))
    rs_in, rs_mix, rs_up, rs_down = chip_parts
    slot = (2 * lax.axis_index("x") + lax.axis_index("y")).astype(jnp.int32).reshape(1)
    big = {}

    def finish(rs, names, after):
        sums, lands = rs.finish(after)
        for nme, own, land in zip(names, sums, lands):
            i = BIG_NAMES.index(nme)
            big[nme] = _adamw_big_landed(big_w[i], big_m[i], big_v[i], own, land, slot, "adamw_" + nme)
        return [big[nme][1] for nme in names]

    done = finish(rs_down, ["w_down"], [grad_x])
    done = finish(rs_up, ["w_up"], done)
    done = finish(rs_mix, ["w_branch_a", "w_branch_b", "w_out"], done)

    sw = dict(lb_logits=lb_logits, hg_norm_w=hg_norm_w, rel_bias=rel_bias[0], norm_mix_w=norm_mix_w,
              norm_mlp_w=norm_mlp_w, norm_final_w=norm_final_w.reshape(1, D_MODEL))
    sm = dict(lb_logits=m_lb_logits, hg_norm_w=m_hg_norm_w, rel_bias=m_rel_bias[0], norm_mix_w=m_norm_mix_w,
              norm_mlp_w=m_norm_mlp_w, norm_final_w=m_norm_final_w.reshape(1, D_MODEL))
    sv = dict(lb_logits=v_lb_logits, hg_norm_w=v_hg_norm_w, rel_bias=v_rel_bias[0], norm_mix_w=v_norm_mix_w,
              norm_mlp_w=v_norm_mlp_w, norm_final_w=v_norm_final_w.reshape(1, D_MODEL))
    gathered = _gather_small(_pack_small(small), "gather_small")
    small_packed = _adamw_small(_pack_small(sw), _pack_small(sm), _pack_small(sv), gathered, "adamw_small")
    small_out = [_unpack_small(p) for p in small_packed]

    finish(rs_in, ["w_in"], done + [small_packed[0]])

    def leaf(kind, nme):
        if nme in BIG_NAMES:
            return big[nme][kind][None]
        p = small_out[kind][nme]
        if nme == "rel_bias":
            return p[None]
        if nme == "norm_final_w":
            return p.reshape(D_MODEL)
        return p

    order = ("w_in", "lb_logits", "hg_norm_w", "rel_bias", "w_branch_a", "w_branch_b", "w_out", "norm_mix_w",
             "norm_mlp_w", "w_up", "w_down", "norm_final_w")
    outs = [loss, grad_x[None]]
    for kind in range(4):
        outs += [leaf(kind, nme) for nme in order]
    return tuple(outs)
```

```python
import functools

import jax
import jax.numpy as jnp
from jax import lax
from jax.experimental import pallas as pl
from jax.experimental.pallas import tpu as pltpu

F32 = jnp.float32
BF16 = jnp.bfloat16
HIGHEST = lax.Precision.HIGHEST
MESH = pl.DeviceIdType.MESH

D_MODEL = 2048
HG_HEADS = 8
HG_DK = 128
HG_WIDTH = 1024
AT_HEADS = 16
AT_DH = 64
AT_WIDTH = 1024
CHUNK = 64
LEFT_CHUNKS = 8
BAND = (LEFT_CHUNKS + 1) * CHUNK
PAD = LEFT_CHUNKS * CHUNK
REL_CLIP = 256
N_REL = 2 * REL_CLIP + 1
N_REL_PAD = 640
D_FF = 4 * D_MODEL
D_IN = 4 * HG_WIDTH + 3 * AT_WIDTH + 2 * D_MODEL
EPS = 1e-6
N_DEV = 8
N_CHIP = 4

ADAM_LR = 0.001
ADAM_B1 = 0.9
ADAM_B2 = 0.999
ADAM_EPS = 1e-08
ADAM_WD = 0.01
ADAM_STEP = 10

COL_HQ, COL_HF, COL_HI, COL_HG = 0, 8, 16, 24
COL_AQ, COL_AK, COL_AV = 32, 40, 48
COL_GATE_A, COL_GATE_B = 7, 9

VMEM_LIMIT = 56 * 1024 * 1024
SMALL_ROWS = 152


def _cparams(sem=None, **kw):
    if sem is not None:
        kw["dimension_semantics"] = sem
    return pltpu.CompilerParams(vmem_limit_bytes=VMEM_LIMIT, **kw)


def _pick(n, cands):
    for c in cands:
        if n % c == 0:
            return c
    return n


def _sigmoid(x):
    return 1.0 / (1.0 + jnp.exp(-x))


ANY = pl.BlockSpec(memory_space=pl.ANY)


def _position():
    return lax.axis_index("x"), lax.axis_index("y"), lax.axis_index("c")


def _call(body, args, *, name, grid, in_specs, out_specs, out_shape, scratch_shapes=(), sem=None, after=()):
    n_in = len(args)

    def ordered(*refs):
        body(*refs[:n_in], *refs[n_in + len(after):])

    return list(pl.pallas_call(
        ordered if after else body, name=name, grid=grid, in_specs=list(in_specs) + [ANY] * len(after),
        out_specs=out_specs, out_shape=out_shape, scratch_shapes=list(scratch_shapes),
        compiler_params=_cparams(sem))(*args, *after))


def _mm_nn(a, wb, out_dtype, name, after=()):
    M, K = a.shape
    NB, K2, Nb = wb.shape
    assert K == K2
    tm = min(M, 1024)
    tk = min(K, 2048)
    tn = _pick(Nb, (512, 1408, 256))
    nk = K // tk
    nn = Nb // tn

    def body(a_ref, b_ref, o_ref, *acc):
        part = jnp.dot(a_ref[...], b_ref[...], preferred_element_type=F32)
        if nk == 1:
            o_ref[...] = part.astype(out_dtype)
        else:
            acc_ref, = acc
            k = pl.program_id(3)

            @pl.when(k == 0)
            def _():
                acc_ref[...] = part

            @pl.when(k > 0)
            def _():
                acc_ref[...] += part

            @pl.when(k == nk - 1)
            def _():
                o_ref[...] = acc_ref[...].astype(out_dtype)

    out, = _call(
        body, (a, wb), name=name, grid=(M // tm, NB, nn, nk),
        in_specs=[pl.BlockSpec((tm, tk), lambda m, j, n, k: (m, k)),
                  pl.BlockSpec((None, tk, tn), lambda m, j, n, k: (j, k, n))],
        out_specs=[pl.BlockSpec((tm, tn), lambda m, j, n, k: (m, j * nn + n))],
        out_shape=[jax.ShapeDtypeStruct((M, NB * Nb), out_dtype)],
        scratch_shapes=[] if nk == 1 else [pltpu.VMEM((tm, tn), F32)],
        sem=("parallel", "parallel", "parallel", "arbitrary"), after=after)
    return out


def _mm_nt(a, wb, out_dtype, name, after=()):
    M, N = a.shape
    NB, K, Nb = wb.shape
    assert N == NB * Nb
    tm = min(M, 1024)
    tko = _pick(K, (1024,))
    tc = _pick(Nb, (1024, 1408, 256))
    nc = Nb // tc
    nsteps = NB * nc

    def body(a_ref, b_ref, o_ref, acc_ref):
        step = pl.program_id(2) * nc + pl.program_id(3)
        part = lax.dot_general(a_ref[...], b_ref[...], (((1,), (1,)), ((), ())), preferred_element_type=F32)

        @pl.when(step == 0)
        def _():
            acc_ref[...] = part

        @pl.when(step > 0)
        def _():
            acc_ref[...] += part

        @pl.when(step == nsteps - 1)
        def _():
            o_ref[...] = acc_ref[...].astype(out_dtype)

    out, = _call(
        body, (a, wb), name=name,
        grid=(M // tm, K // tko, NB, nc),
        in_specs=[pl.BlockSpec((tm, tc), lambda m, ko, j, c: (m, j * nc + c)),
                  pl.BlockSpec((None, tko, tc), lambda m, ko, j, c: (j, ko, c))],
        out_specs=[pl.BlockSpec((tm, tko), lambda m, ko, j, c: (m, ko))],
        out_shape=[jax.ShapeDtypeStruct((M, K), out_dtype)],
        scratch_shapes=[pltpu.VMEM((tm, tko), F32)],
        sem=("parallel", "parallel", "arbitrary", "arbitrary"), after=after)
    return out


def _mm_tn(a, g, nb, out_dtype, name):
    M, Ka = a.shape
    M2, N = g.shape
    assert M == M2 and N % nb == 0
    Nb = N // nb
    tka = _pick(Ka, (1024,))
    tn = _pick(Nb, (512, 1408, 256))
    nn = Nb // tn

    def body(a_ref, g_ref, o_ref):
        o_ref[...] = lax.dot_general(a_ref[...], g_ref[...], (((0,), (0,)), ((), ())),
                                     preferred_element_type=F32).astype(out_dtype)

    return pl.pallas_call(
        body, name=name,
        grid=(Ka // tka, nb, nn),
        in_specs=[pl.BlockSpec((M, tka), lambda ka, j, n: (0, ka)),
                  pl.BlockSpec((M, tn), lambda ka, j, n: (0, j * nn + n))],
        out_specs=pl.BlockSpec((None, tka, tn), lambda ka, j, n: (j, ka, n)),
        out_shape=jax.ShapeDtypeStruct((nb, Ka, Nb), out_dtype),
        compiler_params=_cparams(("parallel", "parallel", "parallel")),
    )(a, g)


ROW_TILE = 256


def _rms_fwd(x, w, name):
    T, Dm = x.shape

    def body(x_ref, w_ref, u_ref):
        xv = x_ref[...]
        r = lax.rsqrt(jnp.mean(xv * xv, axis=-1, keepdims=True) + EPS)
        u_ref[...] = (xv * r * w_ref[...]).astype(BF16)

    return pl.pallas_call(
        body, name=name, grid=(T // ROW_TILE,),
        in_specs=[pl.BlockSpec((ROW_TILE, Dm), lambda i: (i, 0)), pl.BlockSpec((1, Dm), lambda i: (0, 0))],
        out_specs=pl.BlockSpec((ROW_TILE, Dm), lambda i: (i, 0)),
        out_shape=jax.ShapeDtypeStruct((T, Dm), BF16),
        compiler_params=_cparams(("parallel",)),
    )(x, w)


def _resid_rms_fwd(x, mix, w, name):
    T, Dm = x.shape

    def body(x_ref, m_ref, w_ref, h_ref, u_ref):
        h = x_ref[...] + m_ref[...]
        h_ref[...] = h
        r = lax.rsqrt(jnp.mean(h * h, axis=-1, keepdims=True) + EPS)
        u_ref[...] = (h * r * w_ref[...]).astype(BF16)

    row = pl.BlockSpec((ROW_TILE, Dm), lambda i: (i, 0))
    return pl.pallas_call(
        body, name=name, grid=(T // ROW_TILE,),
        in_specs=[row, row, pl.BlockSpec((1, Dm), lambda i: (0, 0))],
        out_specs=[row, row],
        out_shape=[jax.ShapeDtypeStruct((T, Dm), F32), jax.ShapeDtypeStruct((T, Dm), BF16)],
        compiler_params=_cparams(("parallel",)),
    )(x, mix, w)


def _loss_head(h1, mlp, wf, target, name):
    T, Dm = h1.shape

    def body(h_ref, m_ref, w_ref, t_ref, loss_ref, dh_ref, dhb_ref, dw_ref):
        i = pl.program_id(0)
        h = h_ref[...] + m_ref[...]
        r = lax.rsqrt(jnp.mean(h * h, axis=-1, keepdims=True) + EPS)
        xh = h * r
        wv = w_ref[...]
        e = xh * wv - t_ref[...]
        part = 0.5 * jnp.sum(jnp.mean(e * e, axis=-1, keepdims=True), axis=0, keepdims=True)
        dy = e * (1.0 / Dm)
        dw = jnp.sum(dy * xh, axis=0, keepdims=True)
        gy = dy * wv
        dh = r * (gy - xh * jnp.mean(gy * xh, axis=-1, keepdims=True))
        dh_ref[...] = dh
        dhb_ref[...] = dh.astype(BF16)

        @pl.when(i == 0)
        def _():
            loss_ref[...] = jnp.zeros_like(loss_ref)
            dw_ref[...] = jnp.zeros_like(dw_ref)

        loss_ref[...] += jnp.broadcast_to(part, loss_ref.shape)
        dw_ref[...] += dw

    row = pl.BlockSpec((ROW_TILE, Dm), lambda i: (i, 0))
    vec = pl.BlockSpec((1, Dm), lambda i: (0, 0))
    return pl.pallas_call(
        body, name=name, grid=(T // ROW_TILE,),
        in_specs=[row, row, vec, row],
        out_specs=[pl.BlockSpec((8, 128), lambda i: (0, 0)), row, row, vec],
        out_shape=[jax.ShapeDtypeStruct((8, 128), F32), jax.ShapeDtypeStruct((T, Dm), F32),
                   jax.ShapeDtypeStruct((T, Dm), BF16), jax.ShapeDtypeStruct((1, Dm), F32)],
        compiler_params=_cparams(("arbitrary",)),
    )(h1, mlp, wf, target)


def _rms_bwd(dyn, x, w, dres, name, after=()):
    T, Dm = x.shape

    def body(g_ref, x_ref, w_ref, r_ref, dx_ref, dxb_ref, dw_ref):
        i = pl.program_id(0)
        xv = x_ref[...]
        r = lax.rsqrt(jnp.mean(xv * xv, axis=-1, keepdims=True) + EPS)
        xh = xv * r
        g = g_ref[...]
        dw = jnp.sum(g * xh, axis=0, keepdims=True)
        gy = g * w_ref[...]
        dx = r_ref[...] + r * (gy - xh * jnp.mean(gy * xh, axis=-1, keepdims=True))
        dx_ref[...] = dx
        dxb_ref[...] = dx.astype(BF16)

        @pl.when(i == 0)
        def _():
            dw_ref[...] = jnp.zeros_like(dw_ref)

        dw_ref[...] += dw

    row = pl.BlockSpec((ROW_TILE, Dm), lambda i: (i, 0))
    vec = pl.BlockSpec((1, Dm), lambda i: (0, 0))
    return _call(
        body, (dyn, x, w, dres), name=name, grid=(T // ROW_TILE,),
        in_specs=[row, row, vec, row],
        out_specs=[row, row, vec],
        out_shape=[jax.ShapeDtypeStruct((T, Dm), F32), jax.ShapeDtypeStruct((T, Dm), BF16),
                   jax.ShapeDtypeStruct((1, Dm), F32)],
        sem=("arbitrary",), after=after)


COL_TILE = 2048


def _relu2_fwd(a, name):
    T, N = a.shape

    def body(a_ref, r_ref):
        ra = jnp.maximum(a_ref[...], 0.0)
        r_ref[...] = (ra * ra).astype(BF16)

    blk = pl.BlockSpec((ROW_TILE, COL_TILE), lambda i, j: (i, j))
    return pl.pallas_call(
        body, name=name, grid=(T // ROW_TILE, N // COL_TILE), in_specs=[blk], out_specs=blk,
        out_shape=jax.ShapeDtypeStruct((T, N), BF16),
        compiler_params=_cparams(("parallel", "parallel")),
    )(a)


def _relu2_bwd(dr, a, name, after=()):
    T, N = a.shape

    def body(dr_ref, a_ref, da_ref):
        da_ref[...] = (dr_ref[...] * (2.0 * jnp.maximum(a_ref[...], 0.0))).astype(BF16)

    blk = pl.BlockSpec((ROW_TILE, COL_TILE), lambda i, j: (i, j))
    return _call(
        body, (dr, a), name=name, grid=(T // ROW_TILE, N // COL_TILE), in_specs=[blk, blk], out_specs=[blk],
        out_shape=[jax.ShapeDtypeStruct((T, N), BF16)], sem=("parallel", "parallel"), after=after)[0]


GATE_TILE = 1024


def _merge_fwd(z, pa, pb, name):
    T, Dm = pa.shape

    def body(za_ref, zb_ref, pa_ref, pb_ref, m_ref):
        m_ref[...] = (_sigmoid(za_ref[...]) * pa_ref[...] + _sigmoid(zb_ref[...]) * pb_ref[...]).astype(BF16)

    blk = pl.BlockSpec((ROW_TILE, GATE_TILE), lambda i, j: (i, j))
    return pl.pallas_call(
        body, name=name, grid=(T // ROW_TILE, Dm // GATE_TILE),
        in_specs=[pl.BlockSpec((ROW_TILE, GATE_TILE), lambda i, j: (i, COL_GATE_A + j)),
                  pl.BlockSpec((ROW_TILE, GATE_TILE), lambda i, j: (i, COL_GATE_B + j)), blk, blk],
        out_specs=blk,
        out_shape=jax.ShapeDtypeStruct((T, Dm), BF16),
        compiler_params=_cparams(("parallel", "parallel")),
    )(z, z, pa, pb)


def _merge_bwd(dm, z, pa, pb, name):
    T, Dm = pa.shape

    def body(dm_ref, za_ref, zb_ref, pa_ref, pb_ref, dpa_ref, dpb_ref, dga_ref, dgb_ref):
        d = dm_ref[...]
        ga = _sigmoid(za_ref[...])
        gb = _sigmoid(zb_ref[...])
        dpa_ref[...] = (d * ga).astype(BF16)
        dpb_ref[...] = (d * gb).astype(BF16)
        dga_ref[...] = (d * pa_ref[...] * ga * (1.0 - ga)).astype(BF16)
        dgb_ref[...] = (d * pb_ref[...] * gb * (1.0 - gb)).astype(BF16)

    blk = pl.BlockSpec((ROW_TILE, GATE_TILE), lambda i, j: (i, j))
    out = jax.ShapeDtypeStruct((T, Dm), BF16)
    return pl.pallas_call(
        body, name=name, grid=(T // ROW_TILE, Dm // GATE_TILE),
        in_specs=[blk, pl.BlockSpec((ROW_TILE, GATE_TILE), lambda i, j: (i, COL_GATE_A + j)),
                  pl.BlockSpec((ROW_TILE, GATE_TILE), lambda i, j: (i, COL_GATE_B + j)), blk, blk],
        out_specs=[blk, blk, blk, blk],
        out_shape=[out, out, out, out],
        compiler_params=_cparams(("parallel", "parallel")),
    )(dm, z, z, pa, pb)


def _dot_hi(a, b, dims):
    return lax.dot_general(a, b, (dims, ((), ())), precision=HIGHEST, preferred_element_type=F32)


NN = ((1,), (0,))
NT = ((1,), (1,))
TN = ((0,), (0,))


def _hg_gates(hq, hf, lb):
    sq = _sigmoid(hq)
    q = hq * sq * (HG_DK ** -0.5)
    f = _sigmoid(hf)
    g = lb + (1.0 - lb) * f
    return q, sq, f, g, jnp.log(g), 1.0 - g


def _tri(lower):
    r = lax.broadcasted_iota(jnp.int32, (CHUNK, CHUNK), 0)
    c = lax.broadcasted_iota(jnp.int32, (CHUNK, CHUNK), 1)
    return jnp.where((r >= c) if lower else (r <= c), 1.0, 0.0).astype(F32)


def _hgrn2_fwd(z, lb_logits, hg_norm_w, name, after=()):
    T = z.shape[0]
    n_chunks = T // CHUNK

    def body(hq_ref, hf_ref, hi_ref, hg_ref, lbl_ref, nw_ref, o_ref, ya_ref, sall_ref, st_ref):
        lbl = lbl_ref[...]
        lb = 1.0 / (1.0 + jnp.exp(lbl[1:2, :] - lbl[0:1, :]))
        st_ref[...] = jnp.zeros_like(st_ref)
        tri = _tri(True)
        row8 = lax.broadcasted_iota(jnp.int32, (8, HG_DK), 0)

        def chunk(c, carry):
            rows = pl.ds(pl.multiple_of(c * CHUNK, CHUNK), CHUNK)
            q, _, _, _, lg, kk = _hg_gates(hq_ref[rows, :], hf_ref[rows, :], lb)
            v = hi_ref[rows, :]
            b = _dot_hi(tri, lg, NN)
            st = st_ref[...]
            sall_ref[c] = st
            o_inter = _dot_hi(q * jnp.exp(b), st, NT)
            for g8 in range(CHUNK // 8):
                n = 8 * (g8 + 1)
                bs, ks, vs = b[:n], kk[:n], v[:n]
                sidx = lax.broadcasted_iota(jnp.int32, (n, HG_DK), 0)
                blk = o_inter[8 * g8:n]
                for i in range(8):
                    t = 8 * g8 + i
                    e = jnp.where(sidx <= t, jnp.exp(b[t:t + 1] - bs), 0.0)
                    p = jnp.sum(e * ks * q[t:t + 1], axis=1, keepdims=True)
                    ot = jnp.sum(p * vs, axis=0, keepdims=True)
                    blk = blk + jnp.where(row8 == i, ot, 0.0)
                o_ref[pl.ds(pl.multiple_of(c * CHUNK + 8 * g8, 8), 8), :] = blk
            bl = b[CHUNK - 1:CHUNK]
            ke = kk * jnp.exp(bl - b)
            st_ref[...] = st * jnp.exp(bl) + _dot_hi(v, ke, TN)
            return carry

        lax.fori_loop(0, n_chunks, chunk, 0)
        o = o_ref[...]
        r = lax.rsqrt(jnp.mean(o * o, axis=-1, keepdims=True) + EPS)
        hg = hg_ref[...]
        ya_ref[...] = (o * r * nw_ref[...] * (hg * _sigmoid(hg))).astype(BF16)

    def col(base):
        return pl.BlockSpec((T, HG_DK), lambda h: (0, base + h))

    return _call(
        body, (z, z, z, z, lb_logits, hg_norm_w), name=name, grid=(HG_HEADS,),
        in_specs=[col(COL_HQ), col(COL_HF), col(COL_HI), col(COL_HG),
                  pl.BlockSpec((2, HG_DK), lambda h: (0, h)), pl.BlockSpec((1, HG_DK), lambda h: (0, 0))],
        out_specs=[col(0), col(0), pl.BlockSpec((None, n_chunks, HG_DK, HG_DK), lambda h: (h, 0, 0, 0))],
        out_shape=[jax.ShapeDtypeStruct((T, HG_WIDTH), F32), jax.ShapeDtypeStruct((T, HG_WIDTH), BF16),
                   jax.ShapeDtypeStruct((HG_HEADS, n_chunks, HG_DK, HG_DK), F32)],
        scratch_shapes=[pltpu.VMEM((HG_DK, HG_DK), F32)],
        sem=("parallel",), after=after)


def _hgrn2_bwd(z, lb_logits, hg_norm_w, o_raw, s_all, dya, name, after=()):
    T = z.shape[0]
    n_chunks = T // CHUNK

    def body(hq_ref, hf_ref, hi_ref, hg_ref, lbl_ref, nw_ref, o_ref, sall_ref, dya_ref,
             dhq_ref, dhf_ref, dhi_ref, dhg_ref, dlbl_ref, dnw_ref,
             do_ref, dst_ref, dq_ref, dk_ref, dv_ref, dlb_ref):
        h = pl.program_id(0)
        lbl = lbl_ref[...]
        lb = 1.0 / (1.0 + jnp.exp(lbl[1:2, :] - lbl[0:1, :]))

        o = o_ref[...]
        r = lax.rsqrt(jnp.mean(o * o, axis=-1, keepdims=True) + EPS)
        oh = o * r
        nw = nw_ref[...]
        hg = hg_ref[...]
        sg = _sigmoid(hg)
        dy = dya_ref[...]
        d_on = dy * (hg * sg)
        dhg_ref[...] = (dy * (oh * nw) * (sg * (1.0 + hg * (1.0 - sg)))).astype(BF16)
        dnw = jnp.sum(d_on * oh, axis=0, keepdims=True)
        gy = d_on * nw
        do_ref[...] = r * (gy - oh * jnp.mean(gy * oh, axis=-1, keepdims=True))

        @pl.when(h == 0)
        def _():
            dnw_ref[...] = jnp.zeros_like(dnw_ref)

        dnw_ref[...] += jnp.broadcast_to(dnw, dnw_ref.shape)

        dst_ref[...] = jnp.zeros_like(dst_ref)
        dlb_ref[...] = jnp.zeros_like(dlb_ref)
        tri = _tri(True)
        tri_t = _tri(False)
        row8 = lax.broadcasted_iota(jnp.int32, (8, HG_DK), 0)

        def chunk(ci, carry):
            c = n_chunks - 1 - ci
            rows = pl.ds(pl.multiple_of(c * CHUNK, CHUNK), CHUNK)
            hq = hq_ref[rows, :]
            q, sq, f, g, lg, kk = _hg_gates(hq, hf_ref[rows, :], lb)
            v = hi_ref[rows, :]
            do = do_ref[rows, :]
            b = _dot_hi(tri, lg, NN)
            eb = jnp.exp(b)
            bl = b[CHUNK - 1:CHUNK]
            ebl = jnp.exp(bl)
            ekb = jnp.exp(bl - b)
            qe = q * eb
            ke = kk * ekb
            st = sall_ref[c]
            dst = dst_ref[...]
            dqe = _dot_hi(do, st, NN)
            dke = _dot_hi(v, dst, NN)
            dv_inter = _dot_hi(ke, dst, NT)
            d_ebl = jnp.sum(st * dst, axis=0, keepdims=True)
            dst_ref[...] = dst * ebl + _dot_hi(do, qe, TN)

            dk_ref[...] = jnp.zeros_like(dk_ref)
            dv_ref[...] = jnp.zeros_like(dv_ref)
            for g8 in range(CHUNK // 8):
                n = 8 * (g8 + 1)
                bs, ks, vs = b[:n], kk[:n], v[:n]
                sidx = lax.broadcasted_iota(jnp.int32, (n, HG_DK), 0)
                blk = jnp.zeros((8, HG_DK), F32)
                for i in range(8):
                    t = 8 * g8 + i
                    qt = q[t:t + 1]
                    dot_ = do[t:t + 1]
                    e = jnp.where(sidx <= t, jnp.exp(b[t:t + 1] - bs), 0.0)
                    w = e * ks
                    p = jnp.sum(w * qt, axis=1, keepdims=True)
                    dsc = jnp.sum(vs * dot_, axis=1, keepdims=True)
                    dqt = jnp.sum(dsc * w, axis=0, keepdims=True)
                    blk = blk + jnp.where(row8 == i, dqt, 0.0)
                    dk_ref[0:n, :] += dsc * e * qt
                    dv_ref[0:n, :] += p * dot_
                dq_ref[8 * g8:n, :] = blk
            dq_i = dq_ref[...]
            dk_i = dk_ref[...]
            dke_ke = dke * ke
            db = q * dq_i - kk * dk_i + dqe * qe - dke_ke
            db_last = jnp.sum(dke_ke, axis=0, keepdims=True) + d_ebl * ebl
            dlg = _dot_hi(tri_t, db, NN) + db_last
            dq = dq_i + dqe * eb
            dkk = dk_i + dke * ekb
            dg = dlg / g - dkk
            dhq_ref[rows, :] = (dq * (HG_DK ** -0.5) * (sq * (1.0 + hq * (1.0 - sq)))).astype(BF16)
            dhf_ref[rows, :] = (dg * (1.0 - lb) * f * (1.0 - f)).astype(BF16)
            dhi_ref[rows, :] = (dv_ref[...] + dv_inter).astype(BF16)
            dlb_ref[...] += jnp.sum(dg * (1.0 - f), axis=0, keepdims=True)
            return carry

        lax.fori_loop(0, n_chunks, chunk, 0)
        dl0 = dlb_ref[...] * lb * (1.0 - lb)
        dlbl_ref[0:1, :] = dl0
        dlbl_ref[1:2, :] = -dl0

    def col(base):
        return pl.BlockSpec((T, HG_DK), lambda h: (0, base + h))

    outb = jax.ShapeDtypeStruct((T, HG_WIDTH), BF16)
    return _call(
        body, (z, z, z, z, lb_logits, hg_norm_w, o_raw, s_all, dya), name=name, grid=(HG_HEADS,),
        in_specs=[col(COL_HQ), col(COL_HF), col(COL_HI), col(COL_HG),
                  pl.BlockSpec((2, HG_DK), lambda h: (0, h)), pl.BlockSpec((1, HG_DK), lambda h: (0, 0)),
                  col(0), pl.BlockSpec((None, n_chunks, HG_DK, HG_DK), lambda h: (h, 0, 0, 0)), col(0)],
        out_specs=[col(0), col(0), col(0), col(0), pl.BlockSpec((2, HG_DK), lambda h: (0, h)),
                   pl.BlockSpec((8, HG_DK), lambda h: (0, 0))],
        out_shape=[outb, outb, outb, outb, jax.ShapeDtypeStruct((2, HG_WIDTH), F32),
                   jax.ShapeDtypeStruct((8, HG_DK), F32)],
        scratch_shapes=[pltpu.VMEM((T, HG_DK), F32), pltpu.VMEM((HG_DK, HG_DK), F32),
                        pltpu.VMEM((CHUNK, HG_DK), F32), pltpu.VMEM((CHUNK, HG_DK), F32),
                        pltpu.VMEM((CHUNK, HG_DK), F32), pltpu.VMEM((1, HG_DK), F32)],
        sem=("arbitrary",), after=after)


CONST_KEYS = PAD - REL_CLIP
VAR_KEYS = BAND - CONST_KEYS
REL_LO = 128
REL_SPAN = N_REL_PAD - REL_LO


def _rel_onehot(t):
    r = lax.broadcasted_iota(jnp.int32, (REL_SPAN, VAR_KEYS), 0)
    j = lax.broadcasted_iota(jnp.int32, (REL_SPAN, VAR_KEYS), 1)
    idx = jnp.clip(t + PAD - CONST_KEYS - j, -REL_CLIP, REL_CLIP) + REL_CLIP - REL_LO
    return jnp.where(r == idx, 1.0, 0.0).astype(BF16)


def _split3(x):
    hi = x.astype(BF16)
    r1 = x - hi.astype(F32)
    mid = r1.astype(BF16)
    return hi, mid, (r1 - mid.astype(F32)).astype(BF16)


def _bias_expand(rel, name):
    def body(rel_ref, out_ref):
        tab = rel_ref[...]
        onehot = _rel_onehot(pl.program_id(0))
        out_ref[:, 0:CONST_KEYS] = jnp.broadcast_to(tab[:, 2 * REL_CLIP:2 * REL_CLIP + 1], (AT_HEADS, CONST_KEYS))
        out_ref[:, CONST_KEYS:BAND] = sum(
            jnp.dot(piece, onehot, preferred_element_type=F32) for piece in _split3(tab[:, REL_LO:N_REL_PAD]))

    return pl.pallas_call(
        body, name=name, grid=(CHUNK,),
        in_specs=[pl.BlockSpec((AT_HEADS, N_REL_PAD), lambda t: (0, 0))],
        out_specs=pl.BlockSpec((None, AT_HEADS, BAND), lambda t: (t, 0, 0)),
        out_shape=jax.ShapeDtypeStruct((CHUNK, AT_HEADS, BAND), F32),
        compiler_params=_cparams(("parallel",)),
    )(rel)


def _bias_reduce(dbias_t, name):
    def body(db_ref, out_ref):
        t = pl.program_id(0)

        @pl.when(t == 0)
        def _():
            out_ref[...] = jnp.zeros_like(out_ref)

        db = db_ref[...]
        onehot = _rel_onehot(t)
        acc = sum(lax.dot_general(piece, onehot, (NT, ((), ())), preferred_element_type=F32)
                  for piece in _split3(db[:, CONST_KEYS:BAND]))
        lane = lax.broadcasted_iota(jnp.int32, (AT_HEADS, REL_SPAN), 1)
        last = jnp.sum(db[:, 0:CONST_KEYS], axis=1, keepdims=True)
        out_ref[:, REL_LO:N_REL_PAD] += acc + jnp.where(lane == 2 * REL_CLIP - REL_LO, last, 0.0)

    return pl.pallas_call(
        body, name=name, grid=(CHUNK,),
        in_specs=[pl.BlockSpec((None, AT_HEADS, BAND), lambda t: (t, 0, 0))],
        out_specs=pl.BlockSpec((AT_HEADS, N_REL_PAD), lambda t: (0, 0)),
        out_shape=jax.ShapeDtypeStruct((AT_HEADS, N_REL_PAD), F32),
        compiler_params=_cparams(("arbitrary",)),
    )(dbias_t)


def _pair_lanes():
    return lax.broadcasted_iota(jnp.int32, (CHUNK, 2 * AT_DH), 1) < AT_DH


def _block_diag(a):
    first = _pair_lanes()
    return jnp.concatenate([jnp.where(first, a, 0.0), jnp.where(first, 0.0, a)], axis=0).astype(BF16)


def _diag_blocks(a):
    return jnp.where(_pair_lanes(), a[:CHUNK], a[CHUNK:])


def _band_probs_t(kb, qbd, bias_t, c):
    s = lax.dot_general(kb, qbd, (NT, ((), ())), preferred_element_type=F32) * (AT_DH ** -0.5) + bias_t
    j = lax.broadcasted_iota(jnp.int32, (BAND, 2 * AT_DH), 0)
    s = jnp.where(j + c * CHUNK >= PAD, s, -jnp.inf)
    p = jnp.exp(s - jnp.max(s, axis=0, keepdims=True))
    return p / jnp.sum(p, axis=0, keepdims=True)


def _fill_padded(dst_ref, src_ref, T):
    dst_ref[0:PAD, :] = jnp.zeros((PAD, 2 * AT_DH), BF16)
    dst_ref[PAD:PAD + T, :] = src_ref[...].astype(BF16)


def _attn_fwd(z, bias_t, name, after=()):
    T = z.shape[0]
    n_chunks = T // CHUNK

    def body(q_ref, k_ref, v_ref, bias_ref, y_ref, kp_ref, vp_ref):
        _fill_padded(kp_ref, k_ref, T)
        _fill_padded(vp_ref, v_ref, T)

        def chunk(c, carry):
            rows = pl.ds(pl.multiple_of(c * CHUNK, CHUNK), CHUNK)
            band = pl.ds(pl.multiple_of(c * CHUNK, CHUNK), BAND)
            p = _band_probs_t(kp_ref[band, :], _block_diag(q_ref[rows, :]), bias_ref[...], c)
            o2 = lax.dot_general(p.astype(BF16), vp_ref[band, :], (TN, ((), ())), preferred_element_type=F32)
            y_ref[rows, :] = _diag_blocks(o2).astype(BF16)
            return carry

        lax.fori_loop(0, n_chunks, chunk, 0, unroll=2)

    def col(base):
        return pl.BlockSpec((T, 128), lambda h: (0, base + h))

    return _call(
        body, (z, z, z, bias_t), name=name, grid=(AT_HEADS // 2,),
        in_specs=[col(COL_AQ), col(COL_AK), col(COL_AV), pl.BlockSpec((None, BAND, 128), lambda h: (h, 0, 0))],
        out_specs=[col(0)],
        out_shape=[jax.ShapeDtypeStruct((T, AT_WIDTH), BF16)],
        scratch_shapes=[pltpu.VMEM((PAD + T, 128), BF16), pltpu.VMEM((PAD + T, 128), BF16)],
        sem=("parallel",), after=after)


def _attn_bwd(z, bias_t, dyb, name, after=()):
    T = z.shape[0]
    n_chunks = T // CHUNK

    def body(q_ref, k_ref, v_ref, bias_ref, dy_ref, dq_ref, dk_ref, dv_ref, dbias_ref,
             kp_ref, vp_ref, dkp_ref, dvp_ref):
        _fill_padded(kp_ref, k_ref, T)
        _fill_padded(vp_ref, v_ref, T)
        dkp_ref[...] = jnp.zeros_like(dkp_ref)
        dvp_ref[...] = jnp.zeros_like(dvp_ref)
        dbias_ref[...] = jnp.zeros_like(dbias_ref)

        def chunk(c, carry):
            rows = pl.ds(pl.multiple_of(c * CHUNK, CHUNK), CHUNK)
            band = pl.ds(pl.multiple_of(c * CHUNK, CHUNK), BAND)
            qbd = _block_diag(q_ref[rows, :])
            dobd = _block_diag(dy_ref[rows, :])
            kb = kp_ref[band, :]
            vb = vp_ref[band, :]
            p = _band_probs_t(kb, qbd, bias_ref[...], c)
            dp = lax.dot_general(vb, dobd, (NT, ((), ())), preferred_element_type=F32)
            ds = p * (dp - jnp.sum(dp * p, axis=0, keepdims=True))
            dbias_ref[...] += ds
            dsb = ds.astype(BF16)
            dq2 = lax.dot_general(dsb, kb, (TN, ((), ())), preferred_element_type=F32)
            dq_ref[rows, :] = (_diag_blocks(dq2) * (AT_DH ** -0.5)).astype(BF16)
            dkp_ref[band, :] += jnp.dot(dsb, qbd, preferred_element_type=F32) * (AT_DH ** -0.5)
            dvp_ref[band, :] += jnp.dot(p.astype(BF16), dobd, preferred_element_type=F32)
            return carry

        lax.fori_loop(0, n_chunks, chunk, 0)
        dk_ref[...] = dkp_ref[PAD:PAD + T, :].astype(BF16)
        dv_ref[...] = dvp_ref[PAD:PAD + T, :].astype(BF16)

    def col(base):
        return pl.BlockSpec((T, 128), lambda h: (0, base + h))

    pair = pl.BlockSpec((None, BAND, 128), lambda h: (h, 0, 0))
    outb = jax.ShapeDtypeStruct((T, AT_WIDTH), BF16)
    return _call(
        body, (z, z, z, bias_t, dyb), name=name, grid=(AT_HEADS // 2,),
        in_specs=[col(COL_AQ), col(COL_AK), col(COL_AV), pair, col(0)],
        out_specs=[col(0), col(0), col(0), pair],
        out_shape=[outb, outb, outb, jax.ShapeDtypeStruct((AT_HEADS // 2, BAND, 128), F32)],
        scratch_shapes=[pltpu.VMEM((PAD + T, 128), BF16), pltpu.VMEM((PAD + T, 128), BF16),
                        pltpu.VMEM((PAD + T, 128), F32), pltpu.VMEM((PAD + T, 128), F32)],
        sem=("parallel",), after=after)


def _local_step(x, target, lb_logits, hg_norm_w, rel_bias, norm_mix_w, norm_mlp_w, norm_final_w,
                w_in, rest, exchanges=None):
    ex = exchanges
    rel = jnp.pad(rel_bias, ((0, 0), (0, N_REL_PAD - N_REL)))

    u = _rms_fwd(x, norm_mix_w, "rms_mix_fwd")
    if ex:
        z, w_in = _mm_gathered(u, w_in, ex.order, "mm_in_fwd")
        gather = _Gather(rest, [z], "ag")
        tok = [gather.token]
    else:
        z = _mm_nn(u, w_in, F32, "mm_in_fwd")
        w_a, w_b, w_out, w_up, w_down = rest
        tok = []
    o_raw, y_a, s_all = _hgrn2_fwd(z, lb_logits, hg_norm_w, "hgrn2_fwd", after=tok)
    if ex:
        tok = [gather.pass_on([0, 1, 2], [o_raw], "abo")]
    bias_rows = _bias_expand(rel, "bias_expand")
    bias_t = jnp.transpose(bias_rows.reshape(CHUNK, AT_HEADS // 2, 2, BAND), (1, 3, 2, 0)).reshape(
        AT_HEADS // 2, BAND, 2 * CHUNK)
    y_b, = _attn_fwd(z, bias_t, "attn_fwd", after=tok)
    if ex:
        tok = [gather.pass_on([3], [y_b], "up")]
        w_a, w_b, w_out = gather.finish([0, 1, 2], tok, "abo")
    pa = _mm_nn(y_a, w_a, F32, "mm_a_fwd")
    pb = _mm_nn(y_b, w_b, F32, "mm_b_fwd")
    merged = _merge_fwd(z, pa, pb, "merge_fwd")
    w_out1 = w_out.reshape(1, D_MODEL, D_MODEL)
    mix = _mm_nn(merged, w_out1, F32, "mm_out_fwd")
    h1, u2 = _resid_rms_fwd(x, mix, norm_mlp_w, "rms_mlp_fwd")
    if ex:
        tok = [gather.pass_on([4], [u2], "down")]
        w_up, = gather.finish([3], tok, "up")
    a = _mm_nn(u2, w_up, F32, "mm_up_fwd")
    r = _relu2_fwd(a, "relu2_fwd")
    if ex:
        w_down, = gather.finish([4], [r], "down")
    w_down1 = w_down.reshape(1, D_FF, D_MODEL)
    mlp = _mm_nn(r, w_down1, F32, "mm_down_fwd")
    loss, dh2, dh2b, g_nf = _loss_head(h1, mlp, norm_final_w, target, "loss_head")

    def reduce_scatter(grads, name):
        rs = _ReduceScatter(grads, ex.parity, name) if ex else None
        return rs, ([rs.token] if ex else [])

    g_down = _mm_tn(r, dh2b, 1, BF16, "mm_down_wgrad").reshape(N_DEV, D_FF // N_DEV, D_MODEL)
    rs_down, tok = reduce_scatter([g_down], "rs_down")
    dr = _mm_nt(dh2b, w_down1, F32, "mm_down_dgrad", after=tok)
    tok = [rs_down.pair_sums([dr])] if ex else []
    da = _relu2_bwd(dr, a, "relu2_bwd", after=tok)
    g_up = _mm_tn(u2, da, N_DEV, BF16, "mm_up_wgrad")
    rs_up, tok = reduce_scatter([g_up], "rs_up")
    du2 = _mm_nt(da, w_up, F32, "mm_up_dgrad", after=tok)
    tok = [rs_up.pair_sums([du2])] if ex else []
    dh1, dh1b, g_nmlp = _rms_bwd(du2, h1, norm_mlp_w, dh2, "rms_mlp_bwd", after=tok)

    g_out = _mm_tn(merged, dh1b, 1, BF16, "mm_out_wgrad").reshape(N_DEV, D_MODEL // N_DEV, D_MODEL)
    dmerged = _mm_nt(dh1b, w_out1, F32, "mm_out_dgrad")
    dpa, dpb, dga, dgb = _merge_bwd(dmerged, z, pa, pb, "merge_bwd")
    g_a = _mm_tn(y_a, dpa, N_DEV, BF16, "mm_a_wgrad")
    g_b = _mm_tn(y_b, dpb, N_DEV, BF16, "mm_b_wgrad")
    rs_mix, tok = reduce_scatter([g_a, g_b, g_out], "rs_mix")
    dya = _mm_nt(dpa, w_a, F32, "mm_a_dgrad", after=tok)
    dyb = _mm_nt(dpb, w_b, F32, "mm_b_dgrad", after=tok)
    tok = [rs_mix.pair_sums([dya, dyb])] if ex else []
    daq, dak, dav, dbias_t = _attn_bwd(z, bias_t, dyb, "attn_bwd", after=tok)
    dhq, dhf, dhi, dhg, g_lbl, g_hgw = _hgrn2_bwd(z, lb_logits, hg_norm_w, o_raw, s_all, dya, "hgrn2_bwd",
                                                  after=tok)
    dbias_rows = jnp.transpose(dbias_t.reshape(AT_HEADS // 2, BAND, 2, CHUNK), (3, 0, 2, 1)).reshape(
        CHUNK, AT_HEADS, BAND)
    g_rel = _bias_reduce(dbias_rows, "bias_reduce")[:, :N_REL]
    dz = jnp.concatenate([dhq, dhf, dhi, dhg, daq, dak, dav, dga, dgb], axis=1)
    g_in = _mm_tn(u, dz, N_DEV, BF16, "mm_in_wgrad")
    rs_in, tok = reduce_scatter([g_in], "rs_in")
    du = _mm_nt(dz, w_in, F32, "mm_in_dgrad", after=tok)
    tok = [rs_in.pair_sums([du])] if ex else []
    grad_x, _, g_nmix = _rms_bwd(du, x, norm_mix_w, dh1, "rms_mix_bwd", after=tok)

    small = dict(lb_logits=g_lbl, hg_norm_w=g_hgw[0:1], rel_bias=g_rel, norm_mix_w=g_nmix, norm_mlp_w=g_nmlp,
                 norm_final_w=g_nf)
    grads = [rs_in, rs_mix, rs_up, rs_down] if ex else [g_in, g_a, g_b, g_out, g_up, g_down]
    return loss, grad_x, grads, small


def _gather_exchange(shards, mid_step=None):
    n = len(shards)

    def parts(ins, outs, sems):
        send_sems, recv_sems, local_sems = sems
        x, y, c = _position()
        chips = [(1 - x, y), (x, 1 - y), (1 - x, 1 - y)]

        def copy(w, k, block, to, src=None):
            dst = outs[w].at[4 * block[0] + 2 * block[1] + block[2]]
            return pltpu.make_async_remote_copy(
                src_ref=dst if src is None else src, dst_ref=dst,
                send_sem=send_sems.at[w, k], recv_sem=recv_sems.at[w, k], device_id=to, device_id_type=MESH)

        def local(w):
            return pltpu.make_async_copy(ins[w], outs[w].at[4 * x + 2 * y + c], local_sems.at[w])

        return (x, y, c), (x, y, 1 - c), chips, copy, local

    def start(ins, outs, sems):
        me, sibling, chips, copy, local = parts(ins, outs, sems)
        for w in range(n):
            local(w).start()
        for w in range(n):
            copy(w, 0, me, sibling, src=ins[w]).start()
            for j, chip in enumerate(chips):
                copy(w, 1 + j, me, (*chip, me[2]), src=ins[w]).start()

    def mid(ins, outs, sems):
        me, sibling, chips, copy, _ = parts(ins, outs, sems)
        for w in range(n):
            for j, chip in enumerate(chips):
                copy(w, 1 + j, (*chip, me[2]), me).wait_recv()
                copy(w, 4 + j, (*chip, me[2]), sibling).start()

    def end(ins, outs, sems):
        me, sibling, chips, copy, local = parts(ins, outs, sems)
        for w in range(n):
            copy(w, 0, sibling, me).wait_recv()
            for j, chip in enumerate(chips):
                copy(w, 4 + j, (*chip, sibling[2]), me).wait_recv()
        for w in range(n):
            for k in range(7):
                copy(w, k, me, sibling).wait_send()
            local(w).wait()

    return _Exchange(
        shards, [jax.ShapeDtypeStruct((N_DEV,) + s.shape, s.dtype) for s in shards],
        [pltpu.SemaphoreType.DMA((n, 7)), pltpu.SemaphoreType.DMA((n, 7)), pltpu.SemaphoreType.DMA((n,))],
        start, end, mid, mid_step)


def _mm_gathered(u, shard, order, name):
    T, K = u.shape
    _, Nb = shard.shape

    def body(order_ref, u_ref, shard_ref, z_ref, full_ref, wbuf, load_sem, send_sems, recv_sems, local_sem):
        s = pl.program_id(0)
        x, y, c = _position()
        me, sibling = (x, y, c), (x, y, 1 - c)
        chips = [(1 - x, y), (x, 1 - y), (1 - x, 1 - y)]

        def copy(k, block, to, src=None):
            dst = full_ref.at[4 * block[0] + 2 * block[1] + block[2]]
            return pltpu.make_async_remote_copy(
                src_ref=dst if src is None else src, dst_ref=dst,
                send_sem=send_sems.at[k], recv_sem=recv_sems.at[k], device_id=to, device_id_type=MESH)

        @pl.when(s == 0)
        def _():
            local = pltpu.make_async_copy(shard_ref, full_ref.at[4 * x + 2 * y + c], local_sem)
            local.start()
            copy(0, me, sibling, src=shard_ref).start()
            for j, chip in enumerate(chips):
                copy(1 + j, me, (*chip, c), src=shard_ref).start()
            local.wait()

        @pl.when(s == 1)
        def _():
            copy(0, sibling, me).wait_recv()

        for j, chip in enumerate(chips):
            direct, passed = ((2, 4), (3, 5), (6, 7))[j]

            @pl.when(s == direct)
            def _(j=j, chip=chip):
                copy(1 + j, (*chip, c), me).wait_recv()
                copy(4 + j, (*chip, c), sibling).start()

            @pl.when(s == passed)
            def _(j=j, chip=chip):
                copy(4 + j, (*chip, 1 - c), me).wait_recv()

        load = pltpu.make_async_copy(full_ref.at[order_ref[s]], wbuf, load_sem)
        load.start()
        load.wait()
        z_ref[...] = jnp.dot(u_ref[...], wbuf[...], preferred_element_type=F32)

        @pl.when(s == N_DEV - 1)
        def _():
            for k in range(7):
                copy(k, me, sibling).wait_send()

    return pl.pallas_call(
        body, name=name,
        grid_spec=pltpu.PrefetchScalarGridSpec(
            num_scalar_prefetch=1, grid=(N_DEV,),
            in_specs=[pl.BlockSpec((T, K), lambda s, order: (0, 0)), ANY],
            out_specs=[pl.BlockSpec((T, Nb), lambda s, order: (0, order[s])), ANY],
            scratch_shapes=[pltpu.VMEM((K, Nb), BF16), pltpu.SemaphoreType.DMA,
                            pltpu.SemaphoreType.DMA((7,)), pltpu.SemaphoreType.DMA((7,)), pltpu.SemaphoreType.DMA]),
        out_shape=[jax.ShapeDtypeStruct((T, N_DEV * Nb), F32), jax.ShapeDtypeStruct((N_DEV, K, Nb), BF16)],
        compiler_params=_cparams(("arbitrary",)),
    )(order, u, shard)


def _gather_order():
    x, y, c = _position()
    chips = [(1 - x, y), (x, 1 - y), (1 - x, 1 - y)]
    ids = [4 * x + 2 * y + c, 4 * x + 2 * y + (1 - c)]
    ids += [4 * cx + 2 * cy + c for cx, cy in chips[:2]] + [4 * cx + 2 * cy + (1 - c) for cx, cy in chips[:2]]
    ids += [4 * chips[2][0] + 2 * chips[2][1] + c, 4 * chips[2][0] + 2 * chips[2][1] + (1 - c)]
    return jnp.stack(ids).astype(jnp.int32)


def _run_exchange(comm, name):
    n_i, n_o = len(comm.arrays), len(comm.out_shape)

    def body(*refs):
        ins, outs, sems = refs[:n_i], refs[n_i:n_i + n_o], refs[n_i + n_o:]
        comm.start(ins, outs, sems)
        if comm.mid is not None:
            comm.mid(ins, outs, sems)
        comm.end(ins, outs, sems)

    return pl.pallas_call(
        body, name=name, in_specs=[ANY] * n_i, out_specs=[ANY] * n_o, out_shape=comm.out_shape,
        scratch_shapes=comm.scratch)(*comm.arrays)


def _exchange_sibling(grads, name):
    n = len(grads)

    def body(*refs):
        ins, outs = refs[:n], refs[n:2 * n]
        send_sems, recv_sems = refs[2 * n:]
        x, y, c = _position()
        copies = []
        for w in range(n):
            for s in range(N_CHIP):
                cp = pltpu.make_async_remote_copy(
                    src_ref=ins[w].at[2 * s + (1 - c)], dst_ref=outs[w].at[s],
                    send_sem=send_sems.at[w, s], recv_sem=recv_sems.at[w, s],
                    device_id=(x, y, 1 - c), device_id_type=MESH)
                cp.start()
                copies.append(cp)
        for cp in copies:
            cp.wait()

    return pl.pallas_call(
        body, name=name,
        in_specs=[ANY] * n, out_specs=[ANY] * n,
        out_shape=[jax.ShapeDtypeStruct((N_CHIP,) + g.shape[1:], g.dtype) for g in grads],
        scratch_shapes=[pltpu.SemaphoreType.DMA((n, N_CHIP)), pltpu.SemaphoreType.DMA((n, N_CHIP))],
    )(*grads)


def _pair_sum(g, land, parity, name):
    _, R, C = g.shape
    tr = _pick(R, (512, 256))

    def body(par_ref, g_ref, l_ref, o_ref):
        o_ref[...] = (g_ref[...].astype(F32) + l_ref[...].astype(F32)).astype(BF16)

    return pl.pallas_call(
        body, name=name,
        grid_spec=pltpu.PrefetchScalarGridSpec(
            num_scalar_prefetch=1, grid=(N_CHIP, R // tr),
            in_specs=[pl.BlockSpec((None, tr, C), lambda s, i, par: (2 * s + par[0], i, 0)),
                      pl.BlockSpec((None, tr, C), lambda s, i, par: (s, i, 0))],
            out_specs=pl.BlockSpec((None, tr, C), lambda s, i, par: (s, i, 0))),
        out_shape=jax.ShapeDtypeStruct((N_CHIP, R, C), BF16),
        compiler_params=_cparams(("parallel", "parallel")),
    )(parity, g, land)


def _scatter_exchange(partials):
    n = len(partials)

    def copies(ins, outs, sems):
        send_sems, recv_sems, local_sems = sems
        x, y, c = _position()
        chips = [(1 - x, y), (x, 1 - y), (1 - x, 1 - y)]
        my_slot = 2 * x + y
        local = [pltpu.make_async_copy(ins[w].at[my_slot], outs[w].at[my_slot], local_sems.at[w]) for w in range(n)]
        remote = [pltpu.make_async_remote_copy(
            src_ref=ins[w].at[2 * chip[0] + chip[1]], dst_ref=outs[w].at[my_slot],
            send_sem=send_sems.at[w, j], recv_sem=recv_sems.at[w, j], device_id=(*chip, c), device_id_type=MESH)
            for w in range(n) for j, chip in enumerate(chips)]
        return local, remote

    def start(ins, outs, sems):
        local, remote = copies(ins, outs, sems)
        for cp in local + remote:
            cp.start()

    def end(ins, outs, sems):
        local, remote = copies(ins, outs, sems)
        for cp in remote + local:
            cp.wait()

    return _Exchange(
        partials, [jax.ShapeDtypeStruct(p.shape, p.dtype) for p in partials],
        [pltpu.SemaphoreType.DMA((n, 3)), pltpu.SemaphoreType.DMA((n, 3)), pltpu.SemaphoreType.DMA((n,))],
        start, end)


HBM = pl.BlockSpec(memory_space=pltpu.HBM)
SEM = pl.BlockSpec(memory_space=pltpu.SEMAPHORE)
DATAFLOW = pltpu.SideEffectType.DATAFLOW_SIDE_EFFECTING


def _scatter_copies(ins, lands, send_sems, recv_sems):
    x, y, c = _position()
    chips = [(1 - x, y), (x, 1 - y), (1 - x, 1 - y)]
    return [pltpu.make_async_remote_copy(
        src_ref=ins[w].at[2 * chip[0] + chip[1]], dst_ref=lands[w].at[2 * x + y],
        send_sem=send_sems[3 * w + j], recv_sem=recv_sems[3 * w + j], device_id=(*chip, c), device_id_type=MESH)
        for w in range(len(ins)) for j, chip in enumerate(chips)]


def _scatter_start(partials, name):
    n = len(partials)

    def body(*refs):
        ins, lands = refs[:n], refs[n:2 * n]
        sems = refs[4 * n:10 * n]
        for cp in _scatter_copies(ins, lands, sems[:3 * n], sems[3 * n:]):
            cp.start()
        refs[-1][...] = jnp.zeros_like(refs[-1])

    def in_hbm(a):
        return pltpu.with_memory_space_constraint(a, pltpu.HBM)

    bufs = tuple(pltpu.HBM(p.shape, p.dtype) for p in partials)
    outs = pl.pallas_call(
        body, name=name,
        out_shape=bufs + bufs + (pltpu.SemaphoreType.DMA(()),) * (6 * n) + (jax.ShapeDtypeStruct((8, 128), F32),),
        in_specs=[HBM] * (2 * n),
        out_specs=(HBM,) * (2 * n) + (SEM,) * (6 * n) + (pl.BlockSpec(memory_space=pltpu.VMEM),),
        input_output_aliases={i: i for i in range(2 * n)},
        compiler_params=pltpu.CompilerParams(has_side_effects=DATAFLOW),
    )(*[in_hbm(p) for p in partials], *[in_hbm(lax.empty(p.shape, p.dtype)) for p in partials])
    return list(outs[:-1]), outs[-1]


def _scatter_wait(handle, after, name):
    n = len(handle) // 8
    bufs, sems = handle[:2 * n], handle[2 * n:]

    def body(*refs):
        ins, lands = refs[:n], refs[n:2 * n]
        sems = refs[2 * n:8 * n]
        for cp in _scatter_copies(ins, lands, sems[:3 * n], sems[3 * n:]):
            cp.wait_send()
            cp.wait_recv()

    outs = pl.pallas_call(
        body, name=name,
        out_shape=tuple(pltpu.HBM(b.shape, b.dtype) for b in bufs),
        in_specs=[HBM] * (2 * n) + [SEM] * (6 * n) + [ANY] * len(after), out_specs=(HBM,) * (2 * n),
        input_output_aliases={i: i for i in range(2 * n)},
        compiler_params=pltpu.CompilerParams(has_side_effects=DATAFLOW),
    )(*bufs, *sems, *after)
    return list(outs[:n]), list(outs[n:])


def _split_call(name, bufs, waits=(), starts=None, after=()):
    nb = len(bufs)
    n_new = starts[1] if starts else 0
    wait_sems = [s for w in waits for s in (*w[1], *w[2])]

    def body(*refs):
        b, pos = refs[:nb], nb
        for plan, ss, _, send_idx, recv_idx in waits:
            k = len(ss)
            copies = plan(b, refs[pos:pos + k], refs[pos + k:pos + 2 * k])
            pos += 2 * k
            for i in recv_idx:
                copies[i].wait_recv()
            for i in send_idx:
                copies[i].wait_send()
        outs = refs[pos + len(after):]
        if starts:
            for cp in starts[0](b, outs[nb:nb + n_new], outs[nb + n_new:nb + 2 * n_new]):
                cp.start()
        outs[-1][...] = jnp.zeros_like(outs[-1])

    res = pl.pallas_call(
        body, name=name,
        out_shape=tuple(pltpu.HBM(a.shape, a.dtype) for a in bufs) + (pltpu.SemaphoreType.DMA(()),) * (2 * n_new)
        + (jax.ShapeDtypeStruct((8, 128), F32),),
        in_specs=[HBM] * nb + [SEM] * len(wait_sems) + [ANY] * len(after),
        out_specs=(HBM,) * nb + (SEM,) * (2 * n_new) + (pl.BlockSpec(memory_space=pltpu.VMEM),),
        input_output_aliases={i: i for i in range(nb)},
        compiler_params=pltpu.CompilerParams(has_side_effects=DATAFLOW),
    )(*bufs, *wait_sems, *after)
    return list(res[:nb]), list(res[nb:nb + n_new]), list(res[nb + n_new:nb + 2 * n_new]), res[-1]


def _in_hbm(a):
    return pltpu.with_memory_space_constraint(a, pltpu.HBM)


def _remote(src, dst, send_sem, recv_sem, to):
    return pltpu.make_async_remote_copy(src_ref=src, dst_ref=dst, send_sem=send_sem, recv_sem=recv_sem,
                                        device_id=to, device_id_type=MESH)


def _other_chips():
    x, y, _ = _position()
    return [(1 - x, y), (x, 1 - y), (1 - x, 1 - y)]


def _plan_gather_first(n):
    def plan(b, ss, rs):
        x, y, c = _position()
        to = [(x, y, 1 - c)] + [(*chip, c) for chip in _other_chips()]
        return [_remote(b[w], b[n + w].at[4 * x + 2 * y + c], ss[4 * w + k], rs[4 * w + k], to[k])
                for w in range(n) for k in range(4)]
    return plan, 4 * n


def _plan_gather_pass(n):
    def plan(b, ss, rs):
        x, y, c = _position()
        copies = []
        for w in range(n):
            for j, chip in enumerate(_other_chips()):
                blk = b[n + w].at[4 * chip[0] + 2 * chip[1] + c]
                copies.append(_remote(blk, blk, ss[3 * w + j], rs[3 * w + j], (x, y, 1 - c)))
        return copies
    return plan, 3 * n


def _plan_sibling(n):
    def plan(b, ss, rs):
        x, y, c = _position()
        return [_remote(b[w].at[2 * s + (1 - c)], b[n + w].at[s], ss[4 * w + s], rs[4 * w + s], (x, y, 1 - c))
                for w in range(n) for s in range(N_CHIP)]
    return plan, 4 * n


def _plan_scatter(n):
    def plan(b, ss, rs):
        x, y, c = _position()
        return [_remote(b[w].at[2 * chip[0] + chip[1]], b[n + w].at[2 * x + y], ss[3 * w + j], rs[3 * w + j],
                        (*chip, c))
                for w in range(n) for j, chip in enumerate(_other_chips())]
    return plan, 3 * n


class _Gather:
    def __init__(self, shards, after, name):
        self.n, self.name = len(shards), name
        x, y, c = _position()
        placed = [lax.dynamic_update_index_in_dim(lax.empty((N_DEV,) + s.shape, s.dtype), s, 4 * x + 2 * y + c, 0)
                  for s in shards]
        bufs, self.ss, self.rs, self.token = _split_call(
            name + "_start", [_in_hbm(a) for a in list(shards) + placed], starts=_plan_gather_first(self.n),
            after=after)
        self.shards, self.fulls = bufs[:self.n], bufs[self.n:]
        self.passed = {}

    def _sub(self, ids, sems, per):
        return [sems[per * w + k] for w in ids for k in range(per)]

    def pass_on(self, ids, after, tag):
        m = len(ids)
        first = (_plan_gather_first(m)[0], self._sub(ids, self.ss, 4), self._sub(ids, self.rs, 4),
                 [], [4 * i + k for i in range(m) for k in (1, 2, 3)])
        bufs, ss, rs, token = _split_call(
            "%s_pass_%s" % (self.name, tag), [self.shards[w] for w in ids] + [self.fulls[w] for w in ids],
            waits=[first], starts=_plan_gather_pass(m), after=after)
        for i, w in enumerate(ids):
            self.shards[w], self.fulls[w] = bufs[i], bufs[m + i]
        self.passed[tuple(ids)] = (ss, rs)
        return token

    def finish(self, ids, after, tag):
        m = len(ids)
        ss2, rs2 = self.passed[tuple(ids)]
        first = (_plan_gather_first(m)[0], self._sub(ids, self.ss, 4), self._sub(ids, self.rs, 4),
                 list(range(4 * m)), [4 * i for i in range(m)])
        passed = (_plan_gather_pass(m)[0], ss2, rs2, list(range(3 * m)), list(range(3 * m)))
        bufs, _, _, _ = _split_call(
            "%s_finish_%s" % (self.name, tag), [self.shards[w] for w in ids] + [self.fulls[w] for w in ids],
            waits=[first, passed], after=after)
        return bufs[m:]


class _ReduceScatter:
    def __init__(self, grads, parity, name):
        self.n, self.name, self.parity = len(grads), name, parity
        lands = [lax.empty((N_CHIP,) + g.shape[1:], g.dtype) for g in grads]
        self.bufs, self.ss, self.rs, self.token = _split_call(
            name + "_sibling_start", [_in_hbm(a) for a in list(grads) + lands], starts=_plan_sibling(self.n))

    def pair_sums(self, after):
        n = self.n
        bufs, _, _, _ = _split_call(
            self.name + "_sibling_wait", self.bufs,
            waits=[(_plan_sibling(n)[0], self.ss, self.rs, list(range(4 * n)), list(range(4 * n)))], after=after)
        sums = [_pair_sum(bufs[w], bufs[n + w], self.parity, "%s_pair_sum_%d" % (self.name, w)) for w in range(n)]
        lands = [lax.empty(s.shape, s.dtype) for s in sums]
        self.bufs, self.ss, self.rs, token = _split_call(
            self.name + "_scatter_start", [_in_hbm(a) for a in sums + lands], starts=_plan_scatter(n))
        return token

    def finish(self, after):
        n = self.n
        bufs, _, _, _ = _split_call(
            self.name + "_scatter_wait", self.bufs,
            waits=[(_plan_scatter(n)[0], self.ss, self.rs, list(range(3 * n)), list(range(3 * n)))], after=after)
        return bufs[:n], bufs[n:]


class _Exchanges:
    def __init__(self, parity, order):
        self.parity, self.order = parity, order


def _gather_small(packed, name):
    R = packed.shape[0]

    def body(x_ref, out_ref, send_sems, recv_sems):
        x, y, c = _position()
        me = 4 * x + 2 * y + c
        out_ref[me] = x_ref[...]
        copies = []
        for k in range(1, N_DEV):
            to = (x ^ ((k >> 2) & 1), y ^ ((k >> 1) & 1), c ^ (k & 1))
            cp = pltpu.make_async_remote_copy(
                src_ref=x_ref, dst_ref=out_ref.at[me],
                send_sem=send_sems.at[k], recv_sem=recv_sems.at[k], device_id=to, device_id_type=MESH)
            cp.start()
            copies.append((k, to, cp))
        for k, to, cp in copies:
            cp.wait_send()
            pltpu.make_async_remote_copy(
                src_ref=x_ref, dst_ref=out_ref.at[4 * to[0] + 2 * to[1] + to[2]],
                send_sem=send_sems.at[k], recv_sem=recv_sems.at[k], device_id=to, device_id_type=MESH).wait_recv()

    return pl.pallas_call(
        body, name=name,
        in_specs=[pl.BlockSpec(memory_space=pltpu.VMEM)], out_specs=pl.BlockSpec(memory_space=pltpu.VMEM),
        out_shape=jax.ShapeDtypeStruct((N_DEV, R, 128), F32),
        scratch_shapes=[pltpu.SemaphoreType.DMA((N_DEV,)), pltpu.SemaphoreType.DMA((N_DEV,))],
    )(packed)


def _adamw_math(w, g, m, v):
    m = ADAM_B1 * m + (1.0 - ADAM_B1) * g
    v = ADAM_B2 * v + (1.0 - ADAM_B2) * (g * g)
    m_hat = m / (1.0 - ADAM_B1 ** ADAM_STEP)
    v_hat = v / (1.0 - ADAM_B2 ** ADAM_STEP)
    delta = -ADAM_LR * (m_hat / (jnp.sqrt(v_hat) + ADAM_EPS) + ADAM_WD * w)
    return delta, m, v


def _adamw_big(w, m, v, parts, name):
    R, C = w.shape
    tr = _pick(R, (256,))

    def body(w_ref, m_ref, v_ref, p_ref, g_ref, d_ref, nm_ref, nv_ref):
        g = p_ref[0].astype(F32)
        for s in range(1, N_CHIP):
            g = g + p_ref[s].astype(F32)
        d, nm, nv = _adamw_math(w_ref[...], g, m_ref[...], v_ref[...])
        g_ref[...] = g
        d_ref[...] = d
        nm_ref[...] = nm
        nv_ref[...] = nv

    blk = pl.BlockSpec((tr, C), lambda i: (i, 0))
    out = jax.ShapeDtypeStruct((R, C), F32)
    return pl.pallas_call(
        body, name=name, grid=(R // tr,),
        in_specs=[blk, blk, blk, pl.BlockSpec((N_CHIP, tr, C), lambda i: (0, i, 0))],
        out_specs=[blk, blk, blk, blk], out_shape=[out, out, out, out],
        compiler_params=_cparams(("parallel",)),
    )(w, m, v, parts)


def _adamw_big_landed(w, m, v, parts, lands, slot, name):
    R, C = w.shape
    tr = _pick(R, (256,))

    def body(slot_ref, w_ref, m_ref, v_ref, own_ref, l1_ref, l2_ref, l3_ref, g_ref, d_ref, nm_ref, nv_ref):
        g = own_ref[...].astype(F32)
        for ref in (l1_ref, l2_ref, l3_ref):
            g = g + ref[...].astype(F32)
        d, nm, nv = _adamw_math(w_ref[...], g, m_ref[...], v_ref[...])
        g_ref[...] = g
        d_ref[...] = d
        nm_ref[...] = nm
        nv_ref[...] = nv

    blk = pl.BlockSpec((tr, C), lambda i, slot: (i, 0))

    def chip(k):
        return pl.BlockSpec((None, tr, C), lambda i, slot: ((slot[0] + k) % N_CHIP, i, 0))

    out = jax.ShapeDtypeStruct((R, C), F32)
    return pl.pallas_call(
        body, name=name,
        grid_spec=pltpu.PrefetchScalarGridSpec(
            num_scalar_prefetch=1, grid=(R // tr,),
            in_specs=[blk, blk, blk, chip(0), chip(1), chip(2), chip(3)],
            out_specs=[blk, blk, blk, blk]),
        out_shape=[out, out, out, out],
        compiler_params=_cparams(("parallel",)),
    )(slot, w, m, v, parts, lands, lands, lands)


def _adamw_small(w, m, v, gathered, name):
    R = w.shape[0]

    def body(w_ref, m_ref, v_ref, p_ref, g_ref, d_ref, nm_ref, nv_ref):
        g = p_ref[0]
        for s in range(1, N_DEV):
            g = g + p_ref[s]
        d, nm, nv = _adamw_math(w_ref[...], g, m_ref[...], v_ref[...])
        g_ref[...] = g
        d_ref[...] = d
        nm_ref[...] = nm
        nv_ref[...] = nv

    out = jax.ShapeDtypeStruct((R, 128), F32)
    return pl.pallas_call(
        body, name=name, out_shape=[out, out, out, out],
    )(w, m, v, gathered)


SMALL_NAMES = ("lb_logits", "hg_norm_w", "rel_bias", "norm_mix_w", "norm_mlp_w", "norm_final_w")
SMALL_SHAPES = {"lb_logits": (2, HG_WIDTH), "hg_norm_w": (1, HG_DK), "rel_bias": (AT_HEADS, N_REL_PAD),
                "norm_mix_w": (1, D_MODEL), "norm_mlp_w": (1, D_MODEL), "norm_final_w": (1, D_MODEL)}


def _pack_small(parts):
    rows = []
    for nme in SMALL_NAMES:
        p = parts[nme]
        if nme == "rel_bias":
            p = jnp.pad(p, ((0, 0), (0, N_REL_PAD - N_REL)))
        rows.append(p.reshape(-1, 128))
    flat = jnp.concatenate(rows, axis=0)
    return jnp.pad(flat, ((0, SMALL_ROWS - flat.shape[0]), (0, 0)))


def _unpack_small(packed):
    out, at = {}, 0
    for nme in SMALL_NAMES:
        shp = SMALL_SHAPES[nme]
        nrow = shp[0] * shp[1] // 128
        p = packed[at:at + nrow].reshape(shp)
        at += nrow
        out[nme] = p[:, :N_REL] if nme == "rel_bias" else p
    return out


BIG_NAMES = ("w_in", "w_branch_a", "w_branch_b", "w_out", "w_up", "w_down")


def kernel(x, w_in, lb_logits, hg_norm_w, rel_bias, w_branch_a, w_branch_b, w_out, norm_mix_w, norm_mlp_w, w_up, w_down, norm_final_w, loss_target, m_w_in, m_lb_logits, m_hg_norm_w, m_rel_bias, m_w_branch_a, m_w_branch_b, m_w_out, m_norm_mix_w, m_norm_mlp_w, m_w_up, m_w_down, m_norm_final_w, v_w_in, v_lb_logits, v_hg_norm_w, v_rel_bias, v_w_branch_a, v_w_branch_b, v_w_out, v_norm_mix_w, v_norm_mlp_w, v_w_up, v_w_down, v_norm_final_w):
    big_w = [w_in[0], w_branch_a[0], w_branch_b[0], w_out[0], w_up[0], w_down[0]]
    big_m = [m_w_in[0], m_w_branch_a[0], m_w_branch_b[0], m_w_out[0], m_w_up[0], m_w_down[0]]
    big_v = [v_w_in[0], v_w_branch_a[0], v_w_branch_b[0], v_w_out[0], v_w_up[0], v_w_down[0]]

    shards = [w.astype(BF16) for w in big_w]
    parity = lax.axis_index("c").astype(jnp.int32).reshape(1)
    loss_part, grad_x, chip_parts, small = _local_step(
        x[0], loss_target[0], lb_logits, hg_norm_w, rel_bias[0], norm_mix_w, norm_mlp_w,
        norm_final_w.reshape(1, D_MODEL), shards[0], shards[1:], _Exchanges(parity, _gather_order()))
    loss = lax.psum(loss_part[0, 0], ("x", "y", "c"))
    rs_in, rs_mix, rs_up, rs_down = chip_parts
    slot = (2 * lax.axis_index("x") + lax.axis_index("y")).astype(jnp.int32).reshape(1)
    big = {}

    def finish(rs, names, after):
        sums, lands = rs.finish(after)
        for nme, own, land in zip(names, sums, lands):
            i = BIG_NAMES.index(nme)
            big[nme] = _adamw_big_landed(big_w[i], big_m[i], big_v[i], own, land, slot, "adamw_" + nme)
        return [big[nme][1] for nme in names]

    done = finish(rs_down, ["w_down"], [grad_x])
    done = finish(rs_up, ["w_up"], done)
    done = finish(rs_mix, ["w_branch_a", "w_branch_b", "w_out"], done)

    sw = dict(lb_logits=lb_logits, hg_norm_w=hg_norm_w, rel_bias=rel_bias[0], norm_mix_w=norm_mix_w,
              norm_mlp_w=norm_mlp_w, norm_final_w=norm_final_w.reshape(1, D_MODEL))
    sm = dict(lb_logits=m_lb_logits, hg_norm_w=m_hg_norm_w, rel_bias=m_rel_bias[0], norm_mix_w=m_norm_mix_w,
              norm_mlp_w=m_norm_mlp_w, norm_final_w=m_norm_final_w.reshape(1, D_MODEL))
    sv = dict(lb_logits=v_lb_logits, hg_norm_w=v_hg_norm_w, rel_bias=v_rel_bias[0], norm_mix_w=v_norm_mix_w,
              norm_mlp_w=v_norm_mlp_w, norm_final_w=v_norm_final_w.reshape(1, D_MODEL))
    gathered = _gather_small(_pack_small(small), "gather_small")
    small_packed = _adamw_small(_pack_small(sw), _pack_small(sm), _pack_small(sv), gathered, "adamw_small")
    small_out = [_unpack_small(p) for p in small_packed]

    finish(rs_in, ["w_in"], done + [small_packed[0]])

    def leaf(kind, nme):
        if nme in BIG_NAMES:
            return big[nme][kind][None]
        p = small_out[kind][nme]
        if nme == "rel_bias":
            return p[None]
        if nme == "norm_final_w":
            return p.reshape(D_MODEL)
        return p

    order = ("w_in", "lb_logits", "hg_norm_w", "rel_bias", "w_branch_a", "w_branch_b", "w_out", "norm_mix_w",
             "norm_mlp_w", "w_up", "w_down", "norm_final_w")
    outs = [loss, grad_x[None]]
    for kind in range(4):
        outs += [leaf(kind, nme) for nme in order]
    return tuple(outs)
```

```python
import functools

import jax
import jax.numpy as jnp
from jax import lax
from jax.experimental import pallas as pl
from jax.experimental.pallas import tpu as pltpu

F32 = jnp.float32
BF16 = jnp.bfloat16
HIGHEST = lax.Precision.HIGHEST
MESH = pl.DeviceIdType.MESH

D_MODEL = 2048
HG_HEADS = 8
HG_DK = 128
HG_WIDTH = 1024
AT_HEADS = 16
AT_DH = 64
AT_WIDTH = 1024
CHUNK = 64
LEFT_CHUNKS = 8
BAND = (LEFT_CHUNKS + 1) * CHUNK
PAD = LEFT_CHUNKS * CHUNK
REL_CLIP = 256
N_REL = 2 * REL_CLIP + 1
N_REL_PAD = 640
D_FF = 4 * D_MODEL
D_IN = 4 * HG_WIDTH + 3 * AT_WIDTH + 2 * D_MODEL
EPS = 1e-6
N_DEV = 8
N_CHIP = 4

ADAM_LR = 0.001
ADAM_B1 = 0.9
ADAM_B2 = 0.999
ADAM_EPS = 1e-08
ADAM_WD = 0.01
ADAM_STEP = 10

COL_HQ, COL_HF, COL_HI, COL_HG = 0, 8, 16, 24
COL_AQ, COL_AK, COL_AV = 32, 40, 48
COL_GATE_A, COL_GATE_B = 7, 9

VMEM_LIMIT = 56 * 1024 * 1024
SMALL_ROWS = 152


def _cparams(sem=None, **kw):
    if sem is not None:
        kw["dimension_semantics"] = sem
    return pltpu.CompilerParams(vmem_limit_bytes=VMEM_LIMIT, **kw)


def _pick(n, cands):
    for c in cands:
        if n % c == 0:
            return c
    return n


def _sigmoid(x):
    return 1.0 / (1.0 + jnp.exp(-x))


ANY = pl.BlockSpec(memory_space=pl.ANY)


def _position():
    return lax.axis_index("x"), lax.axis_index("y"), lax.axis_index("c")


def _call(body, args, *, name, grid, in_specs, out_specs, out_shape, scratch_shapes=(), sem=None, after=()):
    n_in = len(args)

    def ordered(*refs):
        body(*refs[:n_in], *refs[n_in + len(after):])

    return list(pl.pallas_call(
        ordered if after else body, name=name, grid=grid, in_specs=list(in_specs) + [ANY] * len(after),
        out_specs=out_specs, out_shape=out_shape, scratch_shapes=list(scratch_shapes),
        compiler_params=_cparams(sem))(*args, *after))


def _accumulate(part, acc_ref, step, n_steps, finish):
    if n_steps == 1:
        finish(part)
        return

    @pl.when(step == 0)
    def _():
        acc_ref[...] = part

    @pl.when(step > 0)
    def _():
        acc_ref[...] += part

    @pl.when(step == n_steps - 1)
    def _():
        finish(acc_ref[...])


def _mm_nn(a, wb, out_dtype, name, after=(), squared_relu=False):
    M, K = a.shape
    NB, K2, Nb = wb.shape
    assert K == K2
    tm = min(M, 1024)
    tk = min(K, 2048)
    tn = _pick(Nb, (512, 1408, 256))
    nk = K // tk
    nn = Nb // tn

    def body(a_ref, b_ref, o_ref, *rest):
        def finish(total):
            o_ref[...] = total.astype(out_dtype)
            if squared_relu:
                ra = jnp.maximum(total, 0.0)
                rest[0][...] = (ra * ra).astype(BF16)

        part = jnp.dot(a_ref[...], b_ref[...], preferred_element_type=F32)
        _accumulate(part, rest[-1] if nk > 1 else None, pl.program_id(3), nk, finish)

    tile = pl.BlockSpec((tm, tn), lambda m, j, n, k: (m, j * nn + n))
    outs = _call(
        body, (a, wb), name=name, grid=(M // tm, NB, nn, nk),
        in_specs=[pl.BlockSpec((tm, tk), lambda m, j, n, k: (m, k)),
                  pl.BlockSpec((None, tk, tn), lambda m, j, n, k: (j, k, n))],
        out_specs=[tile, tile] if squared_relu else [tile],
        out_shape=[jax.ShapeDtypeStruct((M, NB * Nb), out_dtype)]
        + ([jax.ShapeDtypeStruct((M, NB * Nb), BF16)] if squared_relu else []),
        scratch_shapes=[] if nk == 1 else [pltpu.VMEM((tm, tn), F32)],
        sem=("parallel", "parallel", "parallel", "arbitrary"), after=after)
    return outs if squared_relu else outs[0]


def _mm_nt(a, wb, out_dtype, name, after=(), relu_of=None):
    M, N = a.shape
    NB, K, Nb = wb.shape
    assert N == NB * Nb
    tm = min(M, 1024)
    tko = _pick(K, (1024,))
    tc = _pick(Nb, (2048, 1024, 1408, 256))
    nc = Nb // tc
    nsteps = NB * nc
    gated = relu_of is not None

    def body(a_ref, b_ref, *rest):
        o_ref = rest[1] if gated else rest[0]

        def finish(total):
            if gated:
                total = total * (2.0 * jnp.maximum(rest[0][...], 0.0))
            o_ref[...] = total.astype(out_dtype)

        part = lax.dot_general(a_ref[...], b_ref[...], (((1,), (1,)), ((), ())), preferred_element_type=F32)
        _accumulate(part, rest[-1], pl.program_id(2) * nc + pl.program_id(3), nsteps, finish)

    tile = pl.BlockSpec((tm, tko), lambda m, ko, j, c: (m, ko))
    out, = _call(
        body, (a, wb) + ((relu_of,) if gated else ()), name=name,
        grid=(M // tm, K // tko, NB, nc),
        in_specs=[pl.BlockSpec((tm, tc), lambda m, ko, j, c: (m, j * nc + c)),
                  pl.BlockSpec((None, tko, tc), lambda m, ko, j, c: (j, ko, c))] + ([tile] if gated else []),
        out_specs=[tile],
        out_shape=[jax.ShapeDtypeStruct((M, K), out_dtype)],
        scratch_shapes=[] if nsteps == 1 else [pltpu.VMEM((tm, tko), F32)],
        sem=("parallel", "parallel", "arbitrary", "arbitrary"), after=after)
    return out


def _mm_tn(a, g, nb, out_dtype, name, after=()):
    M, Ka = a.shape
    M2, N = g.shape
    assert M == M2 and N % nb == 0
    Nb = N // nb
    tka = _pick(Ka, (1024,))
    tn = _pick(Nb, (512, 1408, 256))
    nn = Nb // tn

    def body(a_ref, g_ref, o_ref):
        o_ref[...] = lax.dot_general(a_ref[...], g_ref[...], (((0,), (0,)), ((), ())),
                                     preferred_element_type=F32).astype(out_dtype)

    return _call(
        body, (a, g), name=name,
        grid=(Ka // tka, nb, nn),
        in_specs=[pl.BlockSpec((M, tka), lambda ka, j, n: (0, ka)),
                  pl.BlockSpec((M, tn), lambda ka, j, n: (0, j * nn + n))],
        out_specs=[pl.BlockSpec((None, tka, tn), lambda ka, j, n: (j, ka, n))],
        out_shape=[jax.ShapeDtypeStruct((nb, Ka, Nb), out_dtype)],
        sem=("parallel", "parallel", "parallel"), after=after)[0]


ROW_TILE = 256


def _rms_fwd(x, w, name):
    T, Dm = x.shape

    def body(x_ref, w_ref, u_ref):
        xv = x_ref[...]
        r = lax.rsqrt(jnp.mean(xv * xv, axis=-1, keepdims=True) + EPS)
        u_ref[...] = (xv * r * w_ref[...]).astype(BF16)

    return pl.pallas_call(
        body, name=name, grid=(T // ROW_TILE,),
        in_specs=[pl.BlockSpec((ROW_TILE, Dm), lambda i: (i, 0)), pl.BlockSpec((1, Dm), lambda i: (0, 0))],
        out_specs=pl.BlockSpec((ROW_TILE, Dm), lambda i: (i, 0)),
        out_shape=jax.ShapeDtypeStruct((T, Dm), BF16),
        compiler_params=_cparams(("parallel",)),
    )(x, w)


def _resid_rms_fwd(x, mix, w, name):
    T, Dm = x.shape

    def body(x_ref, m_ref, w_ref, h_ref, u_ref):
        h = x_ref[...] + m_ref[...]
        h_ref[...] = h
        r = lax.rsqrt(jnp.mean(h * h, axis=-1, keepdims=True) + EPS)
        u_ref[...] = (h * r * w_ref[...]).astype(BF16)

    row = pl.BlockSpec((ROW_TILE, Dm), lambda i: (i, 0))
    return pl.pallas_call(
        body, name=name, grid=(T // ROW_TILE,),
        in_specs=[row, row, pl.BlockSpec((1, Dm), lambda i: (0, 0))],
        out_specs=[row, row],
        out_shape=[jax.ShapeDtypeStruct((T, Dm), F32), jax.ShapeDtypeStruct((T, Dm), BF16)],
        compiler_params=_cparams(("parallel",)),
    )(x, mix, w)


def _loss_head(h1, mlp, wf, target, name):
    T, Dm = h1.shape

    def body(h_ref, m_ref, w_ref, t_ref, loss_ref, dh_ref, dhb_ref, dw_ref):
        i = pl.program_id(0)
        h = h_ref[...] + m_ref[...]
        r = lax.rsqrt(jnp.mean(h * h, axis=-1, keepdims=True) + EPS)
        xh = h * r
        wv = w_ref[...]
        e = xh * wv - t_ref[...]
        part = 0.5 * jnp.sum(jnp.mean(e * e, axis=-1, keepdims=True), axis=0, keepdims=True)
        dy = e * (1.0 / Dm)
        dw = jnp.sum(dy * xh, axis=0, keepdims=True)
        gy = dy * wv
        dh = r * (gy - xh * jnp.mean(gy * xh, axis=-1, keepdims=True))
        dh_ref[...] = dh
        dhb_ref[...] = dh.astype(BF16)

        @pl.when(i == 0)
        def _():
            loss_ref[...] = jnp.zeros_like(loss_ref)
            dw_ref[...] = jnp.zeros_like(dw_ref)

        loss_ref[...] += jnp.broadcast_to(part, loss_ref.shape)
        dw_ref[...] += dw

    row = pl.BlockSpec((ROW_TILE, Dm), lambda i: (i, 0))
    vec = pl.BlockSpec((1, Dm), lambda i: (0, 0))
    return pl.pallas_call(
        body, name=name, grid=(T // ROW_TILE,),
        in_specs=[row, row, vec, row],
        out_specs=[pl.BlockSpec((8, 128), lambda i: (0, 0)), row, row, vec],
        out_shape=[jax.ShapeDtypeStruct((8, 128), F32), jax.ShapeDtypeStruct((T, Dm), F32),
                   jax.ShapeDtypeStruct((T, Dm), BF16), jax.ShapeDtypeStruct((1, Dm), F32)],
        compiler_params=_cparams(("arbitrary",)),
    )(h1, mlp, wf, target)


def _rms_bwd(dyn, x, w, dres, name, after=()):
    T, Dm = x.shape

    def body(g_ref, x_ref, w_ref, r_ref, dx_ref, dxb_ref, dw_ref):
        i = pl.program_id(0)
        xv = x_ref[...]
        r = lax.rsqrt(jnp.mean(xv * xv, axis=-1, keepdims=True) + EPS)
        xh = xv * r
        g = g_ref[...]
        dw = jnp.sum(g * xh, axis=0, keepdims=True)
        gy = g * w_ref[...]
        dx = r_ref[...] + r * (gy - xh * jnp.mean(gy * xh, axis=-1, keepdims=True))
        dx_ref[...] = dx
        dxb_ref[...] = dx.astype(BF16)

        @pl.when(i == 0)
        def _():
            dw_ref[...] = jnp.zeros_like(dw_ref)

        dw_ref[...] += dw

    row = pl.BlockSpec((ROW_TILE, Dm), lambda i: (i, 0))
    vec = pl.BlockSpec((1, Dm), lambda i: (0, 0))
    return _call(
        body, (dyn, x, w, dres), name=name, grid=(T // ROW_TILE,),
        in_specs=[row, row, vec, row],
        out_specs=[row, row, vec],
        out_shape=[jax.ShapeDtypeStruct((T, Dm), F32), jax.ShapeDtypeStruct((T, Dm), BF16),
                   jax.ShapeDtypeStruct((1, Dm), F32)],
        sem=("arbitrary",), after=after)


COL_TILE = 2048


def _relu2_fwd(a, name):
    T, N = a.shape

    def body(a_ref, r_ref):
        ra = jnp.maximum(a_ref[...], 0.0)
        r_ref[...] = (ra * ra).astype(BF16)

    blk = pl.BlockSpec((ROW_TILE, COL_TILE), lambda i, j: (i, j))
    return pl.pallas_call(
        body, name=name, grid=(T // ROW_TILE, N // COL_TILE), in_specs=[blk], out_specs=blk,
        out_shape=jax.ShapeDtypeStruct((T, N), BF16),
        compiler_params=_cparams(("parallel", "parallel")),
    )(a)


def _relu2_bwd(dr, a, name, after=()):
    T, N = a.shape

    def body(dr_ref, a_ref, da_ref):
        da_ref[...] = (dr_ref[...] * (2.0 * jnp.maximum(a_ref[...], 0.0))).astype(BF16)

    blk = pl.BlockSpec((ROW_TILE, COL_TILE), lambda i, j: (i, j))
    return _call(
        body, (dr, a), name=name, grid=(T // ROW_TILE, N // COL_TILE), in_specs=[blk, blk], out_specs=[blk],
        out_shape=[jax.ShapeDtypeStruct((T, N), BF16)], sem=("parallel", "parallel"), after=after)[0]


GATE_TILE = 1024


def _merge_fwd(z, pa, pb, name):
    T, Dm = pa.shape

    def body(za_ref, zb_ref, pa_ref, pb_ref, m_ref):
        m_ref[...] = (_sigmoid(za_ref[...]) * pa_ref[...] + _sigmoid(zb_ref[...]) * pb_ref[...]).astype(BF16)

    blk = pl.BlockSpec((ROW_TILE, GATE_TILE), lambda i, j: (i, j))
    return pl.pallas_call(
        body, name=name, grid=(T // ROW_TILE, Dm // GATE_TILE),
        in_specs=[pl.BlockSpec((ROW_TILE, GATE_TILE), lambda i, j: (i, COL_GATE_A + j)),
                  pl.BlockSpec((ROW_TILE, GATE_TILE), lambda i, j: (i, COL_GATE_B + j)), blk, blk],
        out_specs=blk,
        out_shape=jax.ShapeDtypeStruct((T, Dm), BF16),
        compiler_params=_cparams(("parallel", "parallel")),
    )(z, z, pa, pb)


def _merge_bwd(dm, z, pa, pb, name):
    T, Dm = pa.shape

    def body(dm_ref, za_ref, zb_ref, pa_ref, pb_ref, dpa_ref, dpb_ref, dga_ref, dgb_ref):
        d = dm_ref[...]
        ga = _sigmoid(za_ref[...])
        gb = _sigmoid(zb_ref[...])
        dpa_ref[...] = (d * ga).astype(BF16)
        dpb_ref[...] = (d * gb).astype(BF16)
        dga_ref[...] = (d * pa_ref[...] * ga * (1.0 - ga)).astype(BF16)
        dgb_ref[...] = (d * pb_ref[...] * gb * (1.0 - gb)).astype(BF16)

    blk = pl.BlockSpec((ROW_TILE, GATE_TILE), lambda i, j: (i, j))
    out = jax.ShapeDtypeStruct((T, Dm), BF16)
    return pl.pallas_call(
        body, name=name, grid=(T // ROW_TILE, Dm // GATE_TILE),
        in_specs=[blk, pl.BlockSpec((ROW_TILE, GATE_TILE), lambda i, j: (i, COL_GATE_A + j)),
                  pl.BlockSpec((ROW_TILE, GATE_TILE), lambda i, j: (i, COL_GATE_B + j)), blk, blk],
        out_specs=[blk, blk, blk, blk],
        out_shape=[out, out, out, out],
        compiler_params=_cparams(("parallel", "parallel")),
    )(dm, z, z, pa, pb)


def _dot_hi(a, b, dims):
    return lax.dot_general(a, b, (dims, ((), ())), precision=HIGHEST, preferred_element_type=F32)


NN = ((1,), (0,))
NT = ((1,), (1,))
TN = ((0,), (0,))


def _hg_gates(hq, hf, lb):
    sq = _sigmoid(hq)
    q = hq * sq * (HG_DK ** -0.5)
    f = _sigmoid(hf)
    g = lb + (1.0 - lb) * f
    return q, sq, f, g, jnp.log(g), 1.0 - g


def _tri(lower):
    r = lax.broadcasted_iota(jnp.int32, (CHUNK, CHUNK), 0)
    c = lax.broadcasted_iota(jnp.int32, (CHUNK, CHUNK), 1)
    return jnp.where((r >= c) if lower else (r <= c), 1.0, 0.0).astype(F32)


def _hgrn2_fwd(z, lb_logits, hg_norm_w, name, after=()):
    T = z.shape[0]
    n_chunks = T // CHUNK

    def body(hq_ref, hf_ref, hi_ref, hg_ref, lbl_ref, nw_ref, o_ref, ya_ref, sall_ref, st_ref):
        lbl = lbl_ref[...]
        lb = 1.0 / (1.0 + jnp.exp(lbl[1:2, :] - lbl[0:1, :]))
        st_ref[...] = jnp.zeros_like(st_ref)
        tri = _tri(True)
        row8 = lax.broadcasted_iota(jnp.int32, (8, HG_DK), 0)

        def chunk(c, carry):
            rows = pl.ds(pl.multiple_of(c * CHUNK, CHUNK), CHUNK)
            q, _, _, _, lg, kk = _hg_gates(hq_ref[rows, :], hf_ref[rows, :], lb)
            v = hi_ref[rows, :]
            b = _dot_hi(tri, lg, NN)
            st = st_ref[...]
            sall_ref[c] = st
            o_inter = _dot_hi(q * jnp.exp(b), st, NT)
            for g8 in range(CHUNK // 8):
                n = 8 * (g8 + 1)
                bs, ks, vs = b[:n], kk[:n], v[:n]
                sidx = lax.broadcasted_iota(jnp.int32, (n, HG_DK), 0)
                blk = o_inter[8 * g8:n]
                for i in range(8):
                    t = 8 * g8 + i
                    e = jnp.where(sidx <= t, jnp.exp(b[t:t + 1] - bs), 0.0)
                    p = jnp.sum(e * ks * q[t:t + 1], axis=1, keepdims=True)
                    ot = jnp.sum(p * vs, axis=0, keepdims=True)
                    blk = blk + jnp.where(row8 == i, ot, 0.0)
                o_ref[pl.ds(pl.multiple_of(c * CHUNK + 8 * g8, 8), 8), :] = blk
            bl = b[CHUNK - 1:CHUNK]
            ke = kk * jnp.exp(bl - b)
            st_ref[...] = st * jnp.exp(bl) + _dot_hi(v, ke, TN)
            return carry

        lax.fori_loop(0, n_chunks, chunk, 0)
        o = o_ref[...]
        r = lax.rsqrt(jnp.mean(o * o, axis=-1, keepdims=True) + EPS)
        hg = hg_ref[...]
        ya_ref[...] = (o * r * nw_ref[...] * (hg * _sigmoid(hg))).astype(BF16)

    def col(base):
        return pl.BlockSpec((T, HG_DK), lambda h: (0, base + h))

    return _call(
        body, (z, z, z, z, lb_logits, hg_norm_w), name=name, grid=(HG_HEADS,),
        in_specs=[col(COL_HQ), col(COL_HF), col(COL_HI), col(COL_HG),
                  pl.BlockSpec((2, HG_DK), lambda h: (0, h)), pl.BlockSpec((1, HG_DK), lambda h: (0, 0))],
        out_specs=[col(0), col(0), pl.BlockSpec((None, n_chunks, HG_DK, HG_DK), lambda h: (h, 0, 0, 0))],
        out_shape=[jax.ShapeDtypeStruct((T, HG_WIDTH), F32), jax.ShapeDtypeStruct((T, HG_WIDTH), BF16),
                   jax.ShapeDtypeStruct((HG_HEADS, n_chunks, HG_DK, HG_DK), F32)],
        scratch_shapes=[pltpu.VMEM((HG_DK, HG_DK), F32)],
        sem=("parallel",), after=after)


def _hgrn2_bwd(z, lb_logits, hg_norm_w, o_raw, s_all, dya, name, after=()):
    T = z.shape[0]
    n_chunks = T // CHUNK

    def body(hq_ref, hf_ref, hi_ref, hg_ref, lbl_ref, nw_ref, o_ref, sall_ref, dya_ref,
             dhq_ref, dhf_ref, dhi_ref, dhg_ref, dlbl_ref, dnw_ref,
             do_ref, dst_ref, dq_ref, dk_ref, dv_ref, dlb_ref):
        h = pl.program_id(0)
        lbl = lbl_ref[...]
        lb = 1.0 / (1.0 + jnp.exp(lbl[1:2, :] - lbl[0:1, :]))

        o = o_ref[...]
        r = lax.rsqrt(jnp.mean(o * o, axis=-1, keepdims=True) + EPS)
        oh = o * r
        nw = nw_ref[...]
        hg = hg_ref[...]
        sg = _sigmoid(hg)
        dy = dya_ref[...]
        d_on = dy * (hg * sg)
        dhg_ref[...] = (dy * (oh * nw) * (sg * (1.0 + hg * (1.0 - sg)))).astype(BF16)
        dnw = jnp.sum(d_on * oh, axis=0, keepdims=True)
        gy = d_on * nw
        do_ref[...] = r * (gy - oh * jnp.mean(gy * oh, axis=-1, keepdims=True))

        @pl.when(h == 0)
        def _():
            dnw_ref[...] = jnp.zeros_like(dnw_ref)

        dnw_ref[...] += jnp.broadcast_to(dnw, dnw_ref.shape)

        dst_ref[...] = jnp.zeros_like(dst_ref)
        dlb_ref[...] = jnp.zeros_like(dlb_ref)
        tri = _tri(True)
        tri_t = _tri(False)
        row8 = lax.broadcasted_iota(jnp.int32, (8, HG_DK), 0)

        def chunk(ci, carry):
            c = n_chunks - 1 - ci
            rows = pl.ds(pl.multiple_of(c * CHUNK, CHUNK), CHUNK)
            hq = hq_ref[rows, :]
            q, sq, f, g, lg, kk = _hg_gates(hq, hf_ref[rows, :], lb)
            v = hi_ref[rows, :]
            do = do_ref[rows, :]
            b = _dot_hi(tri, lg, NN)
            eb = jnp.exp(b)
            bl = b[CHUNK - 1:CHUNK]
            ebl = jnp.exp(bl)
            ekb = jnp.exp(bl - b)
            qe = q * eb
            ke = kk * ekb
            st = sall_ref[c]
            dst = dst_ref[...]
            dqe = _dot_hi(do, st, NN)
            dke = _dot_hi(v, dst, NN)
            dv_inter = _dot_hi(ke, dst, NT)
            d_ebl = jnp.sum(st * dst, axis=0, keepdims=True)
            dst_ref[...] = dst * ebl + _dot_hi(do, qe, TN)

            dk_ref[...] = jnp.zeros_like(dk_ref)
            dv_ref[...] = jnp.zeros_like(dv_ref)
            for g8 in range(CHUNK // 8):
                n = 8 * (g8 + 1)
                bs, ks, vs = b[:n], kk[:n], v[:n]
                sidx = lax.broadcasted_iota(jnp.int32, (n, HG_DK), 0)
                blk = jnp.zeros((8, HG_DK), F32)
                for i in range(8):
                    t = 8 * g8 + i
                    qt = q[t:t + 1]
                    dot_ = do[t:t + 1]
                    e = jnp.where(sidx <= t, jnp.exp(b[t:t + 1] - bs), 0.0)
                    w = e * ks
                    p = jnp.sum(w * qt, axis=1, keepdims=True)
                    dsc = jnp.sum(vs * dot_, axis=1, keepdims=True)
                    dqt = jnp.sum(dsc * w, axis=0, keepdims=True)
                    blk = blk + jnp.where(row8 == i, dqt, 0.0)
                    dk_ref[0:n, :] += dsc * e * qt
                    dv_ref[0:n, :] += p * dot_
                dq_ref[8 * g8:n, :] = blk
            dq_i = dq_ref[...]
            dk_i = dk_ref[...]
            dke_ke = dke * ke
            db = q * dq_i - kk * dk_i + dqe * qe - dke_ke
            db_last = jnp.sum(dke_ke, axis=0, keepdims=True) + d_ebl * ebl
            dlg = _dot_hi(tri_t, db, NN) + db_last
            dq = dq_i + dqe * eb
            dkk = dk_i + dke * ekb
            dg = dlg / g - dkk
            dhq_ref[rows, :] = (dq * (HG_DK ** -0.5) * (sq * (1.0 + hq * (1.0 - sq)))).astype(BF16)
            dhf_ref[rows, :] = (dg * (1.0 - lb) * f * (1.0 - f)).astype(BF16)
            dhi_ref[rows, :] = (dv_ref[...] + dv_inter).astype(BF16)
            dlb_ref[...] += jnp.sum(dg * (1.0 - f), axis=0, keepdims=True)
            return carry

        lax.fori_loop(0, n_chunks, chunk, 0)
        dl0 = dlb_ref[...] * lb * (1.0 - lb)
        dlbl_ref[0:1, :] = dl0
        dlbl_ref[1:2, :] = -dl0

    def col(base):
        return pl.BlockSpec((T, HG_DK), lambda h: (0, base + h))

    outb = jax.ShapeDtypeStruct((T, HG_WIDTH), BF16)
    return _call(
        body, (z, z, z, z, lb_logits, hg_norm_w, o_raw, s_all, dya), name=name, grid=(HG_HEADS,),
        in_specs=[col(COL_HQ), col(COL_HF), col(COL_HI), col(COL_HG),
                  pl.BlockSpec((2, HG_DK), lambda h: (0, h)), pl.BlockSpec((1, HG_DK), lambda h: (0, 0)),
                  col(0), pl.BlockSpec((None, n_chunks, HG_DK, HG_DK), lambda h: (h, 0, 0, 0)), col(0)],
        out_specs=[col(0), col(0), col(0), col(0), pl.BlockSpec((2, HG_DK), lambda h: (0, h)),
                   pl.BlockSpec((8, HG_DK), lambda h: (0, 0))],
        out_shape=[outb, outb, outb, outb, jax.ShapeDtypeStruct((2, HG_WIDTH), F32),
                   jax.ShapeDtypeStruct((8, HG_DK), F32)],
        scratch_shapes=[pltpu.VMEM((T, HG_DK), F32), pltpu.VMEM((HG_DK, HG_DK), F32),
                        pltpu.VMEM((CHUNK, HG_DK), F32), pltpu.VMEM((CHUNK, HG_DK), F32),
                        pltpu.VMEM((CHUNK, HG_DK), F32), pltpu.VMEM((1, HG_DK), F32)],
        sem=("arbitrary",), after=after)


CONST_KEYS = PAD - REL_CLIP
VAR_KEYS = BAND - CONST_KEYS
REL_LO = 128
REL_SPAN = N_REL_PAD - REL_LO


def _rel_onehot(t):
    r = lax.broadcasted_iota(jnp.int32, (REL_SPAN, VAR_KEYS), 0)
    j = lax.broadcasted_iota(jnp.int32, (REL_SPAN, VAR_KEYS), 1)
    idx = jnp.clip(t + PAD - CONST_KEYS - j, -REL_CLIP, REL_CLIP) + REL_CLIP - REL_LO
    return jnp.where(r == idx, 1.0, 0.0).astype(BF16)


def _split3(x):
    hi = x.astype(BF16)
    r1 = x - hi.astype(F32)
    mid = r1.astype(BF16)
    return hi, mid, (r1 - mid.astype(F32)).astype(BF16)


def _bias_expand(rel, name):
    def body(rel_ref, out_ref):
        tab = rel_ref[...]
        onehot = _rel_onehot(pl.program_id(0))
        out_ref[:, 0:CONST_KEYS] = jnp.broadcast_to(tab[:, 2 * REL_CLIP:2 * REL_CLIP + 1], (AT_HEADS, CONST_KEYS))
        out_ref[:, CONST_KEYS:BAND] = sum(
            jnp.dot(piece, onehot, preferred_element_type=F32) for piece in _split3(tab[:, REL_LO:N_REL_PAD]))

    return pl.pallas_call(
        body, name=name, grid=(CHUNK,),
        in_specs=[pl.BlockSpec((AT_HEADS, N_REL_PAD), lambda t: (0, 0))],
        out_specs=pl.BlockSpec((None, AT_HEADS, BAND), lambda t: (t, 0, 0)),
        out_shape=jax.ShapeDtypeStruct((CHUNK, AT_HEADS, BAND), F32),
        compiler_params=_cparams(("parallel",)),
    )(rel)


def _bias_reduce(dbias_t, name, after=()):
    def body(db_ref, out_ref):
        t = pl.program_id(0)

        @pl.when(t == 0)
        def _():
            out_ref[...] = jnp.zeros_like(out_ref)

        db = db_ref[...]
        onehot = _rel_onehot(t)
        acc = sum(lax.dot_general(piece, onehot, (NT, ((), ())), preferred_element_type=F32)
                  for piece in _split3(db[:, CONST_KEYS:BAND]))
        lane = lax.broadcasted_iota(jnp.int32, (AT_HEADS, REL_SPAN), 1)
        last = jnp.sum(db[:, 0:CONST_KEYS], axis=1, keepdims=True)
        out_ref[:, REL_LO:N_REL_PAD] += acc + jnp.where(lane == 2 * REL_CLIP - REL_LO, last, 0.0)

    return _call(
        body, (dbias_t,), name=name, grid=(CHUNK,),
        in_specs=[pl.BlockSpec((None, AT_HEADS, BAND), lambda t: (t, 0, 0))],
        out_specs=[pl.BlockSpec((AT_HEADS, N_REL_PAD), lambda t: (0, 0))],
        out_shape=[jax.ShapeDtypeStruct((AT_HEADS, N_REL_PAD), F32)],
        sem=("arbitrary",), after=after)[0]


def _pair_lanes():
    return lax.broadcasted_iota(jnp.int32, (CHUNK, 2 * AT_DH), 1) < AT_DH


def _block_diag(a):
    first = _pair_lanes()
    return jnp.concatenate([jnp.where(first, a, 0.0), jnp.where(first, 0.0, a)], axis=0).astype(BF16)


def _diag_blocks(a):
    return jnp.where(_pair_lanes(), a[:CHUNK], a[CHUNK:])


def _band_probs_t(kb, qbd, bias_t, c):
    s = lax.dot_general(kb, qbd, (NT, ((), ())), preferred_element_type=F32) * (AT_DH ** -0.5) + bias_t
    j = lax.broadcasted_iota(jnp.int32, (BAND, 2 * AT_DH), 0)
    s = jnp.where(j + c * CHUNK >= PAD, s, -jnp.inf)
    p = jnp.exp(s - jnp.max(s, axis=0, keepdims=True))
    return p / jnp.sum(p, axis=0, keepdims=True)


def _fill_padded(dst_ref, src_ref, T):
    dst_ref[0:PAD, :] = jnp.zeros((PAD, 2 * AT_DH), BF16)
    dst_ref[PAD:PAD + T, :] = src_ref[...].astype(BF16)


def _attn_fwd(z, bias_t, name, after=()):
    T = z.shape[0]
    n_chunks = T // CHUNK

    def body(q_ref, k_ref, v_ref, bias_ref, y_ref, kp_ref, vp_ref):
        _fill_padded(kp_ref, k_ref, T)
        _fill_padded(vp_ref, v_ref, T)

        def chunk(c, carry):
            rows = pl.ds(pl.multiple_of(c * CHUNK, CHUNK), CHUNK)
            band = pl.ds(pl.multiple_of(c * CHUNK, CHUNK), BAND)
            p = _band_probs_t(kp_ref[band, :], _block_diag(q_ref[rows, :]), bias_ref[...], c)
            o2 = lax.dot_general(p.astype(BF16), vp_ref[band, :], (TN, ((), ())), preferred_element_type=F32)
            y_ref[rows, :] = _diag_blocks(o2).astype(BF16)
            return carry

        lax.fori_loop(0, n_chunks, chunk, 0, unroll=2)

    def col(base):
        return pl.BlockSpec((T, 128), lambda h: (0, base + h))

    return _call(
        body, (z, z, z, bias_t), name=name, grid=(AT_HEADS // 2,),
        in_specs=[col(COL_AQ), col(COL_AK), col(COL_AV), pl.BlockSpec((None, BAND, 128), lambda h: (h, 0, 0))],
        out_specs=[col(0)],
        out_shape=[jax.ShapeDtypeStruct((T, AT_WIDTH), BF16)],
        scratch_shapes=[pltpu.VMEM((PAD + T, 128), BF16), pltpu.VMEM((PAD + T, 128), BF16)],
        sem=("parallel",), after=after)


def _attn_bwd(z, bias_t, dyb, name, after=()):
    T = z.shape[0]
    n_chunks = T // CHUNK

    def body(q_ref, k_ref, v_ref, bias_ref, dy_ref, dq_ref, dk_ref, dv_ref, dbias_ref,
             kp_ref, vp_ref, dkp_ref, dvp_ref):
        _fill_padded(kp_ref, k_ref, T)
        _fill_padded(vp_ref, v_ref, T)
        dkp_ref[...] = jnp.zeros_like(dkp_ref)
        dvp_ref[...] = jnp.zeros_like(dvp_ref)
        dbias_ref[...] = jnp.zeros_like(dbias_ref)

        def chunk(c, carry):
            rows = pl.ds(pl.multiple_of(c * CHUNK, CHUNK), CHUNK)
            band = pl.ds(pl.multiple_of(c * CHUNK, CHUNK), BAND)
            qbd = _block_diag(q_ref[rows, :])
            dobd = _block_diag(dy_ref[rows, :])
            kb = kp_ref[band, :]
            vb = vp_ref[band, :]
            p = _band_probs_t(kb, qbd, bias_ref[...], c)
            dp = lax.dot_general(vb, dobd, (NT, ((), ())), preferred_element_type=F32)
            ds = p * (dp - jnp.sum(dp * p, axis=0, keepdims=True))
            dbias_ref[...] += ds
            dsb = ds.astype(BF16)
            dq2 = lax.dot_general(dsb, kb, (TN, ((), ())), preferred_element_type=F32)
            dq_ref[rows, :] = (_diag_blocks(dq2) * (AT_DH ** -0.5)).astype(BF16)
            dkp_ref[band, :] += jnp.dot(dsb, qbd, preferred_element_type=F32) * (AT_DH ** -0.5)
            dvp_ref[band, :] += jnp.dot(p.astype(BF16), dobd, preferred_element_type=F32)
            return carry

        lax.fori_loop(0, n_chunks, chunk, 0)
        dk_ref[...] = dkp_ref[PAD:PAD + T, :].astype(BF16)
        dv_ref[...] = dvp_ref[PAD:PAD + T, :].astype(BF16)

    def col(base):
        return pl.BlockSpec((T, 128), lambda h: (0, base + h))

    pair = pl.BlockSpec((None, BAND, 128), lambda h: (h, 0, 0))
    outb = jax.ShapeDtypeStruct((T, AT_WIDTH), BF16)
    return _call(
        body, (z, z, z, bias_t, dyb), name=name, grid=(AT_HEADS // 2,),
        in_specs=[col(COL_AQ), col(COL_AK), col(COL_AV), pair, col(0)],
        out_specs=[col(0), col(0), col(0), pair],
        out_shape=[outb, outb, outb, jax.ShapeDtypeStruct((AT_HEADS // 2, BAND, 128), F32)],
        scratch_shapes=[pltpu.VMEM((PAD + T, 128), BF16), pltpu.VMEM((PAD + T, 128), BF16),
                        pltpu.VMEM((PAD + T, 128), F32), pltpu.VMEM((PAD + T, 128), F32)],
        sem=("parallel",), after=after)


def _local_step(x, target, lb_logits, hg_norm_w, rel_bias, norm_mix_w, norm_mlp_w, norm_final_w,
                w_in, rest, exchanges=None):
    ex = exchanges
    rel = jnp.pad(rel_bias, ((0, 0), (0, N_REL_PAD - N_REL)))

    u = _rms_fwd(x, norm_mix_w, "rms_mix_fwd")
    if ex:
        z, w_in = _mm_gathered(u, w_in, ex.order, "mm_in_fwd")
        gather = _Gather(rest, [z], "ag")
        tok = [gather.token]
    else:
        z = _mm_nn(u, w_in, F32, "mm_in_fwd")
        w_a, w_b, w_out, w_up, w_down = rest
        tok = []
    o_raw, y_a, s_all = _hgrn2_fwd(z, lb_logits, hg_norm_w, "hgrn2_fwd", after=tok)
    if ex:
        tok = [gather.pass_on([0, 1, 2], [o_raw], "abo")]
    bias_rows = _bias_expand(rel, "bias_expand")
    bias_t = jnp.transpose(bias_rows.reshape(CHUNK, AT_HEADS // 2, 2, BAND), (1, 3, 2, 0)).reshape(
        AT_HEADS // 2, BAND, 2 * CHUNK)
    y_b, = _attn_fwd(z, bias_t, "attn_fwd", after=tok)
    if ex:
        tok = [gather.pass_on([3], [y_b], "up")]
        w_a, w_b, w_out = gather.finish([0, 1, 2], tok, "abo")
    pa = _mm_nn(y_a, w_a, F32, "mm_a_fwd")
    pb = _mm_nn(y_b, w_b, F32, "mm_b_fwd")
    merged = _merge_fwd(z, pa, pb, "merge_fwd")
    w_out1 = w_out.reshape(1, D_MODEL, D_MODEL)
    mix = _mm_nn(merged, w_out1, F32, "mm_out_fwd")
    h1, u2 = _resid_rms_fwd(x, mix, norm_mlp_w, "rms_mlp_fwd")
    if ex:
        tok = [gather.pass_on([4], [u2], "down")]
        w_up, = gather.finish([3], tok, "up")
    a, r = _mm_nn(u2, w_up, F32, "mm_up_fwd", squared_relu=True)
    if ex:
        w_down, = gather.finish([4], [r], "down")
    w_down1 = w_down.reshape(1, D_FF, D_MODEL)
    mlp = _mm_nn(r, w_down1, F32, "mm_down_fwd")
    loss, dh2, dh2b, g_nf = _loss_head(h1, mlp, norm_final_w, target, "loss_head")

    def reduce_scatter(grads, name):
        rs = _ReduceScatter(grads, ex.parity, name) if ex else None
        return rs, ([rs.token] if ex else [])

    g_down = _mm_tn(r, dh2b, 1, BF16, "mm_down_wgrad").reshape(N_DEV, D_FF // N_DEV, D_MODEL)
    rs_down, tok = reduce_scatter([g_down], "rs_down")
    da = _mm_nt(dh2b, w_down1, BF16, "mm_down_dgrad", after=tok, relu_of=a)
    tok = [rs_down.pair_sums([da])] if ex else []
    g_up = _mm_tn(u2, da, N_DEV, BF16, "mm_up_wgrad", after=tok)
    rs_up, tok = reduce_scatter([g_up], "rs_up")
    du2 = _mm_nt(da, w_up, F32, "mm_up_dgrad", after=tok)
    tok = [rs_up.pair_sums([du2])] if ex else []
    dh1, dh1b, g_nmlp = _rms_bwd(du2, h1, norm_mlp_w, dh2, "rms_mlp_bwd", after=tok)

    g_out = _mm_tn(merged, dh1b, 1, BF16, "mm_out_wgrad").reshape(N_DEV, D_MODEL // N_DEV, D_MODEL)
    dmerged = _mm_nt(dh1b, w_out1, F32, "mm_out_dgrad")
    dpa, dpb, dga, dgb = _merge_bwd(dmerged, z, pa, pb, "merge_bwd")
    g_a = _mm_tn(y_a, dpa, N_DEV, BF16, "mm_a_wgrad")
    g_b = _mm_tn(y_b, dpb, N_DEV, BF16, "mm_b_wgrad")
    rs_mix, tok = reduce_scatter([g_a, g_b, g_out], "rs_mix")
    dya = _mm_nt(dpa, w_a, F32, "mm_a_dgrad", after=tok)
    dyb = _mm_nt(dpb, w_b, F32, "mm_b_dgrad", after=tok)
    tok = [rs_mix.pair_sums([dya, dyb])] if ex else []
    daq, dak, dav, dbias_t = _attn_bwd(z, bias_t, dyb, "attn_bwd", after=tok)
    dhq, dhf, dhi, dhg, g_lbl, g_hgw = _hgrn2_bwd(z, lb_logits, hg_norm_w, o_raw, s_all, dya, "hgrn2_bwd",
                                                  after=tok)
    dbias_rows = jnp.transpose(dbias_t.reshape(AT_HEADS // 2, BAND, 2, CHUNK), (3, 0, 2, 1)).reshape(
        CHUNK, AT_HEADS, BAND)
    dz =jnp.concatenate([dhq, dhf, dhi, dhg, daq, dak, dav, dga, dgb], axis=1)
    g_in = _mm_tn(u, dz, N_DEV, BF16, "mm_in_wgrad")
    rs_in, tok = reduce_scatter([g_in], "rs_in")
    du = _mm_nt(dz, w_in, F32, "mm_in_dgrad", after=tok)
    tok = [rs_in.pair_sums([du])] if ex else []
    grad_x, _, g_nmix = _rms_bwd(du, x, norm_mix_w, dh1, "rms_mix_bwd", after=tok)
    g_rel = _bias_reduce(dbias_rows, "bias_reduce", after=tok)[:, :N_REL]

    small = dict(lb_logits=g_lbl, hg_norm_w=g_hgw[0:1], rel_bias=g_rel, norm_mix_w=g_nmix, norm_mlp_w=g_nmlp,
                 norm_final_w=g_nf)
    grads = [rs_in, rs_mix, rs_up, rs_down] if ex else [g_in, g_a, g_b, g_out, g_up, g_down]
    return loss, grad_x, grads, small


def _gather_exchange(shards, mid_step=None):
    n = len(shards)

    def parts(ins, outs, sems):
        send_sems, recv_sems, local_sems = sems
        x, y, c = _position()
        chips = [(1 - x, y), (x, 1 - y), (1 - x, 1 - y)]

        def copy(w, k, block, to, src=None):
            dst = outs[w].at[4 * block[0] + 2 * block[1] + block[2]]
            return pltpu.make_async_remote_copy(
                src_ref=dst if src is None else src, dst_ref=dst,
                send_sem=send_sems.at[w, k], recv_sem=recv_sems.at[w, k], device_id=to, device_id_type=MESH)

        def local(w):
            return pltpu.make_async_copy(ins[w], outs[w].at[4 * x + 2 * y + c], local_sems.at[w])

        return (x, y, c), (x, y, 1 - c), chips, copy, local

    def start(ins, outs, sems):
        me, sibling, chips, copy, local = parts(ins, outs, sems)
        for w in range(n):
            local(w).start()
        for w in range(n):
            copy(w, 0, me, sibling, src=ins[w]).start()
            for j, chip in enumerate(chips):
                copy(w, 1 + j, me, (*chip, me[2]), src=ins[w]).start()

    def mid(ins, outs, sems):
        me, sibling, chips, copy, _ = parts(ins, outs, sems)
        for w in range(n):
            for j, chip in enumerate(chips):
                copy(w, 1 + j, (*chip, me[2]), me).wait_recv()
                copy(w, 4 + j, (*chip, me[2]), sibling).start()

    def end(ins, outs, sems):
        me, sibling, chips, copy, local = parts(ins, outs, sems)
        for w in range(n):
            copy(w, 0, sibling, me).wait_recv()
            for j, chip in enumerate(chips):
                copy(w, 4 + j, (*chip, sibling[2]), me).wait_recv()
        for w in range(n):
            for k in range(7):
                copy(w, k, me, sibling).wait_send()
            local(w).wait()

    return _Exchange(
        shards, [jax.ShapeDtypeStruct((N_DEV,) + s.shape, s.dtype) for s in shards],
        [pltpu.SemaphoreType.DMA((n, 7)), pltpu.SemaphoreType.DMA((n, 7)), pltpu.SemaphoreType.DMA((n,))],
        start, end, mid, mid_step)


def _mm_gathered(u, shard, order, name):
    T, K = u.shape
    _, Nb = shard.shape

    def body(order_ref, u_ref, shard_ref, z_ref, full_ref, wbuf, load_sem, send_sems, recv_sems, local_sem):
        s = pl.program_id(0)
        x, y, c = _position()
        me, sibling = (x, y, c), (x, y, 1 - c)
        chips = [(1 - x, y), (x, 1 - y), (1 - x, 1 - y)]

        def copy(k, block, to, src=None):
            dst = full_ref.at[4 * block[0] + 2 * block[1] + block[2]]
            return pltpu.make_async_remote_copy(
                src_ref=dst if src is None else src, dst_ref=dst,
                send_sem=send_sems.at[k], recv_sem=recv_sems.at[k], device_id=to, device_id_type=MESH)

        @pl.when(s == 0)
        def _():
            local = pltpu.make_async_copy(shard_ref, full_ref.at[4 * x + 2 * y + c], local_sem)
            local.start()
            copy(0, me, sibling, src=shard_ref).start()
            for j, chip in enumerate(chips):
                copy(1 + j, me, (*chip, c), src=shard_ref).start()
            local.wait()

        @pl.when(s == 1)
        def _():
            copy(0, sibling, me).wait_recv()

        for j, chip in enumerate(chips):
            direct, passed = ((2, 4), (3, 5), (6, 7))[j]

            @pl.when(s == direct)
            def _(j=j, chip=chip):
                copy(1 + j, (*chip, c), me).wait_recv()
                copy(4 + j, (*chip, c), sibling).start()

            @pl.when(s == passed)
            def _(j=j, chip=chip):
                copy(4 + j, (*chip, 1 - c), me).wait_recv()

        load = pltpu.make_async_copy(full_ref.at[order_ref[s]], wbuf, load_sem)
        load.start()
        load.wait()
        z_ref[...] = jnp.dot(u_ref[...], wbuf[...], preferred_element_type=F32)

        @pl.when(s == N_DEV - 1)
        def _():
            for k in range(7):
                copy(k, me, sibling).wait_send()

    return pl.pallas_call(
        body, name=name,
        grid_spec=pltpu.PrefetchScalarGridSpec(
            num_scalar_prefetch=1, grid=(N_DEV,),
            in_specs=[pl.BlockSpec((T, K), lambda s, order: (0, 0)), ANY],
            out_specs=[pl.BlockSpec((T, Nb), lambda s, order: (0, order[s])), ANY],
            scratch_shapes=[pltpu.VMEM((K, Nb), BF16), pltpu.SemaphoreType.DMA,
                            pltpu.SemaphoreType.DMA((7,)), pltpu.SemaphoreType.DMA((7,)), pltpu.SemaphoreType.DMA]),
        out_shape=[jax.ShapeDtypeStruct((T, N_DEV * Nb), F32), jax.ShapeDtypeStruct((N_DEV, K, Nb), BF16)],
        compiler_params=_cparams(("arbitrary",)),
    )(order, u, shard)


def _gather_order():
    x, y, c = _position()
    chips = [(1 - x, y), (x, 1 - y), (1 - x, 1 - y)]
    ids = [4 * x + 2 * y + c, 4 * x + 2 * y + (1 - c)]
    ids += [4 * cx + 2 * cy + c for cx, cy in chips[:2]] + [4 * cx + 2 * cy + (1 - c) for cx, cy in chips[:2]]
    ids += [4 * chips[2][0] + 2 * chips[2][1] + c, 4 * chips[2][0] + 2 * chips[2][1] + (1 - c)]
    return jnp.stack(ids).astype(jnp.int32)


def _run_exchange(comm, name):
    n_i, n_o = len(comm.arrays), len(comm.out_shape)

    def body(*refs):
        ins, outs, sems = refs[:n_i], refs[n_i:n_i + n_o], refs[n_i + n_o:]
        comm.start(ins, outs, sems)
        if comm.mid is not None:
            comm.mid(ins, outs, sems)
        comm.end(ins, outs, sems)

    return pl.pallas_call(
        body, name=name, in_specs=[ANY] * n_i, out_specs=[ANY] * n_o, out_shape=comm.out_shape,
        scratch_shapes=comm.scratch)(*comm.arrays)


def _exchange_sibling(grads, name):
    n = len(grads)

    def body(*refs):
        ins, outs = refs[:n], refs[n:2 * n]
        send_sems, recv_sems = refs[2 * n:]
        x, y, c = _position()
        copies = []
        for w in range(n):
            for s in range(N_CHIP):
                cp = pltpu.make_async_remote_copy(
                    src_ref=ins[w].at[2 * s + (1 - c)], dst_ref=outs[w].at[s],
                    send_sem=send_sems.at[w, s], recv_sem=recv_sems.at[w, s],
                    device_id=(x, y, 1 - c), device_id_type=MESH)
                cp.start()
                copies.append(cp)
        for cp in copies:
            cp.wait()

    return pl.pallas_call(
        body, name=name,
        in_specs=[ANY] * n, out_specs=[ANY] * n,
        out_shape=[jax.ShapeDtypeStruct((N_CHIP,) + g.shape[1:], g.dtype) for g in grads],
        scratch_shapes=[pltpu.SemaphoreType.DMA((n, N_CHIP)), pltpu.SemaphoreType.DMA((n, N_CHIP))],
    )(*grads)


def _pair_sum(g, land, parity, name):
    _, R, C = g.shape
    tr = _pick(R, (512, 256))

    def body(par_ref, g_ref, l_ref, o_ref):
        o_ref[...] = (g_ref[...].astype(F32) + l_ref[...].astype(F32)).astype(BF16)

    return pl.pallas_call(
        body, name=name,
        grid_spec=pltpu.PrefetchScalarGridSpec(
            num_scalar_prefetch=1, grid=(N_CHIP, R // tr),
            in_specs=[pl.BlockSpec((None, tr, C), lambda s, i, par: (2 * s + par[0], i, 0)),
                      pl.BlockSpec((None, tr, C), lambda s, i, par: (s, i, 0))],
            out_specs=pl.BlockSpec((None, tr, C), lambda s, i, par: (s, i, 0))),
        out_shape=jax.ShapeDtypeStruct((N_CHIP, R, C), BF16),
        compiler_params=_cparams(("parallel", "parallel")),
    )(parity, g, land)


def _scatter_exchange(partials):
    n = len(partials)

    def copies(ins, outs, sems):
        send_sems, recv_sems, local_sems = sems
        x, y, c = _position()
        chips = [(1 - x, y), (x, 1 - y), (1 - x, 1 - y)]
        my_slot = 2 * x + y
        local = [pltpu.make_async_copy(ins[w].at[my_slot], outs[w].at[my_slot], local_sems.at[w]) for w in range(n)]
        remote = [pltpu.make_async_remote_copy(
            src_ref=ins[w].at[2 * chip[0] + chip[1]], dst_ref=outs[w].at[my_slot],
            send_sem=send_sems.at[w, j], recv_sem=recv_sems.at[w, j], device_id=(*chip, c), device_id_type=MESH)
            for w in range(n) for j, chip in enumerate(chips)]
        return local, remote

    def start(ins, outs, sems):
        local, remote = copies(ins, outs, sems)
        for cp in local + remote:
            cp.start()

    def end(ins, outs, sems):
        local, remote = copies(ins, outs, sems)
        for cp in remote + local:
            cp.wait()

    return _Exchange(
        partials, [jax.ShapeDtypeStruct(p.shape, p.dtype) for p in partials],
        [pltpu.SemaphoreType.DMA((n, 3)), pltpu.SemaphoreType.DMA((n, 3)), pltpu.SemaphoreType.DMA((n,))],
        start, end)


HBM = pl.BlockSpec(memory_space=pltpu.HBM)
SEM = pl.BlockSpec(memory_space=pltpu.SEMAPHORE)
DATAFLOW = pltpu.SideEffectType.DATAFLOW_SIDE_EFFECTING


def _scatter_copies(ins, lands, send_sems, recv_sems):
    x, y, c = _position()
    chips = [(1 - x, y), (x, 1 - y), (1 - x, 1 - y)]
    return [pltpu.make_async_remote_copy(
        src_ref=ins[w].at[2 * chip[0] + chip[1]], dst_ref=lands[w].at[2 * x + y],
        send_sem=send_sems[3 * w + j], recv_sem=recv_sems[3 * w + j], device_id=(*chip, c), device_id_type=MESH)
        for w in range(len(ins)) for j, chip in enumerate(chips)]


def _scatter_start(partials, name):
    n = len(partials)

    def body(*refs):
        ins, lands = refs[:n], refs[n:2 * n]
        sems = refs[4 * n:10 * n]
        for cp in _scatter_copies(ins, lands, sems[:3 * n], sems[3 * n:]):
            cp.start()
        refs[-1][...] = jnp.zeros_like(refs[-1])

    def in_hbm(a):
        return pltpu.with_memory_space_constraint(a, pltpu.HBM)

    bufs = tuple(pltpu.HBM(p.shape, p.dtype) for p in partials)
    outs = pl.pallas_call(
        body, name=name,
        out_shape=bufs + bufs + (pltpu.SemaphoreType.DMA(()),) * (6 * n) + (jax.ShapeDtypeStruct((8, 128), F32),),
        in_specs=[HBM] * (2 * n),
        out_specs=(HBM,) * (2 * n) + (SEM,) * (6 * n) + (pl.BlockSpec(memory_space=pltpu.VMEM),),
        input_output_aliases={i: i for i in range(2 * n)},
        compiler_params=pltpu.CompilerParams(has_side_effects=DATAFLOW),
    )(*[in_hbm(p) for p in partials], *[in_hbm(lax.empty(p.shape, p.dtype)) for p in partials])
    return list(outs[:-1]), outs[-1]


def _scatter_wait(handle, after, name):
    n = len(handle) // 8
    bufs, sems = handle[:2 * n], handle[2 * n:]

    def body(*refs):
        ins, lands = refs[:n], refs[n:2 * n]
        sems = refs[2 * n:8 * n]
        for cp in _scatter_copies(ins, lands, sems[:3 * n], sems[3 * n:]):
            cp.wait_send()
            cp.wait_recv()

    outs = pl.pallas_call(
        body, name=name,
        out_shape=tuple(pltpu.HBM(b.shape, b.dtype) for b in bufs),
        in_specs=[HBM] * (2 * n) + [SEM] * (6 * n) + [ANY] * len(after), out_specs=(HBM,) * (2 * n),
        input_output_aliases={i: i for i in range(2 * n)},
        compiler_params=pltpu.CompilerParams(has_side_effects=DATAFLOW),
    )(*bufs, *sems, *after)
    return list(outs[:n]), list(outs[n:])


def _split_call(name, bufs, waits=(), starts=None, after=()):
    nb = len(bufs)
    n_new = starts[1] if starts else 0
    wait_sems = [s for w in waits for s in (*w[1], *w[2])]

    def body(*refs):
        b, pos = refs[:nb], nb
        for plan, ss, _, send_idx, recv_idx in waits:
            k = len(ss)
            copies = plan(b, refs[pos:pos + k], refs[pos + k:pos + 2 * k])
            pos += 2 * k
            for i in recv_idx:
                copies[i].wait_recv()
            for i in send_idx:
                copies[i].wait_send()
        outs = refs[pos + len(after):]
        if starts:
            for cp in starts[0](b, outs[nb:nb + n_new], outs[nb + n_new:nb + 2 * n_new]):
                cp.start()
        outs[-1][...] = jnp.zeros_like(outs[-1])

    res = pl.pallas_call(
        body, name=name,
        out_shape=tuple(pltpu.HBM(a.shape, a.dtype) for a in bufs) + (pltpu.SemaphoreType.DMA(()),) * (2 * n_new)
        + (jax.ShapeDtypeStruct((8, 128), F32),),
        in_specs=[HBM] * nb + [SEM] * len(wait_sems) + [ANY] * len(after),
        out_specs=(HBM,) * nb + (SEM,) * (2 * n_new) + (pl.BlockSpec(memory_space=pltpu.VMEM),),
        input_output_aliases={i: i for i in range(nb)},
        compiler_params=pltpu.CompilerParams(has_side_effects=DATAFLOW),
    )(*bufs, *wait_sems, *after)
    return list(res[:nb]), list(res[nb:nb + n_new]), list(res[nb + n_new:nb + 2 * n_new]), res[-1]


def _in_hbm(a):
    return pltpu.with_memory_space_constraint(a, pltpu.HBM)


def _remote(src, dst, send_sem, recv_sem, to):
    return pltpu.make_async_remote_copy(src_ref=src, dst_ref=dst, send_sem=send_sem, recv_sem=recv_sem,
                                        device_id=to, device_id_type=MESH)


def _other_chips():
    x, y, _ = _position()
    return [(1 - x, y), (x, 1 - y), (1 - x, 1 - y)]


def _plan_gather_first(n):
    def plan(b, ss, rs):
        x, y, c = _position()
        to = [(x, y, 1 - c)] + [(*chip, c) for chip in _other_chips()]
        return [_remote(b[w], b[n + w].at[4 * x + 2 * y + c], ss[4 * w + k], rs[4 * w + k], to[k])
                for w in range(n) for k in range(4)]
    return plan, 4 * n


def _plan_gather_pass(n):
    def plan(b, ss, rs):
        x, y, c = _position()
        copies = []
        for w in range(n):
            for j, chip in enumerate(_other_chips()):
                blk = b[n + w].at[4 * chip[0] + 2 * chip[1] + c]
                copies.append(_remote(blk, blk, ss[3 * w + j], rs[3 * w + j], (x, y, 1 - c)))
        return copies
    return plan, 3 * n


def _plan_sibling(n):
    def plan(b, ss, rs):
        x, y, c = _position()
        return [_remote(b[w].at[2 * s + (1 - c)], b[n + w].at[s], ss[4 * w + s], rs[4 * w + s], (x, y, 1 - c))
                for w in range(n) for s in range(N_CHIP)]
    return plan, 4 * n


def _plan_scatter(n):
    def plan(b, ss, rs):
        x, y, c = _position()
        return [_remote(b[w].at[2 * chip[0] + chip[1]], b[n + w].at[2 * x + y], ss[3 * w + j], rs[3 * w + j],
                        (*chip, c))
                for w in range(n) for j, chip in enumerate(_other_chips())]
    return plan, 3 * n


class _Gather:
    def __init__(self, shards, after, name):
        self.n, self.name = len(shards), name
        x, y, c = _position()
        placed = [lax.dynamic_update_index_in_dim(lax.empty((N_DEV,) + s.shape, s.dtype), s, 4 * x + 2 * y + c, 0)
                  for s in shards]
        bufs, self.ss, self.rs, self.token = _split_call(
            name + "_start", [_in_hbm(a) for a in list(shards) + placed], starts=_plan_gather_first(self.n),
            after=after)
        self.shards, self.fulls = bufs[:self.n], bufs[self.n:]
        self.passed = {}

    def _sub(self, ids, sems, per):
        return [sems[per * w + k] for w in ids for k in range(per)]

    def pass_on(self, ids, after, tag):
        m = len(ids)
        first = (_plan_gather_first(m)[0], self._sub(ids, self.ss, 4), self._sub(ids, self.rs, 4),
                 [], [4 * i + k for i in range(m) for k in (1, 2, 3)])
        bufs, ss, rs, token = _split_call(
            "%s_pass_%s" % (self.name, tag), [self.shards[w] for w in ids] + [self.fulls[w] for w in ids],
            waits=[first], starts=_plan_gather_pass(m), after=after)
        for i, w in enumerate(ids):
            self.shards[w], self.fulls[w] = bufs[i], bufs[m + i]
        self.passed[tuple(ids)] = (ss, rs)
        return token

    def finish(self, ids, after, tag):
        m = len(ids)
        ss2, rs2 = self.passed[tuple(ids)]
        first = (_plan_gather_first(m)[0], self._sub(ids, self.ss, 4), self._sub(ids, self.rs, 4),
                 list(range(4 * m)), [4 * i for i in range(m)])
        passed = (_plan_gather_pass(m)[0], ss2, rs2, list(range(3 * m)), list(range(3 * m)))
        bufs, _, _, _ = _split_call(
            "%s_finish_%s" % (self.name, tag), [self.shards[w] for w in ids] + [self.fulls[w] for w in ids],
            waits=[first, passed], after=after)
        return bufs[m:]


class _ReduceScatter:
    def __init__(self, grads, parity, name):
        self.n, self.name, self.parity = len(grads), name, parity
        lands = [lax.empty((N_CHIP,) + g.shape[1:], g.dtype) for g in grads]
        self.bufs, self.ss, self.rs, self.token = _split_call(
            name + "_sibling_start", [_in_hbm(a) for a in list(grads) + lands], starts=_plan_sibling(self.n))

    def pair_sums(self, after):
        n = self.n
        bufs, _, _, _ = _split_call(
            self.name + "_sibling_wait", self.bufs,
            waits=[(_plan_sibling(n)[0], self.ss, self.rs, list(range(4 * n)), list(range(4 * n)))], after=after)
        sums = [_pair_sum(bufs[w], bufs[n + w], self.parity, "%s_pair_sum_%d" % (self.name, w)) for w in range(n)]
        lands = [lax.empty(s.shape, s.dtype) for s in sums]
        self.bufs, self.ss, self.rs, token = _split_call(
            self.name + "_scatter_start", [_in_hbm(a) for a in sums + lands], starts=_plan_scatter(n))
        return token

    def finish(self, after):
        n = self.n
        bufs, _, _, _ = _split_call(
            self.name + "_scatter_wait", self.bufs,
            waits=[(_plan_scatter(n)[0], self.ss, self.rs, list(range(3 * n)), list(range(3 * n)))], after=after)
        return bufs[:n], bufs[n:]


class _Exchanges:
    def __init__(self, parity, order):
        self.parity, self.order = parity, order


def _gather_small(packed, name):
    R = packed.shape[0]

    def body(x_ref, out_ref, send_sems, recv_sems):
        x, y, c = _position()
        me = 4 * x + 2 * y + c
        out_ref[me] = x_ref[...]
        copies = []
        for k in range(1, N_DEV):
            to = (x ^ ((k >> 2) & 1), y ^ ((k >> 1) & 1), c ^ (k & 1))
            cp = pltpu.make_async_remote_copy(
                src_ref=x_ref, dst_ref=out_ref.at[me],
                send_sem=send_sems.at[k], recv_sem=recv_sems.at[k], device_id=to, device_id_type=MESH)
            cp.start()
            copies.append((k, to, cp))
        for k, to, cp in copies:
            cp.wait_send()
            pltpu.make_async_remote_copy(
                src_ref=x_ref, dst_ref=out_ref.at[4 * to[0] + 2 * to[1] + to[2]],
                send_sem=send_sems.at[k], recv_sem=recv_sems.at[k], device_id=to, device_id_type=MESH).wait_recv()

    return pl.pallas_call(
        body, name=name,
        in_specs=[pl.BlockSpec(memory_space=pltpu.VMEM)], out_specs=pl.BlockSpec(memory_space=pltpu.VMEM),
        out_shape=jax.ShapeDtypeStruct((N_DEV, R, 128), F32),
        scratch_shapes=[pltpu.SemaphoreType.DMA((N_DEV,)), pltpu.SemaphoreType.DMA((N_DEV,))],
    )(packed)


def _adamw_math(w, g, m, v):
    m = ADAM_B1 * m + (1.0 - ADAM_B1) * g
    v = ADAM_B2 * v + (1.0 - ADAM_B2) * (g * g)
    m_hat = m / (1.0 - ADAM_B1 ** ADAM_STEP)
    v_hat = v / (1.0 - ADAM_B2 ** ADAM_STEP)
    delta = -ADAM_LR * (m_hat / (jnp.sqrt(v_hat) + ADAM_EPS) + ADAM_WD * w)
    return delta, m, v


def _adamw_big(w, m, v, parts, name):
    R, C = w.shape
    tr = _pick(R, (256,))

    def body(w_ref, m_ref, v_ref, p_ref, g_ref, d_ref, nm_ref, nv_ref):
        g = p_ref[0].astype(F32)
        for s in range(1, N_CHIP):
            g = g + p_ref[s].astype(F32)
        d, nm, nv = _adamw_math(w_ref[...], g, m_ref[...], v_ref[...])
        g_ref[...] = g
        d_ref[...] = d
        nm_ref[...] = nm
        nv_ref[...] = nv

    blk = pl.BlockSpec((tr, C), lambda i: (i, 0))
    out = jax.ShapeDtypeStruct((R, C), F32)
    return pl.pallas_call(
        body, name=name, grid=(R // tr,),
        in_specs=[blk, blk, blk, pl.BlockSpec((N_CHIP, tr, C), lambda i: (0, i, 0))],
        out_specs=[blk, blk, blk, blk], out_shape=[out, out, out, out],
        compiler_params=_cparams(("parallel",)),
    )(w, m, v, parts)


def _adamw_big_landed(w, m, v, parts, lands, slot, name):
    R, C = w.shape
    tr = _pick(R, (256,))

    def body(slot_ref, w_ref, m_ref, v_ref, own_ref, l1_ref, l2_ref, l3_ref, g_ref, d_ref, nm_ref, nv_ref):
        g = own_ref[...].astype(F32)
        for ref in (l1_ref, l2_ref, l3_ref):
            g = g + ref[...].astype(F32)
        d, nm, nv = _adamw_math(w_ref[...], g, m_ref[...], v_ref[...])
        g_ref[...] = g
        d_ref[...] = d
        nm_ref[...] = nm
        nv_ref[...] = nv

    blk = pl.BlockSpec((tr, C), lambda i, slot: (i, 0))

    def chip(k):
        return pl.BlockSpec((None, tr, C), lambda i, slot: ((slot[0] + k) % N_CHIP, i, 0))

    out = jax.ShapeDtypeStruct((R, C), F32)
    return pl.pallas_call(
        body, name=name,
        grid_spec=pltpu.PrefetchScalarGridSpec(
            num_scalar_prefetch=1, grid=(R // tr,),
            in_specs=[blk, blk, blk, chip(0), chip(1), chip(2), chip(3)],
            out_specs=[blk, blk, blk, blk]),
        out_shape=[out, out, out, out],
        compiler_params=_cparams(("parallel",)),
    )(slot, w, m, v, parts, lands, lands, lands)


def _adamw_small(w, m, v, gathered, name):
    R = w.shape[0]

    def body(w_ref, m_ref, v_ref, p_ref, g_ref, d_ref, nm_ref, nv_ref):
        g = p_ref[0]
        for s in range(1, N_DEV):
            g = g + p_ref[s]
        d, nm, nv = _adamw_math(w_ref[...], g, m_ref[...], v_ref[...])
        g_ref[...] = g
        d_ref[...] = d
        nm_ref[...] = nm
        nv_ref[...] = nv

    out = jax.ShapeDtypeStruct((R, 128), F32)
    return pl.pallas_call(
        body, name=name, out_shape=[out, out, out, out],
    )(w, m, v, gathered)


SMALL_NAMES = ("lb_logits", "hg_norm_w", "rel_bias", "norm_mix_w", "norm_mlp_w", "norm_final_w")
SMALL_SHAPES = {"lb_logits": (2, HG_WIDTH), "hg_norm_w": (1, HG_DK), "rel_bias": (AT_HEADS, N_REL_PAD),
                "norm_mix_w": (1, D_MODEL), "norm_mlp_w": (1, D_MODEL), "norm_final_w": (1, D_MODEL)}


def _pack_small(parts):
    rows = []
    for nme in SMALL_NAMES:
        p = parts[nme]
        if nme == "rel_bias":
            p = jnp.pad(p, ((0, 0), (0, N_REL_PAD - N_REL)))
        rows.append(p.reshape(-1, 128))
    flat = jnp.concatenate(rows, axis=0)
    return jnp.pad(flat, ((0, SMALL_ROWS - flat.shape[0]), (0, 0)))


def _unpack_small(packed):
    out, at = {}, 0
    for nme in SMALL_NAMES:
        shp = SMALL_SHAPES[nme]
        nrow = shp[0] * shp[1] // 128
        p = packed[at:at + nrow].reshape(shp)
        at += nrow
        out[nme] = p[:, :N_REL] if nme == "rel_bias" else p
    return out


BIG_NAMES = ("w_in", "w_branch_a", "w_branch_b", "w_out", "w_up", "w_down")


def kernel(x, w_in, lb_logits, hg_norm_w, rel_bias, w_branch_a, w_branch_b, w_out, norm_mix_w, norm_mlp_w, w_up, w_down, norm_final_w, loss_target, m_w_in, m_lb_logits, m_hg_norm_w, m_rel_bias, m_w_branch_a, m_w_branch_b, m_w_out, m_norm_mix_w, m_norm_mlp_w, m_w_up, m_w_down, m_norm_final_w, v_w_in, v_lb_logits, v_hg_norm_w, v_rel_bias, v_w_branch_a, v_w_branch_b, v_w_out, v_norm_mix_w, v_norm_mlp_w, v_w_up, v_w_down, v_norm_final_w):
    big_w = [w_in[0], w_branch_a[0], w_branch_b[0], w_out[0], w_up[0], w_down[0]]
    big_m = [m_w_in[0], m_w_branch_a[0], m_w_branch_b[0], m_w_out[0], m_w_up[0], m_w_down[0]]
    big_v = [v_w_in[0], v_w_branch_a[0], v_w_branch_b[0], v_w_out[0], v_w_up[0], v_w_down[0]]

    shards = [w.astype(BF16) for w in big_w]
    parity = lax.axis_index("c").astype(jnp.int32).reshape(1)
    loss_part, grad_x, chip_parts, small = _local_step(
        x[0], loss_target[0], lb_logits, hg_norm_w, rel_bias[0], norm_mix_w, norm_mlp_w,
        norm_final_w.reshape(1, D_MODEL), shards[0], shards[1:], _Exchanges(parity, _gather_order()))
    loss = lax.psum(loss_part[0, 0], ("x", "y", "c"))
    rs_in, rs_mix, rs_up, rs_down = chip_parts
    slot = (2 * lax.axis_index("x") + lax.axis_index("y")).astype(jnp.int32).reshape(1)
    big = {}

    def finish(rs, names, after):
        sums, lands = rs.finish(after)
        for nme, own, land in zip(names, sums, lands):
            i = BIG_NAMES.index(nme)
            big[nme] = _adamw_big_landed(big_w[i], big_m[i], big_v[i], own, land, slot, "adamw_" + nme)
        return [big[nme][1] for nme in names]

    done = finish(rs_down, ["w_down"], [grad_x])
    done = finish(rs_up, ["w_up"], done)
    done = finish(rs_mix, ["w_branch_a", "w_branch_b", "w_out"], done)

    sw = dict(lb_logits=lb_logits, hg_norm_w=hg_norm_w, rel_bias=rel_bias[0], norm_mix_w=norm_mix_w,
              norm_mlp_w=norm_mlp_w, norm_final_w=norm_final_w.reshape(1, D_MODEL))
    sm = dict(lb_logits=m_lb_logits, hg_norm_w=m_hg_norm_w, rel_bias=m_rel_bias[0], norm_mix_w=m_norm_mix_w,
              norm_mlp_w=m_norm_mlp_w, norm_final_w=m_norm_final_w.reshape(1, D_MODEL))
    sv = dict(lb_logits=v_lb_logits, hg_norm_w=v_hg_norm_w, rel_bias=v_rel_bias[0], norm_mix_w=v_norm_mix_w,
              norm_mlp_w=v_norm_mlp_w, norm_final_w=v_norm_final_w.reshape(1, D_MODEL))
    gathered = _gather_small(_pack_small(small), "gather_small")
    small_packed = _adamw_small(_pack_small(sw), _pack_small(sm), _pack_small(sv), gathered, "adamw_small")
    small_out = [_unpack_small(p) for p in small_packed]

    finish(rs_in, ["w_in"], done + [small_packed[0]])

    def leaf(kind, nme):
        if nme in BIG_NAMES:
            return big[nme][kind][None]
        p = small_out[kind][nme]
        if nme == "rel_bias":
            return p[None]
        if nme == "norm_final_w":
            return p.reshape(D_MODEL)
        return p

    order = ("w_in", "lb_logits", "hg_norm_w", "rel_bias", "w_branch_a", "w_branch_b", "w_out", "norm_mix_w",
             "norm_mlp_w", "w_up", "w_down", "norm_final_w")
    outs = [loss, grad_x[None]]
    for kind in range(4):
        outs += [leaf(kind, nme) for nme in order]
    return tuple(outs)
```

```python
import functools

import jax
import jax.numpy as jnp
from jax import lax
from jax.experimental import pallas as pl
from jax.experimental.pallas import tpu as pltpu

F32 = jnp.float32
BF16 = jnp.bfloat16
HIGHEST = lax.Precision.HIGHEST
MESH = pl.DeviceIdType.MESH

D_MODEL = 2048
HG_HEADS = 8
HG_DK = 128
HG_WIDTH = 1024
AT_HEADS = 16
AT_DH = 64
AT_WIDTH = 1024
CHUNK = 64
LEFT_CHUNKS = 8
BAND = (LEFT_CHUNKS + 1) * CHUNK
PAD = LEFT_CHUNKS * CHUNK
REL_CLIP = 256
N_REL = 2 * REL_CLIP + 1
N_REL_PAD = 640
D_FF = 4 * D_MODEL
D_IN = 4 * HG_WIDTH + 3 * AT_WIDTH + 2 * D_MODEL
EPS = 1e-6
N_DEV = 8
N_CHIP = 4

ADAM_LR = 0.001
ADAM_B1 = 0.9
ADAM_B2 = 0.999
ADAM_EPS = 1e-08
ADAM_WD = 0.01
ADAM_STEP = 10

COL_HQ, COL_HF, COL_HI, COL_HG = 0, 8, 16, 24
COL_AQ, COL_AK, COL_AV = 32, 40, 48
COL_GATE_A, COL_GATE_B = 7, 9

VMEM_LIMIT = 56 * 1024 * 1024
SMALL_ROWS = 152


def _cparams(sem=None, **kw):
    if sem is not None:
        kw["dimension_semantics"] = sem
    return pltpu.CompilerParams(vmem_limit_bytes=VMEM_LIMIT, **kw)


def _pick(n, cands):
    for c in cands:
        if n % c == 0:
            return c
    return n


def _sigmoid(x):
    return 1.0 / (1.0 + jnp.exp(-x))


ANY = pl.BlockSpec(memory_space=pl.ANY)


def _position():
    return lax.axis_index("x"), lax.axis_index("y"), lax.axis_index("c")


def _call(body, args, *, name, grid, in_specs, out_specs, out_shape, scratch_shapes=(), sem=None, after=()):
    n_in = len(args)

    def ordered(*refs):
        body(*refs[:n_in], *refs[n_in + len(after):])

    return list(pl.pallas_call(
        ordered if after else body, name=name, grid=grid, in_specs=list(in_specs) + [ANY] * len(after),
        out_specs=out_specs, out_shape=out_shape, scratch_shapes=list(scratch_shapes),
        compiler_params=_cparams(sem))(*args, *after))


MAX_CONTRACTION_TILE = 4096


def _accumulate(part, acc_ref, step, n_steps, finish):
    if n_steps == 1:
        finish(part)
        return

    @pl.when(step == 0)
    def _():
        acc_ref[...] = part

    @pl.when(step > 0)
    def _():
        acc_ref[...] += part

    @pl.when(step == n_steps - 1)
    def _():
        finish(acc_ref[...])


def _mm_nn(a, wb, out_dtype, name, after=(), squared_relu=False):
    M, K = a.shape
    NB, K2, Nb = wb.shape
    assert K == K2
    tm = min(M, 1024)
    tk = min(K, MAX_CONTRACTION_TILE)
    tn = _pick(Nb, (512, 1408, 256))
    nk = K // tk
    nn = Nb // tn

    def body(a_ref, b_ref, o_ref, *rest):
        def finish(total):
            o_ref[...] = total.astype(out_dtype)
            if squared_relu:
                ra = jnp.maximum(total, 0.0)
                rest[0][...] = (ra * ra).astype(BF16)

        part = jnp.dot(a_ref[...], b_ref[...], preferred_element_type=F32)
        _accumulate(part, rest[-1] if nk > 1 else None, pl.program_id(3), nk, finish)

    tile = pl.BlockSpec((tm, tn), lambda m, j, n, k: (m, j * nn + n))
    outs = _call(
        body, (a, wb), name=name, grid=(M // tm, NB, nn, nk),
        in_specs=[pl.BlockSpec((tm, tk), lambda m, j, n, k: (m, k)),
                  pl.BlockSpec((None, tk, tn), lambda m, j, n, k: (j, k, n))],
        out_specs=[tile, tile] if squared_relu else [tile],
        out_shape=[jax.ShapeDtypeStruct((M, NB * Nb), out_dtype)]
        + ([jax.ShapeDtypeStruct((M, NB * Nb), BF16)] if squared_relu else []),
        scratch_shapes=[] if nk == 1 else [pltpu.VMEM((tm, tn), F32)],
        sem=("parallel", "parallel", "parallel", "arbitrary"), after=after)
    return outs if squared_relu else outs[0]


def _mm_nt(a, wb, out_dtype, name, after=(), relu_of=None):
    M, N = a.shape
    NB, K, Nb = wb.shape
    assert N == NB * Nb
    tm = min(M, 1024)
    tko = _pick(K, (1024,))
    tc = _pick(Nb, (2048, 1024, 1408, 256))
    nc = Nb // tc
    jb = max([d for d in (8, 4, 2, 1) if NB % d == 0 and d * tc <= MAX_CONTRACTION_TILE]) if nc == 1 else 1
    nsteps = (NB // jb) * nc
    gated = relu_of is not None

    def body(a_ref, b_ref, *rest):
        o_ref = rest[1] if gated else rest[0]

        def finish(total):
            if gated:
                total = total * (2.0 * jnp.maximum(rest[0][...], 0.0))
            o_ref[...] = total.astype(out_dtype)

        part = sum(lax.dot_general(a_ref[:, i * tc:(i + 1) * tc], b_ref[i], (((1,), (1,)), ((), ())),
                                   preferred_element_type=F32) for i in range(jb))
        _accumulate(part, rest[-1], pl.program_id(2) * nc + pl.program_id(3), nsteps, finish)

    tile = pl.BlockSpec((tm, tko), lambda m, ko, j, c: (m, ko))
    out, = _call(
        body, (a, wb) + ((relu_of,) if gated else ()), name=name,
        grid=(M // tm, K // tko, NB // jb, nc),
        in_specs=[pl.BlockSpec((tm, jb * tc), lambda m, ko, j, c: (m, j * nc + c)),
                  pl.BlockSpec((jb, tko, tc), lambda m, ko, j, c: (j, ko, c))] + ([tile] if gated else []),
        out_specs=[tile],
        out_shape=[jax.ShapeDtypeStruct((M, K), out_dtype)],
        scratch_shapes=[] if nsteps == 1 else [pltpu.VMEM((tm, tko), F32)],
        sem=("parallel", "parallel", "arbitrary", "arbitrary"), after=after)
    return out


def _mm_tn(a, g, nb, out_dtype, name, after=()):
    M, Ka = a.shape
    M2, N = g.shape
    assert M == M2 and N % nb == 0
    Nb = N // nb
    tka = _pick(Ka, (1024,))
    tn = _pick(Nb, (512, 1408, 256))
    nn = Nb // tn

    def body(a_ref, g_ref, o_ref):
        o_ref[...] = lax.dot_general(a_ref[...], g_ref[...], (((0,), (0,)), ((), ())),
                                     preferred_element_type=F32).astype(out_dtype)

    return _call(
        body, (a, g), name=name,
        grid=(Ka // tka, nb, nn),
        in_specs=[pl.BlockSpec((M, tka), lambda ka, j, n: (0, ka)),
                  pl.BlockSpec((M, tn), lambda ka, j, n: (0, j * nn + n))],
        out_specs=[pl.BlockSpec((None, tka, tn), lambda ka, j, n: (j, ka, n))],
        out_shape=[jax.ShapeDtypeStruct((nb, Ka, Nb), out_dtype)],
        sem=("parallel", "parallel", "parallel"), after=after)[0]


ROW_TILE = 256


def _rms_fwd(x, w, name):
    T, Dm = x.shape

    def body(x_ref, w_ref, u_ref):
        xv = x_ref[...]
        r = lax.rsqrt(jnp.mean(xv * xv, axis=-1, keepdims=True) + EPS)
        u_ref[...] = (xv * r * w_ref[...]).astype(BF16)

    return pl.pallas_call(
        body, name=name, grid=(T // ROW_TILE,),
        in_specs=[pl.BlockSpec((ROW_TILE, Dm), lambda i: (i, 0)), pl.BlockSpec((1, Dm), lambda i: (0, 0))],
        out_specs=pl.BlockSpec((ROW_TILE, Dm), lambda i: (i, 0)),
        out_shape=jax.ShapeDtypeStruct((T, Dm), BF16),
        compiler_params=_cparams(("parallel",)),
    )(x, w)


def _resid_rms_fwd(x, mix, w, name):
    T, Dm = x.shape

    def body(x_ref, m_ref, w_ref, h_ref, u_ref):
        h = x_ref[...] + m_ref[...]
        h_ref[...] = h
        r = lax.rsqrt(jnp.mean(h * h, axis=-1, keepdims=True) + EPS)
        u_ref[...] = (h * r * w_ref[...]).astype(BF16)

    row = pl.BlockSpec((ROW_TILE, Dm), lambda i: (i, 0))
    return pl.pallas_call(
        body, name=name, grid=(T // ROW_TILE,),
        in_specs=[row, row, pl.BlockSpec((1, Dm), lambda i: (0, 0))],
        out_specs=[row, row],
        out_shape=[jax.ShapeDtypeStruct((T, Dm), F32), jax.ShapeDtypeStruct((T, Dm), BF16)],
        compiler_params=_cparams(("parallel",)),
    )(x, mix, w)


def _loss_head(h1, mlp, wf, target, name):
    T, Dm = h1.shape

    def body(h_ref, m_ref, w_ref, t_ref, loss_ref, dh_ref, dhb_ref, dw_ref):
        i = pl.program_id(0)
        h = h_ref[...] + m_ref[...]
        r = lax.rsqrt(jnp.mean(h * h, axis=-1, keepdims=True) + EPS)
        xh = h * r
        wv = w_ref[...]
        e = xh * wv - t_ref[...]
        part = 0.5 * jnp.sum(jnp.mean(e * e, axis=-1, keepdims=True), axis=0, keepdims=True)
        dy = e * (1.0 / Dm)
        dw = jnp.sum(dy * xh, axis=0, keepdims=True)
        gy = dy * wv
        dh = r * (gy - xh * jnp.mean(gy * xh, axis=-1, keepdims=True))
        dh_ref[...] = dh
        dhb_ref[...] = dh.astype(BF16)

        @pl.when(i == 0)
        def _():
            loss_ref[...] = jnp.zeros_like(loss_ref)
            dw_ref[...] = jnp.zeros_like(dw_ref)

        loss_ref[...] += jnp.broadcast_to(part, loss_ref.shape)
        dw_ref[...] += dw

    row = pl.BlockSpec((ROW_TILE, Dm), lambda i: (i, 0))
    vec = pl.BlockSpec((1, Dm), lambda i: (0, 0))
    return pl.pallas_call(
        body, name=name, grid=(T // ROW_TILE,),
        in_specs=[row, row, vec, row],
        out_specs=[pl.BlockSpec((8, 128), lambda i: (0, 0)), row, row, vec],
        out_shape=[jax.ShapeDtypeStruct((8, 128), F32), jax.ShapeDtypeStruct((T, Dm), F32),
                   jax.ShapeDtypeStruct((T, Dm), BF16), jax.ShapeDtypeStruct((1, Dm), F32)],
        compiler_params=_cparams(("arbitrary",)),
    )(h1, mlp, wf, target)


def _rms_bwd(dyn, x, w, dres, name, after=()):
    T, Dm = x.shape

    def body(g_ref, x_ref, w_ref, r_ref, dx_ref, dxb_ref, dw_ref):
        i = pl.program_id(0)
        xv = x_ref[...]
        r = lax.rsqrt(jnp.mean(xv * xv, axis=-1, keepdims=True) + EPS)
        xh = xv * r
        g = g_ref[...]
        dw = jnp.sum(g * xh, axis=0, keepdims=True)
        gy = g * w_ref[...]
        dx = r_ref[...] + r * (gy - xh * jnp.mean(gy * xh, axis=-1, keepdims=True))
        dx_ref[...] = dx
        dxb_ref[...] = dx.astype(BF16)

        @pl.when(i == 0)
        def _():
            dw_ref[...] = jnp.zeros_like(dw_ref)

        dw_ref[...] += dw

    row = pl.BlockSpec((ROW_TILE, Dm), lambda i: (i, 0))
    vec = pl.BlockSpec((1, Dm), lambda i: (0, 0))
    return _call(
        body, (dyn, x, w, dres), name=name, grid=(T // ROW_TILE,),
        in_specs=[row, row, vec, row],
        out_specs=[row, row, vec],
        out_shape=[jax.ShapeDtypeStruct((T, Dm), F32), jax.ShapeDtypeStruct((T, Dm), BF16),
                   jax.ShapeDtypeStruct((1, Dm), F32)],
        sem=("arbitrary",), after=after)


COL_TILE = 2048


def _relu2_fwd(a, name):
    T, N = a.shape

    def body(a_ref, r_ref):
        ra = jnp.maximum(a_ref[...], 0.0)
        r_ref[...] = (ra * ra).astype(BF16)

    blk = pl.BlockSpec((ROW_TILE, COL_TILE), lambda i, j: (i, j))
    return pl.pallas_call(
        body, name=name, grid=(T // ROW_TILE, N // COL_TILE), in_specs=[blk], out_specs=blk,
        out_shape=jax.ShapeDtypeStruct((T, N), BF16),
        compiler_params=_cparams(("parallel", "parallel")),
    )(a)


def _relu2_bwd(dr, a, name, after=()):
    T, N = a.shape

    def body(dr_ref, a_ref, da_ref):
        da_ref[...] = (dr_ref[...] * (2.0 * jnp.maximum(a_ref[...], 0.0))).astype(BF16)

    blk = pl.BlockSpec((ROW_TILE, COL_TILE), lambda i, j: (i, j))
    return _call(
        body, (dr, a), name=name, grid=(T // ROW_TILE, N // COL_TILE), in_specs=[blk, blk], out_specs=[blk],
        out_shape=[jax.ShapeDtypeStruct((T, N), BF16)], sem=("parallel", "parallel"), after=after)[0]


GATE_TILE = 1024


def _merge_fwd(z, pa, pb, name):
    T, Dm = pa.shape

    def body(za_ref, zb_ref, pa_ref, pb_ref, m_ref):
        m_ref[...] = (_sigmoid(za_ref[...]) * pa_ref[...] + _sigmoid(zb_ref[...]) * pb_ref[...]).astype(BF16)

    blk = pl.BlockSpec((ROW_TILE, GATE_TILE), lambda i, j: (i, j))
    return pl.pallas_call(
        body, name=name, grid=(T // ROW_TILE, Dm // GATE_TILE),
        in_specs=[pl.BlockSpec((ROW_TILE, GATE_TILE), lambda i, j: (i, COL_GATE_A + j)),
                  pl.BlockSpec((ROW_TILE, GATE_TILE), lambda i, j: (i, COL_GATE_B + j)), blk, blk],
        out_specs=blk,
        out_shape=jax.ShapeDtypeStruct((T, Dm), BF16),
        compiler_params=_cparams(("parallel", "parallel")),
    )(z, z, pa, pb)


def _merge_bwd(dm, z, pa, pb, name):
    T, Dm = pa.shape

    def body(dm_ref, za_ref, zb_ref, pa_ref, pb_ref, dpa_ref, dpb_ref, dga_ref, dgb_ref):
        d = dm_ref[...]
        ga = _sigmoid(za_ref[...])
        gb = _sigmoid(zb_ref[...])
        dpa_ref[...] = (d * ga).astype(BF16)
        dpb_ref[...] = (d * gb).astype(BF16)
        dga_ref[...] = (d * pa_ref[...] * ga * (1.0 - ga)).astype(BF16)
        dgb_ref[...] = (d * pb_ref[...] * gb * (1.0 - gb)).astype(BF16)

    blk = pl.BlockSpec((ROW_TILE, GATE_TILE), lambda i, j: (i, j))
    out = jax.ShapeDtypeStruct((T, Dm), BF16)
    return pl.pallas_call(
        body, name=name, grid=(T // ROW_TILE, Dm // GATE_TILE),
        in_specs=[blk, pl.BlockSpec((ROW_TILE, GATE_TILE), lambda i, j: (i, COL_GATE_A + j)),
                  pl.BlockSpec((ROW_TILE, GATE_TILE), lambda i, j: (i, COL_GATE_B + j)), blk, blk],
        out_specs=[blk, blk, blk, blk],
        out_shape=[out, out, out, out],
        compiler_params=_cparams(("parallel", "parallel")),
    )(dm, z, z, pa, pb)


def _dot_hi(a, b, dims):
    return lax.dot_general(a, b, (dims, ((), ())), precision=HIGHEST, preferred_element_type=F32)


NN = ((1,), (0,))
NT = ((1,), (1,))
TN = ((0,), (0,))


def _hg_gates(hq, hf, lb):
    sq = _sigmoid(hq)
    q = hq * sq * (HG_DK ** -0.5)
    f = _sigmoid(hf)
    g = lb + (1.0 - lb) * f
    return q, sq, f, g, jnp.log(g), 1.0 - g


def _tri(lower):
    r = lax.broadcasted_iota(jnp.int32, (CHUNK, CHUNK), 0)
    c = lax.broadcasted_iota(jnp.int32, (CHUNK, CHUNK), 1)
    return jnp.where((r >= c) if lower else (r <= c), 1.0, 0.0).astype(F32)


def _hgrn2_fwd(z, lb_logits, hg_norm_w, name, after=()):
    T = z.shape[0]
    n_chunks = T // CHUNK

    def body(hq_ref, hf_ref, hi_ref, hg_ref, lbl_ref, nw_ref, o_ref, ya_ref, sall_ref, st_ref):
        lbl = lbl_ref[...]
        lb = 1.0 / (1.0 + jnp.exp(lbl[1:2, :] - lbl[0:1, :]))
        st_ref[...] = jnp.zeros_like(st_ref)
        tri = _tri(True)
        row8 = lax.broadcasted_iota(jnp.int32, (8, HG_DK), 0)

        def chunk(c, carry):
            rows = pl.ds(pl.multiple_of(c * CHUNK, CHUNK), CHUNK)
            q, _, _, _, lg, kk = _hg_gates(hq_ref[rows, :], hf_ref[rows, :], lb)
            v = hi_ref[rows, :]
            b = _dot_hi(tri, lg, NN)
            st = st_ref[...]
            sall_ref[c] = st
            o_inter = _dot_hi(q * jnp.exp(b), st, NT)
            for g8 in range(CHUNK // 8):
                n = 8 * (g8 + 1)
                bs, ks, vs = b[:n], kk[:n], v[:n]
                sidx = lax.broadcasted_iota(jnp.int32, (n, HG_DK), 0)
                blk = o_inter[8 * g8:n]
                for i in range(8):
                    t = 8 * g8 + i
                    e = jnp.where(sidx <= t, jnp.exp(b[t:t + 1] - bs), 0.0)
                    p = jnp.sum(e * ks * q[t:t + 1], axis=1, keepdims=True)
                    ot = jnp.sum(p * vs, axis=0, keepdims=True)
                    blk = blk + jnp.where(row8 == i, ot, 0.0)
                o_ref[pl.ds(pl.multiple_of(c * CHUNK + 8 * g8, 8), 8), :] = blk
            bl = b[CHUNK - 1:CHUNK]
            ke = kk * jnp.exp(bl - b)
            st_ref[...] = st * jnp.exp(bl) + _dot_hi(v, ke, TN)
            return carry

        lax.fori_loop(0, n_chunks, chunk, 0)
        o = o_ref[...]
        r = lax.rsqrt(jnp.mean(o * o, axis=-1, keepdims=True) + EPS)
        hg = hg_ref[...]
        ya_ref[...] = (o * r * nw_ref[...] * (hg * _sigmoid(hg))).astype(BF16)

    def col(base):
        return pl.BlockSpec((T, HG_DK), lambda h: (0, base + h))

    return _call(
        body, (z, z, z, z, lb_logits, hg_norm_w), name=name, grid=(HG_HEADS,),
        in_specs=[col(COL_HQ), col(COL_HF), col(COL_HI), col(COL_HG),
                  pl.BlockSpec((2, HG_DK), lambda h: (0, h)), pl.BlockSpec((1, HG_DK), lambda h: (0, 0))],
        out_specs=[col(0), col(0), pl.BlockSpec((None, n_chunks, HG_DK, HG_DK), lambda h: (h, 0, 0, 0))],
        out_shape=[jax.ShapeDtypeStruct((T, HG_WIDTH), F32), jax.ShapeDtypeStruct((T, HG_WIDTH), BF16),
                   jax.ShapeDtypeStruct((HG_HEADS, n_chunks, HG_DK, HG_DK), F32)],
        scratch_shapes=[pltpu.VMEM((HG_DK, HG_DK), F32)],
        sem=("parallel",), after=after)


def _hgrn2_bwd(z, lb_logits, hg_norm_w, o_raw, s_all, dya, name, after=()):
    T = z.shape[0]
    n_chunks = T // CHUNK

    def body(hq_ref, hf_ref, hi_ref, hg_ref, lbl_ref, nw_ref, o_ref, sall_ref, dya_ref,
             dhq_ref, dhf_ref, dhi_ref, dhg_ref, dlbl_ref, dnw_ref,
             do_ref, dst_ref, dq_ref, dk_ref, dv_ref, dlb_ref):
        h = pl.program_id(0)
        lbl = lbl_ref[...]
        lb = 1.0 / (1.0 + jnp.exp(lbl[1:2, :] - lbl[0:1, :]))

        o = o_ref[...]
        r = lax.rsqrt(jnp.mean(o * o, axis=-1, keepdims=True) + EPS)
        oh = o * r
        nw = nw_ref[...]
        hg = hg_ref[...]
        sg = _sigmoid(hg)
        dy = dya_ref[...]
        d_on = dy * (hg * sg)
        dhg_ref[...] = (dy * (oh * nw) * (sg * (1.0 + hg * (1.0 - sg)))).astype(BF16)
        dnw = jnp.sum(d_on * oh, axis=0, keepdims=True)
        gy = d_on * nw
        do_ref[...] = r * (gy - oh * jnp.mean(gy * oh, axis=-1, keepdims=True))

        @pl.when(h == 0)
        def _():
            dnw_ref[...] = jnp.zeros_like(dnw_ref)

        dnw_ref[...] += jnp.broadcast_to(dnw, dnw_ref.shape)

        dst_ref[...] = jnp.zeros_like(dst_ref)
        dlb_ref[...] = jnp.zeros_like(dlb_ref)
        tri = _tri(True)
        tri_t = _tri(False)
        row8 = lax.broadcasted_iota(jnp.int32, (8, HG_DK), 0)

        def chunk(ci, carry):
            c = n_chunks - 1 - ci
            rows = pl.ds(pl.multiple_of(c * CHUNK, CHUNK), CHUNK)
            hq = hq_ref[rows, :]
            q, sq, f, g, lg, kk = _hg_gates(hq, hf_ref[rows, :], lb)
            v = hi_ref[rows, :]
            do = do_ref[rows, :]
            b = _dot_hi(tri, lg, NN)
            eb = jnp.exp(b)
            bl = b[CHUNK - 1:CHUNK]
            ebl = jnp.exp(bl)
            ekb = jnp.exp(bl - b)
            qe = q * eb
            ke = kk * ekb
            st = sall_ref[c]
            dst = dst_ref[...]
            dqe = _dot_hi(do, st, NN)
            dke = _dot_hi(v, dst, NN)
            dv_inter = _dot_hi(ke, dst, NT)
            d_ebl = jnp.sum(st * dst, axis=0, keepdims=True)
            dst_ref[...] = dst * ebl + _dot_hi(do, qe, TN)

            dk_ref[...] = jnp.zeros_like(dk_ref)
            dv_ref[...] = jnp.zeros_like(dv_ref)
            for g8 in range(CHUNK // 8):
                n = 8 * (g8 + 1)
                bs, ks, vs = b[:n], kk[:n], v[:n]
                sidx = lax.broadcasted_iota(jnp.int32, (n, HG_DK), 0)
                blk = jnp.zeros((8, HG_DK), F32)
                for i in range(8):
                    t = 8 * g8 + i
                    qt = q[t:t + 1]
                    dot_ = do[t:t + 1]
                    e = jnp.where(sidx <= t, jnp.exp(b[t:t + 1] - bs), 0.0)
                    w = e * ks
                    p = jnp.sum(w * qt, axis=1, keepdims=True)
                    dsc = jnp.sum(vs * dot_, axis=1, keepdims=True)
                    dqt = jnp.sum(dsc * w, axis=0, keepdims=True)
                    blk = blk + jnp.where(row8 == i, dqt, 0.0)
                    dk_ref[0:n, :] += dsc * e * qt
                    dv_ref[0:n, :] += p * dot_
                dq_ref[8 * g8:n, :] = blk
            dq_i = dq_ref[...]
            dk_i = dk_ref[...]
            dke_ke = dke * ke
            db = q * dq_i - kk * dk_i + dqe * qe - dke_ke
            db_last = jnp.sum(dke_ke, axis=0, keepdims=True) + d_ebl * ebl
            dlg = _dot_hi(tri_t, db, NN) + db_last
            dq = dq_i + dqe * eb
            dkk = dk_i + dke * ekb
            dg = dlg / g - dkk
            dhq_ref[rows, :] = (dq * (HG_DK ** -0.5) * (sq * (1.0 + hq * (1.0 - sq)))).astype(BF16)
            dhf_ref[rows, :] = (dg * (1.0 - lb) * f * (1.0 - f)).astype(BF16)
            dhi_ref[rows, :] = (dv_ref[...] + dv_inter).astype(BF16)
            dlb_ref[...] += jnp.sum(dg * (1.0 - f), axis=0, keepdims=True)
            return carry

        lax.fori_loop(0, n_chunks, chunk, 0)
        dl0 = dlb_ref[...] * lb * (1.0 - lb)
        dlbl_ref[0:1, :] = dl0
        dlbl_ref[1:2, :] = -dl0

    def col(base):
        return pl.BlockSpec((T, HG_DK), lambda h: (0, base + h))

    outb = jax.ShapeDtypeStruct((T, HG_WIDTH), BF16)
    return _call(
        body, (z, z, z, z, lb_logits, hg_norm_w, o_raw, s_all, dya), name=name, grid=(HG_HEADS,),
        in_specs=[col(COL_HQ), col(COL_HF), col(COL_HI), col(COL_HG),
                  pl.BlockSpec((2, HG_DK), lambda h: (0, h)), pl.BlockSpec((1, HG_DK), lambda h: (0, 0)),
                  col(0), pl.BlockSpec((None, n_chunks, HG_DK, HG_DK), lambda h: (h, 0, 0, 0)), col(0)],
        out_specs=[col(0), col(0), col(0), col(0), pl.BlockSpec((2, HG_DK), lambda h: (0, h)),
                   pl.BlockSpec((8, HG_DK), lambda h: (0, 0))],
        out_shape=[outb, outb, outb, outb, jax.ShapeDtypeStruct((2, HG_WIDTH), F32),
                   jax.ShapeDtypeStruct((8, HG_DK), F32)],
        scratch_shapes=[pltpu.VMEM((T, HG_DK), F32), pltpu.VMEM((HG_DK, HG_DK), F32),
                        pltpu.VMEM((CHUNK, HG_DK), F32), pltpu.VMEM((CHUNK, HG_DK), F32),
                        pltpu.VMEM((CHUNK, HG_DK), F32), pltpu.VMEM((1, HG_DK), F32)],
        sem=("arbitrary",), after=after)


CONST_KEYS = PAD - REL_CLIP
VAR_KEYS = BAND - CONST_KEYS
REL_LO = 128
REL_SPAN = N_REL_PAD - REL_LO


def _rel_onehot(t):
    r = lax.broadcasted_iota(jnp.int32, (REL_SPAN, VAR_KEYS), 0)
    j = lax.broadcasted_iota(jnp.int32, (REL_SPAN, VAR_KEYS), 1)
    idx = jnp.clip(t + PAD - CONST_KEYS - j, -REL_CLIP, REL_CLIP) + REL_CLIP - REL_LO
    return jnp.where(r == idx, 1.0, 0.0).astype(BF16)


def _split3(x):
    hi = x.astype(BF16)
    r1 = x - hi.astype(F32)
    mid = r1.astype(BF16)
    return hi, mid, (r1 - mid.astype(F32)).astype(BF16)


def _bias_expand(rel, name):
    def body(rel_ref, out_ref):
        tab = rel_ref[...]
        onehot = _rel_onehot(pl.program_id(0))
        out_ref[:, 0:CONST_KEYS] = jnp.broadcast_to(tab[:, 2 * REL_CLIP:2 * REL_CLIP + 1], (AT_HEADS, CONST_KEYS))
        out_ref[:, CONST_KEYS:BAND] = sum(
            jnp.dot(piece, onehot, preferred_element_type=F32) for piece in _split3(tab[:, REL_LO:N_REL_PAD]))

    return pl.pallas_call(
        body, name=name, grid=(CHUNK,),
        in_specs=[pl.BlockSpec((AT_HEADS, N_REL_PAD), lambda t: (0, 0))],
        out_specs=pl.BlockSpec((None, AT_HEADS, BAND), lambda t: (t, 0, 0)),
        out_shape=jax.ShapeDtypeStruct((CHUNK, AT_HEADS, BAND), F32),
        compiler_params=_cparams(("parallel",)),
    )(rel)


def _bias_reduce(dbias_t, name, after=()):
    def body(db_ref, out_ref):
        t = pl.program_id(0)

        @pl.when(t == 0)
        def _():
            out_ref[...] = jnp.zeros_like(out_ref)

        db = db_ref[...]
        onehot = _rel_onehot(t)
        acc = sum(lax.dot_general(piece, onehot, (NT, ((), ())), preferred_element_type=F32)
                  for piece in _split3(db[:, CONST_KEYS:BAND]))
        lane = lax.broadcasted_iota(jnp.int32, (AT_HEADS, REL_SPAN), 1)
        last = jnp.sum(db[:, 0:CONST_KEYS], axis=1, keepdims=True)
        out_ref[:, REL_LO:N_REL_PAD] += acc + jnp.where(lane == 2 * REL_CLIP - REL_LO, last, 0.0)

    return _call(
        body, (dbias_t,), name=name, grid=(CHUNK,),
        in_specs=[pl.BlockSpec((None, AT_HEADS, BAND), lambda t: (t, 0, 0))],
        out_specs=[pl.BlockSpec((AT_HEADS, N_REL_PAD), lambda t: (0, 0))],
        out_shape=[jax.ShapeDtypeStruct((AT_HEADS, N_REL_PAD), F32)],
        sem=("arbitrary",), after=after)[0]


def _pair_lanes():
    return lax.broadcasted_iota(jnp.int32, (CHUNK, 2 * AT_DH), 1) < AT_DH


def _block_diag(a):
    first = _pair_lanes()
    return jnp.concatenate([jnp.where(first, a, 0.0), jnp.where(first, 0.0, a)], axis=0).astype(BF16)


def _diag_blocks(a):
    return jnp.where(_pair_lanes(), a[:CHUNK], a[CHUNK:])


def _band_probs_t(kb, qbd, bias_t, c):
    s = lax.dot_general(kb, qbd, (NT, ((), ())), preferred_element_type=F32) * (AT_DH ** -0.5) + bias_t
    j = lax.broadcasted_iota(jnp.int32, (BAND, 2 * AT_DH), 0)
    s = jnp.where(j + c * CHUNK >= PAD, s, -jnp.inf)
    p = jnp.exp(s - jnp.max(s, axis=0, keepdims=True))
    return p / jnp.sum(p, axis=0, keepdims=True)


def _fill_padded(dst_ref, src_ref, T):
    dst_ref[0:PAD, :] = jnp.zeros((PAD, 2 * AT_DH), BF16)
    dst_ref[PAD:PAD + T, :] = src_ref[...].astype(BF16)


def _attn_fwd(z, bias_t, name, after=()):
    T = z.shape[0]
    n_chunks = T // CHUNK

    def body(q_ref, k_ref, v_ref, bias_ref, y_ref, kp_ref, vp_ref):
        _fill_padded(kp_ref, k_ref, T)
        _fill_padded(vp_ref, v_ref, T)

        def chunk(c, carry):
            rows = pl.ds(pl.multiple_of(c * CHUNK, CHUNK), CHUNK)
            band = pl.ds(pl.multiple_of(c * CHUNK, CHUNK), BAND)
            p = _band_probs_t(kp_ref[band, :], _block_diag(q_ref[rows, :]), bias_ref[...], c)
            o2 = lax.dot_general(p.astype(BF16), vp_ref[band, :], (TN, ((), ())), preferred_element_type=F32)
            y_ref[rows, :] = _diag_blocks(o2).astype(BF16)
            return carry

        lax.fori_loop(0, n_chunks, chunk, 0, unroll=2)

    def col(base):
        return pl.BlockSpec((T, 128), lambda h: (0, base + h))

    return _call(
        body, (z, z, z, bias_t), name=name, grid=(AT_HEADS // 2,),
        in_specs=[col(COL_AQ), col(COL_AK), col(COL_AV), pl.BlockSpec((None, BAND, 128), lambda h: (h, 0, 0))],
        out_specs=[col(0)],
        out_shape=[jax.ShapeDtypeStruct((T, AT_WIDTH), BF16)],
        scratch_shapes=[pltpu.VMEM((PAD + T, 128), BF16), pltpu.VMEM((PAD + T, 128), BF16)],
        sem=("parallel",), after=after)


def _attn_bwd(z, bias_t, dyb, name, after=()):
    T = z.shape[0]
    n_chunks = T // CHUNK

    def body(q_ref, k_ref, v_ref, bias_ref, dy_ref, dq_ref, dk_ref, dv_ref, dbias_ref,
             kp_ref, vp_ref, dkp_ref, dvp_ref):
        _fill_padded(kp_ref, k_ref, T)
        _fill_padded(vp_ref, v_ref, T)
        dkp_ref[...] = jnp.zeros_like(dkp_ref)
        dvp_ref[...] = jnp.zeros_like(dvp_ref)
        dbias_ref[...] = jnp.zeros_like(dbias_ref)

        def chunk(c, carry):
            rows = pl.ds(pl.multiple_of(c * CHUNK, CHUNK), CHUNK)
            band = pl.ds(pl.multiple_of(c * CHUNK, CHUNK), BAND)
            qbd = _block_diag(q_ref[rows, :])
            dobd = _block_diag(dy_ref[rows, :])
            kb = kp_ref[band, :]
            vb = vp_ref[band, :]
            p = _band_probs_t(kb, qbd, bias_ref[...], c)
            dp = lax.dot_general(vb, dobd, (NT, ((), ())), preferred_element_type=F32)
            ds = p * (dp - jnp.sum(dp * p, axis=0, keepdims=True))
            dbias_ref[...] += ds
            dsb = ds.astype(BF16)
            dq2 = lax.dot_general(dsb, kb, (TN, ((), ())), preferred_element_type=F32)
            dq_ref[rows, :] = (_diag_blocks(dq2) * (AT_DH ** -0.5)).astype(BF16)
            dkp_ref[band, :] += jnp.dot(dsb, qbd, preferred_element_type=F32) * (AT_DH ** -0.5)
            dvp_ref[band, :] += jnp.dot(p.astype(BF16), dobd, preferred_element_type=F32)
            return carry

        lax.fori_loop(0, n_chunks, chunk, 0)
        dk_ref[...] = dkp_ref[PAD:PAD + T, :].astype(BF16)
        dv_ref[...] = dvp_ref[PAD:PAD + T, :].astype(BF16)

    def col(base):
        return pl.BlockSpec((T, 128), lambda h: (0, base + h))

    pair = pl.BlockSpec((None, BAND, 128), lambda h: (h, 0, 0))
    outb = jax.ShapeDtypeStruct((T, AT_WIDTH), BF16)
    return _call(
        body, (z, z, z, bias_t, dyb), name=name, grid=(AT_HEADS // 2,),
        in_specs=[col(COL_AQ), col(COL_AK), col(COL_AV), pair, col(0)],
        out_specs=[col(0), col(0), col(0), pair],
        out_shape=[outb, outb, outb, jax.ShapeDtypeStruct((AT_HEADS // 2, BAND, 128), F32)],
        scratch_shapes=[pltpu.VMEM((PAD + T, 128), BF16), pltpu.VMEM((PAD + T, 128), BF16),
                        pltpu.VMEM((PAD + T, 128), F32), pltpu.VMEM((PAD + T, 128), F32)],
        sem=("parallel",), after=after)


def _local_step(x, target, lb_logits, hg_norm_w, rel_bias, norm_mix_w, norm_mlp_w, norm_final_w,
                w_in, rest, exchanges=None):
    ex = exchanges
    rel = jnp.pad(rel_bias, ((0, 0), (0, N_REL_PAD - N_REL)))

    u = _rms_fwd(x, norm_mix_w, "rms_mix_fwd")
    if ex:
        z, w_in = _mm_gathered(u, w_in, ex.order, "mm_in_fwd")
        gather = _Gather(rest, [z], "ag")
        tok = [gather.token]
    else:
        z = _mm_nn(u, w_in, F32, "mm_in_fwd")
        w_a, w_b, w_out, w_up, w_down = rest
        tok = []
    o_raw, y_a, s_all = _hgrn2_fwd(z, lb_logits, hg_norm_w, "hgrn2_fwd", after=tok)
    if ex:
        tok = [gather.pass_on([0, 1, 2], [o_raw], "abo")]
    bias_rows = _bias_expand(rel, "bias_expand")
    bias_t = jnp.transpose(bias_rows.reshape(CHUNK, AT_HEADS // 2, 2, BAND), (1, 3, 2, 0)).reshape(
        AT_HEADS // 2, BAND, 2 * CHUNK)
    y_b, = _attn_fwd(z, bias_t, "attn_fwd", after=tok)
    if ex:
        tok = [gather.pass_on([3], [y_b], "up")]
        w_a, w_b, w_out = gather.finish([0, 1, 2], tok, "abo")
    pa = _mm_nn(y_a, w_a, F32, "mm_a_fwd")
    pb = _mm_nn(y_b, w_b, F32, "mm_b_fwd")
    merged = _merge_fwd(z, pa, pb, "merge_fwd")
    w_out1 = w_out.reshape(1, D_MODEL, D_MODEL)
    mix = _mm_nn(merged, w_out1, F32, "mm_out_fwd")
    h1, u2 = _resid_rms_fwd(x, mix, norm_mlp_w, "rms_mlp_fwd")
    if ex:
        tok = [gather.pass_on([4], [u2], "down")]
        w_up, = gather.finish([3], tok, "up")
    a, r = _mm_nn(u2, w_up, F32, "mm_up_fwd", squared_relu=True)
    if ex:
        w_down, = gather.finish([4], [r], "down")
    w_down1 = w_down.reshape(1, D_FF, D_MODEL)
    mlp = _mm_nn(r, w_down1, F32, "mm_down_fwd")
    loss, dh2, dh2b, g_nf = _loss_head(h1, mlp, norm_final_w, target, "loss_head")

    def reduce_scatter(grads, name):
        rs = _ReduceScatter(grads, ex.parity, name) if ex else None
        return rs, ([rs.token] if ex else [])

    g_down = _mm_tn(r, dh2b, 1, BF16, "mm_down_wgrad").reshape(N_DEV, D_FF // N_DEV, D_MODEL)
    rs_down, tok = reduce_scatter([g_down], "rs_down")
    da = _mm_nt(dh2b, w_down1, BF16, "mm_down_dgrad", after=tok, relu_of=a)
    tok = [rs_down.pair_sums([da])] if ex else []
    g_up = _mm_tn(u2, da, N_DEV, BF16, "mm_up_wgrad", after=tok)
    rs_up, tok = reduce_scatter([g_up], "rs_up")
    du2 = _mm_nt(da, w_up, F32, "mm_up_dgrad", after=tok)
    tok = [rs_up.pair_sums([du2])] if ex else []
    dh1, dh1b, g_nmlp = _rms_bwd(du2, h1, norm_mlp_w, dh2, "rms_mlp_bwd", after=tok)

    g_out = _mm_tn(merged, dh1b, 1, BF16, "mm_out_wgrad").reshape(N_DEV, D_MODEL // N_DEV, D_MODEL)
    dmerged = _mm_nt(dh1b, w_out1, F32, "mm_out_dgrad")
    dpa, dpb, dga, dgb = _merge_bwd(dmerged, z, pa, pb, "merge_bwd")
    g_a = _mm_tn(y_a, dpa, N_DEV, BF16, "mm_a_wgrad")
    g_b = _mm_tn(y_b, dpb, N_DEV, BF16, "mm_b_wgrad")
    rs_mix, tok = reduce_scatter([g_a, g_b, g_out], "rs_mix")
    dya = _mm_nt(dpa, w_a, F32, "mm_a_dgrad", after=tok)
    dyb = _mm_nt(dpb, w_b, F32, "mm_b_dgrad", after=tok)
    tok = [rs_mix.pair_sums([dya, dyb])] if ex else []
    daq, dak, dav, dbias_t = _attn_bwd(z, bias_t, dyb, "attn_bwd", after=tok)
    dhq, dhf, dhi, dhg, g_lbl, g_hgw = _hgrn2_bwd(z, lb_logits, hg_norm_w, o_raw, s_all, dya, "hgrn2_bwd",
                                                  after=tok)
    dbias_rows = jnp.transpose(dbias_t.reshape(AT_HEADS // 2, BAND, 2, CHUNK), (3, 0, 2, 1)).reshape(
        CHUNK, AT_HEADS, BAND)
    dz =jnp.concatenate([dhq, dhf, dhi, dhg, daq, dak, dav, dga, dgb], axis=1)
    g_in = _mm_tn(u, dz, N_DEV, BF16, "mm_in_wgrad")
    rs_in, _ = reduce_scatter([g_in], "rs_in")
    tok = [rs_in.pair_sums([])] if ex else []
    du = _mm_nt(dz, w_in, F32, "mm_in_dgrad", after=tok)
    grad_x, _, g_nmix = _rms_bwd(du, x, norm_mix_w, dh1, "rms_mix_bwd")
    g_rel = _bias_reduce(dbias_rows, "bias_reduce", after=tok)[:, :N_REL]

    small = dict(lb_logits=g_lbl, hg_norm_w=g_hgw[0:1], rel_bias=g_rel, norm_mix_w=g_nmix, norm_mlp_w=g_nmlp,
                 norm_final_w=g_nf)
    grads = [rs_in, rs_mix, rs_up, rs_down] if ex else [g_in, g_a, g_b, g_out, g_up, g_down]
    return loss, grad_x, grads, small


def _gather_exchange(shards, mid_step=None):
    n = len(shards)

    def parts(ins, outs, sems):
        send_sems, recv_sems, local_sems = sems
        x, y, c = _position()
        chips = [(1 - x, y), (x, 1 - y), (1 - x, 1 - y)]

        def copy(w, k, block, to, src=None):
            dst = outs[w].at[4 * block[0] + 2 * block[1] + block[2]]
            return pltpu.make_async_remote_copy(
                src_ref=dst if src is None else src, dst_ref=dst,
                send_sem=send_sems.at[w, k], recv_sem=recv_sems.at[w, k], device_id=to, device_id_type=MESH)

        def local(w):
            return pltpu.make_async_copy(ins[w], outs[w].at[4 * x + 2 * y + c], local_sems.at[w])

        return (x, y, c), (x, y, 1 - c), chips, copy, local

    def start(ins, outs, sems):
        me, sibling, chips, copy, local = parts(ins, outs, sems)
        for w in range(n):
            local(w).start()
        for w in range(n):
            copy(w, 0, me, sibling, src=ins[w]).start()
            for j, chip in enumerate(chips):
                copy(w, 1 + j, me, (*chip, me[2]), src=ins[w]).start()

    def mid(ins, outs, sems):
        me, sibling, chips, copy, _ = parts(ins, outs, sems)
        for w in range(n):
            for j, chip in enumerate(chips):
                copy(w, 1 + j, (*chip, me[2]), me).wait_recv()
                copy(w, 4 + j, (*chip, me[2]), sibling).start()

    def end(ins, outs, sems):
        me, sibling, chips, copy, local = parts(ins, outs, sems)
        for w in range(n):
            copy(w, 0, sibling, me).wait_recv()
            for j, chip in enumerate(chips):
                copy(w, 4 + j, (*chip, sibling[2]), me).wait_recv()
        for w in range(n):
            for k in range(7):
                copy(w, k, me, sibling).wait_send()
            local(w).wait()

    return _Exchange(
        shards, [jax.ShapeDtypeStruct((N_DEV,) + s.shape, s.dtype) for s in shards],
        [pltpu.SemaphoreType.DMA((n, 7)), pltpu.SemaphoreType.DMA((n, 7)), pltpu.SemaphoreType.DMA((n,))],
        start, end, mid, mid_step)


def _mm_gathered(u, shard, order, name):
    T, K = u.shape
    _, Nb = shard.shape

    def body(order_ref, u_ref, shard_ref, z_ref, full_ref, wbuf, load_sem, send_sems, recv_sems, local_sem):
        s = pl.program_id(0)
        x, y, c = _position()
        me, sibling = (x, y, c), (x, y, 1 - c)
        chips = [(1 - x, y), (x, 1 - y), (1 - x, 1 - y)]

        def copy(k, block, to, src=None):
            dst = full_ref.at[4 * block[0] + 2 * block[1] + block[2]]
            return pltpu.make_async_remote_copy(
                src_ref=dst if src is None else src, dst_ref=dst,
                send_sem=send_sems.at[k], recv_sem=recv_sems.at[k], device_id=to, device_id_type=MESH)

        @pl.when(s == 0)
        def _():
            local = pltpu.make_async_copy(shard_ref, full_ref.at[4 * x + 2 * y + c], local_sem)
            local.start()
            copy(0, me, sibling, src=shard_ref).start()
            for j, chip in enumerate(chips):
                copy(1 + j, me, (*chip, c), src=shard_ref).start()
            local.wait()

        @pl.when(s == 1)
        def _():
            copy(0, sibling, me).wait_recv()

        for j, chip in enumerate(chips):
            direct, passed = ((2, 4), (3, 5), (6, 7))[j]

            @pl.when(s == direct)
            def _(j=j, chip=chip):
                copy(1 + j, (*chip, c), me).wait_recv()
                copy(4 + j, (*chip, c), sibling).start()

            @pl.when(s == passed)
            def _(j=j, chip=chip):
                copy(4 + j, (*chip, 1 - c), me).wait_recv()

        load = pltpu.make_async_copy(full_ref.at[order_ref[s]], wbuf, load_sem)
        load.start()
        load.wait()
        z_ref[...] = jnp.dot(u_ref[...], wbuf[...], preferred_element_type=F32)

        @pl.when(s == N_DEV - 1)
        def _():
            for k in range(7):
                copy(k, me, sibling).wait_send()

    return pl.pallas_call(
        body, name=name,
        grid_spec=pltpu.PrefetchScalarGridSpec(
            num_scalar_prefetch=1, grid=(N_DEV,),
            in_specs=[pl.BlockSpec((T, K), lambda s, order: (0, 0)), ANY],
            out_specs=[pl.BlockSpec((T, Nb), lambda s, order: (0, order[s])), ANY],
            scratch_shapes=[pltpu.VMEM((K, Nb), BF16), pltpu.SemaphoreType.DMA,
                            pltpu.SemaphoreType.DMA((7,)), pltpu.SemaphoreType.DMA((7,)), pltpu.SemaphoreType.DMA]),
        out_shape=[jax.ShapeDtypeStruct((T, N_DEV * Nb), F32), jax.ShapeDtypeStruct((N_DEV, K, Nb), BF16)],
        compiler_params=_cparams(("arbitrary",)),
    )(order, u, shard)


def _gather_order():
    x, y, c = _position()
    chips = [(1 - x, y), (x, 1 - y), (1 - x, 1 - y)]
    ids = [4 * x + 2 * y + c, 4 * x + 2 * y + (1 - c)]
    ids += [4 * cx + 2 * cy + c for cx, cy in chips[:2]] + [4 * cx + 2 * cy + (1 - c) for cx, cy in chips[:2]]
    ids += [4 * chips[2][0] + 2 * chips[2][1] + c, 4 * chips[2][0] + 2 * chips[2][1] + (1 - c)]
    return jnp.stack(ids).astype(jnp.int32)


def _run_exchange(comm, name):
    n_i, n_o = len(comm.arrays), len(comm.out_shape)

    def body(*refs):
        ins, outs, sems = refs[:n_i], refs[n_i:n_i + n_o], refs[n_i + n_o:]
        comm.start(ins, outs, sems)
        if comm.mid is not None:
            comm.mid(ins, outs, sems)
        comm.end(ins, outs, sems)

    return pl.pallas_call(
        body, name=name, in_specs=[ANY] * n_i, out_specs=[ANY] * n_o, out_shape=comm.out_shape,
        scratch_shapes=comm.scratch)(*comm.arrays)


def _exchange_sibling(grads, name):
    n = len(grads)

    def body(*refs):
        ins, outs = refs[:n], refs[n:2 * n]
        send_sems, recv_sems = refs[2 * n:]
        x, y, c = _position()
        copies = []
        for w in range(n):
            for s in range(N_CHIP):
                cp = pltpu.make_async_remote_copy(
                    src_ref=ins[w].at[2 * s + (1 - c)], dst_ref=outs[w].at[s],
                    send_sem=send_sems.at[w, s], recv_sem=recv_sems.at[w, s],
                    device_id=(x, y, 1 - c), device_id_type=MESH)
                cp.start()
                copies.append(cp)
        for cp in copies:
            cp.wait()

    return pl.pallas_call(
        body, name=name,
        in_specs=[ANY] * n, out_specs=[ANY] * n,
        out_shape=[jax.ShapeDtypeStruct((N_CHIP,) + g.shape[1:], g.dtype) for g in grads],
        scratch_shapes=[pltpu.SemaphoreType.DMA((n, N_CHIP)), pltpu.SemaphoreType.DMA((n, N_CHIP))],
    )(*grads)


def _pair_sum(g, land, parity, name):
    _, R, C = g.shape
    tr = _pick(R, (512, 256))

    def body(par_ref, g_ref, l_ref, o_ref):
        o_ref[...] = (g_ref[...].astype(F32) + l_ref[...].astype(F32)).astype(BF16)

    return pl.pallas_call(
        body, name=name,
        grid_spec=pltpu.PrefetchScalarGridSpec(
            num_scalar_prefetch=1, grid=(N_CHIP, R // tr),
            in_specs=[pl.BlockSpec((None, tr, C), lambda s, i, par: (2 * s + par[0], i, 0)),
                      pl.BlockSpec((None, tr, C), lambda s, i, par: (s, i, 0))],
            out_specs=pl.BlockSpec((None, tr, C), lambda s, i, par: (s, i, 0))),
        out_shape=jax.ShapeDtypeStruct((N_CHIP, R, C), BF16),
        compiler_params=_cparams(("parallel", "parallel")),
    )(parity, g, land)


def _scatter_exchange(partials):
    n = len(partials)

    def copies(ins, outs, sems):
        send_sems, recv_sems, local_sems = sems
        x, y, c = _position()
        chips = [(1 - x, y), (x, 1 - y), (1 - x, 1 - y)]
        my_slot = 2 * x + y
        local = [pltpu.make_async_copy(ins[w].at[my_slot], outs[w].at[my_slot], local_sems.at[w]) for w in range(n)]
        remote = [pltpu.make_async_remote_copy(
            src_ref=ins[w].at[2 * chip[0] + chip[1]], dst_ref=outs[w].at[my_slot],
            send_sem=send_sems.at[w, j], recv_sem=recv_sems.at[w, j], device_id=(*chip, c), device_id_type=MESH)
            for w in range(n) for j, chip in enumerate(chips)]
        return local, remote

    def start(ins, outs, sems):
        local, remote = copies(ins, outs, sems)
        for cp in local + remote:
            cp.start()

    def end(ins, outs, sems):
        local, remote = copies(ins, outs, sems)
        for cp in remote + local:
            cp.wait()

    return _Exchange(
        partials, [jax.ShapeDtypeStruct(p.shape, p.dtype) for p in partials],
        [pltpu.SemaphoreType.DMA((n, 3)), pltpu.SemaphoreType.DMA((n, 3)), pltpu.SemaphoreType.DMA((n,))],
        start, end)


HBM = pl.BlockSpec(memory_space=pltpu.HBM)
SEM = pl.BlockSpec(memory_space=pltpu.SEMAPHORE)
DATAFLOW = pltpu.SideEffectType.DATAFLOW_SIDE_EFFECTING


def _scatter_copies(ins, lands, send_sems, recv_sems):
    x, y, c = _position()
    chips = [(1 - x, y), (x, 1 - y), (1 - x, 1 - y)]
    return [pltpu.make_async_remote_copy(
        src_ref=ins[w].at[2 * chip[0] + chip[1]], dst_ref=lands[w].at[2 * x + y],
        send_sem=send_sems[3 * w + j], recv_sem=recv_sems[3 * w + j], device_id=(*chip, c), device_id_type=MESH)
        for w in range(len(ins)) for j, chip in enumerate(chips)]


def _scatter_start(partials, name):
    n = len(partials)

    def body(*refs):
        ins, lands = refs[:n], refs[n:2 * n]
        sems = refs[4 * n:10 * n]
        for cp in _scatter_copies(ins, lands, sems[:3 * n], sems[3 * n:]):
            cp.start()
        refs[-1][...] = jnp.zeros_like(refs[-1])

    def in_hbm(a):
        return pltpu.with_memory_space_constraint(a, pltpu.HBM)

    bufs = tuple(pltpu.HBM(p.shape, p.dtype) for p in partials)
    outs = pl.pallas_call(
        body, name=name,
        out_shape=bufs + bufs + (pltpu.SemaphoreType.DMA(()),) * (6 * n) + (jax.ShapeDtypeStruct((8, 128), F32),),
        in_specs=[HBM] * (2 * n),
        out_specs=(HBM,) * (2 * n) + (SEM,) * (6 * n) + (pl.BlockSpec(memory_space=pltpu.VMEM),),
        input_output_aliases={i: i for i in range(2 * n)},
        compiler_params=pltpu.CompilerParams(has_side_effects=DATAFLOW),
    )(*[in_hbm(p) for p in partials], *[in_hbm(lax.empty(p.shape, p.dtype)) for p in partials])
    return list(outs[:-1]), outs[-1]


def _scatter_wait(handle, after, name):
    n = len(handle) // 8
    bufs, sems = handle[:2 * n], handle[2 * n:]

    def body(*refs):
        ins, lands = refs[:n], refs[n:2 * n]
        sems = refs[2 * n:8 * n]
        for cp in _scatter_copies(ins, lands, sems[:3 * n], sems[3 * n:]):
            cp.wait_send()
            cp.wait_recv()

    outs = pl.pallas_call(
        body, name=name,
        out_shape=tuple(pltpu.HBM(b.shape, b.dtype) for b in bufs),
        in_specs=[HBM] * (2 * n) + [SEM] * (6 * n) + [ANY] * len(after), out_specs=(HBM,) * (2 * n),
        input_output_aliases={i: i for i in range(2 * n)},
        compiler_params=pltpu.CompilerParams(has_side_effects=DATAFLOW),
    )(*bufs, *sems, *after)
    return list(outs[:n]), list(outs[n:])


def _split_call(name, bufs, waits=(), starts=None, after=()):
    nb = len(bufs)
    n_new = starts[1] if starts else 0
    wait_sems = [s for w in waits for s in (*w[1], *w[2])]

    def body(*refs):
        b, pos = refs[:nb], nb
        for plan, ss, _, send_idx, recv_idx in waits:
            k = len(ss)
            copies = plan(b, refs[pos:pos + k], refs[pos + k:pos + 2 * k])
            pos += 2 * k
            for i in recv_idx:
                copies[i].wait_recv()
            for i in send_idx:
                copies[i].wait_send()
        outs = refs[pos + len(after):]
        if starts:
            for cp in starts[0](b, outs[nb:nb + n_new], outs[nb + n_new:nb + 2 * n_new]):
                cp.start()
        outs[-1][...] = jnp.zeros_like(outs[-1])

    res = pl.pallas_call(
        body, name=name,
        out_shape=tuple(pltpu.HBM(a.shape, a.dtype) for a in bufs) + (pltpu.SemaphoreType.DMA(()),) * (2 * n_new)
        + (jax.ShapeDtypeStruct((8, 128), F32),),
        in_specs=[HBM] * nb + [SEM] * len(wait_sems) + [ANY] * len(after),
        out_specs=(HBM,) * nb + (SEM,) * (2 * n_new) + (pl.BlockSpec(memory_space=pltpu.VMEM),),
        input_output_aliases={i: i for i in range(nb)},
        compiler_params=pltpu.CompilerParams(has_side_effects=DATAFLOW),
    )(*bufs, *wait_sems, *after)
    return list(res[:nb]), list(res[nb:nb + n_new]), list(res[nb + n_new:nb + 2 * n_new]), res[-1]


def _in_hbm(a):
    return pltpu.with_memory_space_constraint(a, pltpu.HBM)


def _remote(src, dst, send_sem, recv_sem, to):
    return pltpu.make_async_remote_copy(src_ref=src, dst_ref=dst, send_sem=send_sem, recv_sem=recv_sem,
                                        device_id=to, device_id_type=MESH)


def _other_chips():
    x, y, _ = _position()
    return [(1 - x, y), (x, 1 - y), (1 - x, 1 - y)]


def _plan_gather_first(n):
    def plan(b, ss, rs):
        x, y, c = _position()
        to = [(x, y, 1 - c)] + [(*chip, c) for chip in _other_chips()]
        return [_remote(b[w], b[n + w].at[4 * x + 2 * y + c], ss[4 * w + k], rs[4 * w + k], to[k])
                for w in range(n) for k in range(4)]
    return plan, 4 * n


def _plan_gather_pass(n):
    def plan(b, ss, rs):
        x, y, c = _position()
        copies = []
        for w in range(n):
            for j, chip in enumerate(_other_chips()):
                blk = b[n + w].at[4 * chip[0] + 2 * chip[1] + c]
                copies.append(_remote(blk, blk, ss[3 * w + j], rs[3 * w + j], (x, y, 1 - c)))
        return copies
    return plan, 3 * n


def _plan_sibling(n):
    def plan(b, ss, rs):
        x, y, c = _position()
        return [_remote(b[w].at[2 * s + (1 - c)], b[n + w].at[s], ss[4 * w + s], rs[4 * w + s], (x, y, 1 - c))
                for w in range(n) for s in range(N_CHIP)]
    return plan, 4 * n


def _plan_scatter(n):
    def plan(b, ss, rs):
        x, y, c = _position()
        return [_remote(b[w].at[2 * chip[0] + chip[1]], b[n + w].at[2 * x + y], ss[3 * w + j], rs[3 * w + j],
                        (*chip, c))
                for w in range(n) for j, chip in enumerate(_other_chips())]
    return plan, 3 * n


class _Gather:
    def __init__(self, shards, after, name):
        self.n, self.name = len(shards), name
        x, y, c = _position()
        placed = [lax.dynamic_update_index_in_dim(lax.empty((N_DEV,) + s.shape, s.dtype), s, 4 * x + 2 * y + c, 0)
                  for s in shards]
        bufs, self.ss, self.rs, self.token = _split_call(
            name + "_start", [_in_hbm(a) for a in list(shards) + placed], starts=_plan_gather_first(self.n),
            after=after)
        self.shards, self.fulls = bufs[:self.n], bufs[self.n:]
        self.passed = {}

    def _sub(self, ids, sems, per):
        return [sems[per * w + k] for w in ids for k in range(per)]

    def pass_on(self, ids, after, tag):
        m = len(ids)
        first = (_plan_gather_first(m)[0], self._sub(ids, self.ss, 4), self._sub(ids, self.rs, 4),
                 [], [4 * i + k for i in range(m) for k in (1, 2, 3)])
        bufs, ss, rs, token = _split_call(
            "%s_pass_%s" % (self.name, tag), [self.shards[w] for w in ids] + [self.fulls[w] for w in ids],
            waits=[first], starts=_plan_gather_pass(m), after=after)
        for i, w in enumerate(ids):
            self.shards[w], self.fulls[w] = bufs[i], bufs[m + i]
        self.passed[tuple(ids)] = (ss, rs)
        return token

    def finish(self, ids, after, tag):
        m = len(ids)
        ss2, rs2 = self.passed[tuple(ids)]
        first = (_plan_gather_first(m)[0], self._sub(ids, self.ss, 4), self._sub(ids, self.rs, 4),
                 list(range(4 * m)), [4 * i for i in range(m)])
        passed = (_plan_gather_pass(m)[0], ss2, rs2, list(range(3 * m)), list(range(3 * m)))
        bufs, _, _, _ = _split_call(
            "%s_finish_%s" % (self.name, tag), [self.shards[w] for w in ids] + [self.fulls[w] for w in ids],
            waits=[first, passed], after=after)
        return bufs[m:]


class _ReduceScatter:
    def __init__(self, grads, parity, name):
        self.n, self.name, self.parity = len(grads), name, parity
        lands = [lax.empty((N_CHIP,) + g.shape[1:], g.dtype) for g in grads]
        self.bufs, self.ss, self.rs, self.token = _split_call(
            name + "_sibling_start", [_in_hbm(a) for a in list(grads) + lands], starts=_plan_sibling(self.n))

    def pair_sums(self, after):
        n = self.n
        bufs, _, _, _ = _split_call(
            self.name + "_sibling_wait", self.bufs,
            waits=[(_plan_sibling(n)[0], self.ss, self.rs, list(range(4 * n)), list(range(4 * n)))], after=after)
        sums = [_pair_sum(bufs[w], bufs[n + w], self.parity, "%s_pair_sum_%d" % (self.name, w)) for w in range(n)]
        lands = [lax.empty(s.shape, s.dtype) for s in sums]
        self.bufs, self.ss, self.rs, token = _split_call(
            self.name + "_scatter_start", [_in_hbm(a) for a in sums + lands], starts=_plan_scatter(n))
        return token

    def finish(self, after):
        n = self.n
        bufs, _, _, _ = _split_call(
            self.name + "_scatter_wait", self.bufs,
            waits=[(_plan_scatter(n)[0], self.ss, self.rs, list(range(3 * n)), list(range(3 * n)))], after=after)
        return bufs[:n], bufs[n:]


class _Exchanges:
    def __init__(self, parity, order):
        self.parity, self.order = parity, order


def _gather_small(packed, name):
    R = packed.shape[0]

    def body(x_ref, out_ref, send_sems, recv_sems):
        x, y, c = _position()
        me = 4 * x + 2 * y + c
        out_ref[me] = x_ref[...]
        copies = []
        for k in range(1, N_DEV):
            to = (x ^ ((k >> 2) & 1), y ^ ((k >> 1) & 1), c ^ (k & 1))
            cp = pltpu.make_async_remote_copy(
                src_ref=x_ref, dst_ref=out_ref.at[me],
                send_sem=send_sems.at[k], recv_sem=recv_sems.at[k], device_id=to, device_id_type=MESH)
            cp.start()
            copies.append((k, to, cp))
        for k, to, cp in copies:
            cp.wait_send()
            pltpu.make_async_remote_copy(
                src_ref=x_ref, dst_ref=out_ref.at[4 * to[0] + 2 * to[1] + to[2]],
                send_sem=send_sems.at[k], recv_sem=recv_sems.at[k], device_id=to, device_id_type=MESH).wait_recv()

    return pl.pallas_call(
        body, name=name,
        in_specs=[pl.BlockSpec(memory_space=pltpu.VMEM)], out_specs=pl.BlockSpec(memory_space=pltpu.VMEM),
        out_shape=jax.ShapeDtypeStruct((N_DEV, R, 128), F32),
        scratch_shapes=[pltpu.SemaphoreType.DMA((N_DEV,)), pltpu.SemaphoreType.DMA((N_DEV,))],
    )(packed)


def _adamw_math(w, g, m, v):
    m = ADAM_B1 * m + (1.0 - ADAM_B1) * g
    v = ADAM_B2 * v + (1.0 - ADAM_B2) * (g * g)
    m_hat = m / (1.0 - ADAM_B1 ** ADAM_STEP)
    v_hat = v / (1.0 - ADAM_B2 ** ADAM_STEP)
    delta = -ADAM_LR * (m_hat / (jnp.sqrt(v_hat) + ADAM_EPS) + ADAM_WD * w)
    return delta, m, v


def _adamw_big(w, m, v, parts, name):
    R, C = w.shape
    tr = _pick(R, (256,))

    def body(w_ref, m_ref, v_ref, p_ref, g_ref, d_ref, nm_ref, nv_ref):
        g = p_ref[0].astype(F32)
        for s in range(1, N_CHIP):
            g = g + p_ref[s].astype(F32)
        d, nm, nv = _adamw_math(w_ref[...], g, m_ref[...], v_ref[...])
        g_ref[...] = g
        d_ref[...] = d
        nm_ref[...] = nm
        nv_ref[...] = nv

    blk = pl.BlockSpec((tr, C), lambda i: (i, 0))
    out = jax.ShapeDtypeStruct((R, C), F32)
    return pl.pallas_call(
        body, name=name, grid=(R // tr,),
        in_specs=[blk, blk, blk, pl.BlockSpec((N_CHIP, tr, C), lambda i: (0, i, 0))],
        out_specs=[blk, blk, blk, blk], out_shape=[out, out, out, out],
        compiler_params=_cparams(("parallel",)),
    )(w, m, v, parts)


def _adamw_big_landed(w, m, v, parts, lands, slot, name):
    R, C = w.shape
    tr = _pick(R, (256,))

    def body(slot_ref, w_ref, m_ref, v_ref, own_ref, l1_ref, l2_ref, l3_ref, g_ref, d_ref, nm_ref, nv_ref):
        g = own_ref[...].astype(F32)
        for ref in (l1_ref, l2_ref, l3_ref):
            g = g + ref[...].astype(F32)
        d, nm, nv = _adamw_math(w_ref[...], g, m_ref[...], v_ref[...])
        g_ref[...] = g
        d_ref[...] = d
        nm_ref[...] = nm
        nv_ref[...] = nv

    blk = pl.BlockSpec((tr, C), lambda i, slot: (i, 0))

    def chip(k):
        return pl.BlockSpec((None, tr, C), lambda i, slot: ((slot[0] + k) % N_CHIP, i, 0))

    out = jax.ShapeDtypeStruct((R, C), F32)
    return pl.pallas_call(
        body, name=name,
        grid_spec=pltpu.PrefetchScalarGridSpec(
            num_scalar_prefetch=1, grid=(R // tr,),
            in_specs=[blk, blk, blk, chip(0), chip(1), chip(2), chip(3)],
            out_specs=[blk, blk, blk, blk]),
        out_shape=[out, out, out, out],
        compiler_params=_cparams(("parallel",)),
    )(slot, w, m, v, parts, lands, lands, lands)


def _adamw_small(w, m, v, gathered, name):
    R = w.shape[0]

    def body(w_ref, m_ref, v_ref, p_ref, g_ref, d_ref, nm_ref, nv_ref):
        g = p_ref[0]
        for s in range(1, N_DEV):
            g = g + p_ref[s]
        d, nm, nv = _adamw_math(w_ref[...], g, m_ref[...], v_ref[...])
        g_ref[...] = g
        d_ref[...] = d
        nm_ref[...] = nm
        nv_ref[...] = nv

    out = jax.ShapeDtypeStruct((R, 128), F32)
    return pl.pallas_call(
        body, name=name, out_shape=[out, out, out, out],
    )(w, m, v, gathered)


SMALL_NAMES = ("lb_logits", "hg_norm_w", "rel_bias", "norm_mix_w", "norm_mlp_w", "norm_final_w")
SMALL_SHAPES = {"lb_logits": (2, HG_WIDTH), "hg_norm_w": (1, HG_DK), "rel_bias": (AT_HEADS, N_REL_PAD),
                "norm_mix_w": (1, D_MODEL), "norm_mlp_w": (1, D_MODEL), "norm_final_w": (1, D_MODEL)}


def _pack_small(parts):
    rows = []
    for nme in SMALL_NAMES:
        p = parts[nme]
        if nme == "rel_bias":
            p = jnp.pad(p, ((0, 0), (0, N_REL_PAD - N_REL)))
        rows.append(p.reshape(-1, 128))
    flat = jnp.concatenate(rows, axis=0)
    return jnp.pad(flat, ((0, SMALL_ROWS - flat.shape[0]), (0, 0)))


def _unpack_small(packed):
    out, at = {}, 0
    for nme in SMALL_NAMES:
        shp = SMALL_SHAPES[nme]
        nrow = shp[0] * shp[1] // 128
        p = packed[at:at + nrow].reshape(shp)
        at += nrow
        out[nme] = p[:, :N_REL] if nme == "rel_bias" else p
    return out


BIG_NAMES = ("w_in", "w_branch_a", "w_branch_b", "w_out", "w_up", "w_down")


def kernel(x, w_in, lb_logits, hg_norm_w, rel_bias, w_branch_a, w_branch_b, w_out, norm_mix_w, norm_mlp_w, w_up, w_down, norm_final_w, loss_target, m_w_in, m_lb_logits, m_hg_norm_w, m_rel_bias, m_w_branch_a, m_w_branch_b, m_w_out, m_norm_mix_w, m_norm_mlp_w, m_w_up, m_w_down, m_norm_final_w, v_w_in, v_lb_logits, v_hg_norm_w, v_rel_bias, v_w_branch_a, v_w_branch_b, v_w_out, v_norm_mix_w, v_norm_mlp_w, v_w_up, v_w_down, v_norm_final_w):
    big_w = [w_in[0], w_branch_a[0], w_branch_b[0], w_out[0], w_up[0], w_down[0]]
    big_m = [m_w_in[0], m_w_branch_a[0], m_w_branch_b[0], m_w_out[0], m_w_up[0], m_w_down[0]]
    big_v = [v_w_in[0], v_w_branch_a[0], v_w_branch_b[0], v_w_out[0], v_w_up[0], v_w_down[0]]

    shards = [w.astype(BF16) for w in big_w]
    parity = lax.axis_index("c").astype(jnp.int32).reshape(1)
    loss_part, grad_x, chip_parts, small = _local_step(
        x[0], loss_target[0], lb_logits, hg_norm_w, rel_bias[0], norm_mix_w, norm_mlp_w,
        norm_final_w.reshape(1, D_MODEL), shards[0], shards[1:], _Exchanges(parity, _gather_order()))
    loss = lax.psum(loss_part[0, 0], ("x", "y", "c"))
    rs_in, rs_mix, rs_up, rs_down = chip_parts
    slot = (2 * lax.axis_index("x") + lax.axis_index("y")).astype(jnp.int32).reshape(1)
    big = {}

    def finish(rs, names, after):
        sums, lands = rs.finish(after)
        for nme, own, land in zip(names, sums, lands):
            i = BIG_NAMES.index(nme)
            big[nme] = _adamw_big_landed(big_w[i], big_m[i], big_v[i], own, land, slot, "adamw_" + nme)
        return [big[nme][1] for nme in names]

    done = finish(rs_down, ["w_down"], [grad_x])
    done = finish(rs_up, ["w_up"], done)
    done = finish(rs_mix, ["w_branch_a", "w_branch_b", "w_out"], done)

    sw = dict(lb_logits=lb_logits, hg_norm_w=hg_norm_w, rel_bias=rel_bias[0], norm_mix_w=norm_mix_w,
              norm_mlp_w=norm_mlp_w, norm_final_w=norm_final_w.reshape(1, D_MODEL))
    sm = dict(lb_logits=m_lb_logits, hg_norm_w=m_hg_norm_w, rel_bias=m_rel_bias[0], norm_mix_w=m_norm_mix_w,
              norm_mlp_w=m_norm_mlp_w, norm_final_w=m_norm_final_w.reshape(1, D_MODEL))
    sv = dict(lb_logits=v_lb_logits, hg_norm_w=v_hg_norm_w, rel_bias=v_rel_bias[0], norm_mix_w=v_norm_mix_w,
              norm_mlp_w=v_norm_mlp_w, norm_final_w=v_norm_final_w.reshape(1, D_MODEL))
    gathered = _gather_small(_pack_small(small), "gather_small")
    small_packed = _adamw_small(_pack_small(sw), _pack_small(sm), _pack_small(sv), gathered, "adamw_small")
    small_out = [_unpack_small(p) for p in small_packed]

    finish(rs_in, ["w_in"], done + [small_packed[0]])

    def leaf(kind, nme):
        if nme in BIG_NAMES:
            return big[nme][kind][None]
        p = small_out[kind][nme]
        if nme == "rel_bias":
            return p[None]
        if nme == "norm_final_w":
            return p.reshape(D_MODEL)
        return p

    order = ("w_in", "lb_logits", "hg_norm_w", "rel_bias", "w_branch_a", "w_branch_b", "w_out", "norm_mix_w",
             "norm_mlp_w", "w_up", "w_down", "norm_final_w")
    outs = [loss, grad_x[None]]
    for kind in range(4):
        outs += [leaf(kind, nme) for nme in order]
    return tuple(outs)
```

```python
import functools

import jax
import jax.numpy as jnp
from jax import lax
from jax.experimental import pallas as pl
from jax.experimental.pallas import tpu as pltpu

F32 = jnp.float32
BF16 = jnp.bfloat16
HIGHEST = lax.Precision.HIGHEST
MESH = pl.DeviceIdType.MESH

D_MODEL = 2048
HG_HEADS = 8
HG_DK = 128
HG_WIDTH = 1024
AT_HEADS = 16
AT_DH = 64
AT_WIDTH = 1024
CHUNK = 64
LEFT_CHUNKS = 8
BAND = (LEFT_CHUNKS + 1) * CHUNK
PAD = LEFT_CHUNKS * CHUNK
REL_CLIP = 256
N_REL = 2 * REL_CLIP + 1
N_REL_PAD = 640
D_FF = 4 * D_MODEL
D_IN = 4 * HG_WIDTH + 3 * AT_WIDTH + 2 * D_MODEL
EPS = 1e-6
N_DEV = 8
N_CHIP = 4

ADAM_LR = 0.001
ADAM_B1 = 0.9
ADAM_B2 = 0.999
ADAM_EPS = 1e-08
ADAM_WD = 0.01
ADAM_STEP = 10

COL_HQ, COL_HF, COL_HI, COL_HG = 0, 8, 16, 24
COL_AQ, COL_AK, COL_AV = 32, 40, 48
COL_GATE_A, COL_GATE_B = 7, 9

VMEM_LIMIT = 56 * 1024 * 1024
SMALL_ROWS = 152


def _cparams(sem=None, **kw):
    if sem is not None:
        kw["dimension_semantics"] = sem
    return pltpu.CompilerParams(vmem_limit_bytes=VMEM_LIMIT, **kw)


def _pick(n, cands):
    for c in cands:
        if n % c == 0:
            return c
    return n


def _sigmoid(x):
    return 1.0 / (1.0 + jnp.exp(-x))


ANY = pl.BlockSpec(memory_space=pl.ANY)


def _position():
    return lax.axis_index("x"), lax.axis_index("y"), lax.axis_index("c")


def _call(body, args, *, name, grid, in_specs, out_specs, out_shape, scratch_shapes=(), sem=None, after=()):
    n_in = len(args)

    def ordered(*refs):
        body(*refs[:n_in], *refs[n_in + len(after):])

    return list(pl.pallas_call(
        ordered if after else body, name=name, grid=grid, in_specs=list(in_specs) + [ANY] * len(after),
        out_specs=out_specs, out_shape=out_shape, scratch_shapes=list(scratch_shapes),
        compiler_params=_cparams(sem))(*args, *after))


MAX_CONTRACTION_TILE = 4096


def _accumulate(part, acc_ref, step, n_steps, finish):
    if n_steps == 1:
        finish(part)
        return

    @pl.when(step == 0)
    def _():
        acc_ref[...] = part

    @pl.when(step > 0)
    def _():
        acc_ref[...] += part

    @pl.when(step == n_steps - 1)
    def _():
        finish(acc_ref[...])


def _mm_nn(a, wb, out_dtype, name, after=(), squared_relu=False):
    M, K = a.shape
    NB, K2, Nb = wb.shape
    assert K == K2
    tm = min(M, 1024)
    tk = min(K, MAX_CONTRACTION_TILE)
    tn = _pick(Nb, (512, 1408, 256))
    nk = K // tk
    nn = Nb // tn

    def body(a_ref, b_ref, o_ref, *rest):
        def finish(total):
            o_ref[...] = total.astype(out_dtype)
            if squared_relu:
                ra = jnp.maximum(total, 0.0)
                rest[0][...] = (ra * ra).astype(BF16)

        part = jnp.dot(a_ref[...], b_ref[...], preferred_element_type=F32)
        _accumulate(part, rest[-1] if nk > 1 else None, pl.program_id(3), nk, finish)

    tile = pl.BlockSpec((tm, tn), lambda m, j, n, k: (m, j * nn + n))
    outs = _call(
        body, (a, wb), name=name, grid=(M // tm, NB, nn, nk),
        in_specs=[pl.BlockSpec((tm, tk), lambda m, j, n, k: (m, k)),
                  pl.BlockSpec((None, tk, tn), lambda m, j, n, k: (j, k, n))],
        out_specs=[tile, tile] if squared_relu else [tile],
        out_shape=[jax.ShapeDtypeStruct((M, NB * Nb), out_dtype)]
        + ([jax.ShapeDtypeStruct((M, NB * Nb), BF16)] if squared_relu else []),
        scratch_shapes=[] if nk == 1 else [pltpu.VMEM((tm, tn), F32)],
        sem=("parallel", "parallel", "parallel", "arbitrary"), after=after)
    return outs if squared_relu else outs[0]


def _mm_nt(a, wb, out_dtype, name, after=(), relu_of=None):
    M, N = a.shape
    NB, K, Nb = wb.shape
    assert N == NB * Nb
    tm = min(M, 1024)
    tko = _pick(K, (1024,))
    tc = _pick(Nb, (2048, 1024, 1408, 256))
    nc = Nb // tc
    jb = max([d for d in (8, 4, 2, 1) if NB % d == 0 and d * tc <= MAX_CONTRACTION_TILE]) if nc == 1 else 1
    nsteps = (NB // jb) * nc
    gated = relu_of is not None

    def body(a_ref, b_ref, *rest):
        o_ref = rest[1] if gated else rest[0]

        def finish(total):
            if gated:
                total = total * (2.0 * jnp.maximum(rest[0][...], 0.0))
            o_ref[...] = total.astype(out_dtype)

        part = sum(lax.dot_general(a_ref[:, i * tc:(i + 1) * tc], b_ref[i], (((1,), (1,)), ((), ())),
                                   preferred_element_type=F32) for i in range(jb))
        _accumulate(part, rest[-1], pl.program_id(2) * nc + pl.program_id(3), nsteps, finish)

    tile = pl.BlockSpec((tm, tko), lambda m, ko, j, c: (m, ko))
    out, = _call(
        body, (a, wb) + ((relu_of,) if gated else ()), name=name,
        grid=(M // tm, K // tko, NB // jb, nc),
        in_specs=[pl.BlockSpec((tm, jb * tc), lambda m, ko, j, c: (m, j * nc + c)),
                  pl.BlockSpec((jb, tko, tc), lambda m, ko, j, c: (j, ko, c))] + ([tile] if gated else []),
        out_specs=[tile],
        out_shape=[jax.ShapeDtypeStruct((M, K), out_dtype)],
        scratch_shapes=[] if nsteps == 1 else [pltpu.VMEM((tm, tko), F32)],
        sem=("parallel", "parallel", "arbitrary", "arbitrary"), after=after)
    return out


def _mm_tn(a, g, nb, out_dtype, name, after=()):
    M, Ka = a.shape
    M2, N = g.shape
    assert M == M2 and N % nb == 0
    Nb = N // nb
    tka = _pick(Ka, (1024,))
    tn = _pick(Nb, (512, 1408, 256))
    nn = Nb // tn

    def body(a_ref, g_ref, o_ref):
        o_ref[...] = lax.dot_general(a_ref[...], g_ref[...], (((0,), (0,)), ((), ())),
                                     preferred_element_type=F32).astype(out_dtype)

    return _call(
        body, (a, g), name=name,
        grid=(Ka // tka, nb, nn),
        in_specs=[pl.BlockSpec((M, tka), lambda ka, j, n: (0, ka)),
                  pl.BlockSpec((M, tn), lambda ka, j, n: (0, j * nn + n))],
        out_specs=[pl.BlockSpec((None, tka, tn), lambda ka, j, n: (j, ka, n))],
        out_shape=[jax.ShapeDtypeStruct((nb, Ka, Nb), out_dtype)],
        sem=("parallel", "parallel", "parallel"), after=after)[0]


ROW_TILE = 256


def _rms_fwd(x, w, name):
    T, Dm = x.shape

    def body(x_ref, w_ref, u_ref):
        xv = x_ref[...]
        r = lax.rsqrt(jnp.mean(xv * xv, axis=-1, keepdims=True) + EPS)
        u_ref[...] = (xv * r * w_ref[...]).astype(BF16)

    return pl.pallas_call(
        body, name=name, grid=(T // ROW_TILE,),
        in_specs=[pl.BlockSpec((ROW_TILE, Dm), lambda i: (i, 0)), pl.BlockSpec((1, Dm), lambda i: (0, 0))],
        out_specs=pl.BlockSpec((ROW_TILE, Dm), lambda i: (i, 0)),
        out_shape=jax.ShapeDtypeStruct((T, Dm), BF16),
        compiler_params=_cparams(("parallel",)),
    )(x, w)


def _resid_rms_fwd(x, mix, w, name):
    T, Dm = x.shape

    def body(x_ref, m_ref, w_ref, h_ref, u_ref):
        h = x_ref[...] + m_ref[...]
        h_ref[...] = h
        r = lax.rsqrt(jnp.mean(h * h, axis=-1, keepdims=True) + EPS)
        u_ref[...] = (h * r * w_ref[...]).astype(BF16)

    row = pl.BlockSpec((ROW_TILE, Dm), lambda i: (i, 0))
    return pl.pallas_call(
        body, name=name, grid=(T // ROW_TILE,),
        in_specs=[row, row, pl.BlockSpec((1, Dm), lambda i: (0, 0))],
        out_specs=[row, row],
        out_shape=[jax.ShapeDtypeStruct((T, Dm), F32), jax.ShapeDtypeStruct((T, Dm), BF16)],
        compiler_params=_cparams(("parallel",)),
    )(x, mix, w)


def _loss_head(h1, mlp, wf, target, name):
    T, Dm = h1.shape

    def body(h_ref, m_ref, w_ref, t_ref, loss_ref, dh_ref, dhb_ref, dw_ref):
        i = pl.program_id(0)
        h = h_ref[...] + m_ref[...]
        r = lax.rsqrt(jnp.mean(h * h, axis=-1, keepdims=True) + EPS)
        xh = h * r
        wv = w_ref[...]
        e = xh * wv - t_ref[...]
        part = 0.5 * jnp.sum(jnp.mean(e * e, axis=-1, keepdims=True), axis=0, keepdims=True)
        dy = e * (1.0 / Dm)
        dw = jnp.sum(dy * xh, axis=0, keepdims=True)
        gy = dy * wv
        dh = r * (gy - xh * jnp.mean(gy * xh, axis=-1, keepdims=True))
        dh_ref[...] = dh
        dhb_ref[...] = dh.astype(BF16)

        @pl.when(i == 0)
        def _():
            loss_ref[...] = jnp.zeros_like(loss_ref)
            dw_ref[...] = jnp.zeros_like(dw_ref)

        loss_ref[...] += jnp.broadcast_to(part, loss_ref.shape)
        dw_ref[...] += dw

    row = pl.BlockSpec((ROW_TILE, Dm), lambda i: (i, 0))
    vec = pl.BlockSpec((1, Dm), lambda i: (0, 0))
    return pl.pallas_call(
        body, name=name, grid=(T // ROW_TILE,),
        in_specs=[row, row, vec, row],
        out_specs=[pl.BlockSpec((8, 128), lambda i: (0, 0)), row, row, vec],
        out_shape=[jax.ShapeDtypeStruct((8, 128), F32), jax.ShapeDtypeStruct((T, Dm), F32),
                   jax.ShapeDtypeStruct((T, Dm), BF16), jax.ShapeDtypeStruct((1, Dm), F32)],
        compiler_params=_cparams(("arbitrary",)),
    )(h1, mlp, wf, target)


def _rms_bwd(dyn, x, w, dres, name, after=()):
    T, Dm = x.shape

    def body(g_ref, x_ref, w_ref, r_ref, dx_ref, dxb_ref, dw_ref):
        i = pl.program_id(0)
        xv = x_ref[...]
        r = lax.rsqrt(jnp.mean(xv * xv, axis=-1, keepdims=True) + EPS)
        xh = xv * r
        g = g_ref[...]
        dw = jnp.sum(g * xh, axis=0, keepdims=True)
        gy = g * w_ref[...]
        dx = r_ref[...] + r * (gy - xh * jnp.mean(gy * xh, axis=-1, keepdims=True))
        dx_ref[...] = dx
        dxb_ref[...] = dx.astype(BF16)

        @pl.when(i == 0)
        def _():
            dw_ref[...] = jnp.zeros_like(dw_ref)

        dw_ref[...] += dw

    row = pl.BlockSpec((ROW_TILE, Dm), lambda i: (i, 0))
    vec = pl.BlockSpec((1, Dm), lambda i: (0, 0))
    return _call(
        body, (dyn, x, w, dres), name=name, grid=(T // ROW_TILE,),
        in_specs=[row, row, vec, row],
        out_specs=[row, row, vec],
        out_shape=[jax.ShapeDtypeStruct((T, Dm), F32), jax.ShapeDtypeStruct((T, Dm), BF16),
                   jax.ShapeDtypeStruct((1, Dm), F32)],
        sem=("arbitrary",), after=after)


COL_TILE = 2048


def _relu2_fwd(a, name):
    T, N = a.shape

    def body(a_ref, r_ref):
        ra = jnp.maximum(a_ref[...], 0.0)
        r_ref[...] = (ra * ra).astype(BF16)

    blk = pl.BlockSpec((ROW_TILE, COL_TILE), lambda i, j: (i, j))
    return pl.pallas_call(
        body, name=name, grid=(T // ROW_TILE, N // COL_TILE), in_specs=[blk], out_specs=blk,
        out_shape=jax.ShapeDtypeStruct((T, N), BF16),
        compiler_params=_cparams(("parallel", "parallel")),
    )(a)


def _relu2_bwd(dr, a, name, after=()):
    T, N = a.shape

    def body(dr_ref, a_ref, da_ref):
        da_ref[...] = (dr_ref[...] * (2.0 * jnp.maximum(a_ref[...], 0.0))).astype(BF16)

    blk = pl.BlockSpec((ROW_TILE, COL_TILE), lambda i, j: (i, j))
    return _call(
        body, (dr, a), name=name, grid=(T // ROW_TILE, N // COL_TILE), in_specs=[blk, blk], out_specs=[blk],
        out_shape=[jax.ShapeDtypeStruct((T, N), BF16)], sem=("parallel", "parallel"), after=after)[0]


GATE_TILE = 1024


def _merge_fwd(z, pa, pb, name):
    T, Dm = pa.shape

    def body(za_ref, zb_ref, pa_ref, pb_ref, m_ref):
        m_ref[...] = (_sigmoid(za_ref[...]) * pa_ref[...] + _sigmoid(zb_ref[...]) * pb_ref[...]).astype(BF16)

    blk = pl.BlockSpec((ROW_TILE, GATE_TILE), lambda i, j: (i, j))
    return pl.pallas_call(
        body, name=name, grid=(T // ROW_TILE, Dm // GATE_TILE),
        in_specs=[pl.BlockSpec((ROW_TILE, GATE_TILE), lambda i, j: (i, COL_GATE_A + j)),
                  pl.BlockSpec((ROW_TILE, GATE_TILE), lambda i, j: (i, COL_GATE_B + j)), blk, blk],
        out_specs=blk,
        out_shape=jax.ShapeDtypeStruct((T, Dm), BF16),
        compiler_params=_cparams(("parallel", "parallel")),
    )(z, z, pa, pb)


def _merge_bwd(dm, z, pa, pb, name):
    T, Dm = pa.shape

    def body(dm_ref, za_ref, zb_ref, pa_ref, pb_ref, dpa_ref, dpb_ref, dga_ref, dgb_ref):
        d = dm_ref[...]
        ga = _sigmoid(za_ref[...])
        gb = _sigmoid(zb_ref[...])
        dpa_ref[...] = (d * ga).astype(BF16)
        dpb_ref[...] = (d * gb).astype(BF16)
        dga_ref[...] = (d * pa_ref[...] * ga * (1.0 - ga)).astype(BF16)
        dgb_ref[...] = (d * pb_ref[...] * gb * (1.0 - gb)).astype(BF16)

    blk = pl.BlockSpec((ROW_TILE, GATE_TILE), lambda i, j: (i, j))
    out = jax.ShapeDtypeStruct((T, Dm), BF16)
    return pl.pallas_call(
        body, name=name, grid=(T // ROW_TILE, Dm // GATE_TILE),
        in_specs=[blk, pl.BlockSpec((ROW_TILE, GATE_TILE), lambda i, j: (i, COL_GATE_A + j)),
                  pl.BlockSpec((ROW_TILE, GATE_TILE), lambda i, j: (i, COL_GATE_B + j)), blk, blk],
        out_specs=[blk, blk, blk, blk],
        out_shape=[out, out, out, out],
        compiler_params=_cparams(("parallel", "parallel")),
    )(dm, z, z, pa, pb)


def _dot_hi(a, b, dims):
    return lax.dot_general(a, b, (dims, ((), ())), precision=HIGHEST, preferred_element_type=F32)


NN = ((1,), (0,))
NT = ((1,), (1,))
TN = ((0,), (0,))


def _hg_gates(hq, hf, lb):
    sq = _sigmoid(hq)
    q = hq * sq * (HG_DK ** -0.5)
    f = _sigmoid(hf)
    g = lb + (1.0 - lb) * f
    return q, sq, f, g, jnp.log(g), 1.0 - g


def _tri(lower):
    r = lax.broadcasted_iota(jnp.int32, (CHUNK, CHUNK), 0)
    c = lax.broadcasted_iota(jnp.int32, (CHUNK, CHUNK), 1)
    return jnp.where((r >= c) if lower else (r <= c), 1.0, 0.0).astype(F32)


GROUP = 16
N_GROUPS = CHUNK // GROUP


def _dot_bf16(a, b, dims):
    return lax.dot_general(a.astype(BF16), b.astype(BF16), (dims, ((), ())), preferred_element_type=F32)


def _rows_iota():
    return lax.broadcasted_iota(jnp.int32, (CHUNK, HG_DK), 0)


def _by_query_group(q, kk, b, g):
    r0 = GROUP * g
    b0 = b[r0:r0 + 1]
    decay = jnp.exp(b[r0:r0 + GROUP] - b0)
    ks = jnp.where(_rows_iota() < r0, kk * jnp.exp(jnp.minimum(b0 - b, 0.0)), 0.0)
    return q[r0:r0 + GROUP] * decay, ks, decay


def _by_key_group(q, kk, b, j):
    r1 = GROUP * (j + 1)
    b1 = b[r1 - 1:r1]
    decay = jnp.exp(b1 - b[r1 - GROUP:r1])
    qs = jnp.where(_rows_iota() >= r1, q * jnp.exp(jnp.minimum(b - b1, 0.0)), 0.0)
    return qs, kk[r1 - GROUP:r1] * decay, decay


def _scores_between_groups(q, kk, b):
    blocks = [jnp.zeros((GROUP, CHUNK), F32)]
    for g in range(1, N_GROUPS):
        qs, ks, _ = _by_query_group(q, kk, b, g)
        blocks.append(_dot_bf16(qs, ks, NT))
    return jnp.concatenate(blocks, axis=0)


def _hgrn2_fwd(z, lb_logits, hg_norm_w, name, after=()):
    T = z.shape[0]
    n_chunks = T // CHUNK

    def body(hq_ref, hf_ref, hi_ref, hg_ref, lbl_ref, nw_ref, o_ref, ya_ref, sall_ref, st_ref):
        lbl = lbl_ref[...]
        lb = 1.0 / (1.0 + jnp.exp(lbl[1:2, :] - lbl[0:1, :]))
        st_ref[...] = jnp.zeros_like(st_ref)
        tri = _tri(True)
        row8 = lax.broadcasted_iota(jnp.int32, (8, HG_DK), 0)

        def chunk(c, carry):
            rows = pl.ds(pl.multiple_of(c * CHUNK, CHUNK), CHUNK)
            q, _, _, _, lg, kk = _hg_gates(hq_ref[rows, :], hf_ref[rows, :], lb)
            v = hi_ref[rows, :]
            b = _dot_hi(tri, lg, NN)
            st = st_ref[...]
            sall_ref[c] = st
            for grp in range(N_GROUPS):
                r0 = GROUP * grp
                for h8 in range(GROUP // 8):
                    n = 8 * (h8 + 1)
                    bs, ks, vs = b[r0:r0 + n], kk[r0:r0 + n], v[r0:r0 + n]
                    sidx = lax.broadcasted_iota(jnp.int32, (n, HG_DK), 0)
                    blk = jnp.zeros((8, HG_DK), F32)
                    for i in range(8):
                        t = r0 + 8 * h8 + i
                        e = jnp.where(sidx <= 8 * h8 + i, jnp.exp(b[t:t + 1] - bs), 0.0)
                        p = jnp.sum(e * ks * q[t:t + 1], axis=1, keepdims=True)
                        ot = jnp.sum(p * vs, axis=0, keepdims=True)
                        blk = blk + jnp.where(row8 == i, ot, 0.0)
                    o_ref[pl.ds(pl.multiple_of(c * CHUNK + r0 + 8 * h8, 8), 8), :] = blk
            o_ref[rows, :] += _dot_hi(q * jnp.exp(b), st, NT) + _dot_bf16(_scores_between_groups(q, kk, b), v, NN)
            bl = b[CHUNK - 1:CHUNK]
            ke = kk * jnp.exp(bl - b)
            st_ref[...] = st * jnp.exp(bl) + _dot_hi(v, ke, TN)
            return carry

        lax.fori_loop(0, n_chunks, chunk, 0, unroll=2)
        o = o_ref[...]
        r = lax.rsqrt(jnp.mean(o * o, axis=-1, keepdims=True) + EPS)
        hg = hg_ref[...]
        ya_ref[...] = (o * r * nw_ref[...] * (hg * _sigmoid(hg))).astype(BF16)

    def col(base):
        return pl.BlockSpec((T, HG_DK), lambda h: (0, base + h))

    return _call(
        body, (z, z, z, z, lb_logits, hg_norm_w), name=name, grid=(HG_HEADS,),
        in_specs=[col(COL_HQ), col(COL_HF), col(COL_HI), col(COL_HG),
                  pl.BlockSpec((2, HG_DK), lambda h: (0, h)), pl.BlockSpec((1, HG_DK), lambda h: (0, 0))],
        out_specs=[col(0), col(0), pl.BlockSpec((None, n_chunks, HG_DK, HG_DK), lambda h: (h, 0, 0, 0))],
        out_shape=[jax.ShapeDtypeStruct((T, HG_WIDTH), F32), jax.ShapeDtypeStruct((T, HG_WIDTH), BF16),
                   jax.ShapeDtypeStruct((HG_HEADS, n_chunks, HG_DK, HG_DK), F32)],
        scratch_shapes=[pltpu.VMEM((HG_DK, HG_DK), F32)],
        sem=("parallel",), after=after)


def _hgrn2_bwd(z, lb_logits, hg_norm_w, o_raw, s_all, dya, name, after=()):
    T = z.shape[0]
    n_chunks = T // CHUNK

    def body(hq_ref, hf_ref, hi_ref, hg_ref, lbl_ref, nw_ref, o_ref, sall_ref, dya_ref,
             dhq_ref, dhf_ref, dhi_ref, dhg_ref, dlbl_ref, dnw_ref,
             do_ref, dst_ref, dlb_ref, *per_chunk):
        h = pl.program_id(0)
        lbl = lbl_ref[...]
        lb = 1.0 / (1.0 + jnp.exp(lbl[1:2, :] - lbl[0:1, :]))

        o = o_ref[...]
        r = lax.rsqrt(jnp.mean(o * o, axis=-1, keepdims=True) + EPS)
        oh = o * r
        nw = nw_ref[...]
        hg = hg_ref[...]
        sg = _sigmoid(hg)
        dy = dya_ref[...]
        d_on = dy * (hg * sg)
        dhg_ref[...] = (dy * (oh * nw) * (sg * (1.0 + hg * (1.0 - sg)))).astype(BF16)
        dnw = jnp.sum(d_on * oh, axis=0, keepdims=True)
        gy = d_on * nw
        do_ref[...] = r * (gy - oh * jnp.mean(gy * oh, axis=-1, keepdims=True))

        @pl.when(h == 0)
        def _():
            dnw_ref[...] = jnp.zeros_like(dnw_ref)

        dnw_ref[...] += jnp.broadcast_to(dnw, dnw_ref.shape)

        dst_ref[...] = jnp.zeros_like(dst_ref)
        dlb_ref[...] = jnp.zeros_like(dlb_ref)
        tri = _tri(True)
        tri_t = _tri(False)
        row8 = lax.broadcasted_iota(jnp.int32, (8, HG_DK), 0)
        row_group = lax.broadcasted_iota(jnp.int32, (CHUNK, CHUNK), 0) // GROUP
        col_group = lax.broadcasted_iota(jnp.int32, (CHUNK, CHUNK), 1) // GROUP
        earlier_group = col_group < row_group
        later_group = col_group > row_group

        def chunk(c, dq_ref, dk_ref, dv_ref):
            rows = pl.ds(pl.multiple_of(c * CHUNK, CHUNK), CHUNK)
            hq = hq_ref[rows, :]
            q, sq, f, g, lg, kk = _hg_gates(hq, hf_ref[rows, :], lb)
            v = hi_ref[rows, :]
            do = do_ref[rows, :]
            b = _dot_hi(tri, lg, NN)
            eb = jnp.exp(b)
            bl = b[CHUNK - 1:CHUNK]
            ebl = jnp.exp(bl)
            ekb = jnp.exp(bl - b)
            qe = q * eb
            ke = kk * ekb
            st = sall_ref[c]
            dst = dst_ref[...]
            dqe = _dot_hi(do, st, NN)
            dke = _dot_hi(v, dst, NN)
            dv_inter = _dot_hi(ke, dst, NT)
            d_ebl = jnp.sum(st * dst, axis=0, keepdims=True)
            dst_ref[...] = dst * ebl + _dot_hi(do, qe, TN)

            dk_ref[...] = jnp.zeros_like(dk_ref)
            dv_ref[...] = jnp.zeros_like(dv_ref)
            for grp in range(N_GROUPS):
                r0 = GROUP * grp
                for h8 in range(GROUP // 8):
                    n = 8 * (h8 + 1)
                    bs, ks, vs = b[r0:r0 + n], kk[r0:r0 + n], v[r0:r0 + n]
                    sidx = lax.broadcasted_iota(jnp.int32, (n, HG_DK), 0)
                    blk = jnp.zeros((8, HG_DK), F32)
                    for i in range(8):
                        t = r0 + 8 * h8 + i
                        qt = q[t:t + 1]
                        dot_ = do[t:t + 1]
                        e = jnp.where(sidx <= 8 * h8 + i, jnp.exp(b[t:t + 1] - bs), 0.0)
                        w = e * ks
                        p = jnp.sum(w * qt, axis=1, keepdims=True)
                        dsc = jnp.sum(vs * dot_, axis=1, keepdims=True)
                        dqt = jnp.sum(dsc * w, axis=0, keepdims=True)
                        blk = blk + jnp.where(row8 == i, dqt, 0.0)
                        dk_ref[r0:r0 + n, :] += dsc * e * qt
                        dv_ref[r0:r0 + n, :] += p * dot_
                    dq_ref[r0 + 8 * h8:r0 + n, :] = blk
            ds_far = jnp.where(earlier_group, _dot_bf16(do, v, NT), 0.0)
            ds_far_t = jnp.where(later_group, _dot_bf16(v, do, NT), 0.0)
            dq_far, dk_far = [jnp.zeros((GROUP, HG_DK), F32)], []
            for grp in range(1, N_GROUPS):
                r0 = GROUP * grp
                _, ks, decay = _by_query_group(q, kk, b, grp)
                dq_far.append(decay * _dot_hi(ds_far[r0:r0 + GROUP], ks, NN))
                qs, _, decay = _by_key_group(q, kk, b, grp - 1)
                dk_far.append(decay * _dot_hi(ds_far_t[r0 - GROUP:r0], qs, NN))
            dk_far.append(jnp.zeros((GROUP, HG_DK), F32))
            dv_far = _dot_bf16(_scores_between_groups(q, kk, b), do, TN)
            dq_i = dq_ref[...] + jnp.concatenate(dq_far, axis=0)
            dk_i = dk_ref[...] + jnp.concatenate(dk_far, axis=0)
            dke_ke = dke * ke
            db = q * dq_i - kk * dk_i + dqe * qe - dke_ke
            db_last = jnp.sum(dke_ke, axis=0, keepdims=True) + d_ebl * ebl
            dlg = _dot_hi(tri_t, db, NN) + db_last
            dq = dq_i + dqe * eb
            dkk = dk_i + dke * ekb
            dg = dlg / g - dkk
            dhq_ref[rows, :] = (dq * (HG_DK ** -0.5) * (sq * (1.0 + hq * (1.0 - sq)))).astype(BF16)
            dhf_ref[rows, :] = (dg * (1.0 - lb) * f * (1.0 - f)).astype(BF16)
            dhi_ref[rows, :] = (dv_ref[...] + dv_far + dv_inter).astype(BF16)
            dlb_ref[...] += jnp.sum(dg * (1.0 - f), axis=0, keepdims=True)

        def two_chunks(i, carry):
            chunk(n_chunks - 1 - 2 * i, *per_chunk[:3])
            chunk(n_chunks - 2 - 2 * i, *per_chunk[3:])
            return carry

        lax.fori_loop(0, n_chunks // 2, two_chunks, 0)
        dl0 = dlb_ref[...] * lb * (1.0 - lb)
        dlbl_ref[0:1, :] = dl0
        dlbl_ref[1:2, :] = -dl0

    def col(base):
        return pl.BlockSpec((T, HG_DK), lambda h: (0, base + h))

    outb = jax.ShapeDtypeStruct((T, HG_WIDTH), BF16)
    return _call(
        body, (z, z, z, z, lb_logits, hg_norm_w, o_raw, s_all, dya), name=name, grid=(HG_HEADS,),
        in_specs=[col(COL_HQ), col(COL_HF), col(COL_HI), col(COL_HG),
                  pl.BlockSpec((2, HG_DK), lambda h: (0, h)), pl.BlockSpec((1, HG_DK), lambda h: (0, 0)),
                  col(0), pl.BlockSpec((None, n_chunks, HG_DK, HG_DK), lambda h: (h, 0, 0, 0)), col(0)],
        out_specs=[col(0), col(0), col(0), col(0), pl.BlockSpec((2, HG_DK), lambda h: (0, h)),
                   pl.BlockSpec((8, HG_DK), lambda h: (0, 0))],
        out_shape=[outb, outb, outb, outb, jax.ShapeDtypeStruct((2, HG_WIDTH), F32),
                   jax.ShapeDtypeStruct((8, HG_DK), F32)],
        scratch_shapes=[pltpu.VMEM((T, HG_DK), F32), pltpu.VMEM((HG_DK, HG_DK), F32), pltpu.VMEM((1, HG_DK), F32)]
        + [pltpu.VMEM((CHUNK, HG_DK), F32)] * 6,
        sem=("arbitrary",), after=after)


CONST_KEYS = PAD - REL_CLIP
VAR_KEYS = BAND - CONST_KEYS
REL_LO = 128
REL_SPAN = N_REL_PAD - REL_LO


def _rel_onehot(t):
    r = lax.broadcasted_iota(jnp.int32, (REL_SPAN, VAR_KEYS), 0)
    j = lax.broadcasted_iota(jnp.int32, (REL_SPAN, VAR_KEYS), 1)
    idx = jnp.clip(t + PAD - CONST_KEYS - j, -REL_CLIP, REL_CLIP) + REL_CLIP - REL_LO
    return jnp.where(r == idx, 1.0, 0.0).astype(BF16)


def _split3(x):
    hi = x.astype(BF16)
    r1 = x - hi.astype(F32)
    mid = r1.astype(BF16)
    return hi, mid, (r1 - mid.astype(F32)).astype(BF16)


def _bias_expand(rel, name):
    def body(rel_ref, out_ref):
        tab = rel_ref[...]
        onehot = _rel_onehot(pl.program_id(0))
        out_ref[:, 0:CONST_KEYS] = jnp.broadcast_to(tab[:, 2 * REL_CLIP:2 * REL_CLIP + 1], (AT_HEADS, CONST_KEYS))
        out_ref[:, CONST_KEYS:BAND] = sum(
            jnp.dot(piece, onehot, preferred_element_type=F32) for piece in _split3(tab[:, REL_LO:N_REL_PAD]))

    return pl.pallas_call(
        body, name=name, grid=(CHUNK,),
        in_specs=[pl.BlockSpec((AT_HEADS, N_REL_PAD), lambda t: (0, 0))],
        out_specs=pl.BlockSpec((None, AT_HEADS, BAND), lambda t: (t, 0, 0)),
        out_shape=jax.ShapeDtypeStruct((CHUNK, AT_HEADS, BAND), F32),
        compiler_params=_cparams(("parallel",)),
    )(rel)


def _bias_reduce(dbias_t, name, after=()):
    def body(db_ref, out_ref):
        t = pl.program_id(0)

        @pl.when(t == 0)
        def _():
            out_ref[...] = jnp.zeros_like(out_ref)

        db = db_ref[...]
        onehot = _rel_onehot(t)
        acc = sum(lax.dot_general(piece, onehot, (NT, ((), ())), preferred_element_type=F32)
                  for piece in _split3(db[:, CONST_KEYS:BAND]))
        lane = lax.broadcasted_iota(jnp.int32, (AT_HEADS, REL_SPAN), 1)
        last = jnp.sum(db[:, 0:CONST_KEYS], axis=1, keepdims=True)
        out_ref[:, REL_LO:N_REL_PAD] += acc + jnp.where(lane == 2 * REL_CLIP - REL_LO, last, 0.0)

    return _call(
        body, (dbias_t,), name=name, grid=(CHUNK,),
        in_specs=[pl.BlockSpec((None, AT_HEADS, BAND), lambda t: (t, 0, 0))],
        out_specs=[pl.BlockSpec((AT_HEADS, N_REL_PAD), lambda t: (0, 0))],
        out_shape=[jax.ShapeDtypeStruct((AT_HEADS, N_REL_PAD), F32)],
        sem=("arbitrary",), after=after)[0]


def _pair_lanes():
    return lax.broadcasted_iota(jnp.int32, (CHUNK, 2 * AT_DH), 1) < AT_DH


def _block_diag(a):
    first = _pair_lanes()
    return jnp.concatenate([jnp.where(first, a, 0.0), jnp.where(first, 0.0, a)], axis=0).astype(BF16)


def _diag_blocks(a):
    return jnp.where(_pair_lanes(), a[:CHUNK], a[CHUNK:])


def _band_probs_t(kb, qbd, bias_t, c):
    s = lax.dot_general(kb, qbd, (NT, ((), ())), preferred_element_type=F32) * (AT_DH ** -0.5) + bias_t
    j = lax.broadcasted_iota(jnp.int32, (BAND, 2 * AT_DH), 0)
    s = jnp.where(j + c * CHUNK >= PAD, s, -jnp.inf)
    p = jnp.exp(s - jnp.max(s, axis=0, keepdims=True))
    return p / jnp.sum(p, axis=0, keepdims=True)


def _fill_padded(dst_ref, src_ref, T):
    dst_ref[0:PAD, :] = jnp.zeros((PAD, 2 * AT_DH), BF16)
    dst_ref[PAD:PAD + T, :] = src_ref[...].astype(BF16)


def _attn_fwd(z, bias_t, name, after=()):
    T = z.shape[0]
    n_chunks = T // CHUNK

    def body(q_ref, k_ref, v_ref, bias_ref, y_ref, kp_ref, vp_ref):
        _fill_padded(kp_ref, k_ref, T)
        _fill_padded(vp_ref, v_ref, T)

        def chunk(c, carry):
            rows = pl.ds(pl.multiple_of(c * CHUNK, CHUNK), CHUNK)
            band = pl.ds(pl.multiple_of(c * CHUNK, CHUNK), BAND)
            p = _band_probs_t(kp_ref[band, :], _block_diag(q_ref[rows, :]), bias_ref[...], c)
            o2 = lax.dot_general(p.astype(BF16), vp_ref[band, :], (TN, ((), ())), preferred_element_type=F32)
            y_ref[rows, :] = _diag_blocks(o2).astype(BF16)
            return carry

        lax.fori_loop(0, n_chunks, chunk, 0, unroll=2)

    def col(base):
        return pl.BlockSpec((T, 128), lambda h: (0, base + h))

    return _call(
        body, (z, z, z, bias_t), name=name, grid=(AT_HEADS // 2,),
        in_specs=[col(COL_AQ), col(COL_AK), col(COL_AV), pl.BlockSpec((None, BAND, 128), lambda h: (h, 0, 0))],
        out_specs=[col(0)],
        out_shape=[jax.ShapeDtypeStruct((T, AT_WIDTH), BF16)],
        scratch_shapes=[pltpu.VMEM((PAD + T, 128), BF16), pltpu.VMEM((PAD + T, 128), BF16)],
        sem=("parallel",), after=after)


def _attn_bwd(z, bias_t, dyb, name, after=()):
    T = z.shape[0]
    n_chunks = T // CHUNK

    def body(q_ref, k_ref, v_ref, bias_ref, dy_ref, dq_ref, dk_ref, dv_ref, dbias_ref, *scratch):
        dbias_ref[...] = jnp.zeros_like(dbias_ref)
        for pr in range(2):
            kp_ref, vp_ref, dkp_ref, dvp_ref = scratch[4 * pr:4 * pr + 4]
            lanes = slice(128 * pr, 128 * (pr + 1))
            kp_ref[0:PAD, :] = jnp.zeros((PAD, 128), BF16)
            vp_ref[0:PAD, :] = jnp.zeros((PAD, 128), BF16)
            kp_ref[PAD:PAD + T, :] = k_ref[:, lanes].astype(BF16)
            vp_ref[PAD:PAD + T, :] = v_ref[:, lanes].astype(BF16)
            dkp_ref[...] = jnp.zeros_like(dkp_ref)
            dvp_ref[...] = jnp.zeros_like(dvp_ref)

        def chunk(c, carry):
            rows = pl.ds(pl.multiple_of(c * CHUNK, CHUNK), CHUNK)
            band = pl.ds(pl.multiple_of(c * CHUNK, CHUNK), BAND)
            for pr in range(2):
                kp_ref, vp_ref, dkp_ref, dvp_ref = scratch[4 * pr:4 * pr + 4]
                lanes = slice(128 * pr, 128 * (pr + 1))
                qbd = _block_diag(q_ref[rows, lanes])
                dobd = _block_diag(dy_ref[rows, lanes])
                kb = kp_ref[band, :]
                vb = vp_ref[band, :]
                p = _band_probs_t(kb, qbd, bias_ref[pr], c)
                dp = lax.dot_general(vb, dobd, (NT, ((), ())), preferred_element_type=F32)
                ds = p * (dp - jnp.sum(dp * p, axis=0, keepdims=True))
                dbias_ref[pr] += ds
                dsb = ds.astype(BF16)
                dq2 = lax.dot_general(dsb, kb, (TN, ((), ())), preferred_element_type=F32)
                dq_ref[rows, lanes] = (_diag_blocks(dq2) * (AT_DH ** -0.5)).astype(BF16)
                dkp_ref[band, :] += jnp.dot(dsb, qbd, preferred_element_type=F32) * (AT_DH ** -0.5)
                dvp_ref[band, :] += jnp.dot(p.astype(BF16), dobd, preferred_element_type=F32)
            return carry

        lax.fori_loop(0, n_chunks, chunk, 0)
        for pr in range(2):
            lanes = slice(128 * pr, 128 * (pr + 1))
            dk_ref[:, lanes] = scratch[4 * pr + 2][PAD:PAD + T, :].astype(BF16)
            dv_ref[:, lanes] = scratch[4 * pr + 3][PAD:PAD + T, :].astype(BF16)

    def col(base):
        return pl.BlockSpec((T, 256), lambda h: (0, base // 2 + h))

    pairs = pl.BlockSpec((2, BAND, 128), lambda h: (h, 0, 0))
    outb = jax.ShapeDtypeStruct((T, AT_WIDTH), BF16)
    return _call(
        body, (z, z, z, bias_t, dyb), name=name, grid=(AT_HEADS // 4,),
        in_specs=[col(COL_AQ), col(COL_AK), col(COL_AV), pairs, col(0)],
        out_specs=[col(0), col(0), col(0), pairs],
        out_shape=[outb, outb, outb, jax.ShapeDtypeStruct((AT_HEADS // 2, BAND, 128), F32)],
        scratch_shapes=[pltpu.VMEM((PAD + T, 128), BF16), pltpu.VMEM((PAD + T, 128), BF16),
                        pltpu.VMEM((PAD + T, 128), F32), pltpu.VMEM((PAD + T, 128), F32)] * 2,
        sem=("parallel",), after=after)


def _local_step(x, target, lb_logits, hg_norm_w, rel_bias, norm_mix_w, norm_mlp_w, norm_final_w,
                w_in, rest, exchanges=None):
    ex = exchanges
    rel = jnp.pad(rel_bias, ((0, 0), (0, N_REL_PAD - N_REL)))

    u = _rms_fwd(x, norm_mix_w, "rms_mix_fwd")
    if ex:
        z, w_in = _mm_gathered(u, w_in, ex.order, "mm_in_fwd")
        gather = _Gather(rest, [z], "ag")
        tok = [gather.token]
    else:
        z = _mm_nn(u, w_in, F32, "mm_in_fwd")
        w_a, w_b, w_out, w_up, w_down = rest
        tok = []
    o_raw, y_a, s_all = _hgrn2_fwd(z, lb_logits, hg_norm_w, "hgrn2_fwd", after=tok)
    if ex:
        tok = [gather.pass_on([0, 1, 2], [o_raw], "abo")]
    bias_rows = _bias_expand(rel, "bias_expand")
    bias_t = jnp.transpose(bias_rows.reshape(CHUNK, AT_HEADS // 2, 2, BAND), (1, 3, 2, 0)).reshape(
        AT_HEADS // 2, BAND, 2 * CHUNK)
    y_b, = _attn_fwd(z, bias_t, "attn_fwd", after=tok)
    if ex:
        tok = [gather.pass_on([3], [y_b], "up")]
        w_a, w_b, w_out = gather.finish([0, 1, 2], tok, "abo")
    pa = _mm_nn(y_a, w_a, F32, "mm_a_fwd")
    pb = _mm_nn(y_b, w_b, F32, "mm_b_fwd")
    merged = _merge_fwd(z, pa, pb, "merge_fwd")
    w_out1 = w_out.reshape(1, D_MODEL, D_MODEL)
    mix = _mm_nn(merged, w_out1, F32, "mm_out_fwd")
    h1, u2 = _resid_rms_fwd(x, mix, norm_mlp_w, "rms_mlp_fwd")
    if ex:
        tok = [gather.pass_on([4], [u2], "down")]
        w_up, = gather.finish([3], tok, "up")
    a, r = _mm_nn(u2, w_up, F32, "mm_up_fwd", squared_relu=True)
    if ex:
        w_down, = gather.finish([4], [r], "down")
    w_down1 = w_down.reshape(1, D_FF, D_MODEL)
    mlp = _mm_nn(r, w_down1, F32, "mm_down_fwd")
    loss, dh2, dh2b, g_nf = _loss_head(h1, mlp, norm_final_w, target, "loss_head")

    def reduce_scatter(grads, name):
        rs = _ReduceScatter(grads, ex.parity, name) if ex else None
        return rs, ([rs.token] if ex else [])

    g_down = _mm_tn(r, dh2b, 1, BF16, "mm_down_wgrad").reshape(N_DEV, D_FF // N_DEV, D_MODEL)
    rs_down, tok = reduce_scatter([g_down], "rs_down")
    da = _mm_nt(dh2b, w_down1, BF16, "mm_down_dgrad", after=tok, relu_of=a)
    tok = [rs_down.pair_sums([da])] if ex else []
    g_up = _mm_tn(u2, da, N_DEV, BF16, "mm_up_wgrad", after=tok)
    rs_up, tok = reduce_scatter([g_up], "rs_up")
    du2 = _mm_nt(da, w_up, F32, "mm_up_dgrad", after=tok)
    tok = [rs_up.pair_sums([du2])] if ex else []
    dh1, dh1b, g_nmlp = _rms_bwd(du2, h1, norm_mlp_w, dh2, "rms_mlp_bwd", after=tok)

    g_out = _mm_tn(merged, dh1b, 1, BF16, "mm_out_wgrad").reshape(N_DEV, D_MODEL // N_DEV, D_MODEL)
    dmerged = _mm_nt(dh1b, w_out1, F32, "mm_out_dgrad")
    dpa, dpb, dga, dgb = _merge_bwd(dmerged, z, pa, pb, "merge_bwd")
    g_a = _mm_tn(y_a, dpa, N_DEV, BF16, "mm_a_wgrad")
    g_b = _mm_tn(y_b, dpb, N_DEV, BF16, "mm_b_wgrad")
    rs_mix, tok = reduce_scatter([g_a, g_b, g_out], "rs_mix")
    dya = _mm_nt(dpa, w_a, F32, "mm_a_dgrad", after=tok)
    dyb = _mm_nt(dpb, w_b, F32, "mm_b_dgrad", after=tok)
    tok = [rs_mix.pair_sums([dya, dyb])] if ex else []
    daq, dak, dav, dbias_t = _attn_bwd(z, bias_t, dyb, "attn_bwd", after=tok)
    dhq, dhf, dhi, dhg, g_lbl, g_hgw = _hgrn2_bwd(z, lb_logits, hg_norm_w, o_raw, s_all, dya, "hgrn2_bwd",
                                                  after=tok)
    dbias_rows = jnp.transpose(dbias_t.reshape(AT_HEADS // 2, BAND, 2, CHUNK), (3, 0, 2, 1)).reshape(
        CHUNK, AT_HEADS, BAND)
    dz =jnp.concatenate([dhq, dhf, dhi, dhg, daq, dak, dav, dga, dgb], axis=1)
    g_in = _mm_tn(u, dz, N_DEV, BF16, "mm_in_wgrad")
    rs_in, _ = reduce_scatter([g_in], "rs_in")
    tok = [rs_in.pair_sums([])] if ex else []
    du = _mm_nt(dz, w_in, F32, "mm_in_dgrad", after=tok)
    grad_x, _, g_nmix = _rms_bwd(du, x, norm_mix_w, dh1, "rms_mix_bwd")
    g_rel = _bias_reduce(dbias_rows, "bias_reduce", after=tok)[:, :N_REL]

    small = dict(lb_logits=g_lbl, hg_norm_w=g_hgw[0:1], rel_bias=g_rel, norm_mix_w=g_nmix, norm_mlp_w=g_nmlp,
                 norm_final_w=g_nf)
    grads = [rs_in, rs_mix, rs_up, rs_down] if ex else [g_in, g_a, g_b, g_out, g_up, g_down]
    return loss, grad_x, grads, small


def _gather_exchange(shards, mid_step=None):
    n = len(shards)

    def parts(ins, outs, sems):
        send_sems, recv_sems, local_sems = sems
        x, y, c = _position()
        chips = [(1 - x, y), (x, 1 - y), (1 - x, 1 - y)]

        def copy(w, k, block, to, src=None):
            dst = outs[w].at[4 * block[0] + 2 * block[1] + block[2]]
            return pltpu.make_async_remote_copy(
                src_ref=dst if src is None else src, dst_ref=dst,
                send_sem=send_sems.at[w, k], recv_sem=recv_sems.at[w, k], device_id=to, device_id_type=MESH)

        def local(w):
            return pltpu.make_async_copy(ins[w], outs[w].at[4 * x + 2 * y + c], local_sems.at[w])

        return (x, y, c), (x, y, 1 - c), chips, copy, local

    def start(ins, outs, sems):
        me, sibling, chips, copy, local = parts(ins, outs, sems)
        for w in range(n):
            local(w).start()
        for w in range(n):
            copy(w, 0, me, sibling, src=ins[w]).start()
            for j, chip in enumerate(chips):
                copy(w, 1 + j, me, (*chip, me[2]), src=ins[w]).start()

    def mid(ins, outs, sems):
        me, sibling, chips, copy, _ = parts(ins, outs, sems)
        for w in range(n):
            for j, chip in enumerate(chips):
                copy(w, 1 + j, (*chip, me[2]), me).wait_recv()
                copy(w, 4 + j, (*chip, me[2]), sibling).start()

    def end(ins, outs, sems):
        me, sibling, chips, copy, local = parts(ins, outs, sems)
        for w in range(n):
            copy(w, 0, sibling, me).wait_recv()
            for j, chip in enumerate(chips):
                copy(w, 4 + j, (*chip, sibling[2]), me).wait_recv()
        for w in range(n):
            for k in range(7):
                copy(w, k, me, sibling).wait_send()
            local(w).wait()

    return _Exchange(
        shards, [jax.ShapeDtypeStruct((N_DEV,) + s.shape, s.dtype) for s in shards],
        [pltpu.SemaphoreType.DMA((n, 7)), pltpu.SemaphoreType.DMA((n, 7)), pltpu.SemaphoreType.DMA((n,))],
        start, end, mid, mid_step)


def _mm_gathered(u, shard, order, name):
    T, K = u.shape
    _, Nb = shard.shape

    def body(order_ref, u_ref, shard_ref, z_ref, full_ref, wbuf, load_sem, send_sems, recv_sems, local_sem):
        s = pl.program_id(0)
        x, y, c = _position()
        me, sibling = (x, y, c), (x, y, 1 - c)
        chips = [(1 - x, y), (x, 1 - y), (1 - x, 1 - y)]

        def copy(k, block, to, src=None):
            dst = full_ref.at[4 * block[0] + 2 * block[1] + block[2]]
            return pltpu.make_async_remote_copy(
                src_ref=dst if src is None else src, dst_ref=dst,
                send_sem=send_sems.at[k], recv_sem=recv_sems.at[k], device_id=to, device_id_type=MESH)

        @pl.when(s == 0)
        def _():
            local = pltpu.make_async_copy(shard_ref, full_ref.at[4 * x + 2 * y + c], local_sem)
            local.start()
            copy(0, me, sibling, src=shard_ref).start()
            for j, chip in enumerate(chips):
                copy(1 + j, me, (*chip, c), src=shard_ref).start()
            local.wait()

        @pl.when(s == 1)
        def _():
            copy(0, sibling, me).wait_recv()

        for j, chip in enumerate(chips):
            direct, passed = ((2, 4), (3, 5), (6, 7))[j]

            @pl.when(s == direct)
            def _(j=j, chip=chip):
                copy(1 + j, (*chip, c), me).wait_recv()
                copy(4 + j, (*chip, c), sibling).start()

            @pl.when(s == passed)
            def _(j=j, chip=chip):
                copy(4 + j, (*chip, 1 - c), me).wait_recv()

        load = pltpu.make_async_copy(full_ref.at[order_ref[s]], wbuf, load_sem)
        load.start()
        load.wait()
        z_ref[...] = jnp.dot(u_ref[...], wbuf[...], preferred_element_type=F32)

        @pl.when(s == N_DEV - 1)
        def _():
            for k in range(7):
                copy(k, me, sibling).wait_send()

    return pl.pallas_call(
        body, name=name,
        grid_spec=pltpu.PrefetchScalarGridSpec(
            num_scalar_prefetch=1, grid=(N_DEV,),
            in_specs=[pl.BlockSpec((T, K), lambda s, order: (0, 0)), ANY],
            out_specs=[pl.BlockSpec((T, Nb), lambda s, order: (0, order[s])), ANY],
            scratch_shapes=[pltpu.VMEM((K, Nb), BF16), pltpu.SemaphoreType.DMA,
                            pltpu.SemaphoreType.DMA((7,)), pltpu.SemaphoreType.DMA((7,)), pltpu.SemaphoreType.DMA]),
        out_shape=[jax.ShapeDtypeStruct((T, N_DEV * Nb), F32), jax.ShapeDtypeStruct((N_DEV, K, Nb), BF16)],
        compiler_params=_cparams(("arbitrary",)),
    )(order, u, shard)


def _gather_order():
    x, y, c = _position()
    chips = [(1 - x, y), (x, 1 - y), (1 - x, 1 - y)]
    ids = [4 * x + 2 * y + c, 4 * x + 2 * y + (1 - c)]
    ids += [4 * cx + 2 * cy + c for cx, cy in chips[:2]] + [4 * cx + 2 * cy + (1 - c) for cx, cy in chips[:2]]
    ids += [4 * chips[2][0] + 2 * chips[2][1] + c, 4 * chips[2][0] + 2 * chips[2][1] + (1 - c)]
    return jnp.stack(ids).astype(jnp.int32)


def _run_exchange(comm, name):
    n_i, n_o = len(comm.arrays), len(comm.out_shape)

    def body(*refs):
        ins, outs, sems = refs[:n_i], refs[n_i:n_i + n_o], refs[n_i + n_o:]
        comm.start(ins, outs, sems)
        if comm.mid is not None:
            comm.mid(ins, outs, sems)
        comm.end(ins, outs, sems)

    return pl.pallas_call(
        body, name=name, in_specs=[ANY] * n_i, out_specs=[ANY] * n_o, out_shape=comm.out_shape,
        scratch_shapes=comm.scratch)(*comm.arrays)


def _exchange_sibling(grads, name):
    n = len(grads)

    def body(*refs):
        ins, outs = refs[:n], refs[n:2 * n]
        send_sems, recv_sems = refs[2 * n:]
        x, y, c = _position()
        copies = []
        for w in range(n):
            for s in range(N_CHIP):
                cp = pltpu.make_async_remote_copy(
                    src_ref=ins[w].at[2 * s + (1 - c)], dst_ref=outs[w].at[s],
                    send_sem=send_sems.at[w, s], recv_sem=recv_sems.at[w, s],
                    device_id=(x, y, 1 - c), device_id_type=MESH)
                cp.start()
                copies.append(cp)
        for cp in copies:
            cp.wait()

    return pl.pallas_call(
        body, name=name,
        in_specs=[ANY] * n, out_specs=[ANY] * n,
        out_shape=[jax.ShapeDtypeStruct((N_CHIP,) + g.shape[1:], g.dtype) for g in grads],
        scratch_shapes=[pltpu.SemaphoreType.DMA((n, N_CHIP)), pltpu.SemaphoreType.DMA((n, N_CHIP))],
    )(*grads)


def _pair_sum(g, land, parity, name):
    _, R, C = g.shape
    tr = _pick(R, (512, 256))

    def body(par_ref, g_ref, l_ref, o_ref):
        o_ref[...] = (g_ref[...].astype(F32) + l_ref[...].astype(F32)).astype(BF16)

    return pl.pallas_call(
        body, name=name,
        grid_spec=pltpu.PrefetchScalarGridSpec(
            num_scalar_prefetch=1, grid=(N_CHIP, R // tr),
            in_specs=[pl.BlockSpec((None, tr, C), lambda s, i, par: (2 * s + par[0], i, 0)),
                      pl.BlockSpec((None, tr, C), lambda s, i, par: (s, i, 0))],
            out_specs=pl.BlockSpec((None, tr, C), lambda s, i, par: (s, i, 0))),
        out_shape=jax.ShapeDtypeStruct((N_CHIP, R, C), BF16),
        compiler_params=_cparams(("parallel", "parallel")),
    )(parity, g, land)


def _scatter_exchange(partials):
    n = len(partials)

    def copies(ins, outs, sems):
        send_sems, recv_sems, local_sems = sems
        x, y, c = _position()
        chips = [(1 - x, y), (x, 1 - y), (1 - x, 1 - y)]
        my_slot = 2 * x + y
        local = [pltpu.make_async_copy(ins[w].at[my_slot], outs[w].at[my_slot], local_sems.at[w]) for w in range(n)]
        remote = [pltpu.make_async_remote_copy(
            src_ref=ins[w].at[2 * chip[0] + chip[1]], dst_ref=outs[w].at[my_slot],
            send_sem=send_sems.at[w, j], recv_sem=recv_sems.at[w, j], device_id=(*chip, c), device_id_type=MESH)
            for w in range(n) for j, chip in enumerate(chips)]
        return local, remote

    def start(ins, outs, sems):
        local, remote = copies(ins, outs, sems)
        for cp in local + remote:
            cp.start()

    def end(ins, outs, sems):
        local, remote = copies(ins, outs, sems)
        for cp in remote + local:
            cp.wait()

    return _Exchange(
        partials, [jax.ShapeDtypeStruct(p.shape, p.dtype) for p in partials],
        [pltpu.SemaphoreType.DMA((n, 3)), pltpu.SemaphoreType.DMA((n, 3)), pltpu.SemaphoreType.DMA((n,))],
        start, end)


HBM = pl.BlockSpec(memory_space=pltpu.HBM)
SEM = pl.BlockSpec(memory_space=pltpu.SEMAPHORE)
DATAFLOW = pltpu.SideEffectType.DATAFLOW_SIDE_EFFECTING


def _scatter_copies(ins, lands, send_sems, recv_sems):
    x, y, c = _position()
    chips = [(1 - x, y), (x, 1 - y), (1 - x, 1 - y)]
    return [pltpu.make_async_remote_copy(
        src_ref=ins[w].at[2 * chip[0] + chip[1]], dst_ref=lands[w].at[2 * x + y],
        send_sem=send_sems[3 * w + j], recv_sem=recv_sems[3 * w + j], device_id=(*chip, c), device_id_type=MESH)
        for w in range(len(ins)) for j, chip in enumerate(chips)]


def _scatter_start(partials, name):
    n = len(partials)

    def body(*refs):
        ins, lands = refs[:n], refs[n:2 * n]
        sems = refs[4 * n:10 * n]
        for cp in _scatter_copies(ins, lands, sems[:3 * n], sems[3 * n:]):
            cp.start()
        refs[-1][...] = jnp.zeros_like(refs[-1])

    def in_hbm(a):
        return pltpu.with_memory_space_constraint(a, pltpu.HBM)

    bufs = tuple(pltpu.HBM(p.shape, p.dtype) for p in partials)
    outs = pl.pallas_call(
        body, name=name,
        out_shape=bufs + bufs + (pltpu.SemaphoreType.DMA(()),) * (6 * n) + (jax.ShapeDtypeStruct((8, 128), F32),),
        in_specs=[HBM] * (2 * n),
        out_specs=(HBM,) * (2 * n) + (SEM,) * (6 * n) + (pl.BlockSpec(memory_space=pltpu.VMEM),),
        input_output_aliases={i: i for i in range(2 * n)},
        compiler_params=pltpu.CompilerParams(has_side_effects=DATAFLOW),
    )(*[in_hbm(p) for p in partials], *[in_hbm(lax.empty(p.shape, p.dtype)) for p in partials])
    return list(outs[:-1]), outs[-1]


def _scatter_wait(handle, after, name):
    n = len(handle) // 8
    bufs, sems = handle[:2 * n], handle[2 * n:]

    def body(*refs):
        ins, lands = refs[:n], refs[n:2 * n]
        sems = refs[2 * n:8 * n]
        for cp in _scatter_copies(ins, lands, sems[:3 * n], sems[3 * n:]):
            cp.wait_send()
            cp.wait_recv()

    outs = pl.pallas_call(
        body, name=name,
        out_shape=tuple(pltpu.HBM(b.shape, b.dtype) for b in bufs),
        in_specs=[HBM] * (2 * n) + [SEM] * (6 * n) + [ANY] * len(after), out_specs=(HBM,) * (2 * n),
        input_output_aliases={i: i for i in range(2 * n)},
        compiler_params=pltpu.CompilerParams(has_side_effects=DATAFLOW),
    )(*bufs, *sems, *after)
    return list(outs[:n]), list(outs[n:])


def _split_call(name, bufs, waits=(), starts=None, after=()):
    nb = len(bufs)
    n_new = starts[1] if starts else 0
    wait_sems = [s for w in waits for s in (*w[1], *w[2])]

    def body(*refs):
        b, pos = refs[:nb], nb
        for plan, ss, _, send_idx, recv_idx in waits:
            k = len(ss)
            copies = plan(b, refs[pos:pos + k], refs[pos + k:pos + 2 * k])
            pos += 2 * k
            for i in recv_idx:
                copies[i].wait_recv()
            for i in send_idx:
                copies[i].wait_send()
        outs = refs[pos + len(after):]
        if starts:
            for cp in starts[0](b, outs[nb:nb + n_new], outs[nb + n_new:nb + 2 * n_new]):
                cp.start()
        outs[-1][...] = jnp.zeros_like(outs[-1])

    res = pl.pallas_call(
        body, name=name,
        out_shape=tuple(pltpu.HBM(a.shape, a.dtype) for a in bufs) + (pltpu.SemaphoreType.DMA(()),) * (2 * n_new)
        + (jax.ShapeDtypeStruct((8, 128), F32),),
        in_specs=[HBM] * nb + [SEM] * len(wait_sems) + [ANY] * len(after),
        out_specs=(HBM,) * nb + (SEM,) * (2 * n_new) + (pl.BlockSpec(memory_space=pltpu.VMEM),),
        input_output_aliases={i: i for i in range(nb)},
        compiler_params=pltpu.CompilerParams(has_side_effects=DATAFLOW),
    )(*bufs, *wait_sems, *after)
    return list(res[:nb]), list(res[nb:nb + n_new]), list(res[nb + n_new:nb + 2 * n_new]), res[-1]


def _in_hbm(a):
    return pltpu.with_memory_space_constraint(a, pltpu.HBM)


def _remote(src, dst, send_sem, recv_sem, to):
    return pltpu.make_async_remote_copy(src_ref=src, dst_ref=dst, send_sem=send_sem, recv_sem=recv_sem,
                                        device_id=to, device_id_type=MESH)


def _other_chips():
    x, y, _ = _position()
    return [(1 - x, y), (x, 1 - y), (1 - x, 1 - y)]


def _plan_gather_first(n):
    def plan(b, ss, rs):
        x, y, c = _position()
        to = [(x, y, 1 - c)] + [(*chip, c) for chip in _other_chips()]
        return [_remote(b[w], b[n + w].at[4 * x + 2 * y + c], ss[4 * w + k], rs[4 * w + k], to[k])
                for w in range(n) for k in range(4)]
    return plan, 4 * n


def _plan_gather_pass(n):
    def plan(b, ss, rs):
        x, y, c = _position()
        copies = []
        for w in range(n):
            for j, chip in enumerate(_other_chips()):
                blk = b[n + w].at[4 * chip[0] + 2 * chip[1] + c]
                copies.append(_remote(blk, blk, ss[3 * w + j], rs[3 * w + j], (x, y, 1 - c)))
        return copies
    return plan, 3 * n


def _plan_sibling(n):
    def plan(b, ss, rs):
        x, y, c = _position()
        return [_remote(b[w].at[2 * s + (1 - c)], b[n + w].at[s], ss[4 * w + s], rs[4 * w + s], (x, y, 1 - c))
                for w in range(n) for s in range(N_CHIP)]
    return plan, 4 * n


def _plan_scatter(n):
    def plan(b, ss, rs):
        x, y, c = _position()
        return [_remote(b[w].at[2 * chip[0] + chip[1]], b[n + w].at[2 * x + y], ss[3 * w + j], rs[3 * w + j],
                        (*chip, c))
                for w in range(n) for j, chip in enumerate(_other_chips())]
    return plan, 3 * n


class _Gather:
    def __init__(self, shards, after, name):
        self.n, self.name = len(shards), name
        x, y, c = _position()
        placed = [lax.dynamic_update_index_in_dim(lax.empty((N_DEV,) + s.shape, s.dtype), s, 4 * x + 2 * y + c, 0)
                  for s in shards]
        bufs, self.ss, self.rs, self.token = _split_call(
            name + "_start", [_in_hbm(a) for a in list(shards) + placed], starts=_plan_gather_first(self.n),
            after=after)
        self.shards, self.fulls = bufs[:self.n], bufs[self.n:]
        self.passed = {}

    def _sub(self, ids, sems, per):
        return [sems[per * w + k] for w in ids for k in range(per)]

    def pass_on(self, ids, after, tag):
        m = len(ids)
        first = (_plan_gather_first(m)[0], self._sub(ids, self.ss, 4), self._sub(ids, self.rs, 4),
                 [], [4 * i + k for i in range(m) for k in (1, 2, 3)])
        bufs, ss, rs, token = _split_call(
            "%s_pass_%s" % (self.name, tag), [self.shards[w] for w in ids] + [self.fulls[w] for w in ids],
            waits=[first], starts=_plan_gather_pass(m), after=after)
        for i, w in enumerate(ids):
            self.shards[w], self.fulls[w] = bufs[i], bufs[m + i]
        self.passed[tuple(ids)] = (ss, rs)
        return token

    def finish(self, ids, after, tag):
        m = len(ids)
        ss2, rs2 = self.passed[tuple(ids)]
        first = (_plan_gather_first(m)[0], self._sub(ids, self.ss, 4), self._sub(ids, self.rs, 4),
                 list(range(4 * m)), [4 * i for i in range(m)])
        passed = (_plan_gather_pass(m)[0], ss2, rs2, list(range(3 * m)), list(range(3 * m)))
        bufs, _, _, _ = _split_call(
            "%s_finish_%s" % (self.name, tag), [self.shards[w] for w in ids] + [self.fulls[w] for w in ids],
            waits=[first, passed], after=after)
        return bufs[m:]


class _ReduceScatter:
    def __init__(self, grads, parity, name):
        self.n, self.name, self.parity = len(grads), name, parity
        lands = [lax.empty((N_CHIP,) + g.shape[1:], g.dtype) for g in grads]
        self.bufs, self.ss, self.rs, self.token = _split_call(
            name + "_sibling_start", [_in_hbm(a) for a in list(grads) + lands], starts=_plan_sibling(self.n))

    def pair_sums(self, after):
        n = self.n
        bufs, _, _, _ = _split_call(
            self.name + "_sibling_wait", self.bufs,
            waits=[(_plan_sibling(n)[0], self.ss, self.rs, list(range(4 * n)), list(range(4 * n)))], after=after)
        sums = [_pair_sum(bufs[w], bufs[n + w], self.parity, "%s_pair_sum_%d" % (self.name, w)) for w in range(n)]
        lands = [lax.empty(s.shape, s.dtype) for s in sums]
        self.bufs, self.ss, self.rs, token = _split_call(
            self.name + "_scatter_start", [_in_hbm(a) for a in sums + lands], starts=_plan_scatter(n))
        return token

    def finish(self, after):
        n = self.n
        bufs, _, _, _ = _split_call(
            self.name + "_scatter_wait", self.bufs,
            waits=[(_plan_scatter(n)[0], self.ss, self.rs, list(range(3 * n)), list(range(3 * n)))], after=after)
        return bufs[:n], bufs[n:]


class _Exchanges:
    def __init__(self, parity, order):
        self.parity, self.order = parity, order


def _gather_small(packed, name):
    R = packed.shape[0]

    def body(x_ref, out_ref, send_sems, recv_sems):
        x, y, c = _position()
        me = 4 * x + 2 * y + c
        out_ref[me] = x_ref[...]
        copies = []
        for k in range(1, N_DEV):
            to = (x ^ ((k >> 2) & 1), y ^ ((k >> 1) & 1), c ^ (k & 1))
            cp = pltpu.make_async_remote_copy(
                src_ref=x_ref, dst_ref=out_ref.at[me],
                send_sem=send_sems.at[k], recv_sem=recv_sems.at[k], device_id=to, device_id_type=MESH)
            cp.start()
            copies.append((k, to, cp))
        for k, to, cp in copies:
            cp.wait_send()
            pltpu.make_async_remote_copy(
                src_ref=x_ref, dst_ref=out_ref.at[4 * to[0] + 2 * to[1] + to[2]],
                send_sem=send_sems.at[k], recv_sem=recv_sems.at[k], device_id=to, device_id_type=MESH).wait_recv()

    return pl.pallas_call(
        body, name=name,
        in_specs=[pl.BlockSpec(memory_space=pltpu.VMEM)], out_specs=pl.BlockSpec(memory_space=pltpu.VMEM),
        out_shape=jax.ShapeDtypeStruct((N_DEV, R, 128), F32),
        scratch_shapes=[pltpu.SemaphoreType.DMA((N_DEV,)), pltpu.SemaphoreType.DMA((N_DEV,))],
    )(packed)


def _adamw_math(w, g, m, v):
    m = ADAM_B1 * m + (1.0 - ADAM_B1) * g
    v = ADAM_B2 * v + (1.0 - ADAM_B2) * (g * g)
    m_hat = m / (1.0 - ADAM_B1 ** ADAM_STEP)
    v_hat = v / (1.0 - ADAM_B2 ** ADAM_STEP)
    delta = -ADAM_LR * (m_hat / (jnp.sqrt(v_hat) + ADAM_EPS) + ADAM_WD * w)
    return delta, m, v


def _adamw_big(w, m, v, parts, name):
    R, C = w.shape
    tr = _pick(R, (256,))

    def body(w_ref, m_ref, v_ref, p_ref, g_ref, d_ref, nm_ref, nv_ref):
        g = p_ref[0].astype(F32)
        for s in range(1, N_CHIP):
            g = g + p_ref[s].astype(F32)
        d, nm, nv = _adamw_math(w_ref[...], g, m_ref[...], v_ref[...])
        g_ref[...] = g
        d_ref[...] = d
        nm_ref[...] = nm
        nv_ref[...] = nv

    blk = pl.BlockSpec((tr, C), lambda i: (i, 0))
    out = jax.ShapeDtypeStruct((R, C), F32)
    return pl.pallas_call(
        body, name=name, grid=(R // tr,),
        in_specs=[blk, blk, blk, pl.BlockSpec((N_CHIP, tr, C), lambda i: (0, i, 0))],
        out_specs=[blk, blk, blk, blk], out_shape=[out, out, out, out],
        compiler_params=_cparams(("parallel",)),
    )(w, m, v, parts)


def _adamw_big_landed(w, m, v, parts, lands, slot, name):
    R, C = w.shape
    tr = _pick(R, (256,))

    def body(slot_ref, w_ref, m_ref, v_ref, own_ref, l1_ref, l2_ref, l3_ref, g_ref, d_ref, nm_ref, nv_ref):
        g = own_ref[...].astype(F32)
        for ref in (l1_ref, l2_ref, l3_ref):
            g = g + ref[...].astype(F32)
        d, nm, nv = _adamw_math(w_ref[...], g, m_ref[...], v_ref[...])
        g_ref[...] = g
        d_ref[...] = d
        nm_ref[...] = nm
        nv_ref[...] = nv

    blk = pl.BlockSpec((tr, C), lambda i, slot: (i, 0))

    def chip(k):
        return pl.BlockSpec((None, tr, C), lambda i, slot: ((slot[0] + k) % N_CHIP, i, 0))

    out = jax.ShapeDtypeStruct((R, C), F32)
    return pl.pallas_call(
        body, name=name,
        grid_spec=pltpu.PrefetchScalarGridSpec(
            num_scalar_prefetch=1, grid=(R // tr,),
            in_specs=[blk, blk, blk, chip(0), chip(1), chip(2), chip(3)],
            out_specs=[blk, blk, blk, blk]),
        out_shape=[out, out, out, out],
        compiler_params=_cparams(("parallel",)),
    )(slot, w, m, v, parts, lands, lands, lands)


def _adamw_small(w, m, v, gathered, name):
    R = w.shape[0]

    def body(w_ref, m_ref, v_ref, p_ref, g_ref, d_ref, nm_ref, nv_ref):
        g = p_ref[0]
        for s in range(1, N_DEV):
            g = g + p_ref[s]
        d, nm, nv = _adamw_math(w_ref[...], g, m_ref[...], v_ref[...])
        g_ref[...] = g
        d_ref[...] = d
        nm_ref[...] = nm
        nv_ref[...] = nv

    out = jax.ShapeDtypeStruct((R, 128), F32)
    return pl.pallas_call(
        body, name=name, out_shape=[out, out, out, out],
    )(w, m, v, gathered)


SMALL_NAMES = ("lb_logits", "hg_norm_w", "rel_bias", "norm_mix_w", "norm_mlp_w", "norm_final_w")
SMALL_SHAPES = {"lb_logits": (2, HG_WIDTH), "hg_norm_w": (1, HG_DK), "rel_bias": (AT_HEADS, N_REL_PAD),
                "norm_mix_w": (1, D_MODEL), "norm_mlp_w": (1, D_MODEL), "norm_final_w": (1, D_MODEL)}


def _pack_small(parts):
    rows = []
    for nme in SMALL_NAMES:
        p = parts[nme]
        if nme == "rel_bias":
            p = jnp.pad(p, ((0, 0), (0, N_REL_PAD - N_REL)))
        rows.append(p.reshape(-1, 128))
    flat = jnp.concatenate(rows, axis=0)
    return jnp.pad(flat, ((0, SMALL_ROWS - flat.shape[0]), (0, 0)))


def _unpack_small(packed):
    out, at = {}, 0
    for nme in SMALL_NAMES:
        shp = SMALL_SHAPES[nme]
        nrow = shp[0] * shp[1] // 128
        p = packed[at:at + nrow].reshape(shp)
        at += nrow
        out[nme] = p[:, :N_REL] if nme == "rel_bias" else p
    return out


BIG_NAMES = ("w_in", "w_branch_a", "w_branch_b", "w_out", "w_up", "w_down")


def kernel(x, w_in, lb_logits, hg_norm_w, rel_bias, w_branch_a, w_branch_b, w_out, norm_mix_w, norm_mlp_w, w_up, w_down, norm_final_w, loss_target, m_w_in, m_lb_logits, m_hg_norm_w, m_rel_bias, m_w_branch_a, m_w_branch_b, m_w_out, m_norm_mix_w, m_norm_mlp_w, m_w_up, m_w_down, m_norm_final_w, v_w_in, v_lb_logits, v_hg_norm_w, v_rel_bias, v_w_branch_a, v_w_branch_b, v_w_out, v_norm_mix_w, v_norm_mlp_w, v_w_up, v_w_down, v_norm_final_w):
    big_w = [w_in[0], w_branch_a[0], w_branch_b[0], w_out[0], w_up[0], w_down[0]]
    big_m = [m_w_in[0], m_w_branch_a[0], m_w_branch_b[0], m_w_out[0], m_w_up[0], m_w_down[0]]
    big_v = [v_w_in[0], v_w_branch_a[0], v_w_branch_b[0], v_w_out[0], v_w_up[0], v_w_down[0]]

    shards = [w.astype(BF16) for w in big_w]
    parity = lax.axis_index("c").astype(jnp.int32).reshape(1)
    loss_part, grad_x, chip_parts, small = _local_step(
        x[0], loss_target[0], lb_logits, hg_norm_w, rel_bias[0], norm_mix_w, norm_mlp_w,
        norm_final_w.reshape(1, D_MODEL), shards[0], shards[1:], _Exchanges(parity, _gather_order()))
    loss = lax.psum(loss_part[0, 0], ("x", "y", "c"))
    rs_in, rs_mix, rs_up, rs_down = chip_parts
    slot = (2 * lax.axis_index("x") + lax.axis_index("y")).astype(jnp.int32).reshape(1)
    big = {}

    def finish(rs, names, after):
        sums, lands = rs.finish(after)
        for nme, own, land in zip(names, sums, lands):
            i = BIG_NAMES.index(nme)
            big[nme] = _adamw_big_landed(big_w[i], big_m[i], big_v[i], own, land, slot, "adamw_" + nme)
        return [big[nme][1] for nme in names]

    done = finish(rs_down, ["w_down"], [grad_x])
    done = finish(rs_up, ["w_up"], done)
    done = finish(rs_mix, ["w_branch_a", "w_branch_b", "w_out"], done)

    sw = dict(lb_logits=lb_logits, hg_norm_w=hg_norm_w, rel_bias=rel_bias[0], norm_mix_w=norm_mix_w,
              norm_mlp_w=norm_mlp_w, norm_final_w=norm_final_w.reshape(1, D_MODEL))
    sm = dict(lb_logits=m_lb_logits, hg_norm_w=m_hg_norm_w, rel_bias=m_rel_bias[0], norm_mix_w=m_norm_mix_w,
              norm_mlp_w=m_norm_mlp_w, norm_final_w=m_norm_final_w.reshape(1, D_MODEL))
    sv = dict(lb_logits=v_lb_logits, hg_norm_w=v_hg_norm_w, rel_bias=v_rel_bias[0], norm_mix_w=v_norm_mix_w,
              norm_mlp_w=v_norm_mlp_w, norm_final_w=v_norm_final_w.reshape(1, D_MODEL))
    gathered = _gather_small(_pack_small(small), "gather_small")
    small_packed = _adamw_small(_pack_small(sw), _pack_small(sm), _pack_small(sv), gathered, "adamw_small")
    small_out = [_unpack_small(p) for p in small_packed]

    finish(rs_in, ["w_in"], done + [small_packed[0]])

    def leaf(kind, nme):
        if nme in BIG_NAMES:
            return big[nme][kind][None]
        p = small_out[kind][nme]
        if nme == "rel_bias":
            return p[None]
        if nme == "norm_final_w":
            return p.reshape(D_MODEL)
        return p

    order = ("w_in", "lb_logits", "hg_norm_w", "rel_bias", "w_branch_a", "w_branch_b", "w_out", "norm_mix_w",
             "norm_mlp_w", "w_up", "w_down", "norm_final_w")
    outs = [loss, grad_x[None]]
    for kind in range(4):
        outs += [leaf(kind, nme) for nme in order]
    return tuple(outs)
```

```python
import functools

import jax
import jax.numpy as jnp
from jax import lax
from jax.experimental import pallas as pl
from jax.experimental.pallas import tpu as pltpu

F32 = jnp.float32
BF16 = jnp.bfloat16
HIGHEST = lax.Precision.HIGHEST
MESH = pl.DeviceIdType.MESH

D_MODEL = 2048
HG_HEADS = 8
HG_DK = 128
HG_WIDTH = 1024
AT_HEADS = 16
AT_DH = 64
AT_WIDTH = 1024
CHUNK = 64
LEFT_CHUNKS = 8
BAND = (LEFT_CHUNKS + 1) * CHUNK
PAD = LEFT_CHUNKS * CHUNK
REL_CLIP = 256
N_REL = 2 * REL_CLIP + 1
N_REL_PAD = 640
D_FF = 4 * D_MODEL
D_IN = 4 * HG_WIDTH + 3 * AT_WIDTH + 2 * D_MODEL
EPS = 1e-6
N_DEV = 8
N_CHIP = 4

ADAM_LR = 0.001
ADAM_B1 = 0.9
ADAM_B2 = 0.999
ADAM_EPS = 1e-08
ADAM_WD = 0.01
ADAM_STEP = 10

COL_HQ, COL_HF, COL_HI, COL_HG = 0, 8, 16, 24
COL_AQ, COL_AK, COL_AV = 32, 40, 48
COL_GATE_A, COL_GATE_B = 7, 9

VMEM_LIMIT = 56 * 1024 * 1024
SMALL_ROWS = 152


def _cparams(sem=None, **kw):
    if sem is not None:
        kw["dimension_semantics"] = sem
    return pltpu.CompilerParams(vmem_limit_bytes=VMEM_LIMIT, **kw)


def _pick(n, cands):
    for c in cands:
        if n % c == 0:
            return c
    return n


def _sigmoid(x):
    return 1.0 / (1.0 + jnp.exp(-x))


ANY = pl.BlockSpec(memory_space=pl.ANY)


def _position():
    return lax.axis_index("x"), lax.axis_index("y"), lax.axis_index("c")


def _call(body, args, *, name, grid, in_specs, out_specs, out_shape, scratch_shapes=(), sem=None, after=()):
    n_in = len(args)

    def ordered(*refs):
        body(*refs[:n_in], *refs[n_in + len(after):])

    return list(pl.pallas_call(
        ordered if after else body, name=name, grid=grid, in_specs=list(in_specs) + [ANY] * len(after),
        out_specs=out_specs, out_shape=out_shape, scratch_shapes=list(scratch_shapes),
        compiler_params=_cparams(sem))(*args, *after))


MAX_CONTRACTION_TILE = 4096


def _accumulate(part, acc_ref, step, n_steps, finish):
    if n_steps == 1:
        finish(part)
        return

    @pl.when(step == 0)
    def _():
        acc_ref[...] = part

    @pl.when(step > 0)
    def _():
        acc_ref[...] += part

    @pl.when(step == n_steps - 1)
    def _():
        finish(acc_ref[...])


def _mm_nn(a, wb, out_dtype, name, after=(), squared_relu=False):
    M, K = a.shape
    NB, K2, Nb = wb.shape
    assert K == K2
    tm = min(M, 1024)
    tk = min(K, MAX_CONTRACTION_TILE)
    tn = _pick(Nb, (512, 1408, 256))
    nk = K // tk
    nn = Nb // tn

    def body(a_ref, b_ref, o_ref, *rest):
        def finish(total):
            o_ref[...] = total.astype(out_dtype)
            if squared_relu:
                ra = jnp.maximum(total, 0.0)
                rest[0][...] = (ra * ra).astype(BF16)

        part = jnp.dot(a_ref[...], b_ref[...], preferred_element_type=F32)
        _accumulate(part, rest[-1] if nk > 1 else None, pl.program_id(3), nk, finish)

    tile = pl.BlockSpec((tm, tn), lambda m, j, n, k: (m, j * nn + n))
    outs = _call(
        body, (a, wb), name=name, grid=(M // tm, NB, nn, nk),
        in_specs=[pl.BlockSpec((tm, tk), lambda m, j, n, k: (m, k)),
                  pl.BlockSpec((None, tk, tn), lambda m, j, n, k: (j, k, n))],
        out_specs=[tile, tile] if squared_relu else [tile],
        out_shape=[jax.ShapeDtypeStruct((M, NB * Nb), out_dtype)]
        + ([jax.ShapeDtypeStruct((M, NB * Nb), BF16)] if squared_relu else []),
        scratch_shapes=[] if nk == 1 else [pltpu.VMEM((tm, tn), F32)],
        sem=("parallel", "parallel", "parallel", "arbitrary"), after=after)
    return outs if squared_relu else outs[0]


def _mm_nt(a, wb, out_dtype, name, after=(), epilogue=None):
    M, N = a.shape
    NB, K, Nb = wb.shape
    assert N == NB * Nb
    tm = min(M, 1024)
    n_tiles_live = 1 + (len(epilogue[0]) + len(epilogue[2]) if epilogue else 0)
    tko = _pick(K, (1024,)) if n_tiles_live <= 3 else _pick(K, (512,))
    tc = _pick(Nb, (2048, 1024, 1408, 256))
    nc = Nb // tc
    jb = max([d for d in (8, 4, 2, 1) if NB % d == 0 and d * tc <= MAX_CONTRACTION_TILE]) if nc == 1 else 1
    nsteps = (NB // jb) * nc
    extra, first_cols, out_dtypes, fn = epilogue or ((), (), (out_dtype,), lambda total: (total,))
    n_extra, n_out = len(extra), len(out_dtypes)

    def body(a_ref, b_ref, *rest):
        def finish(total):
            results = fn(total, *[r[...] for r in rest[:n_extra]])
            for o_ref, res, dt in zip(rest[n_extra:n_extra + n_out], results, out_dtypes):
                o_ref[...] = res.astype(dt)

        part = sum(lax.dot_general(a_ref[:, i * tc:(i + 1) * tc], b_ref[i], (((1,), (1,)), ((), ())),
                                   preferred_element_type=F32) for i in range(jb))
        _accumulate(part, rest[-1], pl.program_id(2) * nc + pl.program_id(3), nsteps, finish)

    def tile(first):
        return pl.BlockSpec((tm, tko), lambda m, ko, j, c: (m, first + ko))

    outs = _call(
        body, (a, wb) + tuple(extra), name=name,
        grid=(M // tm, K // tko, NB // jb, nc),
        in_specs=[pl.BlockSpec((tm, jb * tc), lambda m, ko, j, c: (m, j * nc + c)),
                  pl.BlockSpec((jb, tko, tc), lambda m, ko, j, c: (j, ko, c))] + [tile(col // tko) for col in first_cols],
        out_specs=[tile(0)] * n_out,
        out_shape=[jax.ShapeDtypeStruct((M, K), dt) for dt in out_dtypes],
        scratch_shapes=[] if nsteps == 1 else [pltpu.VMEM((tm, tko), F32)],
        sem=("parallel", "parallel", "arbitrary", "arbitrary"), after=after)
    return outs if epilogue else outs[0]


def _mm_tn(a, g, nb, out_dtype, name, after=()):
    M, Ka = a.shape
    M2, N = g.shape
    assert M == M2 and N % nb == 0
    Nb = N // nb
    tka = _pick(Ka, (1024,))
    tn = _pick(Nb, (512, 1408, 256))
    nn = Nb // tn

    def body(a_ref, g_ref, o_ref):
        o_ref[...] = lax.dot_general(a_ref[...], g_ref[...], (((0,), (0,)), ((), ())),
                                     preferred_element_type=F32).astype(out_dtype)

    return _call(
        body, (a, g), name=name,
        grid=(Ka // tka, nb, nn),
        in_specs=[pl.BlockSpec((M, tka), lambda ka, j, n: (0, ka)),
                  pl.BlockSpec((M, tn), lambda ka, j, n: (0, j * nn + n))],
        out_specs=[pl.BlockSpec((None, tka, tn), lambda ka, j, n: (j, ka, n))],
        out_shape=[jax.ShapeDtypeStruct((nb, Ka, Nb), out_dtype)],
        sem=("parallel", "parallel", "parallel"), after=after)[0]


ROW_TILE = 256


def _rms_fwd(x, w, name):
    T, Dm = x.shape

    def body(x_ref, w_ref, u_ref):
        xv = x_ref[...]
        r = lax.rsqrt(jnp.mean(xv * xv, axis=-1, keepdims=True) + EPS)
        u_ref[...] = (xv * r * w_ref[...]).astype(BF16)

    return pl.pallas_call(
        body, name=name, grid=(T // ROW_TILE,),
        in_specs=[pl.BlockSpec((ROW_TILE, Dm), lambda i: (i, 0)), pl.BlockSpec((1, Dm), lambda i: (0, 0))],
        out_specs=pl.BlockSpec((ROW_TILE, Dm), lambda i: (i, 0)),
        out_shape=jax.ShapeDtypeStruct((T, Dm), BF16),
        compiler_params=_cparams(("parallel",)),
    )(x, w)


def _resid_rms_fwd(x, mix, w, name):
    T, Dm = x.shape

    def body(x_ref, m_ref, w_ref, h_ref, u_ref):
        h = x_ref[...] + m_ref[...]
        h_ref[...] = h
        r = lax.rsqrt(jnp.mean(h * h, axis=-1, keepdims=True) + EPS)
        u_ref[...] = (h * r * w_ref[...]).astype(BF16)

    row = pl.BlockSpec((ROW_TILE, Dm), lambda i: (i, 0))
    return pl.pallas_call(
        body, name=name, grid=(T // ROW_TILE,),
        in_specs=[row, row, pl.BlockSpec((1, Dm), lambda i: (0, 0))],
        out_specs=[row, row],
        out_shape=[jax.ShapeDtypeStruct((T, Dm), F32), jax.ShapeDtypeStruct((T, Dm), BF16)],
        compiler_params=_cparams(("parallel",)),
    )(x, mix, w)


def _loss_head(h1, mlp, wf, target, name):
    T, Dm = h1.shape

    def body(h_ref, m_ref, w_ref, t_ref, loss_ref, dh_ref, dhb_ref, dw_ref):
        i = pl.program_id(0)
        h = h_ref[...] + m_ref[...]
        r = lax.rsqrt(jnp.mean(h * h, axis=-1, keepdims=True) + EPS)
        xh = h * r
        wv = w_ref[...]
        e = xh * wv - t_ref[...]
        part = 0.5 * jnp.sum(jnp.mean(e * e, axis=-1, keepdims=True), axis=0, keepdims=True)
        dy = e * (1.0 / Dm)
        dw = jnp.sum(dy * xh, axis=0, keepdims=True)
        gy = dy * wv
        dh = r * (gy - xh * jnp.mean(gy * xh, axis=-1, keepdims=True))
        dh_ref[...] = dh
        dhb_ref[...] = dh.astype(BF16)

        @pl.when(i == 0)
        def _():
            loss_ref[...] = jnp.zeros_like(loss_ref)
            dw_ref[...] = jnp.zeros_like(dw_ref)

        loss_ref[...] += jnp.broadcast_to(part, loss_ref.shape)
        dw_ref[...] += dw

    row = pl.BlockSpec((ROW_TILE, Dm), lambda i: (i, 0))
    vec = pl.BlockSpec((1, Dm), lambda i: (0, 0))
    return pl.pallas_call(
        body, name=name, grid=(T // ROW_TILE,),
        in_specs=[row, row, vec, row],
        out_specs=[pl.BlockSpec((8, 128), lambda i: (0, 0)), row, row, vec],
        out_shape=[jax.ShapeDtypeStruct((8, 128), F32), jax.ShapeDtypeStruct((T, Dm), F32),
                   jax.ShapeDtypeStruct((T, Dm), BF16), jax.ShapeDtypeStruct((1, Dm), F32)],
        compiler_params=_cparams(("arbitrary",)),
    )(h1, mlp, wf, target)


def _rms_bwd(dyn, x, w, dres, name, after=()):
    T, Dm = x.shape

    def body(g_ref, x_ref, w_ref, r_ref, dx_ref, dxb_ref, dw_ref):
        i = pl.program_id(0)
        xv = x_ref[...]
        r = lax.rsqrt(jnp.mean(xv * xv, axis=-1, keepdims=True) + EPS)
        xh = xv * r
        g = g_ref[...]
        dw = jnp.sum(g * xh, axis=0, keepdims=True)
        gy = g * w_ref[...]
        dx = r_ref[...] + r * (gy - xh * jnp.mean(gy * xh, axis=-1, keepdims=True))
        dx_ref[...] = dx
        dxb_ref[...] = dx.astype(BF16)

        @pl.when(i == 0)
        def _():
            dw_ref[...] = jnp.zeros_like(dw_ref)

        dw_ref[...] += dw

    row = pl.BlockSpec((ROW_TILE, Dm), lambda i: (i, 0))
    vec = pl.BlockSpec((1, Dm), lambda i: (0, 0))
    return _call(
        body, (dyn, x, w, dres), name=name, grid=(T // ROW_TILE,),
        in_specs=[row, row, vec, row],
        out_specs=[row, row, vec],
        out_shape=[jax.ShapeDtypeStruct((T, Dm), F32), jax.ShapeDtypeStruct((T, Dm), BF16),
                   jax.ShapeDtypeStruct((1, Dm), F32)],
        sem=("arbitrary",), after=after)


COL_TILE = 2048


def _relu2_fwd(a, name):
    T, N = a.shape

    def body(a_ref, r_ref):
        ra = jnp.maximum(a_ref[...], 0.0)
        r_ref[...] = (ra * ra).astype(BF16)

    blk = pl.BlockSpec((ROW_TILE, COL_TILE), lambda i, j: (i, j))
    return pl.pallas_call(
        body, name=name, grid=(T // ROW_TILE, N // COL_TILE), in_specs=[blk], out_specs=blk,
        out_shape=jax.ShapeDtypeStruct((T, N), BF16),
        compiler_params=_cparams(("parallel", "parallel")),
    )(a)


def _relu2_bwd(dr, a, name, after=()):
    T, N = a.shape

    def body(dr_ref, a_ref, da_ref):
        da_ref[...] = (dr_ref[...] * (2.0 * jnp.maximum(a_ref[...], 0.0))).astype(BF16)

    blk = pl.BlockSpec((ROW_TILE, COL_TILE), lambda i, j: (i, j))
    return _call(
        body, (dr, a), name=name, grid=(T // ROW_TILE, N // COL_TILE), in_specs=[blk, blk], out_specs=[blk],
        out_shape=[jax.ShapeDtypeStruct((T, N), BF16)], sem=("parallel", "parallel"), after=after)[0]


GATE_TILE = 1024


def _merge_fwd(z, pa, pb, name):
    T, Dm = pa.shape

    def body(za_ref, zb_ref, pa_ref, pb_ref, m_ref):
        m_ref[...] = (_sigmoid(za_ref[...]) * pa_ref[...] + _sigmoid(zb_ref[...]) * pb_ref[...]).astype(BF16)

    blk = pl.BlockSpec((ROW_TILE, GATE_TILE), lambda i, j: (i, j))
    return pl.pallas_call(
        body, name=name, grid=(T // ROW_TILE, Dm // GATE_TILE),
        in_specs=[pl.BlockSpec((ROW_TILE, GATE_TILE), lambda i, j: (i, COL_GATE_A + j)),
                  pl.BlockSpec((ROW_TILE, GATE_TILE), lambda i, j: (i, COL_GATE_B + j)), blk, blk],
        out_specs=blk,
        out_shape=jax.ShapeDtypeStruct((T, Dm), BF16),
        compiler_params=_cparams(("parallel", "parallel")),
    )(z, z, pa, pb)


def _merge_grads(d, za, zb, pa, pb):
    ga = _sigmoid(za)
    gb = _sigmoid(zb)
    return d * ga, d * gb, d * pa * ga * (1.0 - ga), d * pb * gb * (1.0 - gb)


def _dot_hi(a, b, dims):
    return lax.dot_general(a, b, (dims, ((), ())), precision=HIGHEST, preferred_element_type=F32)


NN = ((1,), (0,))
NT = ((1,), (1,))
TN = ((0,), (0,))


def _hg_gates(hq, hf, lb):
    sq = _sigmoid(hq)
    q = hq * sq * (HG_DK ** -0.5)
    f = _sigmoid(hf)
    g = lb + (1.0 - lb) * f
    return q, sq, f, g, jnp.log(g), 1.0 - g


def _tri(lower):
    r = lax.broadcasted_iota(jnp.int32, (CHUNK, CHUNK), 0)
    c = lax.broadcasted_iota(jnp.int32, (CHUNK, CHUNK), 1)
    return jnp.where((r >= c) if lower else (r <= c), 1.0, 0.0).astype(F32)


GROUP = 16
N_GROUPS = CHUNK // GROUP


def _dot_bf16(a, b, dims):
    return lax.dot_general(a.astype(BF16), b.astype(BF16), (dims, ((), ())), preferred_element_type=F32)


def _rows_iota():
    return lax.broadcasted_iota(jnp.int32, (CHUNK, HG_DK), 0)


def _by_query_group(q, kk, b, g):
    r0 = GROUP * g
    b0 = b[r0:r0 + 1]
    decay = jnp.exp(b[r0:r0 + GROUP] - b0)
    ks = jnp.where(_rows_iota() < r0, kk * jnp.exp(jnp.minimum(b0 - b, 0.0)), 0.0)
    return q[r0:r0 + GROUP] * decay, ks, decay


def _by_key_group(q, kk, b, j):
    r1 = GROUP * (j + 1)
    b1 = b[r1 - 1:r1]
    decay = jnp.exp(b1 - b[r1 - GROUP:r1])
    qs = jnp.where(_rows_iota() >= r1, q * jnp.exp(jnp.minimum(b - b1, 0.0)), 0.0)
    return qs, kk[r1 - GROUP:r1] * decay, decay


def _scores_between_groups(q, kk, b):
    blocks = [jnp.zeros((GROUP, CHUNK), F32)]
    for g in range(1, N_GROUPS):
        qs, ks, _ = _by_query_group(q, kk, b, g)
        blocks.append(_dot_bf16(qs, ks, NT))
    return jnp.concatenate(blocks, axis=0)


def _hgrn2_fwd(z, lb_logits, hg_norm_w, name, after=()):
    T = z.shape[0]
    n_chunks = T // CHUNK

    def body(hq_ref, hf_ref, hi_ref, hg_ref, lbl_ref, nw_ref, o_ref, ya_ref, sall_ref, st_ref):
        lbl = lbl_ref[...]
        lb = 1.0 / (1.0 + jnp.exp(lbl[1:2, :] - lbl[0:1, :]))
        st_ref[...] = jnp.zeros_like(st_ref)
        tri = _tri(True)
        row8 = lax.broadcasted_iota(jnp.int32, (8, HG_DK), 0)

        def chunk(c, carry):
            rows = pl.ds(pl.multiple_of(c * CHUNK, CHUNK), CHUNK)
            q, _, _, _, lg, kk = _hg_gates(hq_ref[rows, :], hf_ref[rows, :], lb)
            v = hi_ref[rows, :]
            b = _dot_hi(tri, lg, NN)
            st = st_ref[...]
            sall_ref[c] = st
            for grp in range(N_GROUPS):
                r0 = GROUP * grp
                for h8 in range(GROUP // 8):
                    n = 8 * (h8 + 1)
                    bs, ks, vs = b[r0:r0 + n], kk[r0:r0 + n], v[r0:r0 + n]
                    sidx = lax.broadcasted_iota(jnp.int32, (n, HG_DK), 0)
                    blk = jnp.zeros((8, HG_DK), F32)
                    for i in range(8):
                        t = r0 + 8 * h8 + i
                        e = jnp.where(sidx <= 8 * h8 + i, jnp.exp(b[t:t + 1] - bs), 0.0)
                        p = jnp.sum(e * ks * q[t:t + 1], axis=1, keepdims=True)
                        ot = jnp.sum(p * vs, axis=0, keepdims=True)
                        blk = blk + jnp.where(row8 == i, ot, 0.0)
                    o_ref[pl.ds(pl.multiple_of(c * CHUNK + r0 + 8 * h8, 8), 8), :] = blk
            o_ref[rows, :] += _dot_hi(q * jnp.exp(b), st, NT) + _dot_bf16(_scores_between_groups(q, kk, b), v, NN)
            bl = b[CHUNK - 1:CHUNK]
            ke = kk * jnp.exp(bl - b)
            st_ref[...] = st * jnp.exp(bl) + _dot_hi(v, ke, TN)
            return carry

        lax.fori_loop(0, n_chunks, chunk, 0, unroll=2)
        o = o_ref[...]
        r = lax.rsqrt(jnp.mean(o * o, axis=-1, keepdims=True) + EPS)
        hg = hg_ref[...]
        ya_ref[...] = (o * r * nw_ref[...] * (hg * _sigmoid(hg))).astype(BF16)

    def col(base):
        return pl.BlockSpec((T, HG_DK), lambda h: (0, base + h))

    return _call(
        body, (z, z, z, z, lb_logits, hg_norm_w), name=name, grid=(HG_HEADS,),
        in_specs=[col(COL_HQ), col(COL_HF), col(COL_HI), col(COL_HG),
                  pl.BlockSpec((2, HG_DK), lambda h: (0, h)), pl.BlockSpec((1, HG_DK), lambda h: (0, 0))],
        out_specs=[col(0), col(0), pl.BlockSpec((None, n_chunks, HG_DK, HG_DK), lambda h: (h, 0, 0, 0))],
        out_shape=[jax.ShapeDtypeStruct((T, HG_WIDTH), F32), jax.ShapeDtypeStruct((T, HG_WIDTH), BF16),
                   jax.ShapeDtypeStruct((HG_HEADS, n_chunks, HG_DK, HG_DK), F32)],
        scratch_shapes=[pltpu.VMEM((HG_DK, HG_DK), F32)],
        sem=("parallel",), after=after)


def _hgrn2_bwd(z, lb_logits, hg_norm_w, o_raw, s_all, dya, name, after=()):
    T = z.shape[0]
    n_chunks = T // CHUNK

    def body(hq_ref, hf_ref, hi_ref, hg_ref, lbl_ref, nw_ref, o_ref, sall_ref, dya_ref,
             dhq_ref, dhf_ref, dhi_ref, dhg_ref, dlbl_ref, dnw_ref,
             do_ref, dst_ref, dlb_ref, *per_chunk):
        h = pl.program_id(0)
        lbl = lbl_ref[...]
        lb = 1.0 / (1.0 + jnp.exp(lbl[1:2, :] - lbl[0:1, :]))

        o = o_ref[...]
        r = lax.rsqrt(jnp.mean(o * o, axis=-1, keepdims=True) + EPS)
        oh = o * r
        nw = nw_ref[...]
        hg = hg_ref[...]
        sg = _sigmoid(hg)
        dy = dya_ref[...]
        d_on = dy * (hg * sg)
        dhg_ref[...] = (dy * (oh * nw) * (sg * (1.0 + hg * (1.0 - sg)))).astype(BF16)
        dnw = jnp.sum(d_on * oh, axis=0, keepdims=True)
        gy = d_on * nw
        do_ref[...] = r * (gy - oh * jnp.mean(gy * oh, axis=-1, keepdims=True))

        @pl.when(h == 0)
        def _():
            dnw_ref[...] = jnp.zeros_like(dnw_ref)

        dnw_ref[...] += jnp.broadcast_to(dnw, dnw_ref.shape)

        dst_ref[...] = jnp.zeros_like(dst_ref)
        dlb_ref[...] = jnp.zeros_like(dlb_ref)
        tri = _tri(True)
        tri_t = _tri(False)
        row8 = lax.broadcasted_iota(jnp.int32, (8, HG_DK), 0)
        row_group = lax.broadcasted_iota(jnp.int32, (CHUNK, CHUNK), 0) // GROUP
        col_group = lax.broadcasted_iota(jnp.int32, (CHUNK, CHUNK), 1) // GROUP
        earlier_group = col_group < row_group
        later_group = col_group > row_group

        def chunk(c, dq_ref, dk_ref, dv_ref):
            rows = pl.ds(pl.multiple_of(c * CHUNK, CHUNK), CHUNK)
            hq = hq_ref[rows, :]
            q, sq, f, g, lg, kk = _hg_gates(hq, hf_ref[rows, :], lb)
            v = hi_ref[rows, :]
            do = do_ref[rows, :]
            b = _dot_hi(tri, lg, NN)
            eb = jnp.exp(b)
            bl = b[CHUNK - 1:CHUNK]
            ebl = jnp.exp(bl)
            ekb = jnp.exp(bl - b)
            qe = q * eb
            ke = kk * ekb
            st = sall_ref[c]
            dst = dst_ref[...]
            dqe = _dot_hi(do, st, NN)
            dke = _dot_hi(v, dst, NN)
            dv_inter = _dot_hi(ke, dst, NT)
            d_ebl = jnp.sum(st * dst, axis=0, keepdims=True)
            dst_ref[...] = dst * ebl + _dot_hi(do, qe, TN)

            dk_ref[...] = jnp.zeros_like(dk_ref)
            dv_ref[...] = jnp.zeros_like(dv_ref)
            for grp in range(N_GROUPS):
                r0 = GROUP * grp
                for h8 in range(GROUP // 8):
                    n = 8 * (h8 + 1)
                    bs, ks, vs = b[r0:r0 + n], kk[r0:r0 + n], v[r0:r0 + n]
                    sidx = lax.broadcasted_iota(jnp.int32, (n, HG_DK), 0)
                    blk = jnp.zeros((8, HG_DK), F32)
                    for i in range(8):
                        t = r0 + 8 * h8 + i
                        qt = q[t:t + 1]
                        dot_ = do[t:t + 1]
                        e = jnp.where(sidx <= 8 * h8 + i, jnp.exp(b[t:t + 1] - bs), 0.0)
                        w = e * ks
                        p = jnp.sum(w * qt, axis=1, keepdims=True)
                        dsc = jnp.sum(vs * dot_, axis=1, keepdims=True)
                        dqt = jnp.sum(dsc * w, axis=0, keepdims=True)
                        blk = blk + jnp.where(row8 == i, dqt, 0.0)
                        dk_ref[r0:r0 + n, :] += dsc * e * qt
                        dv_ref[r0:r0 + n, :] += p * dot_
                    dq_ref[r0 + 8 * h8:r0 + n, :] = blk
            ds_far = jnp.where(earlier_group, _dot_bf16(do, v, NT), 0.0)
            ds_far_t = jnp.where(later_group, _dot_bf16(v, do, NT), 0.0)
            dq_far, dk_far = [jnp.zeros((GROUP, HG_DK), F32)], []
            for grp in range(1, N_GROUPS):
                r0 = GROUP * grp
                _, ks, decay = _by_query_group(q, kk, b, grp)
                dq_far.append(decay * _dot_hi(ds_far[r0:r0 + GROUP], ks, NN))
                qs, _, decay = _by_key_group(q, kk, b, grp - 1)
                dk_far.append(decay * _dot_hi(ds_far_t[r0 - GROUP:r0], qs, NN))
            dk_far.append(jnp.zeros((GROUP, HG_DK), F32))
            dv_far = _dot_bf16(_scores_between_groups(q, kk, b), do, TN)
            dq_i = dq_ref[...] + jnp.concatenate(dq_far, axis=0)
            dk_i = dk_ref[...] + jnp.concatenate(dk_far, axis=0)
            dke_ke = dke * ke
            db = q * dq_i - kk * dk_i + dqe * qe - dke_ke
            db_last = jnp.sum(dke_ke, axis=0, keepdims=True) + d_ebl * ebl
            dlg = _dot_hi(tri_t, db, NN) + db_last
            dq = dq_i + dqe * eb
            dkk = dk_i + dke * ekb
            dg = dlg / g - dkk
            dhq_ref[rows, :] = (dq * (HG_DK ** -0.5) * (sq * (1.0 + hq * (1.0 - sq)))).astype(BF16)
            dhf_ref[rows, :] = (dg * (1.0 - lb) * f * (1.0 - f)).astype(BF16)
            dhi_ref[rows, :] = (dv_ref[...] + dv_far + dv_inter).astype(BF16)
            dlb_ref[...] += jnp.sum(dg * (1.0 - f), axis=0, keepdims=True)

        def two_chunks(i, carry):
            chunk(n_chunks - 1 - 2 * i, *per_chunk[:3])
            chunk(n_chunks - 2 - 2 * i, *per_chunk[3:])
            return carry

        lax.fori_loop(0, n_chunks // 2, two_chunks, 0)
        dl0 = dlb_ref[...] * lb * (1.0 - lb)
        dlbl_ref[0:1, :] = dl0
        dlbl_ref[1:2, :] = -dl0

    def col(base):
        return pl.BlockSpec((T, HG_DK), lambda h: (0, base + h))

    outb = jax.ShapeDtypeStruct((T, HG_WIDTH), BF16)
    return _call(
        body, (z, z, z, z, lb_logits, hg_norm_w, o_raw, s_all, dya), name=name, grid=(HG_HEADS,),
        in_specs=[col(COL_HQ), col(COL_HF), col(COL_HI), col(COL_HG),
                  pl.BlockSpec((2, HG_DK), lambda h: (0, h)), pl.BlockSpec((1, HG_DK), lambda h: (0, 0)),
                  col(0), pl.BlockSpec((None, n_chunks, HG_DK, HG_DK), lambda h: (h, 0, 0, 0)), col(0)],
        out_specs=[col(0), col(0), col(0), col(0), pl.BlockSpec((2, HG_DK), lambda h: (0, h)),
                   pl.BlockSpec((8, HG_DK), lambda h: (0, 0))],
        out_shape=[outb, outb, outb, outb, jax.ShapeDtypeStruct((2, HG_WIDTH), F32),
                   jax.ShapeDtypeStruct((8, HG_DK), F32)],
        scratch_shapes=[pltpu.VMEM((T, HG_DK), F32), pltpu.VMEM((HG_DK, HG_DK), F32), pltpu.VMEM((1, HG_DK), F32)]
        + [pltpu.VMEM((CHUNK, HG_DK), F32)] * 6,
        sem=("arbitrary",), after=after)


CONST_KEYS = PAD - REL_CLIP
VAR_KEYS = BAND - CONST_KEYS
REL_LO = 128
REL_SPAN = N_REL_PAD - REL_LO


def _rel_onehot(t):
    r = lax.broadcasted_iota(jnp.int32, (REL_SPAN, VAR_KEYS), 0)
    j = lax.broadcasted_iota(jnp.int32, (REL_SPAN, VAR_KEYS), 1)
    idx = jnp.clip(t + PAD - CONST_KEYS - j, -REL_CLIP, REL_CLIP) + REL_CLIP - REL_LO
    return jnp.where(r == idx, 1.0, 0.0).astype(BF16)


def _split3(x):
    hi = x.astype(BF16)
    r1 = x - hi.astype(F32)
    mid = r1.astype(BF16)
    return hi, mid, (r1 - mid.astype(F32)).astype(BF16)


def _bias_expand(rel, name):
    def body(rel_ref, out_ref):
        tab = rel_ref[...]
        onehot = _rel_onehot(pl.program_id(0))
        out_ref[:, 0:CONST_KEYS] = jnp.broadcast_to(tab[:, 2 * REL_CLIP:2 * REL_CLIP + 1], (AT_HEADS, CONST_KEYS))
        out_ref[:, CONST_KEYS:BAND] = sum(
            jnp.dot(piece, onehot, preferred_element_type=F32) for piece in _split3(tab[:, REL_LO:N_REL_PAD]))

    return pl.pallas_call(
        body, name=name, grid=(CHUNK,),
        in_specs=[pl.BlockSpec((AT_HEADS, N_REL_PAD), lambda t: (0, 0))],
        out_specs=pl.BlockSpec((None, AT_HEADS, BAND), lambda t: (t, 0, 0)),
        out_shape=jax.ShapeDtypeStruct((CHUNK, AT_HEADS, BAND), F32),
        compiler_params=_cparams(("parallel",)),
    )(rel)


def _bias_reduce(dbias_t, name, after=()):
    def body(db_ref, out_ref):
        t = pl.program_id(0)

        @pl.when(t == 0)
        def _():
            out_ref[...] = jnp.zeros_like(out_ref)

        db = db_ref[...]
        onehot = _rel_onehot(t)
        acc = sum(lax.dot_general(piece, onehot, (NT, ((), ())), preferred_element_type=F32)
                  for piece in _split3(db[:, CONST_KEYS:BAND]))
        lane = lax.broadcasted_iota(jnp.int32, (AT_HEADS, REL_SPAN), 1)
        last = jnp.sum(db[:, 0:CONST_KEYS], axis=1, keepdims=True)
        out_ref[:, REL_LO:N_REL_PAD] += acc + jnp.where(lane == 2 * REL_CLIP - REL_LO, last, 0.0)

    return _call(
        body, (dbias_t,), name=name, grid=(CHUNK,),
        in_specs=[pl.BlockSpec((None, AT_HEADS, BAND), lambda t: (t, 0, 0))],
        out_specs=[pl.BlockSpec((AT_HEADS, N_REL_PAD), lambda t: (0, 0))],
        out_shape=[jax.ShapeDtypeStruct((AT_HEADS, N_REL_PAD), F32)],
        sem=("arbitrary",), after=after)[0]


def _pair_lanes():
    return lax.broadcasted_iota(jnp.int32, (CHUNK, 2 * AT_DH), 1) < AT_DH


def _block_diag(a):
    first = _pair_lanes()
    return jnp.concatenate([jnp.where(first, a, 0.0), jnp.where(first, 0.0, a)], axis=0).astype(BF16)


def _diag_blocks(a):
    return jnp.where(_pair_lanes(), a[:CHUNK], a[CHUNK:])


def _band_probs_t(kb, qbd, bias_t, c):
    s = lax.dot_general(kb, qbd, (NT, ((), ())), preferred_element_type=F32) * (AT_DH ** -0.5) + bias_t
    j = lax.broadcasted_iota(jnp.int32, (BAND, 2 * AT_DH), 0)
    s = jnp.where(j + c * CHUNK >= PAD, s, -jnp.inf)
    p = jnp.exp(s - jnp.max(s, axis=0, keepdims=True))
    return p / jnp.sum(p, axis=0, keepdims=True)


def _fill_padded(dst_ref, src_ref, T):
    dst_ref[0:PAD, :] = jnp.zeros((PAD, 2 * AT_DH), BF16)
    dst_ref[PAD:PAD + T, :] = src_ref[...].astype(BF16)


def _attn_fwd(z, bias_t, name, after=()):
    T = z.shape[0]
    n_chunks = T // CHUNK

    def body(q_ref, k_ref, v_ref, bias_ref, y_ref, *scratch):
        for pr in range(2):
            lanes = slice(128 * pr, 128 * (pr + 1))
            for dst_ref, src_ref in zip(scratch[2 * pr:2 * pr + 2], (k_ref, v_ref)):
                dst_ref[0:PAD, :] = jnp.zeros((PAD, 128), BF16)
                dst_ref[PAD:PAD + T, :] = src_ref[:, lanes].astype(BF16)

        def chunk(c, carry):
            rows = pl.ds(pl.multiple_of(c * CHUNK, CHUNK), CHUNK)
            band = pl.ds(pl.multiple_of(c * CHUNK, CHUNK), BAND)
            for pr in range(2):
                kp_ref, vp_ref = scratch[2 * pr:2 * pr + 2]
                lanes = slice(128 * pr, 128 * (pr + 1))
                p = _band_probs_t(kp_ref[band, :], _block_diag(q_ref[rows, lanes]), bias_ref[pr], c)
                o2 = lax.dot_general(p.astype(BF16), vp_ref[band, :], (TN, ((), ())), preferred_element_type=F32)
                y_ref[rows, lanes] = _diag_blocks(o2).astype(BF16)
            return carry

        lax.fori_loop(0, n_chunks, chunk, 0, unroll=2)

    def col(base):
        return pl.BlockSpec((T, 256), lambda h: (0, base // 2 + h))

    return _call(
        body, (z, z, z, bias_t), name=name, grid=(AT_HEADS // 4,),
        in_specs=[col(COL_AQ), col(COL_AK), col(COL_AV), pl.BlockSpec((2, BAND, 128), lambda h: (h, 0, 0))],
        out_specs=[col(0)],
        out_shape=[jax.ShapeDtypeStruct((T, AT_WIDTH), BF16)],
        scratch_shapes=[pltpu.VMEM((PAD + T, 128), BF16)] * 4,
        sem=("parallel",), after=after)


def _attn_bwd(z, bias_t, dyb, name, after=()):
    T = z.shape[0]
    n_chunks = T // CHUNK

    def body(q_ref, k_ref, v_ref, bias_ref, dy_ref, dq_ref, dk_ref, dv_ref, dbias_ref, *scratch):
        dbias_odd_ref = scratch[-1]
        dbias_ref[...] = jnp.zeros_like(dbias_ref)
        dbias_odd_ref[...] = jnp.zeros_like(dbias_odd_ref)
        for pr in range(2):
            kp_ref, vp_ref = scratch[6 * pr:6 * pr + 2]
            lanes = slice(128 * pr, 128 * (pr + 1))
            kp_ref[0:PAD, :] = jnp.zeros((PAD, 128), BF16)
            vp_ref[0:PAD, :] = jnp.zeros((PAD, 128), BF16)
            kp_ref[PAD:PAD + T, :] = k_ref[:, lanes].astype(BF16)
            vp_ref[PAD:PAD + T, :] = v_ref[:, lanes].astype(BF16)
            for acc_ref in scratch[6 * pr + 2:6 * pr + 6]:
                acc_ref[...] = jnp.zeros_like(acc_ref)

        def chunk(c, odd):
            rows = pl.ds(pl.multiple_of(c * CHUNK, CHUNK), CHUNK)
            band = pl.ds(pl.multiple_of(c * CHUNK, CHUNK), BAND)
            for pr in range(2):
                kp_ref, vp_ref = scratch[6 * pr:6 * pr + 2]
                dkp_ref, dvp_ref = scratch[6 * pr + 2 + 2 * odd:6 * pr + 4 + 2 * odd]
                db_ref = dbias_odd_ref if odd else dbias_ref
                lanes = slice(128 * pr, 128 * (pr + 1))
                qbd = _block_diag(q_ref[rows, lanes])
                dobd = _block_diag(dy_ref[rows, lanes])
                kb = kp_ref[band, :]
                vb = vp_ref[band, :]
                p = _band_probs_t(kb, qbd, bias_ref[pr], c)
                dp = lax.dot_general(vb, dobd, (NT, ((), ())), preferred_element_type=F32)
                ds = p * (dp - jnp.sum(dp * p, axis=0, keepdims=True))
                db_ref[pr] += ds
                dsb = ds.astype(BF16)
                dq2 = lax.dot_general(dsb, kb, (TN, ((), ())), preferred_element_type=F32)
                dq_ref[rows, lanes] = (_diag_blocks(dq2) * (AT_DH ** -0.5)).astype(BF16)
                dkp_ref[band, :] += jnp.dot(dsb, qbd, preferred_element_type=F32) * (AT_DH ** -0.5)
                dvp_ref[band, :] += jnp.dot(p.astype(BF16), dobd, preferred_element_type=F32)

        def two_chunks(i, carry):
            chunk(2 * i, 0)
            chunk(2 * i + 1, 1)
            return carry

        lax.fori_loop(0, n_chunks // 2, two_chunks, 0)
        dbias_ref[...] += dbias_odd_ref[...]
        for pr in range(2):
            lanes = slice(128 * pr, 128 * (pr + 1))
            dkp0, dvp0, dkp1, dvp1 = scratch[6 * pr + 2:6 * pr + 6]
            dk_ref[:, lanes] = (dkp0[PAD:PAD + T, :] + dkp1[PAD:PAD + T, :]).astype(BF16)
            dv_ref[:, lanes] = (dvp0[PAD:PAD + T, :] + dvp1[PAD:PAD + T, :]).astype(BF16)

    def col(base):
        return pl.BlockSpec((T, 256), lambda h: (0, base // 2 + h))

    pairs = pl.BlockSpec((2, BAND, 128), lambda h: (h, 0, 0))
    outb = jax.ShapeDtypeStruct((T, AT_WIDTH), BF16)
    return _call(
        body, (z, z, z, bias_t, dyb), name=name, grid=(AT_HEADS // 4,),
        in_specs=[col(COL_AQ), col(COL_AK), col(COL_AV), pairs, col(0)],
        out_specs=[col(0), col(0), col(0), pairs],
        out_shape=[outb, outb, outb, jax.ShapeDtypeStruct((AT_HEADS // 2, BAND, 128), F32)],
        scratch_shapes=([pltpu.VMEM((PAD + T, 128), BF16)] * 2 + [pltpu.VMEM((PAD + T, 128), F32)] * 4) * 2
        + [pltpu.VMEM((2, BAND, 128), F32)],
        sem=("parallel",), after=after)


def _local_step(x, target, lb_logits, hg_norm_w, rel_bias, norm_mix_w, norm_mlp_w, norm_final_w,
                w_in, rest, exchanges=None):
    ex = exchanges
    rel = jnp.pad(rel_bias, ((0, 0), (0, N_REL_PAD - N_REL)))

    u = _rms_fwd(x, norm_mix_w, "rms_mix_fwd")
    if ex:
        z, w_in = _mm_gathered(u, w_in, ex.order, "mm_in_fwd")
        gather = _Gather(rest, [z], "ag")
        tok = [gather.token]
    else:
        z = _mm_nn(u, w_in, F32, "mm_in_fwd")
        w_a, w_b, w_out, w_up, w_down = rest
        tok = []
    o_raw, y_a, s_all = _hgrn2_fwd(z, lb_logits, hg_norm_w, "hgrn2_fwd", after=tok)
    if ex:
        tok = [gather.pass_on([0, 1, 2], [o_raw], "abo")]
    bias_rows = _bias_expand(rel, "bias_expand")
    bias_t = jnp.transpose(bias_rows.reshape(CHUNK, AT_HEADS // 2, 2, BAND), (1, 3, 2, 0)).reshape(
        AT_HEADS // 2, BAND, 2 * CHUNK)
    y_b, = _attn_fwd(z, bias_t, "attn_fwd", after=tok)
    if ex:
        tok = [gather.pass_on([3], [y_b], "up")]
        w_a, w_b, w_out = gather.finish([0, 1, 2], tok, "abo")
    pa = _mm_nn(y_a, w_a, F32, "mm_a_fwd")
    pb = _mm_nn(y_b, w_b, F32, "mm_b_fwd")
    merged = _merge_fwd(z, pa, pb, "merge_fwd")
    w_out1 = w_out.reshape(1, D_MODEL, D_MODEL)
    mix = _mm_nn(merged, w_out1, F32, "mm_out_fwd")
    h1, u2 = _resid_rms_fwd(x, mix, norm_mlp_w, "rms_mlp_fwd")
    if ex:
        tok = [gather.pass_on([4], [u2], "down")]
        w_up, = gather.finish([3], tok, "up")
    a, r = _mm_nn(u2, w_up, F32, "mm_up_fwd", squared_relu=True)
    if ex:
        w_down, = gather.finish([4], [r], "down")
    w_down1 = w_down.reshape(1, D_FF, D_MODEL)
    mlp = _mm_nn(r, w_down1, F32, "mm_down_fwd")
    loss, dh2, dh2b, g_nf = _loss_head(h1, mlp, norm_final_w, target, "loss_head")

    def reduce_scatter(grads, name):
        rs = _ReduceScatter(grads, ex.parity, name) if ex else None
        return rs, ([rs.token] if ex else [])

    g_down = _mm_tn(r, dh2b, 1, BF16, "mm_down_wgrad").reshape(N_DEV, D_FF // N_DEV, D_MODEL)
    rs_down, tok = reduce_scatter([g_down], "rs_down")
    da, = _mm_nt(dh2b, w_down1, None, "mm_down_dgrad", after=tok, epilogue=(
        (a,), (0,), (BF16,), lambda dr, av: (dr * (2.0 * jnp.maximum(av, 0.0)),)))
    tok = [rs_down.pair_sums([da])] if ex else []
    g_up = _mm_tn(u2, da, N_DEV, BF16, "mm_up_wgrad", after=tok)
    rs_up, tok = reduce_scatter([g_up], "rs_up")
    du2 = _mm_nt(da, w_up, F32, "mm_up_dgrad", after=tok)
    tok = [rs_up.pair_sums([du2])] if ex else []
    dh1, dh1b, g_nmlp = _rms_bwd(du2, h1, norm_mlp_w, dh2, "rms_mlp_bwd", after=tok)

    g_out = _mm_tn(merged, dh1b, 1, BF16, "mm_out_wgrad").reshape(N_DEV, D_MODEL // N_DEV, D_MODEL)
    dpa, dpb, dga, dgb = _mm_nt(dh1b, w_out1, None, "mm_out_dgrad", epilogue=(
        (z, z, pa, pb), (COL_GATE_A * GATE_TILE, COL_GATE_B * GATE_TILE, 0, 0), (BF16,) * 4, _merge_grads))
    g_a = _mm_tn(y_a, dpa, N_DEV, BF16, "mm_a_wgrad")
    g_b = _mm_tn(y_b, dpb, N_DEV, BF16, "mm_b_wgrad")
    rs_mix, tok = reduce_scatter([g_a, g_b, g_out], "rs_mix")
    dya = _mm_nt(dpa, w_a, F32, "mm_a_dgrad", after=tok)
    dyb = _mm_nt(dpb, w_b, F32, "mm_b_dgrad", after=tok)
    tok = [rs_mix.pair_sums([dya, dyb])] if ex else []
    daq, dak, dav, dbias_t = _attn_bwd(z, bias_t, dyb, "attn_bwd", after=tok)
    dhq, dhf, dhi, dhg, g_lbl, g_hgw = _hgrn2_bwd(z, lb_logits, hg_norm_w, o_raw, s_all, dya, "hgrn2_bwd",
                                                  after=tok)
    dbias_rows = jnp.transpose(dbias_t.reshape(AT_HEADS // 2, BAND, 2, CHUNK), (3, 0, 2, 1)).reshape(
        CHUNK, AT_HEADS, BAND)
    dz =jnp.concatenate([dhq, dhf, dhi, dhg, daq, dak, dav, dga, dgb], axis=1)
    g_in = _mm_tn(u, dz, N_DEV, BF16, "mm_in_wgrad")
    rs_in, _ = reduce_scatter([g_in], "rs_in")
    tok = [rs_in.pair_sums([])] if ex else []
    du = _mm_nt(dz, w_in, F32, "mm_in_dgrad", after=tok)
    grad_x, _, g_nmix = _rms_bwd(du, x, norm_mix_w, dh1, "rms_mix_bwd")
    g_rel = _bias_reduce(dbias_rows, "bias_reduce", after=tok)[:, :N_REL]

    small = dict(lb_logits=g_lbl, hg_norm_w=g_hgw[0:1], rel_bias=g_rel, norm_mix_w=g_nmix, norm_mlp_w=g_nmlp,
                 norm_final_w=g_nf)
    grads = [rs_in, rs_mix, rs_up, rs_down] if ex else [g_in, g_a, g_b, g_out, g_up, g_down]
    return loss, grad_x, grads, small


def _gather_exchange(shards, mid_step=None):
    n = len(shards)

    def parts(ins, outs, sems):
        send_sems, recv_sems, local_sems = sems
        x, y, c = _position()
        chips = [(1 - x, y), (x, 1 - y), (1 - x, 1 - y)]

        def copy(w, k, block, to, src=None):
            dst = outs[w].at[4 * block[0] + 2 * block[1] + block[2]]
            return pltpu.make_async_remote_copy(
                src_ref=dst if src is None else src, dst_ref=dst,
                send_sem=send_sems.at[w, k], recv_sem=recv_sems.at[w, k], device_id=to, device_id_type=MESH)

        def local(w):
            return pltpu.make_async_copy(ins[w], outs[w].at[4 * x + 2 * y + c], local_sems.at[w])

        return (x, y, c), (x, y, 1 - c), chips, copy, local

    def start(ins, outs, sems):
        me, sibling, chips, copy, local = parts(ins, outs, sems)
        for w in range(n):
            local(w).start()
        for w in range(n):
            copy(w, 0, me, sibling, src=ins[w]).start()
            for j, chip in enumerate(chips):
                copy(w, 1 + j, me, (*chip, me[2]), src=ins[w]).start()

    def mid(ins, outs, sems):
        me, sibling, chips, copy, _ = parts(ins, outs, sems)
        for w in range(n):
            for j, chip in enumerate(chips):
                copy(w, 1 + j, (*chip, me[2]), me).wait_recv()
                copy(w, 4 + j, (*chip, me[2]), sibling).start()

    def end(ins, outs, sems):
        me, sibling, chips, copy, local = parts(ins, outs, sems)
        for w in range(n):
            copy(w, 0, sibling, me).wait_recv()
            for j, chip in enumerate(chips):
                copy(w, 4 + j, (*chip, sibling[2]), me).wait_recv()
        for w in range(n):
            for k in range(7):
                copy(w, k, me, sibling).wait_send()
            local(w).wait()

    return _Exchange(
        shards, [jax.ShapeDtypeStruct((N_DEV,) + s.shape, s.dtype) for s in shards],
        [pltpu.SemaphoreType.DMA((n, 7)), pltpu.SemaphoreType.DMA((n, 7)), pltpu.SemaphoreType.DMA((n,))],
        start, end, mid, mid_step)


def _mm_gathered(u, shard, order, name):
    T, K = u.shape
    _, Nb = shard.shape

    def body(order_ref, u_ref, shard_ref, z_ref, full_ref, wbuf, load_sem, send_sems, recv_sems, local_sem):
        s = pl.program_id(0)
        x, y, c = _position()
        me, sibling = (x, y, c), (x, y, 1 - c)
        chips = [(1 - x, y), (x, 1 - y), (1 - x, 1 - y)]

        def copy(k, block, to, src=None):
            dst = full_ref.at[4 * block[0] + 2 * block[1] + block[2]]
            return pltpu.make_async_remote_copy(
                src_ref=dst if src is None else src, dst_ref=dst,
                send_sem=send_sems.at[k], recv_sem=recv_sems.at[k], device_id=to, device_id_type=MESH)

        @pl.when(s == 0)
        def _():
            local = pltpu.make_async_copy(shard_ref, full_ref.at[4 * x + 2 * y + c], local_sem)
            local.start()
            copy(0, me, sibling, src=shard_ref).start()
            for j, chip in enumerate(chips):
                copy(1 + j, me, (*chip, c), src=shard_ref).start()
            local.wait()

        @pl.when(s == 1)
        def _():
            copy(0, sibling, me).wait_recv()

        for j, chip in enumerate(chips):
            direct, passed = ((2, 4), (3, 5), (6, 7))[j]

            @pl.when(s == direct)
            def _(j=j, chip=chip):
                copy(1 + j, (*chip, c), me).wait_recv()
                copy(4 + j, (*chip, c), sibling).start()

            @pl.when(s == passed)
            def _(j=j, chip=chip):
                copy(4 + j, (*chip, 1 - c), me).wait_recv()

        load = pltpu.make_async_copy(full_ref.at[order_ref[s]], wbuf, load_sem)
        load.start()
        load.wait()
        z_ref[...] = jnp.dot(u_ref[...], wbuf[...], preferred_element_type=F32)

        @pl.when(s == N_DEV - 1)
        def _():
            for k in range(7):
                copy(k, me, sibling).wait_send()

    return pl.pallas_call(
        body, name=name,
        grid_spec=pltpu.PrefetchScalarGridSpec(
            num_scalar_prefetch=1, grid=(N_DEV,),
            in_specs=[pl.BlockSpec((T, K), lambda s, order: (0, 0)), ANY],
            out_specs=[pl.BlockSpec((T, Nb), lambda s, order: (0, order[s])), ANY],
            scratch_shapes=[pltpu.VMEM((K, Nb), BF16), pltpu.SemaphoreType.DMA,
                            pltpu.SemaphoreType.DMA((7,)), pltpu.SemaphoreType.DMA((7,)), pltpu.SemaphoreType.DMA]),
        out_shape=[jax.ShapeDtypeStruct((T, N_DEV * Nb), F32), jax.ShapeDtypeStruct((N_DEV, K, Nb), BF16)],
        compiler_params=_cparams(("arbitrary",)),
    )(order, u, shard)


def _gather_order():
    x, y, c = _position()
    chips = [(1 - x, y), (x, 1 - y), (1 - x, 1 - y)]
    ids = [4 * x + 2 * y + c, 4 * x + 2 * y + (1 - c)]
    ids += [4 * cx + 2 * cy + c for cx, cy in chips[:2]] + [4 * cx + 2 * cy + (1 - c) for cx, cy in chips[:2]]
    ids += [4 * chips[2][0] + 2 * chips[2][1] + c, 4 * chips[2][0] + 2 * chips[2][1] + (1 - c)]
    return jnp.stack(ids).astype(jnp.int32)


def _run_exchange(comm, name):
    n_i, n_o = len(comm.arrays), len(comm.out_shape)

    def body(*refs):
        ins, outs, sems = refs[:n_i], refs[n_i:n_i + n_o], refs[n_i + n_o:]
        comm.start(ins, outs, sems)
        if comm.mid is not None:
            comm.mid(ins, outs, sems)
        comm.end(ins, outs, sems)

    return pl.pallas_call(
        body, name=name, in_specs=[ANY] * n_i, out_specs=[ANY] * n_o, out_shape=comm.out_shape,
        scratch_shapes=comm.scratch)(*comm.arrays)


def _exchange_sibling(grads, name):
    n = len(grads)

    def body(*refs):
        ins, outs = refs[:n], refs[n:2 * n]
        send_sems, recv_sems = refs[2 * n:]
        x, y, c = _position()
        copies = []
        for w in range(n):
            for s in range(N_CHIP):
                cp = pltpu.make_async_remote_copy(
                    src_ref=ins[w].at[2 * s + (1 - c)], dst_ref=outs[w].at[s],
                    send_sem=send_sems.at[w, s], recv_sem=recv_sems.at[w, s],
                    device_id=(x, y, 1 - c), device_id_type=MESH)
                cp.start()
                copies.append(cp)
        for cp in copies:
            cp.wait()

    return pl.pallas_call(
        body, name=name,
        in_specs=[ANY] * n, out_specs=[ANY] * n,
        out_shape=[jax.ShapeDtypeStruct((N_CHIP,) + g.shape[1:], g.dtype) for g in grads],
        scratch_shapes=[pltpu.SemaphoreType.DMA((n, N_CHIP)), pltpu.SemaphoreType.DMA((n, N_CHIP))],
    )(*grads)


def _pair_sum(g, land, parity, name):
    _, R, C = g.shape
    tr = _pick(R, (512, 256))

    def body(par_ref, g_ref, l_ref, o_ref):
        o_ref[...] = (g_ref[...].astype(F32) + l_ref[...].astype(F32)).astype(BF16)

    return pl.pallas_call(
        body, name=name,
        grid_spec=pltpu.PrefetchScalarGridSpec(
            num_scalar_prefetch=1, grid=(N_CHIP, R // tr),
            in_specs=[pl.BlockSpec((None, tr, C), lambda s, i, par: (2 * s + par[0], i, 0)),
                      pl.BlockSpec((None, tr, C), lambda s, i, par: (s, i, 0))],
            out_specs=pl.BlockSpec((None, tr, C), lambda s, i, par: (s, i, 0))),
        out_shape=jax.ShapeDtypeStruct((N_CHIP, R, C), BF16),
        compiler_params=_cparams(("parallel", "parallel")),
    )(parity, g, land)


def _scatter_exchange(partials):
    n = len(partials)

    def copies(ins, outs, sems):
        send_sems, recv_sems, local_sems = sems
        x, y, c = _position()
        chips = [(1 - x, y), (x, 1 - y), (1 - x, 1 - y)]
        my_slot = 2 * x + y
        local = [pltpu.make_async_copy(ins[w].at[my_slot], outs[w].at[my_slot], local_sems.at[w]) for w in range(n)]
        remote = [pltpu.make_async_remote_copy(
            src_ref=ins[w].at[2 * chip[0] + chip[1]], dst_ref=outs[w].at[my_slot],
            send_sem=send_sems.at[w, j], recv_sem=recv_sems.at[w, j], device_id=(*chip, c), device_id_type=MESH)
            for w in range(n) for j, chip in enumerate(chips)]
        return local, remote

    def start(ins, outs, sems):
        local, remote = copies(ins, outs, sems)
        for cp in local + remote:
            cp.start()

    def end(ins, outs, sems):
        local, remote = copies(ins, outs, sems)
        for cp in remote + local:
            cp.wait()

    return _Exchange(
        partials, [jax.ShapeDtypeStruct(p.shape, p.dtype) for p in partials],
        [pltpu.SemaphoreType.DMA((n, 3)), pltpu.SemaphoreType.DMA((n, 3)), pltpu.SemaphoreType.DMA((n,))],
        start, end)


HBM = pl.BlockSpec(memory_space=pltpu.HBM)
SEM = pl.BlockSpec(memory_space=pltpu.SEMAPHORE)
DATAFLOW = pltpu.SideEffectType.DATAFLOW_SIDE_EFFECTING


def _scatter_copies(ins, lands, send_sems, recv_sems):
    x, y, c = _position()
    chips = [(1 - x, y), (x, 1 - y), (1 - x, 1 - y)]
    return [pltpu.make_async_remote_copy(
        src_ref=ins[w].at[2 * chip[0] + chip[1]], dst_ref=lands[w].at[2 * x + y],
        send_sem=send_sems[3 * w + j], recv_sem=recv_sems[3 * w + j], device_id=(*chip, c), device_id_type=MESH)
        for w in range(len(ins)) for j, chip in enumerate(chips)]


def _scatter_start(partials, name):
    n = len(partials)

    def body(*refs):
        ins, lands = refs[:n], refs[n:2 * n]
        sems = refs[4 * n:10 * n]
        for cp in _scatter_copies(ins, lands, sems[:3 * n], sems[3 * n:]):
            cp.start()
        refs[-1][...] = jnp.zeros_like(refs[-1])

    def in_hbm(a):
        return pltpu.with_memory_space_constraint(a, pltpu.HBM)

    bufs = tuple(pltpu.HBM(p.shape, p.dtype) for p in partials)
    outs = pl.pallas_call(
        body, name=name,
        out_shape=bufs + bufs + (pltpu.SemaphoreType.DMA(()),) * (6 * n) + (jax.ShapeDtypeStruct((8, 128), F32),),
        in_specs=[HBM] * (2 * n),
        out_specs=(HBM,) * (2 * n) + (SEM,) * (6 * n) + (pl.BlockSpec(memory_space=pltpu.VMEM),),
        input_output_aliases={i: i for i in range(2 * n)},
        compiler_params=pltpu.CompilerParams(has_side_effects=DATAFLOW),
    )(*[in_hbm(p) for p in partials], *[in_hbm(lax.empty(p.shape, p.dtype)) for p in partials])
    return list(outs[:-1]), outs[-1]


def _scatter_wait(handle, after, name):
    n = len(handle) // 8
    bufs, sems = handle[:2 * n], handle[2 * n:]

    def body(*refs):
        ins, lands = refs[:n], refs[n:2 * n]
        sems = refs[2 * n:8 * n]
        for cp in _scatter_copies(ins, lands, sems[:3 * n], sems[3 * n:]):
            cp.wait_send()
            cp.wait_recv()

    outs = pl.pallas_call(
        body, name=name,
        out_shape=tuple(pltpu.HBM(b.shape, b.dtype) for b in bufs),
        in_specs=[HBM] * (2 * n) + [SEM] * (6 * n) + [ANY] * len(after), out_specs=(HBM,) * (2 * n),
        input_output_aliases={i: i for i in range(2 * n)},
        compiler_params=pltpu.CompilerParams(has_side_effects=DATAFLOW),
    )(*bufs, *sems, *after)
    return list(outs[:n]), list(outs[n:])


def _split_call(name, bufs, waits=(), starts=None, after=()):
    nb = len(bufs)
    n_new = starts[1] if starts else 0
    wait_sems = [s for w in waits for s in (*w[1], *w[2])]

    def body(*refs):
        b, pos = refs[:nb], nb
        for plan, ss, _, send_idx, recv_idx in waits:
            k = len(ss)
            copies = plan(b, refs[pos:pos + k], refs[pos + k:pos + 2 * k])
            pos += 2 * k
            for i in recv_idx:
                copies[i].wait_recv()
            for i in send_idx:
                copies[i].wait_send()
        outs = refs[pos + len(after):]
        if starts:
            for cp in starts[0](b, outs[nb:nb + n_new], outs[nb + n_new:nb + 2 * n_new]):
                cp.start()
        outs[-1][...] = jnp.zeros_like(outs[-1])

    res = pl.pallas_call(
        body, name=name,
        out_shape=tuple(pltpu.HBM(a.shape, a.dtype) for a in bufs) + (pltpu.SemaphoreType.DMA(()),) * (2 * n_new)
        + (jax.ShapeDtypeStruct((8, 128), F32),),
        in_specs=[HBM] * nb + [SEM] * len(wait_sems) + [ANY] * len(after),
        out_specs=(HBM,) * nb + (SEM,) * (2 * n_new) + (pl.BlockSpec(memory_space=pltpu.VMEM),),
        input_output_aliases={i: i for i in range(nb)},
        compiler_params=pltpu.CompilerParams(has_side_effects=DATAFLOW),
    )(*bufs, *wait_sems, *after)
    return list(res[:nb]), list(res[nb:nb + n_new]), list(res[nb + n_new:nb + 2 * n_new]), res[-1]


def _in_hbm(a):
    return pltpu.with_memory_space_constraint(a, pltpu.HBM)


def _remote(src, dst, send_sem, recv_sem, to):
    return pltpu.make_async_remote_copy(src_ref=src, dst_ref=dst, send_sem=send_sem, recv_sem=recv_sem,
                                        device_id=to, device_id_type=MESH)


def _other_chips():
    x, y, _ = _position()
    return [(1 - x, y), (x, 1 - y), (1 - x, 1 - y)]


def _plan_gather_first(n):
    def plan(b, ss, rs):
        x, y, c = _position()
        to = [(x, y, 1 - c)] + [(*chip, c) for chip in _other_chips()]
        return [_remote(b[w], b[n + w].at[4 * x + 2 * y + c], ss[4 * w + k], rs[4 * w + k], to[k])
                for w in range(n) for k in range(4)]
    return plan, 4 * n


def _plan_gather_pass(n):
    def plan(b, ss, rs):
        x, y, c = _position()
        copies = []
        for w in range(n):
            for j, chip in enumerate(_other_chips()):
                blk = b[n + w].at[4 * chip[0] + 2 * chip[1] + c]
                copies.append(_remote(blk, blk, ss[3 * w + j], rs[3 * w + j], (x, y, 1 - c)))
        return copies
    return plan, 3 * n


def _plan_sibling(n):
    def plan(b, ss, rs):
        x, y, c = _position()
        return [_remote(b[w].at[2 * s + (1 - c)], b[n + w].at[s], ss[4 * w + s], rs[4 * w + s], (x, y, 1 - c))
                for w in range(n) for s in range(N_CHIP)]
    return plan, 4 * n


def _plan_scatter(n):
    def plan(b, ss, rs):
        x, y, c = _position()
        return [_remote(b[w].at[2 * chip[0] + chip[1]], b[n + w].at[2 * x + y], ss[3 * w + j], rs[3 * w + j],
                        (*chip, c))
                for w in range(n) for j, chip in enumerate(_other_chips())]
    return plan, 3 * n


class _Gather:
    def __init__(self, shards, after, name):
        self.n, self.name = len(shards), name
        x, y, c = _position()
        placed = [lax.dynamic_update_index_in_dim(lax.empty((N_DEV,) + s.shape, s.dtype), s, 4 * x + 2 * y + c, 0)
                  for s in shards]
        bufs, self.ss, self.rs, self.token = _split_call(
            name + "_start", [_in_hbm(a) for a in list(shards) + placed], starts=_plan_gather_first(self.n),
            after=after)
        self.shards, self.fulls = bufs[:self.n], bufs[self.n:]
        self.passed = {}

    def _sub(self, ids, sems, per):
        return [sems[per * w + k] for w in ids for k in range(per)]

    def pass_on(self, ids, after, tag):
        m = len(ids)
        first = (_plan_gather_first(m)[0], self._sub(ids, self.ss, 4), self._sub(ids, self.rs, 4),
                 [], [4 * i + k for i in range(m) for k in (1, 2, 3)])
        bufs, ss, rs, token = _split_call(
            "%s_pass_%s" % (self.name, tag), [self.shards[w] for w in ids] + [self.fulls[w] for w in ids],
            waits=[first], starts=_plan_gather_pass(m), after=after)
        for i, w in enumerate(ids):
            self.shards[w], self.fulls[w] = bufs[i], bufs[m + i]
        self.passed[tuple(ids)] = (ss, rs)
        return token

    def finish(self, ids, after, tag):
        m = len(ids)
        ss2, rs2 = self.passed[tuple(ids)]
        first = (_plan_gather_first(m)[0], self._sub(ids, self.ss, 4), self._sub(ids, self.rs, 4),
                 list(range(4 * m)), [4 * i for i in range(m)])
        passed = (_plan_gather_pass(m)[0], ss2, rs2, list(range(3 * m)), list(range(3 * m)))
        bufs, _, _, _ = _split_call(
            "%s_finish_%s" % (self.name, tag), [self.shards[w] for w in ids] + [self.fulls[w] for w in ids],
            waits=[first, passed], after=after)
        return bufs[m:]


class _ReduceScatter:
    def __init__(self, grads, parity, name):
        self.n, self.name, self.parity = len(grads), name, parity
        lands = [lax.empty((N_CHIP,) + g.shape[1:], g.dtype) for g in grads]
        self.bufs, self.ss, self.rs, self.token = _split_call(
            name + "_sibling_start", [_in_hbm(a) for a in list(grads) + lands], starts=_plan_sibling(self.n))

    def pair_sums(self, after):
        n = self.n
        bufs, _, _, _ = _split_call(
            self.name + "_sibling_wait", self.bufs,
            waits=[(_plan_sibling(n)[0], self.ss, self.rs, list(range(4 * n)), list(range(4 * n)))], after=after)
        sums = [_pair_sum(bufs[w], bufs[n + w], self.parity, "%s_pair_sum_%d" % (self.name, w)) for w in range(n)]
        lands = [lax.empty(s.shape, s.dtype) for s in sums]
        self.bufs, self.ss, self.rs, token = _split_call(
            self.name + "_scatter_start", [_in_hbm(a) for a in sums + lands], starts=_plan_scatter(n))
        return token

    def finish(self, after):
        n = self.n
        bufs, _, _, _ = _split_call(
            self.name + "_scatter_wait", self.bufs,
            waits=[(_plan_scatter(n)[0], self.ss, self.rs, list(range(3 * n)), list(range(3 * n)))], after=after)
        return bufs[:n], bufs[n:]


class _Exchanges:
    def __init__(self, parity, order):
        self.parity, self.order = parity, order


def _gather_small(packed, name):
    R = packed.shape[0]

    def body(x_ref, out_ref, send_sems, recv_sems):
        x, y, c = _position()
        me = 4 * x + 2 * y + c
        out_ref[me] = x_ref[...]
        copies = []
        for k in range(1, N_DEV):
            to = (x ^ ((k >> 2) & 1), y ^ ((k >> 1) & 1), c ^ (k & 1))
            cp = pltpu.make_async_remote_copy(
                src_ref=x_ref, dst_ref=out_ref.at[me],
                send_sem=send_sems.at[k], recv_sem=recv_sems.at[k], device_id=to, device_id_type=MESH)
            cp.start()
            copies.append((k, to, cp))
        for k, to, cp in copies:
            cp.wait_send()
            pltpu.make_async_remote_copy(
                src_ref=x_ref, dst_ref=out_ref.at[4 * to[0] + 2 * to[1] + to[2]],
                send_sem=send_sems.at[k], recv_sem=recv_sems.at[k], device_id=to, device_id_type=MESH).wait_recv()

    return pl.pallas_call(
        body, name=name,
        in_specs=[pl.BlockSpec(memory_space=pltpu.VMEM)], out_specs=pl.BlockSpec(memory_space=pltpu.VMEM),
        out_shape=jax.ShapeDtypeStruct((N_DEV, R, 128), F32),
        scratch_shapes=[pltpu.SemaphoreType.DMA((N_DEV,)), pltpu.SemaphoreType.DMA((N_DEV,))],
    )(packed)


def _adamw_math(w, g, m, v):
    m = ADAM_B1 * m + (1.0 - ADAM_B1) * g
    v = ADAM_B2 * v + (1.0 - ADAM_B2) * (g * g)
    m_hat = m / (1.0 - ADAM_B1 ** ADAM_STEP)
    v_hat = v / (1.0 - ADAM_B2 ** ADAM_STEP)
    delta = -ADAM_LR * (m_hat / (jnp.sqrt(v_hat) + ADAM_EPS) + ADAM_WD * w)
    return delta, m, v


def _adamw_big(w, m, v, parts, name):
    R, C = w.shape
    tr = _pick(R, (256,))

    def body(w_ref, m_ref, v_ref, p_ref, g_ref, d_ref, nm_ref, nv_ref):
        g = p_ref[0].astype(F32)
        for s in range(1, N_CHIP):
            g = g + p_ref[s].astype(F32)
        d, nm, nv = _adamw_math(w_ref[...], g, m_ref[...], v_ref[...])
        g_ref[...] = g
        d_ref[...] = d
        nm_ref[...] = nm
        nv_ref[...] = nv

    blk = pl.BlockSpec((tr, C), lambda i: (i, 0))
    out = jax.ShapeDtypeStruct((R, C), F32)
    return pl.pallas_call(
        body, name=name, grid=(R // tr,),
        in_specs=[blk, blk, blk, pl.BlockSpec((N_CHIP, tr, C), lambda i: (0, i, 0))],
        out_specs=[blk, blk, blk, blk], out_shape=[out, out, out, out],
        compiler_params=_cparams(("parallel",)),
    )(w, m, v, parts)


def _adamw_big_landed(w, m, v, parts, lands, slot, name):
    R, C = w.shape
    tr = _pick(R, (256,))

    def body(slot_ref, w_ref, m_ref, v_ref, own_ref, l1_ref, l2_ref, l3_ref, g_ref, d_ref, nm_ref, nv_ref):
        g = own_ref[...].astype(F32)
        for ref in (l1_ref, l2_ref, l3_ref):
            g = g + ref[...].astype(F32)
        d, nm, nv = _adamw_math(w_ref[...], g, m_ref[...], v_ref[...])
        g_ref[...] = g
        d_ref[...] = d
        nm_ref[...] = nm
        nv_ref[...] = nv

    blk = pl.BlockSpec((tr, C), lambda i, slot: (i, 0))

    def chip(k):
        return pl.BlockSpec((None, tr, C), lambda i, slot: ((slot[0] + k) % N_CHIP, i, 0))

    out = jax.ShapeDtypeStruct((R, C), F32)
    return pl.pallas_call(
        body, name=name,
        grid_spec=pltpu.PrefetchScalarGridSpec(
            num_scalar_prefetch=1, grid=(R // tr,),
            in_specs=[blk, blk, blk, chip(0), chip(1), chip(2), chip(3)],
            out_specs=[blk, blk, blk, blk]),
        out_shape=[out, out, out, out],
        compiler_params=_cparams(("parallel",)),
    )(slot, w, m, v, parts, lands, lands, lands)


def _adamw_small(w, m, v, gathered, name):
    R = w.shape[0]

    def body(w_ref, m_ref, v_ref, p_ref, g_ref, d_ref, nm_ref, nv_ref):
        g = p_ref[0]
        for s in range(1, N_DEV):
            g = g + p_ref[s]
        d, nm, nv = _adamw_math(w_ref[...], g, m_ref[...], v_ref[...])
        g_ref[...] = g
        d_ref[...] = d
        nm_ref[...] = nm
        nv_ref[...] = nv

    out = jax.ShapeDtypeStruct((R, 128), F32)
    return pl.pallas_call(
        body, name=name, out_shape=[out, out, out, out],
    )(w, m, v, gathered)


SMALL_NAMES = ("lb_logits", "hg_norm_w", "rel_bias", "norm_mix_w", "norm_mlp_w", "norm_final_w")
SMALL_SHAPES = {"lb_logits": (2, HG_WIDTH), "hg_norm_w": (1, HG_DK), "rel_bias": (AT_HEADS, N_REL_PAD),
                "norm_mix_w": (1, D_MODEL), "norm_mlp_w": (1, D_MODEL), "norm_final_w": (1, D_MODEL)}


def _pack_small(parts):
    rows = []
    for nme in SMALL_NAMES:
        p = parts[nme]
        if nme == "rel_bias":
            p = jnp.pad(p, ((0, 0), (0, N_REL_PAD - N_REL)))
        rows.append(p.reshape(-1, 128))
    flat = jnp.concatenate(rows, axis=0)
    return jnp.pad(flat, ((0, SMALL_ROWS - flat.shape[0]), (0, 0)))


def _unpack_small(packed):
    out, at = {}, 0
    for nme in SMALL_NAMES:
        shp = SMALL_SHAPES[nme]
        nrow = shp[0] * shp[1] // 128
        p = packed[at:at + nrow].reshape(shp)
        at += nrow
        out[nme] = p[:, :N_REL] if nme == "rel_bias" else p
    return out


BIG_NAMES = ("w_in", "w_branch_a", "w_branch_b", "w_out", "w_up", "w_down")


def kernel(x, w_in, lb_logits, hg_norm_w, rel_bias, w_branch_a, w_branch_b, w_out, norm_mix_w, norm_mlp_w, w_up, w_down, norm_final_w, loss_target, m_w_in, m_lb_logits, m_hg_norm_w, m_rel_bias, m_w_branch_a, m_w_branch_b, m_w_out, m_norm_mix_w, m_norm_mlp_w, m_w_up, m_w_down, m_norm_final_w, v_w_in, v_lb_logits, v_hg_norm_w, v_rel_bias, v_w_branch_a, v_w_branch_b, v_w_out, v_norm_mix_w, v_norm_mlp_w, v_w_up, v_w_down, v_norm_final_w):
    big_w = [w_in[0], w_branch_a[0], w_branch_b[0], w_out[0], w_up[0], w_down[0]]
    big_m = [m_w_in[0], m_w_branch_a[0], m_w_branch_b[0], m_w_out[0], m_w_up[0], m_w_down[0]]
    big_v = [v_w_in[0], v_w_branch_a[0], v_w_branch_b[0], v_w_out[0], v_w_up[0], v_w_down[0]]

    shards = [w.astype(BF16) for w in big_w]
    parity = lax.axis_index("c").astype(jnp.int32).reshape(1)
    loss_part, grad_x, chip_parts, small = _local_step(
        x[0], loss_target[0], lb_logits, hg_norm_w, rel_bias[0], norm_mix_w, norm_mlp_w,
        norm_final_w.reshape(1, D_MODEL), shards[0], shards[1:], _Exchanges(parity, _gather_order()))
    loss = lax.psum(loss_part[0, 0], ("x", "y", "c"))
    rs_in, rs_mix, rs_up, rs_down = chip_parts
    slot = (2 * lax.axis_index("x") + lax.axis_index("y")).astype(jnp.int32).reshape(1)
    big = {}

    def finish(rs, names, after):
        sums, lands = rs.finish(after)
        for nme, own, land in zip(names, sums, lands):
            i = BIG_NAMES.index(nme)
            big[nme] = _adamw_big_landed(big_w[i], big_m[i], big_v[i], own, land, slot, "adamw_" + nme)
        return [big[nme][1] for nme in names]

    done = finish(rs_down, ["w_down"], [grad_x])
    done = finish(rs_up, ["w_up"], done)
    done = finish(rs_mix, ["w_branch_a", "w_branch_b", "w_out"], done)

    sw = dict(lb_logits=lb_logits, hg_norm_w=hg_norm_w, rel_bias=rel_bias[0], norm_mix_w=norm_mix_w,
              norm_mlp_w=norm_mlp_w, norm_final_w=norm_final_w.reshape(1, D_MODEL))
    sm = dict(lb_logits=m_lb_logits, hg_norm_w=m_hg_norm_w, rel_bias=m_rel_bias[0], norm_mix_w=m_norm_mix_w,
              norm_mlp_w=m_norm_mlp_w, norm_final_w=m_norm_final_w.reshape(1, D_MODEL))
    sv = dict(lb_logits=v_lb_logits, hg_norm_w=v_hg_norm_w, rel_bias=v_rel_bias[0], norm_mix_w=v_norm_mix_w,
              norm_mlp_w=v_norm_mlp_w, norm_final_w=v_norm_final_w.reshape(1, D_MODEL))
    gathered = _gather_small(_pack_small(small), "gather_small")
    small_packed = _adamw_small(_pack_small(sw), _pack_small(sm), _pack_small(sv), gathered, "adamw_small")
    small_out = [_unpack_small(p) for p in small_packed]

    finish(rs_in, ["w_in"], done + [small_packed[0]])

    def leaf(kind, nme):
        if nme in BIG_NAMES:
            return big[nme][kind][None]
        p = small_out[kind][nme]
        if nme == "rel_bias":
            return p[None]
        if nme == "norm_final_w":
            return p.reshape(D_MODEL)
        return p

    order = ("w_in", "lb_logits", "hg_norm_w", "rel_bias", "w_branch_a", "w_branch_b", "w_out", "norm_mix_w",
             "norm_mlp_w", "w_up", "w_down", "norm_final_w")
    outs = [loss, grad_x[None]]
    for kind in range(4):
        outs += [leaf(kind, nme) for nme in order]
    return tuple(outs)
```

```python
import functools

import jax
import jax.numpy as jnp
from jax import lax
from jax.experimental import pallas as pl
from jax.experimental.pallas import tpu as pltpu

F32 = jnp.float32
BF16 = jnp.bfloat16
HIGHEST = lax.Precision.HIGHEST
MESH = pl.DeviceIdType.MESH

D_MODEL = 2048
HG_HEADS = 8
HG_DK = 128
HG_WIDTH = 1024
AT_HEADS = 16
AT_DH = 64
AT_WIDTH = 1024
CHUNK = 64
LEFT_CHUNKS = 8
BAND = (LEFT_CHUNKS + 1) * CHUNK
PAD = LEFT_CHUNKS * CHUNK
REL_CLIP = 256
N_REL = 2 * REL_CLIP + 1
N_REL_PAD = 640
D_FF = 4 * D_MODEL
D_IN = 4 * HG_WIDTH + 3 * AT_WIDTH + 2 * D_MODEL
EPS = 1e-6
N_DEV = 8
N_CHIP = 4

ADAM_LR = 0.001
ADAM_B1 = 0.9
ADAM_B2 = 0.999
ADAM_EPS = 1e-08
ADAM_WD = 0.01
ADAM_STEP = 10

COL_HQ, COL_HF, COL_HI, COL_HG = 0, 8, 16, 24
COL_AQ, COL_AK, COL_AV = 32, 40, 48
COL_GATE_A, COL_GATE_B = 7, 9

VMEM_LIMIT = 56 * 1024 * 1024
SMALL_ROWS = 152


def _cparams(sem=None, **kw):
    if sem is not None:
        kw["dimension_semantics"] = sem
    return pltpu.CompilerParams(vmem_limit_bytes=VMEM_LIMIT, **kw)


def _pick(n, cands):
    for c in cands:
        if n % c == 0:
            return c
    return n


def _sigmoid(x):
    return 1.0 / (1.0 + jnp.exp(-x))


ANY = pl.BlockSpec(memory_space=pl.ANY)


def _position():
    return lax.axis_index("x"), lax.axis_index("y"), lax.axis_index("c")


def _call(body, args, *, name, grid, in_specs, out_specs, out_shape, scratch_shapes=(), sem=None, after=()):
    n_in = len(args)

    def ordered(*refs):
        body(*refs[:n_in], *refs[n_in + len(after):])

    return list(pl.pallas_call(
        ordered if after else body, name=name, grid=grid, in_specs=list(in_specs) + [ANY] * len(after),
        out_specs=out_specs, out_shape=out_shape, scratch_shapes=list(scratch_shapes),
        compiler_params=_cparams(sem))(*args, *after))


MAX_CONTRACTION_TILE = 4096


def _accumulate(part, acc_ref, step, n_steps, finish):
    if n_steps == 1:
        finish(part)
        return

    @pl.when(step == 0)
    def _():
        acc_ref[...] = part

    @pl.when(step > 0)
    def _():
        acc_ref[...] += part

    @pl.when(step == n_steps - 1)
    def _():
        finish(acc_ref[...])


def _mm_nn(a, wb, out_dtype, name, after=(), epilogue=None):
    M, K = a.shape
    NB, K2, Nb = wb.shape
    assert K == K2
    tm = min(M, 1024)
    tk = min(K, MAX_CONTRACTION_TILE)
    tn = _pick(Nb, (512, 1408, 256))
    nk = K // tk
    nn = Nb // tn
    extra, first_cols, out_dtypes, fn = epilogue or ((), (), (out_dtype,), lambda total: (total,))
    n_extra, n_out = len(extra), len(out_dtypes)

    def body(a_ref, b_ref, *rest):
        def finish(total):
            results = fn(total, *[r[...] for r in rest[:n_extra]])
            for o_ref, res, dt in zip(rest[n_extra:n_extra + n_out], results, out_dtypes):
                o_ref[...] = res.astype(dt)

        part = jnp.dot(a_ref[...], b_ref[...], preferred_element_type=F32)
        _accumulate(part, rest[-1], pl.program_id(3), nk, finish)

    def tile(first):
        return pl.BlockSpec((tm, tn), lambda m, j, n, k: (m, first + j * nn + n))

    outs = _call(
        body, (a, wb) + tuple(extra), name=name, grid=(M // tm, NB, nn, nk),
        in_specs=[pl.BlockSpec((tm, tk), lambda m, j, n, k: (m, k)),
                  pl.BlockSpec((None, tk, tn), lambda m, j, n, k: (j, k, n))] + [tile(col // tn) for col in first_cols],
        out_specs=[tile(0)] * n_out,
        out_shape=[jax.ShapeDtypeStruct((M, NB * Nb), dt) for dt in out_dtypes],
        scratch_shapes=[] if nk == 1 else [pltpu.VMEM((tm, tn), F32)],
        sem=("parallel", "parallel", "parallel", "arbitrary"), after=after)
    return outs if epilogue else outs[0]


def _squared_relu(a):
    ra = jnp.maximum(a, 0.0)
    return a, ra * ra


def _gated_merge(pb, za, zb, pa):
    return pb, _sigmoid(za) * pa + _sigmoid(zb) * pb


def _mm_nt(a, wb, out_dtype, name, after=(), epilogue=None):
    M, N = a.shape
    NB, K, Nb = wb.shape
    assert N == NB * Nb
    tm = min(M, 1024)
    n_tiles_live = 1 + (len(epilogue[0]) + len(epilogue[2]) if epilogue else 0)
    tko = _pick(K, (1024,)) if n_tiles_live <= 3 else _pick(K, (512,))
    tc = _pick(Nb, (2048, 1024, 1408, 256))
    nc = Nb // tc
    jb = max([d for d in (8, 4, 2, 1) if NB % d == 0 and d * tc <= MAX_CONTRACTION_TILE]) if nc == 1 else 1
    nsteps = (NB // jb) * nc
    extra, first_cols, out_dtypes, fn = epilogue or ((), (), (out_dtype,), lambda total: (total,))
    n_extra, n_out = len(extra), len(out_dtypes)

    def body(a_ref, b_ref, *rest):
        def finish(total):
            results = fn(total, *[r[...] for r in rest[:n_extra]])
            for o_ref, res, dt in zip(rest[n_extra:n_extra + n_out], results, out_dtypes):
                o_ref[...] = res.astype(dt)

        part = sum(lax.dot_general(a_ref[:, i * tc:(i + 1) * tc], b_ref[i], (((1,), (1,)), ((), ())),
                                   preferred_element_type=F32) for i in range(jb))
        _accumulate(part, rest[-1], pl.program_id(2) * nc + pl.program_id(3), nsteps, finish)

    def tile(first):
        return pl.BlockSpec((tm, tko), lambda m, ko, j, c: (m, first + ko))

    outs = _call(
        body, (a, wb) + tuple(extra), name=name,
        grid=(M // tm, K // tko, NB // jb, nc),
        in_specs=[pl.BlockSpec((tm, jb * tc), lambda m, ko, j, c: (m, j * nc + c)),
                  pl.BlockSpec((jb, tko, tc), lambda m, ko, j, c: (j, ko, c))] + [tile(col // tko) for col in first_cols],
        out_specs=[tile(0)] * n_out,
        out_shape=[jax.ShapeDtypeStruct((M, K), dt) for dt in out_dtypes],
        scratch_shapes=[] if nsteps == 1 else [pltpu.VMEM((tm, tko), F32)],
        sem=("parallel", "parallel", "arbitrary", "arbitrary"), after=after)
    return outs if epilogue else outs[0]


ROWS_TILE = 512
ROWS_PIECE = 128


def _mm_rows(a, w, blocked, extras, vectors, row_dtypes, acc_shapes, fn, name, after=()):
    M = a.shape[0]
    tm = min(M, ROWS_TILE)
    if blocked:
        n_steps, N, tk = w.shape
        a_spec = pl.BlockSpec((tm, tk), lambda m, s: (m, s))
        w_spec = pl.BlockSpec((None, N, tk), lambda m, s: (s, 0, 0))
        dims = NT
    else:
        K, N = w.shape
        tk = min(K, 2048 if len(extras) < 2 else 1024)
        n_steps = K // tk
        a_spec = pl.BlockSpec((tm, tk), lambda m, s: (m, s))
        w_spec = pl.BlockSpec((tk, N), lambda m, s: (s, 0))
        dims = NN
    n_e, n_v, n_r, n_a = len(extras), len(vectors), len(row_dtypes), len(acc_shapes)

    def body(a_ref, w_ref, *rest):
        tiles, vecs = rest[:n_e], rest[n_e:n_e + n_v]
        rows = rest[n_e + n_v:n_e + n_v + n_r]
        accs = rest[n_e + n_v + n_r:n_e + n_v + n_r + n_a]
        product_ref = rest[-1]
        first_tile = pl.program_id(0) == 0

        def finish(total):
            product_ref[...] = total
            if n_a:
                @pl.when(first_tile)
                def _():
                    for acc in accs:
                        acc[...] = jnp.zeros_like(acc)

            for i in range(tm // ROWS_PIECE):
                piece = slice(i * ROWS_PIECE, (i + 1) * ROWS_PIECE)
                results = fn(product_ref[piece, :], *[t[piece, :] for t in tiles], *[v[...] for v in vecs])
                for o_ref, res, dt in zip(rows, results[:n_r], row_dtypes):
                    o_ref[piece, :] = res.astype(dt)
                for acc, part in zip(accs, results[n_r:]):
                    acc[...] += jnp.broadcast_to(part, acc.shape)

        part = lax.dot_general(a_ref[...], w_ref[...], (dims, ((), ())), preferred_element_type=F32)
        if n_steps == 1:
            finish(part)
        else:
            step = pl.program_id(1)

            @pl.when(step == 0)
            def _():
                product_ref[...] = part

            @pl.when(jnp.logical_and(step > 0, step < n_steps - 1))
            def _():
                product_ref[...] += part

            @pl.when(step == n_steps - 1)
            def _():
                finish(product_ref[...] + part)

    row = pl.BlockSpec((tm, N), lambda m, s: (m, 0))
    vec = pl.BlockSpec((1, N), lambda m, s: (0, 0))
    return _call(
        body, (a, w) + tuple(extras) + tuple(vectors), name=name, grid=(M // tm, n_steps),
        in_specs=[a_spec, w_spec] + [row] * n_e + [vec] * n_v,
        out_specs=[row] * n_r + [pl.BlockSpec(shp, lambda m, s: (0, 0)) for shp in acc_shapes],
        out_shape=[jax.ShapeDtypeStruct((M, N), dt) for dt in row_dtypes]
        + [jax.ShapeDtypeStruct(shp, F32) for shp in acc_shapes],
        scratch_shapes=[pltpu.VMEM((tm, N), F32)],
        sem=("arbitrary" if n_a else "parallel", "arbitrary"), after=after)


def _rms(h, w):
    return h * lax.rsqrt(jnp.mean(h * h, axis=-1, keepdims=True) + EPS) * w


def _residual_rms_rows(mix, x, w):
    h = x + mix
    return h, _rms(h, w)


def _rms_bwd_rows(g, x, dres, w):
    r = lax.rsqrt(jnp.mean(x * x, axis=-1, keepdims=True) + EPS)
    xh = x * r
    gy = g * w
    dx = dres + r * (gy - xh * jnp.mean(gy * xh, axis=-1, keepdims=True))
    return dx, dx, jnp.sum(g * xh, axis=0, keepdims=True)


def _loss_rows(mlp, h1, target, wf):
    h = h1 + mlp
    r = lax.rsqrt(jnp.mean(h * h, axis=-1, keepdims=True) + EPS)
    xh = h * r
    e = xh * wf - target
    dy = e * (1.0 / h.shape[-1])
    gy = dy * wf
    dh = r * (gy - xh * jnp.mean(gy * xh, axis=-1, keepdims=True))
    loss = 0.5 * jnp.sum(jnp.mean(e * e, axis=-1, keepdims=True), axis=0, keepdims=True)
    return dh, dh, loss, jnp.sum(dy * xh, axis=0, keepdims=True)


def _mm_tn(a, g, nb, out_dtype, name, after=()):
    M, Ka = a.shape
    M2, N = g.shape
    assert M == M2 and N % nb == 0
    Nb = N // nb
    tka = _pick(Ka, (1024,))
    tn = _pick(Nb, (512, 1408, 256))
    nn = Nb // tn

    def body(a_ref, g_ref, o_ref):
        o_ref[...] = lax.dot_general(a_ref[...], g_ref[...], (((0,), (0,)), ((), ())),
                                     preferred_element_type=F32).astype(out_dtype)

    return _call(
        body, (a, g), name=name,
        grid=(Ka // tka, nb, nn),
        in_specs=[pl.BlockSpec((M, tka), lambda ka, j, n: (0, ka)),
                  pl.BlockSpec((M, tn), lambda ka, j, n: (0, j * nn + n))],
        out_specs=[pl.BlockSpec((None, tka, tn), lambda ka, j, n: (j, ka, n))],
        out_shape=[jax.ShapeDtypeStruct((nb, Ka, Nb), out_dtype)],
        sem=("parallel", "parallel", "parallel"), after=after)[0]


ROW_TILE = 256


def _rms_fwd(x, w, name):
    T, Dm = x.shape

    def body(x_ref, w_ref, u_ref):
        xv = x_ref[...]
        r = lax.rsqrt(jnp.mean(xv * xv, axis=-1, keepdims=True) + EPS)
        u_ref[...] = (xv * r * w_ref[...]).astype(BF16)

    return pl.pallas_call(
        body, name=name, grid=(T // ROW_TILE,),
        in_specs=[pl.BlockSpec((ROW_TILE, Dm), lambda i: (i, 0)), pl.BlockSpec((1, Dm), lambda i: (0, 0))],
        out_specs=pl.BlockSpec((ROW_TILE, Dm), lambda i: (i, 0)),
        out_shape=jax.ShapeDtypeStruct((T, Dm), BF16),
        compiler_params=_cparams(("parallel",)),
    )(x, w)


def _resid_rms_fwd(x, mix, w, name):
    T, Dm = x.shape

    def body(x_ref, m_ref, w_ref, h_ref, u_ref):
        h = x_ref[...] + m_ref[...]
        h_ref[...] = h
        r = lax.rsqrt(jnp.mean(h * h, axis=-1, keepdims=True) + EPS)
        u_ref[...] = (h * r * w_ref[...]).astype(BF16)

    row = pl.BlockSpec((ROW_TILE, Dm), lambda i: (i, 0))
    return pl.pallas_call(
        body, name=name, grid=(T // ROW_TILE,),
        in_specs=[row, row, pl.BlockSpec((1, Dm), lambda i: (0, 0))],
        out_specs=[row, row],
        out_shape=[jax.ShapeDtypeStruct((T, Dm), F32), jax.ShapeDtypeStruct((T, Dm), BF16)],
        compiler_params=_cparams(("parallel",)),
    )(x, mix, w)


def _loss_head(h1, mlp, wf, target, name):
    T, Dm = h1.shape

    def body(h_ref, m_ref, w_ref, t_ref, loss_ref, dh_ref, dhb_ref, dw_ref):
        i = pl.program_id(0)
        h = h_ref[...] + m_ref[...]
        r = lax.rsqrt(jnp.mean(h * h, axis=-1, keepdims=True) + EPS)
        xh = h * r
        wv = w_ref[...]
        e = xh * wv - t_ref[...]
        part = 0.5 * jnp.sum(jnp.mean(e * e, axis=-1, keepdims=True), axis=0, keepdims=True)
        dy = e * (1.0 / Dm)
        dw = jnp.sum(dy * xh, axis=0, keepdims=True)
        gy = dy * wv
        dh = r * (gy - xh * jnp.mean(gy * xh, axis=-1, keepdims=True))
        dh_ref[...] = dh
        dhb_ref[...] = dh.astype(BF16)

        @pl.when(i == 0)
        def _():
            loss_ref[...] = jnp.zeros_like(loss_ref)
            dw_ref[...] = jnp.zeros_like(dw_ref)

        loss_ref[...] += jnp.broadcast_to(part, loss_ref.shape)
        dw_ref[...] += dw

    row = pl.BlockSpec((ROW_TILE, Dm), lambda i: (i, 0))
    vec = pl.BlockSpec((1, Dm), lambda i: (0, 0))
    return pl.pallas_call(
        body, name=name, grid=(T // ROW_TILE,),
        in_specs=[row, row, vec, row],
        out_specs=[pl.BlockSpec((8, 128), lambda i: (0, 0)), row, row, vec],
        out_shape=[jax.ShapeDtypeStruct((8, 128), F32), jax.ShapeDtypeStruct((T, Dm), F32),
                   jax.ShapeDtypeStruct((T, Dm), BF16), jax.ShapeDtypeStruct((1, Dm), F32)],
        compiler_params=_cparams(("arbitrary",)),
    )(h1, mlp, wf, target)


def _rms_bwd(dyn, x, w, dres, name, after=()):
    T, Dm = x.shape

    def body(g_ref, x_ref, w_ref, r_ref, dx_ref, dxb_ref, dw_ref):
        i = pl.program_id(0)
        xv = x_ref[...]
        r = lax.rsqrt(jnp.mean(xv * xv, axis=-1, keepdims=True) + EPS)
        xh = xv * r
        g = g_ref[...]
        dw = jnp.sum(g * xh, axis=0, keepdims=True)
        gy = g * w_ref[...]
        dx = r_ref[...] + r * (gy - xh * jnp.mean(gy * xh, axis=-1, keepdims=True))
        dx_ref[...] = dx
        dxb_ref[...] = dx.astype(BF16)

        @pl.when(i == 0)
        def _():
            dw_ref[...] = jnp.zeros_like(dw_ref)

        dw_ref[...] += dw

    row = pl.BlockSpec((ROW_TILE, Dm), lambda i: (i, 0))
    vec = pl.BlockSpec((1, Dm), lambda i: (0, 0))
    return _call(
        body, (dyn, x, w, dres), name=name, grid=(T // ROW_TILE,),
        in_specs=[row, row, vec, row],
        out_specs=[row, row, vec],
        out_shape=[jax.ShapeDtypeStruct((T, Dm), F32), jax.ShapeDtypeStruct((T, Dm), BF16),
                   jax.ShapeDtypeStruct((1, Dm), F32)],
        sem=("arbitrary",), after=after)


COL_TILE = 2048


def _relu2_fwd(a, name):
    T, N = a.shape

    def body(a_ref, r_ref):
        ra = jnp.maximum(a_ref[...], 0.0)
        r_ref[...] = (ra * ra).astype(BF16)

    blk = pl.BlockSpec((ROW_TILE, COL_TILE), lambda i, j: (i, j))
    return pl.pallas_call(
        body, name=name, grid=(T // ROW_TILE, N // COL_TILE), in_specs=[blk], out_specs=blk,
        out_shape=jax.ShapeDtypeStruct((T, N), BF16),
        compiler_params=_cparams(("parallel", "parallel")),
    )(a)


def _relu2_bwd(dr, a, name, after=()):
    T, N = a.shape

    def body(dr_ref, a_ref, da_ref):
        da_ref[...] = (dr_ref[...] * (2.0 * jnp.maximum(a_ref[...], 0.0))).astype(BF16)

    blk = pl.BlockSpec((ROW_TILE, COL_TILE), lambda i, j: (i, j))
    return _call(
        body, (dr, a), name=name, grid=(T // ROW_TILE, N // COL_TILE), in_specs=[blk, blk], out_specs=[blk],
        out_shape=[jax.ShapeDtypeStruct((T, N), BF16)], sem=("parallel", "parallel"), after=after)[0]


GATE_TILE = 1024


def _merge_fwd(z, pa, pb, name):
    T, Dm = pa.shape

    def body(za_ref, zb_ref, pa_ref, pb_ref, m_ref):
        m_ref[...] = (_sigmoid(za_ref[...]) * pa_ref[...] + _sigmoid(zb_ref[...]) * pb_ref[...]).astype(BF16)

    blk = pl.BlockSpec((ROW_TILE, GATE_TILE), lambda i, j: (i, j))
    return pl.pallas_call(
        body, name=name, grid=(T // ROW_TILE, Dm // GATE_TILE),
        in_specs=[pl.BlockSpec((ROW_TILE, GATE_TILE), lambda i, j: (i, COL_GATE_A + j)),
                  pl.BlockSpec((ROW_TILE, GATE_TILE), lambda i, j: (i, COL_GATE_B + j)), blk, blk],
        out_specs=blk,
        out_shape=jax.ShapeDtypeStruct((T, Dm), BF16),
        compiler_params=_cparams(("parallel", "parallel")),
    )(z, z, pa, pb)


def _merge_grads(d, za, zb, pa, pb):
    ga = _sigmoid(za)
    gb = _sigmoid(zb)
    return d * ga, d * gb, d * pa * ga * (1.0 - ga), d * pb * gb * (1.0 - gb)


def _dot_hi(a, b, dims):
    return lax.dot_general(a, b, (dims, ((), ())), precision=HIGHEST, preferred_element_type=F32)


NN = ((1,), (0,))
NT = ((1,), (1,))
TN = ((0,), (0,))


def _hg_gates(hq, hf, lb):
    sq = _sigmoid(hq)
    q = hq * sq * (HG_DK ** -0.5)
    f = _sigmoid(hf)
    g = lb + (1.0 - lb) * f
    return q, sq, f, g, jnp.log(g), 1.0 - g


def _tri(lower):
    r = lax.broadcasted_iota(jnp.int32, (CHUNK, CHUNK), 0)
    c = lax.broadcasted_iota(jnp.int32, (CHUNK, CHUNK), 1)
    return jnp.where((r >= c) if lower else (r <= c), 1.0, 0.0).astype(F32)


GROUP = 16
N_GROUPS = CHUNK // GROUP


def _dot_bf16(a, b, dims):
    return lax.dot_general(a.astype(BF16), b.astype(BF16), (dims, ((), ())), preferred_element_type=F32)


def _rows_iota():
    return lax.broadcasted_iota(jnp.int32, (CHUNK, HG_DK), 0)


def _by_query_group(q, kk, b, g):
    r0 = GROUP * g
    b0 = b[r0:r0 + 1]
    decay = jnp.exp(b[r0:r0 + GROUP] - b0)
    ks = jnp.where(_rows_iota() < r0, kk * jnp.exp(jnp.minimum(b0 - b, 0.0)), 0.0)
    return q[r0:r0 + GROUP] * decay, ks, decay


def _by_key_group(q, kk, b, j):
    r1 = GROUP * (j + 1)
    b1 = b[r1 - 1:r1]
    decay = jnp.exp(b1 - b[r1 - GROUP:r1])
    qs = jnp.where(_rows_iota() >= r1, q * jnp.exp(jnp.minimum(b - b1, 0.0)), 0.0)
    return qs, kk[r1 - GROUP:r1] * decay, decay


def _scores_between_groups(q, kk, b):
    blocks = [jnp.zeros((GROUP, CHUNK), F32)]
    for g in range(1, N_GROUPS):
        qs, ks, _ = _by_query_group(q, kk, b, g)
        blocks.append(_dot_bf16(qs, ks, NT))
    return jnp.concatenate(blocks, axis=0)


def _hgrn2_fwd(z, lb_logits, hg_norm_w, name, after=()):
    T = z.shape[0]
    n_chunks = T // CHUNK

    def body(hq_ref, hf_ref, hi_ref, hg_ref, lbl_ref, nw_ref, o_ref, ya_ref, sall_ref, st_ref):
        lbl = lbl_ref[...]
        lb = 1.0 / (1.0 + jnp.exp(lbl[1:2, :] - lbl[0:1, :]))
        st_ref[...] = jnp.zeros_like(st_ref)
        tri = _tri(True)
        row8 = lax.broadcasted_iota(jnp.int32, (8, HG_DK), 0)

        def chunk(c, carry):
            rows = pl.ds(pl.multiple_of(c * CHUNK, CHUNK), CHUNK)
            q, _, _, _, lg, kk = _hg_gates(hq_ref[rows, :], hf_ref[rows, :], lb)
            v = hi_ref[rows, :]
            b = _dot_hi(tri, lg, NN)
            st = st_ref[...]
            sall_ref[c] = st
            for grp in range(N_GROUPS):
                r0 = GROUP * grp
                for h8 in range(GROUP // 8):
                    n = 8 * (h8 + 1)
                    bs, ks, vs = b[r0:r0 + n], kk[r0:r0 + n], v[r0:r0 + n]
                    sidx = lax.broadcasted_iota(jnp.int32, (n, HG_DK), 0)
                    blk = jnp.zeros((8, HG_DK), F32)
                    for i in range(8):
                        t = r0 + 8 * h8 + i
                        e = jnp.where(sidx <= 8 * h8 + i, jnp.exp(b[t:t + 1] - bs), 0.0)
                        p = jnp.sum(e * ks * q[t:t + 1], axis=1, keepdims=True)
                        ot = jnp.sum(p * vs, axis=0, keepdims=True)
                        blk = blk + jnp.where(row8 == i, ot, 0.0)
                    o_ref[pl.ds(pl.multiple_of(c * CHUNK + r0 + 8 * h8, 8), 8), :] = blk
            o_ref[rows, :] += _dot_hi(q * jnp.exp(b), st, NT) + _dot_bf16(_scores_between_groups(q, kk, b), v, NN)
            bl = b[CHUNK - 1:CHUNK]
            ke = kk * jnp.exp(bl - b)
            st_ref[...] = st * jnp.exp(bl) + _dot_hi(v, ke, TN)
            return carry

        lax.fori_loop(0, n_chunks, chunk, 0, unroll=2)
        o = o_ref[...]
        r = lax.rsqrt(jnp.mean(o * o, axis=-1, keepdims=True) + EPS)
        hg = hg_ref[...]
        ya_ref[...] = (o * r * nw_ref[...] * (hg * _sigmoid(hg))).astype(BF16)

    def col(base):
        return pl.BlockSpec((T, HG_DK), lambda h: (0, base + h))

    return _call(
        body, (z, z, z, z, lb_logits, hg_norm_w), name=name, grid=(HG_HEADS,),
        in_specs=[col(COL_HQ), col(COL_HF), col(COL_HI), col(COL_HG),
                  pl.BlockSpec((2, HG_DK), lambda h: (0, h)), pl.BlockSpec((1, HG_DK), lambda h: (0, 0))],
        out_specs=[col(0), col(0), pl.BlockSpec((None, n_chunks, HG_DK, HG_DK), lambda h: (h, 0, 0, 0))],
        out_shape=[jax.ShapeDtypeStruct((T, HG_WIDTH), F32), jax.ShapeDtypeStruct((T, HG_WIDTH), BF16),
                   jax.ShapeDtypeStruct((HG_HEADS, n_chunks, HG_DK, HG_DK), F32)],
        scratch_shapes=[pltpu.VMEM((HG_DK, HG_DK), F32)],
        sem=("parallel",), after=after)


def _hgrn2_bwd(z, lb_logits, hg_norm_w, o_raw, s_all, dya, name, after=()):
    T = z.shape[0]
    n_chunks = T // CHUNK

    def body(hq_ref, hf_ref, hi_ref, hg_ref, lbl_ref, nw_ref, o_ref, sall_ref, dya_ref,
             dhq_ref, dhf_ref, dhi_ref, dhg_ref, dlbl_ref, dnw_ref,
             do_ref, dst_ref, dlb_ref, *per_chunk):
        h = pl.program_id(0)
        lbl = lbl_ref[...]
        lb = 1.0 / (1.0 + jnp.exp(lbl[1:2, :] - lbl[0:1, :]))

        o = o_ref[...]
        r = lax.rsqrt(jnp.mean(o * o, axis=-1, keepdims=True) + EPS)
        oh = o * r
        nw = nw_ref[...]
        hg = hg_ref[...]
        sg = _sigmoid(hg)
        dy = dya_ref[...]
        d_on = dy * (hg * sg)
        dhg_ref[...] = (dy * (oh * nw) * (sg * (1.0 + hg * (1.0 - sg)))).astype(BF16)
        dnw = jnp.sum(d_on * oh, axis=0, keepdims=True)
        gy = d_on * nw
        do_ref[...] = r * (gy - oh * jnp.mean(gy * oh, axis=-1, keepdims=True))

        @pl.when(h == 0)
        def _():
            dnw_ref[...] = jnp.zeros_like(dnw_ref)

        dnw_ref[...] += jnp.broadcast_to(dnw, dnw_ref.shape)

        dst_ref[...] = jnp.zeros_like(dst_ref)
        dlb_ref[...] = jnp.zeros_like(dlb_ref)
        tri = _tri(True)
        tri_t = _tri(False)
        row8 = lax.broadcasted_iota(jnp.int32, (8, HG_DK), 0)
        row_group = lax.broadcasted_iota(jnp.int32, (CHUNK, CHUNK), 0) // GROUP
        col_group = lax.broadcasted_iota(jnp.int32, (CHUNK, CHUNK), 1) // GROUP
        earlier_group = col_group < row_group
        later_group = col_group > row_group

        def chunk(c, dq_ref, dk_ref, dv_ref):
            rows = pl.ds(pl.multiple_of(c * CHUNK, CHUNK), CHUNK)
            hq = hq_ref[rows, :]
            q, sq, f, g, lg, kk = _hg_gates(hq, hf_ref[rows, :], lb)
            v = hi_ref[rows, :]
            do = do_ref[rows, :]
            b = _dot_hi(tri, lg, NN)
            eb = jnp.exp(b)
            bl = b[CHUNK - 1:CHUNK]
            ebl = jnp.exp(bl)
            ekb = jnp.exp(bl - b)
            qe = q * eb
            ke = kk * ekb
            st = sall_ref[c]
            dst = dst_ref[...]
            dqe = _dot_hi(do, st, NN)
            dke = _dot_hi(v, dst, NN)
            dv_inter = _dot_hi(ke, dst, NT)
            d_ebl = jnp.sum(st * dst, axis=0, keepdims=True)
            dst_ref[...] = dst * ebl + _dot_hi(do, qe, TN)

            dk_ref[...] = jnp.zeros_like(dk_ref)
            dv_ref[...] = jnp.zeros_like(dv_ref)
            for grp in range(N_GROUPS):
                r0 = GROUP * grp
                for h8 in range(GROUP // 8):
                    n = 8 * (h8 + 1)
                    bs, ks, vs = b[r0:r0 + n], kk[r0:r0 + n], v[r0:r0 + n]
                    sidx = lax.broadcasted_iota(jnp.int32, (n, HG_DK), 0)
                    blk = jnp.zeros((8, HG_DK), F32)
                    for i in range(8):
                        t = r0 + 8 * h8 + i
                        qt = q[t:t + 1]
                        dot_ = do[t:t + 1]
                        e = jnp.where(sidx <= 8 * h8 + i, jnp.exp(b[t:t + 1] - bs), 0.0)
                        w = e * ks
                        p = jnp.sum(w * qt, axis=1, keepdims=True)
                        dsc = jnp.sum(vs * dot_, axis=1, keepdims=True)
                        dqt = jnp.sum(dsc * w, axis=0, keepdims=True)
                        blk = blk + jnp.where(row8 == i, dqt, 0.0)
                        dk_ref[r0:r0 + n, :] += dsc * e * qt
                        dv_ref[r0:r0 + n, :] += p * dot_
                    dq_ref[r0 + 8 * h8:r0 + n, :] = blk
            ds_far = jnp.where(earlier_group, _dot_bf16(do, v, NT), 0.0)
            ds_far_t = jnp.where(later_group, _dot_bf16(v, do, NT), 0.0)
            dq_far, dk_far = [jnp.zeros((GROUP, HG_DK), F32)], []
            for grp in range(1, N_GROUPS):
                r0 = GROUP * grp
                _, ks, decay = _by_query_group(q, kk, b, grp)
                dq_far.append(decay * _dot_hi(ds_far[r0:r0 + GROUP], ks, NN))
                qs, _, decay = _by_key_group(q, kk, b, grp - 1)
                dk_far.append(decay * _dot_hi(ds_far_t[r0 - GROUP:r0], qs, NN))
            dk_far.append(jnp.zeros((GROUP, HG_DK), F32))
            dv_far = _dot_bf16(_scores_between_groups(q, kk, b), do, TN)
            dq_i = dq_ref[...] + jnp.concatenate(dq_far, axis=0)
            dk_i = dk_ref[...] + jnp.concatenate(dk_far, axis=0)
            dke_ke = dke * ke
            db = q * dq_i - kk * dk_i + dqe * qe - dke_ke
            db_last = jnp.sum(dke_ke, axis=0, keepdims=True) + d_ebl * ebl
            dlg = _dot_hi(tri_t, db, NN) + db_last
            dq = dq_i + dqe * eb
            dkk = dk_i + dke * ekb
            dg = dlg / g - dkk
            dhq_ref[rows, :] = (dq * (HG_DK ** -0.5) * (sq * (1.0 + hq * (1.0 - sq)))).astype(BF16)
            dhf_ref[rows, :] = (dg * (1.0 - lb) * f * (1.0 - f)).astype(BF16)
            dhi_ref[rows, :] = (dv_ref[...] + dv_far + dv_inter).astype(BF16)
            dlb_ref[...] += jnp.sum(dg * (1.0 - f), axis=0, keepdims=True)

        def two_chunks(i, carry):
            chunk(n_chunks - 1 - 2 * i, *per_chunk[:3])
            chunk(n_chunks - 2 - 2 * i, *per_chunk[3:])
            return carry

        lax.fori_loop(0, n_chunks // 2, two_chunks, 0)
        dl0 = dlb_ref[...] * lb * (1.0 - lb)
        dlbl_ref[0:1, :] = dl0
        dlbl_ref[1:2, :] = -dl0

    def col(base):
        return pl.BlockSpec((T, HG_DK), lambda h: (0, base + h))

    outb = jax.ShapeDtypeStruct((T, HG_WIDTH), BF16)
    return _call(
        body, (z, z, z, z, lb_logits, hg_norm_w, o_raw, s_all, dya), name=name, grid=(HG_HEADS,),
        in_specs=[col(COL_HQ), col(COL_HF), col(COL_HI), col(COL_HG),
                  pl.BlockSpec((2, HG_DK), lambda h: (0, h)), pl.BlockSpec((1, HG_DK), lambda h: (0, 0)),
                  col(0), pl.BlockSpec((None, n_chunks, HG_DK, HG_DK), lambda h: (h, 0, 0, 0)), col(0)],
        out_specs=[col(0), col(0), col(0), col(0), pl.BlockSpec((2, HG_DK), lambda h: (0, h)),
                   pl.BlockSpec((8, HG_DK), lambda h: (0, 0))],
        out_shape=[outb, outb, outb, outb, jax.ShapeDtypeStruct((2, HG_WIDTH), F32),
                   jax.ShapeDtypeStruct((8, HG_DK), F32)],
        scratch_shapes=[pltpu.VMEM((T, HG_DK), F32), pltpu.VMEM((HG_DK, HG_DK), F32), pltpu.VMEM((1, HG_DK), F32)]
        + [pltpu.VMEM((CHUNK, HG_DK), F32)] * 6,
        sem=("arbitrary",), after=after)


CONST_KEYS = PAD - REL_CLIP
VAR_KEYS = BAND - CONST_KEYS
REL_LO = 128
REL_SPAN = N_REL_PAD - REL_LO


def _rel_onehot(t):
    r = lax.broadcasted_iota(jnp.int32, (REL_SPAN, VAR_KEYS), 0)
    j = lax.broadcasted_iota(jnp.int32, (REL_SPAN, VAR_KEYS), 1)
    idx = jnp.clip(t + PAD - CONST_KEYS - j, -REL_CLIP, REL_CLIP) + REL_CLIP - REL_LO
    return jnp.where(r == idx, 1.0, 0.0).astype(BF16)


def _split3(x):
    hi = x.astype(BF16)
    r1 = x - hi.astype(F32)
    mid = r1.astype(BF16)
    return hi, mid, (r1 - mid.astype(F32)).astype(BF16)


def _bias_expand(rel, name):
    def body(rel_ref, out_ref):
        tab = rel_ref[...]
        onehot = _rel_onehot(pl.program_id(0))
        out_ref[:, 0:CONST_KEYS] = jnp.broadcast_to(tab[:, 2 * REL_CLIP:2 * REL_CLIP + 1], (AT_HEADS, CONST_KEYS))
        out_ref[:, CONST_KEYS:BAND] = sum(
            jnp.dot(piece, onehot, preferred_element_type=F32) for piece in _split3(tab[:, REL_LO:N_REL_PAD]))

    return pl.pallas_call(
        body, name=name, grid=(CHUNK,),
        in_specs=[pl.BlockSpec((AT_HEADS, N_REL_PAD), lambda t: (0, 0))],
        out_specs=pl.BlockSpec((None, AT_HEADS, BAND), lambda t: (t, 0, 0)),
        out_shape=jax.ShapeDtypeStruct((CHUNK, AT_HEADS, BAND), F32),
        compiler_params=_cparams(("parallel",)),
    )(rel)


def _bias_reduce(dbias_t, name, after=()):
    def body(db_ref, out_ref):
        t = pl.program_id(0)

        @pl.when(t == 0)
        def _():
            out_ref[...] = jnp.zeros_like(out_ref)

        db = db_ref[...]
        onehot = _rel_onehot(t)
        acc = sum(lax.dot_general(piece, onehot, (NT, ((), ())), preferred_element_type=F32)
                  for piece in _split3(db[:, CONST_KEYS:BAND]))
        lane = lax.broadcasted_iota(jnp.int32, (AT_HEADS, REL_SPAN), 1)
        last = jnp.sum(db[:, 0:CONST_KEYS], axis=1, keepdims=True)
        out_ref[:, REL_LO:N_REL_PAD] += acc + jnp.where(lane == 2 * REL_CLIP - REL_LO, last, 0.0)

    return _call(
        body, (dbias_t,), name=name, grid=(CHUNK,),
        in_specs=[pl.BlockSpec((None, AT_HEADS, BAND), lambda t: (t, 0, 0))],
        out_specs=[pl.BlockSpec((AT_HEADS, N_REL_PAD), lambda t: (0, 0))],
        out_shape=[jax.ShapeDtypeStruct((AT_HEADS, N_REL_PAD), F32)],
        sem=("arbitrary",), after=after)[0]


def _pair_lanes():
    return lax.broadcasted_iota(jnp.int32, (CHUNK, 2 * AT_DH), 1) < AT_DH


def _block_diag(a):
    first = _pair_lanes()
    return jnp.concatenate([jnp.where(first, a, 0.0), jnp.where(first, 0.0, a)], axis=0).astype(BF16)


def _diag_blocks(a):
    return jnp.where(_pair_lanes(), a[:CHUNK], a[CHUNK:])


def _band_probs_t(kb, qbd, bias_t, c):
    s = lax.dot_general(kb, qbd, (NT, ((), ())), preferred_element_type=F32) * (AT_DH ** -0.5) + bias_t
    j = lax.broadcasted_iota(jnp.int32, (BAND, 2 * AT_DH), 0)
    s = jnp.where(j + c * CHUNK >= PAD, s, -jnp.inf)
    p = jnp.exp(s - jnp.max(s, axis=0, keepdims=True))
    return p / jnp.sum(p, axis=0, keepdims=True)


def _fill_padded(dst_ref, src_ref, T):
    dst_ref[0:PAD, :] = jnp.zeros((PAD, 2 * AT_DH), BF16)
    dst_ref[PAD:PAD + T, :] = src_ref[...].astype(BF16)


def _attn_fwd(z, bias_t, name, after=()):
    T = z.shape[0]
    n_chunks = T // CHUNK

    def body(q_ref, k_ref, v_ref, bias_ref, y_ref, *scratch):
        for pr in range(2):
            lanes = slice(128 * pr, 128 * (pr + 1))
            for dst_ref, src_ref in zip(scratch[2 * pr:2 * pr + 2], (k_ref, v_ref)):
                dst_ref[0:PAD, :] = jnp.zeros((PAD, 128), BF16)
                dst_ref[PAD:PAD + T, :] = src_ref[:, lanes].astype(BF16)

        def chunk(c, carry):
            rows = pl.ds(pl.multiple_of(c * CHUNK, CHUNK), CHUNK)
            band = pl.ds(pl.multiple_of(c * CHUNK, CHUNK), BAND)
            for pr in range(2):
                kp_ref, vp_ref = scratch[2 * pr:2 * pr + 2]
                lanes = slice(128 * pr, 128 * (pr + 1))
                p = _band_probs_t(kp_ref[band, :], _block_diag(q_ref[rows, lanes]), bias_ref[pr], c)
                o2 = lax.dot_general(p.astype(BF16), vp_ref[band, :], (TN, ((), ())), preferred_element_type=F32)
                y_ref[rows, lanes] = _diag_blocks(o2).astype(BF16)
            return carry

        lax.fori_loop(0, n_chunks, chunk, 0, unroll=2)

    def col(base):
        return pl.BlockSpec((T, 256), lambda h: (0, base // 2 + h))

    return _call(
        body, (z, z, z, bias_t), name=name, grid=(AT_HEADS // 4,),
        in_specs=[col(COL_AQ), col(COL_AK), col(COL_AV), pl.BlockSpec((2, BAND, 128), lambda h: (h, 0, 0))],
        out_specs=[col(0)],
        out_shape=[jax.ShapeDtypeStruct((T, AT_WIDTH), BF16)],
        scratch_shapes=[pltpu.VMEM((PAD + T, 128), BF16)] * 4,
        sem=("parallel",), after=after)


def _attn_bwd(z, bias_t, dyb, name, after=()):
    T = z.shape[0]
    n_chunks = T // CHUNK

    def body(q_ref, k_ref, v_ref, bias_ref, dy_ref, dq_ref, dk_ref, dv_ref, dbias_ref, *scratch):
        dbias_odd_ref = scratch[-1]
        dbias_ref[...] = jnp.zeros_like(dbias_ref)
        dbias_odd_ref[...] = jnp.zeros_like(dbias_odd_ref)
        for pr in range(2):
            kp_ref, vp_ref = scratch[6 * pr:6 * pr + 2]
            lanes = slice(128 * pr, 128 * (pr + 1))
            kp_ref[0:PAD, :] = jnp.zeros((PAD, 128), BF16)
            vp_ref[0:PAD, :] = jnp.zeros((PAD, 128), BF16)
            kp_ref[PAD:PAD + T, :] = k_ref[:, lanes].astype(BF16)
            vp_ref[PAD:PAD + T, :] = v_ref[:, lanes].astype(BF16)
            for acc_ref in scratch[6 * pr + 2:6 * pr + 6]:
                acc_ref[...] = jnp.zeros_like(acc_ref)

        def chunk(c, odd):
            rows = pl.ds(pl.multiple_of(c * CHUNK, CHUNK), CHUNK)
            band = pl.ds(pl.multiple_of(c * CHUNK, CHUNK), BAND)
            for pr in range(2):
                kp_ref, vp_ref = scratch[6 * pr:6 * pr + 2]
                dkp_ref, dvp_ref = scratch[6 * pr + 2 + 2 * odd:6 * pr + 4 + 2 * odd]
                db_ref = dbias_odd_ref if odd else dbias_ref
                lanes = slice(128 * pr, 128 * (pr + 1))
                qbd = _block_diag(q_ref[rows, lanes])
                dobd = _block_diag(dy_ref[rows, lanes])
                kb = kp_ref[band, :]
                vb = vp_ref[band, :]
                p = _band_probs_t(kb, qbd, bias_ref[pr], c)
                dp = lax.dot_general(vb, dobd, (NT, ((), ())), preferred_element_type=F32)
                ds = p * (dp - jnp.sum(dp * p, axis=0, keepdims=True))
                db_ref[pr] += ds
                dsb = ds.astype(BF16)
                dq2 = lax.dot_general(dsb, kb, (TN, ((), ())), preferred_element_type=F32)
                dq_ref[rows, lanes] = (_diag_blocks(dq2) * (AT_DH ** -0.5)).astype(BF16)
                dkp_ref[band, :] += jnp.dot(dsb, qbd, preferred_element_type=F32) * (AT_DH ** -0.5)
                dvp_ref[band, :] += jnp.dot(p.astype(BF16), dobd, preferred_element_type=F32)

        def two_chunks(i, carry):
            chunk(2 * i, 0)
            chunk(2 * i + 1, 1)
            return carry

        lax.fori_loop(0, n_chunks // 2, two_chunks, 0)
        dbias_ref[...] += dbias_odd_ref[...]
        for pr in range(2):
            lanes = slice(128 * pr, 128 * (pr + 1))
            dkp0, dvp0, dkp1, dvp1 = scratch[6 * pr + 2:6 * pr + 6]
            dk_ref[:, lanes] = (dkp0[PAD:PAD + T, :] + dkp1[PAD:PAD + T, :]).astype(BF16)
            dv_ref[:, lanes] = (dvp0[PAD:PAD + T, :] + dvp1[PAD:PAD + T, :]).astype(BF16)

    def col(base):
        return pl.BlockSpec((T, 256), lambda h: (0, base // 2 + h))

    pairs = pl.BlockSpec((2, BAND, 128), lambda h: (h, 0, 0))
    outb = jax.ShapeDtypeStruct((T, AT_WIDTH), BF16)
    return _call(
        body, (z, z, z, bias_t, dyb), name=name, grid=(AT_HEADS // 4,),
        in_specs=[col(COL_AQ), col(COL_AK), col(COL_AV), pairs, col(0)],
        out_specs=[col(0), col(0), col(0), pairs],
        out_shape=[outb, outb, outb, jax.ShapeDtypeStruct((AT_HEADS // 2, BAND, 128), F32)],
        scratch_shapes=([pltpu.VMEM((PAD + T, 128), BF16)] * 2 + [pltpu.VMEM((PAD + T, 128), F32)] * 4) * 2
        + [pltpu.VMEM((2, BAND, 128), F32)],
        sem=("parallel",), after=after)


def _local_step(x, target, lb_logits, hg_norm_w, rel_bias, norm_mix_w, norm_mlp_w, norm_final_w,
                w_in, rest, exchanges=None):
    ex = exchanges
    rel = jnp.pad(rel_bias, ((0, 0), (0, N_REL_PAD - N_REL)))

    u = _rms_fwd(x, norm_mix_w, "rms_mix_fwd")
    if ex:
        z, w_in = _mm_gathered(u, w_in, ex.order, "mm_in_fwd")
        gather = _Gather(rest, [z], "ag")
        tok = [gather.token]
    else:
        z = _mm_nn(u, w_in, F32, "mm_in_fwd")
        w_a, w_b, w_out, w_up, w_down = rest
        tok = []
    o_raw, y_a, s_all = _hgrn2_fwd(z, lb_logits, hg_norm_w, "hgrn2_fwd", after=tok)
    if ex:
        tok = [gather.pass_on([0, 1, 2], [o_raw], "abo")]
    bias_rows = _bias_expand(rel, "bias_expand")
    bias_t = jnp.transpose(bias_rows.reshape(CHUNK, AT_HEADS // 2, 2, BAND), (1, 3, 2, 0)).reshape(
        AT_HEADS // 2, BAND, 2 * CHUNK)
    y_b, = _attn_fwd(z, bias_t, "attn_fwd", after=tok)
    if ex:
        tok = [gather.pass_on([3], [y_b], "up")]
        w_a, w_b, w_out = gather.finish([0, 1, 2], tok, "abo")
    pa = _mm_nn(y_a, w_a, F32, "mm_a_fwd")
    pb, merged = _mm_nn(y_b, w_b, None, "mm_b_fwd", epilogue=(
        (z, z, pa), (COL_GATE_A * GATE_TILE, COL_GATE_B * GATE_TILE, 0), (F32, BF16), _gated_merge))
    w_out1 = w_out.reshape(1, D_MODEL, D_MODEL)
    h1, u2 = _mm_rows(merged, w_out.reshape(D_MODEL, D_MODEL), False, [x], [norm_mlp_w], (F32, BF16), (),
                      _residual_rms_rows, "mm_out_fwd")
    if ex:
        tok = [gather.pass_on([4], [u2], "down")]
        w_up, = gather.finish([3], tok, "up")
    a, r = _mm_nn(u2, w_up, None, "mm_up_fwd", epilogue=((), (), (F32, BF16), _squared_relu))
    if ex:
        w_down, = gather.finish([4], [r], "down")
    w_down1 = w_down.reshape(1, D_FF, D_MODEL)
    dh2, dh2b, loss, g_nf = _mm_rows(r, w_down.reshape(D_FF, D_MODEL), False, [h1, target], [norm_final_w],
                                     (F32, BF16), ((8, 128), (1, D_MODEL)), _loss_rows, "mm_down_fwd")

    def reduce_scatter(grads, name):
        rs = _ReduceScatter(grads, ex.parity, name) if ex else None
        return rs, ([rs.token] if ex else [])

    g_down = _mm_tn(r, dh2b, 1, BF16, "mm_down_wgrad").reshape(N_DEV, D_FF // N_DEV, D_MODEL)
    rs_down, tok = reduce_scatter([g_down], "rs_down")
    da, = _mm_nt(dh2b, w_down1, None, "mm_down_dgrad", after=tok, epilogue=(
        (a,), (0,), (BF16,), lambda dr, av: (dr * (2.0 * jnp.maximum(av, 0.0)),)))
    tok = [rs_down.pair_sums([da])] if ex else []
    g_up = _mm_tn(u2, da, N_DEV, BF16, "mm_up_wgrad", after=tok)
    rs_up, tok = reduce_scatter([g_up], "rs_up")
    dh1, dh1b, g_nmlp = _mm_rows(da, w_up, True, [h1, dh2], [norm_mlp_w], (F32, BF16), ((1, D_MODEL),),
                                 _rms_bwd_rows, "mm_up_dgrad", after=tok)
    tok = [rs_up.pair_sums([dh1b])] if ex else []

    g_out = _mm_tn(merged, dh1b, 1, BF16, "mm_out_wgrad", after=tok).reshape(N_DEV, D_MODEL // N_DEV, D_MODEL)
    dpa, dpb, dga, dgb = _mm_nt(dh1b, w_out1, None, "mm_out_dgrad", epilogue=(
        (z, z, pa, pb), (COL_GATE_A * GATE_TILE, COL_GATE_B * GATE_TILE, 0, 0), (BF16,) * 4, _merge_grads))
    g_a = _mm_tn(y_a, dpa, N_DEV, BF16, "mm_a_wgrad")
    g_b = _mm_tn(y_b, dpb, N_DEV, BF16, "mm_b_wgrad")
    rs_mix, tok = reduce_scatter([g_a, g_b, g_out], "rs_mix")
    dya = _mm_nt(dpa, w_a, F32, "mm_a_dgrad", after=tok)
    dyb = _mm_nt(dpb, w_b, F32, "mm_b_dgrad", after=tok)
    tok = [rs_mix.pair_sums([dya, dyb])] if ex else []
    daq, dak, dav, dbias_t = _attn_bwd(z, bias_t, dyb, "attn_bwd", after=tok)
    dhq, dhf, dhi, dhg, g_lbl, g_hgw = _hgrn2_bwd(z, lb_logits, hg_norm_w, o_raw, s_all, dya, "hgrn2_bwd",
                                                  after=tok)
    dbias_rows = jnp.transpose(dbias_t.reshape(AT_HEADS // 2, BAND, 2, CHUNK), (3, 0, 2, 1)).reshape(
        CHUNK, AT_HEADS, BAND)
    dz =jnp.concatenate([dhq, dhf, dhi, dhg, daq, dak, dav, dga, dgb], axis=1)
    g_in = _mm_tn(u, dz, N_DEV, BF16, "mm_in_wgrad")
    rs_in, _ = reduce_scatter([g_in], "rs_in")
    tok = [rs_in.pair_sums([])] if ex else []
    grad_x, g_nmix = _mm_rows(dz, w_in, True, [x, dh1], [norm_mix_w], (F32,), ((1, D_MODEL),),
                              lambda *tiles: _rms_bwd_rows(*tiles)[1:], "mm_in_dgrad", after=tok)
    g_rel = _bias_reduce(dbias_rows, "bias_reduce", after=tok)[:, :N_REL]

    small = dict(lb_logits=g_lbl, hg_norm_w=g_hgw[0:1], rel_bias=g_rel, norm_mix_w=g_nmix, norm_mlp_w=g_nmlp,
                 norm_final_w=g_nf)
    grads = [rs_in, rs_mix, rs_up, rs_down] if ex else [g_in, g_a, g_b, g_out, g_up, g_down]
    return loss, grad_x, grads, small


def _gather_exchange(shards, mid_step=None):
    n = len(shards)

    def parts(ins, outs, sems):
        send_sems, recv_sems, local_sems = sems
        x, y, c = _position()
        chips = [(1 - x, y), (x, 1 - y), (1 - x, 1 - y)]

        def copy(w, k, block, to, src=None):
            dst = outs[w].at[4 * block[0] + 2 * block[1] + block[2]]
            return pltpu.make_async_remote_copy(
                src_ref=dst if src is None else src, dst_ref=dst,
                send_sem=send_sems.at[w, k], recv_sem=recv_sems.at[w, k], device_id=to, device_id_type=MESH)

        def local(w):
            return pltpu.make_async_copy(ins[w], outs[w].at[4 * x + 2 * y + c], local_sems.at[w])

        return (x, y, c), (x, y, 1 - c), chips, copy, local

    def start(ins, outs, sems):
        me, sibling, chips, copy, local = parts(ins, outs, sems)
        for w in range(n):
            local(w).start()
        for w in range(n):
            copy(w, 0, me, sibling, src=ins[w]).start()
            for j, chip in enumerate(chips):
                copy(w, 1 + j, me, (*chip, me[2]), src=ins[w]).start()

    def mid(ins, outs, sems):
        me, sibling, chips, copy, _ = parts(ins, outs, sems)
        for w in range(n):
            for j, chip in enumerate(chips):
                copy(w, 1 + j, (*chip, me[2]), me).wait_recv()
                copy(w, 4 + j, (*chip, me[2]), sibling).start()

    def end(ins, outs, sems):
        me, sibling, chips, copy, local = parts(ins, outs, sems)
        for w in range(n):
            copy(w, 0, sibling, me).wait_recv()
            for j, chip in enumerate(chips):
                copy(w, 4 + j, (*chip, sibling[2]), me).wait_recv()
        for w in range(n):
            for k in range(7):
                copy(w, k, me, sibling).wait_send()
            local(w).wait()

    return _Exchange(
        shards, [jax.ShapeDtypeStruct((N_DEV,) + s.shape, s.dtype) for s in shards],
        [pltpu.SemaphoreType.DMA((n, 7)), pltpu.SemaphoreType.DMA((n, 7)), pltpu.SemaphoreType.DMA((n,))],
        start, end, mid, mid_step)


def _mm_gathered(u, shard, order, name):
    T, K = u.shape
    _, Nb = shard.shape

    def body(order_ref, u_ref, shard_ref, z_ref, full_ref, wbuf, load_sem, send_sems, recv_sems, local_sem):
        s = pl.program_id(0)
        x, y, c = _position()
        me, sibling = (x, y, c), (x, y, 1 - c)
        chips = [(1 - x, y), (x, 1 - y), (1 - x, 1 - y)]

        def copy(k, block, to, src=None):
            dst = full_ref.at[4 * block[0] + 2 * block[1] + block[2]]
            return pltpu.make_async_remote_copy(
                src_ref=dst if src is None else src, dst_ref=dst,
                send_sem=send_sems.at[k], recv_sem=recv_sems.at[k], device_id=to, device_id_type=MESH)

        @pl.when(s == 0)
        def _():
            local = pltpu.make_async_copy(shard_ref, full_ref.at[4 * x + 2 * y + c], local_sem)
            local.start()
            copy(0, me, sibling, src=shard_ref).start()
            for j, chip in enumerate(chips):
                copy(1 + j, me, (*chip, c), src=shard_ref).start()
            local.wait()

        @pl.when(s == 1)
        def _():
            copy(0, sibling, me).wait_recv()

        for j, chip in enumerate(chips):
            direct, passed = ((2, 4), (3, 5), (6, 7))[j]

            @pl.when(s == direct)
            def _(j=j, chip=chip):
                copy(1 + j, (*chip, c), me).wait_recv()
                copy(4 + j, (*chip, c), sibling).start()

            @pl.when(s == passed)
            def _(j=j, chip=chip):
                copy(4 + j, (*chip, 1 - c), me).wait_recv()

        load = pltpu.make_async_copy(full_ref.at[order_ref[s]], wbuf, load_sem)
        load.start()
        load.wait()
        z_ref[...] = jnp.dot(u_ref[...], wbuf[...], preferred_element_type=F32)

        @pl.when(s == N_DEV - 1)
        def _():
            for k in range(7):
                copy(k, me, sibling).wait_send()

    return pl.pallas_call(
        body, name=name,
        grid_spec=pltpu.PrefetchScalarGridSpec(
            num_scalar_prefetch=1, grid=(N_DEV,),
            in_specs=[pl.BlockSpec((T, K), lambda s, order: (0, 0)), ANY],
            out_specs=[pl.BlockSpec((T, Nb), lambda s, order: (0, order[s])), ANY],
            scratch_shapes=[pltpu.VMEM((K, Nb), BF16), pltpu.SemaphoreType.DMA,
                            pltpu.SemaphoreType.DMA((7,)), pltpu.SemaphoreType.DMA((7,)), pltpu.SemaphoreType.DMA]),
        out_shape=[jax.ShapeDtypeStruct((T, N_DEV * Nb), F32), jax.ShapeDtypeStruct((N_DEV, K, Nb), BF16)],
        compiler_params=_cparams(("arbitrary",)),
    )(order, u, shard)


def _gather_order():
    x, y, c = _position()
    chips = [(1 - x, y), (x, 1 - y), (1 - x, 1 - y)]
    ids = [4 * x + 2 * y + c, 4 * x + 2 * y + (1 - c)]
    ids += [4 * cx + 2 * cy + c for cx, cy in chips[:2]] + [4 * cx + 2 * cy + (1 - c) for cx, cy in chips[:2]]
    ids += [4 * chips[2][0] + 2 * chips[2][1] + c, 4 * chips[2][0] + 2 * chips[2][1] + (1 - c)]
    return jnp.stack(ids).astype(jnp.int32)


def _run_exchange(comm, name):
    n_i, n_o = len(comm.arrays), len(comm.out_shape)

    def body(*refs):
        ins, outs, sems = refs[:n_i], refs[n_i:n_i + n_o], refs[n_i + n_o:]
        comm.start(ins, outs, sems)
        if comm.mid is not None:
            comm.mid(ins, outs, sems)
        comm.end(ins, outs, sems)

    return pl.pallas_call(
        body, name=name, in_specs=[ANY] * n_i, out_specs=[ANY] * n_o, out_shape=comm.out_shape,
        scratch_shapes=comm.scratch)(*comm.arrays)


def _exchange_sibling(grads, name):
    n = len(grads)

    def body(*refs):
        ins, outs = refs[:n], refs[n:2 * n]
        send_sems, recv_sems = refs[2 * n:]
        x, y, c = _position()
        copies = []
        for w in range(n):
            for s in range(N_CHIP):
                cp = pltpu.make_async_remote_copy(
                    src_ref=ins[w].at[2 * s + (1 - c)], dst_ref=outs[w].at[s],
                    send_sem=send_sems.at[w, s], recv_sem=recv_sems.at[w, s],
                    device_id=(x, y, 1 - c), device_id_type=MESH)
                cp.start()
                copies.append(cp)
        for cp in copies:
            cp.wait()

    return pl.pallas_call(
        body, name=name,
        in_specs=[ANY] * n, out_specs=[ANY] * n,
        out_shape=[jax.ShapeDtypeStruct((N_CHIP,) + g.shape[1:], g.dtype) for g in grads],
        scratch_shapes=[pltpu.SemaphoreType.DMA((n, N_CHIP)), pltpu.SemaphoreType.DMA((n, N_CHIP))],
    )(*grads)


def _pair_sum(g, land, parity, name):
    _, R, C = g.shape
    tr = _pick(R, (512, 256))

    def body(par_ref, g_ref, l_ref, o_ref):
        o_ref[...] = (g_ref[...].astype(F32) + l_ref[...].astype(F32)).astype(BF16)

    return pl.pallas_call(
        body, name=name,
        grid_spec=pltpu.PrefetchScalarGridSpec(
            num_scalar_prefetch=1, grid=(N_CHIP, R // tr),
            in_specs=[pl.BlockSpec((None, tr, C), lambda s, i, par: (2 * s + par[0], i, 0)),
                      pl.BlockSpec((None, tr, C), lambda s, i, par: (s, i, 0))],
            out_specs=pl.BlockSpec((None, tr, C), lambda s, i, par: (s, i, 0))),
        out_shape=jax.ShapeDtypeStruct((N_CHIP, R, C), BF16),
        compiler_params=_cparams(("parallel", "parallel")),
    )(parity, g, land)


def _scatter_exchange(partials):
    n = len(partials)

    def copies(ins, outs, sems):
        send_sems, recv_sems, local_sems = sems
        x, y, c = _position()
        chips = [(1 - x, y), (x, 1 - y), (1 - x, 1 - y)]
        my_slot = 2 * x + y
        local = [pltpu.make_async_copy(ins[w].at[my_slot], outs[w].at[my_slot], local_sems.at[w]) for w in range(n)]
        remote = [pltpu.make_async_remote_copy(
            src_ref=ins[w].at[2 * chip[0] + chip[1]], dst_ref=outs[w].at[my_slot],
            send_sem=send_sems.at[w, j], recv_sem=recv_sems.at[w, j], device_id=(*chip, c), device_id_type=MESH)
            for w in range(n) for j, chip in enumerate(chips)]
        return local, remote

    def start(ins, outs, sems):
        local, remote = copies(ins, outs, sems)
        for cp in local + remote:
            cp.start()

    def end(ins, outs, sems):
        local, remote = copies(ins, outs, sems)
        for cp in remote + local:
            cp.wait()

    return _Exchange(
        partials, [jax.ShapeDtypeStruct(p.shape, p.dtype) for p in partials],
        [pltpu.SemaphoreType.DMA((n, 3)), pltpu.SemaphoreType.DMA((n, 3)), pltpu.SemaphoreType.DMA((n,))],
        start, end)


HBM = pl.BlockSpec(memory_space=pltpu.HBM)
SEM = pl.BlockSpec(memory_space=pltpu.SEMAPHORE)
DATAFLOW = pltpu.SideEffectType.DATAFLOW_SIDE_EFFECTING


def _scatter_copies(ins, lands, send_sems, recv_sems):
    x, y, c = _position()
    chips = [(1 - x, y), (x, 1 - y), (1 - x, 1 - y)]
    return [pltpu.make_async_remote_copy(
        src_ref=ins[w].at[2 * chip[0] + chip[1]], dst_ref=lands[w].at[2 * x + y],
        send_sem=send_sems[3 * w + j], recv_sem=recv_sems[3 * w + j], device_id=(*chip, c), device_id_type=MESH)
        for w in range(len(ins)) for j, chip in enumerate(chips)]


def _scatter_start(partials, name):
    n = len(partials)

    def body(*refs):
        ins, lands = refs[:n], refs[n:2 * n]
        sems = refs[4 * n:10 * n]
        for cp in _scatter_copies(ins, lands, sems[:3 * n], sems[3 * n:]):
            cp.start()
        refs[-1][...] = jnp.zeros_like(refs[-1])

    def in_hbm(a):
        return pltpu.with_memory_space_constraint(a, pltpu.HBM)

    bufs = tuple(pltpu.HBM(p.shape, p.dtype) for p in partials)
    outs = pl.pallas_call(
        body, name=name,
        out_shape=bufs + bufs + (pltpu.SemaphoreType.DMA(()),) * (6 * n) + (jax.ShapeDtypeStruct((8, 128), F32),),
        in_specs=[HBM] * (2 * n),
        out_specs=(HBM,) * (2 * n) + (SEM,) * (6 * n) + (pl.BlockSpec(memory_space=pltpu.VMEM),),
        input_output_aliases={i: i for i in range(2 * n)},
        compiler_params=pltpu.CompilerParams(has_side_effects=DATAFLOW),
    )(*[in_hbm(p) for p in partials], *[in_hbm(lax.empty(p.shape, p.dtype)) for p in partials])
    return list(outs[:-1]), outs[-1]


def _scatter_wait(handle, after, name):
    n = len(handle) // 8
    bufs, sems = handle[:2 * n], handle[2 * n:]

    def body(*refs):
        ins, lands = refs[:n], refs[n:2 * n]
        sems = refs[2 * n:8 * n]
        for cp in _scatter_copies(ins, lands, sems[:3 * n], sems[3 * n:]):
            cp.wait_send()
            cp.wait_recv()

    outs = pl.pallas_call(
        body, name=name,
        out_shape=tuple(pltpu.HBM(b.shape, b.dtype) for b in bufs),
        in_specs=[HBM] * (2 * n) + [SEM] * (6 * n) + [ANY] * len(after), out_specs=(HBM,) * (2 * n),
        input_output_aliases={i: i for i in range(2 * n)},
        compiler_params=pltpu.CompilerParams(has_side_effects=DATAFLOW),
    )(*bufs, *sems, *after)
    return list(outs[:n]), list(outs[n:])


def _split_call(name, bufs, waits=(), starts=None, after=()):
    nb = len(bufs)
    n_new = starts[1] if starts else 0
    wait_sems = [s for w in waits for s in (*w[1], *w[2])]

    def body(*refs):
        b, pos = refs[:nb], nb
        for plan, ss, _, send_idx, recv_idx in waits:
            k = len(ss)
            copies = plan(b, refs[pos:pos + k], refs[pos + k:pos + 2 * k])
            pos += 2 * k
            for i in recv_idx:
                copies[i].wait_recv()
            for i in send_idx:
                copies[i].wait_send()
        outs = refs[pos + len(after):]
        if starts:
            for cp in starts[0](b, outs[nb:nb + n_new], outs[nb + n_new:nb + 2 * n_new]):
                cp.start()
        outs[-1][...] = jnp.zeros_like(outs[-1])

    res = pl.pallas_call(
        body, name=name,
        out_shape=tuple(pltpu.HBM(a.shape, a.dtype) for a in bufs) + (pltpu.SemaphoreType.DMA(()),) * (2 * n_new)
        + (jax.ShapeDtypeStruct((8, 128), F32),),
        in_specs=[HBM] * nb + [SEM] * len(wait_sems) + [ANY] * len(after),
        out_specs=(HBM,) * nb + (SEM,) * (2 * n_new) + (pl.BlockSpec(memory_space=pltpu.VMEM),),
        input_output_aliases={i: i for i in range(nb)},
        compiler_params=pltpu.CompilerParams(has_side_effects=DATAFLOW),
    )(*bufs, *wait_sems, *after)
    return list(res[:nb]), list(res[nb:nb + n_new]), list(res[nb + n_new:nb + 2 * n_new]), res[-1]


def _in_hbm(a):
    return pltpu.with_memory_space_constraint(a, pltpu.HBM)


def _remote(src, dst, send_sem, recv_sem, to):
    return pltpu.make_async_remote_copy(src_ref=src, dst_ref=dst, send_sem=send_sem, recv_sem=recv_sem,
                                        device_id=to, device_id_type=MESH)


def _other_chips():
    x, y, _ = _position()
    return [(1 - x, y), (x, 1 - y), (1 - x, 1 - y)]


def _plan_gather_first(n):
    def plan(b, ss, rs):
        x, y, c = _position()
        to = [(x, y, 1 - c)] + [(*chip, c) for chip in _other_chips()]
        return [_remote(b[w], b[n + w].at[4 * x + 2 * y + c], ss[4 * w + k], rs[4 * w + k], to[k])
                for w in range(n) for k in range(4)]
    return plan, 4 * n


def _plan_gather_pass(n):
    def plan(b, ss, rs):
        x, y, c = _position()
        copies = []
        for w in range(n):
            for j, chip in enumerate(_other_chips()):
                blk = b[n + w].at[4 * chip[0] + 2 * chip[1] + c]
                copies.append(_remote(blk, blk, ss[3 * w + j], rs[3 * w + j], (x, y, 1 - c)))
        return copies
    return plan, 3 * n


def _plan_sibling(n):
    def plan(b, ss, rs):
        x, y, c = _position()
        return [_remote(b[w].at[2 * s + (1 - c)], b[n + w].at[s], ss[4 * w + s], rs[4 * w + s], (x, y, 1 - c))
                for w in range(n) for s in range(N_CHIP)]
    return plan, 4 * n


def _plan_scatter(n):
    def plan(b, ss, rs):
        x, y, c = _position()
        return [_remote(b[w].at[2 * chip[0] + chip[1]], b[n + w].at[2 * x + y], ss[3 * w + j], rs[3 * w + j],
                        (*chip, c))
                for w in range(n) for j, chip in enumerate(_other_chips())]
    return plan, 3 * n


class _Gather:
    def __init__(self, shards, after, name):
        self.n, self.name = len(shards), name
        x, y, c = _position()
        placed = [lax.dynamic_update_index_in_dim(lax.empty((N_DEV,) + s.shape, s.dtype), s, 4 * x + 2 * y + c, 0)
                  for s in shards]
        bufs, self.ss, self.rs, self.token = _split_call(
            name + "_start", [_in_hbm(a) for a in list(shards) + placed], starts=_plan_gather_first(self.n),
            after=after)
        self.shards, self.fulls = bufs[:self.n], bufs[self.n:]
        self.passed = {}

    def _sub(self, ids, sems, per):
        return [sems[per * w + k] for w in ids for k in range(per)]

    def pass_on(self, ids, after, tag):
        m = len(ids)
        first = (_plan_gather_first(m)[0], self._sub(ids, self.ss, 4), self._sub(ids, self.rs, 4),
                 [], [4 * i + k for i in range(m) for k in (1, 2, 3)])
        bufs, ss, rs, token = _split_call(
            "%s_pass_%s" % (self.name, tag), [self.shards[w] for w in ids] + [self.fulls[w] for w in ids],
            waits=[first], starts=_plan_gather_pass(m), after=after)
        for i, w in enumerate(ids):
            self.shards[w], self.fulls[w] = bufs[i], bufs[m + i]
        self.passed[tuple(ids)] = (ss, rs)
        return token

    def finish(self, ids, after, tag):
        m = len(ids)
        ss2, rs2 = self.passed[tuple(ids)]
        first = (_plan_gather_first(m)[0], self._sub(ids, self.ss, 4), self._sub(ids, self.rs, 4),
                 list(range(4 * m)), [4 * i for i in range(m)])
        passed = (_plan_gather_pass(m)[0], ss2, rs2, list(range(3 * m)), list(range(3 * m)))
        bufs, _, _, _ = _split_call(
            "%s_finish_%s" % (self.name, tag), [self.shards[w] for w in ids] + [self.fulls[w] for w in ids],
            waits=[first, passed], after=after)
        return bufs[m:]


class _ReduceScatter:
    def __init__(self, grads, parity, name):
        self.n, self.name, self.parity = len(grads), name, parity
        lands = [lax.empty((N_CHIP,) + g.shape[1:], g.dtype) for g in grads]
        self.bufs, self.ss, self.rs, self.token = _split_call(
            name + "_sibling_start", [_in_hbm(a) for a in list(grads) + lands], starts=_plan_sibling(self.n))

    def pair_sums(self, after):
        n = self.n
        bufs, _, _, _ = _split_call(
            self.name + "_sibling_wait", self.bufs,
            waits=[(_plan_sibling(n)[0], self.ss, self.rs, list(range(4 * n)), list(range(4 * n)))], after=after)
        sums = [_pair_sum(bufs[w], bufs[n + w], self.parity, "%s_pair_sum_%d" % (self.name, w)) for w in range(n)]
        lands = [lax.empty(s.shape, s.dtype) for s in sums]
        self.bufs, self.ss, self.rs, token = _split_call(
            self.name + "_scatter_start", [_in_hbm(a) for a in sums + lands], starts=_plan_scatter(n))
        return token

    def finish(self, after):
        n = self.n
        bufs, _, _, _ = _split_call(
            self.name + "_scatter_wait", self.bufs,
            waits=[(_plan_scatter(n)[0], self.ss, self.rs, list(range(3 * n)), list(range(3 * n)))], after=after)
        return bufs[:n], bufs[n:]


class _Exchanges:
    def __init__(self, parity, order):
        self.parity, self.order = parity, order


def _gather_small(packed, name):
    R = packed.shape[0]

    def body(x_ref, out_ref, send_sems, recv_sems):
        x, y, c = _position()
        me = 4 * x + 2 * y + c
        out_ref[me] = x_ref[...]
        copies = []
        for k in range(1, N_DEV):
            to = (x ^ ((k >> 2) & 1), y ^ ((k >> 1) & 1), c ^ (k & 1))
            cp = pltpu.make_async_remote_copy(
                src_ref=x_ref, dst_ref=out_ref.at[me],
                send_sem=send_sems.at[k], recv_sem=recv_sems.at[k], device_id=to, device_id_type=MESH)
            cp.start()
            copies.append((k, to, cp))
        for k, to, cp in copies:
            cp.wait_send()
            pltpu.make_async_remote_copy(
                src_ref=x_ref, dst_ref=out_ref.at[4 * to[0] + 2 * to[1] + to[2]],
                send_sem=send_sems.at[k], recv_sem=recv_sems.at[k], device_id=to, device_id_type=MESH).wait_recv()

    return pl.pallas_call(
        body, name=name,
        in_specs=[pl.BlockSpec(memory_space=pltpu.VMEM)], out_specs=pl.BlockSpec(memory_space=pltpu.VMEM),
        out_shape=jax.ShapeDtypeStruct((N_DEV, R, 128), F32),
        scratch_shapes=[pltpu.SemaphoreType.DMA((N_DEV,)), pltpu.SemaphoreType.DMA((N_DEV,))],
    )(packed)


def _adamw_math(w, g, m, v):
    m = ADAM_B1 * m + (1.0 - ADAM_B1) * g
    v = ADAM_B2 * v + (1.0 - ADAM_B2) * (g * g)
    m_hat = m / (1.0 - ADAM_B1 ** ADAM_STEP)
    v_hat = v / (1.0 - ADAM_B2 ** ADAM_STEP)
    delta = -ADAM_LR * (m_hat / (jnp.sqrt(v_hat) + ADAM_EPS) + ADAM_WD * w)
    return delta, m, v


def _adamw_big(w, m, v, parts, name):
    R, C = w.shape
    tr = _pick(R, (256,))

    def body(w_ref, m_ref, v_ref, p_ref, g_ref, d_ref, nm_ref, nv_ref):
        g = p_ref[0].astype(F32)
        for s in range(1, N_CHIP):
            g = g + p_ref[s].astype(F32)
        d, nm, nv = _adamw_math(w_ref[...], g, m_ref[...], v_ref[...])
        g_ref[...] = g
        d_ref[...] = d
        nm_ref[...] = nm
        nv_ref[...] = nv

    blk = pl.BlockSpec((tr, C), lambda i: (i, 0))
    out = jax.ShapeDtypeStruct((R, C), F32)
    return pl.pallas_call(
        body, name=name, grid=(R // tr,),
        in_specs=[blk, blk, blk, pl.BlockSpec((N_CHIP, tr, C), lambda i: (0, i, 0))],
        out_specs=[blk, blk, blk, blk], out_shape=[out, out, out, out],
        compiler_params=_cparams(("parallel",)),
    )(w, m, v, parts)


def _adamw_big_landed(w, m, v, parts, lands, slot, name):
    R, C = w.shape
    tr = _pick(R, (256,))

    def body(slot_ref, w_ref, m_ref, v_ref, own_ref, l1_ref, l2_ref, l3_ref, g_ref, d_ref, nm_ref, nv_ref):
        g = own_ref[...].astype(F32)
        for ref in (l1_ref, l2_ref, l3_ref):
            g = g + ref[...].astype(F32)
        d, nm, nv = _adamw_math(w_ref[...], g, m_ref[...], v_ref[...])
        g_ref[...] = g
        d_ref[...] = d
        nm_ref[...] = nm
        nv_ref[...] = nv

    blk = pl.BlockSpec((tr, C), lambda i, slot: (i, 0))

    def chip(k):
        return pl.BlockSpec((None, tr, C), lambda i, slot: ((slot[0] + k) % N_CHIP, i, 0))

    out = jax.ShapeDtypeStruct((R, C), F32)
    return pl.pallas_call(
        body, name=name,
        grid_spec=pltpu.PrefetchScalarGridSpec(
            num_scalar_prefetch=1, grid=(R // tr,),
            in_specs=[blk, blk, blk, chip(0), chip(1), chip(2), chip(3)],
            out_specs=[blk, blk, blk, blk]),
        out_shape=[out, out, out, out],
        compiler_params=_cparams(("parallel",)),
    )(slot, w, m, v, parts, lands, lands, lands)


def _adamw_small(w, m, v, gathered, name):
    R = w.shape[0]

    def body(w_ref, m_ref, v_ref, p_ref, g_ref, d_ref, nm_ref, nv_ref):
        g = p_ref[0]
        for s in range(1, N_DEV):
            g = g + p_ref[s]
        d, nm, nv = _adamw_math(w_ref[...], g, m_ref[...], v_ref[...])
        g_ref[...] = g
        d_ref[...] = d
        nm_ref[...] = nm
        nv_ref[...] = nv

    out = jax.ShapeDtypeStruct((R, 128), F32)
    return pl.pallas_call(
        body, name=name, out_shape=[out, out, out, out],
    )(w, m, v, gathered)


SMALL_NAMES = ("lb_logits", "hg_norm_w", "rel_bias", "norm_mix_w", "norm_mlp_w", "norm_final_w")
SMALL_SHAPES = {"lb_logits": (2, HG_WIDTH), "hg_norm_w": (1, HG_DK), "rel_bias": (AT_HEADS, N_REL_PAD),
                "norm_mix_w": (1, D_MODEL), "norm_mlp_w": (1, D_MODEL), "norm_final_w": (1, D_MODEL)}


def _pack_small(parts):
    rows = []
    for nme in SMALL_NAMES:
        p = parts[nme]
        if nme == "rel_bias":
            p = jnp.pad(p, ((0, 0), (0, N_REL_PAD - N_REL)))
        rows.append(p.reshape(-1, 128))
    flat = jnp.concatenate(rows, axis=0)
    return jnp.pad(flat, ((0, SMALL_ROWS - flat.shape[0]), (0, 0)))


def _unpack_small(packed):
    out, at = {}, 0
    for nme in SMALL_NAMES:
        shp = SMALL_SHAPES[nme]
        nrow = shp[0] * shp[1] // 128
        p = packed[at:at + nrow].reshape(shp)
        at += nrow
        out[nme] = p[:, :N_REL] if nme == "rel_bias" else p
    return out


BIG_NAMES = ("w_in", "w_branch_a", "w_branch_b", "w_out", "w_up", "w_down")


def kernel(x, w_in, lb_logits, hg_norm_w, rel_bias, w_branch_a, w_branch_b, w_out, norm_mix_w, norm_mlp_w, w_up, w_down, norm_final_w, loss_target, m_w_in, m_lb_logits, m_hg_norm_w, m_rel_bias, m_w_branch_a, m_w_branch_b, m_w_out, m_norm_mix_w, m_norm_mlp_w, m_w_up, m_w_down, m_norm_final_w, v_w_in, v_lb_logits, v_hg_norm_w, v_rel_bias, v_w_branch_a, v_w_branch_b, v_w_out, v_norm_mix_w, v_norm_mlp_w, v_w_up, v_w_down, v_norm_final_w):
    big_w = [w_in[0], w_branch_a[0], w_branch_b[0], w_out[0], w_up[0], w_down[0]]
    big_m = [m_w_in[0], m_w_branch_a[0], m_w_branch_b[0], m_w_out[0], m_w_up[0], m_w_down[0]]
    big_v = [v_w_in[0], v_w_branch_a[0], v_w_branch_b[0], v_w_out[0], v_w_up[0], v_w_down[0]]

    shards = [w.astype(BF16) for w in big_w]
    parity = lax.axis_index("c").astype(jnp.int32).reshape(1)
    loss_part, grad_x, chip_parts, small = _local_step(
        x[0], loss_target[0], lb_logits, hg_norm_w, rel_bias[0], norm_mix_w, norm_mlp_w,
        norm_final_w.reshape(1, D_MODEL), shards[0], shards[1:], _Exchanges(parity, _gather_order()))
    loss = lax.psum(loss_part[0, 0], ("x", "y", "c"))
    rs_in, rs_mix, rs_up, rs_down = chip_parts
    slot = (2 * lax.axis_index("x") + lax.axis_index("y")).astype(jnp.int32).reshape(1)
    big = {}

    def finish(rs, names, after):
        sums, lands = rs.finish(after)
        for nme, own, land in zip(names, sums, lands):
            i = BIG_NAMES.index(nme)
            big[nme] = _adamw_big_landed(big_w[i], big_m[i], big_v[i], own, land, slot, "adamw_" + nme)
        return [big[nme][1] for nme in names]

    done = finish(rs_down, ["w_down"], [grad_x])
    done = finish(rs_up, ["w_up"], done)
    done = finish(rs_mix, ["w_branch_a", "w_branch_b", "w_out"], done)

    sw = dict(lb_logits=lb_logits, hg_norm_w=hg_norm_w, rel_bias=rel_bias[0], norm_mix_w=norm_mix_w,
              norm_mlp_w=norm_mlp_w, norm_final_w=norm_final_w.reshape(1, D_MODEL))
    sm = dict(lb_logits=m_lb_logits, hg_norm_w=m_hg_norm_w, rel_bias=m_rel_bias[0], norm_mix_w=m_norm_mix_w,
              norm_mlp_w=m_norm_mlp_w, norm_final_w=m_norm_final_w.reshape(1, D_MODEL))
    sv = dict(lb_logits=v_lb_logits, hg_norm_w=v_hg_norm_w, rel_bias=v_rel_bias[0], norm_mix_w=v_norm_mix_w,
              norm_mlp_w=v_norm_mlp_w, norm_final_w=v_norm_final_w.reshape(1, D_MODEL))
    gathered = _gather_small(_pack_small(small), "gather_small")
    small_packed = _adamw_small(_pack_small(sw), _pack_small(sm), _pack_small(sv), gathered, "adamw_small")
    small_out = [_unpack_small(p) for p in small_packed]

    finish(rs_in, ["w_in"], done + [small_packed[0]])

    def leaf(kind, nme):
        if nme in BIG_NAMES:
            return big[nme][kind][None]
        p = small_out[kind][nme]
        if nme == "rel_bias":
            return p[None]
        if nme == "norm_final_w":
            return p.reshape(D_MODEL)
        return p

    order = ("w_in", "lb_logits", "hg_norm_w", "rel_bias", "w_branch_a", "w_branch_b", "w_out", "norm_mix_w",
             "norm_mlp_w", "w_up", "w_down", "norm_final_w")
    outs = [loss, grad_x[None]]
    for kind in range(4):
        outs += [leaf(kind, nme) for nme in order]
    return tuple(outs)
```

```python
import functools

import jax
import jax.numpy as jnp
from jax import lax
from jax.experimental import pallas as pl
from jax.experimental.pallas import tpu as pltpu

F32 = jnp.float32
BF16 = jnp.bfloat16
HIGHEST = lax.Precision.HIGHEST
MESH = pl.DeviceIdType.MESH

D_MODEL = 2048
HG_HEADS = 8
HG_DK = 128
HG_WIDTH = 1024
AT_HEADS = 16
AT_DH = 64
AT_WIDTH = 1024
CHUNK = 64
LEFT_CHUNKS = 8
BAND = (LEFT_CHUNKS + 1) * CHUNK
PAD = LEFT_CHUNKS * CHUNK
REL_CLIP = 256
N_REL = 2 * REL_CLIP + 1
N_REL_PAD = 640
D_FF = 4 * D_MODEL
D_IN = 4 * HG_WIDTH + 3 * AT_WIDTH + 2 * D_MODEL
EPS = 1e-6
N_DEV = 8
N_CHIP = 4

ADAM_LR = 0.001
ADAM_B1 = 0.9
ADAM_B2 = 0.999
ADAM_EPS = 1e-08
ADAM_WD = 0.01
ADAM_STEP = 10

COL_HQ, COL_HF, COL_HI, COL_HG = 0, 8, 16, 24
COL_AQ, COL_AK, COL_AV = 32, 40, 48
COL_GATE_A, COL_GATE_B = 7, 9

VMEM_LIMIT = 56 * 1024 * 1024
SMALL_ROWS = 152


def _cparams(sem=None, **kw):
    if sem is not None:
        kw["dimension_semantics"] = sem
    return pltpu.CompilerParams(vmem_limit_bytes=VMEM_LIMIT, **kw)


def _pick(n, cands):
    for c in cands:
        if n % c == 0:
            return c
    return n


def _sigmoid(x):
    return 1.0 / (1.0 + jnp.exp(-x))


ANY = pl.BlockSpec(memory_space=pl.ANY)


def _position():
    return lax.axis_index("x"), lax.axis_index("y"), lax.axis_index("c")


def _call(body, args, *, name, grid, in_specs, out_specs, out_shape, scratch_shapes=(), sem=None, after=()):
    n_in = len(args)

    def ordered(*refs):
        body(*refs[:n_in], *refs[n_in + len(after):])

    return list(pl.pallas_call(
        ordered if after else body, name=name, grid=grid, in_specs=list(in_specs) + [ANY] * len(after),
        out_specs=out_specs, out_shape=out_shape, scratch_shapes=list(scratch_shapes),
        compiler_params=_cparams(sem))(*args, *after))


MAX_CONTRACTION_TILE = 4096


def _accumulate(part, acc_ref, step, n_steps, finish):
    if n_steps == 1:
        finish(part)
        return

    @pl.when(step == 0)
    def _():
        acc_ref[...] = part

    @pl.when(step > 0)
    def _():
        acc_ref[...] += part

    @pl.when(step == n_steps - 1)
    def _():
        finish(acc_ref[...])


def _mm_nn(a, wb, out_dtype, name, after=(), epilogue=None):
    M, K = a.shape
    NB, K2, Nb = wb.shape
    assert K == K2
    tm = min(M, 1024)
    tk = min(K, MAX_CONTRACTION_TILE)
    tn = _pick(Nb, (512, 1408, 256))
    nk = K // tk
    nn = Nb // tn
    extra, first_cols, out_dtypes, fn = epilogue or ((), (), (out_dtype,), lambda total: (total,))
    n_extra, n_out = len(extra), len(out_dtypes)

    def body(a_ref, b_ref, *rest):
        def finish(total):
            results = fn(total, *[r[...] for r in rest[:n_extra]])
            for o_ref, res, dt in zip(rest[n_extra:n_extra + n_out], results, out_dtypes):
                o_ref[...] = res.astype(dt)

        part = jnp.dot(a_ref[...], b_ref[...], preferred_element_type=F32)
        _accumulate(part, rest[-1], pl.program_id(3), nk, finish)

    def tile(first):
        return pl.BlockSpec((tm, tn), lambda m, j, n, k: (m, first + j * nn + n))

    outs = _call(
        body, (a, wb) + tuple(extra), name=name, grid=(M // tm, NB, nn, nk),
        in_specs=[pl.BlockSpec((tm, tk), lambda m, j, n, k: (m, k)),
                  pl.BlockSpec((None, tk, tn), lambda m, j, n, k: (j, k, n))] + [tile(col // tn) for col in first_cols],
        out_specs=[tile(0)] * n_out,
        out_shape=[jax.ShapeDtypeStruct((M, NB * Nb), dt) for dt in out_dtypes],
        scratch_shapes=[] if nk == 1 else [pltpu.VMEM((tm, tn), F32)],
        sem=("parallel", "parallel", "parallel", "arbitrary"), after=after)
    return outs if epilogue else outs[0]


def _squared_relu(a):
    ra = jnp.maximum(a, 0.0)
    return a, ra * ra


def _gated_merge(pb, za, zb, pa):
    return pb, _sigmoid(za) * pa + _sigmoid(zb) * pb


def _mm_nt(a, wb, out_dtype, name, after=(), epilogue=None):
    M, N = a.shape
    NB, K, Nb = wb.shape
    assert N == NB * Nb
    tm = min(M, 1024)
    n_tiles_live = 1 + (len(epilogue[0]) + len(epilogue[2]) if epilogue else 0)
    tko = _pick(K, (1024,)) if n_tiles_live <= 3 else _pick(K, (512,))
    tc = _pick(Nb, (2048, 1024, 1408, 256))
    nc = Nb // tc
    jb = max([d for d in (8, 4, 2, 1) if NB % d == 0 and d * tc <= MAX_CONTRACTION_TILE]) if nc == 1 else 1
    nsteps = (NB // jb) * nc
    extra, first_cols, out_dtypes, fn = epilogue or ((), (), (out_dtype,), lambda total: (total,))
    n_extra, n_out = len(extra), len(out_dtypes)

    def body(a_ref, b_ref, *rest):
        def finish(total):
            results = fn(total, *[r[...] for r in rest[:n_extra]])
            for o_ref, res, dt in zip(rest[n_extra:n_extra + n_out], results, out_dtypes):
                o_ref[...] = res.astype(dt)

        part = sum(lax.dot_general(a_ref[:, i * tc:(i + 1) * tc], b_ref[i], (((1,), (1,)), ((), ())),
                                   preferred_element_type=F32) for i in range(jb))
        _accumulate(part, rest[-1], pl.program_id(2) * nc + pl.program_id(3), nsteps, finish)

    def tile(first):
        return pl.BlockSpec((tm, tko), lambda m, ko, j, c: (m, first + ko))

    outs = _call(
        body, (a, wb) + tuple(extra), name=name,
        grid=(M // tm, K // tko, NB // jb, nc),
        in_specs=[pl.BlockSpec((tm, jb * tc), lambda m, ko, j, c: (m, j * nc + c)),
                  pl.BlockSpec((jb, tko, tc), lambda m, ko, j, c: (j, ko, c))] + [tile(col // tko) for col in first_cols],
        out_specs=[tile(0)] * n_out,
        out_shape=[jax.ShapeDtypeStruct((M, K), dt) for dt in out_dtypes],
        scratch_shapes=[] if nsteps == 1 else [pltpu.VMEM((tm, tko), F32)],
        sem=("parallel", "parallel", "arbitrary", "arbitrary"), after=after)
    return outs if epilogue else outs[0]


ROWS_TILE = 512
ROWS_PIECE = 128


def _mm_rows(a, w, blocked, extras, vectors, row_dtypes, acc_shapes, fn, name, after=()):
    M = a.shape[0]
    tm = min(M, ROWS_TILE)
    if blocked:
        n_steps, N, tk = w.shape
        a_spec = pl.BlockSpec((tm, tk), lambda m, s: (m, s))
        w_spec = pl.BlockSpec((None, N, tk), lambda m, s: (s, 0, 0))
        dims = NT
    else:
        K, N = w.shape
        tk = min(K, 2048 if len(extras) < 2 else 1024)
        n_steps = K // tk
        a_spec = pl.BlockSpec((tm, tk), lambda m, s: (m, s))
        w_spec = pl.BlockSpec((tk, N), lambda m, s: (s, 0))
        dims = NN
    n_e, n_v, n_r, n_a = len(extras), len(vectors), len(row_dtypes), len(acc_shapes)

    def body(a_ref, w_ref, *rest):
        tiles, vecs = rest[:n_e], rest[n_e:n_e + n_v]
        rows = rest[n_e + n_v:n_e + n_v + n_r]
        accs = rest[n_e + n_v + n_r:n_e + n_v + n_r + n_a]
        product_ref = rest[-1]
        first_tile = pl.program_id(0) == 0

        def finish(total):
            product_ref[...] = total
            if n_a:
                @pl.when(first_tile)
                def _():
                    for acc in accs:
                        acc[...] = jnp.zeros_like(acc)

            for i in range(tm // ROWS_PIECE):
                piece = slice(i * ROWS_PIECE, (i + 1) * ROWS_PIECE)
                results = fn(product_ref[piece, :], *[t[piece, :] for t in tiles], *[v[...] for v in vecs])
                for o_ref, res, dt in zip(rows, results[:n_r], row_dtypes):
                    o_ref[piece, :] = res.astype(dt)
                for acc, part in zip(accs, results[n_r:]):
                    acc[...] += jnp.broadcast_to(part, acc.shape)

        part = lax.dot_general(a_ref[...], w_ref[...], (dims, ((), ())), preferred_element_type=F32)
        if n_steps == 1:
            finish(part)
        else:
            step = pl.program_id(1)

            @pl.when(step == 0)
            def _():
                product_ref[...] = part

            @pl.when(jnp.logical_and(step > 0, step < n_steps - 1))
            def _():
                product_ref[...] += part

            @pl.when(step == n_steps - 1)
            def _():
                finish(product_ref[...] + part)

    row = pl.BlockSpec((tm, N), lambda m, s: (m, 0))
    vec = pl.BlockSpec((1, N), lambda m, s: (0, 0))
    return _call(
        body, (a, w) + tuple(extras) + tuple(vectors), name=name, grid=(M // tm, n_steps),
        in_specs=[a_spec, w_spec] + [row] * n_e + [vec] * n_v,
        out_specs=[row] * n_r + [pl.BlockSpec(shp, lambda m, s: (0, 0)) for shp in acc_shapes],
        out_shape=[jax.ShapeDtypeStruct((M, N), dt) for dt in row_dtypes]
        + [jax.ShapeDtypeStruct(shp, F32) for shp in acc_shapes],
        scratch_shapes=[pltpu.VMEM((tm, N), F32)],
        sem=("arbitrary" if n_a else "parallel", "arbitrary"), after=after)


def _rms(h, w):
    return h * lax.rsqrt(jnp.mean(h * h, axis=-1, keepdims=True) + EPS) * w


def _residual_rms_rows(mix, x, w):
    h = x + mix
    return h, _rms(h, w)


def _rms_bwd_rows(g, x, dres, w):
    r = lax.rsqrt(jnp.mean(x * x, axis=-1, keepdims=True) + EPS)
    xh = x * r
    gy = g * w
    dx = dres + r * (gy - xh * jnp.mean(gy * xh, axis=-1, keepdims=True))
    return dx, dx, jnp.sum(g * xh, axis=0, keepdims=True)


def _loss_rows(mlp, h1, target, wf):
    h = h1 + mlp
    r = lax.rsqrt(jnp.mean(h * h, axis=-1, keepdims=True) + EPS)
    xh = h * r
    e = xh * wf - target
    dy = e * (1.0 / h.shape[-1])
    gy = dy * wf
    dh = r * (gy - xh * jnp.mean(gy * xh, axis=-1, keepdims=True))
    loss = 0.5 * jnp.sum(jnp.mean(e * e, axis=-1, keepdims=True), axis=0, keepdims=True)
    return dh, dh, loss, jnp.sum(dy * xh, axis=0, keepdims=True)


def _mm_tn(a, g, nb, out_dtype, name, after=()):
    M, Ka = a.shape
    M2, N = g.shape
    assert M == M2 and N % nb == 0
    Nb = N // nb
    tka = _pick(Ka, (1024,))
    tn = _pick(Nb, (512, 1408, 256))
    nn = Nb // tn

    def body(a_ref, g_ref, o_ref):
        o_ref[...] = lax.dot_general(a_ref[...], g_ref[...], (((0,), (0,)), ((), ())),
                                     preferred_element_type=F32).astype(out_dtype)

    return _call(
        body, (a, g), name=name,
        grid=(Ka // tka, nb, nn),
        in_specs=[pl.BlockSpec((M, tka), lambda ka, j, n: (0, ka)),
                  pl.BlockSpec((M, tn), lambda ka, j, n: (0, j * nn + n))],
        out_specs=[pl.BlockSpec((None, tka, tn), lambda ka, j, n: (j, ka, n))],
        out_shape=[jax.ShapeDtypeStruct((nb, Ka, Nb), out_dtype)],
        sem=("parallel", "parallel", "parallel"), after=after)[0]


ROW_TILE = 256


def _rms_fwd(x, w, name):
    T, Dm = x.shape

    def body(x_ref, w_ref, u_ref):
        xv = x_ref[...]
        r = lax.rsqrt(jnp.mean(xv * xv, axis=-1, keepdims=True) + EPS)
        u_ref[...] = (xv * r * w_ref[...]).astype(BF16)

    return pl.pallas_call(
        body, name=name, grid=(T // ROW_TILE,),
        in_specs=[pl.BlockSpec((ROW_TILE, Dm), lambda i: (i, 0)), pl.BlockSpec((1, Dm), lambda i: (0, 0))],
        out_specs=pl.BlockSpec((ROW_TILE, Dm), lambda i: (i, 0)),
        out_shape=jax.ShapeDtypeStruct((T, Dm), BF16),
        compiler_params=_cparams(("parallel",)),
    )(x, w)


def _resid_rms_fwd(x, mix, w, name):
    T, Dm = x.shape

    def body(x_ref, m_ref, w_ref, h_ref, u_ref):
        h = x_ref[...] + m_ref[...]
        h_ref[...] = h
        r = lax.rsqrt(jnp.mean(h * h, axis=-1, keepdims=True) + EPS)
        u_ref[...] = (h * r * w_ref[...]).astype(BF16)

    row = pl.BlockSpec((ROW_TILE, Dm), lambda i: (i, 0))
    return pl.pallas_call(
        body, name=name, grid=(T // ROW_TILE,),
        in_specs=[row, row, pl.BlockSpec((1, Dm), lambda i: (0, 0))],
        out_specs=[row, row],
        out_shape=[jax.ShapeDtypeStruct((T, Dm), F32), jax.ShapeDtypeStruct((T, Dm), BF16)],
        compiler_params=_cparams(("parallel",)),
    )(x, mix, w)


def _loss_head(h1, mlp, wf, target, name):
    T, Dm = h1.shape

    def body(h_ref, m_ref, w_ref, t_ref, loss_ref, dh_ref, dhb_ref, dw_ref):
        i = pl.program_id(0)
        h = h_ref[...] + m_ref[...]
        r = lax.rsqrt(jnp.mean(h * h, axis=-1, keepdims=True) + EPS)
        xh = h * r
        wv = w_ref[...]
        e = xh * wv - t_ref[...]
        part = 0.5 * jnp.sum(jnp.mean(e * e, axis=-1, keepdims=True), axis=0, keepdims=True)
        dy = e * (1.0 / Dm)
        dw = jnp.sum(dy * xh, axis=0, keepdims=True)
        gy = dy * wv
        dh = r * (gy - xh * jnp.mean(gy * xh, axis=-1, keepdims=True))
        dh_ref[...] = dh
        dhb_ref[...] = dh.astype(BF16)

        @pl.when(i == 0)
        def _():
            loss_ref[...] = jnp.zeros_like(loss_ref)
            dw_ref[...] = jnp.zeros_like(dw_ref)

        loss_ref[...] += jnp.broadcast_to(part, loss_ref.shape)
        dw_ref[...] += dw

    row = pl.BlockSpec((ROW_TILE, Dm), lambda i: (i, 0))
    vec = pl.BlockSpec((1, Dm), lambda i: (0, 0))
    return pl.pallas_call(
        body, name=name, grid=(T // ROW_TILE,),
        in_specs=[row, row, vec, row],
        out_specs=[pl.BlockSpec((8, 128), lambda i: (0, 0)), row, row, vec],
        out_shape=[jax.ShapeDtypeStruct((8, 128), F32), jax.ShapeDtypeStruct((T, Dm), F32),
                   jax.ShapeDtypeStruct((T, Dm), BF16), jax.ShapeDtypeStruct((1, Dm), F32)],
        compiler_params=_cparams(("arbitrary",)),
    )(h1, mlp, wf, target)


def _rms_bwd(dyn, x, w, dres, name, after=()):
    T, Dm = x.shape

    def body(g_ref, x_ref, w_ref, r_ref, dx_ref, dxb_ref, dw_ref):
        i = pl.program_id(0)
        xv = x_ref[...]
        r = lax.rsqrt(jnp.mean(xv * xv, axis=-1, keepdims=True) + EPS)
        xh = xv * r
        g = g_ref[...]
        dw = jnp.sum(g * xh, axis=0, keepdims=True)
        gy = g * w_ref[...]
        dx = r_ref[...] + r * (gy - xh * jnp.mean(gy * xh, axis=-1, keepdims=True))
        dx_ref[...] = dx
        dxb_ref[...] = dx.astype(BF16)

        @pl.when(i == 0)
        def _():
            dw_ref[...] = jnp.zeros_like(dw_ref)

        dw_ref[...] += dw

    row = pl.BlockSpec((ROW_TILE, Dm), lambda i: (i, 0))
    vec = pl.BlockSpec((1, Dm), lambda i: (0, 0))
    return _call(
        body, (dyn, x, w, dres), name=name, grid=(T // ROW_TILE,),
        in_specs=[row, row, vec, row],
        out_specs=[row, row, vec],
        out_shape=[jax.ShapeDtypeStruct((T, Dm), F32), jax.ShapeDtypeStruct((T, Dm), BF16),
                   jax.ShapeDtypeStruct((1, Dm), F32)],
        sem=("arbitrary",), after=after)


COL_TILE = 2048


def _relu2_fwd(a, name):
    T, N = a.shape

    def body(a_ref, r_ref):
        ra = jnp.maximum(a_ref[...], 0.0)
        r_ref[...] = (ra * ra).astype(BF16)

    blk = pl.BlockSpec((ROW_TILE, COL_TILE), lambda i, j: (i, j))
    return pl.pallas_call(
        body, name=name, grid=(T // ROW_TILE, N // COL_TILE), in_specs=[blk], out_specs=blk,
        out_shape=jax.ShapeDtypeStruct((T, N), BF16),
        compiler_params=_cparams(("parallel", "parallel")),
    )(a)


def _relu2_bwd(dr, a, name, after=()):
    T, N = a.shape

    def body(dr_ref, a_ref, da_ref):
        da_ref[...] = (dr_ref[...] * (2.0 * jnp.maximum(a_ref[...], 0.0))).astype(BF16)

    blk = pl.BlockSpec((ROW_TILE, COL_TILE), lambda i, j: (i, j))
    return _call(
        body, (dr, a), name=name, grid=(T // ROW_TILE, N // COL_TILE), in_specs=[blk, blk], out_specs=[blk],
        out_shape=[jax.ShapeDtypeStruct((T, N), BF16)], sem=("parallel", "parallel"), after=after)[0]


GATE_TILE = 1024


def _merge_fwd(z, pa, pb, name):
    T, Dm = pa.shape

    def body(za_ref, zb_ref, pa_ref, pb_ref, m_ref):
        m_ref[...] = (_sigmoid(za_ref[...]) * pa_ref[...] + _sigmoid(zb_ref[...]) * pb_ref[...]).astype(BF16)

    blk = pl.BlockSpec((ROW_TILE, GATE_TILE), lambda i, j: (i, j))
    return pl.pallas_call(
        body, name=name, grid=(T // ROW_TILE, Dm // GATE_TILE),
        in_specs=[pl.BlockSpec((ROW_TILE, GATE_TILE), lambda i, j: (i, COL_GATE_A + j)),
                  pl.BlockSpec((ROW_TILE, GATE_TILE), lambda i, j: (i, COL_GATE_B + j)), blk, blk],
        out_specs=blk,
        out_shape=jax.ShapeDtypeStruct((T, Dm), BF16),
        compiler_params=_cparams(("parallel", "parallel")),
    )(z, z, pa, pb)


def _merge_grads(d, za, zb, pa, pb):
    ga = _sigmoid(za)
    gb = _sigmoid(zb)
    return d * ga, d * gb, d * pa * ga * (1.0 - ga), d * pb * gb * (1.0 - gb)


def _dot_hi(a, b, dims):
    return lax.dot_general(a, b, (dims, ((), ())), precision=HIGHEST, preferred_element_type=F32)


NN = ((1,), (0,))
NT = ((1,), (1,))
TN = ((0,), (0,))


def _hg_gates(hq, hf, lb):
    sq = _sigmoid(hq)
    q = hq * sq * (HG_DK ** -0.5)
    f = _sigmoid(hf)
    g = lb + (1.0 - lb) * f
    return q, sq, f, g, jnp.log(g), 1.0 - g


def _tri(lower):
    r = lax.broadcasted_iota(jnp.int32, (CHUNK, CHUNK), 0)
    c = lax.broadcasted_iota(jnp.int32, (CHUNK, CHUNK), 1)
    return jnp.where((r >= c) if lower else (r <= c), 1.0, 0.0).astype(F32)


GROUP = 16
N_GROUPS = CHUNK // GROUP


def _dot_bf16(a, b, dims):
    return lax.dot_general(a.astype(BF16), b.astype(BF16), (dims, ((), ())), preferred_element_type=F32)


def _rows_iota():
    return lax.broadcasted_iota(jnp.int32, (CHUNK, HG_DK), 0)


def _by_query_group(q, kk, b, g):
    r0 = GROUP * g
    b0 = b[r0:r0 + 1]
    decay = jnp.exp(b[r0:r0 + GROUP] - b0)
    ks = jnp.where(_rows_iota() < r0, kk * jnp.exp(jnp.minimum(b0 - b, 0.0)), 0.0)
    return q[r0:r0 + GROUP] * decay, ks, decay


def _by_key_group(q, kk, b, j):
    r1 = GROUP * (j + 1)
    b1 = b[r1 - 1:r1]
    decay = jnp.exp(b1 - b[r1 - GROUP:r1])
    qs = jnp.where(_rows_iota() >= r1, q * jnp.exp(jnp.minimum(b - b1, 0.0)), 0.0)
    return qs, kk[r1 - GROUP:r1] * decay, decay


def _scores_between_groups(q, kk, b):
    blocks = [jnp.zeros((GROUP, CHUNK), F32)]
    for g in range(1, N_GROUPS):
        qs, ks, _ = _by_query_group(q, kk, b, g)
        blocks.append(_dot_bf16(qs, ks, NT))
    return jnp.concatenate(blocks, axis=0)


def _hgrn2_fwd(z, lb_logits, hg_norm_w, name, after=()):
    T = z.shape[0]
    n_chunks = T // CHUNK

    def body(hq_ref, hf_ref, hi_ref, hg_ref, lbl_ref, nw_ref, o_ref, ya_ref, sall_ref, st_ref):
        lbl = lbl_ref[...]
        lb = 1.0 / (1.0 + jnp.exp(lbl[1:2, :] - lbl[0:1, :]))
        st_ref[...] = jnp.zeros_like(st_ref)
        tri = _tri(True)
        row8 = lax.broadcasted_iota(jnp.int32, (8, HG_DK), 0)

        def chunk(c, carry):
            rows = pl.ds(pl.multiple_of(c * CHUNK, CHUNK), CHUNK)
            q, _, _, _, lg, kk = _hg_gates(hq_ref[rows, :], hf_ref[rows, :], lb)
            v = hi_ref[rows, :]
            b = _dot_hi(tri, lg, NN)
            st = st_ref[...]
            sall_ref[c] = st
            for grp in range(N_GROUPS):
                r0 = GROUP * grp
                for h8 in range(GROUP // 8):
                    n = 8 * (h8 + 1)
                    bs, ks, vs = b[r0:r0 + n], kk[r0:r0 + n], v[r0:r0 + n]
                    sidx = lax.broadcasted_iota(jnp.int32, (n, HG_DK), 0)
                    blk = jnp.zeros((8, HG_DK), F32)
                    for i in range(8):
                        t = r0 + 8 * h8 + i
                        e = jnp.where(sidx <= 8 * h8 + i, jnp.exp(b[t:t + 1] - bs), 0.0)
                        p = jnp.sum(e * ks * q[t:t + 1], axis=1, keepdims=True)
                        ot = jnp.sum(p * vs, axis=0, keepdims=True)
                        blk = blk + jnp.where(row8 == i, ot, 0.0)
                    o_ref[pl.ds(pl.multiple_of(c * CHUNK + r0 + 8 * h8, 8), 8), :] = blk
            o_ref[rows, :] += _dot_hi(q * jnp.exp(b), st, NT) + _dot_bf16(_scores_between_groups(q, kk, b), v, NN)
            bl = b[CHUNK - 1:CHUNK]
            ke = kk * jnp.exp(bl - b)
            st_ref[...] = st * jnp.exp(bl) + _dot_hi(v, ke, TN)
            return carry

        lax.fori_loop(0, n_chunks, chunk, 0, unroll=2)
        o = o_ref[...]
        r = lax.rsqrt(jnp.mean(o * o, axis=-1, keepdims=True) + EPS)
        hg = hg_ref[...]
        ya_ref[...] = (o * r * nw_ref[...] * (hg * _sigmoid(hg))).astype(BF16)

    def col(base):
        return pl.BlockSpec((T, HG_DK), lambda h: (0, base + h))

    return _call(
        body, (z, z, z, z, lb_logits, hg_norm_w), name=name, grid=(HG_HEADS,),
        in_specs=[col(COL_HQ), col(COL_HF), col(COL_HI), col(COL_HG),
                  pl.BlockSpec((2, HG_DK), lambda h: (0, h)), pl.BlockSpec((1, HG_DK), lambda h: (0, 0))],
        out_specs=[col(0), col(0), pl.BlockSpec((None, n_chunks, HG_DK, HG_DK), lambda h: (h, 0, 0, 0))],
        out_shape=[jax.ShapeDtypeStruct((T, HG_WIDTH), F32), jax.ShapeDtypeStruct((T, HG_WIDTH), BF16),
                   jax.ShapeDtypeStruct((HG_HEADS, n_chunks, HG_DK, HG_DK), F32)],
        scratch_shapes=[pltpu.VMEM((HG_DK, HG_DK), F32)],
        sem=("parallel",), after=after)


def _hgrn2_bwd(z, lb_logits, hg_norm_w, o_raw, s_all, dya, name, after=()):
    T = z.shape[0]
    n_chunks = T // CHUNK

    def body(hq_ref, hf_ref, hi_ref, hg_ref, lbl_ref, nw_ref, o_ref, sall_ref, dya_ref,
             dhq_ref, dhf_ref, dhi_ref, dhg_ref, dlbl_ref, dnw_ref,
             do_ref, dst_ref, dlb_ref, *per_chunk):
        h = pl.program_id(0)
        lbl = lbl_ref[...]
        lb = 1.0 / (1.0 + jnp.exp(lbl[1:2, :] - lbl[0:1, :]))

        o = o_ref[...]
        r = lax.rsqrt(jnp.mean(o * o, axis=-1, keepdims=True) + EPS)
        oh = o * r
        nw = nw_ref[...]
        hg = hg_ref[...]
        sg = _sigmoid(hg)
        dy = dya_ref[...]
        d_on = dy * (hg * sg)
        dhg_ref[...] = (dy * (oh * nw) * (sg * (1.0 + hg * (1.0 - sg)))).astype(BF16)
        dnw = jnp.sum(d_on * oh, axis=0, keepdims=True)
        gy = d_on * nw
        do_ref[...] = r * (gy - oh * jnp.mean(gy * oh, axis=-1, keepdims=True))

        @pl.when(h == 0)
        def _():
            dnw_ref[...] = jnp.zeros_like(dnw_ref)

        dnw_ref[...] += jnp.broadcast_to(dnw, dnw_ref.shape)

        dst_ref[...] = jnp.zeros_like(dst_ref)
        dlb_ref[...] = jnp.zeros_like(dlb_ref)
        tri = _tri(True)
        tri_t = _tri(False)
        row8 = lax.broadcasted_iota(jnp.int32, (8, HG_DK), 0)
        row_group = lax.broadcasted_iota(jnp.int32, (CHUNK, CHUNK), 0) // GROUP
        col_group = lax.broadcasted_iota(jnp.int32, (CHUNK, CHUNK), 1) // GROUP
        earlier_group = col_group < row_group
        later_group = col_group > row_group

        def chunk(c, dq_ref, dk_ref, dv_ref):
            rows = pl.ds(pl.multiple_of(c * CHUNK, CHUNK), CHUNK)
            hq = hq_ref[rows, :]
            q, sq, f, g, lg, kk = _hg_gates(hq, hf_ref[rows, :], lb)
            v = hi_ref[rows, :]
            do = do_ref[rows, :]
            b = _dot_hi(tri, lg, NN)
            eb = jnp.exp(b)
            bl = b[CHUNK - 1:CHUNK]
            ebl = jnp.exp(bl)
            ekb = jnp.exp(bl - b)
            qe = q * eb
            ke = kk * ekb
            st = sall_ref[c]
            dst = dst_ref[...]
            dqe = _dot_hi(do, st, NN)
            dke = _dot_hi(v, dst, NN)
            dv_inter = _dot_hi(ke, dst, NT)
            d_ebl = jnp.sum(st * dst, axis=0, keepdims=True)
            dst_ref[...] = dst * ebl + _dot_hi(do, qe, TN)

            dk_ref[...] = jnp.zeros_like(dk_ref)
            dv_ref[...] = jnp.zeros_like(dv_ref)
            for grp in range(N_GROUPS):
                r0 = GROUP * grp
                for h8 in range(GROUP // 8):
                    n = 8 * (h8 + 1)
                    bs, ks, vs = b[r0:r0 + n], kk[r0:r0 + n], v[r0:r0 + n]
                    sidx = lax.broadcasted_iota(jnp.int32, (n, HG_DK), 0)
                    blk = jnp.zeros((8, HG_DK), F32)
                    for i in range(8):
                        t = r0 + 8 * h8 + i
                        qt = q[t:t + 1]
                        dot_ = do[t:t + 1]
                        e = jnp.where(sidx <= 8 * h8 + i, jnp.exp(b[t:t + 1] - bs), 0.0)
                        w = e * ks
                        p = jnp.sum(w * qt, axis=1, keepdims=True)
                        dsc = jnp.sum(vs * dot_, axis=1, keepdims=True)
                        dqt = jnp.sum(dsc * w, axis=0, keepdims=True)
                        blk = blk + jnp.where(row8 == i, dqt, 0.0)
                        dk_ref[r0:r0 + n, :] += dsc * e * qt
                        dv_ref[r0:r0 + n, :] += p * dot_
                    dq_ref[r0 + 8 * h8:r0 + n, :] = blk
            ds_far = jnp.where(earlier_group, _dot_bf16(do, v, NT), 0.0)
            ds_far_t = jnp.where(later_group, _dot_bf16(v, do, NT), 0.0)
            dq_far, dk_far = [jnp.zeros((GROUP, HG_DK), F32)], []
            for grp in range(1, N_GROUPS):
                r0 = GROUP * grp
                _, ks, decay = _by_query_group(q, kk, b, grp)
                dq_far.append(decay * _dot_hi(ds_far[r0:r0 + GROUP], ks, NN))
                qs, _, decay = _by_key_group(q, kk, b, grp - 1)
                dk_far.append(decay * _dot_hi(ds_far_t[r0 - GROUP:r0], qs, NN))
            dk_far.append(jnp.zeros((GROUP, HG_DK), F32))
            dv_far = _dot_bf16(_scores_between_groups(q, kk, b), do, TN)
            dq_i = dq_ref[...] + jnp.concatenate(dq_far, axis=0)
            dk_i = dk_ref[...] + jnp.concatenate(dk_far, axis=0)
            dke_ke = dke * ke
            db = q * dq_i - kk * dk_i + dqe * qe - dke_ke
            db_last = jnp.sum(dke_ke, axis=0, keepdims=True) + d_ebl * ebl
            dlg = _dot_hi(tri_t, db, NN) + db_last
            dq = dq_i + dqe * eb
            dkk = dk_i + dke * ekb
            dg = dlg / g - dkk
            dhq_ref[rows, :] = (dq * (HG_DK ** -0.5) * (sq * (1.0 + hq * (1.0 - sq)))).astype(BF16)
            dhf_ref[rows, :] = (dg * (1.0 - lb) * f * (1.0 - f)).astype(BF16)
            dhi_ref[rows, :] = (dv_ref[...] + dv_far + dv_inter).astype(BF16)
            dlb_ref[...] += jnp.sum(dg * (1.0 - f), axis=0, keepdims=True)

        def two_chunks(i, carry):
            chunk(n_chunks - 1 - 2 * i, *per_chunk[:3])
            chunk(n_chunks - 2 - 2 * i, *per_chunk[3:])
            return carry

        lax.fori_loop(0, n_chunks // 2, two_chunks, 0)
        dl0 = dlb_ref[...] * lb * (1.0 - lb)
        dlbl_ref[0:1, :] = dl0
        dlbl_ref[1:2, :] = -dl0

    def col(base):
        return pl.BlockSpec((T, HG_DK), lambda h: (0, base + h))

    outb = jax.ShapeDtypeStruct((T, HG_WIDTH), BF16)
    return _call(
        body, (z, z, z, z, lb_logits, hg_norm_w, o_raw, s_all, dya), name=name, grid=(HG_HEADS,),
        in_specs=[col(COL_HQ), col(COL_HF), col(COL_HI), col(COL_HG),
                  pl.BlockSpec((2, HG_DK), lambda h: (0, h)), pl.BlockSpec((1, HG_DK), lambda h: (0, 0)),
                  col(0), pl.BlockSpec((None, n_chunks, HG_DK, HG_DK), lambda h: (h, 0, 0, 0)), col(0)],
        out_specs=[col(0), col(0), col(0), col(0), pl.BlockSpec((2, HG_DK), lambda h: (0, h)),
                   pl.BlockSpec((8, HG_DK), lambda h: (0, 0))],
        out_shape=[outb, outb, outb, outb, jax.ShapeDtypeStruct((2, HG_WIDTH), F32),
                   jax.ShapeDtypeStruct((8, HG_DK), F32)],
        scratch_shapes=[pltpu.VMEM((T, HG_DK), F32), pltpu.VMEM((HG_DK, HG_DK), F32), pltpu.VMEM((1, HG_DK), F32)]
        + [pltpu.VMEM((CHUNK, HG_DK), F32)] * 6,
        sem=("arbitrary",), after=after)


CONST_KEYS = PAD - REL_CLIP
VAR_KEYS = BAND - CONST_KEYS
REL_LO = 128
REL_SPAN = N_REL_PAD - REL_LO


def _rel_onehot(t):
    r = lax.broadcasted_iota(jnp.int32, (REL_SPAN, VAR_KEYS), 0)
    j = lax.broadcasted_iota(jnp.int32, (REL_SPAN, VAR_KEYS), 1)
    idx = jnp.clip(t + PAD - CONST_KEYS - j, -REL_CLIP, REL_CLIP) + REL_CLIP - REL_LO
    return jnp.where(r == idx, 1.0, 0.0).astype(BF16)


def _split3(x):
    hi = x.astype(BF16)
    r1 = x - hi.astype(F32)
    mid = r1.astype(BF16)
    return hi, mid, (r1 - mid.astype(F32)).astype(BF16)


def _bias_expand(rel, name):
    def body(rel_ref, out_ref):
        tab = rel_ref[...]
        onehot = _rel_onehot(pl.program_id(0))
        out_ref[:, 0:CONST_KEYS] = jnp.broadcast_to(tab[:, 2 * REL_CLIP:2 * REL_CLIP + 1], (AT_HEADS, CONST_KEYS))
        out_ref[:, CONST_KEYS:BAND] = sum(
            jnp.dot(piece, onehot, preferred_element_type=F32) for piece in _split3(tab[:, REL_LO:N_REL_PAD]))

    return pl.pallas_call(
        body, name=name, grid=(CHUNK,),
        in_specs=[pl.BlockSpec((AT_HEADS, N_REL_PAD), lambda t: (0, 0))],
        out_specs=pl.BlockSpec((None, AT_HEADS, BAND), lambda t: (t, 0, 0)),
        out_shape=jax.ShapeDtypeStruct((CHUNK, AT_HEADS, BAND), F32),
        compiler_params=_cparams(("parallel",)),
    )(rel)


def _bias_reduce(dbias_rows, name, after=()):
    def body(db_ref, out_ref):
        lane = lax.broadcasted_iota(jnp.int32, (AT_HEADS, N_REL_PAD), 1)
        varying = lane >= CONST_KEYS
        by_offset = jnp.zeros((AT_HEADS, N_REL_PAD), F32)
        constant = jnp.zeros((AT_HEADS, N_REL_PAD), F32)
        for t in range(CHUNK):
            row = db_ref[t]
            constant = constant + jnp.where(varying, 0.0, row)
            moved = jnp.where(varying, row, 0.0)
            by_offset = by_offset + (pltpu.roll(moved, N_REL_PAD - t, axis=1) if t else moved)
        offset = lax.broadcasted_iota(jnp.int32, (N_REL_PAD, N_REL_PAD), 0)
        entry = lax.broadcasted_iota(jnp.int32, (N_REL_PAD, N_REL_PAD), 1)
        onehot = jnp.where(entry == jnp.clip(PAD - offset, -REL_CLIP, REL_CLIP) + REL_CLIP, 1.0, 0.0).astype(BF16)
        acc = sum(jnp.dot(piece, onehot, preferred_element_type=F32) for piece in _split3(by_offset))
        last = jnp.sum(constant, axis=1, keepdims=True)
        out_ref[...] = acc + jnp.where(lane == 2 * REL_CLIP, last, 0.0)

    whole = pl.BlockSpec((CHUNK, AT_HEADS, N_REL_PAD), lambda i: (0, 0, 0))
    return _call(
        body, (dbias_rows,), name=name, grid=(1,), in_specs=[whole],
        out_specs=[pl.BlockSpec((AT_HEADS, N_REL_PAD), lambda i: (0, 0))],
        out_shape=[jax.ShapeDtypeStruct((AT_HEADS, N_REL_PAD), F32)],
        sem=("arbitrary",), after=after)[0]


def _pair_lanes():
    return lax.broadcasted_iota(jnp.int32, (CHUNK, 2 * AT_DH), 1) < AT_DH


def _block_diag(a):
    first = _pair_lanes()
    return jnp.concatenate([jnp.where(first, a, 0.0), jnp.where(first, 0.0, a)], axis=0).astype(BF16)


def _diag_blocks(a):
    return jnp.where(_pair_lanes(), a[:CHUNK], a[CHUNK:])


def _band_probs_t(kb, qbd, bias_t, c):
    s = lax.dot_general(kb, qbd, (NT, ((), ())), preferred_element_type=F32) * (AT_DH ** -0.5) + bias_t
    j = lax.broadcasted_iota(jnp.int32, (BAND, 2 * AT_DH), 0)
    s = jnp.where(j + c * CHUNK >= PAD, s, -jnp.inf)
    p = jnp.exp(s - jnp.max(s, axis=0, keepdims=True))
    return p / jnp.sum(p, axis=0, keepdims=True)


def _fill_padded(dst_ref, src_ref, T):
    dst_ref[0:PAD, :] = jnp.zeros((PAD, 2 * AT_DH), BF16)
    dst_ref[PAD:PAD + T, :] = src_ref[...].astype(BF16)


def _attn_fwd(z, bias_t, name, after=()):
    T = z.shape[0]
    n_chunks = T // CHUNK

    def body(q_ref, k_ref, v_ref, bias_ref, y_ref, p_ref, *scratch):
        for pr in range(2):
            lanes = slice(128 * pr, 128 * (pr + 1))
            for dst_ref, src_ref in zip(scratch[2 * pr:2 * pr + 2], (k_ref, v_ref)):
                dst_ref[0:PAD, :] = jnp.zeros((PAD, 128), BF16)
                dst_ref[PAD:PAD + T, :] = src_ref[:, lanes].astype(BF16)

        def chunk(c, carry):
            rows = pl.ds(pl.multiple_of(c * CHUNK, CHUNK), CHUNK)
            band = pl.ds(pl.multiple_of(c * CHUNK, CHUNK), BAND)
            for pr in range(2):
                kp_ref, vp_ref = scratch[2 * pr:2 * pr + 2]
                lanes = slice(128 * pr, 128 * (pr + 1))
                p = _band_probs_t(kp_ref[band, :], _block_diag(q_ref[rows, lanes]), bias_ref[pr], c).astype(BF16)
                p_ref[pr, c] = p
                o2 = lax.dot_general(p, vp_ref[band, :], (TN, ((), ())), preferred_element_type=F32)
                y_ref[rows, lanes] = _diag_blocks(o2).astype(BF16)
            return carry

        lax.fori_loop(0, n_chunks, chunk, 0, unroll=2)

    def col(base):
        return pl.BlockSpec((T, 256), lambda h: (0, base // 2 + h))

    return _call(
        body, (z, z, z, bias_t), name=name, grid=(AT_HEADS // 4,),
        in_specs=[col(COL_AQ), col(COL_AK), col(COL_AV), pl.BlockSpec((2, BAND, 128), lambda h: (h, 0, 0))],
        out_specs=[col(0), pl.BlockSpec((2, n_chunks, BAND, 128), lambda h: (h, 0, 0, 0))],
        out_shape=[jax.ShapeDtypeStruct((T, AT_WIDTH), BF16),
                   jax.ShapeDtypeStruct((AT_HEADS // 2, n_chunks, BAND, 128), BF16)],
        scratch_shapes=[pltpu.VMEM((PAD + T, 128), BF16)] * 4,
        sem=("parallel",), after=after)


def _attn_bwd(z, probs, dyb, name, after=()):
    T = z.shape[0]
    n_chunks = T // CHUNK

    def body(q_ref, k_ref, v_ref, p_ref, dy_ref, dq_ref, dk_ref, dv_ref, dbias_ref, *scratch):
        dbias_ref[...] = jnp.zeros_like(dbias_ref)
        for pr in range(2):
            kp_ref, vp_ref, dkp_ref, dvp_ref = scratch[4 * pr:4 * pr + 4]
            lanes = slice(128 * pr, 128 * (pr + 1))
            kp_ref[0:PAD, :] = jnp.zeros((PAD, 128), BF16)
            vp_ref[0:PAD, :] = jnp.zeros((PAD, 128), BF16)
            kp_ref[PAD:PAD + T, :] = k_ref[:, lanes].astype(BF16)
            vp_ref[PAD:PAD + T, :] = v_ref[:, lanes].astype(BF16)
            dkp_ref[...] = jnp.zeros_like(dkp_ref)
            dvp_ref[...] = jnp.zeros_like(dvp_ref)

        def chunk(c, carry):
            rows = pl.ds(pl.multiple_of(c * CHUNK, CHUNK), CHUNK)
            band = pl.ds(pl.multiple_of(c * CHUNK, CHUNK), BAND)
            for pr in range(2):
                kp_ref, vp_ref, dkp_ref, dvp_ref = scratch[4 * pr:4 * pr + 4]
                lanes = slice(128 * pr, 128 * (pr + 1))
                qbd = _block_diag(q_ref[rows, lanes])
                dobd = _block_diag(dy_ref[rows, lanes])
                pb = p_ref[pr, c]
                p = pb.astype(F32)
                dp = lax.dot_general(vp_ref[band, :], dobd, (NT, ((), ())), preferred_element_type=F32)
                ds = p * (dp - jnp.sum(dp * p, axis=0, keepdims=True))
                dbias_ref[pr] += ds
                dsb = ds.astype(BF16)
                dq2 = lax.dot_general(dsb, kp_ref[band, :], (TN, ((), ())), preferred_element_type=F32)
                dq_ref[rows, lanes] = (_diag_blocks(dq2) * (AT_DH ** -0.5)).astype(BF16)
                dkp_ref[band, :] += jnp.dot(dsb, qbd, preferred_element_type=F32) * (AT_DH ** -0.5)
                dvp_ref[band, :] += jnp.dot(pb, dobd, preferred_element_type=F32)
            return carry

        lax.fori_loop(0, n_chunks, chunk, 0)
        for pr in range(2):
            lanes = slice(128 * pr, 128 * (pr + 1))
            dk_ref[:, lanes] = scratch[4 * pr + 2][PAD:PAD + T, :].astype(BF16)
            dv_ref[:, lanes] = scratch[4 * pr + 3][PAD:PAD + T, :].astype(BF16)

    def col(base):
        return pl.BlockSpec((T, 256), lambda h: (0, base // 2 + h))

    outb = jax.ShapeDtypeStruct((T, AT_WIDTH), BF16)
    return _call(
        body, (z, z, z, probs, dyb), name=name, grid=(AT_HEADS // 4,),
        in_specs=[col(COL_AQ), col(COL_AK), col(COL_AV),
                  pl.BlockSpec((2, n_chunks, BAND, 128), lambda h: (h, 0, 0, 0)), col(0)],
        out_specs=[col(0), col(0), col(0), pl.BlockSpec((2, BAND, 128), lambda h: (h, 0, 0))],
        out_shape=[outb, outb, outb, jax.ShapeDtypeStruct((AT_HEADS // 2, BAND, 128), F32)],
        scratch_shapes=([pltpu.VMEM((PAD + T, 128), BF16)] * 2 + [pltpu.VMEM((PAD + T, 128), F32)] * 2) * 2,
        sem=("parallel",), after=after)


def _local_step(x, target, lb_logits, hg_norm_w, rel_bias, norm_mix_w, norm_mlp_w, norm_final_w,
                w_in, rest, exchanges=None):
    ex = exchanges
    rel = jnp.pad(rel_bias, ((0, 0), (0, N_REL_PAD - N_REL)))

    u = _rms_fwd(x, norm_mix_w, "rms_mix_fwd")
    if ex:
        z, w_in = _mm_gathered(u, w_in, ex.order, "mm_in_fwd")
        gather = _Gather(rest, [z], "ag")
        tok = [gather.token]
    else:
        z = _mm_nn(u, w_in, F32, "mm_in_fwd")
        w_a, w_b, w_out, w_up, w_down = rest
        tok = []
    o_raw, y_a, s_all = _hgrn2_fwd(z, lb_logits, hg_norm_w, "hgrn2_fwd", after=tok)
    if ex:
        tok = [gather.pass_on([0, 1, 2], [o_raw], "abo")]
    bias_rows = _bias_expand(rel, "bias_expand")
    bias_t = jnp.transpose(bias_rows.reshape(CHUNK, AT_HEADS // 2, 2, BAND), (1, 3, 2, 0)).reshape(
        AT_HEADS // 2, BAND, 2 * CHUNK)
    y_b, probs = _attn_fwd(z, bias_t, "attn_fwd", after=tok)
    if ex:
        tok = [gather.pass_on([3], [y_b], "up")]
        w_a, w_b, w_out = gather.finish([0, 1, 2], tok, "abo")
    pa = _mm_nn(y_a, w_a, F32, "mm_a_fwd")
    pb, merged = _mm_nn(y_b, w_b, None, "mm_b_fwd", epilogue=(
        (z, z, pa), (COL_GATE_A * GATE_TILE, COL_GATE_B * GATE_TILE, 0), (F32, BF16), _gated_merge))
    w_out1 = w_out.reshape(1, D_MODEL, D_MODEL)
    h1, u2 = _mm_rows(merged, w_out.reshape(D_MODEL, D_MODEL), False, [x], [norm_mlp_w], (F32, BF16), (),
                      _residual_rms_rows, "mm_out_fwd")
    if ex:
        tok = [gather.pass_on([4], [u2], "down")]
        w_up, = gather.finish([3], tok, "up")
    a, r = _mm_nn(u2, w_up, None, "mm_up_fwd", epilogue=((), (), (F32, BF16), _squared_relu))
    if ex:
        w_down, = gather.finish([4], [r], "down")
    w_down1 = w_down.reshape(1, D_FF, D_MODEL)
    mlp = _mm_nn(r, w_down1, F32, "mm_down_fwd")
    loss, dh2, dh2b, g_nf = _loss_head(h1, mlp, norm_final_w, target, "loss_head")

    def reduce_scatter(grads, name):
        rs = _ReduceScatter(grads, ex.parity, name) if ex else None
        return rs, ([rs.token] if ex else [])

    g_down = _mm_tn(r, dh2b, 1, BF16, "mm_down_wgrad").reshape(N_DEV, D_FF // N_DEV, D_MODEL)
    rs_down, tok = reduce_scatter([g_down], "rs_down")
    da, = _mm_nt(dh2b, w_down1, None, "mm_down_dgrad", after=tok, epilogue=(
        (a,), (0,), (BF16,), lambda dr, av: (dr * (2.0 * jnp.maximum(av, 0.0)),)))
    tok = [rs_down.pair_sums([da])] if ex else []
    g_up = _mm_tn(u2, da, N_DEV, BF16, "mm_up_wgrad", after=tok)
    rs_up, tok = reduce_scatter([g_up], "rs_up")
    du2 = _mm_nt(da, w_up, F32, "mm_up_dgrad", after=tok)
    tok = [rs_up.pair_sums([du2])] if ex else []
    dh1, dh1b, g_nmlp = _rms_bwd(du2, h1, norm_mlp_w, dh2, "rms_mlp_bwd", after=tok)

    g_out = _mm_tn(merged, dh1b, 1, BF16, "mm_out_wgrad").reshape(N_DEV, D_MODEL // N_DEV, D_MODEL)
    dpa, dpb, dga, dgb = _mm_nt(dh1b, w_out1, None, "mm_out_dgrad", epilogue=(
        (z, z, pa, pb), (COL_GATE_A * GATE_TILE, COL_GATE_B * GATE_TILE, 0, 0), (BF16,) * 4, _merge_grads))
    g_a = _mm_tn(y_a, dpa, N_DEV, BF16, "mm_a_wgrad")
    g_b = _mm_tn(y_b, dpb, N_DEV, BF16, "mm_b_wgrad")
    rs_mix, tok = reduce_scatter([g_a, g_b, g_out], "rs_mix")
    dya = _mm_nt(dpa, w_a, F32, "mm_a_dgrad", after=tok)
    dyb = _mm_nt(dpb, w_b, F32, "mm_b_dgrad", after=tok)
    tok = [rs_mix.pair_sums([dya, dyb])] if ex else []
    daq, dak, dav, dbias_t = _attn_bwd(z, probs, dyb, "attn_bwd", after=tok)
    dhq, dhf, dhi, dhg, g_lbl, g_hgw = _hgrn2_bwd(z, lb_logits, hg_norm_w, o_raw, s_all, dya, "hgrn2_bwd",
                                                  after=tok)
    dbias_rows = jnp.pad(jnp.transpose(dbias_t.reshape(AT_HEADS // 2, BAND, 2, CHUNK), (3, 0, 2, 1)).reshape(
        CHUNK, AT_HEADS, BAND), ((0, 0), (0, 0), (0, N_REL_PAD - BAND)))
    dz =jnp.concatenate([dhq, dhf, dhi, dhg, daq, dak, dav, dga, dgb], axis=1)
    g_in = _mm_tn(u, dz, N_DEV, BF16, "mm_in_wgrad")
    rs_in, _ = reduce_scatter([g_in], "rs_in")
    tok = [rs_in.pair_sums([])] if ex else []
    du = _mm_nt(dz, w_in, F32, "mm_in_dgrad", after=tok)
    grad_x, _, g_nmix = _rms_bwd(du, x, norm_mix_w, dh1, "rms_mix_bwd")
    g_rel = _bias_reduce(dbias_rows, "bias_reduce", after=tok)[:, :N_REL]

    small = dict(lb_logits=g_lbl, hg_norm_w=g_hgw[0:1], rel_bias=g_rel, norm_mix_w=g_nmix, norm_mlp_w=g_nmlp,
                 norm_final_w=g_nf)
    grads = [rs_in, rs_mix, rs_up, rs_down] if ex else [g_in, g_a, g_b, g_out, g_up, g_down]
    return loss, grad_x, grads, small


def _gather_exchange(shards, mid_step=None):
    n = len(shards)

    def parts(ins, outs, sems):
        send_sems, recv_sems, local_sems = sems
        x, y, c = _position()
        chips = [(1 - x, y), (x, 1 - y), (1 - x, 1 - y)]

        def copy(w, k, block, to, src=None):
            dst = outs[w].at[4 * block[0] + 2 * block[1] + block[2]]
            return pltpu.make_async_remote_copy(
                src_ref=dst if src is None else src, dst_ref=dst,
                send_sem=send_sems.at[w, k], recv_sem=recv_sems.at[w, k], device_id=to, device_id_type=MESH)

        def local(w):
            return pltpu.make_async_copy(ins[w], outs[w].at[4 * x + 2 * y + c], local_sems.at[w])

        return (x, y, c), (x, y, 1 - c), chips, copy, local

    def start(ins, outs, sems):
        me, sibling, chips, copy, local = parts(ins, outs, sems)
        for w in range(n):
            local(w).start()
        for w in range(n):
            copy(w, 0, me, sibling, src=ins[w]).start()
            for j, chip in enumerate(chips):
                copy(w, 1 + j, me, (*chip, me[2]), src=ins[w]).start()

    def mid(ins, outs, sems):
        me, sibling, chips, copy, _ = parts(ins, outs, sems)
        for w in range(n):
            for j, chip in enumerate(chips):
                copy(w, 1 + j, (*chip, me[2]), me).wait_recv()
                copy(w, 4 + j, (*chip, me[2]), sibling).start()

    def end(ins, outs, sems):
        me, sibling, chips, copy, local = parts(ins, outs, sems)
        for w in range(n):
            copy(w, 0, sibling, me).wait_recv()
            for j, chip in enumerate(chips):
                copy(w, 4 + j, (*chip, sibling[2]), me).wait_recv()
        for w in range(n):
            for k in range(7):
                copy(w, k, me, sibling).wait_send()
            local(w).wait()

    return _Exchange(
        shards, [jax.ShapeDtypeStruct((N_DEV,) + s.shape, s.dtype) for s in shards],
        [pltpu.SemaphoreType.DMA((n, 7)), pltpu.SemaphoreType.DMA((n, 7)), pltpu.SemaphoreType.DMA((n,))],
        start, end, mid, mid_step)


def _mm_gathered(u, shard, order, name):
    T, K = u.shape
    _, Nb = shard.shape

    def body(order_ref, u_ref, shard_ref, z_ref, full_ref, wbuf, load_sem, send_sems, recv_sems, local_sem):
        s = pl.program_id(0)
        x, y, c = _position()
        me, sibling = (x, y, c), (x, y, 1 - c)
        chips = [(1 - x, y), (x, 1 - y), (1 - x, 1 - y)]

        def copy(k, block, to, src=None):
            dst = full_ref.at[4 * block[0] + 2 * block[1] + block[2]]
            return pltpu.make_async_remote_copy(
                src_ref=dst if src is None else src, dst_ref=dst,
                send_sem=send_sems.at[k], recv_sem=recv_sems.at[k], device_id=to, device_id_type=MESH)

        @pl.when(s == 0)
        def _():
            local = pltpu.make_async_copy(shard_ref, full_ref.at[4 * x + 2 * y + c], local_sem)
            local.start()
            copy(0, me, sibling, src=shard_ref).start()
            for j, chip in enumerate(chips):
                copy(1 + j, me, (*chip, c), src=shard_ref).start()
            local.wait()

        @pl.when(s == 1)
        def _():
            copy(0, sibling, me).wait_recv()

        for j, chip in enumerate(chips):
            direct, passed = ((2, 4), (3, 5), (6, 7))[j]

            @pl.when(s == direct)
            def _(j=j, chip=chip):
                copy(1 + j, (*chip, c), me).wait_recv()
                copy(4 + j, (*chip, c), sibling).start()

            @pl.when(s == passed)
            def _(j=j, chip=chip):
                copy(4 + j, (*chip, 1 - c), me).wait_recv()

        load = pltpu.make_async_copy(full_ref.at[order_ref[s]], wbuf, load_sem)
        load.start()
        load.wait()
        z_ref[...] = jnp.dot(u_ref[...], wbuf[...], preferred_element_type=F32)

        @pl.when(s == N_DEV - 1)
        def _():
            for k in range(7):
                copy(k, me, sibling).wait_send()

    return pl.pallas_call(
        body, name=name,
        grid_spec=pltpu.PrefetchScalarGridSpec(
            num_scalar_prefetch=1, grid=(N_DEV,),
            in_specs=[pl.BlockSpec((T, K), lambda s, order: (0, 0)), ANY],
            out_specs=[pl.BlockSpec((T, Nb), lambda s, order: (0, order[s])), ANY],
            scratch_shapes=[pltpu.VMEM((K, Nb), BF16), pltpu.SemaphoreType.DMA,
                            pltpu.SemaphoreType.DMA((7,)), pltpu.SemaphoreType.DMA((7,)), pltpu.SemaphoreType.DMA]),
        out_shape=[jax.ShapeDtypeStruct((T, N_DEV * Nb), F32), jax.ShapeDtypeStruct((N_DEV, K, Nb), BF16)],
        compiler_params=_cparams(("arbitrary",)),
    )(order, u, shard)


def _gather_order():
    x, y, c = _position()
    chips = [(1 - x, y), (x, 1 - y), (1 - x, 1 - y)]
    ids = [4 * x + 2 * y + c, 4 * x + 2 * y + (1 - c)]
    ids += [4 * cx + 2 * cy + c for cx, cy in chips[:2]] + [4 * cx + 2 * cy + (1 - c) for cx, cy in chips[:2]]
    ids += [4 * chips[2][0] + 2 * chips[2][1] + c, 4 * chips[2][0] + 2 * chips[2][1] + (1 - c)]
    return jnp.stack(ids).astype(jnp.int32)


def _run_exchange(comm, name):
    n_i, n_o = len(comm.arrays), len(comm.out_shape)

    def body(*refs):
        ins, outs, sems = refs[:n_i], refs[n_i:n_i + n_o], refs[n_i + n_o:]
        comm.start(ins, outs, sems)
        if comm.mid is not None:
            comm.mid(ins, outs, sems)
        comm.end(ins, outs, sems)

    return pl.pallas_call(
        body, name=name, in_specs=[ANY] * n_i, out_specs=[ANY] * n_o, out_shape=comm.out_shape,
        scratch_shapes=comm.scratch)(*comm.arrays)


def _exchange_sibling(grads, name):
    n = len(grads)

    def body(*refs):
        ins, outs = refs[:n], refs[n:2 * n]
        send_sems, recv_sems = refs[2 * n:]
        x, y, c = _position()
        copies = []
        for w in range(n):
            for s in range(N_CHIP):
                cp = pltpu.make_async_remote_copy(
                    src_ref=ins[w].at[2 * s + (1 - c)], dst_ref=outs[w].at[s],
                    send_sem=send_sems.at[w, s], recv_sem=recv_sems.at[w, s],
                    device_id=(x, y, 1 - c), device_id_type=MESH)
                cp.start()
                copies.append(cp)
        for cp in copies:
            cp.wait()

    return pl.pallas_call(
        body, name=name,
        in_specs=[ANY] * n, out_specs=[ANY] * n,
        out_shape=[jax.ShapeDtypeStruct((N_CHIP,) + g.shape[1:], g.dtype) for g in grads],
        scratch_shapes=[pltpu.SemaphoreType.DMA((n, N_CHIP)), pltpu.SemaphoreType.DMA((n, N_CHIP))],
    )(*grads)


def _pair_sum(g, land, parity, name):
    _, R, C = g.shape
    tr = _pick(R, (512, 256))

    def body(par_ref, g_ref, l_ref, o_ref):
        o_ref[...] = (g_ref[...].astype(F32) + l_ref[...].astype(F32)).astype(BF16)

    return pl.pallas_call(
        body, name=name,
        grid_spec=pltpu.PrefetchScalarGridSpec(
            num_scalar_prefetch=1, grid=(N_CHIP, R // tr),
            in_specs=[pl.BlockSpec((None, tr, C), lambda s, i, par: (2 * s + par[0], i, 0)),
                      pl.BlockSpec((None, tr, C), lambda s, i, par: (s, i, 0))],
            out_specs=pl.BlockSpec((None, tr, C), lambda s, i, par: (s, i, 0))),
        out_shape=jax.ShapeDtypeStruct((N_CHIP, R, C), BF16),
        compiler_params=_cparams(("parallel", "parallel")),
    )(parity, g, land)


def _scatter_exchange(partials):
    n = len(partials)

    def copies(ins, outs, sems):
        send_sems, recv_sems, local_sems = sems
        x, y, c = _position()
        chips = [(1 - x, y), (x, 1 - y), (1 - x, 1 - y)]
        my_slot = 2 * x + y
        local = [pltpu.make_async_copy(ins[w].at[my_slot], outs[w].at[my_slot], local_sems.at[w]) for w in range(n)]
        remote = [pltpu.make_async_remote_copy(
            src_ref=ins[w].at[2 * chip[0] + chip[1]], dst_ref=outs[w].at[my_slot],
            send_sem=send_sems.at[w, j], recv_sem=recv_sems.at[w, j], device_id=(*chip, c), device_id_type=MESH)
            for w in range(n) for j, chip in enumerate(chips)]
        return local, remote

    def start(ins, outs, sems):
        local, remote = copies(ins, outs, sems)
        for cp in local + remote:
            cp.start()

    def end(ins, outs, sems):
        local, remote = copies(ins, outs, sems)
        for cp in remote + local:
            cp.wait()

    return _Exchange(
        partials, [jax.ShapeDtypeStruct(p.shape, p.dtype) for p in partials],
        [pltpu.SemaphoreType.DMA((n, 3)), pltpu.SemaphoreType.DMA((n, 3)), pltpu.SemaphoreType.DMA((n,))],
        start, end)


HBM = pl.BlockSpec(memory_space=pltpu.HBM)
SEM = pl.BlockSpec(memory_space=pltpu.SEMAPHORE)
DATAFLOW = pltpu.SideEffectType.DATAFLOW_SIDE_EFFECTING


def _scatter_copies(ins, lands, send_sems, recv_sems):
    x, y, c = _position()
    chips = [(1 - x, y), (x, 1 - y), (1 - x, 1 - y)]
    return [pltpu.make_async_remote_copy(
        src_ref=ins[w].at[2 * chip[0] + chip[1]], dst_ref=lands[w].at[2 * x + y],
        send_sem=send_sems[3 * w + j], recv_sem=recv_sems[3 * w + j], device_id=(*chip, c), device_id_type=MESH)
        for w in range(len(ins)) for j, chip in enumerate(chips)]


def _scatter_start(partials, name):
    n = len(partials)

    def body(*refs):
        ins, lands = refs[:n], refs[n:2 * n]
        sems = refs[4 * n:10 * n]
        for cp in _scatter_copies(ins, lands, sems[:3 * n], sems[3 * n:]):
            cp.start()
        refs[-1][...] = jnp.zeros_like(refs[-1])

    def in_hbm(a):
        return pltpu.with_memory_space_constraint(a, pltpu.HBM)

    bufs = tuple(pltpu.HBM(p.shape, p.dtype) for p in partials)
    outs = pl.pallas_call(
        body, name=name,
        out_shape=bufs + bufs + (pltpu.SemaphoreType.DMA(()),) * (6 * n) + (jax.ShapeDtypeStruct((8, 128), F32),),
        in_specs=[HBM] * (2 * n),
        out_specs=(HBM,) * (2 * n) + (SEM,) * (6 * n) + (pl.BlockSpec(memory_space=pltpu.VMEM),),
        input_output_aliases={i: i for i in range(2 * n)},
        compiler_params=pltpu.CompilerParams(has_side_effects=DATAFLOW),
    )(*[in_hbm(p) for p in partials], *[in_hbm(lax.empty(p.shape, p.dtype)) for p in partials])
    return list(outs[:-1]), outs[-1]


def _scatter_wait(handle, after, name):
    n = len(handle) // 8
    bufs, sems = handle[:2 * n], handle[2 * n:]

    def body(*refs):
        ins, lands = refs[:n], refs[n:2 * n]
        sems = refs[2 * n:8 * n]
        for cp in _scatter_copies(ins, lands, sems[:3 * n], sems[3 * n:]):
            cp.wait_send()
            cp.wait_recv()

    outs = pl.pallas_call(
        body, name=name,
        out_shape=tuple(pltpu.HBM(b.shape, b.dtype) for b in bufs),
        in_specs=[HBM] * (2 * n) + [SEM] * (6 * n) + [ANY] * len(after), out_specs=(HBM,) * (2 * n),
        input_output_aliases={i: i for i in range(2 * n)},
        compiler_params=pltpu.CompilerParams(has_side_effects=DATAFLOW),
    )(*bufs, *sems, *after)
    return list(outs[:n]), list(outs[n:])


def _split_call(name, bufs, waits=(), starts=None, after=()):
    nb = len(bufs)
    n_new = starts[1] if starts else 0
    wait_sems = [s for w in waits for s in (*w[1], *w[2])]

    def body(*refs):
        b, pos = refs[:nb], nb
        for plan, ss, _, send_idx, recv_idx in waits:
            k = len(ss)
            copies = plan(b, refs[pos:pos + k], refs[pos + k:pos + 2 * k])
            pos += 2 * k
            for i in recv_idx:
                copies[i].wait_recv()
            for i in send_idx:
                copies[i].wait_send()
        outs = refs[pos + len(after):]
        if starts:
            for cp in starts[0](b, outs[nb:nb + n_new], outs[nb + n_new:nb + 2 * n_new]):
                cp.start()
        outs[-1][...] = jnp.zeros_like(outs[-1])

    res = pl.pallas_call(
        body, name=name,
        out_shape=tuple(pltpu.HBM(a.shape, a.dtype) for a in bufs) + (pltpu.SemaphoreType.DMA(()),) * (2 * n_new)
        + (jax.ShapeDtypeStruct((8, 128), F32),),
        in_specs=[HBM] * nb + [SEM] * len(wait_sems) + [ANY] * len(after),
        out_specs=(HBM,) * nb + (SEM,) * (2 * n_new) + (pl.BlockSpec(memory_space=pltpu.VMEM),),
        input_output_aliases={i: i for i in range(nb)},
        compiler_params=pltpu.CompilerParams(has_side_effects=DATAFLOW),
    )(*bufs, *wait_sems, *after)
    return list(res[:nb]), list(res[nb:nb + n_new]), list(res[nb + n_new:nb + 2 * n_new]), res[-1]


def _in_hbm(a):
    return pltpu.with_memory_space_constraint(a, pltpu.HBM)


def _remote(src, dst, send_sem, recv_sem, to):
    return pltpu.make_async_remote_copy(src_ref=src, dst_ref=dst, send_sem=send_sem, recv_sem=recv_sem,
                                        device_id=to, device_id_type=MESH)


def _other_chips():
    x, y, _ = _position()
    return [(1 - x, y), (x, 1 - y), (1 - x, 1 - y)]


def _plan_gather_first(n):
    def plan(b, ss, rs):
        x, y, c = _position()
        to = [(x, y, 1 - c)] + [(*chip, c) for chip in _other_chips()]
        return [_remote(b[w], b[n + w].at[4 * x + 2 * y + c], ss[4 * w + k], rs[4 * w + k], to[k])
                for w in range(n) for k in range(4)]
    return plan, 4 * n


def _plan_gather_pass(n):
    def plan(b, ss, rs):
        x, y, c = _position()
        copies = []
        for w in range(n):
            for j, chip in enumerate(_other_chips()):
                blk = b[n + w].at[4 * chip[0] + 2 * chip[1] + c]
                copies.append(_remote(blk, blk, ss[3 * w + j], rs[3 * w + j], (x, y, 1 - c)))
        return copies
    return plan, 3 * n


def _plan_sibling(n):
    def plan(b, ss, rs):
        x, y, c = _position()
        return [_remote(b[w].at[2 * s + (1 - c)], b[n + w].at[s], ss[4 * w + s], rs[4 * w + s], (x, y, 1 - c))
                for w in range(n) for s in range(N_CHIP)]
    return plan, 4 * n


def _plan_scatter(n):
    def plan(b, ss, rs):
        x, y, c = _position()
        return [_remote(b[w].at[2 * chip[0] + chip[1]], b[n + w].at[2 * x + y], ss[3 * w + j], rs[3 * w + j],
                        (*chip, c))
                for w in range(n) for j, chip in enumerate(_other_chips())]
    return plan, 3 * n


class _Gather:
    def __init__(self, shards, after, name):
        self.n, self.name = len(shards), name
        x, y, c = _position()
        placed = [lax.dynamic_update_index_in_dim(lax.empty((N_DEV,) + s.shape, s.dtype), s, 4 * x + 2 * y + c, 0)
                  for s in shards]
        bufs, self.ss, self.rs, self.token = _split_call(
            name + "_start", [_in_hbm(a) for a in list(shards) + placed], starts=_plan_gather_first(self.n),
            after=after)
        self.shards, self.fulls = bufs[:self.n], bufs[self.n:]
        self.passed = {}

    def _sub(self, ids, sems, per):
        return [sems[per * w + k] for w in ids for k in range(per)]

    def pass_on(self, ids, after, tag):
        m = len(ids)
        first = (_plan_gather_first(m)[0], self._sub(ids, self.ss, 4), self._sub(ids, self.rs, 4),
                 [], [4 * i + k for i in range(m) for k in (1, 2, 3)])
        bufs, ss, rs, token = _split_call(
            "%s_pass_%s" % (self.name, tag), [self.shards[w] for w in ids] + [self.fulls[w] for w in ids],
            waits=[first], starts=_plan_gather_pass(m), after=after)
        for i, w in enumerate(ids):
            self.shards[w], self.fulls[w] = bufs[i], bufs[m + i]
        self.passed[tuple(ids)] = (ss, rs)
        return token

    def finish(self, ids, after, tag):
        m = len(ids)
        ss2, rs2 = self.passed[tuple(ids)]
        first = (_plan_gather_first(m)[0], self._sub(ids, self.ss, 4), self._sub(ids, self.rs, 4),
                 list(range(4 * m)), [4 * i for i in range(m)])
        passed = (_plan_gather_pass(m)[0], ss2, rs2, list(range(3 * m)), list(range(3 * m)))
        bufs, _, _, _ = _split_call(
            "%s_finish_%s" % (self.name, tag), [self.shards[w] for w in ids] + [self.fulls[w] for w in ids],
            waits=[first, passed], after=after)
        return bufs[m:]


class _ReduceScatter:
    def __init__(self, grads, parity, name):
        self.n, self.name, self.parity = len(grads), name, parity
        lands = [lax.empty((N_CHIP,) + g.shape[1:], g.dtype) for g in grads]
        self.bufs, self.ss, self.rs, self.token = _split_call(
            name + "_sibling_start", [_in_hbm(a) for a in list(grads) + lands], starts=_plan_sibling(self.n))

    def pair_sums(self, after):
        n = self.n
        bufs, _, _, _ = _split_call(
            self.name + "_sibling_wait", self.bufs,
            waits=[(_plan_sibling(n)[0], self.ss, self.rs, list(range(4 * n)), list(range(4 * n)))], after=after)
        sums = [_pair_sum(bufs[w], bufs[n + w], self.parity, "%s_pair_sum_%d" % (self.name, w)) for w in range(n)]
        lands = [lax.empty(s.shape, s.dtype) for s in sums]
        self.bufs, self.ss, self.rs, token = _split_call(
            self.name + "_scatter_start", [_in_hbm(a) for a in sums + lands], starts=_plan_scatter(n))
        return token

    def finish(self, after):
        n = self.n
        bufs, _, _, _ = _split_call(
            self.name + "_scatter_wait", self.bufs,
            waits=[(_plan_scatter(n)[0], self.ss, self.rs, list(range(3 * n)), list(range(3 * n)))], after=after)
        return bufs[:n], bufs[n:]


class _Exchanges:
    def __init__(self, parity, order):
        self.parity, self.order = parity, order


def _gather_small(packed, name):
    R = packed.shape[0]

    def body(x_ref, out_ref, send_sems, recv_sems):
        x, y, c = _position()
        me = 4 * x + 2 * y + c
        out_ref[me] = x_ref[...]
        copies = []
        for k in range(1, N_DEV):
            to = (x ^ ((k >> 2) & 1), y ^ ((k >> 1) & 1), c ^ (k & 1))
            cp = pltpu.make_async_remote_copy(
                src_ref=x_ref, dst_ref=out_ref.at[me],
                send_sem=send_sems.at[k], recv_sem=recv_sems.at[k], device_id=to, device_id_type=MESH)
            cp.start()
            copies.append((k, to, cp))
        for k, to, cp in copies:
            cp.wait_send()
            pltpu.make_async_remote_copy(
                src_ref=x_ref, dst_ref=out_ref.at[4 * to[0] + 2 * to[1] + to[2]],
                send_sem=send_sems.at[k], recv_sem=recv_sems.at[k], device_id=to, device_id_type=MESH).wait_recv()

    return pl.pallas_call(
        body, name=name,
        in_specs=[pl.BlockSpec(memory_space=pltpu.VMEM)], out_specs=pl.BlockSpec(memory_space=pltpu.VMEM),
        out_shape=jax.ShapeDtypeStruct((N_DEV, R, 128), F32),
        scratch_shapes=[pltpu.SemaphoreType.DMA((N_DEV,)), pltpu.SemaphoreType.DMA((N_DEV,))],
    )(packed)


def _adamw_math(w, g, m, v):
    m = ADAM_B1 * m + (1.0 - ADAM_B1) * g
    v = ADAM_B2 * v + (1.0 - ADAM_B2) * (g * g)
    m_hat = m / (1.0 - ADAM_B1 ** ADAM_STEP)
    v_hat = v / (1.0 - ADAM_B2 ** ADAM_STEP)
    delta = -ADAM_LR * (m_hat / (jnp.sqrt(v_hat) + ADAM_EPS) + ADAM_WD * w)
    return delta, m, v


def _adamw_big(w, m, v, parts, name):
    R, C = w.shape
    tr = _pick(R, (256,))

    def body(w_ref, m_ref, v_ref, p_ref, g_ref, d_ref, nm_ref, nv_ref):
        g = p_ref[0].astype(F32)
        for s in range(1, N_CHIP):
            g = g + p_ref[s].astype(F32)
        d, nm, nv = _adamw_math(w_ref[...], g, m_ref[...], v_ref[...])
        g_ref[...] = g
        d_ref[...] = d
        nm_ref[...] = nm
        nv_ref[...] = nv

    blk = pl.BlockSpec((tr, C), lambda i: (i, 0))
    out = jax.ShapeDtypeStruct((R, C), F32)
    return pl.pallas_call(
        body, name=name, grid=(R // tr,),
        in_specs=[blk, blk, blk, pl.BlockSpec((N_CHIP, tr, C), lambda i: (0, i, 0))],
        out_specs=[blk, blk, blk, blk], out_shape=[out, out, out, out],
        compiler_params=_cparams(("parallel",)),
    )(w, m, v, parts)


def _adamw_big_landed(w, m, v, parts, lands, slot, name):
    R, C = w.shape
    tr = _pick(R, (256,))

    def body(slot_ref, w_ref, m_ref, v_ref, own_ref, l1_ref, l2_ref, l3_ref, g_ref, d_ref, nm_ref, nv_ref):
        g = own_ref[...].astype(F32)
        for ref in (l1_ref, l2_ref, l3_ref):
            g = g + ref[...].astype(F32)
        d, nm, nv = _adamw_math(w_ref[...], g, m_ref[...], v_ref[...])
        g_ref[...] = g
        d_ref[...] = d
        nm_ref[...] = nm
        nv_ref[...] = nv

    blk = pl.BlockSpec((tr, C), lambda i, slot: (i, 0))

    def chip(k):
        return pl.BlockSpec((None, tr, C), lambda i, slot: ((slot[0] + k) % N_CHIP, i, 0))

    out = jax.ShapeDtypeStruct((R, C), F32)
    return pl.pallas_call(
        body, name=name,
        grid_spec=pltpu.PrefetchScalarGridSpec(
            num_scalar_prefetch=1, grid=(R // tr,),
            in_specs=[blk, blk, blk, chip(0), chip(1), chip(2), chip(3)],
            out_specs=[blk, blk, blk, blk]),
        out_shape=[out, out, out, out],
        compiler_params=_cparams(("parallel",)),
    )(slot, w, m, v, parts, lands, lands, lands)


def _adamw_small(w, m, v, gathered, name):
    R = w.shape[0]

    def body(w_ref, m_ref, v_ref, p_ref, g_ref, d_ref, nm_ref, nv_ref):
        g = p_ref[0]
        for s in range(1, N_DEV):
            g = g + p_ref[s]
        d, nm, nv = _adamw_math(w_ref[...], g, m_ref[...], v_ref[...])
        g_ref[...] = g
        d_ref[...] = d
        nm_ref[...] = nm
        nv_ref[...] = nv

    out = jax.ShapeDtypeStruct((R, 128), F32)
    return pl.pallas_call(
        body, name=name, out_shape=[out, out, out, out],
    )(w, m, v, gathered)


SMALL_NAMES = ("lb_logits", "hg_norm_w", "rel_bias", "norm_mix_w", "norm_mlp_w", "norm_final_w")
SMALL_SHAPES = {"lb_logits": (2, HG_WIDTH), "hg_norm_w": (1, HG_DK), "rel_bias": (AT_HEADS, N_REL_PAD),
                "norm_mix_w": (1, D_MODEL), "norm_mlp_w": (1, D_MODEL), "norm_final_w": (1, D_MODEL)}


def _pack_small(parts):
    rows = []
    for nme in SMALL_NAMES:
        p = parts[nme]
        if nme == "rel_bias":
            p = jnp.pad(p, ((0, 0), (0, N_REL_PAD - N_REL)))
        rows.append(p.reshape(-1, 128))
    flat = jnp.concatenate(rows, axis=0)
    return jnp.pad(flat, ((0, SMALL_ROWS - flat.shape[0]), (0, 0)))


def _unpack_small(packed):
    out, at = {}, 0
    for nme in SMALL_NAMES:
        shp = SMALL_SHAPES[nme]
        nrow = shp[0] * shp[1] // 128
        p = packed[at:at + nrow].reshape(shp)
        at += nrow
        out[nme] = p[:, :N_REL] if nme == "rel_bias" else p
    return out


BIG_NAMES = ("w_in", "w_branch_a", "w_branch_b", "w_out", "w_up", "w_down")


def kernel(x, w_in, lb_logits, hg_norm_w, rel_bias, w_branch_a, w_branch_b, w_out, norm_mix_w, norm_mlp_w, w_up, w_down, norm_final_w, loss_target, m_w_in, m_lb_logits, m_hg_norm_w, m_rel_bias, m_w_branch_a, m_w_branch_b, m_w_out, m_norm_mix_w, m_norm_mlp_w, m_w_up, m_w_down, m_norm_final_w, v_w_in, v_lb_logits, v_hg_norm_w, v_rel_bias, v_w_branch_a, v_w_branch_b, v_w_out, v_norm_mix_w, v_norm_mlp_w, v_w_up, v_w_down, v_norm_final_w):
    big_w = [w_in[0], w_branch_a[0], w_branch_b[0], w_out[0], w_up[0], w_down[0]]
    big_m = [m_w_in[0], m_w_branch_a[0], m_w_branch_b[0], m_w_out[0], m_w_up[0], m_w_down[0]]
    big_v = [v_w_in[0], v_w_branch_a[0], v_w_branch_b[0], v_w_out[0], v_w_up[0], v_w_down[0]]

    shards = [w.astype(BF16) for w in big_w]
    parity = lax.axis_index("c").astype(jnp.int32).reshape(1)
    loss_part, grad_x, chip_parts, small = _local_step(
        x[0], loss_target[0], lb_logits, hg_norm_w, rel_bias[0], norm_mix_w, norm_mlp_w,
        norm_final_w.reshape(1, D_MODEL), shards[0], shards[1:], _Exchanges(parity, _gather_order()))
    loss = lax.psum(loss_part[0, 0], ("x", "y", "c"))
    rs_in, rs_mix, rs_up, rs_down = chip_parts
    slot = (2 * lax.axis_index("x") + lax.axis_index("y")).astype(jnp.int32).reshape(1)
    big = {}

    def finish(rs, names, after):
        sums, lands = rs.finish(after)
        for nme, own, land in zip(names, sums, lands):
            i = BIG_NAMES.index(nme)
            big[nme] = _adamw_big_landed(big_w[i], big_m[i], big_v[i], own, land, slot, "adamw_" + nme)
        return [big[nme][1] for nme in names]

    done = finish(rs_down, ["w_down"], [grad_x])
    done = finish(rs_up, ["w_up"], done)
    done = finish(rs_mix, ["w_branch_a", "w_branch_b", "w_out"], done)

    sw = dict(lb_logits=lb_logits, hg_norm_w=hg_norm_w, rel_bias=rel_bias[0], norm_mix_w=norm_mix_w,
              norm_mlp_w=norm_mlp_w, norm_final_w=norm_final_w.reshape(1, D_MODEL))
    sm = dict(lb_logits=m_lb_logits, hg_norm_w=m_hg_norm_w, rel_bias=m_rel_bias[0], norm_mix_w=m_norm_mix_w,
              norm_mlp_w=m_norm_mlp_w, norm_final_w=m_norm_final_w.reshape(1, D_MODEL))
    sv = dict(lb_logits=v_lb_logits, hg_norm_w=v_hg_norm_w, rel_bias=v_rel_bias[0], norm_mix_w=v_norm_mix_w,
              norm_mlp_w=v_norm_mlp_w, norm_final_w=v_norm_final_w.reshape(1, D_MODEL))
    gathered = _gather_small(_pack_small(small), "gather_small")
    small_packed = _adamw_small(_pack_small(sw), _pack_small(sm), _pack_small(sv), gathered, "adamw_small")
    small_out = [_unpack_small(p) for p in small_packed]

    finish(rs_in, ["w_in"], done + [small_packed[0]])

    def leaf(kind, nme):
        if nme in BIG_NAMES:
            return big[nme][kind][None]
        p = small_out[kind][nme]
        if nme == "rel_bias":
            return p[None]
        if nme == "norm_final_w":
            return p.reshape(D_MODEL)
        return p

    order = ("w_in", "lb_logits", "hg_norm_w", "rel_bias", "w_branch_a", "w_branch_b", "w_out", "norm_mix_w",
             "norm_mlp_w", "w_up", "w_down", "norm_final_w")
    outs = [loss, grad_x[None]]
    for kind in range(4):
        outs += [leaf(kind, nme) for nme in order]
    return tuple(outs)
```

```python
import jax
import jax.numpy as jnp
from jax import lax
from jax.experimental import pallas as pl
from jax.experimental.pallas import tpu as pltpu

F32 = jnp.float32
BF16 = jnp.bfloat16
HIGHEST = lax.Precision.HIGHEST
MESH = pl.DeviceIdType.MESH

D_MODEL = 2048
HG_HEADS = 8
HG_DK = 128
HG_WIDTH = 1024
AT_HEADS = 16
AT_DH = 64
AT_WIDTH = 1024
CHUNK = 64
LEFT_CHUNKS = 8
BAND = (LEFT_CHUNKS + 1) * CHUNK
PAD = LEFT_CHUNKS * CHUNK
REL_CLIP = 256
N_REL = 2 * REL_CLIP + 1
N_REL_PAD = 640
D_FF = 4 * D_MODEL
EPS = 1e-6
N_DEV = 8
N_CHIP = 4

ADAM_LR = 0.001
ADAM_B1 = 0.9
ADAM_B2 = 0.999
ADAM_EPS = 1e-08
ADAM_WD = 0.01
ADAM_STEP = 10

COL_HQ, COL_HF, COL_HI, COL_HG = 0, 8, 16, 24
COL_AQ, COL_AK, COL_AV = 32, 40, 48
COL_GATE_A, COL_GATE_B = 7, 9

VMEM_LIMIT = 56 * 1024 * 1024
SMALL_ROWS = 152


def _cparams(sem=None, **kw):
    if sem is not None:
        kw["dimension_semantics"] = sem
    return pltpu.CompilerParams(vmem_limit_bytes=VMEM_LIMIT, **kw)


def _pick(n, cands):
    for c in cands:
        if n % c == 0:
            return c
    return n


def _sigmoid(x):
    return 1.0 / (1.0 + jnp.exp(-x))


ANY = pl.BlockSpec(memory_space=pl.ANY)


def _position():
    return lax.axis_index("x"), lax.axis_index("y"), lax.axis_index("c")


def _call(body, args, *, name, grid, in_specs, out_specs, out_shape, scratch_shapes=(), sem=None, after=()):
    n_in = len(args)

    def ordered(*refs):
        body(*refs[:n_in], *refs[n_in + len(after):])

    return list(pl.pallas_call(
        ordered if after else body, name=name, grid=grid, in_specs=list(in_specs) + [ANY] * len(after),
        out_specs=out_specs, out_shape=out_shape, scratch_shapes=list(scratch_shapes),
        compiler_params=_cparams(sem))(*args, *after))


MAX_CONTRACTION_TILE = 4096


def _accumulate(part, acc_ref, step, n_steps, finish):
    if n_steps == 1:
        finish(part)
        return

    @pl.when(step == 0)
    def _():
        acc_ref[...] = part

    @pl.when(step > 0)
    def _():
        acc_ref[...] += part

    @pl.when(step == n_steps - 1)
    def _():
        finish(acc_ref[...])


def _mm_nn(a, wb, out_dtype, name, after=(), epilogue=None):
    M, K = a.shape
    NB, K2, Nb = wb.shape
    assert K == K2
    tm = min(M, 1024)
    tk = min(K, MAX_CONTRACTION_TILE)
    tn = _pick(Nb, (512, 1408, 256))
    nk = K // tk
    nn = Nb // tn
    extra, first_cols, out_dtypes, fn = epilogue or ((), (), (out_dtype,), lambda total: (total,))
    n_extra, n_out = len(extra), len(out_dtypes)

    def body(a_ref, b_ref, *rest):
        def finish(total):
            results = fn(total, *[r[...] for r in rest[:n_extra]])
            for o_ref, res, dt in zip(rest[n_extra:n_extra + n_out], results, out_dtypes):
                o_ref[...] = res.astype(dt)

        part = jnp.dot(a_ref[...], b_ref[...], preferred_element_type=F32)
        _accumulate(part, rest[-1], pl.program_id(3), nk, finish)

    def tile(first):
        return pl.BlockSpec((tm, tn), lambda m, j, n, k: (m, first + j * nn + n))

    outs = _call(
        body, (a, wb) + tuple(extra), name=name, grid=(M // tm, NB, nn, nk),
        in_specs=[pl.BlockSpec((tm, tk), lambda m, j, n, k: (m, k)),
                  pl.BlockSpec((None, tk, tn), lambda m, j, n, k: (j, k, n))] + [tile(col // tn) for col in first_cols],
        out_specs=[tile(0)] * n_out,
        out_shape=[jax.ShapeDtypeStruct((M, NB * Nb), dt) for dt in out_dtypes],
        scratch_shapes=[] if nk == 1 else [pltpu.VMEM((tm, tn), F32)],
        sem=("parallel", "parallel", "parallel", "arbitrary"), after=after)
    return outs if epilogue else outs[0]


def _squared_relu(a):
    ra = jnp.maximum(a, 0.0)
    return a, ra * ra


def _gated_merge(pb, za, zb, pa):
    return pb, _sigmoid(za) * pa + _sigmoid(zb) * pb


def _mm_nt(a, wb, out_dtype, name, after=(), epilogue=None):
    M, N = a.shape
    NB, K, Nb = wb.shape
    assert N == NB * Nb
    tm = min(M, 1024)
    n_tiles_live = 1 + (len(epilogue[0]) + len(epilogue[2]) if epilogue else 0)
    tko = _pick(K, (1024,)) if n_tiles_live <= 3 else _pick(K, (512,))
    tc = _pick(Nb, (2048, 1024, 1408, 256))
    nc = Nb // tc
    jb = max([d for d in (8, 4, 2, 1) if NB % d == 0 and d * tc <= MAX_CONTRACTION_TILE]) if nc == 1 else 1
    nsteps = (NB // jb) * nc
    extra, first_cols, out_dtypes, fn = epilogue or ((), (), (out_dtype,), lambda total: (total,))
    n_extra, n_out = len(extra), len(out_dtypes)

    def body(a_ref, b_ref, *rest):
        def finish(total):
            results = fn(total, *[r[...] for r in rest[:n_extra]])
            for o_ref, res, dt in zip(rest[n_extra:n_extra + n_out], results, out_dtypes):
                o_ref[...] = res.astype(dt)

        part = sum(lax.dot_general(a_ref[:, i * tc:(i + 1) * tc], b_ref[i], (((1,), (1,)), ((), ())),
                                   preferred_element_type=F32) for i in range(jb))
        _accumulate(part, rest[-1], pl.program_id(2) * nc + pl.program_id(3), nsteps, finish)

    def tile(first):
        return pl.BlockSpec((tm, tko), lambda m, ko, j, c: (m, first + ko))

    outs = _call(
        body, (a, wb) + tuple(extra), name=name,
        grid=(M // tm, K // tko, NB // jb, nc),
        in_specs=[pl.BlockSpec((tm, jb * tc), lambda m, ko, j, c: (m, j * nc + c)),
                  pl.BlockSpec((jb, tko, tc), lambda m, ko, j, c: (j, ko, c))] + [tile(col // tko) for col in first_cols],
        out_specs=[tile(0)] * n_out,
        out_shape=[jax.ShapeDtypeStruct((M, K), dt) for dt in out_dtypes],
        scratch_shapes=[] if nsteps == 1 else [pltpu.VMEM((tm, tko), F32)],
        sem=("parallel", "parallel", "arbitrary", "arbitrary"), after=after)
    return outs if epilogue else outs[0]


ROWS_TILE = 512
ROWS_PIECE = 128


def _mm_rows(a, w, extras, vectors, row_dtypes, fn, name):
    M, K = a.shape
    N = w.shape[1]
    tm = min(M, ROWS_TILE)
    n_e, n_v = len(extras), len(vectors)

    def body(a_ref, w_ref, *rest):
        tiles, vecs, outs, product_ref = rest[:n_e], rest[n_e:n_e + n_v], rest[n_e + n_v:-1], rest[-1]
        product_ref[...] = jnp.dot(a_ref[...], w_ref[...], preferred_element_type=F32)
        for i in range(tm // ROWS_PIECE):
            piece = slice(i * ROWS_PIECE, (i + 1) * ROWS_PIECE)
            results = fn(product_ref[piece, :], *[t[piece, :] for t in tiles], *[v[...] for v in vecs])
            for o_ref, res, dt in zip(outs, results, row_dtypes):
                o_ref[piece, :] = res.astype(dt)

    row = pl.BlockSpec((tm, N), lambda m: (m, 0))
    return _call(
        body, (a, w) + tuple(extras) + tuple(vectors), name=name, grid=(M // tm,),
        in_specs=[pl.BlockSpec((tm, K), lambda m: (m, 0)), pl.BlockSpec((K, N), lambda m: (0, 0))]
        + [row] * n_e + [pl.BlockSpec((1, N), lambda m: (0, 0))] * n_v,
        out_specs=[row] * len(row_dtypes),
        out_shape=[jax.ShapeDtypeStruct((M, N), dt) for dt in row_dtypes],
        scratch_shapes=[pltpu.VMEM((tm, N), F32)], sem=("parallel",))


def _rms(h, w):
    return h * lax.rsqrt(jnp.mean(h * h, axis=-1, keepdims=True) + EPS) * w


def _residual_rms_rows(mix, x, w):
    h = x + mix
    return h, _rms(h, w)


def _mm_tn(a, g, nb, out_dtype, name, after=()):
    M, Ka = a.shape
    M2, N = g.shape
    assert M == M2 and N % nb == 0
    Nb = N // nb
    tka = _pick(Ka, (1024,))
    tn = _pick(Nb, (512, 1408, 256))
    nn = Nb // tn

    def body(a_ref, g_ref, o_ref):
        o_ref[...] = lax.dot_general(a_ref[...], g_ref[...], (((0,), (0,)), ((), ())),
                                     preferred_element_type=F32).astype(out_dtype)

    return _call(
        body, (a, g), name=name,
        grid=(Ka // tka, nb, nn),
        in_specs=[pl.BlockSpec((M, tka), lambda ka, j, n: (0, ka)),
                  pl.BlockSpec((M, tn), lambda ka, j, n: (0, j * nn + n))],
        out_specs=[pl.BlockSpec((None, tka, tn), lambda ka, j, n: (j, ka, n))],
        out_shape=[jax.ShapeDtypeStruct((nb, Ka, Nb), out_dtype)],
        sem=("parallel", "parallel", "parallel"), after=after)[0]


ROW_TILE = 256


def _rms_fwd(x, w, name):
    T, Dm = x.shape

    def body(x_ref, w_ref, u_ref):
        xv = x_ref[...]
        r = lax.rsqrt(jnp.mean(xv * xv, axis=-1, keepdims=True) + EPS)
        u_ref[...] = (xv * r * w_ref[...]).astype(BF16)

    return pl.pallas_call(
        body, name=name, grid=(T // ROW_TILE,),
        in_specs=[pl.BlockSpec((ROW_TILE, Dm), lambda i: (i, 0)), pl.BlockSpec((1, Dm), lambda i: (0, 0))],
        out_specs=pl.BlockSpec((ROW_TILE, Dm), lambda i: (i, 0)),
        out_shape=jax.ShapeDtypeStruct((T, Dm), BF16),
        compiler_params=_cparams(("parallel",)),
    )(x, w)


def _loss_head(h1, mlp, wf, target, name):
    T, Dm = h1.shape

    def body(h_ref, m_ref, w_ref, t_ref, loss_ref, dh_ref, dhb_ref, dw_ref):
        i = pl.program_id(0)
        h = h_ref[...] + m_ref[...]
        r = lax.rsqrt(jnp.mean(h * h, axis=-1, keepdims=True) + EPS)
        xh = h * r
        wv = w_ref[...]
        e = xh * wv - t_ref[...]
        part = 0.5 * jnp.sum(jnp.mean(e * e, axis=-1, keepdims=True), axis=0, keepdims=True)
        dy = e * (1.0 / Dm)
        dw = jnp.sum(dy * xh, axis=0, keepdims=True)
        gy = dy * wv
        dh = r * (gy - xh * jnp.mean(gy * xh, axis=-1, keepdims=True))
        dh_ref[...] = dh
        dhb_ref[...] = dh.astype(BF16)

        @pl.when(i == 0)
        def _():
            loss_ref[...] = jnp.zeros_like(loss_ref)
            dw_ref[...] = jnp.zeros_like(dw_ref)

        loss_ref[...] += jnp.broadcast_to(part, loss_ref.shape)
        dw_ref[...] += dw

    row = pl.BlockSpec((ROW_TILE, Dm), lambda i: (i, 0))
    vec = pl.BlockSpec((1, Dm), lambda i: (0, 0))
    return pl.pallas_call(
        body, name=name, grid=(T // ROW_TILE,),
        in_specs=[row, row, vec, row],
        out_specs=[pl.BlockSpec((8, 128), lambda i: (0, 0)), row, row, vec],
        out_shape=[jax.ShapeDtypeStruct((8, 128), F32), jax.ShapeDtypeStruct((T, Dm), F32),
                   jax.ShapeDtypeStruct((T, Dm), BF16), jax.ShapeDtypeStruct((1, Dm), F32)],
        compiler_params=_cparams(("arbitrary",)),
    )(h1, mlp, wf, target)


def _rms_bwd(dyn, x, w, dres, name, after=()):
    T, Dm = x.shape

    def body(g_ref, x_ref, w_ref, r_ref, dx_ref, dxb_ref, dw_ref):
        i = pl.program_id(0)
        xv = x_ref[...]
        r = lax.rsqrt(jnp.mean(xv * xv, axis=-1, keepdims=True) + EPS)
        xh = xv * r
        g = g_ref[...]
        dw = jnp.sum(g * xh, axis=0, keepdims=True)
        gy = g * w_ref[...]
        dx = r_ref[...] + r * (gy - xh * jnp.mean(gy * xh, axis=-1, keepdims=True))
        dx_ref[...] = dx
        dxb_ref[...] = dx.astype(BF16)

        @pl.when(i == 0)
        def _():
            dw_ref[...] = jnp.zeros_like(dw_ref)

        dw_ref[...] += dw

    row = pl.BlockSpec((ROW_TILE, Dm), lambda i: (i, 0))
    vec = pl.BlockSpec((1, Dm), lambda i: (0, 0))
    return _call(
        body, (dyn, x, w, dres), name=name, grid=(T // ROW_TILE,),
        in_specs=[row, row, vec, row],
        out_specs=[row, row, vec],
        out_shape=[jax.ShapeDtypeStruct((T, Dm), F32), jax.ShapeDtypeStruct((T, Dm), BF16),
                   jax.ShapeDtypeStruct((1, Dm), F32)],
        sem=("arbitrary",), after=after)


GATE_TILE = 1024


def _merge_grads(d, za, zb, pa, pb):
    ga = _sigmoid(za)
    gb = _sigmoid(zb)
    return d * ga, d * gb, d * pa * ga * (1.0 - ga), d * pb * gb * (1.0 - gb)


def _dot_hi(a, b, dims):
    return lax.dot_general(a, b, (dims, ((), ())), precision=HIGHEST, preferred_element_type=F32)


NN = ((1,), (0,))
NT = ((1,), (1,))
TN = ((0,), (0,))


def _hg_gates(hq, hf, lb):
    sq = _sigmoid(hq)
    q = hq * sq * (HG_DK ** -0.5)
    f = _sigmoid(hf)
    g = lb + (1.0 - lb) * f
    return q, sq, f, g, jnp.log(g), 1.0 - g


def _tri(lower):
    r = lax.broadcasted_iota(jnp.int32, (CHUNK, CHUNK), 0)
    c = lax.broadcasted_iota(jnp.int32, (CHUNK, CHUNK), 1)
    return jnp.where((r >= c) if lower else (r <= c), 1.0, 0.0).astype(F32)


GROUP = 16
N_GROUPS = CHUNK // GROUP


def _dot_bf16(a, b, dims):
    return lax.dot_general(a.astype(BF16), b.astype(BF16), (dims, ((), ())), preferred_element_type=F32)


def _rows_iota():
    return lax.broadcasted_iota(jnp.int32, (CHUNK, HG_DK), 0)


def _by_query_group(q, kk, b, g):
    r0 = GROUP * g
    b0 = b[r0:r0 + 1]
    decay = jnp.exp(b[r0:r0 + GROUP] - b0)
    ks = jnp.where(_rows_iota() < r0, kk * jnp.exp(jnp.minimum(b0 - b, 0.0)), 0.0)
    return q[r0:r0 + GROUP] * decay, ks, decay


def _by_key_group(q, kk, b, j):
    r1 = GROUP * (j + 1)
    b1 = b[r1 - 1:r1]
    decay = jnp.exp(b1 - b[r1 - GROUP:r1])
    qs = jnp.where(_rows_iota() >= r1, q * jnp.exp(jnp.minimum(b - b1, 0.0)), 0.0)
    return qs, kk[r1 - GROUP:r1] * decay, decay


def _scores_between_groups(q, kk, b):
    blocks = [jnp.zeros((GROUP, CHUNK), F32)]
    for g in range(1, N_GROUPS):
        qs, ks, _ = _by_query_group(q, kk, b, g)
        blocks.append(_dot_bf16(qs, ks, NT))
    return jnp.concatenate(blocks, axis=0)


def _hgrn2_fwd(z, lb_logits, hg_norm_w, name, after=()):
    T = z.shape[0]
    n_chunks = T // CHUNK

    def body(hq_ref, hf_ref, hi_ref, hg_ref, lbl_ref, nw_ref, o_ref, ya_ref, sall_ref, st_ref):
        lbl = lbl_ref[...]
        lb = 1.0 / (1.0 + jnp.exp(lbl[1:2, :] - lbl[0:1, :]))
        st_ref[...] = jnp.zeros_like(st_ref)
        tri = _tri(True)
        row8 = lax.broadcasted_iota(jnp.int32, (8, HG_DK), 0)

        def chunk(c, carry):
            rows = pl.ds(pl.multiple_of(c * CHUNK, CHUNK), CHUNK)
            q, _, _, _, lg, kk = _hg_gates(hq_ref[rows, :], hf_ref[rows, :], lb)
            v = hi_ref[rows, :]
            b = _dot_hi(tri, lg, NN)
            st = st_ref[...]
            sall_ref[c] = st
            for grp in range(N_GROUPS):
                r0 = GROUP * grp
                for h8 in range(GROUP // 8):
                    n = 8 * (h8 + 1)
                    bs, ks, vs = b[r0:r0 + n], kk[r0:r0 + n], v[r0:r0 + n]
                    sidx = lax.broadcasted_iota(jnp.int32, (n, HG_DK), 0)
                    blk = jnp.zeros((8, HG_DK), F32)
                    for i in range(8):
                        t = r0 + 8 * h8 + i
                        e = jnp.where(sidx <= 8 * h8 + i, jnp.exp(b[t:t + 1] - bs), 0.0)
                        p = jnp.sum(e * ks * q[t:t + 1], axis=1, keepdims=True)
                        ot = jnp.sum(p * vs, axis=0, keepdims=True)
                        blk = blk + jnp.where(row8 == i, ot, 0.0)
                    o_ref[pl.ds(pl.multiple_of(c * CHUNK + r0 + 8 * h8, 8), 8), :] = blk
            o_ref[rows, :] += _dot_hi(q * jnp.exp(b), st, NT) + _dot_bf16(_scores_between_groups(q, kk, b), v, NN)
            bl = b[CHUNK - 1:CHUNK]
            ke = kk * jnp.exp(bl - b)
            st_ref[...] = st * jnp.exp(bl) + _dot_hi(v, ke, TN)
            return carry

        lax.fori_loop(0, n_chunks, chunk, 0, unroll=2)
        o = o_ref[...]
        r = lax.rsqrt(jnp.mean(o * o, axis=-1, keepdims=True) + EPS)
        hg = hg_ref[...]
        ya_ref[...] = (o * r * nw_ref[...] * (hg * _sigmoid(hg))).astype(BF16)

    def col(base):
        return pl.BlockSpec((T, HG_DK), lambda h: (0, base + h))

    return _call(
        body, (z, z, z, z, lb_logits, hg_norm_w), name=name, grid=(HG_HEADS,),
        in_specs=[col(COL_HQ), col(COL_HF), col(COL_HI), col(COL_HG),
                  pl.BlockSpec((2, HG_DK), lambda h: (0, h)), pl.BlockSpec((1, HG_DK), lambda h: (0, 0))],
        out_specs=[col(0), col(0), pl.BlockSpec((None, n_chunks, HG_DK, HG_DK), lambda h: (h, 0, 0, 0))],
        out_shape=[jax.ShapeDtypeStruct((T, HG_WIDTH), F32), jax.ShapeDtypeStruct((T, HG_WIDTH), BF16),
                   jax.ShapeDtypeStruct((HG_HEADS, n_chunks, HG_DK, HG_DK), F32)],
        scratch_shapes=[pltpu.VMEM((HG_DK, HG_DK), F32)],
        sem=("parallel",), after=after)


def _hgrn2_bwd(z, lb_logits, hg_norm_w, o_raw, s_all, dya, name, after=()):
    T = z.shape[0]
    n_chunks = T // CHUNK

    def body(hq_ref, hf_ref, hi_ref, hg_ref, lbl_ref, nw_ref, o_ref, sall_ref, dya_ref,
             dhq_ref, dhf_ref, dhi_ref, dhg_ref, dlbl_ref, dnw_ref,
             do_ref, dst_ref, dlb_ref, *per_chunk):
        h = pl.program_id(0)
        lbl = lbl_ref[...]
        lb = 1.0 / (1.0 + jnp.exp(lbl[1:2, :] - lbl[0:1, :]))

        o = o_ref[...]
        r = lax.rsqrt(jnp.mean(o * o, axis=-1, keepdims=True) + EPS)
        oh = o * r
        nw = nw_ref[...]
        hg = hg_ref[...]
        sg = _sigmoid(hg)
        dy = dya_ref[...]
        d_on = dy * (hg * sg)
        dhg_ref[...] = (dy * (oh * nw) * (sg * (1.0 + hg * (1.0 - sg)))).astype(BF16)
        dnw = jnp.sum(d_on * oh, axis=0, keepdims=True)
        gy = d_on * nw
        do_ref[...] = r * (gy - oh * jnp.mean(gy * oh, axis=-1, keepdims=True))

        @pl.when(h == 0)
        def _():
            dnw_ref[...] = jnp.zeros_like(dnw_ref)

        dnw_ref[...] += jnp.broadcast_to(dnw, dnw_ref.shape)

        dst_ref[...] = jnp.zeros_like(dst_ref)
        dlb_ref[...] = jnp.zeros_like(dlb_ref)
        tri = _tri(True)
        tri_t = _tri(False)
        row8 = lax.broadcasted_iota(jnp.int32, (8, HG_DK), 0)
        row_group = lax.broadcasted_iota(jnp.int32, (CHUNK, CHUNK), 0) // GROUP
        col_group = lax.broadcasted_iota(jnp.int32, (CHUNK, CHUNK), 1) // GROUP
        earlier_group = col_group < row_group
        later_group = col_group > row_group

        def chunk(c, dq_ref, dk_ref, dv_ref):
            rows = pl.ds(pl.multiple_of(c * CHUNK, CHUNK), CHUNK)
            hq = hq_ref[rows, :]
            q, sq, f, g, lg, kk = _hg_gates(hq, hf_ref[rows, :], lb)
            v = hi_ref[rows, :]
            do = do_ref[rows, :]
            b = _dot_hi(tri, lg, NN)
            eb = jnp.exp(b)
            bl = b[CHUNK - 1:CHUNK]
            ebl = jnp.exp(bl)
            ekb = jnp.exp(bl - b)
            qe = q * eb
            ke = kk * ekb
            st = sall_ref[c]
            dst = dst_ref[...]
            dqe = _dot_hi(do, st, NN)
            dke = _dot_hi(v, dst, NN)
            dv_inter = _dot_hi(ke, dst, NT)
            d_ebl = jnp.sum(st * dst, axis=0, keepdims=True)
            dst_ref[...] = dst * ebl + _dot_hi(do, qe, TN)

            dk_ref[...] = jnp.zeros_like(dk_ref)
            dv_ref[...] = jnp.zeros_like(dv_ref)
            for grp in range(N_GROUPS):
                r0 = GROUP * grp
                for h8 in range(GROUP // 8):
                    n = 8 * (h8 + 1)
                    bs, ks, vs = b[r0:r0 + n], kk[r0:r0 + n], v[r0:r0 + n]
                    sidx = lax.broadcasted_iota(jnp.int32, (n, HG_DK), 0)
                    blk = jnp.zeros((8, HG_DK), F32)
                    for i in range(8):
                        t = r0 + 8 * h8 + i
                        qt = q[t:t + 1]
                        dot_ = do[t:t + 1]
                        e = jnp.where(sidx <= 8 * h8 + i, jnp.exp(b[t:t + 1] - bs), 0.0)
                        w = e * ks
                        p = jnp.sum(w * qt, axis=1, keepdims=True)
                        dsc = jnp.sum(vs * dot_, axis=1, keepdims=True)
                        dqt = jnp.sum(dsc * w, axis=0, keepdims=True)
                        blk = blk + jnp.where(row8 == i, dqt, 0.0)
                        dk_ref[r0:r0 + n, :] += dsc * e * qt
                        dv_ref[r0:r0 + n, :] += p * dot_
                    dq_ref[r0 + 8 * h8:r0 + n, :] = blk
            ds_far = jnp.where(earlier_group, _dot_bf16(do, v, NT), 0.0)
            ds_far_t = jnp.where(later_group, _dot_bf16(v, do, NT), 0.0)
            dq_far, dk_far = [jnp.zeros((GROUP, HG_DK), F32)], []
            for grp in range(1, N_GROUPS):
                r0 = GROUP * grp
                _, ks, decay = _by_query_group(q, kk, b, grp)
                dq_far.append(decay * _dot_hi(ds_far[r0:r0 + GROUP], ks, NN))
                qs, _, decay = _by_key_group(q, kk, b, grp - 1)
                dk_far.append(decay * _dot_hi(ds_far_t[r0 - GROUP:r0], qs, NN))
            dk_far.append(jnp.zeros((GROUP, HG_DK), F32))
            dv_far = _dot_bf16(_scores_between_groups(q, kk, b), do, TN)
            dq_i = dq_ref[...] + jnp.concatenate(dq_far, axis=0)
            dk_i = dk_ref[...] + jnp.concatenate(dk_far, axis=0)
            dke_ke = dke * ke
            db = q * dq_i - kk * dk_i + dqe * qe - dke_ke
            db_last = jnp.sum(dke_ke, axis=0, keepdims=True) + d_ebl * ebl
            dlg = _dot_hi(tri_t, db, NN) + db_last
            dq = dq_i + dqe * eb
            dkk = dk_i + dke * ekb
            dg = dlg / g - dkk
            dhq_ref[rows, :] = (dq * (HG_DK ** -0.5) * (sq * (1.0 + hq * (1.0 - sq)))).astype(BF16)
            dhf_ref[rows, :] = (dg * (1.0 - lb) * f * (1.0 - f)).astype(BF16)
            dhi_ref[rows, :] = (dv_ref[...] + dv_far + dv_inter).astype(BF16)
            dlb_ref[...] += jnp.sum(dg * (1.0 - f), axis=0, keepdims=True)

        def two_chunks(i, carry):
            chunk(n_chunks - 1 - 2 * i, *per_chunk[:3])
            chunk(n_chunks - 2 - 2 * i, *per_chunk[3:])
            return carry

        lax.fori_loop(0, n_chunks // 2, two_chunks, 0)
        dl0 = dlb_ref[...] * lb * (1.0 - lb)
        dlbl_ref[0:1, :] = dl0
        dlbl_ref[1:2, :] = -dl0

    def col(base):
        return pl.BlockSpec((T, HG_DK), lambda h: (0, base + h))

    outb = jax.ShapeDtypeStruct((T, HG_WIDTH), BF16)
    return _call(
        body, (z, z, z, z, lb_logits, hg_norm_w, o_raw, s_all, dya), name=name, grid=(HG_HEADS,),
        in_specs=[col(COL_HQ), col(COL_HF), col(COL_HI), col(COL_HG),
                  pl.BlockSpec((2, HG_DK), lambda h: (0, h)), pl.BlockSpec((1, HG_DK), lambda h: (0, 0)),
                  col(0), pl.BlockSpec((None, n_chunks, HG_DK, HG_DK), lambda h: (h, 0, 0, 0)), col(0)],
        out_specs=[col(0), col(0), col(0), col(0), pl.BlockSpec((2, HG_DK), lambda h: (0, h)),
                   pl.BlockSpec((8, HG_DK), lambda h: (0, 0))],
        out_shape=[outb, outb, outb, outb, jax.ShapeDtypeStruct((2, HG_WIDTH), F32),
                   jax.ShapeDtypeStruct((8, HG_DK), F32)],
        scratch_shapes=[pltpu.VMEM((T, HG_DK), F32), pltpu.VMEM((HG_DK, HG_DK), F32), pltpu.VMEM((1, HG_DK), F32)]
        + [pltpu.VMEM((CHUNK, HG_DK), F32)] * 6,
        sem=("arbitrary",), after=after)


CONST_KEYS = PAD - REL_CLIP
VAR_KEYS = BAND - CONST_KEYS
REL_LO = 128
REL_SPAN = N_REL_PAD - REL_LO


def _rel_onehot(t):
    r = lax.broadcasted_iota(jnp.int32, (REL_SPAN, VAR_KEYS), 0)
    j = lax.broadcasted_iota(jnp.int32, (REL_SPAN, VAR_KEYS), 1)
    idx = jnp.clip(t + PAD - CONST_KEYS - j, -REL_CLIP, REL_CLIP) + REL_CLIP - REL_LO
    return jnp.where(r == idx, 1.0, 0.0).astype(BF16)


def _split3(x):
    hi = x.astype(BF16)
    r1 = x - hi.astype(F32)
    mid = r1.astype(BF16)
    return hi, mid, (r1 - mid.astype(F32)).astype(BF16)


def _bias_expand(rel, name):
    def body(rel_ref, out_ref):
        tab = rel_ref[...]
        onehot = _rel_onehot(pl.program_id(0))
        out_ref[:, 0:CONST_KEYS] = jnp.broadcast_to(tab[:, 2 * REL_CLIP:2 * REL_CLIP + 1], (AT_HEADS, CONST_KEYS))
        out_ref[:, CONST_KEYS:BAND] = sum(
            jnp.dot(piece, onehot, preferred_element_type=F32) for piece in _split3(tab[:, REL_LO:N_REL_PAD]))

    return pl.pallas_call(
        body, name=name, grid=(CHUNK,),
        in_specs=[pl.BlockSpec((AT_HEADS, N_REL_PAD), lambda t: (0, 0))],
        out_specs=pl.BlockSpec((None, AT_HEADS, BAND), lambda t: (t, 0, 0)),
        out_shape=jax.ShapeDtypeStruct((CHUNK, AT_HEADS, BAND), F32),
        compiler_params=_cparams(("parallel",)),
    )(rel)


def _bias_reduce(dbias_rows, name, after=()):
    def body(db_ref, out_ref):
        lane = lax.broadcasted_iota(jnp.int32, (AT_HEADS, N_REL_PAD), 1)
        varying = lane >= CONST_KEYS
        by_offset = jnp.zeros((AT_HEADS, N_REL_PAD), F32)
        constant = jnp.zeros((AT_HEADS, N_REL_PAD), F32)
        for t in range(CHUNK):
            row = db_ref[t]
            constant = constant + jnp.where(varying, 0.0, row)
            moved = jnp.where(varying, row, 0.0)
            by_offset = by_offset + (pltpu.roll(moved, N_REL_PAD - t, axis=1) if t else moved)
        offset = lax.broadcasted_iota(jnp.int32, (N_REL_PAD, N_REL_PAD), 0)
        entry = lax.broadcasted_iota(jnp.int32, (N_REL_PAD, N_REL_PAD), 1)
        onehot = jnp.where(entry == jnp.clip(PAD - offset, -REL_CLIP, REL_CLIP) + REL_CLIP, 1.0, 0.0).astype(BF16)
        acc = sum(jnp.dot(piece, onehot, preferred_element_type=F32) for piece in _split3(by_offset))
        last = jnp.sum(constant, axis=1, keepdims=True)
        out_ref[...] = acc + jnp.where(lane == 2 * REL_CLIP, last, 0.0)

    whole = pl.BlockSpec((CHUNK, AT_HEADS, N_REL_PAD), lambda i: (0, 0, 0))
    return _call(
        body, (dbias_rows,), name=name, grid=(1,), in_specs=[whole],
        out_specs=[pl.BlockSpec((AT_HEADS, N_REL_PAD), lambda i: (0, 0))],
        out_shape=[jax.ShapeDtypeStruct((AT_HEADS, N_REL_PAD), F32)],
        sem=("arbitrary",), after=after)[0]


def _pair_lanes():
    return lax.broadcasted_iota(jnp.int32, (CHUNK, 2 * AT_DH), 1) < AT_DH


def _block_diag(a):
    first = _pair_lanes()
    return jnp.concatenate([jnp.where(first, a, 0.0), jnp.where(first, 0.0, a)], axis=0).astype(BF16)


def _diag_blocks(a):
    return jnp.where(_pair_lanes(), a[:CHUNK], a[CHUNK:])


def _band_probs_t(kb, qbd, bias_t, c):
    s = lax.dot_general(kb, qbd, (NT, ((), ())), preferred_element_type=F32) * (AT_DH ** -0.5) + bias_t
    j = lax.broadcasted_iota(jnp.int32, (BAND, 2 * AT_DH), 0)
    s = jnp.where(j + c * CHUNK >= PAD, s, -jnp.inf)
    p = jnp.exp(s - jnp.max(s, axis=0, keepdims=True))
    return p / jnp.sum(p, axis=0, keepdims=True)


def _attn_fwd(z, bias_t, name, after=()):
    T = z.shape[0]
    n_chunks = T // CHUNK

    def body(q_ref, k_ref, v_ref, bias_ref, y_ref, p_ref, *scratch):
        for pr in range(2):
            lanes = slice(128 * pr, 128 * (pr + 1))
            for dst_ref, src_ref in zip(scratch[2 * pr:2 * pr + 2], (k_ref, v_ref)):
                dst_ref[0:PAD, :] = jnp.zeros((PAD, 128), BF16)
                dst_ref[PAD:PAD + T, :] = src_ref[:, lanes].astype(BF16)

        def chunk(c, carry):
            rows = pl.ds(pl.multiple_of(c * CHUNK, CHUNK), CHUNK)
            band = pl.ds(pl.multiple_of(c * CHUNK, CHUNK), BAND)
            for pr in range(2):
                kp_ref, vp_ref = scratch[2 * pr:2 * pr + 2]
                lanes = slice(128 * pr, 128 * (pr + 1))
                p = _band_probs_t(kp_ref[band, :], _block_diag(q_ref[rows, lanes]), bias_ref[pr], c).astype(BF16)
                p_ref[pr, c] = p
                o2 = lax.dot_general(p, vp_ref[band, :], (TN, ((), ())), preferred_element_type=F32)
                y_ref[rows, lanes] = _diag_blocks(o2).astype(BF16)
            return carry

        lax.fori_loop(0, n_chunks, chunk, 0, unroll=2)

    def col(base):
        return pl.BlockSpec((T, 256), lambda h: (0, base // 2 + h))

    return _call(
        body, (z, z, z, bias_t), name=name, grid=(AT_HEADS // 4,),
        in_specs=[col(COL_AQ), col(COL_AK), col(COL_AV), pl.BlockSpec((2, BAND, 128), lambda h: (h, 0, 0))],
        out_specs=[col(0), pl.BlockSpec((2, n_chunks, BAND, 128), lambda h: (h, 0, 0, 0))],
        out_shape=[jax.ShapeDtypeStruct((T, AT_WIDTH), BF16),
                   jax.ShapeDtypeStruct((AT_HEADS // 2, n_chunks, BAND, 128), BF16)],
        scratch_shapes=[pltpu.VMEM((PAD + T, 128), BF16)] * 4,
        sem=("parallel",), after=after)


def _attn_bwd(z, probs, dyb, name, after=()):
    T = z.shape[0]
    n_chunks = T // CHUNK

    def body(q_ref, k_ref, v_ref, p_ref, dy_ref, dq_ref, dk_ref, dv_ref, dbias_ref, *scratch):
        dbias_ref[...] = jnp.zeros_like(dbias_ref)
        for pr in range(2):
            kp_ref, vp_ref, dkp_ref, dvp_ref = scratch[4 * pr:4 * pr + 4]
            lanes = slice(128 * pr, 128 * (pr + 1))
            kp_ref[0:PAD, :] = jnp.zeros((PAD, 128), BF16)
            vp_ref[0:PAD, :] = jnp.zeros((PAD, 128), BF16)
            kp_ref[PAD:PAD + T, :] = k_ref[:, lanes].astype(BF16)
            vp_ref[PAD:PAD + T, :] = v_ref[:, lanes].astype(BF16)
            dkp_ref[...] = jnp.zeros_like(dkp_ref)
            dvp_ref[...] = jnp.zeros_like(dvp_ref)

        def chunk(c, carry):
            rows = pl.ds(pl.multiple_of(c * CHUNK, CHUNK), CHUNK)
            band = pl.ds(pl.multiple_of(c * CHUNK, CHUNK), BAND)
            for pr in range(2):
                kp_ref, vp_ref, dkp_ref, dvp_ref = scratch[4 * pr:4 * pr + 4]
                lanes = slice(128 * pr, 128 * (pr + 1))
                qbd = _block_diag(q_ref[rows, lanes])
                dobd = _block_diag(dy_ref[rows, lanes])
                pb = p_ref[pr, c]
                p = pb.astype(F32)
                dp = lax.dot_general(vp_ref[band, :], dobd, (NT, ((), ())), preferred_element_type=F32)
                ds = p * (dp - jnp.sum(dp * p, axis=0, keepdims=True))
                dbias_ref[pr] += ds
                dsb = ds.astype(BF16)
                dq2 = lax.dot_general(dsb, kp_ref[band, :], (TN, ((), ())), preferred_element_type=F32)
                dq_ref[rows, lanes] = (_diag_blocks(dq2) * (AT_DH ** -0.5)).astype(BF16)
                dkp_ref[band, :] += jnp.dot(dsb, qbd, preferred_element_type=F32) * (AT_DH ** -0.5)
                dvp_ref[band, :] += jnp.dot(pb, dobd, preferred_element_type=F32)
            return carry

        lax.fori_loop(0, n_chunks, chunk, 0)
        for pr in range(2):
            lanes = slice(128 * pr, 128 * (pr + 1))
            dk_ref[:, lanes] = scratch[4 * pr + 2][PAD:PAD + T, :].astype(BF16)
            dv_ref[:, lanes] = scratch[4 * pr + 3][PAD:PAD + T, :].astype(BF16)

    def col(base):
        return pl.BlockSpec((T, 256), lambda h: (0, base // 2 + h))

    outb = jax.ShapeDtypeStruct((T, AT_WIDTH), BF16)
    return _call(
        body, (z, z, z, probs, dyb), name=name, grid=(AT_HEADS // 4,),
        in_specs=[col(COL_AQ), col(COL_AK), col(COL_AV),
                  pl.BlockSpec((2, n_chunks, BAND, 128), lambda h: (h, 0, 0, 0)), col(0)],
        out_specs=[col(0), col(0), col(0), pl.BlockSpec((2, BAND, 128), lambda h: (h, 0, 0))],
        out_shape=[outb, outb, outb, jax.ShapeDtypeStruct((AT_HEADS // 2, BAND, 128), F32)],
        scratch_shapes=([pltpu.VMEM((PAD + T, 128), BF16)] * 2 + [pltpu.VMEM((PAD + T, 128), F32)] * 2) * 2,
        sem=("parallel",), after=after)


def _local_step(x, target, lb_logits, hg_norm_w, rel_bias, norm_mix_w, norm_mlp_w, norm_final_w,
                w_in, rest, exchanges=None):
    ex = exchanges
    rel = jnp.pad(rel_bias, ((0, 0), (0, N_REL_PAD - N_REL)))

    u = _rms_fwd(x, norm_mix_w, "rms_mix_fwd")
    if ex:
        z, w_in = _mm_gathered(u, w_in, ex.order, "mm_in_fwd")
        gather = _Gather(rest, [w_in], "ag")
        z = _mm_gathered_tail(u, w_in, z, ex.order, "mm_in_fwd_tail", after=[gather.token])
        tok = []
    else:
        z = _mm_nn(u, w_in, F32, "mm_in_fwd")
        w_a, w_b, w_out, w_up, w_down = rest
        tok = []
    o_raw, y_a, s_all = _hgrn2_fwd(z, lb_logits, hg_norm_w, "hgrn2_fwd", after=tok)
    if ex:
        tok = [gather.pass_on([0, 1, 2], [o_raw], "abo")]
    bias_rows = _bias_expand(rel, "bias_expand")
    bias_t = jnp.transpose(bias_rows.reshape(CHUNK, AT_HEADS // 2, 2, BAND), (1, 3, 2, 0)).reshape(
        AT_HEADS // 2, BAND, 2 * CHUNK)
    y_b, probs = _attn_fwd(z, bias_t, "attn_fwd", after=tok)
    if ex:
        tok = [gather.pass_on([3], [y_b], "up")]
        w_a, w_b, w_out = gather.finish([0, 1, 2], tok, "abo")
    pa = _mm_nn(y_a, w_a, F32, "mm_a_fwd")
    pb, merged = _mm_nn(y_b, w_b, None, "mm_b_fwd", epilogue=(
        (z, z, pa), (COL_GATE_A * GATE_TILE, COL_GATE_B * GATE_TILE, 0), (F32, BF16), _gated_merge))
    w_out1 = w_out.reshape(1, D_MODEL, D_MODEL)
    h1, u2 = _mm_rows(merged, w_out.reshape(D_MODEL, D_MODEL), [x], [norm_mlp_w], (F32, BF16),
                      _residual_rms_rows, "mm_out_fwd")
    if ex:
        tok = [gather.pass_on([4], [u2], "down")]
        w_up, = gather.finish([3], tok, "up")
    a, r = _mm_nn(u2, w_up, None, "mm_up_fwd", epilogue=((), (), (F32, BF16), _squared_relu))
    if ex:
        w_down, = gather.finish([4], [r], "down")
    w_down1 = w_down.reshape(1, D_FF, D_MODEL)
    mlp = _mm_nn(r, w_down1, F32, "mm_down_fwd")
    loss, dh2, dh2b, g_nf = _loss_head(h1, mlp, norm_final_w, target, "loss_head")

    def reduce_scatter(grads, name):
        rs = _ReduceScatter(grads, ex.parity, name) if ex else None
        return rs, ([rs.token] if ex else [])

    g_down = _mm_tn(r, dh2b, 1, BF16, "mm_down_wgrad").reshape(N_DEV, D_FF // N_DEV, D_MODEL)
    rs_down, tok = reduce_scatter([g_down], "rs_down")
    da, = _mm_nt(dh2b, w_down1, None, "mm_down_dgrad", after=tok, epilogue=(
        (a,), (0,), (BF16,), lambda dr, av: (dr * (2.0 * jnp.maximum(av, 0.0)),)))
    tok = [rs_down.pair_sums([da])] if ex else []
    g_up = _mm_tn(u2, da, N_DEV, BF16, "mm_up_wgrad", after=tok)
    rs_up, tok = reduce_scatter([g_up], "rs_up")
    du2 = _mm_nt(da, w_up, F32, "mm_up_dgrad", after=tok)
    tok = [rs_up.pair_sums([du2])] if ex else []
    dh1, dh1b, g_nmlp = _rms_bwd(du2, h1, norm_mlp_w, dh2, "rms_mlp_bwd", after=tok)

    g_out = _mm_tn(merged, dh1b, 1, BF16, "mm_out_wgrad").reshape(N_DEV, D_MODEL // N_DEV, D_MODEL)
    dpa, dpb, dga, dgb = _mm_nt(dh1b, w_out1, None, "mm_out_dgrad", epilogue=(
        (z, z, pa, pb), (COL_GATE_A * GATE_TILE, COL_GATE_B * GATE_TILE, 0, 0), (BF16,) * 4, _merge_grads))
    g_a = _mm_tn(y_a, dpa, N_DEV, BF16, "mm_a_wgrad")
    g_b = _mm_tn(y_b, dpb, N_DEV, BF16, "mm_b_wgrad")
    rs_mix, tok = reduce_scatter([g_a, g_b, g_out], "rs_mix")
    dya = _mm_nt(dpa, w_a, F32, "mm_a_dgrad", after=tok)
    dyb = _mm_nt(dpb, w_b, F32, "mm_b_dgrad", after=tok)
    tok = [rs_mix.pair_sums([dya, dyb])] if ex else []
    daq, dak, dav, dbias_t = _attn_bwd(z, probs, dyb, "attn_bwd", after=tok)
    dhq, dhf, dhi, dhg, g_lbl, g_hgw = _hgrn2_bwd(z, lb_logits, hg_norm_w, o_raw, s_all, dya, "hgrn2_bwd",
                                                  after=tok)
    dbias_rows = jnp.pad(jnp.transpose(dbias_t.reshape(AT_HEADS // 2, BAND, 2, CHUNK), (3, 0, 2, 1)).reshape(
        CHUNK, AT_HEADS, BAND), ((0, 0), (0, 0), (0, N_REL_PAD - BAND)))
    dz =jnp.concatenate([dhq, dhf, dhi, dhg, daq, dak, dav, dga, dgb], axis=1)
    g_in = _mm_tn(u, dz, N_DEV, BF16, "mm_in_wgrad")
    rs_in, _ = reduce_scatter([g_in], "rs_in")
    tok = [rs_in.pair_sums([])] if ex else []
    du = _mm_nt(dz, w_in, F32, "mm_in_dgrad", after=tok)
    grad_x, _, g_nmix = _rms_bwd(du, x, norm_mix_w, dh1, "rms_mix_bwd")
    g_rel = _bias_reduce(dbias_rows, "bias_reduce", after=tok)[:, :N_REL]

    small = dict(lb_logits=g_lbl, hg_norm_w=g_hgw[0:1], rel_bias=g_rel, norm_mix_w=g_nmix, norm_mlp_w=g_nmlp,
                 norm_final_w=g_nf)
    grads = [rs_in, rs_mix, rs_up, rs_down] if ex else [g_in, g_a, g_b, g_out, g_up, g_down]
    return loss, grad_x, grads, small


def _mm_gathered(u, shard, order, name):
    T, K = u.shape
    _, Nb = shard.shape

    def body(order_ref, u_ref, shard_ref, z_ref, full_ref, wbuf, load_sem, send_sems, recv_sems, local_sem):
        s = pl.program_id(0)
        x, y, c = _position()
        me, sibling = (x, y, c), (x, y, 1 - c)
        chips = [(1 - x, y), (x, 1 - y), (1 - x, 1 - y)]

        def copy(k, block, to, src=None):
            dst = full_ref.at[4 * block[0] + 2 * block[1] + block[2]]
            return pltpu.make_async_remote_copy(
                src_ref=dst if src is None else src, dst_ref=dst,
                send_sem=send_sems.at[k], recv_sem=recv_sems.at[k], device_id=to, device_id_type=MESH)

        @pl.when(s == 0)
        def _():
            local = pltpu.make_async_copy(shard_ref, full_ref.at[4 * x + 2 * y + c], local_sem)
            local.start()
            copy(0, me, sibling, src=shard_ref).start()
            for j, chip in enumerate(chips):
                copy(1 + j, me, (*chip, c), src=shard_ref).start()
            local.wait()

        @pl.when(s == 1)
        def _():
            copy(0, sibling, me).wait_recv()

        for j, chip in enumerate(chips):
            direct, passed = ((2, 4), (3, 5), (6, 7))[j]

            @pl.when(s == direct)
            def _(j=j, chip=chip):
                copy(1 + j, (*chip, c), me).wait_recv()
                copy(4 + j, (*chip, c), sibling).start()

            @pl.when(s == passed)
            def _(j=j, chip=chip):
                copy(4 + j, (*chip, 1 - c), me).wait_recv()

        @pl.when(s < N_EARLY_BLOCKS)
        def _():
            load = pltpu.make_async_copy(full_ref.at[order_ref[s]], wbuf, load_sem)
            load.start()
            load.wait()
            z_ref[...] = jnp.dot(u_ref[...], wbuf[...], preferred_element_type=F32)

        @pl.when(s == N_DEV - 1)
        def _():
            for k in range(7):
                copy(k, me, sibling).wait_send()

    z, full = pl.pallas_call(
        body, name=name,
        grid_spec=pltpu.PrefetchScalarGridSpec(
            num_scalar_prefetch=1, grid=(N_DEV,),
            in_specs=[pl.BlockSpec((T, K), lambda s, order: (0, 0)), ANY],
            out_specs=[pl.BlockSpec((T, Nb), lambda s, order: (0, order[jnp.minimum(s, N_EARLY_BLOCKS - 1)])), ANY],
            scratch_shapes=[pltpu.VMEM((K, Nb), BF16), pltpu.SemaphoreType.DMA,
                            pltpu.SemaphoreType.DMA((7,)), pltpu.SemaphoreType.DMA((7,)), pltpu.SemaphoreType.DMA]),
        out_shape=[jax.ShapeDtypeStruct((T, N_DEV * Nb), F32), jax.ShapeDtypeStruct((N_DEV, K, Nb), BF16)],
        compiler_params=_cparams(("arbitrary",)),
    )(order, u, shard)
    return z, full


N_EARLY_BLOCKS = 6


def _mm_gathered_tail(u, full, z, order, name, after=()):
    T, K = u.shape
    _, _, Nb = full.shape
    n_after = len(after)

    def body(order_ref, u_ref, w_ref, z_in_ref, *rest):
        rest[n_after][...] = jnp.dot(u_ref[...], w_ref[...], preferred_element_type=F32)

    return pl.pallas_call(
        body, name=name,
        grid_spec=pltpu.PrefetchScalarGridSpec(
            num_scalar_prefetch=1, grid=(N_DEV - N_EARLY_BLOCKS,),
            in_specs=[pl.BlockSpec((T, K), lambda s, order: (0, 0)),
                      pl.BlockSpec((None, K, Nb), lambda s, order: (order[N_EARLY_BLOCKS + s], 0, 0)), ANY]
            + [ANY] * n_after,
            out_specs=pl.BlockSpec((T, Nb), lambda s, order: (0, order[N_EARLY_BLOCKS + s]))),
        out_shape=jax.ShapeDtypeStruct(z.shape, z.dtype),
        input_output_aliases={3: 0},
        compiler_params=_cparams(("arbitrary",)),
    )(order, u, full, z, *after)


def _gather_order():
    x, y, c = _position()
    chips = [(1 - x, y), (x, 1 - y), (1 - x, 1 - y)]
    ids = [4 * x + 2 * y + c, 4 * x + 2 * y + (1 - c)]
    ids += [4 * cx + 2 * cy + c for cx, cy in chips[:2]] + [4 * cx + 2 * cy + (1 - c) for cx, cy in chips[:2]]
    ids += [4 * chips[2][0] + 2 * chips[2][1] + c, 4 * chips[2][0] + 2 * chips[2][1] + (1 - c)]
    return jnp.stack(ids).astype(jnp.int32)


def _pair_sum(g, land, parity, name):
    _, R, C = g.shape
    tr = _pick(R, (512, 256))

    def body(par_ref, g_ref, l_ref, o_ref):
        o_ref[...] = (g_ref[...].astype(F32) + l_ref[...].astype(F32)).astype(BF16)

    return pl.pallas_call(
        body, name=name,
        grid_spec=pltpu.PrefetchScalarGridSpec(
            num_scalar_prefetch=1, grid=(N_CHIP, R // tr),
            in_specs=[pl.BlockSpec((None, tr, C), lambda s, i, par: (2 * s + par[0], i, 0)),
                      pl.BlockSpec((None, tr, C), lambda s, i, par: (s, i, 0))],
            out_specs=pl.BlockSpec((None, tr, C), lambda s, i, par: (s, i, 0))),
        out_shape=jax.ShapeDtypeStruct((N_CHIP, R, C), BF16),
        compiler_params=_cparams(("parallel", "parallel")),
    )(parity, g, land)


HBM = pl.BlockSpec(memory_space=pltpu.HBM)
SEM = pl.BlockSpec(memory_space=pltpu.SEMAPHORE)
DATAFLOW = pltpu.SideEffectType.DATAFLOW_SIDE_EFFECTING


def _split_call(name, bufs, waits=(), starts=None, after=()):
    nb = len(bufs)
    n_new = starts[1] if starts else 0
    wait_sems = [s for w in waits for s in (*w[1], *w[2])]

    def body(*refs):
        b, pos = refs[:nb], nb
        for plan, ss, _, send_idx, recv_idx in waits:
            k = len(ss)
            copies = plan(b, refs[pos:pos + k], refs[pos + k:pos + 2 * k])
            pos += 2 * k
            for i in recv_idx:
                copies[i].wait_recv()
            for i in send_idx:
                copies[i].wait_send()
        outs = refs[pos + len(after):]
        if starts:
            for cp in starts[0](b, outs[nb:nb + n_new], outs[nb + n_new:nb + 2 * n_new]):
                cp.start()
        outs[-1][...] = jnp.zeros_like(outs[-1])

    res = pl.pallas_call(
        body, name=name,
        out_shape=tuple(pltpu.HBM(a.shape, a.dtype) for a in bufs) + (pltpu.SemaphoreType.DMA(()),) * (2 * n_new)
        + (jax.ShapeDtypeStruct((8, 128), F32),),
        in_specs=[HBM] * nb + [SEM] * len(wait_sems) + [ANY] * len(after),
        out_specs=(HBM,) * nb + (SEM,) * (2 * n_new) + (pl.BlockSpec(memory_space=pltpu.VMEM),),
        input_output_aliases={i: i for i in range(nb)},
        compiler_params=pltpu.CompilerParams(has_side_effects=DATAFLOW),
    )(*bufs, *wait_sems, *after)
    return list(res[:nb]), list(res[nb:nb + n_new]), list(res[nb + n_new:nb + 2 * n_new]), res[-1]


def _in_hbm(a):
    return pltpu.with_memory_space_constraint(a, pltpu.HBM)


def _remote(src, dst, send_sem, recv_sem, to):
    return pltpu.make_async_remote_copy(src_ref=src, dst_ref=dst, send_sem=send_sem, recv_sem=recv_sem,
                                        device_id=to, device_id_type=MESH)


def _other_chips():
    x, y, _ = _position()
    return [(1 - x, y), (x, 1 - y), (1 - x, 1 - y)]


def _plan_gather_first(n):
    def plan(b, ss, rs):
        x, y, c = _position()
        to = [(x, y, 1 - c)] + [(*chip, c) for chip in _other_chips()]
        return [_remote(b[w], b[n + w].at[4 * x + 2 * y + c], ss[4 * w + k], rs[4 * w + k], to[k])
                for w in range(n) for k in range(4)]
    return plan, 4 * n


def _plan_gather_pass(n):
    def plan(b, ss, rs):
        x, y, c = _position()
        copies = []
        for w in range(n):
            for j, chip in enumerate(_other_chips()):
                blk = b[n + w].at[4 * chip[0] + 2 * chip[1] + c]
                copies.append(_remote(blk, blk, ss[3 * w + j], rs[3 * w + j], (x, y, 1 - c)))
        return copies
    return plan, 3 * n


def _plan_sibling(n):
    def plan(b, ss, rs):
        x, y, c = _position()
        return [_remote(b[w].at[2 * s + (1 - c)], b[n + w].at[s], ss[4 * w + s], rs[4 * w + s], (x, y, 1 - c))
                for w in range(n) for s in range(N_CHIP)]
    return plan, 4 * n


def _plan_scatter(n):
    def plan(b, ss, rs):
        x, y, c = _position()
        return [_remote(b[w].at[2 * chip[0] + chip[1]], b[n + w].at[2 * x + y], ss[3 * w + j], rs[3 * w + j],
                        (*chip, c))
                for w in range(n) for j, chip in enumerate(_other_chips())]
    return plan, 3 * n


class _Gather:
    def __init__(self, shards, after, name):
        self.n, self.name = len(shards), name
        x, y, c = _position()
        placed = [lax.dynamic_update_index_in_dim(lax.empty((N_DEV,) + s.shape, s.dtype), s, 4 * x + 2 * y + c, 0)
                  for s in shards]
        bufs, self.ss, self.rs, self.token = _split_call(
            name + "_start", [_in_hbm(a) for a in list(shards) + placed], starts=_plan_gather_first(self.n),
            after=after)
        self.shards, self.fulls = bufs[:self.n], bufs[self.n:]
        self.passed = {}

    def _sub(self, ids, sems, per):
        return [sems[per * w + k] for w in ids for k in range(per)]

    def pass_on(self, ids, after, tag):
        m = len(ids)
        first = (_plan_gather_first(m)[0], self._sub(ids, self.ss, 4), self._sub(ids, self.rs, 4),
                 [], [4 * i + k for i in range(m) for k in (1, 2, 3)])
        bufs, ss, rs, token = _split_call(
            "%s_pass_%s" % (self.name, tag), [self.shards[w] for w in ids] + [self.fulls[w] for w in ids],
            waits=[first], starts=_plan_gather_pass(m), after=after)
        for i, w in enumerate(ids):
            self.shards[w], self.fulls[w] = bufs[i], bufs[m + i]
        self.passed[tuple(ids)] = (ss, rs)
        return token

    def finish(self, ids, after, tag):
        m = len(ids)
        ss2, rs2 = self.passed[tuple(ids)]
        first = (_plan_gather_first(m)[0], self._sub(ids, self.ss, 4), self._sub(ids, self.rs, 4),
                 list(range(4 * m)), [4 * i for i in range(m)])
        passed = (_plan_gather_pass(m)[0], ss2, rs2, list(range(3 * m)), list(range(3 * m)))
        bufs, _, _, _ = _split_call(
            "%s_finish_%s" % (self.name, tag), [self.shards[w] for w in ids] + [self.fulls[w] for w in ids],
            waits=[first, passed], after=after)
        return bufs[m:]


class _ReduceScatter:
    def __init__(self, grads, parity, name):
        self.n, self.name, self.parity = len(grads), name, parity
        lands = [lax.empty((N_CHIP,) + g.shape[1:], g.dtype) for g in grads]
        self.bufs, self.ss, self.rs, self.token = _split_call(
            name + "_sibling_start", [_in_hbm(a) for a in list(grads) + lands], starts=_plan_sibling(self.n))

    def pair_sums(self, after):
        n = self.n
        bufs, _, _, _ = _split_call(
            self.name + "_sibling_wait", self.bufs,
            waits=[(_plan_sibling(n)[0], self.ss, self.rs, list(range(4 * n)), list(range(4 * n)))], after=after)
        sums = [_pair_sum(bufs[w], bufs[n + w], self.parity, "%s_pair_sum_%d" % (self.name, w)) for w in range(n)]
        lands = [lax.empty(s.shape, s.dtype) for s in sums]
        self.bufs, self.ss, self.rs, token = _split_call(
            self.name + "_scatter_start", [_in_hbm(a) for a in sums + lands], starts=_plan_scatter(n))
        return token

    def finish(self, after):
        n = self.n
        bufs, _, _, _ = _split_call(
            self.name + "_scatter_wait", self.bufs,
            waits=[(_plan_scatter(n)[0], self.ss, self.rs, list(range(3 * n)), list(range(3 * n)))], after=after)
        return bufs[:n], bufs[n:]


class _Exchanges:
    def __init__(self, parity, order):
        self.parity, self.order = parity, order


def _gather_small(packed, name):
    R = packed.shape[0]

    def body(x_ref, out_ref, send_sems, recv_sems):
        x, y, c = _position()
        me = 4 * x + 2 * y + c
        out_ref[me] = x_ref[...]
        copies = []
        for k in range(1, N_DEV):
            to = (x ^ ((k >> 2) & 1), y ^ ((k >> 1) & 1), c ^ (k & 1))
            cp = pltpu.make_async_remote_copy(
                src_ref=x_ref, dst_ref=out_ref.at[me],
                send_sem=send_sems.at[k], recv_sem=recv_sems.at[k], device_id=to, device_id_type=MESH)
            cp.start()
            copies.append((k, to, cp))
        for k, to, cp in copies:
            cp.wait_send()
            pltpu.make_async_remote_copy(
                src_ref=x_ref, dst_ref=out_ref.at[4 * to[0] + 2 * to[1] + to[2]],
                send_sem=send_sems.at[k], recv_sem=recv_sems.at[k], device_id=to, device_id_type=MESH).wait_recv()

    return pl.pallas_call(
        body, name=name,
        in_specs=[pl.BlockSpec(memory_space=pltpu.VMEM)], out_specs=pl.BlockSpec(memory_space=pltpu.VMEM),
        out_shape=jax.ShapeDtypeStruct((N_DEV, R, 128), F32),
        scratch_shapes=[pltpu.SemaphoreType.DMA((N_DEV,)), pltpu.SemaphoreType.DMA((N_DEV,))],
    )(packed)


def _adamw_math(w, g, m, v):
    m = ADAM_B1 * m + (1.0 - ADAM_B1) * g
    v = ADAM_B2 * v + (1.0 - ADAM_B2) * (g * g)
    m_hat = m / (1.0 - ADAM_B1 ** ADAM_STEP)
    v_hat = v / (1.0 - ADAM_B2 ** ADAM_STEP)
    delta = -ADAM_LR * (m_hat / (jnp.sqrt(v_hat) + ADAM_EPS) + ADAM_WD * w)
    return delta, m, v


def _adamw_big_landed(w, m, v, parts, lands, slot, name):
    R, C = w.shape
    tr = _pick(R, (256,))

    def body(slot_ref, w_ref, m_ref, v_ref, own_ref, l1_ref, l2_ref, l3_ref, g_ref, d_ref, nm_ref, nv_ref):
        g = own_ref[...].astype(F32)
        for ref in (l1_ref, l2_ref, l3_ref):
            g = g + ref[...].astype(F32)
        d, nm, nv = _adamw_math(w_ref[...], g, m_ref[...], v_ref[...])
        g_ref[...] = g
        d_ref[...] = d
        nm_ref[...] = nm
        nv_ref[...] = nv

    blk = pl.BlockSpec((tr, C), lambda i, slot: (i, 0))

    def chip(k):
        return pl.BlockSpec((None, tr, C), lambda i, slot: ((slot[0] + k) % N_CHIP, i, 0))

    out = jax.ShapeDtypeStruct((R, C), F32)
    return pl.pallas_call(
        body, name=name,
        grid_spec=pltpu.PrefetchScalarGridSpec(
            num_scalar_prefetch=1, grid=(R // tr,),
            in_specs=[blk, blk, blk, chip(0), chip(1), chip(2), chip(3)],
            out_specs=[blk, blk, blk, blk]),
        out_shape=[out, out, out, out],
        compiler_params=_cparams(("parallel",)),
    )(slot, w, m, v, parts, lands, lands, lands)


def _adamw_small(w, m, v, gathered, name):
    R = w.shape[0]

    def body(w_ref, m_ref, v_ref, p_ref, g_ref, d_ref, nm_ref, nv_ref):
        g = p_ref[0]
        for s in range(1, N_DEV):
            g = g + p_ref[s]
        d, nm, nv = _adamw_math(w_ref[...], g, m_ref[...], v_ref[...])
        g_ref[...] = g
        d_ref[...] = d
        nm_ref[...] = nm
        nv_ref[...] = nv

    out = jax.ShapeDtypeStruct((R, 128), F32)
    return pl.pallas_call(
        body, name=name, out_shape=[out, out, out, out],
    )(w, m, v, gathered)


SMALL_NAMES = ("lb_logits", "hg_norm_w", "rel_bias", "norm_mix_w", "norm_mlp_w", "norm_final_w")
SMALL_SHAPES = {"lb_logits": (2, HG_WIDTH), "hg_norm_w": (1, HG_DK), "rel_bias": (AT_HEADS, N_REL_PAD),
                "norm_mix_w": (1, D_MODEL), "norm_mlp_w": (1, D_MODEL), "norm_final_w": (1, D_MODEL)}


def _pack_small(parts):
    rows = []
    for nme in SMALL_NAMES:
        p = parts[nme]
        if nme == "rel_bias":
            p = jnp.pad(p, ((0, 0), (0, N_REL_PAD - N_REL)))
        rows.append(p.reshape(-1, 128))
    flat = jnp.concatenate(rows, axis=0)
    return jnp.pad(flat, ((0, SMALL_ROWS - flat.shape[0]), (0, 0)))


def _unpack_small(packed):
    out, at = {}, 0
    for nme in SMALL_NAMES:
        shp = SMALL_SHAPES[nme]
        nrow = shp[0] * shp[1] // 128
        p = packed[at:at + nrow].reshape(shp)
        at += nrow
        out[nme] = p[:, :N_REL] if nme == "rel_bias" else p
    return out


BIG_NAMES = ("w_in", "w_branch_a", "w_branch_b", "w_out", "w_up", "w_down")


def kernel(x, w_in, lb_logits, hg_norm_w, rel_bias, w_branch_a, w_branch_b, w_out, norm_mix_w, norm_mlp_w, w_up, w_down, norm_final_w, loss_target, m_w_in, m_lb_logits, m_hg_norm_w, m_rel_bias, m_w_branch_a, m_w_branch_b, m_w_out, m_norm_mix_w, m_norm_mlp_w, m_w_up, m_w_down, m_norm_final_w, v_w_in, v_lb_logits, v_hg_norm_w, v_rel_bias, v_w_branch_a, v_w_branch_b, v_w_out, v_norm_mix_w, v_norm_mlp_w, v_w_up, v_w_down, v_norm_final_w):
    big_w = [w_in[0], w_branch_a[0], w_branch_b[0], w_out[0], w_up[0], w_down[0]]
    big_m = [m_w_in[0], m_w_branch_a[0], m_w_branch_b[0], m_w_out[0], m_w_up[0], m_w_down[0]]
    big_v = [v_w_in[0], v_w_branch_a[0], v_w_branch_b[0], v_w_out[0], v_w_up[0], v_w_down[0]]

    shards = [w.astype(BF16) for w in big_w]
    parity = lax.axis_index("c").astype(jnp.int32).reshape(1)
    loss_part, grad_x, chip_parts, small = _local_step(
        x[0], loss_target[0], lb_logits, hg_norm_w, rel_bias[0], norm_mix_w, norm_mlp_w,
        norm_final_w.reshape(1, D_MODEL), shards[0], shards[1:], _Exchanges(parity, _gather_order()))
    loss = lax.psum(loss_part[0, 0], ("x", "y", "c"))
    rs_in, rs_mix, rs_up, rs_down = chip_parts
    slot = (2 * lax.axis_index("x") + lax.axis_index("y")).astype(jnp.int32).reshape(1)
    big = {}

    def finish(rs, names, after):
        sums, lands = rs.finish(after)
        for nme, own, land in zip(names, sums, lands):
            i = BIG_NAMES.index(nme)
            big[nme] = _adamw_big_landed(big_w[i], big_m[i], big_v[i], own, land, slot, "adamw_" + nme)
        return [big[nme][1] for nme in names]

    done = finish(rs_down, ["w_down"], [grad_x])
    done = finish(rs_up, ["w_up"], done)
    done = finish(rs_mix, ["w_branch_a", "w_branch_b", "w_out"], done)

    sw = dict(lb_logits=lb_logits, hg_norm_w=hg_norm_w, rel_bias=rel_bias[0], norm_mix_w=norm_mix_w,
              norm_mlp_w=norm_mlp_w, norm_final_w=norm_final_w.reshape(1, D_MODEL))
    sm = dict(lb_logits=m_lb_logits, hg_norm_w=m_hg_norm_w, rel_bias=m_rel_bias[0], norm_mix_w=m_norm_mix_w,
              norm_mlp_w=m_norm_mlp_w, norm_final_w=m_norm_final_w.reshape(1, D_MODEL))
    sv = dict(lb_logits=v_lb_logits, hg_norm_w=v_hg_norm_w, rel_bias=v_rel_bias[0], norm_mix_w=v_norm_mix_w,
              norm_mlp_w=v_norm_mlp_w, norm_final_w=v_norm_final_w.reshape(1, D_MODEL))
    gathered = _gather_small(_pack_small(small), "gather_small")
    small_packed = _adamw_small(_pack_small(sw), _pack_small(sm), _pack_small(sv), gathered, "adamw_small")
    small_out = [_unpack_small(p) for p in small_packed]

    finish(rs_in, ["w_in"], done + [small_packed[0]])

    def leaf(kind, nme):
        if nme in BIG_NAMES:
            return big[nme][kind][None]
        p = small_out[kind][nme]
        if nme == "rel_bias":
            return p[None]
        if nme == "norm_final_w":
            return p.reshape(D_MODEL)
        return p

    order = ("w_in", "lb_logits", "hg_norm_w", "rel_bias", "w_branch_a", "w_branch_b", "w_out", "norm_mix_w",
             "norm_mlp_w", "w_up", "w_down", "norm_final_w")
    outs = [loss, grad_x[None]]
    for kind in range(4):
        outs += [leaf(kind, nme) for nme in order]
    return tuple(outs)
```

```python
import jax
import jax.numpy as jnp
from jax import lax
from jax.experimental import pallas as pl
from jax.experimental.pallas import tpu as pltpu

F32 = jnp.float32
BF16 = jnp.bfloat16
HIGHEST = lax.Precision.HIGHEST
MESH = pl.DeviceIdType.MESH

D_MODEL = 2048
HG_HEADS = 8
HG_DK = 128
HG_WIDTH = 1024
AT_HEADS = 16
AT_DH = 64
AT_WIDTH = 1024
CHUNK = 64
LEFT_CHUNKS = 8
BAND = (LEFT_CHUNKS + 1) * CHUNK
PAD = LEFT_CHUNKS * CHUNK
REL_CLIP = 256
N_REL = 2 * REL_CLIP + 1
N_REL_PAD = 640
D_FF = 4 * D_MODEL
EPS = 1e-6
N_DEV = 8
N_CHIP = 4

ADAM_LR = 0.001
ADAM_B1 = 0.9
ADAM_B2 = 0.999
ADAM_EPS = 1e-08
ADAM_WD = 0.01
ADAM_STEP = 10

COL_HQ, COL_HF, COL_HI, COL_HG = 0, 8, 16, 24
COL_AQ, COL_AK, COL_AV = 32, 40, 48
COL_GATE_A, COL_GATE_B = 7, 9

VMEM_LIMIT = 56 * 1024 * 1024
SMALL_ROWS = 152


def _cparams(sem=None, **kw):
    if sem is not None:
        kw["dimension_semantics"] = sem
    return pltpu.CompilerParams(vmem_limit_bytes=VMEM_LIMIT, **kw)


def _pick(n, cands):
    for c in cands:
        if n % c == 0:
            return c
    return n


def _sigmoid(x):
    return 1.0 / (1.0 + jnp.exp(-x))


ANY = pl.BlockSpec(memory_space=pl.ANY)


def _position():
    return lax.axis_index("x"), lax.axis_index("y"), lax.axis_index("c")


def _call(body, args, *, name, grid, in_specs, out_specs, out_shape, scratch_shapes=(), sem=None, after=()):
    n_in = len(args)

    def ordered(*refs):
        body(*refs[:n_in], *refs[n_in + len(after):])

    return list(pl.pallas_call(
        ordered if after else body, name=name, grid=grid, in_specs=list(in_specs) + [ANY] * len(after),
        out_specs=out_specs, out_shape=out_shape, scratch_shapes=list(scratch_shapes),
        compiler_params=_cparams(sem))(*args, *after))


MAX_CONTRACTION_TILE = 4096


def _accumulate(part, acc_ref, step, n_steps, finish):
    if n_steps == 1:
        finish(part)
        return

    @pl.when(step == 0)
    def _():
        acc_ref[...] = part

    @pl.when(step > 0)
    def _():
        acc_ref[...] += part

    @pl.when(step == n_steps - 1)
    def _():
        finish(acc_ref[...])


def _mm_nn(a, wb, out_dtype, name, after=(), epilogue=None):
    M, K = a.shape
    NB, K2, Nb = wb.shape
    assert K == K2
    tm = min(M, 1024)
    tk = min(K, MAX_CONTRACTION_TILE)
    tn = _pick(Nb, (512, 1408, 256))
    nk = K // tk
    nn = Nb // tn
    extra, first_cols, out_dtypes, fn = epilogue or ((), (), (out_dtype,), lambda total: (total,))
    n_extra, n_out = len(extra), len(out_dtypes)

    def body(a_ref, b_ref, *rest):
        def finish(total):
            results = fn(total, *[r[...] for r in rest[:n_extra]])
            for o_ref, res, dt in zip(rest[n_extra:n_extra + n_out], results, out_dtypes):
                o_ref[...] = res.astype(dt)

        part = jnp.dot(a_ref[...], b_ref[...], preferred_element_type=F32)
        _accumulate(part, rest[-1], pl.program_id(3), nk, finish)

    def tile(first):
        return pl.BlockSpec((tm, tn), lambda m, j, n, k: (m, first + j * nn + n))

    outs = _call(
        body, (a, wb) + tuple(extra), name=name, grid=(M // tm, NB, nn, nk),
        in_specs=[pl.BlockSpec((tm, tk), lambda m, j, n, k: (m, k)),
                  pl.BlockSpec((None, tk, tn), lambda m, j, n, k: (j, k, n))] + [tile(col // tn) for col in first_cols],
        out_specs=[tile(0)] * n_out,
        out_shape=[jax.ShapeDtypeStruct((M, NB * Nb), dt) for dt in out_dtypes],
        scratch_shapes=[] if nk == 1 else [pltpu.VMEM((tm, tn), F32)],
        sem=("parallel", "parallel", "parallel", "arbitrary"), after=after)
    return outs if epilogue else outs[0]


def _squared_relu(a):
    ra = jnp.maximum(a, 0.0)
    return a, ra * ra


def _gated_merge(pb, za, zb, pa):
    return pb, _sigmoid(za) * pa + _sigmoid(zb) * pb


def _mm_nt(a, wb, out_dtype, name, after=(), epilogue=None):
    M, N = a.shape
    NB, K, Nb = wb.shape
    assert N == NB * Nb
    tm = min(M, 1024)
    n_tiles_live = 1 + (len(epilogue[0]) + len(epilogue[2]) if epilogue else 0)
    tko = _pick(K, (1024,)) if n_tiles_live <= 3 else _pick(K, (512,))
    tc = _pick(Nb, (2048, 1024, 1408, 256))
    nc = Nb // tc
    jb = max([d for d in (8, 4, 2, 1) if NB % d == 0 and d * tc <= MAX_CONTRACTION_TILE]) if nc == 1 else 1
    nsteps = (NB // jb) * nc
    extra, first_cols, out_dtypes, fn = epilogue or ((), (), (out_dtype,), lambda total: (total,))
    n_extra, n_out = len(extra), len(out_dtypes)

    def body(a_ref, b_ref, *rest):
        def finish(total):
            results = fn(total, *[r[...] for r in rest[:n_extra]])
            for o_ref, res, dt in zip(rest[n_extra:n_extra + n_out], results, out_dtypes):
                o_ref[...] = res.astype(dt)

        part = sum(lax.dot_general(a_ref[:, i * tc:(i + 1) * tc], b_ref[i], (((1,), (1,)), ((), ())),
                                   preferred_element_type=F32) for i in range(jb))
        _accumulate(part, rest[-1], pl.program_id(2) * nc + pl.program_id(3), nsteps, finish)

    def tile(first):
        return pl.BlockSpec((tm, tko), lambda m, ko, j, c: (m, first + ko))

    outs = _call(
        body, (a, wb) + tuple(extra), name=name,
        grid=(M // tm, K // tko, NB // jb, nc),
        in_specs=[pl.BlockSpec((tm, jb * tc), lambda m, ko, j, c: (m, j * nc + c)),
                  pl.BlockSpec((jb, tko, tc), lambda m, ko, j, c: (j, ko, c))] + [tile(col // tko) for col in first_cols],
        out_specs=[tile(0)] * n_out,
        out_shape=[jax.ShapeDtypeStruct((M, K), dt) for dt in out_dtypes],
        scratch_shapes=[] if nsteps == 1 else [pltpu.VMEM((tm, tko), F32)],
        sem=("parallel", "parallel", "arbitrary", "arbitrary"), after=after)
    return outs if epilogue else outs[0]


ROWS_TILE = 512
ROWS_PIECE = 128


def _mm_rows(a, w, extras, vectors, row_dtypes, fn, name):
    M, K = a.shape
    N = w.shape[1]
    tm = min(M, ROWS_TILE)
    n_e, n_v = len(extras), len(vectors)

    def body(a_ref, w_ref, *rest):
        tiles, vecs, outs, product_ref = rest[:n_e], rest[n_e:n_e + n_v], rest[n_e + n_v:-1], rest[-1]
        product_ref[...] = jnp.dot(a_ref[...], w_ref[...], preferred_element_type=F32)
        for i in range(tm // ROWS_PIECE):
            piece = slice(i * ROWS_PIECE, (i + 1) * ROWS_PIECE)
            results = fn(product_ref[piece, :], *[t[piece, :] for t in tiles], *[v[...] for v in vecs])
            for o_ref, res, dt in zip(outs, results, row_dtypes):
                o_ref[piece, :] = res.astype(dt)

    row = pl.BlockSpec((tm, N), lambda m: (m, 0))
    return _call(
        body, (a, w) + tuple(extras) + tuple(vectors), name=name, grid=(M // tm,),
        in_specs=[pl.BlockSpec((tm, K), lambda m: (m, 0)), pl.BlockSpec((K, N), lambda m: (0, 0))]
        + [row] * n_e + [pl.BlockSpec((1, N), lambda m: (0, 0))] * n_v,
        out_specs=[row] * len(row_dtypes),
        out_shape=[jax.ShapeDtypeStruct((M, N), dt) for dt in row_dtypes],
        scratch_shapes=[pltpu.VMEM((tm, N), F32)], sem=("parallel",))


def _rms(h, w):
    return h * lax.rsqrt(jnp.mean(h * h, axis=-1, keepdims=True) + EPS) * w


def _residual_rms_rows(mix, x, w):
    h = x + mix
    return h, _rms(h, w)


def _mm_tn(a, g, nb, out_dtype, name, after=()):
    M, Ka = a.shape
    M2, N = g.shape
    assert M == M2 and N % nb == 0
    Nb = N // nb
    tka = _pick(Ka, (1024,))
    tn = _pick(Nb, (512, 1408, 256))
    nn = Nb // tn

    def body(a_ref, g_ref, o_ref):
        o_ref[...] = lax.dot_general(a_ref[...], g_ref[...], (((0,), (0,)), ((), ())),
                                     preferred_element_type=F32).astype(out_dtype)

    return _call(
        body, (a, g), name=name,
        grid=(Ka // tka, nb, nn),
        in_specs=[pl.BlockSpec((M, tka), lambda ka, j, n: (0, ka)),
                  pl.BlockSpec((M, tn), lambda ka, j, n: (0, j * nn + n))],
        out_specs=[pl.BlockSpec((None, tka, tn), lambda ka, j, n: (j, ka, n))],
        out_shape=[jax.ShapeDtypeStruct((nb, Ka, Nb), out_dtype)],
        sem=("parallel", "parallel", "parallel"), after=after)[0]


ROW_TILE = 256


def _rms_fwd(x, w, name):
    T, Dm = x.shape

    def body(x_ref, w_ref, u_ref):
        xv = x_ref[...]
        r = lax.rsqrt(jnp.mean(xv * xv, axis=-1, keepdims=True) + EPS)
        u_ref[...] = (xv * r * w_ref[...]).astype(BF16)

    return pl.pallas_call(
        body, name=name, grid=(T // ROW_TILE,),
        in_specs=[pl.BlockSpec((ROW_TILE, Dm), lambda i: (i, 0)), pl.BlockSpec((1, Dm), lambda i: (0, 0))],
        out_specs=pl.BlockSpec((ROW_TILE, Dm), lambda i: (i, 0)),
        out_shape=jax.ShapeDtypeStruct((T, Dm), BF16),
        compiler_params=_cparams(("parallel",)),
    )(x, w)


def _loss_head(h1, mlp, wf, target, name):
    T, Dm = h1.shape

    def body(h_ref, m_ref, w_ref, t_ref, loss_ref, dh_ref, dhb_ref, dw_ref):
        i = pl.program_id(0)
        h = h_ref[...] + m_ref[...]
        r = lax.rsqrt(jnp.mean(h * h, axis=-1, keepdims=True) + EPS)
        xh = h * r
        wv = w_ref[...]
        e = xh * wv - t_ref[...]
        part = 0.5 * jnp.sum(jnp.mean(e * e, axis=-1, keepdims=True), axis=0, keepdims=True)
        dy = e * (1.0 / Dm)
        dw = jnp.sum(dy * xh, axis=0, keepdims=True)
        gy = dy * wv
        dh = r * (gy - xh * jnp.mean(gy * xh, axis=-1, keepdims=True))
        dh_ref[...] = dh
        dhb_ref[...] = dh.astype(BF16)

        @pl.when(i == 0)
        def _():
            loss_ref[...] = jnp.zeros_like(loss_ref)
            dw_ref[...] = jnp.zeros_like(dw_ref)

        loss_ref[...] += jnp.broadcast_to(part, loss_ref.shape)
        dw_ref[...] += dw

    row = pl.BlockSpec((ROW_TILE, Dm), lambda i: (i, 0))
    vec = pl.BlockSpec((1, Dm), lambda i: (0, 0))
    return pl.pallas_call(
        body, name=name, grid=(T // ROW_TILE,),
        in_specs=[row, row, vec, row],
        out_specs=[pl.BlockSpec((8, 128), lambda i: (0, 0)), row, row, vec],
        out_shape=[jax.ShapeDtypeStruct((8, 128), F32), jax.ShapeDtypeStruct((T, Dm), F32),
                   jax.ShapeDtypeStruct((T, Dm), BF16), jax.ShapeDtypeStruct((1, Dm), F32)],
        compiler_params=_cparams(("arbitrary",)),
    )(h1, mlp, wf, target)


def _rms_bwd(dyn, x, w, dres, name, after=()):
    T, Dm = x.shape

    def body(g_ref, x_ref, w_ref, r_ref, dx_ref, dxb_ref, dw_ref):
        i = pl.program_id(0)
        xv = x_ref[...]
        r = lax.rsqrt(jnp.mean(xv * xv, axis=-1, keepdims=True) + EPS)
        xh = xv * r
        g = g_ref[...]
        dw = jnp.sum(g * xh, axis=0, keepdims=True)
        gy = g * w_ref[...]
        dx = r_ref[...] + r * (gy - xh * jnp.mean(gy * xh, axis=-1, keepdims=True))
        dx_ref[...] = dx
        dxb_ref[...] = dx.astype(BF16)

        @pl.when(i == 0)
        def _():
            dw_ref[...] = jnp.zeros_like(dw_ref)

        dw_ref[...] += dw

    row = pl.BlockSpec((ROW_TILE, Dm), lambda i: (i, 0))
    vec = pl.BlockSpec((1, Dm), lambda i: (0, 0))
    return _call(
        body, (dyn, x, w, dres), name=name, grid=(T // ROW_TILE,),
        in_specs=[row, row, vec, row],
        out_specs=[row, row, vec],
        out_shape=[jax.ShapeDtypeStruct((T, Dm), F32), jax.ShapeDtypeStruct((T, Dm), BF16),
                   jax.ShapeDtypeStruct((1, Dm), F32)],
        sem=("arbitrary",), after=after)


GATE_TILE = 1024


def _merge_grads(d, za, zb, pa, pb):
    ga = _sigmoid(za)
    gb = _sigmoid(zb)
    return d * ga, d * gb, d * pa * ga * (1.0 - ga), d * pb * gb * (1.0 - gb)


def _dot_hi(a, b, dims):
    return lax.dot_general(a, b, (dims, ((), ())), precision=HIGHEST, preferred_element_type=F32)


NN = ((1,), (0,))
NT = ((1,), (1,))
TN = ((0,), (0,))


def _hg_gates(hq, hf, lb):
    sq = _sigmoid(hq)
    q = hq * sq * (HG_DK ** -0.5)
    f = _sigmoid(hf)
    g = lb + (1.0 - lb) * f
    return q, sq, f, g, jnp.log(g), 1.0 - g


def _tri(lower):
    r = lax.broadcasted_iota(jnp.int32, (CHUNK, CHUNK), 0)
    c = lax.broadcasted_iota(jnp.int32, (CHUNK, CHUNK), 1)
    return jnp.where((r >= c) if lower else (r <= c), 1.0, 0.0).astype(F32)


GROUP = 16
N_GROUPS = CHUNK // GROUP
BWD_CHUNKS_PER_TRIP = 4


def _dot_bf16(a, b, dims):
    return lax.dot_general(a.astype(BF16), b.astype(BF16), (dims, ((), ())), preferred_element_type=F32)


def _rows_iota():
    return lax.broadcasted_iota(jnp.int32, (CHUNK, HG_DK), 0)


def _by_query_group(q, kk, b, g):
    r0 = GROUP * g
    b0 = b[r0:r0 + 1]
    decay = jnp.exp(b[r0:r0 + GROUP] - b0)
    ks = jnp.where(_rows_iota() < r0, kk * jnp.exp(jnp.minimum(b0 - b, 0.0)), 0.0)
    return q[r0:r0 + GROUP] * decay, ks, decay


def _by_key_group(q, kk, b, j):
    r1 = GROUP * (j + 1)
    b1 = b[r1 - 1:r1]
    decay = jnp.exp(b1 - b[r1 - GROUP:r1])
    qs = jnp.where(_rows_iota() >= r1, q * jnp.exp(jnp.minimum(b - b1, 0.0)), 0.0)
    return qs, kk[r1 - GROUP:r1] * decay, decay


def _scores_between_groups(q, kk, b):
    blocks = [jnp.zeros((GROUP, CHUNK), F32)]
    for g in range(1, N_GROUPS):
        qs, ks, _ = _by_query_group(q, kk, b, g)
        blocks.append(_dot_bf16(qs, ks, NT))
    return jnp.concatenate(blocks, axis=0)


def _hgrn2_fwd(z, lb_logits, hg_norm_w, name, after=()):
    T = z.shape[0]
    n_chunks = T // CHUNK

    def body(hq_ref, hf_ref, hi_ref, hg_ref, lbl_ref, nw_ref, o_ref, ya_ref, sall_ref, st_ref):
        lbl = lbl_ref[...]
        lb = 1.0 / (1.0 + jnp.exp(lbl[1:2, :] - lbl[0:1, :]))
        st_ref[...] = jnp.zeros_like(st_ref)
        tri = _tri(True)
        row8 = lax.broadcasted_iota(jnp.int32, (8, HG_DK), 0)

        def chunk(c, carry):
            rows = pl.ds(pl.multiple_of(c * CHUNK, CHUNK), CHUNK)
            q, _, _, _, lg, kk = _hg_gates(hq_ref[rows, :], hf_ref[rows, :], lb)
            v = hi_ref[rows, :]
            b = _dot_hi(tri, lg, NN)
            st = st_ref[...]
            sall_ref[c] = st
            for grp in range(N_GROUPS):
                r0 = GROUP * grp
                for h8 in range(GROUP // 8):
                    n = 8 * (h8 + 1)
                    bs, ks, vs = b[r0:r0 + n], kk[r0:r0 + n], v[r0:r0 + n]
                    sidx = lax.broadcasted_iota(jnp.int32, (n, HG_DK), 0)
                    blk = jnp.zeros((8, HG_DK), F32)
                    for i in range(8):
                        t = r0 + 8 * h8 + i
                        e = jnp.where(sidx <= 8 * h8 + i, jnp.exp(b[t:t + 1] - bs), 0.0)
                        p = jnp.sum(e * ks * q[t:t + 1], axis=1, keepdims=True)
                        ot = jnp.sum(p * vs, axis=0, keepdims=True)
                        blk = blk + jnp.where(row8 == i, ot, 0.0)
                    o_ref[pl.ds(pl.multiple_of(c * CHUNK + r0 + 8 * h8, 8), 8), :] = blk
            o_ref[rows, :] += _dot_hi(q * jnp.exp(b), st, NT) + _dot_bf16(_scores_between_groups(q, kk, b), v, NN)
            bl = b[CHUNK - 1:CHUNK]
            ke = kk * jnp.exp(bl - b)
            st_ref[...] = st * jnp.exp(bl) + _dot_hi(v, ke, TN)
            return carry

        lax.fori_loop(0, n_chunks, chunk, 0, unroll=2)
        o = o_ref[...]
        r = lax.rsqrt(jnp.mean(o * o, axis=-1, keepdims=True) + EPS)
        hg = hg_ref[...]
        ya_ref[...] = (o * r * nw_ref[...] * (hg * _sigmoid(hg))).astype(BF16)

    def col(base):
        return pl.BlockSpec((T, HG_DK), lambda h: (0, base + h))

    return _call(
        body, (z, z, z, z, lb_logits, hg_norm_w), name=name, grid=(HG_HEADS,),
        in_specs=[col(COL_HQ), col(COL_HF), col(COL_HI), col(COL_HG),
                  pl.BlockSpec((2, HG_DK), lambda h: (0, h)), pl.BlockSpec((1, HG_DK), lambda h: (0, 0))],
        out_specs=[col(0), col(0), pl.BlockSpec((None, n_chunks, HG_DK, HG_DK), lambda h: (h, 0, 0, 0))],
        out_shape=[jax.ShapeDtypeStruct((T, HG_WIDTH), F32), jax.ShapeDtypeStruct((T, HG_WIDTH), BF16),
                   jax.ShapeDtypeStruct((HG_HEADS, n_chunks, HG_DK, HG_DK), F32)],
        scratch_shapes=[pltpu.VMEM((HG_DK, HG_DK), F32)],
        sem=("parallel",), after=after)


def _hgrn2_bwd(z, lb_logits, hg_norm_w, o_raw, s_all, dya, name, after=()):
    T = z.shape[0]
    n_chunks = T // CHUNK

    def body(hq_ref, hf_ref, hi_ref, hg_ref, lbl_ref, nw_ref, o_ref, sall_ref, dya_ref,
             dhq_ref, dhf_ref, dhi_ref, dhg_ref, dlbl_ref, dnw_ref,
             do_ref, dst_ref, dlb_ref, *per_chunk):
        h = pl.program_id(0)
        lbl = lbl_ref[...]
        lb = 1.0 / (1.0 + jnp.exp(lbl[1:2, :] - lbl[0:1, :]))

        o = o_ref[...]
        r = lax.rsqrt(jnp.mean(o * o, axis=-1, keepdims=True) + EPS)
        oh = o * r
        nw = nw_ref[...]
        hg = hg_ref[...]
        sg = _sigmoid(hg)
        dy = dya_ref[...]
        d_on = dy * (hg * sg)
        dhg_ref[...] = (dy * (oh * nw) * (sg * (1.0 + hg * (1.0 - sg)))).astype(BF16)
        dnw = jnp.sum(d_on * oh, axis=0, keepdims=True)
        gy = d_on * nw
        do_ref[...] = r * (gy - oh * jnp.mean(gy * oh, axis=-1, keepdims=True))

        @pl.when(h == 0)
        def _():
            dnw_ref[...] = jnp.zeros_like(dnw_ref)

        dnw_ref[...] += jnp.broadcast_to(dnw, dnw_ref.shape)

        dst_ref[...] = jnp.zeros_like(dst_ref)
        dlb_ref[...] = jnp.zeros_like(dlb_ref)
        tri = _tri(True)
        tri_t = _tri(False)
        row8 = lax.broadcasted_iota(jnp.int32, (8, HG_DK), 0)
        row_group = lax.broadcasted_iota(jnp.int32, (CHUNK, CHUNK), 0) // GROUP
        col_group = lax.broadcasted_iota(jnp.int32, (CHUNK, CHUNK), 1) // GROUP
        earlier_group = col_group < row_group
        later_group = col_group > row_group

        def chunk(c, dq_ref, dk_ref, dv_ref):
            rows = pl.ds(pl.multiple_of(c * CHUNK, CHUNK), CHUNK)
            hq = hq_ref[rows, :]
            q, sq, f, g, lg, kk = _hg_gates(hq, hf_ref[rows, :], lb)
            v = hi_ref[rows, :]
            do = do_ref[rows, :]
            b = _dot_hi(tri, lg, NN)
            eb = jnp.exp(b)
            bl = b[CHUNK - 1:CHUNK]
            ebl = jnp.exp(bl)
            ekb = jnp.exp(bl - b)
            qe = q * eb
            ke = kk * ekb
            st = sall_ref[c]
            dst = dst_ref[...]
            dqe = _dot_hi(do, st, NN)
            dke = _dot_hi(v, dst, NN)
            dv_inter = _dot_hi(ke, dst, NT)
            d_ebl = jnp.sum(st * dst, axis=0, keepdims=True)
            dst_ref[...] = dst * ebl + _dot_hi(do, qe, TN)

            dk_ref[...] = jnp.zeros_like(dk_ref)
            dv_ref[...] = jnp.zeros_like(dv_ref)
            for grp in range(N_GROUPS):
                r0 = GROUP * grp
                for h8 in range(GROUP // 8):
                    n = 8 * (h8 + 1)
                    bs, ks, vs = b[r0:r0 + n], kk[r0:r0 + n], v[r0:r0 + n]
                    sidx = lax.broadcasted_iota(jnp.int32, (n, HG_DK), 0)
                    blk = jnp.zeros((8, HG_DK), F32)
                    for i in range(8):
                        t = r0 + 8 * h8 + i
                        qt = q[t:t + 1]
                        dot_ = do[t:t + 1]
                        e = jnp.where(sidx <= 8 * h8 + i, jnp.exp(b[t:t + 1] - bs), 0.0)
                        w = e * ks
                        p = jnp.sum(w * qt, axis=1, keepdims=True)
                        dsc = jnp.sum(vs * dot_, axis=1, keepdims=True)
                        dqt = jnp.sum(dsc * w, axis=0, keepdims=True)
                        blk = blk + jnp.where(row8 == i, dqt, 0.0)
                        dk_ref[r0:r0 + n, :] += dsc * e * qt
                        dv_ref[r0:r0 + n, :] += p * dot_
                    dq_ref[r0 + 8 * h8:r0 + n, :] = blk
            ds_far = jnp.where(earlier_group, _dot_bf16(do, v, NT), 0.0)
            ds_far_t = jnp.where(later_group, _dot_bf16(v, do, NT), 0.0)
            dq_far, dk_far = [jnp.zeros((GROUP, HG_DK), F32)], []
            for grp in range(1, N_GROUPS):
                r0 = GROUP * grp
                _, ks, decay = _by_query_group(q, kk, b, grp)
                dq_far.append(decay * _dot_hi(ds_far[r0:r0 + GROUP], ks, NN))
                qs, _, decay = _by_key_group(q, kk, b, grp - 1)
                dk_far.append(decay * _dot_hi(ds_far_t[r0 - GROUP:r0], qs, NN))
            dk_far.append(jnp.zeros((GROUP, HG_DK), F32))
            dv_far = _dot_bf16(_scores_between_groups(q, kk, b), do, TN)
            dq_i = dq_ref[...] + jnp.concatenate(dq_far, axis=0)
            dk_i = dk_ref[...] + jnp.concatenate(dk_far, axis=0)
            dke_ke = dke * ke
            db = q * dq_i - kk * dk_i + dqe * qe - dke_ke
            db_last = jnp.sum(dke_ke, axis=0, keepdims=True) + d_ebl * ebl
            dlg = _dot_hi(tri_t, db, NN) + db_last
            dq = dq_i + dqe * eb
            dkk = dk_i + dke * ekb
            dg = dlg / g - dkk
            dhq_ref[rows, :] = (dq * (HG_DK ** -0.5) * (sq * (1.0 + hq * (1.0 - sq)))).astype(BF16)
            dhf_ref[rows, :] = (dg * (1.0 - lb) * f * (1.0 - f)).astype(BF16)
            dhi_ref[rows, :] = (dv_ref[...] + dv_far + dv_inter).astype(BF16)
            dlb_ref[...] += jnp.sum(dg * (1.0 - f), axis=0, keepdims=True)

        def trip(i, carry):
            for k in range(BWD_CHUNKS_PER_TRIP):
                chunk(n_chunks - 1 - k - BWD_CHUNKS_PER_TRIP * i, *per_chunk[3 * k:3 * k + 3])
            return carry

        lax.fori_loop(0, n_chunks // BWD_CHUNKS_PER_TRIP, trip, 0)
        dl0 = dlb_ref[...] * lb * (1.0 - lb)
        dlbl_ref[0:1, :] = dl0
        dlbl_ref[1:2, :] = -dl0

    def col(base):
        return pl.BlockSpec((T, HG_DK), lambda h: (0, base + h))

    outb = jax.ShapeDtypeStruct((T, HG_WIDTH), BF16)
    return _call(
        body, (z, z, z, z, lb_logits, hg_norm_w, o_raw, s_all, dya), name=name, grid=(HG_HEADS,),
        in_specs=[col(COL_HQ), col(COL_HF), col(COL_HI), col(COL_HG),
                  pl.BlockSpec((2, HG_DK), lambda h: (0, h)), pl.BlockSpec((1, HG_DK), lambda h: (0, 0)),
                  col(0), pl.BlockSpec((None, n_chunks, HG_DK, HG_DK), lambda h: (h, 0, 0, 0)), col(0)],
        out_specs=[col(0), col(0), col(0), col(0), pl.BlockSpec((2, HG_DK), lambda h: (0, h)),
                   pl.BlockSpec((8, HG_DK), lambda h: (0, 0))],
        out_shape=[outb, outb, outb, outb, jax.ShapeDtypeStruct((2, HG_WIDTH), F32),
                   jax.ShapeDtypeStruct((8, HG_DK), F32)],
        scratch_shapes=[pltpu.VMEM((T, HG_DK), F32), pltpu.VMEM((HG_DK, HG_DK), F32), pltpu.VMEM((1, HG_DK), F32)]
        + [pltpu.VMEM((CHUNK, HG_DK), F32)] * (3 * BWD_CHUNKS_PER_TRIP),
        sem=("arbitrary",), after=after)


CONST_KEYS = PAD - REL_CLIP
VAR_KEYS = BAND - CONST_KEYS
REL_LO = 128
REL_SPAN = N_REL_PAD - REL_LO


def _rel_onehot(t):
    r = lax.broadcasted_iota(jnp.int32, (REL_SPAN, VAR_KEYS), 0)
    j = lax.broadcasted_iota(jnp.int32, (REL_SPAN, VAR_KEYS), 1)
    idx = jnp.clip(t + PAD - CONST_KEYS - j, -REL_CLIP, REL_CLIP) + REL_CLIP - REL_LO
    return jnp.where(r == idx, 1.0, 0.0).astype(BF16)


def _split3(x):
    hi = x.astype(BF16)
    r1 = x - hi.astype(F32)
    mid = r1.astype(BF16)
    return hi, mid, (r1 - mid.astype(F32)).astype(BF16)


def _bias_expand(rel, name):
    def body(rel_ref, out_ref):
        tab = rel_ref[...]
        onehot = _rel_onehot(pl.program_id(0))
        out_ref[:, 0:CONST_KEYS] = jnp.broadcast_to(tab[:, 2 * REL_CLIP:2 * REL_CLIP + 1], (AT_HEADS, CONST_KEYS))
        out_ref[:, CONST_KEYS:BAND] = sum(
            jnp.dot(piece, onehot, preferred_element_type=F32) for piece in _split3(tab[:, REL_LO:N_REL_PAD]))

    return pl.pallas_call(
        body, name=name, grid=(CHUNK,),
        in_specs=[pl.BlockSpec((AT_HEADS, N_REL_PAD), lambda t: (0, 0))],
        out_specs=pl.BlockSpec((None, AT_HEADS, BAND), lambda t: (t, 0, 0)),
        out_shape=jax.ShapeDtypeStruct((CHUNK, AT_HEADS, BAND), F32),
        compiler_params=_cparams(("parallel",)),
    )(rel)


def _bias_reduce(dbias_rows, name, after=()):
    def body(db_ref, out_ref):
        lane = lax.broadcasted_iota(jnp.int32, (AT_HEADS, N_REL_PAD), 1)
        varying = lane >= CONST_KEYS
        by_offset = jnp.zeros((AT_HEADS, N_REL_PAD), F32)
        constant = jnp.zeros((AT_HEADS, N_REL_PAD), F32)
        for t in range(CHUNK):
            row = db_ref[t]
            constant = constant + jnp.where(varying, 0.0, row)
            moved = jnp.where(varying, row, 0.0)
            by_offset = by_offset + (pltpu.roll(moved, N_REL_PAD - t, axis=1) if t else moved)
        offset = lax.broadcasted_iota(jnp.int32, (N_REL_PAD, N_REL_PAD), 0)
        entry = lax.broadcasted_iota(jnp.int32, (N_REL_PAD, N_REL_PAD), 1)
        onehot = jnp.where(entry == jnp.clip(PAD - offset, -REL_CLIP, REL_CLIP) + REL_CLIP, 1.0, 0.0).astype(BF16)
        acc = sum(jnp.dot(piece, onehot, preferred_element_type=F32) for piece in _split3(by_offset))
        last = jnp.sum(constant, axis=1, keepdims=True)
        out_ref[...] = acc + jnp.where(lane == 2 * REL_CLIP, last, 0.0)

    whole = pl.BlockSpec((CHUNK, AT_HEADS, N_REL_PAD), lambda i: (0, 0, 0))
    return _call(
        body, (dbias_rows,), name=name, grid=(1,), in_specs=[whole],
        out_specs=[pl.BlockSpec((AT_HEADS, N_REL_PAD), lambda i: (0, 0))],
        out_shape=[jax.ShapeDtypeStruct((AT_HEADS, N_REL_PAD), F32)],
        sem=("arbitrary",), after=after)[0]


def _pair_lanes():
    return lax.broadcasted_iota(jnp.int32, (CHUNK, 2 * AT_DH), 1) < AT_DH


def _block_diag(a):
    first = _pair_lanes()
    return jnp.concatenate([jnp.where(first, a, 0.0), jnp.where(first, 0.0, a)], axis=0).astype(BF16)


def _diag_blocks(a):
    return jnp.where(_pair_lanes(), a[:CHUNK], a[CHUNK:])


def _band_probs_t(kb, qbd, bias_t, c):
    s = lax.dot_general(kb, qbd, (NT, ((), ())), preferred_element_type=F32) * (AT_DH ** -0.5) + bias_t
    j = lax.broadcasted_iota(jnp.int32, (BAND, 2 * AT_DH), 0)
    s = jnp.where(j + c * CHUNK >= PAD, s, -jnp.inf)
    p = jnp.exp(s - jnp.max(s, axis=0, keepdims=True))
    return p / jnp.sum(p, axis=0, keepdims=True)


def _attn_fwd(z, bias_t, name, after=()):
    T = z.shape[0]
    n_chunks = T // CHUNK

    def body(q_ref, k_ref, v_ref, bias_ref, y_ref, p_ref, *scratch):
        for pr in range(2):
            lanes = slice(128 * pr, 128 * (pr + 1))
            for dst_ref, src_ref in zip(scratch[2 * pr:2 * pr + 2], (k_ref, v_ref)):
                dst_ref[0:PAD, :] = jnp.zeros((PAD, 128), BF16)
                dst_ref[PAD:PAD + T, :] = src_ref[:, lanes].astype(BF16)

        def chunk(c, carry):
            rows = pl.ds(pl.multiple_of(c * CHUNK, CHUNK), CHUNK)
            band = pl.ds(pl.multiple_of(c * CHUNK, CHUNK), BAND)
            for pr in range(2):
                kp_ref, vp_ref = scratch[2 * pr:2 * pr + 2]
                lanes = slice(128 * pr, 128 * (pr + 1))
                p = _band_probs_t(kp_ref[band, :], _block_diag(q_ref[rows, lanes]), bias_ref[pr], c).astype(BF16)
                p_ref[pr, c] = p
                o2 = lax.dot_general(p, vp_ref[band, :], (TN, ((), ())), preferred_element_type=F32)
                y_ref[rows, lanes] = _diag_blocks(o2).astype(BF16)
            return carry

        lax.fori_loop(0, n_chunks, chunk, 0, unroll=2)

    def col(base):
        return pl.BlockSpec((T, 256), lambda h: (0, base // 2 + h))

    return _call(
        body, (z, z, z, bias_t), name=name, grid=(AT_HEADS // 4,),
        in_specs=[col(COL_AQ), col(COL_AK), col(COL_AV), pl.BlockSpec((2, BAND, 128), lambda h: (h, 0, 0))],
        out_specs=[col(0), pl.BlockSpec((2, n_chunks, BAND, 128), lambda h: (h, 0, 0, 0))],
        out_shape=[jax.ShapeDtypeStruct((T, AT_WIDTH), BF16),
                   jax.ShapeDtypeStruct((AT_HEADS // 2, n_chunks, BAND, 128), BF16)],
        scratch_shapes=[pltpu.VMEM((PAD + T, 128), BF16)] * 4,
        sem=("parallel",), after=after)


def _attn_bwd(z, probs, dyb, name, after=()):
    T = z.shape[0]
    n_chunks = T // CHUNK

    def body(q_ref, k_ref, v_ref, p_ref, dy_ref, dq_ref, dk_ref, dv_ref, dbias_ref, *scratch):
        dbias_ref[...] = jnp.zeros_like(dbias_ref)
        for pr in range(2):
            kp_ref, vp_ref, dkp_ref, dvp_ref = scratch[4 * pr:4 * pr + 4]
            lanes = slice(128 * pr, 128 * (pr + 1))
            kp_ref[0:PAD, :] = jnp.zeros((PAD, 128), BF16)
            vp_ref[0:PAD, :] = jnp.zeros((PAD, 128), BF16)
            kp_ref[PAD:PAD + T, :] = k_ref[:, lanes].astype(BF16)
            vp_ref[PAD:PAD + T, :] = v_ref[:, lanes].astype(BF16)
            dkp_ref[...] = jnp.zeros_like(dkp_ref)
            dvp_ref[...] = jnp.zeros_like(dvp_ref)

        def chunk(c, carry):
            rows = pl.ds(pl.multiple_of(c * CHUNK, CHUNK), CHUNK)
            band = pl.ds(pl.multiple_of(c * CHUNK, CHUNK), BAND)
            for pr in range(2):
                kp_ref, vp_ref, dkp_ref, dvp_ref = scratch[4 * pr:4 * pr + 4]
                lanes = slice(128 * pr, 128 * (pr + 1))
                qbd = _block_diag(q_ref[rows, lanes])
                dobd = _block_diag(dy_ref[rows, lanes])
                pb = p_ref[pr, c]
                p = pb.astype(F32)
                dp = lax.dot_general(vp_ref[band, :], dobd, (NT, ((), ())), preferred_element_type=F32)
                ds = p * (dp - jnp.sum(dp * p, axis=0, keepdims=True))
                dbias_ref[pr] += ds
                dsb = ds.astype(BF16)
                dq2 = lax.dot_general(dsb, kp_ref[band, :], (TN, ((), ())), preferred_element_type=F32)
                dq_ref[rows, lanes] = (_diag_blocks(dq2) * (AT_DH ** -0.5)).astype(BF16)
                dkp_ref[band, :] += jnp.dot(dsb, qbd, preferred_element_type=F32) * (AT_DH ** -0.5)
                dvp_ref[band, :] += jnp.dot(pb, dobd, preferred_element_type=F32)
            return carry

        lax.fori_loop(0, n_chunks, chunk, 0)
        for pr in range(2):
            lanes = slice(128 * pr, 128 * (pr + 1))
            dk_ref[:, lanes] = scratch[4 * pr + 2][PAD:PAD + T, :].astype(BF16)
            dv_ref[:, lanes] = scratch[4 * pr + 3][PAD:PAD + T, :].astype(BF16)

    def col(base):
        return pl.BlockSpec((T, 256), lambda h: (0, base // 2 + h))

    outb = jax.ShapeDtypeStruct((T, AT_WIDTH), BF16)
    return _call(
        body, (z, z, z, probs, dyb), name=name, grid=(AT_HEADS // 4,),
        in_specs=[col(COL_AQ), col(COL_AK), col(COL_AV),
                  pl.BlockSpec((2, n_chunks, BAND, 128), lambda h: (h, 0, 0, 0)), col(0)],
        out_specs=[col(0), col(0), col(0), pl.BlockSpec((2, BAND, 128), lambda h: (h, 0, 0))],
        out_shape=[outb, outb, outb, jax.ShapeDtypeStruct((AT_HEADS // 2, BAND, 128), F32)],
        scratch_shapes=([pltpu.VMEM((PAD + T, 128), BF16)] * 2 + [pltpu.VMEM((PAD + T, 128), F32)] * 2) * 2,
        sem=("parallel",), after=after)


def _local_step(x, target, lb_logits, hg_norm_w, rel_bias, norm_mix_w, norm_mlp_w, norm_final_w,
                w_in, rest, exchanges=None):
    ex = exchanges
    rel = jnp.pad(rel_bias, ((0, 0), (0, N_REL_PAD - N_REL)))

    u = _rms_fwd(x, norm_mix_w, "rms_mix_fwd")
    if ex:
        z, w_in = _mm_gathered(u, w_in, ex.order, "mm_in_fwd")
        gather = _Gather(rest, [w_in], "ag")
        z = _mm_gathered_tail(u, w_in, z, ex.order, "mm_in_fwd_tail", after=[gather.token])
        tok = []
    else:
        z = _mm_nn(u, w_in, F32, "mm_in_fwd")
        w_a, w_b, w_out, w_up, w_down = rest
        tok = []
    o_raw, y_a, s_all = _hgrn2_fwd(z, lb_logits, hg_norm_w, "hgrn2_fwd", after=tok)
    if ex:
        tok = [gather.pass_on([0, 1, 2], [o_raw], "abo")]
    bias_rows = _bias_expand(rel, "bias_expand")
    bias_t = jnp.transpose(bias_rows.reshape(CHUNK, AT_HEADS // 2, 2, BAND), (1, 3, 2, 0)).reshape(
        AT_HEADS // 2, BAND, 2 * CHUNK)
    y_b, probs = _attn_fwd(z, bias_t, "attn_fwd", after=tok)
    if ex:
        tok = [gather.pass_on([3], [y_b], "up")]
        w_a, w_b, w_out = gather.finish([0, 1, 2], tok, "abo")
    pa = _mm_nn(y_a, w_a, F32, "mm_a_fwd")
    pb, merged = _mm_nn(y_b, w_b, None, "mm_b_fwd", epilogue=(
        (z, z, pa), (COL_GATE_A * GATE_TILE, COL_GATE_B * GATE_TILE, 0), (F32, BF16), _gated_merge))
    w_out1 = w_out.reshape(1, D_MODEL, D_MODEL)
    h1, u2 = _mm_rows(merged, w_out.reshape(D_MODEL, D_MODEL), [x], [norm_mlp_w], (F32, BF16),
                      _residual_rms_rows, "mm_out_fwd")
    if ex:
        tok = [gather.pass_on([4], [u2], "down")]
        w_up, = gather.finish([3], tok, "up")
    a, r = _mm_nn(u2, w_up, None, "mm_up_fwd", epilogue=((), (), (F32, BF16), _squared_relu))
    if ex:
        w_down, = gather.finish([4], [r], "down")
    w_down1 = w_down.reshape(1, D_FF, D_MODEL)
    mlp = _mm_nn(r, w_down1, F32, "mm_down_fwd")
    loss, dh2, dh2b, g_nf = _loss_head(h1, mlp, norm_final_w, target, "loss_head")

    def reduce_scatter(grads, name):
        rs = _ReduceScatter(grads, ex.parity, name) if ex else None
        return rs, ([rs.token] if ex else [])

    g_down = _mm_tn(r, dh2b, 1, BF16, "mm_down_wgrad").reshape(N_DEV, D_FF // N_DEV, D_MODEL)
    rs_down, tok = reduce_scatter([g_down], "rs_down")
    da, = _mm_nt(dh2b, w_down1, None, "mm_down_dgrad", after=tok, epilogue=(
        (a,), (0,), (BF16,), lambda dr, av: (dr * (2.0 * jnp.maximum(av, 0.0)),)))
    tok = [rs_down.pair_sums([da])] if ex else []
    g_up = _mm_tn(u2, da, N_DEV, BF16, "mm_up_wgrad", after=tok)
    rs_up, tok = reduce_scatter([g_up], "rs_up")
    du2 = _mm_nt(da, w_up, F32, "mm_up_dgrad", after=tok)
    tok = [rs_up.pair_sums([du2])] if ex else []
    dh1, dh1b, g_nmlp = _rms_bwd(du2, h1, norm_mlp_w, dh2, "rms_mlp_bwd", after=tok)

    g_out = _mm_tn(merged, dh1b, 1, BF16, "mm_out_wgrad").reshape(N_DEV, D_MODEL // N_DEV, D_MODEL)
    dpa, dpb, dga, dgb = _mm_nt(dh1b, w_out1, None, "mm_out_dgrad", epilogue=(
        (z, z, pa, pb), (COL_GATE_A * GATE_TILE, COL_GATE_B * GATE_TILE, 0, 0), (BF16,) * 4, _merge_grads))
    g_a = _mm_tn(y_a, dpa, N_DEV, BF16, "mm_a_wgrad")
    g_b = _mm_tn(y_b, dpb, N_DEV, BF16, "mm_b_wgrad")
    rs_mix, tok = reduce_scatter([g_a, g_b, g_out], "rs_mix")
    dya = _mm_nt(dpa, w_a, F32, "mm_a_dgrad", after=tok)
    dyb = _mm_nt(dpb, w_b, F32, "mm_b_dgrad", after=tok)
    tok = [rs_mix.pair_sums([dya, dyb])] if ex else []
    daq, dak, dav, dbias_t = _attn_bwd(z, probs, dyb, "attn_bwd", after=tok)
    dhq, dhf, dhi, dhg, g_lbl, g_hgw = _hgrn2_bwd(z, lb_logits, hg_norm_w, o_raw, s_all, dya, "hgrn2_bwd",
                                                  after=tok)
    dbias_rows = jnp.pad(jnp.transpose(dbias_t.reshape(AT_HEADS // 2, BAND, 2, CHUNK), (3, 0, 2, 1)).reshape(
        CHUNK, AT_HEADS, BAND), ((0, 0), (0, 0), (0, N_REL_PAD - BAND)))
    dz =jnp.concatenate([dhq, dhf, dhi, dhg, daq, dak, dav, dga, dgb], axis=1)
    half = D_MODEL // 2
    g_in_lo = _mm_tn(u[:, :half], dz, N_DEV, BF16, "mm_in_wgrad_lo")
    rs_in_lo, tok = reduce_scatter([g_in_lo], "rs_in_lo")
    g_in_hi = _mm_tn(u[:, half:], dz, N_DEV, BF16, "mm_in_wgrad_hi", after=tok)
    tok = [rs_in_lo.pair_sums([g_in_hi])] if ex else []
    rs_in_hi, tok_hi = reduce_scatter([g_in_hi], "rs_in_hi")
    du = _mm_nt(dz, w_in, F32, "mm_in_dgrad", after=tok + tok_hi)
    tok = [rs_in_hi.pair_sums([du])] if ex else []
    grad_x, _, g_nmix = _rms_bwd(du, x, norm_mix_w, dh1, "rms_mix_bwd", after=tok)
    g_rel = _bias_reduce(dbias_rows, "bias_reduce", after=tok)[:, :N_REL]

    small = dict(lb_logits=g_lbl, hg_norm_w=g_hgw[0:1], rel_bias=g_rel, norm_mix_w=g_nmix, norm_mlp_w=g_nmlp,
                 norm_final_w=g_nf)
    if ex:
        grads = [(rs_in_lo, rs_in_hi), rs_mix, rs_up, rs_down]
    else:
        grads = [jnp.concatenate([g_in_lo, g_in_hi], axis=1), g_a, g_b, g_out, g_up, g_down]
    return loss, grad_x, grads, small


def _mm_gathered(u, shard, order, name):
    T, K = u.shape
    _, Nb = shard.shape

    def body(order_ref, u_ref, shard_ref, z_ref, full_ref, wbuf, load_sem, send_sems, recv_sems, local_sem):
        s = pl.program_id(0)
        x, y, c = _position()
        me, sibling = (x, y, c), (x, y, 1 - c)
        chips = [(1 - x, y), (x, 1 - y), (1 - x, 1 - y)]

        def copy(k, block, to, src=None):
            dst = full_ref.at[4 * block[0] + 2 * block[1] + block[2]]
            return pltpu.make_async_remote_copy(
                src_ref=dst if src is None else src, dst_ref=dst,
                send_sem=send_sems.at[k], recv_sem=recv_sems.at[k], device_id=to, device_id_type=MESH)

        @pl.when(s == 0)
        def _():
            local = pltpu.make_async_copy(shard_ref, full_ref.at[4 * x + 2 * y + c], local_sem)
            local.start()
            copy(0, me, sibling, src=shard_ref).start()
            for j, chip in enumerate(chips):
                copy(1 + j, me, (*chip, c), src=shard_ref).start()
            local.wait()

        @pl.when(s == 1)
        def _():
            copy(0, sibling, me).wait_recv()

        for j, chip in enumerate(chips):
            direct, passed = ((2, 4), (3, 5), (6, 7))[j]

            @pl.when(s == direct)
            def _(j=j, chip=chip):
                copy(1 + j, (*chip, c), me).wait_recv()
                copy(4 + j, (*chip, c), sibling).start()

            @pl.when(s == passed)
            def _(j=j, chip=chip):
                copy(4 + j, (*chip, 1 - c), me).wait_recv()

        @pl.when(s < N_EARLY_BLOCKS)
        def _():
            load = pltpu.make_async_copy(full_ref.at[order_ref[s]], wbuf, load_sem)
            load.start()
            load.wait()
            z_ref[...] = jnp.dot(u_ref[...], wbuf[...], preferred_element_type=F32)

        @pl.when(s == N_DEV - 1)
        def _():
            for k in range(7):
                copy(k, me, sibling).wait_send()

    z, full = pl.pallas_call(
        body, name=name,
        grid_spec=pltpu.PrefetchScalarGridSpec(
            num_scalar_prefetch=1, grid=(N_DEV,),
            in_specs=[pl.BlockSpec((T, K), lambda s, order: (0, 0)), ANY],
            out_specs=[pl.BlockSpec((T, Nb), lambda s, order: (0, order[jnp.minimum(s, N_EARLY_BLOCKS - 1)])), ANY],
            scratch_shapes=[pltpu.VMEM((K, Nb), BF16), pltpu.SemaphoreType.DMA,
                            pltpu.SemaphoreType.DMA((7,)), pltpu.SemaphoreType.DMA((7,)), pltpu.SemaphoreType.DMA]),
        out_shape=[jax.ShapeDtypeStruct((T, N_DEV * Nb), F32), jax.ShapeDtypeStruct((N_DEV, K, Nb), BF16)],
        compiler_params=_cparams(("arbitrary",)),
    )(order, u, shard)
    return z, full


N_EARLY_BLOCKS = 6


def _mm_gathered_tail(u, full, z, order, name, after=()):
    T, K = u.shape
    _, _, Nb = full.shape
    n_after = len(after)

    def body(order_ref, u_ref, w_ref, z_in_ref, *rest):
        rest[n_after][...] = jnp.dot(u_ref[...], w_ref[...], preferred_element_type=F32)

    return pl.pallas_call(
        body, name=name,
        grid_spec=pltpu.PrefetchScalarGridSpec(
            num_scalar_prefetch=1, grid=(N_DEV - N_EARLY_BLOCKS,),
            in_specs=[pl.BlockSpec((T, K), lambda s, order: (0, 0)),
                      pl.BlockSpec((None, K, Nb), lambda s, order: (order[N_EARLY_BLOCKS + s], 0, 0)), ANY]
            + [ANY] * n_after,
            out_specs=pl.BlockSpec((T, Nb), lambda s, order: (0, order[N_EARLY_BLOCKS + s]))),
        out_shape=jax.ShapeDtypeStruct(z.shape, z.dtype),
        input_output_aliases={3: 0},
        compiler_params=_cparams(("arbitrary",)),
    )(order, u, full, z, *after)


def _gather_order():
    x, y, c = _position()
    chips = [(1 - x, y), (x, 1 - y), (1 - x, 1 - y)]
    ids = [4 * x + 2 * y + c, 4 * x + 2 * y + (1 - c)]
    ids += [4 * cx + 2 * cy + c for cx, cy in chips[:2]] + [4 * cx + 2 * cy + (1 - c) for cx, cy in chips[:2]]
    ids += [4 * chips[2][0] + 2 * chips[2][1] + c, 4 * chips[2][0] + 2 * chips[2][1] + (1 - c)]
    return jnp.stack(ids).astype(jnp.int32)


def _pair_sum(g, land, parity, name):
    _, R, C = g.shape
    tr = _pick(R, (512, 256))

    def body(par_ref, g_ref, l_ref, o_ref):
        o_ref[...] = (g_ref[...].astype(F32) + l_ref[...].astype(F32)).astype(BF16)

    return pl.pallas_call(
        body, name=name,
        grid_spec=pltpu.PrefetchScalarGridSpec(
            num_scalar_prefetch=1, grid=(N_CHIP, R // tr),
            in_specs=[pl.BlockSpec((None, tr, C), lambda s, i, par: (2 * s + par[0], i, 0)),
                      pl.BlockSpec((None, tr, C), lambda s, i, par: (s, i, 0))],
            out_specs=pl.BlockSpec((None, tr, C), lambda s, i, par: (s, i, 0))),
        out_shape=jax.ShapeDtypeStruct((N_CHIP, R, C), BF16),
        compiler_params=_cparams(("parallel", "parallel")),
    )(parity, g, land)


HBM = pl.BlockSpec(memory_space=pltpu.HBM)
SEM = pl.BlockSpec(memory_space=pltpu.SEMAPHORE)
DATAFLOW = pltpu.SideEffectType.DATAFLOW_SIDE_EFFECTING


def _split_call(name, bufs, waits=(), starts=None, after=()):
    nb = len(bufs)
    n_new = starts[1] if starts else 0
    wait_sems = [s for w in waits for s in (*w[1], *w[2])]

    def body(*refs):
        b, pos = refs[:nb], nb
        for plan, ss, _, send_idx, recv_idx in waits:
            k = len(ss)
            copies = plan(b, refs[pos:pos + k], refs[pos + k:pos + 2 * k])
            pos += 2 * k
            for i in recv_idx:
                copies[i].wait_recv()
            for i in send_idx:
                copies[i].wait_send()
        outs = refs[pos + len(after):]
        if starts:
            for cp in starts[0](b, outs[nb:nb + n_new], outs[nb + n_new:nb + 2 * n_new]):
                cp.start()
        outs[-1][...] = jnp.zeros_like(outs[-1])

    res = pl.pallas_call(
        body, name=name,
        out_shape=tuple(pltpu.HBM(a.shape, a.dtype) for a in bufs) + (pltpu.SemaphoreType.DMA(()),) * (2 * n_new)
        + (jax.ShapeDtypeStruct((8, 128), F32),),
        in_specs=[HBM] * nb + [SEM] * len(wait_sems) + [ANY] * len(after),
        out_specs=(HBM,) * nb + (SEM,) * (2 * n_new) + (pl.BlockSpec(memory_space=pltpu.VMEM),),
        input_output_aliases={i: i for i in range(nb)},
        compiler_params=pltpu.CompilerParams(has_side_effects=DATAFLOW),
    )(*bufs, *wait_sems, *after)
    return list(res[:nb]), list(res[nb:nb + n_new]), list(res[nb + n_new:nb + 2 * n_new]), res[-1]


def _in_hbm(a):
    return pltpu.with_memory_space_constraint(a, pltpu.HBM)


def _remote(src, dst, send_sem, recv_sem, to):
    return pltpu.make_async_remote_copy(src_ref=src, dst_ref=dst, send_sem=send_sem, recv_sem=recv_sem,
                                        device_id=to, device_id_type=MESH)


def _other_chips():
    x, y, _ = _position()
    return [(1 - x, y), (x, 1 - y), (1 - x, 1 - y)]


def _plan_gather_first(n):
    def plan(b, ss, rs):
        x, y, c = _position()
        to = [(x, y, 1 - c)] + [(*chip, c) for chip in _other_chips()]
        return [_remote(b[w], b[n + w].at[4 * x + 2 * y + c], ss[4 * w + k], rs[4 * w + k], to[k])
                for w in range(n) for k in range(4)]
    return plan, 4 * n


def _plan_gather_pass(n):
    def plan(b, ss, rs):
        x, y, c = _position()
        copies = []
        for w in range(n):
            for j, chip in enumerate(_other_chips()):
                blk = b[n + w].at[4 * chip[0] + 2 * chip[1] + c]
                copies.append(_remote(blk, blk, ss[3 * w + j], rs[3 * w + j], (x, y, 1 - c)))
        return copies
    return plan, 3 * n


def _plan_sibling(n):
    def plan(b, ss, rs):
        x, y, c = _position()
        return [_remote(b[w].at[2 * s + (1 - c)], b[n + w].at[s], ss[4 * w + s], rs[4 * w + s], (x, y, 1 - c))
                for w in range(n) for s in range(N_CHIP)]
    return plan, 4 * n


def _plan_scatter(n):
    def plan(b, ss, rs):
        x, y, c = _position()
        return [_remote(b[w].at[2 * chip[0] + chip[1]], b[n + w].at[2 * x + y], ss[3 * w + j], rs[3 * w + j],
                        (*chip, c))
                for w in range(n) for j, chip in enumerate(_other_chips())]
    return plan, 3 * n


class _Gather:
    def __init__(self, shards, after, name):
        self.n, self.name = len(shards), name
        x, y, c = _position()
        placed = [lax.dynamic_update_index_in_dim(lax.empty((N_DEV,) + s.shape, s.dtype), s, 4 * x + 2 * y + c, 0)
                  for s in shards]
        bufs, self.ss, self.rs, self.token = _split_call(
            name + "_start", [_in_hbm(a) for a in list(shards) + placed], starts=_plan_gather_first(self.n),
            after=after)
        self.shards, self.fulls = bufs[:self.n], bufs[self.n:]
        self.passed = {}

    def _sub(self, ids, sems, per):
        return [sems[per * w + k] for w in ids for k in range(per)]

    def pass_on(self, ids, after, tag):
        m = len(ids)
        first = (_plan_gather_first(m)[0], self._sub(ids, self.ss, 4), self._sub(ids, self.rs, 4),
                 [], [4 * i + k for i in range(m) for k in (1, 2, 3)])
        bufs, ss, rs, token = _split_call(
            "%s_pass_%s" % (self.name, tag), [self.shards[w] for w in ids] + [self.fulls[w] for w in ids],
            waits=[first], starts=_plan_gather_pass(m), after=after)
        for i, w in enumerate(ids):
            self.shards[w], self.fulls[w] = bufs[i], bufs[m + i]
        self.passed[tuple(ids)] = (ss, rs)
        return token

    def finish(self, ids, after, tag):
        m = len(ids)
        ss2, rs2 = self.passed[tuple(ids)]
        first = (_plan_gather_first(m)[0], self._sub(ids, self.ss, 4), self._sub(ids, self.rs, 4),
                 list(range(4 * m)), [4 * i for i in range(m)])
        passed = (_plan_gather_pass(m)[0], ss2, rs2, list(range(3 * m)), list(range(3 * m)))
        bufs, _, _, _ = _split_call(
            "%s_finish_%s" % (self.name, tag), [self.shards[w] for w in ids] + [self.fulls[w] for w in ids],
            waits=[first, passed], after=after)
        return bufs[m:]


class _ReduceScatter:
    def __init__(self, grads, parity, name):
        self.n, self.name, self.parity = len(grads), name, parity
        lands = [lax.empty((N_CHIP,) + g.shape[1:], g.dtype) for g in grads]
        self.bufs, self.ss, self.rs, self.token = _split_call(
            name + "_sibling_start", [_in_hbm(a) for a in list(grads) + lands], starts=_plan_sibling(self.n))

    def pair_sums(self, after):
        n = self.n
        bufs, _, _, _ = _split_call(
            self.name + "_sibling_wait", self.bufs,
            waits=[(_plan_sibling(n)[0], self.ss, self.rs, list(range(4 * n)), list(range(4 * n)))], after=after)
        sums = [_pair_sum(bufs[w], bufs[n + w], self.parity, "%s_pair_sum_%d" % (self.name, w)) for w in range(n)]
        lands = [lax.empty(s.shape, s.dtype) for s in sums]
        self.bufs, self.ss, self.rs, token = _split_call(
            self.name + "_scatter_start", [_in_hbm(a) for a in sums + lands], starts=_plan_scatter(n))
        return token

    def finish(self, after):
        n = self.n
        bufs, _, _, _ = _split_call(
            self.name + "_scatter_wait", self.bufs,
            waits=[(_plan_scatter(n)[0], self.ss, self.rs, list(range(3 * n)), list(range(3 * n)))], after=after)
        return bufs[:n], bufs[n:]


class _Exchanges:
    def __init__(self, parity, order):
        self.parity, self.order = parity, order


def _gather_small(packed, name):
    R = packed.shape[0]

    def body(x_ref, out_ref, send_sems, recv_sems):
        x, y, c = _position()
        me = 4 * x + 2 * y + c
        out_ref[me] = x_ref[...]
        copies = []
        for k in range(1, N_DEV):
            to = (x ^ ((k >> 2) & 1), y ^ ((k >> 1) & 1), c ^ (k & 1))
            cp = pltpu.make_async_remote_copy(
                src_ref=x_ref, dst_ref=out_ref.at[me],
                send_sem=send_sems.at[k], recv_sem=recv_sems.at[k], device_id=to, device_id_type=MESH)
            cp.start()
            copies.append((k, to, cp))
        for k, to, cp in copies:
            cp.wait_send()
            pltpu.make_async_remote_copy(
                src_ref=x_ref, dst_ref=out_ref.at[4 * to[0] + 2 * to[1] + to[2]],
                send_sem=send_sems.at[k], recv_sem=recv_sems.at[k], device_id=to, device_id_type=MESH).wait_recv()

    return pl.pallas_call(
        body, name=name,
        in_specs=[pl.BlockSpec(memory_space=pltpu.VMEM)], out_specs=pl.BlockSpec(memory_space=pltpu.VMEM),
        out_shape=jax.ShapeDtypeStruct((N_DEV, R, 128), F32),
        scratch_shapes=[pltpu.SemaphoreType.DMA((N_DEV,)), pltpu.SemaphoreType.DMA((N_DEV,))],
    )(packed)


def _adamw_math(w, g, m, v):
    m = ADAM_B1 * m + (1.0 - ADAM_B1) * g
    v = ADAM_B2 * v + (1.0 - ADAM_B2) * (g * g)
    m_hat = m / (1.0 - ADAM_B1 ** ADAM_STEP)
    v_hat = v / (1.0 - ADAM_B2 ** ADAM_STEP)
    delta = -ADAM_LR * (m_hat / (jnp.sqrt(v_hat) + ADAM_EPS) + ADAM_WD * w)
    return delta, m, v


def _adamw_big_landed(w, m, v, parts, lands, slot, name, row0=0, into=None):
    R, C = w.shape
    rows = parts.shape[1]
    tr = _pick(rows, (256,))
    first = row0 // tr
    n_into = len(into) if into else 0

    def body(slot_ref, w_ref, m_ref, v_ref, own_ref, l1_ref, l2_ref, l3_ref, *rest):
        g = own_ref[...].astype(F32)
        for ref in (l1_ref, l2_ref, l3_ref):
            g = g + ref[...].astype(F32)
        for o_ref, res in zip(rest[n_into:], (g,) + _adamw_math(w_ref[...], g, m_ref[...], v_ref[...])):
            o_ref[...] = res

    blk = pl.BlockSpec((tr, C), lambda i, slot: (first + i, 0))

    def chip(k):
        return pl.BlockSpec((None, tr, C), lambda i, slot: ((slot[0] + k) % N_CHIP, i, 0))

    out = jax.ShapeDtypeStruct((R, C), F32)
    return pl.pallas_call(
        body, name=name,
        grid_spec=pltpu.PrefetchScalarGridSpec(
            num_scalar_prefetch=1, grid=(rows // tr,),
            in_specs=[blk, blk, blk, chip(0), chip(1), chip(2), chip(3)] + [ANY] * n_into,
            out_specs=[blk, blk, blk, blk]),
        out_shape=[out, out, out, out],
        input_output_aliases={8 + j: j for j in range(n_into)},
        compiler_params=_cparams(("parallel",)),
    )(slot, w, m, v, parts, lands, lands, lands, *(into or ()))


def _adamw_small(w, m, v, gathered, name):
    R = w.shape[0]

    def body(w_ref, m_ref, v_ref, p_ref, g_ref, d_ref, nm_ref, nv_ref):
        g = p_ref[0]
        for s in range(1, N_DEV):
            g = g + p_ref[s]
        d, nm, nv = _adamw_math(w_ref[...], g, m_ref[...], v_ref[...])
        g_ref[...] = g
        d_ref[...] = d
        nm_ref[...] = nm
        nv_ref[...] = nv

    out = jax.ShapeDtypeStruct((R, 128), F32)
    return pl.pallas_call(
        body, name=name, out_shape=[out, out, out, out],
    )(w, m, v, gathered)


SMALL_NAMES = ("lb_logits", "hg_norm_w", "rel_bias", "norm_mix_w", "norm_mlp_w", "norm_final_w")
SMALL_SHAPES = {"lb_logits": (2, HG_WIDTH), "hg_norm_w": (1, HG_DK), "rel_bias": (AT_HEADS, N_REL_PAD),
                "norm_mix_w": (1, D_MODEL), "norm_mlp_w": (1, D_MODEL), "norm_final_w": (1, D_MODEL)}


def _pack_small(parts):
    rows = []
    for nme in SMALL_NAMES:
        p = parts[nme]
        if nme == "rel_bias":
            p = jnp.pad(p, ((0, 0), (0, N_REL_PAD - N_REL)))
        rows.append(p.reshape(-1, 128))
    flat = jnp.concatenate(rows, axis=0)
    return jnp.pad(flat, ((0, SMALL_ROWS - flat.shape[0]), (0, 0)))


def _unpack_small(packed):
    out, at = {}, 0
    for nme in SMALL_NAMES:
        shp = SMALL_SHAPES[nme]
        nrow = shp[0] * shp[1] // 128
        p = packed[at:at + nrow].reshape(shp)
        at += nrow
        out[nme] = p[:, :N_REL] if nme == "rel_bias" else p
    return out


BIG_NAMES = ("w_in", "w_branch_a", "w_branch_b", "w_out", "w_up", "w_down")


def kernel(x, w_in, lb_logits, hg_norm_w, rel_bias, w_branch_a, w_branch_b, w_out, norm_mix_w, norm_mlp_w, w_up, w_down, norm_final_w, loss_target, m_w_in, m_lb_logits, m_hg_norm_w, m_rel_bias, m_w_branch_a, m_w_branch_b, m_w_out, m_norm_mix_w, m_norm_mlp_w, m_w_up, m_w_down, m_norm_final_w, v_w_in, v_lb_logits, v_hg_norm_w, v_rel_bias, v_w_branch_a, v_w_branch_b, v_w_out, v_norm_mix_w, v_norm_mlp_w, v_w_up, v_w_down, v_norm_final_w):
    big_w = [w_in[0], w_branch_a[0], w_branch_b[0], w_out[0], w_up[0], w_down[0]]
    big_m = [m_w_in[0], m_w_branch_a[0], m_w_branch_b[0], m_w_out[0], m_w_up[0], m_w_down[0]]
    big_v = [v_w_in[0], v_w_branch_a[0], v_w_branch_b[0], v_w_out[0], v_w_up[0], v_w_down[0]]

    shards = [w.astype(BF16) for w in big_w]
    parity = lax.axis_index("c").astype(jnp.int32).reshape(1)
    loss_part, grad_x, chip_parts, small = _local_step(
        x[0], loss_target[0], lb_logits, hg_norm_w, rel_bias[0], norm_mix_w, norm_mlp_w,
        norm_final_w.reshape(1, D_MODEL), shards[0], shards[1:], _Exchanges(parity, _gather_order()))
    loss = lax.psum(loss_part[0, 0], ("x", "y", "c"))
    (rs_in_lo, rs_in_hi), rs_mix, rs_up, rs_down = chip_parts
    slot =(2 * lax.axis_index("x") + lax.axis_index("y")).astype(jnp.int32).reshape(1)
    big = {}

    def finish(rs, names, after):
        sums, lands = rs.finish(after)
        for nme, own, land in zip(names, sums, lands):
            i = BIG_NAMES.index(nme)
            big[nme] = _adamw_big_landed(big_w[i], big_m[i], big_v[i], own, land, slot, "adamw_" + nme)
        return [big[nme][1] for nme in names]

    done = finish(rs_down, ["w_down"], [grad_x])
    done = finish(rs_up, ["w_up"], done)
    done = finish(rs_mix, ["w_branch_a", "w_branch_b", "w_out"], done)

    sw = dict(lb_logits=lb_logits, hg_norm_w=hg_norm_w, rel_bias=rel_bias[0], norm_mix_w=norm_mix_w,
              norm_mlp_w=norm_mlp_w, norm_final_w=norm_final_w.reshape(1, D_MODEL))
    sm = dict(lb_logits=m_lb_logits, hg_norm_w=m_hg_norm_w, rel_bias=m_rel_bias[0], norm_mix_w=m_norm_mix_w,
              norm_mlp_w=m_norm_mlp_w, norm_final_w=m_norm_final_w.reshape(1, D_MODEL))
    sv = dict(lb_logits=v_lb_logits, hg_norm_w=v_hg_norm_w, rel_bias=v_rel_bias[0], norm_mix_w=v_norm_mix_w,
              norm_mlp_w=v_norm_mlp_w, norm_final_w=v_norm_final_w.reshape(1, D_MODEL))
    gathered = _gather_small(_pack_small(small), "gather_small")
    small_packed = _adamw_small(_pack_small(sw), _pack_small(sm), _pack_small(sv), gathered, "adamw_small")
    small_out = [_unpack_small(p) for p in small_packed]

    (own,), (land,) = rs_in_lo.finish(done + [small_packed[0]])
    lo = _adamw_big_landed(big_w[0], big_m[0], big_v[0], own, land, slot, "adamw_w_in_lo")
    (own,), (land,) = rs_in_hi.finish([lo[1]])
    big["w_in"] = _adamw_big_landed(big_w[0], big_m[0], big_v[0], own, land, slot, "adamw_w_in_hi",
                                    row0=D_MODEL // 2, into=lo)

    def leaf(kind, nme):
        if nme in BIG_NAMES:
            return big[nme][kind][None]
        p = small_out[kind][nme]
        if nme == "rel_bias":
            return p[None]
        if nme == "norm_final_w":
            return p.reshape(D_MODEL)
        return p

    order = ("w_in", "lb_logits", "hg_norm_w", "rel_bias", "w_branch_a", "w_branch_b", "w_out", "norm_mix_w",
             "norm_mlp_w", "w_up", "w_down", "norm_final_w")
    outs = [loss, grad_x[None]]
    for kind in range(4):
        outs += [leaf(kind, nme) for nme in order]
    return tuple(outs)
```

```python
import jax
import jax.numpy as jnp
from jax import lax
from jax.experimental import pallas as pl
from jax.experimental.pallas import tpu as pltpu

F32 = jnp.float32
BF16 = jnp.bfloat16
HIGHEST = lax.Precision.HIGHEST
MESH = pl.DeviceIdType.MESH

D_MODEL = 2048
HG_HEADS = 8
HG_DK = 128
HG_WIDTH = 1024
AT_HEADS = 16
AT_DH = 64
AT_WIDTH = 1024
CHUNK = 64
LEFT_CHUNKS = 8
BAND = (LEFT_CHUNKS + 1) * CHUNK
PAD = LEFT_CHUNKS * CHUNK
REL_CLIP = 256
N_REL = 2 * REL_CLIP + 1
N_REL_PAD = 640
D_FF = 4 * D_MODEL
EPS = 1e-6
N_DEV = 8
N_CHIP = 4

ADAM_LR = 0.001
ADAM_B1 = 0.9
ADAM_B2 = 0.999
ADAM_EPS = 1e-08
ADAM_WD = 0.01
ADAM_STEP = 10

COL_HQ, COL_HF, COL_HI, COL_HG = 0, 8, 16, 24
COL_AQ, COL_AK, COL_AV = 32, 40, 48
COL_GATE_A, COL_GATE_B = 7, 9

VMEM_LIMIT = 56 * 1024 * 1024
SMALL_ROWS = 152


def _cparams(sem=None, **kw):
    if sem is not None:
        kw["dimension_semantics"] = sem
    return pltpu.CompilerParams(vmem_limit_bytes=VMEM_LIMIT, **kw)


def _pick(n, cands):
    for c in cands:
        if n % c == 0:
            return c
    return n


def _sigmoid(x):
    return 1.0 / (1.0 + jnp.exp(-x))


ANY = pl.BlockSpec(memory_space=pl.ANY)


def _position():
    return lax.axis_index("x"), lax.axis_index("y"), lax.axis_index("c")


def _call(body, args, *, name, grid, in_specs, out_specs, out_shape, scratch_shapes=(), sem=None, after=()):
    n_in = len(args)

    def ordered(*refs):
        body(*refs[:n_in], *refs[n_in + len(after):])

    return list(pl.pallas_call(
        ordered if after else body, name=name, grid=grid, in_specs=list(in_specs) + [ANY] * len(after),
        out_specs=out_specs, out_shape=out_shape, scratch_shapes=list(scratch_shapes),
        compiler_params=_cparams(sem))(*args, *after))


MAX_CONTRACTION_TILE = 4096


def _accumulate(part, acc_ref, step, n_steps, finish):
    if n_steps == 1:
        finish(part)
        return

    @pl.when(step == 0)
    def _():
        acc_ref[...] = part

    @pl.when(step > 0)
    def _():
        acc_ref[...] += part

    @pl.when(step == n_steps - 1)
    def _():
        finish(acc_ref[...])


def _mm_nn(a, wb, out_dtype, name, after=(), epilogue=None):
    M, K = a.shape
    NB, K2, Nb = wb.shape
    assert K == K2
    tm = min(M, 1024)
    tk = min(K, MAX_CONTRACTION_TILE)
    tn = _pick(Nb, (512, 1408, 256))
    nk = K // tk
    nn = Nb // tn
    extra, first_cols, out_dtypes, fn = epilogue or ((), (), (out_dtype,), lambda total: (total,))
    n_extra, n_out = len(extra), len(out_dtypes)

    def body(a_ref, b_ref, *rest):
        def finish(total):
            results = fn(total, *[r[...] for r in rest[:n_extra]])
            for o_ref, res, dt in zip(rest[n_extra:n_extra + n_out], results, out_dtypes):
                o_ref[...] = res.astype(dt)

        part = jnp.dot(a_ref[...], b_ref[...], preferred_element_type=F32)
        _accumulate(part, rest[-1], pl.program_id(3), nk, finish)

    def tile(first):
        return pl.BlockSpec((tm, tn), lambda m, j, n, k: (m, first + j * nn + n))

    outs = _call(
        body, (a, wb) + tuple(extra), name=name, grid=(M // tm, NB, nn, nk),
        in_specs=[pl.BlockSpec((tm, tk), lambda m, j, n, k: (m, k)),
                  pl.BlockSpec((None, tk, tn), lambda m, j, n, k: (j, k, n))] + [tile(col // tn) for col in first_cols],
        out_specs=[tile(0)] * n_out,
        out_shape=[jax.ShapeDtypeStruct((M, NB * Nb), dt) for dt in out_dtypes],
        scratch_shapes=[] if nk == 1 else [pltpu.VMEM((tm, tn), F32)],
        sem=("parallel", "parallel", "parallel", "arbitrary"), after=after)
    return outs if epilogue else outs[0]


def _squared_relu(a):
    ra = jnp.maximum(a, 0.0)
    return a, ra * ra


def _gated_merge(pb, za, zb, pa):
    return pb, _sigmoid(za) * pa + _sigmoid(zb) * pb


def _mm_nt(a, wb, out_dtype, name, after=(), epilogue=None):
    M, N = a.shape
    NB, K, Nb = wb.shape
    assert N == NB * Nb
    tm = min(M, 1024)
    n_tiles_live = 1 + (len(epilogue[0]) + len(epilogue[2]) if epilogue else 0)
    tko = _pick(K, (1024,)) if n_tiles_live <= 3 else _pick(K, (512,))
    tc = _pick(Nb, (2048, 1024, 1408, 256))
    nc = Nb // tc
    jb = max([d for d in (8, 4, 2, 1) if NB % d == 0 and d * tc <= MAX_CONTRACTION_TILE]) if nc == 1 else 1
    nsteps = (NB // jb) * nc
    extra, first_cols, out_dtypes, fn = epilogue or ((), (), (out_dtype,), lambda total: (total,))
    n_extra, n_out = len(extra), len(out_dtypes)

    def body(a_ref, b_ref, *rest):
        def finish(total):
            results = fn(total, *[r[...] for r in rest[:n_extra]])
            for o_ref, res, dt in zip(rest[n_extra:n_extra + n_out], results, out_dtypes):
                o_ref[...] = res.astype(dt)

        part = sum(lax.dot_general(a_ref[:, i * tc:(i + 1) * tc], b_ref[i], (((1,), (1,)), ((), ())),
                                   preferred_element_type=F32) for i in range(jb))
        _accumulate(part, rest[-1], pl.program_id(2) * nc + pl.program_id(3), nsteps, finish)

    def tile(first):
        return pl.BlockSpec((tm, tko), lambda m, ko, j, c: (m, first + ko))

    outs = _call(
        body, (a, wb) + tuple(extra), name=name,
        grid=(M // tm, K // tko, NB // jb, nc),
        in_specs=[pl.BlockSpec((tm, jb * tc), lambda m, ko, j, c: (m, j * nc + c)),
                  pl.BlockSpec((jb, tko, tc), lambda m, ko, j, c: (j, ko, c))] + [tile(col // tko) for col in first_cols],
        out_specs=[tile(0)] * n_out,
        out_shape=[jax.ShapeDtypeStruct((M, K), dt) for dt in out_dtypes],
        scratch_shapes=[] if nsteps == 1 else [pltpu.VMEM((tm, tko), F32)],
        sem=("parallel", "parallel", "arbitrary", "arbitrary"), after=after)
    return outs if epilogue else outs[0]


ROWS_TILE = 512
ROWS_PIECE = 128


def _mm_rows(a, w, extras, vectors, row_dtypes, fn, name):
    M, K = a.shape
    N = w.shape[1]
    tm = min(M, ROWS_TILE)
    n_e, n_v = len(extras), len(vectors)

    def body(a_ref, w_ref, *rest):
        tiles, vecs, outs, product_ref = rest[:n_e], rest[n_e:n_e + n_v], rest[n_e + n_v:-1], rest[-1]
        product_ref[...] = jnp.dot(a_ref[...], w_ref[...], preferred_element_type=F32)
        for i in range(tm // ROWS_PIECE):
            piece = slice(i * ROWS_PIECE, (i + 1) * ROWS_PIECE)
            results = fn(product_ref[piece, :], *[t[piece, :] for t in tiles], *[v[...] for v in vecs])
            for o_ref, res, dt in zip(outs, results, row_dtypes):
                o_ref[piece, :] = res.astype(dt)

    row = pl.BlockSpec((tm, N), lambda m: (m, 0))
    return _call(
        body, (a, w) + tuple(extras) + tuple(vectors), name=name, grid=(M // tm,),
        in_specs=[pl.BlockSpec((tm, K), lambda m: (m, 0)), pl.BlockSpec((K, N), lambda m: (0, 0))]
        + [row] * n_e + [pl.BlockSpec((1, N), lambda m: (0, 0))] * n_v,
        out_specs=[row] * len(row_dtypes),
        out_shape=[jax.ShapeDtypeStruct((M, N), dt) for dt in row_dtypes],
        scratch_shapes=[pltpu.VMEM((tm, N), F32)], sem=("parallel",))


def _rms(h, w):
    return h * lax.rsqrt(jnp.mean(h * h, axis=-1, keepdims=True) + EPS) * w


def _residual_rms_rows(mix, x, w):
    h = x + mix
    return h, _rms(h, w)


def _mm_tn_half(a, g, which, blocks_on, add, name, after=()):
    M, Ka = a.shape
    N = g.shape[1]
    if blocks_on == "g":
        rows, cols = _pick(Ka, (1024,)), N // N_DEV
        tn = _pick(cols, (512, 1408, 256))
        nn = cols // tn
        grid = (Ka // rows, N_CHIP, nn)
        a_spec = pl.BlockSpec((M, rows), lambda ka, s, n, w: (0, ka))
        g_spec = pl.BlockSpec((M, tn), lambda ka, s, n, w: (0, (2 * s + w[0]) * nn + n))
        out_rows = Ka
    else:
        rows, cols = Ka // N_DEV, N
        tn = _pick(cols, (512,))
        nn = cols // tn
        grid = (1, N_CHIP, nn)
        a_spec = pl.BlockSpec((M, rows), lambda ka, s, n, w: (0, 2 * s + w[0]))
        g_spec = pl.BlockSpec((M, tn), lambda ka, s, n, w: (0, n))
        out_rows = rows
    o_spec = pl.BlockSpec((None, rows, tn), lambda ka, s, n, w: (s, ka, n))
    n_add = 0 if add is None else 1

    def body(which_ref, a_ref, g_ref, *rest):
        acc = lax.dot_general(a_ref[...], g_ref[...], (((0,), (0,)), ((), ())), preferred_element_type=F32)
        if n_add:
            acc = acc + rest[0][...].astype(F32)
        rest[-1][...] = acc.astype(BF16)

    return pl.pallas_call(
        body, name=name,
        grid_spec=pltpu.PrefetchScalarGridSpec(
            num_scalar_prefetch=1, grid=grid,
            in_specs=[a_spec, g_spec] + [o_spec] * n_add + [ANY] * len(after),
            out_specs=o_spec),
        out_shape=jax.ShapeDtypeStruct((N_CHIP, out_rows, cols), BF16),
        compiler_params=_cparams(("parallel", "parallel", "parallel")),
    )(which, a, g, *(() if add is None else (add,)), *after)


ROW_TILE = 256


def _rms_fwd(x, w, name):
    T, Dm = x.shape

    def body(x_ref, w_ref, u_ref):
        xv = x_ref[...]
        r = lax.rsqrt(jnp.mean(xv * xv, axis=-1, keepdims=True) + EPS)
        u_ref[...] = (xv * r * w_ref[...]).astype(BF16)

    return pl.pallas_call(
        body, name=name, grid=(T // ROW_TILE,),
        in_specs=[pl.BlockSpec((ROW_TILE, Dm), lambda i: (i, 0)), pl.BlockSpec((1, Dm), lambda i: (0, 0))],
        out_specs=pl.BlockSpec((ROW_TILE, Dm), lambda i: (i, 0)),
        out_shape=jax.ShapeDtypeStruct((T, Dm), BF16),
        compiler_params=_cparams(("parallel",)),
    )(x, w)


def _loss_head(h1, mlp, wf, target, name):
    T, Dm = h1.shape

    def body(h_ref, m_ref, w_ref, t_ref, loss_ref, dh_ref, dhb_ref, dw_ref):
        i = pl.program_id(0)
        h = h_ref[...] + m_ref[...]
        r = lax.rsqrt(jnp.mean(h * h, axis=-1, keepdims=True) + EPS)
        xh = h * r
        wv = w_ref[...]
        e = xh * wv - t_ref[...]
        part = 0.5 * jnp.sum(jnp.mean(e * e, axis=-1, keepdims=True), axis=0, keepdims=True)
        dy = e * (1.0 / Dm)
        dw = jnp.sum(dy * xh, axis=0, keepdims=True)
        gy = dy * wv
        dh = r * (gy - xh * jnp.mean(gy * xh, axis=-1, keepdims=True))
        dh_ref[...] = dh
        dhb_ref[...] = dh.astype(BF16)

        @pl.when(i == 0)
        def _():
            loss_ref[...] = jnp.zeros_like(loss_ref)
            dw_ref[...] = jnp.zeros_like(dw_ref)

        loss_ref[...] += jnp.broadcast_to(part, loss_ref.shape)
        dw_ref[...] += dw

    row = pl.BlockSpec((ROW_TILE, Dm), lambda i: (i, 0))
    vec = pl.BlockSpec((1, Dm), lambda i: (0, 0))
    return pl.pallas_call(
        body, name=name, grid=(T // ROW_TILE,),
        in_specs=[row, row, vec, row],
        out_specs=[pl.BlockSpec((8, 128), lambda i: (0, 0)), row, row, vec],
        out_shape=[jax.ShapeDtypeStruct((8, 128), F32), jax.ShapeDtypeStruct((T, Dm), F32),
                   jax.ShapeDtypeStruct((T, Dm), BF16), jax.ShapeDtypeStruct((1, Dm), F32)],
        compiler_params=_cparams(("arbitrary",)),
    )(h1, mlp, wf, target)


def _rms_bwd(dyn, x, w, dres, name, after=()):
    T, Dm = x.shape

    def body(g_ref, x_ref, w_ref, r_ref, dx_ref, dxb_ref, dw_ref):
        i = pl.program_id(0)
        xv = x_ref[...]
        r = lax.rsqrt(jnp.mean(xv * xv, axis=-1, keepdims=True) + EPS)
        xh = xv * r
        g = g_ref[...]
        dw = jnp.sum(g * xh, axis=0, keepdims=True)
        gy = g * w_ref[...]
        dx = r_ref[...] + r * (gy - xh * jnp.mean(gy * xh, axis=-1, keepdims=True))
        dx_ref[...] = dx
        dxb_ref[...] = dx.astype(BF16)

        @pl.when(i == 0)
        def _():
            dw_ref[...] = jnp.zeros_like(dw_ref)

        dw_ref[...] += dw

    row = pl.BlockSpec((ROW_TILE, Dm), lambda i: (i, 0))
    vec = pl.BlockSpec((1, Dm), lambda i: (0, 0))
    return _call(
        body, (dyn, x, w, dres), name=name, grid=(T // ROW_TILE,),
        in_specs=[row, row, vec, row],
        out_specs=[row, row, vec],
        out_shape=[jax.ShapeDtypeStruct((T, Dm), F32), jax.ShapeDtypeStruct((T, Dm), BF16),
                   jax.ShapeDtypeStruct((1, Dm), F32)],
        sem=("arbitrary",), after=after)


GATE_TILE = 1024


def _merge_grads(d, za, zb, pa, pb):
    ga = _sigmoid(za)
    gb = _sigmoid(zb)
    return d * ga, d * gb, d * pa * ga * (1.0 - ga), d * pb * gb * (1.0 - gb)


def _dot_hi(a, b, dims):
    return lax.dot_general(a, b, (dims, ((), ())), precision=HIGHEST, preferred_element_type=F32)


NN = ((1,), (0,))
NT = ((1,), (1,))
TN = ((0,), (0,))


def _hg_gates(hq, hf, lb):
    sq = _sigmoid(hq)
    q = hq * sq * (HG_DK ** -0.5)
    f = _sigmoid(hf)
    g = lb + (1.0 - lb) * f
    return q, sq, f, g, jnp.log(g), 1.0 - g


def _tri(lower):
    r = lax.broadcasted_iota(jnp.int32, (CHUNK, CHUNK), 0)
    c = lax.broadcasted_iota(jnp.int32, (CHUNK, CHUNK), 1)
    return jnp.where((r >= c) if lower else (r <= c), 1.0, 0.0).astype(F32)


GROUP = 16
N_GROUPS = CHUNK // GROUP
BWD_CHUNKS_PER_TRIP = 4


def _dot_bf16(a, b, dims):
    return lax.dot_general(a.astype(BF16), b.astype(BF16), (dims, ((), ())), preferred_element_type=F32)


def _rows_iota():
    return lax.broadcasted_iota(jnp.int32, (CHUNK, HG_DK), 0)


def _by_query_group(q, kk, b, g):
    r0 = GROUP * g
    b0 = b[r0:r0 + 1]
    decay = jnp.exp(b[r0:r0 + GROUP] - b0)
    ks = jnp.where(_rows_iota() < r0, kk * jnp.exp(jnp.minimum(b0 - b, 0.0)), 0.0)
    return q[r0:r0 + GROUP] * decay, ks, decay


def _by_key_group(q, kk, b, j):
    r1 = GROUP * (j + 1)
    b1 = b[r1 - 1:r1]
    decay = jnp.exp(b1 - b[r1 - GROUP:r1])
    qs = jnp.where(_rows_iota() >= r1, q * jnp.exp(jnp.minimum(b - b1, 0.0)), 0.0)
    return qs, kk[r1 - GROUP:r1] * decay, decay


def _scores_between_groups(q, kk, b):
    blocks = [jnp.zeros((GROUP, CHUNK), F32)]
    for g in range(1, N_GROUPS):
        qs, ks, _ = _by_query_group(q, kk, b, g)
        blocks.append(_dot_bf16(qs, ks, NT))
    return jnp.concatenate(blocks, axis=0)


def _hgrn2_fwd(z, lb_logits, hg_norm_w, name, after=()):
    T = z.shape[0]
    n_chunks = T // CHUNK

    def body(hq_ref, hf_ref, hi_ref, hg_ref, lbl_ref, nw_ref, o_ref, ya_ref, sall_ref, st_ref):
        lbl = lbl_ref[...]
        lb = 1.0 / (1.0 + jnp.exp(lbl[1:2, :] - lbl[0:1, :]))
        st_ref[...] = jnp.zeros_like(st_ref)
        tri = _tri(True)
        row8 = lax.broadcasted_iota(jnp.int32, (8, HG_DK), 0)

        def chunk(c, carry):
            rows = pl.ds(pl.multiple_of(c * CHUNK, CHUNK), CHUNK)
            q, _, _, _, lg, kk = _hg_gates(hq_ref[rows, :], hf_ref[rows, :], lb)
            v = hi_ref[rows, :]
            b = _dot_hi(tri, lg, NN)
            st = st_ref[...]
            sall_ref[c] = st
            for grp in range(N_GROUPS):
                r0 = GROUP * grp
                for h8 in range(GROUP // 8):
                    n = 8 * (h8 + 1)
                    bs, ks, vs = b[r0:r0 + n], kk[r0:r0 + n], v[r0:r0 + n]
                    sidx = lax.broadcasted_iota(jnp.int32, (n, HG_DK), 0)
                    blk = jnp.zeros((8, HG_DK), F32)
                    for i in range(8):
                        t = r0 + 8 * h8 + i
                        e = jnp.where(sidx <= 8 * h8 + i, jnp.exp(b[t:t + 1] - bs), 0.0)
                        p = jnp.sum(e * ks * q[t:t + 1], axis=1, keepdims=True)
                        ot = jnp.sum(p * vs, axis=0, keepdims=True)
                        blk = blk + jnp.where(row8 == i, ot, 0.0)
                    o_ref[pl.ds(pl.multiple_of(c * CHUNK + r0 + 8 * h8, 8), 8), :] = blk
            o_ref[rows, :] += _dot_hi(q * jnp.exp(b), st, NT) + _dot_bf16(_scores_between_groups(q, kk, b), v, NN)
            bl = b[CHUNK - 1:CHUNK]
            ke = kk * jnp.exp(bl - b)
            st_ref[...] = st * jnp.exp(bl) + _dot_hi(v, ke, TN)
            return carry

        lax.fori_loop(0, n_chunks, chunk, 0, unroll=2)
        o = o_ref[...]
        r = lax.rsqrt(jnp.mean(o * o, axis=-1, keepdims=True) + EPS)
        hg = hg_ref[...]
        ya_ref[...] = (o * r * nw_ref[...] * (hg * _sigmoid(hg))).astype(BF16)

    def col(base):
        return pl.BlockSpec((T, HG_DK), lambda h: (0, base + h))

    return _call(
        body, (z, z, z, z, lb_logits, hg_norm_w), name=name, grid=(HG_HEADS,),
        in_specs=[col(COL_HQ), col(COL_HF), col(COL_HI), col(COL_HG),
                  pl.BlockSpec((2, HG_DK), lambda h: (0, h)), pl.BlockSpec((1, HG_DK), lambda h: (0, 0))],
        out_specs=[col(0), col(0), pl.BlockSpec((None, n_chunks, HG_DK, HG_DK), lambda h: (h, 0, 0, 0))],
        out_shape=[jax.ShapeDtypeStruct((T, HG_WIDTH), F32), jax.ShapeDtypeStruct((T, HG_WIDTH), BF16),
                   jax.ShapeDtypeStruct((HG_HEADS, n_chunks, HG_DK, HG_DK), F32)],
        scratch_shapes=[pltpu.VMEM((HG_DK, HG_DK), F32)],
        sem=("parallel",), after=after)


def _hgrn2_bwd(z, lb_logits, hg_norm_w, o_raw, s_all, dya, name, after=()):
    T = z.shape[0]
    n_chunks = T // CHUNK

    def body(hq_ref, hf_ref, hi_ref, hg_ref, lbl_ref, nw_ref, o_ref, sall_ref, dya_ref,
             dhq_ref, dhf_ref, dhi_ref, dhg_ref, dlbl_ref, dnw_ref,
             do_ref, dst_ref, dlb_ref, *per_chunk):
        h = pl.program_id(0)
        lbl = lbl_ref[...]
        lb = 1.0 / (1.0 + jnp.exp(lbl[1:2, :] - lbl[0:1, :]))

        o = o_ref[...]
        r = lax.rsqrt(jnp.mean(o * o, axis=-1, keepdims=True) + EPS)
        oh = o * r
        nw = nw_ref[...]
        hg = hg_ref[...]
        sg = _sigmoid(hg)
        dy = dya_ref[...]
        d_on = dy * (hg * sg)
        dhg_ref[...] = (dy * (oh * nw) * (sg * (1.0 + hg * (1.0 - sg)))).astype(BF16)
        dnw = jnp.sum(d_on * oh, axis=0, keepdims=True)
        gy = d_on * nw
        do_ref[...] = r * (gy - oh * jnp.mean(gy * oh, axis=-1, keepdims=True))

        @pl.when(h == 0)
        def _():
            dnw_ref[...] = jnp.zeros_like(dnw_ref)

        dnw_ref[...] += jnp.broadcast_to(dnw, dnw_ref.shape)

        dst_ref[...] = jnp.zeros_like(dst_ref)
        dlb_ref[...] = jnp.zeros_like(dlb_ref)
        tri = _tri(True)
        tri_t = _tri(False)
        row8 = lax.broadcasted_iota(jnp.int32, (8, HG_DK), 0)
        row_group = lax.broadcasted_iota(jnp.int32, (CHUNK, CHUNK), 0) // GROUP
        col_group = lax.broadcasted_iota(jnp.int32, (CHUNK, CHUNK), 1) // GROUP
        earlier_group = col_group < row_group
        later_group = col_group > row_group

        def chunk(c, dq_ref, dk_ref, dv_ref):
            rows = pl.ds(pl.multiple_of(c * CHUNK, CHUNK), CHUNK)
            hq = hq_ref[rows, :]
            q, sq, f, g, lg, kk = _hg_gates(hq, hf_ref[rows, :], lb)
            v = hi_ref[rows, :]
            do = do_ref[rows, :]
            b = _dot_hi(tri, lg, NN)
            eb = jnp.exp(b)
            bl = b[CHUNK - 1:CHUNK]
            ebl = jnp.exp(bl)
            ekb = jnp.exp(bl - b)
            qe = q * eb
            ke = kk * ekb
            st = sall_ref[c]
            dst = dst_ref[...]
            dqe = _dot_hi(do, st, NN)
            dke = _dot_hi(v, dst, NN)
            dv_inter = _dot_hi(ke, dst, NT)
            d_ebl = jnp.sum(st * dst, axis=0, keepdims=True)
            dst_ref[...] = dst * ebl + _dot_hi(do, qe, TN)

            dk_ref[...] = jnp.zeros_like(dk_ref)
            dv_ref[...] = jnp.zeros_like(dv_ref)
            for grp in range(N_GROUPS):
                r0 = GROUP * grp
                for h8 in range(GROUP // 8):
                    n = 8 * (h8 + 1)
                    bs, ks, vs = b[r0:r0 + n], kk[r0:r0 + n], v[r0:r0 + n]
                    sidx = lax.broadcasted_iota(jnp.int32, (n, HG_DK), 0)
                    blk = jnp.zeros((8, HG_DK), F32)
                    for i in range(8):
                        t = r0 + 8 * h8 + i
                        qt = q[t:t + 1]
                        dot_ = do[t:t + 1]
                        e = jnp.where(sidx <= 8 * h8 + i, jnp.exp(b[t:t + 1] - bs), 0.0)
                        w = e * ks
                        p = jnp.sum(w * qt, axis=1, keepdims=True)
                        dsc = jnp.sum(vs * dot_, axis=1, keepdims=True)
                        dqt = jnp.sum(dsc * w, axis=0, keepdims=True)
                        blk = blk + jnp.where(row8 == i, dqt, 0.0)
                        dk_ref[r0:r0 + n, :] += dsc * e * qt
                        dv_ref[r0:r0 + n, :] += p * dot_
                    dq_ref[r0 + 8 * h8:r0 + n, :] = blk
            ds_far = jnp.where(earlier_group, _dot_bf16(do, v, NT), 0.0)
            ds_far_t = jnp.where(later_group, _dot_bf16(v, do, NT), 0.0)
            dq_far, dk_far = [jnp.zeros((GROUP, HG_DK), F32)], []
            for grp in range(1, N_GROUPS):
                r0 = GROUP * grp
                _, ks, decay = _by_query_group(q, kk, b, grp)
                dq_far.append(decay * _dot_hi(ds_far[r0:r0 + GROUP], ks, NN))
                qs, _, decay = _by_key_group(q, kk, b, grp - 1)
                dk_far.append(decay * _dot_hi(ds_far_t[r0 - GROUP:r0], qs, NN))
            dk_far.append(jnp.zeros((GROUP, HG_DK), F32))
            dv_far = _dot_bf16(_scores_between_groups(q, kk, b), do, TN)
            dq_i = dq_ref[...] + jnp.concatenate(dq_far, axis=0)
            dk_i = dk_ref[...] + jnp.concatenate(dk_far, axis=0)
            dke_ke = dke * ke
            db = q * dq_i - kk * dk_i + dqe * qe - dke_ke
            db_last = jnp.sum(dke_ke, axis=0, keepdims=True) + d_ebl * ebl
            dlg = _dot_hi(tri_t, db, NN) + db_last
            dq = dq_i + dqe * eb
            dkk = dk_i + dke * ekb
            dg = dlg / g - dkk
            dhq_ref[rows, :] = (dq * (HG_DK ** -0.5) * (sq * (1.0 + hq * (1.0 - sq)))).astype(BF16)
            dhf_ref[rows, :] = (dg * (1.0 - lb) * f * (1.0 - f)).astype(BF16)
            dhi_ref[rows, :] = (dv_ref[...] + dv_far + dv_inter).astype(BF16)
            dlb_ref[...] += jnp.sum(dg * (1.0 - f), axis=0, keepdims=True)

        def trip(i, carry):
            for k in range(BWD_CHUNKS_PER_TRIP):
                chunk(n_chunks - 1 - k - BWD_CHUNKS_PER_TRIP * i, *per_chunk[3 * k:3 * k + 3])
            return carry

        lax.fori_loop(0, n_chunks // BWD_CHUNKS_PER_TRIP, trip, 0)
        dl0 = dlb_ref[...] * lb * (1.0 - lb)
        dlbl_ref[0:1, :] = dl0
        dlbl_ref[1:2, :] = -dl0

    def col(base):
        return pl.BlockSpec((T, HG_DK), lambda h: (0, base + h))

    outb = jax.ShapeDtypeStruct((T, HG_WIDTH), BF16)
    return _call(
        body, (z, z, z, z, lb_logits, hg_norm_w, o_raw, s_all, dya), name=name, grid=(HG_HEADS,),
        in_specs=[col(COL_HQ), col(COL_HF), col(COL_HI), col(COL_HG),
                  pl.BlockSpec((2, HG_DK), lambda h: (0, h)), pl.BlockSpec((1, HG_DK), lambda h: (0, 0)),
                  col(0), pl.BlockSpec((None, n_chunks, HG_DK, HG_DK), lambda h: (h, 0, 0, 0)), col(0)],
        out_specs=[col(0), col(0), col(0), col(0), pl.BlockSpec((2, HG_DK), lambda h: (0, h)),
                   pl.BlockSpec((8, HG_DK), lambda h: (0, 0))],
        out_shape=[outb, outb, outb, outb, jax.ShapeDtypeStruct((2, HG_WIDTH), F32),
                   jax.ShapeDtypeStruct((8, HG_DK), F32)],
        scratch_shapes=[pltpu.VMEM((T, HG_DK), F32), pltpu.VMEM((HG_DK, HG_DK), F32), pltpu.VMEM((1, HG_DK), F32)]
        + [pltpu.VMEM((CHUNK, HG_DK), F32)] * (3 * BWD_CHUNKS_PER_TRIP),
        sem=("arbitrary",), after=after)


CONST_KEYS = PAD - REL_CLIP
VAR_KEYS = BAND - CONST_KEYS
REL_LO = 128
REL_SPAN = N_REL_PAD - REL_LO


def _rel_onehot(t):
    r = lax.broadcasted_iota(jnp.int32, (REL_SPAN, VAR_KEYS), 0)
    j = lax.broadcasted_iota(jnp.int32, (REL_SPAN, VAR_KEYS), 1)
    idx = jnp.clip(t + PAD - CONST_KEYS - j, -REL_CLIP, REL_CLIP) + REL_CLIP - REL_LO
    return jnp.where(r == idx, 1.0, 0.0).astype(BF16)


def _split3(x):
    hi = x.astype(BF16)
    r1 = x - hi.astype(F32)
    mid = r1.astype(BF16)
    return hi, mid, (r1 - mid.astype(F32)).astype(BF16)


def _bias_expand(rel, name):
    def body(rel_ref, out_ref):
        tab = rel_ref[...]
        onehot = _rel_onehot(pl.program_id(0))
        out_ref[:, 0:CONST_KEYS] = jnp.broadcast_to(tab[:, 2 * REL_CLIP:2 * REL_CLIP + 1], (AT_HEADS, CONST_KEYS))
        out_ref[:, CONST_KEYS:BAND] = sum(
            jnp.dot(piece, onehot, preferred_element_type=F32) for piece in _split3(tab[:, REL_LO:N_REL_PAD]))

    return pl.pallas_call(
        body, name=name, grid=(CHUNK,),
        in_specs=[pl.BlockSpec((AT_HEADS, N_REL_PAD), lambda t: (0, 0))],
        out_specs=pl.BlockSpec((None, AT_HEADS, BAND), lambda t: (t, 0, 0)),
        out_shape=jax.ShapeDtypeStruct((CHUNK, AT_HEADS, BAND), F32),
        compiler_params=_cparams(("parallel",)),
    )(rel)


def _bias_reduce(dbias_rows, name, after=()):
    def body(db_ref, out_ref):
        lane = lax.broadcasted_iota(jnp.int32, (AT_HEADS, N_REL_PAD), 1)
        varying = lane >= CONST_KEYS
        by_offset = jnp.zeros((AT_HEADS, N_REL_PAD), F32)
        constant = jnp.zeros((AT_HEADS, N_REL_PAD), F32)
        for t in range(CHUNK):
            row = db_ref[t]
            constant = constant + jnp.where(varying, 0.0, row)
            moved = jnp.where(varying, row, 0.0)
            by_offset = by_offset + (pltpu.roll(moved, N_REL_PAD - t, axis=1) if t else moved)
        offset = lax.broadcasted_iota(jnp.int32, (N_REL_PAD, N_REL_PAD), 0)
        entry = lax.broadcasted_iota(jnp.int32, (N_REL_PAD, N_REL_PAD), 1)
        onehot = jnp.where(entry == jnp.clip(PAD - offset, -REL_CLIP, REL_CLIP) + REL_CLIP, 1.0, 0.0).astype(BF16)
        acc = sum(jnp.dot(piece, onehot, preferred_element_type=F32) for piece in _split3(by_offset))
        last = jnp.sum(constant, axis=1, keepdims=True)
        out_ref[...] = acc + jnp.where(lane == 2 * REL_CLIP, last, 0.0)

    whole = pl.BlockSpec((CHUNK, AT_HEADS, N_REL_PAD), lambda i: (0, 0, 0))
    return _call(
        body, (dbias_rows,), name=name, grid=(1,), in_specs=[whole],
        out_specs=[pl.BlockSpec((AT_HEADS, N_REL_PAD), lambda i: (0, 0))],
        out_shape=[jax.ShapeDtypeStruct((AT_HEADS, N_REL_PAD), F32)],
        sem=("arbitrary",), after=after)[0]


def _pair_lanes():
    return lax.broadcasted_iota(jnp.int32, (CHUNK, 2 * AT_DH), 1) < AT_DH


def _block_diag(a):
    first = _pair_lanes()
    return jnp.concatenate([jnp.where(first, a, 0.0), jnp.where(first, 0.0, a)], axis=0).astype(BF16)


def _diag_blocks(a):
    return jnp.where(_pair_lanes(), a[:CHUNK], a[CHUNK:])


def _band_probs_t(kb, qbd, bias_t, c):
    s = lax.dot_general(kb, qbd, (NT, ((), ())), preferred_element_type=F32) * (AT_DH ** -0.5) + bias_t
    j = lax.broadcasted_iota(jnp.int32, (BAND, 2 * AT_DH), 0)
    s = jnp.where(j + c * CHUNK >= PAD, s, -jnp.inf)
    p = jnp.exp(s - jnp.max(s, axis=0, keepdims=True))
    return p / jnp.sum(p, axis=0, keepdims=True)


def _attn_fwd(z, bias_t, name, after=()):
    T = z.shape[0]
    n_chunks = T // CHUNK

    def body(q_ref, k_ref, v_ref, bias_ref, y_ref, p_ref, *scratch):
        for pr in range(2):
            lanes = slice(128 * pr, 128 * (pr + 1))
            for dst_ref, src_ref in zip(scratch[2 * pr:2 * pr + 2], (k_ref, v_ref)):
                dst_ref[0:PAD, :] = jnp.zeros((PAD, 128), BF16)
                dst_ref[PAD:PAD + T, :] = src_ref[:, lanes].astype(BF16)

        def chunk(c, carry):
            rows = pl.ds(pl.multiple_of(c * CHUNK, CHUNK), CHUNK)
            band = pl.ds(pl.multiple_of(c * CHUNK, CHUNK), BAND)
            for pr in range(2):
                kp_ref, vp_ref = scratch[2 * pr:2 * pr + 2]
                lanes = slice(128 * pr, 128 * (pr + 1))
                p = _band_probs_t(kp_ref[band, :], _block_diag(q_ref[rows, lanes]), bias_ref[pr], c).astype(BF16)
                p_ref[pr, c] = p
                o2 = lax.dot_general(p, vp_ref[band, :], (TN, ((), ())), preferred_element_type=F32)
                y_ref[rows, lanes] = _diag_blocks(o2).astype(BF16)
            return carry

        lax.fori_loop(0, n_chunks, chunk, 0, unroll=2)

    def col(base):
        return pl.BlockSpec((T, 256), lambda h: (0, base // 2 + h))

    return _call(
        body, (z, z, z, bias_t), name=name, grid=(AT_HEADS // 4,),
        in_specs=[col(COL_AQ), col(COL_AK), col(COL_AV), pl.BlockSpec((2, BAND, 128), lambda h: (h, 0, 0))],
        out_specs=[col(0), pl.BlockSpec((2, n_chunks, BAND, 128), lambda h: (h, 0, 0, 0))],
        out_shape=[jax.ShapeDtypeStruct((T, AT_WIDTH), BF16),
                   jax.ShapeDtypeStruct((AT_HEADS // 2, n_chunks, BAND, 128), BF16)],
        scratch_shapes=[pltpu.VMEM((PAD + T, 128), BF16)] * 4,
        sem=("parallel",), after=after)


def _attn_bwd(z, probs, dyb, name, after=()):
    T = z.shape[0]
    n_chunks = T // CHUNK

    def body(q_ref, k_ref, v_ref, p_ref, dy_ref, dq_ref, dk_ref, dv_ref, dbias_ref, *scratch):
        dbias_ref[...] = jnp.zeros_like(dbias_ref)
        for pr in range(2):
            kp_ref, vp_ref, dkp_ref, dvp_ref = scratch[4 * pr:4 * pr + 4]
            lanes = slice(128 * pr, 128 * (pr + 1))
            kp_ref[0:PAD, :] = jnp.zeros((PAD, 128), BF16)
            vp_ref[0:PAD, :] = jnp.zeros((PAD, 128), BF16)
            kp_ref[PAD:PAD + T, :] = k_ref[:, lanes].astype(BF16)
            vp_ref[PAD:PAD + T, :] = v_ref[:, lanes].astype(BF16)
            dkp_ref[...] = jnp.zeros_like(dkp_ref)
            dvp_ref[...] = jnp.zeros_like(dvp_ref)

        def chunk(c, carry):
            rows = pl.ds(pl.multiple_of(c * CHUNK, CHUNK), CHUNK)
            band = pl.ds(pl.multiple_of(c * CHUNK, CHUNK), BAND)
            for pr in range(2):
                kp_ref, vp_ref, dkp_ref, dvp_ref = scratch[4 * pr:4 * pr + 4]
                lanes = slice(128 * pr, 128 * (pr + 1))
                qbd = _block_diag(q_ref[rows, lanes])
                dobd = _block_diag(dy_ref[rows, lanes])
                pb = p_ref[pr, c]
                p = pb.astype(F32)
                dp = lax.dot_general(vp_ref[band, :], dobd, (NT, ((), ())), preferred_element_type=F32)
                ds = p * (dp - jnp.sum(dp * p, axis=0, keepdims=True))
                dbias_ref[pr] += ds
                dsb = ds.astype(BF16)
                dq2 = lax.dot_general(dsb, kp_ref[band, :], (TN, ((), ())), preferred_element_type=F32)
                dq_ref[rows, lanes] = (_diag_blocks(dq2) * (AT_DH ** -0.5)).astype(BF16)
                dkp_ref[band, :] += jnp.dot(dsb, qbd, preferred_element_type=F32) * (AT_DH ** -0.5)
                dvp_ref[band, :] += jnp.dot(pb, dobd, preferred_element_type=F32)
            return carry

        lax.fori_loop(0, n_chunks, chunk, 0)
        for pr in range(2):
            lanes = slice(128 * pr, 128 * (pr + 1))
            dk_ref[:, lanes] = scratch[4 * pr + 2][PAD:PAD + T, :].astype(BF16)
            dv_ref[:, lanes] = scratch[4 * pr + 3][PAD:PAD + T, :].astype(BF16)

    def col(base):
        return pl.BlockSpec((T, 256), lambda h: (0, base // 2 + h))

    outb = jax.ShapeDtypeStruct((T, AT_WIDTH), BF16)
    return _call(
        body, (z, z, z, probs, dyb), name=name, grid=(AT_HEADS // 4,),
        in_specs=[col(COL_AQ), col(COL_AK), col(COL_AV),
                  pl.BlockSpec((2, n_chunks, BAND, 128), lambda h: (h, 0, 0, 0)), col(0)],
        out_specs=[col(0), col(0), col(0), pl.BlockSpec((2, BAND, 128), lambda h: (h, 0, 0))],
        out_shape=[outb, outb, outb, jax.ShapeDtypeStruct((AT_HEADS // 2, BAND, 128), F32)],
        scratch_shapes=([pltpu.VMEM((PAD + T, 128), BF16)] * 2 + [pltpu.VMEM((PAD + T, 128), F32)] * 2) * 2,
        sem=("parallel",), after=after)


def _local_step(x, target, lb_logits, hg_norm_w, rel_bias, norm_mix_w, norm_mlp_w, norm_final_w,
                w_in, rest, exchanges=None):
    ex = exchanges
    rel = jnp.pad(rel_bias, ((0, 0), (0, N_REL_PAD - N_REL)))

    u = _rms_fwd(x, norm_mix_w, "rms_mix_fwd")
    if ex:
        z, w_in = _mm_gathered(u, w_in, ex.order, "mm_in_fwd")
        gather = _Gather(rest, [w_in], "ag")
        z = _mm_gathered_tail(u, w_in, z, ex.order, "mm_in_fwd_tail", after=[gather.token])
        tok = []
    else:
        z = _mm_nn(u, w_in, F32, "mm_in_fwd")
        w_a, w_b, w_out, w_up, w_down = rest
        tok = []
    o_raw, y_a, s_all = _hgrn2_fwd(z, lb_logits, hg_norm_w, "hgrn2_fwd", after=tok)
    if ex:
        tok = [gather.pass_on([0, 1, 2], [o_raw], "abo")]
    bias_rows = _bias_expand(rel, "bias_expand")
    bias_t = jnp.transpose(bias_rows.reshape(CHUNK, AT_HEADS // 2, 2, BAND), (1, 3, 2, 0)).reshape(
        AT_HEADS // 2, BAND, 2 * CHUNK)
    y_b, probs = _attn_fwd(z, bias_t, "attn_fwd", after=tok)
    if ex:
        tok = [gather.pass_on([3], [y_b], "up")]
        w_a, w_b, w_out = gather.finish([0, 1, 2], tok, "abo")
    pa = _mm_nn(y_a, w_a, F32, "mm_a_fwd")
    pb, merged = _mm_nn(y_b, w_b, None, "mm_b_fwd", epilogue=(
        (z, z, pa), (COL_GATE_A * GATE_TILE, COL_GATE_B * GATE_TILE, 0), (F32, BF16), _gated_merge))
    w_out1 = w_out.reshape(1, D_MODEL, D_MODEL)
    h1, u2 = _mm_rows(merged, w_out.reshape(D_MODEL, D_MODEL), [x], [norm_mlp_w], (F32, BF16),
                      _residual_rms_rows, "mm_out_fwd")
    if ex:
        tok = [gather.pass_on([4], [u2], "down")]
        w_up, = gather.finish([3], tok, "up")
    a, r = _mm_nn(u2, w_up, None, "mm_up_fwd", epilogue=((), (), (F32, BF16), _squared_relu))
    if ex:
        w_down, = gather.finish([4], [r], "down")
    w_down1 = w_down.reshape(1, D_FF, D_MODEL)
    mlp = _mm_nn(r, w_down1, F32, "mm_down_fwd")
    loss, dh2, dh2b, g_nf = _loss_head(h1, mlp, norm_final_w, target, "loss_head")

    own = ex.parity if ex else jnp.zeros((1,), jnp.int32)

    def sibling_half(weights, name, after=()):
        others = [_mm_tn_half(a_, g_, 1 - own, on, None, nm + "_sibling", after) for a_, g_, on, nm in weights]
        rs = _ReduceScatter(others, name) if ex else None
        return rs, others, ([rs.token] if ex else [])

    def own_half(rs, weights, others, after):
        landed = rs.from_sibling(after) if ex else [None] * len(weights)
        sums = [_mm_tn_half(a_, g_, own, on, l, nm + "_own") for (a_, g_, on, nm), l in zip(weights, landed)]
        if ex:
            return [rs.scatter(sums)], None
        return [], [jnp.stack([s_, o_], axis=1).reshape((N_DEV,) + s_.shape[1:]) for s_, o_ in zip(sums, others)]

    down = [(r, dh2b, "a", "mm_down_wgrad")]
    rs_down, others, tok = sibling_half(down, "rs_down")
    da, = _mm_nt(dh2b, w_down1, None, "mm_down_dgrad", after=tok, epilogue=(
        (a,), (0,), (BF16,), lambda dr, av: (dr * (2.0 * jnp.maximum(av, 0.0)),)))
    tok, g_down = own_half(rs_down, down, others, [da])
    up = [(u2, da, "g", "mm_up_wgrad")]
    rs_up, others, tok = sibling_half(up, "rs_up", tok)
    du2 = _mm_nt(da, w_up, F32, "mm_up_dgrad", after=tok)
    tok, g_up = own_half(rs_up, up, others, [du2])
    dh1, dh1b, g_nmlp = _rms_bwd(du2, h1, norm_mlp_w, dh2, "rms_mlp_bwd", after=tok)

    dpa, dpb, dga, dgb = _mm_nt(dh1b, w_out1, None, "mm_out_dgrad", epilogue=(
        (z, z, pa, pb), (COL_GATE_A * GATE_TILE, COL_GATE_B * GATE_TILE, 0, 0), (BF16,) * 4, _merge_grads))
    mix = [(y_a, dpa, "g", "mm_a_wgrad"), (y_b, dpb, "g", "mm_b_wgrad"), (merged, dh1b, "a", "mm_out_wgrad")]
    rs_mix, others, tok = sibling_half(mix, "rs_mix")
    dya = _mm_nt(dpa, w_a, F32, "mm_a_dgrad", after=tok)
    dyb = _mm_nt(dpb, w_b, F32, "mm_b_dgrad", after=tok)
    tok, g_mix = own_half(rs_mix, mix, others, [dya, dyb])
    daq, dak, dav, dbias_t = _attn_bwd(z, probs, dyb, "attn_bwd", after=tok)
    dhq, dhf, dhi, dhg, g_lbl, g_hgw = _hgrn2_bwd(z, lb_logits, hg_norm_w, o_raw, s_all, dya, "hgrn2_bwd",
                                                  after=tok)
    dbias_rows = jnp.pad(jnp.transpose(dbias_t.reshape(AT_HEADS // 2, BAND, 2, CHUNK), (3, 0, 2, 1)).reshape(
        CHUNK, AT_HEADS, BAND), ((0, 0), (0, 0), (0, N_REL_PAD - BAND)))
    dz = jnp.concatenate([dhq, dhf, dhi, dhg, daq, dak, dav, dga, dgb], axis=1)
    half = D_MODEL // 2
    lo = [(u[:, :half], dz, "g", "mm_in_wgrad_lo")]
    hi = [(u[:, half:], dz, "g", "mm_in_wgrad_hi")]
    rs_in_lo, others_lo, tok = sibling_half(lo, "rs_in_lo")
    rs_in_hi, others_hi, tok = sibling_half(hi, "rs_in_hi", tok)
    tok, g_in_lo = own_half(rs_in_lo, lo, others_lo, tok)
    du = _mm_nt(dz, w_in, F32, "mm_in_dgrad", after=tok)
    tok, g_in_hi = own_half(rs_in_hi, hi, others_hi, [du])
    grad_x, _, g_nmix = _rms_bwd(du, x, norm_mix_w, dh1, "rms_mix_bwd", after=tok)
    g_rel = _bias_reduce(dbias_rows, "bias_reduce", after=tok)[:, :N_REL]

    small = dict(lb_logits=g_lbl, hg_norm_w=g_hgw[0:1], rel_bias=g_rel, norm_mix_w=g_nmix, norm_mlp_w=g_nmlp,
                 norm_final_w=g_nf)
    if ex:
        grads = [(rs_in_lo, rs_in_hi), rs_mix, rs_up, rs_down]
    else:
        grads = [jnp.concatenate([g_in_lo[0], g_in_hi[0]], axis=1)] + g_mix + [g_up[0], g_down[0]]
    return loss, grad_x, grads, small


def _mm_gathered(u, shard, order, name):
    T, K = u.shape
    _, Nb = shard.shape

    def body(order_ref, u_ref, shard_ref, z_ref, full_ref, wbuf, load_sem, send_sems, recv_sems, local_sem):
        s = pl.program_id(0)
        x, y, c = _position()
        me, sibling = (x, y, c), (x, y, 1 - c)
        chips = [(1 - x, y), (x, 1 - y), (1 - x, 1 - y)]

        def copy(k, block, to, src=None):
            dst = full_ref.at[4 * block[0] + 2 * block[1] + block[2]]
            return pltpu.make_async_remote_copy(
                src_ref=dst if src is None else src, dst_ref=dst,
                send_sem=send_sems.at[k], recv_sem=recv_sems.at[k], device_id=to, device_id_type=MESH)

        @pl.when(s == 0)
        def _():
            local = pltpu.make_async_copy(shard_ref, full_ref.at[4 * x + 2 * y + c], local_sem)
            local.start()
            copy(0, me, sibling, src=shard_ref).start()
            for j, chip in enumerate(chips):
                copy(1 + j, me, (*chip, c), src=shard_ref).start()
            local.wait()

        @pl.when(s == 1)
        def _():
            copy(0, sibling, me).wait_recv()

        for j, chip in enumerate(chips):
            direct, passed = ((2, 4), (3, 5), (6, 7))[j]

            @pl.when(s == direct)
            def _(j=j, chip=chip):
                copy(1 + j, (*chip, c), me).wait_recv()
                copy(4 + j, (*chip, c), sibling).start()

            @pl.when(s == passed)
            def _(j=j, chip=chip):
                copy(4 + j, (*chip, 1 - c), me).wait_recv()

        @pl.when(s < N_EARLY_BLOCKS)
        def _():
            load = pltpu.make_async_copy(full_ref.at[order_ref[s]], wbuf, load_sem)
            load.start()
            load.wait()
            z_ref[...] = jnp.dot(u_ref[...], wbuf[...], preferred_element_type=F32)

        @pl.when(s == N_DEV - 1)
        def _():
            for k in range(7):
                copy(k, me, sibling).wait_send()

    z, full = pl.pallas_call(
        body, name=name,
        grid_spec=pltpu.PrefetchScalarGridSpec(
            num_scalar_prefetch=1, grid=(N_DEV,),
            in_specs=[pl.BlockSpec((T, K), lambda s, order: (0, 0)), ANY],
            out_specs=[pl.BlockSpec((T, Nb), lambda s, order: (0, order[jnp.minimum(s, N_EARLY_BLOCKS - 1)])), ANY],
            scratch_shapes=[pltpu.VMEM((K, Nb), BF16), pltpu.SemaphoreType.DMA,
                            pltpu.SemaphoreType.DMA((7,)), pltpu.SemaphoreType.DMA((7,)), pltpu.SemaphoreType.DMA]),
        out_shape=[jax.ShapeDtypeStruct((T, N_DEV * Nb), F32), jax.ShapeDtypeStruct((N_DEV, K, Nb), BF16)],
        compiler_params=_cparams(("arbitrary",)),
    )(order, u, shard)
    return z, full


N_EARLY_BLOCKS = 6


def _mm_gathered_tail(u, full, z, order, name, after=()):
    T, K = u.shape
    _, _, Nb = full.shape
    n_after = len(after)

    def body(order_ref, u_ref, w_ref, z_in_ref, *rest):
        rest[n_after][...] = jnp.dot(u_ref[...], w_ref[...], preferred_element_type=F32)

    return pl.pallas_call(
        body, name=name,
        grid_spec=pltpu.PrefetchScalarGridSpec(
            num_scalar_prefetch=1, grid=(N_DEV - N_EARLY_BLOCKS,),
            in_specs=[pl.BlockSpec((T, K), lambda s, order: (0, 0)),
                      pl.BlockSpec((None, K, Nb), lambda s, order: (order[N_EARLY_BLOCKS + s], 0, 0)), ANY]
            + [ANY] * n_after,
            out_specs=pl.BlockSpec((T, Nb), lambda s, order: (0, order[N_EARLY_BLOCKS + s]))),
        out_shape=jax.ShapeDtypeStruct(z.shape, z.dtype),
        input_output_aliases={3: 0},
        compiler_params=_cparams(("arbitrary",)),
    )(order, u, full, z, *after)


def _gather_order():
    x, y, c = _position()
    chips = [(1 - x, y), (x, 1 - y), (1 - x, 1 - y)]
    ids = [4 * x + 2 * y + c, 4 * x + 2 * y + (1 - c)]
    ids += [4 * cx + 2 * cy + c for cx, cy in chips[:2]] + [4 * cx + 2 * cy + (1 - c) for cx, cy in chips[:2]]
    ids += [4 * chips[2][0] + 2 * chips[2][1] + c, 4 * chips[2][0] + 2 * chips[2][1] + (1 - c)]
    return jnp.stack(ids).astype(jnp.int32)


HBM = pl.BlockSpec(memory_space=pltpu.HBM)
SEM = pl.BlockSpec(memory_space=pltpu.SEMAPHORE)
DATAFLOW = pltpu.SideEffectType.DATAFLOW_SIDE_EFFECTING


def _split_call(name, bufs, waits=(), starts=None, after=()):
    nb = len(bufs)
    n_new = starts[1] if starts else 0
    wait_sems = [s for w in waits for s in (*w[1], *w[2])]

    def body(*refs):
        b, pos = refs[:nb], nb
        for plan, ss, _, send_idx, recv_idx in waits:
            k = len(ss)
            copies = plan(b, refs[pos:pos + k], refs[pos + k:pos + 2 * k])
            pos += 2 * k
            for i in recv_idx:
                copies[i].wait_recv()
            for i in send_idx:
                copies[i].wait_send()
        outs = refs[pos + len(after):]
        if starts:
            for cp in starts[0](b, outs[nb:nb + n_new], outs[nb + n_new:nb + 2 * n_new]):
                cp.start()
        outs[-1][...] = jnp.zeros_like(outs[-1])

    res = pl.pallas_call(
        body, name=name,
        out_shape=tuple(pltpu.HBM(a.shape, a.dtype) for a in bufs) + (pltpu.SemaphoreType.DMA(()),) * (2 * n_new)
        + (jax.ShapeDtypeStruct((8, 128), F32),),
        in_specs=[HBM] * nb + [SEM] * len(wait_sems) + [ANY] * len(after),
        out_specs=(HBM,) * nb + (SEM,) * (2 * n_new) + (pl.BlockSpec(memory_space=pltpu.VMEM),),
        input_output_aliases={i: i for i in range(nb)},
        compiler_params=pltpu.CompilerParams(has_side_effects=DATAFLOW),
    )(*bufs, *wait_sems, *after)
    return list(res[:nb]), list(res[nb:nb + n_new]), list(res[nb + n_new:nb + 2 * n_new]), res[-1]


def _in_hbm(a):
    return pltpu.with_memory_space_constraint(a, pltpu.HBM)


def _remote(src, dst, send_sem, recv_sem, to):
    return pltpu.make_async_remote_copy(src_ref=src, dst_ref=dst, send_sem=send_sem, recv_sem=recv_sem,
                                        device_id=to, device_id_type=MESH)


def _other_chips():
    x, y, _ = _position()
    return [(1 - x, y), (x, 1 - y), (1 - x, 1 - y)]


def _plan_gather_first(n):
    def plan(b, ss, rs):
        x, y, c = _position()
        to = [(x, y, 1 - c)] + [(*chip, c) for chip in _other_chips()]
        return [_remote(b[w], b[n + w].at[4 * x + 2 * y + c], ss[4 * w + k], rs[4 * w + k], to[k])
                for w in range(n) for k in range(4)]
    return plan, 4 * n


def _plan_gather_pass(n):
    def plan(b, ss, rs):
        x, y, c = _position()
        copies = []
        for w in range(n):
            for j, chip in enumerate(_other_chips()):
                blk = b[n + w].at[4 * chip[0] + 2 * chip[1] + c]
                copies.append(_remote(blk, blk, ss[3 * w + j], rs[3 * w + j], (x, y, 1 - c)))
        return copies
    return plan, 3 * n


def _plan_sibling(n):
    def plan(b, ss, rs):
        x, y, c = _position()
        return [_remote(b[w].at[s], b[n + w].at[s], ss[4 * w + s], rs[4 * w + s], (x, y, 1 - c))
                for w in range(n) for s in range(N_CHIP)]
    return plan, 4 * n


def _plan_scatter(n):
    def plan(b, ss, rs):
        x, y, c = _position()
        return [_remote(b[w].at[2 * chip[0] + chip[1]], b[n + w].at[2 * x + y], ss[3 * w + j], rs[3 * w + j],
                        (*chip, c))
                for w in range(n) for j, chip in enumerate(_other_chips())]
    return plan, 3 * n


class _Gather:
    def __init__(self, shards, after, name):
        self.n, self.name = len(shards), name
        x, y, c = _position()
        placed = [lax.dynamic_update_index_in_dim(lax.empty((N_DEV,) + s.shape, s.dtype), s, 4 * x + 2 * y + c, 0)
                  for s in shards]
        bufs, self.ss, self.rs, self.token = _split_call(
            name + "_start", [_in_hbm(a) for a in list(shards) + placed], starts=_plan_gather_first(self.n),
            after=after)
        self.shards, self.fulls = bufs[:self.n], bufs[self.n:]
        self.passed = {}

    def _sub(self, ids, sems, per):
        return [sems[per * w + k] for w in ids for k in range(per)]

    def pass_on(self, ids, after, tag):
        m = len(ids)
        first = (_plan_gather_first(m)[0], self._sub(ids, self.ss, 4), self._sub(ids, self.rs, 4),
                 [], [4 * i + k for i in range(m) for k in (1, 2, 3)])
        bufs, ss, rs, token = _split_call(
            "%s_pass_%s" % (self.name, tag), [self.shards[w] for w in ids] + [self.fulls[w] for w in ids],
            waits=[first], starts=_plan_gather_pass(m), after=after)
        for i, w in enumerate(ids):
            self.shards[w], self.fulls[w] = bufs[i], bufs[m + i]
        self.passed[tuple(ids)] = (ss, rs)
        return token

    def finish(self, ids, after, tag):
        m = len(ids)
        ss2, rs2 = self.passed[tuple(ids)]
        first = (_plan_gather_first(m)[0], self._sub(ids, self.ss, 4), self._sub(ids, self.rs, 4),
                 list(range(4 * m)), [4 * i for i in range(m)])
        passed = (_plan_gather_pass(m)[0], ss2, rs2, list(range(3 * m)), list(range(3 * m)))
        bufs, _, _, _ = _split_call(
            "%s_finish_%s" % (self.name, tag), [self.shards[w] for w in ids] + [self.fulls[w] for w in ids],
            waits=[first, passed], after=after)
        return bufs[m:]


class _ReduceScatter:
    def __init__(self, others, name):
        self.n, self.name = len(others), name
        lands = [lax.empty(g.shape, g.dtype) for g in others]
        self.bufs, self.ss, self.rs, self.token = _split_call(
            name + "_sibling_start", [_in_hbm(a) for a in list(others) + lands], starts=_plan_sibling(self.n))

    def from_sibling(self, after):
        n = self.n
        bufs, _, _, _ = _split_call(
            self.name + "_sibling_wait", self.bufs,
            waits=[(_plan_sibling(n)[0], self.ss, self.rs, list(range(4 * n)), list(range(4 * n)))], after=after)
        return bufs[n:]

    def scatter(self, sums):
        lands = [lax.empty(s.shape, s.dtype) for s in sums]
        self.bufs, self.ss, self.rs, token = _split_call(
            self.name + "_scatter_start", [_in_hbm(a) for a in list(sums) + lands], starts=_plan_scatter(self.n))
        return token

    def finish(self, after):
        n = self.n
        bufs, _, _, _ = _split_call(
            self.name + "_scatter_wait", self.bufs,
            waits=[(_plan_scatter(n)[0], self.ss, self.rs, list(range(3 * n)), list(range(3 * n)))], after=after)
        return bufs[:n], bufs[n:]


class _Exchanges:
    def __init__(self, parity, order):
        self.parity, self.order = parity, order


def _gather_small(packed, name):
    R = packed.shape[0]

    def body(x_ref, out_ref, send_sems, recv_sems):
        x, y, c = _position()
        me = 4 * x + 2 * y + c
        out_ref[me] = x_ref[...]
        copies = []
        for k in range(1, N_DEV):
            to = (x ^ ((k >> 2) & 1), y ^ ((k >> 1) & 1), c ^ (k & 1))
            cp = pltpu.make_async_remote_copy(
                src_ref=x_ref, dst_ref=out_ref.at[me],
                send_sem=send_sems.at[k], recv_sem=recv_sems.at[k], device_id=to, device_id_type=MESH)
            cp.start()
            copies.append((k, to, cp))
        for k, to, cp in copies:
            cp.wait_send()
            pltpu.make_async_remote_copy(
                src_ref=x_ref, dst_ref=out_ref.at[4 * to[0] + 2 * to[1] + to[2]],
                send_sem=send_sems.at[k], recv_sem=recv_sems.at[k], device_id=to, device_id_type=MESH).wait_recv()

    return pl.pallas_call(
        body, name=name,
        in_specs=[pl.BlockSpec(memory_space=pltpu.VMEM)], out_specs=pl.BlockSpec(memory_space=pltpu.VMEM),
        out_shape=jax.ShapeDtypeStruct((N_DEV, R, 128), F32),
        scratch_shapes=[pltpu.SemaphoreType.DMA((N_DEV,)), pltpu.SemaphoreType.DMA((N_DEV,))],
    )(packed)


def _adamw_math(w, g, m, v):
    m = ADAM_B1 * m + (1.0 - ADAM_B1) * g
    v = ADAM_B2 * v + (1.0 - ADAM_B2) * (g * g)
    m_hat = m / (1.0 - ADAM_B1 ** ADAM_STEP)
    v_hat = v / (1.0 - ADAM_B2 ** ADAM_STEP)
    delta = -ADAM_LR * (m_hat / (jnp.sqrt(v_hat) + ADAM_EPS) + ADAM_WD * w)
    return delta, m, v


def _adamw_big_landed(w, m, v, parts, lands, slot, name, row0=0, into=None):
    R, C = w.shape
    rows = parts.shape[1]
    tr = _pick(rows, (256,))
    first = row0 // tr
    n_into = len(into) if into else 0

    def body(slot_ref, w_ref, m_ref, v_ref, own_ref, l1_ref, l2_ref, l3_ref, *rest):
        g = own_ref[...].astype(F32)
        for ref in (l1_ref, l2_ref, l3_ref):
            g = g + ref[...].astype(F32)
        for o_ref, res in zip(rest[n_into:], (g,) + _adamw_math(w_ref[...], g, m_ref[...], v_ref[...])):
            o_ref[...] = res

    blk = pl.BlockSpec((tr, C), lambda i, slot: (first + i, 0))

    def chip(k):
        return pl.BlockSpec((None, tr, C), lambda i, slot: ((slot[0] + k) % N_CHIP, i, 0))

    out = jax.ShapeDtypeStruct((R, C), F32)
    return pl.pallas_call(
        body, name=name,
        grid_spec=pltpu.PrefetchScalarGridSpec(
            num_scalar_prefetch=1, grid=(rows // tr,),
            in_specs=[blk, blk, blk, chip(0), chip(1), chip(2), chip(3)] + [ANY] * n_into,
            out_specs=[blk, blk, blk, blk]),
        out_shape=[out, out, out, out],
        input_output_aliases={8 + j: j for j in range(n_into)},
        compiler_params=_cparams(("parallel",)),
    )(slot, w, m, v, parts, lands, lands, lands, *(into or ()))


def _adamw_small(w, m, v, gathered, name):
    R = w.shape[0]

    def body(w_ref, m_ref, v_ref, p_ref, g_ref, d_ref, nm_ref, nv_ref):
        g = p_ref[0]
        for s in range(1, N_DEV):
            g = g + p_ref[s]
        d, nm, nv = _adamw_math(w_ref[...], g, m_ref[...], v_ref[...])
        g_ref[...] = g
        d_ref[...] = d
        nm_ref[...] = nm
        nv_ref[...] = nv

    out = jax.ShapeDtypeStruct((R, 128), F32)
    return pl.pallas_call(
        body, name=name, out_shape=[out, out, out, out],
    )(w, m, v, gathered)


SMALL_NAMES = ("lb_logits", "hg_norm_w", "rel_bias", "norm_mix_w", "norm_mlp_w", "norm_final_w")
SMALL_SHAPES = {"lb_logits": (2, HG_WIDTH), "hg_norm_w": (1, HG_DK), "rel_bias": (AT_HEADS, N_REL_PAD),
                "norm_mix_w": (1, D_MODEL), "norm_mlp_w": (1, D_MODEL), "norm_final_w": (1, D_MODEL)}


def _pack_small(parts):
    rows = []
    for nme in SMALL_NAMES:
        p = parts[nme]
        if nme == "rel_bias":
            p = jnp.pad(p, ((0, 0), (0, N_REL_PAD - N_REL)))
        rows.append(p.reshape(-1, 128))
    flat = jnp.concatenate(rows, axis=0)
    return jnp.pad(flat, ((0, SMALL_ROWS - flat.shape[0]), (0, 0)))


def _unpack_small(packed):
    out, at = {}, 0
    for nme in SMALL_NAMES:
        shp = SMALL_SHAPES[nme]
        nrow = shp[0] * shp[1] // 128
        p = packed[at:at + nrow].reshape(shp)
        at += nrow
        out[nme] = p[:, :N_REL] if nme == "rel_bias" else p
    return out


BIG_NAMES = ("w_in", "w_branch_a", "w_branch_b", "w_out", "w_up", "w_down")


def kernel(x, w_in, lb_logits, hg_norm_w, rel_bias, w_branch_a, w_branch_b, w_out, norm_mix_w, norm_mlp_w, w_up, w_down, norm_final_w, loss_target, m_w_in, m_lb_logits, m_hg_norm_w, m_rel_bias, m_w_branch_a, m_w_branch_b, m_w_out, m_norm_mix_w, m_norm_mlp_w, m_w_up, m_w_down, m_norm_final_w, v_w_in, v_lb_logits, v_hg_norm_w, v_rel_bias, v_w_branch_a, v_w_branch_b, v_w_out, v_norm_mix_w, v_norm_mlp_w, v_w_up, v_w_down, v_norm_final_w):
    big_w = [w_in[0], w_branch_a[0], w_branch_b[0], w_out[0], w_up[0], w_down[0]]
    big_m = [m_w_in[0], m_w_branch_a[0], m_w_branch_b[0], m_w_out[0], m_w_up[0], m_w_down[0]]
    big_v = [v_w_in[0], v_w_branch_a[0], v_w_branch_b[0], v_w_out[0], v_w_up[0], v_w_down[0]]

    shards = [w.astype(BF16) for w in big_w]
    parity = lax.axis_index("c").astype(jnp.int32).reshape(1)
    loss_part, grad_x, chip_parts, small = _local_step(
        x[0], loss_target[0], lb_logits, hg_norm_w, rel_bias[0], norm_mix_w, norm_mlp_w,
        norm_final_w.reshape(1, D_MODEL), shards[0], shards[1:], _Exchanges(parity, _gather_order()))
    loss = lax.psum(loss_part[0, 0], ("x", "y", "c"))
    (rs_in_lo, rs_in_hi), rs_mix, rs_up, rs_down = chip_parts
    slot =(2 * lax.axis_index("x") + lax.axis_index("y")).astype(jnp.int32).reshape(1)
    big = {}

    def finish(rs, names, after):
        sums, lands = rs.finish(after)
        for nme, own, land in zip(names, sums, lands):
            i = BIG_NAMES.index(nme)
            big[nme] = _adamw_big_landed(big_w[i], big_m[i], big_v[i], own, land, slot, "adamw_" + nme)
        return [big[nme][1] for nme in names]

    done = finish(rs_down, ["w_down"], [grad_x])
    done = finish(rs_up, ["w_up"], done)
    done = finish(rs_mix, ["w_branch_a", "w_branch_b", "w_out"], done)

    sw = dict(lb_logits=lb_logits, hg_norm_w=hg_norm_w, rel_bias=rel_bias[0], norm_mix_w=norm_mix_w,
              norm_mlp_w=norm_mlp_w, norm_final_w=norm_final_w.reshape(1, D_MODEL))
    sm = dict(lb_logits=m_lb_logits, hg_norm_w=m_hg_norm_w, rel_bias=m_rel_bias[0], norm_mix_w=m_norm_mix_w,
              norm_mlp_w=m_norm_mlp_w, norm_final_w=m_norm_final_w.reshape(1, D_MODEL))
    sv = dict(lb_logits=v_lb_logits, hg_norm_w=v_hg_norm_w, rel_bias=v_rel_bias[0], norm_mix_w=v_norm_mix_w,
              norm_mlp_w=v_norm_mlp_w, norm_final_w=v_norm_final_w.reshape(1, D_MODEL))
    gathered = _gather_small(_pack_small(small), "gather_small")
    small_packed = _adamw_small(_pack_small(sw), _pack_small(sm), _pack_small(sv), gathered, "adamw_small")
    small_out = [_unpack_small(p) for p in small_packed]

    (own,), (land,) = rs_in_lo.finish(done + [small_packed[0]])
    lo = _adamw_big_landed(big_w[0], big_m[0], big_v[0], own, land, slot, "adamw_w_in_lo")
    (own,), (land,) = rs_in_hi.finish([lo[1]])
    big["w_in"] = _adamw_big_landed(big_w[0], big_m[0], big_v[0], own, land, slot, "adamw_w_in_hi",
                                    row0=D_MODEL // 2, into=lo)

    def leaf(kind, nme):
        if nme in BIG_NAMES:
            return big[nme][kind][None]
        p = small_out[kind][nme]
        if nme == "rel_bias":
            return p[None]
        if nme == "norm_final_w":
            return p.reshape(D_MODEL)
        return p

    order = ("w_in", "lb_logits", "hg_norm_w", "rel_bias", "w_branch_a", "w_branch_b", "w_out", "norm_mix_w",
             "norm_mlp_w", "w_up", "w_down", "norm_final_w")
    outs = [loss, grad_x[None]]
    for kind in range(4):
        outs += [leaf(kind, nme) for nme in order]
    return tuple(outs)
```

```python
import jax
import jax.numpy as jnp
from jax import lax
from jax.experimental import pallas as pl
from jax.experimental.pallas import tpu as pltpu

F32 = jnp.float32
BF16 = jnp.bfloat16
HIGHEST = lax.Precision.HIGHEST
MESH = pl.DeviceIdType.MESH

D_MODEL = 2048
HG_HEADS = 8
HG_DK = 128
HG_WIDTH = 1024
AT_HEADS = 16
AT_DH = 64
AT_WIDTH = 1024
CHUNK = 64
LEFT_CHUNKS = 8
BAND = (LEFT_CHUNKS + 1) * CHUNK
PAD = LEFT_CHUNKS * CHUNK
REL_CLIP = 256
N_REL = 2 * REL_CLIP + 1
N_REL_PAD = 640
D_FF = 4 * D_MODEL
EPS = 1e-6
N_DEV = 8
N_CHIP = 4

ADAM_LR = 0.001
ADAM_B1 = 0.9
ADAM_B2 = 0.999
ADAM_EPS = 1e-08
ADAM_WD = 0.01
ADAM_STEP = 10

COL_HQ, COL_HF, COL_HI, COL_HG = 0, 8, 16, 24
COL_AQ, COL_AK, COL_AV = 32, 40, 48
COL_GATE_A, COL_GATE_B = 7, 9

VMEM_LIMIT = 56 * 1024 * 1024
SMALL_ROWS = 152


def _cparams(sem=None, **kw):
    if sem is not None:
        kw["dimension_semantics"] = sem
    return pltpu.CompilerParams(vmem_limit_bytes=VMEM_LIMIT, **kw)


def _pick(n, cands):
    for c in cands:
        if n % c == 0:
            return c
    return n


def _sigmoid(x):
    return 1.0 / (1.0 + jnp.exp(-x))


ANY = pl.BlockSpec(memory_space=pl.ANY)


def _position():
    return lax.axis_index("x"), lax.axis_index("y"), lax.axis_index("c")


def _call(body, args, *, name, grid, in_specs, out_specs, out_shape, scratch_shapes=(), sem=None, after=()):
    n_in = len(args)

    def ordered(*refs):
        body(*refs[:n_in], *refs[n_in + len(after):])

    return list(pl.pallas_call(
        ordered if after else body, name=name, grid=grid, in_specs=list(in_specs) + [ANY] * len(after),
        out_specs=out_specs, out_shape=out_shape, scratch_shapes=list(scratch_shapes),
        compiler_params=_cparams(sem))(*args, *after))


MAX_CONTRACTION_TILE = 4096


def _accumulate(part, acc_ref, step, n_steps, finish):
    if n_steps == 1:
        finish(part)
        return

    @pl.when(step == 0)
    def _():
        acc_ref[...] = part

    @pl.when(step > 0)
    def _():
        acc_ref[...] += part

    @pl.when(step == n_steps - 1)
    def _():
        finish(acc_ref[...])


def _mm_nn(a, wb, out_dtype, name, after=(), epilogue=None):
    M, K = a.shape
    NB, K2, Nb = wb.shape
    assert K == K2
    tm = min(M, 1024)
    tk = min(K, MAX_CONTRACTION_TILE)
    tn = _pick(Nb, (512, 1408, 256))
    nk = K // tk
    nn = Nb // tn
    extra, first_cols, out_dtypes, fn = epilogue or ((), (), (out_dtype,), lambda total: (total,))
    n_extra, n_out = len(extra), len(out_dtypes)

    def body(a_ref, b_ref, *rest):
        def finish(total):
            results = fn(total, *[r[...] for r in rest[:n_extra]])
            for o_ref, res, dt in zip(rest[n_extra:n_extra + n_out], results, out_dtypes):
                o_ref[...] = res.astype(dt)

        part = jnp.dot(a_ref[...], b_ref[...], preferred_element_type=F32)
        _accumulate(part, rest[-1], pl.program_id(3), nk, finish)

    def tile(first):
        return pl.BlockSpec((tm, tn), lambda m, j, n, k: (m, first + j * nn + n))

    outs = _call(
        body, (a, wb) + tuple(extra), name=name, grid=(M // tm, NB, nn, nk),
        in_specs=[pl.BlockSpec((tm, tk), lambda m, j, n, k: (m, k)),
                  pl.BlockSpec((None, tk, tn), lambda m, j, n, k: (j, k, n))] + [tile(col // tn) for col in first_cols],
        out_specs=[tile(0)] * n_out,
        out_shape=[jax.ShapeDtypeStruct((M, NB * Nb), dt) for dt in out_dtypes],
        scratch_shapes=[] if nk == 1 else [pltpu.VMEM((tm, tn), F32)],
        sem=("parallel", "parallel", "parallel", "arbitrary"), after=after)
    return outs if epilogue else outs[0]


def _squared_relu(a):
    ra = jnp.maximum(a, 0.0)
    return a, ra * ra


def _gated_merge(pb, za, zb, pa):
    return pb, _sigmoid(za) * pa + _sigmoid(zb) * pb


def _mm_nt(a, wb, out_dtype, name, after=(), epilogue=None):
    M, N = a.shape
    NB, K, Nb = wb.shape
    assert N == NB * Nb
    tm = min(M, 1024)
    n_tiles_live = 1 + (len(epilogue[0]) + len(epilogue[2]) if epilogue else 0)
    tko = _pick(K, (1024,)) if n_tiles_live <= 3 else _pick(K, (512,))
    tc = _pick(Nb, (2048, 1024, 1408, 256))
    nc = Nb // tc
    jb = max([d for d in (8, 4, 2, 1) if NB % d == 0 and d * tc <= MAX_CONTRACTION_TILE]) if nc == 1 else 1
    nsteps = (NB // jb) * nc
    extra, first_cols, out_dtypes, fn = epilogue or ((), (), (out_dtype,), lambda total: (total,))
    n_extra, n_out = len(extra), len(out_dtypes)

    def body(a_ref, b_ref, *rest):
        def finish(total):
            results = fn(total, *[r[...] for r in rest[:n_extra]])
            for o_ref, res, dt in zip(rest[n_extra:n_extra + n_out], results, out_dtypes):
                o_ref[...] = res.astype(dt)

        part = sum(lax.dot_general(a_ref[:, i * tc:(i + 1) * tc], b_ref[i], (((1,), (1,)), ((), ())),
                                   preferred_element_type=F32) for i in range(jb))
        _accumulate(part, rest[-1], pl.program_id(2) * nc + pl.program_id(3), nsteps, finish)

    def tile(first):
        return pl.BlockSpec((tm, tko), lambda m, ko, j, c: (m, first + ko))

    outs = _call(
        body, (a, wb) + tuple(extra), name=name,
        grid=(M // tm, K // tko, NB // jb, nc),
        in_specs=[pl.BlockSpec((tm, jb * tc), lambda m, ko, j, c: (m, j * nc + c)),
                  pl.BlockSpec((jb, tko, tc), lambda m, ko, j, c: (j, ko, c))] + [tile(col // tko) for col in first_cols],
        out_specs=[tile(0)] * n_out,
        out_shape=[jax.ShapeDtypeStruct((M, K), dt) for dt in out_dtypes],
        scratch_shapes=[] if nsteps == 1 else [pltpu.VMEM((tm, tko), F32)],
        sem=("parallel", "parallel", "arbitrary", "arbitrary"), after=after)
    return outs if epilogue else outs[0]


ROWS_TILE = 512
ROWS_PIECE = 128


def _mm_rows(a, w, extras, vectors, row_dtypes, fn, name):
    M, K = a.shape
    N = w.shape[1]
    tm = min(M, ROWS_TILE)
    n_e, n_v = len(extras), len(vectors)

    def body(a_ref, w_ref, *rest):
        tiles, vecs, outs, product_ref = rest[:n_e], rest[n_e:n_e + n_v], rest[n_e + n_v:-1], rest[-1]
        product_ref[...] = jnp.dot(a_ref[...], w_ref[...], preferred_element_type=F32)
        for i in range(tm // ROWS_PIECE):
            piece = slice(i * ROWS_PIECE, (i + 1) * ROWS_PIECE)
            results = fn(product_ref[piece, :], *[t[piece, :] for t in tiles], *[v[...] for v in vecs])
            for o_ref, res, dt in zip(outs, results, row_dtypes):
                o_ref[piece, :] = res.astype(dt)

    row = pl.BlockSpec((tm, N), lambda m: (m, 0))
    return _call(
        body, (a, w) + tuple(extras) + tuple(vectors), name=name, grid=(M // tm,),
        in_specs=[pl.BlockSpec((tm, K), lambda m: (m, 0)), pl.BlockSpec((K, N), lambda m: (0, 0))]
        + [row] * n_e + [pl.BlockSpec((1, N), lambda m: (0, 0))] * n_v,
        out_specs=[row] * len(row_dtypes),
        out_shape=[jax.ShapeDtypeStruct((M, N), dt) for dt in row_dtypes],
        scratch_shapes=[pltpu.VMEM((tm, N), F32)], sem=("parallel",))


def _rms(h, w):
    return h * lax.rsqrt(jnp.mean(h * h, axis=-1, keepdims=True) + EPS) * w


def _residual_rms_rows(mix, x, w):
    h = x + mix
    return h, _rms(h, w)


def _mm_tn_half(a, g, which, blocks_on, add, name, after=(), a_cols=None):
    M, Ka = a.shape
    N = g.shape[1]
    first_col = 0
    if a_cols is not None:
        first_col, Ka = a_cols
    if blocks_on == "g":
        rows, cols = _pick(Ka, (1024,)), N // N_DEV
        tn = _pick(cols, (512, 1408, 256))
        nn = cols // tn
        grid = (Ka // rows, N_CHIP, nn)
        a_spec = pl.BlockSpec((M, rows), lambda ka, s, n, w: (0, first_col // rows + ka))
        g_spec = pl.BlockSpec((M, tn), lambda ka, s, n, w: (0, (2 * s + w[0]) * nn + n))
        out_rows = Ka
    else:
        rows, cols = Ka // N_DEV, N
        tn = _pick(cols, (2048, 512))
        nn = cols // tn
        grid = (1, N_CHIP, nn)
        a_spec = pl.BlockSpec((M, rows), lambda ka, s, n, w: (0, 2 * s + w[0]))
        g_spec = pl.BlockSpec((M, tn), lambda ka, s, n, w: (0, n))
        out_rows = rows
    o_spec = pl.BlockSpec((None, rows, tn), lambda ka, s, n, w: (s, ka, n))
    n_add = 0 if add is None else 1

    def body(which_ref, a_ref, g_ref, *rest):
        acc = lax.dot_general(a_ref[...], g_ref[...], (((0,), (0,)), ((), ())), preferred_element_type=F32)
        if n_add:
            acc = acc + rest[0][...].astype(F32)
        rest[-1][...] = acc.astype(BF16)

    return pl.pallas_call(
        body, name=name,
        grid_spec=pltpu.PrefetchScalarGridSpec(
            num_scalar_prefetch=1, grid=grid,
            in_specs=[a_spec, g_spec] + [o_spec] * n_add + [ANY] * len(after),
            out_specs=o_spec),
        out_shape=jax.ShapeDtypeStruct((N_CHIP, out_rows, cols), BF16),
        compiler_params=_cparams(("parallel", "parallel", "parallel")),
    )(which, a, g, *(() if add is None else (add,)), *after)


ROW_TILE = 256


def _rms_fwd(x, w, name):
    T, Dm = x.shape

    def body(x_ref, w_ref, u_ref):
        xv = x_ref[...]
        r = lax.rsqrt(jnp.mean(xv * xv, axis=-1, keepdims=True) + EPS)
        u_ref[...] = (xv * r * w_ref[...]).astype(BF16)

    return pl.pallas_call(
        body, name=name, grid=(T // ROW_TILE,),
        in_specs=[pl.BlockSpec((ROW_TILE, Dm), lambda i: (i, 0)), pl.BlockSpec((1, Dm), lambda i: (0, 0))],
        out_specs=pl.BlockSpec((ROW_TILE, Dm), lambda i: (i, 0)),
        out_shape=jax.ShapeDtypeStruct((T, Dm), BF16),
        compiler_params=_cparams(("parallel",)),
    )(x, w)


def _loss_head(h1, mlp, wf, target, name):
    T, Dm = h1.shape

    def body(h_ref, m_ref, w_ref, t_ref, loss_ref, dh_ref, dhb_ref, dw_ref):
        i = pl.program_id(0)
        h = h_ref[...] + m_ref[...]
        r = lax.rsqrt(jnp.mean(h * h, axis=-1, keepdims=True) + EPS)
        xh = h * r
        wv = w_ref[...]
        e = xh * wv - t_ref[...]
        part = 0.5 * jnp.sum(jnp.mean(e * e, axis=-1, keepdims=True), axis=0, keepdims=True)
        dy = e * (1.0 / Dm)
        dw = jnp.sum(dy * xh, axis=0, keepdims=True)
        gy = dy * wv
        dh = r * (gy - xh * jnp.mean(gy * xh, axis=-1, keepdims=True))
        dh_ref[...] = dh
        dhb_ref[...] = dh.astype(BF16)

        @pl.when(i == 0)
        def _():
            loss_ref[...] = jnp.zeros_like(loss_ref)
            dw_ref[...] = jnp.zeros_like(dw_ref)

        loss_ref[...] += jnp.broadcast_to(part, loss_ref.shape)
        dw_ref[...] += dw

    row = pl.BlockSpec((ROW_TILE, Dm), lambda i: (i, 0))
    vec = pl.BlockSpec((1, Dm), lambda i: (0, 0))
    return pl.pallas_call(
        body, name=name, grid=(T // ROW_TILE,),
        in_specs=[row, row, vec, row],
        out_specs=[pl.BlockSpec((8, 128), lambda i: (0, 0)), row, row, vec],
        out_shape=[jax.ShapeDtypeStruct((8, 128), F32), jax.ShapeDtypeStruct((T, Dm), F32),
                   jax.ShapeDtypeStruct((T, Dm), BF16), jax.ShapeDtypeStruct((1, Dm), F32)],
        compiler_params=_cparams(("arbitrary",)),
    )(h1, mlp, wf, target)


def _rms_bwd(dyn, x, w, dres, dx_dtypes, name, after=()):
    T, Dm = x.shape
    n_dx = len(dx_dtypes)

    def body(g_ref, x_ref, w_ref, r_ref, *outs):
        i = pl.program_id(0)
        xv = x_ref[...]
        r = lax.rsqrt(jnp.mean(xv * xv, axis=-1, keepdims=True) + EPS)
        xh = xv * r
        g = g_ref[...]
        dw = jnp.sum(g * xh, axis=0, keepdims=True)
        gy = g * w_ref[...]
        dx = r_ref[...] + r * (gy - xh * jnp.mean(gy * xh, axis=-1, keepdims=True))
        for dx_ref, dt in zip(outs, dx_dtypes):
            dx_ref[...] = dx.astype(dt)
        dw_ref = outs[n_dx]

        @pl.when(i == 0)
        def _():
            dw_ref[...] = jnp.zeros_like(dw_ref)

        dw_ref[...] += dw

    row = pl.BlockSpec((ROW_TILE, Dm), lambda i: (i, 0))
    vec = pl.BlockSpec((1, Dm), lambda i: (0, 0))
    return _call(
        body, (dyn, x, w, dres), name=name, grid=(T // ROW_TILE,),
        in_specs=[row, row, vec, row],
        out_specs=[row] * n_dx + [vec],
        out_shape=[jax.ShapeDtypeStruct((T, Dm), dt) for dt in dx_dtypes] + [jax.ShapeDtypeStruct((1, Dm), F32)],
        sem=("arbitrary",), after=after)


GATE_TILE = 1024


def _merge_grads(d, za, zb, pa, pb):
    ga = _sigmoid(za)
    gb = _sigmoid(zb)
    return d * ga, d * gb, d * pa * ga * (1.0 - ga), d * pb * gb * (1.0 - gb)


def _dot_hi(a, b, dims):
    return lax.dot_general(a, b, (dims, ((), ())), precision=HIGHEST, preferred_element_type=F32)


NN = ((1,), (0,))
NT = ((1,), (1,))
TN = ((0,), (0,))


def _hg_gates(hq, hf, lb):
    sq = _sigmoid(hq)
    q = hq * sq * (HG_DK ** -0.5)
    f = _sigmoid(hf)
    g = lb + (1.0 - lb) * f
    return q, sq, f, g, jnp.log(g), 1.0 - g


def _tri(lower):
    r = lax.broadcasted_iota(jnp.int32, (CHUNK, CHUNK), 0)
    c = lax.broadcasted_iota(jnp.int32, (CHUNK, CHUNK), 1)
    return jnp.where((r >= c) if lower else (r <= c), 1.0, 0.0).astype(F32)


GROUP = 16
N_GROUPS = CHUNK // GROUP
BWD_CHUNKS_PER_TRIP = 4


def _dot_bf16(a, b, dims):
    return lax.dot_general(a.astype(BF16), b.astype(BF16), (dims, ((), ())), preferred_element_type=F32)


def _rows_iota():
    return lax.broadcasted_iota(jnp.int32, (CHUNK, HG_DK), 0)


def _by_query_group(q, kk, b, g):
    r0 = GROUP * g
    b0 = b[r0:r0 + 1]
    decay = jnp.exp(b[r0:r0 + GROUP] - b0)
    ks = jnp.where(_rows_iota() < r0, kk * jnp.exp(jnp.minimum(b0 - b, 0.0)), 0.0)
    return q[r0:r0 + GROUP] * decay, ks, decay


def _by_key_group(q, kk, b, j):
    r1 = GROUP * (j + 1)
    b1 = b[r1 - 1:r1]
    decay = jnp.exp(b1 - b[r1 - GROUP:r1])
    qs = jnp.where(_rows_iota() >= r1, q * jnp.exp(jnp.minimum(b - b1, 0.0)), 0.0)
    return qs, kk[r1 - GROUP:r1] * decay, decay


def _scores_between_groups(q, kk, b):
    blocks = [jnp.zeros((GROUP, CHUNK), F32)]
    for g in range(1, N_GROUPS):
        qs, ks, _ = _by_query_group(q, kk, b, g)
        blocks.append(_dot_bf16(qs, ks, NT))
    return jnp.concatenate(blocks, axis=0)


def _hgrn2_fwd(z, lb_logits, hg_norm_w, name, after=()):
    T = z.shape[0]
    n_chunks = T // CHUNK

    def body(hq_ref, hf_ref, hi_ref, hg_ref, lbl_ref, nw_ref, o_ref, ya_ref, sall_ref, st_ref):
        lbl = lbl_ref[...]
        lb = 1.0 / (1.0 + jnp.exp(lbl[1:2, :] - lbl[0:1, :]))
        st_ref[...] = jnp.zeros_like(st_ref)
        tri = _tri(True)
        row8 = lax.broadcasted_iota(jnp.int32, (8, HG_DK), 0)

        def chunk(c, carry):
            rows = pl.ds(pl.multiple_of(c * CHUNK, CHUNK), CHUNK)
            q, _, _, _, lg, kk = _hg_gates(hq_ref[rows, :], hf_ref[rows, :], lb)
            v = hi_ref[rows, :]
            b = _dot_hi(tri, lg, NN)
            st = st_ref[...]
            sall_ref[c] = st
            for grp in range(N_GROUPS):
                r0 = GROUP * grp
                for h8 in range(GROUP // 8):
                    n = 8 * (h8 + 1)
                    bs, ks, vs = b[r0:r0 + n], kk[r0:r0 + n], v[r0:r0 + n]
                    sidx = lax.broadcasted_iota(jnp.int32, (n, HG_DK), 0)
                    blk = jnp.zeros((8, HG_DK), F32)
                    for i in range(8):
                        t = r0 + 8 * h8 + i
                        e = jnp.where(sidx <= 8 * h8 + i, jnp.exp(b[t:t + 1] - bs), 0.0)
                        p = jnp.sum(e * ks * q[t:t + 1], axis=1, keepdims=True)
                        ot = jnp.sum(p * vs, axis=0, keepdims=True)
                        blk = blk + jnp.where(row8 == i, ot, 0.0)
                    o_ref[pl.ds(pl.multiple_of(c * CHUNK + r0 + 8 * h8, 8), 8), :] = blk
            o_ref[rows, :] += _dot_hi(q * jnp.exp(b), st, NT) + _dot_bf16(_scores_between_groups(q, kk, b), v, NN)
            bl = b[CHUNK - 1:CHUNK]
            ke = kk * jnp.exp(bl - b)
            st_ref[...] = st * jnp.exp(bl) + _dot_hi(v, ke, TN)
            return carry

        lax.fori_loop(0, n_chunks, chunk, 0, unroll=2)
        o = o_ref[...]
        r = lax.rsqrt(jnp.mean(o * o, axis=-1, keepdims=True) + EPS)
        hg = hg_ref[...]
        ya_ref[...] = (o * r * nw_ref[...] * (hg * _sigmoid(hg))).astype(BF16)

    def col(base):
        return pl.BlockSpec((T, HG_DK), lambda h: (0, base + h))

    return _call(
        body, (z, z, z, z, lb_logits, hg_norm_w), name=name, grid=(HG_HEADS,),
        in_specs=[col(COL_HQ), col(COL_HF), col(COL_HI), col(COL_HG),
                  pl.BlockSpec((2, HG_DK), lambda h: (0, h)), pl.BlockSpec((1, HG_DK), lambda h: (0, 0))],
        out_specs=[col(0), col(0), pl.BlockSpec((None, n_chunks, HG_DK, HG_DK), lambda h: (h, 0, 0, 0))],
        out_shape=[jax.ShapeDtypeStruct((T, HG_WIDTH), F32), jax.ShapeDtypeStruct((T, HG_WIDTH), BF16),
                   jax.ShapeDtypeStruct((HG_HEADS, n_chunks, HG_DK, HG_DK), F32)],
        scratch_shapes=[pltpu.VMEM((HG_DK, HG_DK), F32)],
        sem=("parallel",), after=after)


def _hgrn2_bwd(z, lb_logits, hg_norm_w, o_raw, s_all, dya, name, after=()):
    T = z.shape[0]
    n_chunks = T // CHUNK

    def body(hq_ref, hf_ref, hi_ref, hg_ref, lbl_ref, nw_ref, o_ref, sall_ref, dya_ref,
             dhq_ref, dhf_ref, dhi_ref, dhg_ref, dlbl_ref, dnw_ref,
             do_ref, dst_ref, dlb_ref, *per_chunk):
        h = pl.program_id(0)
        lbl = lbl_ref[...]
        lb = 1.0 / (1.0 + jnp.exp(lbl[1:2, :] - lbl[0:1, :]))

        o = o_ref[...]
        r = lax.rsqrt(jnp.mean(o * o, axis=-1, keepdims=True) + EPS)
        oh = o * r
        nw = nw_ref[...]
        hg = hg_ref[...]
        sg = _sigmoid(hg)
        dy = dya_ref[...]
        d_on = dy * (hg * sg)
        dhg_ref[...] = (dy * (oh * nw) * (sg * (1.0 + hg * (1.0 - sg)))).astype(BF16)
        dnw = jnp.sum(d_on * oh, axis=0, keepdims=True)
        gy = d_on * nw
        do_ref[...] = r * (gy - oh * jnp.mean(gy * oh, axis=-1, keepdims=True))

        @pl.when(h == 0)
        def _():
            dnw_ref[...] = jnp.zeros_like(dnw_ref)

        dnw_ref[...] += jnp.broadcast_to(dnw, dnw_ref.shape)

        dst_ref[...] = jnp.zeros_like(dst_ref)
        dlb_ref[...] = jnp.zeros_like(dlb_ref)
        tri = _tri(True)
        tri_t = _tri(False)
        row8 = lax.broadcasted_iota(jnp.int32, (8, HG_DK), 0)
        row_group = lax.broadcasted_iota(jnp.int32, (CHUNK, CHUNK), 0) // GROUP
        col_group = lax.broadcasted_iota(jnp.int32, (CHUNK, CHUNK), 1) // GROUP
        earlier_group = col_group < row_group
        later_group = col_group > row_group

        def chunk(c, dq_ref, dk_ref, dv_ref):
            rows = pl.ds(pl.multiple_of(c * CHUNK, CHUNK), CHUNK)
            hq = hq_ref[rows, :]
            q, sq, f, g, lg, kk = _hg_gates(hq, hf_ref[rows, :], lb)
            v = hi_ref[rows, :]
            do = do_ref[rows, :]
            b = _dot_hi(tri, lg, NN)
            eb = jnp.exp(b)
            bl = b[CHUNK - 1:CHUNK]
            ebl = jnp.exp(bl)
            ekb = jnp.exp(bl - b)
            qe = q * eb
            ke = kk * ekb
            st = sall_ref[c]
            dst = dst_ref[...]
            dqe = _dot_hi(do, st, NN)
            dke = _dot_hi(v, dst, NN)
            dv_inter = _dot_hi(ke, dst, NT)
            d_ebl = jnp.sum(st * dst, axis=0, keepdims=True)
            dst_ref[...] = dst * ebl + _dot_hi(do, qe, TN)

            dk_ref[...] = jnp.zeros_like(dk_ref)
            dv_ref[...] = jnp.zeros_like(dv_ref)
            for grp in range(N_GROUPS):
                r0 = GROUP * grp
                for h8 in range(GROUP // 8):
                    n = 8 * (h8 + 1)
                    bs, ks, vs = b[r0:r0 + n], kk[r0:r0 + n], v[r0:r0 + n]
                    sidx = lax.broadcasted_iota(jnp.int32, (n, HG_DK), 0)
                    blk = jnp.zeros((8, HG_DK), F32)
                    for i in range(8):
                        t = r0 + 8 * h8 + i
                        qt = q[t:t + 1]
                        dot_ = do[t:t + 1]
                        e = jnp.where(sidx <= 8 * h8 + i, jnp.exp(b[t:t + 1] - bs), 0.0)
                        w = e * ks
                        p = jnp.sum(w * qt, axis=1, keepdims=True)
                        dsc = jnp.sum(vs * dot_, axis=1, keepdims=True)
                        dqt = jnp.sum(dsc * w, axis=0, keepdims=True)
                        blk = blk + jnp.where(row8 == i, dqt, 0.0)
                        dk_ref[r0:r0 + n, :] += dsc * e * qt
                        dv_ref[r0:r0 + n, :] += p * dot_
                    dq_ref[r0 + 8 * h8:r0 + n, :] = blk
            ds_far = jnp.where(earlier_group, _dot_bf16(do, v, NT), 0.0)
            ds_far_t = jnp.where(later_group, _dot_bf16(v, do, NT), 0.0)
            dq_far, dk_far = [jnp.zeros((GROUP, HG_DK), F32)], []
            for grp in range(1, N_GROUPS):
                r0 = GROUP * grp
                _, ks, decay = _by_query_group(q, kk, b, grp)
                dq_far.append(decay * _dot_hi(ds_far[r0:r0 + GROUP], ks, NN))
                qs, _, decay = _by_key_group(q, kk, b, grp - 1)
                dk_far.append(decay * _dot_hi(ds_far_t[r0 - GROUP:r0], qs, NN))
            dk_far.append(jnp.zeros((GROUP, HG_DK), F32))
            dv_far = _dot_bf16(_scores_between_groups(q, kk, b), do, TN)
            dq_i = dq_ref[...] + jnp.concatenate(dq_far, axis=0)
            dk_i = dk_ref[...] + jnp.concatenate(dk_far, axis=0)
            dke_ke = dke * ke
            db = q * dq_i - kk * dk_i + dqe * qe - dke_ke
            db_last = jnp.sum(dke_ke, axis=0, keepdims=True) + d_ebl * ebl
            dlg = _dot_hi(tri_t, db, NN) + db_last
            dq = dq_i + dqe * eb
            dkk = dk_i + dke * ekb
            dg = dlg / g - dkk
            dhq_ref[rows, :] = (dq * (HG_DK ** -0.5) * (sq * (1.0 + hq * (1.0 - sq)))).astype(BF16)
            dhf_ref[rows, :] = (dg * (1.0 - lb) * f * (1.0 - f)).astype(BF16)
            dhi_ref[rows, :] = (dv_ref[...] + dv_far + dv_inter).astype(BF16)
            dlb_ref[...] += jnp.sum(dg * (1.0 - f), axis=0, keepdims=True)

        def trip(i, carry):
            for k in range(BWD_CHUNKS_PER_TRIP):
                chunk(n_chunks - 1 - k - BWD_CHUNKS_PER_TRIP * i, *per_chunk[3 * k:3 * k + 3])
            return carry

        lax.fori_loop(0, n_chunks // BWD_CHUNKS_PER_TRIP, trip, 0)
        dl0 = dlb_ref[...] * lb * (1.0 - lb)
        dlbl_ref[0:1, :] = dl0
        dlbl_ref[1:2, :] = -dl0

    def col(base):
        return pl.BlockSpec((T, HG_DK), lambda h: (0, base + h))

    outb = jax.ShapeDtypeStruct((T, HG_WIDTH), BF16)
    return _call(
        body, (z, z, z, z, lb_logits, hg_norm_w, o_raw, s_all, dya), name=name, grid=(HG_HEADS,),
        in_specs=[col(COL_HQ), col(COL_HF), col(COL_HI), col(COL_HG),
                  pl.BlockSpec((2, HG_DK), lambda h: (0, h)), pl.BlockSpec((1, HG_DK), lambda h: (0, 0)),
                  col(0), pl.BlockSpec((None, n_chunks, HG_DK, HG_DK), lambda h: (h, 0, 0, 0)), col(0)],
        out_specs=[col(0), col(0), col(0), col(0), pl.BlockSpec((2, HG_DK), lambda h: (0, h)),
                   pl.BlockSpec((8, HG_DK), lambda h: (0, 0))],
        out_shape=[outb, outb, outb, outb, jax.ShapeDtypeStruct((2, HG_WIDTH), F32),
                   jax.ShapeDtypeStruct((8, HG_DK), F32)],
        scratch_shapes=[pltpu.VMEM((T, HG_DK), F32), pltpu.VMEM((HG_DK, HG_DK), F32), pltpu.VMEM((1, HG_DK), F32)]
        + [pltpu.VMEM((CHUNK, HG_DK), F32)] * (3 * BWD_CHUNKS_PER_TRIP),
        sem=("arbitrary",), after=after)


CONST_KEYS = PAD - REL_CLIP
VAR_KEYS = BAND - CONST_KEYS
REL_LO = 128
REL_SPAN = N_REL_PAD - REL_LO


def _rel_onehot(t):
    r = lax.broadcasted_iota(jnp.int32, (REL_SPAN, VAR_KEYS), 0)
    j = lax.broadcasted_iota(jnp.int32, (REL_SPAN, VAR_KEYS), 1)
    idx = jnp.clip(t + PAD - CONST_KEYS - j, -REL_CLIP, REL_CLIP) + REL_CLIP - REL_LO
    return jnp.where(r == idx, 1.0, 0.0).astype(BF16)


def _split3(x):
    hi = x.astype(BF16)
    r1 = x - hi.astype(F32)
    mid = r1.astype(BF16)
    return hi, mid, (r1 - mid.astype(F32)).astype(BF16)


def _bias_expand(rel, name):
    def body(rel_ref, out_ref):
        tab = rel_ref[...]
        onehot = _rel_onehot(pl.program_id(0))
        out_ref[:, 0:CONST_KEYS] = jnp.broadcast_to(tab[:, 2 * REL_CLIP:2 * REL_CLIP + 1], (AT_HEADS, CONST_KEYS))
        out_ref[:, CONST_KEYS:BAND] = sum(
            jnp.dot(piece, onehot, preferred_element_type=F32) for piece in _split3(tab[:, REL_LO:N_REL_PAD]))

    return pl.pallas_call(
        body, name=name, grid=(CHUNK,),
        in_specs=[pl.BlockSpec((AT_HEADS, N_REL_PAD), lambda t: (0, 0))],
        out_specs=pl.BlockSpec((None, AT_HEADS, BAND), lambda t: (t, 0, 0)),
        out_shape=jax.ShapeDtypeStruct((CHUNK, AT_HEADS, BAND), F32),
        compiler_params=_cparams(("parallel",)),
    )(rel)


def _bias_reduce(dbias_rows, name, after=()):
    def body(db_ref, out_ref):
        lane = lax.broadcasted_iota(jnp.int32, (AT_HEADS, N_REL_PAD), 1)
        varying = lane >= CONST_KEYS
        by_offset = jnp.zeros((AT_HEADS, N_REL_PAD), F32)
        constant = jnp.zeros((AT_HEADS, N_REL_PAD), F32)
        for t in range(CHUNK):
            row = db_ref[t]
            constant = constant + jnp.where(varying, 0.0, row)
            moved = jnp.where(varying, row, 0.0)
            by_offset = by_offset + (pltpu.roll(moved, N_REL_PAD - t, axis=1) if t else moved)
        offset = lax.broadcasted_iota(jnp.int32, (N_REL_PAD, N_REL_PAD), 0)
        entry = lax.broadcasted_iota(jnp.int32, (N_REL_PAD, N_REL_PAD), 1)
        onehot = jnp.where(entry == jnp.clip(PAD - offset, -REL_CLIP, REL_CLIP) + REL_CLIP, 1.0, 0.0).astype(BF16)
        acc = sum(jnp.dot(piece, onehot, preferred_element_type=F32) for piece in _split3(by_offset))
        last = jnp.sum(constant, axis=1, keepdims=True)
        out_ref[...] = acc + jnp.where(lane == 2 * REL_CLIP, last, 0.0)

    whole = pl.BlockSpec((CHUNK, AT_HEADS, N_REL_PAD), lambda i: (0, 0, 0))
    return _call(
        body, (dbias_rows,), name=name, grid=(1,), in_specs=[whole],
        out_specs=[pl.BlockSpec((AT_HEADS, N_REL_PAD), lambda i: (0, 0))],
        out_shape=[jax.ShapeDtypeStruct((AT_HEADS, N_REL_PAD), F32)],
        sem=("arbitrary",), after=after)[0]


def _pair_lanes():
    return lax.broadcasted_iota(jnp.int32, (CHUNK, 2 * AT_DH), 1) < AT_DH


def _block_diag(a):
    first = _pair_lanes()
    return jnp.concatenate([jnp.where(first, a, 0.0), jnp.where(first, 0.0, a)], axis=0).astype(BF16)


def _diag_blocks(a):
    return jnp.where(_pair_lanes(), a[:CHUNK], a[CHUNK:])


def _band_probs_t(kb, qbd, bias_t, c):
    s = lax.dot_general(kb, qbd, (NT, ((), ())), preferred_element_type=F32) * (AT_DH ** -0.5) + bias_t
    j = lax.broadcasted_iota(jnp.int32, (BAND, 2 * AT_DH), 0)
    s = jnp.where(j + c * CHUNK >= PAD, s, -jnp.inf)
    p = jnp.exp(s - jnp.max(s, axis=0, keepdims=True))
    return p / jnp.sum(p, axis=0, keepdims=True)


def _attn_fwd(z, bias_t, name, after=()):
    T = z.shape[0]
    n_chunks = T // CHUNK

    def body(q_ref, k_ref, v_ref, bias_ref, y_ref, p_ref, *scratch):
        for pr in range(2):
            lanes = slice(128 * pr, 128 * (pr + 1))
            for dst_ref, src_ref in zip(scratch[2 * pr:2 * pr + 2], (k_ref, v_ref)):
                dst_ref[0:PAD, :] = jnp.zeros((PAD, 128), BF16)
                dst_ref[PAD:PAD + T, :] = src_ref[:, lanes].astype(BF16)

        def chunk(c, carry):
            rows = pl.ds(pl.multiple_of(c * CHUNK, CHUNK), CHUNK)
            band = pl.ds(pl.multiple_of(c * CHUNK, CHUNK), BAND)
            for pr in range(2):
                kp_ref, vp_ref = scratch[2 * pr:2 * pr + 2]
                lanes = slice(128 * pr, 128 * (pr + 1))
                p = _band_probs_t(kp_ref[band, :], _block_diag(q_ref[rows, lanes]), bias_ref[pr], c).astype(BF16)
                p_ref[pr, c] = p
                o2 = lax.dot_general(p, vp_ref[band, :], (TN, ((), ())), preferred_element_type=F32)
                y_ref[rows, lanes] = _diag_blocks(o2).astype(BF16)
            return carry

        lax.fori_loop(0, n_chunks, chunk, 0, unroll=2)

    def col(base):
        return pl.BlockSpec((T, 256), lambda h: (0, base // 2 + h))

    return _call(
        body, (z, z, z, bias_t), name=name, grid=(AT_HEADS // 4,),
        in_specs=[col(COL_AQ), col(COL_AK), col(COL_AV), pl.BlockSpec((2, BAND, 128), lambda h: (h, 0, 0))],
        out_specs=[col(0), pl.BlockSpec((2, n_chunks, BAND, 128), lambda h: (h, 0, 0, 0))],
        out_shape=[jax.ShapeDtypeStruct((T, AT_WIDTH), BF16),
                   jax.ShapeDtypeStruct((AT_HEADS // 2, n_chunks, BAND, 128), BF16)],
        scratch_shapes=[pltpu.VMEM((PAD + T, 128), BF16)] * 4,
        sem=("parallel",), after=after)


def _attn_bwd(z, probs, dyb, name, after=()):
    T = z.shape[0]
    n_chunks = T // CHUNK

    def body(q_ref, k_ref, v_ref, p_ref, dy_ref, dq_ref, dk_ref, dv_ref, dbias_ref, *scratch):
        dbias_ref[...] = jnp.zeros_like(dbias_ref)
        for pr in range(2):
            kp_ref, vp_ref, dkp_ref, dvp_ref = scratch[4 * pr:4 * pr + 4]
            lanes = slice(128 * pr, 128 * (pr + 1))
            kp_ref[0:PAD, :] = jnp.zeros((PAD, 128), BF16)
            vp_ref[0:PAD, :] = jnp.zeros((PAD, 128), BF16)
            kp_ref[PAD:PAD + T, :] = k_ref[:, lanes].astype(BF16)
            vp_ref[PAD:PAD + T, :] = v_ref[:, lanes].astype(BF16)
            dkp_ref[...] = jnp.zeros_like(dkp_ref)
            dvp_ref[...] = jnp.zeros_like(dvp_ref)

        def chunk(c, carry):
            rows = pl.ds(pl.multiple_of(c * CHUNK, CHUNK), CHUNK)
            band = pl.ds(pl.multiple_of(c * CHUNK, CHUNK), BAND)
            for pr in range(2):
                kp_ref, vp_ref, dkp_ref, dvp_ref = scratch[4 * pr:4 * pr + 4]
                lanes = slice(128 * pr, 128 * (pr + 1))
                qbd = _block_diag(q_ref[rows, lanes])
                dobd = _block_diag(dy_ref[rows, lanes])
                pb = p_ref[pr, c]
                p = pb.astype(F32)
                dp = lax.dot_general(vp_ref[band, :], dobd, (NT, ((), ())), preferred_element_type=F32)
                ds = p * (dp - jnp.sum(dp * p, axis=0, keepdims=True))
                dbias_ref[pr] += ds
                dsb = ds.astype(BF16)
                dq2 = lax.dot_general(dsb, kp_ref[band, :], (TN, ((), ())), preferred_element_type=F32)
                dq_ref[rows, lanes] = (_diag_blocks(dq2) * (AT_DH ** -0.5)).astype(BF16)
                dkp_ref[band, :] += jnp.dot(dsb, qbd, preferred_element_type=F32) * (AT_DH ** -0.5)
                dvp_ref[band, :] += jnp.dot(pb, dobd, preferred_element_type=F32)
            return carry

        lax.fori_loop(0, n_chunks, chunk, 0)
        for pr in range(2):
            lanes = slice(128 * pr, 128 * (pr + 1))
            dk_ref[:, lanes] = scratch[4 * pr + 2][PAD:PAD + T, :].astype(BF16)
            dv_ref[:, lanes] = scratch[4 * pr + 3][PAD:PAD + T, :].astype(BF16)

    def col(base):
        return pl.BlockSpec((T, 256), lambda h: (0, base // 2 + h))

    outb = jax.ShapeDtypeStruct((T, AT_WIDTH), BF16)
    return _call(
        body, (z, z, z, probs, dyb), name=name, grid=(AT_HEADS // 4,),
        in_specs=[col(COL_AQ), col(COL_AK), col(COL_AV),
                  pl.BlockSpec((2, n_chunks, BAND, 128), lambda h: (h, 0, 0, 0)), col(0)],
        out_specs=[col(0), col(0), col(0), pl.BlockSpec((2, BAND, 128), lambda h: (h, 0, 0))],
        out_shape=[outb, outb, outb, jax.ShapeDtypeStruct((AT_HEADS // 2, BAND, 128), F32)],
        scratch_shapes=([pltpu.VMEM((PAD + T, 128), BF16)] * 2 + [pltpu.VMEM((PAD + T, 128), F32)] * 2) * 2,
        sem=("parallel",), after=after)


def _local_step(x, target, lb_logits, hg_norm_w, rel_bias, norm_mix_w, norm_mlp_w, norm_final_w,
                w_in, rest, exchanges=None):
    ex = exchanges
    rel = jnp.pad(rel_bias, ((0, 0), (0, N_REL_PAD - N_REL)))

    u = _rms_fwd(x, norm_mix_w, "rms_mix_fwd")
    if ex:
        z, w_in = _mm_gathered(u, w_in, ex.order, "mm_in_fwd")
        gather = _Gather(rest, [w_in], "ag")
        z = _mm_gathered_tail(u, w_in, z, ex.order, "mm_in_fwd_tail", after=[gather.token])
        tok = []
    else:
        z = _mm_nn(u, w_in, F32, "mm_in_fwd")
        w_a, w_b, w_out, w_up, w_down = rest
        tok = []
    o_raw, y_a, s_all = _hgrn2_fwd(z, lb_logits, hg_norm_w, "hgrn2_fwd", after=tok)
    if ex:
        tok = [gather.pass_on([0, 1, 2], [o_raw], "abo")]
    bias_rows = _bias_expand(rel, "bias_expand")
    bias_t = jnp.transpose(bias_rows.reshape(CHUNK, AT_HEADS // 2, 2, BAND), (1, 3, 2, 0)).reshape(
        AT_HEADS // 2, BAND, 2 * CHUNK)
    y_b, probs = _attn_fwd(z, bias_t, "attn_fwd", after=tok)
    if ex:
        tok = [gather.pass_on([3], [y_b], "up")]
        w_a, w_b, w_out = gather.finish([0, 1, 2], tok, "abo")
    pa = _mm_nn(y_a, w_a, F32, "mm_a_fwd")
    pb, merged = _mm_nn(y_b, w_b, None, "mm_b_fwd", epilogue=(
        (z, z, pa), (COL_GATE_A * GATE_TILE, COL_GATE_B * GATE_TILE, 0), (F32, BF16), _gated_merge))
    w_out1 = w_out.reshape(1, D_MODEL, D_MODEL)
    h1, u2 = _mm_rows(merged, w_out.reshape(D_MODEL, D_MODEL), [x], [norm_mlp_w], (F32, BF16),
                      _residual_rms_rows, "mm_out_fwd")
    if ex:
        tok = [gather.pass_on([4], [u2], "down")]
        w_up, = gather.finish([3], tok, "up")
    a, r = _mm_nn(u2, w_up, None, "mm_up_fwd", epilogue=((), (), (F32, BF16), _squared_relu))
    if ex:
        w_down, = gather.finish([4], [r], "down")
    w_down1 = w_down.reshape(1, D_FF, D_MODEL)
    mlp = _mm_nn(r, w_down1, F32, "mm_down_fwd")
    loss, dh2, dh2b, g_nf = _loss_head(h1, mlp, norm_final_w, target, "loss_head")

    own = ex.parity if ex else jnp.zeros((1,), jnp.int32)

    def sibling_half(weights, name, after=()):
        others = [_mm_tn_half(a_, g_, 1 - own, on, None, nm + "_sibling", after, *cols)
                  for a_, g_, on, nm, *cols in weights]
        rs = _ReduceScatter(others, name) if ex else None
        return rs, others, ([rs.token] if ex else [])

    def own_half(rs, weights, others, after):
        landed = rs.from_sibling(after) if ex else [None] * len(weights)
        sums = [_mm_tn_half(a_, g_, own, on, l, nm + "_own", (), *cols)
                for (a_, g_, on, nm, *cols), l in zip(weights, landed)]
        if ex:
            return [rs.scatter(sums)], None
        return [], [jnp.stack([s_, o_], axis=1).reshape((N_DEV,) + s_.shape[1:]) for s_, o_ in zip(sums, others)]

    down = [(r, dh2b, "a", "mm_down_wgrad")]
    rs_down, others, tok = sibling_half(down, "rs_down")
    da, = _mm_nt(dh2b, w_down1, None, "mm_down_dgrad", after=tok, epilogue=(
        (a,), (0,), (BF16,), lambda dr, av: (dr * (2.0 * jnp.maximum(av, 0.0)),)))
    tok, g_down = own_half(rs_down, down, others, [da])
    up = [(u2, da, "g", "mm_up_wgrad")]
    rs_up, others, tok = sibling_half(up, "rs_up", tok)
    du2 = _mm_nt(da, w_up, F32, "mm_up_dgrad", after=tok)
    tok, g_up = own_half(rs_up, up, others, [du2])
    dh1, dh1b, g_nmlp = _rms_bwd(du2, h1, norm_mlp_w, dh2, (F32, BF16), "rms_mlp_bwd", after=tok)

    dpa, dpb, dga, dgb = _mm_nt(dh1b, w_out1, None, "mm_out_dgrad", epilogue=(
        (z, z, pa, pb), (COL_GATE_A * GATE_TILE, COL_GATE_B * GATE_TILE, 0, 0), (BF16,) * 4, _merge_grads))
    mix = [(y_a, dpa, "g", "mm_a_wgrad"), (y_b, dpb, "g", "mm_b_wgrad"), (merged, dh1b, "a", "mm_out_wgrad")]
    rs_mix, others, tok = sibling_half(mix, "rs_mix")
    dya = _mm_nt(dpa, w_a, F32, "mm_a_dgrad", after=tok)
    dyb = _mm_nt(dpb, w_b, F32, "mm_b_dgrad", after=tok)
    tok, g_mix = own_half(rs_mix, mix, others, [dya, dyb])
    daq, dak, dav, dbias_t = _attn_bwd(z, probs, dyb, "attn_bwd", after=tok)
    dhq, dhf, dhi, dhg, g_lbl, g_hgw = _hgrn2_bwd(z, lb_logits, hg_norm_w, o_raw, s_all, dya, "hgrn2_bwd",
                                                  after=tok)
    dbias_rows = jnp.pad(jnp.transpose(dbias_t.reshape(AT_HEADS // 2, BAND, 2, CHUNK), (3, 0, 2, 1)).reshape(
        CHUNK, AT_HEADS, BAND), ((0, 0), (0, 0), (0, N_REL_PAD - BAND)))
    dz = jnp.concatenate([dhq, dhf, dhi, dhg, daq, dak, dav, dga, dgb], axis=1)
    half = D_MODEL // 2
    lo = [(u, dz, "g", "mm_in_wgrad_lo", (0, half))]
    hi = [(u, dz, "g", "mm_in_wgrad_hi", (half, half))]
    rs_in_lo, others_lo, tok = sibling_half(lo, "rs_in_lo")
    rs_in_hi, others_hi, tok = sibling_half(hi, "rs_in_hi", tok)
    tok, g_in_lo = own_half(rs_in_lo, lo, others_lo, tok)
    du = _mm_nt(dz, w_in, F32, "mm_in_dgrad", after=tok)
    tok, g_in_hi = own_half(rs_in_hi, hi, others_hi, [du])
    grad_x, g_nmix = _rms_bwd(du, x, norm_mix_w, dh1, (F32,), "rms_mix_bwd", after=tok)
    g_rel = _bias_reduce(dbias_rows, "bias_reduce", after=tok)[:, :N_REL]

    small = dict(lb_logits=g_lbl, hg_norm_w=g_hgw[0:1], rel_bias=g_rel, norm_mix_w=g_nmix, norm_mlp_w=g_nmlp,
                 norm_final_w=g_nf)
    if ex:
        grads = [(rs_in_lo, rs_in_hi), rs_mix, rs_up, rs_down]
    else:
        grads = [jnp.concatenate([g_in_lo[0], g_in_hi[0]], axis=1)] + g_mix + [g_up[0], g_down[0]]
    return loss, grad_x, grads, small


def _mm_gathered(u, shard, order, name):
    T, K = u.shape
    _, Nb = shard.shape

    def body(order_ref, u_ref, shard_ref, z_ref, full_ref, wbuf, load_sem, send_sems, recv_sems, local_sem):
        s = pl.program_id(0)
        x, y, c = _position()
        me, sibling = (x, y, c), (x, y, 1 - c)
        chips = [(1 - x, y), (x, 1 - y), (1 - x, 1 - y)]

        def copy(k, block, to, src=None):
            dst = full_ref.at[4 * block[0] + 2 * block[1] + block[2]]
            return pltpu.make_async_remote_copy(
                src_ref=dst if src is None else src, dst_ref=dst,
                send_sem=send_sems.at[k], recv_sem=recv_sems.at[k], device_id=to, device_id_type=MESH)

        @pl.when(s == 0)
        def _():
            local = pltpu.make_async_copy(shard_ref, full_ref.at[4 * x + 2 * y + c], local_sem)
            local.start()
            copy(0, me, sibling, src=shard_ref).start()
            for j, chip in enumerate(chips):
                copy(1 + j, me, (*chip, c), src=shard_ref).start()
            local.wait()

        @pl.when(s == 1)
        def _():
            copy(0, sibling, me).wait_recv()

        for j, chip in enumerate(chips):
            direct, passed = ((2, 4), (3, 5), (6, 7))[j]

            @pl.when(s == direct)
            def _(j=j, chip=chip):
                copy(1 + j, (*chip, c), me).wait_recv()
                copy(4 + j, (*chip, c), sibling).start()

            @pl.when(s == passed)
            def _(j=j, chip=chip):
                copy(4 + j, (*chip, 1 - c), me).wait_recv()

        @pl.when(s < N_EARLY_BLOCKS)
        def _():
            load = pltpu.make_async_copy(full_ref.at[order_ref[s]], wbuf, load_sem)
            load.start()
            load.wait()
            z_ref[...] = jnp.dot(u_ref[...], wbuf[...], preferred_element_type=F32)

        @pl.when(s == N_DEV - 1)
        def _():
            for k in range(7):
                copy(k, me, sibling).wait_send()

    z, full = pl.pallas_call(
        body, name=name,
        grid_spec=pltpu.PrefetchScalarGridSpec(
            num_scalar_prefetch=1, grid=(N_DEV,),
            in_specs=[pl.BlockSpec((T, K), lambda s, order: (0, 0)), ANY],
            out_specs=[pl.BlockSpec((T, Nb), lambda s, order: (0, order[jnp.minimum(s, N_EARLY_BLOCKS - 1)])), ANY],
            scratch_shapes=[pltpu.VMEM((K, Nb), BF16), pltpu.SemaphoreType.DMA,
                            pltpu.SemaphoreType.DMA((7,)), pltpu.SemaphoreType.DMA((7,)), pltpu.SemaphoreType.DMA]),
        out_shape=[jax.ShapeDtypeStruct((T, N_DEV * Nb), F32), jax.ShapeDtypeStruct((N_DEV, K, Nb), BF16)],
        compiler_params=_cparams(("arbitrary",)),
    )(order, u, shard)
    return z, full


N_EARLY_BLOCKS = 6


def _mm_gathered_tail(u, full, z, order, name, after=()):
    T, K = u.shape
    _, _, Nb = full.shape
    n_after = len(after)

    def body(order_ref, u_ref, w_ref, z_in_ref, *rest):
        rest[n_after][...] = jnp.dot(u_ref[...], w_ref[...], preferred_element_type=F32)

    return pl.pallas_call(
        body, name=name,
        grid_spec=pltpu.PrefetchScalarGridSpec(
            num_scalar_prefetch=1, grid=(N_DEV - N_EARLY_BLOCKS,),
            in_specs=[pl.BlockSpec((T, K), lambda s, order: (0, 0)),
                      pl.BlockSpec((None, K, Nb), lambda s, order: (order[N_EARLY_BLOCKS + s], 0, 0)), ANY]
            + [ANY] * n_after,
            out_specs=pl.BlockSpec((T, Nb), lambda s, order: (0, order[N_EARLY_BLOCKS + s]))),
        out_shape=jax.ShapeDtypeStruct(z.shape, z.dtype),
        input_output_aliases={3: 0},
        compiler_params=_cparams(("arbitrary",)),
    )(order, u, full, z, *after)


def _gather_order():
    x, y, c = _position()
    chips = [(1 - x, y), (x, 1 - y), (1 - x, 1 - y)]
    ids = [4 * x + 2 * y + c, 4 * x + 2 * y + (1 - c)]
    ids += [4 * cx + 2 * cy + c for cx, cy in chips[:2]] + [4 * cx + 2 * cy + (1 - c) for cx, cy in chips[:2]]
    ids += [4 * chips[2][0] + 2 * chips[2][1] + c, 4 * chips[2][0] + 2 * chips[2][1] + (1 - c)]
    return jnp.stack(ids).astype(jnp.int32)


HBM = pl.BlockSpec(memory_space=pltpu.HBM)
SEM = pl.BlockSpec(memory_space=pltpu.SEMAPHORE)
DATAFLOW = pltpu.SideEffectType.DATAFLOW_SIDE_EFFECTING


def _split_call(name, bufs, waits=(), starts=None, after=()):
    nb = len(bufs)
    n_new = starts[1] if starts else 0
    wait_sems = [s for w in waits for s in (*w[1], *w[2])]

    def body(*refs):
        b, pos = refs[:nb], nb
        for plan, ss, _, send_idx, recv_idx in waits:
            k = len(ss)
            copies = plan(b, refs[pos:pos + k], refs[pos + k:pos + 2 * k])
            pos += 2 * k
            for i in recv_idx:
                copies[i].wait_recv()
            for i in send_idx:
                copies[i].wait_send()
        outs = refs[pos + len(after):]
        if starts:
            for cp in starts[0](b, outs[nb:nb + n_new], outs[nb + n_new:nb + 2 * n_new]):
                cp.start()
        outs[-1][...] = jnp.zeros_like(outs[-1])

    res = pl.pallas_call(
        body, name=name,
        out_shape=tuple(pltpu.HBM(a.shape, a.dtype) for a in bufs) + (pltpu.SemaphoreType.DMA(()),) * (2 * n_new)
        + (jax.ShapeDtypeStruct((8, 128), F32),),
        in_specs=[HBM] * nb + [SEM] * len(wait_sems) + [ANY] * len(after),
        out_specs=(HBM,) * nb + (SEM,) * (2 * n_new) + (pl.BlockSpec(memory_space=pltpu.VMEM),),
        input_output_aliases={i: i for i in range(nb)},
        compiler_params=pltpu.CompilerParams(has_side_effects=DATAFLOW),
    )(*bufs, *wait_sems, *after)
    return list(res[:nb]), list(res[nb:nb + n_new]), list(res[nb + n_new:nb + 2 * n_new]), res[-1]


def _in_hbm(a):
    return pltpu.with_memory_space_constraint(a, pltpu.HBM)


def _remote(src, dst, send_sem, recv_sem, to):
    return pltpu.make_async_remote_copy(src_ref=src, dst_ref=dst, send_sem=send_sem, recv_sem=recv_sem,
                                        device_id=to, device_id_type=MESH)


def _other_chips():
    x, y, _ = _position()
    return [(1 - x, y), (x, 1 - y), (1 - x, 1 - y)]


def _plan_gather_first(n):
    def plan(b, ss, rs):
        x, y, c = _position()
        to = [(x, y, 1 - c)] + [(*chip, c) for chip in _other_chips()]
        return [_remote(b[w], b[n + w].at[4 * x + 2 * y + c], ss[4 * w + k], rs[4 * w + k], to[k])
                for w in range(n) for k in range(4)]
    return plan, 4 * n


def _plan_gather_pass(n):
    def plan(b, ss, rs):
        x, y, c = _position()
        copies = []
        for w in range(n):
            for j, chip in enumerate(_other_chips()):
                blk = b[n + w].at[4 * chip[0] + 2 * chip[1] + c]
                copies.append(_remote(blk, blk, ss[3 * w + j], rs[3 * w + j], (x, y, 1 - c)))
        return copies
    return plan, 3 * n


def _plan_sibling(n):
    def plan(b, ss, rs):
        x, y, c = _position()
        return [_remote(b[w].at[s], b[n + w].at[s], ss[4 * w + s], rs[4 * w + s], (x, y, 1 - c))
                for w in range(n) for s in range(N_CHIP)]
    return plan, 4 * n


def _plan_scatter(n):
    def plan(b, ss, rs):
        x, y, c = _position()
        return [_remote(b[w].at[2 * chip[0] + chip[1]], b[n + w].at[2 * x + y], ss[3 * w + j], rs[3 * w + j],
                        (*chip, c))
                for w in range(n) for j, chip in enumerate(_other_chips())]
    return plan, 3 * n


class _Gather:
    def __init__(self, shards, after, name):
        self.n, self.name = len(shards), name
        x, y, c = _position()
        placed = [lax.dynamic_update_index_in_dim(lax.empty((N_DEV,) + s.shape, s.dtype), s, 4 * x + 2 * y + c, 0)
                  for s in shards]
        bufs, self.ss, self.rs, self.token = _split_call(
            name + "_start", [_in_hbm(a) for a in list(shards) + placed], starts=_plan_gather_first(self.n),
            after=after)
        self.shards, self.fulls = bufs[:self.n], bufs[self.n:]
        self.passed = {}

    def _sub(self, ids, sems, per):
        return [sems[per * w + k] for w in ids for k in range(per)]

    def pass_on(self, ids, after, tag):
        m = len(ids)
        first = (_plan_gather_first(m)[0], self._sub(ids, self.ss, 4), self._sub(ids, self.rs, 4),
                 [], [4 * i + k for i in range(m) for k in (1, 2, 3)])
        bufs, ss, rs, token = _split_call(
            "%s_pass_%s" % (self.name, tag), [self.shards[w] for w in ids] + [self.fulls[w] for w in ids],
            waits=[first], starts=_plan_gather_pass(m), after=after)
        for i, w in enumerate(ids):
            self.shards[w], self.fulls[w] = bufs[i], bufs[m + i]
        self.passed[tuple(ids)] = (ss, rs)
        return token

    def finish(self, ids, after, tag):
        m = len(ids)
        ss2, rs2 = self.passed[tuple(ids)]
        first = (_plan_gather_first(m)[0], self._sub(ids, self.ss, 4), self._sub(ids, self.rs, 4),
                 list(range(4 * m)), [4 * i for i in range(m)])
        passed = (_plan_gather_pass(m)[0], ss2, rs2, list(range(3 * m)), list(range(3 * m)))
        bufs, _, _, _ = _split_call(
            "%s_finish_%s" % (self.name, tag), [self.shards[w] for w in ids] + [self.fulls[w] for w in ids],
            waits=[first, passed], after=after)
        return bufs[m:]


class _ReduceScatter:
    def __init__(self, others, name):
        self.n, self.name = len(others), name
        lands = [lax.empty(g.shape, g.dtype) for g in others]
        self.bufs, self.ss, self.rs, self.token = _split_call(
            name + "_sibling_start", [_in_hbm(a) for a in list(others) + lands], starts=_plan_sibling(self.n))

    def from_sibling(self, after):
        n = self.n
        bufs, _, _, _ = _split_call(
            self.name + "_sibling_wait", self.bufs,
            waits=[(_plan_sibling(n)[0], self.ss, self.rs, list(range(4 * n)), list(range(4 * n)))], after=after)
        return bufs[n:]

    def scatter(self, sums):
        lands = [lax.empty(s.shape, s.dtype) for s in sums]
        self.bufs, self.ss, self.rs, token = _split_call(
            self.name + "_scatter_start", [_in_hbm(a) for a in list(sums) + lands], starts=_plan_scatter(self.n))
        return token

    def finish(self, after):
        n = self.n
        bufs, _, _, _ = _split_call(
            self.name + "_scatter_wait", self.bufs,
            waits=[(_plan_scatter(n)[0], self.ss, self.rs, list(range(3 * n)), list(range(3 * n)))], after=after)
        return bufs[:n], bufs[n:]


class _Exchanges:
    def __init__(self, parity, order):
        self.parity, self.order = parity, order


def _gather_small(packed, name):
    R = packed.shape[0]

    def body(x_ref, out_ref, send_sems, recv_sems):
        x, y, c = _position()
        me = 4 * x + 2 * y + c
        out_ref[me] = x_ref[...]
        copies = []
        for k in range(1, N_DEV):
            to = (x ^ ((k >> 2) & 1), y ^ ((k >> 1) & 1), c ^ (k & 1))
            cp = pltpu.make_async_remote_copy(
                src_ref=x_ref, dst_ref=out_ref.at[me],
                send_sem=send_sems.at[k], recv_sem=recv_sems.at[k], device_id=to, device_id_type=MESH)
            cp.start()
            copies.append((k, to, cp))
        for k, to, cp in copies:
            cp.wait_send()
            pltpu.make_async_remote_copy(
                src_ref=x_ref, dst_ref=out_ref.at[4 * to[0] + 2 * to[1] + to[2]],
                send_sem=send_sems.at[k], recv_sem=recv_sems.at[k], device_id=to, device_id_type=MESH).wait_recv()

    return pl.pallas_call(
        body, name=name,
        in_specs=[pl.BlockSpec(memory_space=pltpu.VMEM)], out_specs=pl.BlockSpec(memory_space=pltpu.VMEM),
        out_shape=jax.ShapeDtypeStruct((N_DEV, R, 128), F32),
        scratch_shapes=[pltpu.SemaphoreType.DMA((N_DEV,)), pltpu.SemaphoreType.DMA((N_DEV,))],
    )(packed)


def _adamw_math(w, g, m, v):
    m = ADAM_B1 * m + (1.0 - ADAM_B1) * g
    v = ADAM_B2 * v + (1.0 - ADAM_B2) * (g * g)
    m_hat = m / (1.0 - ADAM_B1 ** ADAM_STEP)
    v_hat = v / (1.0 - ADAM_B2 ** ADAM_STEP)
    delta = -ADAM_LR * (m_hat / (jnp.sqrt(v_hat) + ADAM_EPS) + ADAM_WD * w)
    return delta, m, v


def _adamw_big_landed(w, m, v, parts, lands, slot, name, row0=0, into=None):
    R, C = w.shape
    rows = parts.shape[1]
    tr = _pick(rows, (256,))
    first = row0 // tr
    n_into = len(into) if into else 0

    def body(slot_ref, w_ref, m_ref, v_ref, own_ref, l1_ref, l2_ref, l3_ref, *rest):
        g = own_ref[...].astype(F32)
        for ref in (l1_ref, l2_ref, l3_ref):
            g = g + ref[...].astype(F32)
        for o_ref, res in zip(rest[n_into:], (g,) + _adamw_math(w_ref[...], g, m_ref[...], v_ref[...])):
            o_ref[...] = res

    blk = pl.BlockSpec((tr, C), lambda i, slot: (first + i, 0))

    def chip(k):
        return pl.BlockSpec((None, tr, C), lambda i, slot: ((slot[0] + k) % N_CHIP, i, 0))

    out = jax.ShapeDtypeStruct((R, C), F32)
    return pl.pallas_call(
        body, name=name,
        grid_spec=pltpu.PrefetchScalarGridSpec(
            num_scalar_prefetch=1, grid=(rows // tr,),
            in_specs=[blk, blk, blk, chip(0), chip(1), chip(2), chip(3)] + [ANY] * n_into,
            out_specs=[blk, blk, blk, blk]),
        out_shape=[out, out, out, out],
        input_output_aliases={8 + j: j for j in range(n_into)},
        compiler_params=_cparams(("parallel",)),
    )(slot, w, m, v, parts, lands, lands, lands, *(into or ()))


def _adamw_small(w, m, v, gathered, name):
    R = w.shape[0]

    def body(w_ref, m_ref, v_ref, p_ref, g_ref, d_ref, nm_ref, nv_ref):
        g = p_ref[0]
        for s in range(1, N_DEV):
            g = g + p_ref[s]
        d, nm, nv = _adamw_math(w_ref[...], g, m_ref[...], v_ref[...])
        g_ref[...] = g
        d_ref[...] = d
        nm_ref[...] = nm
        nv_ref[...] = nv

    out = jax.ShapeDtypeStruct((R, 128), F32)
    return pl.pallas_call(
        body, name=name, out_shape=[out, out, out, out],
    )(w, m, v, gathered)


SMALL_NAMES = ("lb_logits", "hg_norm_w", "rel_bias", "norm_mix_w", "norm_mlp_w", "norm_final_w")
SMALL_SHAPES = {"lb_logits": (2, HG_WIDTH), "hg_norm_w": (1, HG_DK), "rel_bias": (AT_HEADS, N_REL_PAD),
                "norm_mix_w": (1, D_MODEL), "norm_mlp_w": (1, D_MODEL), "norm_final_w": (1, D_MODEL)}


def _pack_small(parts):
    rows = []
    for nme in SMALL_NAMES:
        p = parts[nme]
        if nme == "rel_bias":
            p = jnp.pad(p, ((0, 0), (0, N_REL_PAD - N_REL)))
        rows.append(p.reshape(-1, 128))
    flat = jnp.concatenate(rows, axis=0)
    return jnp.pad(flat, ((0, SMALL_ROWS - flat.shape[0]), (0, 0)))


def _unpack_small(packed):
    out, at = {}, 0
    for nme in SMALL_NAMES:
        shp = SMALL_SHAPES[nme]
        nrow = shp[0] * shp[1] // 128
        p = packed[at:at + nrow].reshape(shp)
        at += nrow
        out[nme] = p[:, :N_REL] if nme == "rel_bias" else p
    return out


BIG_NAMES = ("w_in", "w_branch_a", "w_branch_b", "w_out", "w_up", "w_down")


def kernel(x, w_in, lb_logits, hg_norm_w, rel_bias, w_branch_a, w_branch_b, w_out, norm_mix_w, norm_mlp_w, w_up, w_down, norm_final_w, loss_target, m_w_in, m_lb_logits, m_hg_norm_w, m_rel_bias, m_w_branch_a, m_w_branch_b, m_w_out, m_norm_mix_w, m_norm_mlp_w, m_w_up, m_w_down, m_norm_final_w, v_w_in, v_lb_logits, v_hg_norm_w, v_rel_bias, v_w_branch_a, v_w_branch_b, v_w_out, v_norm_mix_w, v_norm_mlp_w, v_w_up, v_w_down, v_norm_final_w):
    big_w = [w_in[0], w_branch_a[0], w_branch_b[0], w_out[0], w_up[0], w_down[0]]
    big_m = [m_w_in[0], m_w_branch_a[0], m_w_branch_b[0], m_w_out[0], m_w_up[0], m_w_down[0]]
    big_v = [v_w_in[0], v_w_branch_a[0], v_w_branch_b[0], v_w_out[0], v_w_up[0], v_w_down[0]]

    shards = [w.astype(BF16) for w in big_w]
    parity = lax.axis_index("c").astype(jnp.int32).reshape(1)
    loss_part, grad_x, chip_parts, small = _local_step(
        x[0], loss_target[0], lb_logits, hg_norm_w, rel_bias[0], norm_mix_w, norm_mlp_w,
        norm_final_w.reshape(1, D_MODEL), shards[0], shards[1:], _Exchanges(parity, _gather_order()))
    loss = lax.psum(loss_part[0, 0], ("x", "y", "c"))
    (rs_in_lo, rs_in_hi), rs_mix, rs_up, rs_down = chip_parts
    slot =(2 * lax.axis_index("x") + lax.axis_index("y")).astype(jnp.int32).reshape(1)
    big = {}

    def finish(rs, names, after):
        sums, lands = rs.finish(after)
        for nme, own, land in zip(names, sums, lands):
            i = BIG_NAMES.index(nme)
            big[nme] = _adamw_big_landed(big_w[i], big_m[i], big_v[i], own, land, slot, "adamw_" + nme)
        return [big[nme][1] for nme in names]

    done = finish(rs_down, ["w_down"], [grad_x])
    done = finish(rs_up, ["w_up"], done)
    done = finish(rs_mix, ["w_branch_a", "w_branch_b", "w_out"], done)

    sw = dict(lb_logits=lb_logits, hg_norm_w=hg_norm_w, rel_bias=rel_bias[0], norm_mix_w=norm_mix_w,
              norm_mlp_w=norm_mlp_w, norm_final_w=norm_final_w.reshape(1, D_MODEL))
    sm = dict(lb_logits=m_lb_logits, hg_norm_w=m_hg_norm_w, rel_bias=m_rel_bias[0], norm_mix_w=m_norm_mix_w,
              norm_mlp_w=m_norm_mlp_w, norm_final_w=m_norm_final_w.reshape(1, D_MODEL))
    sv = dict(lb_logits=v_lb_logits, hg_norm_w=v_hg_norm_w, rel_bias=v_rel_bias[0], norm_mix_w=v_norm_mix_w,
              norm_mlp_w=v_norm_mlp_w, norm_final_w=v_norm_final_w.reshape(1, D_MODEL))
    gathered = _gather_small(_pack_small(small), "gather_small")
    small_packed = _adamw_small(_pack_small(sw), _pack_small(sm), _pack_small(sv), gathered, "adamw_small")
    small_out = [_unpack_small(p) for p in small_packed]

    (own,), (land,) = rs_in_lo.finish(done + [small_packed[0]])
    lo = _adamw_big_landed(big_w[0], big_m[0], big_v[0], own, land, slot, "adamw_w_in_lo")
    (own,), (land,) = rs_in_hi.finish([lo[1]])
    big["w_in"] = _adamw_big_landed(big_w[0], big_m[0], big_v[0], own, land, slot, "adamw_w_in_hi",
                                    row0=D_MODEL // 2, into=lo)

    def leaf(kind, nme):
        if nme in BIG_NAMES:
            return big[nme][kind][None]
        p = small_out[kind][nme]
        if nme == "rel_bias":
            return p[None]
        if nme == "norm_final_w":
            return p.reshape(D_MODEL)
        return p

    order = ("w_in", "lb_logits", "hg_norm_w", "rel_bias", "w_branch_a", "w_branch_b", "w_out", "norm_mix_w",
             "norm_mlp_w", "w_up", "w_down", "norm_final_w")
    outs = [loss, grad_x[None]]
    for kind in range(4):
        outs += [leaf(kind, nme) for nme in order]
    return tuple(outs)
```

```python
import jax
import jax.numpy as jnp
from jax import lax
from jax.experimental import pallas as pl
from jax.experimental.pallas import tpu as pltpu

F32 = jnp.float32
BF16 = jnp.bfloat16
HIGHEST = lax.Precision.HIGHEST
MESH = pl.DeviceIdType.MESH

D_MODEL = 2048
HG_HEADS = 8
HG_DK = 128
HG_WIDTH = 1024
AT_HEADS = 16
AT_DH = 64
AT_WIDTH = 1024
CHUNK = 64
LEFT_CHUNKS = 8
BAND = (LEFT_CHUNKS + 1) * CHUNK
PAD = LEFT_CHUNKS * CHUNK
REL_CLIP = 256
N_REL = 2 * REL_CLIP + 1
N_REL_PAD = 640
D_FF = 4 * D_MODEL
EPS = 1e-6
N_DEV = 8
N_CHIP = 4

ADAM_LR = 0.001
ADAM_B1 = 0.9
ADAM_B2 = 0.999
ADAM_EPS = 1e-08
ADAM_WD = 0.01
ADAM_STEP = 10

COL_HQ, COL_HF, COL_HI, COL_HG = 0, 8, 16, 24
COL_AQ, COL_AK, COL_AV = 32, 40, 48
COL_GATE_A, COL_GATE_B = 7, 9

VMEM_LIMIT = 56 * 1024 * 1024
SMALL_ROWS = 152


def _cparams(sem=None, **kw):
    if sem is not None:
        kw["dimension_semantics"] = sem
    return pltpu.CompilerParams(vmem_limit_bytes=VMEM_LIMIT, **kw)


def _pick(n, cands):
    for c in cands:
        if n % c == 0:
            return c
    return n


def _sigmoid(x):
    return 1.0 / (1.0 + jnp.exp(-x))


ANY = pl.BlockSpec(memory_space=pl.ANY)


def _position():
    return lax.axis_index("x"), lax.axis_index("y"), lax.axis_index("c")


def _call(body, args, *, name, grid, in_specs, out_specs, out_shape, scratch_shapes=(), sem=None, after=()):
    n_in = len(args)

    def ordered(*refs):
        body(*refs[:n_in], *refs[n_in + len(after):])

    return list(pl.pallas_call(
        ordered if after else body, name=name, grid=grid, in_specs=list(in_specs) + [ANY] * len(after),
        out_specs=out_specs, out_shape=out_shape, scratch_shapes=list(scratch_shapes),
        compiler_params=_cparams(sem))(*args, *after))


MAX_CONTRACTION_TILE = 4096


def _accumulate(part, acc_ref, step, n_steps, finish):
    if n_steps == 1:
        finish(part)
        return

    @pl.when(step == 0)
    def _():
        acc_ref[...] = part

    @pl.when(step > 0)
    def _():
        acc_ref[...] += part

    @pl.when(step == n_steps - 1)
    def _():
        finish(acc_ref[...])


def _mm_nn(a, wb, out_dtype, name, after=(), epilogue=None):
    M, K = a.shape
    NB, K2, Nb = wb.shape
    assert K == K2
    tm = min(M, 1024)
    tk = min(K, MAX_CONTRACTION_TILE)
    tn = _pick(Nb, (512, 1408, 256))
    nk = K // tk
    nn = Nb // tn
    extra, first_cols, out_dtypes, fn = epilogue or ((), (), (out_dtype,), lambda total: (total,))
    n_extra, n_out = len(extra), len(out_dtypes)

    def body(a_ref, b_ref, *rest):
        def finish(total):
            results = fn(total, *[r[...] for r in rest[:n_extra]])
            for o_ref, res, dt in zip(rest[n_extra:n_extra + n_out], results, out_dtypes):
                o_ref[...] = res.astype(dt)

        part = jnp.dot(a_ref[...], b_ref[...], preferred_element_type=F32)
        _accumulate(part, rest[-1], pl.program_id(3), nk, finish)

    def tile(first):
        return pl.BlockSpec((tm, tn), lambda m, j, n, k: (m, first + j * nn + n))

    outs = _call(
        body, (a, wb) + tuple(extra), name=name, grid=(M // tm, NB, nn, nk),
        in_specs=[pl.BlockSpec((tm, tk), lambda m, j, n, k: (m, k)),
                  pl.BlockSpec((None, tk, tn), lambda m, j, n, k: (j, k, n))] + [tile(col // tn) for col in first_cols],
        out_specs=[tile(0)] * n_out,
        out_shape=[jax.ShapeDtypeStruct((M, NB * Nb), dt) for dt in out_dtypes],
        scratch_shapes=[] if nk == 1 else [pltpu.VMEM((tm, tn), F32)],
        sem=("parallel", "parallel", "parallel", "arbitrary"), after=after)
    return outs if epilogue else outs[0]


def _squared_relu(a):
    ra = jnp.maximum(a, 0.0)
    return a, ra * ra


def _gated_merge(pb, za, zb, pa):
    return pb, _sigmoid(za) * pa + _sigmoid(zb) * pb


def _mm_nt(a, wb, out_dtype, name, after=(), epilogue=None):
    M, N = a.shape
    NB, K, Nb = wb.shape
    assert N == NB * Nb
    tm = min(M, 1024)
    n_tiles_live = 1 + (len(epilogue[0]) + len(epilogue[2]) if epilogue else 0)
    tko = _pick(K, (1024,)) if n_tiles_live <= 3 else _pick(K, (512,))
    tc = _pick(Nb, (2048, 1024, 1408, 256))
    nc = Nb // tc
    jb = max([d for d in (8, 4, 2, 1) if NB % d == 0 and d * tc <= MAX_CONTRACTION_TILE]) if nc == 1 else 1
    nsteps = (NB // jb) * nc
    extra, first_cols, out_dtypes, fn = epilogue or ((), (), (out_dtype,), lambda total: (total,))
    n_extra, n_out = len(extra), len(out_dtypes)

    def body(a_ref, b_ref, *rest):
        def finish(total):
            results = fn(total, *[r[...] for r in rest[:n_extra]])
            for o_ref, res, dt in zip(rest[n_extra:n_extra + n_out], results, out_dtypes):
                o_ref[...] = res.astype(dt)

        part = sum(lax.dot_general(a_ref[:, i * tc:(i + 1) * tc], b_ref[i], (((1,), (1,)), ((), ())),
                                   preferred_element_type=F32) for i in range(jb))
        _accumulate(part, rest[-1], pl.program_id(2) * nc + pl.program_id(3), nsteps, finish)

    def tile(first):
        return pl.BlockSpec((tm, tko), lambda m, ko, j, c: (m, first + ko))

    outs = _call(
        body, (a, wb) + tuple(extra), name=name,
        grid=(M // tm, K // tko, NB // jb, nc),
        in_specs=[pl.BlockSpec((tm, jb * tc), lambda m, ko, j, c: (m, j * nc + c)),
                  pl.BlockSpec((jb, tko, tc), lambda m, ko, j, c: (j, ko, c))] + [tile(col // tko) for col in first_cols],
        out_specs=[tile(0)] * n_out,
        out_shape=[jax.ShapeDtypeStruct((M, K), dt) for dt in out_dtypes],
        scratch_shapes=[] if nsteps == 1 else [pltpu.VMEM((tm, tko), F32)],
        sem=("parallel", "parallel", "arbitrary", "arbitrary"), after=after)
    return outs if epilogue else outs[0]


ROWS_TILE = 512
ROWS_PIECE = 128


def _mm_rows(a, w, extras, vectors, row_dtypes, fn, name):
    M, K = a.shape
    N = w.shape[1]
    tm = min(M, ROWS_TILE)
    n_e, n_v = len(extras), len(vectors)

    def body(a_ref, w_ref, *rest):
        tiles, vecs, outs, product_ref = rest[:n_e], rest[n_e:n_e + n_v], rest[n_e + n_v:-1], rest[-1]
        product_ref[...] = jnp.dot(a_ref[...], w_ref[...], preferred_element_type=F32)
        for i in range(tm // ROWS_PIECE):
            piece = slice(i * ROWS_PIECE, (i + 1) * ROWS_PIECE)
            results = fn(product_ref[piece, :], *[t[piece, :] for t in tiles], *[v[...] for v in vecs])
            for o_ref, res, dt in zip(outs, results, row_dtypes):
                o_ref[piece, :] = res.astype(dt)

    row = pl.BlockSpec((tm, N), lambda m: (m, 0))
    return _call(
        body, (a, w) + tuple(extras) + tuple(vectors), name=name, grid=(M // tm,),
        in_specs=[pl.BlockSpec((tm, K), lambda m: (m, 0)), pl.BlockSpec((K, N), lambda m: (0, 0))]
        + [row] * n_e + [pl.BlockSpec((1, N), lambda m: (0, 0))] * n_v,
        out_specs=[row] * len(row_dtypes),
        out_shape=[jax.ShapeDtypeStruct((M, N), dt) for dt in row_dtypes],
        scratch_shapes=[pltpu.VMEM((tm, N), F32)], sem=("parallel",))


def _rms(h, w):
    return h * lax.rsqrt(jnp.mean(h * h, axis=-1, keepdims=True) + EPS) * w


def _residual_rms_rows(mix, x, w):
    h = x + mix
    return h, _rms(h, w)


def _mm_tn_half(a, g, which, blocks_on, add, name, after=(), a_cols=None):
    M, Ka = a.shape
    N = g.shape[1]
    first_col = 0
    if a_cols is not None:
        first_col, Ka = a_cols
    if blocks_on == "g":
        rows, cols = _pick(Ka, (1024,)), N // N_DEV
        tn = _pick(cols, (512, 1408, 256))
        nn = cols // tn
        grid = (Ka // rows, N_CHIP, nn)
        a_spec = pl.BlockSpec((M, rows), lambda ka, s, n, w: (0, first_col // rows + ka))
        g_spec = pl.BlockSpec((M, tn), lambda ka, s, n, w: (0, (2 * s + w[0]) * nn + n))
        out_rows = Ka
    else:
        rows, cols = Ka // N_DEV, N
        tn = _pick(cols, (2048, 512))
        nn = cols // tn
        grid = (1, N_CHIP, nn)
        a_spec = pl.BlockSpec((M, rows), lambda ka, s, n, w: (0, 2 * s + w[0]))
        g_spec = pl.BlockSpec((M, tn), lambda ka, s, n, w: (0, n))
        out_rows = rows
    o_spec = pl.BlockSpec((None, rows, tn), lambda ka, s, n, w: (s, ka, n))
    n_add = 0 if add is None else 1

    def body(which_ref, a_ref, g_ref, *rest):
        acc = lax.dot_general(a_ref[...], g_ref[...], (((0,), (0,)), ((), ())), preferred_element_type=F32)
        if n_add:
            acc = acc + rest[0][...].astype(F32)
        rest[-1][...] = acc.astype(BF16)

    return pl.pallas_call(
        body, name=name,
        grid_spec=pltpu.PrefetchScalarGridSpec(
            num_scalar_prefetch=1, grid=grid,
            in_specs=[a_spec, g_spec] + [o_spec] * n_add + [ANY] * len(after),
            out_specs=o_spec),
        out_shape=jax.ShapeDtypeStruct((N_CHIP, out_rows, cols), BF16),
        compiler_params=_cparams(("parallel", "parallel", "parallel")),
    )(which, a, g, *(() if add is None else (add,)), *after)


ROW_TILE = 256


def _rms_fwd(x, w, name):
    T, Dm = x.shape

    def body(x_ref, w_ref, u_ref):
        xv = x_ref[...]
        r = lax.rsqrt(jnp.mean(xv * xv, axis=-1, keepdims=True) + EPS)
        u_ref[...] = (xv * r * w_ref[...]).astype(BF16)

    return pl.pallas_call(
        body, name=name, grid=(T // ROW_TILE,),
        in_specs=[pl.BlockSpec((ROW_TILE, Dm), lambda i: (i, 0)), pl.BlockSpec((1, Dm), lambda i: (0, 0))],
        out_specs=pl.BlockSpec((ROW_TILE, Dm), lambda i: (i, 0)),
        out_shape=jax.ShapeDtypeStruct((T, Dm), BF16),
        compiler_params=_cparams(("parallel",)),
    )(x, w)


def _loss_head(h1, mlp, wf, target, name):
    T, Dm = h1.shape

    def body(h_ref, m_ref, w_ref, t_ref, loss_ref, dh_ref, dhb_ref, dw_ref):
        i = pl.program_id(0)
        h = h_ref[...] + m_ref[...]
        r = lax.rsqrt(jnp.mean(h * h, axis=-1, keepdims=True) + EPS)
        xh = h * r
        wv = w_ref[...]
        e = xh * wv - t_ref[...]
        part = 0.5 * jnp.sum(jnp.mean(e * e, axis=-1, keepdims=True), axis=0, keepdims=True)
        dy = e * (1.0 / Dm)
        dw = jnp.sum(dy * xh, axis=0, keepdims=True)
        gy = dy * wv
        dh = r * (gy - xh * jnp.mean(gy * xh, axis=-1, keepdims=True))
        dh_ref[...] = dh
        dhb_ref[...] = dh.astype(BF16)

        @pl.when(i == 0)
        def _():
            loss_ref[...] = jnp.zeros_like(loss_ref)
            dw_ref[...] = jnp.zeros_like(dw_ref)

        loss_ref[...] += jnp.broadcast_to(part, loss_ref.shape)
        dw_ref[...] += dw

    row = pl.BlockSpec((ROW_TILE, Dm), lambda i: (i, 0))
    vec = pl.BlockSpec((1, Dm), lambda i: (0, 0))
    return pl.pallas_call(
        body, name=name, grid=(T // ROW_TILE,),
        in_specs=[row, row, vec, row],
        out_specs=[pl.BlockSpec((8, 128), lambda i: (0, 0)), row, row, vec],
        out_shape=[jax.ShapeDtypeStruct((8, 128), F32), jax.ShapeDtypeStruct((T, Dm), F32),
                   jax.ShapeDtypeStruct((T, Dm), BF16), jax.ShapeDtypeStruct((1, Dm), F32)],
        compiler_params=_cparams(("arbitrary",)),
    )(h1, mlp, wf, target)


def _rms_bwd(dyn, x, w, dres, dx_dtypes, name, after=()):
    T, Dm = x.shape
    n_dx = len(dx_dtypes)

    def body(g_ref, x_ref, w_ref, r_ref, *outs):
        i = pl.program_id(0)
        xv = x_ref[...]
        r = lax.rsqrt(jnp.mean(xv * xv, axis=-1, keepdims=True) + EPS)
        xh = xv * r
        g = g_ref[...]
        dw = jnp.sum(g * xh, axis=0, keepdims=True)
        gy = g * w_ref[...]
        dx = r_ref[...] + r * (gy - xh * jnp.mean(gy * xh, axis=-1, keepdims=True))
        for dx_ref, dt in zip(outs, dx_dtypes):
            dx_ref[...] = dx.astype(dt)
        dw_ref = outs[n_dx]

        @pl.when(i == 0)
        def _():
            dw_ref[...] = jnp.zeros_like(dw_ref)

        dw_ref[...] += dw

    row = pl.BlockSpec((ROW_TILE, Dm), lambda i: (i, 0))
    vec = pl.BlockSpec((1, Dm), lambda i: (0, 0))
    return _call(
        body, (dyn, x, w, dres), name=name, grid=(T // ROW_TILE,),
        in_specs=[row, row, vec, row],
        out_specs=[row] * n_dx + [vec],
        out_shape=[jax.ShapeDtypeStruct((T, Dm), dt) for dt in dx_dtypes] + [jax.ShapeDtypeStruct((1, Dm), F32)],
        sem=("arbitrary",), after=after)


GATE_TILE = 1024


def _merge_grads(d, za, zb, pa, pb):
    ga = _sigmoid(za)
    gb = _sigmoid(zb)
    return d * ga, d * gb, d * pa * ga * (1.0 - ga), d * pb * gb * (1.0 - gb)


def _dot_hi(a, b, dims):
    return lax.dot_general(a, b, (dims, ((), ())), precision=HIGHEST, preferred_element_type=F32)


NN = ((1,), (0,))
NT = ((1,), (1,))
TN = ((0,), (0,))


def _hg_gates(hq, hf, lb):
    sq = _sigmoid(hq)
    q = hq * sq * (HG_DK ** -0.5)
    f = _sigmoid(hf)
    g = lb + (1.0 - lb) * f
    return q, sq, f, g, jnp.log(g), 1.0 - g


def _tri(lower):
    r = lax.broadcasted_iota(jnp.int32, (CHUNK, CHUNK), 0)
    c = lax.broadcasted_iota(jnp.int32, (CHUNK, CHUNK), 1)
    return jnp.where((r >= c) if lower else (r <= c), 1.0, 0.0).astype(BF16)


def _running_sum(tri, x):
    return sum(jnp.dot(tri, piece, preferred_element_type=F32) for piece in _split3(x))


GROUP = 16
N_GROUPS = CHUNK // GROUP
BWD_CHUNKS_PER_TRIP = 4


def _dot_bf16(a, b, dims):
    return lax.dot_general(a.astype(BF16), b.astype(BF16), (dims, ((), ())), preferred_element_type=F32)


def _rows_iota():
    return lax.broadcasted_iota(jnp.int32, (CHUNK, HG_DK), 0)


def _by_query_group(q, kk, b, g):
    r0 = GROUP * g
    b0 = b[r0:r0 + 1]
    decay = jnp.exp(b[r0:r0 + GROUP] - b0)
    ks = jnp.where(_rows_iota() < r0, kk * jnp.exp(jnp.minimum(b0 - b, 0.0)), 0.0)
    return q[r0:r0 + GROUP] * decay, ks, decay


def _by_key_group(q, kk, b, j):
    r1 = GROUP * (j + 1)
    b1 = b[r1 - 1:r1]
    decay = jnp.exp(b1 - b[r1 - GROUP:r1])
    qs = jnp.where(_rows_iota() >= r1, q * jnp.exp(jnp.minimum(b - b1, 0.0)), 0.0)
    return qs, kk[r1 - GROUP:r1] * decay, decay


def _scores_between_groups(q, kk, b):
    blocks = [jnp.zeros((GROUP, CHUNK), F32)]
    for g in range(1, N_GROUPS):
        qs, ks, _ = _by_query_group(q, kk, b, g)
        blocks.append(_dot_bf16(qs, ks, NT))
    return jnp.concatenate(blocks, axis=0)


def _hgrn2_fwd(z, lb_logits, hg_norm_w, name, after=()):
    T = z.shape[0]
    n_chunks = T // CHUNK

    def body(hq_ref, hf_ref, hi_ref, hg_ref, lbl_ref, nw_ref, o_ref, ya_ref, sall_ref, st_ref):
        lbl = lbl_ref[...]
        lb = 1.0 / (1.0 + jnp.exp(lbl[1:2, :] - lbl[0:1, :]))
        st_ref[...] = jnp.zeros_like(st_ref)
        tri = _tri(True)
        row8 = lax.broadcasted_iota(jnp.int32, (8, HG_DK), 0)

        def chunk(c, carry):
            rows = pl.ds(pl.multiple_of(c * CHUNK, CHUNK), CHUNK)
            q, _, _, _, lg, kk = _hg_gates(hq_ref[rows, :], hf_ref[rows, :], lb)
            v = hi_ref[rows, :]
            b = _running_sum(tri, lg)
            st = st_ref[...]
            sall_ref[c] = st
            for grp in range(N_GROUPS):
                r0 = GROUP * grp
                for h8 in range(GROUP // 8):
                    n = 8 * (h8 + 1)
                    bs, ks, vs = b[r0:r0 + n], kk[r0:r0 + n], v[r0:r0 + n]
                    sidx = lax.broadcasted_iota(jnp.int32, (n, HG_DK), 0)
                    blk = jnp.zeros((8, HG_DK), F32)
                    for i in range(8):
                        t = r0 + 8 * h8 + i
                        e = jnp.where(sidx <= 8 * h8 + i, jnp.exp(b[t:t + 1] - bs), 0.0)
                        p = jnp.sum(e * ks * q[t:t + 1], axis=1, keepdims=True)
                        ot = jnp.sum(p * vs, axis=0, keepdims=True)
                        blk = blk + jnp.where(row8 == i, ot, 0.0)
                    o_ref[pl.ds(pl.multiple_of(c * CHUNK + r0 + 8 * h8, 8), 8), :] = blk
            o_ref[rows, :] += _dot_hi(q * jnp.exp(b), st, NT) + _dot_bf16(_scores_between_groups(q, kk, b), v, NN)
            bl = b[CHUNK - 1:CHUNK]
            ke = kk * jnp.exp(bl - b)
            st_ref[...] = st * jnp.exp(bl) + _dot_hi(v, ke, TN)
            return carry

        lax.fori_loop(0, n_chunks, chunk, 0, unroll=2)
        o = o_ref[...]
        r = lax.rsqrt(jnp.mean(o * o, axis=-1, keepdims=True) + EPS)
        hg = hg_ref[...]
        ya_ref[...] = (o * r * nw_ref[...] * (hg * _sigmoid(hg))).astype(BF16)

    def col(base):
        return pl.BlockSpec((T, HG_DK), lambda h: (0, base + h))

    return _call(
        body, (z, z, z, z, lb_logits, hg_norm_w), name=name, grid=(HG_HEADS,),
        in_specs=[col(COL_HQ), col(COL_HF), col(COL_HI), col(COL_HG),
                  pl.BlockSpec((2, HG_DK), lambda h: (0, h)), pl.BlockSpec((1, HG_DK), lambda h: (0, 0))],
        out_specs=[col(0), col(0), pl.BlockSpec((None, n_chunks, HG_DK, HG_DK), lambda h: (h, 0, 0, 0))],
        out_shape=[jax.ShapeDtypeStruct((T, HG_WIDTH), F32), jax.ShapeDtypeStruct((T, HG_WIDTH), BF16),
                   jax.ShapeDtypeStruct((HG_HEADS, n_chunks, HG_DK, HG_DK), F32)],
        scratch_shapes=[pltpu.VMEM((HG_DK, HG_DK), F32)],
        sem=("parallel",), after=after)


def _hgrn2_bwd(z, lb_logits, hg_norm_w, o_raw, s_all, dya, name, after=()):
    T = z.shape[0]
    n_chunks = T // CHUNK

    def body(hq_ref, hf_ref, hi_ref, hg_ref, lbl_ref, nw_ref, o_ref, sall_ref, dya_ref,
             dhq_ref, dhf_ref, dhi_ref, dhg_ref, dlbl_ref, dnw_ref,
             do_ref, dst_ref, dlb_ref, *per_chunk):
        h = pl.program_id(0)
        lbl = lbl_ref[...]
        lb = 1.0 / (1.0 + jnp.exp(lbl[1:2, :] - lbl[0:1, :]))

        o = o_ref[...]
        r = lax.rsqrt(jnp.mean(o * o, axis=-1, keepdims=True) + EPS)
        oh = o * r
        nw = nw_ref[...]
        hg = hg_ref[...]
        sg = _sigmoid(hg)
        dy = dya_ref[...]
        d_on = dy * (hg * sg)
        dhg_ref[...] = (dy * (oh * nw) * (sg * (1.0 + hg * (1.0 - sg)))).astype(BF16)
        dnw = jnp.sum(d_on * oh, axis=0, keepdims=True)
        gy = d_on * nw
        do_ref[...] = r * (gy - oh * jnp.mean(gy * oh, axis=-1, keepdims=True))

        @pl.when(h == 0)
        def _():
            dnw_ref[...] = jnp.zeros_like(dnw_ref)

        dnw_ref[...] += jnp.broadcast_to(dnw, dnw_ref.shape)

        dst_ref[...] = jnp.zeros_like(dst_ref)
        dlb_ref[...] = jnp.zeros_like(dlb_ref)
        tri = _tri(True)
        tri_t = _tri(False)
        row8 = lax.broadcasted_iota(jnp.int32, (8, HG_DK), 0)
        row_group = lax.broadcasted_iota(jnp.int32, (CHUNK, CHUNK), 0) // GROUP
        col_group = lax.broadcasted_iota(jnp.int32, (CHUNK, CHUNK), 1) // GROUP
        earlier_group = col_group < row_group
        later_group = col_group > row_group

        def chunk(c, dq_ref, dk_ref, dv_ref):
            rows = pl.ds(pl.multiple_of(c * CHUNK, CHUNK), CHUNK)
            hq = hq_ref[rows, :]
            q, sq, f, g, lg, kk = _hg_gates(hq, hf_ref[rows, :], lb)
            v = hi_ref[rows, :]
            do = do_ref[rows, :]
            b = _running_sum(tri, lg)
            eb = jnp.exp(b)
            bl = b[CHUNK - 1:CHUNK]
            ebl = jnp.exp(bl)
            ekb = jnp.exp(bl - b)
            qe = q * eb
            ke = kk * ekb
            st = sall_ref[c]
            dst = dst_ref[...]
            dqe = _dot_bf16(do, st, NN)
            dke = _dot_bf16(v, dst, NN)
            dv_inter = _dot_bf16(ke, dst, NT)
            d_ebl = jnp.sum(st * dst, axis=0, keepdims=True)
            dst_ref[...] = dst * ebl + _dot_bf16(do, qe, TN)

            dk_ref[...] = jnp.zeros_like(dk_ref)
            dv_ref[...] = jnp.zeros_like(dv_ref)
            for grp in range(N_GROUPS):
                r0 = GROUP * grp
                for h8 in range(GROUP // 8):
                    n = 8 * (h8 + 1)
                    bs, ks, vs = b[r0:r0 + n], kk[r0:r0 + n], v[r0:r0 + n]
                    sidx = lax.broadcasted_iota(jnp.int32, (n, HG_DK), 0)
                    blk = jnp.zeros((8, HG_DK), F32)
                    for i in range(8):
                        t = r0 + 8 * h8 + i
                        qt = q[t:t + 1]
                        dot_ = do[t:t + 1]
                        e = jnp.where(sidx <= 8 * h8 + i, jnp.exp(b[t:t + 1] - bs), 0.0)
                        w = e * ks
                        p = jnp.sum(w * qt, axis=1, keepdims=True)
                        dsc = jnp.sum(vs * dot_, axis=1, keepdims=True)
                        dqt = jnp.sum(dsc * w, axis=0, keepdims=True)
                        blk = blk + jnp.where(row8 == i, dqt, 0.0)
                        dk_ref[r0:r0 + n, :] += dsc * e * qt
                        dv_ref[r0:r0 + n, :] += p * dot_
                    dq_ref[r0 + 8 * h8:r0 + n, :] = blk
            ds_far = jnp.where(earlier_group, _dot_bf16(do, v, NT), 0.0)
            ds_far_t = jnp.where(later_group, _dot_bf16(v, do, NT), 0.0)
            dq_far, dk_far = [jnp.zeros((GROUP, HG_DK), F32)], []
            for grp in range(1, N_GROUPS):
                r0 = GROUP * grp
                _, ks, decay = _by_query_group(q, kk, b, grp)
                dq_far.append(decay * _dot_hi(ds_far[r0:r0 + GROUP], ks, NN))
                qs, _, decay = _by_key_group(q, kk, b, grp - 1)
                dk_far.append(decay * _dot_hi(ds_far_t[r0 - GROUP:r0], qs, NN))
            dk_far.append(jnp.zeros((GROUP, HG_DK), F32))
            dv_far = _dot_bf16(_scores_between_groups(q, kk, b), do, TN)
            dq_i = dq_ref[...] + jnp.concatenate(dq_far, axis=0)
            dk_i = dk_ref[...] + jnp.concatenate(dk_far, axis=0)
            dke_ke = dke * ke
            db = q * dq_i - kk * dk_i + dqe * qe - dke_ke
            db_last = jnp.sum(dke_ke, axis=0, keepdims=True) + d_ebl * ebl
            dlg = _running_sum(tri_t, db) + db_last
            dq = dq_i + dqe * eb
            dkk = dk_i + dke * ekb
            dg = dlg / g - dkk
            dhq_ref[rows, :] = (dq * (HG_DK ** -0.5) * (sq * (1.0 + hq * (1.0 - sq)))).astype(BF16)
            dhf_ref[rows, :] = (dg * (1.0 - lb) * f * (1.0 - f)).astype(BF16)
            dhi_ref[rows, :] = (dv_ref[...] + dv_far + dv_inter).astype(BF16)
            dlb_ref[...] += jnp.sum(dg * (1.0 - f), axis=0, keepdims=True)

        def trip(i, carry):
            for k in range(BWD_CHUNKS_PER_TRIP):
                chunk(n_chunks - 1 - k - BWD_CHUNKS_PER_TRIP * i, *per_chunk[3 * k:3 * k + 3])
            return carry

        lax.fori_loop(0, n_chunks // BWD_CHUNKS_PER_TRIP, trip, 0)
        dl0 = dlb_ref[...] * lb * (1.0 - lb)
        dlbl_ref[0:1, :] = dl0
        dlbl_ref[1:2, :] = -dl0

    def col(base):
        return pl.BlockSpec((T, HG_DK), lambda h: (0, base + h))

    outb = jax.ShapeDtypeStruct((T, HG_WIDTH), BF16)
    return _call(
        body, (z, z, z, z, lb_logits, hg_norm_w, o_raw, s_all, dya), name=name, grid=(HG_HEADS,),
        in_specs=[col(COL_HQ), col(COL_HF), col(COL_HI), col(COL_HG),
                  pl.BlockSpec((2, HG_DK), lambda h: (0, h)), pl.BlockSpec((1, HG_DK), lambda h: (0, 0)),
                  col(0), pl.BlockSpec((None, n_chunks, HG_DK, HG_DK), lambda h: (h, 0, 0, 0)), col(0)],
        out_specs=[col(0), col(0), col(0), col(0), pl.BlockSpec((2, HG_DK), lambda h: (0, h)),
                   pl.BlockSpec((8, HG_DK), lambda h: (0, 0))],
        out_shape=[outb, outb, outb, outb, jax.ShapeDtypeStruct((2, HG_WIDTH), F32),
                   jax.ShapeDtypeStruct((8, HG_DK), F32)],
        scratch_shapes=[pltpu.VMEM((T, HG_DK), F32), pltpu.VMEM((HG_DK, HG_DK), F32), pltpu.VMEM((1, HG_DK), F32)]
        + [pltpu.VMEM((CHUNK, HG_DK), F32)] * (3 * BWD_CHUNKS_PER_TRIP),
        sem=("arbitrary",), after=after)


CONST_KEYS = PAD - REL_CLIP
VAR_KEYS = BAND - CONST_KEYS
REL_LO = 128
REL_SPAN = N_REL_PAD - REL_LO


def _rel_onehot(t):
    r = lax.broadcasted_iota(jnp.int32, (REL_SPAN, VAR_KEYS), 0)
    j = lax.broadcasted_iota(jnp.int32, (REL_SPAN, VAR_KEYS), 1)
    idx = jnp.clip(t + PAD - CONST_KEYS - j, -REL_CLIP, REL_CLIP) + REL_CLIP - REL_LO
    return jnp.where(r == idx, 1.0, 0.0).astype(BF16)


def _split3(x):
    hi = x.astype(BF16)
    r1 = x - hi.astype(F32)
    mid = r1.astype(BF16)
    return hi, mid, (r1 - mid.astype(F32)).astype(BF16)


def _bias_expand(rel, name):
    def body(rel_ref, out_ref):
        tab = rel_ref[...]
        onehot = _rel_onehot(pl.program_id(0))
        out_ref[:, 0:CONST_KEYS] = jnp.broadcast_to(tab[:, 2 * REL_CLIP:2 * REL_CLIP + 1], (AT_HEADS, CONST_KEYS))
        out_ref[:, CONST_KEYS:BAND] = sum(
            jnp.dot(piece, onehot, preferred_element_type=F32) for piece in _split3(tab[:, REL_LO:N_REL_PAD]))

    return pl.pallas_call(
        body, name=name, grid=(CHUNK,),
        in_specs=[pl.BlockSpec((AT_HEADS, N_REL_PAD), lambda t: (0, 0))],
        out_specs=pl.BlockSpec((None, AT_HEADS, BAND), lambda t: (t, 0, 0)),
        out_shape=jax.ShapeDtypeStruct((CHUNK, AT_HEADS, BAND), F32),
        compiler_params=_cparams(("parallel",)),
    )(rel)


def _bias_reduce(dbias_rows, name, after=()):
    def body(db_ref, out_ref):
        lane = lax.broadcasted_iota(jnp.int32, (AT_HEADS, N_REL_PAD), 1)
        varying = lane >= CONST_KEYS
        by_offset = jnp.zeros((AT_HEADS, N_REL_PAD), F32)
        constant = jnp.zeros((AT_HEADS, N_REL_PAD), F32)
        for t in range(CHUNK):
            row = db_ref[t]
            constant = constant + jnp.where(varying, 0.0, row)
            moved = jnp.where(varying, row, 0.0)
            by_offset = by_offset + (pltpu.roll(moved, N_REL_PAD - t, axis=1) if t else moved)
        offset = lax.broadcasted_iota(jnp.int32, (N_REL_PAD, N_REL_PAD), 0)
        entry = lax.broadcasted_iota(jnp.int32, (N_REL_PAD, N_REL_PAD), 1)
        onehot = jnp.where(entry == jnp.clip(PAD - offset, -REL_CLIP, REL_CLIP) + REL_CLIP, 1.0, 0.0).astype(BF16)
        acc = sum(jnp.dot(piece, onehot, preferred_element_type=F32) for piece in _split3(by_offset))
        last = jnp.sum(constant, axis=1, keepdims=True)
        out_ref[...] = acc + jnp.where(lane == 2 * REL_CLIP, last, 0.0)

    whole = pl.BlockSpec((CHUNK, AT_HEADS, N_REL_PAD), lambda i: (0, 0, 0))
    return _call(
        body, (dbias_rows,), name=name, grid=(1,), in_specs=[whole],
        out_specs=[pl.BlockSpec((AT_HEADS, N_REL_PAD), lambda i: (0, 0))],
        out_shape=[jax.ShapeDtypeStruct((AT_HEADS, N_REL_PAD), F32)],
        sem=("arbitrary",), after=after)[0]


def _pair_lanes():
    return lax.broadcasted_iota(jnp.int32, (CHUNK, 2 * AT_DH), 1) < AT_DH


def _block_diag(a):
    first = _pair_lanes()
    return jnp.concatenate([jnp.where(first, a, 0.0), jnp.where(first, 0.0, a)], axis=0).astype(BF16)


def _diag_blocks(a):
    return jnp.where(_pair_lanes(), a[:CHUNK], a[CHUNK:])


def _band_probs_t(kb, qbd, bias_t, c):
    s = lax.dot_general(kb, qbd, (NT, ((), ())), preferred_element_type=F32) * (AT_DH ** -0.5) + bias_t
    j = lax.broadcasted_iota(jnp.int32, (BAND, 2 * AT_DH), 0)
    s = jnp.where(j + c * CHUNK >= PAD, s, -jnp.inf)
    p = jnp.exp(s - jnp.max(s, axis=0, keepdims=True))
    return p / jnp.sum(p, axis=0, keepdims=True)


def _attn_fwd(z, bias_t, name, after=()):
    T = z.shape[0]
    n_chunks = T // CHUNK

    def body(q_ref, k_ref, v_ref, bias_ref, y_ref, p_ref, *scratch):
        for pr in range(2):
            lanes = slice(128 * pr, 128 * (pr + 1))
            for dst_ref, src_ref in zip(scratch[2 * pr:2 * pr + 2], (k_ref, v_ref)):
                dst_ref[0:PAD, :] = jnp.zeros((PAD, 128), BF16)
                dst_ref[PAD:PAD + T, :] = src_ref[:, lanes].astype(BF16)

        def chunk(c, carry):
            rows = pl.ds(pl.multiple_of(c * CHUNK, CHUNK), CHUNK)
            band = pl.ds(pl.multiple_of(c * CHUNK, CHUNK), BAND)
            for pr in range(2):
                kp_ref, vp_ref = scratch[2 * pr:2 * pr + 2]
                lanes = slice(128 * pr, 128 * (pr + 1))
                p = _band_probs_t(kp_ref[band, :], _block_diag(q_ref[rows, lanes]), bias_ref[pr], c).astype(BF16)
                p_ref[pr, c] = p
                o2 = lax.dot_general(p, vp_ref[band, :], (TN, ((), ())), preferred_element_type=F32)
                y_ref[rows, lanes] = _diag_blocks(o2).astype(BF16)
            return carry

        lax.fori_loop(0, n_chunks, chunk, 0, unroll=2)

    def col(base):
        return pl.BlockSpec((T, 256), lambda h: (0, base // 2 + h))

    return _call(
        body, (z, z, z, bias_t), name=name, grid=(AT_HEADS // 4,),
        in_specs=[col(COL_AQ), col(COL_AK), col(COL_AV), pl.BlockSpec((2, BAND, 128), lambda h: (h, 0, 0))],
        out_specs=[col(0), pl.BlockSpec((2, n_chunks, BAND, 128), lambda h: (h, 0, 0, 0))],
        out_shape=[jax.ShapeDtypeStruct((T, AT_WIDTH), BF16),
                   jax.ShapeDtypeStruct((AT_HEADS // 2, n_chunks, BAND, 128), BF16)],
        scratch_shapes=[pltpu.VMEM((PAD + T, 128), BF16)] * 4,
        sem=("parallel",), after=after)


def _attn_bwd(z, probs, dyb, name, after=()):
    T = z.shape[0]
    n_chunks = T // CHUNK

    def body(q_ref, k_ref, v_ref, p_ref, dy_ref, dq_ref, dk_ref, dv_ref, dbias_ref, *scratch):
        dbias_ref[...] = jnp.zeros_like(dbias_ref)
        for pr in range(2):
            kp_ref, vp_ref, dkp_ref, dvp_ref = scratch[4 * pr:4 * pr + 4]
            lanes = slice(128 * pr, 128 * (pr + 1))
            kp_ref[0:PAD, :] = jnp.zeros((PAD, 128), BF16)
            vp_ref[0:PAD, :] = jnp.zeros((PAD, 128), BF16)
            kp_ref[PAD:PAD + T, :] = k_ref[:, lanes].astype(BF16)
            vp_ref[PAD:PAD + T, :] = v_ref[:, lanes].astype(BF16)
            dkp_ref[...] = jnp.zeros_like(dkp_ref)
            dvp_ref[...] = jnp.zeros_like(dvp_ref)

        def chunk(c, carry):
            rows = pl.ds(pl.multiple_of(c * CHUNK, CHUNK), CHUNK)
            band = pl.ds(pl.multiple_of(c * CHUNK, CHUNK), BAND)
            for pr in range(2):
                kp_ref, vp_ref, dkp_ref, dvp_ref = scratch[4 * pr:4 * pr + 4]
                lanes = slice(128 * pr, 128 * (pr + 1))
                qbd = _block_diag(q_ref[rows, lanes])
                dobd = _block_diag(dy_ref[rows, lanes])
                pb = p_ref[pr, c]
                p = pb.astype(F32)
                dp = lax.dot_general(vp_ref[band, :], dobd, (NT, ((), ())), preferred_element_type=F32)
                ds = p * (dp - jnp.sum(dp * p, axis=0, keepdims=True))
                dbias_ref[pr] += ds
                dsb = ds.astype(BF16)
                dq2 = lax.dot_general(dsb, kp_ref[band, :], (TN, ((), ())), preferred_element_type=F32)
                dq_ref[rows, lanes] = (_diag_blocks(dq2) * (AT_DH ** -0.5)).astype(BF16)
                dkp_ref[band, :] += jnp.dot(dsb, qbd, preferred_element_type=F32) * (AT_DH ** -0.5)
                dvp_ref[band, :] += jnp.dot(pb, dobd, preferred_element_type=F32)
            return carry

        lax.fori_loop(0, n_chunks, chunk, 0)
        for pr in range(2):
            lanes = slice(128 * pr, 128 * (pr + 1))
            dk_ref[:, lanes] = scratch[4 * pr + 2][PAD:PAD + T, :].astype(BF16)
            dv_ref[:, lanes] = scratch[4 * pr + 3][PAD:PAD + T, :].astype(BF16)

    def col(base):
        return pl.BlockSpec((T, 256), lambda h: (0, base // 2 + h))

    outb = jax.ShapeDtypeStruct((T, AT_WIDTH), BF16)
    return _call(
        body, (z, z, z, probs, dyb), name=name, grid=(AT_HEADS // 4,),
        in_specs=[col(COL_AQ), col(COL_AK), col(COL_AV),
                  pl.BlockSpec((2, n_chunks, BAND, 128), lambda h: (h, 0, 0, 0)), col(0)],
        out_specs=[col(0), col(0), col(0), pl.BlockSpec((2, BAND, 128), lambda h: (h, 0, 0))],
        out_shape=[outb, outb, outb, jax.ShapeDtypeStruct((AT_HEADS // 2, BAND, 128), F32)],
        scratch_shapes=([pltpu.VMEM((PAD + T, 128), BF16)] * 2 + [pltpu.VMEM((PAD + T, 128), F32)] * 2) * 2,
        sem=("parallel",), after=after)


def _local_step(x, target, lb_logits, hg_norm_w, rel_bias, norm_mix_w, norm_mlp_w, norm_final_w,
                w_in, rest, exchanges=None):
    ex = exchanges
    rel = jnp.pad(rel_bias, ((0, 0), (0, N_REL_PAD - N_REL)))

    u = _rms_fwd(x, norm_mix_w, "rms_mix_fwd")
    if ex:
        z, w_in = _mm_gathered(u, w_in, ex.order, "mm_in_fwd")
        gather = _Gather(rest, [w_in], "ag")
        z = _mm_gathered_tail(u, w_in, z, ex.order, "mm_in_fwd_tail", after=[gather.token])
        tok = []
    else:
        z = _mm_nn(u, w_in, F32, "mm_in_fwd")
        w_a, w_b, w_out, w_up, w_down = rest
        tok = []
    o_raw, y_a, s_all = _hgrn2_fwd(z, lb_logits, hg_norm_w, "hgrn2_fwd", after=tok)
    if ex:
        tok = [gather.pass_on([0, 1, 2], [o_raw], "abo")]
    bias_rows = _bias_expand(rel, "bias_expand")
    bias_t = jnp.transpose(bias_rows.reshape(CHUNK, AT_HEADS // 2, 2, BAND), (1, 3, 2, 0)).reshape(
        AT_HEADS // 2, BAND, 2 * CHUNK)
    y_b, probs = _attn_fwd(z, bias_t, "attn_fwd", after=tok)
    if ex:
        tok = [gather.pass_on([3], [y_b], "up")]
        w_a, w_b, w_out = gather.finish([0, 1, 2], tok, "abo")
    pa = _mm_nn(y_a, w_a, F32, "mm_a_fwd")
    pb, merged = _mm_nn(y_b, w_b, None, "mm_b_fwd", epilogue=(
        (z, z, pa), (COL_GATE_A * GATE_TILE, COL_GATE_B * GATE_TILE, 0), (F32, BF16), _gated_merge))
    w_out1 = w_out.reshape(1, D_MODEL, D_MODEL)
    h1, u2 = _mm_rows(merged, w_out.reshape(D_MODEL, D_MODEL), [x], [norm_mlp_w], (F32, BF16),
                      _residual_rms_rows, "mm_out_fwd")
    if ex:
        tok = [gather.pass_on([4], [u2], "down")]
        w_up, = gather.finish([3], tok, "up")
    a, r = _mm_nn(u2, w_up, None, "mm_up_fwd", epilogue=((), (), (F32, BF16), _squared_relu))
    if ex:
        w_down, = gather.finish([4], [r], "down")
    w_down1 = w_down.reshape(1, D_FF, D_MODEL)
    mlp = _mm_nn(r, w_down1, F32, "mm_down_fwd")
    loss, dh2, dh2b, g_nf = _loss_head(h1, mlp, norm_final_w, target, "loss_head")

    own = ex.parity if ex else jnp.zeros((1,), jnp.int32)

    def sibling_half(weights, name, after=()):
        others = [_mm_tn_half(a_, g_, 1 - own, on, None, nm + "_sibling", after, *cols)
                  for a_, g_, on, nm, *cols in weights]
        rs = _ReduceScatter(others, name) if ex else None
        return rs, others, ([rs.token] if ex else [])

    def own_half(rs, weights, others, after):
        landed = rs.from_sibling(after) if ex else [None] * len(weights)
        sums = [_mm_tn_half(a_, g_, own, on, l, nm + "_own", (), *cols)
                for (a_, g_, on, nm, *cols), l in zip(weights, landed)]
        if ex:
            return [rs.scatter(sums)], None
        return [], [jnp.stack([s_, o_], axis=1).reshape((N_DEV,) + s_.shape[1:]) for s_, o_ in zip(sums, others)]

    down = [(r, dh2b, "a", "mm_down_wgrad")]
    rs_down, others, tok = sibling_half(down, "rs_down")
    da, = _mm_nt(dh2b, w_down1, None, "mm_down_dgrad", after=tok, epilogue=(
        (a,), (0,), (BF16,), lambda dr, av: (dr * (2.0 * jnp.maximum(av, 0.0)),)))
    tok, g_down = own_half(rs_down, down, others, [da])
    up = [(u2, da, "g", "mm_up_wgrad")]
    rs_up, others, tok = sibling_half(up, "rs_up", tok)
    du2 = _mm_nt(da, w_up, F32, "mm_up_dgrad", after=tok)
    tok, g_up = own_half(rs_up, up, others, [du2])
    dh1, dh1b, g_nmlp = _rms_bwd(du2, h1, norm_mlp_w, dh2, (F32, BF16), "rms_mlp_bwd", after=tok)

    dpa, dpb, dga, dgb = _mm_nt(dh1b, w_out1, None, "mm_out_dgrad", epilogue=(
        (z, z, pa, pb), (COL_GATE_A * GATE_TILE, COL_GATE_B * GATE_TILE, 0, 0), (BF16,) * 4, _merge_grads))
    mix = [(y_a, dpa, "g", "mm_a_wgrad"), (y_b, dpb, "g", "mm_b_wgrad"), (merged, dh1b, "a", "mm_out_wgrad")]
    rs_mix, others, tok = sibling_half(mix, "rs_mix")
    dya = _mm_nt(dpa, w_a, F32, "mm_a_dgrad", after=tok)
    dyb = _mm_nt(dpb, w_b, F32, "mm_b_dgrad", after=tok)
    tok, g_mix = own_half(rs_mix, mix, others, [dya, dyb])
    daq, dak, dav, dbias_t = _attn_bwd(z, probs, dyb, "attn_bwd", after=tok)
    dhq, dhf, dhi, dhg, g_lbl, g_hgw = _hgrn2_bwd(z, lb_logits, hg_norm_w, o_raw, s_all, dya, "hgrn2_bwd",
                                                  after=tok)
    dbias_rows = jnp.pad(jnp.transpose(dbias_t.reshape(AT_HEADS // 2, BAND, 2, CHUNK), (3, 0, 2, 1)).reshape(
        CHUNK, AT_HEADS, BAND), ((0, 0), (0, 0), (0, N_REL_PAD - BAND)))
    dz = jnp.concatenate([dhq, dhf, dhi, dhg, daq, dak, dav, dga, dgb], axis=1)
    half = D_MODEL // 2
    lo = [(u, dz, "g", "mm_in_wgrad_lo", (0, half))]
    hi = [(u, dz, "g", "mm_in_wgrad_hi", (half, half))]
    rs_in_lo, others_lo, tok = sibling_half(lo, "rs_in_lo")
    rs_in_hi, others_hi, tok = sibling_half(hi, "rs_in_hi", tok)
    tok, g_in_lo = own_half(rs_in_lo, lo, others_lo, tok)
    du = _mm_nt(dz, w_in, F32, "mm_in_dgrad", after=tok)
    tok, g_in_hi = own_half(rs_in_hi, hi, others_hi, [du])
    grad_x, g_nmix = _rms_bwd(du, x, norm_mix_w, dh1, (F32,), "rms_mix_bwd", after=tok)
    g_rel = _bias_reduce(dbias_rows, "bias_reduce", after=tok)[:, :N_REL]

    small = dict(lb_logits=g_lbl, hg_norm_w=g_hgw[0:1], rel_bias=g_rel, norm_mix_w=g_nmix, norm_mlp_w=g_nmlp,
                 norm_final_w=g_nf)
    if ex:
        grads = [(rs_in_lo, rs_in_hi), rs_mix, rs_up, rs_down]
    else:
        grads = [jnp.concatenate([g_in_lo[0], g_in_hi[0]], axis=1)] + g_mix + [g_up[0], g_down[0]]
    return loss, grad_x, grads, small


def _mm_gathered(u, shard, order, name):
    T, K = u.shape
    _, Nb = shard.shape

    def body(order_ref, u_ref, shard_ref, z_ref, full_ref, wbuf, load_sem, send_sems, recv_sems, local_sem):
        s = pl.program_id(0)
        x, y, c = _position()
        me, sibling = (x, y, c), (x, y, 1 - c)
        chips = [(1 - x, y), (x, 1 - y), (1 - x, 1 - y)]

        def copy(k, block, to, src=None):
            dst = full_ref.at[4 * block[0] + 2 * block[1] + block[2]]
            return pltpu.make_async_remote_copy(
                src_ref=dst if src is None else src, dst_ref=dst,
                send_sem=send_sems.at[k], recv_sem=recv_sems.at[k], device_id=to, device_id_type=MESH)

        @pl.when(s == 0)
        def _():
            local = pltpu.make_async_copy(shard_ref, full_ref.at[4 * x + 2 * y + c], local_sem)
            local.start()
            copy(0, me, sibling, src=shard_ref).start()
            for j, chip in enumerate(chips):
                copy(1 + j, me, (*chip, c), src=shard_ref).start()
            local.wait()

        @pl.when(s == 1)
        def _():
            copy(0, sibling, me).wait_recv()

        for j, chip in enumerate(chips):
            direct, passed = ((2, 4), (3, 5), (6, 7))[j]

            @pl.when(s == direct)
            def _(j=j, chip=chip):
                copy(1 + j, (*chip, c), me).wait_recv()
                copy(4 + j, (*chip, c), sibling).start()

            @pl.when(s == passed)
            def _(j=j, chip=chip):
                copy(4 + j, (*chip, 1 - c), me).wait_recv()

        @pl.when(s < N_EARLY_BLOCKS)
        def _():
            load = pltpu.make_async_copy(full_ref.at[order_ref[s]], wbuf, load_sem)
            load.start()
            load.wait()
            z_ref[...] = jnp.dot(u_ref[...], wbuf[...], preferred_element_type=F32)

        @pl.when(s == N_DEV - 1)
        def _():
            for k in range(7):
                copy(k, me, sibling).wait_send()

    z, full = pl.pallas_call(
        body, name=name,
        grid_spec=pltpu.PrefetchScalarGridSpec(
            num_scalar_prefetch=1, grid=(N_DEV,),
            in_specs=[pl.BlockSpec((T, K), lambda s, order: (0, 0)), ANY],
            out_specs=[pl.BlockSpec((T, Nb), lambda s, order: (0, order[jnp.minimum(s, N_EARLY_BLOCKS - 1)])), ANY],
            scratch_shapes=[pltpu.VMEM((K, Nb), BF16), pltpu.SemaphoreType.DMA,
                            pltpu.SemaphoreType.DMA((7,)), pltpu.SemaphoreType.DMA((7,)), pltpu.SemaphoreType.DMA]),
        out_shape=[jax.ShapeDtypeStruct((T, N_DEV * Nb), F32), jax.ShapeDtypeStruct((N_DEV, K, Nb), BF16)],
        compiler_params=_cparams(("arbitrary",)),
    )(order, u, shard)
    return z, full


N_EARLY_BLOCKS = 6


def _mm_gathered_tail(u, full, z, order, name, after=()):
    T, K = u.shape
    _, _, Nb = full.shape
    n_after = len(after)

    def body(order_ref, u_ref, w_ref, z_in_ref, *rest):
        rest[n_after][...] = jnp.dot(u_ref[...], w_ref[...], preferred_element_type=F32)

    return pl.pallas_call(
        body, name=name,
        grid_spec=pltpu.PrefetchScalarGridSpec(
            num_scalar_prefetch=1, grid=(N_DEV - N_EARLY_BLOCKS,),
            in_specs=[pl.BlockSpec((T, K), lambda s, order: (0, 0)),
                      pl.BlockSpec((None, K, Nb), lambda s, order: (order[N_EARLY_BLOCKS + s], 0, 0)), ANY]
            + [ANY] * n_after,
            out_specs=pl.BlockSpec((T, Nb), lambda s, order: (0, order[N_EARLY_BLOCKS + s]))),
        out_shape=jax.ShapeDtypeStruct(z.shape, z.dtype),
        input_output_aliases={3: 0},
        compiler_params=_cparams(("arbitrary",)),
    )(order, u, full, z, *after)


def _gather_order():
    x, y, c = _position()
    chips = [(1 - x, y), (x, 1 - y), (1 - x, 1 - y)]
    ids = [4 * x + 2 * y + c, 4 * x + 2 * y + (1 - c)]
    ids += [4 * cx + 2 * cy + c for cx, cy in chips[:2]] + [4 * cx + 2 * cy + (1 - c) for cx, cy in chips[:2]]
    ids += [4 * chips[2][0] + 2 * chips[2][1] + c, 4 * chips[2][0] + 2 * chips[2][1] + (1 - c)]
    return jnp.stack(ids).astype(jnp.int32)


HBM = pl.BlockSpec(memory_space=pltpu.HBM)
SEM = pl.BlockSpec(memory_space=pltpu.SEMAPHORE)
DATAFLOW = pltpu.SideEffectType.DATAFLOW_SIDE_EFFECTING


def _split_call(name, bufs, waits=(), starts=None, after=()):
    nb = len(bufs)
    n_new = starts[1] if starts else 0
    wait_sems = [s for w in waits for s in (*w[1], *w[2])]

    def body(*refs):
        b, pos = refs[:nb], nb
        for plan, ss, _, send_idx, recv_idx in waits:
            k = len(ss)
            copies = plan(b, refs[pos:pos + k], refs[pos + k:pos + 2 * k])
            pos += 2 * k
            for i in recv_idx:
                copies[i].wait_recv()
            for i in send_idx:
                copies[i].wait_send()
        outs = refs[pos + len(after):]
        if starts:
            for cp in starts[0](b, outs[nb:nb + n_new], outs[nb + n_new:nb + 2 * n_new]):
                cp.start()
        outs[-1][...] = jnp.zeros_like(outs[-1])

    res = pl.pallas_call(
        body, name=name,
        out_shape=tuple(pltpu.HBM(a.shape, a.dtype) for a in bufs) + (pltpu.SemaphoreType.DMA(()),) * (2 * n_new)
        + (jax.ShapeDtypeStruct((8, 128), F32),),
        in_specs=[HBM] * nb + [SEM] * len(wait_sems) + [ANY] * len(after),
        out_specs=(HBM,) * nb + (SEM,) * (2 * n_new) + (pl.BlockSpec(memory_space=pltpu.VMEM),),
        input_output_aliases={i: i for i in range(nb)},
        compiler_params=pltpu.CompilerParams(has_side_effects=DATAFLOW),
    )(*bufs, *wait_sems, *after)
    return list(res[:nb]), list(res[nb:nb + n_new]), list(res[nb + n_new:nb + 2 * n_new]), res[-1]


def _in_hbm(a):
    return pltpu.with_memory_space_constraint(a, pltpu.HBM)


def _remote(src, dst, send_sem, recv_sem, to):
    return pltpu.make_async_remote_copy(src_ref=src, dst_ref=dst, send_sem=send_sem, recv_sem=recv_sem,
                                        device_id=to, device_id_type=MESH)


def _other_chips():
    x, y, _ = _position()
    return [(1 - x, y), (x, 1 - y), (1 - x, 1 - y)]


def _plan_gather_first(n):
    def plan(b, ss, rs):
        x, y, c = _position()
        to = [(x, y, 1 - c)] + [(*chip, c) for chip in _other_chips()]
        return [_remote(b[w], b[n + w].at[4 * x + 2 * y + c], ss[4 * w + k], rs[4 * w + k], to[k])
                for w in range(n) for k in range(4)]
    return plan, 4 * n


def _plan_gather_pass(n):
    def plan(b, ss, rs):
        x, y, c = _position()
        copies = []
        for w in range(n):
            for j, chip in enumerate(_other_chips()):
                blk = b[n + w].at[4 * chip[0] + 2 * chip[1] + c]
                copies.append(_remote(blk, blk, ss[3 * w + j], rs[3 * w + j], (x, y, 1 - c)))
        return copies
    return plan, 3 * n


def _plan_sibling(n):
    def plan(b, ss, rs):
        x, y, c = _position()
        return [_remote(b[w].at[s], b[n + w].at[s], ss[4 * w + s], rs[4 * w + s], (x, y, 1 - c))
                for w in range(n) for s in range(N_CHIP)]
    return plan, 4 * n


def _plan_scatter(n):
    def plan(b, ss, rs):
        x, y, c = _position()
        return [_remote(b[w].at[2 * chip[0] + chip[1]], b[n + w].at[2 * x + y], ss[3 * w + j], rs[3 * w + j],
                        (*chip, c))
                for w in range(n) for j, chip in enumerate(_other_chips())]
    return plan, 3 * n


class _Gather:
    def __init__(self, shards, after, name):
        self.n, self.name = len(shards), name
        x, y, c = _position()
        placed = [lax.dynamic_update_index_in_dim(lax.empty((N_DEV,) + s.shape, s.dtype), s, 4 * x + 2 * y + c, 0)
                  for s in shards]
        bufs, self.ss, self.rs, self.token = _split_call(
            name + "_start", [_in_hbm(a) for a in list(shards) + placed], starts=_plan_gather_first(self.n),
            after=after)
        self.shards, self.fulls = bufs[:self.n], bufs[self.n:]
        self.passed = {}

    def _sub(self, ids, sems, per):
        return [sems[per * w + k] for w in ids for k in range(per)]

    def pass_on(self, ids, after, tag):
        m = len(ids)
        first = (_plan_gather_first(m)[0], self._sub(ids, self.ss, 4), self._sub(ids, self.rs, 4),
                 [], [4 * i + k for i in range(m) for k in (1, 2, 3)])
        bufs, ss, rs, token = _split_call(
            "%s_pass_%s" % (self.name, tag), [self.shards[w] for w in ids] + [self.fulls[w] for w in ids],
            waits=[first], starts=_plan_gather_pass(m), after=after)
        for i, w in enumerate(ids):
            self.shards[w], self.fulls[w] = bufs[i], bufs[m + i]
        self.passed[tuple(ids)] = (ss, rs)
        return token

    def finish(self, ids, after, tag):
        m = len(ids)
        ss2, rs2 = self.passed[tuple(ids)]
        first = (_plan_gather_first(m)[0], self._sub(ids, self.ss, 4), self._sub(ids, self.rs, 4),
                 list(range(4 * m)), [4 * i for i in range(m)])
        passed = (_plan_gather_pass(m)[0], ss2, rs2, list(range(3 * m)), list(range(3 * m)))
        bufs, _, _, _ = _split_call(
            "%s_finish_%s" % (self.name, tag), [self.shards[w] for w in ids] + [self.fulls[w] for w in ids],
            waits=[first, passed], after=after)
        return bufs[m:]


class _ReduceScatter:
    def __init__(self, others, name):
        self.n, self.name = len(others), name
        lands = [lax.empty(g.shape, g.dtype) for g in others]
        self.bufs, self.ss, self.rs, self.token = _split_call(
            name + "_sibling_start", [_in_hbm(a) for a in list(others) + lands], starts=_plan_sibling(self.n))

    def from_sibling(self, after):
        n = self.n
        bufs, _, _, _ = _split_call(
            self.name + "_sibling_wait", self.bufs,
            waits=[(_plan_sibling(n)[0], self.ss, self.rs, list(range(4 * n)), list(range(4 * n)))], after=after)
        return bufs[n:]

    def scatter(self, sums):
        lands = [lax.empty(s.shape, s.dtype) for s in sums]
        self.bufs, self.ss, self.rs, token = _split_call(
            self.name + "_scatter_start", [_in_hbm(a) for a in list(sums) + lands], starts=_plan_scatter(self.n))
        return token

    def finish(self, after):
        n = self.n
        bufs, _, _, _ = _split_call(
            self.name + "_scatter_wait", self.bufs,
            waits=[(_plan_scatter(n)[0], self.ss, self.rs, list(range(3 * n)), list(range(3 * n)))], after=after)
        return bufs[:n], bufs[n:]


class _Exchanges:
    def __init__(self, parity, order):
        self.parity, self.order = parity, order


def _gather_small(packed, name):
    R = packed.shape[0]

    def body(x_ref, out_ref, send_sems, recv_sems):
        x, y, c = _position()
        me = 4 * x + 2 * y + c
        out_ref[me] = x_ref[...]
        copies = []
        for k in range(1, N_DEV):
            to = (x ^ ((k >> 2) & 1), y ^ ((k >> 1) & 1), c ^ (k & 1))
            cp = pltpu.make_async_remote_copy(
                src_ref=x_ref, dst_ref=out_ref.at[me],
                send_sem=send_sems.at[k], recv_sem=recv_sems.at[k], device_id=to, device_id_type=MESH)
            cp.start()
            copies.append((k, to, cp))
        for k, to, cp in copies:
            cp.wait_send()
            pltpu.make_async_remote_copy(
                src_ref=x_ref, dst_ref=out_ref.at[4 * to[0] + 2 * to[1] + to[2]],
                send_sem=send_sems.at[k], recv_sem=recv_sems.at[k], device_id=to, device_id_type=MESH).wait_recv()

    return pl.pallas_call(
        body, name=name,
        in_specs=[pl.BlockSpec(memory_space=pltpu.VMEM)], out_specs=pl.BlockSpec(memory_space=pltpu.VMEM),
        out_shape=jax.ShapeDtypeStruct((N_DEV, R, 128), F32),
        scratch_shapes=[pltpu.SemaphoreType.DMA((N_DEV,)), pltpu.SemaphoreType.DMA((N_DEV,))],
    )(packed)


def _adamw_math(w, g, m, v):
    m = ADAM_B1 * m + (1.0 - ADAM_B1) * g
    v = ADAM_B2 * v + (1.0 - ADAM_B2) * (g * g)
    m_hat = m / (1.0 - ADAM_B1 ** ADAM_STEP)
    v_hat = v / (1.0 - ADAM_B2 ** ADAM_STEP)
    delta = -ADAM_LR * (m_hat / (jnp.sqrt(v_hat) + ADAM_EPS) + ADAM_WD * w)
    return delta, m, v


def _adamw_big_landed(w, m, v, parts, lands, slot, name, row0=0, into=None):
    R, C = w.shape
    rows = parts.shape[1]
    tr = _pick(rows, (256,))
    first = row0 // tr
    n_into = len(into) if into else 0

    def body(slot_ref, w_ref, m_ref, v_ref, own_ref, l1_ref, l2_ref, l3_ref, *rest):
        g = own_ref[...].astype(F32)
        for ref in (l1_ref, l2_ref, l3_ref):
            g = g + ref[...].astype(F32)
        for o_ref, res in zip(rest[n_into:], (g,) + _adamw_math(w_ref[...], g, m_ref[...], v_ref[...])):
            o_ref[...] = res

    blk = pl.BlockSpec((tr, C), lambda i, slot: (first + i, 0))

    def chip(k):
        return pl.BlockSpec((None, tr, C), lambda i, slot: ((slot[0] + k) % N_CHIP, i, 0))

    out = jax.ShapeDtypeStruct((R, C), F32)
    return pl.pallas_call(
        body, name=name,
        grid_spec=pltpu.PrefetchScalarGridSpec(
            num_scalar_prefetch=1, grid=(rows // tr,),
            in_specs=[blk, blk, blk, chip(0), chip(1), chip(2), chip(3)] + [ANY] * n_into,
            out_specs=[blk, blk, blk, blk]),
        out_shape=[out, out, out, out],
        input_output_aliases={8 + j: j for j in range(n_into)},
        compiler_params=_cparams(("parallel",)),
    )(slot, w, m, v, parts, lands, lands, lands, *(into or ()))


def _adamw_small(w, m, v, gathered, name):
    R = w.shape[0]

    def body(w_ref, m_ref, v_ref, p_ref, g_ref, d_ref, nm_ref, nv_ref):
        g = p_ref[0]
        for s in range(1, N_DEV):
            g = g + p_ref[s]
        d, nm, nv = _adamw_math(w_ref[...], g, m_ref[...], v_ref[...])
        g_ref[...] = g
        d_ref[...] = d
        nm_ref[...] = nm
        nv_ref[...] = nv

    out = jax.ShapeDtypeStruct((R, 128), F32)
    return pl.pallas_call(
        body, name=name, out_shape=[out, out, out, out],
    )(w, m, v, gathered)


SMALL_NAMES = ("lb_logits", "hg_norm_w", "rel_bias", "norm_mix_w", "norm_mlp_w", "norm_final_w")
SMALL_SHAPES = {"lb_logits": (2, HG_WIDTH), "hg_norm_w": (1, HG_DK), "rel_bias": (AT_HEADS, N_REL_PAD),
                "norm_mix_w": (1, D_MODEL), "norm_mlp_w": (1, D_MODEL), "norm_final_w": (1, D_MODEL)}


def _pack_small(parts):
    rows = []
    for nme in SMALL_NAMES:
        p = parts[nme]
        if nme == "rel_bias":
            p = jnp.pad(p, ((0, 0), (0, N_REL_PAD - N_REL)))
        rows.append(p.reshape(-1, 128))
    flat = jnp.concatenate(rows, axis=0)
    return jnp.pad(flat, ((0, SMALL_ROWS - flat.shape[0]), (0, 0)))


def _unpack_small(packed):
    out, at = {}, 0
    for nme in SMALL_NAMES:
        shp = SMALL_SHAPES[nme]
        nrow = shp[0] * shp[1] // 128
        p = packed[at:at + nrow].reshape(shp)
        at += nrow
        out[nme] = p[:, :N_REL] if nme == "rel_bias" else p
    return out


BIG_NAMES = ("w_in", "w_branch_a", "w_branch_b", "w_out", "w_up", "w_down")


def kernel(x, w_in, lb_logits, hg_norm_w, rel_bias, w_branch_a, w_branch_b, w_out, norm_mix_w, norm_mlp_w, w_up, w_down, norm_final_w, loss_target, m_w_in, m_lb_logits, m_hg_norm_w, m_rel_bias, m_w_branch_a, m_w_branch_b, m_w_out, m_norm_mix_w, m_norm_mlp_w, m_w_up, m_w_down, m_norm_final_w, v_w_in, v_lb_logits, v_hg_norm_w, v_rel_bias, v_w_branch_a, v_w_branch_b, v_w_out, v_norm_mix_w, v_norm_mlp_w, v_w_up, v_w_down, v_norm_final_w):
    big_w = [w_in[0], w_branch_a[0], w_branch_b[0], w_out[0], w_up[0], w_down[0]]
    big_m = [m_w_in[0], m_w_branch_a[0], m_w_branch_b[0], m_w_out[0], m_w_up[0], m_w_down[0]]
    big_v = [v_w_in[0], v_w_branch_a[0], v_w_branch_b[0], v_w_out[0], v_w_up[0], v_w_down[0]]

    shards = [w.astype(BF16) for w in big_w]
    parity = lax.axis_index("c").astype(jnp.int32).reshape(1)
    loss_part, grad_x, chip_parts, small = _local_step(
        x[0], loss_target[0], lb_logits, hg_norm_w, rel_bias[0], norm_mix_w, norm_mlp_w,
        norm_final_w.reshape(1, D_MODEL), shards[0], shards[1:], _Exchanges(parity, _gather_order()))
    loss = lax.psum(loss_part[0, 0], ("x", "y", "c"))
    (rs_in_lo, rs_in_hi), rs_mix, rs_up, rs_down = chip_parts
    slot =(2 * lax.axis_index("x") + lax.axis_index("y")).astype(jnp.int32).reshape(1)
    big = {}

    def finish(rs, names, after):
        sums, lands = rs.finish(after)
        for nme, own, land in zip(names, sums, lands):
            i = BIG_NAMES.index(nme)
            big[nme] = _adamw_big_landed(big_w[i], big_m[i], big_v[i], own, land, slot, "adamw_" + nme)
        return [big[nme][1] for nme in names]

    done = finish(rs_down, ["w_down"], [grad_x])
    done = finish(rs_up, ["w_up"], done)
    done = finish(rs_mix, ["w_branch_a", "w_branch_b", "w_out"], done)

    sw = dict(lb_logits=lb_logits, hg_norm_w=hg_norm_w, rel_bias=rel_bias[0], norm_mix_w=norm_mix_w,
              norm_mlp_w=norm_mlp_w, norm_final_w=norm_final_w.reshape(1, D_MODEL))
    sm = dict(lb_logits=m_lb_logits, hg_norm_w=m_hg_norm_w, rel_bias=m_rel_bias[0], norm_mix_w=m_norm_mix_w,
              norm_mlp_w=m_norm_mlp_w, norm_final_w=m_norm_final_w.reshape(1, D_MODEL))
    sv = dict(lb_logits=v_lb_logits, hg_norm_w=v_hg_norm_w, rel_bias=v_rel_bias[0], norm_mix_w=v_norm_mix_w,
              norm_mlp_w=v_norm_mlp_w, norm_final_w=v_norm_final_w.reshape(1, D_MODEL))
    gathered = _gather_small(_pack_small(small), "gather_small")
    small_packed = _adamw_small(_pack_small(sw), _pack_small(sm), _pack_small(sv), gathered, "adamw_small")
    small_out = [_unpack_small(p) for p in small_packed]

    (own,), (land,) = rs_in_lo.finish(done + [small_packed[0]])
    lo = _adamw_big_landed(big_w[0], big_m[0], big_v[0], own, land, slot, "adamw_w_in_lo")
    (own,), (land,) = rs_in_hi.finish([lo[1]])
    big["w_in"] = _adamw_big_landed(big_w[0], big_m[0], big_v[0], own, land, slot, "adamw_w_in_hi",
                                    row0=D_MODEL // 2, into=lo)

    def leaf(kind, nme):
        if nme in BIG_NAMES:
            return big[nme][kind][None]
        p = small_out[kind][nme]
        if nme == "rel_bias":
            return p[None]
        if nme == "norm_final_w":
            return p.reshape(D_MODEL)
        return p

    order = ("w_in", "lb_logits", "hg_norm_w", "rel_bias", "w_branch_a", "w_branch_b", "w_out", "norm_mix_w",
             "norm_mlp_w", "w_up", "w_down", "norm_final_w")
    outs = [loss, grad_x[None]]
    for kind in range(4):
        outs += [leaf(kind, nme) for nme in order]
    return tuple(outs)
```

```python
import jax
import jax.numpy as jnp
from jax import lax
from jax.experimental import pallas as pl
from jax.experimental.pallas import tpu as pltpu

F32 = jnp.float32
BF16 = jnp.bfloat16
HIGHEST = lax.Precision.HIGHEST
MESH = pl.DeviceIdType.MESH

D_MODEL = 2048
HG_HEADS = 8
HG_DK = 128
HG_WIDTH = 1024
AT_HEADS = 16
AT_DH = 64
AT_WIDTH = 1024
CHUNK = 64
LEFT_CHUNKS = 8
BAND = (LEFT_CHUNKS + 1) * CHUNK
PAD = LEFT_CHUNKS * CHUNK
REL_CLIP = 256
N_REL = 2 * REL_CLIP + 1
N_REL_PAD = 640
D_FF = 4 * D_MODEL
EPS = 1e-6
N_DEV = 8
N_CHIP = 4

ADAM_LR = 0.001
ADAM_B1 = 0.9
ADAM_B2 = 0.999
ADAM_EPS = 1e-08
ADAM_WD = 0.01
ADAM_STEP = 10

COL_HQ, COL_HF, COL_HI, COL_HG = 0, 8, 16, 24
COL_AQ, COL_AK, COL_AV = 32, 40, 48
COL_GATE_A, COL_GATE_B = 7, 9

VMEM_LIMIT = 56 * 1024 * 1024
SMALL_ROWS = 152


def _cparams(sem=None, **kw):
    if sem is not None:
        kw["dimension_semantics"] = sem
    return pltpu.CompilerParams(vmem_limit_bytes=VMEM_LIMIT, **kw)


def _pick(n, cands):
    for c in cands:
        if n % c == 0:
            return c
    return n


def _sigmoid(x):
    return 1.0 / (1.0 + jnp.exp(-x))


ANY = pl.BlockSpec(memory_space=pl.ANY)


def _position():
    return lax.axis_index("x"), lax.axis_index("y"), lax.axis_index("c")


def _call(body, args, *, name, grid, in_specs, out_specs, out_shape, scratch_shapes=(), sem=None, after=()):
    n_in = len(args)

    def ordered(*refs):
        body(*refs[:n_in], *refs[n_in + len(after):])

    return list(pl.pallas_call(
        ordered if after else body, name=name, grid=grid, in_specs=list(in_specs) + [ANY] * len(after),
        out_specs=out_specs, out_shape=out_shape, scratch_shapes=list(scratch_shapes),
        compiler_params=_cparams(sem))(*args, *after))


MAX_CONTRACTION_TILE = 4096


def _accumulate(part, acc_ref, step, n_steps, finish):
    if n_steps == 1:
        finish(part)
        return

    @pl.when(step == 0)
    def _():
        acc_ref[...] = part

    @pl.when(step > 0)
    def _():
        acc_ref[...] += part

    @pl.when(step == n_steps - 1)
    def _():
        finish(acc_ref[...])


def _mm_nn(a, wb, out_dtype, name, after=(), epilogue=None):
    M, K = a.shape
    NB, K2, Nb = wb.shape
    assert K == K2
    tm = min(M, 1024)
    tk = min(K, MAX_CONTRACTION_TILE)
    tn = _pick(Nb, (512, 1408, 256))
    nk = K // tk
    nn = Nb // tn
    extra, first_cols, out_dtypes, fn = epilogue or ((), (), (out_dtype,), lambda total: (total,))
    n_extra, n_out = len(extra), len(out_dtypes)

    def body(a_ref, b_ref, *rest):
        def finish(total):
            results = fn(total, *[r[...] for r in rest[:n_extra]])
            for o_ref, res, dt in zip(rest[n_extra:n_extra + n_out], results, out_dtypes):
                o_ref[...] = res.astype(dt)

        part = jnp.dot(a_ref[...], b_ref[...], preferred_element_type=F32)
        _accumulate(part, rest[-1], pl.program_id(3), nk, finish)

    def tile(first):
        return pl.BlockSpec((tm, tn), lambda m, j, n, k: (m, first + j * nn + n))

    outs = _call(
        body, (a, wb) + tuple(extra), name=name, grid=(M // tm, NB, nn, nk),
        in_specs=[pl.BlockSpec((tm, tk), lambda m, j, n, k: (m, k)),
                  pl.BlockSpec((None, tk, tn), lambda m, j, n, k: (j, k, n))] + [tile(col // tn) for col in first_cols],
        out_specs=[tile(0)] * n_out,
        out_shape=[jax.ShapeDtypeStruct((M, NB * Nb), dt) for dt in out_dtypes],
        scratch_shapes=[] if nk == 1 else [pltpu.VMEM((tm, tn), F32)],
        sem=("parallel", "parallel", "parallel", "arbitrary"), after=after)
    return outs if epilogue else outs[0]


def _squared_relu(a):
    ra = jnp.maximum(a, 0.0)
    return a, ra * ra


def _gated_merge(pb, za, zb, pa):
    return pb, _sigmoid(za) * pa + _sigmoid(zb) * pb


def _mm_nt(a, wb, out_dtype, name, after=(), epilogue=None):
    M, N = a.shape
    NB, K, Nb = wb.shape
    assert N == NB * Nb
    tm = min(M, 1024)
    n_tiles_live = 1 + (len(epilogue[0]) + len(epilogue[2]) if epilogue else 0)
    tko = _pick(K, (1024,)) if n_tiles_live <= 3 else _pick(K, (512,))
    tc = _pick(Nb, (2048, 1024, 1408, 256))
    nc = Nb // tc
    jb = max([d for d in (8, 4, 2, 1) if NB % d == 0 and d * tc <= MAX_CONTRACTION_TILE]) if nc == 1 else 1
    nsteps = (NB // jb) * nc
    extra, first_cols, out_dtypes, fn = epilogue or ((), (), (out_dtype,), lambda total: (total,))
    n_extra, n_out = len(extra), len(out_dtypes)

    def body(a_ref, b_ref, *rest):
        def finish(total):
            results = fn(total, *[r[...] for r in rest[:n_extra]])
            for o_ref, res, dt in zip(rest[n_extra:n_extra + n_out], results, out_dtypes):
                o_ref[...] = res.astype(dt)

        part = sum(lax.dot_general(a_ref[:, i * tc:(i + 1) * tc], b_ref[i], (((1,), (1,)), ((), ())),
                                   preferred_element_type=F32) for i in range(jb))
        _accumulate(part, rest[-1], pl.program_id(2) * nc + pl.program_id(3), nsteps, finish)

    def tile(first):
        return pl.BlockSpec((tm, tko), lambda m, ko, j, c: (m, first + ko))

    outs = _call(
        body, (a, wb) + tuple(extra), name=name,
        grid=(M // tm, K // tko, NB // jb, nc),
        in_specs=[pl.BlockSpec((tm, jb * tc), lambda m, ko, j, c: (m, j * nc + c)),
                  pl.BlockSpec((jb, tko, tc), lambda m, ko, j, c: (j, ko, c))] + [tile(col // tko) for col in first_cols],
        out_specs=[tile(0)] * n_out,
        out_shape=[jax.ShapeDtypeStruct((M, K), dt) for dt in out_dtypes],
        scratch_shapes=[] if nsteps == 1 else [pltpu.VMEM((tm, tko), F32)],
        sem=("parallel", "parallel", "arbitrary", "arbitrary"), after=after)
    return outs if epilogue else outs[0]


ROWS_TILE = 512
ROWS_PIECE = 128


def _mm_rows(a, w, extras, vectors, row_dtypes, fn, name):
    M, K = a.shape
    N = w.shape[1]
    tm = min(M, ROWS_TILE)
    n_e, n_v = len(extras), len(vectors)

    def body(a_ref, w_ref, *rest):
        tiles, vecs, outs, product_ref = rest[:n_e], rest[n_e:n_e + n_v], rest[n_e + n_v:-1], rest[-1]
        product_ref[...] = jnp.dot(a_ref[...], w_ref[...], preferred_element_type=F32)
        for i in range(tm // ROWS_PIECE):
            piece = slice(i * ROWS_PIECE, (i + 1) * ROWS_PIECE)
            results = fn(product_ref[piece, :], *[t[piece, :] for t in tiles], *[v[...] for v in vecs])
            for o_ref, res, dt in zip(outs, results, row_dtypes):
                o_ref[piece, :] = res.astype(dt)

    row = pl.BlockSpec((tm, N), lambda m: (m, 0))
    return _call(
        body, (a, w) + tuple(extras) + tuple(vectors), name=name, grid=(M // tm,),
        in_specs=[pl.BlockSpec((tm, K), lambda m: (m, 0)), pl.BlockSpec((K, N), lambda m: (0, 0))]
        + [row] * n_e + [pl.BlockSpec((1, N), lambda m: (0, 0))] * n_v,
        out_specs=[row] * len(row_dtypes),
        out_shape=[jax.ShapeDtypeStruct((M, N), dt) for dt in row_dtypes],
        scratch_shapes=[pltpu.VMEM((tm, N), F32)], sem=("parallel",))


def _rms(h, w):
    return h * lax.rsqrt(jnp.mean(h * h, axis=-1, keepdims=True) + EPS) * w


def _residual_rms_rows(mix, x, w):
    h = x + mix
    return h, _rms(h, w)


def _mm_tn_half(a, g, which, blocks_on, add, name, after=(), a_cols=None):
    M, Ka = a.shape
    N = g.shape[1]
    first_col = 0
    if a_cols is not None:
        first_col, Ka = a_cols
    if blocks_on == "g":
        rows, cols = _pick(Ka, (1024,)), N // N_DEV
        tn = _pick(cols, (512, 1408, 256))
        nn = cols // tn
        grid = (Ka // rows, N_CHIP, nn)
        a_spec = pl.BlockSpec((M, rows), lambda ka, s, n, w: (0, first_col // rows + ka))
        g_spec = pl.BlockSpec((M, tn), lambda ka, s, n, w: (0, (2 * s + w[0]) * nn + n))
        out_rows = Ka
    else:
        rows, cols = Ka // N_DEV, N
        tn = _pick(cols, (2048, 512))
        nn = cols // tn
        grid = (1, N_CHIP, nn)
        a_spec = pl.BlockSpec((M, rows), lambda ka, s, n, w: (0, 2 * s + w[0]))
        g_spec = pl.BlockSpec((M, tn), lambda ka, s, n, w: (0, n))
        out_rows = rows
    o_spec = pl.BlockSpec((None, rows, tn), lambda ka, s, n, w: (s, ka, n))
    n_add = 0 if add is None else 1

    def body(which_ref, a_ref, g_ref, *rest):
        acc = lax.dot_general(a_ref[...], g_ref[...], (((0,), (0,)), ((), ())), preferred_element_type=F32)
        if n_add:
            acc = acc + rest[0][...].astype(F32)
        rest[-1][...] = acc.astype(BF16)

    return pl.pallas_call(
        body, name=name,
        grid_spec=pltpu.PrefetchScalarGridSpec(
            num_scalar_prefetch=1, grid=grid,
            in_specs=[a_spec, g_spec] + [o_spec] * n_add + [ANY] * len(after),
            out_specs=o_spec),
        out_shape=jax.ShapeDtypeStruct((N_CHIP, out_rows, cols), BF16),
        compiler_params=_cparams(("parallel", "parallel", "parallel")),
    )(which, a, g, *(() if add is None else (add,)), *after)


ROW_TILE = 256


def _rms_fwd(x, w, name):
    T, Dm = x.shape

    def body(x_ref, w_ref, u_ref):
        xv = x_ref[...]
        r = lax.rsqrt(jnp.mean(xv * xv, axis=-1, keepdims=True) + EPS)
        u_ref[...] = (xv * r * w_ref[...]).astype(BF16)

    return pl.pallas_call(
        body, name=name, grid=(T // ROW_TILE,),
        in_specs=[pl.BlockSpec((ROW_TILE, Dm), lambda i: (i, 0)), pl.BlockSpec((1, Dm), lambda i: (0, 0))],
        out_specs=pl.BlockSpec((ROW_TILE, Dm), lambda i: (i, 0)),
        out_shape=jax.ShapeDtypeStruct((T, Dm), BF16),
        compiler_params=_cparams(("parallel",)),
    )(x, w)


def _loss_head(h1, mlp, wf, target, name):
    T, Dm = h1.shape

    def body(h_ref, m_ref, w_ref, t_ref, loss_ref, dh_ref, dhb_ref, dw_ref):
        i = pl.program_id(0)
        h = h_ref[...] + m_ref[...]
        r = lax.rsqrt(jnp.mean(h * h, axis=-1, keepdims=True) + EPS)
        xh = h * r
        wv = w_ref[...]
        e = xh * wv - t_ref[...]
        part = 0.5 * jnp.sum(jnp.mean(e * e, axis=-1, keepdims=True), axis=0, keepdims=True)
        dy = e * (1.0 / Dm)
        dw = jnp.sum(dy * xh, axis=0, keepdims=True)
        gy = dy * wv
        dh = r * (gy - xh * jnp.mean(gy * xh, axis=-1, keepdims=True))
        dh_ref[...] = dh
        dhb_ref[...] = dh.astype(BF16)

        @pl.when(i == 0)
        def _():
            loss_ref[...] = jnp.zeros_like(loss_ref)
            dw_ref[...] = jnp.zeros_like(dw_ref)

        loss_ref[...] += jnp.broadcast_to(part, loss_ref.shape)
        dw_ref[...] += dw

    row = pl.BlockSpec((ROW_TILE, Dm), lambda i: (i, 0))
    vec = pl.BlockSpec((1, Dm), lambda i: (0, 0))
    return pl.pallas_call(
        body, name=name, grid=(T // ROW_TILE,),
        in_specs=[row, row, vec, row],
        out_specs=[pl.BlockSpec((8, 128), lambda i: (0, 0)), row, row, vec],
        out_shape=[jax.ShapeDtypeStruct((8, 128), F32), jax.ShapeDtypeStruct((T, Dm), F32),
                   jax.ShapeDtypeStruct((T, Dm), BF16), jax.ShapeDtypeStruct((1, Dm), F32)],
        compiler_params=_cparams(("arbitrary",)),
    )(h1, mlp, wf, target)


def _rms_bwd(dyn, x, w, dres, dx_dtypes, name, after=()):
    T, Dm = x.shape
    n_dx = len(dx_dtypes)

    def body(g_ref, x_ref, w_ref, r_ref, *outs):
        i = pl.program_id(0)
        xv = x_ref[...]
        r = lax.rsqrt(jnp.mean(xv * xv, axis=-1, keepdims=True) + EPS)
        xh = xv * r
        g = g_ref[...]
        dw = jnp.sum(g * xh, axis=0, keepdims=True)
        gy = g * w_ref[...]
        dx = r_ref[...] + r * (gy - xh * jnp.mean(gy * xh, axis=-1, keepdims=True))
        for dx_ref, dt in zip(outs, dx_dtypes):
            dx_ref[...] = dx.astype(dt)
        dw_ref = outs[n_dx]

        @pl.when(i == 0)
        def _():
            dw_ref[...] = jnp.zeros_like(dw_ref)

        dw_ref[...] += dw

    row = pl.BlockSpec((ROW_TILE, Dm), lambda i: (i, 0))
    vec = pl.BlockSpec((1, Dm), lambda i: (0, 0))
    return _call(
        body, (dyn, x, w, dres), name=name, grid=(T // ROW_TILE,),
        in_specs=[row, row, vec, row],
        out_specs=[row] * n_dx + [vec],
        out_shape=[jax.ShapeDtypeStruct((T, Dm), dt) for dt in dx_dtypes] + [jax.ShapeDtypeStruct((1, Dm), F32)],
        sem=("arbitrary",), after=after)


GATE_TILE = 1024


def _merge_grads(d, za, zb, pa, pb):
    ga = _sigmoid(za)
    gb = _sigmoid(zb)
    return d * ga, d * gb, d * pa * ga * (1.0 - ga), d * pb * gb * (1.0 - gb)


def _dot_hi(a, b, dims):
    return lax.dot_general(a, b, (dims, ((), ())), precision=HIGHEST, preferred_element_type=F32)


NN = ((1,), (0,))
NT = ((1,), (1,))
TN = ((0,), (0,))


def _hg_gates(hq, hf, lb):
    sq = _sigmoid(hq)
    q = hq * sq * (HG_DK ** -0.5)
    f = _sigmoid(hf)
    g = lb + (1.0 - lb) * f
    return q, sq, f, g, jnp.log(g), 1.0 - g


def _tri(lower):
    r = lax.broadcasted_iota(jnp.int32, (CHUNK, CHUNK), 0)
    c = lax.broadcasted_iota(jnp.int32, (CHUNK, CHUNK), 1)
    return jnp.where((r >= c) if lower else (r <= c), 1.0, 0.0).astype(BF16)


def _running_sum(tri, x):
    return sum(jnp.dot(tri, piece, preferred_element_type=F32) for piece in _split3(x))


GROUP = 16
N_GROUPS = CHUNK // GROUP
BWD_CHUNKS_PER_TRIP = 4


def _dot_bf16(a, b, dims):
    return lax.dot_general(a.astype(BF16), b.astype(BF16), (dims, ((), ())), preferred_element_type=F32)


def _rows_iota():
    return lax.broadcasted_iota(jnp.int32, (CHUNK, HG_DK), 0)


def _by_query_group(q, kk, b, g):
    r0 = GROUP * g
    b0 = b[r0:r0 + 1]
    decay = jnp.exp(b[r0:r0 + GROUP] - b0)
    ks = jnp.where(_rows_iota() < r0, kk * jnp.exp(jnp.minimum(b0 - b, 0.0)), 0.0)
    return q[r0:r0 + GROUP] * decay, ks, decay


def _by_key_group(q, kk, b, j):
    r1 = GROUP * (j + 1)
    b1 = b[r1 - 1:r1]
    decay = jnp.exp(b1 - b[r1 - GROUP:r1])
    qs = jnp.where(_rows_iota() >= r1, q * jnp.exp(jnp.minimum(b - b1, 0.0)), 0.0)
    return qs, kk[r1 - GROUP:r1] * decay, decay


def _scores_between_groups(q, kk, b):
    blocks = [jnp.zeros((GROUP, CHUNK), F32)]
    for g in range(1, N_GROUPS):
        qs, ks, _ = _by_query_group(q, kk, b, g)
        blocks.append(_dot_bf16(qs, ks, NT))
    return jnp.concatenate(blocks, axis=0)


def _hgrn2_fwd(z, lb_logits, hg_norm_w, name, after=()):
    T = z.shape[0]
    n_chunks = T // CHUNK

    def body(hq_ref, hf_ref, hi_ref, hg_ref, lbl_ref, nw_ref, o_ref, ya_ref, sall_ref, st_ref):
        lbl = lbl_ref[...]
        lb = 1.0 / (1.0 + jnp.exp(lbl[1:2, :] - lbl[0:1, :]))
        st_ref[...] = jnp.zeros_like(st_ref)
        tri = _tri(True)
        row8 = lax.broadcasted_iota(jnp.int32, (8, HG_DK), 0)

        def chunk(c, carry):
            rows = pl.ds(pl.multiple_of(c * CHUNK, CHUNK), CHUNK)
            q, _, _, _, lg, kk = _hg_gates(hq_ref[rows, :], hf_ref[rows, :], lb)
            v = hi_ref[rows, :]
            b = _running_sum(tri, lg)
            st = st_ref[...]
            sall_ref[c] = st
            for grp in range(N_GROUPS):
                r0 = GROUP * grp
                for h8 in range(GROUP // 8):
                    n = 8 * (h8 + 1)
                    bs, ks, vs = b[r0:r0 + n], kk[r0:r0 + n], v[r0:r0 + n]
                    sidx = lax.broadcasted_iota(jnp.int32, (n, HG_DK), 0)
                    blk = jnp.zeros((8, HG_DK), F32)
                    for i in range(8):
                        t = r0 + 8 * h8 + i
                        e = jnp.where(sidx <= 8 * h8 + i, jnp.exp(b[t:t + 1] - bs), 0.0)
                        p = jnp.sum(e * ks * q[t:t + 1], axis=1, keepdims=True)
                        ot = jnp.sum(p * vs, axis=0, keepdims=True)
                        blk = blk + jnp.where(row8 == i, ot, 0.0)
                    o_ref[pl.ds(pl.multiple_of(c * CHUNK + r0 + 8 * h8, 8), 8), :] = blk
            o_ref[rows, :] += _dot_hi(q * jnp.exp(b), st, NT) + _dot_bf16(_scores_between_groups(q, kk, b), v, NN)
            bl = b[CHUNK - 1:CHUNK]
            ke = kk * jnp.exp(bl - b)
            st_ref[...] = st * jnp.exp(bl) + _dot_hi(v, ke, TN)
            return carry

        lax.fori_loop(0, n_chunks, chunk, 0, unroll=2)
        o = o_ref[...]
        r = lax.rsqrt(jnp.mean(o * o, axis=-1, keepdims=True) + EPS)
        hg = hg_ref[...]
        ya_ref[...] = (o * r * nw_ref[...] * (hg * _sigmoid(hg))).astype(BF16)

    def col(base):
        return pl.BlockSpec((T, HG_DK), lambda h: (0, base + h))

    return _call(
        body, (z, z, z, z, lb_logits, hg_norm_w), name=name, grid=(HG_HEADS,),
        in_specs=[col(COL_HQ), col(COL_HF), col(COL_HI), col(COL_HG),
                  pl.BlockSpec((2, HG_DK), lambda h: (0, h)), pl.BlockSpec((1, HG_DK), lambda h: (0, 0))],
        out_specs=[col(0), col(0), pl.BlockSpec((None, n_chunks, HG_DK, HG_DK), lambda h: (h, 0, 0, 0))],
        out_shape=[jax.ShapeDtypeStruct((T, HG_WIDTH), F32), jax.ShapeDtypeStruct((T, HG_WIDTH), BF16),
                   jax.ShapeDtypeStruct((HG_HEADS, n_chunks, HG_DK, HG_DK), F32)],
        scratch_shapes=[pltpu.VMEM((HG_DK, HG_DK), F32)],
        sem=("parallel",), after=after)


def _hgrn2_bwd(z, lb_logits, hg_norm_w, o_raw, s_all, dya, name, after=()):
    T = z.shape[0]
    n_chunks = T // CHUNK

    def body(hq_ref, hf_ref, hi_ref, hg_ref, lbl_ref, nw_ref, o_ref, sall_ref, dya_ref,
             dhq_ref, dhf_ref, dhi_ref, dhg_ref, dlbl_ref, dnw_ref,
             do_ref, dst_ref, dlb_ref, *per_chunk):
        h = pl.program_id(0)
        lbl = lbl_ref[...]
        lb = 1.0 / (1.0 + jnp.exp(lbl[1:2, :] - lbl[0:1, :]))

        o = o_ref[...]
        r = lax.rsqrt(jnp.mean(o * o, axis=-1, keepdims=True) + EPS)
        oh = o * r
        nw = nw_ref[...]
        hg = hg_ref[...]
        sg = _sigmoid(hg)
        dy = dya_ref[...]
        d_on = dy * (hg * sg)
        dhg_ref[...] = (dy * (oh * nw) * (sg * (1.0 + hg * (1.0 - sg)))).astype(BF16)
        dnw = jnp.sum(d_on * oh, axis=0, keepdims=True)
        gy = d_on * nw
        do_ref[...] = r * (gy - oh * jnp.mean(gy * oh, axis=-1, keepdims=True))

        @pl.when(h == 0)
        def _():
            dnw_ref[...] = jnp.zeros_like(dnw_ref)

        dnw_ref[...] += jnp.broadcast_to(dnw, dnw_ref.shape)

        dst_ref[...] = jnp.zeros_like(dst_ref)
        dlb_ref[...] = jnp.zeros_like(dlb_ref)
        tri = _tri(True)
        tri_t = _tri(False)
        row8 = lax.broadcasted_iota(jnp.int32, (8, HG_DK), 0)
        row_group = lax.broadcasted_iota(jnp.int32, (CHUNK, CHUNK), 0) // GROUP
        col_group = lax.broadcasted_iota(jnp.int32, (CHUNK, CHUNK), 1) // GROUP
        earlier_group = col_group < row_group
        later_group = col_group > row_group

        def chunk(c, dq_ref, dk_ref, dv_ref):
            rows = pl.ds(pl.multiple_of(c * CHUNK, CHUNK), CHUNK)
            hq = hq_ref[rows, :]
            q, sq, f, g, lg, kk = _hg_gates(hq, hf_ref[rows, :], lb)
            v = hi_ref[rows, :]
            do = do_ref[rows, :]
            b = _running_sum(tri, lg)
            eb = jnp.exp(b)
            bl = b[CHUNK - 1:CHUNK]
            ebl = jnp.exp(bl)
            ekb = jnp.exp(bl - b)
            qe = q * eb
            ke = kk * ekb
            st = sall_ref[c]
            dst = dst_ref[...]
            dqe = _dot_bf16(do, st, NN)
            dke = _dot_bf16(v, dst, NN)
            dv_inter = _dot_bf16(ke, dst, NT)
            d_ebl = jnp.sum(st * dst, axis=0, keepdims=True)
            dst_ref[...] = dst * ebl + _dot_bf16(do, qe, TN)

            dk_ref[...] = jnp.zeros_like(dk_ref)
            dv_ref[...] = jnp.zeros_like(dv_ref)
            for grp in range(N_GROUPS):
                r0 = GROUP * grp
                for h8 in range(GROUP // 8):
                    n = 8 * (h8 + 1)
                    bs, ks, vs = b[r0:r0 + n], kk[r0:r0 + n], v[r0:r0 + n]
                    sidx = lax.broadcasted_iota(jnp.int32, (n, HG_DK), 0)
                    blk = jnp.zeros((8, HG_DK), F32)
                    for i in range(8):
                        t = r0 + 8 * h8 + i
                        qt = q[t:t + 1]
                        dot_ = do[t:t + 1]
                        e = jnp.where(sidx <= 8 * h8 + i, jnp.exp(b[t:t + 1] - bs), 0.0)
                        w = e * ks
                        p = jnp.sum(w * qt, axis=1, keepdims=True)
                        dsc = jnp.sum(vs * dot_, axis=1, keepdims=True)
                        dqt = jnp.sum(dsc * w, axis=0, keepdims=True)
                        blk = blk + jnp.where(row8 == i, dqt, 0.0)
                        dk_ref[r0:r0 + n, :] += dsc * e * qt
                        dv_ref[r0:r0 + n, :] += p * dot_
                    dq_ref[r0 + 8 * h8:r0 + n, :] = blk
            ds_far = jnp.where(earlier_group, _dot_bf16(do, v, NT), 0.0)
            ds_far_t = jnp.where(later_group, _dot_bf16(v, do, NT), 0.0)
            dq_far, dk_far = [jnp.zeros((GROUP, HG_DK), F32)], []
            for grp in range(1, N_GROUPS):
                r0 = GROUP * grp
                _, ks, decay = _by_query_group(q, kk, b, grp)
                dq_far.append(decay * _dot_hi(ds_far[r0:r0 + GROUP], ks, NN))
                qs, _, decay = _by_key_group(q, kk, b, grp - 1)
                dk_far.append(decay * _dot_hi(ds_far_t[r0 - GROUP:r0], qs, NN))
            dk_far.append(jnp.zeros((GROUP, HG_DK), F32))
            dv_far = _dot_bf16(_scores_between_groups(q, kk, b), do, TN)
            dq_i = dq_ref[...] + jnp.concatenate(dq_far, axis=0)
            dk_i = dk_ref[...] + jnp.concatenate(dk_far, axis=0)
            dke_ke = dke * ke
            db = q * dq_i - kk * dk_i + dqe * qe - dke_ke
            db_last = jnp.sum(dke_ke, axis=0, keepdims=True) + d_ebl * ebl
            dlg = _running_sum(tri_t, db) + db_last
            dq = dq_i + dqe * eb
            dkk = dk_i + dke * ekb
            dg = dlg / g - dkk
            dhq_ref[rows, :] = (dq * (HG_DK ** -0.5) * (sq * (1.0 + hq * (1.0 - sq)))).astype(BF16)
            dhf_ref[rows, :] = (dg * (1.0 - lb) * f * (1.0 - f)).astype(BF16)
            dhi_ref[rows, :] = (dv_ref[...] + dv_far + dv_inter).astype(BF16)
            dlb_ref[...] += jnp.sum(dg * (1.0 - f), axis=0, keepdims=True)

        def trip(i, carry):
            for k in range(BWD_CHUNKS_PER_TRIP):
                chunk(n_chunks - 1 - k - BWD_CHUNKS_PER_TRIP * i, *per_chunk[3 * k:3 * k + 3])
            return carry

        lax.fori_loop(0, n_chunks // BWD_CHUNKS_PER_TRIP, trip, 0)
        dl0 = dlb_ref[...] * lb * (1.0 - lb)
        dlbl_ref[0:1, :] = dl0
        dlbl_ref[1:2, :] = -dl0

    def col(base):
        return pl.BlockSpec((T, HG_DK), lambda h: (0, base + h))

    outb = jax.ShapeDtypeStruct((T, HG_WIDTH), BF16)
    return _call(
        body, (z, z, z, z, lb_logits, hg_norm_w, o_raw, s_all, dya), name=name, grid=(HG_HEADS,),
        in_specs=[col(COL_HQ), col(COL_HF), col(COL_HI), col(COL_HG),
                  pl.BlockSpec((2, HG_DK), lambda h: (0, h)), pl.BlockSpec((1, HG_DK), lambda h: (0, 0)),
                  col(0), pl.BlockSpec((None, n_chunks, HG_DK, HG_DK), lambda h: (h, 0, 0, 0)), col(0)],
        out_specs=[col(0), col(0), col(0), col(0), pl.BlockSpec((2, HG_DK), lambda h: (0, h)),
                   pl.BlockSpec((8, HG_DK), lambda h: (0, 0))],
        out_shape=[outb, outb, outb, outb, jax.ShapeDtypeStruct((2, HG_WIDTH), F32),
                   jax.ShapeDtypeStruct((8, HG_DK), F32)],
        scratch_shapes=[pltpu.VMEM((T, HG_DK), F32), pltpu.VMEM((HG_DK, HG_DK), F32), pltpu.VMEM((1, HG_DK), F32)]
        + [pltpu.VMEM((CHUNK, HG_DK), F32)] * (3 * BWD_CHUNKS_PER_TRIP),
        sem=("arbitrary",), after=after)


CONST_KEYS = PAD - REL_CLIP
VAR_KEYS = BAND - CONST_KEYS
REL_LO = 128
REL_SPAN = N_REL_PAD - REL_LO


def _rel_onehot(t):
    r = lax.broadcasted_iota(jnp.int32, (REL_SPAN, VAR_KEYS), 0)
    j = lax.broadcasted_iota(jnp.int32, (REL_SPAN, VAR_KEYS), 1)
    idx = jnp.clip(t + PAD - CONST_KEYS - j, -REL_CLIP, REL_CLIP) + REL_CLIP - REL_LO
    return jnp.where(r == idx, 1.0, 0.0).astype(BF16)


def _split3(x):
    hi = x.astype(BF16)
    r1 = x - hi.astype(F32)
    mid = r1.astype(BF16)
    return hi, mid, (r1 - mid.astype(F32)).astype(BF16)


def _bias_expand(rel, name):
    def body(rel_ref, out_ref):
        tab = rel_ref[...]
        onehot = _rel_onehot(pl.program_id(0))
        out_ref[:, 0:CONST_KEYS] = jnp.broadcast_to(tab[:, 2 * REL_CLIP:2 * REL_CLIP + 1], (AT_HEADS, CONST_KEYS))
        out_ref[:, CONST_KEYS:BAND] = sum(
            jnp.dot(piece, onehot, preferred_element_type=F32) for piece in _split3(tab[:, REL_LO:N_REL_PAD]))

    return pl.pallas_call(
        body, name=name, grid=(CHUNK,),
        in_specs=[pl.BlockSpec((AT_HEADS, N_REL_PAD), lambda t: (0, 0))],
        out_specs=pl.BlockSpec((None, AT_HEADS, BAND), lambda t: (t, 0, 0)),
        out_shape=jax.ShapeDtypeStruct((CHUNK, AT_HEADS, BAND), F32),
        compiler_params=_cparams(("parallel",)),
    )(rel)


def _bias_reduce(dbias_rows, name, after=()):
    def body(db_ref, out_ref):
        lane = lax.broadcasted_iota(jnp.int32, (AT_HEADS, N_REL_PAD), 1)
        varying = lane >= CONST_KEYS
        by_offset = jnp.zeros((AT_HEADS, N_REL_PAD), F32)
        constant = jnp.zeros((AT_HEADS, N_REL_PAD), F32)
        for t in range(CHUNK):
            row = db_ref[t]
            constant = constant + jnp.where(varying, 0.0, row)
            moved = jnp.where(varying, row, 0.0)
            by_offset = by_offset + (pltpu.roll(moved, N_REL_PAD - t, axis=1) if t else moved)
        offset = lax.broadcasted_iota(jnp.int32, (N_REL_PAD, N_REL_PAD), 0)
        entry = lax.broadcasted_iota(jnp.int32, (N_REL_PAD, N_REL_PAD), 1)
        onehot = jnp.where(entry == jnp.clip(PAD - offset, -REL_CLIP, REL_CLIP) + REL_CLIP, 1.0, 0.0).astype(BF16)
        acc = sum(jnp.dot(piece, onehot, preferred_element_type=F32) for piece in _split3(by_offset))
        last = jnp.sum(constant, axis=1, keepdims=True)
        out_ref[...] = acc + jnp.where(lane == 2 * REL_CLIP, last, 0.0)

    whole = pl.BlockSpec((CHUNK, AT_HEADS, N_REL_PAD), lambda i: (0, 0, 0))
    return _call(
        body, (dbias_rows,), name=name, grid=(1,), in_specs=[whole],
        out_specs=[pl.BlockSpec((AT_HEADS, N_REL_PAD), lambda i: (0, 0))],
        out_shape=[jax.ShapeDtypeStruct((AT_HEADS, N_REL_PAD), F32)],
        sem=("arbitrary",), after=after)[0]


def _pair_lanes():
    return lax.broadcasted_iota(jnp.int32, (CHUNK, 2 * AT_DH), 1) < AT_DH


def _block_diag(a):
    first = _pair_lanes()
    return jnp.concatenate([jnp.where(first, a, 0.0), jnp.where(first, 0.0, a)], axis=0).astype(BF16)


def _diag_blocks(a):
    return jnp.where(_pair_lanes(), a[:CHUNK], a[CHUNK:])


def _band_probs_t(kb, qbd, bias_t, c):
    s = lax.dot_general(kb, qbd, (NT, ((), ())), preferred_element_type=F32) * (AT_DH ** -0.5) + bias_t
    j = lax.broadcasted_iota(jnp.int32, (BAND, 2 * AT_DH), 0)
    s = jnp.where(j + c * CHUNK >= PAD, s, -jnp.inf)
    p = jnp.exp(s - jnp.max(s, axis=0, keepdims=True))
    return p / jnp.sum(p, axis=0, keepdims=True)


def _attn_fwd(z, bias_t, name, after=()):
    T = z.shape[0]
    n_chunks = T // CHUNK

    def body(q_ref, k_ref, v_ref, bias_ref, y_ref, p_ref, *scratch):
        for pr in range(2):
            lanes = slice(128 * pr, 128 * (pr + 1))
            for dst_ref, src_ref in zip(scratch[2 * pr:2 * pr + 2], (k_ref, v_ref)):
                dst_ref[0:PAD, :] = jnp.zeros((PAD, 128), BF16)
                dst_ref[PAD:PAD + T, :] = src_ref[:, lanes].astype(BF16)

        def chunk(c, carry):
            rows = pl.ds(pl.multiple_of(c * CHUNK, CHUNK), CHUNK)
            band = pl.ds(pl.multiple_of(c * CHUNK, CHUNK), BAND)
            for pr in range(2):
                kp_ref, vp_ref = scratch[2 * pr:2 * pr + 2]
                lanes = slice(128 * pr, 128 * (pr + 1))
                p = _band_probs_t(kp_ref[band, :], _block_diag(q_ref[rows, lanes]), bias_ref[pr], c).astype(BF16)
                p_ref[pr, c] = p
                o2 = lax.dot_general(p, vp_ref[band, :], (TN, ((), ())), preferred_element_type=F32)
                y_ref[rows, lanes] = _diag_blocks(o2).astype(BF16)
            return carry

        lax.fori_loop(0, n_chunks, chunk, 0, unroll=2)

    def col(base):
        return pl.BlockSpec((T, 256), lambda h: (0, base // 2 + h))

    return _call(
        body, (z, z, z, bias_t), name=name, grid=(AT_HEADS // 4,),
        in_specs=[col(COL_AQ), col(COL_AK), col(COL_AV), pl.BlockSpec((2, BAND, 128), lambda h: (h, 0, 0))],
        out_specs=[col(0), pl.BlockSpec((2, n_chunks, BAND, 128), lambda h: (h, 0, 0, 0))],
        out_shape=[jax.ShapeDtypeStruct((T, AT_WIDTH), BF16),
                   jax.ShapeDtypeStruct((AT_HEADS // 2, n_chunks, BAND, 128), BF16)],
        scratch_shapes=[pltpu.VMEM((PAD + T, 128), BF16)] * 4,
        sem=("parallel",), after=after)


def _attn_bwd(z, probs, dyb, name, after=()):
    T = z.shape[0]
    n_chunks = T // CHUNK

    def body(q_ref, k_ref, v_ref, p_ref, dy_ref, dq_ref, dk_ref, dv_ref, dbias_ref, *scratch):
        dbias_ref[...] = jnp.zeros_like(dbias_ref)
        for pr in range(2):
            kp_ref, vp_ref, dkp_ref, dvp_ref = scratch[4 * pr:4 * pr + 4]
            lanes = slice(128 * pr, 128 * (pr + 1))
            kp_ref[0:PAD, :] = jnp.zeros((PAD, 128), BF16)
            vp_ref[0:PAD, :] = jnp.zeros((PAD, 128), BF16)
            kp_ref[PAD:PAD + T, :] = k_ref[:, lanes].astype(BF16)
            vp_ref[PAD:PAD + T, :] = v_ref[:, lanes].astype(BF16)
            dkp_ref[...] = jnp.zeros_like(dkp_ref)
            dvp_ref[...] = jnp.zeros_like(dvp_ref)

        def chunk(c, carry):
            rows = pl.ds(pl.multiple_of(c * CHUNK, CHUNK), CHUNK)
            band = pl.ds(pl.multiple_of(c * CHUNK, CHUNK), BAND)
            for pr in range(2):
                kp_ref, vp_ref, dkp_ref, dvp_ref = scratch[4 * pr:4 * pr + 4]
                lanes = slice(128 * pr, 128 * (pr + 1))
                qbd = _block_diag(q_ref[rows, lanes])
                dobd = _block_diag(dy_ref[rows, lanes])
                pb = p_ref[pr, c]
                p = pb.astype(F32)
                dp = lax.dot_general(vp_ref[band, :], dobd, (NT, ((), ())), preferred_element_type=F32)
                ds = p * (dp - jnp.sum(dp * p, axis=0, keepdims=True))
                dbias_ref[pr] += ds
                dsb = ds.astype(BF16)
                dq2 = lax.dot_general(dsb, kp_ref[band, :], (TN, ((), ())), preferred_element_type=F32)
                dq_ref[rows, lanes] = (_diag_blocks(dq2) * (AT_DH ** -0.5)).astype(BF16)
                dkp_ref[band, :] += jnp.dot(dsb, qbd, preferred_element_type=F32) * (AT_DH ** -0.5)
                dvp_ref[band, :] += jnp.dot(pb, dobd, preferred_element_type=F32)
            return carry

        lax.fori_loop(0, n_chunks, chunk, 0)
        for pr in range(2):
            lanes = slice(128 * pr, 128 * (pr + 1))
            dk_ref[:, lanes] = scratch[4 * pr + 2][PAD:PAD + T, :].astype(BF16)
            dv_ref[:, lanes] = scratch[4 * pr + 3][PAD:PAD + T, :].astype(BF16)

    def col(base):
        return pl.BlockSpec((T, 256), lambda h: (0, base // 2 + h))

    outb = jax.ShapeDtypeStruct((T, AT_WIDTH), BF16)
    return _call(
        body, (z, z, z, probs, dyb), name=name, grid=(AT_HEADS // 4,),
        in_specs=[col(COL_AQ), col(COL_AK), col(COL_AV),
                  pl.BlockSpec((2, n_chunks, BAND, 128), lambda h: (h, 0, 0, 0)), col(0)],
        out_specs=[col(0), col(0), col(0), pl.BlockSpec((2, BAND, 128), lambda h: (h, 0, 0))],
        out_shape=[outb, outb, outb, jax.ShapeDtypeStruct((AT_HEADS // 2, BAND, 128), F32)],
        scratch_shapes=([pltpu.VMEM((PAD + T, 128), BF16)] * 2 + [pltpu.VMEM((PAD + T, 128), F32)] * 2) * 2,
        sem=("parallel",), after=after)


def _local_step(x, target, lb_logits, hg_norm_w, rel_bias, norm_mix_w, norm_mlp_w, norm_final_w,
                w_in, rest, exchanges=None):
    ex = exchanges
    rel = jnp.pad(rel_bias, ((0, 0), (0, N_REL_PAD - N_REL)))

    u = _rms_fwd(x, norm_mix_w, "rms_mix_fwd")
    if ex:
        z, w_in, rest, placed = _mm_gathered(u, w_in, ex.order, "mm_in_fwd", rest)
        gather = _Gather(rest, placed, [w_in], "ag")
        z = _mm_gathered_tail(u, w_in, z, ex.order, "mm_in_fwd_tail", after=[gather.token])
        tok = []
    else:
        z = _mm_nn(u, w_in, F32, "mm_in_fwd")
        w_a, w_b, w_out, w_up, w_down = rest
        tok = []
    o_raw, y_a, s_all = _hgrn2_fwd(z, lb_logits, hg_norm_w, "hgrn2_fwd", after=tok)
    if ex:
        tok = [gather.pass_on([0, 1, 2], [o_raw], "abo")]
    bias_rows = _bias_expand(rel, "bias_expand")
    bias_t = jnp.transpose(bias_rows.reshape(CHUNK, AT_HEADS // 2, 2, BAND), (1, 3, 2, 0)).reshape(
        AT_HEADS // 2, BAND, 2 * CHUNK)
    y_b, probs = _attn_fwd(z, bias_t, "attn_fwd", after=tok)
    if ex:
        tok = [gather.pass_on([3], [y_b], "up")]
        w_a, w_b, w_out = gather.finish([0, 1, 2], tok, "abo")
    pa = _mm_nn(y_a, w_a, F32, "mm_a_fwd")
    pb, merged = _mm_nn(y_b, w_b, None, "mm_b_fwd", epilogue=(
        (z, z, pa), (COL_GATE_A * GATE_TILE, COL_GATE_B * GATE_TILE, 0), (F32, BF16), _gated_merge))
    w_out1 = w_out.reshape(1, D_MODEL, D_MODEL)
    h1, u2 = _mm_rows(merged, w_out.reshape(D_MODEL, D_MODEL), [x], [norm_mlp_w], (F32, BF16),
                      _residual_rms_rows, "mm_out_fwd")
    if ex:
        tok = [gather.pass_on([4], [u2], "down")]
        w_up, = gather.finish([3], tok, "up")
    a, r = _mm_nn(u2, w_up, None, "mm_up_fwd", epilogue=((), (), (F32, BF16), _squared_relu))
    if ex:
        w_down, = gather.finish([4], [r], "down")
    w_down1 = w_down.reshape(1, D_FF, D_MODEL)
    mlp = _mm_nn(r, w_down1, F32, "mm_down_fwd")
    loss, dh2, dh2b, g_nf = _loss_head(h1, mlp, norm_final_w, target, "loss_head")

    own = ex.parity if ex else jnp.zeros((1,), jnp.int32)

    def sibling_half(weights, name, after=()):
        others = [_mm_tn_half(a_, g_, 1 - own, on, None, nm + "_sibling", after, *cols)
                  for a_, g_, on, nm, *cols in weights]
        rs = _ReduceScatter(others, name) if ex else None
        return rs, others, ([rs.token] if ex else [])

    def own_half(rs, weights, others, after):
        landed = rs.from_sibling(after) if ex else [None] * len(weights)
        sums = [_mm_tn_half(a_, g_, own, on, l, nm + "_own", (), *cols)
                for (a_, g_, on, nm, *cols), l in zip(weights, landed)]
        if ex:
            return [rs.scatter(sums)], None
        return [], [jnp.stack([s_, o_], axis=1).reshape((N_DEV,) + s_.shape[1:]) for s_, o_ in zip(sums, others)]

    down = [(r, dh2b, "a", "mm_down_wgrad")]
    rs_down, others, tok = sibling_half(down, "rs_down")
    da, = _mm_nt(dh2b, w_down1, None, "mm_down_dgrad", after=tok, epilogue=(
        (a,), (0,), (BF16,), lambda dr, av: (dr * (2.0 * jnp.maximum(av, 0.0)),)))
    tok, g_down = own_half(rs_down, down, others, [da])
    up = [(u2, da, "g", "mm_up_wgrad")]
    rs_up, others, tok = sibling_half(up, "rs_up", tok)
    du2 = _mm_nt(da, w_up, F32, "mm_up_dgrad", after=tok)
    tok, g_up = own_half(rs_up, up, others, [du2])
    dh1, dh1b, g_nmlp = _rms_bwd(du2, h1, norm_mlp_w, dh2, (F32, BF16), "rms_mlp_bwd", after=tok)

    dpa, dpb, dga, dgb = _mm_nt(dh1b, w_out1, None, "mm_out_dgrad", epilogue=(
        (z, z, pa, pb), (COL_GATE_A * GATE_TILE, COL_GATE_B * GATE_TILE, 0, 0), (BF16,) * 4, _merge_grads))
    mix = [(y_a, dpa, "g", "mm_a_wgrad"), (y_b, dpb, "g", "mm_b_wgrad"), (merged, dh1b, "a", "mm_out_wgrad")]
    rs_mix, others, tok = sibling_half(mix, "rs_mix")
    dya = _mm_nt(dpa, w_a, F32, "mm_a_dgrad", after=tok)
    dyb = _mm_nt(dpb, w_b, F32, "mm_b_dgrad", after=tok)
    tok, g_mix = own_half(rs_mix, mix, others, [dya, dyb])
    daq, dak, dav, dbias_t = _attn_bwd(z, probs, dyb, "attn_bwd", after=tok)
    dhq, dhf, dhi, dhg, g_lbl, g_hgw = _hgrn2_bwd(z, lb_logits, hg_norm_w, o_raw, s_all, dya, "hgrn2_bwd",
                                                  after=tok)
    dbias_rows = jnp.pad(jnp.transpose(dbias_t.reshape(AT_HEADS // 2, BAND, 2, CHUNK), (3, 0, 2, 1)).reshape(
        CHUNK, AT_HEADS, BAND), ((0, 0), (0, 0), (0, N_REL_PAD - BAND)))
    dz = jnp.concatenate([dhq, dhf, dhi, dhg, daq, dak, dav, dga, dgb], axis=1)
    half = D_MODEL // 2
    lo = [(u, dz, "g", "mm_in_wgrad_lo", (0, half))]
    hi = [(u, dz, "g", "mm_in_wgrad_hi", (half, half))]
    rs_in_lo, others_lo, tok = sibling_half(lo, "rs_in_lo")
    rs_in_hi, others_hi, tok = sibling_half(hi, "rs_in_hi", tok)
    tok, g_in_lo = own_half(rs_in_lo, lo, others_lo, tok)
    du = _mm_nt(dz, w_in, F32, "mm_in_dgrad", after=tok)
    tok, g_in_hi = own_half(rs_in_hi, hi, others_hi, [du])
    grad_x, g_nmix = _rms_bwd(du, x, norm_mix_w, dh1, (F32,), "rms_mix_bwd", after=tok)
    g_rel = _bias_reduce(dbias_rows, "bias_reduce", after=tok)[:, :N_REL]

    small = dict(lb_logits=g_lbl, hg_norm_w=g_hgw[0:1], rel_bias=g_rel, norm_mix_w=g_nmix, norm_mlp_w=g_nmlp,
                 norm_final_w=g_nf)
    if ex:
        grads = [(rs_in_lo, rs_in_hi), rs_mix, rs_up, rs_down]
    else:
        grads = [jnp.concatenate([g_in_lo[0], g_in_hi[0]], axis=1)] + g_mix + [g_up[0], g_down[0]]
    return loss, grad_x, grads, small


CAST_CHUNK = 1 << 18


def _mm_gathered(u, shard, order, name, to_cast):
    T, K = u.shape
    _, Nb = shard.shape
    n_cast = len(to_cast)
    widths = sorted({a.shape[1] for a in to_cast})

    def body(order_ref, u_ref, shard_ref, *refs):
        cast_in, (z_ref, full_ref), cast_out = refs[:n_cast], refs[n_cast:n_cast + 2], refs[n_cast + 2:3 * n_cast + 2]
        wbuf, load_sem, send_sems, recv_sems, local_sem, cast_sems = refs[3 * n_cast + 2:3 * n_cast + 8]
        stages = dict(zip(widths, zip(refs[3 * n_cast + 8::2], refs[3 * n_cast + 9::2])))
        s = pl.program_id(0)
        x, y, c = _position()
        me, sibling = (x, y, c), (x, y, 1 - c)
        chips = [(1 - x, y), (x, 1 - y), (1 - x, 1 - y)]

        def copy(k, block, to, src=None):
            dst = full_ref.at[4 * block[0] + 2 * block[1] + block[2]]
            return pltpu.make_async_remote_copy(
                src_ref=dst if src is None else src, dst_ref=dst,
                send_sem=send_sems.at[k], recv_sem=recv_sems.at[k], device_id=to, device_id_type=MESH)

        @pl.when(s == 0)
        def _():
            local = pltpu.make_async_copy(shard_ref, full_ref.at[4 * x + 2 * y + c], local_sem)
            local.start()
            copy(0, me, sibling, src=shard_ref).start()
            for j, chip in enumerate(chips):
                copy(1 + j, me, (*chip, c), src=shard_ref).start()
            local.wait()

        @pl.when(s == 1)
        def _():
            copy(0, sibling, me).wait_recv()

        @pl.when(s == 2)
        def _():
            for src, dst, placed in zip(cast_in, cast_out[:n_cast], cast_out[n_cast:]):
                wide, narrow = stages[src.shape[1]]
                for r0 in range(0, src.shape[0], wide.shape[0]):
                    rows = pl.ds(r0, wide.shape[0])
                    load = pltpu.make_async_copy(src.at[rows], wide, cast_sems.at[0])
                    load.start()
                    load.wait()
                    narrow[...] = wide[...].astype(BF16)
                    stores = [pltpu.make_async_copy(narrow, dst.at[rows], cast_sems.at[1]),
                              pltpu.make_async_copy(narrow, placed.at[4 * x + 2 * y + c, rows], cast_sems.at[2])]
                    for st in stores:
                        st.start()
                    for st in stores:
                        st.wait()

        for j, chip in enumerate(chips):
            direct, passed = ((2, 4), (3, 5), (6, 7))[j]

            @pl.when(s == direct)
            def _(j=j, chip=chip):
                copy(1 + j, (*chip, c), me).wait_recv()
                copy(4 + j, (*chip, c), sibling).start()

            @pl.when(s == passed)
            def _(j=j, chip=chip):
                copy(4 + j, (*chip, 1 - c), me).wait_recv()

        @pl.when(s < N_EARLY_BLOCKS)
        def _():
            load = pltpu.make_async_copy(full_ref.at[order_ref[s]], wbuf, load_sem)
            load.start()
            load.wait()
            z_ref[...] = jnp.dot(u_ref[...], wbuf[...], preferred_element_type=F32)

        @pl.when(s == N_DEV - 1)
        def _():
            for k in range(7):
                copy(k, me, sibling).wait_send()

    stage_shapes = []
    for wd in widths:
        stage_shapes += [pltpu.VMEM((CAST_CHUNK // wd, wd), F32), pltpu.VMEM((CAST_CHUNK // wd, wd), BF16)]
    z, full, *cast = pl.pallas_call(
        body, name=name,
        grid_spec=pltpu.PrefetchScalarGridSpec(
            num_scalar_prefetch=1, grid=(N_DEV,),
            in_specs=[pl.BlockSpec((T, K), lambda s, order: (0, 0)), ANY] + [ANY] * n_cast,
            out_specs=[pl.BlockSpec((T, Nb), lambda s, order: (0, order[jnp.minimum(s, N_EARLY_BLOCKS - 1)])), ANY]
            + [ANY] * (2 * n_cast),
            scratch_shapes=[pltpu.VMEM((K, Nb), BF16), pltpu.SemaphoreType.DMA,
                            pltpu.SemaphoreType.DMA((7,)), pltpu.SemaphoreType.DMA((7,)), pltpu.SemaphoreType.DMA,
                            pltpu.SemaphoreType.DMA((3,))] + stage_shapes),
        out_shape=[jax.ShapeDtypeStruct((T, N_DEV * Nb), F32), jax.ShapeDtypeStruct((N_DEV, K, Nb), BF16)]
        + [jax.ShapeDtypeStruct(a.shape, BF16) for a in to_cast]
        + [jax.ShapeDtypeStruct((N_DEV,) + a.shape, BF16) for a in to_cast],
        compiler_params=_cparams(("arbitrary",)),
    )(order, u, shard, *to_cast)
    return z, full, cast[:n_cast], cast[n_cast:]


N_EARLY_BLOCKS = 6


def _mm_gathered_tail(u, full, z, order, name, after=()):
    T, K = u.shape
    _, _, Nb = full.shape
    n_after = len(after)

    def body(order_ref, u_ref, w_ref, z_in_ref, *rest):
        rest[n_after][...] = jnp.dot(u_ref[...], w_ref[...], preferred_element_type=F32)

    return pl.pallas_call(
        body, name=name,
        grid_spec=pltpu.PrefetchScalarGridSpec(
            num_scalar_prefetch=1, grid=(N_DEV - N_EARLY_BLOCKS,),
            in_specs=[pl.BlockSpec((T, K), lambda s, order: (0, 0)),
                      pl.BlockSpec((None, K, Nb), lambda s, order: (order[N_EARLY_BLOCKS + s], 0, 0)), ANY]
            + [ANY] * n_after,
            out_specs=pl.BlockSpec((T, Nb), lambda s, order: (0, order[N_EARLY_BLOCKS + s]))),
        out_shape=jax.ShapeDtypeStruct(z.shape, z.dtype),
        input_output_aliases={3: 0},
        compiler_params=_cparams(("arbitrary",)),
    )(order, u, full, z, *after)


def _gather_order():
    x, y, c = _position()
    chips = [(1 - x, y), (x, 1 - y), (1 - x, 1 - y)]
    ids = [4 * x + 2 * y + c, 4 * x + 2 * y + (1 - c)]
    ids += [4 * cx + 2 * cy + c for cx, cy in chips[:2]] + [4 * cx + 2 * cy + (1 - c) for cx, cy in chips[:2]]
    ids += [4 * chips[2][0] + 2 * chips[2][1] + c, 4 * chips[2][0] + 2 * chips[2][1] + (1 - c)]
    return jnp.stack(ids).astype(jnp.int32)


HBM = pl.BlockSpec(memory_space=pltpu.HBM)
SEM = pl.BlockSpec(memory_space=pltpu.SEMAPHORE)
DATAFLOW = pltpu.SideEffectType.DATAFLOW_SIDE_EFFECTING


def _split_call(name, bufs, waits=(), starts=None, after=()):
    nb = len(bufs)
    n_new = starts[1] if starts else 0
    wait_sems = [s for w in waits for s in (*w[1], *w[2])]

    def body(*refs):
        b, pos = refs[:nb], nb
        for plan, ss, _, send_idx, recv_idx in waits:
            k = len(ss)
            copies = plan(b, refs[pos:pos + k], refs[pos + k:pos + 2 * k])
            pos += 2 * k
            for i in recv_idx:
                copies[i].wait_recv()
            for i in send_idx:
                copies[i].wait_send()
        outs = refs[pos + len(after):]
        if starts:
            for cp in starts[0](b, outs[nb:nb + n_new], outs[nb + n_new:nb + 2 * n_new]):
                cp.start()
        outs[-1][...] = jnp.zeros_like(outs[-1])

    res = pl.pallas_call(
        body, name=name,
        out_shape=tuple(pltpu.HBM(a.shape, a.dtype) for a in bufs) + (pltpu.SemaphoreType.DMA(()),) * (2 * n_new)
        + (jax.ShapeDtypeStruct((8, 128), F32),),
        in_specs=[HBM] * nb + [SEM] * len(wait_sems) + [ANY] * len(after),
        out_specs=(HBM,) * nb + (SEM,) * (2 * n_new) + (pl.BlockSpec(memory_space=pltpu.VMEM),),
        input_output_aliases={i: i for i in range(nb)},
        compiler_params=pltpu.CompilerParams(has_side_effects=DATAFLOW),
    )(*bufs, *wait_sems, *after)
    return list(res[:nb]), list(res[nb:nb + n_new]), list(res[nb + n_new:nb + 2 * n_new]), res[-1]


def _in_hbm(a):
    return pltpu.with_memory_space_constraint(a, pltpu.HBM)


def _remote(src, dst, send_sem, recv_sem, to):
    return pltpu.make_async_remote_copy(src_ref=src, dst_ref=dst, send_sem=send_sem, recv_sem=recv_sem,
                                        device_id=to, device_id_type=MESH)


def _other_chips():
    x, y, _ = _position()
    return [(1 - x, y), (x, 1 - y), (1 - x, 1 - y)]


def _plan_gather_first(n):
    def plan(b, ss, rs):
        x, y, c = _position()
        to = [(x, y, 1 - c)] + [(*chip, c) for chip in _other_chips()]
        return [_remote(b[w], b[n + w].at[4 * x + 2 * y + c], ss[4 * w + k], rs[4 * w + k], to[k])
                for w in range(n) for k in range(4)]
    return plan, 4 * n


def _plan_gather_pass(n):
    def plan(b, ss, rs):
        x, y, c = _position()
        copies = []
        for w in range(n):
            for j, chip in enumerate(_other_chips()):
                blk = b[n + w].at[4 * chip[0] + 2 * chip[1] + c]
                copies.append(_remote(blk, blk, ss[3 * w + j], rs[3 * w + j], (x, y, 1 - c)))
        return copies
    return plan, 3 * n


def _plan_sibling(n):
    def plan(b, ss, rs):
        x, y, c = _position()
        return [_remote(b[w].at[s], b[n + w].at[s], ss[4 * w + s], rs[4 * w + s], (x, y, 1 - c))
                for w in range(n) for s in range(N_CHIP)]
    return plan, 4 * n


def _plan_scatter(n):
    def plan(b, ss, rs):
        x, y, c = _position()
        return [_remote(b[w].at[2 * chip[0] + chip[1]], b[n + w].at[2 * x + y], ss[3 * w + j], rs[3 * w + j],
                        (*chip, c))
                for w in range(n) for j, chip in enumerate(_other_chips())]
    return plan, 3 * n


class _Gather:
    def __init__(self, shards, placed, after, name):
        self.n, self.name = len(shards), name
        bufs, self.ss, self.rs, self.token = _split_call(
            name + "_start", [_in_hbm(a) for a in list(shards) + placed], starts=_plan_gather_first(self.n),
            after=after)
        self.shards, self.fulls = bufs[:self.n], bufs[self.n:]
        self.passed = {}

    def _sub(self, ids, sems, per):
        return [sems[per * w + k] for w in ids for k in range(per)]

    def pass_on(self, ids, after, tag):
        m = len(ids)
        first = (_plan_gather_first(m)[0], self._sub(ids, self.ss, 4), self._sub(ids, self.rs, 4),
                 [], [4 * i + k for i in range(m) for k in (1, 2, 3)])
        bufs, ss, rs, token = _split_call(
            "%s_pass_%s" % (self.name, tag), [self.shards[w] for w in ids] + [self.fulls[w] for w in ids],
            waits=[first], starts=_plan_gather_pass(m), after=after)
        for i, w in enumerate(ids):
            self.shards[w], self.fulls[w] = bufs[i], bufs[m + i]
        self.passed[tuple(ids)] = (ss, rs)
        return token

    def finish(self, ids, after, tag):
        m = len(ids)
        ss2, rs2 = self.passed[tuple(ids)]
        first = (_plan_gather_first(m)[0], self._sub(ids, self.ss, 4), self._sub(ids, self.rs, 4),
                 list(range(4 * m)), [4 * i for i in range(m)])
        passed = (_plan_gather_pass(m)[0], ss2, rs2, list(range(3 * m)), list(range(3 * m)))
        bufs, _, _, _ = _split_call(
            "%s_finish_%s" % (self.name, tag), [self.shards[w] for w in ids] + [self.fulls[w] for w in ids],
            waits=[first, passed], after=after)
        return bufs[m:]


class _ReduceScatter:
    def __init__(self, others, name):
        self.n, self.name = len(others), name
        lands = [lax.empty(g.shape, g.dtype) for g in others]
        self.bufs, self.ss, self.rs, self.token = _split_call(
            name + "_sibling_start", [_in_hbm(a) for a in list(others) + lands], starts=_plan_sibling(self.n))

    def from_sibling(self, after):
        n = self.n
        bufs, _, _, _ = _split_call(
            self.name + "_sibling_wait", self.bufs,
            waits=[(_plan_sibling(n)[0], self.ss, self.rs, list(range(4 * n)), list(range(4 * n)))], after=after)
        return bufs[n:]

    def scatter(self, sums):
        lands = [lax.empty(s.shape, s.dtype) for s in sums]
        self.bufs, self.ss, self.rs, token = _split_call(
            self.name + "_scatter_start", [_in_hbm(a) for a in list(sums) + lands], starts=_plan_scatter(self.n))
        return token

    def finish(self, after):
        n = self.n
        bufs, _, _, _ = _split_call(
            self.name + "_scatter_wait", self.bufs,
            waits=[(_plan_scatter(n)[0], self.ss, self.rs, list(range(3 * n)), list(range(3 * n)))], after=after)
        return bufs[:n], bufs[n:]


class _Exchanges:
    def __init__(self, parity, order):
        self.parity, self.order = parity, order


def _gather_small(packed, name):
    R = packed.shape[0]

    def body(x_ref, out_ref, send_sems, recv_sems):
        x, y, c = _position()
        me = 4 * x + 2 * y + c
        out_ref[me] = x_ref[...]
        copies = []
        for k in range(1, N_DEV):
            to = (x ^ ((k >> 2) & 1), y ^ ((k >> 1) & 1), c ^ (k & 1))
            cp = pltpu.make_async_remote_copy(
                src_ref=x_ref, dst_ref=out_ref.at[me],
                send_sem=send_sems.at[k], recv_sem=recv_sems.at[k], device_id=to, device_id_type=MESH)
            cp.start()
            copies.append((k, to, cp))
        for k, to, cp in copies:
            cp.wait_send()
            pltpu.make_async_remote_copy(
                src_ref=x_ref, dst_ref=out_ref.at[4 * to[0] + 2 * to[1] + to[2]],
                send_sem=send_sems.at[k], recv_sem=recv_sems.at[k], device_id=to, device_id_type=MESH).wait_recv()

    return pl.pallas_call(
        body, name=name,
        in_specs=[pl.BlockSpec(memory_space=pltpu.VMEM)], out_specs=pl.BlockSpec(memory_space=pltpu.VMEM),
        out_shape=jax.ShapeDtypeStruct((N_DEV, R, 128), F32),
        scratch_shapes=[pltpu.SemaphoreType.DMA((N_DEV,)), pltpu.SemaphoreType.DMA((N_DEV,))],
    )(packed)


def _adamw_math(w, g, m, v):
    m = ADAM_B1 * m + (1.0 - ADAM_B1) * g
    v = ADAM_B2 * v + (1.0 - ADAM_B2) * (g * g)
    m_hat = m / (1.0 - ADAM_B1 ** ADAM_STEP)
    v_hat = v / (1.0 - ADAM_B2 ** ADAM_STEP)
    delta = -ADAM_LR * (m_hat / (jnp.sqrt(v_hat) + ADAM_EPS) + ADAM_WD * w)
    return delta, m, v


def _adamw_big_landed(w, m, v, parts, lands, slot, name, row0=0, into=None):
    R, C = w.shape
    rows = parts.shape[1]
    tr = _pick(rows, (256,))
    first = row0 // tr
    n_into = len(into) if into else 0

    def body(slot_ref, w_ref, m_ref, v_ref, own_ref, l1_ref, l2_ref, l3_ref, *rest):
        g = own_ref[...].astype(F32)
        for ref in (l1_ref, l2_ref, l3_ref):
            g = g + ref[...].astype(F32)
        for o_ref, res in zip(rest[n_into:], (g,) + _adamw_math(w_ref[...], g, m_ref[...], v_ref[...])):
            o_ref[...] = res

    blk = pl.BlockSpec((tr, C), lambda i, slot: (first + i, 0))

    def chip(k):
        return pl.BlockSpec((None, tr, C), lambda i, slot: ((slot[0] + k) % N_CHIP, i, 0))

    out = jax.ShapeDtypeStruct((R, C), F32)
    return pl.pallas_call(
        body, name=name,
        grid_spec=pltpu.PrefetchScalarGridSpec(
            num_scalar_prefetch=1, grid=(rows // tr,),
            in_specs=[blk, blk, blk, chip(0), chip(1), chip(2), chip(3)] + [ANY] * n_into,
            out_specs=[blk, blk, blk, blk]),
        out_shape=[out, out, out, out],
        input_output_aliases={8 + j: j for j in range(n_into)},
        compiler_params=_cparams(("parallel",)),
    )(slot, w, m, v, parts, lands, lands, lands, *(into or ()))


def _adamw_small(w, m, v, gathered, name):
    R = w.shape[0]

    def body(w_ref, m_ref, v_ref, p_ref, g_ref, d_ref, nm_ref, nv_ref):
        g = p_ref[0]
        for s in range(1, N_DEV):
            g = g + p_ref[s]
        d, nm, nv = _adamw_math(w_ref[...], g, m_ref[...], v_ref[...])
        g_ref[...] = g
        d_ref[...] = d
        nm_ref[...] = nm
        nv_ref[...] = nv

    out = jax.ShapeDtypeStruct((R, 128), F32)
    return pl.pallas_call(
        body, name=name, out_shape=[out, out, out, out],
    )(w, m, v, gathered)


SMALL_NAMES = ("lb_logits", "hg_norm_w", "rel_bias", "norm_mix_w", "norm_mlp_w", "norm_final_w")
SMALL_SHAPES = {"lb_logits": (2, HG_WIDTH), "hg_norm_w": (1, HG_DK), "rel_bias": (AT_HEADS, N_REL_PAD),
                "norm_mix_w": (1, D_MODEL), "norm_mlp_w": (1, D_MODEL), "norm_final_w": (1, D_MODEL)}


def _pack_small(parts):
    rows = []
    for nme in SMALL_NAMES:
        p = parts[nme]
        if nme == "rel_bias":
            p = jnp.pad(p, ((0, 0), (0, N_REL_PAD - N_REL)))
        rows.append(p.reshape(-1, 128))
    flat = jnp.concatenate(rows, axis=0)
    return jnp.pad(flat, ((0, SMALL_ROWS - flat.shape[0]), (0, 0)))


def _unpack_small(packed):
    out, at = {}, 0
    for nme in SMALL_NAMES:
        shp = SMALL_SHAPES[nme]
        nrow = shp[0] * shp[1] // 128
        p = packed[at:at + nrow].reshape(shp)
        at += nrow
        out[nme] = p[:, :N_REL] if nme == "rel_bias" else p
    return out


BIG_NAMES = ("w_in", "w_branch_a", "w_branch_b", "w_out", "w_up", "w_down")


def kernel(x, w_in, lb_logits, hg_norm_w, rel_bias, w_branch_a, w_branch_b, w_out, norm_mix_w, norm_mlp_w, w_up, w_down, norm_final_w, loss_target, m_w_in, m_lb_logits, m_hg_norm_w, m_rel_bias, m_w_branch_a, m_w_branch_b, m_w_out, m_norm_mix_w, m_norm_mlp_w, m_w_up, m_w_down, m_norm_final_w, v_w_in, v_lb_logits, v_hg_norm_w, v_rel_bias, v_w_branch_a, v_w_branch_b, v_w_out, v_norm_mix_w, v_norm_mlp_w, v_w_up, v_w_down, v_norm_final_w):
    big_w = [w_in[0], w_branch_a[0], w_branch_b[0], w_out[0], w_up[0], w_down[0]]
    big_m = [m_w_in[0], m_w_branch_a[0], m_w_branch_b[0], m_w_out[0], m_w_up[0], m_w_down[0]]
    big_v = [v_w_in[0], v_w_branch_a[0], v_w_branch_b[0], v_w_out[0], v_w_up[0], v_w_down[0]]

    parity = lax.axis_index("c").astype(jnp.int32).reshape(1)
    loss_part, grad_x, chip_parts, small = _local_step(
        x[0], loss_target[0], lb_logits, hg_norm_w, rel_bias[0], norm_mix_w, norm_mlp_w,
        norm_final_w.reshape(1, D_MODEL), big_w[0].astype(BF16), big_w[1:], _Exchanges(parity, _gather_order()))
    loss = lax.psum(loss_part[0, 0], ("x", "y", "c"))
    (rs_in_lo, rs_in_hi), rs_mix, rs_up, rs_down = chip_parts
    slot =(2 * lax.axis_index("x") + lax.axis_index("y")).astype(jnp.int32).reshape(1)
    big = {}

    def finish(rs, names, after):
        sums, lands = rs.finish(after)
        for nme, own, land in zip(names, sums, lands):
            i = BIG_NAMES.index(nme)
            big[nme] = _adamw_big_landed(big_w[i], big_m[i], big_v[i], own, land, slot, "adamw_" + nme)
        return [big[nme][1] for nme in names]

    done = finish(rs_down, ["w_down"], [grad_x])
    done = finish(rs_up, ["w_up"], done)
    done = finish(rs_mix, ["w_branch_a", "w_branch_b", "w_out"], done)

    sw = dict(lb_logits=lb_logits, hg_norm_w=hg_norm_w, rel_bias=rel_bias[0], norm_mix_w=norm_mix_w,
              norm_mlp_w=norm_mlp_w, norm_final_w=norm_final_w.reshape(1, D_MODEL))
    sm = dict(lb_logits=m_lb_logits, hg_norm_w=m_hg_norm_w, rel_bias=m_rel_bias[0], norm_mix_w=m_norm_mix_w,
              norm_mlp_w=m_norm_mlp_w, norm_final_w=m_norm_final_w.reshape(1, D_MODEL))
    sv = dict(lb_logits=v_lb_logits, hg_norm_w=v_hg_norm_w, rel_bias=v_rel_bias[0], norm_mix_w=v_norm_mix_w,
              norm_mlp_w=v_norm_mlp_w, norm_final_w=v_norm_final_w.reshape(1, D_MODEL))
    gathered = _gather_small(_pack_small(small), "gather_small")
    small_packed = _adamw_small(_pack_small(sw), _pack_small(sm), _pack_small(sv), gathered, "adamw_small")
    small_out = [_unpack_small(p) for p in small_packed]

    (own,), (land,) = rs_in_lo.finish(done + [small_packed[0]])
    lo = _adamw_big_landed(big_w[0], big_m[0], big_v[0], own, land, slot, "adamw_w_in_lo")
    (own,), (land,) = rs_in_hi.finish([lo[1]])
    big["w_in"] = _adamw_big_landed(big_w[0], big_m[0], big_v[0], own, land, slot, "adamw_w_in_hi",
                                    row0=D_MODEL // 2, into=lo)

    def leaf(kind, nme):
        if nme in BIG_NAMES:
            return big[nme][kind][None]
        p = small_out[kind][nme]
        if nme == "rel_bias":
            return p[None]
        if nme == "norm_final_w":
            return p.reshape(D_MODEL)
        return p

    order = ("w_in", "lb_logits", "hg_norm_w", "rel_bias", "w_branch_a", "w_branch_b", "w_out", "norm_mix_w",
             "norm_mlp_w", "w_up", "w_down", "norm_final_w")
    outs = [loss, grad_x[None]]
    for kind in range(4):
        outs += [leaf(kind, nme) for nme in order]
    return tuple(outs)
```

```python
import jax
import jax.numpy as jnp
from jax import lax
from jax.experimental import pallas as pl
from jax.experimental.pallas import tpu as pltpu

F32 = jnp.float32
BF16 = jnp.bfloat16
HIGHEST = lax.Precision.HIGHEST
MESH = pl.DeviceIdType.MESH

D_MODEL = 2048
HG_HEADS = 8
HG_DK = 128
HG_WIDTH = 1024
AT_HEADS = 16
AT_DH = 64
AT_WIDTH = 1024
CHUNK = 64
LEFT_CHUNKS = 8
BAND = (LEFT_CHUNKS + 1) * CHUNK
PAD = LEFT_CHUNKS * CHUNK
REL_CLIP = 256
N_REL = 2 * REL_CLIP + 1
N_REL_PAD = 640
D_FF = 4 * D_MODEL
EPS = 1e-6
N_DEV = 8
N_CHIP = 4

ADAM_LR = 0.001
ADAM_B1 = 0.9
ADAM_B2 = 0.999
ADAM_EPS = 1e-08
ADAM_WD = 0.01
ADAM_STEP = 10

COL_HQ, COL_HF, COL_HI, COL_HG = 0, 8, 16, 24
COL_AQ, COL_AK, COL_AV = 32, 40, 48
COL_GATE_A, COL_GATE_B = 7, 9

VMEM_LIMIT = 56 * 1024 * 1024
SMALL_ROWS = 152


def _cparams(sem=None, **kw):
    if sem is not None:
        kw["dimension_semantics"] = sem
    return pltpu.CompilerParams(vmem_limit_bytes=VMEM_LIMIT, **kw)


def _pick(n, cands):
    for c in cands:
        if n % c == 0:
            return c
    return n


def _sigmoid(x):
    return 1.0 / (1.0 + jnp.exp(-x))


ANY = pl.BlockSpec(memory_space=pl.ANY)


def _position():
    return lax.axis_index("x"), lax.axis_index("y"), lax.axis_index("c")


def _call(body, args, *, name, grid, in_specs, out_specs, out_shape, scratch_shapes=(), sem=None, after=()):
    n_in = len(args)

    def ordered(*refs):
        body(*refs[:n_in], *refs[n_in + len(after):])

    return list(pl.pallas_call(
        ordered if after else body, name=name, grid=grid, in_specs=list(in_specs) + [ANY] * len(after),
        out_specs=out_specs, out_shape=out_shape, scratch_shapes=list(scratch_shapes),
        compiler_params=_cparams(sem))(*args, *after))


MAX_CONTRACTION_TILE = 4096


def _accumulate(part, acc_ref, step, n_steps, finish):
    if n_steps == 1:
        finish(part)
        return

    @pl.when(step == 0)
    def _():
        acc_ref[...] = part

    @pl.when(step > 0)
    def _():
        acc_ref[...] += part

    @pl.when(step == n_steps - 1)
    def _():
        finish(acc_ref[...])


def _mm_nn(a, wb, out_dtype, name, after=(), epilogue=None):
    M, K = a.shape
    NB, K2, Nb = wb.shape
    assert K == K2
    tm = min(M, 1024)
    tk = min(K, MAX_CONTRACTION_TILE)
    tn = _pick(Nb, (512, 1408, 256))
    nk = K // tk
    nn = Nb // tn
    extra, first_cols, out_dtypes, fn = epilogue or ((), (), (out_dtype,), lambda total: (total,))
    n_extra, n_out = len(extra), len(out_dtypes)

    def body(a_ref, b_ref, *rest):
        def finish(total):
            results = fn(total, *[r[...] for r in rest[:n_extra]])
            for o_ref, res, dt in zip(rest[n_extra:n_extra + n_out], results, out_dtypes):
                o_ref[...] = res.astype(dt)

        part = jnp.dot(a_ref[...], b_ref[...], preferred_element_type=F32)
        _accumulate(part, rest[-1], pl.program_id(3), nk, finish)

    def tile(first):
        return pl.BlockSpec((tm, tn), lambda m, j, n, k: (m, first + j * nn + n))

    outs = _call(
        body, (a, wb) + tuple(extra), name=name, grid=(M // tm, NB, nn, nk),
        in_specs=[pl.BlockSpec((tm, tk), lambda m, j, n, k: (m, k)),
                  pl.BlockSpec((None, tk, tn), lambda m, j, n, k: (j, k, n))] + [tile(col // tn) for col in first_cols],
        out_specs=[tile(0)] * n_out,
        out_shape=[jax.ShapeDtypeStruct((M, NB * Nb), dt) for dt in out_dtypes],
        scratch_shapes=[] if nk == 1 else [pltpu.VMEM((tm, tn), F32)],
        sem=("parallel", "parallel", "parallel", "arbitrary"), after=after)
    return outs if epilogue else outs[0]


def _squared_relu(a):
    ra = jnp.maximum(a, 0.0)
    return a, ra * ra


def _gated_merge(pb, za, zb, pa):
    return pb, _sigmoid(za) * pa + _sigmoid(zb) * pb


def _mm_nt(a, wb, out_dtype, name, after=(), epilogue=None):
    M, N = a.shape
    NB, K, Nb = wb.shape
    assert N == NB * Nb
    tm = min(M, 1024)
    n_tiles_live = 1 + (len(epilogue[0]) + len(epilogue[2]) if epilogue else 0)
    tko = _pick(K, (1024,)) if n_tiles_live <= 3 else _pick(K, (512,))
    tc = _pick(Nb, (2048, 1024, 1408, 256))
    nc = Nb // tc
    jb = max([d for d in (8, 4, 2, 1) if NB % d == 0 and d * tc <= MAX_CONTRACTION_TILE]) if nc == 1 else 1
    nsteps = (NB // jb) * nc
    extra, first_cols, out_dtypes, fn = epilogue or ((), (), (out_dtype,), lambda total: (total,))
    n_extra, n_out = len(extra), len(out_dtypes)

    def body(a_ref, b_ref, *rest):
        def finish(total):
            results = fn(total, *[r[...] for r in rest[:n_extra]])
            for o_ref, res, dt in zip(rest[n_extra:n_extra + n_out], results, out_dtypes):
                o_ref[...] = res.astype(dt)

        part = sum(lax.dot_general(a_ref[:, i * tc:(i + 1) * tc], b_ref[i], (((1,), (1,)), ((), ())),
                                   preferred_element_type=F32) for i in range(jb))
        _accumulate(part, rest[-1], pl.program_id(2) * nc + pl.program_id(3), nsteps, finish)

    def tile(first):
        return pl.BlockSpec((tm, tko), lambda m, ko, j, c: (m, first + ko))

    outs = _call(
        body, (a, wb) + tuple(extra), name=name,
        grid=(M // tm, K // tko, NB // jb, nc),
        in_specs=[pl.BlockSpec((tm, jb * tc), lambda m, ko, j, c: (m, j * nc + c)),
                  pl.BlockSpec((jb, tko, tc), lambda m, ko, j, c: (j, ko, c))] + [tile(col // tko) for col in first_cols],
        out_specs=[tile(0)] * n_out,
        out_shape=[jax.ShapeDtypeStruct((M, K), dt) for dt in out_dtypes],
        scratch_shapes=[] if nsteps == 1 else [pltpu.VMEM((tm, tko), F32)],
        sem=("parallel", "parallel", "arbitrary", "arbitrary"), after=after)
    return outs if epilogue else outs[0]


ROWS_TILE = 512
ROWS_PIECE = 128


def _mm_rows(a, w, extras, vectors, row_dtypes, fn, name):
    M, K = a.shape
    N = w.shape[1]
    tm = min(M, ROWS_TILE)
    n_e, n_v = len(extras), len(vectors)

    def body(a_ref, w_ref, *rest):
        tiles, vecs, outs, product_ref = rest[:n_e], rest[n_e:n_e + n_v], rest[n_e + n_v:-1], rest[-1]
        product_ref[...] = jnp.dot(a_ref[...], w_ref[...], preferred_element_type=F32)
        for i in range(tm // ROWS_PIECE):
            piece = slice(i * ROWS_PIECE, (i + 1) * ROWS_PIECE)
            results = fn(product_ref[piece, :], *[t[piece, :] for t in tiles], *[v[...] for v in vecs])
            for o_ref, res, dt in zip(outs, results, row_dtypes):
                o_ref[piece, :] = res.astype(dt)

    row = pl.BlockSpec((tm, N), lambda m: (m, 0))
    return _call(
        body, (a, w) + tuple(extras) + tuple(vectors), name=name, grid=(M // tm,),
        in_specs=[pl.BlockSpec((tm, K), lambda m: (m, 0)), pl.BlockSpec((K, N), lambda m: (0, 0))]
        + [row] * n_e + [pl.BlockSpec((1, N), lambda m: (0, 0))] * n_v,
        out_specs=[row] * len(row_dtypes),
        out_shape=[jax.ShapeDtypeStruct((M, N), dt) for dt in row_dtypes],
        scratch_shapes=[pltpu.VMEM((tm, N), F32)], sem=("parallel",))


def _rms(h, w):
    return h * lax.rsqrt(jnp.mean(h * h, axis=-1, keepdims=True) + EPS) * w


def _residual_rms_rows(mix, x, w):
    h = x + mix
    return h, _rms(h, w)


def _mm_tn_half(a, g, which, blocks_on, add, name, after=(), a_cols=None):
    M, Ka = a.shape
    N = g.shape[1]
    first_col = 0
    if a_cols is not None:
        first_col, Ka = a_cols
    if blocks_on == "g":
        rows, cols = _pick(Ka, (1024,)), N // N_DEV
        tn = _pick(cols, (512, 1408, 256))
        nn = cols // tn
        grid = (Ka // rows, N_CHIP, nn)
        a_spec = pl.BlockSpec((M, rows), lambda ka, s, n, w: (0, first_col // rows + ka))
        g_spec = pl.BlockSpec((M, tn), lambda ka, s, n, w: (0, (2 * s + w[0]) * nn + n))
        out_rows = Ka
    else:
        rows, cols = Ka // N_DEV, N
        tn = _pick(cols, (2048, 512))
        nn = cols // tn
        grid = (1, N_CHIP, nn)
        a_spec = pl.BlockSpec((M, rows), lambda ka, s, n, w: (0, 2 * s + w[0]))
        g_spec = pl.BlockSpec((M, tn), lambda ka, s, n, w: (0, n))
        out_rows = rows
    o_spec = pl.BlockSpec((None, rows, tn), lambda ka, s, n, w: (s, ka, n))
    n_add = 0 if add is None else 1

    def body(which_ref, a_ref, g_ref, *rest):
        acc = lax.dot_general(a_ref[...], g_ref[...], (((0,), (0,)), ((), ())), preferred_element_type=F32)
        if n_add:
            acc = acc + rest[0][...].astype(F32)
        rest[-1][...] = acc.astype(BF16)

    return pl.pallas_call(
        body, name=name,
        grid_spec=pltpu.PrefetchScalarGridSpec(
            num_scalar_prefetch=1, grid=grid,
            in_specs=[a_spec, g_spec] + [o_spec] * n_add + [ANY] * len(after),
            out_specs=o_spec),
        out_shape=jax.ShapeDtypeStruct((N_CHIP, out_rows, cols), BF16),
        compiler_params=_cparams(("parallel", "parallel", "parallel")),
    )(which, a, g, *(() if add is None else (add,)), *after)


ROW_TILE = 256


def _rms_fwd(x, w, name):
    T, Dm = x.shape

    def body(x_ref, w_ref, u_ref):
        xv = x_ref[...]
        r = lax.rsqrt(jnp.mean(xv * xv, axis=-1, keepdims=True) + EPS)
        u_ref[...] = (xv * r * w_ref[...]).astype(BF16)

    return pl.pallas_call(
        body, name=name, grid=(T // ROW_TILE,),
        in_specs=[pl.BlockSpec((ROW_TILE, Dm), lambda i: (i, 0)), pl.BlockSpec((1, Dm), lambda i: (0, 0))],
        out_specs=pl.BlockSpec((ROW_TILE, Dm), lambda i: (i, 0)),
        out_shape=jax.ShapeDtypeStruct((T, Dm), BF16),
        compiler_params=_cparams(("parallel",)),
    )(x, w)


def _loss_head(h1, mlp, wf, target, name):
    T, Dm = h1.shape

    def body(h_ref, m_ref, w_ref, t_ref, loss_ref, dh_ref, dhb_ref, dw_ref):
        i = pl.program_id(0)
        h = h_ref[...] + m_ref[...]
        r = lax.rsqrt(jnp.mean(h * h, axis=-1, keepdims=True) + EPS)
        xh = h * r
        wv = w_ref[...]
        e = xh * wv - t_ref[...]
        part = 0.5 * jnp.sum(jnp.mean(e * e, axis=-1, keepdims=True), axis=0, keepdims=True)
        dy = e * (1.0 / Dm)
        dw = jnp.sum(dy * xh, axis=0, keepdims=True)
        gy = dy * wv
        dh = r * (gy - xh * jnp.mean(gy * xh, axis=-1, keepdims=True))
        dh_ref[...] = dh
        dhb_ref[...] = dh.astype(BF16)

        @pl.when(i == 0)
        def _():
            loss_ref[...] = jnp.zeros_like(loss_ref)
            dw_ref[...] = jnp.zeros_like(dw_ref)

        loss_ref[...] += jnp.broadcast_to(part, loss_ref.shape)
        dw_ref[...] += dw

    row = pl.BlockSpec((ROW_TILE, Dm), lambda i: (i, 0))
    vec = pl.BlockSpec((1, Dm), lambda i: (0, 0))
    return pl.pallas_call(
        body, name=name, grid=(T // ROW_TILE,),
        in_specs=[row, row, vec, row],
        out_specs=[pl.BlockSpec((8, 128), lambda i: (0, 0)), row, row, vec],
        out_shape=[jax.ShapeDtypeStruct((8, 128), F32), jax.ShapeDtypeStruct((T, Dm), F32),
                   jax.ShapeDtypeStruct((T, Dm), BF16), jax.ShapeDtypeStruct((1, Dm), F32)],
        compiler_params=_cparams(("arbitrary",)),
    )(h1, mlp, wf, target)


def _rms_bwd(dyn, x, w, dres, dx_dtypes, name, after=()):
    T, Dm = x.shape
    n_dx = len(dx_dtypes)

    def body(g_ref, x_ref, w_ref, r_ref, *outs):
        i = pl.program_id(0)
        xv = x_ref[...]
        r = lax.rsqrt(jnp.mean(xv * xv, axis=-1, keepdims=True) + EPS)
        xh = xv * r
        g = g_ref[...]
        dw = jnp.sum(g * xh, axis=0, keepdims=True)
        gy = g * w_ref[...]
        dx = r_ref[...] + r * (gy - xh * jnp.mean(gy * xh, axis=-1, keepdims=True))
        for dx_ref, dt in zip(outs, dx_dtypes):
            dx_ref[...] = dx.astype(dt)
        dw_ref = outs[n_dx]

        @pl.when(i == 0)
        def _():
            dw_ref[...] = jnp.zeros_like(dw_ref)

        dw_ref[...] += dw

    row = pl.BlockSpec((ROW_TILE, Dm), lambda i: (i, 0))
    vec = pl.BlockSpec((1, Dm), lambda i: (0, 0))
    return _call(
        body, (dyn, x, w, dres), name=name, grid=(T // ROW_TILE,),
        in_specs=[row, row, vec, row],
        out_specs=[row] * n_dx + [vec],
        out_shape=[jax.ShapeDtypeStruct((T, Dm), dt) for dt in dx_dtypes] + [jax.ShapeDtypeStruct((1, Dm), F32)],
        sem=("arbitrary",), after=after)


GATE_TILE = 1024


def _merge_grads(d, za, zb, pa, pb):
    ga = _sigmoid(za)
    gb = _sigmoid(zb)
    return d * ga, d * gb, d * pa * ga * (1.0 - ga), d * pb * gb * (1.0 - gb)


def _dot_hi(a, b, dims):
    return lax.dot_general(a, b, (dims, ((), ())), precision=HIGHEST, preferred_element_type=F32)


NN = ((1,), (0,))
NT = ((1,), (1,))
TN = ((0,), (0,))


def _hg_gates(hq, hf, lb):
    sq = _sigmoid(hq)
    q = hq * sq * (HG_DK ** -0.5)
    f = _sigmoid(hf)
    g = lb + (1.0 - lb) * f
    return q, sq, f, g, jnp.log(g), 1.0 - g


def _tri(lower):
    r = lax.broadcasted_iota(jnp.int32, (CHUNK, CHUNK), 0)
    c = lax.broadcasted_iota(jnp.int32, (CHUNK, CHUNK), 1)
    return jnp.where((r >= c) if lower else (r <= c), 1.0, 0.0).astype(BF16)


def _running_sum(tri, x):
    return sum(jnp.dot(tri, piece, preferred_element_type=F32) for piece in _split3(x))


GROUP = 16
N_GROUPS = CHUNK // GROUP
BWD_CHUNKS_PER_TRIP = 4


def _dot_bf16(a, b, dims):
    return lax.dot_general(a.astype(BF16), b.astype(BF16), (dims, ((), ())), preferred_element_type=F32)


def _rows_iota():
    return lax.broadcasted_iota(jnp.int32, (CHUNK, HG_DK), 0)


def _by_query_group(q, kk, b, g):
    r0 = GROUP * g
    b0 = b[r0:r0 + 1]
    decay = jnp.exp(b[r0:r0 + GROUP] - b0)
    ks = jnp.where(_rows_iota() < r0, kk * jnp.exp(jnp.minimum(b0 - b, 0.0)), 0.0)
    return q[r0:r0 + GROUP] * decay, ks, decay


def _by_key_group(q, kk, b, j):
    r1 = GROUP * (j + 1)
    b1 = b[r1 - 1:r1]
    decay = jnp.exp(b1 - b[r1 - GROUP:r1])
    qs = jnp.where(_rows_iota() >= r1, q * jnp.exp(jnp.minimum(b - b1, 0.0)), 0.0)
    return qs, kk[r1 - GROUP:r1] * decay, decay


def _scores_between_groups(q, kk, b):
    blocks = [jnp.zeros((GROUP, CHUNK), F32)]
    for g in range(1, N_GROUPS):
        qs, ks, _ = _by_query_group(q, kk, b, g)
        blocks.append(_dot_bf16(qs, ks, NT))
    return jnp.concatenate(blocks, axis=0)


def _hgrn2_fwd(z, lb_logits, hg_norm_w, name, after=()):
    T = z.shape[0]
    n_chunks = T // CHUNK

    def body(hq_ref, hf_ref, hi_ref, hg_ref, lbl_ref, nw_ref, o_ref, ya_ref, sall_ref, st_ref):
        lbl = lbl_ref[...]
        lb = 1.0 / (1.0 + jnp.exp(lbl[1:2, :] - lbl[0:1, :]))
        st_ref[...] = jnp.zeros_like(st_ref)
        tri = _tri(True)
        row8 = lax.broadcasted_iota(jnp.int32, (8, HG_DK), 0)

        def chunk(c, carry):
            rows = pl.ds(pl.multiple_of(c * CHUNK, CHUNK), CHUNK)
            q, _, _, _, lg, kk = _hg_gates(hq_ref[rows, :], hf_ref[rows, :], lb)
            v = hi_ref[rows, :]
            b = _running_sum(tri, lg)
            st = st_ref[...]
            sall_ref[c] = st
            for grp in range(N_GROUPS):
                r0 = GROUP * grp
                for h8 in range(GROUP // 8):
                    n = 8 * (h8 + 1)
                    bs, ks, vs = b[r0:r0 + n], kk[r0:r0 + n], v[r0:r0 + n]
                    sidx = lax.broadcasted_iota(jnp.int32, (n, HG_DK), 0)
                    blk = jnp.zeros((8, HG_DK), F32)
                    for i in range(8):
                        t = r0 + 8 * h8 + i
                        e = jnp.where(sidx <= 8 * h8 + i, jnp.exp(b[t:t + 1] - bs), 0.0)
                        p = jnp.sum(e * ks * q[t:t + 1], axis=1, keepdims=True)
                        ot = jnp.sum(p * vs, axis=0, keepdims=True)
                        blk = blk + jnp.where(row8 == i, ot, 0.0)
                    o_ref[pl.ds(pl.multiple_of(c * CHUNK + r0 + 8 * h8, 8), 8), :] = blk
            o_ref[rows, :] += _dot_hi(q * jnp.exp(b), st, NT) + _dot_bf16(_scores_between_groups(q, kk, b), v, NN)
            bl = b[CHUNK - 1:CHUNK]
            ke = kk * jnp.exp(bl - b)
            st_ref[...] = st * jnp.exp(bl) + _dot_hi(v, ke, TN)
            return carry

        lax.fori_loop(0, n_chunks, chunk, 0, unroll=2)
        o = o_ref[...]
        r = lax.rsqrt(jnp.mean(o * o, axis=-1, keepdims=True) + EPS)
        hg = hg_ref[...]
        ya_ref[...] = (o * r * nw_ref[...] * (hg * _sigmoid(hg))).astype(BF16)

    def col(base):
        return pl.BlockSpec((T, HG_DK), lambda h: (0, base + h))

    return _call(
        body, (z, z, z, z, lb_logits, hg_norm_w), name=name, grid=(HG_HEADS,),
        in_specs=[col(COL_HQ), col(COL_HF), col(COL_HI), col(COL_HG),
                  pl.BlockSpec((2, HG_DK), lambda h: (0, h)), pl.BlockSpec((1, HG_DK), lambda h: (0, 0))],
        out_specs=[col(0), col(0), pl.BlockSpec((None, n_chunks, HG_DK, HG_DK), lambda h: (h, 0, 0, 0))],
        out_shape=[jax.ShapeDtypeStruct((T, HG_WIDTH), F32), jax.ShapeDtypeStruct((T, HG_WIDTH), BF16),
                   jax.ShapeDtypeStruct((HG_HEADS, n_chunks, HG_DK, HG_DK), F32)],
        scratch_shapes=[pltpu.VMEM((HG_DK, HG_DK), F32)],
        sem=("parallel",), after=after)


def _hgrn2_bwd(z, lb_logits, hg_norm_w, o_raw, s_all, dya, name, after=()):
    T = z.shape[0]
    n_chunks = T // CHUNK

    def body(hq_ref, hf_ref, hi_ref, hg_ref, lbl_ref, nw_ref, o_ref, sall_ref, dya_ref,
             dhq_ref, dhf_ref, dhi_ref, dhg_ref, dlbl_ref, dnw_ref,
             do_ref, dst_ref, dlb_ref, *per_chunk):
        h = pl.program_id(0)
        lbl = lbl_ref[...]
        lb = 1.0 / (1.0 + jnp.exp(lbl[1:2, :] - lbl[0:1, :]))

        o = o_ref[...]
        r = lax.rsqrt(jnp.mean(o * o, axis=-1, keepdims=True) + EPS)
        oh = o * r
        nw = nw_ref[...]
        hg = hg_ref[...]
        sg = _sigmoid(hg)
        dy = dya_ref[...]
        d_on = dy * (hg * sg)
        dhg_ref[...] = (dy * (oh * nw) * (sg * (1.0 + hg * (1.0 - sg)))).astype(BF16)
        dnw = jnp.sum(d_on * oh, axis=0, keepdims=True)
        gy = d_on * nw
        do_ref[...] = r * (gy - oh * jnp.mean(gy * oh, axis=-1, keepdims=True))

        @pl.when(h == 0)
        def _():
            dnw_ref[...] = jnp.zeros_like(dnw_ref)

        dnw_ref[...] += jnp.broadcast_to(dnw, dnw_ref.shape)

        dst_ref[...] = jnp.zeros_like(dst_ref)
        dlb_ref[...] = jnp.zeros_like(dlb_ref)
        tri = _tri(True)
        tri_t = _tri(False)
        row8 = lax.broadcasted_iota(jnp.int32, (8, HG_DK), 0)
        row_group = lax.broadcasted_iota(jnp.int32, (CHUNK, CHUNK), 0) // GROUP
        col_group = lax.broadcasted_iota(jnp.int32, (CHUNK, CHUNK), 1) // GROUP
        earlier_group = col_group < row_group
        later_group = col_group > row_group

        def chunk(c, dq_ref, dk_ref, dv_ref):
            rows = pl.ds(pl.multiple_of(c * CHUNK, CHUNK), CHUNK)
            hq = hq_ref[rows, :]
            q, sq, f, g, lg, kk = _hg_gates(hq, hf_ref[rows, :], lb)
            v = hi_ref[rows, :]
            do = do_ref[rows, :]
            b = _running_sum(tri, lg)
            eb = jnp.exp(b)
            bl = b[CHUNK - 1:CHUNK]
            ebl = jnp.exp(bl)
            ekb = jnp.exp(bl - b)
            qe = q * eb
            ke = kk * ekb
            st = sall_ref[c]
            dst = dst_ref[...]
            dqe = _dot_bf16(do, st, NN)
            dke = _dot_bf16(v, dst, NN)
            dv_inter = _dot_bf16(ke, dst, NT)
            d_ebl = jnp.sum(st * dst, axis=0, keepdims=True)
            dst_ref[...] = dst * ebl + _dot_bf16(do, qe, TN)

            dk_ref[...] = jnp.zeros_like(dk_ref)
            dv_ref[...] = jnp.zeros_like(dv_ref)
            for grp in range(N_GROUPS):
                r0 = GROUP * grp
                for h8 in range(GROUP // 8):
                    n = 8 * (h8 + 1)
                    bs, ks, vs = b[r0:r0 + n], kk[r0:r0 + n], v[r0:r0 + n]
                    sidx = lax.broadcasted_iota(jnp.int32, (n, HG_DK), 0)
                    blk = jnp.zeros((8, HG_DK), F32)
                    for i in range(8):
                        t = r0 + 8 * h8 + i
                        qt = q[t:t + 1]
                        dot_ = do[t:t + 1]
                        e = jnp.where(sidx <= 8 * h8 + i, jnp.exp(b[t:t + 1] - bs), 0.0)
                        w = e * ks
                        p = jnp.sum(w * qt, axis=1, keepdims=True)
                        dsc = jnp.sum(vs * dot_, axis=1, keepdims=True)
                        dqt = jnp.sum(dsc * w, axis=0, keepdims=True)
                        blk = blk + jnp.where(row8 == i, dqt, 0.0)
                        dk_ref[r0:r0 + n, :] += dsc * e * qt
                        dv_ref[r0:r0 + n, :] += p * dot_
                    dq_ref[r0 + 8 * h8:r0 + n, :] = blk
            ds_far = jnp.where(earlier_group, _dot_bf16(do, v, NT), 0.0)
            ds_far_t = jnp.where(later_group, _dot_bf16(v, do, NT), 0.0)
            dq_far, dk_far = [jnp.zeros((GROUP, HG_DK), F32)], []
            for grp in range(1, N_GROUPS):
                r0 = GROUP * grp
                _, ks, decay = _by_query_group(q, kk, b, grp)
                dq_far.append(decay * _dot_hi(ds_far[r0:r0 + GROUP], ks, NN))
                qs, _, decay = _by_key_group(q, kk, b, grp - 1)
                dk_far.append(decay * _dot_hi(ds_far_t[r0 - GROUP:r0], qs, NN))
            dk_far.append(jnp.zeros((GROUP, HG_DK), F32))
            dv_far = _dot_bf16(_scores_between_groups(q, kk, b), do, TN)
            dq_i = dq_ref[...] + jnp.concatenate(dq_far, axis=0)
            dk_i = dk_ref[...] + jnp.concatenate(dk_far, axis=0)
            dke_ke = dke * ke
            db = q * dq_i - kk * dk_i + dqe * qe - dke_ke
            db_last = jnp.sum(dke_ke, axis=0, keepdims=True) + d_ebl * ebl
            dlg = _running_sum(tri_t, db) + db_last
            dq = dq_i + dqe * eb
            dkk = dk_i + dke * ekb
            dg = dlg / g - dkk
            dhq_ref[rows, :] = (dq * (HG_DK ** -0.5) * (sq * (1.0 + hq * (1.0 - sq)))).astype(BF16)
            dhf_ref[rows, :] = (dg * (1.0 - lb) * f * (1.0 - f)).astype(BF16)
            dhi_ref[rows, :] = (dv_ref[...] + dv_far + dv_inter).astype(BF16)
            dlb_ref[...] += jnp.sum(dg * (1.0 - f), axis=0, keepdims=True)

        def trip(i, carry):
            for k in range(BWD_CHUNKS_PER_TRIP):
                chunk(n_chunks - 1 - k - BWD_CHUNKS_PER_TRIP * i, *per_chunk[3 * k:3 * k + 3])
            return carry

        lax.fori_loop(0, n_chunks // BWD_CHUNKS_PER_TRIP, trip, 0)
        dl0 = dlb_ref[...] * lb * (1.0 - lb)
        dlbl_ref[0:1, :] = dl0
        dlbl_ref[1:2, :] = -dl0

    def col(base):
        return pl.BlockSpec((T, HG_DK), lambda h: (0, base + h))

    outb = jax.ShapeDtypeStruct((T, HG_WIDTH), BF16)
    return _call(
        body, (z, z, z, z, lb_logits, hg_norm_w, o_raw, s_all, dya), name=name, grid=(HG_HEADS,),
        in_specs=[col(COL_HQ), col(COL_HF), col(COL_HI), col(COL_HG),
                  pl.BlockSpec((2, HG_DK), lambda h: (0, h)), pl.BlockSpec((1, HG_DK), lambda h: (0, 0)),
                  col(0), pl.BlockSpec((None, n_chunks, HG_DK, HG_DK), lambda h: (h, 0, 0, 0)), col(0)],
        out_specs=[col(0), col(0), col(0), col(0), pl.BlockSpec((2, HG_DK), lambda h: (0, h)),
                   pl.BlockSpec((8, HG_DK), lambda h: (0, 0))],
        out_shape=[outb, outb, outb, outb, jax.ShapeDtypeStruct((2, HG_WIDTH), F32),
                   jax.ShapeDtypeStruct((8, HG_DK), F32)],
        scratch_shapes=[pltpu.VMEM((T, HG_DK), F32), pltpu.VMEM((HG_DK, HG_DK), F32), pltpu.VMEM((1, HG_DK), F32)]
        + [pltpu.VMEM((CHUNK, HG_DK), F32)] * (3 * BWD_CHUNKS_PER_TRIP),
        sem=("arbitrary",), after=after)


CONST_KEYS = PAD - REL_CLIP
VAR_KEYS = BAND - CONST_KEYS
REL_LO = 128
REL_SPAN = N_REL_PAD - REL_LO


def _rel_onehot(t):
    r = lax.broadcasted_iota(jnp.int32, (REL_SPAN, VAR_KEYS), 0)
    j = lax.broadcasted_iota(jnp.int32, (REL_SPAN, VAR_KEYS), 1)
    idx = jnp.clip(t + PAD - CONST_KEYS - j, -REL_CLIP, REL_CLIP) + REL_CLIP - REL_LO
    return jnp.where(r == idx, 1.0, 0.0).astype(BF16)


def _split3(x):
    hi = x.astype(BF16)
    r1 = x - hi.astype(F32)
    mid = r1.astype(BF16)
    return hi, mid, (r1 - mid.astype(F32)).astype(BF16)


def _bias_expand(rel, name):
    def body(rel_ref, out_ref):
        tab = rel_ref[...]
        onehot = _rel_onehot(pl.program_id(0))
        out_ref[:, 0:CONST_KEYS] = jnp.broadcast_to(tab[:, 2 * REL_CLIP:2 * REL_CLIP + 1], (AT_HEADS, CONST_KEYS))
        out_ref[:, CONST_KEYS:BAND] = sum(
            jnp.dot(piece, onehot, preferred_element_type=F32) for piece in _split3(tab[:, REL_LO:N_REL_PAD]))

    return pl.pallas_call(
        body, name=name, grid=(CHUNK,),
        in_specs=[pl.BlockSpec((AT_HEADS, N_REL_PAD), lambda t: (0, 0))],
        out_specs=pl.BlockSpec((None, AT_HEADS, BAND), lambda t: (t, 0, 0)),
        out_shape=jax.ShapeDtypeStruct((CHUNK, AT_HEADS, BAND), F32),
        compiler_params=_cparams(("parallel",)),
    )(rel)


def _bias_reduce(dbias_rows, name, after=()):
    def body(db_ref, out_ref):
        lane = lax.broadcasted_iota(jnp.int32, (AT_HEADS, N_REL_PAD), 1)
        varying = lane >= CONST_KEYS
        by_offset = jnp.zeros((AT_HEADS, N_REL_PAD), F32)
        constant = jnp.zeros((AT_HEADS, N_REL_PAD), F32)
        for t in range(CHUNK):
            row = db_ref[t]
            constant = constant + jnp.where(varying, 0.0, row)
            moved = jnp.where(varying, row, 0.0)
            by_offset = by_offset + (pltpu.roll(moved, N_REL_PAD - t, axis=1) if t else moved)
        offset = lax.broadcasted_iota(jnp.int32, (N_REL_PAD, N_REL_PAD), 0)
        entry = lax.broadcasted_iota(jnp.int32, (N_REL_PAD, N_REL_PAD), 1)
        onehot = jnp.where(entry == jnp.clip(PAD - offset, -REL_CLIP, REL_CLIP) + REL_CLIP, 1.0, 0.0).astype(BF16)
        acc = sum(jnp.dot(piece, onehot, preferred_element_type=F32) for piece in _split3(by_offset))
        last = jnp.sum(constant, axis=1, keepdims=True)
        out_ref[...] = acc + jnp.where(lane == 2 * REL_CLIP, last, 0.0)

    whole = pl.BlockSpec((CHUNK, AT_HEADS, N_REL_PAD), lambda i: (0, 0, 0))
    return _call(
        body, (dbias_rows,), name=name, grid=(1,), in_specs=[whole],
        out_specs=[pl.BlockSpec((AT_HEADS, N_REL_PAD), lambda i: (0, 0))],
        out_shape=[jax.ShapeDtypeStruct((AT_HEADS, N_REL_PAD), F32)],
        sem=("arbitrary",), after=after)[0]


def _pair_lanes():
    return lax.broadcasted_iota(jnp.int32, (CHUNK, 2 * AT_DH), 1) < AT_DH


def _block_diag(a):
    first = _pair_lanes()
    return jnp.concatenate([jnp.where(first, a, 0.0), jnp.where(first, 0.0, a)], axis=0).astype(BF16)


def _diag_blocks(a):
    return jnp.where(_pair_lanes(), a[:CHUNK], a[CHUNK:])


def _band_probs_t(kb, qbd, bias_t, c):
    s = lax.dot_general(kb, qbd, (NT, ((), ())), preferred_element_type=F32) * (AT_DH ** -0.5) + bias_t
    j = lax.broadcasted_iota(jnp.int32, (BAND, 2 * AT_DH), 0)
    s = jnp.where(j + c * CHUNK >= PAD, s, -jnp.inf)
    p = jnp.exp(s - jnp.max(s, axis=0, keepdims=True))
    return p / jnp.sum(p, axis=0, keepdims=True)


def _attn_fwd(z, bias_t, name, after=()):
    T = z.shape[0]
    n_chunks = T // CHUNK

    def body(q_ref, k_ref, v_ref, bias_ref, y_ref, p_ref, *scratch):
        for pr in range(2):
            lanes = slice(128 * pr, 128 * (pr + 1))
            for dst_ref, src_ref in zip(scratch[2 * pr:2 * pr + 2], (k_ref, v_ref)):
                dst_ref[0:PAD, :] = jnp.zeros((PAD, 128), BF16)
                dst_ref[PAD:PAD + T, :] = src_ref[:, lanes].astype(BF16)

        def chunk(c, carry):
            rows = pl.ds(pl.multiple_of(c * CHUNK, CHUNK), CHUNK)
            band = pl.ds(pl.multiple_of(c * CHUNK, CHUNK), BAND)
            for pr in range(2):
                kp_ref, vp_ref = scratch[2 * pr:2 * pr + 2]
                lanes = slice(128 * pr, 128 * (pr + 1))
                p = _band_probs_t(kp_ref[band, :], _block_diag(q_ref[rows, lanes]), bias_ref[pr], c).astype(BF16)
                p_ref[pr, c] = p
                o2 = lax.dot_general(p, vp_ref[band, :], (TN, ((), ())), preferred_element_type=F32)
                y_ref[rows, lanes] = _diag_blocks(o2).astype(BF16)
            return carry

        lax.fori_loop(0, n_chunks, chunk, 0, unroll=2)

    def col(base):
        return pl.BlockSpec((T, 256), lambda h: (0, base // 2 + h))

    return _call(
        body, (z, z, z, bias_t), name=name, grid=(AT_HEADS // 4,),
        in_specs=[col(COL_AQ), col(COL_AK), col(COL_AV), pl.BlockSpec((2, BAND, 128), lambda h: (h, 0, 0))],
        out_specs=[col(0), pl.BlockSpec((2, n_chunks, BAND, 128), lambda h: (h, 0, 0, 0))],
        out_shape=[jax.ShapeDtypeStruct((T, AT_WIDTH), BF16),
                   jax.ShapeDtypeStruct((AT_HEADS // 2, n_chunks, BAND, 128), BF16)],
        scratch_shapes=[pltpu.VMEM((PAD + T, 128), BF16)] * 4,
        sem=("parallel",), after=after)


def _attn_bwd(z, probs, dyb, name, after=()):
    T = z.shape[0]
    n_chunks = T // CHUNK

    def body(q_ref, k_ref, v_ref, p_ref, dy_ref, dq_ref, dk_ref, dv_ref, dbias_ref, *scratch):
        dbias_ref[...] = jnp.zeros_like(dbias_ref)
        for pr in range(2):
            kp_ref, vp_ref, dkp_ref, dvp_ref = scratch[4 * pr:4 * pr + 4]
            lanes = slice(128 * pr, 128 * (pr + 1))
            kp_ref[0:PAD, :] = jnp.zeros((PAD, 128), BF16)
            vp_ref[0:PAD, :] = jnp.zeros((PAD, 128), BF16)
            kp_ref[PAD:PAD + T, :] = k_ref[:, lanes].astype(BF16)
            vp_ref[PAD:PAD + T, :] = v_ref[:, lanes].astype(BF16)
            dkp_ref[...] = jnp.zeros_like(dkp_ref)
            dvp_ref[...] = jnp.zeros_like(dvp_ref)

        def chunk(c, carry):
            rows = pl.ds(pl.multiple_of(c * CHUNK, CHUNK), CHUNK)
            band = pl.ds(pl.multiple_of(c * CHUNK, CHUNK), BAND)
            for pr in range(2):
                kp_ref, vp_ref, dkp_ref, dvp_ref = scratch[4 * pr:4 * pr + 4]
                lanes = slice(128 * pr, 128 * (pr + 1))
                qbd = _block_diag(q_ref[rows, lanes])
                dobd = _block_diag(dy_ref[rows, lanes])
                pb = p_ref[pr, c]
                p = pb.astype(F32)
                dp = lax.dot_general(vp_ref[band, :], dobd, (NT, ((), ())), preferred_element_type=F32)
                ds = p * (dp - jnp.sum(dp * p, axis=0, keepdims=True))
                dbias_ref[pr] += ds
                dsb = ds.astype(BF16)
                dq2 = lax.dot_general(dsb, kp_ref[band, :], (TN, ((), ())), preferred_element_type=F32)
                dq_ref[rows, lanes] = (_diag_blocks(dq2) * (AT_DH ** -0.5)).astype(BF16)
                dkp_ref[band, :] += jnp.dot(dsb, qbd, preferred_element_type=F32) * (AT_DH ** -0.5)
                dvp_ref[band, :] += jnp.dot(pb, dobd, preferred_element_type=F32)
            return carry

        lax.fori_loop(0, n_chunks, chunk, 0)
        for pr in range(2):
            lanes = slice(128 * pr, 128 * (pr + 1))
            dk_ref[:, lanes] = scratch[4 * pr + 2][PAD:PAD + T, :].astype(BF16)
            dv_ref[:, lanes] = scratch[4 * pr + 3][PAD:PAD + T, :].astype(BF16)

    def col(base):
        return pl.BlockSpec((T, 256), lambda h: (0, base // 2 + h))

    outb = jax.ShapeDtypeStruct((T, AT_WIDTH), BF16)
    return _call(
        body, (z, z, z, probs, dyb), name=name, grid=(AT_HEADS // 4,),
        in_specs=[col(COL_AQ), col(COL_AK), col(COL_AV),
                  pl.BlockSpec((2, n_chunks, BAND, 128), lambda h: (h, 0, 0, 0)), col(0)],
        out_specs=[col(0), col(0), col(0), pl.BlockSpec((2, BAND, 128), lambda h: (h, 0, 0))],
        out_shape=[outb, outb, outb, jax.ShapeDtypeStruct((AT_HEADS // 2, BAND, 128), F32)],
        scratch_shapes=([pltpu.VMEM((PAD + T, 128), BF16)] * 2 + [pltpu.VMEM((PAD + T, 128), F32)] * 2) * 2,
        sem=("parallel",), after=after)


def _local_step(x, target, lb_logits, hg_norm_w, rel_bias, norm_mix_w, norm_mlp_w, norm_final_w,
                w_in, rest, exchanges=None):
    ex = exchanges
    rel = jnp.pad(rel_bias, ((0, 0), (0, N_REL_PAD - N_REL)))

    u = _rms_fwd(x, norm_mix_w, "rms_mix_fwd")
    if ex:
        z, w_in, rest, placed = _mm_gathered(u, w_in, ex.order, "mm_in_fwd", rest)
        gather = _Gather(rest, placed, [w_in], "ag")
        z = _mm_gathered_tail(u, w_in, z, ex.order, "mm_in_fwd_tail", after=[gather.token])
        tok = []
    else:
        z = _mm_nn(u, w_in, F32, "mm_in_fwd")
        w_a, w_b, w_out, w_up, w_down = rest
        tok = []
    o_raw, y_a, s_all = _hgrn2_fwd(z, lb_logits, hg_norm_w, "hgrn2_fwd", after=tok)
    if ex:
        tok = [gather.pass_on([0, 1, 2], [o_raw], "abo")]
    bias_rows = _bias_expand(rel, "bias_expand")
    bias_t = jnp.transpose(bias_rows.reshape(CHUNK, AT_HEADS // 2, 2, BAND), (1, 3, 2, 0)).reshape(
        AT_HEADS // 2, BAND, 2 * CHUNK)
    y_b, probs = _attn_fwd(z, bias_t, "attn_fwd", after=tok)
    if ex:
        tok = [gather.pass_on([3], [y_b], "up")]
        w_a, w_b, w_out = gather.finish([0, 1, 2], tok, "abo")
    pa = _mm_nn(y_a, w_a, F32, "mm_a_fwd")
    pb, merged = _mm_nn(y_b, w_b, None, "mm_b_fwd", epilogue=(
        (z, z, pa), (COL_GATE_A * GATE_TILE, COL_GATE_B * GATE_TILE, 0), (F32, BF16), _gated_merge))
    w_out1 = w_out.reshape(1, D_MODEL, D_MODEL)
    h1, u2 = _mm_rows(merged, w_out.reshape(D_MODEL, D_MODEL), [x], [norm_mlp_w], (F32, BF16),
                      _residual_rms_rows, "mm_out_fwd")
    if ex:
        tok = [gather.pass_on([4], [u2], "down")]
        w_up, = gather.finish([3], tok, "up")
    a, r = _mm_nn(u2, w_up, None, "mm_up_fwd", epilogue=((), (), (F32, BF16), _squared_relu))
    if ex:
        w_down, = gather.finish([4], [r], "down")
    w_down1 = w_down.reshape(1, D_FF, D_MODEL)
    mlp = _mm_nn(r, w_down1, F32, "mm_down_fwd")
    loss, dh2, dh2b, g_nf = _loss_head(h1, mlp, norm_final_w, target, "loss_head")

    own = ex.parity if ex else jnp.zeros((1,), jnp.int32)

    def sibling_half(weights, name, after=()):
        others = [_mm_tn_half(a_, g_, 1 - own, on, None, nm + "_sibling", after, *cols)
                  for a_, g_, on, nm, *cols in weights]
        rs = _ReduceScatter(others, name) if ex else None
        return rs, others, ([rs.token] if ex else [])

    def own_half(rs, weights, others, after):
        landed = rs.from_sibling(after) if ex else [None] * len(weights)
        sums = [_mm_tn_half(a_, g_, own, on, l, nm + "_own", (), *cols)
                for (a_, g_, on, nm, *cols), l in zip(weights, landed)]
        if ex:
            return [rs.scatter(sums)], None
        return [], [jnp.stack([s_, o_], axis=1).reshape((N_DEV,) + s_.shape[1:]) for s_, o_ in zip(sums, others)]

    down = [(r, dh2b, "a", "mm_down_wgrad")]
    rs_down, others, tok = sibling_half(down, "rs_down")
    da, = _mm_nt(dh2b, w_down1, None, "mm_down_dgrad", after=tok, epilogue=(
        (a,), (0,), (BF16,), lambda dr, av: (dr * (2.0 * jnp.maximum(av, 0.0)),)))
    tok, g_down = own_half(rs_down, down, others, [da])
    up = [(u2, da, "g", "mm_up_wgrad")]
    rs_up, others, tok = sibling_half(up, "rs_up", tok)
    du2 = _mm_nt(da, w_up, F32, "mm_up_dgrad", after=tok)
    tok, g_up = own_half(rs_up, up, others, [du2])
    dh1, dh1b, g_nmlp = _rms_bwd(du2, h1, norm_mlp_w, dh2, (F32, BF16), "rms_mlp_bwd", after=tok)

    dpa, dpb, dga, dgb = _mm_nt(dh1b, w_out1, None, "mm_out_dgrad", epilogue=(
        (z, z, pa, pb), (COL_GATE_A * GATE_TILE, COL_GATE_B * GATE_TILE, 0, 0), (BF16,) * 4, _merge_grads))
    mix = [(y_a, dpa, "g", "mm_a_wgrad"), (y_b, dpb, "g", "mm_b_wgrad"), (merged, dh1b, "a", "mm_out_wgrad")]
    rs_mix, others, tok = sibling_half(mix, "rs_mix")
    dya = _mm_nt(dpa, w_a, F32, "mm_a_dgrad", after=tok)
    dyb = _mm_nt(dpb, w_b, F32, "mm_b_dgrad", after=tok)
    tok, g_mix = own_half(rs_mix, mix, others, [dya, dyb])
    daq, dak, dav, dbias_t = _attn_bwd(z, probs, dyb, "attn_bwd", after=tok)
    dhq, dhf, dhi, dhg, g_lbl, g_hgw = _hgrn2_bwd(z, lb_logits, hg_norm_w, o_raw, s_all, dya, "hgrn2_bwd",
                                                  after=tok)
    dbias_rows = jnp.pad(jnp.transpose(dbias_t.reshape(AT_HEADS // 2, BAND, 2, CHUNK), (3, 0, 2, 1)).reshape(
        CHUNK, AT_HEADS, BAND), ((0, 0), (0, 0), (0, N_REL_PAD - BAND)))
    dz = jnp.concatenate([dhq, dhf, dhi, dhg, daq, dak, dav, dga, dgb], axis=1)
    half = D_MODEL // 2
    lo = [(u, dz, "g", "mm_in_wgrad_lo", (0, half))]
    hi = [(u, dz, "g", "mm_in_wgrad_hi", (half, half))]
    rs_in_lo, others_lo, tok = sibling_half(lo, "rs_in_lo")
    rs_in_hi, others_hi, tok = sibling_half(hi, "rs_in_hi", tok)
    tok, g_in_lo = own_half(rs_in_lo, lo, others_lo, tok)
    du = _mm_nt(dz, w_in, F32, "mm_in_dgrad", after=tok)
    tok, g_in_hi = own_half(rs_in_hi, hi, others_hi, [du])
    grad_x, g_nmix = _rms_bwd(du, x, norm_mix_w, dh1, (F32,), "rms_mix_bwd", after=tok)
    g_rel = _bias_reduce(dbias_rows, "bias_reduce", after=tok)[:, :N_REL]

    small = dict(lb_logits=g_lbl, hg_norm_w=g_hgw[0:1], rel_bias=g_rel, norm_mix_w=g_nmix, norm_mlp_w=g_nmlp,
                 norm_final_w=g_nf)
    if ex:
        grads = [(rs_in_lo, rs_in_hi), rs_mix, rs_up, rs_down]
    else:
        grads = [jnp.concatenate([g_in_lo[0], g_in_hi[0]], axis=1)] + g_mix + [g_up[0], g_down[0]]
    return loss, grad_x, grads, small


CAST_CHUNK = 1 << 18


def _mm_gathered(u, shard, order, name, to_cast):
    T, K = u.shape
    _, Nb = shard.shape
    n_cast = len(to_cast)
    widths = sorted({a.shape[1] for a in to_cast})

    def body(order_ref, u_ref, shard_ref, *refs):
        cast_in, (z_ref, full_ref), cast_out = refs[:n_cast], refs[n_cast:n_cast + 2], refs[n_cast + 2:3 * n_cast + 2]
        wbuf, load_sem, send_sems, recv_sems, local_sem, cast_sems = refs[3 * n_cast + 2:3 * n_cast + 8]
        stages = dict(zip(widths, zip(refs[3 * n_cast + 8::2], refs[3 * n_cast + 9::2])))
        s = pl.program_id(0)
        x, y, c = _position()
        me, sibling = (x, y, c), (x, y, 1 - c)
        chips = [(1 - x, y), (x, 1 - y), (1 - x, 1 - y)]

        def copy(k, block, to, src=None):
            dst = full_ref.at[4 * block[0] + 2 * block[1] + block[2]]
            return pltpu.make_async_remote_copy(
                src_ref=dst if src is None else src, dst_ref=dst,
                send_sem=send_sems.at[k], recv_sem=recv_sems.at[k], device_id=to, device_id_type=MESH)

        @pl.when(s == 0)
        def _():
            local = pltpu.make_async_copy(shard_ref, full_ref.at[4 * x + 2 * y + c], local_sem)
            local.start()
            copy(0, me, sibling, src=shard_ref).start()
            for j, chip in enumerate(chips):
                copy(1 + j, me, (*chip, c), src=shard_ref).start()
            local.wait()

        @pl.when(s == 1)
        def _():
            copy(0, sibling, me).wait_recv()

        @pl.when(s == N_EARLY_BLOCKS)
        def _():
            for src, dst, placed in zip(cast_in, cast_out[:n_cast], cast_out[n_cast:]):
                wide, narrow = stages[src.shape[1]]
                for r0 in range(0, src.shape[0], wide.shape[0]):
                    rows = pl.ds(r0, wide.shape[0])
                    load = pltpu.make_async_copy(src.at[rows], wide, cast_sems.at[0])
                    load.start()
                    load.wait()
                    narrow[...] = wide[...].astype(BF16)
                    stores = [pltpu.make_async_copy(narrow, dst.at[rows], cast_sems.at[1]),
                              pltpu.make_async_copy(narrow, placed.at[4 * x + 2 * y + c, rows], cast_sems.at[2])]
                    for st in stores:
                        st.start()
                    for st in stores:
                        st.wait()

        for j, chip in enumerate(chips):
            direct, passed = ((2, 4), (3, 5), (6, 7))[j]

            @pl.when(s == direct)
            def _(j=j, chip=chip):
                copy(1 + j, (*chip, c), me).wait_recv()
                copy(4 + j, (*chip, c), sibling).start()

            @pl.when(s == passed)
            def _(j=j, chip=chip):
                copy(4 + j, (*chip, 1 - c), me).wait_recv()

        @pl.when(s < N_EARLY_BLOCKS)
        def _():
            load = pltpu.make_async_copy(full_ref.at[order_ref[s]], wbuf, load_sem)
            load.start()
            load.wait()
            z_ref[...] = jnp.dot(u_ref[...], wbuf[...], preferred_element_type=F32)

        @pl.when(s == N_DEV - 1)
        def _():
            for k in range(7):
                copy(k, me, sibling).wait_send()

    stage_shapes = []
    for wd in widths:
        stage_shapes += [pltpu.VMEM((CAST_CHUNK // wd, wd), F32), pltpu.VMEM((CAST_CHUNK // wd, wd), BF16)]
    z, full, *cast = pl.pallas_call(
        body, name=name,
        grid_spec=pltpu.PrefetchScalarGridSpec(
            num_scalar_prefetch=1, grid=(N_DEV,),
            in_specs=[pl.BlockSpec((T, K), lambda s, order: (0, 0)), ANY] + [ANY] * n_cast,
            out_specs=[pl.BlockSpec((T, Nb), lambda s, order: (0, order[jnp.minimum(s, N_EARLY_BLOCKS - 1)])), ANY]
            + [ANY] * (2 * n_cast),
            scratch_shapes=[pltpu.VMEM((K, Nb), BF16), pltpu.SemaphoreType.DMA,
                            pltpu.SemaphoreType.DMA((7,)), pltpu.SemaphoreType.DMA((7,)), pltpu.SemaphoreType.DMA,
                            pltpu.SemaphoreType.DMA((3,))] + stage_shapes),
        out_shape=[jax.ShapeDtypeStruct((T, N_DEV * Nb), F32), jax.ShapeDtypeStruct((N_DEV, K, Nb), BF16)]
        + [jax.ShapeDtypeStruct(a.shape, BF16) for a in to_cast]
        + [jax.ShapeDtypeStruct((N_DEV,) + a.shape, BF16) for a in to_cast],
        compiler_params=_cparams(("arbitrary",)),
    )(order, u, shard, *to_cast)
    return z, full, cast[:n_cast], cast[n_cast:]


N_EARLY_BLOCKS = 6


def _mm_gathered_tail(u, full, z, order, name, after=()):
    T, K = u.shape
    _, _, Nb = full.shape
    n_after = len(after)

    def body(order_ref, u_ref, w_ref, z_in_ref, *rest):
        rest[n_after][...] = jnp.dot(u_ref[...], w_ref[...], preferred_element_type=F32)

    return pl.pallas_call(
        body, name=name,
        grid_spec=pltpu.PrefetchScalarGridSpec(
            num_scalar_prefetch=1, grid=(N_DEV - N_EARLY_BLOCKS,),
            in_specs=[pl.BlockSpec((T, K), lambda s, order: (0, 0)),
                      pl.BlockSpec((None, K, Nb), lambda s, order: (order[N_EARLY_BLOCKS + s], 0, 0)), ANY]
            + [ANY] * n_after,
            out_specs=pl.BlockSpec((T, Nb), lambda s, order: (0, order[N_EARLY_BLOCKS + s]))),
        out_shape=jax.ShapeDtypeStruct(z.shape, z.dtype),
        input_output_aliases={3: 0},
        compiler_params=_cparams(("arbitrary",)),
    )(order, u, full, z, *after)


def _gather_order():
    x, y, c = _position()
    chips = [(1 - x, y), (x, 1 - y), (1 - x, 1 - y)]
    ids = [4 * x + 2 * y + c, 4 * x + 2 * y + (1 - c)]
    ids += [4 * cx + 2 * cy + c for cx, cy in chips[:2]] + [4 * cx + 2 * cy + (1 - c) for cx, cy in chips[:2]]
    ids += [4 * chips[2][0] + 2 * chips[2][1] + c, 4 * chips[2][0] + 2 * chips[2][1] + (1 - c)]
    return jnp.stack(ids).astype(jnp.int32)


HBM = pl.BlockSpec(memory_space=pltpu.HBM)
SEM = pl.BlockSpec(memory_space=pltpu.SEMAPHORE)
DATAFLOW = pltpu.SideEffectType.DATAFLOW_SIDE_EFFECTING


def _split_call(name, bufs, waits=(), starts=None, after=()):
    nb = len(bufs)
    n_new = starts[1] if starts else 0
    wait_sems = [s for w in waits for s in (*w[1], *w[2])]

    def body(*refs):
        b, pos = refs[:nb], nb
        for plan, ss, _, send_idx, recv_idx in waits:
            k = len(ss)
            copies = plan(b, refs[pos:pos + k], refs[pos + k:pos + 2 * k])
            pos += 2 * k
            for i in recv_idx:
                copies[i].wait_recv()
            for i in send_idx:
                copies[i].wait_send()
        outs = refs[pos + len(after):]
        if starts:
            for cp in starts[0](b, outs[nb:nb + n_new], outs[nb + n_new:nb + 2 * n_new]):
                cp.start()
        outs[-1][...] = jnp.zeros_like(outs[-1])

    res = pl.pallas_call(
        body, name=name,
        out_shape=tuple(pltpu.HBM(a.shape, a.dtype) for a in bufs) + (pltpu.SemaphoreType.DMA(()),) * (2 * n_new)
        + (jax.ShapeDtypeStruct((8, 128), F32),),
        in_specs=[HBM] * nb + [SEM] * len(wait_sems) + [ANY] * len(after),
        out_specs=(HBM,) * nb + (SEM,) * (2 * n_new) + (pl.BlockSpec(memory_space=pltpu.VMEM),),
        input_output_aliases={i: i for i in range(nb)},
        compiler_params=pltpu.CompilerParams(has_side_effects=DATAFLOW),
    )(*bufs, *wait_sems, *after)
    return list(res[:nb]), list(res[nb:nb + n_new]), list(res[nb + n_new:nb + 2 * n_new]), res[-1]


def _in_hbm(a):
    return pltpu.with_memory_space_constraint(a, pltpu.HBM)


def _remote(src, dst, send_sem, recv_sem, to):
    return pltpu.make_async_remote_copy(src_ref=src, dst_ref=dst, send_sem=send_sem, recv_sem=recv_sem,
                                        device_id=to, device_id_type=MESH)


def _other_chips():
    x, y, _ = _position()
    return [(1 - x, y), (x, 1 - y), (1 - x, 1 - y)]


def _plan_gather_first(n):
    def plan(b, ss, rs):
        x, y, c = _position()
        to = [(x, y, 1 - c)] + [(*chip, c) for chip in _other_chips()]
        return [_remote(b[w], b[n + w].at[4 * x + 2 * y + c], ss[4 * w + k], rs[4 * w + k], to[k])
                for w in range(n) for k in range(4)]
    return plan, 4 * n


def _plan_gather_pass(n):
    def plan(b, ss, rs):
        x, y, c = _position()
        copies = []
        for w in range(n):
            for j, chip in enumerate(_other_chips()):
                blk = b[n + w].at[4 * chip[0] + 2 * chip[1] + c]
                copies.append(_remote(blk, blk, ss[3 * w + j], rs[3 * w + j], (x, y, 1 - c)))
        return copies
    return plan, 3 * n


def _plan_sibling(n):
    def plan(b, ss, rs):
        x, y, c = _position()
        return [_remote(b[w].at[s], b[n + w].at[s], ss[4 * w + s], rs[4 * w + s], (x, y, 1 - c))
                for w in range(n) for s in range(N_CHIP)]
    return plan, 4 * n


def _plan_scatter(n):
    def plan(b, ss, rs):
        x, y, c = _position()
        return [_remote(b[w].at[2 * chip[0] + chip[1]], b[n + w].at[2 * x + y], ss[3 * w + j], rs[3 * w + j],
                        (*chip, c))
                for w in range(n) for j, chip in enumerate(_other_chips())]
    return plan, 3 * n


class _Gather:
    def __init__(self, shards, placed, after, name):
        self.n, self.name = len(shards), name
        bufs, self.ss, self.rs, self.token = _split_call(
            name + "_start", [_in_hbm(a) for a in list(shards) + placed], starts=_plan_gather_first(self.n),
            after=after)
        self.shards, self.fulls = bufs[:self.n], bufs[self.n:]
        self.passed = {}

    def _sub(self, ids, sems, per):
        return [sems[per * w + k] for w in ids for k in range(per)]

    def pass_on(self, ids, after, tag):
        m = len(ids)
        first = (_plan_gather_first(m)[0], self._sub(ids, self.ss, 4), self._sub(ids, self.rs, 4),
                 [], [4 * i + k for i in range(m) for k in (1, 2, 3)])
        bufs, ss, rs, token = _split_call(
            "%s_pass_%s" % (self.name, tag), [self.shards[w] for w in ids] + [self.fulls[w] for w in ids],
            waits=[first], starts=_plan_gather_pass(m), after=after)
        for i, w in enumerate(ids):
            self.shards[w], self.fulls[w] = bufs[i], bufs[m + i]
        self.passed[tuple(ids)] = (ss, rs)
        return token

    def finish(self, ids, after, tag):
        m = len(ids)
        ss2, rs2 = self.passed[tuple(ids)]
        first = (_plan_gather_first(m)[0], self._sub(ids, self.ss, 4), self._sub(ids, self.rs, 4),
                 list(range(4 * m)), [4 * i for i in range(m)])
        passed = (_plan_gather_pass(m)[0], ss2, rs2, list(range(3 * m)), list(range(3 * m)))
        bufs, _, _, _ = _split_call(
            "%s_finish_%s" % (self.name, tag), [self.shards[w] for w in ids] + [self.fulls[w] for w in ids],
            waits=[first, passed], after=after)
        return bufs[m:]


class _ReduceScatter:
    def __init__(self, others, name):
        self.n, self.name = len(others), name
        lands = [lax.empty(g.shape, g.dtype) for g in others]
        self.bufs, self.ss, self.rs, self.token = _split_call(
            name + "_sibling_start", [_in_hbm(a) for a in list(others) + lands], starts=_plan_sibling(self.n))

    def from_sibling(self, after):
        n = self.n
        bufs, _, _, _ = _split_call(
            self.name + "_sibling_wait", self.bufs,
            waits=[(_plan_sibling(n)[0], self.ss, self.rs, list(range(4 * n)), list(range(4 * n)))], after=after)
        return bufs[n:]

    def scatter(self, sums):
        lands = [lax.empty(s.shape, s.dtype) for s in sums]
        self.bufs, self.ss, self.rs, token = _split_call(
            self.name + "_scatter_start", [_in_hbm(a) for a in list(sums) + lands], starts=_plan_scatter(self.n))
        return token

    def finish(self, after):
        n = self.n
        bufs, _, _, _ = _split_call(
            self.name + "_scatter_wait", self.bufs,
            waits=[(_plan_scatter(n)[0], self.ss, self.rs, list(range(3 * n)), list(range(3 * n)))], after=after)
        return bufs[:n], bufs[n:]


class _Exchanges:
    def __init__(self, parity, order):
        self.parity, self.order = parity, order


def _gather_small(packed, name):
    R = packed.shape[0]

    def body(x_ref, out_ref, send_sems, recv_sems):
        x, y, c = _position()
        me = 4 * x + 2 * y + c
        out_ref[me] = x_ref[...]
        copies = []
        for k in range(1, N_DEV):
            to = (x ^ ((k >> 2) & 1), y ^ ((k >> 1) & 1), c ^ (k & 1))
            cp = pltpu.make_async_remote_copy(
                src_ref=x_ref, dst_ref=out_ref.at[me],
                send_sem=send_sems.at[k], recv_sem=recv_sems.at[k], device_id=to, device_id_type=MESH)
            cp.start()
            copies.append((k, to, cp))
        for k, to, cp in copies:
            cp.wait_send()
            pltpu.make_async_remote_copy(
                src_ref=x_ref, dst_ref=out_ref.at[4 * to[0] + 2 * to[1] + to[2]],
                send_sem=send_sems.at[k], recv_sem=recv_sems.at[k], device_id=to, device_id_type=MESH).wait_recv()

    return pl.pallas_call(
        body, name=name,
        in_specs=[pl.BlockSpec(memory_space=pltpu.VMEM)], out_specs=pl.BlockSpec(memory_space=pltpu.VMEM),
        out_shape=jax.ShapeDtypeStruct((N_DEV, R, 128), F32),
        scratch_shapes=[pltpu.SemaphoreType.DMA((N_DEV,)), pltpu.SemaphoreType.DMA((N_DEV,))],
    )(packed)


def _adamw_math(w, g, m, v):
    m = ADAM_B1 * m + (1.0 - ADAM_B1) * g
    v = ADAM_B2 * v + (1.0 - ADAM_B2) * (g * g)
    m_hat = m / (1.0 - ADAM_B1 ** ADAM_STEP)
    v_hat = v / (1.0 - ADAM_B2 ** ADAM_STEP)
    delta = -ADAM_LR * (m_hat / (jnp.sqrt(v_hat) + ADAM_EPS) + ADAM_WD * w)
    return delta, m, v


def _adamw_big_landed(w, m, v, parts, lands, slot, name, row0=0, into=None):
    R, C = w.shape
    rows = parts.shape[1]
    tr = _pick(rows, (256,))
    first = row0 // tr
    n_into = len(into) if into else 0

    def body(slot_ref, w_ref, m_ref, v_ref, own_ref, l1_ref, l2_ref, l3_ref, *rest):
        g = own_ref[...].astype(F32)
        for ref in (l1_ref, l2_ref, l3_ref):
            g = g + ref[...].astype(F32)
        for o_ref, res in zip(rest[n_into:], (g,) + _adamw_math(w_ref[...], g, m_ref[...], v_ref[...])):
            o_ref[...] = res

    blk = pl.BlockSpec((tr, C), lambda i, slot: (first + i, 0))

    def chip(k):
        return pl.BlockSpec((None, tr, C), lambda i, slot: ((slot[0] + k) % N_CHIP, i, 0))

    out = jax.ShapeDtypeStruct((R, C), F32)
    return pl.pallas_call(
        body, name=name,
        grid_spec=pltpu.PrefetchScalarGridSpec(
            num_scalar_prefetch=1, grid=(rows // tr,),
            in_specs=[blk, blk, blk, chip(0), chip(1), chip(2), chip(3)] + [ANY] * n_into,
            out_specs=[blk, blk, blk, blk]),
        out_shape=[out, out, out, out],
        input_output_aliases={8 + j: j for j in range(n_into)},
        compiler_params=_cparams(("parallel",)),
    )(slot, w, m, v, parts, lands, lands, lands, *(into or ()))


def _adamw_small(w, m, v, gathered, name):
    R = w.shape[0]

    def body(w_ref, m_ref, v_ref, p_ref, g_ref, d_ref, nm_ref, nv_ref):
        g = p_ref[0]
        for s in range(1, N_DEV):
            g = g + p_ref[s]
        d, nm, nv = _adamw_math(w_ref[...], g, m_ref[...], v_ref[...])
        g_ref[...] = g
        d_ref[...] = d
        nm_ref[...] = nm
        nv_ref[...] = nv

    out = jax.ShapeDtypeStruct((R, 128), F32)
    return pl.pallas_call(
        body, name=name, out_shape=[out, out, out, out],
    )(w, m, v, gathered)


SMALL_NAMES = ("lb_logits", "hg_norm_w", "rel_bias", "norm_mix_w", "norm_mlp_w", "norm_final_w")
SMALL_SHAPES = {"lb_logits": (2, HG_WIDTH), "hg_norm_w": (1, HG_DK), "rel_bias": (AT_HEADS, N_REL_PAD),
                "norm_mix_w": (1, D_MODEL), "norm_mlp_w": (1, D_MODEL), "norm_final_w": (1, D_MODEL)}


def _pack_small(parts):
    rows = []
    for nme in SMALL_NAMES:
        p = parts[nme]
        if nme == "rel_bias":
            p = jnp.pad(p, ((0, 0), (0, N_REL_PAD - N_REL)))
        rows.append(p.reshape(-1, 128))
    flat = jnp.concatenate(rows, axis=0)
    return jnp.pad(flat, ((0, SMALL_ROWS - flat.shape[0]), (0, 0)))


def _unpack_small(packed):
    out, at = {}, 0
    for nme in SMALL_NAMES:
        shp = SMALL_SHAPES[nme]
        nrow = shp[0] * shp[1] // 128
        p = packed[at:at + nrow].reshape(shp)
        at += nrow
        out[nme] = p[:, :N_REL] if nme == "rel_bias" else p
    return out


BIG_NAMES = ("w_in", "w_branch_a", "w_branch_b", "w_out", "w_up", "w_down")


def kernel(x, w_in, lb_logits, hg_norm_w, rel_bias, w_branch_a, w_branch_b, w_out, norm_mix_w, norm_mlp_w, w_up, w_down, norm_final_w, loss_target, m_w_in, m_lb_logits, m_hg_norm_w, m_rel_bias, m_w_branch_a, m_w_branch_b, m_w_out, m_norm_mix_w, m_norm_mlp_w, m_w_up, m_w_down, m_norm_final_w, v_w_in, v_lb_logits, v_hg_norm_w, v_rel_bias, v_w_branch_a, v_w_branch_b, v_w_out, v_norm_mix_w, v_norm_mlp_w, v_w_up, v_w_down, v_norm_final_w):
    big_w = [w_in[0], w_branch_a[0], w_branch_b[0], w_out[0], w_up[0], w_down[0]]
    big_m = [m_w_in[0], m_w_branch_a[0], m_w_branch_b[0], m_w_out[0], m_w_up[0], m_w_down[0]]
    big_v = [v_w_in[0], v_w_branch_a[0], v_w_branch_b[0], v_w_out[0], v_w_up[0], v_w_down[0]]

    parity = lax.axis_index("c").astype(jnp.int32).reshape(1)
    loss_part, grad_x, chip_parts, small = _local_step(
        x[0], loss_target[0], lb_logits, hg_norm_w, rel_bias[0], norm_mix_w, norm_mlp_w,
        norm_final_w.reshape(1, D_MODEL), big_w[0].astype(BF16), big_w[1:], _Exchanges(parity, _gather_order()))
    loss = lax.psum(loss_part[0, 0], ("x", "y", "c"))
    (rs_in_lo, rs_in_hi), rs_mix, rs_up, rs_down = chip_parts
    slot =(2 * lax.axis_index("x") + lax.axis_index("y")).astype(jnp.int32).reshape(1)
    big = {}

    def finish(rs, names, after):
        sums, lands = rs.finish(after)
        for nme, own, land in zip(names, sums, lands):
            i = BIG_NAMES.index(nme)
            big[nme] = _adamw_big_landed(big_w[i], big_m[i], big_v[i], own, land, slot, "adamw_" + nme)
        return [big[nme][1] for nme in names]

    done = finish(rs_down, ["w_down"], [grad_x])
    done = finish(rs_up, ["w_up"], done)
    done = finish(rs_mix, ["w_branch_a", "w_branch_b", "w_out"], done)

    sw = dict(lb_logits=lb_logits, hg_norm_w=hg_norm_w, rel_bias=rel_bias[0], norm_mix_w=norm_mix_w,
              norm_mlp_w=norm_mlp_w, norm_final_w=norm_final_w.reshape(1, D_MODEL))
    sm = dict(lb_logits=m_lb_logits, hg_norm_w=m_hg_norm_w, rel_bias=m_rel_bias[0], norm_mix_w=m_norm_mix_w,
              norm_mlp_w=m_norm_mlp_w, norm_final_w=m_norm_final_w.reshape(1, D_MODEL))
    sv = dict(lb_logits=v_lb_logits, hg_norm_w=v_hg_norm_w, rel_bias=v_rel_bias[0], norm_mix_w=v_norm_mix_w,
              norm_mlp_w=v_norm_mlp_w, norm_final_w=v_norm_final_w.reshape(1, D_MODEL))
    gathered = _gather_small(_pack_small(small), "gather_small")
    small_packed = _adamw_small(_pack_small(sw), _pack_small(sm), _pack_small(sv), gathered, "adamw_small")
    small_out = [_unpack_small(p) for p in small_packed]

    (own,), (land,) = rs_in_lo.finish(done + [small_packed[0]])
    lo = _adamw_big_landed(big_w[0], big_m[0], big_v[0], own, land, slot, "adamw_w_in_lo")
    (own,), (land,) = rs_in_hi.finish([lo[1]])
    big["w_in"] = _adamw_big_landed(big_w[0], big_m[0], big_v[0], own, land, slot, "adamw_w_in_hi",
                                    row0=D_MODEL // 2, into=lo)

    def leaf(kind, nme):
        if nme in BIG_NAMES:
            return big[nme][kind][None]
        p = small_out[kind][nme]
        if nme == "rel_bias":
            return p[None]
        if nme == "norm_final_w":
            return p.reshape(D_MODEL)
        return p

    order = ("w_in", "lb_logits", "hg_norm_w", "rel_bias", "w_branch_a", "w_branch_b", "w_out", "norm_mix_w",
             "norm_mlp_w", "w_up", "w_down", "norm_final_w")
    outs = [loss, grad_x[None]]
    for kind in range(4):
        outs += [leaf(kind, nme) for nme in order]
    return tuple(outs)
```

```python
import jax
import jax.numpy as jnp
from jax import lax
from jax.experimental import pallas as pl
from jax.experimental.pallas import tpu as pltpu

F32 = jnp.float32
BF16 = jnp.bfloat16
HIGHEST = lax.Precision.HIGHEST
MESH = pl.DeviceIdType.MESH

D_MODEL = 2048
HG_HEADS = 8
HG_DK = 128
HG_WIDTH = 1024
AT_HEADS = 16
AT_DH = 64
AT_WIDTH = 1024
CHUNK = 64
LEFT_CHUNKS = 8
BAND = (LEFT_CHUNKS + 1) * CHUNK
PAD = LEFT_CHUNKS * CHUNK
REL_CLIP = 256
N_REL = 2 * REL_CLIP + 1
N_REL_PAD = 640
D_FF = 4 * D_MODEL
EPS = 1e-6
N_DEV = 8
N_CHIP = 4

ADAM_LR = 0.001
ADAM_B1 = 0.9
ADAM_B2 = 0.999
ADAM_EPS = 1e-08
ADAM_WD = 0.01
ADAM_STEP = 10

COL_HQ, COL_HF, COL_HI, COL_HG = 0, 8, 16, 24
COL_AQ, COL_AK, COL_AV = 32, 40, 48
COL_GATE_A, COL_GATE_B = 7, 9

VMEM_LIMIT = 56 * 1024 * 1024
SMALL_ROWS = 152


def _cparams(sem=None, **kw):
    if sem is not None:
        kw["dimension_semantics"] = sem
    return pltpu.CompilerParams(vmem_limit_bytes=VMEM_LIMIT, **kw)


def _pick(n, cands):
    for c in cands:
        if n % c == 0:
            return c
    return n


def _sigmoid(x):
    return 1.0 / (1.0 + jnp.exp(-x))


ANY = pl.BlockSpec(memory_space=pl.ANY)


def _position():
    return lax.axis_index("x"), lax.axis_index("y"), lax.axis_index("c")


def _call(body, args, *, name, grid, in_specs, out_specs, out_shape, scratch_shapes=(), sem=None, after=()):
    n_in = len(args)

    def ordered(*refs):
        body(*refs[:n_in], *refs[n_in + len(after):])

    return list(pl.pallas_call(
        ordered if after else body, name=name, grid=grid, in_specs=list(in_specs) + [ANY] * len(after),
        out_specs=out_specs, out_shape=out_shape, scratch_shapes=list(scratch_shapes),
        compiler_params=_cparams(sem))(*args, *after))


MAX_CONTRACTION_TILE = 4096


def _accumulate(part, acc_ref, step, n_steps, finish):
    if n_steps == 1:
        finish(part)
        return

    @pl.when(step == 0)
    def _():
        acc_ref[...] = part

    @pl.when(step > 0)
    def _():
        acc_ref[...] += part

    @pl.when(step == n_steps - 1)
    def _():
        finish(acc_ref[...])


def _mm_nn(a, wb, out_dtype, name, after=(), epilogue=None):
    M, K = a.shape
    NB, K2, Nb = wb.shape
    assert K == K2
    tm = min(M, 1024)
    tk = min(K, MAX_CONTRACTION_TILE)
    tn = _pick(Nb, (512, 1408, 256))
    nk = K // tk
    nn = Nb // tn
    extra, first_cols, out_dtypes, fn = epilogue or ((), (), (out_dtype,), lambda total: (total,))
    n_extra, n_out = len(extra), len(out_dtypes)

    def body(a_ref, b_ref, *rest):
        def finish(total):
            results = fn(total, *[r[...] for r in rest[:n_extra]])
            for o_ref, res, dt in zip(rest[n_extra:n_extra + n_out], results, out_dtypes):
                o_ref[...] = res.astype(dt)

        part = jnp.dot(a_ref[...], b_ref[...], preferred_element_type=F32)
        _accumulate(part, rest[-1], pl.program_id(3), nk, finish)

    def tile(first):
        return pl.BlockSpec((tm, tn), lambda m, j, n, k: (m, first + j * nn + n))

    outs = _call(
        body, (a, wb) + tuple(extra), name=name, grid=(M // tm, NB, nn, nk),
        in_specs=[pl.BlockSpec((tm, tk), lambda m, j, n, k: (m, k)),
                  pl.BlockSpec((None, tk, tn), lambda m, j, n, k: (j, k, n))] + [tile(col // tn) for col in first_cols],
        out_specs=[tile(0)] * n_out,
        out_shape=[jax.ShapeDtypeStruct((M, NB * Nb), dt) for dt in out_dtypes],
        scratch_shapes=[] if nk == 1 else [pltpu.VMEM((tm, tn), F32)],
        sem=("parallel", "parallel", "parallel", "arbitrary"), after=after)
    return outs if epilogue else outs[0]


def _squared_relu(a):
    ra = jnp.maximum(a, 0.0)
    return a, ra * ra


def _gated_merge(pb, za, zb, pa):
    return pb, _sigmoid(za) * pa + _sigmoid(zb) * pb


def _mm_nt(a, wb, out_dtype, name, after=(), epilogue=None):
    M, N = a.shape
    NB, K, Nb = wb.shape
    assert N == NB * Nb
    tm = min(M, 1024)
    n_tiles_live = 1 + (len(epilogue[0]) + len(epilogue[2]) if epilogue else 0)
    tko = _pick(K, (1024,)) if n_tiles_live <= 3 else _pick(K, (512,))
    tc = _pick(Nb, (2048, 1024, 1408, 256))
    nc = Nb // tc
    jb = max([d for d in (8, 4, 2, 1) if NB % d == 0 and d * tc <= MAX_CONTRACTION_TILE]) if nc == 1 else 1
    nsteps = (NB // jb) * nc
    extra, first_cols, out_dtypes, fn = epilogue or ((), (), (out_dtype,), lambda total: (total,))
    n_extra, n_out = len(extra), len(out_dtypes)

    def body(a_ref, b_ref, *rest):
        def finish(total):
            results = fn(total, *[r[...] for r in rest[:n_extra]])
            for o_ref, res, dt in zip(rest[n_extra:n_extra + n_out], results, out_dtypes):
                o_ref[...] = res.astype(dt)

        part = sum(lax.dot_general(a_ref[:, i * tc:(i + 1) * tc], b_ref[i], (((1,), (1,)), ((), ())),
                                   preferred_element_type=F32) for i in range(jb))
        _accumulate(part, rest[-1], pl.program_id(2) * nc + pl.program_id(3), nsteps, finish)

    def tile(first):
        return pl.BlockSpec((tm, tko), lambda m, ko, j, c: (m, first + ko))

    outs = _call(
        body, (a, wb) + tuple(extra), name=name,
        grid=(M // tm, K // tko, NB // jb, nc),
        in_specs=[pl.BlockSpec((tm, jb * tc), lambda m, ko, j, c: (m, j * nc + c)),
                  pl.BlockSpec((jb, tko, tc), lambda m, ko, j, c: (j, ko, c))] + [tile(col // tko) for col in first_cols],
        out_specs=[tile(0)] * n_out,
        out_shape=[jax.ShapeDtypeStruct((M, K), dt) for dt in out_dtypes],
        scratch_shapes=[] if nsteps == 1 else [pltpu.VMEM((tm, tko), F32)],
        sem=("parallel", "parallel", "arbitrary", "arbitrary"), after=after)
    return outs if epilogue else outs[0]


ROWS_TILE = 512
ROWS_PIECE = 128


def _mm_rows(a, w, extras, vectors, row_dtypes, fn, name):
    M, K = a.shape
    N = w.shape[1]
    tm = min(M, ROWS_TILE)
    n_e, n_v = len(extras), len(vectors)

    def body(a_ref, w_ref, *rest):
        tiles, vecs, outs, product_ref = rest[:n_e], rest[n_e:n_e + n_v], rest[n_e + n_v:-1], rest[-1]
        product_ref[...] = jnp.dot(a_ref[...], w_ref[...], preferred_element_type=F32)
        for i in range(tm // ROWS_PIECE):
            piece = slice(i * ROWS_PIECE, (i + 1) * ROWS_PIECE)
            results = fn(product_ref[piece, :], *[t[piece, :] for t in tiles], *[v[...] for v in vecs])
            for o_ref, res, dt in zip(outs, results, row_dtypes):
                o_ref[piece, :] = res.astype(dt)

    row = pl.BlockSpec((tm, N), lambda m: (m, 0))
    return _call(
        body, (a, w) + tuple(extras) + tuple(vectors), name=name, grid=(M // tm,),
        in_specs=[pl.BlockSpec((tm, K), lambda m: (m, 0)), pl.BlockSpec((K, N), lambda m: (0, 0))]
        + [row] * n_e + [pl.BlockSpec((1, N), lambda m: (0, 0))] * n_v,
        out_specs=[row] * len(row_dtypes),
        out_shape=[jax.ShapeDtypeStruct((M, N), dt) for dt in row_dtypes],
        scratch_shapes=[pltpu.VMEM((tm, N), F32)], sem=("parallel",))


def _rms(h, w):
    return h * lax.rsqrt(jnp.mean(h * h, axis=-1, keepdims=True) + EPS) * w


def _residual_rms_rows(mix, x, w):
    h = x + mix
    return h, _rms(h, w)


def _mm_tn_half(a, g, which, blocks_on, add, name, after=(), a_cols=None):
    M, Ka = a.shape
    N = g.shape[1]
    first_col = 0
    if a_cols is not None:
        first_col, Ka = a_cols
    if blocks_on == "g":
        rows, cols = _pick(Ka, (1024,)), N // N_DEV
        tn = _pick(cols, (512, 1408, 256))
        nn = cols // tn
        grid = (Ka // rows, N_CHIP, nn)
        a_spec = pl.BlockSpec((M, rows), lambda ka, s, n, w: (0, first_col // rows + ka))
        g_spec = pl.BlockSpec((M, tn), lambda ka, s, n, w: (0, (2 * s + w[0]) * nn + n))
        out_rows = Ka
    else:
        rows, cols = Ka // N_DEV, N
        tn = _pick(cols, (2048, 512))
        nn = cols // tn
        grid = (1, N_CHIP, nn)
        a_spec = pl.BlockSpec((M, rows), lambda ka, s, n, w: (0, 2 * s + w[0]))
        g_spec = pl.BlockSpec((M, tn), lambda ka, s, n, w: (0, n))
        out_rows = rows
    o_spec = pl.BlockSpec((None, rows, tn), lambda ka, s, n, w: (s, ka, n))
    n_add = 0 if add is None else 1

    def body(which_ref, a_ref, g_ref, *rest):
        acc = lax.dot_general(a_ref[...], g_ref[...], (((0,), (0,)), ((), ())), preferred_element_type=F32)
        if n_add:
            acc = acc + rest[0][...].astype(F32)
        rest[-1][...] = acc.astype(BF16)

    return pl.pallas_call(
        body, name=name,
        grid_spec=pltpu.PrefetchScalarGridSpec(
            num_scalar_prefetch=1, grid=grid,
            in_specs=[a_spec, g_spec] + [o_spec] * n_add + [ANY] * len(after),
            out_specs=o_spec),
        out_shape=jax.ShapeDtypeStruct((N_CHIP, out_rows, cols), BF16),
        compiler_params=_cparams(("parallel", "parallel", "parallel")),
    )(which, a, g, *(() if add is None else (add,)), *after)


ROW_TILE = 256


def _rms_fwd(x, w, name):
    T, Dm = x.shape

    def body(x_ref, w_ref, u_ref):
        xv = x_ref[...]
        r = lax.rsqrt(jnp.mean(xv * xv, axis=-1, keepdims=True) + EPS)
        u_ref[...] = (xv * r * w_ref[...]).astype(BF16)

    return pl.pallas_call(
        body, name=name, grid=(T // ROW_TILE,),
        in_specs=[pl.BlockSpec((ROW_TILE, Dm), lambda i: (i, 0)), pl.BlockSpec((1, Dm), lambda i: (0, 0))],
        out_specs=pl.BlockSpec((ROW_TILE, Dm), lambda i: (i, 0)),
        out_shape=jax.ShapeDtypeStruct((T, Dm), BF16),
        compiler_params=_cparams(("parallel",)),
    )(x, w)


def _loss_head(h1, mlp, wf, target, name):
    T, Dm = h1.shape

    def body(h_ref, m_ref, w_ref, t_ref, loss_ref, dh_ref, dhb_ref, dw_ref):
        i = pl.program_id(0)
        h = h_ref[...] + m_ref[...]
        r = lax.rsqrt(jnp.mean(h * h, axis=-1, keepdims=True) + EPS)
        xh = h * r
        wv = w_ref[...]
        e = xh * wv - t_ref[...]
        part = 0.5 * jnp.sum(jnp.mean(e * e, axis=-1, keepdims=True), axis=0, keepdims=True)
        dy = e * (1.0 / Dm)
        dw = jnp.sum(dy * xh, axis=0, keepdims=True)
        gy = dy * wv
        dh = r * (gy - xh * jnp.mean(gy * xh, axis=-1, keepdims=True))
        dh_ref[...] = dh
        dhb_ref[...] = dh.astype(BF16)

        @pl.when(i == 0)
        def _():
            loss_ref[...] = jnp.zeros_like(loss_ref)
            dw_ref[...] = jnp.zeros_like(dw_ref)

        loss_ref[...] += jnp.broadcast_to(part, loss_ref.shape)
        dw_ref[...] += dw

    row = pl.BlockSpec((ROW_TILE, Dm), lambda i: (i, 0))
    vec = pl.BlockSpec((1, Dm), lambda i: (0, 0))
    return pl.pallas_call(
        body, name=name, grid=(T // ROW_TILE,),
        in_specs=[row, row, vec, row],
        out_specs=[pl.BlockSpec((8, 128), lambda i: (0, 0)), row, row, vec],
        out_shape=[jax.ShapeDtypeStruct((8, 128), F32), jax.ShapeDtypeStruct((T, Dm), F32),
                   jax.ShapeDtypeStruct((T, Dm), BF16), jax.ShapeDtypeStruct((1, Dm), F32)],
        compiler_params=_cparams(("arbitrary",)),
    )(h1, mlp, wf, target)


def _rms_bwd(dyn, x, w, dres, dx_dtypes, name, after=()):
    T, Dm = x.shape
    n_dx = len(dx_dtypes)

    def body(g_ref, x_ref, w_ref, r_ref, *outs):
        i = pl.program_id(0)
        xv = x_ref[...]
        r = lax.rsqrt(jnp.mean(xv * xv, axis=-1, keepdims=True) + EPS)
        xh = xv * r
        g = g_ref[...]
        dw = jnp.sum(g * xh, axis=0, keepdims=True)
        gy = g * w_ref[...]
        dx = r_ref[...] + r * (gy - xh * jnp.mean(gy * xh, axis=-1, keepdims=True))
        for dx_ref, dt in zip(outs, dx_dtypes):
            dx_ref[...] = dx.astype(dt)
        dw_ref = outs[n_dx]

        @pl.when(i == 0)
        def _():
            dw_ref[...] = jnp.zeros_like(dw_ref)

        dw_ref[...] += dw

    row = pl.BlockSpec((ROW_TILE, Dm), lambda i: (i, 0))
    vec = pl.BlockSpec((1, Dm), lambda i: (0, 0))
    return _call(
        body, (dyn, x, w, dres), name=name, grid=(T // ROW_TILE,),
        in_specs=[row, row, vec, row],
        out_specs=[row] * n_dx + [vec],
        out_shape=[jax.ShapeDtypeStruct((T, Dm), dt) for dt in dx_dtypes] + [jax.ShapeDtypeStruct((1, Dm), F32)],
        sem=("arbitrary",), after=after)


GATE_TILE = 1024


def _merge_grads(d, za, zb, pa, pb):
    ga = _sigmoid(za)
    gb = _sigmoid(zb)
    return d * ga, d * gb, d * pa * ga * (1.0 - ga), d * pb * gb * (1.0 - gb)


def _dot_hi(a, b, dims):
    return lax.dot_general(a, b, (dims, ((), ())), precision=HIGHEST, preferred_element_type=F32)


NN = ((1,), (0,))
NT = ((1,), (1,))
TN = ((0,), (0,))


def _hg_gates(hq, hf, lb):
    sq = _sigmoid(hq)
    q = hq * sq * (HG_DK ** -0.5)
    f = _sigmoid(hf)
    g = lb + (1.0 - lb) * f
    return q, sq, f, g, jnp.log(g), 1.0 - g


def _tri(lower):
    r = lax.broadcasted_iota(jnp.int32, (CHUNK, CHUNK), 0)
    c = lax.broadcasted_iota(jnp.int32, (CHUNK, CHUNK), 1)
    return jnp.where((r >= c) if lower else (r <= c), 1.0, 0.0).astype(BF16)


def _running_sum(tri, x):
    return sum(jnp.dot(tri, piece, preferred_element_type=F32) for piece in _split3(x))


GROUP = 16
N_GROUPS = CHUNK // GROUP
BWD_CHUNKS_PER_TRIP = 4


def _dot_bf16(a, b, dims):
    return lax.dot_general(a.astype(BF16), b.astype(BF16), (dims, ((), ())), preferred_element_type=F32)


def _rows_iota():
    return lax.broadcasted_iota(jnp.int32, (CHUNK, HG_DK), 0)


def _by_query_group(q, kk, b, g):
    r0 = GROUP * g
    b0 = b[r0:r0 + 1]
    decay = jnp.exp(b[r0:r0 + GROUP] - b0)
    ks = jnp.where(_rows_iota() < r0, kk * jnp.exp(jnp.minimum(b0 - b, 0.0)), 0.0)
    return q[r0:r0 + GROUP] * decay, ks, decay


def _by_key_group(q, kk, b, j):
    r1 = GROUP * (j + 1)
    b1 = b[r1 - 1:r1]
    decay = jnp.exp(b1 - b[r1 - GROUP:r1])
    qs = jnp.where(_rows_iota() >= r1, q * jnp.exp(jnp.minimum(b - b1, 0.0)), 0.0)
    return qs, kk[r1 - GROUP:r1] * decay, decay


def _scores_between_groups(q, kk, b):
    blocks = [jnp.zeros((GROUP, CHUNK), F32)]
    for g in range(1, N_GROUPS):
        qs, ks, _ = _by_query_group(q, kk, b, g)
        blocks.append(_dot_bf16(qs, ks, NT))
    return jnp.concatenate(blocks, axis=0)


def _hgrn2_fwd(z, lb_logits, hg_norm_w, name, after=()):
    T = z.shape[0]
    n_chunks = T // CHUNK

    def body(hq_ref, hf_ref, hi_ref, hg_ref, lbl_ref, nw_ref, o_ref, ya_ref, sall_ref, st_ref):
        lbl = lbl_ref[...]
        lb = 1.0 / (1.0 + jnp.exp(lbl[1:2, :] - lbl[0:1, :]))
        st_ref[...] = jnp.zeros_like(st_ref)
        tri = _tri(True)
        row8 = lax.broadcasted_iota(jnp.int32, (8, HG_DK), 0)

        def chunk(c, carry):
            rows = pl.ds(pl.multiple_of(c * CHUNK, CHUNK), CHUNK)
            q, _, _, _, lg, kk = _hg_gates(hq_ref[rows, :], hf_ref[rows, :], lb)
            v = hi_ref[rows, :]
            b = _running_sum(tri, lg)
            st = st_ref[...]
            sall_ref[c] = st
            for grp in range(N_GROUPS):
                r0 = GROUP * grp
                for h8 in range(GROUP // 8):
                    n = 8 * (h8 + 1)
                    bs, ks, vs = b[r0:r0 + n], kk[r0:r0 + n], v[r0:r0 + n]
                    sidx = lax.broadcasted_iota(jnp.int32, (n, HG_DK), 0)
                    blk = jnp.zeros((8, HG_DK), F32)
                    for i in range(8):
                        t = r0 + 8 * h8 + i
                        e = jnp.where(sidx <= 8 * h8 + i, jnp.exp(b[t:t + 1] - bs), 0.0)
                        p = jnp.sum(e * ks * q[t:t + 1], axis=1, keepdims=True)
                        ot = jnp.sum(p * vs, axis=0, keepdims=True)
                        blk = blk + jnp.where(row8 == i, ot, 0.0)
                    o_ref[pl.ds(pl.multiple_of(c * CHUNK + r0 + 8 * h8, 8), 8), :] = blk
            o_ref[rows, :] += _dot_hi(q * jnp.exp(b), st, NT) + _dot_bf16(_scores_between_groups(q, kk, b), v, NN)
            bl = b[CHUNK - 1:CHUNK]
            ke = kk * jnp.exp(bl - b)
            st_ref[...] = st * jnp.exp(bl) + _dot_hi(v, ke, TN)
            return carry

        lax.fori_loop(0, n_chunks, chunk, 0, unroll=2)
        o = o_ref[...]
        r = lax.rsqrt(jnp.mean(o * o, axis=-1, keepdims=True) + EPS)
        hg = hg_ref[...]
        ya_ref[...] = (o * r * nw_ref[...] * (hg * _sigmoid(hg))).astype(BF16)

    def col(base):
        return pl.BlockSpec((T, HG_DK), lambda h: (0, base + h))

    return _call(
        body, (z, z, z, z, lb_logits, hg_norm_w), name=name, grid=(HG_HEADS,),
        in_specs=[col(COL_HQ), col(COL_HF), col(COL_HI), col(COL_HG),
                  pl.BlockSpec((2, HG_DK), lambda h: (0, h)), pl.BlockSpec((1, HG_DK), lambda h: (0, 0))],
        out_specs=[col(0), col(0), pl.BlockSpec((None, n_chunks, HG_DK, HG_DK), lambda h: (h, 0, 0, 0))],
        out_shape=[jax.ShapeDtypeStruct((T, HG_WIDTH), F32), jax.ShapeDtypeStruct((T, HG_WIDTH), BF16),
                   jax.ShapeDtypeStruct((HG_HEADS, n_chunks, HG_DK, HG_DK), F32)],
        scratch_shapes=[pltpu.VMEM((HG_DK, HG_DK), F32)],
        sem=("parallel",), after=after)


def _hgrn2_bwd(z, lb_logits, hg_norm_w, o_raw, s_all, dya, name, after=()):
    T = z.shape[0]
    n_chunks = T // CHUNK

    def body(hq_ref, hf_ref, hi_ref, hg_ref, lbl_ref, nw_ref, o_ref, sall_ref, dya_ref,
             dhq_ref, dhf_ref, dhi_ref, dhg_ref, dlbl_ref, dnw_ref,
             do_ref, dst_ref, dlb_ref, *per_chunk):
        h = pl.program_id(0)
        lbl = lbl_ref[...]
        lb = 1.0 / (1.0 + jnp.exp(lbl[1:2, :] - lbl[0:1, :]))

        o = o_ref[...]
        r = lax.rsqrt(jnp.mean(o * o, axis=-1, keepdims=True) + EPS)
        oh = o * r
        nw = nw_ref[...]
        hg = hg_ref[...]
        sg = _sigmoid(hg)
        dy = dya_ref[...]
        d_on = dy * (hg * sg)
        dhg_ref[...] = (dy * (oh * nw) * (sg * (1.0 + hg * (1.0 - sg)))).astype(BF16)
        dnw = jnp.sum(d_on * oh, axis=0, keepdims=True)
        gy = d_on * nw
        do_ref[...] = r * (gy - oh * jnp.mean(gy * oh, axis=-1, keepdims=True))

        @pl.when(h == 0)
        def _():
            dnw_ref[...] = jnp.zeros_like(dnw_ref)

        dnw_ref[...] += jnp.broadcast_to(dnw, dnw_ref.shape)

        dst_ref[...] = jnp.zeros_like(dst_ref)
        dlb_ref[...] = jnp.zeros_like(dlb_ref)
        tri = _tri(True)
        tri_t = _tri(False)
        row8 = lax.broadcasted_iota(jnp.int32, (8, HG_DK), 0)
        row_group = lax.broadcasted_iota(jnp.int32, (CHUNK, CHUNK), 0) // GROUP
        col_group = lax.broadcasted_iota(jnp.int32, (CHUNK, CHUNK), 1) // GROUP
        earlier_group = col_group < row_group
        later_group = col_group > row_group

        def chunk(c, dq_ref, dk_ref, dv_ref):
            rows = pl.ds(pl.multiple_of(c * CHUNK, CHUNK), CHUNK)
            hq = hq_ref[rows, :]
            q, sq, f, g, lg, kk = _hg_gates(hq, hf_ref[rows, :], lb)
            v = hi_ref[rows, :]
            do = do_ref[rows, :]
            b = _running_sum(tri, lg)
            eb = jnp.exp(b)
            bl = b[CHUNK - 1:CHUNK]
            ebl = jnp.exp(bl)
            ekb = jnp.exp(bl - b)
            qe = q * eb
            ke = kk * ekb
            st = sall_ref[c]
            dst = dst_ref[...]
            dqe = _dot_bf16(do, st, NN)
            dke = _dot_bf16(v, dst, NN)
            dv_inter = _dot_bf16(ke, dst, NT)
            d_ebl = jnp.sum(st * dst, axis=0, keepdims=True)
            dst_ref[...] = dst * ebl + _dot_bf16(do, qe, TN)

            dk_ref[...] = jnp.zeros_like(dk_ref)
            dv_ref[...] = jnp.zeros_like(dv_ref)
            for grp in range(N_GROUPS):
                r0 = GROUP * grp
                for h8 in range(GROUP // 8):
                    n = 8 * (h8 + 1)
                    bs, ks, vs = b[r0:r0 + n], kk[r0:r0 + n], v[r0:r0 + n]
                    sidx = lax.broadcasted_iota(jnp.int32, (n, HG_DK), 0)
                    blk = jnp.zeros((8, HG_DK), F32)
                    for i in range(8):
                        t = r0 + 8 * h8 + i
                        qt = q[t:t + 1]
                        dot_ = do[t:t + 1]
                        e = jnp.where(sidx <= 8 * h8 + i, jnp.exp(b[t:t + 1] - bs), 0.0)
                        w = e * ks
                        p = jnp.sum(w * qt, axis=1, keepdims=True)
                        dsc = jnp.sum(vs * dot_, axis=1, keepdims=True)
                        dqt = jnp.sum(dsc * w, axis=0, keepdims=True)
                        blk = blk + jnp.where(row8 == i, dqt, 0.0)
                        dk_ref[r0:r0 + n, :] += dsc * e * qt
                        dv_ref[r0:r0 + n, :] += p * dot_
                    dq_ref[r0 + 8 * h8:r0 + n, :] = blk
            ds_far = jnp.where(earlier_group, _dot_bf16(do, v, NT), 0.0)
            ds_far_t = jnp.where(later_group, _dot_bf16(v, do, NT), 0.0)
            dq_far, dk_far = [jnp.zeros((GROUP, HG_DK), F32)], []
            for grp in range(1, N_GROUPS):
                r0 = GROUP * grp
                _, ks, decay = _by_query_group(q, kk, b, grp)
                dq_far.append(decay * _dot_hi(ds_far[r0:r0 + GROUP], ks, NN))
                qs, _, decay = _by_key_group(q, kk, b, grp - 1)
                dk_far.append(decay * _dot_hi(ds_far_t[r0 - GROUP:r0], qs, NN))
            dk_far.append(jnp.zeros((GROUP, HG_DK), F32))
            dv_far = _dot_bf16(_scores_between_groups(q, kk, b), do, TN)
            dq_i = dq_ref[...] + jnp.concatenate(dq_far, axis=0)
            dk_i = dk_ref[...] + jnp.concatenate(dk_far, axis=0)
            dke_ke = dke * ke
            db = q * dq_i - kk * dk_i + dqe * qe - dke_ke
            db_last = jnp.sum(dke_ke, axis=0, keepdims=True) + d_ebl * ebl
            dlg = _running_sum(tri_t, db) + db_last
            dq = dq_i + dqe * eb
            dkk = dk_i + dke * ekb
            dg = dlg / g - dkk
            dhq_ref[rows, :] = (dq * (HG_DK ** -0.5) * (sq * (1.0 + hq * (1.0 - sq)))).astype(BF16)
            dhf_ref[rows, :] = (dg * (1.0 - lb) * f * (1.0 - f)).astype(BF16)
            dhi_ref[rows, :] = (dv_ref[...] + dv_far + dv_inter).astype(BF16)
            dlb_ref[...] += jnp.sum(dg * (1.0 - f), axis=0, keepdims=True)

        def trip(i, carry):
            for k in range(BWD_CHUNKS_PER_TRIP):
                chunk(n_chunks - 1 - k - BWD_CHUNKS_PER_TRIP * i, *per_chunk[3 * k:3 * k + 3])
            return carry

        lax.fori_loop(0, n_chunks // BWD_CHUNKS_PER_TRIP, trip, 0)
        dl0 = dlb_ref[...] * lb * (1.0 - lb)
        dlbl_ref[0:1, :] = dl0
        dlbl_ref[1:2, :] = -dl0

    def col(base):
        return pl.BlockSpec((T, HG_DK), lambda h: (0, base + h))

    outb = jax.ShapeDtypeStruct((T, HG_WIDTH), BF16)
    return _call(
        body, (z, z, z, z, lb_logits, hg_norm_w, o_raw, s_all, dya), name=name, grid=(HG_HEADS,),
        in_specs=[col(COL_HQ), col(COL_HF), col(COL_HI), col(COL_HG),
                  pl.BlockSpec((2, HG_DK), lambda h: (0, h)), pl.BlockSpec((1, HG_DK), lambda h: (0, 0)),
                  col(0), pl.BlockSpec((None, n_chunks, HG_DK, HG_DK), lambda h: (h, 0, 0, 0)), col(0)],
        out_specs=[col(0), col(0), col(0), col(0), pl.BlockSpec((2, HG_DK), lambda h: (0, h)),
                   pl.BlockSpec((8, HG_DK), lambda h: (0, 0))],
        out_shape=[outb, outb, outb, outb, jax.ShapeDtypeStruct((2, HG_WIDTH), F32),
                   jax.ShapeDtypeStruct((8, HG_DK), F32)],
        scratch_shapes=[pltpu.VMEM((T, HG_DK), F32), pltpu.VMEM((HG_DK, HG_DK), F32), pltpu.VMEM((1, HG_DK), F32)]
        + [pltpu.VMEM((CHUNK, HG_DK), F32)] * (3 * BWD_CHUNKS_PER_TRIP),
        sem=("arbitrary",), after=after)


CONST_KEYS = PAD - REL_CLIP
VAR_KEYS = BAND - CONST_KEYS
REL_LO = 128
REL_SPAN = N_REL_PAD - REL_LO


def _rel_onehot(t):
    r = lax.broadcasted_iota(jnp.int32, (REL_SPAN, VAR_KEYS), 0)
    j = lax.broadcasted_iota(jnp.int32, (REL_SPAN, VAR_KEYS), 1)
    idx = jnp.clip(t + PAD - CONST_KEYS - j, -REL_CLIP, REL_CLIP) + REL_CLIP - REL_LO
    return jnp.where(r == idx, 1.0, 0.0).astype(BF16)


def _split3(x):
    hi = x.astype(BF16)
    r1 = x - hi.astype(F32)
    mid = r1.astype(BF16)
    return hi, mid, (r1 - mid.astype(F32)).astype(BF16)


def _bias_expand(rel, name):
    def body(rel_ref, out_ref):
        tab = rel_ref[...]
        onehot = _rel_onehot(pl.program_id(0))
        out_ref[:, 0:CONST_KEYS] = jnp.broadcast_to(tab[:, 2 * REL_CLIP:2 * REL_CLIP + 1], (AT_HEADS, CONST_KEYS))
        out_ref[:, CONST_KEYS:BAND] = sum(
            jnp.dot(piece, onehot, preferred_element_type=F32) for piece in _split3(tab[:, REL_LO:N_REL_PAD]))

    return pl.pallas_call(
        body, name=name, grid=(CHUNK,),
        in_specs=[pl.BlockSpec((AT_HEADS, N_REL_PAD), lambda t: (0, 0))],
        out_specs=pl.BlockSpec((None, AT_HEADS, BAND), lambda t: (t, 0, 0)),
        out_shape=jax.ShapeDtypeStruct((CHUNK, AT_HEADS, BAND), F32),
        compiler_params=_cparams(("parallel",)),
    )(rel)


def _bias_reduce(dbias_rows, name, after=()):
    def body(db_ref, out_ref):
        lane = lax.broadcasted_iota(jnp.int32, (AT_HEADS, N_REL_PAD), 1)
        varying = lane >= CONST_KEYS
        by_offset = jnp.zeros((AT_HEADS, N_REL_PAD), F32)
        constant = jnp.zeros((AT_HEADS, N_REL_PAD), F32)
        for t in range(CHUNK):
            row = db_ref[t]
            constant = constant + jnp.where(varying, 0.0, row)
            moved = jnp.where(varying, row, 0.0)
            by_offset = by_offset + (pltpu.roll(moved, N_REL_PAD - t, axis=1) if t else moved)
        offset = lax.broadcasted_iota(jnp.int32, (N_REL_PAD, N_REL_PAD), 0)
        entry = lax.broadcasted_iota(jnp.int32, (N_REL_PAD, N_REL_PAD), 1)
        onehot = jnp.where(entry == jnp.clip(PAD - offset, -REL_CLIP, REL_CLIP) + REL_CLIP, 1.0, 0.0).astype(BF16)
        acc = sum(jnp.dot(piece, onehot, preferred_element_type=F32) for piece in _split3(by_offset))
        last = jnp.sum(constant, axis=1, keepdims=True)
        out_ref[...] = acc + jnp.where(lane == 2 * REL_CLIP, last, 0.0)

    whole = pl.BlockSpec((CHUNK, AT_HEADS, N_REL_PAD), lambda i: (0, 0, 0))
    return _call(
        body, (dbias_rows,), name=name, grid=(1,), in_specs=[whole],
        out_specs=[pl.BlockSpec((AT_HEADS, N_REL_PAD), lambda i: (0, 0))],
        out_shape=[jax.ShapeDtypeStruct((AT_HEADS, N_REL_PAD), F32)],
        sem=("arbitrary",), after=after)[0]


def _pair_lanes():
    return lax.broadcasted_iota(jnp.int32, (CHUNK, 2 * AT_DH), 1) < AT_DH


def _block_diag(a):
    first = _pair_lanes()
    return jnp.concatenate([jnp.where(first, a, 0.0), jnp.where(first, 0.0, a)], axis=0).astype(BF16)


def _diag_blocks(a):
    return jnp.where(_pair_lanes(), a[:CHUNK], a[CHUNK:])


def _band_probs_t(kb, qbd, bias_t, c):
    s = lax.dot_general(kb, qbd, (NT, ((), ())), preferred_element_type=F32) * (AT_DH ** -0.5) + bias_t
    j = lax.broadcasted_iota(jnp.int32, (BAND, 2 * AT_DH), 0)
    s = jnp.where(j + c * CHUNK >= PAD, s, -jnp.inf)
    p = jnp.exp(s - jnp.max(s, axis=0, keepdims=True))
    return p / jnp.sum(p, axis=0, keepdims=True)


def _attn_fwd(z, bias_t, name, after=()):
    T = z.shape[0]
    n_chunks = T // CHUNK

    def body(q_ref, k_ref, v_ref, bias_ref, y_ref, p_ref, *scratch):
        for pr in range(2):
            lanes = slice(128 * pr, 128 * (pr + 1))
            for dst_ref, src_ref in zip(scratch[2 * pr:2 * pr + 2], (k_ref, v_ref)):
                dst_ref[0:PAD, :] = jnp.zeros((PAD, 128), BF16)
                dst_ref[PAD:PAD + T, :] = src_ref[:, lanes].astype(BF16)

        def chunk(c, carry):
            rows = pl.ds(pl.multiple_of(c * CHUNK, CHUNK), CHUNK)
            band = pl.ds(pl.multiple_of(c * CHUNK, CHUNK), BAND)
            for pr in range(2):
                kp_ref, vp_ref = scratch[2 * pr:2 * pr + 2]
                lanes = slice(128 * pr, 128 * (pr + 1))
                p = _band_probs_t(kp_ref[band, :], _block_diag(q_ref[rows, lanes]), bias_ref[pr], c).astype(BF16)
                p_ref[pr, c] = p
                o2 = lax.dot_general(p, vp_ref[band, :], (TN, ((), ())), preferred_element_type=F32)
                y_ref[rows, lanes] = _diag_blocks(o2).astype(BF16)
            return carry

        lax.fori_loop(0, n_chunks, chunk, 0, unroll=2)

    def col(base):
        return pl.BlockSpec((T, 256), lambda h: (0, base // 2 + h))

    return _call(
        body, (z, z, z, bias_t), name=name, grid=(AT_HEADS // 4,),
        in_specs=[col(COL_AQ), col(COL_AK), col(COL_AV), pl.BlockSpec((2, BAND, 128), lambda h: (h, 0, 0))],
        out_specs=[col(0), pl.BlockSpec((2, n_chunks, BAND, 128), lambda h: (h, 0, 0, 0))],
        out_shape=[jax.ShapeDtypeStruct((T, AT_WIDTH), BF16),
                   jax.ShapeDtypeStruct((AT_HEADS // 2, n_chunks, BAND, 128), BF16)],
        scratch_shapes=[pltpu.VMEM((PAD + T, 128), BF16)] * 4,
        sem=("parallel",), after=after)


def _attn_bwd(z, probs, dyb, name, after=()):
    T = z.shape[0]
    n_chunks = T // CHUNK

    def body(q_ref, k_ref, v_ref, p_ref, dy_ref, dq_ref, dk_ref, dv_ref, dbias_ref, *scratch):
        dbias_ref[...] = jnp.zeros_like(dbias_ref)
        for pr in range(2):
            kp_ref, vp_ref, dkp_ref, dvp_ref = scratch[4 * pr:4 * pr + 4]
            lanes = slice(128 * pr, 128 * (pr + 1))
            kp_ref[0:PAD, :] = jnp.zeros((PAD, 128), BF16)
            vp_ref[0:PAD, :] = jnp.zeros((PAD, 128), BF16)
            kp_ref[PAD:PAD + T, :] = k_ref[:, lanes].astype(BF16)
            vp_ref[PAD:PAD + T, :] = v_ref[:, lanes].astype(BF16)
            dkp_ref[...] = jnp.zeros_like(dkp_ref)
            dvp_ref[...] = jnp.zeros_like(dvp_ref)

        def chunk(c, carry):
            rows = pl.ds(pl.multiple_of(c * CHUNK, CHUNK), CHUNK)
            band = pl.ds(pl.multiple_of(c * CHUNK, CHUNK), BAND)
            for pr in range(2):
                kp_ref, vp_ref, dkp_ref, dvp_ref = scratch[4 * pr:4 * pr + 4]
                lanes = slice(128 * pr, 128 * (pr + 1))
                qbd = _block_diag(q_ref[rows, lanes])
                dobd = _block_diag(dy_ref[rows, lanes])
                pb = p_ref[pr, c]
                p = pb.astype(F32)
                dp = lax.dot_general(vp_ref[band, :], dobd, (NT, ((), ())), preferred_element_type=F32)
                ds = p * (dp - jnp.sum(dp * p, axis=0, keepdims=True))
                dbias_ref[pr] += ds
                dsb = ds.astype(BF16)
                dq2 = lax.dot_general(dsb, kp_ref[band, :], (TN, ((), ())), preferred_element_type=F32)
                dq_ref[rows, lanes] = (_diag_blocks(dq2) * (AT_DH ** -0.5)).astype(BF16)
                dkp_ref[band, :] += jnp.dot(dsb, qbd, preferred_element_type=F32) * (AT_DH ** -0.5)
                dvp_ref[band, :] += jnp.dot(pb, dobd, preferred_element_type=F32)
            return carry

        lax.fori_loop(0, n_chunks, chunk, 0)
        for pr in range(2):
            lanes = slice(128 * pr, 128 * (pr + 1))
            dk_ref[:, lanes] = scratch[4 * pr + 2][PAD:PAD + T, :].astype(BF16)
            dv_ref[:, lanes] = scratch[4 * pr + 3][PAD:PAD + T, :].astype(BF16)

    def col(base):
        return pl.BlockSpec((T, 256), lambda h: (0, base // 2 + h))

    outb = jax.ShapeDtypeStruct((T, AT_WIDTH), BF16)
    return _call(
        body, (z, z, z, probs, dyb), name=name, grid=(AT_HEADS // 4,),
        in_specs=[col(COL_AQ), col(COL_AK), col(COL_AV),
                  pl.BlockSpec((2, n_chunks, BAND, 128), lambda h: (h, 0, 0, 0)), col(0)],
        out_specs=[col(0), col(0), col(0), pl.BlockSpec((2, BAND, 128), lambda h: (h, 0, 0))],
        out_shape=[outb, outb, outb, jax.ShapeDtypeStruct((AT_HEADS // 2, BAND, 128), F32)],
        scratch_shapes=([pltpu.VMEM((PAD + T, 128), BF16)] * 2 + [pltpu.VMEM((PAD + T, 128), F32)] * 2) * 2,
        sem=("parallel",), after=after)


def _local_step(x, target, lb_logits, hg_norm_w, rel_bias, norm_mix_w, norm_mlp_w, norm_final_w,
                w_in, rest, exchanges=None):
    ex = exchanges
    rel = jnp.pad(rel_bias, ((0, 0), (0, N_REL_PAD - N_REL)))

    u = _rms_fwd(x, norm_mix_w, "rms_mix_fwd")
    if ex:
        z, w_in = _mm_gathered(u, w_in, ex.order, "mm_in_fwd")
        gather = _Gather(rest[:3], [w_in], "ag")
        gather_mlp = _Gather(rest[3:], [gather.token], "ag_mlp")
        z = _mm_gathered_tail(u, w_in, z, ex.order, "mm_in_fwd_tail", after=[gather_mlp.token])
        tok = []
    else:
        z = _mm_nn(u, w_in, F32, "mm_in_fwd")
        w_a, w_b, w_out, w_up, w_down = rest
        tok = []
    o_raw, y_a, s_all = _hgrn2_fwd(z, lb_logits, hg_norm_w, "hgrn2_fwd", after=tok)
    if ex:
        tok = [gather.pass_on([0, 1, 2], [o_raw], "abo")]
    bias_rows = _bias_expand(rel, "bias_expand")
    bias_t = jnp.transpose(bias_rows.reshape(CHUNK, AT_HEADS // 2, 2, BAND), (1, 3, 2, 0)).reshape(
        AT_HEADS // 2, BAND, 2 * CHUNK)
    y_b, probs = _attn_fwd(z, bias_t, "attn_fwd", after=tok)
    if ex:
        tok = [gather_mlp.pass_on([0], [y_b], "up")]
        w_a, w_b, w_out = gather.finish([0, 1, 2], tok, "abo")
    pa = _mm_nn(y_a, w_a, F32, "mm_a_fwd")
    pb, merged = _mm_nn(y_b, w_b, None, "mm_b_fwd", epilogue=(
        (z, z, pa), (COL_GATE_A * GATE_TILE, COL_GATE_B * GATE_TILE, 0), (F32, BF16), _gated_merge))
    w_out1 = w_out.reshape(1, D_MODEL, D_MODEL)
    h1, u2 = _mm_rows(merged, w_out.reshape(D_MODEL, D_MODEL), [x], [norm_mlp_w], (F32, BF16),
                      _residual_rms_rows, "mm_out_fwd")
    if ex:
        tok = [gather_mlp.pass_on([1], [u2], "down")]
        w_up, = gather_mlp.finish([0], tok, "up")
    a, r = _mm_nn(u2, w_up, None, "mm_up_fwd", epilogue=((), (), (F32, BF16), _squared_relu))
    if ex:
        w_down, = gather_mlp.finish([1], [r], "down")
    w_down1 = w_down.reshape(1, D_FF, D_MODEL)
    mlp = _mm_nn(r, w_down1, F32, "mm_down_fwd")
    loss, dh2, dh2b, g_nf = _loss_head(h1, mlp, norm_final_w, target, "loss_head")

    own = ex.parity if ex else jnp.zeros((1,), jnp.int32)

    def sibling_half(weights, name, after=()):
        others = [_mm_tn_half(a_, g_, 1 - own, on, None, nm + "_sibling", after, *cols)
                  for a_, g_, on, nm, *cols in weights]
        rs = _ReduceScatter(others, name) if ex else None
        return rs, others, ([rs.token] if ex else [])

    def own_half(rs, weights, others, after):
        landed = rs.from_sibling(after) if ex else [None] * len(weights)
        sums = [_mm_tn_half(a_, g_, own, on, l, nm + "_own", (), *cols)
                for (a_, g_, on, nm, *cols), l in zip(weights, landed)]
        if ex:
            return [rs.scatter(sums)], None
        return [], [jnp.stack([s_, o_], axis=1).reshape((N_DEV,) + s_.shape[1:]) for s_, o_ in zip(sums, others)]

    down = [(r, dh2b, "a", "mm_down_wgrad")]
    rs_down, others, tok = sibling_half(down, "rs_down")
    da, = _mm_nt(dh2b, w_down1, None, "mm_down_dgrad", after=tok, epilogue=(
        (a,), (0,), (BF16,), lambda dr, av: (dr * (2.0 * jnp.maximum(av, 0.0)),)))
    tok, g_down = own_half(rs_down, down, others, [da])
    up = [(u2, da, "g", "mm_up_wgrad")]
    rs_up, others, tok = sibling_half(up, "rs_up", tok)
    du2 = _mm_nt(da, w_up, F32, "mm_up_dgrad", after=tok)
    tok, g_up = own_half(rs_up, up, others, [du2])
    dh1, dh1b, g_nmlp = _rms_bwd(du2, h1, norm_mlp_w, dh2, (F32, BF16), "rms_mlp_bwd", after=tok)

    dpa, dpb, dga, dgb = _mm_nt(dh1b, w_out1, None, "mm_out_dgrad", epilogue=(
        (z, z, pa, pb), (COL_GATE_A * GATE_TILE, COL_GATE_B * GATE_TILE, 0, 0), (BF16,) * 4, _merge_grads))
    mix = [(y_a, dpa, "g", "mm_a_wgrad"), (y_b, dpb, "g", "mm_b_wgrad"), (merged, dh1b, "a", "mm_out_wgrad")]
    rs_mix, others, tok = sibling_half(mix, "rs_mix")
    dya = _mm_nt(dpa, w_a, F32, "mm_a_dgrad", after=tok)
    dyb = _mm_nt(dpb, w_b, F32, "mm_b_dgrad", after=tok)
    tok, g_mix = own_half(rs_mix, mix, others, [dya, dyb])
    daq, dak, dav, dbias_t = _attn_bwd(z, probs, dyb, "attn_bwd", after=tok)
    dhq, dhf, dhi, dhg, g_lbl, g_hgw = _hgrn2_bwd(z, lb_logits, hg_norm_w, o_raw, s_all, dya, "hgrn2_bwd",
                                                  after=tok)
    dbias_rows = jnp.pad(jnp.transpose(dbias_t.reshape(AT_HEADS // 2, BAND, 2, CHUNK), (3, 0, 2, 1)).reshape(
        CHUNK, AT_HEADS, BAND), ((0, 0), (0, 0), (0, N_REL_PAD - BAND)))
    dz = jnp.concatenate([dhq, dhf, dhi, dhg, daq, dak, dav, dga, dgb], axis=1)
    half = D_MODEL // 2
    lo = [(u, dz, "g", "mm_in_wgrad_lo", (0, half))]
    hi = [(u, dz, "g", "mm_in_wgrad_hi", (half, half))]
    rs_in_lo, others_lo, tok = sibling_half(lo, "rs_in_lo")
    rs_in_hi, others_hi, tok = sibling_half(hi, "rs_in_hi", tok)
    tok, g_in_lo = own_half(rs_in_lo, lo, others_lo, tok)
    du = _mm_nt(dz, w_in, F32, "mm_in_dgrad", after=tok)
    tok, g_in_hi = own_half(rs_in_hi, hi, others_hi, [du])
    grad_x, g_nmix = _rms_bwd(du, x, norm_mix_w, dh1, (F32,), "rms_mix_bwd", after=tok)
    g_rel = _bias_reduce(dbias_rows, "bias_reduce", after=tok)[:, :N_REL]

    small = dict(lb_logits=g_lbl, hg_norm_w=g_hgw[0:1], rel_bias=g_rel, norm_mix_w=g_nmix, norm_mlp_w=g_nmlp,
                 norm_final_w=g_nf)
    if ex:
        grads = [(rs_in_lo, rs_in_hi), rs_mix, rs_up, rs_down]
    else:
        grads = [jnp.concatenate([g_in_lo[0], g_in_hi[0]], axis=1)] + g_mix + [g_up[0], g_down[0]]
    return loss, grad_x, grads, small


def _mm_gathered(u, shard, order, name):
    T, K = u.shape
    _, Nb = shard.shape

    def body(order_ref, u_ref, shard_ref, z_ref, full_ref, wbuf, load_sem, send_sems, recv_sems, local_sem):
        s = pl.program_id(0)
        x, y, c = _position()
        me, sibling = (x, y, c), (x, y, 1 - c)
        chips = [(1 - x, y), (x, 1 - y), (1 - x, 1 - y)]

        def copy(k, block, to, src=None):
            dst = full_ref.at[4 * block[0] + 2 * block[1] + block[2]]
            return pltpu.make_async_remote_copy(
                src_ref=dst if src is None else src, dst_ref=dst,
                send_sem=send_sems.at[k], recv_sem=recv_sems.at[k], device_id=to, device_id_type=MESH)

        @pl.when(s == 0)
        def _():
            local = pltpu.make_async_copy(shard_ref, full_ref.at[4 * x + 2 * y + c], local_sem)
            local.start()
            copy(0, me, sibling, src=shard_ref).start()
            for j, chip in enumerate(chips):
                copy(1 + j, me, (*chip, c), src=shard_ref).start()
            local.wait()

        @pl.when(s == 1)
        def _():
            copy(0, sibling, me).wait_recv()

        for j, chip in enumerate(chips):
            direct, passed = ((2, 4), (3, 5), (6, 7))[j]

            @pl.when(s == direct)
            def _(j=j, chip=chip):
                copy(1 + j, (*chip, c), me).wait_recv()
                copy(4 + j, (*chip, c), sibling).start()

            @pl.when(s == passed)
            def _(j=j, chip=chip):
                copy(4 + j, (*chip, 1 - c), me).wait_recv()

        @pl.when(s < N_EARLY_BLOCKS)
        def _():
            load = pltpu.make_async_copy(full_ref.at[order_ref[s]], wbuf, load_sem)
            load.start()
            load.wait()
            z_ref[...] = jnp.dot(u_ref[...], wbuf[...], preferred_element_type=F32)

        @pl.when(s == N_DEV - 1)
        def _():
            for k in range(7):
                copy(k, me, sibling).wait_send()

    z, full = pl.pallas_call(
        body, name=name,
        grid_spec=pltpu.PrefetchScalarGridSpec(
            num_scalar_prefetch=1, grid=(N_DEV,),
            in_specs=[pl.BlockSpec((T, K), lambda s, order: (0, 0)), ANY],
            out_specs=[pl.BlockSpec((T, Nb), lambda s, order: (0, order[jnp.minimum(s, N_EARLY_BLOCKS - 1)])), ANY],
            scratch_shapes=[pltpu.VMEM((K, Nb), BF16), pltpu.SemaphoreType.DMA,
                            pltpu.SemaphoreType.DMA((7,)), pltpu.SemaphoreType.DMA((7,)), pltpu.SemaphoreType.DMA]),
        out_shape=[jax.ShapeDtypeStruct((T, N_DEV * Nb), F32), jax.ShapeDtypeStruct((N_DEV, K, Nb), BF16)],
        compiler_params=_cparams(("arbitrary",)),
    )(order, u, shard)
    return z, full


N_EARLY_BLOCKS = 6


def _mm_gathered_tail(u, full, z, order, name, after=()):
    T, K = u.shape
    _, _, Nb = full.shape
    n_after = len(after)

    def body(order_ref, u_ref, w_ref, z_in_ref, *rest):
        rest[n_after][...] = jnp.dot(u_ref[...], w_ref[...], preferred_element_type=F32)

    return pl.pallas_call(
        body, name=name,
        grid_spec=pltpu.PrefetchScalarGridSpec(
            num_scalar_prefetch=1, grid=(N_DEV - N_EARLY_BLOCKS,),
            in_specs=[pl.BlockSpec((T, K), lambda s, order: (0, 0)),
                      pl.BlockSpec((None, K, Nb), lambda s, order: (order[N_EARLY_BLOCKS + s], 0, 0)), ANY]
            + [ANY] * n_after,
            out_specs=pl.BlockSpec((T, Nb), lambda s, order: (0, order[N_EARLY_BLOCKS + s]))),
        out_shape=jax.ShapeDtypeStruct(z.shape, z.dtype),
        input_output_aliases={3: 0},
        compiler_params=_cparams(("arbitrary",)),
    )(order, u, full, z, *after)


def _gather_order():
    x, y, c = _position()
    chips = [(1 - x, y), (x, 1 - y), (1 - x, 1 - y)]
    ids = [4 * x + 2 * y + c, 4 * x + 2 * y + (1 - c)]
    ids += [4 * cx + 2 * cy + c for cx, cy in chips[:2]] + [4 * cx + 2 * cy + (1 - c) for cx, cy in chips[:2]]
    ids += [4 * chips[2][0] + 2 * chips[2][1] + c, 4 * chips[2][0] + 2 * chips[2][1] + (1 - c)]
    return jnp.stack(ids).astype(jnp.int32)


HBM = pl.BlockSpec(memory_space=pltpu.HBM)
SEM = pl.BlockSpec(memory_space=pltpu.SEMAPHORE)
DATAFLOW = pltpu.SideEffectType.DATAFLOW_SIDE_EFFECTING


def _split_call(name, bufs, waits=(), starts=None, after=()):
    nb = len(bufs)
    n_new = starts[1] if starts else 0
    wait_sems = [s for w in waits for s in (*w[1], *w[2])]

    def body(*refs):
        b, pos = refs[:nb], nb
        for plan, ss, _, send_idx, recv_idx in waits:
            k = len(ss)
            copies = plan(b, refs[pos:pos + k], refs[pos + k:pos + 2 * k])
            pos += 2 * k
            for i in recv_idx:
                copies[i].wait_recv()
            for i in send_idx:
                copies[i].wait_send()
        outs = refs[pos + len(after):]
        if starts:
            for cp in starts[0](b, outs[nb:nb + n_new], outs[nb + n_new:nb + 2 * n_new]):
                cp.start()
        outs[-1][...] = jnp.zeros_like(outs[-1])

    res = pl.pallas_call(
        body, name=name,
        out_shape=tuple(pltpu.HBM(a.shape, a.dtype) for a in bufs) + (pltpu.SemaphoreType.DMA(()),) * (2 * n_new)
        + (jax.ShapeDtypeStruct((8, 128), F32),),
        in_specs=[HBM] * nb + [SEM] * len(wait_sems) + [ANY] * len(after),
        out_specs=(HBM,) * nb + (SEM,) * (2 * n_new) + (pl.BlockSpec(memory_space=pltpu.VMEM),),
        input_output_aliases={i: i for i in range(nb)},
        compiler_params=pltpu.CompilerParams(has_side_effects=DATAFLOW),
    )(*bufs, *wait_sems, *after)
    return list(res[:nb]), list(res[nb:nb + n_new]), list(res[nb + n_new:nb + 2 * n_new]), res[-1]


def _in_hbm(a):
    return pltpu.with_memory_space_constraint(a, pltpu.HBM)


def _remote(src, dst, send_sem, recv_sem, to):
    return pltpu.make_async_remote_copy(src_ref=src, dst_ref=dst, send_sem=send_sem, recv_sem=recv_sem,
                                        device_id=to, device_id_type=MESH)


def _other_chips():
    x, y, _ = _position()
    return [(1 - x, y), (x, 1 - y), (1 - x, 1 - y)]


def _plan_gather_first(n):
    def plan(b, ss, rs):
        x, y, c = _position()
        to = [(x, y, 1 - c)] + [(*chip, c) for chip in _other_chips()]
        return [_remote(b[w], b[n + w].at[4 * x + 2 * y + c], ss[4 * w + k], rs[4 * w + k], to[k])
                for w in range(n) for k in range(4)]
    return plan, 4 * n


def _plan_gather_pass(n):
    def plan(b, ss, rs):
        x, y, c = _position()
        copies = []
        for w in range(n):
            for j, chip in enumerate(_other_chips()):
                blk = b[n + w].at[4 * chip[0] + 2 * chip[1] + c]
                copies.append(_remote(blk, blk, ss[3 * w + j], rs[3 * w + j], (x, y, 1 - c)))
        return copies
    return plan, 3 * n


def _plan_sibling(n):
    def plan(b, ss, rs):
        x, y, c = _position()
        return [_remote(b[w].at[s], b[n + w].at[s], ss[4 * w + s], rs[4 * w + s], (x, y, 1 - c))
                for w in range(n) for s in range(N_CHIP)]
    return plan, 4 * n


def _plan_scatter(n):
    def plan(b, ss, rs):
        x, y, c = _position()
        return [_remote(b[w].at[2 * chip[0] + chip[1]], b[n + w].at[2 * x + y], ss[3 * w + j], rs[3 * w + j],
                        (*chip, c))
                for w in range(n) for j, chip in enumerate(_other_chips())]
    return plan, 3 * n


class _Gather:
    def __init__(self, shards, after, name):
        self.n, self.name = len(shards), name
        x, y, c = _position()
        placed = [lax.dynamic_update_index_in_dim(lax.empty((N_DEV,) + s.shape, s.dtype), s, 4 * x + 2 * y + c, 0)
                  for s in shards]
        bufs, self.ss, self.rs, self.token = _split_call(
            name + "_start", [_in_hbm(a) for a in list(shards) + placed], starts=_plan_gather_first(self.n),
            after=after)
        self.shards, self.fulls = bufs[:self.n], bufs[self.n:]
        self.passed = {}

    def _sub(self, ids, sems, per):
        return [sems[per * w + k] for w in ids for k in range(per)]

    def pass_on(self, ids, after, tag):
        m = len(ids)
        first = (_plan_gather_first(m)[0], self._sub(ids, self.ss, 4), self._sub(ids, self.rs, 4),
                 [], [4 * i + k for i in range(m) for k in (1, 2, 3)])
        bufs, ss, rs, token = _split_call(
            "%s_pass_%s" % (self.name, tag), [self.shards[w] for w in ids] + [self.fulls[w] for w in ids],
            waits=[first], starts=_plan_gather_pass(m), after=after)
        for i, w in enumerate(ids):
            self.shards[w], self.fulls[w] = bufs[i], bufs[m + i]
        self.passed[tuple(ids)] = (ss, rs)
        return token

    def finish(self, ids, after, tag):
        m = len(ids)
        ss2, rs2 = self.passed[tuple(ids)]
        first = (_plan_gather_first(m)[0], self._sub(ids, self.ss, 4), self._sub(ids, self.rs, 4),
                 list(range(4 * m)), [4 * i for i in range(m)])
        passed = (_plan_gather_pass(m)[0], ss2, rs2, list(range(3 * m)), list(range(3 * m)))
        bufs, _, _, _ = _split_call(
            "%s_finish_%s" % (self.name, tag), [self.shards[w] for w in ids] + [self.fulls[w] for w in ids],
            waits=[first, passed], after=after)
        return bufs[m:]


class _ReduceScatter:
    def __init__(self, others, name):
        self.n, self.name = len(others), name
        lands = [lax.empty(g.shape, g.dtype) for g in others]
        self.bufs, self.ss, self.rs, self.token = _split_call(
            name + "_sibling_start", [_in_hbm(a) for a in list(others) + lands], starts=_plan_sibling(self.n))

    def from_sibling(self, after):
        n = self.n
        bufs, _, _, _ = _split_call(
            self.name + "_sibling_wait", self.bufs,
            waits=[(_plan_sibling(n)[0], self.ss, self.rs, list(range(4 * n)), list(range(4 * n)))], after=after)
        return bufs[n:]

    def scatter(self, sums):
        lands = [lax.empty(s.shape, s.dtype) for s in sums]
        self.bufs, self.ss, self.rs, token = _split_call(
            self.name + "_scatter_start", [_in_hbm(a) for a in list(sums) + lands], starts=_plan_scatter(self.n))
        return token

    def finish(self, after):
        n = self.n
        bufs, _, _, _ = _split_call(
            self.name + "_scatter_wait", self.bufs,
            waits=[(_plan_scatter(n)[0], self.ss, self.rs, list(range(3 * n)), list(range(3 * n)))], after=after)
        return bufs[:n], bufs[n:]


class _Exchanges:
    def __init__(self, parity, order):
        self.parity, self.order = parity, order


def _gather_small(packed, name):
    R = packed.shape[0]

    def body(x_ref, out_ref, send_sems, recv_sems):
        x, y, c = _position()
        me = 4 * x + 2 * y + c
        out_ref[me] = x_ref[...]
        copies = []
        for k in range(1, N_DEV):
            to = (x ^ ((k >> 2) & 1), y ^ ((k >> 1) & 1), c ^ (k & 1))
            cp = pltpu.make_async_remote_copy(
                src_ref=x_ref, dst_ref=out_ref.at[me],
                send_sem=send_sems.at[k], recv_sem=recv_sems.at[k], device_id=to, device_id_type=MESH)
            cp.start()
            copies.append((k, to, cp))
        for k, to, cp in copies:
            cp.wait_send()
            pltpu.make_async_remote_copy(
                src_ref=x_ref, dst_ref=out_ref.at[4 * to[0] + 2 * to[1] + to[2]],
                send_sem=send_sems.at[k], recv_sem=recv_sems.at[k], device_id=to, device_id_type=MESH).wait_recv()

    return pl.pallas_call(
        body, name=name,
        in_specs=[pl.BlockSpec(memory_space=pltpu.VMEM)], out_specs=pl.BlockSpec(memory_space=pltpu.VMEM),
        out_shape=jax.ShapeDtypeStruct((N_DEV, R, 128), F32),
        scratch_shapes=[pltpu.SemaphoreType.DMA((N_DEV,)), pltpu.SemaphoreType.DMA((N_DEV,))],
    )(packed)


def _adamw_math(w, g, m, v):
    m = ADAM_B1 * m + (1.0 - ADAM_B1) * g
    v = ADAM_B2 * v + (1.0 - ADAM_B2) * (g * g)
    m_hat = m / (1.0 - ADAM_B1 ** ADAM_STEP)
    v_hat = v / (1.0 - ADAM_B2 ** ADAM_STEP)
    delta = -ADAM_LR * (m_hat / (jnp.sqrt(v_hat) + ADAM_EPS) + ADAM_WD * w)
    return delta, m, v


def _adamw_big_landed(w, m, v, parts, lands, slot, name, row0=0, into=None):
    R, C = w.shape
    rows = parts.shape[1]
    tr = _pick(rows, (256,))
    first = row0 // tr
    n_into = len(into) if into else 0

    def body(slot_ref, w_ref, m_ref, v_ref, own_ref, l1_ref, l2_ref, l3_ref, *rest):
        g = own_ref[...].astype(F32)
        for ref in (l1_ref, l2_ref, l3_ref):
            g = g + ref[...].astype(F32)
        for o_ref, res in zip(rest[n_into:], (g,) + _adamw_math(w_ref[...], g, m_ref[...], v_ref[...])):
            o_ref[...] = res

    blk = pl.BlockSpec((tr, C), lambda i, slot: (first + i, 0))

    def chip(k):
        return pl.BlockSpec((None, tr, C), lambda i, slot: ((slot[0] + k) % N_CHIP, i, 0))

    out = jax.ShapeDtypeStruct((R, C), F32)
    return pl.pallas_call(
        body, name=name,
        grid_spec=pltpu.PrefetchScalarGridSpec(
            num_scalar_prefetch=1, grid=(rows // tr,),
            in_specs=[blk, blk, blk, chip(0), chip(1), chip(2), chip(3)] + [ANY] * n_into,
            out_specs=[blk, blk, blk, blk]),
        out_shape=[out, out, out, out],
        input_output_aliases={8 + j: j for j in range(n_into)},
        compiler_params=_cparams(("parallel",)),
    )(slot, w, m, v, parts, lands, lands, lands, *(into or ()))


def _adamw_small(w, m, v, gathered, name):
    R = w.shape[0]

    def body(w_ref, m_ref, v_ref, p_ref, g_ref, d_ref, nm_ref, nv_ref):
        g = p_ref[0]
        for s in range(1, N_DEV):
            g = g + p_ref[s]
        d, nm, nv = _adamw_math(w_ref[...], g, m_ref[...], v_ref[...])
        g_ref[...] = g
        d_ref[...] = d
        nm_ref[...] = nm
        nv_ref[...] = nv

    out = jax.ShapeDtypeStruct((R, 128), F32)
    return pl.pallas_call(
        body, name=name, out_shape=[out, out, out, out],
    )(w, m, v, gathered)


SMALL_NAMES = ("lb_logits", "hg_norm_w", "rel_bias", "norm_mix_w", "norm_mlp_w", "norm_final_w")
SMALL_SHAPES = {"lb_logits": (2, HG_WIDTH), "hg_norm_w": (1, HG_DK), "rel_bias": (AT_HEADS, N_REL_PAD),
                "norm_mix_w": (1, D_MODEL), "norm_mlp_w": (1, D_MODEL), "norm_final_w": (1, D_MODEL)}


def _pack_small(parts):
    rows = []
    for nme in SMALL_NAMES:
        p = parts[nme]
        if nme == "rel_bias":
            p = jnp.pad(p, ((0, 0), (0, N_REL_PAD - N_REL)))
        rows.append(p.reshape(-1, 128))
    flat = jnp.concatenate(rows, axis=0)
    return jnp.pad(flat, ((0, SMALL_ROWS - flat.shape[0]), (0, 0)))


def _unpack_small(packed):
    out, at = {}, 0
    for nme in SMALL_NAMES:
        shp = SMALL_SHAPES[nme]
        nrow = shp[0] * shp[1] // 128
        p = packed[at:at + nrow].reshape(shp)
        at += nrow
        out[nme] = p[:, :N_REL] if nme == "rel_bias" else p
    return out


BIG_NAMES = ("w_in", "w_branch_a", "w_branch_b", "w_out", "w_up", "w_down")


def kernel(x, w_in, lb_logits, hg_norm_w, rel_bias, w_branch_a, w_branch_b, w_out, norm_mix_w, norm_mlp_w, w_up, w_down, norm_final_w, loss_target, m_w_in, m_lb_logits, m_hg_norm_w, m_rel_bias, m_w_branch_a, m_w_branch_b, m_w_out, m_norm_mix_w, m_norm_mlp_w, m_w_up, m_w_down, m_norm_final_w, v_w_in, v_lb_logits, v_hg_norm_w, v_rel_bias, v_w_branch_a, v_w_branch_b, v_w_out, v_norm_mix_w, v_norm_mlp_w, v_w_up, v_w_down, v_norm_final_w):
    big_w = [w_in[0], w_branch_a[0], w_branch_b[0], w_out[0], w_up[0], w_down[0]]
    big_m = [m_w_in[0], m_w_branch_a[0], m_w_branch_b[0], m_w_out[0], m_w_up[0], m_w_down[0]]
    big_v = [v_w_in[0], v_w_branch_a[0], v_w_branch_b[0], v_w_out[0], v_w_up[0], v_w_down[0]]

    shards = [w.astype(BF16) for w in big_w]
    parity = lax.axis_index("c").astype(jnp.int32).reshape(1)
    loss_part, grad_x, chip_parts, small = _local_step(
        x[0], loss_target[0], lb_logits, hg_norm_w, rel_bias[0], norm_mix_w, norm_mlp_w,
        norm_final_w.reshape(1, D_MODEL), shards[0], shards[1:], _Exchanges(parity, _gather_order()))
    loss = lax.psum(loss_part[0, 0], ("x", "y", "c"))
    (rs_in_lo, rs_in_hi), rs_mix, rs_up, rs_down = chip_parts
    slot =(2 * lax.axis_index("x") + lax.axis_index("y")).astype(jnp.int32).reshape(1)
    big = {}

    def finish(rs, names, after):
        sums, lands = rs.finish(after)
        for nme, own, land in zip(names, sums, lands):
            i = BIG_NAMES.index(nme)
            big[nme] = _adamw_big_landed(big_w[i], big_m[i], big_v[i], own, land, slot, "adamw_" + nme)
        return [big[nme][1] for nme in names]

    done = finish(rs_down, ["w_down"], [grad_x])
    done = finish(rs_up, ["w_up"], done)
    done = finish(rs_mix, ["w_branch_a", "w_branch_b", "w_out"], done)

    sw = dict(lb_logits=lb_logits, hg_norm_w=hg_norm_w, rel_bias=rel_bias[0], norm_mix_w=norm_mix_w,
              norm_mlp_w=norm_mlp_w, norm_final_w=norm_final_w.reshape(1, D_MODEL))
    sm = dict(lb_logits=m_lb_logits, hg_norm_w=m_hg_norm_w, rel_bias=m_rel_bias[0], norm_mix_w=m_norm_mix_w,
              norm_mlp_w=m_norm_mlp_w, norm_final_w=m_norm_final_w.reshape(1, D_MODEL))
    sv = dict(lb_logits=v_lb_logits, hg_norm_w=v_hg_norm_w, rel_bias=v_rel_bias[0], norm_mix_w=v_norm_mix_w,
              norm_mlp_w=v_norm_mlp_w, norm_final_w=v_norm_final_w.reshape(1, D_MODEL))
    gathered = _gather_small(_pack_small(small), "gather_small")
    small_packed = _adamw_small(_pack_small(sw), _pack_small(sm), _pack_small(sv), gathered, "adamw_small")
    small_out = [_unpack_small(p) for p in small_packed]

    (own,), (land,) = rs_in_lo.finish(done + [small_packed[0]])
    lo = _adamw_big_landed(big_w[0], big_m[0], big_v[0], own, land, slot, "adamw_w_in_lo")
    (own,), (land,) = rs_in_hi.finish([lo[1]])
    big["w_in"] = _adamw_big_landed(big_w[0], big_m[0], big_v[0], own, land, slot, "adamw_w_in_hi",
                                    row0=D_MODEL // 2, into=lo)

    def leaf(kind, nme):
        if nme in BIG_NAMES:
            return big[nme][kind][None]
        p = small_out[kind][nme]
        if nme == "rel_bias":
            return p[None]
        if nme == "norm_final_w":
            return p.reshape(D_MODEL)
        return p

    order = ("w_in", "lb_logits", "hg_norm_w", "rel_bias", "w_branch_a", "w_branch_b", "w_out", "norm_mix_w",
             "norm_mlp_w", "w_up", "w_down", "norm_final_w")
    outs = [loss, grad_x[None]]
    for kind in range(4):
        outs += [leaf(kind, nme) for nme in order]
    return tuple(outs)
```

```python
import jax
import jax.numpy as jnp
from jax import lax
from jax.experimental import pallas as pl
from jax.experimental.pallas import tpu as pltpu

F32 = jnp.float32
BF16 = jnp.bfloat16
HIGHEST = lax.Precision.HIGHEST
MESH = pl.DeviceIdType.MESH

D_MODEL = 2048
HG_HEADS = 8
HG_DK = 128
HG_WIDTH = 1024
AT_HEADS = 16
AT_DH = 64
AT_WIDTH = 1024
CHUNK = 64
LEFT_CHUNKS = 8
BAND = (LEFT_CHUNKS + 1) * CHUNK
PAD = LEFT_CHUNKS * CHUNK
REL_CLIP = 256
N_REL = 2 * REL_CLIP + 1
N_REL_PAD = 640
D_FF = 4 * D_MODEL
EPS = 1e-6
N_DEV = 8
N_CHIP = 4

ADAM_LR = 0.001
ADAM_B1 = 0.9
ADAM_B2 = 0.999
ADAM_EPS = 1e-08
ADAM_WD = 0.01
ADAM_STEP = 10

COL_HQ, COL_HF, COL_HI, COL_HG = 0, 8, 16, 24
COL_AQ, COL_AK, COL_AV = 32, 40, 48
COL_GATE_A, COL_GATE_B = 7, 9

VMEM_LIMIT = 56 * 1024 * 1024
SMALL_ROWS = 152


def _cparams(sem=None, **kw):
    if sem is not None:
        kw["dimension_semantics"] = sem
    return pltpu.CompilerParams(vmem_limit_bytes=VMEM_LIMIT, **kw)


def _pick(n, cands):
    for c in cands:
        if n % c == 0:
            return c
    return n


def _sigmoid(x):
    return 1.0 / (1.0 + jnp.exp(-x))


ANY = pl.BlockSpec(memory_space=pl.ANY)


def _position():
    return lax.axis_index("x"), lax.axis_index("y"), lax.axis_index("c")


def _call(body, args, *, name, grid, in_specs, out_specs, out_shape, scratch_shapes=(), sem=None, after=()):
    n_in = len(args)

    def ordered(*refs):
        body(*refs[:n_in], *refs[n_in + len(after):])

    return list(pl.pallas_call(
        ordered if after else body, name=name, grid=grid, in_specs=list(in_specs) + [ANY] * len(after),
        out_specs=out_specs, out_shape=out_shape, scratch_shapes=list(scratch_shapes),
        compiler_params=_cparams(sem))(*args, *after))


MAX_CONTRACTION_TILE = 4096


def _accumulate(part, acc_ref, step, n_steps, finish):
    if n_steps == 1:
        finish(part)
        return

    @pl.when(step == 0)
    def _():
        acc_ref[...] = part

    @pl.when(step > 0)
    def _():
        acc_ref[...] += part

    @pl.when(step == n_steps - 1)
    def _():
        finish(acc_ref[...])


def _mm_nn(a, wb, out_dtype, name, after=(), epilogue=None):
    M, K = a.shape
    NB, K2, Nb = wb.shape
    assert K == K2
    tm = min(M, 1024)
    tk = min(K, MAX_CONTRACTION_TILE)
    tn = _pick(Nb, (512, 1408, 256))
    nk = K // tk
    nn = Nb // tn
    extra, first_cols, out_dtypes, fn = epilogue or ((), (), (out_dtype,), lambda total: (total,))
    n_extra, n_out = len(extra), len(out_dtypes)

    def body(a_ref, b_ref, *rest):
        def finish(total):
            results = fn(total, *[r[...] for r in rest[:n_extra]])
            for o_ref, res, dt in zip(rest[n_extra:n_extra + n_out], results, out_dtypes):
                o_ref[...] = res.astype(dt)

        part = jnp.dot(a_ref[...], b_ref[...], preferred_element_type=F32)
        _accumulate(part, rest[-1], pl.program_id(3), nk, finish)

    def tile(first):
        return pl.BlockSpec((tm, tn), lambda m, j, n, k: (m, first + j * nn + n))

    outs = _call(
        body, (a, wb) + tuple(extra), name=name, grid=(M // tm, NB, nn, nk),
        in_specs=[pl.BlockSpec((tm, tk), lambda m, j, n, k: (m, k)),
                  pl.BlockSpec((None, tk, tn), lambda m, j, n, k: (j, k, n))] + [tile(col // tn) for col in first_cols],
        out_specs=[tile(0)] * n_out,
        out_shape=[jax.ShapeDtypeStruct((M, NB * Nb), dt) for dt in out_dtypes],
        scratch_shapes=[] if nk == 1 else [pltpu.VMEM((tm, tn), F32)],
        sem=("parallel", "parallel", "parallel", "arbitrary"), after=after)
    return outs if epilogue else outs[0]


def _squared_relu(a):
    ra = jnp.maximum(a, 0.0)
    return a, ra * ra


def _gated_merge(pb, za, zb, pa):
    return pb, _sigmoid(za) * pa + _sigmoid(zb) * pb


def _mm_nt(a, wb, out_dtype, name, after=(), epilogue=None):
    M, N = a.shape
    NB, K, Nb = wb.shape
    assert N == NB * Nb
    tm = min(M, 1024)
    n_tiles_live = 1 + (len(epilogue[0]) + len(epilogue[2]) if epilogue else 0)
    tko = _pick(K, (1024,)) if n_tiles_live <= 3 else _pick(K, (512,))
    tc = _pick(Nb, (2048, 1024, 1408, 256))
    nc = Nb // tc
    jb = max([d for d in (8, 4, 2, 1) if NB % d == 0 and d * tc <= MAX_CONTRACTION_TILE]) if nc == 1 else 1
    nsteps = (NB // jb) * nc
    extra, first_cols, out_dtypes, fn = epilogue or ((), (), (out_dtype,), lambda total: (total,))
    n_extra, n_out = len(extra), len(out_dtypes)

    def body(a_ref, b_ref, *rest):
        def finish(total):
            results = fn(total, *[r[...] for r in rest[:n_extra]])
            for o_ref, res, dt in zip(rest[n_extra:n_extra + n_out], results, out_dtypes):
                o_ref[...] = res.astype(dt)

        part = sum(lax.dot_general(a_ref[:, i * tc:(i + 1) * tc], b_ref[i], (((1,), (1,)), ((), ())),
                                   preferred_element_type=F32) for i in range(jb))
        _accumulate(part, rest[-1], pl.program_id(2) * nc + pl.program_id(3), nsteps, finish)

    def tile(first):
        return pl.BlockSpec((tm, tko), lambda m, ko, j, c: (m, first + ko))

    outs = _call(
        body, (a, wb) + tuple(extra), name=name,
        grid=(M // tm, K // tko, NB // jb, nc),
        in_specs=[pl.BlockSpec((tm, jb * tc), lambda m, ko, j, c: (m, j * nc + c)),
                  pl.BlockSpec((jb, tko, tc), lambda m, ko, j, c: (j, ko, c))] + [tile(col // tko) for col in first_cols],
        out_specs=[tile(0)] * n_out,
        out_shape=[jax.ShapeDtypeStruct((M, K), dt) for dt in out_dtypes],
        scratch_shapes=[] if nsteps == 1 else [pltpu.VMEM((tm, tko), F32)],
        sem=("parallel", "parallel", "arbitrary", "arbitrary"), after=after)
    return outs if epilogue else outs[0]


ROWS_TILE = 512
ROWS_PIECE = 128


def _mm_rows(a, w, extras, vectors, row_dtypes, fn, name):
    M, K = a.shape
    N = w.shape[1]
    tm = min(M, ROWS_TILE)
    n_e, n_v = len(extras), len(vectors)

    def body(a_ref, w_ref, *rest):
        tiles, vecs, outs, product_ref = rest[:n_e], rest[n_e:n_e + n_v], rest[n_e + n_v:-1], rest[-1]
        product_ref[...] = jnp.dot(a_ref[...], w_ref[...], preferred_element_type=F32)
        for i in range(tm // ROWS_PIECE):
            piece = slice(i * ROWS_PIECE, (i + 1) * ROWS_PIECE)
            results = fn(product_ref[piece, :], *[t[piece, :] for t in tiles], *[v[...] for v in vecs])
            for o_ref, res, dt in zip(outs, results, row_dtypes):
                o_ref[piece, :] = res.astype(dt)

    row = pl.BlockSpec((tm, N), lambda m: (m, 0))
    return _call(
        body, (a, w) + tuple(extras) + tuple(vectors), name=name, grid=(M // tm,),
        in_specs=[pl.BlockSpec((tm, K), lambda m: (m, 0)), pl.BlockSpec((K, N), lambda m: (0, 0))]
        + [row] * n_e + [pl.BlockSpec((1, N), lambda m: (0, 0))] * n_v,
        out_specs=[row] * len(row_dtypes),
        out_shape=[jax.ShapeDtypeStruct((M, N), dt) for dt in row_dtypes],
        scratch_shapes=[pltpu.VMEM((tm, N), F32)], sem=("parallel",))


def _rms(h, w):
    return h * lax.rsqrt(jnp.mean(h * h, axis=-1, keepdims=True) + EPS) * w


def _residual_rms_rows(mix, x, w):
    h = x + mix
    return h, _rms(h, w)


def _mm_tn_half(a, g, which, blocks_on, add, name, after=(), a_cols=None):
    M, Ka = a.shape
    N = g.shape[1]
    first_col = 0
    if a_cols is not None:
        first_col, Ka = a_cols
    if blocks_on == "g":
        rows, cols = _pick(Ka, (1024,)), N // N_DEV
        tn = _pick(cols, (512, 1408, 256))
        nn = cols // tn
        grid = (Ka // rows, N_CHIP, nn)
        a_spec = pl.BlockSpec((M, rows), lambda ka, s, n, w: (0, first_col // rows + ka))
        g_spec = pl.BlockSpec((M, tn), lambda ka, s, n, w: (0, (2 * s + w[0]) * nn + n))
        out_rows = Ka
    else:
        rows, cols = Ka // N_DEV, N
        tn = _pick(cols, (2048, 512))
        nn = cols // tn
        grid = (1, N_CHIP, nn)
        a_spec = pl.BlockSpec((M, rows), lambda ka, s, n, w: (0, 2 * s + w[0]))
        g_spec = pl.BlockSpec((M, tn), lambda ka, s, n, w: (0, n))
        out_rows = rows
    o_spec = pl.BlockSpec((None, rows, tn), lambda ka, s, n, w: (s, ka, n))
    n_add = 0 if add is None else 1

    def body(which_ref, a_ref, g_ref, *rest):
        acc = lax.dot_general(a_ref[...], g_ref[...], (((0,), (0,)), ((), ())), preferred_element_type=F32)
        if n_add:
            acc = acc + rest[0][...].astype(F32)
        rest[-1][...] = acc.astype(BF16)

    return pl.pallas_call(
        body, name=name,
        grid_spec=pltpu.PrefetchScalarGridSpec(
            num_scalar_prefetch=1, grid=grid,
            in_specs=[a_spec, g_spec] + [o_spec] * n_add + [ANY] * len(after),
            out_specs=o_spec),
        out_shape=jax.ShapeDtypeStruct((N_CHIP, out_rows, cols), BF16),
        compiler_params=_cparams(("parallel", "parallel", "parallel")),
    )(which, a, g, *(() if add is None else (add,)), *after)


ROW_TILE = 256


def _rms_fwd(x, w, name):
    T, Dm = x.shape

    def body(x_ref, w_ref, u_ref):
        xv = x_ref[...]
        r = lax.rsqrt(jnp.mean(xv * xv, axis=-1, keepdims=True) + EPS)
        u_ref[...] = (xv * r * w_ref[...]).astype(BF16)

    return pl.pallas_call(
        body, name=name, grid=(T // ROW_TILE,),
        in_specs=[pl.BlockSpec((ROW_TILE, Dm), lambda i: (i, 0)), pl.BlockSpec((1, Dm), lambda i: (0, 0))],
        out_specs=pl.BlockSpec((ROW_TILE, Dm), lambda i: (i, 0)),
        out_shape=jax.ShapeDtypeStruct((T, Dm), BF16),
        compiler_params=_cparams(("parallel",)),
    )(x, w)


def _loss_head(h1, mlp, wf, target, name):
    T, Dm = h1.shape

    def body(h_ref, m_ref, w_ref, t_ref, loss_ref, dh_ref, dhb_ref, dw_ref):
        i = pl.program_id(0)
        h = h_ref[...] + m_ref[...]
        r = lax.rsqrt(jnp.mean(h * h, axis=-1, keepdims=True) + EPS)
        xh = h * r
        wv = w_ref[...]
        e = xh * wv - t_ref[...]
        part = 0.5 * jnp.sum(jnp.mean(e * e, axis=-1, keepdims=True), axis=0, keepdims=True)
        dy = e * (1.0 / Dm)
        dw = jnp.sum(dy * xh, axis=0, keepdims=True)
        gy = dy * wv
        dh = r * (gy - xh * jnp.mean(gy * xh, axis=-1, keepdims=True))
        dh_ref[...] = dh
        dhb_ref[...] = dh.astype(BF16)

        @pl.when(i == 0)
        def _():
            loss_ref[...] = jnp.zeros_like(loss_ref)
            dw_ref[...] = jnp.zeros_like(dw_ref)

        loss_ref[...] += jnp.broadcast_to(part, loss_ref.shape)
        dw_ref[...] += dw

    row = pl.BlockSpec((ROW_TILE, Dm), lambda i: (i, 0))
    vec = pl.BlockSpec((1, Dm), lambda i: (0, 0))
    return pl.pallas_call(
        body, name=name, grid=(T // ROW_TILE,),
        in_specs=[row, row, vec, row],
        out_specs=[pl.BlockSpec((8, 128), lambda i: (0, 0)), row, row, vec],
        out_shape=[jax.ShapeDtypeStruct((8, 128), F32), jax.ShapeDtypeStruct((T, Dm), F32),
                   jax.ShapeDtypeStruct((T, Dm), BF16), jax.ShapeDtypeStruct((1, Dm), F32)],
        compiler_params=_cparams(("arbitrary",)),
    )(h1, mlp, wf, target)


def _rms_bwd(dyn, x, w, dres, dx_dtypes, name, after=()):
    T, Dm = x.shape
    n_dx = len(dx_dtypes)

    def body(g_ref, x_ref, w_ref, r_ref, *outs):
        i = pl.program_id(0)
        xv = x_ref[...]
        r = lax.rsqrt(jnp.mean(xv * xv, axis=-1, keepdims=True) + EPS)
        xh = xv * r
        g = g_ref[...]
        dw = jnp.sum(g * xh, axis=0, keepdims=True)
        gy = g * w_ref[...]
        dx = r_ref[...] + r * (gy - xh * jnp.mean(gy * xh, axis=-1, keepdims=True))
        for dx_ref, dt in zip(outs, dx_dtypes):
            dx_ref[...] = dx.astype(dt)
        dw_ref = outs[n_dx]

        @pl.when(i == 0)
        def _():
            dw_ref[...] = jnp.zeros_like(dw_ref)

        dw_ref[...] += dw

    row = pl.BlockSpec((ROW_TILE, Dm), lambda i: (i, 0))
    vec = pl.BlockSpec((1, Dm), lambda i: (0, 0))
    return _call(
        body, (dyn, x, w, dres), name=name, grid=(T // ROW_TILE,),
        in_specs=[row, row, vec, row],
        out_specs=[row] * n_dx + [vec],
        out_shape=[jax.ShapeDtypeStruct((T, Dm), dt) for dt in dx_dtypes] + [jax.ShapeDtypeStruct((1, Dm), F32)],
        sem=("arbitrary",), after=after)


GATE_TILE = 1024


def _merge_grads(d, za, zb, pa, pb):
    ga = _sigmoid(za)
    gb = _sigmoid(zb)
    return d * ga, d * gb, d * pa * ga * (1.0 - ga), d * pb * gb * (1.0 - gb)


def _dot_hi(a, b, dims):
    return lax.dot_general(a, b, (dims, ((), ())), precision=HIGHEST, preferred_element_type=F32)


NN = ((1,), (0,))
NT = ((1,), (1,))
TN = ((0,), (0,))


def _hg_gates(hq, hf, lb):
    sq = _sigmoid(hq)
    q = hq * sq * (HG_DK ** -0.5)
    f = _sigmoid(hf)
    g = lb + (1.0 - lb) * f
    return q, sq, f, g, jnp.log(g), 1.0 - g


def _tri(lower):
    r = lax.broadcasted_iota(jnp.int32, (CHUNK, CHUNK), 0)
    c = lax.broadcasted_iota(jnp.int32, (CHUNK, CHUNK), 1)
    return jnp.where((r >= c) if lower else (r <= c), 1.0, 0.0).astype(BF16)


def _running_sum(tri, x):
    return sum(jnp.dot(tri, piece, preferred_element_type=F32) for piece in _split3(x))


GROUP = 16
N_GROUPS = CHUNK // GROUP
BWD_CHUNKS_PER_TRIP = 4


def _dot_bf16(a, b, dims):
    return lax.dot_general(a.astype(BF16), b.astype(BF16), (dims, ((), ())), preferred_element_type=F32)


def _rows_iota():
    return lax.broadcasted_iota(jnp.int32, (CHUNK, HG_DK), 0)


def _by_query_group(q, kk, b, g):
    r0 = GROUP * g
    b0 = b[r0:r0 + 1]
    decay = jnp.exp(b[r0:r0 + GROUP] - b0)
    ks = jnp.where(_rows_iota() < r0, kk * jnp.exp(jnp.minimum(b0 - b, 0.0)), 0.0)
    return q[r0:r0 + GROUP] * decay, ks, decay


def _by_key_group(q, kk, b, j):
    r1 = GROUP * (j + 1)
    b1 = b[r1 - 1:r1]
    decay = jnp.exp(b1 - b[r1 - GROUP:r1])
    qs = jnp.where(_rows_iota() >= r1, q * jnp.exp(jnp.minimum(b - b1, 0.0)), 0.0)
    return qs, kk[r1 - GROUP:r1] * decay, decay


def _scores_between_groups(q, kk, b):
    blocks = [jnp.zeros((GROUP, CHUNK), F32)]
    for g in range(1, N_GROUPS):
        qs, ks, _ = _by_query_group(q, kk, b, g)
        blocks.append(_dot_bf16(qs, ks, NT))
    return jnp.concatenate(blocks, axis=0)


def _hgrn2_fwd(z, lb_logits, hg_norm_w, name, after=()):
    T = z.shape[0]
    n_chunks = T // CHUNK

    def body(hq_ref, hf_ref, hi_ref, hg_ref, lbl_ref, nw_ref, o_ref, ya_ref, sall_ref, st_ref):
        lbl = lbl_ref[...]
        lb = 1.0 / (1.0 + jnp.exp(lbl[1:2, :] - lbl[0:1, :]))
        st_ref[...] = jnp.zeros_like(st_ref)
        tri = _tri(True)
        row8 = lax.broadcasted_iota(jnp.int32, (8, HG_DK), 0)

        def chunk(c, carry):
            rows = pl.ds(pl.multiple_of(c * CHUNK, CHUNK), CHUNK)
            q, _, _, _, lg, kk = _hg_gates(hq_ref[rows, :], hf_ref[rows, :], lb)
            v = hi_ref[rows, :]
            b = _running_sum(tri, lg)
            st = st_ref[...]
            sall_ref[c] = st
            for grp in range(N_GROUPS):
                r0 = GROUP * grp
                for h8 in range(GROUP // 8):
                    n = 8 * (h8 + 1)
                    bs, ks, vs = b[r0:r0 + n], kk[r0:r0 + n], v[r0:r0 + n]
                    sidx = lax.broadcasted_iota(jnp.int32, (n, HG_DK), 0)
                    blk = jnp.zeros((8, HG_DK), F32)
                    for i in range(8):
                        t = r0 + 8 * h8 + i
                        e = jnp.where(sidx <= 8 * h8 + i, jnp.exp(b[t:t + 1] - bs), 0.0)
                        p = jnp.sum(e * ks * q[t:t + 1], axis=1, keepdims=True)
                        ot = jnp.sum(p * vs, axis=0, keepdims=True)
                        blk = blk + jnp.where(row8 == i, ot, 0.0)
                    o_ref[pl.ds(pl.multiple_of(c * CHUNK + r0 + 8 * h8, 8), 8), :] = blk
            o_ref[rows, :] += _dot_hi(q * jnp.exp(b), st, NT) + _dot_bf16(_scores_between_groups(q, kk, b), v, NN)
            bl = b[CHUNK - 1:CHUNK]
            ke = kk * jnp.exp(bl - b)
            st_ref[...] = st * jnp.exp(bl) + _dot_hi(v, ke, TN)
            return carry

        lax.fori_loop(0, n_chunks, chunk, 0, unroll=2)
        o = o_ref[...]
        r = lax.rsqrt(jnp.mean(o * o, axis=-1, keepdims=True) + EPS)
        hg = hg_ref[...]
        ya_ref[...] = (o * r * nw_ref[...] * (hg * _sigmoid(hg))).astype(BF16)

    def col(base):
        return pl.BlockSpec((T, HG_DK), lambda h: (0, base + h))

    return _call(
        body, (z, z, z, z, lb_logits, hg_norm_w), name=name, grid=(HG_HEADS,),
        in_specs=[col(COL_HQ), col(COL_HF), col(COL_HI), col(COL_HG),
                  pl.BlockSpec((2, HG_DK), lambda h: (0, h)), pl.BlockSpec((1, HG_DK), lambda h: (0, 0))],
        out_specs=[col(0), col(0), pl.BlockSpec((None, n_chunks, HG_DK, HG_DK), lambda h: (h, 0, 0, 0))],
        out_shape=[jax.ShapeDtypeStruct((T, HG_WIDTH), F32), jax.ShapeDtypeStruct((T, HG_WIDTH), BF16),
                   jax.ShapeDtypeStruct((HG_HEADS, n_chunks, HG_DK, HG_DK), F32)],
        scratch_shapes=[pltpu.VMEM((HG_DK, HG_DK), F32)],
        sem=("parallel",), after=after)


def _hgrn2_bwd(z, lb_logits, hg_norm_w, o_raw, s_all, dya, name, after=()):
    T = z.shape[0]
    n_chunks = T // CHUNK

    def body(hq_ref, hf_ref, hi_ref, hg_ref, lbl_ref, nw_ref, o_ref, sall_ref, dya_ref,
             dhq_ref, dhf_ref, dhi_ref, dhg_ref, dlbl_ref, dnw_ref,
             do_ref, dst_ref, dlb_ref, *per_chunk):
        h = pl.program_id(0)
        lbl = lbl_ref[...]
        lb = 1.0 / (1.0 + jnp.exp(lbl[1:2, :] - lbl[0:1, :]))

        o = o_ref[...]
        r = lax.rsqrt(jnp.mean(o * o, axis=-1, keepdims=True) + EPS)
        oh = o * r
        nw = nw_ref[...]
        hg = hg_ref[...]
        sg = _sigmoid(hg)
        dy = dya_ref[...]
        d_on = dy * (hg * sg)
        dhg_ref[...] = (dy * (oh * nw) * (sg * (1.0 + hg * (1.0 - sg)))).astype(BF16)
        dnw = jnp.sum(d_on * oh, axis=0, keepdims=True)
        gy = d_on * nw
        do_ref[...] = r * (gy - oh * jnp.mean(gy * oh, axis=-1, keepdims=True))

        @pl.when(h == 0)
        def _():
            dnw_ref[...] = jnp.zeros_like(dnw_ref)

        dnw_ref[...] += jnp.broadcast_to(dnw, dnw_ref.shape)

        dst_ref[...] = jnp.zeros_like(dst_ref)
        dlb_ref[...] = jnp.zeros_like(dlb_ref)
        tri = _tri(True)
        tri_t = _tri(False)
        row8 = lax.broadcasted_iota(jnp.int32, (8, HG_DK), 0)
        row_group = lax.broadcasted_iota(jnp.int32, (CHUNK, CHUNK), 0) // GROUP
        col_group = lax.broadcasted_iota(jnp.int32, (CHUNK, CHUNK), 1) // GROUP
        earlier_group = col_group < row_group
        later_group = col_group > row_group

        def chunk(c, dq_ref, dk_ref, dv_ref):
            rows = pl.ds(pl.multiple_of(c * CHUNK, CHUNK), CHUNK)
            hq = hq_ref[rows, :]
            q, sq, f, g, lg, kk = _hg_gates(hq, hf_ref[rows, :], lb)
            v = hi_ref[rows, :]
            do = do_ref[rows, :]
            b = _running_sum(tri, lg)
            eb = jnp.exp(b)
            bl = b[CHUNK - 1:CHUNK]
            ebl = jnp.exp(bl)
            ekb = jnp.exp(bl - b)
            qe = q * eb
            ke = kk * ekb
            st = sall_ref[c]
            dst = dst_ref[...]
            dqe = _dot_bf16(do, st, NN)
            dke = _dot_bf16(v, dst, NN)
            dv_inter = _dot_bf16(ke, dst, NT)
            d_ebl = jnp.sum(st * dst, axis=0, keepdims=True)
            dst_ref[...] = dst * ebl + _dot_bf16(do, qe, TN)

            dk_ref[...] = jnp.zeros_like(dk_ref)
            dv_ref[...] = jnp.zeros_like(dv_ref)
            for grp in range(N_GROUPS):
                r0 = GROUP * grp
                for h8 in range(GROUP // 8):
                    n = 8 * (h8 + 1)
                    bs, ks, vs = b[r0:r0 + n], kk[r0:r0 + n], v[r0:r0 + n]
                    sidx = lax.broadcasted_iota(jnp.int32, (n, HG_DK), 0)
                    blk = jnp.zeros((8, HG_DK), F32)
                    for i in range(8):
                        t = r0 + 8 * h8 + i
                        qt = q[t:t + 1]
                        dot_ = do[t:t + 1]
                        e = jnp.where(sidx <= 8 * h8 + i, jnp.exp(b[t:t + 1] - bs), 0.0)
                        w = e * ks
                        p = jnp.sum(w * qt, axis=1, keepdims=True)
                        dsc = jnp.sum(vs * dot_, axis=1, keepdims=True)
                        dqt = jnp.sum(dsc * w, axis=0, keepdims=True)
                        blk = blk + jnp.where(row8 == i, dqt, 0.0)
                        dk_ref[r0:r0 + n, :] += dsc * e * qt
                        dv_ref[r0:r0 + n, :] += p * dot_
                    dq_ref[r0 + 8 * h8:r0 + n, :] = blk
            ds_far = jnp.where(earlier_group, _dot_bf16(do, v, NT), 0.0)
            ds_far_t = jnp.where(later_group, _dot_bf16(v, do, NT), 0.0)
            dq_far, dk_far = [jnp.zeros((GROUP, HG_DK), F32)], []
            for grp in range(1, N_GROUPS):
                r0 = GROUP * grp
                _, ks, decay = _by_query_group(q, kk, b, grp)
                dq_far.append(decay * _dot_hi(ds_far[r0:r0 + GROUP], ks, NN))
                qs, _, decay = _by_key_group(q, kk, b, grp - 1)
                dk_far.append(decay * _dot_hi(ds_far_t[r0 - GROUP:r0], qs, NN))
            dk_far.append(jnp.zeros((GROUP, HG_DK), F32))
            dv_far = _dot_bf16(_scores_between_groups(q, kk, b), do, TN)
            dq_i = dq_ref[...] + jnp.concatenate(dq_far, axis=0)
            dk_i = dk_ref[...] + jnp.concatenate(dk_far, axis=0)
            dke_ke = dke * ke
            db = q * dq_i - kk * dk_i + dqe * qe - dke_ke
            db_last = jnp.sum(dke_ke, axis=0, keepdims=True) + d_ebl * ebl
            dlg = _running_sum(tri_t, db) + db_last
            dq = dq_i + dqe * eb
            dkk = dk_i + dke * ekb
            dg = dlg / g - dkk
            dhq_ref[rows, :] = (dq * (HG_DK ** -0.5) * (sq * (1.0 + hq * (1.0 - sq)))).astype(BF16)
            dhf_ref[rows, :] = (dg * (1.0 - lb) * f * (1.0 - f)).astype(BF16)
            dhi_ref[rows, :] = (dv_ref[...] + dv_far + dv_inter).astype(BF16)
            dlb_ref[...] += jnp.sum(dg * (1.0 - f), axis=0, keepdims=True)

        def trip(i, carry):
            for k in range(BWD_CHUNKS_PER_TRIP):
                chunk(n_chunks - 1 - k - BWD_CHUNKS_PER_TRIP * i, *per_chunk[3 * k:3 * k + 3])
            return carry

        lax.fori_loop(0, n_chunks // BWD_CHUNKS_PER_TRIP, trip, 0)
        dl0 = dlb_ref[...] * lb * (1.0 - lb)
        dlbl_ref[0:1, :] = dl0
        dlbl_ref[1:2, :] = -dl0

    def col(base):
        return pl.BlockSpec((T, HG_DK), lambda h: (0, base + h))

    outb = jax.ShapeDtypeStruct((T, HG_WIDTH), BF16)
    return _call(
        body, (z, z, z, z, lb_logits, hg_norm_w, o_raw, s_all, dya), name=name, grid=(HG_HEADS,),
        in_specs=[col(COL_HQ), col(COL_HF), col(COL_HI), col(COL_HG),
                  pl.BlockSpec((2, HG_DK), lambda h: (0, h)), pl.BlockSpec((1, HG_DK), lambda h: (0, 0)),
                  col(0), pl.BlockSpec((None, n_chunks, HG_DK, HG_DK), lambda h: (h, 0, 0, 0)), col(0)],
        out_specs=[col(0), col(0), col(0), col(0), pl.BlockSpec((2, HG_DK), lambda h: (0, h)),
                   pl.BlockSpec((8, HG_DK), lambda h: (0, 0))],
        out_shape=[outb, outb, outb, outb, jax.ShapeDtypeStruct((2, HG_WIDTH), F32),
                   jax.ShapeDtypeStruct((8, HG_DK), F32)],
        scratch_shapes=[pltpu.VMEM((T, HG_DK), F32), pltpu.VMEM((HG_DK, HG_DK), F32), pltpu.VMEM((1, HG_DK), F32)]
        + [pltpu.VMEM((CHUNK, HG_DK), F32)] * (3 * BWD_CHUNKS_PER_TRIP),
        sem=("arbitrary",), after=after)


CONST_KEYS = PAD - REL_CLIP
VAR_KEYS = BAND - CONST_KEYS
REL_LO = 128
REL_SPAN = N_REL_PAD - REL_LO


def _rel_onehot(t):
    r = lax.broadcasted_iota(jnp.int32, (REL_SPAN, VAR_KEYS), 0)
    j = lax.broadcasted_iota(jnp.int32, (REL_SPAN, VAR_KEYS), 1)
    idx = jnp.clip(t + PAD - CONST_KEYS - j, -REL_CLIP, REL_CLIP) + REL_CLIP - REL_LO
    return jnp.where(r == idx, 1.0, 0.0).astype(BF16)


def _split3(x):
    hi = x.astype(BF16)
    r1 = x - hi.astype(F32)
    mid = r1.astype(BF16)
    return hi, mid, (r1 - mid.astype(F32)).astype(BF16)


def _bias_expand(rel, name):
    def body(rel_ref, out_ref):
        tab = rel_ref[...]
        onehot = _rel_onehot(pl.program_id(0))
        out_ref[:, 0:CONST_KEYS] = jnp.broadcast_to(tab[:, 2 * REL_CLIP:2 * REL_CLIP + 1], (AT_HEADS, CONST_KEYS))
        out_ref[:, CONST_KEYS:BAND] = sum(
            jnp.dot(piece, onehot, preferred_element_type=F32) for piece in _split3(tab[:, REL_LO:N_REL_PAD]))

    return pl.pallas_call(
        body, name=name, grid=(CHUNK,),
        in_specs=[pl.BlockSpec((AT_HEADS, N_REL_PAD), lambda t: (0, 0))],
        out_specs=pl.BlockSpec((None, AT_HEADS, BAND), lambda t: (t, 0, 0)),
        out_shape=jax.ShapeDtypeStruct((CHUNK, AT_HEADS, BAND), F32),
        compiler_params=_cparams(("parallel",)),
    )(rel)


def _bias_reduce(dbias_rows, name, after=()):
    def body(db_ref, out_ref):
        lane = lax.broadcasted_iota(jnp.int32, (AT_HEADS, N_REL_PAD), 1)
        varying = lane >= CONST_KEYS
        by_offset = jnp.zeros((AT_HEADS, N_REL_PAD), F32)
        constant = jnp.zeros((AT_HEADS, N_REL_PAD), F32)
        for t in range(CHUNK):
            row = db_ref[t]
            constant = constant + jnp.where(varying, 0.0, row)
            moved = jnp.where(varying, row, 0.0)
            by_offset = by_offset + (pltpu.roll(moved, N_REL_PAD - t, axis=1) if t else moved)
        offset = lax.broadcasted_iota(jnp.int32, (N_REL_PAD, N_REL_PAD), 0)
        entry = lax.broadcasted_iota(jnp.int32, (N_REL_PAD, N_REL_PAD), 1)
        onehot = jnp.where(entry == jnp.clip(PAD - offset, -REL_CLIP, REL_CLIP) + REL_CLIP, 1.0, 0.0).astype(BF16)
        acc = sum(jnp.dot(piece, onehot, preferred_element_type=F32) for piece in _split3(by_offset))
        last = jnp.sum(constant, axis=1, keepdims=True)
        out_ref[...] = acc + jnp.where(lane == 2 * REL_CLIP, last, 0.0)

    whole = pl.BlockSpec((CHUNK, AT_HEADS, N_REL_PAD), lambda i: (0, 0, 0))
    return _call(
        body, (dbias_rows,), name=name, grid=(1,), in_specs=[whole],
        out_specs=[pl.BlockSpec((AT_HEADS, N_REL_PAD), lambda i: (0, 0))],
        out_shape=[jax.ShapeDtypeStruct((AT_HEADS, N_REL_PAD), F32)],
        sem=("arbitrary",), after=after)[0]


def _pair_lanes():
    return lax.broadcasted_iota(jnp.int32, (CHUNK, 2 * AT_DH), 1) < AT_DH


def _block_diag(a):
    first = _pair_lanes()
    return jnp.concatenate([jnp.where(first, a, 0.0), jnp.where(first, 0.0, a)], axis=0).astype(BF16)


def _diag_blocks(a):
    return jnp.where(_pair_lanes(), a[:CHUNK], a[CHUNK:])


def _band_probs_t(kb, qbd, bias_t, c):
    s = lax.dot_general(kb, qbd, (NT, ((), ())), preferred_element_type=F32) * (AT_DH ** -0.5) + bias_t
    j = lax.broadcasted_iota(jnp.int32, (BAND, 2 * AT_DH), 0)
    s = jnp.where(j + c * CHUNK >= PAD, s, -jnp.inf)
    p = jnp.exp(s - jnp.max(s, axis=0, keepdims=True))
    return p / jnp.sum(p, axis=0, keepdims=True)


def _attn_fwd(z, bias_t, name, after=()):
    T = z.shape[0]
    n_chunks = T // CHUNK

    def body(q_ref, k_ref, v_ref, bias_ref, y_ref, p_ref, *scratch):
        for pr in range(2):
            lanes = slice(128 * pr, 128 * (pr + 1))
            for dst_ref, src_ref in zip(scratch[2 * pr:2 * pr + 2], (k_ref, v_ref)):
                dst_ref[0:PAD, :] = jnp.zeros((PAD, 128), BF16)
                dst_ref[PAD:PAD + T, :] = src_ref[:, lanes].astype(BF16)

        def chunk(c, carry):
            rows = pl.ds(pl.multiple_of(c * CHUNK, CHUNK), CHUNK)
            band = pl.ds(pl.multiple_of(c * CHUNK, CHUNK), BAND)
            for pr in range(2):
                kp_ref, vp_ref = scratch[2 * pr:2 * pr + 2]
                lanes = slice(128 * pr, 128 * (pr + 1))
                p = _band_probs_t(kp_ref[band, :], _block_diag(q_ref[rows, lanes]), bias_ref[pr], c).astype(BF16)
                p_ref[pr, c] = p
                o2 = lax.dot_general(p, vp_ref[band, :], (TN, ((), ())), preferred_element_type=F32)
                y_ref[rows, lanes] = _diag_blocks(o2).astype(BF16)
            return carry

        lax.fori_loop(0, n_chunks, chunk, 0, unroll=2)

    def col(base):
        return pl.BlockSpec((T, 256), lambda h: (0, base // 2 + h))

    return _call(
        body, (z, z, z, bias_t), name=name, grid=(AT_HEADS // 4,),
        in_specs=[col(COL_AQ), col(COL_AK), col(COL_AV), pl.BlockSpec((2, BAND, 128), lambda h: (h, 0, 0))],
        out_specs=[col(0), pl.BlockSpec((2, n_chunks, BAND, 128), lambda h: (h, 0, 0, 0))],
        out_shape=[jax.ShapeDtypeStruct((T, AT_WIDTH), BF16),
                   jax.ShapeDtypeStruct((AT_HEADS // 2, n_chunks, BAND, 128), BF16)],
        scratch_shapes=[pltpu.VMEM((PAD + T, 128), BF16)] * 4,
        sem=("parallel",), after=after)


def _attn_bwd(z, probs, dyb, name, after=()):
    T = z.shape[0]
    n_chunks = T // CHUNK

    def body(q_ref, k_ref, v_ref, p_ref, dy_ref, dq_ref, dk_ref, dv_ref, dbias_ref, *scratch):
        dbias_ref[...] = jnp.zeros_like(dbias_ref)
        for pr in range(2):
            kp_ref, vp_ref, dkp_ref, dvp_ref = scratch[4 * pr:4 * pr + 4]
            lanes = slice(128 * pr, 128 * (pr + 1))
            kp_ref[0:PAD, :] = jnp.zeros((PAD, 128), BF16)
            vp_ref[0:PAD, :] = jnp.zeros((PAD, 128), BF16)
            kp_ref[PAD:PAD + T, :] = k_ref[:, lanes].astype(BF16)
            vp_ref[PAD:PAD + T, :] = v_ref[:, lanes].astype(BF16)
            dkp_ref[...] = jnp.zeros_like(dkp_ref)
            dvp_ref[...] = jnp.zeros_like(dvp_ref)

        def chunk(c, carry):
            rows = pl.ds(pl.multiple_of(c * CHUNK, CHUNK), CHUNK)
            band = pl.ds(pl.multiple_of(c * CHUNK, CHUNK), BAND)
            for pr in range(2):
                kp_ref, vp_ref, dkp_ref, dvp_ref = scratch[4 * pr:4 * pr + 4]
                lanes = slice(128 * pr, 128 * (pr + 1))
                qbd = _block_diag(q_ref[rows, lanes])
                dobd = _block_diag(dy_ref[rows, lanes])
                pb = p_ref[pr, c]
                p = pb.astype(F32)
                dp = lax.dot_general(vp_ref[band, :], dobd, (NT, ((), ())), preferred_element_type=F32)
                ds = p * (dp - jnp.sum(dp * p, axis=0, keepdims=True))
                dbias_ref[pr] += ds
                dsb = ds.astype(BF16)
                dq2 = lax.dot_general(dsb, kp_ref[band, :], (TN, ((), ())), preferred_element_type=F32)
                dq_ref[rows, lanes] = (_diag_blocks(dq2) * (AT_DH ** -0.5)).astype(BF16)
                dkp_ref[band, :] += jnp.dot(dsb, qbd, preferred_element_type=F32) * (AT_DH ** -0.5)
                dvp_ref[band, :] += jnp.dot(pb, dobd, preferred_element_type=F32)
            return carry

        lax.fori_loop(0, n_chunks, chunk, 0)
        for pr in range(2):
            lanes = slice(128 * pr, 128 * (pr + 1))
            dk_ref[:, lanes] = scratch[4 * pr + 2][PAD:PAD + T, :].astype(BF16)
            dv_ref[:, lanes] = scratch[4 * pr + 3][PAD:PAD + T, :].astype(BF16)

    def col(base):
        return pl.BlockSpec((T, 256), lambda h: (0, base // 2 + h))

    outb = jax.ShapeDtypeStruct((T, AT_WIDTH), BF16)
    return _call(
        body, (z, z, z, probs, dyb), name=name, grid=(AT_HEADS // 4,),
        in_specs=[col(COL_AQ), col(COL_AK), col(COL_AV),
                  pl.BlockSpec((2, n_chunks, BAND, 128), lambda h: (h, 0, 0, 0)), col(0)],
        out_specs=[col(0), col(0), col(0), pl.BlockSpec((2, BAND, 128), lambda h: (h, 0, 0))],
        out_shape=[outb, outb, outb, jax.ShapeDtypeStruct((AT_HEADS // 2, BAND, 128), F32)],
        scratch_shapes=([pltpu.VMEM((PAD + T, 128), BF16)] * 2 + [pltpu.VMEM((PAD + T, 128), F32)] * 2) * 2,
        sem=("parallel",), after=after)


def _local_step(x, target, lb_logits, hg_norm_w, rel_bias, norm_mix_w, norm_mlp_w, norm_final_w,
                w_in, rest, exchanges=None):
    ex = exchanges
    rel = jnp.pad(rel_bias, ((0, 0), (0, N_REL_PAD - N_REL)))

    u = _rms_fwd(x, norm_mix_w, "rms_mix_fwd")
    if ex:
        z, w_in = _mm_gathered(u, w_in, ex.order, "mm_in_fwd")
        gather = _Gather(rest[:3], [w_in], "ag")
        mlp_shards, _ = lax.optimization_barrier((rest[3:], gather.token))
        gather_mlp = _Gather([s.astype(BF16) for s in mlp_shards], [gather.token], "ag_mlp")
        z = _mm_gathered_tail(u, w_in, z, ex.order, "mm_in_fwd_tail", after=[gather_mlp.token])
        tok = []
    else:
        z = _mm_nn(u, w_in, F32, "mm_in_fwd")
        w_a, w_b, w_out, w_up, w_down = rest
        tok = []
    o_raw, y_a, s_all = _hgrn2_fwd(z, lb_logits, hg_norm_w, "hgrn2_fwd", after=tok)
    if ex:
        tok = [gather.pass_on([0, 1, 2], [o_raw], "abo")]
    bias_rows = _bias_expand(rel, "bias_expand")
    bias_t = jnp.transpose(bias_rows.reshape(CHUNK, AT_HEADS // 2, 2, BAND), (1, 3, 2, 0)).reshape(
        AT_HEADS // 2, BAND, 2 * CHUNK)
    y_b, probs = _attn_fwd(z, bias_t, "attn_fwd", after=tok)
    if ex:
        tok = [gather_mlp.pass_on([0], [y_b], "up")]
        w_a, w_b, w_out = gather.finish([0, 1, 2], tok, "abo")
    pa = _mm_nn(y_a, w_a, F32, "mm_a_fwd")
    pb, merged = _mm_nn(y_b, w_b, None, "mm_b_fwd", epilogue=(
        (z, z, pa), (COL_GATE_A * GATE_TILE, COL_GATE_B * GATE_TILE, 0), (F32, BF16), _gated_merge))
    w_out1 = w_out.reshape(1, D_MODEL, D_MODEL)
    h1, u2 = _mm_rows(merged, w_out.reshape(D_MODEL, D_MODEL), [x], [norm_mlp_w], (F32, BF16),
                      _residual_rms_rows, "mm_out_fwd")
    if ex:
        tok = [gather_mlp.pass_on([1], [u2], "down")]
        w_up, = gather_mlp.finish([0], tok, "up")
    a, r = _mm_nn(u2, w_up, None, "mm_up_fwd", epilogue=((), (), (F32, BF16), _squared_relu))
    if ex:
        w_down, = gather_mlp.finish([1], [r], "down")
    w_down1 = w_down.reshape(1, D_FF, D_MODEL)
    mlp = _mm_nn(r, w_down1, F32, "mm_down_fwd")
    loss, dh2, dh2b, g_nf = _loss_head(h1, mlp, norm_final_w, target, "loss_head")

    own = ex.parity if ex else jnp.zeros((1,), jnp.int32)

    def sibling_half(weights, name, after=()):
        others = [_mm_tn_half(a_, g_, 1 - own, on, None, nm + "_sibling", after, *cols)
                  for a_, g_, on, nm, *cols in weights]
        rs = _ReduceScatter(others, name) if ex else None
        return rs, others, ([rs.token] if ex else [])

    def own_half(rs, weights, others, after):
        landed = rs.from_sibling(after) if ex else [None] * len(weights)
        sums = [_mm_tn_half(a_, g_, own, on, l, nm + "_own", (), *cols)
                for (a_, g_, on, nm, *cols), l in zip(weights, landed)]
        if ex:
            return [rs.scatter(sums)], None
        return [], [jnp.stack([s_, o_], axis=1).reshape((N_DEV,) + s_.shape[1:]) for s_, o_ in zip(sums, others)]

    down = [(r, dh2b, "a", "mm_down_wgrad")]
    rs_down, others, tok = sibling_half(down, "rs_down")
    da, = _mm_nt(dh2b, w_down1, None, "mm_down_dgrad", after=tok, epilogue=(
        (a,), (0,), (BF16,), lambda dr, av: (dr * (2.0 * jnp.maximum(av, 0.0)),)))
    tok, g_down = own_half(rs_down, down, others, [da])
    up = [(u2, da, "g", "mm_up_wgrad")]
    rs_up, others, tok = sibling_half(up, "rs_up", tok)
    du2 = _mm_nt(da, w_up, F32, "mm_up_dgrad", after=tok)
    tok, g_up = own_half(rs_up, up, others, [du2])
    dh1, dh1b, g_nmlp = _rms_bwd(du2, h1, norm_mlp_w, dh2, (F32, BF16), "rms_mlp_bwd", after=tok)

    dpa, dpb, dga, dgb = _mm_nt(dh1b, w_out1, None, "mm_out_dgrad", epilogue=(
        (z, z, pa, pb), (COL_GATE_A * GATE_TILE, COL_GATE_B * GATE_TILE, 0, 0), (BF16,) * 4, _merge_grads))
    mix = [(y_a, dpa, "g", "mm_a_wgrad"), (y_b, dpb, "g", "mm_b_wgrad"), (merged, dh1b, "a", "mm_out_wgrad")]
    rs_mix, others, tok = sibling_half(mix, "rs_mix")
    dya = _mm_nt(dpa, w_a, F32, "mm_a_dgrad", after=tok)
    dyb = _mm_nt(dpb, w_b, F32, "mm_b_dgrad", after=tok)
    tok, g_mix = own_half(rs_mix, mix, others, [dya, dyb])
    daq, dak, dav, dbias_t = _attn_bwd(z, probs, dyb, "attn_bwd", after=tok)
    dhq, dhf, dhi, dhg, g_lbl, g_hgw = _hgrn2_bwd(z, lb_logits, hg_norm_w, o_raw, s_all, dya, "hgrn2_bwd",
                                                  after=tok)
    dbias_rows = jnp.pad(jnp.transpose(dbias_t.reshape(AT_HEADS // 2, BAND, 2, CHUNK), (3, 0, 2, 1)).reshape(
        CHUNK, AT_HEADS, BAND), ((0, 0), (0, 0), (0, N_REL_PAD - BAND)))
    dz = jnp.concatenate([dhq, dhf, dhi, dhg, daq, dak, dav, dga, dgb], axis=1)
    half = D_MODEL // 2
    lo = [(u, dz, "g", "mm_in_wgrad_lo", (0, half))]
    hi = [(u, dz, "g", "mm_in_wgrad_hi", (half, half))]
    rs_in_lo, others_lo, tok = sibling_half(lo, "rs_in_lo")
    rs_in_hi, others_hi, tok = sibling_half(hi, "rs_in_hi", tok)
    tok, g_in_lo = own_half(rs_in_lo, lo, others_lo, tok)
    du = _mm_nt(dz, w_in, F32, "mm_in_dgrad", after=tok)
    tok, g_in_hi = own_half(rs_in_hi, hi, others_hi, [du])
    grad_x, g_nmix = _rms_bwd(du, x, norm_mix_w, dh1, (F32,), "rms_mix_bwd", after=tok)
    g_rel = _bias_reduce(dbias_rows, "bias_reduce", after=tok)[:, :N_REL]

    small = dict(lb_logits=g_lbl, hg_norm_w=g_hgw[0:1], rel_bias=g_rel, norm_mix_w=g_nmix, norm_mlp_w=g_nmlp,
                 norm_final_w=g_nf)
    if ex:
        grads = [(rs_in_lo, rs_in_hi), rs_mix, rs_up, rs_down]
    else:
        grads = [jnp.concatenate([g_in_lo[0], g_in_hi[0]], axis=1)] + g_mix + [g_up[0], g_down[0]]
    return loss, grad_x, grads, small


def _mm_gathered(u, shard, order, name):
    T, K = u.shape
    _, Nb = shard.shape

    def body(order_ref, u_ref, shard_ref, z_ref, full_ref, wbuf, load_sem, send_sems, recv_sems, local_sem):
        s = pl.program_id(0)
        x, y, c = _position()
        me, sibling = (x, y, c), (x, y, 1 - c)
        chips = [(1 - x, y), (x, 1 - y), (1 - x, 1 - y)]

        def copy(k, block, to, src=None):
            dst = full_ref.at[4 * block[0] + 2 * block[1] + block[2]]
            return pltpu.make_async_remote_copy(
                src_ref=dst if src is None else src, dst_ref=dst,
                send_sem=send_sems.at[k], recv_sem=recv_sems.at[k], device_id=to, device_id_type=MESH)

        @pl.when(s == 0)
        def _():
            local = pltpu.make_async_copy(shard_ref, full_ref.at[4 * x + 2 * y + c], local_sem)
            local.start()
            copy(0, me, sibling, src=shard_ref).start()
            for j, chip in enumerate(chips):
                copy(1 + j, me, (*chip, c), src=shard_ref).start()
            local.wait()

        @pl.when(s == 1)
        def _():
            copy(0, sibling, me).wait_recv()

        for j, chip in enumerate(chips):
            direct, passed = ((2, 4), (3, 5), (6, 7))[j]

            @pl.when(s == direct)
            def _(j=j, chip=chip):
                copy(1 + j, (*chip, c), me).wait_recv()
                copy(4 + j, (*chip, c), sibling).start()

            @pl.when(s == passed)
            def _(j=j, chip=chip):
                copy(4 + j, (*chip, 1 - c), me).wait_recv()

        @pl.when(s < N_EARLY_BLOCKS)
        def _():
            load = pltpu.make_async_copy(full_ref.at[order_ref[s]], wbuf, load_sem)
            load.start()
            load.wait()
            z_ref[...] = jnp.dot(u_ref[...], wbuf[...], preferred_element_type=F32)

        @pl.when(s == N_DEV - 1)
        def _():
            for k in range(7):
                copy(k, me, sibling).wait_send()

    z, full = pl.pallas_call(
        body, name=name,
        grid_spec=pltpu.PrefetchScalarGridSpec(
            num_scalar_prefetch=1, grid=(N_DEV,),
            in_specs=[pl.BlockSpec((T, K), lambda s, order: (0, 0)), ANY],
            out_specs=[pl.BlockSpec((T, Nb), lambda s, order: (0, order[jnp.minimum(s, N_EARLY_BLOCKS - 1)])), ANY],
            scratch_shapes=[pltpu.VMEM((K, Nb), BF16), pltpu.SemaphoreType.DMA,
                            pltpu.SemaphoreType.DMA((7,)), pltpu.SemaphoreType.DMA((7,)), pltpu.SemaphoreType.DMA]),
        out_shape=[jax.ShapeDtypeStruct((T, N_DEV * Nb), F32), jax.ShapeDtypeStruct((N_DEV, K, Nb), BF16)],
        compiler_params=_cparams(("arbitrary",)),
    )(order, u, shard)
    return z, full


N_EARLY_BLOCKS = 6


def _mm_gathered_tail(u, full, z, order, name, after=()):
    T, K = u.shape
    _, _, Nb = full.shape
    n_after = len(after)

    def body(order_ref, u_ref, w_ref, z_in_ref, *rest):
        rest[n_after][...] = jnp.dot(u_ref[...], w_ref[...], preferred_element_type=F32)

    return pl.pallas_call(
        body, name=name,
        grid_spec=pltpu.PrefetchScalarGridSpec(
            num_scalar_prefetch=1, grid=(N_DEV - N_EARLY_BLOCKS,),
            in_specs=[pl.BlockSpec((T, K), lambda s, order: (0, 0)),
                      pl.BlockSpec((None, K, Nb), lambda s, order: (order[N_EARLY_BLOCKS + s], 0, 0)), ANY]
            + [ANY] * n_after,
            out_specs=pl.BlockSpec((T, Nb), lambda s, order: (0, order[N_EARLY_BLOCKS + s]))),
        out_shape=jax.ShapeDtypeStruct(z.shape, z.dtype),
        input_output_aliases={3: 0},
        compiler_params=_cparams(("arbitrary",)),
    )(order, u, full, z, *after)


def _gather_order():
    x, y, c = _position()
    chips = [(1 - x, y), (x, 1 - y), (1 - x, 1 - y)]
    ids = [4 * x + 2 * y + c, 4 * x + 2 * y + (1 - c)]
    ids += [4 * cx + 2 * cy + c for cx, cy in chips[:2]] + [4 * cx + 2 * cy + (1 - c) for cx, cy in chips[:2]]
    ids += [4 * chips[2][0] + 2 * chips[2][1] + c, 4 * chips[2][0] + 2 * chips[2][1] + (1 - c)]
    return jnp.stack(ids).astype(jnp.int32)


HBM = pl.BlockSpec(memory_space=pltpu.HBM)
SEM = pl.BlockSpec(memory_space=pltpu.SEMAPHORE)
DATAFLOW = pltpu.SideEffectType.DATAFLOW_SIDE_EFFECTING


def _split_call(name, bufs, waits=(), starts=None, after=()):
    nb = len(bufs)
    n_new = starts[1] if starts else 0
    wait_sems = [s for w in waits for s in (*w[1], *w[2])]

    def body(*refs):
        b, pos = refs[:nb], nb
        for plan, ss, _, send_idx, recv_idx in waits:
            k = len(ss)
            copies = plan(b, refs[pos:pos + k], refs[pos + k:pos + 2 * k])
            pos += 2 * k
            for i in recv_idx:
                copies[i].wait_recv()
            for i in send_idx:
                copies[i].wait_send()
        outs = refs[pos + len(after):]
        if starts:
            for cp in starts[0](b, outs[nb:nb + n_new], outs[nb + n_new:nb + 2 * n_new]):
                cp.start()
        outs[-1][...] = jnp.zeros_like(outs[-1])

    res = pl.pallas_call(
        body, name=name,
        out_shape=tuple(pltpu.HBM(a.shape, a.dtype) for a in bufs) + (pltpu.SemaphoreType.DMA(()),) * (2 * n_new)
        + (jax.ShapeDtypeStruct((8, 128), F32),),
        in_specs=[HBM] * nb + [SEM] * len(wait_sems) + [ANY] * len(after),
        out_specs=(HBM,) * nb + (SEM,) * (2 * n_new) + (pl.BlockSpec(memory_space=pltpu.VMEM),),
        input_output_aliases={i: i for i in range(nb)},
        compiler_params=pltpu.CompilerParams(has_side_effects=DATAFLOW),
    )(*bufs, *wait_sems, *after)
    return list(res[:nb]), list(res[nb:nb + n_new]), list(res[nb + n_new:nb + 2 * n_new]), res[-1]


def _in_hbm(a):
    return pltpu.with_memory_space_constraint(a, pltpu.HBM)


def _remote(src, dst, send_sem, recv_sem, to):
    return pltpu.make_async_remote_copy(src_ref=src, dst_ref=dst, send_sem=send_sem, recv_sem=recv_sem,
                                        device_id=to, device_id_type=MESH)


def _other_chips():
    x, y, _ = _position()
    return [(1 - x, y), (x, 1 - y), (1 - x, 1 - y)]


def _plan_gather_first(n):
    def plan(b, ss, rs):
        x, y, c = _position()
        to = [(x, y, 1 - c)] + [(*chip, c) for chip in _other_chips()]
        return [_remote(b[w], b[n + w].at[4 * x + 2 * y + c], ss[4 * w + k], rs[4 * w + k], to[k])
                for w in range(n) for k in range(4)]
    return plan, 4 * n


def _plan_gather_pass(n):
    def plan(b, ss, rs):
        x, y, c = _position()
        copies = []
        for w in range(n):
            for j, chip in enumerate(_other_chips()):
                blk = b[n + w].at[4 * chip[0] + 2 * chip[1] + c]
                copies.append(_remote(blk, blk, ss[3 * w + j], rs[3 * w + j], (x, y, 1 - c)))
        return copies
    return plan, 3 * n


def _plan_sibling(n):
    def plan(b, ss, rs):
        x, y, c = _position()
        return [_remote(b[w].at[s], b[n + w].at[s], ss[4 * w + s], rs[4 * w + s], (x, y, 1 - c))
                for w in range(n) for s in range(N_CHIP)]
    return plan, 4 * n


def _plan_scatter(n):
    def plan(b, ss, rs):
        x, y, c = _position()
        return [_remote(b[w].at[2 * chip[0] + chip[1]], b[n + w].at[2 * x + y], ss[3 * w + j], rs[3 * w + j],
                        (*chip, c))
                for w in range(n) for j, chip in enumerate(_other_chips())]
    return plan, 3 * n


class _Gather:
    def __init__(self, shards, after, name):
        self.n, self.name = len(shards), name
        x, y, c = _position()
        placed = [lax.dynamic_update_index_in_dim(lax.empty((N_DEV,) + s.shape, s.dtype), s, 4 * x + 2 * y + c, 0)
                  for s in shards]
        bufs, self.ss, self.rs, self.token = _split_call(
            name + "_start", [_in_hbm(a) for a in list(shards) + placed], starts=_plan_gather_first(self.n),
            after=after)
        self.shards, self.fulls = bufs[:self.n], bufs[self.n:]
        self.passed = {}

    def _sub(self, ids, sems, per):
        return [sems[per * w + k] for w in ids for k in range(per)]

    def pass_on(self, ids, after, tag):
        m = len(ids)
        first = (_plan_gather_first(m)[0], self._sub(ids, self.ss, 4), self._sub(ids, self.rs, 4),
                 [], [4 * i + k for i in range(m) for k in (1, 2, 3)])
        bufs, ss, rs, token = _split_call(
            "%s_pass_%s" % (self.name, tag), [self.shards[w] for w in ids] + [self.fulls[w] for w in ids],
            waits=[first], starts=_plan_gather_pass(m), after=after)
        for i, w in enumerate(ids):
            self.shards[w], self.fulls[w] = bufs[i], bufs[m + i]
        self.passed[tuple(ids)] = (ss, rs)
        return token

    def finish(self, ids, after, tag):
        m = len(ids)
        ss2, rs2 = self.passed[tuple(ids)]
        first = (_plan_gather_first(m)[0], self._sub(ids, self.ss, 4), self._sub(ids, self.rs, 4),
                 list(range(4 * m)), [4 * i for i in range(m)])
        passed = (_plan_gather_pass(m)[0], ss2, rs2, list(range(3 * m)), list(range(3 * m)))
        bufs, _, _, _ = _split_call(
            "%s_finish_%s" % (self.name, tag), [self.shards[w] for w in ids] + [self.fulls[w] for w in ids],
            waits=[first, passed], after=after)
        return bufs[m:]


class _ReduceScatter:
    def __init__(self, others, name):
        self.n, self.name = len(others), name
        lands = [lax.empty(g.shape, g.dtype) for g in others]
        self.bufs, self.ss, self.rs, self.token = _split_call(
            name + "_sibling_start", [_in_hbm(a) for a in list(others) + lands], starts=_plan_sibling(self.n))

    def from_sibling(self, after):
        n = self.n
        bufs, _, _, _ = _split_call(
            self.name + "_sibling_wait", self.bufs,
            waits=[(_plan_sibling(n)[0], self.ss, self.rs, list(range(4 * n)), list(range(4 * n)))], after=after)
        return bufs[n:]

    def scatter(self, sums):
        lands = [lax.empty(s.shape, s.dtype) for s in sums]
        self.bufs, self.ss, self.rs, token = _split_call(
            self.name + "_scatter_start", [_in_hbm(a) for a in list(sums) + lands], starts=_plan_scatter(self.n))
        return token

    def finish(self, after):
        n = self.n
        bufs, _, _, _ = _split_call(
            self.name + "_scatter_wait", self.bufs,
            waits=[(_plan_scatter(n)[0], self.ss, self.rs, list(range(3 * n)), list(range(3 * n)))], after=after)
        return bufs[:n], bufs[n:]


class _Exchanges:
    def __init__(self, parity, order):
        self.parity, self.order = parity, order


def _gather_small(packed, name):
    R = packed.shape[0]

    def body(x_ref, out_ref, send_sems, recv_sems):
        x, y, c = _position()
        me = 4 * x + 2 * y + c
        out_ref[me] = x_ref[...]
        copies = []
        for k in range(1, N_DEV):
            to = (x ^ ((k >> 2) & 1), y ^ ((k >> 1) & 1), c ^ (k & 1))
            cp = pltpu.make_async_remote_copy(
                src_ref=x_ref, dst_ref=out_ref.at[me],
                send_sem=send_sems.at[k], recv_sem=recv_sems.at[k], device_id=to, device_id_type=MESH)
            cp.start()
            copies.append((k, to, cp))
        for k, to, cp in copies:
            cp.wait_send()
            pltpu.make_async_remote_copy(
                src_ref=x_ref, dst_ref=out_ref.at[4 * to[0] + 2 * to[1] + to[2]],
                send_sem=send_sems.at[k], recv_sem=recv_sems.at[k], device_id=to, device_id_type=MESH).wait_recv()

    return pl.pallas_call(
        body, name=name,
        in_specs=[pl.BlockSpec(memory_space=pltpu.VMEM)], out_specs=pl.BlockSpec(memory_space=pltpu.VMEM),
        out_shape=jax.ShapeDtypeStruct((N_DEV, R, 128), F32),
        scratch_shapes=[pltpu.SemaphoreType.DMA((N_DEV,)), pltpu.SemaphoreType.DMA((N_DEV,))],
    )(packed)


def _adamw_math(w, g, m, v):
    m = ADAM_B1 * m + (1.0 - ADAM_B1) * g
    v = ADAM_B2 * v + (1.0 - ADAM_B2) * (g * g)
    m_hat = m / (1.0 - ADAM_B1 ** ADAM_STEP)
    v_hat = v / (1.0 - ADAM_B2 ** ADAM_STEP)
    delta = -ADAM_LR * (m_hat / (jnp.sqrt(v_hat) + ADAM_EPS) + ADAM_WD * w)
    return delta, m, v


def _adamw_big_landed(w, m, v, parts, lands, slot, name, row0=0, into=None):
    R, C = w.shape
    rows = parts.shape[1]
    tr = _pick(rows, (256,))
    first = row0 // tr
    n_into = len(into) if into else 0

    def body(slot_ref, w_ref, m_ref, v_ref, own_ref, l1_ref, l2_ref, l3_ref, *rest):
        g = own_ref[...].astype(F32)
        for ref in (l1_ref, l2_ref, l3_ref):
            g = g + ref[...].astype(F32)
        for o_ref, res in zip(rest[n_into:], (g,) + _adamw_math(w_ref[...], g, m_ref[...], v_ref[...])):
            o_ref[...] = res

    blk = pl.BlockSpec((tr, C), lambda i, slot: (first + i, 0))

    def chip(k):
        return pl.BlockSpec((None, tr, C), lambda i, slot: ((slot[0] + k) % N_CHIP, i, 0))

    out = jax.ShapeDtypeStruct((R, C), F32)
    return pl.pallas_call(
        body, name=name,
        grid_spec=pltpu.PrefetchScalarGridSpec(
            num_scalar_prefetch=1, grid=(rows // tr,),
            in_specs=[blk, blk, blk, chip(0), chip(1), chip(2), chip(3)] + [ANY] * n_into,
            out_specs=[blk, blk, blk, blk]),
        out_shape=[out, out, out, out],
        input_output_aliases={8 + j: j for j in range(n_into)},
        compiler_params=_cparams(("parallel",)),
    )(slot, w, m, v, parts, lands, lands, lands, *(into or ()))


def _adamw_small(w, m, v, gathered, name):
    R = w.shape[0]

    def body(w_ref, m_ref, v_ref, p_ref, g_ref, d_ref, nm_ref, nv_ref):
        g = p_ref[0]
        for s in range(1, N_DEV):
            g = g + p_ref[s]
        d, nm, nv = _adamw_math(w_ref[...], g, m_ref[...], v_ref[...])
        g_ref[...] = g
        d_ref[...] = d
        nm_ref[...] = nm
        nv_ref[...] = nv

    out = jax.ShapeDtypeStruct((R, 128), F32)
    return pl.pallas_call(
        body, name=name, out_shape=[out, out, out, out],
    )(w, m, v, gathered)


SMALL_NAMES = ("lb_logits", "hg_norm_w", "rel_bias", "norm_mix_w", "norm_mlp_w", "norm_final_w")
SMALL_SHAPES = {"lb_logits": (2, HG_WIDTH), "hg_norm_w": (1, HG_DK), "rel_bias": (AT_HEADS, N_REL_PAD),
                "norm_mix_w": (1, D_MODEL), "norm_mlp_w": (1, D_MODEL), "norm_final_w": (1, D_MODEL)}


def _pack_small(parts):
    rows = []
    for nme in SMALL_NAMES:
        p = parts[nme]
        if nme == "rel_bias":
            p = jnp.pad(p, ((0, 0), (0, N_REL_PAD - N_REL)))
        rows.append(p.reshape(-1, 128))
    flat = jnp.concatenate(rows, axis=0)
    return jnp.pad(flat, ((0, SMALL_ROWS - flat.shape[0]), (0, 0)))


def _unpack_small(packed):
    out, at = {}, 0
    for nme in SMALL_NAMES:
        shp = SMALL_SHAPES[nme]
        nrow = shp[0] * shp[1] // 128
        p = packed[at:at + nrow].reshape(shp)
        at += nrow
        out[nme] = p[:, :N_REL] if nme == "rel_bias" else p
    return out


BIG_NAMES = ("w_in", "w_branch_a", "w_branch_b", "w_out", "w_up", "w_down")


def kernel(x, w_in, lb_logits, hg_norm_w, rel_bias, w_branch_a, w_branch_b, w_out, norm_mix_w, norm_mlp_w, w_up, w_down, norm_final_w, loss_target, m_w_in, m_lb_logits, m_hg_norm_w, m_rel_bias, m_w_branch_a, m_w_branch_b, m_w_out, m_norm_mix_w, m_norm_mlp_w, m_w_up, m_w_down, m_norm_final_w, v_w_in, v_lb_logits, v_hg_norm_w, v_rel_bias, v_w_branch_a, v_w_branch_b, v_w_out, v_norm_mix_w, v_norm_mlp_w, v_w_up, v_w_down, v_norm_final_w):
    big_w = [w_in[0], w_branch_a[0], w_branch_b[0], w_out[0], w_up[0], w_down[0]]
    big_m = [m_w_in[0], m_w_branch_a[0], m_w_branch_b[0], m_w_out[0], m_w_up[0], m_w_down[0]]
    big_v = [v_w_in[0], v_w_branch_a[0], v_w_branch_b[0], v_w_out[0], v_w_up[0], v_w_down[0]]

    shards = [w.astype(BF16) for w in big_w[:4]] + big_w[4:]
    parity = lax.axis_index("c").astype(jnp.int32).reshape(1)
    loss_part, grad_x, chip_parts, small = _local_step(
        x[0], loss_target[0], lb_logits, hg_norm_w, rel_bias[0], norm_mix_w, norm_mlp_w,
        norm_final_w.reshape(1, D_MODEL), shards[0], shards[1:], _Exchanges(parity, _gather_order()))
    loss = lax.psum(loss_part[0, 0], ("x", "y", "c"))
    (rs_in_lo, rs_in_hi), rs_mix, rs_up, rs_down = chip_parts
    slot =(2 * lax.axis_index("x") + lax.axis_index("y")).astype(jnp.int32).reshape(1)
    big = {}

    def finish(rs, names, after):
        sums, lands = rs.finish(after)
        for nme, own, land in zip(names, sums, lands):
            i = BIG_NAMES.index(nme)
            big[nme] = _adamw_big_landed(big_w[i], big_m[i], big_v[i], own, land, slot, "adamw_" + nme)
        return [big[nme][1] for nme in names]

    done = finish(rs_down, ["w_down"], [grad_x])
    done = finish(rs_up, ["w_up"], done)
    done = finish(rs_mix, ["w_branch_a", "w_branch_b", "w_out"], done)

    sw = dict(lb_logits=lb_logits, hg_norm_w=hg_norm_w, rel_bias=rel_bias[0], norm_mix_w=norm_mix_w,
              norm_mlp_w=norm_mlp_w, norm_final_w=norm_final_w.reshape(1, D_MODEL))
    sm = dict(lb_logits=m_lb_logits, hg_norm_w=m_hg_norm_w, rel_bias=m_rel_bias[0], norm_mix_w=m_norm_mix_w,
              norm_mlp_w=m_norm_mlp_w, norm_final_w=m_norm_final_w.reshape(1, D_MODEL))
    sv = dict(lb_logits=v_lb_logits, hg_norm_w=v_hg_norm_w, rel_bias=v_rel_bias[0], norm_mix_w=v_norm_mix_w,
              norm_mlp_w=v_norm_mlp_w, norm_final_w=v_norm_final_w.reshape(1, D_MODEL))
    gathered = _gather_small(_pack_small(small), "gather_small")
    small_packed = _adamw_small(_pack_small(sw), _pack_small(sm), _pack_small(sv), gathered, "adamw_small")
    small_out = [_unpack_small(p) for p in small_packed]

    (own,), (land,) = rs_in_lo.finish(done + [small_packed[0]])
    lo = _adamw_big_landed(big_w[0], big_m[0], big_v[0], own, land, slot, "adamw_w_in_lo")
    (own,), (land,) = rs_in_hi.finish([lo[1]])
    big["w_in"] = _adamw_big_landed(big_w[0], big_m[0], big_v[0], own, land, slot, "adamw_w_in_hi",
                                    row0=D_MODEL // 2, into=lo)

    def leaf(kind, nme):
        if nme in BIG_NAMES:
            return big[nme][kind][None]
        p = small_out[kind][nme]
        if nme == "rel_bias":
            return p[None]
        if nme == "norm_final_w":
            return p.reshape(D_MODEL)
        return p

    order = ("w_in", "lb_logits", "hg_norm_w", "rel_bias", "w_branch_a", "w_branch_b", "w_out", "norm_mix_w",
             "norm_mlp_w", "w_up", "w_down", "norm_final_w")
    outs = [loss, grad_x[None]]
    for kind in range(4):
        outs += [leaf(kind, nme) for nme in order]
    return tuple(outs)
```

```python
import jax
import jax.numpy as jnp
from jax import lax
from jax.experimental import pallas as pl
from jax.experimental.pallas import tpu as pltpu

F32 = jnp.float32
BF16 = jnp.bfloat16
HIGHEST = lax.Precision.HIGHEST
MESH = pl.DeviceIdType.MESH

D_MODEL = 2048
HG_HEADS = 8
HG_DK = 128
HG_WIDTH = 1024
AT_HEADS = 16
AT_DH = 64
AT_WIDTH = 1024
CHUNK = 64
LEFT_CHUNKS = 8
BAND = (LEFT_CHUNKS + 1) * CHUNK
PAD = LEFT_CHUNKS * CHUNK
REL_CLIP = 256
N_REL = 2 * REL_CLIP + 1
N_REL_PAD = 640
D_FF = 4 * D_MODEL
EPS = 1e-6
N_DEV = 8
N_CHIP = 4

ADAM_LR = 0.001
ADAM_B1 = 0.9
ADAM_B2 = 0.999
ADAM_EPS = 1e-08
ADAM_WD = 0.01
ADAM_STEP = 10

COL_HQ, COL_HF, COL_HI, COL_HG = 0, 8, 16, 24
COL_AQ, COL_AK, COL_AV = 32, 40, 48
COL_GATE_A, COL_GATE_B = 7, 9

VMEM_LIMIT = 56 * 1024 * 1024
SMALL_ROWS = 152


def _cparams(sem=None, **kw):
    if sem is not None:
        kw["dimension_semantics"] = sem
    return pltpu.CompilerParams(vmem_limit_bytes=VMEM_LIMIT, **kw)


def _pick(n, cands):
    for c in cands:
        if n % c == 0:
            return c
    return n


def _sigmoid(x):
    return 1.0 / (1.0 + jnp.exp(-x))


ANY = pl.BlockSpec(memory_space=pl.ANY)


def _position():
    return lax.axis_index("x"), lax.axis_index("y"), lax.axis_index("c")


def _call(body, args, *, name, grid, in_specs, out_specs, out_shape, scratch_shapes=(), sem=None, after=(),
          aliases=None):
    n_in = len(args)

    def ordered(*refs):
        body(*refs[:n_in], *refs[n_in + len(after):])

    return list(pl.pallas_call(
        ordered if after else body, name=name, grid=grid, in_specs=list(in_specs) + [ANY] * len(after),
        out_specs=out_specs, out_shape=out_shape, scratch_shapes=list(scratch_shapes),
        input_output_aliases=aliases or {}, compiler_params=_cparams(sem))(*args, *after))


MAX_CONTRACTION_TILE = 4096


def _accumulate(part, acc_ref, step, n_steps, finish):
    if n_steps == 1:
        finish(part)
        return

    @pl.when(step == 0)
    def _():
        acc_ref[...] = part

    @pl.when(step > 0)
    def _():
        acc_ref[...] += part

    @pl.when(step == n_steps - 1)
    def _():
        finish(acc_ref[...])


def _mm_nn(a, wb, out_dtype, name, after=(), epilogue=None, blocks=None, into=()):
    M, K = a.shape
    NB, K2, Nb = wb.shape
    assert K == K2
    j0, nj = blocks or (0, NB)
    n_into = len(into)
    tm = min(M, 1024)
    tk = min(K, MAX_CONTRACTION_TILE)
    tn = _pick(Nb, (512, 1408, 256))
    nk = K // tk
    nn = Nb // tn
    extra, first_cols, out_dtypes, fn = epilogue or ((), (), (out_dtype,), lambda total: (total,))
    n_extra, n_out = len(extra), len(out_dtypes)

    def body(a_ref, b_ref, *rest):
        def finish(total):
            results = fn(total, *[r[...] for r in rest[:n_extra]])
            for o_ref, res, dt in zip(rest[n_extra + n_into:n_extra + n_into + n_out], results, out_dtypes):
                o_ref[...] = res.astype(dt)

        part = jnp.dot(a_ref[...], b_ref[...], preferred_element_type=F32)
        _accumulate(part, rest[-1], pl.program_id(3), nk, finish)

    def tile(first):
        return pl.BlockSpec((tm, tn), lambda m, j, n, k: (m, first + (j0 + j) * nn + n))

    outs = _call(
        body, (a, wb) + tuple(extra) + tuple(into), name=name, grid=(M // tm, nj, nn, nk),
        in_specs=[pl.BlockSpec((tm, tk), lambda m, j, n, k: (m, k)),
                  pl.BlockSpec((None, tk, tn), lambda m, j, n, k: (j0 + j, k, n))]
        + [tile(col // tn) for col in first_cols] + [ANY] * n_into,
        out_specs=[tile(0)] * n_out,
        out_shape=[jax.ShapeDtypeStruct((M, NB * Nb), dt) for dt in out_dtypes],
        scratch_shapes=[] if nk == 1 else [pltpu.VMEM((tm, tn), F32)],
        sem=("parallel", "parallel", "parallel", "arbitrary"), after=after,
        aliases={2 + n_extra + i: i for i in range(n_into)})
    return outs if epilogue else outs[0]


def _squared_relu(a):
    ra = jnp.maximum(a, 0.0)
    return a, ra * ra


def _gated_merge(pb, za, zb, pa):
    return pb, _sigmoid(za) * pa + _sigmoid(zb) * pb


def _mm_nt(a, wb, out_dtype, name, after=(), epilogue=None):
    M, N = a.shape
    NB, K, Nb = wb.shape
    assert N == NB * Nb
    tm = min(M, 1024)
    n_tiles_live = 1 + (len(epilogue[0]) + len(epilogue[2]) if epilogue else 0)
    tko = _pick(K, (1024,)) if n_tiles_live <= 3 else _pick(K, (512,))
    tc = _pick(Nb, (2048, 1024, 1408, 256))
    nc = Nb // tc
    jb = max([d for d in (8, 4, 2, 1) if NB % d == 0 and d * tc <= MAX_CONTRACTION_TILE]) if nc == 1 else 1
    nsteps = (NB // jb) * nc
    extra, first_cols, out_dtypes, fn = epilogue or ((), (), (out_dtype,), lambda total: (total,))
    n_extra, n_out = len(extra), len(out_dtypes)

    def body(a_ref, b_ref, *rest):
        def finish(total):
            results = fn(total, *[r[...] for r in rest[:n_extra]])
            for o_ref, res, dt in zip(rest[n_extra:n_extra + n_out], results, out_dtypes):
                o_ref[...] = res.astype(dt)

        part = sum(lax.dot_general(a_ref[:, i * tc:(i + 1) * tc], b_ref[i], (((1,), (1,)), ((), ())),
                                   preferred_element_type=F32) for i in range(jb))
        _accumulate(part, rest[-1], pl.program_id(2) * nc + pl.program_id(3), nsteps, finish)

    def tile(first):
        return pl.BlockSpec((tm, tko), lambda m, ko, j, c: (m, first + ko))

    outs = _call(
        body, (a, wb) + tuple(extra), name=name,
        grid=(M // tm, K // tko, NB // jb, nc),
        in_specs=[pl.BlockSpec((tm, jb * tc), lambda m, ko, j, c: (m, j * nc + c)),
                  pl.BlockSpec((jb, tko, tc), lambda m, ko, j, c: (j, ko, c))] + [tile(col // tko) for col in first_cols],
        out_specs=[tile(0)] * n_out,
        out_shape=[jax.ShapeDtypeStruct((M, K), dt) for dt in out_dtypes],
        scratch_shapes=[] if nsteps == 1 else [pltpu.VMEM((tm, tko), F32)],
        sem=("parallel", "parallel", "arbitrary", "arbitrary"), after=after)
    return outs if epilogue else outs[0]


ROWS_TILE = 512
ROWS_PIECE = 128


def _mm_rows(a, w, extras, vectors, row_dtypes, fn, name):
    M, K = a.shape
    N = w.shape[1]
    tm = min(M, ROWS_TILE)
    n_e, n_v = len(extras), len(vectors)

    def body(a_ref, w_ref, *rest):
        tiles, vecs, outs, product_ref = rest[:n_e], rest[n_e:n_e + n_v], rest[n_e + n_v:-1], rest[-1]
        product_ref[...] = jnp.dot(a_ref[...], w_ref[...], preferred_element_type=F32)
        for i in range(tm // ROWS_PIECE):
            piece = slice(i * ROWS_PIECE, (i + 1) * ROWS_PIECE)
            results = fn(product_ref[piece, :], *[t[piece, :] for t in tiles], *[v[...] for v in vecs])
            for o_ref, res, dt in zip(outs, results, row_dtypes):
                o_ref[piece, :] = res.astype(dt)

    row = pl.BlockSpec((tm, N), lambda m: (m, 0))
    return _call(
        body, (a, w) + tuple(extras) + tuple(vectors), name=name, grid=(M // tm,),
        in_specs=[pl.BlockSpec((tm, K), lambda m: (m, 0)), pl.BlockSpec((K, N), lambda m: (0, 0))]
        + [row] * n_e + [pl.BlockSpec((1, N), lambda m: (0, 0))] * n_v,
        out_specs=[row] * len(row_dtypes),
        out_shape=[jax.ShapeDtypeStruct((M, N), dt) for dt in row_dtypes],
        scratch_shapes=[pltpu.VMEM((tm, N), F32)], sem=("parallel",))


def _rms(h, w):
    return h * lax.rsqrt(jnp.mean(h * h, axis=-1, keepdims=True) + EPS) * w


def _residual_rms_rows(mix, x, w):
    h = x + mix
    return h, _rms(h, w)


def _mm_tn_half(a, g, which, blocks_on, add, name, after=(), a_cols=None):
    M, Ka = a.shape
    N = g.shape[1]
    first_col = 0
    if a_cols is not None:
        first_col, Ka = a_cols
    if blocks_on == "g":
        rows, cols = _pick(Ka, (1024,)), N // N_DEV
        tn = _pick(cols, (512, 1408, 256))
        nn = cols // tn
        grid = (Ka // rows, N_CHIP, nn)
        a_spec = pl.BlockSpec((M, rows), lambda ka, s, n, w: (0, first_col // rows + ka))
        g_spec = pl.BlockSpec((M, tn), lambda ka, s, n, w: (0, (2 * s + w[0]) * nn + n))
        out_rows = Ka
    else:
        rows, cols = Ka // N_DEV, N
        tn = _pick(cols, (2048, 512))
        nn = cols // tn
        grid = (1, N_CHIP, nn)
        a_spec = pl.BlockSpec((M, rows), lambda ka, s, n, w: (0, 2 * s + w[0]))
        g_spec = pl.BlockSpec((M, tn), lambda ka, s, n, w: (0, n))
        out_rows = rows
    o_spec = pl.BlockSpec((None, rows, tn), lambda ka, s, n, w: (s, ka, n))
    n_add = 0 if add is None else 1

    def body(which_ref, a_ref, g_ref, *rest):
        acc = lax.dot_general(a_ref[...], g_ref[...], (((0,), (0,)), ((), ())), preferred_element_type=F32)
        if n_add:
            acc = acc + rest[0][...].astype(F32)
        rest[-1][...] = acc.astype(BF16)

    return pl.pallas_call(
        body, name=name,
        grid_spec=pltpu.PrefetchScalarGridSpec(
            num_scalar_prefetch=1, grid=grid,
            in_specs=[a_spec, g_spec] + [o_spec] * n_add + [ANY] * len(after),
            out_specs=o_spec),
        out_shape=jax.ShapeDtypeStruct((N_CHIP, out_rows, cols), BF16),
        compiler_params=_cparams(("parallel", "parallel", "parallel")),
    )(which, a, g, *(() if add is None else (add,)), *after)


ROW_TILE = 256


def _rms_fwd(x, w, name):
    T, Dm = x.shape

    def body(x_ref, w_ref, u_ref):
        xv = x_ref[...]
        r = lax.rsqrt(jnp.mean(xv * xv, axis=-1, keepdims=True) + EPS)
        u_ref[...] = (xv * r * w_ref[...]).astype(BF16)

    return pl.pallas_call(
        body, name=name, grid=(T // ROW_TILE,),
        in_specs=[pl.BlockSpec((ROW_TILE, Dm), lambda i: (i, 0)), pl.BlockSpec((1, Dm), lambda i: (0, 0))],
        out_specs=pl.BlockSpec((ROW_TILE, Dm), lambda i: (i, 0)),
        out_shape=jax.ShapeDtypeStruct((T, Dm), BF16),
        compiler_params=_cparams(("parallel",)),
    )(x, w)


def _loss_head(h1, mlp, wf, target, name):
    T, Dm = h1.shape

    def body(h_ref, m_ref, w_ref, t_ref, loss_ref, dh_ref, dhb_ref, dw_ref):
        i = pl.program_id(0)
        h = h_ref[...] + m_ref[...]
        r = lax.rsqrt(jnp.mean(h * h, axis=-1, keepdims=True) + EPS)
        xh = h * r
        wv = w_ref[...]
        e = xh * wv - t_ref[...]
        part = 0.5 * jnp.sum(jnp.mean(e * e, axis=-1, keepdims=True), axis=0, keepdims=True)
        dy = e * (1.0 / Dm)
        dw = jnp.sum(dy * xh, axis=0, keepdims=True)
        gy = dy * wv
        dh = r * (gy - xh * jnp.mean(gy * xh, axis=-1, keepdims=True))
        dh_ref[...] = dh
        dhb_ref[...] = dh.astype(BF16)

        @pl.when(i == 0)
        def _():
            loss_ref[...] = jnp.zeros_like(loss_ref)
            dw_ref[...] = jnp.zeros_like(dw_ref)

        loss_ref[...] += jnp.broadcast_to(part, loss_ref.shape)
        dw_ref[...] += dw

    row = pl.BlockSpec((ROW_TILE, Dm), lambda i: (i, 0))
    vec = pl.BlockSpec((1, Dm), lambda i: (0, 0))
    return pl.pallas_call(
        body, name=name, grid=(T // ROW_TILE,),
        in_specs=[row, row, vec, row],
        out_specs=[pl.BlockSpec((8, 128), lambda i: (0, 0)), row, row, vec],
        out_shape=[jax.ShapeDtypeStruct((8, 128), F32), jax.ShapeDtypeStruct((T, Dm), F32),
                   jax.ShapeDtypeStruct((T, Dm), BF16), jax.ShapeDtypeStruct((1, Dm), F32)],
        compiler_params=_cparams(("arbitrary",)),
    )(h1, mlp, wf, target)


def _rms_bwd(dyn, x, w, dres, dx_dtypes, name, after=()):
    T, Dm = x.shape
    n_dx = len(dx_dtypes)

    def body(g_ref, x_ref, w_ref, r_ref, *outs):
        i = pl.program_id(0)
        xv = x_ref[...]
        r = lax.rsqrt(jnp.mean(xv * xv, axis=-1, keepdims=True) + EPS)
        xh = xv * r
        g = g_ref[...]
        dw = jnp.sum(g * xh, axis=0, keepdims=True)
        gy = g * w_ref[...]
        dx = r_ref[...] + r * (gy - xh * jnp.mean(gy * xh, axis=-1, keepdims=True))
        for dx_ref, dt in zip(outs, dx_dtypes):
            dx_ref[...] = dx.astype(dt)
        dw_ref = outs[n_dx]

        @pl.when(i == 0)
        def _():
            dw_ref[...] = jnp.zeros_like(dw_ref)

        dw_ref[...] += dw

    row = pl.BlockSpec((ROW_TILE, Dm), lambda i: (i, 0))
    vec = pl.BlockSpec((1, Dm), lambda i: (0, 0))
    return _call(
        body, (dyn, x, w, dres), name=name, grid=(T // ROW_TILE,),
        in_specs=[row, row, vec, row],
        out_specs=[row] * n_dx + [vec],
        out_shape=[jax.ShapeDtypeStruct((T, Dm), dt) for dt in dx_dtypes] + [jax.ShapeDtypeStruct((1, Dm), F32)],
        sem=("arbitrary",), after=after)


GATE_TILE = 1024


def _merge_grads(d, za, zb, pa, pb):
    ga = _sigmoid(za)
    gb = _sigmoid(zb)
    return d * ga, d * gb, d * pa * ga * (1.0 - ga), d * pb * gb * (1.0 - gb)


def _dot_hi(a, b, dims):
    return lax.dot_general(a, b, (dims, ((), ())), precision=HIGHEST, preferred_element_type=F32)


NN = ((1,), (0,))
NT = ((1,), (1,))
TN = ((0,), (0,))


def _hg_gates(hq, hf, lb):
    sq = _sigmoid(hq)
    q = hq * sq * (HG_DK ** -0.5)
    f = _sigmoid(hf)
    g = lb + (1.0 - lb) * f
    return q, sq, f, g, jnp.log(g), 1.0 - g


def _tri(lower):
    r = lax.broadcasted_iota(jnp.int32, (CHUNK, CHUNK), 0)
    c = lax.broadcasted_iota(jnp.int32, (CHUNK, CHUNK), 1)
    return jnp.where((r >= c) if lower else (r <= c), 1.0, 0.0).astype(BF16)


def _running_sum(tri, x):
    return sum(jnp.dot(tri, piece, preferred_element_type=F32) for piece in _split3(x))


GROUP = 16
N_GROUPS = CHUNK // GROUP
BWD_CHUNKS_PER_TRIP = 4


def _dot_bf16(a, b, dims):
    return lax.dot_general(a.astype(BF16), b.astype(BF16), (dims, ((), ())), preferred_element_type=F32)


def _rows_iota():
    return lax.broadcasted_iota(jnp.int32, (CHUNK, HG_DK), 0)


def _by_query_group(q, kk, b, g):
    r0 = GROUP * g
    b0 = b[r0:r0 + 1]
    decay = jnp.exp(b[r0:r0 + GROUP] - b0)
    ks = jnp.where(_rows_iota() < r0, kk * jnp.exp(jnp.minimum(b0 - b, 0.0)), 0.0)
    return q[r0:r0 + GROUP] * decay, ks, decay


def _by_key_group(q, kk, b, j):
    r1 = GROUP * (j + 1)
    b1 = b[r1 - 1:r1]
    decay = jnp.exp(b1 - b[r1 - GROUP:r1])
    qs = jnp.where(_rows_iota() >= r1, q * jnp.exp(jnp.minimum(b - b1, 0.0)), 0.0)
    return qs, kk[r1 - GROUP:r1] * decay, decay


def _scores_between_groups(q, kk, b):
    blocks = [jnp.zeros((GROUP, CHUNK), F32)]
    for g in range(1, N_GROUPS):
        qs, ks, _ = _by_query_group(q, kk, b, g)
        blocks.append(_dot_bf16(qs, ks, NT))
    return jnp.concatenate(blocks, axis=0)


def _hgrn2_fwd(z, lb_logits, hg_norm_w, name, after=()):
    T = z.shape[0]
    n_chunks = T // CHUNK

    def body(hq_ref, hf_ref, hi_ref, hg_ref, lbl_ref, nw_ref, o_ref, ya_ref, sall_ref, st_ref):
        lbl = lbl_ref[...]
        lb = 1.0 / (1.0 + jnp.exp(lbl[1:2, :] - lbl[0:1, :]))
        st_ref[...] = jnp.zeros_like(st_ref)
        tri = _tri(True)
        row8 = lax.broadcasted_iota(jnp.int32, (8, HG_DK), 0)

        def chunk(c, carry):
            rows = pl.ds(pl.multiple_of(c * CHUNK, CHUNK), CHUNK)
            q, _, _, _, lg, kk = _hg_gates(hq_ref[rows, :], hf_ref[rows, :], lb)
            v = hi_ref[rows, :]
            b = _running_sum(tri, lg)
            st = st_ref[...]
            sall_ref[c] = st
            for grp in range(N_GROUPS):
                r0 = GROUP * grp
                for h8 in range(GROUP // 8):
                    n = 8 * (h8 + 1)
                    bs, ks, vs = b[r0:r0 + n], kk[r0:r0 + n], v[r0:r0 + n]
                    sidx = lax.broadcasted_iota(jnp.int32, (n, HG_DK), 0)
                    blk = jnp.zeros((8, HG_DK), F32)
                    for i in range(8):
                        t = r0 + 8 * h8 + i
                        e = jnp.where(sidx <= 8 * h8 + i, jnp.exp(b[t:t + 1] - bs), 0.0)
                        p = jnp.sum(e * ks * q[t:t + 1], axis=1, keepdims=True)
                        ot = jnp.sum(p * vs, axis=0, keepdims=True)
                        blk = blk + jnp.where(row8 == i, ot, 0.0)
                    o_ref[pl.ds(pl.multiple_of(c * CHUNK + r0 + 8 * h8, 8), 8), :] = blk
            o_ref[rows, :] += _dot_hi(q * jnp.exp(b), st, NT) + _dot_bf16(_scores_between_groups(q, kk, b), v, NN)
            bl = b[CHUNK - 1:CHUNK]
            ke = kk * jnp.exp(bl - b)
            st_ref[...] = st * jnp.exp(bl) + _dot_hi(v, ke, TN)
            return carry

        lax.fori_loop(0, n_chunks, chunk, 0, unroll=2)
        o = o_ref[...]
        r = lax.rsqrt(jnp.mean(o * o, axis=-1, keepdims=True) + EPS)
        hg = hg_ref[...]
        ya_ref[...] = (o * r * nw_ref[...] * (hg * _sigmoid(hg))).astype(BF16)

    def col(base):
        return pl.BlockSpec((T, HG_DK), lambda h: (0, base + h))

    return _call(
        body, (z, z, z, z, lb_logits, hg_norm_w), name=name, grid=(HG_HEADS,),
        in_specs=[col(COL_HQ), col(COL_HF), col(COL_HI), col(COL_HG),
                  pl.BlockSpec((2, HG_DK), lambda h: (0, h)), pl.BlockSpec((1, HG_DK), lambda h: (0, 0))],
        out_specs=[col(0), col(0), pl.BlockSpec((None, n_chunks, HG_DK, HG_DK), lambda h: (h, 0, 0, 0))],
        out_shape=[jax.ShapeDtypeStruct((T, HG_WIDTH), F32), jax.ShapeDtypeStruct((T, HG_WIDTH), BF16),
                   jax.ShapeDtypeStruct((HG_HEADS, n_chunks, HG_DK, HG_DK), F32)],
        scratch_shapes=[pltpu.VMEM((HG_DK, HG_DK), F32)],
        sem=("parallel",), after=after)


def _hgrn2_bwd(z, lb_logits, hg_norm_w, o_raw, s_all, dya, name, after=()):
    T = z.shape[0]
    n_chunks = T // CHUNK

    def body(hq_ref, hf_ref, hi_ref, hg_ref, lbl_ref, nw_ref, o_ref, sall_ref, dya_ref,
             dhq_ref, dhf_ref, dhi_ref, dhg_ref, dlbl_ref, dnw_ref,
             do_ref, dst_ref, dlb_ref, *per_chunk):
        h = pl.program_id(0)
        lbl = lbl_ref[...]
        lb = 1.0 / (1.0 + jnp.exp(lbl[1:2, :] - lbl[0:1, :]))

        o = o_ref[...]
        r = lax.rsqrt(jnp.mean(o * o, axis=-1, keepdims=True) + EPS)
        oh = o * r
        nw = nw_ref[...]
        hg = hg_ref[...]
        sg = _sigmoid(hg)
        dy = dya_ref[...]
        d_on = dy * (hg * sg)
        dhg_ref[...] = (dy * (oh * nw) * (sg * (1.0 + hg * (1.0 - sg)))).astype(BF16)
        dnw = jnp.sum(d_on * oh, axis=0, keepdims=True)
        gy = d_on * nw
        do_ref[...] = r * (gy - oh * jnp.mean(gy * oh, axis=-1, keepdims=True))

        @pl.when(h == 0)
        def _():
            dnw_ref[...] = jnp.zeros_like(dnw_ref)

        dnw_ref[...] += jnp.broadcast_to(dnw, dnw_ref.shape)

        dst_ref[...] = jnp.zeros_like(dst_ref)
        dlb_ref[...] = jnp.zeros_like(dlb_ref)
        tri = _tri(True)
        tri_t = _tri(False)
        row8 = lax.broadcasted_iota(jnp.int32, (8, HG_DK), 0)
        row_group = lax.broadcasted_iota(jnp.int32, (CHUNK, CHUNK), 0) // GROUP
        col_group = lax.broadcasted_iota(jnp.int32, (CHUNK, CHUNK), 1) // GROUP
        earlier_group = col_group < row_group
        later_group = col_group > row_group

        def chunk(c, dq_ref, dk_ref, dv_ref):
            rows = pl.ds(pl.multiple_of(c * CHUNK, CHUNK), CHUNK)
            hq = hq_ref[rows, :]
            q, sq, f, g, lg, kk = _hg_gates(hq, hf_ref[rows, :], lb)
            v = hi_ref[rows, :]
            do = do_ref[rows, :]
            b = _running_sum(tri, lg)
            eb = jnp.exp(b)
            bl = b[CHUNK - 1:CHUNK]
            ebl = jnp.exp(bl)
            ekb = jnp.exp(bl - b)
            qe = q * eb
            ke = kk * ekb
            st = sall_ref[c]
            dst = dst_ref[...]
            dqe = _dot_bf16(do, st, NN)
            dke = _dot_bf16(v, dst, NN)
            dv_inter = _dot_bf16(ke, dst, NT)
            d_ebl = jnp.sum(st * dst, axis=0, keepdims=True)
            dst_ref[...] = dst * ebl + _dot_bf16(do, qe, TN)

            dk_ref[...] = jnp.zeros_like(dk_ref)
            dv_ref[...] = jnp.zeros_like(dv_ref)
            for grp in range(N_GROUPS):
                r0 = GROUP * grp
                for h8 in range(GROUP // 8):
                    n = 8 * (h8 + 1)
                    bs, ks, vs = b[r0:r0 + n], kk[r0:r0 + n], v[r0:r0 + n]
                    sidx = lax.broadcasted_iota(jnp.int32, (n, HG_DK), 0)
                    blk = jnp.zeros((8, HG_DK), F32)
                    for i in range(8):
                        t = r0 + 8 * h8 + i
                        qt = q[t:t + 1]
                        dot_ = do[t:t + 1]
                        e = jnp.where(sidx <= 8 * h8 + i, jnp.exp(b[t:t + 1] - bs), 0.0)
                        w = e * ks
                        p = jnp.sum(w * qt, axis=1, keepdims=True)
                        dsc = jnp.sum(vs * dot_, axis=1, keepdims=True)
                        dqt = jnp.sum(dsc * w, axis=0, keepdims=True)
                        blk = blk + jnp.where(row8 == i, dqt, 0.0)
                        dk_ref[r0:r0 + n, :] += dsc * e * qt
                        dv_ref[r0:r0 + n, :] += p * dot_
                    dq_ref[r0 + 8 * h8:r0 + n, :] = blk
            ds_far = jnp.where(earlier_group, _dot_bf16(do, v, NT), 0.0)
            ds_far_t = jnp.where(later_group, _dot_bf16(v, do, NT), 0.0)
            dq_far, dk_far = [jnp.zeros((GROUP, HG_DK), F32)], []
            for grp in range(1, N_GROUPS):
                r0 = GROUP * grp
                _, ks, decay = _by_query_group(q, kk, b, grp)
                dq_far.append(decay * _dot_hi(ds_far[r0:r0 + GROUP], ks, NN))
                qs, _, decay = _by_key_group(q, kk, b, grp - 1)
                dk_far.append(decay * _dot_hi(ds_far_t[r0 - GROUP:r0], qs, NN))
            dk_far.append(jnp.zeros((GROUP, HG_DK), F32))
            dv_far = _dot_bf16(_scores_between_groups(q, kk, b), do, TN)
            dq_i = dq_ref[...] + jnp.concatenate(dq_far, axis=0)
            dk_i = dk_ref[...] + jnp.concatenate(dk_far, axis=0)
            dke_ke = dke * ke
            db = q * dq_i - kk * dk_i + dqe * qe - dke_ke
            db_last = jnp.sum(dke_ke, axis=0, keepdims=True) + d_ebl * ebl
            dlg = _running_sum(tri_t, db) + db_last
            dq = dq_i + dqe * eb
            dkk = dk_i + dke * ekb
            dg = dlg / g - dkk
            dhq_ref[rows, :] = (dq * (HG_DK ** -0.5) * (sq * (1.0 + hq * (1.0 - sq)))).astype(BF16)
            dhf_ref[rows, :] = (dg * (1.0 - lb) * f * (1.0 - f)).astype(BF16)
            dhi_ref[rows, :] = (dv_ref[...] + dv_far + dv_inter).astype(BF16)
            dlb_ref[...] += jnp.sum(dg * (1.0 - f), axis=0, keepdims=True)

        def trip(i, carry):
            for k in range(BWD_CHUNKS_PER_TRIP):
                chunk(n_chunks - 1 - k - BWD_CHUNKS_PER_TRIP * i, *per_chunk[3 * k:3 * k + 3])
            return carry

        lax.fori_loop(0, n_chunks // BWD_CHUNKS_PER_TRIP, trip, 0)
        dl0 = dlb_ref[...] * lb * (1.0 - lb)
        dlbl_ref[0:1, :] = dl0
        dlbl_ref[1:2, :] = -dl0

    def col(base):
        return pl.BlockSpec((T, HG_DK), lambda h: (0, base + h))

    outb = jax.ShapeDtypeStruct((T, HG_WIDTH), BF16)
    return _call(
        body, (z, z, z, z, lb_logits, hg_norm_w, o_raw, s_all, dya), name=name, grid=(HG_HEADS,),
        in_specs=[col(COL_HQ), col(COL_HF), col(COL_HI), col(COL_HG),
                  pl.BlockSpec((2, HG_DK), lambda h: (0, h)), pl.BlockSpec((1, HG_DK), lambda h: (0, 0)),
                  col(0), pl.BlockSpec((None, n_chunks, HG_DK, HG_DK), lambda h: (h, 0, 0, 0)), col(0)],
        out_specs=[col(0), col(0), col(0), col(0), pl.BlockSpec((2, HG_DK), lambda h: (0, h)),
                   pl.BlockSpec((8, HG_DK), lambda h: (0, 0))],
        out_shape=[outb, outb, outb, outb, jax.ShapeDtypeStruct((2, HG_WIDTH), F32),
                   jax.ShapeDtypeStruct((8, HG_DK), F32)],
        scratch_shapes=[pltpu.VMEM((T, HG_DK), F32), pltpu.VMEM((HG_DK, HG_DK), F32), pltpu.VMEM((1, HG_DK), F32)]
        + [pltpu.VMEM((CHUNK, HG_DK), F32)] * (3 * BWD_CHUNKS_PER_TRIP),
        sem=("arbitrary",), after=after)


CONST_KEYS = PAD - REL_CLIP
VAR_KEYS = BAND - CONST_KEYS
REL_LO = 128
REL_SPAN = N_REL_PAD - REL_LO


def _rel_onehot(t):
    r = lax.broadcasted_iota(jnp.int32, (REL_SPAN, VAR_KEYS), 0)
    j = lax.broadcasted_iota(jnp.int32, (REL_SPAN, VAR_KEYS), 1)
    idx = jnp.clip(t + PAD - CONST_KEYS - j, -REL_CLIP, REL_CLIP) + REL_CLIP - REL_LO
    return jnp.where(r == idx, 1.0, 0.0).astype(BF16)


def _split3(x):
    hi = x.astype(BF16)
    r1 = x - hi.astype(F32)
    mid = r1.astype(BF16)
    return hi, mid, (r1 - mid.astype(F32)).astype(BF16)


def _bias_expand(rel, name):
    def body(rel_ref, out_ref):
        tab = rel_ref[...]
        onehot = _rel_onehot(pl.program_id(0))
        out_ref[:, 0:CONST_KEYS] = jnp.broadcast_to(tab[:, 2 * REL_CLIP:2 * REL_CLIP + 1], (AT_HEADS, CONST_KEYS))
        out_ref[:, CONST_KEYS:BAND] = sum(
            jnp.dot(piece, onehot, preferred_element_type=F32) for piece in _split3(tab[:, REL_LO:N_REL_PAD]))

    return pl.pallas_call(
        body, name=name, grid=(CHUNK,),
        in_specs=[pl.BlockSpec((AT_HEADS, N_REL_PAD), lambda t: (0, 0))],
        out_specs=pl.BlockSpec((None, AT_HEADS, BAND), lambda t: (t, 0, 0)),
        out_shape=jax.ShapeDtypeStruct((CHUNK, AT_HEADS, BAND), F32),
        compiler_params=_cparams(("parallel",)),
    )(rel)


def _bias_reduce(dbias_rows, name, after=()):
    def body(db_ref, out_ref):
        lane = lax.broadcasted_iota(jnp.int32, (AT_HEADS, N_REL_PAD), 1)
        varying = lane >= CONST_KEYS
        by_offset = jnp.zeros((AT_HEADS, N_REL_PAD), F32)
        constant = jnp.zeros((AT_HEADS, N_REL_PAD), F32)
        for t in range(CHUNK):
            row = db_ref[t]
            constant = constant + jnp.where(varying, 0.0, row)
            moved = jnp.where(varying, row, 0.0)
            by_offset = by_offset + (pltpu.roll(moved, N_REL_PAD - t, axis=1) if t else moved)
        offset = lax.broadcasted_iota(jnp.int32, (N_REL_PAD, N_REL_PAD), 0)
        entry = lax.broadcasted_iota(jnp.int32, (N_REL_PAD, N_REL_PAD), 1)
        onehot = jnp.where(entry == jnp.clip(PAD - offset, -REL_CLIP, REL_CLIP) + REL_CLIP, 1.0, 0.0).astype(BF16)
        acc = sum(jnp.dot(piece, onehot, preferred_element_type=F32) for piece in _split3(by_offset))
        last = jnp.sum(constant, axis=1, keepdims=True)
        out_ref[...] = acc + jnp.where(lane == 2 * REL_CLIP, last, 0.0)

    whole = pl.BlockSpec((CHUNK, AT_HEADS, N_REL_PAD), lambda i: (0, 0, 0))
    return _call(
        body, (dbias_rows,), name=name, grid=(1,), in_specs=[whole],
        out_specs=[pl.BlockSpec((AT_HEADS, N_REL_PAD), lambda i: (0, 0))],
        out_shape=[jax.ShapeDtypeStruct((AT_HEADS, N_REL_PAD), F32)],
        sem=("arbitrary",), after=after)[0]


def _pair_lanes():
    return lax.broadcasted_iota(jnp.int32, (CHUNK, 2 * AT_DH), 1) < AT_DH


def _block_diag(a):
    first = _pair_lanes()
    return jnp.concatenate([jnp.where(first, a, 0.0), jnp.where(first, 0.0, a)], axis=0).astype(BF16)


def _diag_blocks(a):
    return jnp.where(_pair_lanes(), a[:CHUNK], a[CHUNK:])


def _band_probs_t(kb, qbd, bias_t, c):
    s = lax.dot_general(kb, qbd, (NT, ((), ())), preferred_element_type=F32) * (AT_DH ** -0.5) + bias_t
    j = lax.broadcasted_iota(jnp.int32, (BAND, 2 * AT_DH), 0)
    s = jnp.where(j + c * CHUNK >= PAD, s, -jnp.inf)
    p = jnp.exp(s - jnp.max(s, axis=0, keepdims=True))
    return p / jnp.sum(p, axis=0, keepdims=True)


def _attn_fwd(z, bias_t, name, after=()):
    T = z.shape[0]
    n_chunks = T // CHUNK

    def body(q_ref, k_ref, v_ref, bias_ref, y_ref, p_ref, *scratch):
        for pr in range(2):
            lanes = slice(128 * pr, 128 * (pr + 1))
            for dst_ref, src_ref in zip(scratch[2 * pr:2 * pr + 2], (k_ref, v_ref)):
                dst_ref[0:PAD, :] = jnp.zeros((PAD, 128), BF16)
                dst_ref[PAD:PAD + T, :] = src_ref[:, lanes].astype(BF16)

        def chunk(c, carry):
            rows = pl.ds(pl.multiple_of(c * CHUNK, CHUNK), CHUNK)
            band = pl.ds(pl.multiple_of(c * CHUNK, CHUNK), BAND)
            for pr in range(2):
                kp_ref, vp_ref = scratch[2 * pr:2 * pr + 2]
                lanes = slice(128 * pr, 128 * (pr + 1))
                p = _band_probs_t(kp_ref[band, :], _block_diag(q_ref[rows, lanes]), bias_ref[pr], c).astype(BF16)
                p_ref[pr, c] = p
                o2 = lax.dot_general(p, vp_ref[band, :], (TN, ((), ())), preferred_element_type=F32)
                y_ref[rows, lanes] = _diag_blocks(o2).astype(BF16)
            return carry

        lax.fori_loop(0, n_chunks, chunk, 0, unroll=2)

    def col(base):
        return pl.BlockSpec((T, 256), lambda h: (0, base // 2 + h))

    return _call(
        body, (z, z, z, bias_t), name=name, grid=(AT_HEADS // 4,),
        in_specs=[col(COL_AQ), col(COL_AK), col(COL_AV), pl.BlockSpec((2, BAND, 128), lambda h: (h, 0, 0))],
        out_specs=[col(0), pl.BlockSpec((2, n_chunks, BAND, 128), lambda h: (h, 0, 0, 0))],
        out_shape=[jax.ShapeDtypeStruct((T, AT_WIDTH), BF16),
                   jax.ShapeDtypeStruct((AT_HEADS // 2, n_chunks, BAND, 128), BF16)],
        scratch_shapes=[pltpu.VMEM((PAD + T, 128), BF16)] * 4,
        sem=("parallel",), after=after)


def _attn_bwd(z, probs, dyb, name, after=()):
    T = z.shape[0]
    n_chunks = T // CHUNK

    def body(q_ref, k_ref, v_ref, p_ref, dy_ref, dq_ref, dk_ref, dv_ref, dbias_ref, *scratch):
        dbias_ref[...] = jnp.zeros_like(dbias_ref)
        for pr in range(2):
            kp_ref, vp_ref, dkp_ref, dvp_ref = scratch[4 * pr:4 * pr + 4]
            lanes = slice(128 * pr, 128 * (pr + 1))
            kp_ref[0:PAD, :] = jnp.zeros((PAD, 128), BF16)
            vp_ref[0:PAD, :] = jnp.zeros((PAD, 128), BF16)
            kp_ref[PAD:PAD + T, :] = k_ref[:, lanes].astype(BF16)
            vp_ref[PAD:PAD + T, :] = v_ref[:, lanes].astype(BF16)
            dkp_ref[...] = jnp.zeros_like(dkp_ref)
            dvp_ref[...] = jnp.zeros_like(dvp_ref)

        def chunk(c, carry):
            rows = pl.ds(pl.multiple_of(c * CHUNK, CHUNK), CHUNK)
            band = pl.ds(pl.multiple_of(c * CHUNK, CHUNK), BAND)
            for pr in range(2):
                kp_ref, vp_ref, dkp_ref, dvp_ref = scratch[4 * pr:4 * pr + 4]
                lanes = slice(128 * pr, 128 * (pr + 1))
                qbd = _block_diag(q_ref[rows, lanes])
                dobd = _block_diag(dy_ref[rows, lanes])
                pb = p_ref[pr, c]
                p = pb.astype(F32)
                dp = lax.dot_general(vp_ref[band, :], dobd, (NT, ((), ())), preferred_element_type=F32)
                ds = p * (dp - jnp.sum(dp * p, axis=0, keepdims=True))
                dbias_ref[pr] += ds
                dsb = ds.astype(BF16)
                dq2 = lax.dot_general(dsb, kp_ref[band, :], (TN, ((), ())), preferred_element_type=F32)
                dq_ref[rows, lanes] = (_diag_blocks(dq2) * (AT_DH ** -0.5)).astype(BF16)
                dkp_ref[band, :] += jnp.dot(dsb, qbd, preferred_element_type=F32) * (AT_DH ** -0.5)
                dvp_ref[band, :] += jnp.dot(pb, dobd, preferred_element_type=F32)
            return carry

        lax.fori_loop(0, n_chunks, chunk, 0)
        for pr in range(2):
            lanes = slice(128 * pr, 128 * (pr + 1))
            dk_ref[:, lanes] = scratch[4 * pr + 2][PAD:PAD + T, :].astype(BF16)
            dv_ref[:, lanes] = scratch[4 * pr + 3][PAD:PAD + T, :].astype(BF16)

    def col(base):
        return pl.BlockSpec((T, 256), lambda h: (0, base // 2 + h))

    outb = jax.ShapeDtypeStruct((T, AT_WIDTH), BF16)
    return _call(
        body, (z, z, z, probs, dyb), name=name, grid=(AT_HEADS // 4,),
        in_specs=[col(COL_AQ), col(COL_AK), col(COL_AV),
                  pl.BlockSpec((2, n_chunks, BAND, 128), lambda h: (h, 0, 0, 0)), col(0)],
        out_specs=[col(0), col(0), col(0), pl.BlockSpec((2, BAND, 128), lambda h: (h, 0, 0))],
        out_shape=[outb, outb, outb, jax.ShapeDtypeStruct((AT_HEADS // 2, BAND, 128), F32)],
        scratch_shapes=([pltpu.VMEM((PAD + T, 128), BF16)] * 2 + [pltpu.VMEM((PAD + T, 128), F32)] * 2) * 2,
        sem=("parallel",), after=after)


def _local_step(x, target, lb_logits, hg_norm_w, rel_bias, norm_mix_w, norm_mlp_w, norm_final_w,
                w_in, rest, exchanges=None):
    ex = exchanges
    rel = jnp.pad(rel_bias, ((0, 0), (0, N_REL_PAD - N_REL)))

    u = _rms_fwd(x, norm_mix_w, "rms_mix_fwd")
    if ex:
        z, w_in = _mm_gathered(u, w_in, ex.order, "mm_in_fwd")
        gather = _Gather(rest[:3], [w_in], "ag")
        mlp_shards, _ = lax.optimization_barrier((rest[3:], gather.token))
        gather_mlp = _Gather([s.astype(BF16) for s in mlp_shards], [gather.token], "ag_mlp")
        z = _mm_gathered_tail(u, w_in, z, ex.order, "mm_in_fwd_tail", after=[gather_mlp.token])
        tok = []
    else:
        z = _mm_nn(u, w_in, F32, "mm_in_fwd")
        w_a, w_b, w_out, w_up, w_down = rest
        tok = []
    o_raw, y_a, s_all = _hgrn2_fwd(z, lb_logits, hg_norm_w, "hgrn2_fwd", after=tok)
    if ex:
        tok = [gather.pass_on([0, 1, 2], [o_raw], "abo")]
    bias_rows = _bias_expand(rel, "bias_expand")
    bias_t = jnp.transpose(bias_rows.reshape(CHUNK, AT_HEADS // 2, 2, BAND), (1, 3, 2, 0)).reshape(
        AT_HEADS // 2, BAND, 2 * CHUNK)
    y_b, probs = _attn_fwd(z, bias_t, "attn_fwd", after=tok)
    if ex:
        tok = [gather_mlp.pass_on([0], [y_b], "up")]
        w_a, w_b, w_out = gather.finish([0, 1, 2], tok, "abo")
    pa = _mm_nn(y_a, w_a, F32, "mm_a_fwd")
    pb, merged = _mm_nn(y_b, w_b, None, "mm_b_fwd", epilogue=(
        (z, z, pa), (COL_GATE_A * GATE_TILE, COL_GATE_B * GATE_TILE, 0), (F32, BF16), _gated_merge))
    w_out1 = w_out.reshape(1, D_MODEL, D_MODEL)
    h1, u2 = _mm_rows(merged, w_out.reshape(D_MODEL, D_MODEL), [x], [norm_mlp_w], (F32, BF16),
                      _residual_rms_rows, "mm_out_fwd")
    if ex:
        w_up, = gather_mlp.finish([0], [u2], "up")
    act = ((), (), (F32, BF16), _squared_relu)
    a, r = _mm_nn(u2, w_up, None, "mm_up_fwd_first", epilogue=act, blocks=(0, N_DEV // 2))
    tok = [gather_mlp.pass_on([1], [r], "down")] if ex else []
    a, r = _mm_nn(u2, w_up, None, "mm_up_fwd_second", after=tok, epilogue=act, blocks=(N_DEV // 2, N_DEV // 2),
                  into=(a, r))
    if ex:
        w_down, = gather_mlp.finish([1], [r], "down")
    w_down1 = w_down.reshape(1, D_FF, D_MODEL)
    mlp = _mm_nn(r, w_down1, F32, "mm_down_fwd")
    loss, dh2, dh2b, g_nf = _loss_head(h1, mlp, norm_final_w, target, "loss_head")

    own = ex.parity if ex else jnp.zeros((1,), jnp.int32)

    def sibling_half(weights, name, after=()):
        others = [_mm_tn_half(a_, g_, 1 - own, on, None, nm + "_sibling", after, *cols)
                  for a_, g_, on, nm, *cols in weights]
        rs = _ReduceScatter(others, name) if ex else None
        return rs, others, ([rs.token] if ex else [])

    def own_half(rs, weights, others, after):
        landed = rs.from_sibling(after) if ex else [None] * len(weights)
        sums = [_mm_tn_half(a_, g_, own, on, l, nm + "_own", (), *cols)
                for (a_, g_, on, nm, *cols), l in zip(weights, landed)]
        if ex:
            return [rs.scatter(sums)], None
        return [], [jnp.stack([s_, o_], axis=1).reshape((N_DEV,) + s_.shape[1:]) for s_, o_ in zip(sums, others)]

    down = [(r, dh2b, "a", "mm_down_wgrad")]
    rs_down, others, tok = sibling_half(down, "rs_down")
    da, = _mm_nt(dh2b, w_down1, None, "mm_down_dgrad", after=tok, epilogue=(
        (a,), (0,), (BF16,), lambda dr, av: (dr * (2.0 * jnp.maximum(av, 0.0)),)))
    tok, g_down = own_half(rs_down, down, others, [da])
    up = [(u2, da, "g", "mm_up_wgrad")]
    rs_up, others, tok = sibling_half(up, "rs_up", tok)
    du2 = _mm_nt(da, w_up, F32, "mm_up_dgrad", after=tok)
    tok, g_up = own_half(rs_up, up, others, [du2])
    dh1, dh1b, g_nmlp = _rms_bwd(du2, h1, norm_mlp_w, dh2, (F32, BF16), "rms_mlp_bwd", after=tok)

    dpa, dpb, dga, dgb = _mm_nt(dh1b, w_out1, None, "mm_out_dgrad", epilogue=(
        (z, z, pa, pb), (COL_GATE_A * GATE_TILE, COL_GATE_B * GATE_TILE, 0, 0), (BF16,) * 4, _merge_grads))
    mix = [(y_a, dpa, "g", "mm_a_wgrad"), (y_b, dpb, "g", "mm_b_wgrad"), (merged, dh1b, "a", "mm_out_wgrad")]
    rs_mix, others, tok = sibling_half(mix, "rs_mix")
    dya = _mm_nt(dpa, w_a, F32, "mm_a_dgrad", after=tok)
    dyb = _mm_nt(dpb, w_b, F32, "mm_b_dgrad", after=tok)
    tok, g_mix = own_half(rs_mix, mix, others, [dya, dyb])
    daq, dak, dav, dbias_t = _attn_bwd(z, probs, dyb, "attn_bwd", after=tok)
    dhq, dhf, dhi, dhg, g_lbl, g_hgw = _hgrn2_bwd(z, lb_logits, hg_norm_w, o_raw, s_all, dya, "hgrn2_bwd",
                                                  after=tok)
    dbias_rows = jnp.pad(jnp.transpose(dbias_t.reshape(AT_HEADS // 2, BAND, 2, CHUNK), (3, 0, 2, 1)).reshape(
        CHUNK, AT_HEADS, BAND), ((0, 0), (0, 0), (0, N_REL_PAD - BAND)))
    dz = jnp.concatenate([dhq, dhf, dhi, dhg, daq, dak, dav, dga, dgb], axis=1)
    half = D_MODEL // 2
    lo = [(u, dz, "g", "mm_in_wgrad_lo", (0, half))]
    hi = [(u, dz, "g", "mm_in_wgrad_hi", (half, half))]
    rs_in_lo, others_lo, tok = sibling_half(lo, "rs_in_lo")
    rs_in_hi, others_hi, tok = sibling_half(hi, "rs_in_hi", tok)
    tok, g_in_lo = own_half(rs_in_lo, lo, others_lo, tok)
    du = _mm_nt(dz, w_in, F32, "mm_in_dgrad", after=tok)
    tok, g_in_hi = own_half(rs_in_hi, hi, others_hi, [du])
    grad_x, g_nmix = _rms_bwd(du, x, norm_mix_w, dh1, (F32,), "rms_mix_bwd", after=tok)
    g_rel = _bias_reduce(dbias_rows, "bias_reduce", after=tok)[:, :N_REL]

    small = dict(lb_logits=g_lbl, hg_norm_w=g_hgw[0:1], rel_bias=g_rel, norm_mix_w=g_nmix, norm_mlp_w=g_nmlp,
                 norm_final_w=g_nf)
    if ex:
        grads = [(rs_in_lo, rs_in_hi), rs_mix, rs_up, rs_down]
    else:
        grads = [jnp.concatenate([g_in_lo[0], g_in_hi[0]], axis=1)] + g_mix + [g_up[0], g_down[0]]
    return loss, grad_x, grads, small


def _mm_gathered(u, shard, order, name):
    T, K = u.shape
    _, Nb = shard.shape

    def body(order_ref, u_ref, shard_ref, z_ref, full_ref, wbuf, load_sem, send_sems, recv_sems, local_sem):
        s = pl.program_id(0)
        x, y, c = _position()
        me, sibling = (x, y, c), (x, y, 1 - c)
        chips = [(1 - x, y), (x, 1 - y), (1 - x, 1 - y)]

        def copy(k, block, to, src=None):
            dst = full_ref.at[4 * block[0] + 2 * block[1] + block[2]]
            return pltpu.make_async_remote_copy(
                src_ref=dst if src is None else src, dst_ref=dst,
                send_sem=send_sems.at[k], recv_sem=recv_sems.at[k], device_id=to, device_id_type=MESH)

        @pl.when(s == 0)
        def _():
            local = pltpu.make_async_copy(shard_ref, full_ref.at[4 * x + 2 * y + c], local_sem)
            local.start()
            copy(0, me, sibling, src=shard_ref).start()
            for j, chip in enumerate(chips):
                copy(1 + j, me, (*chip, c), src=shard_ref).start()
            local.wait()

        @pl.when(s == 1)
        def _():
            copy(0, sibling, me).wait_recv()

        for j, chip in enumerate(chips):
            direct, passed = ((2, 4), (3, 5), (6, 7))[j]

            @pl.when(s == direct)
            def _(j=j, chip=chip):
                copy(1 + j, (*chip, c), me).wait_recv()
                copy(4 + j, (*chip, c), sibling).start()

            @pl.when(s == passed)
            def _(j=j, chip=chip):
                copy(4 + j, (*chip, 1 - c), me).wait_recv()

        @pl.when(s < N_EARLY_BLOCKS)
        def _():
            load = pltpu.make_async_copy(full_ref.at[order_ref[s]], wbuf, load_sem)
            load.start()
            load.wait()
            z_ref[...] = jnp.dot(u_ref[...], wbuf[...], preferred_element_type=F32)

        @pl.when(s == N_DEV - 1)
        def _():
            for k in range(7):
                copy(k, me, sibling).wait_send()

    z, full = pl.pallas_call(
        body, name=name,
        grid_spec=pltpu.PrefetchScalarGridSpec(
            num_scalar_prefetch=1, grid=(N_DEV,),
            in_specs=[pl.BlockSpec((T, K), lambda s, order: (0, 0)), ANY],
            out_specs=[pl.BlockSpec((T, Nb), lambda s, order: (0, order[jnp.minimum(s, N_EARLY_BLOCKS - 1)])), ANY],
            scratch_shapes=[pltpu.VMEM((K, Nb), BF16), pltpu.SemaphoreType.DMA,
                            pltpu.SemaphoreType.DMA((7,)), pltpu.SemaphoreType.DMA((7,)), pltpu.SemaphoreType.DMA]),
        out_shape=[jax.ShapeDtypeStruct((T, N_DEV * Nb), F32), jax.ShapeDtypeStruct((N_DEV, K, Nb), BF16)],
        compiler_params=_cparams(("arbitrary",)),
    )(order, u, shard)
    return z, full


N_EARLY_BLOCKS = 6


def _mm_gathered_tail(u, full, z, order, name, after=()):
    T, K = u.shape
    _, _, Nb = full.shape
    n_after = len(after)

    def body(order_ref, u_ref, w_ref, z_in_ref, *rest):
        rest[n_after][...] = jnp.dot(u_ref[...], w_ref[...], preferred_element_type=F32)

    return pl.pallas_call(
        body, name=name,
        grid_spec=pltpu.PrefetchScalarGridSpec(
            num_scalar_prefetch=1, grid=(N_DEV - N_EARLY_BLOCKS,),
            in_specs=[pl.BlockSpec((T, K), lambda s, order: (0, 0)),
                      pl.BlockSpec((None, K, Nb), lambda s, order: (order[N_EARLY_BLOCKS + s], 0, 0)), ANY]
            + [ANY] * n_after,
            out_specs=pl.BlockSpec((T, Nb), lambda s, order: (0, order[N_EARLY_BLOCKS + s]))),
        out_shape=jax.ShapeDtypeStruct(z.shape, z.dtype),
        input_output_aliases={3: 0},
        compiler_params=_cparams(("arbitrary",)),
    )(order, u, full, z, *after)


def _gather_order():
    x, y, c = _position()
    chips = [(1 - x, y), (x, 1 - y), (1 - x, 1 - y)]
    ids = [4 * x + 2 * y + c, 4 * x + 2 * y + (1 - c)]
    ids += [4 * cx + 2 * cy + c for cx, cy in chips[:2]] + [4 * cx + 2 * cy + (1 - c) for cx, cy in chips[:2]]
    ids += [4 * chips[2][0] + 2 * chips[2][1] + c, 4 * chips[2][0] + 2 * chips[2][1] + (1 - c)]
    return jnp.stack(ids).astype(jnp.int32)


HBM = pl.BlockSpec(memory_space=pltpu.HBM)
SEM = pl.BlockSpec(memory_space=pltpu.SEMAPHORE)
DATAFLOW = pltpu.SideEffectType.DATAFLOW_SIDE_EFFECTING


def _split_call(name, bufs, waits=(), starts=None, after=()):
    nb = len(bufs)
    n_new = starts[1] if starts else 0
    wait_sems = [s for w in waits for s in (*w[1], *w[2])]

    def body(*refs):
        b, pos = refs[:nb], nb
        for plan, ss, _, send_idx, recv_idx in waits:
            k = len(ss)
            copies = plan(b, refs[pos:pos + k], refs[pos + k:pos + 2 * k])
            pos += 2 * k
            for i in recv_idx:
                copies[i].wait_recv()
            for i in send_idx:
                copies[i].wait_send()
        outs = refs[pos + len(after):]
        if starts:
            for cp in starts[0](b, outs[nb:nb + n_new], outs[nb + n_new:nb + 2 * n_new]):
                cp.start()
        outs[-1][...] = jnp.zeros_like(outs[-1])

    res = pl.pallas_call(
        body, name=name,
        out_shape=tuple(pltpu.HBM(a.shape, a.dtype) for a in bufs) + (pltpu.SemaphoreType.DMA(()),) * (2 * n_new)
        + (jax.ShapeDtypeStruct((8, 128), F32),),
        in_specs=[HBM] * nb + [SEM] * len(wait_sems) + [ANY] * len(after),
        out_specs=(HBM,) * nb + (SEM,) * (2 * n_new) + (pl.BlockSpec(memory_space=pltpu.VMEM),),
        input_output_aliases={i: i for i in range(nb)},
        compiler_params=pltpu.CompilerParams(has_side_effects=DATAFLOW),
    )(*bufs, *wait_sems, *after)
    return list(res[:nb]), list(res[nb:nb + n_new]), list(res[nb + n_new:nb + 2 * n_new]), res[-1]


def _in_hbm(a):
    return pltpu.with_memory_space_constraint(a, pltpu.HBM)


def _remote(src, dst, send_sem, recv_sem, to):
    return pltpu.make_async_remote_copy(src_ref=src, dst_ref=dst, send_sem=send_sem, recv_sem=recv_sem,
                                        device_id=to, device_id_type=MESH)


def _other_chips():
    x, y, _ = _position()
    return [(1 - x, y), (x, 1 - y), (1 - x, 1 - y)]


def _plan_gather_first(n):
    def plan(b, ss, rs):
        x, y, c = _position()
        to = [(x, y, 1 - c)] + [(*chip, c) for chip in _other_chips()]
        return [_remote(b[w], b[n + w].at[4 * x + 2 * y + c], ss[4 * w + k], rs[4 * w + k], to[k])
                for w in range(n) for k in range(4)]
    return plan, 4 * n


def _plan_gather_pass(n):
    def plan(b, ss, rs):
        x, y, c = _position()
        copies = []
        for w in range(n):
            for j, chip in enumerate(_other_chips()):
                blk = b[n + w].at[4 * chip[0] + 2 * chip[1] + c]
                copies.append(_remote(blk, blk, ss[3 * w + j], rs[3 * w + j], (x, y, 1 - c)))
        return copies
    return plan, 3 * n


def _plan_sibling(n):
    def plan(b, ss, rs):
        x, y, c = _position()
        return [_remote(b[w].at[s], b[n + w].at[s], ss[4 * w + s], rs[4 * w + s], (x, y, 1 - c))
                for w in range(n) for s in range(N_CHIP)]
    return plan, 4 * n


def _plan_scatter(n):
    def plan(b, ss, rs):
        x, y, c = _position()
        return [_remote(b[w].at[2 * chip[0] + chip[1]], b[n + w].at[2 * x + y], ss[3 * w + j], rs[3 * w + j],
                        (*chip, c))
                for w in range(n) for j, chip in enumerate(_other_chips())]
    return plan, 3 * n


class _Gather:
    def __init__(self, shards, after, name):
        self.n, self.name = len(shards), name
        x, y, c = _position()
        placed = [lax.dynamic_update_index_in_dim(lax.empty((N_DEV,) + s.shape, s.dtype), s, 4 * x + 2 * y + c, 0)
                  for s in shards]
        bufs, self.ss, self.rs, self.token = _split_call(
            name + "_start", [_in_hbm(a) for a in list(shards) + placed], starts=_plan_gather_first(self.n),
            after=after)
        self.shards, self.fulls = bufs[:self.n], bufs[self.n:]
        self.passed = {}

    def _sub(self, ids, sems, per):
        return [sems[per * w + k] for w in ids for k in range(per)]

    def pass_on(self, ids, after, tag):
        m = len(ids)
        first = (_plan_gather_first(m)[0], self._sub(ids, self.ss, 4), self._sub(ids, self.rs, 4),
                 [], [4 * i + k for i in range(m) for k in (1, 2, 3)])
        bufs, ss, rs, token = _split_call(
            "%s_pass_%s" % (self.name, tag), [self.shards[w] for w in ids] + [self.fulls[w] for w in ids],
            waits=[first], starts=_plan_gather_pass(m), after=after)
        for i, w in enumerate(ids):
            self.shards[w], self.fulls[w] = bufs[i], bufs[m + i]
        self.passed[tuple(ids)] = (ss, rs)
        return token

    def finish(self, ids, after, tag):
        m = len(ids)
        ss2, rs2 = self.passed[tuple(ids)]
        first = (_plan_gather_first(m)[0], self._sub(ids, self.ss, 4), self._sub(ids, self.rs, 4),
                 list(range(4 * m)), [4 * i for i in range(m)])
        passed = (_plan_gather_pass(m)[0], ss2, rs2, list(range(3 * m)), list(range(3 * m)))
        bufs, _, _, _ = _split_call(
            "%s_finish_%s" % (self.name, tag), [self.shards[w] for w in ids] + [self.fulls[w] for w in ids],
            waits=[first, passed], after=after)
        return bufs[m:]


class _ReduceScatter:
    def __init__(self, others, name):
        self.n, self.name = len(others), name
        lands = [lax.empty(g.shape, g.dtype) for g in others]
        self.bufs, self.ss, self.rs, self.token = _split_call(
            name + "_sibling_start", [_in_hbm(a) for a in list(others) + lands], starts=_plan_sibling(self.n))

    def from_sibling(self, after):
        n = self.n
        bufs, _, _, _ = _split_call(
            self.name + "_sibling_wait", self.bufs,
            waits=[(_plan_sibling(n)[0], self.ss, self.rs, list(range(4 * n)), list(range(4 * n)))], after=after)
        return bufs[n:]

    def scatter(self, sums):
        lands = [lax.empty(s.shape, s.dtype) for s in sums]
        self.bufs, self.ss, self.rs, token = _split_call(
            self.name + "_scatter_start", [_in_hbm(a) for a in list(sums) + lands], starts=_plan_scatter(self.n))
        return token

    def finish(self, after):
        n = self.n
        bufs, _, _, _ = _split_call(
            self.name + "_scatter_wait", self.bufs,
            waits=[(_plan_scatter(n)[0], self.ss, self.rs, list(range(3 * n)), list(range(3 * n)))], after=after)
        return bufs[:n], bufs[n:]


class _Exchanges:
    def __init__(self, parity, order):
        self.parity, self.order = parity, order


def _gather_small(packed, name):
    R = packed.shape[0]

    def body(x_ref, out_ref, send_sems, recv_sems):
        x, y, c = _position()
        me = 4 * x + 2 * y + c
        out_ref[me] = x_ref[...]
        copies = []
        for k in range(1, N_DEV):
            to = (x ^ ((k >> 2) & 1), y ^ ((k >> 1) & 1), c ^ (k & 1))
            cp = pltpu.make_async_remote_copy(
                src_ref=x_ref, dst_ref=out_ref.at[me],
                send_sem=send_sems.at[k], recv_sem=recv_sems.at[k], device_id=to, device_id_type=MESH)
            cp.start()
            copies.append((k, to, cp))
        for k, to, cp in copies:
            cp.wait_send()
            pltpu.make_async_remote_copy(
                src_ref=x_ref, dst_ref=out_ref.at[4 * to[0] + 2 * to[1] + to[2]],
                send_sem=send_sems.at[k], recv_sem=recv_sems.at[k], device_id=to, device_id_type=MESH).wait_recv()

    return pl.pallas_call(
        body, name=name,
        in_specs=[pl.BlockSpec(memory_space=pltpu.VMEM)], out_specs=pl.BlockSpec(memory_space=pltpu.VMEM),
        out_shape=jax.ShapeDtypeStruct((N_DEV, R, 128), F32),
        scratch_shapes=[pltpu.SemaphoreType.DMA((N_DEV,)), pltpu.SemaphoreType.DMA((N_DEV,))],
    )(packed)


def _adamw_math(w, g, m, v):
    m = ADAM_B1 * m + (1.0 - ADAM_B1) * g
    v = ADAM_B2 * v + (1.0 - ADAM_B2) * (g * g)
    m_hat = m / (1.0 - ADAM_B1 ** ADAM_STEP)
    v_hat = v / (1.0 - ADAM_B2 ** ADAM_STEP)
    delta = -ADAM_LR * (m_hat / (jnp.sqrt(v_hat) + ADAM_EPS) + ADAM_WD * w)
    return delta, m, v


def _adamw_big_landed(w, m, v, parts, lands, slot, name, row0=0, into=None):
    R, C = w.shape
    rows = parts.shape[1]
    tr = _pick(rows, (256,))
    first = row0 // tr
    n_into = len(into) if into else 0

    def body(slot_ref, w_ref, m_ref, v_ref, own_ref, l1_ref, l2_ref, l3_ref, *rest):
        g = own_ref[...].astype(F32)
        for ref in (l1_ref, l2_ref, l3_ref):
            g = g + ref[...].astype(F32)
        for o_ref, res in zip(rest[n_into:], (g,) + _adamw_math(w_ref[...], g, m_ref[...], v_ref[...])):
            o_ref[...] = res

    blk = pl.BlockSpec((tr, C), lambda i, slot: (first + i, 0))

    def chip(k):
        return pl.BlockSpec((None, tr, C), lambda i, slot: ((slot[0] + k) % N_CHIP, i, 0))

    out = jax.ShapeDtypeStruct((R, C), F32)
    return pl.pallas_call(
        body, name=name,
        grid_spec=pltpu.PrefetchScalarGridSpec(
            num_scalar_prefetch=1, grid=(rows // tr,),
            in_specs=[blk, blk, blk, chip(0), chip(1), chip(2), chip(3)] + [ANY] * n_into,
            out_specs=[blk, blk, blk, blk]),
        out_shape=[out, out, out, out],
        input_output_aliases={8 + j: j for j in range(n_into)},
        compiler_params=_cparams(("parallel",)),
    )(slot, w, m, v, parts, lands, lands, lands, *(into or ()))


def _adamw_small(w, m, v, gathered, name):
    R = w.shape[0]

    def body(w_ref, m_ref, v_ref, p_ref, g_ref, d_ref, nm_ref, nv_ref):
        g = p_ref[0]
        for s in range(1, N_DEV):
            g = g + p_ref[s]
        d, nm, nv = _adamw_math(w_ref[...], g, m_ref[...], v_ref[...])
        g_ref[...] = g
        d_ref[...] = d
        nm_ref[...] = nm
        nv_ref[...] = nv

    out = jax.ShapeDtypeStruct((R, 128), F32)
    return pl.pallas_call(
        body, name=name, out_shape=[out, out, out, out],
    )(w, m, v, gathered)


SMALL_NAMES = ("lb_logits", "hg_norm_w", "rel_bias", "norm_mix_w", "norm_mlp_w", "norm_final_w")
SMALL_SHAPES = {"lb_logits": (2, HG_WIDTH), "hg_norm_w": (1, HG_DK), "rel_bias": (AT_HEADS, N_REL_PAD),
                "norm_mix_w": (1, D_MODEL), "norm_mlp_w": (1, D_MODEL), "norm_final_w": (1, D_MODEL)}


def _pack_small(parts):
    rows = []
    for nme in SMALL_NAMES:
        p = parts[nme]
        if nme == "rel_bias":
            p = jnp.pad(p, ((0, 0), (0, N_REL_PAD - N_REL)))
        rows.append(p.reshape(-1, 128))
    flat = jnp.concatenate(rows, axis=0)
    return jnp.pad(flat, ((0, SMALL_ROWS - flat.shape[0]), (0, 0)))


def _unpack_small(packed):
    out, at = {}, 0
    for nme in SMALL_NAMES:
        shp = SMALL_SHAPES[nme]
        nrow = shp[0] * shp[1] // 128
        p = packed[at:at + nrow].reshape(shp)
        at += nrow
        out[nme] = p[:, :N_REL] if nme == "rel_bias" else p
    return out


BIG_NAMES = ("w_in", "w_branch_a", "w_branch_b", "w_out", "w_up", "w_down")


def kernel(x, w_in, lb_logits, hg_norm_w, rel_bias, w_branch_a, w_branch_b, w_out, norm_mix_w, norm_mlp_w, w_up, w_down, norm_final_w, loss_target, m_w_in, m_lb_logits, m_hg_norm_w, m_rel_bias, m_w_branch_a, m_w_branch_b, m_w_out, m_norm_mix_w, m_norm_mlp_w, m_w_up, m_w_down, m_norm_final_w, v_w_in, v_lb_logits, v_hg_norm_w, v_rel_bias, v_w_branch_a, v_w_branch_b, v_w_out, v_norm_mix_w, v_norm_mlp_w, v_w_up, v_w_down, v_norm_final_w):
    big_w = [w_in[0], w_branch_a[0], w_branch_b[0], w_out[0], w_up[0], w_down[0]]
    big_m = [m_w_in[0], m_w_branch_a[0], m_w_branch_b[0], m_w_out[0], m_w_up[0], m_w_down[0]]
    big_v = [v_w_in[0], v_w_branch_a[0], v_w_branch_b[0], v_w_out[0], v_w_up[0], v_w_down[0]]

    shards = [w.astype(BF16) for w in big_w[:4]] + big_w[4:]
    parity = lax.axis_index("c").astype(jnp.int32).reshape(1)
    loss_part, grad_x, chip_parts, small = _local_step(
        x[0], loss_target[0], lb_logits, hg_norm_w, rel_bias[0], norm_mix_w, norm_mlp_w,
        norm_final_w.reshape(1, D_MODEL), shards[0], shards[1:], _Exchanges(parity, _gather_order()))
    loss = lax.psum(loss_part[0, 0], ("x", "y", "c"))
    (rs_in_lo, rs_in_hi), rs_mix, rs_up, rs_down = chip_parts
    slot =(2 * lax.axis_index("x") + lax.axis_index("y")).astype(jnp.int32).reshape(1)
    big = {}

    def finish(rs, names, after):
        sums, lands = rs.finish(after)
        for nme, own, land in zip(names, sums, lands):
            i = BIG_NAMES.index(nme)
            big[nme] = _adamw_big_landed(big_w[i], big_m[i], big_v[i], own, land, slot, "adamw_" + nme)
        return [big[nme][1] for nme in names]

    done = finish(rs_down, ["w_down"], [grad_x])
    done = finish(rs_up, ["w_up"], done)
    done = finish(rs_mix, ["w_branch_a", "w_branch_b", "w_out"], done)

    sw = dict(lb_logits=lb_logits, hg_norm_w=hg_norm_w, rel_bias=rel_bias[0], norm_mix_w=norm_mix_w,
              norm_mlp_w=norm_mlp_w, norm_final_w=norm_final_w.reshape(1, D_MODEL))
    sm = dict(lb_logits=m_lb_logits, hg_norm_w=m_hg_norm_w, rel_bias=m_rel_bias[0], norm_mix_w=m_norm_mix_w,
              norm_mlp_w=m_norm_mlp_w, norm_final_w=m_norm_final_w.reshape(1, D_MODEL))
    sv = dict(lb_logits=v_lb_logits, hg_norm_w=v_hg_norm_w, rel_bias=v_rel_bias[0], norm_mix_w=v_norm_mix_w,
              norm_mlp_w=v_norm_mlp_w, norm_final_w=v_norm_final_w.reshape(1, D_MODEL))
    gathered = _gather_small(_pack_small(small), "gather_small")
    small_packed = _adamw_small(_pack_small(sw), _pack_small(sm), _pack_small(sv), gathered, "adamw_small")
    small_out = [_unpack_small(p) for p in small_packed]

    (own,), (land,) = rs_in_lo.finish(done + [small_packed[0]])
    lo = _adamw_big_landed(big_w[0], big_m[0], big_v[0], own, land, slot, "adamw_w_in_lo")
    (own,), (land,) = rs_in_hi.finish([lo[1]])
    big["w_in"] = _adamw_big_landed(big_w[0], big_m[0], big_v[0], own, land, slot, "adamw_w_in_hi",
                                    row0=D_MODEL // 2, into=lo)

    def leaf(kind, nme):
        if nme in BIG_NAMES:
            return big[nme][kind][None]
        p = small_out[kind][nme]
        if nme == "rel_bias":
            return p[None]
        if nme == "norm_final_w":
            return p.reshape(D_MODEL)
        return p

    order = ("w_in", "lb_logits", "hg_norm_w", "rel_bias", "w_branch_a", "w_branch_b", "w_out", "norm_mix_w",
             "norm_mlp_w", "w_up", "w_down", "norm_final_w")
    outs = [loss, grad_x[None]]
    for kind in range(4):
        outs += [leaf(kind, nme) for nme in order]
    return tuple(outs)
```

```python
import jax
import jax.numpy as jnp
from jax import lax
from jax.experimental import pallas as pl
from jax.experimental.pallas import tpu as pltpu

F32 = jnp.float32
BF16 = jnp.bfloat16
HIGHEST = lax.Precision.HIGHEST
MESH = pl.DeviceIdType.MESH

D_MODEL = 2048
HG_HEADS = 8
HG_DK = 128
HG_WIDTH = 1024
AT_HEADS = 16
AT_DH = 64
AT_WIDTH = 1024
CHUNK = 64
LEFT_CHUNKS = 8
BAND = (LEFT_CHUNKS + 1) * CHUNK
PAD = LEFT_CHUNKS * CHUNK
REL_CLIP = 256
N_REL = 2 * REL_CLIP + 1
N_REL_PAD = 640
D_FF = 4 * D_MODEL
EPS = 1e-6
N_DEV = 8
N_CHIP = 4

ADAM_LR = 0.001
ADAM_B1 = 0.9
ADAM_B2 = 0.999
ADAM_EPS = 1e-08
ADAM_WD = 0.01
ADAM_STEP = 10

COL_HQ, COL_HF, COL_HI, COL_HG = 0, 8, 16, 24
COL_AQ, COL_AK, COL_AV = 32, 40, 48
COL_GATE_A, COL_GATE_B = 7, 9

VMEM_LIMIT = 56 * 1024 * 1024
SMALL_ROWS = 152


def _cparams(sem=None, **kw):
    if sem is not None:
        kw["dimension_semantics"] = sem
    return pltpu.CompilerParams(vmem_limit_bytes=VMEM_LIMIT, **kw)


def _pick(n, cands):
    for c in cands:
        if n % c == 0:
            return c
    return n


def _sigmoid(x):
    return 1.0 / (1.0 + jnp.exp(-x))


ANY = pl.BlockSpec(memory_space=pl.ANY)


def _position():
    return lax.axis_index("x"), lax.axis_index("y"), lax.axis_index("c")


def _call(body, args, *, name, grid, in_specs, out_specs, out_shape, scratch_shapes=(), sem=None, after=(),
          aliases=None):
    n_in = len(args)

    def ordered(*refs):
        body(*refs[:n_in], *refs[n_in + len(after):])

    return list(pl.pallas_call(
        ordered if after else body, name=name, grid=grid, in_specs=list(in_specs) + [ANY] * len(after),
        out_specs=out_specs, out_shape=out_shape, scratch_shapes=list(scratch_shapes),
        input_output_aliases=aliases or {}, compiler_params=_cparams(sem))(*args, *after))


MAX_CONTRACTION_TILE = 4096


def _accumulate(part, acc_ref, step, n_steps, finish):
    if n_steps == 1:
        finish(part)
        return

    @pl.when(step == 0)
    def _():
        acc_ref[...] = part

    @pl.when(step > 0)
    def _():
        acc_ref[...] += part

    @pl.when(step == n_steps - 1)
    def _():
        finish(acc_ref[...])


def _mm_nn(a, wb, out_dtype, name, after=(), epilogue=None, blocks=None, into=()):
    M, K = a.shape
    NB, K2, Nb = wb.shape
    assert K == K2
    j0, nj = blocks or (0, NB)
    n_into = len(into)
    tm = min(M, 1024)
    tk = min(K, MAX_CONTRACTION_TILE)
    tn = _pick(Nb, (512, 1408, 256))
    nk = K // tk
    nn = Nb // tn
    extra, first_cols, out_dtypes, fn = epilogue or ((), (), (out_dtype,), lambda total: (total,))
    n_extra, n_out = len(extra), len(out_dtypes)

    def body(a_ref, b_ref, *rest):
        def finish(total):
            results = fn(total, *[r[...] for r in rest[:n_extra]])
            for o_ref, res, dt in zip(rest[n_extra + n_into:n_extra + n_into + n_out], results, out_dtypes):
                o_ref[...] = res.astype(dt)

        part = jnp.dot(a_ref[...], b_ref[...], preferred_element_type=F32)
        _accumulate(part, rest[-1], pl.program_id(3), nk, finish)

    def tile(first):
        return pl.BlockSpec((tm, tn), lambda m, j, n, k: (m, first + (j0 + j) * nn + n))

    outs = _call(
        body, (a, wb) + tuple(extra) + tuple(into), name=name, grid=(M // tm, nj, nn, nk),
        in_specs=[pl.BlockSpec((tm, tk), lambda m, j, n, k: (m, k)),
                  pl.BlockSpec((None, tk, tn), lambda m, j, n, k: (j0 + j, k, n))]
        + [tile(col // tn) for col in first_cols] + [ANY] * n_into,
        out_specs=[tile(0)] * n_out,
        out_shape=[jax.ShapeDtypeStruct((M, NB * Nb), dt) for dt in out_dtypes],
        scratch_shapes=[] if nk == 1 else [pltpu.VMEM((tm, tn), F32)],
        sem=("parallel", "parallel", "parallel", "arbitrary"), after=after,
        aliases={2 + n_extra + i: i for i in range(n_into)})
    return outs if epilogue else outs[0]


def _squared_relu(a):
    ra = jnp.maximum(a, 0.0)
    return a, ra * ra


def _gated_merge(pb, za, zb, pa):
    return pb, _sigmoid(za) * pa + _sigmoid(zb) * pb


def _mm_nt(a, wb, out_dtype, name, after=(), epilogue=None):
    M, N = a.shape
    NB, K, Nb = wb.shape
    assert N == NB * Nb
    tm = min(M, 1024)
    n_tiles_live = 1 + (len(epilogue[0]) + len(epilogue[2]) if epilogue else 0)
    tko = _pick(K, (1024,)) if n_tiles_live <= 3 else _pick(K, (512,))
    tc = _pick(Nb, (2048, 1024, 1408, 256))
    nc = Nb // tc
    jb = max([d for d in (8, 4, 2, 1) if NB % d == 0 and d * tc <= MAX_CONTRACTION_TILE]) if nc == 1 else 1
    nsteps = (NB // jb) * nc
    extra, first_cols, out_dtypes, fn = epilogue or ((), (), (out_dtype,), lambda total: (total,))
    n_extra, n_out = len(extra), len(out_dtypes)

    def body(a_ref, b_ref, *rest):
        def finish(total):
            results = fn(total, *[r[...] for r in rest[:n_extra]])
            for o_ref, res, dt in zip(rest[n_extra:n_extra + n_out], results, out_dtypes):
                o_ref[...] = res.astype(dt)

        part = sum(lax.dot_general(a_ref[:, i * tc:(i + 1) * tc], b_ref[i], (((1,), (1,)), ((), ())),
                                   preferred_element_type=F32) for i in range(jb))
        _accumulate(part, rest[-1], pl.program_id(2) * nc + pl.program_id(3), nsteps, finish)

    def tile(first):
        return pl.BlockSpec((tm, tko), lambda m, ko, j, c: (m, first + ko))

    outs = _call(
        body, (a, wb) + tuple(extra), name=name,
        grid=(M // tm, K // tko, NB // jb, nc),
        in_specs=[pl.BlockSpec((tm, jb * tc), lambda m, ko, j, c: (m, j * nc + c)),
                  pl.BlockSpec((jb, tko, tc), lambda m, ko, j, c: (j, ko, c))] + [tile(col // tko) for col in first_cols],
        out_specs=[tile(0)] * n_out,
        out_shape=[jax.ShapeDtypeStruct((M, K), dt) for dt in out_dtypes],
        scratch_shapes=[] if nsteps == 1 else [pltpu.VMEM((tm, tko), F32)],
        sem=("parallel", "parallel", "arbitrary", "arbitrary"), after=after)
    return outs if epilogue else outs[0]


ROWS_TILE = 512
ROWS_PIECE = 128


def _mm_rows(a, w, extras, vectors, row_dtypes, fn, name):
    M, K = a.shape
    N = w.shape[1]
    tm = min(M, ROWS_TILE)
    n_e, n_v = len(extras), len(vectors)

    def body(a_ref, w_ref, *rest):
        tiles, vecs, outs, product_ref = rest[:n_e], rest[n_e:n_e + n_v], rest[n_e + n_v:-1], rest[-1]
        product_ref[...] = jnp.dot(a_ref[...], w_ref[...], preferred_element_type=F32)
        for i in range(tm // ROWS_PIECE):
            piece = slice(i * ROWS_PIECE, (i + 1) * ROWS_PIECE)
            results = fn(product_ref[piece, :], *[t[piece, :] for t in tiles], *[v[...] for v in vecs])
            for o_ref, res, dt in zip(outs, results, row_dtypes):
                o_ref[piece, :] = res.astype(dt)

    row = pl.BlockSpec((tm, N), lambda m: (m, 0))
    return _call(
        body, (a, w) + tuple(extras) + tuple(vectors), name=name, grid=(M // tm,),
        in_specs=[pl.BlockSpec((tm, K), lambda m: (m, 0)), pl.BlockSpec((K, N), lambda m: (0, 0))]
        + [row] * n_e + [pl.BlockSpec((1, N), lambda m: (0, 0))] * n_v,
        out_specs=[row] * len(row_dtypes),
        out_shape=[jax.ShapeDtypeStruct((M, N), dt) for dt in row_dtypes],
        scratch_shapes=[pltpu.VMEM((tm, N), F32)], sem=("parallel",))


def _rms(h, w):
    return h * lax.rsqrt(jnp.mean(h * h, axis=-1, keepdims=True) + EPS) * w


def _residual_rms_rows(mix, x, w):
    h = x + mix
    return h, _rms(h, w)


def _mm_tn_half(a, g, which, blocks_on, add, name, after=(), a_cols=None):
    M, Ka = a.shape
    N = g.shape[1]
    first_col = 0
    if a_cols is not None:
        first_col, Ka = a_cols
    if blocks_on == "g":
        rows, cols = _pick(Ka, (1024,)), N // N_DEV
        tn = _pick(cols, (512, 1408, 256))
        nn = cols // tn
        grid = (Ka // rows, N_CHIP, nn)
        a_spec = pl.BlockSpec((M, rows), lambda ka, s, n, w: (0, first_col // rows + ka))
        g_spec = pl.BlockSpec((M, tn), lambda ka, s, n, w: (0, (2 * s + w[0]) * nn + n))
        out_rows = Ka
    else:
        rows, cols = Ka // N_DEV, N
        tn = _pick(cols, (2048, 512))
        nn = cols // tn
        grid = (1, N_CHIP, nn)
        a_spec = pl.BlockSpec((M, rows), lambda ka, s, n, w: (0, 2 * s + w[0]))
        g_spec = pl.BlockSpec((M, tn), lambda ka, s, n, w: (0, n))
        out_rows = rows
    o_spec = pl.BlockSpec((None, rows, tn), lambda ka, s, n, w: (s, ka, n))
    n_add = 0 if add is None else 1

    def body(which_ref, a_ref, g_ref, *rest):
        acc = lax.dot_general(a_ref[...], g_ref[...], (((0,), (0,)), ((), ())), preferred_element_type=F32)
        if n_add:
            acc = acc + rest[0][...].astype(F32)
        rest[-1][...] = acc.astype(BF16)

    return pl.pallas_call(
        body, name=name,
        grid_spec=pltpu.PrefetchScalarGridSpec(
            num_scalar_prefetch=1, grid=grid,
            in_specs=[a_spec, g_spec] + [o_spec] * n_add + [ANY] * len(after),
            out_specs=o_spec),
        out_shape=jax.ShapeDtypeStruct((N_CHIP, out_rows, cols), BF16),
        compiler_params=_cparams(("parallel", "parallel", "parallel")),
    )(which, a, g, *(() if add is None else (add,)), *after)


ROW_TILE = 256


def _rms_fwd(x, w, name):
    T, Dm = x.shape

    def body(x_ref, w_ref, u_ref):
        xv = x_ref[...]
        r = lax.rsqrt(jnp.mean(xv * xv, axis=-1, keepdims=True) + EPS)
        u_ref[...] = (xv * r * w_ref[...]).astype(BF16)

    return pl.pallas_call(
        body, name=name, grid=(T // ROW_TILE,),
        in_specs=[pl.BlockSpec((ROW_TILE, Dm), lambda i: (i, 0)), pl.BlockSpec((1, Dm), lambda i: (0, 0))],
        out_specs=pl.BlockSpec((ROW_TILE, Dm), lambda i: (i, 0)),
        out_shape=jax.ShapeDtypeStruct((T, Dm), BF16),
        compiler_params=_cparams(("parallel",)),
    )(x, w)


def _loss_head(h1, mlp, wf, target, name):
    T, Dm = h1.shape

    def body(h_ref, m_ref, w_ref, t_ref, loss_ref, dh_ref, dhb_ref, dw_ref):
        i = pl.program_id(0)
        h = h_ref[...] + m_ref[...]
        r = lax.rsqrt(jnp.mean(h * h, axis=-1, keepdims=True) + EPS)
        xh = h * r
        wv = w_ref[...]
        e = xh * wv - t_ref[...]
        part = 0.5 * jnp.sum(jnp.mean(e * e, axis=-1, keepdims=True), axis=0, keepdims=True)
        dy = e * (1.0 / Dm)
        dw = jnp.sum(dy * xh, axis=0, keepdims=True)
        gy = dy * wv
        dh = r * (gy - xh * jnp.mean(gy * xh, axis=-1, keepdims=True))
        dh_ref[...] = dh
        dhb_ref[...] = dh.astype(BF16)

        @pl.when(i == 0)
        def _():
            loss_ref[...] = jnp.zeros_like(loss_ref)
            dw_ref[...] = jnp.zeros_like(dw_ref)

        loss_ref[...] += jnp.broadcast_to(part, loss_ref.shape)
        dw_ref[...] += dw

    row = pl.BlockSpec((ROW_TILE, Dm), lambda i: (i, 0))
    vec = pl.BlockSpec((1, Dm), lambda i: (0, 0))
    return pl.pallas_call(
        body, name=name, grid=(T // ROW_TILE,),
        in_specs=[row, row, vec, row],
        out_specs=[pl.BlockSpec((8, 128), lambda i: (0, 0)), row, row, vec],
        out_shape=[jax.ShapeDtypeStruct((8, 128), F32), jax.ShapeDtypeStruct((T, Dm), F32),
                   jax.ShapeDtypeStruct((T, Dm), BF16), jax.ShapeDtypeStruct((1, Dm), F32)],
        compiler_params=_cparams(("arbitrary",)),
    )(h1, mlp, wf, target)


def _rms_bwd(dyn, x, w, dres, dx_dtypes, name, after=()):
    T, Dm = x.shape
    n_dx = len(dx_dtypes)

    def body(g_ref, x_ref, w_ref, r_ref, *outs):
        i = pl.program_id(0)
        xv = x_ref[...]
        r = lax.rsqrt(jnp.mean(xv * xv, axis=-1, keepdims=True) + EPS)
        xh = xv * r
        g = g_ref[...]
        dw = jnp.sum(g * xh, axis=0, keepdims=True)
        gy = g * w_ref[...]
        dx = r_ref[...] + r * (gy - xh * jnp.mean(gy * xh, axis=-1, keepdims=True))
        for dx_ref, dt in zip(outs, dx_dtypes):
            dx_ref[...] = dx.astype(dt)
        dw_ref = outs[n_dx]

        @pl.when(i == 0)
        def _():
            dw_ref[...] = jnp.zeros_like(dw_ref)

        dw_ref[...] += dw

    row = pl.BlockSpec((ROW_TILE, Dm), lambda i: (i, 0))
    vec = pl.BlockSpec((1, Dm), lambda i: (0, 0))
    return _call(
        body, (dyn, x, w, dres), name=name, grid=(T // ROW_TILE,),
        in_specs=[row, row, vec, row],
        out_specs=[row] * n_dx + [vec],
        out_shape=[jax.ShapeDtypeStruct((T, Dm), dt) for dt in dx_dtypes] + [jax.ShapeDtypeStruct((1, Dm), F32)],
        sem=("arbitrary",), after=after)


GATE_TILE = 1024


def _merge_grads(d, za, zb, pa, pb):
    ga = _sigmoid(za)
    gb = _sigmoid(zb)
    return d * ga, d * gb, d * pa * ga * (1.0 - ga), d * pb * gb * (1.0 - gb)


def _dot_hi(a, b, dims):
    return lax.dot_general(a, b, (dims, ((), ())), precision=HIGHEST, preferred_element_type=F32)


NN = ((1,), (0,))
NT = ((1,), (1,))
TN = ((0,), (0,))


def _hg_gates(hq, hf, lb):
    sq = _sigmoid(hq)
    q = hq * sq * (HG_DK ** -0.5)
    f = _sigmoid(hf)
    g = lb + (1.0 - lb) * f
    return q, sq, f, g, jnp.log(g), 1.0 - g


def _tri(lower):
    r = lax.broadcasted_iota(jnp.int32, (CHUNK, CHUNK), 0)
    c = lax.broadcasted_iota(jnp.int32, (CHUNK, CHUNK), 1)
    return jnp.where((r >= c) if lower else (r <= c), 1.0, 0.0).astype(BF16)


def _running_sum(tri, x):
    return sum(jnp.dot(tri, piece, preferred_element_type=F32) for piece in _split3(x))


GROUP = 16
N_GROUPS = CHUNK // GROUP
BWD_CHUNKS_PER_TRIP = 4


def _dot_bf16(a, b, dims):
    return lax.dot_general(a.astype(BF16), b.astype(BF16), (dims, ((), ())), preferred_element_type=F32)


def _rows_iota():
    return lax.broadcasted_iota(jnp.int32, (CHUNK, HG_DK), 0)


def _by_query_group(q, kk, b, g):
    r0 = GROUP * g
    b0 = b[r0:r0 + 1]
    decay = jnp.exp(b[r0:r0 + GROUP] - b0)
    ks = jnp.where(_rows_iota() < r0, kk * jnp.exp(jnp.minimum(b0 - b, 0.0)), 0.0)
    return q[r0:r0 + GROUP] * decay, ks, decay


def _by_key_group(q, kk, b, j):
    r1 = GROUP * (j + 1)
    b1 = b[r1 - 1:r1]
    decay = jnp.exp(b1 - b[r1 - GROUP:r1])
    qs = jnp.where(_rows_iota() >= r1, q * jnp.exp(jnp.minimum(b - b1, 0.0)), 0.0)
    return qs, kk[r1 - GROUP:r1] * decay, decay


def _scores_between_groups(q, kk, b):
    blocks = [jnp.zeros((GROUP, CHUNK), F32)]
    for g in range(1, N_GROUPS):
        qs, ks, _ = _by_query_group(q, kk, b, g)
        blocks.append(_dot_bf16(qs, ks, NT))
    return jnp.concatenate(blocks, axis=0)


def _hgrn2_fwd(z, lb_logits, hg_norm_w, name, after=()):
    T = z.shape[0]
    n_chunks = T // CHUNK

    def body(hq_ref, hf_ref, hi_ref, hg_ref, lbl_ref, nw_ref, o_ref, ya_ref, sall_ref, st_ref):
        lbl = lbl_ref[...]
        lb = 1.0 / (1.0 + jnp.exp(lbl[1:2, :] - lbl[0:1, :]))
        st_ref[...] = jnp.zeros_like(st_ref)
        tri = _tri(True)
        row8 = lax.broadcasted_iota(jnp.int32, (8, HG_DK), 0)

        def chunk(c, carry):
            rows = pl.ds(pl.multiple_of(c * CHUNK, CHUNK), CHUNK)
            q, _, _, _, lg, kk = _hg_gates(hq_ref[rows, :], hf_ref[rows, :], lb)
            v = hi_ref[rows, :]
            b = _running_sum(tri, lg)
            st = st_ref[...]
            sall_ref[c] = st
            for grp in range(N_GROUPS):
                r0 = GROUP * grp
                for h8 in range(GROUP // 8):
                    n = 8 * (h8 + 1)
                    bs, ks, vs = b[r0:r0 + n], kk[r0:r0 + n], v[r0:r0 + n]
                    sidx = lax.broadcasted_iota(jnp.int32, (n, HG_DK), 0)
                    blk = jnp.zeros((8, HG_DK), F32)
                    for i in range(8):
                        t = r0 + 8 * h8 + i
                        e = jnp.where(sidx <= 8 * h8 + i, jnp.exp(b[t:t + 1] - bs), 0.0)
                        p = jnp.sum(e * ks * q[t:t + 1], axis=1, keepdims=True)
                        ot = jnp.sum(p * vs, axis=0, keepdims=True)
                        blk = blk + jnp.where(row8 == i, ot, 0.0)
                    o_ref[pl.ds(pl.multiple_of(c * CHUNK + r0 + 8 * h8, 8), 8), :] = blk
            o_ref[rows, :] += _dot_hi(q * jnp.exp(b), st, NT) + _dot_bf16(_scores_between_groups(q, kk, b), v, NN)
            bl = b[CHUNK - 1:CHUNK]
            ke = kk * jnp.exp(bl - b)
            st_ref[...] = st * jnp.exp(bl) + _dot_hi(v, ke, TN)
            return carry

        lax.fori_loop(0, n_chunks, chunk, 0, unroll=2)
        o = o_ref[...]
        r = lax.rsqrt(jnp.mean(o * o, axis=-1, keepdims=True) + EPS)
        hg = hg_ref[...]
        ya_ref[...] = (o * r * nw_ref[...] * (hg * _sigmoid(hg))).astype(BF16)

    def col(base):
        return pl.BlockSpec((T, HG_DK), lambda h: (0, base + h))

    return _call(
        body, (z, z, z, z, lb_logits, hg_norm_w), name=name, grid=(HG_HEADS,),
        in_specs=[col(COL_HQ), col(COL_HF), col(COL_HI), col(COL_HG),
                  pl.BlockSpec((2, HG_DK), lambda h: (0, h)), pl.BlockSpec((1, HG_DK), lambda h: (0, 0))],
        out_specs=[col(0), col(0), pl.BlockSpec((None, n_chunks, HG_DK, HG_DK), lambda h: (h, 0, 0, 0))],
        out_shape=[jax.ShapeDtypeStruct((T, HG_WIDTH), F32), jax.ShapeDtypeStruct((T, HG_WIDTH), BF16),
                   jax.ShapeDtypeStruct((HG_HEADS, n_chunks, HG_DK, HG_DK), F32)],
        scratch_shapes=[pltpu.VMEM((HG_DK, HG_DK), F32)],
        sem=("parallel",), after=after)


def _hgrn2_bwd(z, lb_logits, hg_norm_w, o_raw, s_all, dya, name, after=()):
    T = z.shape[0]
    n_chunks = T // CHUNK

    def body(hq_ref, hf_ref, hi_ref, hg_ref, lbl_ref, nw_ref, o_ref, sall_ref, dya_ref,
             dhq_ref, dhf_ref, dhi_ref, dhg_ref, dlbl_ref, dnw_ref,
             do_ref, dst_ref, dlb_ref, *per_chunk):
        h = pl.program_id(0)
        lbl = lbl_ref[...]
        lb = 1.0 / (1.0 + jnp.exp(lbl[1:2, :] - lbl[0:1, :]))

        o = o_ref[...]
        r = lax.rsqrt(jnp.mean(o * o, axis=-1, keepdims=True) + EPS)
        oh = o * r
        nw = nw_ref[...]
        hg = hg_ref[...]
        sg = _sigmoid(hg)
        dy = dya_ref[...]
        d_on = dy * (hg * sg)
        dhg_ref[...] = (dy * (oh * nw) * (sg * (1.0 + hg * (1.0 - sg)))).astype(BF16)
        dnw = jnp.sum(d_on * oh, axis=0, keepdims=True)
        gy = d_on * nw
        do_ref[...] = r * (gy - oh * jnp.mean(gy * oh, axis=-1, keepdims=True))

        @pl.when(h == 0)
        def _():
            dnw_ref[...] = jnp.zeros_like(dnw_ref)

        dnw_ref[...] += jnp.broadcast_to(dnw, dnw_ref.shape)

        dst_ref[...] = jnp.zeros_like(dst_ref)
        dlb_ref[...] = jnp.zeros_like(dlb_ref)
        tri = _tri(True)
        tri_t = _tri(False)
        row8 = lax.broadcasted_iota(jnp.int32, (8, HG_DK), 0)
        row_group = lax.broadcasted_iota(jnp.int32, (CHUNK, CHUNK), 0) // GROUP
        col_group = lax.broadcasted_iota(jnp.int32, (CHUNK, CHUNK), 1) // GROUP
        earlier_group = col_group < row_group
        later_group = col_group > row_group

        def chunk(c, dq_ref, dk_ref, dv_ref):
            rows = pl.ds(pl.multiple_of(c * CHUNK, CHUNK), CHUNK)
            hq = hq_ref[rows, :]
            q, sq, f, g, lg, kk = _hg_gates(hq, hf_ref[rows, :], lb)
            v = hi_ref[rows, :]
            do = do_ref[rows, :]
            b = _running_sum(tri, lg)
            eb = jnp.exp(b)
            bl = b[CHUNK - 1:CHUNK]
            ebl = jnp.exp(bl)
            ekb = jnp.exp(bl - b)
            qe = q * eb
            ke = kk * ekb
            st = sall_ref[c]
            dst = dst_ref[...]
            dqe = _dot_bf16(do, st, NN)
            dke = _dot_bf16(v, dst, NN)
            dv_inter = _dot_bf16(ke, dst, NT)
            d_ebl = jnp.sum(st * dst, axis=0, keepdims=True)
            dst_ref[...] = dst * ebl + _dot_bf16(do, qe, TN)

            dk_ref[...] = jnp.zeros_like(dk_ref)
            dv_ref[...] = jnp.zeros_like(dv_ref)
            for grp in range(N_GROUPS):
                r0 = GROUP * grp
                for h8 in range(GROUP // 8):
                    n = 8 * (h8 + 1)
                    bs, ks, vs = b[r0:r0 + n], kk[r0:r0 + n], v[r0:r0 + n]
                    sidx = lax.broadcasted_iota(jnp.int32, (n, HG_DK), 0)
                    blk = jnp.zeros((8, HG_DK), F32)
                    for i in range(8):
                        t = r0 + 8 * h8 + i
                        qt = q[t:t + 1]
                        dot_ = do[t:t + 1]
                        e = jnp.where(sidx <= 8 * h8 + i, jnp.exp(b[t:t + 1] - bs), 0.0)
                        w = e * ks
                        p = jnp.sum(w * qt, axis=1, keepdims=True)
                        dsc = jnp.sum(vs * dot_, axis=1, keepdims=True)
                        dqt = jnp.sum(dsc * w, axis=0, keepdims=True)
                        blk = blk + jnp.where(row8 == i, dqt, 0.0)
                        dk_ref[r0:r0 + n, :] += dsc * e * qt
                        dv_ref[r0:r0 + n, :] += p * dot_
                    dq_ref[r0 + 8 * h8:r0 + n, :] = blk
            ds_far = jnp.where(earlier_group, _dot_bf16(do, v, NT), 0.0)
            ds_far_t = jnp.where(later_group, _dot_bf16(v, do, NT), 0.0)
            dq_far, dk_far = [jnp.zeros((GROUP, HG_DK), F32)], []
            for grp in range(1, N_GROUPS):
                r0 = GROUP * grp
                _, ks, decay = _by_query_group(q, kk, b, grp)
                dq_far.append(decay * _dot_hi(ds_far[r0:r0 + GROUP], ks, NN))
                qs, _, decay = _by_key_group(q, kk, b, grp - 1)
                dk_far.append(decay * _dot_hi(ds_far_t[r0 - GROUP:r0], qs, NN))
            dk_far.append(jnp.zeros((GROUP, HG_DK), F32))
            dv_far = _dot_bf16(_scores_between_groups(q, kk, b), do, TN)
            dq_i = dq_ref[...] + jnp.concatenate(dq_far, axis=0)
            dk_i = dk_ref[...] + jnp.concatenate(dk_far, axis=0)
            dke_ke = dke * ke
            db = q * dq_i - kk * dk_i + dqe * qe - dke_ke
            db_last = jnp.sum(dke_ke, axis=0, keepdims=True) + d_ebl * ebl
            dlg = _running_sum(tri_t, db) + db_last
            dq = dq_i + dqe * eb
            dkk = dk_i + dke * ekb
            dg = dlg / g - dkk
            dhq_ref[rows, :] = (dq * (HG_DK ** -0.5) * (sq * (1.0 + hq * (1.0 - sq)))).astype(BF16)
            dhf_ref[rows, :] = (dg * (1.0 - lb) * f * (1.0 - f)).astype(BF16)
            dhi_ref[rows, :] = (dv_ref[...] + dv_far + dv_inter).astype(BF16)
            dlb_ref[...] += jnp.sum(dg * (1.0 - f), axis=0, keepdims=True)

        def trip(i, carry):
            for k in range(BWD_CHUNKS_PER_TRIP):
                chunk(n_chunks - 1 - k - BWD_CHUNKS_PER_TRIP * i, *per_chunk[3 * k:3 * k + 3])
            return carry

        lax.fori_loop(0, n_chunks // BWD_CHUNKS_PER_TRIP, trip, 0)
        dl0 = dlb_ref[...] * lb * (1.0 - lb)
        dlbl_ref[0:1, :] = dl0
        dlbl_ref[1:2, :] = -dl0

    def col(base):
        return pl.BlockSpec((T, HG_DK), lambda h: (0, base + h))

    outb = jax.ShapeDtypeStruct((T, HG_WIDTH), BF16)
    return _call(
        body, (z, z, z, z, lb_logits, hg_norm_w, o_raw, s_all, dya), name=name, grid=(HG_HEADS,),
        in_specs=[col(COL_HQ), col(COL_HF), col(COL_HI), col(COL_HG),
                  pl.BlockSpec((2, HG_DK), lambda h: (0, h)), pl.BlockSpec((1, HG_DK), lambda h: (0, 0)),
                  col(0), pl.BlockSpec((None, n_chunks, HG_DK, HG_DK), lambda h: (h, 0, 0, 0)), col(0)],
        out_specs=[col(0), col(0), col(0), col(0), pl.BlockSpec((2, HG_DK), lambda h: (0, h)),
                   pl.BlockSpec((8, HG_DK), lambda h: (0, 0))],
        out_shape=[outb, outb, outb, outb, jax.ShapeDtypeStruct((2, HG_WIDTH), F32),
                   jax.ShapeDtypeStruct((8, HG_DK), F32)],
        scratch_shapes=[pltpu.VMEM((T, HG_DK), F32), pltpu.VMEM((HG_DK, HG_DK), F32), pltpu.VMEM((1, HG_DK), F32)]
        + [pltpu.VMEM((CHUNK, HG_DK), F32)] * (3 * BWD_CHUNKS_PER_TRIP),
        sem=("arbitrary",), after=after)


CONST_KEYS = PAD - REL_CLIP
VAR_KEYS = BAND - CONST_KEYS
REL_LO = 128
REL_SPAN = N_REL_PAD - REL_LO


def _rel_onehot(t):
    r = lax.broadcasted_iota(jnp.int32, (REL_SPAN, VAR_KEYS), 0)
    j = lax.broadcasted_iota(jnp.int32, (REL_SPAN, VAR_KEYS), 1)
    idx = jnp.clip(t + PAD - CONST_KEYS - j, -REL_CLIP, REL_CLIP) + REL_CLIP - REL_LO
    return jnp.where(r == idx, 1.0, 0.0).astype(BF16)


def _split3(x):
    hi = x.astype(BF16)
    r1 = x - hi.astype(F32)
    mid = r1.astype(BF16)
    return hi, mid, (r1 - mid.astype(F32)).astype(BF16)


def _bias_expand(rel, name):
    def body(rel_ref, out_ref):
        tab = rel_ref[...]
        onehot = _rel_onehot(pl.program_id(0))
        out_ref[:, 0:CONST_KEYS] = jnp.broadcast_to(tab[:, 2 * REL_CLIP:2 * REL_CLIP + 1], (AT_HEADS, CONST_KEYS))
        out_ref[:, CONST_KEYS:BAND] = sum(
            jnp.dot(piece, onehot, preferred_element_type=F32) for piece in _split3(tab[:, REL_LO:N_REL_PAD]))

    return pl.pallas_call(
        body, name=name, grid=(CHUNK,),
        in_specs=[pl.BlockSpec((AT_HEADS, N_REL_PAD), lambda t: (0, 0))],
        out_specs=pl.BlockSpec((None, AT_HEADS, BAND), lambda t: (t, 0, 0)),
        out_shape=jax.ShapeDtypeStruct((CHUNK, AT_HEADS, BAND), F32),
        compiler_params=_cparams(("parallel",)),
    )(rel)


def _bias_reduce(dbias_rows, name, after=()):
    def body(db_ref, out_ref):
        lane = lax.broadcasted_iota(jnp.int32, (AT_HEADS, N_REL_PAD), 1)
        varying = lane >= CONST_KEYS
        by_offset = jnp.zeros((AT_HEADS, N_REL_PAD), F32)
        constant = jnp.zeros((AT_HEADS, N_REL_PAD), F32)
        for t in range(CHUNK):
            row = db_ref[t]
            constant = constant + jnp.where(varying, 0.0, row)
            moved = jnp.where(varying, row, 0.0)
            by_offset = by_offset + (pltpu.roll(moved, N_REL_PAD - t, axis=1) if t else moved)
        offset = lax.broadcasted_iota(jnp.int32, (N_REL_PAD, N_REL_PAD), 0)
        entry = lax.broadcasted_iota(jnp.int32, (N_REL_PAD, N_REL_PAD), 1)
        onehot = jnp.where(entry == jnp.clip(PAD - offset, -REL_CLIP, REL_CLIP) + REL_CLIP, 1.0, 0.0).astype(BF16)
        acc = sum(jnp.dot(piece, onehot, preferred_element_type=F32) for piece in _split3(by_offset))
        last = jnp.sum(constant, axis=1, keepdims=True)
        out_ref[...] = acc + jnp.where(lane == 2 * REL_CLIP, last, 0.0)

    whole = pl.BlockSpec((CHUNK, AT_HEADS, N_REL_PAD), lambda i: (0, 0, 0))
    return _call(
        body, (dbias_rows,), name=name, grid=(1,), in_specs=[whole],
        out_specs=[pl.BlockSpec((AT_HEADS, N_REL_PAD), lambda i: (0, 0))],
        out_shape=[jax.ShapeDtypeStruct((AT_HEADS, N_REL_PAD), F32)],
        sem=("arbitrary",), after=after)[0]


def _pair_lanes():
    return lax.broadcasted_iota(jnp.int32, (CHUNK, 2 * AT_DH), 1) < AT_DH


def _block_diag(a):
    first = _pair_lanes()
    return jnp.concatenate([jnp.where(first, a, 0.0), jnp.where(first, 0.0, a)], axis=0).astype(BF16)


def _diag_blocks(a):
    return jnp.where(_pair_lanes(), a[:CHUNK], a[CHUNK:])


def _band_probs_t(kb, qbd, bias_t, c):
    s = lax.dot_general(kb, qbd, (NT, ((), ())), preferred_element_type=F32) * (AT_DH ** -0.5) + bias_t
    j = lax.broadcasted_iota(jnp.int32, (BAND, 2 * AT_DH), 0)
    s = jnp.where(j + c * CHUNK >= PAD, s, -jnp.inf)
    p = jnp.exp(s - jnp.max(s, axis=0, keepdims=True))
    return p / jnp.sum(p, axis=0, keepdims=True)


def _attn_fwd(z, bias_t, name, after=()):
    T = z.shape[0]
    n_chunks = T // CHUNK

    def body(q_ref, k_ref, v_ref, bias_ref, y_ref, p_ref, *scratch):
        for pr in range(2):
            lanes = slice(128 * pr, 128 * (pr + 1))
            for dst_ref, src_ref in zip(scratch[2 * pr:2 * pr + 2], (k_ref, v_ref)):
                dst_ref[0:PAD, :] = jnp.zeros((PAD, 128), BF16)
                dst_ref[PAD:PAD + T, :] = src_ref[:, lanes].astype(BF16)

        def chunk(c, carry):
            rows = pl.ds(pl.multiple_of(c * CHUNK, CHUNK), CHUNK)
            band = pl.ds(pl.multiple_of(c * CHUNK, CHUNK), BAND)
            for pr in range(2):
                kp_ref, vp_ref = scratch[2 * pr:2 * pr + 2]
                lanes = slice(128 * pr, 128 * (pr + 1))
                p = _band_probs_t(kp_ref[band, :], _block_diag(q_ref[rows, lanes]), bias_ref[pr], c).astype(BF16)
                p_ref[pr, c] = p
                o2 = lax.dot_general(p, vp_ref[band, :], (TN, ((), ())), preferred_element_type=F32)
                y_ref[rows, lanes] = _diag_blocks(o2).astype(BF16)
            return carry

        lax.fori_loop(0, n_chunks, chunk, 0, unroll=2)

    def col(base):
        return pl.BlockSpec((T, 256), lambda h: (0, base // 2 + h))

    return _call(
        body, (z, z, z, bias_t), name=name, grid=(AT_HEADS // 4,),
        in_specs=[col(COL_AQ), col(COL_AK), col(COL_AV), pl.BlockSpec((2, BAND, 128), lambda h: (h, 0, 0))],
        out_specs=[col(0), pl.BlockSpec((2, n_chunks, BAND, 128), lambda h: (h, 0, 0, 0))],
        out_shape=[jax.ShapeDtypeStruct((T, AT_WIDTH), BF16),
                   jax.ShapeDtypeStruct((AT_HEADS // 2, n_chunks, BAND, 128), BF16)],
        scratch_shapes=[pltpu.VMEM((PAD + T, 128), BF16)] * 4,
        sem=("parallel",), after=after)


def _attn_bwd(z, probs, dyb, name, after=()):
    T = z.shape[0]
    n_chunks = T // CHUNK

    def body(q_ref, k_ref, v_ref, p_ref, dy_ref, dq_ref, dk_ref, dv_ref, dbias_ref, *scratch):
        dbias_ref[...] = jnp.zeros_like(dbias_ref)
        for pr in range(2):
            kp_ref, vp_ref, dkp_ref, dvp_ref = scratch[4 * pr:4 * pr + 4]
            lanes = slice(128 * pr, 128 * (pr + 1))
            kp_ref[0:PAD, :] = jnp.zeros((PAD, 128), BF16)
            vp_ref[0:PAD, :] = jnp.zeros((PAD, 128), BF16)
            kp_ref[PAD:PAD + T, :] = k_ref[:, lanes].astype(BF16)
            vp_ref[PAD:PAD + T, :] = v_ref[:, lanes].astype(BF16)
            dkp_ref[...] = jnp.zeros_like(dkp_ref)
            dvp_ref[...] = jnp.zeros_like(dvp_ref)

        def chunk(c, carry):
            rows = pl.ds(pl.multiple_of(c * CHUNK, CHUNK), CHUNK)
            band = pl.ds(pl.multiple_of(c * CHUNK, CHUNK), BAND)
            for pr in range(2):
                kp_ref, vp_ref, dkp_ref, dvp_ref = scratch[4 * pr:4 * pr + 4]
                lanes = slice(128 * pr, 128 * (pr + 1))
                qbd = _block_diag(q_ref[rows, lanes])
                dobd = _block_diag(dy_ref[rows, lanes])
                pb = p_ref[pr, c]
                p = pb.astype(F32)
                dp = lax.dot_general(vp_ref[band, :], dobd, (NT, ((), ())), preferred_element_type=F32)
                ds = p * (dp - jnp.sum(dp * p, axis=0, keepdims=True))
                dbias_ref[pr] += ds
                dsb = ds.astype(BF16)
                dq2 = lax.dot_general(dsb, kp_ref[band, :], (TN, ((), ())), preferred_element_type=F32)
                dq_ref[rows, lanes] = (_diag_blocks(dq2) * (AT_DH ** -0.5)).astype(BF16)
                dkp_ref[band, :] += jnp.dot(dsb, qbd, preferred_element_type=F32) * (AT_DH ** -0.5)
                dvp_ref[band, :] += jnp.dot(pb, dobd, preferred_element_type=F32)
            return carry

        lax.fori_loop(0, n_chunks, chunk, 0)
        for pr in range(2):
            lanes = slice(128 * pr, 128 * (pr + 1))
            dk_ref[:, lanes] = scratch[4 * pr + 2][PAD:PAD + T, :].astype(BF16)
            dv_ref[:, lanes] = scratch[4 * pr + 3][PAD:PAD + T, :].astype(BF16)

    def col(base):
        return pl.BlockSpec((T, 256), lambda h: (0, base // 2 + h))

    outb = jax.ShapeDtypeStruct((T, AT_WIDTH), BF16)
    return _call(
        body, (z, z, z, probs, dyb), name=name, grid=(AT_HEADS // 4,),
        in_specs=[col(COL_AQ), col(COL_AK), col(COL_AV),
                  pl.BlockSpec((2, n_chunks, BAND, 128), lambda h: (h, 0, 0, 0)), col(0)],
        out_specs=[col(0), col(0), col(0), pl.BlockSpec((2, BAND, 128), lambda h: (h, 0, 0))],
        out_shape=[outb, outb, outb, jax.ShapeDtypeStruct((AT_HEADS // 2, BAND, 128), F32)],
        scratch_shapes=([pltpu.VMEM((PAD + T, 128), BF16)] * 2 + [pltpu.VMEM((PAD + T, 128), F32)] * 2) * 2,
        sem=("parallel",), after=after)


def _local_step(x, target, lb_logits, hg_norm_w, rel_bias, norm_mix_w, norm_mlp_w, norm_final_w,
                w_in, rest, exchanges=None):
    ex = exchanges
    rel = jnp.pad(rel_bias, ((0, 0), (0, N_REL_PAD - N_REL)))

    u = _rms_fwd(x, norm_mix_w, "rms_mix_fwd")
    if ex:
        z, w_in = _mm_gathered(u, w_in, ex.order, "mm_in_fwd")
        gather = _Gather(rest[:3], [w_in], "ag")
        mlp_shards, _ = lax.optimization_barrier((rest[3:], gather.token))
        gather_mlp = _Gather([s.astype(BF16) for s in mlp_shards], [gather.token], "ag_mlp")
        z = _mm_gathered_tail(u, w_in, z, ex.order, "mm_in_fwd_tail", after=[gather_mlp.token])
        tok = []
    else:
        z = _mm_nn(u, w_in, F32, "mm_in_fwd")
        w_a, w_b, w_out, w_up, w_down = rest
        tok = []
    o_raw, y_a, s_all = _hgrn2_fwd(z, lb_logits, hg_norm_w, "hgrn2_fwd", after=tok)
    if ex:
        tok = [gather.pass_on([0, 1, 2], [o_raw], "abo")]
    bias_rows = _bias_expand(rel, "bias_expand")
    bias_t = jnp.transpose(bias_rows.reshape(CHUNK, AT_HEADS // 2, 2, BAND), (1, 3, 2, 0)).reshape(
        AT_HEADS // 2, BAND, 2 * CHUNK)
    y_b, probs = _attn_fwd(z, bias_t, "attn_fwd", after=tok)
    if ex:
        tok = [gather_mlp.pass_on([0], [y_b], "up")]
        w_a, w_b, w_out = gather.finish([0, 1, 2], tok, "abo")
    pa = _mm_nn(y_a, w_a, F32, "mm_a_fwd")
    pb, merged = _mm_nn(y_b, w_b, None, "mm_b_fwd", epilogue=(
        (z, z, pa), (COL_GATE_A * GATE_TILE, COL_GATE_B * GATE_TILE, 0), (F32, BF16), _gated_merge))
    w_out1 = w_out.reshape(1, D_MODEL, D_MODEL)
    h1, u2 = _mm_rows(merged, w_out.reshape(D_MODEL, D_MODEL), [x], [norm_mlp_w], (F32, BF16),
                      _residual_rms_rows, "mm_out_fwd")
    if ex:
        w_up, = gather_mlp.finish([0], [u2], "up")
    act = ((), (), (F32, BF16), _squared_relu)
    a, r = _mm_nn(u2, w_up, None, "mm_up_fwd_first", epilogue=act, blocks=(0, N_DEV // 2))
    tok = [gather_mlp.pass_on([1], [r], "down")] if ex else []
    a, r = _mm_nn(u2, w_up, None, "mm_up_fwd_second", after=tok, epilogue=act, blocks=(N_DEV // 2, N_DEV // 2),
                  into=(a, r))
    if ex:
        w_down, = gather_mlp.finish([1], [r], "down")
    w_down1 = w_down.reshape(1, D_FF, D_MODEL)
    mlp = _mm_nn(r, w_down1, F32, "mm_down_fwd")
    loss, dh2, dh2b, g_nf = _loss_head(h1, mlp, norm_final_w, target, "loss_head")

    own = ex.parity if ex else jnp.zeros((1,), jnp.int32)

    def sibling_half(weights, name, after=()):
        others = [_mm_tn_half(a_, g_, 1 - own, on, None, nm + "_sibling", after, *cols)
                  for a_, g_, on, nm, *cols in weights]
        rs = _ReduceScatter(others, name) if ex else None
        return rs, others, ([rs.token] if ex else [])

    def own_half(rs, weights, others, after):
        landed = rs.from_sibling(after) if ex else [None] * len(weights)
        sums = [_mm_tn_half(a_, g_, own, on, l, nm + "_own", (), *cols)
                for (a_, g_, on, nm, *cols), l in zip(weights, landed)]
        if ex:
            return [rs.scatter(sums)], None
        return [], [jnp.stack([s_, o_], axis=1).reshape((N_DEV,) + s_.shape[1:]) for s_, o_ in zip(sums, others)]

    down = [(r, dh2b, "a", "mm_down_wgrad")]
    rs_down, others, tok = sibling_half(down, "rs_down")
    da, = _mm_nt(dh2b, w_down1, None, "mm_down_dgrad", after=tok, epilogue=(
        (a,), (0,), (BF16,), lambda dr, av: (dr * (2.0 * jnp.maximum(av, 0.0)),)))
    tok, g_down = own_half(rs_down, down, others, [da])
    up = [(u2, da, "g", "mm_up_wgrad")]
    rs_up, others, tok = sibling_half(up, "rs_up", tok)
    du2 = _mm_nt(da, w_up, F32, "mm_up_dgrad", after=tok)
    tok, g_up = own_half(rs_up, up, others, [du2])
    dh1, dh1b, g_nmlp = _rms_bwd(du2, h1, norm_mlp_w, dh2, (F32, BF16), "rms_mlp_bwd", after=tok)

    dpa, dpb, dga, dgb = _mm_nt(dh1b, w_out1, None, "mm_out_dgrad", epilogue=(
        (z, z, pa, pb), (COL_GATE_A * GATE_TILE, COL_GATE_B * GATE_TILE, 0, 0), (BF16,) * 4, _merge_grads))
    mix = [(y_a, dpa, "g", "mm_a_wgrad"), (y_b, dpb, "g", "mm_b_wgrad"), (merged, dh1b, "a", "mm_out_wgrad")]
    rs_mix, others, tok = sibling_half(mix, "rs_mix")
    dya = _mm_nt(dpa, w_a, F32, "mm_a_dgrad", after=tok)
    dyb = _mm_nt(dpb, w_b, F32, "mm_b_dgrad", after=tok)
    tok, g_mix = own_half(rs_mix, mix, others, [dya, dyb])
    daq, dak, dav, dbias_t = _attn_bwd(z, probs, dyb, "attn_bwd", after=tok)
    dhq, dhf, dhi, dhg, g_lbl, g_hgw = _hgrn2_bwd(z, lb_logits, hg_norm_w, o_raw, s_all, dya, "hgrn2_bwd",
                                                  after=tok)
    dbias_rows = jnp.pad(jnp.transpose(dbias_t.reshape(AT_HEADS // 2, BAND, 2, CHUNK), (3, 0, 2, 1)).reshape(
        CHUNK, AT_HEADS, BAND), ((0, 0), (0, 0), (0, N_REL_PAD - BAND)))
    dz = jnp.concatenate([dhq, dhf, dhi, dhg, daq, dak, dav, dga, dgb], axis=1)
    half = D_MODEL // 2
    lo = [(u, dz, "g", "mm_in_wgrad_lo", (0, half))]
    hi = [(u, dz, "g", "mm_in_wgrad_hi", (half, half))]
    rs_in_lo, others_lo, tok = sibling_half(lo, "rs_in_lo")
    rs_in_hi, others_hi, tok = sibling_half(hi, "rs_in_hi", tok)
    tok, g_in_lo = own_half(rs_in_lo, lo, others_lo, tok)
    du = _mm_nt(dz, w_in, F32, "mm_in_dgrad", after=tok)
    tok, g_in_hi = own_half(rs_in_hi, hi, others_hi, [du])
    grad_x, g_nmix = _rms_bwd(du, x, norm_mix_w, dh1, (F32,), "rms_mix_bwd", after=tok)
    g_rel = _bias_reduce(dbias_rows, "bias_reduce", after=tok)[:, :N_REL]

    small = dict(lb_logits=g_lbl, hg_norm_w=g_hgw[0:1], rel_bias=g_rel, norm_mix_w=g_nmix, norm_mlp_w=g_nmlp,
                 norm_final_w=g_nf)
    if ex:
        grads = [(rs_in_lo, rs_in_hi), rs_mix, rs_up, rs_down]
    else:
        grads = [jnp.concatenate([g_in_lo[0], g_in_hi[0]], axis=1)] + g_mix + [g_up[0], g_down[0]]
    return loss, grad_x, grads, small


def _mm_gathered(u, shard, order, name):
    T, K = u.shape
    _, Nb = shard.shape

    def body(order_ref, u_ref, shard_ref, z_ref, full_ref, wbuf, load_sem, send_sems, recv_sems, local_sem):
        s = pl.program_id(0)
        x, y, c = _position()
        me, sibling = (x, y, c), (x, y, 1 - c)
        chips = [(1 - x, y), (x, 1 - y), (1 - x, 1 - y)]

        def copy(k, block, to, src=None):
            dst = full_ref.at[4 * block[0] + 2 * block[1] + block[2]]
            return pltpu.make_async_remote_copy(
                src_ref=dst if src is None else src, dst_ref=dst,
                send_sem=send_sems.at[k], recv_sem=recv_sems.at[k], device_id=to, device_id_type=MESH)

        @pl.when(s == 0)
        def _():
            local = pltpu.make_async_copy(shard_ref, full_ref.at[4 * x + 2 * y + c], local_sem)
            local.start()
            copy(0, me, sibling, src=shard_ref).start()
            for j, chip in enumerate(chips):
                copy(1 + j, me, (*chip, c), src=shard_ref).start()
            local.wait()

        @pl.when(s == 1)
        def _():
            copy(0, sibling, me).wait_recv()

        for j, chip in enumerate(chips):
            direct, passed = ((2, 4), (3, 5), (6, 7))[j]

            @pl.when(s == direct)
            def _(j=j, chip=chip):
                copy(1 + j, (*chip, c), me).wait_recv()
                copy(4 + j, (*chip, c), sibling).start()

            @pl.when(s == passed)
            def _(j=j, chip=chip):
                copy(4 + j, (*chip, 1 - c), me).wait_recv()

        @pl.when(s < N_EARLY_BLOCKS)
        def _():
            load = pltpu.make_async_copy(full_ref.at[order_ref[s]], wbuf, load_sem)
            load.start()
            load.wait()
            z_ref[...] = jnp.dot(u_ref[...], wbuf[...], preferred_element_type=F32)

        @pl.when(s == N_DEV - 1)
        def _():
            for k in range(7):
                copy(k, me, sibling).wait_send()

    z, full = pl.pallas_call(
        body, name=name,
        grid_spec=pltpu.PrefetchScalarGridSpec(
            num_scalar_prefetch=1, grid=(N_DEV,),
            in_specs=[pl.BlockSpec((T, K), lambda s, order: (0, 0)), ANY],
            out_specs=[pl.BlockSpec((T, Nb), lambda s, order: (0, order[jnp.minimum(s, N_EARLY_BLOCKS - 1)])), ANY],
            scratch_shapes=[pltpu.VMEM((K, Nb), BF16), pltpu.SemaphoreType.DMA,
                            pltpu.SemaphoreType.DMA((7,)), pltpu.SemaphoreType.DMA((7,)), pltpu.SemaphoreType.DMA]),
        out_shape=[jax.ShapeDtypeStruct((T, N_DEV * Nb), F32), jax.ShapeDtypeStruct((N_DEV, K, Nb), BF16)],
        compiler_params=_cparams(("arbitrary",)),
    )(order, u, shard)
    return z, full


N_EARLY_BLOCKS = 6


def _mm_gathered_tail(u, full, z, order, name, after=()):
    T, K = u.shape
    _, _, Nb = full.shape
    n_after = len(after)

    def body(order_ref, u_ref, w_ref, z_in_ref, *rest):
        rest[n_after][...] = jnp.dot(u_ref[...], w_ref[...], preferred_element_type=F32)

    return pl.pallas_call(
        body, name=name,
        grid_spec=pltpu.PrefetchScalarGridSpec(
            num_scalar_prefetch=1, grid=(N_DEV - N_EARLY_BLOCKS,),
            in_specs=[pl.BlockSpec((T, K), lambda s, order: (0, 0)),
                      pl.BlockSpec((None, K, Nb), lambda s, order: (order[N_EARLY_BLOCKS + s], 0, 0)), ANY]
            + [ANY] * n_after,
            out_specs=pl.BlockSpec((T, Nb), lambda s, order: (0, order[N_EARLY_BLOCKS + s]))),
        out_shape=jax.ShapeDtypeStruct(z.shape, z.dtype),
        input_output_aliases={3: 0},
        compiler_params=_cparams(("arbitrary",)),
    )(order, u, full, z, *after)


def _gather_order():
    x, y, c = _position()
    chips = [(1 - x, y), (x, 1 - y), (1 - x, 1 - y)]
    ids = [4 * x + 2 * y + c, 4 * x + 2 * y + (1 - c)]
    ids += [4 * cx + 2 * cy + c for cx, cy in chips[:2]] + [4 * cx + 2 * cy + (1 - c) for cx, cy in chips[:2]]
    ids += [4 * chips[2][0] + 2 * chips[2][1] + c, 4 * chips[2][0] + 2 * chips[2][1] + (1 - c)]
    return jnp.stack(ids).astype(jnp.int32)


HBM = pl.BlockSpec(memory_space=pltpu.HBM)
SEM = pl.BlockSpec(memory_space=pltpu.SEMAPHORE)
DATAFLOW = pltpu.SideEffectType.DATAFLOW_SIDE_EFFECTING


def _split_call(name, bufs, waits=(), starts=None, after=()):
    nb = len(bufs)
    n_new = starts[1] if starts else 0
    wait_sems = [s for w in waits for s in (*w[1], *w[2])]

    def body(*refs):
        b, pos = refs[:nb], nb
        for plan, ss, _, send_idx, recv_idx in waits:
            k = len(ss)
            copies = plan(b, refs[pos:pos + k], refs[pos + k:pos + 2 * k])
            pos += 2 * k
            for i in recv_idx:
                copies[i].wait_recv()
            for i in send_idx:
                copies[i].wait_send()
        outs = refs[pos + len(after):]
        if starts:
            for cp in starts[0](b, outs[nb:nb + n_new], outs[nb + n_new:nb + 2 * n_new]):
                cp.start()
        outs[-1][...] = jnp.zeros_like(outs[-1])

    res = pl.pallas_call(
        body, name=name,
        out_shape=tuple(pltpu.HBM(a.shape, a.dtype) for a in bufs) + (pltpu.SemaphoreType.DMA(()),) * (2 * n_new)
        + (jax.ShapeDtypeStruct((8, 128), F32),),
        in_specs=[HBM] * nb + [SEM] * len(wait_sems) + [ANY] * len(after),
        out_specs=(HBM,) * nb + (SEM,) * (2 * n_new) + (pl.BlockSpec(memory_space=pltpu.VMEM),),
        input_output_aliases={i: i for i in range(nb)},
        compiler_params=pltpu.CompilerParams(has_side_effects=DATAFLOW),
    )(*bufs, *wait_sems, *after)
    return list(res[:nb]), list(res[nb:nb + n_new]), list(res[nb + n_new:nb + 2 * n_new]), res[-1]


def _in_hbm(a):
    return pltpu.with_memory_space_constraint(a, pltpu.HBM)


def _remote(src, dst, send_sem, recv_sem, to):
    return pltpu.make_async_remote_copy(src_ref=src, dst_ref=dst, send_sem=send_sem, recv_sem=recv_sem,
                                        device_id=to, device_id_type=MESH)


def _other_chips():
    x, y, _ = _position()
    return [(1 - x, y), (x, 1 - y), (1 - x, 1 - y)]


def _plan_gather_first(n):
    def plan(b, ss, rs):
        x, y, c = _position()
        to = [(x, y, 1 - c)] + [(*chip, c) for chip in _other_chips()]
        return [_remote(b[w], b[n + w].at[4 * x + 2 * y + c], ss[4 * w + k], rs[4 * w + k], to[k])
                for w in range(n) for k in range(4)]
    return plan, 4 * n


def _plan_gather_pass(n):
    def plan(b, ss, rs):
        x, y, c = _position()
        copies = []
        for w in range(n):
            for j, chip in enumerate(_other_chips()):
                blk = b[n + w].at[4 * chip[0] + 2 * chip[1] + c]
                copies.append(_remote(blk, blk, ss[3 * w + j], rs[3 * w + j], (x, y, 1 - c)))
        return copies
    return plan, 3 * n


def _plan_sibling(n):
    def plan(b, ss, rs):
        x, y, c = _position()
        return [_remote(b[w].at[s], b[n + w].at[s], ss[4 * w + s], rs[4 * w + s], (x, y, 1 - c))
                for w in range(n) for s in range(N_CHIP)]
    return plan, 4 * n


def _plan_scatter(n):
    def plan(b, ss, rs):
        x, y, c = _position()
        return [_remote(b[w].at[2 * chip[0] + chip[1]], b[n + w].at[2 * x + y], ss[3 * w + j], rs[3 * w + j],
                        (*chip, c))
                for w in range(n) for j, chip in enumerate(_other_chips())]
    return plan, 3 * n


class _Gather:
    def __init__(self, shards, after, name):
        self.n, self.name = len(shards), name
        x, y, c = _position()
        placed = [lax.dynamic_update_index_in_dim(lax.empty((N_DEV,) + s.shape, s.dtype), s, 4 * x + 2 * y + c, 0)
                  for s in shards]
        bufs, self.ss, self.rs, self.token = _split_call(
            name + "_start", [_in_hbm(a) for a in list(shards) + placed], starts=_plan_gather_first(self.n),
            after=after)
        self.shards, self.fulls = bufs[:self.n], bufs[self.n:]
        self.passed = {}

    def _sub(self, ids, sems, per):
        return [sems[per * w + k] for w in ids for k in range(per)]

    def pass_on(self, ids, after, tag):
        m = len(ids)
        first = (_plan_gather_first(m)[0], self._sub(ids, self.ss, 4), self._sub(ids, self.rs, 4),
                 [], [4 * i + k for i in range(m) for k in (1, 2, 3)])
        bufs, ss, rs, token = _split_call(
            "%s_pass_%s" % (self.name, tag), [self.shards[w] for w in ids] + [self.fulls[w] for w in ids],
            waits=[first], starts=_plan_gather_pass(m), after=after)
        for i, w in enumerate(ids):
            self.shards[w], self.fulls[w] = bufs[i], bufs[m + i]
        self.passed[tuple(ids)] = (ss, rs)
        return token

    def finish(self, ids, after, tag):
        m = len(ids)
        ss2, rs2 = self.passed[tuple(ids)]
        first = (_plan_gather_first(m)[0], self._sub(ids, self.ss, 4), self._sub(ids, self.rs, 4),
                 list(range(4 * m)), [4 * i for i in range(m)])
        passed = (_plan_gather_pass(m)[0], ss2, rs2, list(range(3 * m)), list(range(3 * m)))
        bufs, _, _, _ = _split_call(
            "%s_finish_%s" % (self.name, tag), [self.shards[w] for w in ids] + [self.fulls[w] for w in ids],
            waits=[first, passed], after=after)
        return bufs[m:]


class _ReduceScatter:
    def __init__(self, others, name):
        self.n, self.name = len(others), name
        lands = [lax.empty(g.shape, g.dtype) for g in others]
        self.bufs, self.ss, self.rs, self.token = _split_call(
            name + "_sibling_start", [_in_hbm(a) for a in list(others) + lands], starts=_plan_sibling(self.n))

    def from_sibling(self, after):
        n = self.n
        bufs, _, _, _ = _split_call(
            self.name + "_sibling_wait", self.bufs,
            waits=[(_plan_sibling(n)[0], self.ss, self.rs, list(range(4 * n)), list(range(4 * n)))], after=after)
        return bufs[n:]

    def scatter(self, sums):
        lands = [lax.empty(s.shape, s.dtype) for s in sums]
        self.bufs, self.ss, self.rs, token = _split_call(
            self.name + "_scatter_start", [_in_hbm(a) for a in list(sums) + lands], starts=_plan_scatter(self.n))
        return token

    def finish(self, after):
        n = self.n
        bufs, _, _, _ = _split_call(
            self.name + "_scatter_wait", self.bufs,
            waits=[(_plan_scatter(n)[0], self.ss, self.rs, list(range(3 * n)), list(range(3 * n)))], after=after)
        return bufs[:n], bufs[n:]


class _Exchanges:
    def __init__(self, parity, order):
        self.parity, self.order = parity, order


def _plan_everyone():
    def plan(b, ss, rs):
        x, y, c = _position()
        return [_remote(b[0], b[1].at[4 * x + 2 * y + c], ss[k - 1], rs[k - 1],
                        (x ^ ((k >> 2) & 1), y ^ ((k >> 1) & 1), c ^ (k & 1))) for k in range(1, N_DEV)]
    return plan, N_DEV - 1


class _GatherSmall:
    def __init__(self, packed, after, name):
        self.name = name
        x, y, c = _position()
        placed = lax.dynamic_update_index_in_dim(lax.empty((N_DEV,) + packed.shape, packed.dtype), packed,
                                                 4 * x + 2 * y + c, 0)
        self.bufs, self.ss, self.rs, self.token = _split_call(
            name + "_start", [_in_hbm(packed), _in_hbm(placed)], starts=_plan_everyone(), after=after)

    def finish(self, after):
        everyone = list(range(N_DEV - 1))
        bufs, _, _, _ = _split_call(
            self.name + "_wait", self.bufs, waits=[(_plan_everyone()[0], self.ss, self.rs, everyone, everyone)],
            after=after)
        return bufs[1]


def _adamw_math(w, g, m, v):
    m = ADAM_B1 * m + (1.0 - ADAM_B1) * g
    v = ADAM_B2 * v + (1.0 - ADAM_B2) * (g * g)
    m_hat = m / (1.0 - ADAM_B1 ** ADAM_STEP)
    v_hat = v / (1.0 - ADAM_B2 ** ADAM_STEP)
    delta = -ADAM_LR * (m_hat / (jnp.sqrt(v_hat) + ADAM_EPS) + ADAM_WD * w)
    return delta, m, v


def _adamw_big_landed(w, m, v, parts, lands, slot, name, row0=0, into=None):
    R, C = w.shape
    rows = parts.shape[1]
    tr = _pick(rows, (256,))
    first = row0 // tr
    n_into = len(into) if into else 0

    def body(slot_ref, w_ref, m_ref, v_ref, own_ref, l1_ref, l2_ref, l3_ref, *rest):
        g = own_ref[...].astype(F32)
        for ref in (l1_ref, l2_ref, l3_ref):
            g = g + ref[...].astype(F32)
        for o_ref, res in zip(rest[n_into:], (g,) + _adamw_math(w_ref[...], g, m_ref[...], v_ref[...])):
            o_ref[...] = res

    blk = pl.BlockSpec((tr, C), lambda i, slot: (first + i, 0))

    def chip(k):
        return pl.BlockSpec((None, tr, C), lambda i, slot: ((slot[0] + k) % N_CHIP, i, 0))

    out = jax.ShapeDtypeStruct((R, C), F32)
    return pl.pallas_call(
        body, name=name,
        grid_spec=pltpu.PrefetchScalarGridSpec(
            num_scalar_prefetch=1, grid=(rows // tr,),
            in_specs=[blk, blk, blk, chip(0), chip(1), chip(2), chip(3)] + [ANY] * n_into,
            out_specs=[blk, blk, blk, blk]),
        out_shape=[out, out, out, out],
        input_output_aliases={8 + j: j for j in range(n_into)},
        compiler_params=_cparams(("parallel",)),
    )(slot, w, m, v, parts, lands, lands, lands, *(into or ()))


def _adamw_small(w, m, v, gathered, name):
    R = w.shape[0]

    def body(w_ref, m_ref, v_ref, p_ref, g_ref, d_ref, nm_ref, nv_ref):
        g = p_ref[0]
        for s in range(1, N_DEV):
            g = g + p_ref[s]
        d, nm, nv = _adamw_math(w_ref[...], g, m_ref[...], v_ref[...])
        g_ref[...] = g
        d_ref[...] = d
        nm_ref[...] = nm
        nv_ref[...] = nv

    out = jax.ShapeDtypeStruct((R, 128), F32)
    return pl.pallas_call(
        body, name=name, out_shape=[out, out, out, out],
    )(w, m, v, gathered)


SMALL_NAMES = ("lb_logits", "hg_norm_w", "rel_bias", "norm_mix_w", "norm_mlp_w", "norm_final_w")
SMALL_SHAPES = {"lb_logits": (2, HG_WIDTH), "hg_norm_w": (1, HG_DK), "rel_bias": (AT_HEADS, N_REL_PAD),
                "norm_mix_w": (1, D_MODEL), "norm_mlp_w": (1, D_MODEL), "norm_final_w": (1, D_MODEL)}


def _pack_small(parts):
    rows = []
    for nme in SMALL_NAMES:
        p = parts[nme]
        if nme == "rel_bias":
            p = jnp.pad(p, ((0, 0), (0, N_REL_PAD - N_REL)))
        rows.append(p.reshape(-1, 128))
    flat = jnp.concatenate(rows, axis=0)
    return jnp.pad(flat, ((0, SMALL_ROWS - flat.shape[0]), (0, 0)))


def _unpack_small(packed):
    out, at = {}, 0
    for nme in SMALL_NAMES:
        shp = SMALL_SHAPES[nme]
        nrow = shp[0] * shp[1] // 128
        p = packed[at:at + nrow].reshape(shp)
        at += nrow
        out[nme] = p[:, :N_REL] if nme == "rel_bias" else p
    return out


BIG_NAMES = ("w_in", "w_branch_a", "w_branch_b", "w_out", "w_up", "w_down")


def kernel(x, w_in, lb_logits, hg_norm_w, rel_bias, w_branch_a, w_branch_b, w_out, norm_mix_w, norm_mlp_w, w_up, w_down, norm_final_w, loss_target, m_w_in, m_lb_logits, m_hg_norm_w, m_rel_bias, m_w_branch_a, m_w_branch_b, m_w_out, m_norm_mix_w, m_norm_mlp_w, m_w_up, m_w_down, m_norm_final_w, v_w_in, v_lb_logits, v_hg_norm_w, v_rel_bias, v_w_branch_a, v_w_branch_b, v_w_out, v_norm_mix_w, v_norm_mlp_w, v_w_up, v_w_down, v_norm_final_w):
    big_w = [w_in[0], w_branch_a[0], w_branch_b[0], w_out[0], w_up[0], w_down[0]]
    big_m = [m_w_in[0], m_w_branch_a[0], m_w_branch_b[0], m_w_out[0], m_w_up[0], m_w_down[0]]
    big_v = [v_w_in[0], v_w_branch_a[0], v_w_branch_b[0], v_w_out[0], v_w_up[0], v_w_down[0]]

    shards = [w.astype(BF16) for w in big_w[:4]] + big_w[4:]
    parity = lax.axis_index("c").astype(jnp.int32).reshape(1)
    loss_part, grad_x, chip_parts, small = _local_step(
        x[0], loss_target[0], lb_logits, hg_norm_w, rel_bias[0], norm_mix_w, norm_mlp_w,
        norm_final_w.reshape(1, D_MODEL), shards[0], shards[1:], _Exchanges(parity, _gather_order()))
    loss = lax.psum(loss_part[0, 0], ("x", "y", "c"))
    (rs_in_lo, rs_in_hi), rs_mix, rs_up, rs_down = chip_parts
    slot =(2 * lax.axis_index("x") + lax.axis_index("y")).astype(jnp.int32).reshape(1)
    big = {}

    def finish(rs, names, after):
        sums, lands = rs.finish(after)
        for nme, own, land in zip(names, sums, lands):
            i = BIG_NAMES.index(nme)
            big[nme] = _adamw_big_landed(big_w[i], big_m[i], big_v[i], own, land, slot, "adamw_" + nme)
        return [big[nme][1] for nme in names]

    gather_small = _GatherSmall(_pack_small(small), [grad_x], "gather_small")
    done = finish(rs_down, ["w_down"], [grad_x, gather_small.token])
    done = finish(rs_up, ["w_up"], done)
    done = finish(rs_mix, ["w_branch_a", "w_branch_b", "w_out"], done)

    sw = dict(lb_logits=lb_logits, hg_norm_w=hg_norm_w, rel_bias=rel_bias[0], norm_mix_w=norm_mix_w,
              norm_mlp_w=norm_mlp_w, norm_final_w=norm_final_w.reshape(1, D_MODEL))
    sm = dict(lb_logits=m_lb_logits, hg_norm_w=m_hg_norm_w, rel_bias=m_rel_bias[0], norm_mix_w=m_norm_mix_w,
              norm_mlp_w=m_norm_mlp_w, norm_final_w=m_norm_final_w.reshape(1, D_MODEL))
    sv = dict(lb_logits=v_lb_logits, hg_norm_w=v_hg_norm_w, rel_bias=v_rel_bias[0], norm_mix_w=v_norm_mix_w,
              norm_mlp_w=v_norm_mlp_w, norm_final_w=v_norm_final_w.reshape(1, D_MODEL))
    gathered = gather_small.finish(done)
    small_packed = _adamw_small(_pack_small(sw), _pack_small(sm), _pack_small(sv), gathered, "adamw_small")
    small_out = [_unpack_small(p) for p in small_packed]

    (own,), (land,) = rs_in_lo.finish(done + [small_packed[0]])
    lo = _adamw_big_landed(big_w[0], big_m[0], big_v[0], own, land, slot, "adamw_w_in_lo")
    (own,), (land,) = rs_in_hi.finish([lo[1]])
    big["w_in"] = _adamw_big_landed(big_w[0], big_m[0], big_v[0], own, land, slot, "adamw_w_in_hi",
                                    row0=D_MODEL // 2, into=lo)

    def leaf(kind, nme):
        if nme in BIG_NAMES:
            return big[nme][kind][None]
        p = small_out[kind][nme]
        if nme == "rel_bias":
            return p[None]
        if nme == "norm_final_w":
            return p.reshape(D_MODEL)
        return p

    order = ("w_in", "lb_logits", "hg_norm_w", "rel_bias", "w_branch_a", "w_branch_b", "w_out", "norm_mix_w",
             "norm_mlp_w", "w_up", "w_down", "norm_final_w")
    outs = [loss, grad_x[None]]
    for kind in range(4):
        outs += [leaf(kind, nme) for nme in order]
    return tuple(outs)
```

```python
import jax
import jax.numpy as jnp
from jax import lax
from jax.experimental import pallas as pl
from jax.experimental.pallas import tpu as pltpu

F32 = jnp.float32
BF16 = jnp.bfloat16
HIGHEST = lax.Precision.HIGHEST
MESH = pl.DeviceIdType.MESH

D_MODEL = 2048
HG_HEADS = 8
HG_DK = 128
HG_WIDTH = 1024
AT_HEADS = 16
AT_DH = 64
AT_WIDTH = 1024
CHUNK = 64
LEFT_CHUNKS = 8
BAND = (LEFT_CHUNKS + 1) * CHUNK
PAD = LEFT_CHUNKS * CHUNK
REL_CLIP = 256
N_REL = 2 * REL_CLIP + 1
N_REL_PAD = 640
D_FF = 4 * D_MODEL
EPS = 1e-6
N_DEV = 8
N_CHIP = 4

ADAM_LR = 0.001
ADAM_B1 = 0.9
ADAM_B2 = 0.999
ADAM_EPS = 1e-08
ADAM_WD = 0.01
ADAM_STEP = 10

COL_HQ, COL_HF, COL_HI, COL_HG = 0, 8, 16, 24
COL_AQ, COL_AK, COL_AV = 32, 40, 48
COL_GATE_A, COL_GATE_B = 7, 9

VMEM_LIMIT = 56 * 1024 * 1024
SMALL_ROWS = 152


def _cparams(sem=None, **kw):
    if sem is not None:
        kw["dimension_semantics"] = sem
    return pltpu.CompilerParams(vmem_limit_bytes=VMEM_LIMIT, **kw)


def _pick(n, cands):
    for c in cands:
        if n % c == 0:
            return c
    return n


def _sigmoid(x):
    return 1.0 / (1.0 + jnp.exp(-x))


ANY = pl.BlockSpec(memory_space=pl.ANY)


def _position():
    return lax.axis_index("x"), lax.axis_index("y"), lax.axis_index("c")


def _call(body, args, *, name, grid, in_specs, out_specs, out_shape, scratch_shapes=(), sem=None, after=(),
          aliases=None):
    n_in = len(args)

    def ordered(*refs):
        body(*refs[:n_in], *refs[n_in + len(after):])

    return list(pl.pallas_call(
        ordered if after else body, name=name, grid=grid, in_specs=list(in_specs) + [ANY] * len(after),
        out_specs=out_specs, out_shape=out_shape, scratch_shapes=list(scratch_shapes),
        input_output_aliases=aliases or {}, compiler_params=_cparams(sem))(*args, *after))


MAX_CONTRACTION_TILE = 4096


def _accumulate(part, acc_ref, step, n_steps, finish):
    if n_steps == 1:
        finish(part)
        return

    @pl.when(step == 0)
    def _():
        acc_ref[...] = part

    @pl.when(step > 0)
    def _():
        acc_ref[...] += part

    @pl.when(step == n_steps - 1)
    def _():
        finish(acc_ref[...])


def _mm_nn(a, wb, out_dtype, name, after=(), epilogue=None, blocks=None, into=()):
    M, K = a.shape
    NB, K2, Nb = wb.shape
    assert K == K2
    j0, nj = blocks or (0, NB)
    n_into = len(into)
    tm = min(M, 1024)
    tk = min(K, MAX_CONTRACTION_TILE)
    tn = _pick(Nb, (512, 1408, 256))
    nk = K // tk
    nn = Nb // tn
    extra, first_cols, out_dtypes, fn = epilogue or ((), (), (out_dtype,), lambda total: (total,))
    n_extra, n_out = len(extra), len(out_dtypes)

    def body(a_ref, b_ref, *rest):
        def finish(total):
            results = fn(total, *[r[...] for r in rest[:n_extra]])
            for o_ref, res, dt in zip(rest[n_extra + n_into:n_extra + n_into + n_out], results, out_dtypes):
                o_ref[...] = res.astype(dt)

        part = jnp.dot(a_ref[...], b_ref[...], preferred_element_type=F32)
        _accumulate(part, rest[-1], pl.program_id(3), nk, finish)

    def tile(first):
        return pl.BlockSpec((tm, tn), lambda m, j, n, k: (m, first + (j0 + j) * nn + n))

    outs = _call(
        body, (a, wb) + tuple(extra) + tuple(into), name=name, grid=(M // tm, nj, nn, nk),
        in_specs=[pl.BlockSpec((tm, tk), lambda m, j, n, k: (m, k)),
                  pl.BlockSpec((None, tk, tn), lambda m, j, n, k: (j0 + j, k, n))]
        + [tile(col // tn) for col in first_cols] + [ANY] * n_into,
        out_specs=[tile(0)] * n_out,
        out_shape=[jax.ShapeDtypeStruct((M, NB * Nb), dt) for dt in out_dtypes],
        scratch_shapes=[] if nk == 1 else [pltpu.VMEM((tm, tn), F32)],
        sem=("parallel", "parallel", "parallel", "arbitrary"), after=after,
        aliases={2 + n_extra + i: i for i in range(n_into)})
    return outs if epilogue else outs[0]


def _squared_relu(a):
    ra = jnp.maximum(a, 0.0)
    return a, ra * ra


def _gated_merge(pb, za, zb, pa):
    return pb, _sigmoid(za) * pa + _sigmoid(zb) * pb


def _mm_nt(a, wb, out_dtype, name, after=(), epilogue=None):
    M, N = a.shape
    NB, K, Nb = wb.shape
    assert N == NB * Nb
    tm = min(M, 1024)
    n_tiles_live = 1 + (len(epilogue[0]) + len(epilogue[2]) if epilogue else 0)
    tko = _pick(K, (1024,)) if n_tiles_live <= 3 else _pick(K, (512,))
    tc = _pick(Nb, (2048, 1024, 1408, 256))
    nc = Nb // tc
    jb = max([d for d in (8, 4, 2, 1) if NB % d == 0 and d * tc <= MAX_CONTRACTION_TILE]) if nc == 1 else 1
    nsteps = (NB // jb) * nc
    extra, first_cols, out_dtypes, fn = epilogue or ((), (), (out_dtype,), lambda total: (total,))
    n_extra, n_out = len(extra), len(out_dtypes)

    def body(a_ref, b_ref, *rest):
        def finish(total):
            results = fn(total, *[r[...] for r in rest[:n_extra]])
            for o_ref, res, dt in zip(rest[n_extra:n_extra + n_out], results, out_dtypes):
                o_ref[...] = res.astype(dt)

        part = sum(lax.dot_general(a_ref[:, i * tc:(i + 1) * tc], b_ref[i], (((1,), (1,)), ((), ())),
                                   preferred_element_type=F32) for i in range(jb))
        _accumulate(part, rest[-1], pl.program_id(2) * nc + pl.program_id(3), nsteps, finish)

    def tile(first):
        return pl.BlockSpec((tm, tko), lambda m, ko, j, c: (m, first + ko))

    outs = _call(
        body, (a, wb) + tuple(extra), name=name,
        grid=(M // tm, K // tko, NB // jb, nc),
        in_specs=[pl.BlockSpec((tm, jb * tc), lambda m, ko, j, c: (m, j * nc + c)),
                  pl.BlockSpec((jb, tko, tc), lambda m, ko, j, c: (j, ko, c))] + [tile(col // tko) for col in first_cols],
        out_specs=[tile(0)] * n_out,
        out_shape=[jax.ShapeDtypeStruct((M, K), dt) for dt in out_dtypes],
        scratch_shapes=[] if nsteps == 1 else [pltpu.VMEM((tm, tko), F32)],
        sem=("parallel", "parallel", "arbitrary", "arbitrary"), after=after)
    return outs if epilogue else outs[0]


ROWS_TILE = 512
ROWS_PIECE = 128


def _mm_rows(a, w, extras, vectors, row_dtypes, fn, name):
    M, K = a.shape
    N = w.shape[1]
    tm = min(M, ROWS_TILE)
    n_e, n_v = len(extras), len(vectors)

    def body(a_ref, w_ref, *rest):
        tiles, vecs, outs, product_ref = rest[:n_e], rest[n_e:n_e + n_v], rest[n_e + n_v:-1], rest[-1]
        product_ref[...] = jnp.dot(a_ref[...], w_ref[...], preferred_element_type=F32)
        for i in range(tm // ROWS_PIECE):
            piece = slice(i * ROWS_PIECE, (i + 1) * ROWS_PIECE)
            results = fn(product_ref[piece, :], *[t[piece, :] for t in tiles], *[v[...] for v in vecs])
            for o_ref, res, dt in zip(outs, results, row_dtypes):
                o_ref[piece, :] = res.astype(dt)

    row = pl.BlockSpec((tm, N), lambda m: (m, 0))
    return _call(
        body, (a, w) + tuple(extras) + tuple(vectors), name=name, grid=(M // tm,),
        in_specs=[pl.BlockSpec((tm, K), lambda m: (m, 0)), pl.BlockSpec((K, N), lambda m: (0, 0))]
        + [row] * n_e + [pl.BlockSpec((1, N), lambda m: (0, 0))] * n_v,
        out_specs=[row] * len(row_dtypes),
        out_shape=[jax.ShapeDtypeStruct((M, N), dt) for dt in row_dtypes],
        scratch_shapes=[pltpu.VMEM((tm, N), F32)], sem=("parallel",))


def _rms(h, w):
    return h * lax.rsqrt(jnp.mean(h * h, axis=-1, keepdims=True) + EPS) * w


def _residual_rms_rows(mix, x, w):
    h = x + mix
    return h, _rms(h, w)


def _mm_tn_half(a, g, which, blocks_on, add, name, after=(), a_cols=None):
    M, Ka = a.shape
    N = g.shape[1]
    first_col = 0
    if a_cols is not None:
        first_col, Ka = a_cols
    if blocks_on == "g":
        rows, cols = _pick(Ka, (1024,)), N // N_DEV
        tn = _pick(cols, (512, 1408, 256))
        nn = cols // tn
        grid = (Ka // rows, N_CHIP, nn)
        a_spec = pl.BlockSpec((M, rows), lambda ka, s, n, w: (0, first_col // rows + ka))
        g_spec = pl.BlockSpec((M, tn), lambda ka, s, n, w: (0, (2 * s + w[0]) * nn + n))
        out_rows = Ka
    else:
        rows, cols = Ka // N_DEV, N
        tn = _pick(cols, (2048, 512))
        nn = cols // tn
        grid = (1, N_CHIP, nn)
        a_spec = pl.BlockSpec((M, rows), lambda ka, s, n, w: (0, 2 * s + w[0]))
        g_spec = pl.BlockSpec((M, tn), lambda ka, s, n, w: (0, n))
        out_rows = rows
    o_spec = pl.BlockSpec((None, rows, tn), lambda ka, s, n, w: (s, ka, n))
    n_add = 0 if add is None else 1

    def body(which_ref, a_ref, g_ref, *rest):
        acc = lax.dot_general(a_ref[...], g_ref[...], (((0,), (0,)), ((), ())), preferred_element_type=F32)
        if n_add:
            acc = acc + rest[0][...].astype(F32)
        rest[-1][...] = acc.astype(BF16)

    return pl.pallas_call(
        body, name=name,
        grid_spec=pltpu.PrefetchScalarGridSpec(
            num_scalar_prefetch=1, grid=grid,
            in_specs=[a_spec, g_spec] + [o_spec] * n_add + [ANY] * len(after),
            out_specs=o_spec),
        out_shape=jax.ShapeDtypeStruct((N_CHIP, out_rows, cols), BF16),
        compiler_params=_cparams(("parallel", "parallel", "parallel")),
    )(which, a, g, *(() if add is None else (add,)), *after)


ROW_TILE = 256


def _rms_fwd(x, w, name):
    T, Dm = x.shape

    def body(x_ref, w_ref, u_ref):
        xv = x_ref[...]
        r = lax.rsqrt(jnp.mean(xv * xv, axis=-1, keepdims=True) + EPS)
        u_ref[...] = (xv * r * w_ref[...]).astype(BF16)

    return pl.pallas_call(
        body, name=name, grid=(T // ROW_TILE,),
        in_specs=[pl.BlockSpec((ROW_TILE, Dm), lambda i: (i, 0)), pl.BlockSpec((1, Dm), lambda i: (0, 0))],
        out_specs=pl.BlockSpec((ROW_TILE, Dm), lambda i: (i, 0)),
        out_shape=jax.ShapeDtypeStruct((T, Dm), BF16),
        compiler_params=_cparams(("parallel",)),
    )(x, w)


def _loss_head(h1, mlp, wf, target, name):
    T, Dm = h1.shape

    def body(h_ref, m_ref, w_ref, t_ref, loss_ref, dh_ref, dhb_ref, dw_ref):
        i = pl.program_id(0)
        h = h_ref[...] + m_ref[...]
        r = lax.rsqrt(jnp.mean(h * h, axis=-1, keepdims=True) + EPS)
        xh = h * r
        wv = w_ref[...]
        e = xh * wv - t_ref[...]
        part = 0.5 * jnp.sum(jnp.mean(e * e, axis=-1, keepdims=True), axis=0, keepdims=True)
        dy = e * (1.0 / Dm)
        dw = jnp.sum(dy * xh, axis=0, keepdims=True)
        gy = dy * wv
        dh = r * (gy - xh * jnp.mean(gy * xh, axis=-1, keepdims=True))
        dh_ref[...] = dh
        dhb_ref[...] = dh.astype(BF16)

        @pl.when(i == 0)
        def _():
            loss_ref[...] = jnp.zeros_like(loss_ref)
            dw_ref[...] = jnp.zeros_like(dw_ref)

        loss_ref[...] += jnp.broadcast_to(part, loss_ref.shape)
        dw_ref[...] += dw

    row = pl.BlockSpec((ROW_TILE, Dm), lambda i: (i, 0))
    vec = pl.BlockSpec((1, Dm), lambda i: (0, 0))
    return pl.pallas_call(
        body, name=name, grid=(T // ROW_TILE,),
        in_specs=[row, row, vec, row],
        out_specs=[pl.BlockSpec((8, 128), lambda i: (0, 0)), row, row, vec],
        out_shape=[jax.ShapeDtypeStruct((8, 128), F32), jax.ShapeDtypeStruct((T, Dm), F32),
                   jax.ShapeDtypeStruct((T, Dm), BF16), jax.ShapeDtypeStruct((1, Dm), F32)],
        compiler_params=_cparams(("arbitrary",)),
    )(h1, mlp, wf, target)


def _rms_bwd(dyn, x, w, dres, dx_dtypes, name, after=()):
    T, Dm = x.shape
    n_dx = len(dx_dtypes)

    def body(g_ref, x_ref, w_ref, r_ref, *outs):
        i = pl.program_id(0)
        xv = x_ref[...]
        r = lax.rsqrt(jnp.mean(xv * xv, axis=-1, keepdims=True) + EPS)
        xh = xv * r
        g = g_ref[...]
        dw = jnp.sum(g * xh, axis=0, keepdims=True)
        gy = g * w_ref[...]
        dx = r_ref[...] + r * (gy - xh * jnp.mean(gy * xh, axis=-1, keepdims=True))
        for dx_ref, dt in zip(outs, dx_dtypes):
            dx_ref[...] = dx.astype(dt)
        dw_ref = outs[n_dx]

        @pl.when(i == 0)
        def _():
            dw_ref[...] = jnp.zeros_like(dw_ref)

        dw_ref[...] += dw

    row = pl.BlockSpec((ROW_TILE, Dm), lambda i: (i, 0))
    vec = pl.BlockSpec((1, Dm), lambda i: (0, 0))
    return _call(
        body, (dyn, x, w, dres), name=name, grid=(T // ROW_TILE,),
        in_specs=[row, row, vec, row],
        out_specs=[row] * n_dx + [vec],
        out_shape=[jax.ShapeDtypeStruct((T, Dm), dt) for dt in dx_dtypes] + [jax.ShapeDtypeStruct((1, Dm), F32)],
        sem=("arbitrary",), after=after)


GATE_TILE = 1024


def _merge_grads(d, za, zb, pa, pb):
    ga = _sigmoid(za)
    gb = _sigmoid(zb)
    return d * ga, d * gb, d * pa * ga * (1.0 - ga), d * pb * gb * (1.0 - gb)


def _dot_hi(a, b, dims):
    return lax.dot_general(a, b, (dims, ((), ())), precision=HIGHEST, preferred_element_type=F32)


NN = ((1,), (0,))
NT = ((1,), (1,))
TN = ((0,), (0,))


def _hg_gates(hq, hf, lb):
    sq = _sigmoid(hq)
    q = hq * sq * (HG_DK ** -0.5)
    f = _sigmoid(hf)
    g = lb + (1.0 - lb) * f
    return q, sq, f, g, jnp.log(g), 1.0 - g


def _tri(lower):
    r = lax.broadcasted_iota(jnp.int32, (CHUNK, CHUNK), 0)
    c = lax.broadcasted_iota(jnp.int32, (CHUNK, CHUNK), 1)
    return jnp.where((r >= c) if lower else (r <= c), 1.0, 0.0).astype(BF16)


def _running_sum(tri, x):
    return sum(jnp.dot(tri, piece, preferred_element_type=F32) for piece in _split3(x))


GROUP = 16
N_GROUPS = CHUNK // GROUP
BWD_CHUNKS_PER_TRIP = 4


def _dot_bf16(a, b, dims):
    return lax.dot_general(a.astype(BF16), b.astype(BF16), (dims, ((), ())), preferred_element_type=F32)


def _rows_iota():
    return lax.broadcasted_iota(jnp.int32, (CHUNK, HG_DK), 0)


def _by_query_group(q, kk, b, g):
    r0 = GROUP * g
    b0 = b[r0:r0 + 1]
    decay = jnp.exp(b[r0:r0 + GROUP] - b0)
    ks = jnp.where(_rows_iota() < r0, kk * jnp.exp(jnp.minimum(b0 - b, 0.0)), 0.0)
    return q[r0:r0 + GROUP] * decay, ks, decay


def _by_key_group(q, kk, b, j):
    r1 = GROUP * (j + 1)
    b1 = b[r1 - 1:r1]
    decay = jnp.exp(b1 - b[r1 - GROUP:r1])
    qs = jnp.where(_rows_iota() >= r1, q * jnp.exp(jnp.minimum(b - b1, 0.0)), 0.0)
    return qs, kk[r1 - GROUP:r1] * decay, decay


def _scores_between_groups(q, kk, b):
    blocks = [jnp.zeros((GROUP, CHUNK), F32)]
    for g in range(1, N_GROUPS):
        qs, ks, _ = _by_query_group(q, kk, b, g)
        blocks.append(_dot_bf16(qs, ks, NT))
    return jnp.concatenate(blocks, axis=0)


def _hgrn2_fwd(z, lb_logits, hg_norm_w, name, after=()):
    T = z.shape[0]
    n_chunks = T // CHUNK

    def body(hq_ref, hf_ref, hi_ref, hg_ref, lbl_ref, nw_ref, o_ref, ya_ref, sall_ref, st_ref):
        lbl = lbl_ref[...]
        lb = 1.0 / (1.0 + jnp.exp(lbl[1:2, :] - lbl[0:1, :]))
        st_ref[...] = jnp.zeros_like(st_ref)
        tri = _tri(True)
        row8 = lax.broadcasted_iota(jnp.int32, (8, HG_DK), 0)

        def chunk(c, carry):
            rows = pl.ds(pl.multiple_of(c * CHUNK, CHUNK), CHUNK)
            q, _, _, _, lg, kk = _hg_gates(hq_ref[rows, :], hf_ref[rows, :], lb)
            v = hi_ref[rows, :]
            b = _running_sum(tri, lg)
            st = st_ref[...]
            sall_ref[c] = st
            for grp in range(N_GROUPS):
                r0 = GROUP * grp
                for h8 in range(GROUP // 8):
                    n = 8 * (h8 + 1)
                    bs, ks, vs = b[r0:r0 + n], kk[r0:r0 + n], v[r0:r0 + n]
                    sidx = lax.broadcasted_iota(jnp.int32, (n, HG_DK), 0)
                    blk = jnp.zeros((8, HG_DK), F32)
                    for i in range(8):
                        t = r0 + 8 * h8 + i
                        e = jnp.where(sidx <= 8 * h8 + i, jnp.exp(b[t:t + 1] - bs), 0.0)
                        p = jnp.sum(e * ks * q[t:t + 1], axis=1, keepdims=True)
                        ot = jnp.sum(p * vs, axis=0, keepdims=True)
                        blk = blk + jnp.where(row8 == i, ot, 0.0)
                    o_ref[pl.ds(pl.multiple_of(c * CHUNK + r0 + 8 * h8, 8), 8), :] = blk
            o_ref[rows, :] += _dot_hi(q * jnp.exp(b), st, NT) + _dot_bf16(_scores_between_groups(q, kk, b), v, NN)
            bl = b[CHUNK - 1:CHUNK]
            ke = kk * jnp.exp(bl - b)
            st_ref[...] = st * jnp.exp(bl) + _dot_hi(v, ke, TN)
            return carry

        lax.fori_loop(0, n_chunks, chunk, 0, unroll=2)
        o = o_ref[...]
        r = lax.rsqrt(jnp.mean(o * o, axis=-1, keepdims=True) + EPS)
        hg = hg_ref[...]
        ya_ref[...] = (o * r * nw_ref[...] * (hg * _sigmoid(hg))).astype(BF16)

    def col(base):
        return pl.BlockSpec((T, HG_DK), lambda h: (0, base + h))

    return _call(
        body, (z, z, z, z, lb_logits, hg_norm_w), name=name, grid=(HG_HEADS,),
        in_specs=[col(COL_HQ), col(COL_HF), col(COL_HI), col(COL_HG),
                  pl.BlockSpec((2, HG_DK), lambda h: (0, h)), pl.BlockSpec((1, HG_DK), lambda h: (0, 0))],
        out_specs=[col(0), col(0), pl.BlockSpec((None, n_chunks, HG_DK, HG_DK), lambda h: (h, 0, 0, 0))],
        out_shape=[jax.ShapeDtypeStruct((T, HG_WIDTH), F32), jax.ShapeDtypeStruct((T, HG_WIDTH), BF16),
                   jax.ShapeDtypeStruct((HG_HEADS, n_chunks, HG_DK, HG_DK), F32)],
        scratch_shapes=[pltpu.VMEM((HG_DK, HG_DK), F32)],
        sem=("parallel",), after=after)


def _hgrn2_bwd(z, lb_logits, hg_norm_w, o_raw, s_all, dya, name, after=()):
    T = z.shape[0]
    n_chunks = T // CHUNK

    def body(hq_ref, hf_ref, hi_ref, hg_ref, lbl_ref, nw_ref, o_ref, sall_ref, dya_ref,
             dhq_ref, dhf_ref, dhi_ref, dhg_ref, dlbl_ref, dnw_ref,
             do_ref, dst_ref, dlb_ref, *per_chunk):
        h = pl.program_id(0)
        lbl = lbl_ref[...]
        lb = 1.0 / (1.0 + jnp.exp(lbl[1:2, :] - lbl[0:1, :]))

        o = o_ref[...]
        r = lax.rsqrt(jnp.mean(o * o, axis=-1, keepdims=True) + EPS)
        oh = o * r
        nw = nw_ref[...]
        hg = hg_ref[...]
        sg = _sigmoid(hg)
        dy = dya_ref[...]
        d_on = dy * (hg * sg)
        dhg_ref[...] = (dy * (oh * nw) * (sg * (1.0 + hg * (1.0 - sg)))).astype(BF16)
        dnw = jnp.sum(d_on * oh, axis=0, keepdims=True)
        gy = d_on * nw
        do_ref[...] = r * (gy - oh * jnp.mean(gy * oh, axis=-1, keepdims=True))

        @pl.when(h == 0)
        def _():
            dnw_ref[...] = jnp.zeros_like(dnw_ref)

        dnw_ref[...] += jnp.broadcast_to(dnw, dnw_ref.shape)

        dst_ref[...] = jnp.zeros_like(dst_ref)
        dlb_ref[...] = jnp.zeros_like(dlb_ref)
        tri = _tri(True)
        tri_t = _tri(False)
        row8 = lax.broadcasted_iota(jnp.int32, (8, HG_DK), 0)
        row_group = lax.broadcasted_iota(jnp.int32, (CHUNK, CHUNK), 0) // GROUP
        col_group = lax.broadcasted_iota(jnp.int32, (CHUNK, CHUNK), 1) // GROUP
        earlier_group = col_group < row_group
        later_group = col_group > row_group

        def chunk(c, dq_ref, dk_ref, dv_ref):
            rows = pl.ds(pl.multiple_of(c * CHUNK, CHUNK), CHUNK)
            hq = hq_ref[rows, :]
            q, sq, f, g, lg, kk = _hg_gates(hq, hf_ref[rows, :], lb)
            v = hi_ref[rows, :]
            do = do_ref[rows, :]
            b = _running_sum(tri, lg)
            eb = jnp.exp(b)
            bl = b[CHUNK - 1:CHUNK]
            ebl = jnp.exp(bl)
            ekb = jnp.exp(bl - b)
            qe = q * eb
            ke = kk * ekb
            st = sall_ref[c]
            dst = dst_ref[...]
            dqe = _dot_bf16(do, st, NN)
            dke = _dot_bf16(v, dst, NN)
            dv_inter = _dot_bf16(ke, dst, NT)
            d_ebl = jnp.sum(st * dst, axis=0, keepdims=True)
            dst_ref[...] = dst * ebl + _dot_bf16(do, qe, TN)

            dk_ref[...] = jnp.zeros_like(dk_ref)
            dv_ref[...] = jnp.zeros_like(dv_ref)
            for grp in range(N_GROUPS):
                r0 = GROUP * grp
                for h8 in range(GROUP // 8):
                    n = 8 * (h8 + 1)
                    bs, ks, vs = b[r0:r0 + n], kk[r0:r0 + n], v[r0:r0 + n]
                    sidx = lax.broadcasted_iota(jnp.int32, (n, HG_DK), 0)
                    blk = jnp.zeros((8, HG_DK), F32)
                    for i in range(8):
                        t = r0 + 8 * h8 + i
                        qt = q[t:t + 1]
                        dot_ = do[t:t + 1]
                        e = jnp.where(sidx <= 8 * h8 + i, jnp.exp(b[t:t + 1] - bs), 0.0)
                        w = e * ks
                        p = jnp.sum(w * qt, axis=1, keepdims=True)
                        dsc = jnp.sum(vs * dot_, axis=1, keepdims=True)
                        dqt = jnp.sum(dsc * w, axis=0, keepdims=True)
                        blk = blk + jnp.where(row8 == i, dqt, 0.0)
                        dk_ref[r0:r0 + n, :] += dsc * e * qt
                        dv_ref[r0:r0 + n, :] += p * dot_
                    dq_ref[r0 + 8 * h8:r0 + n, :] = blk
            ds_far = jnp.where(earlier_group, _dot_bf16(do, v, NT), 0.0)
            ds_far_t = jnp.where(later_group, _dot_bf16(v, do, NT), 0.0)
            dq_far, dk_far = [jnp.zeros((GROUP, HG_DK), F32)], []
            for grp in range(1, N_GROUPS):
                r0 = GROUP * grp
                _, ks, decay = _by_query_group(q, kk, b, grp)
                dq_far.append(decay * _dot_hi(ds_far[r0:r0 + GROUP], ks, NN))
                qs, _, decay = _by_key_group(q, kk, b, grp - 1)
                dk_far.append(decay * _dot_hi(ds_far_t[r0 - GROUP:r0], qs, NN))
            dk_far.append(jnp.zeros((GROUP, HG_DK), F32))
            dv_far = _dot_bf16(_scores_between_groups(q, kk, b), do, TN)
            dq_i = dq_ref[...] + jnp.concatenate(dq_far, axis=0)
            dk_i = dk_ref[...] + jnp.concatenate(dk_far, axis=0)
            dke_ke = dke * ke
            db = q * dq_i - kk * dk_i + dqe * qe - dke_ke
            db_last = jnp.sum(dke_ke, axis=0, keepdims=True) + d_ebl * ebl
            dlg = _running_sum(tri_t, db) + db_last
            dq = dq_i + dqe * eb
            dkk = dk_i + dke * ekb
            dg = dlg / g - dkk
            dhq_ref[rows, :] = (dq * (HG_DK ** -0.5) * (sq * (1.0 + hq * (1.0 - sq)))).astype(BF16)
            dhf_ref[rows, :] = (dg * (1.0 - lb) * f * (1.0 - f)).astype(BF16)
            dhi_ref[rows, :] = (dv_ref[...] + dv_far + dv_inter).astype(BF16)
            dlb_ref[...] += jnp.sum(dg * (1.0 - f), axis=0, keepdims=True)

        def trip(i, carry):
            for k in range(BWD_CHUNKS_PER_TRIP):
                chunk(n_chunks - 1 - k - BWD_CHUNKS_PER_TRIP * i, *per_chunk[3 * k:3 * k + 3])
            return carry

        lax.fori_loop(0, n_chunks // BWD_CHUNKS_PER_TRIP, trip, 0)
        dl0 = dlb_ref[...] * lb * (1.0 - lb)
        dlbl_ref[0:1, :] = dl0
        dlbl_ref[1:2, :] = -dl0

    def col(base):
        return pl.BlockSpec((T, HG_DK), lambda h: (0, base + h))

    outb = jax.ShapeDtypeStruct((T, HG_WIDTH), BF16)
    return _call(
        body, (z, z, z, z, lb_logits, hg_norm_w, o_raw, s_all, dya), name=name, grid=(HG_HEADS,),
        in_specs=[col(COL_HQ), col(COL_HF), col(COL_HI), col(COL_HG),
                  pl.BlockSpec((2, HG_DK), lambda h: (0, h)), pl.BlockSpec((1, HG_DK), lambda h: (0, 0)),
                  col(0), pl.BlockSpec((None, n_chunks, HG_DK, HG_DK), lambda h: (h, 0, 0, 0)), col(0)],
        out_specs=[col(0), col(0), col(0), col(0), pl.BlockSpec((2, HG_DK), lambda h: (0, h)),
                   pl.BlockSpec((8, HG_DK), lambda h: (0, 0))],
        out_shape=[outb, outb, outb, outb, jax.ShapeDtypeStruct((2, HG_WIDTH), F32),
                   jax.ShapeDtypeStruct((8, HG_DK), F32)],
        scratch_shapes=[pltpu.VMEM((T, HG_DK), F32), pltpu.VMEM((HG_DK, HG_DK), F32), pltpu.VMEM((1, HG_DK), F32)]
        + [pltpu.VMEM((CHUNK, HG_DK), F32)] * (3 * BWD_CHUNKS_PER_TRIP),
        sem=("arbitrary",), after=after)


CONST_KEYS = PAD - REL_CLIP
VAR_KEYS = BAND - CONST_KEYS
REL_LO = 128
REL_SPAN = N_REL_PAD - REL_LO


def _rel_onehot(t):
    r = lax.broadcasted_iota(jnp.int32, (REL_SPAN, VAR_KEYS), 0)
    j = lax.broadcasted_iota(jnp.int32, (REL_SPAN, VAR_KEYS), 1)
    idx = jnp.clip(t + PAD - CONST_KEYS - j, -REL_CLIP, REL_CLIP) + REL_CLIP - REL_LO
    return jnp.where(r == idx, 1.0, 0.0).astype(BF16)


def _split3(x):
    hi = x.astype(BF16)
    r1 = x - hi.astype(F32)
    mid = r1.astype(BF16)
    return hi, mid, (r1 - mid.astype(F32)).astype(BF16)


def _bias_expand(rel, name):
    rows = 8

    def body(rel_ref, out_ref):
        tab = rel_ref[...]
        pieces = _split3(tab[:, REL_LO:N_REL_PAD])
        constant = jnp.broadcast_to(tab[:, 2 * REL_CLIP:2 * REL_CLIP + 1], (AT_HEADS, CONST_KEYS))
        for i in range(rows):
            onehot = _rel_onehot(pl.program_id(0) * rows + i)
            out_ref[i, :, 0:CONST_KEYS] = constant
            out_ref[i, :, CONST_KEYS:BAND] = sum(jnp.dot(piece, onehot, preferred_element_type=F32)
                                                 for piece in pieces)

    return pl.pallas_call(
        body, name=name, grid=(CHUNK // rows,),
        in_specs=[pl.BlockSpec((AT_HEADS, N_REL_PAD), lambda t: (0, 0))],
        out_specs=pl.BlockSpec((rows, AT_HEADS, BAND), lambda t: (t, 0, 0)),
        out_shape=jax.ShapeDtypeStruct((CHUNK, AT_HEADS, BAND), F32),
        compiler_params=_cparams(("parallel",)),
    )(rel)


def _bias_reduce(dbias_rows, name, after=()):
    def body(db_ref, out_ref):
        lane = lax.broadcasted_iota(jnp.int32, (AT_HEADS, N_REL_PAD), 1)
        varying = lane >= CONST_KEYS
        by_offset = jnp.zeros((AT_HEADS, N_REL_PAD), F32)
        constant = jnp.zeros((AT_HEADS, N_REL_PAD), F32)
        for t in range(CHUNK):
            row = db_ref[t]
            constant = constant + jnp.where(varying, 0.0, row)
            moved = jnp.where(varying, row, 0.0)
            by_offset = by_offset + (pltpu.roll(moved, N_REL_PAD - t, axis=1) if t else moved)
        offset = lax.broadcasted_iota(jnp.int32, (N_REL_PAD, N_REL_PAD), 0)
        entry = lax.broadcasted_iota(jnp.int32, (N_REL_PAD, N_REL_PAD), 1)
        onehot = jnp.where(entry == jnp.clip(PAD - offset, -REL_CLIP, REL_CLIP) + REL_CLIP, 1.0, 0.0).astype(BF16)
        acc = sum(jnp.dot(piece, onehot, preferred_element_type=F32) for piece in _split3(by_offset))
        last = jnp.sum(constant, axis=1, keepdims=True)
        out_ref[...] = acc + jnp.where(lane == 2 * REL_CLIP, last, 0.0)

    whole = pl.BlockSpec((CHUNK, AT_HEADS, N_REL_PAD), lambda i: (0, 0, 0))
    return _call(
        body, (dbias_rows,), name=name, grid=(1,), in_specs=[whole],
        out_specs=[pl.BlockSpec((AT_HEADS, N_REL_PAD), lambda i: (0, 0))],
        out_shape=[jax.ShapeDtypeStruct((AT_HEADS, N_REL_PAD), F32)],
        sem=("arbitrary",), after=after)[0]


def _pair_lanes():
    return lax.broadcasted_iota(jnp.int32, (CHUNK, 2 * AT_DH), 1) < AT_DH


def _block_diag(a):
    first = _pair_lanes()
    return jnp.concatenate([jnp.where(first, a, 0.0), jnp.where(first, 0.0, a)], axis=0).astype(BF16)


def _diag_blocks(a):
    return jnp.where(_pair_lanes(), a[:CHUNK], a[CHUNK:])


def _band_probs_t(kb, qbd, bias_t, c):
    s = lax.dot_general(kb, qbd, (NT, ((), ())), preferred_element_type=F32) * (AT_DH ** -0.5) + bias_t
    j = lax.broadcasted_iota(jnp.int32, (BAND, 2 * AT_DH), 0)
    s = jnp.where(j + c * CHUNK >= PAD, s, -jnp.inf)
    p = jnp.exp(s - jnp.max(s, axis=0, keepdims=True))
    return p / jnp.sum(p, axis=0, keepdims=True)


def _attn_fwd(z, bias_t, name, after=()):
    T = z.shape[0]
    n_chunks = T // CHUNK

    def body(q_ref, k_ref, v_ref, bias_ref, y_ref, p_ref, *scratch):
        for pr in range(2):
            lanes = slice(128 * pr, 128 * (pr + 1))
            for dst_ref, src_ref in zip(scratch[2 * pr:2 * pr + 2], (k_ref, v_ref)):
                dst_ref[0:PAD, :] = jnp.zeros((PAD, 128), BF16)
                dst_ref[PAD:PAD + T, :] = src_ref[:, lanes].astype(BF16)

        def chunk(c, carry):
            rows = pl.ds(pl.multiple_of(c * CHUNK, CHUNK), CHUNK)
            band = pl.ds(pl.multiple_of(c * CHUNK, CHUNK), BAND)
            for pr in range(2):
                kp_ref, vp_ref = scratch[2 * pr:2 * pr + 2]
                lanes = slice(128 * pr, 128 * (pr + 1))
                p = _band_probs_t(kp_ref[band, :], _block_diag(q_ref[rows, lanes]), bias_ref[pr], c).astype(BF16)
                p_ref[pr, c] = p
                o2 = lax.dot_general(p, vp_ref[band, :], (TN, ((), ())), preferred_element_type=F32)
                y_ref[rows, lanes] = _diag_blocks(o2).astype(BF16)
            return carry

        lax.fori_loop(0, n_chunks, chunk, 0, unroll=2)

    def col(base):
        return pl.BlockSpec((T, 256), lambda h: (0, base // 2 + h))

    return _call(
        body, (z, z, z, bias_t), name=name, grid=(AT_HEADS // 4,),
        in_specs=[col(COL_AQ), col(COL_AK), col(COL_AV), pl.BlockSpec((2, BAND, 128), lambda h: (h, 0, 0))],
        out_specs=[col(0), pl.BlockSpec((2, n_chunks, BAND, 128), lambda h: (h, 0, 0, 0))],
        out_shape=[jax.ShapeDtypeStruct((T, AT_WIDTH), BF16),
                   jax.ShapeDtypeStruct((AT_HEADS // 2, n_chunks, BAND, 128), BF16)],
        scratch_shapes=[pltpu.VMEM((PAD + T, 128), BF16)] * 4,
        sem=("parallel",), after=after)


def _attn_bwd(z, probs, dyb, name, after=()):
    T = z.shape[0]
    n_chunks = T // CHUNK

    def body(q_ref, k_ref, v_ref, p_ref, dy_ref, dq_ref, dk_ref, dv_ref, dbias_ref, *scratch):
        dbias_ref[...] = jnp.zeros_like(dbias_ref)
        for pr in range(2):
            kp_ref, vp_ref, dkp_ref, dvp_ref = scratch[4 * pr:4 * pr + 4]
            lanes = slice(128 * pr, 128 * (pr + 1))
            kp_ref[0:PAD, :] = jnp.zeros((PAD, 128), BF16)
            vp_ref[0:PAD, :] = jnp.zeros((PAD, 128), BF16)
            kp_ref[PAD:PAD + T, :] = k_ref[:, lanes].astype(BF16)
            vp_ref[PAD:PAD + T, :] = v_ref[:, lanes].astype(BF16)
            dkp_ref[...] = jnp.zeros_like(dkp_ref)
            dvp_ref[...] = jnp.zeros_like(dvp_ref)

        def chunk(c, carry):
            rows = pl.ds(pl.multiple_of(c * CHUNK, CHUNK), CHUNK)
            band = pl.ds(pl.multiple_of(c * CHUNK, CHUNK), BAND)
            for pr in range(2):
                kp_ref, vp_ref, dkp_ref, dvp_ref = scratch[4 * pr:4 * pr + 4]
                lanes = slice(128 * pr, 128 * (pr + 1))
                qbd = _block_diag(q_ref[rows, lanes])
                dobd = _block_diag(dy_ref[rows, lanes])
                pb = p_ref[pr, c]
                p = pb.astype(F32)
                dp = lax.dot_general(vp_ref[band, :], dobd, (NT, ((), ())), preferred_element_type=F32)
                ds = p * (dp - jnp.sum(dp * p, axis=0, keepdims=True))
                dbias_ref[pr] += ds
                dsb = ds.astype(BF16)
                dq2 = lax.dot_general(dsb, kp_ref[band, :], (TN, ((), ())), preferred_element_type=F32)
                dq_ref[rows, lanes] = (_diag_blocks(dq2) * (AT_DH ** -0.5)).astype(BF16)
                dkp_ref[band, :] += jnp.dot(dsb, qbd, preferred_element_type=F32) * (AT_DH ** -0.5)
                dvp_ref[band, :] += jnp.dot(pb, dobd, preferred_element_type=F32)
            return carry

        lax.fori_loop(0, n_chunks, chunk, 0)
        for pr in range(2):
            lanes = slice(128 * pr, 128 * (pr + 1))
            dk_ref[:, lanes] = scratch[4 * pr + 2][PAD:PAD + T, :].astype(BF16)
            dv_ref[:, lanes] = scratch[4 * pr + 3][PAD:PAD + T, :].astype(BF16)

    def col(base):
        return pl.BlockSpec((T, 256), lambda h: (0, base // 2 + h))

    outb = jax.ShapeDtypeStruct((T, AT_WIDTH), BF16)
    return _call(
        body, (z, z, z, probs, dyb), name=name, grid=(AT_HEADS // 4,),
        in_specs=[col(COL_AQ), col(COL_AK), col(COL_AV),
                  pl.BlockSpec((2, n_chunks, BAND, 128), lambda h: (h, 0, 0, 0)), col(0)],
        out_specs=[col(0), col(0), col(0), pl.BlockSpec((2, BAND, 128), lambda h: (h, 0, 0))],
        out_shape=[outb, outb, outb, jax.ShapeDtypeStruct((AT_HEADS // 2, BAND, 128), F32)],
        scratch_shapes=([pltpu.VMEM((PAD + T, 128), BF16)] * 2 + [pltpu.VMEM((PAD + T, 128), F32)] * 2) * 2,
        sem=("parallel",), after=after)


def _local_step(x, target, lb_logits, hg_norm_w, rel_bias, norm_mix_w, norm_mlp_w, norm_final_w,
                w_in, rest, exchanges=None):
    ex = exchanges
    rel = jnp.pad(rel_bias, ((0, 0), (0, N_REL_PAD - N_REL)))

    u = _rms_fwd(x, norm_mix_w, "rms_mix_fwd")
    if ex:
        z, w_in = _mm_gathered(u, w_in, ex.order, "mm_in_fwd")
        gather = _Gather(rest[:3], [w_in], "ag")
        mlp_shards, _ = lax.optimization_barrier((rest[3:], gather.token))
        gather_mlp = _Gather([s.astype(BF16) for s in mlp_shards], [gather.token], "ag_mlp")
        z = _mm_gathered_tail(u, w_in, z, ex.order, "mm_in_fwd_tail", after=[gather_mlp.token])
        tok = []
    else:
        z = _mm_nn(u, w_in, F32, "mm_in_fwd")
        w_a, w_b, w_out, w_up, w_down = rest
        tok = []
    o_raw, y_a, s_all = _hgrn2_fwd(z, lb_logits, hg_norm_w, "hgrn2_fwd", after=tok)
    if ex:
        tok = [gather.pass_on([0, 1, 2], [o_raw], "abo")]
    bias_rows = _bias_expand(rel, "bias_expand")
    bias_t = jnp.transpose(bias_rows.reshape(CHUNK, AT_HEADS // 2, 2, BAND), (1, 3, 2, 0)).reshape(
        AT_HEADS // 2, BAND, 2 * CHUNK)
    y_b, probs = _attn_fwd(z, bias_t, "attn_fwd", after=tok)
    if ex:
        tok = [gather_mlp.pass_on([0], [y_b], "up")]
        w_a, w_b, w_out = gather.finish([0, 1, 2], tok, "abo")
    pa = _mm_nn(y_a, w_a, F32, "mm_a_fwd")
    pb, merged = _mm_nn(y_b, w_b, None, "mm_b_fwd", epilogue=(
        (z, z, pa), (COL_GATE_A * GATE_TILE, COL_GATE_B * GATE_TILE, 0), (F32, BF16), _gated_merge))
    w_out1 = w_out.reshape(1, D_MODEL, D_MODEL)
    h1, u2 = _mm_rows(merged, w_out.reshape(D_MODEL, D_MODEL), [x], [norm_mlp_w], (F32, BF16),
                      _residual_rms_rows, "mm_out_fwd")
    if ex:
        w_up, = gather_mlp.finish([0], [u2], "up")
    act = ((), (), (F32, BF16), _squared_relu)
    a, r = _mm_nn(u2, w_up, None, "mm_up_fwd_first", epilogue=act, blocks=(0, N_DEV // 2))
    tok = [gather_mlp.pass_on([1], [r], "down")] if ex else []
    a, r = _mm_nn(u2, w_up, None, "mm_up_fwd_second", after=tok, epilogue=act, blocks=(N_DEV // 2, N_DEV // 2),
                  into=(a, r))
    if ex:
        w_down, = gather_mlp.finish([1], [r], "down")
    w_down1 = w_down.reshape(1, D_FF, D_MODEL)
    mlp = _mm_nn(r, w_down1, F32, "mm_down_fwd")
    loss, dh2, dh2b, g_nf = _loss_head(h1, mlp, norm_final_w, target, "loss_head")

    own = ex.parity if ex else jnp.zeros((1,), jnp.int32)

    def sibling_half(weights, name, after=()):
        others = [_mm_tn_half(a_, g_, 1 - own, on, None, nm + "_sibling", after, *cols)
                  for a_, g_, on, nm, *cols in weights]
        rs = _ReduceScatter(others, name) if ex else None
        return rs, others, ([rs.token] if ex else [])

    def own_half(rs, weights, others, after):
        landed = rs.from_sibling(after) if ex else [None] * len(weights)
        sums = [_mm_tn_half(a_, g_, own, on, l, nm + "_own", (), *cols)
                for (a_, g_, on, nm, *cols), l in zip(weights, landed)]
        if ex:
            return [rs.scatter(sums)], None
        return [], [jnp.stack([s_, o_], axis=1).reshape((N_DEV,) + s_.shape[1:]) for s_, o_ in zip(sums, others)]

    down = [(r, dh2b, "a", "mm_down_wgrad")]
    rs_down, others, tok = sibling_half(down, "rs_down")
    da, = _mm_nt(dh2b, w_down1, None, "mm_down_dgrad", after=tok, epilogue=(
        (a,), (0,), (BF16,), lambda dr, av: (dr * (2.0 * jnp.maximum(av, 0.0)),)))
    tok, g_down = own_half(rs_down, down, others, [da])
    up = [(u2, da, "g", "mm_up_wgrad")]
    rs_up, others, tok = sibling_half(up, "rs_up", tok)
    du2 = _mm_nt(da, w_up, F32, "mm_up_dgrad", after=tok)
    tok, g_up = own_half(rs_up, up, others, [du2])
    dh1, dh1b, g_nmlp = _rms_bwd(du2, h1, norm_mlp_w, dh2, (F32, BF16), "rms_mlp_bwd", after=tok)

    dpa, dpb, dga, dgb = _mm_nt(dh1b, w_out1, None, "mm_out_dgrad", epilogue=(
        (z, z, pa, pb), (COL_GATE_A * GATE_TILE, COL_GATE_B * GATE_TILE, 0, 0), (BF16,) * 4, _merge_grads))
    mix = [(y_a, dpa, "g", "mm_a_wgrad"), (y_b, dpb, "g", "mm_b_wgrad"), (merged, dh1b, "a", "mm_out_wgrad")]
    rs_mix, others, tok = sibling_half(mix, "rs_mix")
    dya = _mm_nt(dpa, w_a, F32, "mm_a_dgrad", after=tok)
    dyb = _mm_nt(dpb, w_b, F32, "mm_b_dgrad", after=tok)
    tok, g_mix = own_half(rs_mix, mix, others, [dya, dyb])
    daq, dak, dav, dbias_t = _attn_bwd(z, probs, dyb, "attn_bwd", after=tok)
    dhq, dhf, dhi, dhg, g_lbl, g_hgw = _hgrn2_bwd(z, lb_logits, hg_norm_w, o_raw, s_all, dya, "hgrn2_bwd",
                                                  after=tok)
    dbias_rows = jnp.pad(jnp.transpose(dbias_t.reshape(AT_HEADS // 2, BAND, 2, CHUNK), (3, 0, 2, 1)).reshape(
        CHUNK, AT_HEADS, BAND), ((0, 0), (0, 0), (0, N_REL_PAD - BAND)))
    dz = jnp.concatenate([dhq, dhf, dhi, dhg, daq, dak, dav, dga, dgb], axis=1)
    half = D_MODEL // 2
    lo = [(u, dz, "g", "mm_in_wgrad_lo", (0, half))]
    hi = [(u, dz, "g", "mm_in_wgrad_hi", (half, half))]
    rs_in_lo, others_lo, tok = sibling_half(lo, "rs_in_lo")
    rs_in_hi, others_hi, tok = sibling_half(hi, "rs_in_hi", tok)
    tok, g_in_lo = own_half(rs_in_lo, lo, others_lo, tok)
    du = _mm_nt(dz, w_in, F32, "mm_in_dgrad", after=tok)
    tok, g_in_hi = own_half(rs_in_hi, hi, others_hi, [du])
    grad_x, g_nmix = _rms_bwd(du, x, norm_mix_w, dh1, (F32,), "rms_mix_bwd", after=tok)
    g_rel = _bias_reduce(dbias_rows, "bias_reduce", after=tok)[:, :N_REL]

    small = dict(lb_logits=g_lbl, hg_norm_w=g_hgw[0:1], rel_bias=g_rel, norm_mix_w=g_nmix, norm_mlp_w=g_nmlp,
                 norm_final_w=g_nf)
    if ex:
        grads = [(rs_in_lo, rs_in_hi), rs_mix, rs_up, rs_down]
    else:
        grads = [jnp.concatenate([g_in_lo[0], g_in_hi[0]], axis=1)] + g_mix + [g_up[0], g_down[0]]
    return loss, grad_x, grads, small


def _mm_gathered(u, shard, order, name):
    T, K = u.shape
    _, Nb = shard.shape

    def body(order_ref, u_ref, shard_ref, z_ref, full_ref, wbuf, load_sem, send_sems, recv_sems, local_sem):
        s = pl.program_id(0)
        x, y, c = _position()
        me, sibling = (x, y, c), (x, y, 1 - c)
        chips = [(1 - x, y), (x, 1 - y), (1 - x, 1 - y)]

        def copy(k, block, to, src=None):
            dst = full_ref.at[4 * block[0] + 2 * block[1] + block[2]]
            return pltpu.make_async_remote_copy(
                src_ref=dst if src is None else src, dst_ref=dst,
                send_sem=send_sems.at[k], recv_sem=recv_sems.at[k], device_id=to, device_id_type=MESH)

        @pl.when(s == 0)
        def _():
            local = pltpu.make_async_copy(shard_ref, full_ref.at[4 * x + 2 * y + c], local_sem)
            local.start()
            copy(0, me, sibling, src=shard_ref).start()
            for j, chip in enumerate(chips):
                copy(1 + j, me, (*chip, c), src=shard_ref).start()
            local.wait()

        @pl.when(s == 1)
        def _():
            copy(0, sibling, me).wait_recv()

        for j, chip in enumerate(chips):
            direct, passed = ((2, 4), (3, 5), (6, 7))[j]

            @pl.when(s == direct)
            def _(j=j, chip=chip):
                copy(1 + j, (*chip, c), me).wait_recv()
                copy(4 + j, (*chip, c), sibling).start()

            @pl.when(s == passed)
            def _(j=j, chip=chip):
                copy(4 + j, (*chip, 1 - c), me).wait_recv()

        @pl.when(s < N_EARLY_BLOCKS)
        def _():
            load = pltpu.make_async_copy(full_ref.at[order_ref[s]], wbuf, load_sem)
            load.start()
            load.wait()
            z_ref[...] = jnp.dot(u_ref[...], wbuf[...], preferred_element_type=F32)

        @pl.when(s == N_DEV - 1)
        def _():
            for k in range(7):
                copy(k, me, sibling).wait_send()

    z, full = pl.pallas_call(
        body, name=name,
        grid_spec=pltpu.PrefetchScalarGridSpec(
            num_scalar_prefetch=1, grid=(N_DEV,),
            in_specs=[pl.BlockSpec((T, K), lambda s, order: (0, 0)), ANY],
            out_specs=[pl.BlockSpec((T, Nb), lambda s, order: (0, order[jnp.minimum(s, N_EARLY_BLOCKS - 1)])), ANY],
            scratch_shapes=[pltpu.VMEM((K, Nb), BF16), pltpu.SemaphoreType.DMA,
                            pltpu.SemaphoreType.DMA((7,)), pltpu.SemaphoreType.DMA((7,)), pltpu.SemaphoreType.DMA]),
        out_shape=[jax.ShapeDtypeStruct((T, N_DEV * Nb), F32), jax.ShapeDtypeStruct((N_DEV, K, Nb), BF16)],
        compiler_params=_cparams(("arbitrary",)),
    )(order, u, shard)
    return z, full


N_EARLY_BLOCKS = 6


def _mm_gathered_tail(u, full, z, order, name, after=()):
    T, K = u.shape
    _, _, Nb = full.shape
    n_after = len(after)

    def body(order_ref, u_ref, w_ref, z_in_ref, *rest):
        rest[n_after][...] = jnp.dot(u_ref[...], w_ref[...], preferred_element_type=F32)

    return pl.pallas_call(
        body, name=name,
        grid_spec=pltpu.PrefetchScalarGridSpec(
            num_scalar_prefetch=1, grid=(N_DEV - N_EARLY_BLOCKS,),
            in_specs=[pl.BlockSpec((T, K), lambda s, order: (0, 0)),
                      pl.BlockSpec((None, K, Nb), lambda s, order: (order[N_EARLY_BLOCKS + s], 0, 0)), ANY]
            + [ANY] * n_after,
            out_specs=pl.BlockSpec((T, Nb), lambda s, order: (0, order[N_EARLY_BLOCKS + s]))),
        out_shape=jax.ShapeDtypeStruct(z.shape, z.dtype),
        input_output_aliases={3: 0},
        compiler_params=_cparams(("arbitrary",)),
    )(order, u, full, z, *after)


def _gather_order():
    x, y, c = _position()
    chips = [(1 - x, y), (x, 1 - y), (1 - x, 1 - y)]
    ids = [4 * x + 2 * y + c, 4 * x + 2 * y + (1 - c)]
    ids += [4 * cx + 2 * cy + c for cx, cy in chips[:2]] + [4 * cx + 2 * cy + (1 - c) for cx, cy in chips[:2]]
    ids += [4 * chips[2][0] + 2 * chips[2][1] + c, 4 * chips[2][0] + 2 * chips[2][1] + (1 - c)]
    return jnp.stack(ids).astype(jnp.int32)


HBM = pl.BlockSpec(memory_space=pltpu.HBM)
SEM = pl.BlockSpec(memory_space=pltpu.SEMAPHORE)
DATAFLOW = pltpu.SideEffectType.DATAFLOW_SIDE_EFFECTING


def _split_call(name, bufs, waits=(), starts=None, after=()):
    nb = len(bufs)
    n_new = starts[1] if starts else 0
    wait_sems = [s for w in waits for s in (*w[1], *w[2])]

    def body(*refs):
        b, pos = refs[:nb], nb
        for plan, ss, _, send_idx, recv_idx in waits:
            k = len(ss)
            copies = plan(b, refs[pos:pos + k], refs[pos + k:pos + 2 * k])
            pos += 2 * k
            for i in recv_idx:
                copies[i].wait_recv()
            for i in send_idx:
                copies[i].wait_send()
        outs = refs[pos + len(after):]
        if starts:
            for cp in starts[0](b, outs[nb:nb + n_new], outs[nb + n_new:nb + 2 * n_new]):
                cp.start()
        outs[-1][...] = jnp.zeros_like(outs[-1])

    res = pl.pallas_call(
        body, name=name,
        out_shape=tuple(pltpu.HBM(a.shape, a.dtype) for a in bufs) + (pltpu.SemaphoreType.DMA(()),) * (2 * n_new)
        + (jax.ShapeDtypeStruct((8, 128), F32),),
        in_specs=[HBM] * nb + [SEM] * len(wait_sems) + [ANY] * len(after),
        out_specs=(HBM,) * nb + (SEM,) * (2 * n_new) + (pl.BlockSpec(memory_space=pltpu.VMEM),),
        input_output_aliases={i: i for i in range(nb)},
        compiler_params=pltpu.CompilerParams(has_side_effects=DATAFLOW),
    )(*bufs, *wait_sems, *after)
    return list(res[:nb]), list(res[nb:nb + n_new]), list(res[nb + n_new:nb + 2 * n_new]), res[-1]


def _in_hbm(a):
    return pltpu.with_memory_space_constraint(a, pltpu.HBM)


def _remote(src, dst, send_sem, recv_sem, to):
    return pltpu.make_async_remote_copy(src_ref=src, dst_ref=dst, send_sem=send_sem, recv_sem=recv_sem,
                                        device_id=to, device_id_type=MESH)


def _other_chips():
    x, y, _ = _position()
    return [(1 - x, y), (x, 1 - y), (1 - x, 1 - y)]


def _plan_gather_first(n):
    def plan(b, ss, rs):
        x, y, c = _position()
        to = [(x, y, 1 - c)] + [(*chip, c) for chip in _other_chips()]
        return [_remote(b[w], b[n + w].at[4 * x + 2 * y + c], ss[4 * w + k], rs[4 * w + k], to[k])
                for w in range(n) for k in range(4)]
    return plan, 4 * n


def _plan_gather_pass(n):
    def plan(b, ss, rs):
        x, y, c = _position()
        copies = []
        for w in range(n):
            for j, chip in enumerate(_other_chips()):
                blk = b[n + w].at[4 * chip[0] + 2 * chip[1] + c]
                copies.append(_remote(blk, blk, ss[3 * w + j], rs[3 * w + j], (x, y, 1 - c)))
        return copies
    return plan, 3 * n


def _plan_sibling(n):
    def plan(b, ss, rs):
        x, y, c = _position()
        return [_remote(b[w].at[s], b[n + w].at[s], ss[4 * w + s], rs[4 * w + s], (x, y, 1 - c))
                for w in range(n) for s in range(N_CHIP)]
    return plan, 4 * n


def _plan_scatter(n):
    def plan(b, ss, rs):
        x, y, c = _position()
        return [_remote(b[w].at[2 * chip[0] + chip[1]], b[n + w].at[2 * x + y], ss[3 * w + j], rs[3 * w + j],
                        (*chip, c))
                for w in range(n) for j, chip in enumerate(_other_chips())]
    return plan, 3 * n


class _Gather:
    def __init__(self, shards, after, name):
        self.n, self.name = len(shards), name
        x, y, c = _position()
        placed = [lax.dynamic_update_index_in_dim(lax.empty((N_DEV,) + s.shape, s.dtype), s, 4 * x + 2 * y + c, 0)
                  for s in shards]
        bufs, self.ss, self.rs, self.token = _split_call(
            name + "_start", [_in_hbm(a) for a in list(shards) + placed], starts=_plan_gather_first(self.n),
            after=after)
        self.shards, self.fulls = bufs[:self.n], bufs[self.n:]
        self.passed = {}

    def _sub(self, ids, sems, per):
        return [sems[per * w + k] for w in ids for k in range(per)]

    def pass_on(self, ids, after, tag):
        m = len(ids)
        first = (_plan_gather_first(m)[0], self._sub(ids, self.ss, 4), self._sub(ids, self.rs, 4),
                 [], [4 * i + k for i in range(m) for k in (1, 2, 3)])
        bufs, ss, rs, token = _split_call(
            "%s_pass_%s" % (self.name, tag), [self.shards[w] for w in ids] + [self.fulls[w] for w in ids],
            waits=[first], starts=_plan_gather_pass(m), after=after)
        for i, w in enumerate(ids):
            self.shards[w], self.fulls[w] = bufs[i], bufs[m + i]
        self.passed[tuple(ids)] = (ss, rs)
        return token

    def finish(self, ids, after, tag):
        m = len(ids)
        ss2, rs2 = self.passed[tuple(ids)]
        first = (_plan_gather_first(m)[0], self._sub(ids, self.ss, 4), self._sub(ids, self.rs, 4),
                 list(range(4 * m)), [4 * i for i in range(m)])
        passed = (_plan_gather_pass(m)[0], ss2, rs2, list(range(3 * m)), list(range(3 * m)))
        bufs, _, _, _ = _split_call(
            "%s_finish_%s" % (self.name, tag), [self.shards[w] for w in ids] + [self.fulls[w] for w in ids],
            waits=[first, passed], after=after)
        return bufs[m:]


class _ReduceScatter:
    def __init__(self, others, name):
        self.n, self.name = len(others), name
        lands = [lax.empty(g.shape, g.dtype) for g in others]
        self.bufs, self.ss, self.rs, self.token = _split_call(
            name + "_sibling_start", [_in_hbm(a) for a in list(others) + lands], starts=_plan_sibling(self.n))

    def from_sibling(self, after):
        n = self.n
        bufs, _, _, _ = _split_call(
            self.name + "_sibling_wait", self.bufs,
            waits=[(_plan_sibling(n)[0], self.ss, self.rs, list(range(4 * n)), list(range(4 * n)))], after=after)
        return bufs[n:]

    def scatter(self, sums):
        lands = [lax.empty(s.shape, s.dtype) for s in sums]
        self.bufs, self.ss, self.rs, token = _split_call(
            self.name + "_scatter_start", [_in_hbm(a) for a in list(sums) + lands], starts=_plan_scatter(self.n))
        return token

    def finish(self, after):
        n = self.n
        bufs, _, _, _ = _split_call(
            self.name + "_scatter_wait", self.bufs,
            waits=[(_plan_scatter(n)[0], self.ss, self.rs, list(range(3 * n)), list(range(3 * n)))], after=after)
        return bufs[:n], bufs[n:]


class _Exchanges:
    def __init__(self, parity, order):
        self.parity, self.order = parity, order


def _plan_everyone():
    def plan(b, ss, rs):
        x, y, c = _position()
        return [_remote(b[0], b[1].at[4 * x + 2 * y + c], ss[k - 1], rs[k - 1],
                        (x ^ ((k >> 2) & 1), y ^ ((k >> 1) & 1), c ^ (k & 1))) for k in range(1, N_DEV)]
    return plan, N_DEV - 1


class _GatherSmall:
    def __init__(self, packed, after, name):
        self.name = name
        x, y, c = _position()
        placed = lax.dynamic_update_index_in_dim(lax.empty((N_DEV,) + packed.shape, packed.dtype), packed,
                                                 4 * x + 2 * y + c, 0)
        self.bufs, self.ss, self.rs, self.token = _split_call(
            name + "_start", [_in_hbm(packed), _in_hbm(placed)], starts=_plan_everyone(), after=after)

    def finish(self, after):
        everyone = list(range(N_DEV - 1))
        bufs, _, _, _ = _split_call(
            self.name + "_wait", self.bufs, waits=[(_plan_everyone()[0], self.ss, self.rs, everyone, everyone)],
            after=after)
        return bufs[1]


def _adamw_math(w, g, m, v):
    m = ADAM_B1 * m + (1.0 - ADAM_B1) * g
    v = ADAM_B2 * v + (1.0 - ADAM_B2) * (g * g)
    m_hat = m / (1.0 - ADAM_B1 ** ADAM_STEP)
    v_hat = v / (1.0 - ADAM_B2 ** ADAM_STEP)
    delta = -ADAM_LR * (m_hat / (jnp.sqrt(v_hat) + ADAM_EPS) + ADAM_WD * w)
    return delta, m, v


def _adamw_big_landed(w, m, v, parts, lands, slot, name, row0=0, into=None):
    R, C = w.shape
    rows = parts.shape[1]
    tr = _pick(rows, (256,))
    first = row0 // tr
    n_into = len(into) if into else 0

    def body(slot_ref, w_ref, m_ref, v_ref, own_ref, l1_ref, l2_ref, l3_ref, *rest):
        g = own_ref[...].astype(F32)
        for ref in (l1_ref, l2_ref, l3_ref):
            g = g + ref[...].astype(F32)
        for o_ref, res in zip(rest[n_into:], (g,) + _adamw_math(w_ref[...], g, m_ref[...], v_ref[...])):
            o_ref[...] = res

    blk = pl.BlockSpec((tr, C), lambda i, slot: (first + i, 0))

    def chip(k):
        return pl.BlockSpec((None, tr, C), lambda i, slot: ((slot[0] + k) % N_CHIP, i, 0))

    out = jax.ShapeDtypeStruct((R, C), F32)
    return pl.pallas_call(
        body, name=name,
        grid_spec=pltpu.PrefetchScalarGridSpec(
            num_scalar_prefetch=1, grid=(rows // tr,),
            in_specs=[blk, blk, blk, chip(0), chip(1), chip(2), chip(3)] + [ANY] * n_into,
            out_specs=[blk, blk, blk, blk]),
        out_shape=[out, out, out, out],
        input_output_aliases={8 + j: j for j in range(n_into)},
        compiler_params=_cparams(("parallel",)),
    )(slot, w, m, v, parts, lands, lands, lands, *(into or ()))


def _adamw_small(w, m, v, gathered, name):
    R = w.shape[0]

    def body(w_ref, m_ref, v_ref, p_ref, g_ref, d_ref, nm_ref, nv_ref):
        g = p_ref[0]
        for s in range(1, N_DEV):
            g = g + p_ref[s]
        d, nm, nv = _adamw_math(w_ref[...], g, m_ref[...], v_ref[...])
        g_ref[...] = g
        d_ref[...] = d
        nm_ref[...] = nm
        nv_ref[...] = nv

    out = jax.ShapeDtypeStruct((R, 128), F32)
    return pl.pallas_call(
        body, name=name, out_shape=[out, out, out, out],
    )(w, m, v, gathered)


SMALL_NAMES = ("lb_logits", "hg_norm_w", "rel_bias", "norm_mix_w", "norm_mlp_w", "norm_final_w")
SMALL_SHAPES = {"lb_logits": (2, HG_WIDTH), "hg_norm_w": (1, HG_DK), "rel_bias": (AT_HEADS, N_REL_PAD),
                "norm_mix_w": (1, D_MODEL), "norm_mlp_w": (1, D_MODEL), "norm_final_w": (1, D_MODEL)}


def _pack_small(parts):
    rows = []
    for nme in SMALL_NAMES:
        p = parts[nme]
        if nme == "rel_bias":
            p = jnp.pad(p, ((0, 0), (0, N_REL_PAD - N_REL)))
        rows.append(p.reshape(-1, 128))
    flat = jnp.concatenate(rows, axis=0)
    return jnp.pad(flat, ((0, SMALL_ROWS - flat.shape[0]), (0, 0)))


def _unpack_small(packed):
    out, at = {}, 0
    for nme in SMALL_NAMES:
        shp = SMALL_SHAPES[nme]
        nrow = shp[0] * shp[1] // 128
        p = packed[at:at + nrow].reshape(shp)
        at += nrow
        out[nme] = p[:, :N_REL] if nme == "rel_bias" else p
    return out


BIG_NAMES = ("w_in", "w_branch_a", "w_branch_b", "w_out", "w_up", "w_down")


def kernel(x, w_in, lb_logits, hg_norm_w, rel_bias, w_branch_a, w_branch_b, w_out, norm_mix_w, norm_mlp_w, w_up, w_down, norm_final_w, loss_target, m_w_in, m_lb_logits, m_hg_norm_w, m_rel_bias, m_w_branch_a, m_w_branch_b, m_w_out, m_norm_mix_w, m_norm_mlp_w, m_w_up, m_w_down, m_norm_final_w, v_w_in, v_lb_logits, v_hg_norm_w, v_rel_bias, v_w_branch_a, v_w_branch_b, v_w_out, v_norm_mix_w, v_norm_mlp_w, v_w_up, v_w_down, v_norm_final_w):
    big_w = [w_in[0], w_branch_a[0], w_branch_b[0], w_out[0], w_up[0], w_down[0]]
    big_m = [m_w_in[0], m_w_branch_a[0], m_w_branch_b[0], m_w_out[0], m_w_up[0], m_w_down[0]]
    big_v = [v_w_in[0], v_w_branch_a[0], v_w_branch_b[0], v_w_out[0], v_w_up[0], v_w_down[0]]

    shards = [w.astype(BF16) for w in big_w[:4]] + big_w[4:]
    parity = lax.axis_index("c").astype(jnp.int32).reshape(1)
    loss_part, grad_x, chip_parts, small = _local_step(
        x[0], loss_target[0], lb_logits, hg_norm_w, rel_bias[0], norm_mix_w, norm_mlp_w,
        norm_final_w.reshape(1, D_MODEL), shards[0], shards[1:], _Exchanges(parity, _gather_order()))
    loss = lax.psum(loss_part[0, 0], ("x", "y", "c"))
    (rs_in_lo, rs_in_hi), rs_mix, rs_up, rs_down = chip_parts
    slot =(2 * lax.axis_index("x") + lax.axis_index("y")).astype(jnp.int32).reshape(1)
    big = {}

    def finish(rs, names, after):
        sums, lands = rs.finish(after)
        for nme, own, land in zip(names, sums, lands):
            i = BIG_NAMES.index(nme)
            big[nme] = _adamw_big_landed(big_w[i], big_m[i], big_v[i], own, land, slot, "adamw_" + nme)
        return [big[nme][1] for nme in names]

    gather_small = _GatherSmall(_pack_small(small), [grad_x], "gather_small")
    done = finish(rs_down, ["w_down"], [grad_x, gather_small.token])
    done = finish(rs_up, ["w_up"], done)
    done = finish(rs_mix, ["w_branch_a", "w_branch_b", "w_out"], done)

    sw = dict(lb_logits=lb_logits, hg_norm_w=hg_norm_w, rel_bias=rel_bias[0], norm_mix_w=norm_mix_w,
              norm_mlp_w=norm_mlp_w, norm_final_w=norm_final_w.reshape(1, D_MODEL))
    sm = dict(lb_logits=m_lb_logits, hg_norm_w=m_hg_norm_w, rel_bias=m_rel_bias[0], norm_mix_w=m_norm_mix_w,
              norm_mlp_w=m_norm_mlp_w, norm_final_w=m_norm_final_w.reshape(1, D_MODEL))
    sv = dict(lb_logits=v_lb_logits, hg_norm_w=v_hg_norm_w, rel_bias=v_rel_bias[0], norm_mix_w=v_norm_mix_w,
              norm_mlp_w=v_norm_mlp_w, norm_final_w=v_norm_final_w.reshape(1, D_MODEL))
    gathered = gather_small.finish(done)
    small_packed = _adamw_small(_pack_small(sw), _pack_small(sm), _pack_small(sv), gathered, "adamw_small")
    small_out = [_unpack_small(p) for p in small_packed]

    (own,), (land,) = rs_in_lo.finish(done + [small_packed[0]])
    lo = _adamw_big_landed(big_w[0], big_m[0], big_v[0], own, land, slot, "adamw_w_in_lo")
    (own,), (land,) = rs_in_hi.finish([lo[1]])
    big["w_in"] = _adamw_big_landed(big_w[0], big_m[0], big_v[0], own, land, slot, "adamw_w_in_hi",
                                    row0=D_MODEL // 2, into=lo)

    def leaf(kind, nme):
        if nme in BIG_NAMES:
            return big[nme][kind][None]
        p = small_out[kind][nme]
        if nme == "rel_bias":
            return p[None]
        if nme == "norm_final_w":
            return p.reshape(D_MODEL)
        return p

    order = ("w_in", "lb_logits", "hg_norm_w", "rel_bias", "w_branch_a", "w_branch_b", "w_out", "norm_mix_w",
             "norm_mlp_w", "w_up", "w_down", "norm_final_w")
    outs = [loss, grad_x[None]]
    for kind in range(4):
        outs += [leaf(kind, nme) for nme in order]
    return tuple(outs)
```

```python
import jax
import jax.numpy as jnp
from jax import lax
from jax.experimental import pallas as pl
from jax.experimental.pallas import tpu as pltpu

F32 = jnp.float32
BF16 = jnp.bfloat16
HIGHEST = lax.Precision.HIGHEST
MESH = pl.DeviceIdType.MESH

D_MODEL = 2048
HG_HEADS = 8
HG_DK = 128
HG_WIDTH = 1024
AT_HEADS = 16
AT_DH = 64
AT_WIDTH = 1024
CHUNK = 64
LEFT_CHUNKS = 8
BAND = (LEFT_CHUNKS + 1) * CHUNK
PAD = LEFT_CHUNKS * CHUNK
REL_CLIP = 256
N_REL = 2 * REL_CLIP + 1
N_REL_PAD = 640
D_FF = 4 * D_MODEL
EPS = 1e-6
N_DEV = 8
N_CHIP = 4

ADAM_LR = 0.001
ADAM_B1 = 0.9
ADAM_B2 = 0.999
ADAM_EPS = 1e-08
ADAM_WD = 0.01
ADAM_STEP = 10

COL_HQ, COL_HF, COL_HI, COL_HG = 0, 8, 16, 24
COL_AQ, COL_AK, COL_AV = 32, 40, 48
COL_GATE_A, COL_GATE_B = 7, 9

VMEM_LIMIT = 56 * 1024 * 1024
SMALL_ROWS = 152


def _cparams(sem=None, **kw):
    if sem is not None:
        kw["dimension_semantics"] = sem
    return pltpu.CompilerParams(vmem_limit_bytes=VMEM_LIMIT, **kw)


def _pick(n, cands):
    for c in cands:
        if n % c == 0:
            return c
    return n


def _sigmoid(x):
    return 1.0 / (1.0 + jnp.exp(-x))


ANY = pl.BlockSpec(memory_space=pl.ANY)


def _position():
    return lax.axis_index("x"), lax.axis_index("y"), lax.axis_index("c")


def _call(body, args, *, name, grid, in_specs, out_specs, out_shape, scratch_shapes=(), sem=None, after=(),
          aliases=None):
    n_in = len(args)

    def ordered(*refs):
        body(*refs[:n_in], *refs[n_in + len(after):])

    return list(pl.pallas_call(
        ordered if after else body, name=name, grid=grid, in_specs=list(in_specs) + [ANY] * len(after),
        out_specs=out_specs, out_shape=out_shape, scratch_shapes=list(scratch_shapes),
        input_output_aliases=aliases or {}, compiler_params=_cparams(sem))(*args, *after))


MAX_CONTRACTION_TILE = 4096


def _accumulate(part, acc_ref, step, n_steps, finish):
    if n_steps == 1:
        finish(part)
        return

    @pl.when(step == 0)
    def _():
        acc_ref[...] = part

    @pl.when(step > 0)
    def _():
        acc_ref[...] += part

    @pl.when(step == n_steps - 1)
    def _():
        finish(acc_ref[...])


def _mm_nn(a, wb, out_dtype, name, after=(), epilogue=None, blocks=None, into=()):
    M, K = a.shape
    NB, K2, Nb = wb.shape
    assert K == K2
    j0, nj = blocks or (0, NB)
    n_into = len(into)
    tm = min(M, 1024)
    tk = min(K, MAX_CONTRACTION_TILE)
    tn = _pick(Nb, (512, 1408, 256))
    nk = K // tk
    nn = Nb // tn
    extra, first_cols, out_dtypes, fn = epilogue or ((), (), (out_dtype,), lambda total: (total,))
    n_extra, n_out = len(extra), len(out_dtypes)

    def body(a_ref, b_ref, *rest):
        def finish(total):
            results = fn(total, *[r[...] for r in rest[:n_extra]])
            for o_ref, res, dt in zip(rest[n_extra + n_into:n_extra + n_into + n_out], results, out_dtypes):
                o_ref[...] = res.astype(dt)

        part = jnp.dot(a_ref[...], b_ref[...], preferred_element_type=F32)
        _accumulate(part, rest[-1], pl.program_id(3), nk, finish)

    def tile(first):
        return pl.BlockSpec((tm, tn), lambda m, j, n, k: (m, first + (j0 + j) * nn + n))

    outs = _call(
        body, (a, wb) + tuple(extra) + tuple(into), name=name, grid=(M // tm, nj, nn, nk),
        in_specs=[pl.BlockSpec((tm, tk), lambda m, j, n, k: (m, k)),
                  pl.BlockSpec((None, tk, tn), lambda m, j, n, k: (j0 + j, k, n))]
        + [tile(col // tn) for col in first_cols] + [ANY] * n_into,
        out_specs=[tile(0)] * n_out,
        out_shape=[jax.ShapeDtypeStruct((M, NB * Nb), dt) for dt in out_dtypes],
        scratch_shapes=[] if nk == 1 else [pltpu.VMEM((tm, tn), F32)],
        sem=("parallel", "parallel", "parallel", "arbitrary"), after=after,
        aliases={2 + n_extra + i: i for i in range(n_into)})
    return outs if epilogue else outs[0]


def _squared_relu(a):
    ra = jnp.maximum(a, 0.0)
    return a, ra * ra


def _gated_merge(pb, za, zb, pa):
    return pb, _sigmoid(za) * pa + _sigmoid(zb) * pb


def _mm_nt(a, wb, out_dtype, name, after=(), epilogue=None):
    M, N = a.shape
    NB, K, Nb = wb.shape
    assert N == NB * Nb
    tm = min(M, 1024)
    n_tiles_live = 1 + (len(epilogue[0]) + len(epilogue[2]) if epilogue else 0)
    tko = _pick(K, (1024,)) if n_tiles_live <= 3 else _pick(K, (512,))
    tc = _pick(Nb, (2048, 1024, 1408, 256))
    nc = Nb // tc
    jb = max([d for d in (8, 4, 2, 1) if NB % d == 0 and d * tc <= MAX_CONTRACTION_TILE]) if nc == 1 else 1
    nsteps = (NB // jb) * nc
    extra, first_cols, out_dtypes, fn = epilogue or ((), (), (out_dtype,), lambda total: (total,))
    n_extra, n_out = len(extra), len(out_dtypes)

    def body(a_ref, b_ref, *rest):
        def finish(total):
            results = fn(total, *[r[...] for r in rest[:n_extra]])
            for o_ref, res, dt in zip(rest[n_extra:n_extra + n_out], results, out_dtypes):
                o_ref[...] = res.astype(dt)

        part = sum(lax.dot_general(a_ref[:, i * tc:(i + 1) * tc], b_ref[i], (((1,), (1,)), ((), ())),
                                   preferred_element_type=F32) for i in range(jb))
        _accumulate(part, rest[-1], pl.program_id(2) * nc + pl.program_id(3), nsteps, finish)

    def tile(first):
        return pl.BlockSpec((tm, tko), lambda m, ko, j, c: (m, first + ko))

    outs = _call(
        body, (a, wb) + tuple(extra), name=name,
        grid=(M // tm, K // tko, NB // jb, nc),
        in_specs=[pl.BlockSpec((tm, jb * tc), lambda m, ko, j, c: (m, j * nc + c)),
                  pl.BlockSpec((jb, tko, tc), lambda m, ko, j, c: (j, ko, c))] + [tile(col // tko) for col in first_cols],
        out_specs=[tile(0)] * n_out,
        out_shape=[jax.ShapeDtypeStruct((M, K), dt) for dt in out_dtypes],
        scratch_shapes=[] if nsteps == 1 else [pltpu.VMEM((tm, tko), F32)],
        sem=("parallel", "parallel", "arbitrary", "arbitrary"), after=after)
    return outs if epilogue else outs[0]


ROWS_TILE = 512
ROWS_PIECE = 128


def _mm_rows(a, w, extras, vectors, row_dtypes, fn, name):
    M, K = a.shape
    N = w.shape[1]
    tm = min(M, ROWS_TILE)
    n_e, n_v = len(extras), len(vectors)

    def body(a_ref, w_ref, *rest):
        tiles, vecs, outs, product_ref = rest[:n_e], rest[n_e:n_e + n_v], rest[n_e + n_v:-1], rest[-1]
        product_ref[...] = jnp.dot(a_ref[...], w_ref[...], preferred_element_type=F32)
        for i in range(tm // ROWS_PIECE):
            piece = slice(i * ROWS_PIECE, (i + 1) * ROWS_PIECE)
            results = fn(product_ref[piece, :], *[t[piece, :] for t in tiles], *[v[...] for v in vecs])
            for o_ref, res, dt in zip(outs, results, row_dtypes):
                o_ref[piece, :] = res.astype(dt)

    row = pl.BlockSpec((tm, N), lambda m: (m, 0))
    return _call(
        body, (a, w) + tuple(extras) + tuple(vectors), name=name, grid=(M // tm,),
        in_specs=[pl.BlockSpec((tm, K), lambda m: (m, 0)), pl.BlockSpec((K, N), lambda m: (0, 0))]
        + [row] * n_e + [pl.BlockSpec((1, N), lambda m: (0, 0))] * n_v,
        out_specs=[row] * len(row_dtypes),
        out_shape=[jax.ShapeDtypeStruct((M, N), dt) for dt in row_dtypes],
        scratch_shapes=[pltpu.VMEM((tm, N), F32)], sem=("parallel",))


def _rms(h, w):
    return h * lax.rsqrt(jnp.mean(h * h, axis=-1, keepdims=True) + EPS) * w


def _residual_rms_rows(mix, x, w):
    h = x + mix
    return h, _rms(h, w)


def _mm_tn_half(a, g, which, blocks_on, add, name, after=(), a_cols=None):
    M, Ka = a.shape
    N = g.shape[1]
    first_col = 0
    if a_cols is not None:
        first_col, Ka = a_cols
    if blocks_on == "g":
        rows, cols = _pick(Ka, (1024,)), N // N_DEV
        tn = _pick(cols, (512, 1408, 256))
        nn = cols // tn
        grid = (Ka // rows, N_CHIP, nn)
        a_spec = pl.BlockSpec((M, rows), lambda ka, s, n, w: (0, first_col // rows + ka))
        g_spec = pl.BlockSpec((M, tn), lambda ka, s, n, w: (0, (2 * s + w[0]) * nn + n))
        out_rows = Ka
    else:
        rows, cols = Ka // N_DEV, N
        tn = _pick(cols, (2048, 512))
        nn = cols // tn
        grid = (1, N_CHIP, nn)
        a_spec = pl.BlockSpec((M, rows), lambda ka, s, n, w: (0, 2 * s + w[0]))
        g_spec = pl.BlockSpec((M, tn), lambda ka, s, n, w: (0, n))
        out_rows = rows
    o_spec = pl.BlockSpec((None, rows, tn), lambda ka, s, n, w: (s, ka, n))
    n_add = 0 if add is None else 1

    def body(which_ref, a_ref, g_ref, *rest):
        acc = lax.dot_general(a_ref[...], g_ref[...], (((0,), (0,)), ((), ())), preferred_element_type=F32)
        if n_add:
            acc = acc + rest[0][...].astype(F32)
        rest[-1][...] = acc.astype(BF16)

    return pl.pallas_call(
        body, name=name,
        grid_spec=pltpu.PrefetchScalarGridSpec(
            num_scalar_prefetch=1, grid=grid,
            in_specs=[a_spec, g_spec] + [o_spec] * n_add + [ANY] * len(after),
            out_specs=o_spec),
        out_shape=jax.ShapeDtypeStruct((N_CHIP, out_rows, cols), BF16),
        compiler_params=_cparams(("parallel", "parallel", "parallel")),
    )(which, a, g, *(() if add is None else (add,)), *after)


ROW_TILE = 256


def _rms_fwd(x, w, name):
    T, Dm = x.shape

    def body(x_ref, w_ref, u_ref):
        xv = x_ref[...]
        r = lax.rsqrt(jnp.mean(xv * xv, axis=-1, keepdims=True) + EPS)
        u_ref[...] = (xv * r * w_ref[...]).astype(BF16)

    return pl.pallas_call(
        body, name=name, grid=(T // ROW_TILE,),
        in_specs=[pl.BlockSpec((ROW_TILE, Dm), lambda i: (i, 0)), pl.BlockSpec((1, Dm), lambda i: (0, 0))],
        out_specs=pl.BlockSpec((ROW_TILE, Dm), lambda i: (i, 0)),
        out_shape=jax.ShapeDtypeStruct((T, Dm), BF16),
        compiler_params=_cparams(("parallel",)),
    )(x, w)


def _loss_head(h1, mlp, wf, target, name):
    T, Dm = h1.shape

    def body(h_ref, m_ref, w_ref, t_ref, loss_ref, dh_ref, dhb_ref, dw_ref):
        i = pl.program_id(0)
        h = h_ref[...] + m_ref[...]
        r = lax.rsqrt(jnp.mean(h * h, axis=-1, keepdims=True) + EPS)
        xh = h * r
        wv = w_ref[...]
        e = xh * wv - t_ref[...]
        part = 0.5 * jnp.sum(jnp.mean(e * e, axis=-1, keepdims=True), axis=0, keepdims=True)
        dy = e * (1.0 / Dm)
        dw = jnp.sum(dy * xh, axis=0, keepdims=True)
        gy = dy * wv
        dh = r * (gy - xh * jnp.mean(gy * xh, axis=-1, keepdims=True))
        dh_ref[...] = dh
        dhb_ref[...] = dh.astype(BF16)

        @pl.when(i == 0)
        def _():
            loss_ref[...] = jnp.zeros_like(loss_ref)
            dw_ref[...] = jnp.zeros_like(dw_ref)

        loss_ref[...] += jnp.broadcast_to(part, loss_ref.shape)
        dw_ref[...] += dw

    row = pl.BlockSpec((ROW_TILE, Dm), lambda i: (i, 0))
    vec = pl.BlockSpec((1, Dm), lambda i: (0, 0))
    return pl.pallas_call(
        body, name=name, grid=(T // ROW_TILE,),
        in_specs=[row, row, vec, row],
        out_specs=[pl.BlockSpec((8, 128), lambda i: (0, 0)), row, row, vec],
        out_shape=[jax.ShapeDtypeStruct((8, 128), F32), jax.ShapeDtypeStruct((T, Dm), F32),
                   jax.ShapeDtypeStruct((T, Dm), BF16), jax.ShapeDtypeStruct((1, Dm), F32)],
        compiler_params=_cparams(("arbitrary",)),
    )(h1, mlp, wf, target)


def _rms_bwd(dyn, x, w, dres, dx_dtypes, name, after=()):
    T, Dm = x.shape
    n_dx = len(dx_dtypes)

    def body(g_ref, x_ref, w_ref, r_ref, *outs):
        i = pl.program_id(0)
        xv = x_ref[...]
        r = lax.rsqrt(jnp.mean(xv * xv, axis=-1, keepdims=True) + EPS)
        xh = xv * r
        g = g_ref[...]
        dw = jnp.sum(g * xh, axis=0, keepdims=True)
        gy = g * w_ref[...]
        dx = r_ref[...] + r * (gy - xh * jnp.mean(gy * xh, axis=-1, keepdims=True))
        for dx_ref, dt in zip(outs, dx_dtypes):
            dx_ref[...] = dx.astype(dt)
        dw_ref = outs[n_dx]

        @pl.when(i == 0)
        def _():
            dw_ref[...] = jnp.zeros_like(dw_ref)

        dw_ref[...] += dw

    row = pl.BlockSpec((ROW_TILE, Dm), lambda i: (i, 0))
    vec = pl.BlockSpec((1, Dm), lambda i: (0, 0))
    return _call(
        body, (dyn, x, w, dres), name=name, grid=(T // ROW_TILE,),
        in_specs=[row, row, vec, row],
        out_specs=[row] * n_dx + [vec],
        out_shape=[jax.ShapeDtypeStruct((T, Dm), dt) for dt in dx_dtypes] + [jax.ShapeDtypeStruct((1, Dm), F32)],
        sem=("arbitrary",), after=after)


GATE_TILE = 1024


def _merge_grads(d, za, zb, pa, pb):
    ga = _sigmoid(za)
    gb = _sigmoid(zb)
    return d * ga, d * gb, d * pa * ga * (1.0 - ga), d * pb * gb * (1.0 - gb)


def _dot_hi(a, b, dims):
    return lax.dot_general(a, b, (dims, ((), ())), precision=HIGHEST, preferred_element_type=F32)


NN = ((1,), (0,))
NT = ((1,), (1,))
TN = ((0,), (0,))


def _hg_gates(hq, hf, lb):
    sq = _sigmoid(hq)
    q = hq * sq * (HG_DK ** -0.5)
    f = _sigmoid(hf)
    g = lb + (1.0 - lb) * f
    return q, sq, f, g, jnp.log(g), 1.0 - g


def _tri(lower):
    r = lax.broadcasted_iota(jnp.int32, (CHUNK, CHUNK), 0)
    c = lax.broadcasted_iota(jnp.int32, (CHUNK, CHUNK), 1)
    return jnp.where((r >= c) if lower else (r <= c), 1.0, 0.0).astype(BF16)


def _running_sum(tri, x):
    return sum(jnp.dot(tri, piece, preferred_element_type=F32) for piece in _split3(x))


GROUP = 16
N_GROUPS = CHUNK // GROUP
BWD_CHUNKS_PER_TRIP = 4


def _dot_bf16(a, b, dims):
    return lax.dot_general(a.astype(BF16), b.astype(BF16), (dims, ((), ())), preferred_element_type=F32)


def _rows_iota():
    return lax.broadcasted_iota(jnp.int32, (CHUNK, HG_DK), 0)


def _by_query_group(q, kk, b, g):
    r0 = GROUP * g
    b0 = b[r0:r0 + 1]
    decay = jnp.exp(b[r0:r0 + GROUP] - b0)
    ks = jnp.where(_rows_iota() < r0, kk * jnp.exp(jnp.minimum(b0 - b, 0.0)), 0.0)
    return q[r0:r0 + GROUP] * decay, ks, decay


def _by_key_group(q, kk, b, j):
    r1 = GROUP * (j + 1)
    b1 = b[r1 - 1:r1]
    decay = jnp.exp(b1 - b[r1 - GROUP:r1])
    qs = jnp.where(_rows_iota() >= r1, q * jnp.exp(jnp.minimum(b - b1, 0.0)), 0.0)
    return qs, kk[r1 - GROUP:r1] * decay, decay


def _scores_between_groups(q, kk, b):
    blocks = [jnp.zeros((GROUP, CHUNK), F32)]
    for g in range(1, N_GROUPS):
        qs, ks, _ = _by_query_group(q, kk, b, g)
        blocks.append(_dot_bf16(qs, ks, NT))
    return jnp.concatenate(blocks, axis=0)


def _hgrn2_fwd(z, lb_logits, hg_norm_w, name, after=()):
    T = z.shape[0]
    n_chunks = T // CHUNK

    def body(hq_ref, hf_ref, hi_ref, hg_ref, lbl_ref, nw_ref, o_ref, ya_ref, sall_ref, st_ref):
        lbl = lbl_ref[...]
        lb = 1.0 / (1.0 + jnp.exp(lbl[1:2, :] - lbl[0:1, :]))
        st_ref[...] = jnp.zeros_like(st_ref)
        tri = _tri(True)
        row8 = lax.broadcasted_iota(jnp.int32, (8, HG_DK), 0)

        def chunk(c, carry):
            rows = pl.ds(pl.multiple_of(c * CHUNK, CHUNK), CHUNK)
            q, _, _, _, lg, kk = _hg_gates(hq_ref[rows, :], hf_ref[rows, :], lb)
            v = hi_ref[rows, :]
            b = _running_sum(tri, lg)
            st = st_ref[...]
            sall_ref[c] = st
            for grp in range(N_GROUPS):
                r0 = GROUP * grp
                for h8 in range(GROUP // 8):
                    n = 8 * (h8 + 1)
                    bs, ks, vs = b[r0:r0 + n], kk[r0:r0 + n], v[r0:r0 + n]
                    sidx = lax.broadcasted_iota(jnp.int32, (n, HG_DK), 0)
                    blk = jnp.zeros((8, HG_DK), F32)
                    for i in range(8):
                        t = r0 + 8 * h8 + i
                        e = jnp.where(sidx <= 8 * h8 + i, jnp.exp(b[t:t + 1] - bs), 0.0)
                        p = jnp.sum(e * ks * q[t:t + 1], axis=1, keepdims=True)
                        ot = jnp.sum(p * vs, axis=0, keepdims=True)
                        blk = blk + jnp.where(row8 == i, ot, 0.0)
                    o_ref[pl.ds(pl.multiple_of(c * CHUNK + r0 + 8 * h8, 8), 8), :] = blk
            o_ref[rows, :] += _dot_hi(q * jnp.exp(b), st, NT) + _dot_bf16(_scores_between_groups(q, kk, b), v, NN)
            bl = b[CHUNK - 1:CHUNK]
            ke = kk * jnp.exp(bl - b)
            st_ref[...] = st * jnp.exp(bl) + _dot_hi(v, ke, TN)
            return carry

        lax.fori_loop(0, n_chunks, chunk, 0, unroll=2)
        o = o_ref[...]
        r = lax.rsqrt(jnp.mean(o * o, axis=-1, keepdims=True) + EPS)
        hg = hg_ref[...]
        ya_ref[...] = (o * r * nw_ref[...] * (hg * _sigmoid(hg))).astype(BF16)

    def col(base):
        return pl.BlockSpec((T, HG_DK), lambda h: (0, base + h))

    return _call(
        body, (z, z, z, z, lb_logits, hg_norm_w), name=name, grid=(HG_HEADS,),
        in_specs=[col(COL_HQ), col(COL_HF), col(COL_HI), col(COL_HG),
                  pl.BlockSpec((2, HG_DK), lambda h: (0, h)), pl.BlockSpec((1, HG_DK), lambda h: (0, 0))],
        out_specs=[col(0), col(0), pl.BlockSpec((None, n_chunks, HG_DK, HG_DK), lambda h: (h, 0, 0, 0))],
        out_shape=[jax.ShapeDtypeStruct((T, HG_WIDTH), F32), jax.ShapeDtypeStruct((T, HG_WIDTH), BF16),
                   jax.ShapeDtypeStruct((HG_HEADS, n_chunks, HG_DK, HG_DK), F32)],
        scratch_shapes=[pltpu.VMEM((HG_DK, HG_DK), F32)],
        sem=("parallel",), after=after)


def _hgrn2_bwd(z, lb_logits, hg_norm_w, o_raw, s_all, dya, name, after=()):
    T = z.shape[0]
    n_chunks = T // CHUNK

    def body(hq_ref, hf_ref, hi_ref, hg_ref, lbl_ref, nw_ref, o_ref, sall_ref, dya_ref,
             dhq_ref, dhf_ref, dhi_ref, dhg_ref, dlbl_ref, dnw_ref,
             do_ref, dst_ref, dlb_ref, *per_chunk):
        h = pl.program_id(0)
        lbl = lbl_ref[...]
        lb = 1.0 / (1.0 + jnp.exp(lbl[1:2, :] - lbl[0:1, :]))

        o = o_ref[...]
        r = lax.rsqrt(jnp.mean(o * o, axis=-1, keepdims=True) + EPS)
        oh = o * r
        nw = nw_ref[...]
        hg = hg_ref[...]
        sg = _sigmoid(hg)
        dy = dya_ref[...]
        d_on = dy * (hg * sg)
        dhg_ref[...] = (dy * (oh * nw) * (sg * (1.0 + hg * (1.0 - sg)))).astype(BF16)
        dnw = jnp.sum(d_on * oh, axis=0, keepdims=True)
        gy = d_on * nw
        do_ref[...] = r * (gy - oh * jnp.mean(gy * oh, axis=-1, keepdims=True))

        @pl.when(h == 0)
        def _():
            dnw_ref[...] = jnp.zeros_like(dnw_ref)

        dnw_ref[...] += jnp.broadcast_to(dnw, dnw_ref.shape)

        dst_ref[...] = jnp.zeros_like(dst_ref)
        dlb_ref[...] = jnp.zeros_like(dlb_ref)
        tri = _tri(True)
        tri_t = _tri(False)
        row8 = lax.broadcasted_iota(jnp.int32, (8, HG_DK), 0)
        row_group = lax.broadcasted_iota(jnp.int32, (CHUNK, CHUNK), 0) // GROUP
        col_group = lax.broadcasted_iota(jnp.int32, (CHUNK, CHUNK), 1) // GROUP
        earlier_group = col_group < row_group
        later_group = col_group > row_group

        def chunk(c, dq_ref, dk_ref, dv_ref):
            rows = pl.ds(pl.multiple_of(c * CHUNK, CHUNK), CHUNK)
            hq = hq_ref[rows, :]
            q, sq, f, g, lg, kk = _hg_gates(hq, hf_ref[rows, :], lb)
            v = hi_ref[rows, :]
            do = do_ref[rows, :]
            b = _running_sum(tri, lg)
            eb = jnp.exp(b)
            bl = b[CHUNK - 1:CHUNK]
            ebl = jnp.exp(bl)
            ekb = jnp.exp(bl - b)
            qe = q * eb
            ke = kk * ekb
            st = sall_ref[c]
            dst = dst_ref[...]
            dqe = _dot_bf16(do, st, NN)
            dke = _dot_bf16(v, dst, NN)
            dv_inter = _dot_bf16(ke, dst, NT)
            d_ebl = jnp.sum(st * dst, axis=0, keepdims=True)
            dst_ref[...] = dst * ebl + _dot_bf16(do, qe, TN)

            dk_ref[...] = jnp.zeros_like(dk_ref)
            dv_ref[...] = jnp.zeros_like(dv_ref)
            for grp in range(N_GROUPS):
                r0 = GROUP * grp
                for h8 in range(GROUP // 8):
                    n = 8 * (h8 + 1)
                    bs, ks, vs = b[r0:r0 + n], kk[r0:r0 + n], v[r0:r0 + n]
                    sidx = lax.broadcasted_iota(jnp.int32, (n, HG_DK), 0)
                    blk = jnp.zeros((8, HG_DK), F32)
                    for i in range(8):
                        t = r0 + 8 * h8 + i
                        qt = q[t:t + 1]
                        dot_ = do[t:t + 1]
                        e = jnp.where(sidx <= 8 * h8 + i, jnp.exp(b[t:t + 1] - bs), 0.0)
                        w = e * ks
                        p = jnp.sum(w * qt, axis=1, keepdims=True)
                        dsc = jnp.sum(vs * dot_, axis=1, keepdims=True)
                        dqt = jnp.sum(dsc * w, axis=0, keepdims=True)
                        blk = blk + jnp.where(row8 == i, dqt, 0.0)
                        dk_ref[r0:r0 + n, :] += dsc * e * qt
                        dv_ref[r0:r0 + n, :] += p * dot_
                    dq_ref[r0 + 8 * h8:r0 + n, :] = blk
            ds_far = jnp.where(earlier_group, _dot_bf16(do, v, NT), 0.0)
            ds_far_t = jnp.where(later_group, _dot_bf16(v, do, NT), 0.0)
            dq_far, dk_far = [jnp.zeros((GROUP, HG_DK), F32)], []
            for grp in range(1, N_GROUPS):
                r0 = GROUP * grp
                _, ks, decay = _by_query_group(q, kk, b, grp)
                dq_far.append(decay * _dot_hi(ds_far[r0:r0 + GROUP], ks, NN))
                qs, _, decay = _by_key_group(q, kk, b, grp - 1)
                dk_far.append(decay * _dot_hi(ds_far_t[r0 - GROUP:r0], qs, NN))
            dk_far.append(jnp.zeros((GROUP, HG_DK), F32))
            dv_far = _dot_bf16(_scores_between_groups(q, kk, b), do, TN)
            dq_i = dq_ref[...] + jnp.concatenate(dq_far, axis=0)
            dk_i = dk_ref[...] + jnp.concatenate(dk_far, axis=0)
            dke_ke = dke * ke
            db = q * dq_i - kk * dk_i + dqe * qe - dke_ke
            db_last = jnp.sum(dke_ke, axis=0, keepdims=True) + d_ebl * ebl
            dlg = _running_sum(tri_t, db) + db_last
            dq = dq_i + dqe * eb
            dkk = dk_i + dke * ekb
            dg = dlg / g - dkk
            dhq_ref[rows, :] = (dq * (HG_DK ** -0.5) * (sq * (1.0 + hq * (1.0 - sq)))).astype(BF16)
            dhf_ref[rows, :] = (dg * (1.0 - lb) * f * (1.0 - f)).astype(BF16)
            dhi_ref[rows, :] = (dv_ref[...] + dv_far + dv_inter).astype(BF16)
            dlb_ref[...] += jnp.sum(dg * (1.0 - f), axis=0, keepdims=True)

        def trip(i, carry):
            for k in range(BWD_CHUNKS_PER_TRIP):
                chunk(n_chunks - 1 - k - BWD_CHUNKS_PER_TRIP * i, *per_chunk[3 * k:3 * k + 3])
            return carry

        lax.fori_loop(0, n_chunks // BWD_CHUNKS_PER_TRIP, trip, 0)
        dl0 = dlb_ref[...] * lb * (1.0 - lb)
        dlbl_ref[0:1, :] = dl0
        dlbl_ref[1:2, :] = -dl0

    def col(base):
        return pl.BlockSpec((T, HG_DK), lambda h: (0, base + h))

    outb = jax.ShapeDtypeStruct((T, HG_WIDTH), BF16)
    return _call(
        body, (z, z, z, z, lb_logits, hg_norm_w, o_raw, s_all, dya), name=name, grid=(HG_HEADS,),
        in_specs=[col(COL_HQ), col(COL_HF), col(COL_HI), col(COL_HG),
                  pl.BlockSpec((2, HG_DK), lambda h: (0, h)), pl.BlockSpec((1, HG_DK), lambda h: (0, 0)),
                  col(0), pl.BlockSpec((None, n_chunks, HG_DK, HG_DK), lambda h: (h, 0, 0, 0)), col(0)],
        out_specs=[col(0), col(0), col(0), col(0), pl.BlockSpec((2, HG_DK), lambda h: (0, h)),
                   pl.BlockSpec((8, HG_DK), lambda h: (0, 0))],
        out_shape=[outb, outb, outb, outb, jax.ShapeDtypeStruct((2, HG_WIDTH), F32),
                   jax.ShapeDtypeStruct((8, HG_DK), F32)],
        scratch_shapes=[pltpu.VMEM((T, HG_DK), F32), pltpu.VMEM((HG_DK, HG_DK), F32), pltpu.VMEM((1, HG_DK), F32)]
        + [pltpu.VMEM((CHUNK, HG_DK), F32)] * (3 * BWD_CHUNKS_PER_TRIP),
        sem=("arbitrary",), after=after)


CONST_KEYS = PAD - REL_CLIP
VAR_KEYS = BAND - CONST_KEYS
REL_LO = 128
REL_SPAN = N_REL_PAD - REL_LO


def _rel_onehot(t):
    r = lax.broadcasted_iota(jnp.int32, (REL_SPAN, VAR_KEYS), 0)
    j = lax.broadcasted_iota(jnp.int32, (REL_SPAN, VAR_KEYS), 1)
    idx = jnp.clip(t + PAD - CONST_KEYS - j, -REL_CLIP, REL_CLIP) + REL_CLIP - REL_LO
    return jnp.where(r == idx, 1.0, 0.0).astype(BF16)


def _split3(x):
    hi = x.astype(BF16)
    r1 = x - hi.astype(F32)
    mid = r1.astype(BF16)
    return hi, mid, (r1 - mid.astype(F32)).astype(BF16)


def _bias_expand(rel, name):
    rows = 8

    def body(rel_ref, out_ref):
        tab = rel_ref[...]
        pieces = _split3(tab[:, REL_LO:N_REL_PAD])
        constant = jnp.broadcast_to(tab[:, 2 * REL_CLIP:2 * REL_CLIP + 1], (AT_HEADS, CONST_KEYS))
        for i in range(rows):
            onehot = _rel_onehot(pl.program_id(0) * rows + i)
            out_ref[i, :, 0:CONST_KEYS] = constant
            out_ref[i, :, CONST_KEYS:BAND] = sum(jnp.dot(piece, onehot, preferred_element_type=F32)
                                                 for piece in pieces)

    return pl.pallas_call(
        body, name=name, grid=(CHUNK // rows,),
        in_specs=[pl.BlockSpec((AT_HEADS, N_REL_PAD), lambda t: (0, 0))],
        out_specs=pl.BlockSpec((rows, AT_HEADS, BAND), lambda t: (t, 0, 0)),
        out_shape=jax.ShapeDtypeStruct((CHUNK, AT_HEADS, BAND), F32),
        compiler_params=_cparams(("parallel",)),
    )(rel)


def _bias_reduce(dbias_rows, name, after=()):
    def body(db_ref, out_ref):
        lane = lax.broadcasted_iota(jnp.int32, (AT_HEADS, N_REL_PAD), 1)
        varying = lane >= CONST_KEYS
        by_offset = jnp.zeros((AT_HEADS, N_REL_PAD), F32)
        constant = jnp.zeros((AT_HEADS, N_REL_PAD), F32)
        for t in range(CHUNK):
            row = db_ref[t]
            constant = constant + jnp.where(varying, 0.0, row)
            moved = jnp.where(varying, row, 0.0)
            by_offset = by_offset + (pltpu.roll(moved, N_REL_PAD - t, axis=1) if t else moved)
        offset = lax.broadcasted_iota(jnp.int32, (N_REL_PAD, N_REL_PAD), 0)
        entry = lax.broadcasted_iota(jnp.int32, (N_REL_PAD, N_REL_PAD), 1)
        onehot = jnp.where(entry == jnp.clip(PAD - offset, -REL_CLIP, REL_CLIP) + REL_CLIP, 1.0, 0.0).astype(BF16)
        acc = sum(jnp.dot(piece, onehot, preferred_element_type=F32) for piece in _split3(by_offset))
        last = jnp.sum(constant, axis=1, keepdims=True)
        out_ref[...] = acc + jnp.where(lane == 2 * REL_CLIP, last, 0.0)

    whole = pl.BlockSpec((CHUNK, AT_HEADS, N_REL_PAD), lambda i: (0, 0, 0))
    return _call(
        body, (dbias_rows,), name=name, grid=(1,), in_specs=[whole],
        out_specs=[pl.BlockSpec((AT_HEADS, N_REL_PAD), lambda i: (0, 0))],
        out_shape=[jax.ShapeDtypeStruct((AT_HEADS, N_REL_PAD), F32)],
        sem=("arbitrary",), after=after)[0]


def _pair_lanes():
    return lax.broadcasted_iota(jnp.int32, (CHUNK, 2 * AT_DH), 1) < AT_DH


def _block_diag(a):
    first = _pair_lanes()
    return jnp.concatenate([jnp.where(first, a, 0.0), jnp.where(first, 0.0, a)], axis=0).astype(BF16)


def _diag_blocks(a):
    return jnp.where(_pair_lanes(), a[:CHUNK], a[CHUNK:])


def _band_probs_t(kb, qbd, bias_t, c):
    s = lax.dot_general(kb, qbd, (NT, ((), ())), preferred_element_type=F32) * (AT_DH ** -0.5) + bias_t
    j = lax.broadcasted_iota(jnp.int32, (BAND, 2 * AT_DH), 0)
    s = jnp.where(j + c * CHUNK >= PAD, s, -jnp.inf)
    p = jnp.exp(s - jnp.max(s, axis=0, keepdims=True))
    return p / jnp.sum(p, axis=0, keepdims=True)


def _attn_fwd(z, bias_t, name, after=()):
    T = z.shape[0]
    n_chunks = T // CHUNK

    def body(q_ref, k_ref, v_ref, bias_ref, y_ref, p_ref, *scratch):
        for pr in range(2):
            lanes = slice(128 * pr, 128 * (pr + 1))
            for dst_ref, src_ref in zip(scratch[2 * pr:2 * pr + 2], (k_ref, v_ref)):
                dst_ref[0:PAD, :] = jnp.zeros((PAD, 128), BF16)
                dst_ref[PAD:PAD + T, :] = src_ref[:, lanes].astype(BF16)

        def chunk(c, carry):
            rows = pl.ds(pl.multiple_of(c * CHUNK, CHUNK), CHUNK)
            band = pl.ds(pl.multiple_of(c * CHUNK, CHUNK), BAND)
            for pr in range(2):
                kp_ref, vp_ref = scratch[2 * pr:2 * pr + 2]
                lanes = slice(128 * pr, 128 * (pr + 1))
                p = _band_probs_t(kp_ref[band, :], _block_diag(q_ref[rows, lanes]), bias_ref[pr], c).astype(BF16)
                p_ref[pr, c] = p
                o2 = lax.dot_general(p, vp_ref[band, :], (TN, ((), ())), preferred_element_type=F32)
                y_ref[rows, lanes] = _diag_blocks(o2).astype(BF16)
            return carry

        lax.fori_loop(0, n_chunks, chunk, 0, unroll=2)

    def col(base):
        return pl.BlockSpec((T, 256), lambda h: (0, base // 2 + h))

    return _call(
        body, (z, z, z, bias_t), name=name, grid=(AT_HEADS // 4,),
        in_specs=[col(COL_AQ), col(COL_AK), col(COL_AV), pl.BlockSpec((2, BAND, 128), lambda h: (h, 0, 0))],
        out_specs=[col(0), pl.BlockSpec((2, n_chunks, BAND, 128), lambda h: (h, 0, 0, 0))],
        out_shape=[jax.ShapeDtypeStruct((T, AT_WIDTH), BF16),
                   jax.ShapeDtypeStruct((AT_HEADS // 2, n_chunks, BAND, 128), BF16)],
        scratch_shapes=[pltpu.VMEM((PAD + T, 128), BF16)] * 4,
        sem=("parallel",), after=after)


def _attn_bwd(z, probs, dyb, name, after=()):
    T = z.shape[0]
    n_chunks = T // CHUNK

    def body(q_ref, k_ref, v_ref, p_ref, dy_ref, dq_ref, dk_ref, dv_ref, dbias_ref, *scratch):
        dbias_ref[...] = jnp.zeros_like(dbias_ref)
        for pr in range(2):
            kp_ref, vp_ref, dkp_ref, dvp_ref = scratch[4 * pr:4 * pr + 4]
            lanes = slice(128 * pr, 128 * (pr + 1))
            kp_ref[0:PAD, :] = jnp.zeros((PAD, 128), BF16)
            vp_ref[0:PAD, :] = jnp.zeros((PAD, 128), BF16)
            kp_ref[PAD:PAD + T, :] = k_ref[:, lanes].astype(BF16)
            vp_ref[PAD:PAD + T, :] = v_ref[:, lanes].astype(BF16)
            dkp_ref[...] = jnp.zeros_like(dkp_ref)
            dvp_ref[...] = jnp.zeros_like(dvp_ref)

        def chunk(c, carry):
            rows = pl.ds(pl.multiple_of(c * CHUNK, CHUNK), CHUNK)
            band = pl.ds(pl.multiple_of(c * CHUNK, CHUNK), BAND)
            for pr in range(2):
                kp_ref, vp_ref, dkp_ref, dvp_ref = scratch[4 * pr:4 * pr + 4]
                lanes = slice(128 * pr, 128 * (pr + 1))
                qbd = _block_diag(q_ref[rows, lanes])
                dobd = _block_diag(dy_ref[rows, lanes])
                pb = p_ref[pr, c]
                p = pb.astype(F32)
                dp = lax.dot_general(vp_ref[band, :], dobd, (NT, ((), ())), preferred_element_type=F32)
                ds = p * (dp - jnp.sum(dp * p, axis=0, keepdims=True))
                dbias_ref[pr] += ds
                dsb = ds.astype(BF16)
                dq2 = lax.dot_general(dsb, kp_ref[band, :], (TN, ((), ())), preferred_element_type=F32)
                dq_ref[rows, lanes] = (_diag_blocks(dq2) * (AT_DH ** -0.5)).astype(BF16)
                dkp_ref[band, :] += jnp.dot(dsb, qbd, preferred_element_type=F32) * (AT_DH ** -0.5)
                dvp_ref[band, :] += jnp.dot(pb, dobd, preferred_element_type=F32)
            return carry

        lax.fori_loop(0, n_chunks, chunk, 0)
        for pr in range(2):
            lanes = slice(128 * pr, 128 * (pr + 1))
            dk_ref[:, lanes] = scratch[4 * pr + 2][PAD:PAD + T, :].astype(BF16)
            dv_ref[:, lanes] = scratch[4 * pr + 3][PAD:PAD + T, :].astype(BF16)

    def col(base):
        return pl.BlockSpec((T, 256), lambda h: (0, base // 2 + h))

    outb = jax.ShapeDtypeStruct((T, AT_WIDTH), BF16)
    return _call(
        body, (z, z, z, probs, dyb), name=name, grid=(AT_HEADS // 4,),
        in_specs=[col(COL_AQ), col(COL_AK), col(COL_AV),
                  pl.BlockSpec((2, n_chunks, BAND, 128), lambda h: (h, 0, 0, 0)), col(0)],
        out_specs=[col(0), col(0), col(0), pl.BlockSpec((2, BAND, 128), lambda h: (h, 0, 0))],
        out_shape=[outb, outb, outb, jax.ShapeDtypeStruct((AT_HEADS // 2, BAND, 128), F32)],
        scratch_shapes=([pltpu.VMEM((PAD + T, 128), BF16)] * 2 + [pltpu.VMEM((PAD + T, 128), F32)] * 2) * 2,
        sem=("parallel",), after=after)


def _local_step(x, target, lb_logits, hg_norm_w, rel_bias, norm_mix_w, norm_mlp_w, norm_final_w,
                w_in, rest, exchanges=None):
    ex = exchanges
    rel = jnp.pad(rel_bias, ((0, 0), (0, N_REL_PAD - N_REL)))

    u = _rms_fwd(x, norm_mix_w, "rms_mix_fwd")
    if ex:
        z, w_in = _mm_gathered(u, w_in, ex.order, "mm_in_fwd")
        gather = _Gather(rest[:3], [w_in], "ag")
        mlp_shards, _ = lax.optimization_barrier((rest[3:], gather.token))
        gather_mlp = _Gather([s.astype(BF16) for s in mlp_shards], [gather.token], "ag_mlp")
        z = _mm_gathered_tail(u, w_in, z, ex.order, "mm_in_fwd_tail", after=[gather_mlp.token])
        tok = []
    else:
        z = _mm_nn(u, w_in, F32, "mm_in_fwd")
        w_a, w_b, w_out, w_up, w_down = rest
        tok = []
    o_raw, y_a, s_all = _hgrn2_fwd(z, lb_logits, hg_norm_w, "hgrn2_fwd", after=tok)
    if ex:
        tok = [gather.pass_on([0, 1, 2], [o_raw], "abo")]
    bias_rows = _bias_expand(rel, "bias_expand")
    bias_t = jnp.transpose(bias_rows.reshape(CHUNK, AT_HEADS // 2, 2, BAND), (1, 3, 2, 0)).reshape(
        AT_HEADS // 2, BAND, 2 * CHUNK)
    y_b, probs = _attn_fwd(z, bias_t, "attn_fwd", after=tok)
    if ex:
        tok = [gather_mlp.pass_on([0], [y_b], "up")]
        w_a, w_b, w_out = gather.finish([0, 1, 2], tok, "abo")
    pa = _mm_nn(y_a, w_a, F32, "mm_a_fwd")
    pb, merged = _mm_nn(y_b, w_b, None, "mm_b_fwd", epilogue=(
        (z, z, pa), (COL_GATE_A * GATE_TILE, COL_GATE_B * GATE_TILE, 0), (F32, BF16), _gated_merge))
    w_out1 = w_out.reshape(1, D_MODEL, D_MODEL)
    h1, u2 = _mm_rows(merged, w_out.reshape(D_MODEL, D_MODEL), [x], [norm_mlp_w], (F32, BF16),
                      _residual_rms_rows, "mm_out_fwd")
    if ex:
        w_up, = gather_mlp.finish([0], [u2], "up")
    act = ((), (), (F32, BF16), _squared_relu)
    a, r = _mm_nn(u2, w_up, None, "mm_up_fwd_first", epilogue=act, blocks=(0, N_DEV // 2))
    tok = [gather_mlp.pass_on([1], [r], "down")] if ex else []
    a, r = _mm_nn(u2, w_up, None, "mm_up_fwd_second", after=tok, epilogue=act, blocks=(N_DEV // 2, N_DEV // 2),
                  into=(a, r))
    if ex:
        w_down, = gather_mlp.finish([1], [r], "down")
    w_down1 = w_down.reshape(1, D_FF, D_MODEL)
    mlp = _mm_nn(r, w_down1, F32, "mm_down_fwd")
    loss, dh2, dh2b, g_nf = _loss_head(h1, mlp, norm_final_w, target, "loss_head")

    own = ex.parity if ex else jnp.zeros((1,), jnp.int32)

    def sibling_half(weights, name, after=()):
        others = [_mm_tn_half(a_, g_, 1 - own, on, None, nm + "_sibling", after, *cols)
                  for a_, g_, on, nm, *cols in weights]
        rs = _ReduceScatter(others, name) if ex else None
        return rs, others, ([rs.token] if ex else [])

    def own_half(rs, weights, others, after):
        landed = rs.from_sibling(after) if ex else [None] * len(weights)
        sums = [_mm_tn_half(a_, g_, own, on, l, nm + "_own", (), *cols)
                for (a_, g_, on, nm, *cols), l in zip(weights, landed)]
        if ex:
            return [rs.scatter(sums)], None
        return [], [jnp.stack([s_, o_], axis=1).reshape((N_DEV,) + s_.shape[1:]) for s_, o_ in zip(sums, others)]

    down = [(r, dh2b, "a", "mm_down_wgrad")]
    rs_down, others, tok = sibling_half(down, "rs_down")
    da, = _mm_nt(dh2b, w_down1, None, "mm_down_dgrad", after=tok, epilogue=(
        (a,), (0,), (BF16,), lambda dr, av: (dr * (2.0 * jnp.maximum(av, 0.0)),)))
    tok, g_down = own_half(rs_down, down, others, [da])
    up = [(u2, da, "g", "mm_up_wgrad")]
    rs_up, others, tok = sibling_half(up, "rs_up", tok)
    du2 = _mm_nt(da, w_up, F32, "mm_up_dgrad", after=tok)
    tok, g_up = own_half(rs_up, up, others, [du2])
    dh1, dh1b, g_nmlp = _rms_bwd(du2, h1, norm_mlp_w, dh2, (F32, BF16), "rms_mlp_bwd", after=tok)

    dpa, dpb, dga, dgb = _mm_nt(dh1b, w_out1, None, "mm_out_dgrad", epilogue=(
        (z, z, pa, pb), (COL_GATE_A * GATE_TILE, COL_GATE_B * GATE_TILE, 0, 0), (BF16,) * 4, _merge_grads))
    mix = [(y_a, dpa, "g", "mm_a_wgrad"), (y_b, dpb, "g", "mm_b_wgrad"), (merged, dh1b, "a", "mm_out_wgrad")]
    rs_mix, others, tok = sibling_half(mix, "rs_mix")
    dya = _mm_nt(dpa, w_a, F32, "mm_a_dgrad", after=tok)
    dyb = _mm_nt(dpb, w_b, F32, "mm_b_dgrad", after=tok)
    tok, g_mix = own_half(rs_mix, mix, others, [dya, dyb])
    daq, dak, dav, dbias_t = _attn_bwd(z, probs, dyb, "attn_bwd", after=tok)
    dhq, dhf, dhi, dhg, g_lbl, g_hgw = _hgrn2_bwd(z, lb_logits, hg_norm_w, o_raw, s_all, dya, "hgrn2_bwd",
                                                  after=tok)
    dbias_rows = jnp.pad(jnp.transpose(dbias_t.reshape(AT_HEADS // 2, BAND, 2, CHUNK), (3, 0, 2, 1)).reshape(
        CHUNK, AT_HEADS, BAND), ((0, 0), (0, 0), (0, N_REL_PAD - BAND)))
    dz = jnp.concatenate([dhq, dhf, dhi, dhg, daq, dak, dav, dga, dgb], axis=1)
    half = D_MODEL // 2
    lo = [(u, dz, "g", "mm_in_wgrad_lo", (0, half))]
    hi = [(u, dz, "g", "mm_in_wgrad_hi", (half, half))]
    rs_in_lo, others_lo, tok = sibling_half(lo, "rs_in_lo")
    rs_in_hi, others_hi, tok = sibling_half(hi, "rs_in_hi", tok)
    tok, g_in_lo = own_half(rs_in_lo, lo, others_lo, tok)
    du = _mm_nt(dz, w_in, F32, "mm_in_dgrad", after=tok)
    tok, g_in_hi = own_half(rs_in_hi, hi, others_hi, [du])
    grad_x, g_nmix = _rms_bwd(du, x, norm_mix_w, dh1, (F32,), "rms_mix_bwd", after=tok)
    g_rel = _bias_reduce(dbias_rows, "bias_reduce", after=tok)[:, :N_REL]

    small = dict(lb_logits=g_lbl, hg_norm_w=g_hgw[0:1], rel_bias=g_rel, norm_mix_w=g_nmix, norm_mlp_w=g_nmlp,
                 norm_final_w=g_nf)
    if ex:
        grads = [(rs_in_lo, rs_in_hi), rs_mix, rs_up, rs_down]
    else:
        grads = [jnp.concatenate([g_in_lo[0], g_in_hi[0]], axis=1)] + g_mix + [g_up[0], g_down[0]]
    return loss, grad_x, grads, small


def _mm_gathered(u, shard, order, name):
    T, K = u.shape
    _, Nb = shard.shape

    def body(order_ref, u_ref, shard_ref, z_ref, full_ref, wbuf, load_sem, send_sems, recv_sems, local_sem):
        s = pl.program_id(0)
        x, y, c = _position()
        me, sibling = (x, y, c), (x, y, 1 - c)
        chips = [(1 - x, y), (x, 1 - y), (1 - x, 1 - y)]

        def copy(k, block, to, src=None):
            dst = full_ref.at[4 * block[0] + 2 * block[1] + block[2]]
            return pltpu.make_async_remote_copy(
                src_ref=dst if src is None else src, dst_ref=dst,
                send_sem=send_sems.at[k], recv_sem=recv_sems.at[k], device_id=to, device_id_type=MESH)

        @pl.when(s == 0)
        def _():
            local = pltpu.make_async_copy(shard_ref, full_ref.at[4 * x + 2 * y + c], local_sem)
            local.start()
            copy(0, me, sibling, src=shard_ref).start()
            for j, chip in enumerate(chips):
                copy(1 + j, me, (*chip, c), src=shard_ref).start()
            local.wait()

        @pl.when(s == 1)
        def _():
            copy(0, sibling, me).wait_recv()

        for j, chip in enumerate(chips):
            direct, passed = ((2, 4), (3, 5), (6, 7))[j]

            @pl.when(s == direct)
            def _(j=j, chip=chip):
                copy(1 + j, (*chip, c), me).wait_recv()
                copy(4 + j, (*chip, c), sibling).start()

            @pl.when(s == passed)
            def _(j=j, chip=chip):
                copy(4 + j, (*chip, 1 - c), me).wait_recv()

        @pl.when(s < N_EARLY_BLOCKS)
        def _():
            load = pltpu.make_async_copy(full_ref.at[order_ref[s]], wbuf, load_sem)
            load.start()
            load.wait()
            z_ref[...] = jnp.dot(u_ref[...], wbuf[...], preferred_element_type=F32)

        @pl.when(s == N_DEV - 1)
        def _():
            for k in range(7):
                copy(k, me, sibling).wait_send()

    z, full = pl.pallas_call(
        body, name=name,
        grid_spec=pltpu.PrefetchScalarGridSpec(
            num_scalar_prefetch=1, grid=(N_DEV,),
            in_specs=[pl.BlockSpec((T, K), lambda s, order: (0, 0)), ANY],
            out_specs=[pl.BlockSpec((T, Nb), lambda s, order: (0, order[jnp.minimum(s, N_EARLY_BLOCKS - 1)])), ANY],
            scratch_shapes=[pltpu.VMEM((K, Nb), BF16), pltpu.SemaphoreType.DMA,
                            pltpu.SemaphoreType.DMA((7,)), pltpu.SemaphoreType.DMA((7,)), pltpu.SemaphoreType.DMA]),
        out_shape=[jax.ShapeDtypeStruct((T, N_DEV * Nb), F32), jax.ShapeDtypeStruct((N_DEV, K, Nb), BF16)],
        compiler_params=_cparams(("arbitrary",)),
    )(order, u, shard)
    return z, full


N_EARLY_BLOCKS = 6


def _mm_gathered_tail(u, full, z, order, name, after=()):
    T, K = u.shape
    _, _, Nb = full.shape
    n_after = len(after)

    def body(order_ref, u_ref, w_ref, z_in_ref, *rest):
        rest[n_after][...] = jnp.dot(u_ref[...], w_ref[...], preferred_element_type=F32)

    return pl.pallas_call(
        body, name=name,
        grid_spec=pltpu.PrefetchScalarGridSpec(
            num_scalar_prefetch=1, grid=(N_DEV - N_EARLY_BLOCKS,),
            in_specs=[pl.BlockSpec((T, K), lambda s, order: (0, 0)),
                      pl.BlockSpec((None, K, Nb), lambda s, order: (order[N_EARLY_BLOCKS + s], 0, 0)), ANY]
            + [ANY] * n_after,
            out_specs=pl.BlockSpec((T, Nb), lambda s, order: (0, order[N_EARLY_BLOCKS + s]))),
        out_shape=jax.ShapeDtypeStruct(z.shape, z.dtype),
        input_output_aliases={3: 0},
        compiler_params=_cparams(("arbitrary",)),
    )(order, u, full, z, *after)


def _gather_order():
    x, y, c = _position()
    chips = [(1 - x, y), (x, 1 - y), (1 - x, 1 - y)]
    ids = [4 * x + 2 * y + c, 4 * x + 2 * y + (1 - c)]
    ids += [4 * cx + 2 * cy + c for cx, cy in chips[:2]] + [4 * cx + 2 * cy + (1 - c) for cx, cy in chips[:2]]
    ids += [4 * chips[2][0] + 2 * chips[2][1] + c, 4 * chips[2][0] + 2 * chips[2][1] + (1 - c)]
    return jnp.stack(ids).astype(jnp.int32)


HBM = pl.BlockSpec(memory_space=pltpu.HBM)
SEM = pl.BlockSpec(memory_space=pltpu.SEMAPHORE)
DATAFLOW = pltpu.SideEffectType.DATAFLOW_SIDE_EFFECTING


def _split_call(name, bufs, waits=(), starts=None, after=()):
    nb = len(bufs)
    n_new = starts[1] if starts else 0
    wait_sems = [s for w in waits for s in (*w[1], *w[2])]

    def body(*refs):
        b, pos = refs[:nb], nb
        for plan, ss, _, send_idx, recv_idx in waits:
            k = len(ss)
            copies = plan(b, refs[pos:pos + k], refs[pos + k:pos + 2 * k])
            pos += 2 * k
            for i in recv_idx:
                copies[i].wait_recv()
            for i in send_idx:
                copies[i].wait_send()
        outs = refs[pos + len(after):]
        if starts:
            for cp in starts[0](b, outs[nb:nb + n_new], outs[nb + n_new:nb + 2 * n_new]):
                cp.start()
        outs[-1][...] = jnp.zeros_like(outs[-1])

    res = pl.pallas_call(
        body, name=name,
        out_shape=tuple(pltpu.HBM(a.shape, a.dtype) for a in bufs) + (pltpu.SemaphoreType.DMA(()),) * (2 * n_new)
        + (jax.ShapeDtypeStruct((8, 128), F32),),
        in_specs=[HBM] * nb + [SEM] * len(wait_sems) + [ANY] * len(after),
        out_specs=(HBM,) * nb + (SEM,) * (2 * n_new) + (pl.BlockSpec(memory_space=pltpu.VMEM),),
        input_output_aliases={i: i for i in range(nb)},
        compiler_params=pltpu.CompilerParams(has_side_effects=DATAFLOW),
    )(*bufs, *wait_sems, *after)
    return list(res[:nb]), list(res[nb:nb + n_new]), list(res[nb + n_new:nb + 2 * n_new]), res[-1]


def _in_hbm(a):
    return pltpu.with_memory_space_constraint(a, pltpu.HBM)


def _remote(src, dst, send_sem, recv_sem, to):
    return pltpu.make_async_remote_copy(src_ref=src, dst_ref=dst, send_sem=send_sem, recv_sem=recv_sem,
                                        device_id=to, device_id_type=MESH)


def _other_chips():
    x, y, _ = _position()
    return [(1 - x, y), (x, 1 - y), (1 - x, 1 - y)]


def _plan_gather_first(n):
    def plan(b, ss, rs):
        x, y, c = _position()
        to = [(x, y, 1 - c)] + [(*chip, c) for chip in _other_chips()]
        return [_remote(b[w], b[n + w].at[4 * x + 2 * y + c], ss[4 * w + k], rs[4 * w + k], to[k])
                for w in range(n) for k in range(4)]
    return plan, 4 * n


def _plan_gather_pass(n):
    def plan(b, ss, rs):
        x, y, c = _position()
        copies = []
        for w in range(n):
            for j, chip in enumerate(_other_chips()):
                blk = b[n + w].at[4 * chip[0] + 2 * chip[1] + c]
                copies.append(_remote(blk, blk, ss[3 * w + j], rs[3 * w + j], (x, y, 1 - c)))
        return copies
    return plan, 3 * n


def _plan_sibling(n):
    def plan(b, ss, rs):
        x, y, c = _position()
        return [_remote(b[w].at[s], b[n + w].at[s], ss[4 * w + s], rs[4 * w + s], (x, y, 1 - c))
                for w in range(n) for s in range(N_CHIP)]
    return plan, 4 * n


def _plan_scatter(n):
    def plan(b, ss, rs):
        x, y, c = _position()
        return [_remote(b[w].at[2 * chip[0] + chip[1]], b[n + w].at[2 * x + y], ss[3 * w + j], rs[3 * w + j],
                        (*chip, c))
                for w in range(n) for j, chip in enumerate(_other_chips())]
    return plan, 3 * n


class _Gather:
    def __init__(self, shards, after, name):
        self.n, self.name = len(shards), name
        x, y, c = _position()
        placed = [lax.dynamic_update_index_in_dim(lax.empty((N_DEV,) + s.shape, s.dtype), s, 4 * x + 2 * y + c, 0)
                  for s in shards]
        bufs, self.ss, self.rs, self.token = _split_call(
            name + "_start", [_in_hbm(a) for a in list(shards) + placed], starts=_plan_gather_first(self.n),
            after=after)
        self.shards, self.fulls = bufs[:self.n], bufs[self.n:]
        self.passed = {}

    def _sub(self, ids, sems, per):
        return [sems[per * w + k] for w in ids for k in range(per)]

    def pass_on(self, ids, after, tag):
        m = len(ids)
        first = (_plan_gather_first(m)[0], self._sub(ids, self.ss, 4), self._sub(ids, self.rs, 4),
                 [], [4 * i + k for i in range(m) for k in (1, 2, 3)])
        bufs, ss, rs, token = _split_call(
            "%s_pass_%s" % (self.name, tag), [self.shards[w] for w in ids] + [self.fulls[w] for w in ids],
            waits=[first], starts=_plan_gather_pass(m), after=after)
        for i, w in enumerate(ids):
            self.shards[w], self.fulls[w] = bufs[i], bufs[m + i]
        self.passed[tuple(ids)] = (ss, rs)
        return token

    def finish(self, ids, after, tag):
        m = len(ids)
        ss2, rs2 = self.passed[tuple(ids)]
        first = (_plan_gather_first(m)[0], self._sub(ids, self.ss, 4), self._sub(ids, self.rs, 4),
                 list(range(4 * m)), [4 * i for i in range(m)])
        passed = (_plan_gather_pass(m)[0], ss2, rs2, list(range(3 * m)), list(range(3 * m)))
        bufs, _, _, _ = _split_call(
            "%s_finish_%s" % (self.name, tag), [self.shards[w] for w in ids] + [self.fulls[w] for w in ids],
            waits=[first, passed], after=after)
        return bufs[m:]


class _ReduceScatter:
    def __init__(self, others, name):
        self.n, self.name = len(others), name
        lands = [lax.empty(g.shape, g.dtype) for g in others]
        self.bufs, self.ss, self.rs, self.token = _split_call(
            name + "_sibling_start", [_in_hbm(a) for a in list(others) + lands], starts=_plan_sibling(self.n))

    def from_sibling(self, after):
        n = self.n
        bufs, _, _, _ = _split_call(
            self.name + "_sibling_wait", self.bufs,
            waits=[(_plan_sibling(n)[0], self.ss, self.rs, list(range(4 * n)), list(range(4 * n)))], after=after)
        return bufs[n:]

    def scatter(self, sums):
        lands = [lax.empty(s.shape, s.dtype) for s in sums]
        self.bufs, self.ss, self.rs, token = _split_call(
            self.name + "_scatter_start", [_in_hbm(a) for a in list(sums) + lands], starts=_plan_scatter(self.n))
        return token

    def finish(self, after):
        n = self.n
        bufs, _, _, _ = _split_call(
            self.name + "_scatter_wait", self.bufs,
            waits=[(_plan_scatter(n)[0], self.ss, self.rs, list(range(3 * n)), list(range(3 * n)))], after=after)
        return bufs[:n], bufs[n:]


class _Exchanges:
    def __init__(self, parity, order):
        self.parity, self.order = parity, order


def _plan_everyone():
    def plan(b, ss, rs):
        x, y, c = _position()
        return [_remote(b[0], b[1].at[4 * x + 2 * y + c], ss[k - 1], rs[k - 1],
                        (x ^ ((k >> 2) & 1), y ^ ((k >> 1) & 1), c ^ (k & 1))) for k in range(1, N_DEV)]
    return plan, N_DEV - 1


class _GatherSmall:
    def __init__(self, packed, after, name):
        self.name = name
        x, y, c = _position()
        placed = lax.dynamic_update_index_in_dim(lax.empty((N_DEV,) + packed.shape, packed.dtype), packed,
                                                 4 * x + 2 * y + c, 0)
        self.bufs, self.ss, self.rs, self.token = _split_call(
            name + "_start", [_in_hbm(packed), _in_hbm(placed)], starts=_plan_everyone(), after=after)

    def finish(self, after):
        everyone = list(range(N_DEV - 1))
        bufs, _, _, _ = _split_call(
            self.name + "_wait", self.bufs, waits=[(_plan_everyone()[0], self.ss, self.rs, everyone, everyone)],
            after=after)
        return bufs[1]


def _adamw_math(w, g, m, v):
    m = ADAM_B1 * m + (1.0 - ADAM_B1) * g
    v = ADAM_B2 * v + (1.0 - ADAM_B2) * (g * g)
    m_hat = m / (1.0 - ADAM_B1 ** ADAM_STEP)
    v_hat = v / (1.0 - ADAM_B2 ** ADAM_STEP)
    delta = -ADAM_LR * (m_hat / (jnp.sqrt(v_hat) + ADAM_EPS) + ADAM_WD * w)
    return delta, m, v


def _adamw_big_landed(w, m, v, parts, lands, slot, name, row0=0, into=None):
    R, C = w.shape
    rows = parts.shape[1]
    tr = _pick(rows, (256,))
    first = row0 // tr
    n_into = len(into) if into else 0

    def body(slot_ref, w_ref, m_ref, v_ref, own_ref, l1_ref, l2_ref, l3_ref, *rest):
        g = own_ref[...].astype(F32)
        for ref in (l1_ref, l2_ref, l3_ref):
            g = g + ref[...].astype(F32)
        for o_ref, res in zip(rest[n_into:], (g,) + _adamw_math(w_ref[...], g, m_ref[...], v_ref[...])):
            o_ref[...] = res

    blk = pl.BlockSpec((tr, C), lambda i, slot: (first + i, 0))

    def chip(k):
        return pl.BlockSpec((None, tr, C), lambda i, slot: ((slot[0] + k) % N_CHIP, i, 0))

    out = jax.ShapeDtypeStruct((R, C), F32)
    return pl.pallas_call(
        body, name=name,
        grid_spec=pltpu.PrefetchScalarGridSpec(
            num_scalar_prefetch=1, grid=(rows // tr,),
            in_specs=[blk, blk, blk, chip(0), chip(1), chip(2), chip(3)] + [ANY] * n_into,
            out_specs=[blk, blk, blk, blk]),
        out_shape=[out, out, out, out],
        input_output_aliases={8 + j: j for j in range(n_into)},
        compiler_params=_cparams(("parallel",)),
    )(slot, w, m, v, parts, lands, lands, lands, *(into or ()))


def _adamw_small(w, m, v, gathered, name):
    R = w.shape[0]

    def body(w_ref, m_ref, v_ref, p_ref, g_ref, d_ref, nm_ref, nv_ref):
        g = p_ref[0]
        for s in range(1, N_DEV):
            g = g + p_ref[s]
        d, nm, nv = _adamw_math(w_ref[...], g, m_ref[...], v_ref[...])
        g_ref[...] = g
        d_ref[...] = d
        nm_ref[...] = nm
        nv_ref[...] = nv

    out = jax.ShapeDtypeStruct((R, 128), F32)
    return pl.pallas_call(
        body, name=name, out_shape=[out, out, out, out],
    )(w, m, v, gathered)


SMALL_NAMES = ("lb_logits", "hg_norm_w", "rel_bias", "norm_mix_w", "norm_mlp_w", "norm_final_w")
SMALL_SHAPES = {"lb_logits": (2, HG_WIDTH), "hg_norm_w": (1, HG_DK), "rel_bias": (AT_HEADS, N_REL_PAD),
                "norm_mix_w": (1, D_MODEL), "norm_mlp_w": (1, D_MODEL), "norm_final_w": (1, D_MODEL)}


LOSS_ROW = sum(r * c for r, c in SMALL_SHAPES.values()) // 128


def _pack_small(parts, loss_row=None):
    rows = []
    for nme in SMALL_NAMES:
        p = parts[nme]
        if nme == "rel_bias":
            p = jnp.pad(p, ((0, 0), (0, N_REL_PAD - N_REL)))
        rows.append(p.reshape(-1, 128))
    if loss_row is not None:
        rows.append(loss_row)
    flat = jnp.concatenate(rows, axis=0)
    return jnp.pad(flat, ((0, SMALL_ROWS - flat.shape[0]), (0, 0)))


def _unpack_small(packed):
    out, at = {}, 0
    for nme in SMALL_NAMES:
        shp = SMALL_SHAPES[nme]
        nrow = shp[0] * shp[1] // 128
        p = packed[at:at + nrow].reshape(shp)
        at += nrow
        out[nme] = p[:, :N_REL] if nme == "rel_bias" else p
    return out


BIG_NAMES = ("w_in", "w_branch_a", "w_branch_b", "w_out", "w_up", "w_down")


def kernel(x, w_in, lb_logits, hg_norm_w, rel_bias, w_branch_a, w_branch_b, w_out, norm_mix_w, norm_mlp_w, w_up, w_down, norm_final_w, loss_target, m_w_in, m_lb_logits, m_hg_norm_w, m_rel_bias, m_w_branch_a, m_w_branch_b, m_w_out, m_norm_mix_w, m_norm_mlp_w, m_w_up, m_w_down, m_norm_final_w, v_w_in, v_lb_logits, v_hg_norm_w, v_rel_bias, v_w_branch_a, v_w_branch_b, v_w_out, v_norm_mix_w, v_norm_mlp_w, v_w_up, v_w_down, v_norm_final_w):
    big_w = [w_in[0], w_branch_a[0], w_branch_b[0], w_out[0], w_up[0], w_down[0]]
    big_m = [m_w_in[0], m_w_branch_a[0], m_w_branch_b[0], m_w_out[0], m_w_up[0], m_w_down[0]]
    big_v = [v_w_in[0], v_w_branch_a[0], v_w_branch_b[0], v_w_out[0], v_w_up[0], v_w_down[0]]

    shards = [w.astype(BF16) for w in big_w[:4]] + big_w[4:]
    parity = lax.axis_index("c").astype(jnp.int32).reshape(1)
    loss_part, grad_x, chip_parts, small = _local_step(
        x[0], loss_target[0], lb_logits, hg_norm_w, rel_bias[0], norm_mix_w, norm_mlp_w,
        norm_final_w.reshape(1, D_MODEL), shards[0], shards[1:], _Exchanges(parity, _gather_order()))
    (rs_in_lo, rs_in_hi), rs_mix, rs_up, rs_down = chip_parts
    slot =(2 * lax.axis_index("x") + lax.axis_index("y")).astype(jnp.int32).reshape(1)
    big = {}

    def finish(rs, names, after):
        sums, lands = rs.finish(after)
        for nme, own, land in zip(names, sums, lands):
            i = BIG_NAMES.index(nme)
            big[nme] = _adamw_big_landed(big_w[i], big_m[i], big_v[i], own, land, slot, "adamw_" + nme)
        return [big[nme][1] for nme in names]

    gather_small = _GatherSmall(_pack_small(small, loss_part[0:1]), [grad_x], "gather_small")
    done = finish(rs_down, ["w_down"], [grad_x, gather_small.token])
    done = finish(rs_up, ["w_up"], done)
    done = finish(rs_mix, ["w_branch_a", "w_branch_b", "w_out"], done)

    sw = dict(lb_logits=lb_logits, hg_norm_w=hg_norm_w, rel_bias=rel_bias[0], norm_mix_w=norm_mix_w,
              norm_mlp_w=norm_mlp_w, norm_final_w=norm_final_w.reshape(1, D_MODEL))
    sm = dict(lb_logits=m_lb_logits, hg_norm_w=m_hg_norm_w, rel_bias=m_rel_bias[0], norm_mix_w=m_norm_mix_w,
              norm_mlp_w=m_norm_mlp_w, norm_final_w=m_norm_final_w.reshape(1, D_MODEL))
    sv = dict(lb_logits=v_lb_logits, hg_norm_w=v_hg_norm_w, rel_bias=v_rel_bias[0], norm_mix_w=v_norm_mix_w,
              norm_mlp_w=v_norm_mlp_w, norm_final_w=v_norm_final_w.reshape(1, D_MODEL))
    gathered = gather_small.finish(done)
    small_packed = _adamw_small(_pack_small(sw), _pack_small(sm), _pack_small(sv), gathered, "adamw_small")
    loss = small_packed[0][LOSS_ROW, 0]
    small_out = [_unpack_small(p) for p in small_packed]

    (own,), (land,) = rs_in_lo.finish(done + [small_packed[0]])
    lo = _adamw_big_landed(big_w[0], big_m[0], big_v[0], own, land, slot, "adamw_w_in_lo")
    (own,), (land,) = rs_in_hi.finish([lo[1]])
    big["w_in"] = _adamw_big_landed(big_w[0], big_m[0], big_v[0], own, land, slot, "adamw_w_in_hi",
                                    row0=D_MODEL // 2, into=lo)

    def leaf(kind, nme):
        if nme in BIG_NAMES:
            return big[nme][kind][None]
        p = small_out[kind][nme]
        if nme == "rel_bias":
            return p[None]
        if nme == "norm_final_w":
            return p.reshape(D_MODEL)
        return p

    order = ("w_in", "lb_logits", "hg_norm_w", "rel_bias", "w_branch_a", "w_branch_b", "w_out", "norm_mix_w",
             "norm_mlp_w", "w_up", "w_down", "norm_final_w")
    outs = [loss, grad_x[None]]
    for kind in range(4):
        outs += [leaf(kind, nme) for nme in order]
    return tuple(outs)
```

```python
import jax
import jax.numpy as jnp
from jax import lax
from jax.experimental import pallas as pl
from jax.experimental.pallas import tpu as pltpu

F32 = jnp.float32
BF16 = jnp.bfloat16
HIGHEST = lax.Precision.HIGHEST
MESH = pl.DeviceIdType.MESH

D_MODEL = 2048
HG_HEADS = 8
HG_DK = 128
HG_WIDTH = 1024
AT_HEADS = 16
AT_DH = 64
AT_WIDTH = 1024
CHUNK = 64
LEFT_CHUNKS = 8
BAND = (LEFT_CHUNKS + 1) * CHUNK
PAD = LEFT_CHUNKS * CHUNK
REL_CLIP = 256
N_REL = 2 * REL_CLIP + 1
N_REL_PAD = 640
D_FF = 4 * D_MODEL
EPS = 1e-6
N_DEV = 8
N_CHIP = 4

ADAM_LR = 0.001
ADAM_B1 = 0.9
ADAM_B2 = 0.999
ADAM_EPS = 1e-08
ADAM_WD = 0.01
ADAM_STEP = 10

COL_HQ, COL_HF, COL_HI, COL_HG = 0, 8, 16, 24
COL_AQ, COL_AK, COL_AV = 32, 40, 48
COL_GATE_A, COL_GATE_B = 7, 9

VMEM_LIMIT = 56 * 1024 * 1024
SMALL_ROWS = 152


def _cparams(sem=None, **kw):
    if sem is not None:
        kw["dimension_semantics"] = sem
    return pltpu.CompilerParams(vmem_limit_bytes=VMEM_LIMIT, **kw)


def _pick(n, cands):
    for c in cands:
        if n % c == 0:
            return c
    return n


def _sigmoid(x):
    return 1.0 / (1.0 + jnp.exp(-x))


ANY = pl.BlockSpec(memory_space=pl.ANY)


def _position():
    return lax.axis_index("x"), lax.axis_index("y"), lax.axis_index("c")


def _call(body, args, *, name, grid, in_specs, out_specs, out_shape, scratch_shapes=(), sem=None, after=(),
          aliases=None):
    n_in = len(args)

    def ordered(*refs):
        body(*refs[:n_in], *refs[n_in + len(after):])

    return list(pl.pallas_call(
        ordered if after else body, name=name, grid=grid, in_specs=list(in_specs) + [ANY] * len(after),
        out_specs=out_specs, out_shape=out_shape, scratch_shapes=list(scratch_shapes),
        input_output_aliases=aliases or {}, compiler_params=_cparams(sem))(*args, *after))


MAX_CONTRACTION_TILE = 4096


def _accumulate(part, acc_ref, step, n_steps, finish):
    if n_steps == 1:
        finish(part)
        return

    @pl.when(step == 0)
    def _():
        acc_ref[...] = part

    @pl.when(step > 0)
    def _():
        acc_ref[...] += part

    @pl.when(step == n_steps - 1)
    def _():
        finish(acc_ref[...])


def _mm_nn(a, wb, out_dtype, name, after=(), epilogue=None, blocks=None, into=()):
    M, K = a.shape
    NB, K2, Nb = wb.shape
    assert K == K2
    j0, nj = blocks or (0, NB)
    n_into = len(into)
    tm = min(M, 1024)
    tk = min(K, MAX_CONTRACTION_TILE)
    tn = _pick(Nb, (512, 1408, 256))
    nk = K // tk
    nn = Nb // tn
    extra, first_cols, out_dtypes, fn = epilogue or ((), (), (out_dtype,), lambda total: (total,))
    n_extra, n_out = len(extra), len(out_dtypes)

    def body(a_ref, b_ref, *rest):
        def finish(total):
            results = fn(total, *[r[...] for r in rest[:n_extra]])
            for o_ref, res, dt in zip(rest[n_extra + n_into:n_extra + n_into + n_out], results, out_dtypes):
                o_ref[...] = res.astype(dt)

        part = jnp.dot(a_ref[...], b_ref[...], preferred_element_type=F32)
        _accumulate(part, rest[-1], pl.program_id(3), nk, finish)

    def tile(first):
        return pl.BlockSpec((tm, tn), lambda m, j, n, k: (m, first + (j0 + j) * nn + n))

    outs = _call(
        body, (a, wb) + tuple(extra) + tuple(into), name=name, grid=(M // tm, nj, nn, nk),
        in_specs=[pl.BlockSpec((tm, tk), lambda m, j, n, k: (m, k)),
                  pl.BlockSpec((None, tk, tn), lambda m, j, n, k: (j0 + j, k, n))]
        + [tile(col // tn) for col in first_cols] + [ANY] * n_into,
        out_specs=[tile(0)] * n_out,
        out_shape=[jax.ShapeDtypeStruct((M, NB * Nb), dt) for dt in out_dtypes],
        scratch_shapes=[] if nk == 1 else [pltpu.VMEM((tm, tn), F32)],
        sem=("parallel", "parallel", "parallel", "arbitrary"), after=after,
        aliases={2 + n_extra + i: i for i in range(n_into)})
    return outs if epilogue else outs[0]


def _squared_relu(a):
    ra = jnp.maximum(a, 0.0)
    return a, ra * ra


def _gated_merge(pb, za, zb, pa):
    return pb, _sigmoid(za) * pa + _sigmoid(zb) * pb


def _mm_nt(a, wb, out_dtype, name, after=(), epilogue=None):
    M, N = a.shape
    NB, K, Nb = wb.shape
    assert N == NB * Nb
    tm = min(M, 1024)
    n_tiles_live = 1 + (len(epilogue[0]) + len(epilogue[2]) if epilogue else 0)
    tko = _pick(K, (1024,)) if n_tiles_live <= 3 else _pick(K, (512,))
    tc = _pick(Nb, (2048, 1024, 1408, 256))
    nc = Nb // tc
    jb = max([d for d in (8, 4, 2, 1) if NB % d == 0 and d * tc <= MAX_CONTRACTION_TILE]) if nc == 1 else 1
    nsteps = (NB // jb) * nc
    extra, first_cols, out_dtypes, fn = epilogue or ((), (), (out_dtype,), lambda total: (total,))
    n_extra, n_out = len(extra), len(out_dtypes)

    def body(a_ref, b_ref, *rest):
        def finish(total):
            results = fn(total, *[r[...] for r in rest[:n_extra]])
            for o_ref, res, dt in zip(rest[n_extra:n_extra + n_out], results, out_dtypes):
                o_ref[...] = res.astype(dt)

        part = sum(lax.dot_general(a_ref[:, i * tc:(i + 1) * tc], b_ref[i], (((1,), (1,)), ((), ())),
                                   preferred_element_type=F32) for i in range(jb))
        _accumulate(part, rest[-1], pl.program_id(2) * nc + pl.program_id(3), nsteps, finish)

    def tile(first):
        return pl.BlockSpec((tm, tko), lambda m, ko, j, c: (m, first + ko))

    outs = _call(
        body, (a, wb) + tuple(extra), name=name,
        grid=(M // tm, K // tko, NB // jb, nc),
        in_specs=[pl.BlockSpec((tm, jb * tc), lambda m, ko, j, c: (m, j * nc + c)),
                  pl.BlockSpec((jb, tko, tc), lambda m, ko, j, c: (j, ko, c))] + [tile(col // tko) for col in first_cols],
        out_specs=[tile(0)] * n_out,
        out_shape=[jax.ShapeDtypeStruct((M, K), dt) for dt in out_dtypes],
        scratch_shapes=[] if nsteps == 1 else [pltpu.VMEM((tm, tko), F32)],
        sem=("parallel", "parallel", "arbitrary", "arbitrary"), after=after)
    return outs if epilogue else outs[0]


ROWS_TILE = 512
ROWS_PIECE = 128


def _mm_rows(a, w, extras, vectors, row_dtypes, fn, name):
    M, K = a.shape
    N = w.shape[1]
    tm = min(M, ROWS_TILE)
    n_e, n_v = len(extras), len(vectors)

    def body(a_ref, w_ref, *rest):
        tiles, vecs, outs, product_ref = rest[:n_e], rest[n_e:n_e + n_v], rest[n_e + n_v:-1], rest[-1]
        product_ref[...] = jnp.dot(a_ref[...], w_ref[...], preferred_element_type=F32)
        for i in range(tm // ROWS_PIECE):
            piece = slice(i * ROWS_PIECE, (i + 1) * ROWS_PIECE)
            results = fn(product_ref[piece, :], *[t[piece, :] for t in tiles], *[v[...] for v in vecs])
            for o_ref, res, dt in zip(outs, results, row_dtypes):
                o_ref[piece, :] = res.astype(dt)

    row = pl.BlockSpec((tm, N), lambda m: (m, 0))
    return _call(
        body, (a, w) + tuple(extras) + tuple(vectors), name=name, grid=(M // tm,),
        in_specs=[pl.BlockSpec((tm, K), lambda m: (m, 0)), pl.BlockSpec((K, N), lambda m: (0, 0))]
        + [row] * n_e + [pl.BlockSpec((1, N), lambda m: (0, 0))] * n_v,
        out_specs=[row] * len(row_dtypes),
        out_shape=[jax.ShapeDtypeStruct((M, N), dt) for dt in row_dtypes],
        scratch_shapes=[pltpu.VMEM((tm, N), F32)], sem=("parallel",))


def _rms(h, w):
    return h * lax.rsqrt(jnp.mean(h * h, axis=-1, keepdims=True) + EPS) * w


def _residual_rms_rows(mix, x, w):
    h = x + mix
    return h, _rms(h, w)


def _mm_tn_half(a, g, which, blocks_on, add, name, after=(), a_cols=None):
    M, Ka = a.shape
    N = g.shape[1]
    first_col = 0
    if a_cols is not None:
        first_col, Ka = a_cols
    if blocks_on == "g":
        rows, cols = _pick(Ka, (1024,)), N // N_DEV
        tn = _pick(cols, (512, 1408, 256))
        nn = cols // tn
        grid = (Ka // rows, N_CHIP, nn)
        a_spec = pl.BlockSpec((M, rows), lambda ka, s, n, w: (0, first_col // rows + ka))
        g_spec = pl.BlockSpec((M, tn), lambda ka, s, n, w: (0, (2 * s + w[0]) * nn + n))
        out_rows = Ka
    else:
        rows, cols = Ka // N_DEV, N
        tn = _pick(cols, (2048, 512))
        nn = cols // tn
        grid = (1, N_CHIP, nn)
        a_spec = pl.BlockSpec((M, rows), lambda ka, s, n, w: (0, 2 * s + w[0]))
        g_spec = pl.BlockSpec((M, tn), lambda ka, s, n, w: (0, n))
        out_rows = rows
    o_spec = pl.BlockSpec((None, rows, tn), lambda ka, s, n, w: (s, ka, n))
    n_add = 0 if add is None else 1

    def body(which_ref, a_ref, g_ref, *rest):
        acc = lax.dot_general(a_ref[...], g_ref[...], (((0,), (0,)), ((), ())), preferred_element_type=F32)
        if n_add:
            acc = acc + rest[0][...].astype(F32)
        rest[-1][...] = acc.astype(BF16)

    return pl.pallas_call(
        body, name=name,
        grid_spec=pltpu.PrefetchScalarGridSpec(
            num_scalar_prefetch=1, grid=grid,
            in_specs=[a_spec, g_spec] + [o_spec] * n_add + [ANY] * len(after),
            out_specs=o_spec),
        out_shape=jax.ShapeDtypeStruct((N_CHIP, out_rows, cols), BF16),
        compiler_params=_cparams(("parallel", "parallel", "parallel")),
    )(which, a, g, *(() if add is None else (add,)), *after)


ROW_TILE = 256


def _rms_fwd(x, w, name):
    T, Dm = x.shape

    def body(x_ref, w_ref, u_ref):
        xv = x_ref[...]
        r = lax.rsqrt(jnp.mean(xv * xv, axis=-1, keepdims=True) + EPS)
        u_ref[...] = (xv * r * w_ref[...]).astype(BF16)

    return pl.pallas_call(
        body, name=name, grid=(T // ROW_TILE,),
        in_specs=[pl.BlockSpec((ROW_TILE, Dm), lambda i: (i, 0)), pl.BlockSpec((1, Dm), lambda i: (0, 0))],
        out_specs=pl.BlockSpec((ROW_TILE, Dm), lambda i: (i, 0)),
        out_shape=jax.ShapeDtypeStruct((T, Dm), BF16),
        compiler_params=_cparams(("parallel",)),
    )(x, w)


def _loss_head(h1, mlp, wf, target, name):
    T, Dm = h1.shape

    def body(h_ref, m_ref, w_ref, t_ref, loss_ref, dh_ref, dhb_ref, dw_ref):
        i = pl.program_id(0)
        h = h_ref[...] + m_ref[...]
        r = lax.rsqrt(jnp.mean(h * h, axis=-1, keepdims=True) + EPS)
        xh = h * r
        wv = w_ref[...]
        e = xh * wv - t_ref[...]
        part = 0.5 * jnp.sum(jnp.mean(e * e, axis=-1, keepdims=True), axis=0, keepdims=True)
        dy = e * (1.0 / Dm)
        dw = jnp.sum(dy * xh, axis=0, keepdims=True)
        gy = dy * wv
        dh = r * (gy - xh * jnp.mean(gy * xh, axis=-1, keepdims=True))
        dh_ref[...] = dh
        dhb_ref[...] = dh.astype(BF16)

        @pl.when(i == 0)
        def _():
            loss_ref[...] = jnp.zeros_like(loss_ref)
            dw_ref[...] = jnp.zeros_like(dw_ref)

        loss_ref[...] += jnp.broadcast_to(part, loss_ref.shape)
        dw_ref[...] += dw

    row = pl.BlockSpec((ROW_TILE, Dm), lambda i: (i, 0))
    vec = pl.BlockSpec((1, Dm), lambda i: (0, 0))
    return pl.pallas_call(
        body, name=name, grid=(T // ROW_TILE,),
        in_specs=[row, row, vec, row],
        out_specs=[pl.BlockSpec((8, 128), lambda i: (0, 0)), row, row, vec],
        out_shape=[jax.ShapeDtypeStruct((8, 128), F32), jax.ShapeDtypeStruct((T, Dm), F32),
                   jax.ShapeDtypeStruct((T, Dm), BF16), jax.ShapeDtypeStruct((1, Dm), F32)],
        compiler_params=_cparams(("arbitrary",)),
    )(h1, mlp, wf, target)


def _rms_bwd(dyn, x, w, dres, dx_dtypes, name, after=()):
    T, Dm = x.shape
    n_dx = len(dx_dtypes)

    def body(g_ref, x_ref, w_ref, r_ref, *outs):
        i = pl.program_id(0)
        xv = x_ref[...]
        r = lax.rsqrt(jnp.mean(xv * xv, axis=-1, keepdims=True) + EPS)
        xh = xv * r
        g = g_ref[...]
        dw = jnp.sum(g * xh, axis=0, keepdims=True)
        gy = g * w_ref[...]
        dx = r_ref[...] + r * (gy - xh * jnp.mean(gy * xh, axis=-1, keepdims=True))
        for dx_ref, dt in zip(outs, dx_dtypes):
            dx_ref[...] = dx.astype(dt)
        dw_ref = outs[n_dx]

        @pl.when(i == 0)
        def _():
            dw_ref[...] = jnp.zeros_like(dw_ref)

        dw_ref[...] += dw

    row = pl.BlockSpec((ROW_TILE, Dm), lambda i: (i, 0))
    vec = pl.BlockSpec((1, Dm), lambda i: (0, 0))
    return _call(
        body, (dyn, x, w, dres), name=name, grid=(T // ROW_TILE,),
        in_specs=[row, row, vec, row],
        out_specs=[row] * n_dx + [vec],
        out_shape=[jax.ShapeDtypeStruct((T, Dm), dt) for dt in dx_dtypes] + [jax.ShapeDtypeStruct((1, Dm), F32)],
        sem=("arbitrary",), after=after)


GATE_TILE = 1024


def _merge_grads(d, za, zb, pa, pb):
    ga = _sigmoid(za)
    gb = _sigmoid(zb)
    return d * ga, d * gb, d * pa * ga * (1.0 - ga), d * pb * gb * (1.0 - gb)


def _dot_hi(a, b, dims):
    return lax.dot_general(a, b, (dims, ((), ())), precision=HIGHEST, preferred_element_type=F32)


NN = ((1,), (0,))
NT = ((1,), (1,))
TN = ((0,), (0,))


def _hg_gates(hq, hf, lb):
    sq = _sigmoid(hq)
    q = hq * sq * (HG_DK ** -0.5)
    f = _sigmoid(hf)
    g = lb + (1.0 - lb) * f
    return q, sq, f, g, jnp.log(g), 1.0 - g


def _tri(lower):
    r = lax.broadcasted_iota(jnp.int32, (CHUNK, CHUNK), 0)
    c = lax.broadcasted_iota(jnp.int32, (CHUNK, CHUNK), 1)
    return jnp.where((r >= c) if lower else (r <= c), 1.0, 0.0).astype(BF16)


def _running_sum(tri, x):
    return sum(jnp.dot(tri, piece, preferred_element_type=F32) for piece in _split3(x))


GROUP = 16
N_GROUPS = CHUNK // GROUP
BWD_CHUNKS_PER_TRIP = 4


def _dot_bf16(a, b, dims):
    return lax.dot_general(a.astype(BF16), b.astype(BF16), (dims, ((), ())), preferred_element_type=F32)


def _rows_iota():
    return lax.broadcasted_iota(jnp.int32, (CHUNK, HG_DK), 0)


def _by_query_group(q, kk, b, g):
    r0 = GROUP * g
    b0 = b[r0:r0 + 1]
    decay = jnp.exp(b[r0:r0 + GROUP] - b0)
    ks = jnp.where(_rows_iota() < r0, kk * jnp.exp(jnp.minimum(b0 - b, 0.0)), 0.0)
    return q[r0:r0 + GROUP] * decay, ks, decay


def _by_key_group(q, kk, b, j):
    r1 = GROUP * (j + 1)
    b1 = b[r1 - 1:r1]
    decay = jnp.exp(b1 - b[r1 - GROUP:r1])
    qs = jnp.where(_rows_iota() >= r1, q * jnp.exp(jnp.minimum(b - b1, 0.0)), 0.0)
    return qs, kk[r1 - GROUP:r1] * decay, decay


def _scores_between_groups(q, kk, b):
    blocks = [jnp.zeros((GROUP, CHUNK), F32)]
    for g in range(1, N_GROUPS):
        qs, ks, _ = _by_query_group(q, kk, b, g)
        blocks.append(_dot_bf16(qs, ks, NT))
    return jnp.concatenate(blocks, axis=0)


def _hgrn2_fwd(z, lb_logits, hg_norm_w, name, after=()):
    T = z.shape[0]
    n_chunks = T // CHUNK

    def body(hq_ref, hf_ref, hi_ref, hg_ref, lbl_ref, nw_ref, o_ref, ya_ref, sall_ref, st_ref):
        lbl = lbl_ref[...]
        lb = 1.0 / (1.0 + jnp.exp(lbl[1:2, :] - lbl[0:1, :]))
        st_ref[...] = jnp.zeros_like(st_ref)
        tri = _tri(True)
        row8 = lax.broadcasted_iota(jnp.int32, (8, HG_DK), 0)

        def chunk(c, carry):
            rows = pl.ds(pl.multiple_of(c * CHUNK, CHUNK), CHUNK)
            q, _, _, _, lg, kk = _hg_gates(hq_ref[rows, :], hf_ref[rows, :], lb)
            v = hi_ref[rows, :]
            b = _running_sum(tri, lg)
            st = st_ref[...]
            sall_ref[c] = st
            for grp in range(N_GROUPS):
                r0 = GROUP * grp
                for h8 in range(GROUP // 8):
                    n = 8 * (h8 + 1)
                    bs, ks, vs = b[r0:r0 + n], kk[r0:r0 + n], v[r0:r0 + n]
                    sidx = lax.broadcasted_iota(jnp.int32, (n, HG_DK), 0)
                    blk = jnp.zeros((8, HG_DK), F32)
                    for i in range(8):
                        t = r0 + 8 * h8 + i
                        e = jnp.where(sidx <= 8 * h8 + i, jnp.exp(b[t:t + 1] - bs), 0.0)
                        p = jnp.sum(e * ks * q[t:t + 1], axis=1, keepdims=True)
                        ot = jnp.sum(p * vs, axis=0, keepdims=True)
                        blk = blk + jnp.where(row8 == i, ot, 0.0)
                    o_ref[pl.ds(pl.multiple_of(c * CHUNK + r0 + 8 * h8, 8), 8), :] = blk
            o_ref[rows, :] += _dot_hi(q * jnp.exp(b), st, NT) + _dot_bf16(_scores_between_groups(q, kk, b), v, NN)
            bl = b[CHUNK - 1:CHUNK]
            ke = kk * jnp.exp(bl - b)
            st_ref[...] = st * jnp.exp(bl) + _dot_hi(v, ke, TN)
            return carry

        lax.fori_loop(0, n_chunks, chunk, 0, unroll=2)
        o = o_ref[...]
        r = lax.rsqrt(jnp.mean(o * o, axis=-1, keepdims=True) + EPS)
        hg = hg_ref[...]
        ya_ref[...] = (o * r * nw_ref[...] * (hg * _sigmoid(hg))).astype(BF16)

    def col(base):
        return pl.BlockSpec((T, HG_DK), lambda h: (0, base + h))

    return _call(
        body, (z, z, z, z, lb_logits, hg_norm_w), name=name, grid=(HG_HEADS,),
        in_specs=[col(COL_HQ), col(COL_HF), col(COL_HI), col(COL_HG),
                  pl.BlockSpec((2, HG_DK), lambda h: (0, h)), pl.BlockSpec((1, HG_DK), lambda h: (0, 0))],
        out_specs=[col(0), col(0), pl.BlockSpec((None, n_chunks, HG_DK, HG_DK), lambda h: (h, 0, 0, 0))],
        out_shape=[jax.ShapeDtypeStruct((T, HG_WIDTH), F32), jax.ShapeDtypeStruct((T, HG_WIDTH), BF16),
                   jax.ShapeDtypeStruct((HG_HEADS, n_chunks, HG_DK, HG_DK), F32)],
        scratch_shapes=[pltpu.VMEM((HG_DK, HG_DK), F32)],
        sem=("parallel",), after=after)


def _hgrn2_bwd(z, lb_logits, hg_norm_w, o_raw, s_all, dya, name, after=()):
    T = z.shape[0]
    n_chunks = T // CHUNK

    def body(hq_ref, hf_ref, hi_ref, hg_ref, lbl_ref, nw_ref, o_ref, sall_ref, dya_ref,
             dhq_ref, dhf_ref, dhi_ref, dhg_ref, dlbl_ref, dnw_ref,
             do_ref, dst_ref, dlb_ref, *per_chunk):
        h = pl.program_id(0)
        lbl = lbl_ref[...]
        lb = 1.0 / (1.0 + jnp.exp(lbl[1:2, :] - lbl[0:1, :]))

        o = o_ref[...]
        r = lax.rsqrt(jnp.mean(o * o, axis=-1, keepdims=True) + EPS)
        oh = o * r
        nw = nw_ref[...]
        hg = hg_ref[...]
        sg = _sigmoid(hg)
        dy = dya_ref[...]
        d_on = dy * (hg * sg)
        dhg_ref[...] = (dy * (oh * nw) * (sg * (1.0 + hg * (1.0 - sg)))).astype(BF16)
        dnw = jnp.sum(d_on * oh, axis=0, keepdims=True)
        gy = d_on * nw
        do_ref[...] = r * (gy - oh * jnp.mean(gy * oh, axis=-1, keepdims=True))

        @pl.when(h == 0)
        def _():
            dnw_ref[...] = jnp.zeros_like(dnw_ref)

        dnw_ref[...] += jnp.broadcast_to(dnw, dnw_ref.shape)

        dst_ref[...] = jnp.zeros_like(dst_ref)
        dlb_ref[...] = jnp.zeros_like(dlb_ref)
        tri = _tri(True)
        tri_t = _tri(False)
        row8 = lax.broadcasted_iota(jnp.int32, (8, HG_DK), 0)
        row_group = lax.broadcasted_iota(jnp.int32, (CHUNK, CHUNK), 0) // GROUP
        col_group = lax.broadcasted_iota(jnp.int32, (CHUNK, CHUNK), 1) // GROUP
        earlier_group = col_group < row_group
        later_group = col_group > row_group

        def chunk(c, dq_ref, dk_ref, dv_ref):
            rows = pl.ds(pl.multiple_of(c * CHUNK, CHUNK), CHUNK)
            hq = hq_ref[rows, :]
            q, sq, f, g, lg, kk = _hg_gates(hq, hf_ref[rows, :], lb)
            v = hi_ref[rows, :]
            do = do_ref[rows, :]
            b = _running_sum(tri, lg)
            eb = jnp.exp(b)
            bl = b[CHUNK - 1:CHUNK]
            ebl = jnp.exp(bl)
            ekb = jnp.exp(bl - b)
            qe = q * eb
            ke = kk * ekb
            st = sall_ref[c]
            dst = dst_ref[...]
            dqe = _dot_bf16(do, st, NN)
            dke = _dot_bf16(v, dst, NN)
            dv_inter = _dot_bf16(ke, dst, NT)
            d_ebl = jnp.sum(st * dst, axis=0, keepdims=True)
            dst_ref[...] = dst * ebl + _dot_bf16(do, qe, TN)

            dk_ref[...] = jnp.zeros_like(dk_ref)
            dv_ref[...] = jnp.zeros_like(dv_ref)
            for grp in range(N_GROUPS):
                r0 = GROUP * grp
                for h8 in range(GROUP // 8):
                    n = 8 * (h8 + 1)
                    bs, ks, vs = b[r0:r0 + n], kk[r0:r0 + n], v[r0:r0 + n]
                    sidx = lax.broadcasted_iota(jnp.int32, (n, HG_DK), 0)
                    blk = jnp.zeros((8, HG_DK), F32)
                    for i in range(8):
                        t = r0 + 8 * h8 + i
                        qt = q[t:t + 1]
                        dot_ = do[t:t + 1]
                        e = jnp.where(sidx <= 8 * h8 + i, jnp.exp(b[t:t + 1] - bs), 0.0)
                        w = e * ks
                        p = jnp.sum(w * qt, axis=1, keepdims=True)
                        dsc = jnp.sum(vs * dot_, axis=1, keepdims=True)
                        dqt = jnp.sum(dsc * w, axis=0, keepdims=True)
                        blk = blk + jnp.where(row8 == i, dqt, 0.0)
                        dk_ref[r0:r0 + n, :] += dsc * e * qt
                        dv_ref[r0:r0 + n, :] += p * dot_
                    dq_ref[r0 + 8 * h8:r0 + n, :] = blk
            ds_far = jnp.where(earlier_group, _dot_bf16(do, v, NT), 0.0)
            ds_far_t = jnp.where(later_group, _dot_bf16(v, do, NT), 0.0)
            dq_far, dk_far = [jnp.zeros((GROUP, HG_DK), F32)], []
            for grp in range(1, N_GROUPS):
                r0 = GROUP * grp
                _, ks, decay = _by_query_group(q, kk, b, grp)
                dq_far.append(decay * _dot_hi(ds_far[r0:r0 + GROUP], ks, NN))
                qs, _, decay = _by_key_group(q, kk, b, grp - 1)
                dk_far.append(decay * _dot_hi(ds_far_t[r0 - GROUP:r0], qs, NN))
            dk_far.append(jnp.zeros((GROUP, HG_DK), F32))
            dv_far = _dot_bf16(_scores_between_groups(q, kk, b), do, TN)
            dq_i = dq_ref[...] + jnp.concatenate(dq_far, axis=0)
            dk_i = dk_ref[...] + jnp.concatenate(dk_far, axis=0)
            dke_ke = dke * ke
            db = q * dq_i - kk * dk_i + dqe * qe - dke_ke
            db_last = jnp.sum(dke_ke, axis=0, keepdims=True) + d_ebl * ebl
            dlg = _running_sum(tri_t, db) + db_last
            dq = dq_i + dqe * eb
            dkk = dk_i + dke * ekb
            dg = dlg / g - dkk
            dhq_ref[rows, :] = (dq * (HG_DK ** -0.5) * (sq * (1.0 + hq * (1.0 - sq)))).astype(BF16)
            dhf_ref[rows, :] = (dg * (1.0 - lb) * f * (1.0 - f)).astype(BF16)
            dhi_ref[rows, :] = (dv_ref[...] + dv_far + dv_inter).astype(BF16)
            dlb_ref[...] += jnp.sum(dg * (1.0 - f), axis=0, keepdims=True)

        def trip(i, carry):
            for k in range(BWD_CHUNKS_PER_TRIP):
                chunk(n_chunks - 1 - k - BWD_CHUNKS_PER_TRIP * i, *per_chunk[3 * k:3 * k + 3])
            return carry

        lax.fori_loop(0, n_chunks // BWD_CHUNKS_PER_TRIP, trip, 0)
        dl0 = dlb_ref[...] * lb * (1.0 - lb)
        dlbl_ref[0:1, :] = dl0
        dlbl_ref[1:2, :] = -dl0

    def col(base):
        return pl.BlockSpec((T, HG_DK), lambda h: (0, base + h))

    outb = jax.ShapeDtypeStruct((T, HG_WIDTH), BF16)
    return _call(
        body, (z, z, z, z, lb_logits, hg_norm_w, o_raw, s_all, dya), name=name, grid=(HG_HEADS,),
        in_specs=[col(COL_HQ), col(COL_HF), col(COL_HI), col(COL_HG),
                  pl.BlockSpec((2, HG_DK), lambda h: (0, h)), pl.BlockSpec((1, HG_DK), lambda h: (0, 0)),
                  col(0), pl.BlockSpec((None, n_chunks, HG_DK, HG_DK), lambda h: (h, 0, 0, 0)), col(0)],
        out_specs=[col(0), col(0), col(0), col(0), pl.BlockSpec((2, HG_DK), lambda h: (0, h)),
                   pl.BlockSpec((8, HG_DK), lambda h: (0, 0))],
        out_shape=[outb, outb, outb, outb, jax.ShapeDtypeStruct((2, HG_WIDTH), F32),
                   jax.ShapeDtypeStruct((8, HG_DK), F32)],
        scratch_shapes=[pltpu.VMEM((T, HG_DK), F32), pltpu.VMEM((HG_DK, HG_DK), F32), pltpu.VMEM((1, HG_DK), F32)]
        + [pltpu.VMEM((CHUNK, HG_DK), F32)] * (3 * BWD_CHUNKS_PER_TRIP),
        sem=("arbitrary",), after=after)


CONST_KEYS = PAD - REL_CLIP
VAR_KEYS = BAND - CONST_KEYS
REL_LO = 128
REL_SPAN = N_REL_PAD - REL_LO


def _rel_onehot(t):
    r = lax.broadcasted_iota(jnp.int32, (REL_SPAN, VAR_KEYS), 0)
    j = lax.broadcasted_iota(jnp.int32, (REL_SPAN, VAR_KEYS), 1)
    idx = jnp.clip(t + PAD - CONST_KEYS - j, -REL_CLIP, REL_CLIP) + REL_CLIP - REL_LO
    return jnp.where(r == idx, 1.0, 0.0).astype(BF16)


def _split3(x):
    hi = x.astype(BF16)
    r1 = x - hi.astype(F32)
    mid = r1.astype(BF16)
    return hi, mid, (r1 - mid.astype(F32)).astype(BF16)


def _bias_expand(rel, name):
    rows = 8

    def body(rel_ref, out_ref):
        tab = rel_ref[...]
        pieces = _split3(tab[:, REL_LO:N_REL_PAD])
        constant = jnp.broadcast_to(tab[:, 2 * REL_CLIP:2 * REL_CLIP + 1], (AT_HEADS, CONST_KEYS))
        for i in range(rows):
            onehot = _rel_onehot(pl.program_id(0) * rows + i)
            out_ref[i, :, 0:CONST_KEYS] = constant
            out_ref[i, :, CONST_KEYS:BAND] = sum(jnp.dot(piece, onehot, preferred_element_type=F32)
                                                 for piece in pieces)

    return pl.pallas_call(
        body, name=name, grid=(CHUNK // rows,),
        in_specs=[pl.BlockSpec((AT_HEADS, N_REL_PAD), lambda t: (0, 0))],
        out_specs=pl.BlockSpec((rows, AT_HEADS, BAND), lambda t: (t, 0, 0)),
        out_shape=jax.ShapeDtypeStruct((CHUNK, AT_HEADS, BAND), F32),
        compiler_params=_cparams(("parallel",)),
    )(rel)


def _bias_reduce(dbias_rows, name, after=()):
    def body(db_ref, out_ref):
        lane = lax.broadcasted_iota(jnp.int32, (AT_HEADS, N_REL_PAD), 1)
        varying = lane >= CONST_KEYS
        by_offset = jnp.zeros((AT_HEADS, N_REL_PAD), F32)
        constant = jnp.zeros((AT_HEADS, N_REL_PAD), F32)
        for t in range(CHUNK):
            row = db_ref[t]
            constant = constant + jnp.where(varying, 0.0, row)
            moved = jnp.where(varying, row, 0.0)
            by_offset = by_offset + (pltpu.roll(moved, N_REL_PAD - t, axis=1) if t else moved)
        offset = lax.broadcasted_iota(jnp.int32, (N_REL_PAD, N_REL_PAD), 0)
        entry = lax.broadcasted_iota(jnp.int32, (N_REL_PAD, N_REL_PAD), 1)
        onehot = jnp.where(entry == jnp.clip(PAD - offset, -REL_CLIP, REL_CLIP) + REL_CLIP, 1.0, 0.0).astype(BF16)
        acc = sum(jnp.dot(piece, onehot, preferred_element_type=F32) for piece in _split3(by_offset))
        last = jnp.sum(constant, axis=1, keepdims=True)
        out_ref[...] = acc + jnp.where(lane == 2 * REL_CLIP, last, 0.0)

    whole = pl.BlockSpec((CHUNK, AT_HEADS, N_REL_PAD), lambda i: (0, 0, 0))
    return _call(
        body, (dbias_rows,), name=name, grid=(1,), in_specs=[whole],
        out_specs=[pl.BlockSpec((AT_HEADS, N_REL_PAD), lambda i: (0, 0))],
        out_shape=[jax.ShapeDtypeStruct((AT_HEADS, N_REL_PAD), F32)],
        sem=("arbitrary",), after=after)[0]


def _pair_lanes():
    return lax.broadcasted_iota(jnp.int32, (CHUNK, 2 * AT_DH), 1) < AT_DH


def _block_diag(a):
    first = _pair_lanes()
    return jnp.concatenate([jnp.where(first, a, 0.0), jnp.where(first, 0.0, a)], axis=0).astype(BF16)


def _diag_blocks(a):
    return jnp.where(_pair_lanes(), a[:CHUNK], a[CHUNK:])


def _band_probs_t(kb, qbd, bias_t, c):
    s = lax.dot_general(kb, qbd, (NT, ((), ())), preferred_element_type=F32) * (AT_DH ** -0.5) + bias_t
    j = lax.broadcasted_iota(jnp.int32, (BAND, 2 * AT_DH), 0)
    s = jnp.where(j + c * CHUNK >= PAD, s, -jnp.inf)
    p = jnp.exp(s - jnp.max(s, axis=0, keepdims=True))
    return p / jnp.sum(p, axis=0, keepdims=True)


def _attn_fwd(z, bias_t, name, after=()):
    T = z.shape[0]
    n_chunks = T // CHUNK

    def body(q_ref, k_ref, v_ref, bias_ref, y_ref, p_ref, *scratch):
        for pr in range(2):
            lanes = slice(128 * pr, 128 * (pr + 1))
            for dst_ref, src_ref in zip(scratch[2 * pr:2 * pr + 2], (k_ref, v_ref)):
                dst_ref[0:PAD, :] = jnp.zeros((PAD, 128), BF16)
                dst_ref[PAD:PAD + T, :] = src_ref[:, lanes].astype(BF16)

        def chunk(c, carry):
            rows = pl.ds(pl.multiple_of(c * CHUNK, CHUNK), CHUNK)
            band = pl.ds(pl.multiple_of(c * CHUNK, CHUNK), BAND)
            for pr in range(2):
                kp_ref, vp_ref = scratch[2 * pr:2 * pr + 2]
                lanes = slice(128 * pr, 128 * (pr + 1))
                p = _band_probs_t(kp_ref[band, :], _block_diag(q_ref[rows, lanes]), bias_ref[pr], c).astype(BF16)
                p_ref[pr, c] = p
                o2 = lax.dot_general(p, vp_ref[band, :], (TN, ((), ())), preferred_element_type=F32)
                y_ref[rows, lanes] = _diag_blocks(o2).astype(BF16)
            return carry

        lax.fori_loop(0, n_chunks, chunk, 0, unroll=2)

    def col(base):
        return pl.BlockSpec((T, 256), lambda h: (0, base // 2 + h))

    return _call(
        body, (z, z, z, bias_t), name=name, grid=(AT_HEADS // 4,),
        in_specs=[col(COL_AQ), col(COL_AK), col(COL_AV), pl.BlockSpec((2, BAND, 128), lambda h: (h, 0, 0))],
        out_specs=[col(0), pl.BlockSpec((2, n_chunks, BAND, 128), lambda h: (h, 0, 0, 0))],
        out_shape=[jax.ShapeDtypeStruct((T, AT_WIDTH), BF16),
                   jax.ShapeDtypeStruct((AT_HEADS // 2, n_chunks, BAND, 128), BF16)],
        scratch_shapes=[pltpu.VMEM((PAD + T, 128), BF16)] * 4,
        sem=("parallel",), after=after)


def _attn_bwd(z, probs, dyb, name, after=()):
    T = z.shape[0]
    n_chunks = T // CHUNK

    def body(q_ref, k_ref, v_ref, p_ref, dy_ref, dq_ref, dk_ref, dv_ref, dbias_ref, *scratch):
        dbias_ref[...] = jnp.zeros_like(dbias_ref)
        for pr in range(2):
            kp_ref, vp_ref, dkp_ref, dvp_ref = scratch[4 * pr:4 * pr + 4]
            lanes = slice(128 * pr, 128 * (pr + 1))
            kp_ref[0:PAD, :] = jnp.zeros((PAD, 128), BF16)
            vp_ref[0:PAD, :] = jnp.zeros((PAD, 128), BF16)
            kp_ref[PAD:PAD + T, :] = k_ref[:, lanes].astype(BF16)
            vp_ref[PAD:PAD + T, :] = v_ref[:, lanes].astype(BF16)
            dkp_ref[...] = jnp.zeros_like(dkp_ref)
            dvp_ref[...] = jnp.zeros_like(dvp_ref)

        def chunk(c, carry):
            rows = pl.ds(pl.multiple_of(c * CHUNK, CHUNK), CHUNK)
            band = pl.ds(pl.multiple_of(c * CHUNK, CHUNK), BAND)
            for pr in range(2):
                kp_ref, vp_ref, dkp_ref, dvp_ref = scratch[4 * pr:4 * pr + 4]
                lanes = slice(128 * pr, 128 * (pr + 1))
                qbd = _block_diag(q_ref[rows, lanes])
                dobd = _block_diag(dy_ref[rows, lanes])
                pb = p_ref[pr, c]
                p = pb.astype(F32)
                dp = lax.dot_general(vp_ref[band, :], dobd, (NT, ((), ())), preferred_element_type=F32)
                ds = p * (dp - jnp.sum(dp * p, axis=0, keepdims=True))
                dbias_ref[pr] += ds
                dsb = ds.astype(BF16)
                dq2 = lax.dot_general(dsb, kp_ref[band, :], (TN, ((), ())), preferred_element_type=F32)
                dq_ref[rows, lanes] = (_diag_blocks(dq2) * (AT_DH ** -0.5)).astype(BF16)
                dkp_ref[band, :] += jnp.dot(dsb, qbd, preferred_element_type=F32) * (AT_DH ** -0.5)
                dvp_ref[band, :] += jnp.dot(pb, dobd, preferred_element_type=F32)
            return carry

        lax.fori_loop(0, n_chunks, chunk, 0)
        for pr in range(2):
            lanes = slice(128 * pr, 128 * (pr + 1))
            dk_ref[:, lanes] = scratch[4 * pr + 2][PAD:PAD + T, :].astype(BF16)
            dv_ref[:, lanes] = scratch[4 * pr + 3][PAD:PAD + T, :].astype(BF16)

    def col(base):
        return pl.BlockSpec((T, 256), lambda h: (0, base // 2 + h))

    outb = jax.ShapeDtypeStruct((T, AT_WIDTH), BF16)
    return _call(
        body, (z, z, z, probs, dyb), name=name, grid=(AT_HEADS // 4,),
        in_specs=[col(COL_AQ), col(COL_AK), col(COL_AV),
                  pl.BlockSpec((2, n_chunks, BAND, 128), lambda h: (h, 0, 0, 0)), col(0)],
        out_specs=[col(0), col(0), col(0), pl.BlockSpec((2, BAND, 128), lambda h: (h, 0, 0))],
        out_shape=[outb, outb, outb, jax.ShapeDtypeStruct((AT_HEADS // 2, BAND, 128), F32)],
        scratch_shapes=([pltpu.VMEM((PAD + T, 128), BF16)] * 2 + [pltpu.VMEM((PAD + T, 128), F32)] * 2) * 2,
        sem=("parallel",), after=after)


def _local_step(x, target, lb_logits, hg_norm_w, rel_bias, norm_mix_w, norm_mlp_w, norm_final_w,
                w_in, rest, exchanges=None):
    ex = exchanges
    rel = jnp.pad(rel_bias, ((0, 0), (0, N_REL_PAD - N_REL)))

    u = _rms_fwd(x, norm_mix_w, "rms_mix_fwd")
    if ex:
        z, w_in = _mm_gathered(u, w_in, ex.order, "mm_in_fwd")
        gather = _Gather(rest[:3], [w_in], "ag")
        mlp_shards, _ = lax.optimization_barrier((rest[3:], gather.token))
        gather_mlp = _Gather([s.astype(BF16) for s in mlp_shards], [gather.token], "ag_mlp")
        z = _mm_gathered_tail(u, w_in, z, ex.order, "mm_in_fwd_tail", after=[gather_mlp.token])
        tok = []
    else:
        z = _mm_nn(u, w_in, F32, "mm_in_fwd")
        w_a, w_b, w_out, w_up, w_down = rest
        tok = []
    o_raw, y_a, s_all = _hgrn2_fwd(z, lb_logits, hg_norm_w, "hgrn2_fwd", after=tok)
    if ex:
        tok = [gather.pass_on([0, 1, 2], [o_raw], "abo")]
    bias_rows = _bias_expand(rel, "bias_expand")
    bias_t = jnp.transpose(bias_rows.reshape(CHUNK, AT_HEADS // 2, 2, BAND), (1, 3, 2, 0)).reshape(
        AT_HEADS // 2, BAND, 2 * CHUNK)
    y_b, probs = _attn_fwd(z, bias_t, "attn_fwd", after=tok)
    if ex:
        tok = [gather_mlp.pass_on([0], [y_b], "up")]
        w_a, w_b, w_out = gather.finish([0, 1, 2], tok, "abo")
    pa = _mm_nn(y_a, w_a, F32, "mm_a_fwd")
    pb, merged = _mm_nn(y_b, w_b, None, "mm_b_fwd", epilogue=(
        (z, z, pa), (COL_GATE_A * GATE_TILE, COL_GATE_B * GATE_TILE, 0), (F32, BF16), _gated_merge))
    w_out1 = w_out.reshape(1, D_MODEL, D_MODEL)
    h1, u2 = _mm_rows(merged, w_out.reshape(D_MODEL, D_MODEL), [x], [norm_mlp_w], (F32, BF16),
                      _residual_rms_rows, "mm_out_fwd")
    if ex:
        w_up, = gather_mlp.finish([0], [u2], "up")
    act = ((), (), (F32, BF16), _squared_relu)
    a, r = _mm_nn(u2, w_up, None, "mm_up_fwd_first", epilogue=act, blocks=(0, N_DEV // 2))
    tok = [gather_mlp.pass_on([1], [r], "down")] if ex else []
    a, r = _mm_nn(u2, w_up, None, "mm_up_fwd_second", after=tok, epilogue=act, blocks=(N_DEV // 2, N_DEV // 2),
                  into=(a, r))
    if ex:
        w_down, = gather_mlp.finish([1], [r], "down")
    w_down1 = w_down.reshape(1, D_FF, D_MODEL)
    mlp = _mm_nn(r, w_down1, F32, "mm_down_fwd")
    loss, dh2, dh2b, g_nf = _loss_head(h1, mlp, norm_final_w, target, "loss_head")

    own = ex.parity if ex else jnp.zeros((1,), jnp.int32)

    def sibling_half(weights, name, after=()):
        others = [_mm_tn_half(a_, g_, 1 - own, on, None, nm + "_sibling", after, *cols)
                  for a_, g_, on, nm, *cols in weights]
        rs = _ReduceScatter(others, name) if ex else None
        return rs, others, ([rs.token] if ex else [])

    def own_half(rs, weights, others, after):
        landed = rs.from_sibling(after) if ex else [None] * len(weights)
        sums = [_mm_tn_half(a_, g_, own, on, l, nm + "_own", (), *cols)
                for (a_, g_, on, nm, *cols), l in zip(weights, landed)]
        if ex:
            return [rs.scatter(sums)], None
        return [], [jnp.stack([s_, o_], axis=1).reshape((N_DEV,) + s_.shape[1:]) for s_, o_ in zip(sums, others)]

    down = [(r, dh2b, "a", "mm_down_wgrad")]
    rs_down, others, tok = sibling_half(down, "rs_down")
    da, = _mm_nt(dh2b, w_down1, None, "mm_down_dgrad", after=tok, epilogue=(
        (a,), (0,), (BF16,), lambda dr, av: (dr * (2.0 * jnp.maximum(av, 0.0)),)))
    tok, g_down = own_half(rs_down, down, others, [da])
    up = [(u2, da, "g", "mm_up_wgrad")]
    rs_up, others, tok = sibling_half(up, "rs_up", tok)
    du2 = _mm_nt(da, w_up, F32, "mm_up_dgrad", after=tok)
    tok, g_up = own_half(rs_up, up, others, [du2])
    dh1, dh1b, g_nmlp = _rms_bwd(du2, h1, norm_mlp_w, dh2, (F32, BF16), "rms_mlp_bwd", after=tok)

    dpa, dpb, dga, dgb = _mm_nt(dh1b, w_out1, None, "mm_out_dgrad", epilogue=(
        (z, z, pa, pb), (COL_GATE_A * GATE_TILE, COL_GATE_B * GATE_TILE, 0, 0), (BF16,) * 4, _merge_grads))
    mix = [(y_a, dpa, "g", "mm_a_wgrad"), (y_b, dpb, "g", "mm_b_wgrad"), (merged, dh1b, "a", "mm_out_wgrad")]
    rs_mix, others, tok = sibling_half(mix, "rs_mix")
    dya = _mm_nt(dpa, w_a, F32, "mm_a_dgrad", after=tok)
    dyb = _mm_nt(dpb, w_b, F32, "mm_b_dgrad", after=tok)
    tok, g_mix = own_half(rs_mix, mix, others, [dya, dyb])
    daq, dak, dav, dbias_t = _attn_bwd(z, probs, dyb, "attn_bwd", after=tok)
    dhq, dhf, dhi, dhg, g_lbl, g_hgw = _hgrn2_bwd(z, lb_logits, hg_norm_w, o_raw, s_all, dya, "hgrn2_bwd",
                                                  after=tok)
    dbias_rows = jnp.pad(jnp.transpose(dbias_t.reshape(AT_HEADS // 2, BAND, 2, CHUNK), (3, 0, 2, 1)).reshape(
        CHUNK, AT_HEADS, BAND), ((0, 0), (0, 0), (0, N_REL_PAD - BAND)))
    dz = jnp.concatenate([dhq, dhf, dhi, dhg, daq, dak, dav, dga, dgb], axis=1)
    half = D_MODEL // 2
    lo = [(u, dz, "g", "mm_in_wgrad_lo", (0, half))]
    hi = [(u, dz, "g", "mm_in_wgrad_hi", (half, half))]
    rs_in_lo, others_lo, tok = sibling_half(lo, "rs_in_lo")
    rs_in_hi, others_hi, tok = sibling_half(hi, "rs_in_hi", tok)
    tok, g_in_lo = own_half(rs_in_lo, lo, others_lo, tok)
    du = _mm_nt(dz, w_in, F32, "mm_in_dgrad", after=tok)
    tok, g_in_hi = own_half(rs_in_hi, hi, others_hi, [du])
    grad_x, g_nmix = _rms_bwd(du, x, norm_mix_w, dh1, (F32,), "rms_mix_bwd", after=tok)
    g_rel = _bias_reduce(dbias_rows, "bias_reduce", after=tok)[:, :N_REL]

    small = dict(lb_logits=g_lbl, hg_norm_w=g_hgw[0:1], rel_bias=g_rel, norm_mix_w=g_nmix, norm_mlp_w=g_nmlp,
                 norm_final_w=g_nf)
    if ex:
        grads = [(rs_in_lo, rs_in_hi), rs_mix, rs_up, rs_down]
    else:
        grads = [jnp.concatenate([g_in_lo[0], g_in_hi[0]], axis=1)] + g_mix + [g_up[0], g_down[0]]
    return loss, grad_x, grads, small


def _mm_gathered(u, shard, order, name):
    T, K = u.shape
    _, Nb = shard.shape

    def body(order_ref, u_ref, shard_ref, z_ref, full_ref, wbuf, load_sem, send_sems, recv_sems, local_sem):
        s = pl.program_id(0)
        x, y, c = _position()
        me, sibling = (x, y, c), (x, y, 1 - c)
        chips = [(1 - x, y), (x, 1 - y), (1 - x, 1 - y)]

        def copy(k, block, to, src=None):
            dst = full_ref.at[4 * block[0] + 2 * block[1] + block[2]]
            return pltpu.make_async_remote_copy(
                src_ref=dst if src is None else src, dst_ref=dst,
                send_sem=send_sems.at[k], recv_sem=recv_sems.at[k], device_id=to, device_id_type=MESH)

        @pl.when(s == 0)
        def _():
            local = pltpu.make_async_copy(shard_ref, full_ref.at[4 * x + 2 * y + c], local_sem)
            local.start()
            copy(0, me, sibling, src=shard_ref).start()
            for j, chip in enumerate(chips):
                copy(1 + j, me, (*chip, c), src=shard_ref).start()
            local.wait()

        @pl.when(s == 1)
        def _():
            copy(0, sibling, me).wait_recv()

        for j, chip in enumerate(chips):
            direct, passed = ((2, 4), (3, 5), (6, 7))[j]

            @pl.when(s == direct)
            def _(j=j, chip=chip):
                copy(1 + j, (*chip, c), me).wait_recv()
                copy(4 + j, (*chip, c), sibling).start()

            @pl.when(s == passed)
            def _(j=j, chip=chip):
                copy(4 + j, (*chip, 1 - c), me).wait_recv()

        @pl.when(s < N_EARLY_BLOCKS)
        def _():
            load = pltpu.make_async_copy(full_ref.at[order_ref[s]], wbuf, load_sem)
            load.start()
            load.wait()
            z_ref[...] = jnp.dot(u_ref[...], wbuf[...], preferred_element_type=F32)

        @pl.when(s == N_DEV - 1)
        def _():
            for k in range(7):
                copy(k, me, sibling).wait_send()

    z, full = pl.pallas_call(
        body, name=name,
        grid_spec=pltpu.PrefetchScalarGridSpec(
            num_scalar_prefetch=1, grid=(N_DEV,),
            in_specs=[pl.BlockSpec((T, K), lambda s, order: (0, 0)), ANY],
            out_specs=[pl.BlockSpec((T, Nb), lambda s, order: (0, order[jnp.minimum(s, N_EARLY_BLOCKS - 1)])), ANY],
            scratch_shapes=[pltpu.VMEM((K, Nb), BF16), pltpu.SemaphoreType.DMA,
                            pltpu.SemaphoreType.DMA((7,)), pltpu.SemaphoreType.DMA((7,)), pltpu.SemaphoreType.DMA]),
        out_shape=[jax.ShapeDtypeStruct((T, N_DEV * Nb), F32), jax.ShapeDtypeStruct((N_DEV, K, Nb), BF16)],
        compiler_params=_cparams(("arbitrary",)),
    )(order, u, shard)
    return z, full


N_EARLY_BLOCKS = 6


def _mm_gathered_tail(u, full, z, order, name, after=()):
    T, K = u.shape
    _, _, Nb = full.shape
    n_after = len(after)

    def body(order_ref, u_ref, w_ref, z_in_ref, *rest):
        rest[n_after][...] = jnp.dot(u_ref[...], w_ref[...], preferred_element_type=F32)

    return pl.pallas_call(
        body, name=name,
        grid_spec=pltpu.PrefetchScalarGridSpec(
            num_scalar_prefetch=1, grid=(N_DEV - N_EARLY_BLOCKS,),
            in_specs=[pl.BlockSpec((T, K), lambda s, order: (0, 0)),
                      pl.BlockSpec((None, K, Nb), lambda s, order: (order[N_EARLY_BLOCKS + s], 0, 0)), ANY]
            + [ANY] * n_after,
            out_specs=pl.BlockSpec((T, Nb), lambda s, order: (0, order[N_EARLY_BLOCKS + s]))),
        out_shape=jax.ShapeDtypeStruct(z.shape, z.dtype),
        input_output_aliases={3: 0},
        compiler_params=_cparams(("arbitrary",)),
    )(order, u, full, z, *after)


def _gather_order():
    x, y, c = _position()
    chips = [(1 - x, y), (x, 1 - y), (1 - x, 1 - y)]
    ids = [4 * x + 2 * y + c, 4 * x + 2 * y + (1 - c)]
    ids += [4 * cx + 2 * cy + c for cx, cy in chips[:2]] + [4 * cx + 2 * cy + (1 - c) for cx, cy in chips[:2]]
    ids += [4 * chips[2][0] + 2 * chips[2][1] + c, 4 * chips[2][0] + 2 * chips[2][1] + (1 - c)]
    return jnp.stack(ids).astype(jnp.int32)


HBM = pl.BlockSpec(memory_space=pltpu.HBM)
SEM = pl.BlockSpec(memory_space=pltpu.SEMAPHORE)
DATAFLOW = pltpu.SideEffectType.DATAFLOW_SIDE_EFFECTING


def _split_call(name, bufs, waits=(), starts=None, after=()):
    nb = len(bufs)
    n_new = starts[1] if starts else 0
    wait_sems = [s for w in waits for s in (*w[1], *w[2])]

    def body(*refs):
        b, pos = refs[:nb], nb
        for plan, ss, _, send_idx, recv_idx in waits:
            k = len(ss)
            copies = plan(b, refs[pos:pos + k], refs[pos + k:pos + 2 * k])
            pos += 2 * k
            for i in recv_idx:
                copies[i].wait_recv()
            for i in send_idx:
                copies[i].wait_send()
        outs = refs[pos + len(after):]
        if starts:
            for cp in starts[0](b, outs[nb:nb + n_new], outs[nb + n_new:nb + 2 * n_new]):
                cp.start()
        outs[-1][...] = jnp.zeros_like(outs[-1])

    res = pl.pallas_call(
        body, name=name,
        out_shape=tuple(pltpu.HBM(a.shape, a.dtype) for a in bufs) + (pltpu.SemaphoreType.DMA(()),) * (2 * n_new)
        + (jax.ShapeDtypeStruct((8, 128), F32),),
        in_specs=[HBM] * nb + [SEM] * len(wait_sems) + [ANY] * len(after),
        out_specs=(HBM,) * nb + (SEM,) * (2 * n_new) + (pl.BlockSpec(memory_space=pltpu.VMEM),),
        input_output_aliases={i: i for i in range(nb)},
        compiler_params=pltpu.CompilerParams(has_side_effects=DATAFLOW),
    )(*bufs, *wait_sems, *after)
    return list(res[:nb]), list(res[nb:nb + n_new]), list(res[nb + n_new:nb + 2 * n_new]), res[-1]


def _in_hbm(a):
    return pltpu.with_memory_space_constraint(a, pltpu.HBM)


def _remote(src, dst, send_sem, recv_sem, to):
    return pltpu.make_async_remote_copy(src_ref=src, dst_ref=dst, send_sem=send_sem, recv_sem=recv_sem,
                                        device_id=to, device_id_type=MESH)


def _other_chips():
    x, y, _ = _position()
    return [(1 - x, y), (x, 1 - y), (1 - x, 1 - y)]


def _plan_gather_first(n):
    def plan(b, ss, rs):
        x, y, c = _position()
        to = [(x, y, 1 - c)] + [(*chip, c) for chip in _other_chips()]
        return [_remote(b[w], b[n + w].at[4 * x + 2 * y + c], ss[4 * w + k], rs[4 * w + k], to[k])
                for w in range(n) for k in range(4)]
    return plan, 4 * n


def _plan_gather_pass(n):
    def plan(b, ss, rs):
        x, y, c = _position()
        copies = []
        for w in range(n):
            for j, chip in enumerate(_other_chips()):
                blk = b[n + w].at[4 * chip[0] + 2 * chip[1] + c]
                copies.append(_remote(blk, blk, ss[3 * w + j], rs[3 * w + j], (x, y, 1 - c)))
        return copies
    return plan, 3 * n


def _plan_sibling(n):
    def plan(b, ss, rs):
        x, y, c = _position()
        return [_remote(b[w].at[s], b[n + w].at[s], ss[4 * w + s], rs[4 * w + s], (x, y, 1 - c))
                for w in range(n) for s in range(N_CHIP)]
    return plan, 4 * n


def _plan_scatter(n):
    def plan(b, ss, rs):
        x, y, c = _position()
        return [_remote(b[w].at[2 * chip[0] + chip[1]], b[n + w].at[2 * x + y], ss[3 * w + j], rs[3 * w + j],
                        (*chip, c))
                for w in range(n) for j, chip in enumerate(_other_chips())]
    return plan, 3 * n


class _Gather:
    def __init__(self, shards, after, name):
        self.n, self.name = len(shards), name
        x, y, c = _position()
        placed = [lax.dynamic_update_index_in_dim(lax.empty((N_DEV,) + s.shape, s.dtype), s, 4 * x + 2 * y + c, 0)
                  for s in shards]
        bufs, self.ss, self.rs, self.token = _split_call(
            name + "_start", [_in_hbm(a) for a in list(shards) + placed], starts=_plan_gather_first(self.n),
            after=after)
        self.shards, self.fulls = bufs[:self.n], bufs[self.n:]
        self.passed = {}

    def _sub(self, ids, sems, per):
        return [sems[per * w + k] for w in ids for k in range(per)]

    def pass_on(self, ids, after, tag):
        m = len(ids)
        first = (_plan_gather_first(m)[0], self._sub(ids, self.ss, 4), self._sub(ids, self.rs, 4),
                 [], [4 * i + k for i in range(m) for k in (1, 2, 3)])
        bufs, ss, rs, token = _split_call(
            "%s_pass_%s" % (self.name, tag), [self.shards[w] for w in ids] + [self.fulls[w] for w in ids],
            waits=[first], starts=_plan_gather_pass(m), after=after)
        for i, w in enumerate(ids):
            self.shards[w], self.fulls[w] = bufs[i], bufs[m + i]
        self.passed[tuple(ids)] = (ss, rs)
        return token

    def finish(self, ids, after, tag):
        m = len(ids)
        ss2, rs2 = self.passed[tuple(ids)]
        first = (_plan_gather_first(m)[0], self._sub(ids, self.ss, 4), self._sub(ids, self.rs, 4),
                 list(range(4 * m)), [4 * i for i in range(m)])
        passed = (_plan_gather_pass(m)[0], ss2, rs2, list(range(3 * m)), list(range(3 * m)))
        bufs, _, _, _ = _split_call(
            "%s_finish_%s" % (self.name, tag), [self.shards[w] for w in ids] + [self.fulls[w] for w in ids],
            waits=[first, passed], after=after)
        return bufs[m:]


class _ReduceScatter:
    def __init__(self, others, name):
        self.n, self.name = len(others), name
        lands = [lax.empty(g.shape, g.dtype) for g in others]
        self.bufs, self.ss, self.rs, self.token = _split_call(
            name + "_sibling_start", [_in_hbm(a) for a in list(others) + lands], starts=_plan_sibling(self.n))

    def from_sibling(self, after):
        n = self.n
        bufs, _, _, _ = _split_call(
            self.name + "_sibling_wait", self.bufs,
            waits=[(_plan_sibling(n)[0], self.ss, self.rs, list(range(4 * n)), list(range(4 * n)))], after=after)
        return bufs[n:]

    def scatter(self, sums):
        lands = [lax.empty(s.shape, s.dtype) for s in sums]
        self.bufs, self.ss, self.rs, token = _split_call(
            self.name + "_scatter_start", [_in_hbm(a) for a in list(sums) + lands], starts=_plan_scatter(self.n))
        return token

    def finish(self, after):
        n = self.n
        bufs, _, _, _ = _split_call(
            self.name + "_scatter_wait", self.bufs,
            waits=[(_plan_scatter(n)[0], self.ss, self.rs, list(range(3 * n)), list(range(3 * n)))], after=after)
        return bufs[:n], bufs[n:]


class _Exchanges:
    def __init__(self, parity, order):
        self.parity, self.order = parity, order


def _plan_everyone():
    def plan(b, ss, rs):
        x, y, c = _position()
        return [_remote(b[0], b[1].at[4 * x + 2 * y + c], ss[k - 1], rs[k - 1],
                        (x ^ ((k >> 2) & 1), y ^ ((k >> 1) & 1), c ^ (k & 1))) for k in range(1, N_DEV)]
    return plan, N_DEV - 1


class _GatherSmall:
    def __init__(self, packed, after, name):
        self.name = name
        x, y, c = _position()
        placed = lax.dynamic_update_index_in_dim(lax.empty((N_DEV,) + packed.shape, packed.dtype), packed,
                                                 4 * x + 2 * y + c, 0)
        self.bufs, self.ss, self.rs, self.token = _split_call(
            name + "_start", [_in_hbm(packed), _in_hbm(placed)], starts=_plan_everyone(), after=after)

    def finish(self, after):
        everyone = list(range(N_DEV - 1))
        bufs, _, _, _ = _split_call(
            self.name + "_wait", self.bufs, waits=[(_plan_everyone()[0], self.ss, self.rs, everyone, everyone)],
            after=after)
        return bufs[1]


def _adamw_math(w, g, m, v):
    m = ADAM_B1 * m + (1.0 - ADAM_B1) * g
    v = ADAM_B2 * v + (1.0 - ADAM_B2) * (g * g)
    m_hat = m / (1.0 - ADAM_B1 ** ADAM_STEP)
    v_hat = v / (1.0 - ADAM_B2 ** ADAM_STEP)
    delta = -ADAM_LR * (m_hat / (jnp.sqrt(v_hat) + ADAM_EPS) + ADAM_WD * w)
    return delta, m, v


def _adamw_big_landed(w, m, v, parts, lands, slot, name, row0=0, into=None):
    R, C = w.shape
    rows = parts.shape[1]
    tr = _pick(rows, (128,))
    first = row0 // tr
    n_into = len(into) if into else 0

    def body(slot_ref, w_ref, m_ref, v_ref, own_ref, l1_ref, l2_ref, l3_ref, *rest):
        g = own_ref[...].astype(F32)
        for ref in (l1_ref, l2_ref, l3_ref):
            g = g + ref[...].astype(F32)
        for o_ref, res in zip(rest[n_into:], (g,) + _adamw_math(w_ref[...], g, m_ref[...], v_ref[...])):
            o_ref[...] = res

    blk = pl.BlockSpec((tr, C), lambda i, slot: (first + i, 0))

    def chip(k):
        return pl.BlockSpec((None, tr, C), lambda i, slot: ((slot[0] + k) % N_CHIP, i, 0))

    out = jax.ShapeDtypeStruct((R, C), F32)
    return pl.pallas_call(
        body, name=name,
        grid_spec=pltpu.PrefetchScalarGridSpec(
            num_scalar_prefetch=1, grid=(rows // tr,),
            in_specs=[blk, blk, blk, chip(0), chip(1), chip(2), chip(3)] + [ANY] * n_into,
            out_specs=[blk, blk, blk, blk]),
        out_shape=[out, out, out, out],
        input_output_aliases={8 + j: j for j in range(n_into)},
        compiler_params=_cparams(("parallel",)),
    )(slot, w, m, v, parts, lands, lands, lands, *(into or ()))


def _adamw_small(w, m, v, gathered, name):
    R = w.shape[0]

    def body(w_ref, m_ref, v_ref, p_ref, g_ref, d_ref, nm_ref, nv_ref):
        g = p_ref[0]
        for s in range(1, N_DEV):
            g = g + p_ref[s]
        d, nm, nv = _adamw_math(w_ref[...], g, m_ref[...], v_ref[...])
        g_ref[...] = g
        d_ref[...] = d
        nm_ref[...] = nm
        nv_ref[...] = nv

    out = jax.ShapeDtypeStruct((R, 128), F32)
    return pl.pallas_call(
        body, name=name, out_shape=[out, out, out, out],
    )(w, m, v, gathered)


SMALL_NAMES = ("lb_logits", "hg_norm_w", "rel_bias", "norm_mix_w", "norm_mlp_w", "norm_final_w")
SMALL_SHAPES = {"lb_logits": (2, HG_WIDTH), "hg_norm_w": (1, HG_DK), "rel_bias": (AT_HEADS, N_REL_PAD),
                "norm_mix_w": (1, D_MODEL), "norm_mlp_w": (1, D_MODEL), "norm_final_w": (1, D_MODEL)}


LOSS_ROW = sum(r * c for r, c in SMALL_SHAPES.values()) // 128


def _pack_small(parts, loss_row=None):
    rows = []
    for nme in SMALL_NAMES:
        p = parts[nme]
        if nme == "rel_bias":
            p = jnp.pad(p, ((0, 0), (0, N_REL_PAD - N_REL)))
        rows.append(p.reshape(-1, 128))
    if loss_row is not None:
        rows.append(loss_row)
    flat = jnp.concatenate(rows, axis=0)
    return jnp.pad(flat, ((0, SMALL_ROWS - flat.shape[0]), (0, 0)))


def _unpack_small(packed):
    out, at = {}, 0
    for nme in SMALL_NAMES:
        shp = SMALL_SHAPES[nme]
        nrow = shp[0] * shp[1] // 128
        p = packed[at:at + nrow].reshape(shp)
        at += nrow
        out[nme] = p[:, :N_REL] if nme == "rel_bias" else p
    return out


BIG_NAMES = ("w_in", "w_branch_a", "w_branch_b", "w_out", "w_up", "w_down")


def kernel(x, w_in, lb_logits, hg_norm_w, rel_bias, w_branch_a, w_branch_b, w_out, norm_mix_w, norm_mlp_w, w_up, w_down, norm_final_w, loss_target, m_w_in, m_lb_logits, m_hg_norm_w, m_rel_bias, m_w_branch_a, m_w_branch_b, m_w_out, m_norm_mix_w, m_norm_mlp_w, m_w_up, m_w_down, m_norm_final_w, v_w_in, v_lb_logits, v_hg_norm_w, v_rel_bias, v_w_branch_a, v_w_branch_b, v_w_out, v_norm_mix_w, v_norm_mlp_w, v_w_up, v_w_down, v_norm_final_w):
    big_w = [w_in[0], w_branch_a[0], w_branch_b[0], w_out[0], w_up[0], w_down[0]]
    big_m = [m_w_in[0], m_w_branch_a[0], m_w_branch_b[0], m_w_out[0], m_w_up[0], m_w_down[0]]
    big_v = [v_w_in[0], v_w_branch_a[0], v_w_branch_b[0], v_w_out[0], v_w_up[0], v_w_down[0]]

    shards = [w.astype(BF16) for w in big_w[:4]] + big_w[4:]
    parity = lax.axis_index("c").astype(jnp.int32).reshape(1)
    loss_part, grad_x, chip_parts, small = _local_step(
        x[0], loss_target[0], lb_logits, hg_norm_w, rel_bias[0], norm_mix_w, norm_mlp_w,
        norm_final_w.reshape(1, D_MODEL), shards[0], shards[1:], _Exchanges(parity, _gather_order()))
    (rs_in_lo, rs_in_hi), rs_mix, rs_up, rs_down = chip_parts
    slot =(2 * lax.axis_index("x") + lax.axis_index("y")).astype(jnp.int32).reshape(1)
    big = {}

    def finish(rs, names, after):
        sums, lands = rs.finish(after)
        for nme, own, land in zip(names, sums, lands):
            i = BIG_NAMES.index(nme)
            big[nme] = _adamw_big_landed(big_w[i], big_m[i], big_v[i], own, land, slot, "adamw_" + nme)
        return [big[nme][1] for nme in names]

    gather_small = _GatherSmall(_pack_small(small, loss_part[0:1]), [grad_x], "gather_small")
    done = finish(rs_down, ["w_down"], [grad_x, gather_small.token])
    done = finish(rs_up, ["w_up"], done)
    done = finish(rs_mix, ["w_branch_a", "w_branch_b", "w_out"], done)

    sw = dict(lb_logits=lb_logits, hg_norm_w=hg_norm_w, rel_bias=rel_bias[0], norm_mix_w=norm_mix_w,
              norm_mlp_w=norm_mlp_w, norm_final_w=norm_final_w.reshape(1, D_MODEL))
    sm = dict(lb_logits=m_lb_logits, hg_norm_w=m_hg_norm_w, rel_bias=m_rel_bias[0], norm_mix_w=m_norm_mix_w,
              norm_mlp_w=m_norm_mlp_w, norm_final_w=m_norm_final_w.reshape(1, D_MODEL))
    sv = dict(lb_logits=v_lb_logits, hg_norm_w=v_hg_norm_w, rel_bias=v_rel_bias[0], norm_mix_w=v_norm_mix_w,
              norm_mlp_w=v_norm_mlp_w, norm_final_w=v_norm_final_w.reshape(1, D_MODEL))
    gathered = gather_small.finish(done)
    small_packed = _adamw_small(_pack_small(sw), _pack_small(sm), _pack_small(sv), gathered, "adamw_small")
    loss = small_packed[0][LOSS_ROW, 0]
    small_out = [_unpack_small(p) for p in small_packed]

    (own,), (land,) = rs_in_lo.finish(done + [small_packed[0]])
    lo = _adamw_big_landed(big_w[0], big_m[0], big_v[0], own, land, slot, "adamw_w_in_lo")
    (own,), (land,) = rs_in_hi.finish([lo[1]])
    big["w_in"] = _adamw_big_landed(big_w[0], big_m[0], big_v[0], own, land, slot, "adamw_w_in_hi",
                                    row0=D_MODEL // 2, into=lo)

    def leaf(kind, nme):
        if nme in BIG_NAMES:
            return big[nme][kind][None]
        p = small_out[kind][nme]
        if nme == "rel_bias":
            return p[None]
        if nme == "norm_final_w":
            return p.reshape(D_MODEL)
        return p

    order = ("w_in", "lb_logits", "hg_norm_w", "rel_bias", "w_branch_a", "w_branch_b", "w_out", "norm_mix_w",
             "norm_mlp_w", "w_up", "w_down", "norm_final_w")
    outs = [loss, grad_x[None]]
    for kind in range(4):
        outs += [leaf(kind, nme) for nme in order]
    return tuple(outs)
```

```python
import jax
import jax.numpy as jnp
from jax import lax
from jax.experimental import pallas as pl
from jax.experimental.pallas import tpu as pltpu

F32 = jnp.float32
BF16 = jnp.bfloat16
HIGHEST = lax.Precision.HIGHEST
MESH = pl.DeviceIdType.MESH

D_MODEL = 2048
HG_HEADS = 8
HG_DK = 128
HG_WIDTH = 1024
AT_HEADS = 16
AT_DH = 64
AT_WIDTH = 1024
CHUNK = 64
LEFT_CHUNKS = 8
BAND = (LEFT_CHUNKS + 1) * CHUNK
PAD = LEFT_CHUNKS * CHUNK
REL_CLIP = 256
N_REL = 2 * REL_CLIP + 1
N_REL_PAD = 640
D_FF = 4 * D_MODEL
EPS = 1e-6
N_DEV = 8
N_CHIP = 4

ADAM_LR = 0.001
ADAM_B1 = 0.9
ADAM_B2 = 0.999
ADAM_EPS = 1e-08
ADAM_WD = 0.01
ADAM_STEP = 10

COL_HQ, COL_HF, COL_HI, COL_HG = 0, 8, 16, 24
COL_AQ, COL_AK, COL_AV = 32, 40, 48
COL_GATE_A, COL_GATE_B = 7, 9

VMEM_LIMIT = 56 * 1024 * 1024
SMALL_ROWS = 152


def _cparams(sem=None, **kw):
    if sem is not None:
        kw["dimension_semantics"] = sem
    return pltpu.CompilerParams(vmem_limit_bytes=VMEM_LIMIT, **kw)


def _pick(n, cands):
    for c in cands:
        if n % c == 0:
            return c
    return n


def _sigmoid(x):
    return 1.0 / (1.0 + jnp.exp(-x))


ANY = pl.BlockSpec(memory_space=pl.ANY)


def _position():
    return lax.axis_index("x"), lax.axis_index("y"), lax.axis_index("c")


def _call(body, args, *, name, grid, in_specs, out_specs, out_shape, scratch_shapes=(), sem=None, after=(),
          aliases=None):
    n_in = len(args)

    def ordered(*refs):
        body(*refs[:n_in], *refs[n_in + len(after):])

    return list(pl.pallas_call(
        ordered if after else body, name=name, grid=grid, in_specs=list(in_specs) + [ANY] * len(after),
        out_specs=out_specs, out_shape=out_shape, scratch_shapes=list(scratch_shapes),
        input_output_aliases=aliases or {}, compiler_params=_cparams(sem))(*args, *after))


MAX_CONTRACTION_TILE = 4096


def _accumulate(part, acc_ref, step, n_steps, finish):
    if n_steps == 1:
        finish(part)
        return

    @pl.when(step == 0)
    def _():
        acc_ref[...] = part

    @pl.when(step > 0)
    def _():
        acc_ref[...] += part

    @pl.when(step == n_steps - 1)
    def _():
        finish(acc_ref[...])


def _mm_nn(a, wb, out_dtype, name, after=(), epilogue=None, blocks=None, into=()):
    M, K = a.shape
    NB, K2, Nb = wb.shape
    assert K == K2
    j0, nj = blocks or (0, NB)
    n_into = len(into)
    tm = min(M, 1024)
    tk = min(K, MAX_CONTRACTION_TILE)
    tn = _pick(Nb, (512, 1408, 256))
    nk = K // tk
    nn = Nb // tn
    extra, first_cols, out_dtypes, fn = epilogue or ((), (), (out_dtype,), lambda total: (total,))
    n_extra, n_out = len(extra), len(out_dtypes)

    def body(a_ref, b_ref, *rest):
        def finish(total):
            results = fn(total, *[r[...] for r in rest[:n_extra]])
            for o_ref, res, dt in zip(rest[n_extra + n_into:n_extra + n_into + n_out], results, out_dtypes):
                o_ref[...] = res.astype(dt)

        part = jnp.dot(a_ref[...], b_ref[...], preferred_element_type=F32)
        _accumulate(part, rest[-1], pl.program_id(3), nk, finish)

    def tile(first):
        return pl.BlockSpec((tm, tn), lambda m, j, n, k: (m, first + (j0 + j) * nn + n))

    outs = _call(
        body, (a, wb) + tuple(extra) + tuple(into), name=name, grid=(M // tm, nj, nn, nk),
        in_specs=[pl.BlockSpec((tm, tk), lambda m, j, n, k: (m, k)),
                  pl.BlockSpec((None, tk, tn), lambda m, j, n, k: (j0 + j, k, n))]
        + [tile(col // tn) for col in first_cols] + [ANY] * n_into,
        out_specs=[tile(0)] * n_out,
        out_shape=[jax.ShapeDtypeStruct((M, NB * Nb), dt) for dt in out_dtypes],
        scratch_shapes=[] if nk == 1 else [pltpu.VMEM((tm, tn), F32)],
        sem=("parallel", "parallel", "parallel", "arbitrary"), after=after,
        aliases={2 + n_extra + i: i for i in range(n_into)})
    return outs if epilogue else outs[0]


def _squared_relu(a):
    ra = jnp.maximum(a, 0.0)
    return a, ra * ra


def _gated_merge(pb, za, zb, pa):
    return pb, _sigmoid(za) * pa + _sigmoid(zb) * pb


def _mm_nt(a, wb, out_dtype, name, after=(), epilogue=None):
    M, N = a.shape
    NB, K, Nb = wb.shape
    assert N == NB * Nb
    tm = min(M, 1024)
    n_tiles_live = 1 + (len(epilogue[0]) + len(epilogue[2]) if epilogue else 0)
    tko = _pick(K, (1024,)) if n_tiles_live <= 3 else _pick(K, (512,))
    tc = _pick(Nb, (2048, 1024, 1408, 256))
    nc = Nb // tc
    jb = max([d for d in (8, 4, 2, 1) if NB % d == 0 and d * tc <= MAX_CONTRACTION_TILE]) if nc == 1 else 1
    nsteps = (NB // jb) * nc
    extra, first_cols, out_dtypes, fn = epilogue or ((), (), (out_dtype,), lambda total: (total,))
    n_extra, n_out = len(extra), len(out_dtypes)

    def body(a_ref, b_ref, *rest):
        def finish(total):
            results = fn(total, *[r[...] for r in rest[:n_extra]])
            for o_ref, res, dt in zip(rest[n_extra:n_extra + n_out], results, out_dtypes):
                o_ref[...] = res.astype(dt)

        part = sum(lax.dot_general(a_ref[:, i * tc:(i + 1) * tc], b_ref[i], (((1,), (1,)), ((), ())),
                                   preferred_element_type=F32) for i in range(jb))
        _accumulate(part, rest[-1], pl.program_id(2) * nc + pl.program_id(3), nsteps, finish)

    def tile(first):
        return pl.BlockSpec((tm, tko), lambda m, ko, j, c: (m, first + ko))

    outs = _call(
        body, (a, wb) + tuple(extra), name=name,
        grid=(M // tm, K // tko, NB // jb, nc),
        in_specs=[pl.BlockSpec((tm, jb * tc), lambda m, ko, j, c: (m, j * nc + c)),
                  pl.BlockSpec((jb, tko, tc), lambda m, ko, j, c: (j, ko, c))] + [tile(col // tko) for col in first_cols],
        out_specs=[tile(0)] * n_out,
        out_shape=[jax.ShapeDtypeStruct((M, K), dt) for dt in out_dtypes],
        scratch_shapes=[] if nsteps == 1 else [pltpu.VMEM((tm, tko), F32)],
        sem=("parallel", "parallel", "arbitrary", "arbitrary"), after=after)
    return outs if epilogue else outs[0]


ROWS_TILE = 512
ROWS_PIECE = 128


def _mm_rows(a, w, extras, vectors, row_dtypes, fn, name):
    M, K = a.shape
    N = w.shape[1]
    tm = min(M, ROWS_TILE)
    n_e, n_v = len(extras), len(vectors)

    def body(a_ref, w_ref, *rest):
        tiles, vecs, outs, product_ref = rest[:n_e], rest[n_e:n_e + n_v], rest[n_e + n_v:-1], rest[-1]
        product_ref[...] = jnp.dot(a_ref[...], w_ref[...], preferred_element_type=F32)
        for i in range(tm // ROWS_PIECE):
            piece = slice(i * ROWS_PIECE, (i + 1) * ROWS_PIECE)
            results = fn(product_ref[piece, :], *[t[piece, :] for t in tiles], *[v[...] for v in vecs])
            for o_ref, res, dt in zip(outs, results, row_dtypes):
                o_ref[piece, :] = res.astype(dt)

    row = pl.BlockSpec((tm, N), lambda m: (m, 0))
    return _call(
        body, (a, w) + tuple(extras) + tuple(vectors), name=name, grid=(M // tm,),
        in_specs=[pl.BlockSpec((tm, K), lambda m: (m, 0)), pl.BlockSpec((K, N), lambda m: (0, 0))]
        + [row] * n_e + [pl.BlockSpec((1, N), lambda m: (0, 0))] * n_v,
        out_specs=[row] * len(row_dtypes),
        out_shape=[jax.ShapeDtypeStruct((M, N), dt) for dt in row_dtypes],
        scratch_shapes=[pltpu.VMEM((tm, N), F32)], sem=("parallel",))


def _rms(h, w):
    return h * lax.rsqrt(jnp.mean(h * h, axis=-1, keepdims=True) + EPS) * w


def _residual_rms_rows(mix, x, w):
    h = x + mix
    return h, _rms(h, w)


def _mm_tn_half(a, g, which, blocks_on, add, name, after=(), a_cols=None):
    M, Ka = a.shape
    N = g.shape[1]
    first_col = 0
    if a_cols is not None:
        first_col, Ka = a_cols
    if blocks_on == "g":
        rows, cols = _pick(Ka, (1024,)), N // N_DEV
        tn = _pick(cols, (512, 1408, 256))
        nn = cols // tn
        grid = (Ka // rows, N_CHIP, nn)
        a_spec = pl.BlockSpec((M, rows), lambda ka, s, n, w: (0, first_col // rows + ka))
        g_spec = pl.BlockSpec((M, tn), lambda ka, s, n, w: (0, (2 * s + w[0]) * nn + n))
        out_rows = Ka
    else:
        rows, cols = Ka // N_DEV, N
        tn = _pick(cols, (2048, 512))
        nn = cols // tn
        grid = (1, N_CHIP, nn)
        a_spec = pl.BlockSpec((M, rows), lambda ka, s, n, w: (0, 2 * s + w[0]))
        g_spec = pl.BlockSpec((M, tn), lambda ka, s, n, w: (0, n))
        out_rows = rows
    o_spec = pl.BlockSpec((None, rows, tn), lambda ka, s, n, w: (s, ka, n))
    n_add = 0 if add is None else 1

    def body(which_ref, a_ref, g_ref, *rest):
        acc = lax.dot_general(a_ref[...], g_ref[...], (((0,), (0,)), ((), ())), preferred_element_type=F32)
        if n_add:
            acc = acc + rest[0][...].astype(F32)
        rest[-1][...] = acc.astype(BF16)

    return pl.pallas_call(
        body, name=name,
        grid_spec=pltpu.PrefetchScalarGridSpec(
            num_scalar_prefetch=1, grid=grid,
            in_specs=[a_spec, g_spec] + [o_spec] * n_add + [ANY] * len(after),
            out_specs=o_spec),
        out_shape=jax.ShapeDtypeStruct((N_CHIP, out_rows, cols), BF16),
        compiler_params=_cparams(("parallel", "parallel", "parallel")),
    )(which, a, g, *(() if add is None else (add,)), *after)


ROW_TILE = 256


def _rms_fwd(x, w, name):
    T, Dm = x.shape

    def body(x_ref, w_ref, u_ref):
        xv = x_ref[...]
        r = lax.rsqrt(jnp.mean(xv * xv, axis=-1, keepdims=True) + EPS)
        u_ref[...] = (xv * r * w_ref[...]).astype(BF16)

    return pl.pallas_call(
        body, name=name, grid=(T // ROW_TILE,),
        in_specs=[pl.BlockSpec((ROW_TILE, Dm), lambda i: (i, 0)), pl.BlockSpec((1, Dm), lambda i: (0, 0))],
        out_specs=pl.BlockSpec((ROW_TILE, Dm), lambda i: (i, 0)),
        out_shape=jax.ShapeDtypeStruct((T, Dm), BF16),
        compiler_params=_cparams(("parallel",)),
    )(x, w)


def _loss_head(h1, mlp, wf, target, name):
    T, Dm = h1.shape

    def body(h_ref, m_ref, w_ref, t_ref, loss_ref, dh_ref, dhb_ref, dw_ref):
        i = pl.program_id(0)
        h = h_ref[...] + m_ref[...]
        r = lax.rsqrt(jnp.mean(h * h, axis=-1, keepdims=True) + EPS)
        xh = h * r
        wv = w_ref[...]
        e = xh * wv - t_ref[...]
        part = 0.5 * jnp.sum(jnp.mean(e * e, axis=-1, keepdims=True), axis=0, keepdims=True)
        dy = e * (1.0 / Dm)
        dw = jnp.sum(dy * xh, axis=0, keepdims=True)
        gy = dy * wv
        dh = r * (gy - xh * jnp.mean(gy * xh, axis=-1, keepdims=True))
        dh_ref[...] = dh
        dhb_ref[...] = dh.astype(BF16)

        @pl.when(i == 0)
        def _():
            loss_ref[...] = jnp.zeros_like(loss_ref)
            dw_ref[...] = jnp.zeros_like(dw_ref)

        loss_ref[...] += jnp.broadcast_to(part, loss_ref.shape)
        dw_ref[...] += dw

    row = pl.BlockSpec((ROW_TILE, Dm), lambda i: (i, 0))
    vec = pl.BlockSpec((1, Dm), lambda i: (0, 0))
    return pl.pallas_call(
        body, name=name, grid=(T // ROW_TILE,),
        in_specs=[row, row, vec, row],
        out_specs=[pl.BlockSpec((8, 128), lambda i: (0, 0)), row, row, vec],
        out_shape=[jax.ShapeDtypeStruct((8, 128), F32), jax.ShapeDtypeStruct((T, Dm), F32),
                   jax.ShapeDtypeStruct((T, Dm), BF16), jax.ShapeDtypeStruct((1, Dm), F32)],
        compiler_params=_cparams(("arbitrary",)),
    )(h1, mlp, wf, target)


def _rms_bwd(dyn, x, w, dres, dx_dtypes, name, after=()):
    T, Dm = x.shape
    n_dx = len(dx_dtypes)

    def body(g_ref, x_ref, w_ref, r_ref, *outs):
        i = pl.program_id(0)
        xv = x_ref[...]
        r = lax.rsqrt(jnp.mean(xv * xv, axis=-1, keepdims=True) + EPS)
        xh = xv * r
        g = g_ref[...]
        dw = jnp.sum(g * xh, axis=0, keepdims=True)
        gy = g * w_ref[...]
        dx = r_ref[...] + r * (gy - xh * jnp.mean(gy * xh, axis=-1, keepdims=True))
        for dx_ref, dt in zip(outs, dx_dtypes):
            dx_ref[...] = dx.astype(dt)
        dw_ref = outs[n_dx]

        @pl.when(i == 0)
        def _():
            dw_ref[...] = jnp.zeros_like(dw_ref)

        dw_ref[...] += dw

    row = pl.BlockSpec((ROW_TILE, Dm), lambda i: (i, 0))
    vec = pl.BlockSpec((1, Dm), lambda i: (0, 0))
    return _call(
        body, (dyn, x, w, dres), name=name, grid=(T // ROW_TILE,),
        in_specs=[row, row, vec, row],
        out_specs=[row] * n_dx + [vec],
        out_shape=[jax.ShapeDtypeStruct((T, Dm), dt) for dt in dx_dtypes] + [jax.ShapeDtypeStruct((1, Dm), F32)],
        sem=("arbitrary",), after=after)


GATE_TILE = 1024


def _merge_grads(d, za, zb, pa, pb):
    ga = _sigmoid(za)
    gb = _sigmoid(zb)
    return d * ga, d * gb, d * pa * ga * (1.0 - ga), d * pb * gb * (1.0 - gb)


def _dot_hi(a, b, dims):
    return lax.dot_general(a, b, (dims, ((), ())), precision=HIGHEST, preferred_element_type=F32)


NN = ((1,), (0,))
NT = ((1,), (1,))
TN = ((0,), (0,))


def _hg_gates(hq, hf, lb):
    sq = _sigmoid(hq)
    q = hq * sq * (HG_DK ** -0.5)
    f = _sigmoid(hf)
    g = lb + (1.0 - lb) * f
    return q, sq, f, g, jnp.log(g), 1.0 - g


def _tri(lower):
    r = lax.broadcasted_iota(jnp.int32, (CHUNK, CHUNK), 0)
    c = lax.broadcasted_iota(jnp.int32, (CHUNK, CHUNK), 1)
    return jnp.where((r >= c) if lower else (r <= c), 1.0, 0.0).astype(BF16)


def _running_sum(tri, x):
    return sum(jnp.dot(tri, piece, preferred_element_type=F32) for piece in _split3(x))


GROUP = 16
N_GROUPS = CHUNK // GROUP
BWD_CHUNKS_PER_TRIP = 4


def _dot_bf16(a, b, dims):
    return lax.dot_general(a.astype(BF16), b.astype(BF16), (dims, ((), ())), preferred_element_type=F32)


def _rows_iota():
    return lax.broadcasted_iota(jnp.int32, (CHUNK, HG_DK), 0)


def _by_query_group(q, kk, b, g):
    r0 = GROUP * g
    b0 = b[r0:r0 + 1]
    decay = jnp.exp(b[r0:r0 + GROUP] - b0)
    ks = jnp.where(_rows_iota() < r0, kk * jnp.exp(jnp.minimum(b0 - b, 0.0)), 0.0)
    return q[r0:r0 + GROUP] * decay, ks, decay


def _by_key_group(q, kk, b, j):
    r1 = GROUP * (j + 1)
    b1 = b[r1 - 1:r1]
    decay = jnp.exp(b1 - b[r1 - GROUP:r1])
    qs = jnp.where(_rows_iota() >= r1, q * jnp.exp(jnp.minimum(b - b1, 0.0)), 0.0)
    return qs, kk[r1 - GROUP:r1] * decay, decay


def _scores_between_groups(q, kk, b):
    blocks = [jnp.zeros((GROUP, CHUNK), F32)]
    for g in range(1, N_GROUPS):
        qs, ks, _ = _by_query_group(q, kk, b, g)
        blocks.append(_dot_bf16(qs, ks, NT))
    return jnp.concatenate(blocks, axis=0)


def _hgrn2_fwd(z, lb_logits, hg_norm_w, name, after=()):
    T = z.shape[0]
    n_chunks = T // CHUNK

    def body(hq_ref, hf_ref, hi_ref, hg_ref, lbl_ref, nw_ref, o_ref, ya_ref, sall_ref, st_ref):
        lbl = lbl_ref[...]
        lb = 1.0 / (1.0 + jnp.exp(lbl[1:2, :] - lbl[0:1, :]))
        st_ref[...] = jnp.zeros_like(st_ref)
        tri = _tri(True)
        row8 = lax.broadcasted_iota(jnp.int32, (8, HG_DK), 0)

        def chunk(c, carry):
            rows = pl.ds(pl.multiple_of(c * CHUNK, CHUNK), CHUNK)
            q, _, _, _, lg, kk = _hg_gates(hq_ref[rows, :], hf_ref[rows, :], lb)
            v = hi_ref[rows, :]
            b = _running_sum(tri, lg)
            st = st_ref[...]
            sall_ref[c] = st
            for grp in range(N_GROUPS):
                r0 = GROUP * grp
                for h8 in range(GROUP // 8):
                    n = 8 * (h8 + 1)
                    bs, ks, vs = b[r0:r0 + n], kk[r0:r0 + n], v[r0:r0 + n]
                    sidx = lax.broadcasted_iota(jnp.int32, (n, HG_DK), 0)
                    blk = jnp.zeros((8, HG_DK), F32)
                    for i in range(8):
                        t = r0 + 8 * h8 + i
                        e = jnp.where(sidx <= 8 * h8 + i, jnp.exp(b[t:t + 1] - bs), 0.0)
                        p = jnp.sum(e * ks * q[t:t + 1], axis=1, keepdims=True)
                        ot = jnp.sum(p * vs, axis=0, keepdims=True)
                        blk = blk + jnp.where(row8 == i, ot, 0.0)
                    o_ref[pl.ds(pl.multiple_of(c * CHUNK + r0 + 8 * h8, 8), 8), :] = blk
            o_ref[rows, :] += _dot_hi(q * jnp.exp(b), st, NT) + _dot_bf16(_scores_between_groups(q, kk, b), v, NN)
            bl = b[CHUNK - 1:CHUNK]
            ke = kk * jnp.exp(bl - b)
            st_ref[...] = st * jnp.exp(bl) + _dot_hi(v, ke, TN)
            return carry

        lax.fori_loop(0, n_chunks, chunk, 0, unroll=2)
        o = o_ref[...]
        r = lax.rsqrt(jnp.mean(o * o, axis=-1, keepdims=True) + EPS)
        hg = hg_ref[...]
        ya_ref[...] = (o * r * nw_ref[...] * (hg * _sigmoid(hg))).astype(BF16)

    def col(base):
        return pl.BlockSpec((T, HG_DK), lambda h: (0, base + h))

    return _call(
        body, (z, z, z, z, lb_logits, hg_norm_w), name=name, grid=(HG_HEADS,),
        in_specs=[col(COL_HQ), col(COL_HF), col(COL_HI), col(COL_HG),
                  pl.BlockSpec((2, HG_DK), lambda h: (0, h)), pl.BlockSpec((1, HG_DK), lambda h: (0, 0))],
        out_specs=[col(0), col(0), pl.BlockSpec((None, n_chunks, HG_DK, HG_DK), lambda h: (h, 0, 0, 0))],
        out_shape=[jax.ShapeDtypeStruct((T, HG_WIDTH), F32), jax.ShapeDtypeStruct((T, HG_WIDTH), BF16),
                   jax.ShapeDtypeStruct((HG_HEADS, n_chunks, HG_DK, HG_DK), F32)],
        scratch_shapes=[pltpu.VMEM((HG_DK, HG_DK), F32)],
        sem=("parallel",), after=after)


def _hgrn2_bwd(z, lb_logits, hg_norm_w, o_raw, s_all, dya, name, after=()):
    T = z.shape[0]
    n_chunks = T // CHUNK

    def body(hq_ref, hf_ref, hi_ref, hg_ref, lbl_ref, nw_ref, o_ref, sall_ref, dya_ref,
             dhq_ref, dhf_ref, dhi_ref, dhg_ref, dlbl_ref, dnw_ref,
             do_ref, dst_ref, dlb_ref, *per_chunk):
        h = pl.program_id(0)
        lbl = lbl_ref[...]
        lb = 1.0 / (1.0 + jnp.exp(lbl[1:2, :] - lbl[0:1, :]))

        o = o_ref[...]
        r = lax.rsqrt(jnp.mean(o * o, axis=-1, keepdims=True) + EPS)
        oh = o * r
        nw = nw_ref[...]
        hg = hg_ref[...]
        sg = _sigmoid(hg)
        dy = dya_ref[...]
        d_on = dy * (hg * sg)
        dhg_ref[...] = (dy * (oh * nw) * (sg * (1.0 + hg * (1.0 - sg)))).astype(BF16)
        dnw = jnp.sum(d_on * oh, axis=0, keepdims=True)
        gy = d_on * nw
        do_ref[...] = r * (gy - oh * jnp.mean(gy * oh, axis=-1, keepdims=True))

        @pl.when(h == 0)
        def _():
            dnw_ref[...] = jnp.zeros_like(dnw_ref)

        dnw_ref[...] += jnp.broadcast_to(dnw, dnw_ref.shape)

        dst_ref[...] = jnp.zeros_like(dst_ref)
        dlb_ref[...] = jnp.zeros_like(dlb_ref)
        tri = _tri(True)
        tri_t = _tri(False)
        row8 = lax.broadcasted_iota(jnp.int32, (8, HG_DK), 0)
        row_group = lax.broadcasted_iota(jnp.int32, (CHUNK, CHUNK), 0) // GROUP
        col_group = lax.broadcasted_iota(jnp.int32, (CHUNK, CHUNK), 1) // GROUP
        earlier_group = col_group < row_group
        later_group = col_group > row_group

        def chunk(c, dq_ref, dk_ref, dv_ref):
            rows = pl.ds(pl.multiple_of(c * CHUNK, CHUNK), CHUNK)
            hq = hq_ref[rows, :]
            q, sq, f, g, lg, kk = _hg_gates(hq, hf_ref[rows, :], lb)
            v = hi_ref[rows, :]
            do = do_ref[rows, :]
            b = _running_sum(tri, lg)
            eb = jnp.exp(b)
            bl = b[CHUNK - 1:CHUNK]
            ebl = jnp.exp(bl)
            ekb = jnp.exp(bl - b)
            qe = q * eb
            ke = kk * ekb
            st = sall_ref[c]
            dst = dst_ref[...]
            dqe = _dot_bf16(do, st, NN)
            dke = _dot_bf16(v, dst, NN)
            dv_inter = _dot_bf16(ke, dst, NT)
            d_ebl = jnp.sum(st * dst, axis=0, keepdims=True)
            dst_ref[...] = dst * ebl + _dot_bf16(do, qe, TN)

            dk_ref[...] = jnp.zeros_like(dk_ref)
            dv_ref[...] = jnp.zeros_like(dv_ref)
            for grp in range(N_GROUPS):
                r0 = GROUP * grp
                for h8 in range(GROUP // 8):
                    n = 8 * (h8 + 1)
                    bs, ks, vs = b[r0:r0 + n], kk[r0:r0 + n], v[r0:r0 + n]
                    sidx = lax.broadcasted_iota(jnp.int32, (n, HG_DK), 0)
                    blk = jnp.zeros((8, HG_DK), F32)
                    for i in range(8):
                        t = r0 + 8 * h8 + i
                        qt = q[t:t + 1]
                        dot_ = do[t:t + 1]
                        e = jnp.where(sidx <= 8 * h8 + i, jnp.exp(b[t:t + 1] - bs), 0.0)
                        w = e * ks
                        p = jnp.sum(w * qt, axis=1, keepdims=True)
                        dsc = jnp.sum(vs * dot_, axis=1, keepdims=True)
                        dqt = jnp.sum(dsc * w, axis=0, keepdims=True)
                        blk = blk + jnp.where(row8 == i, dqt, 0.0)
                        dk_ref[r0:r0 + n, :] += dsc * e * qt
                        dv_ref[r0:r0 + n, :] += p * dot_
                    dq_ref[r0 + 8 * h8:r0 + n, :] = blk
            ds_far = jnp.where(earlier_group, _dot_bf16(do, v, NT), 0.0)
            ds_far_t = jnp.where(later_group, _dot_bf16(v, do, NT), 0.0)
            dq_far, dk_far = [jnp.zeros((GROUP, HG_DK), F32)], []
            for grp in range(1, N_GROUPS):
                r0 = GROUP * grp
                _, ks, decay = _by_query_group(q, kk, b, grp)
                dq_far.append(decay * _dot_hi(ds_far[r0:r0 + GROUP], ks, NN))
                qs, _, decay = _by_key_group(q, kk, b, grp - 1)
                dk_far.append(decay * _dot_hi(ds_far_t[r0 - GROUP:r0], qs, NN))
            dk_far.append(jnp.zeros((GROUP, HG_DK), F32))
            dv_far = _dot_bf16(_scores_between_groups(q, kk, b), do, TN)
            dq_i = dq_ref[...] + jnp.concatenate(dq_far, axis=0)
            dk_i = dk_ref[...] + jnp.concatenate(dk_far, axis=0)
            dke_ke = dke * ke
            db = q * dq_i - kk * dk_i + dqe * qe - dke_ke
            db_last = jnp.sum(dke_ke, axis=0, keepdims=True) + d_ebl * ebl
            dlg = _running_sum(tri_t, db) + db_last
            dq = dq_i + dqe * eb
            dkk = dk_i + dke * ekb
            dg = dlg / g - dkk
            dhq_ref[rows, :] = (dq * (HG_DK ** -0.5) * (sq * (1.0 + hq * (1.0 - sq)))).astype(BF16)
            dhf_ref[rows, :] = (dg * (1.0 - lb) * f * (1.0 - f)).astype(BF16)
            dhi_ref[rows, :] = (dv_ref[...] + dv_far + dv_inter).astype(BF16)
            dlb_ref[...] += jnp.sum(dg * (1.0 - f), axis=0, keepdims=True)

        def trip(i, carry):
            for k in range(BWD_CHUNKS_PER_TRIP):
                chunk(n_chunks - 1 - k - BWD_CHUNKS_PER_TRIP * i, *per_chunk[3 * k:3 * k + 3])
            return carry

        lax.fori_loop(0, n_chunks // BWD_CHUNKS_PER_TRIP, trip, 0)
        dl0 = dlb_ref[...] * lb * (1.0 - lb)
        dlbl_ref[0:1, :] = dl0
        dlbl_ref[1:2, :] = -dl0

    def col(base):
        return pl.BlockSpec((T, HG_DK), lambda h: (0, base + h))

    outb = jax.ShapeDtypeStruct((T, HG_WIDTH), BF16)
    return _call(
        body, (z, z, z, z, lb_logits, hg_norm_w, o_raw, s_all, dya), name=name, grid=(HG_HEADS,),
        in_specs=[col(COL_HQ), col(COL_HF), col(COL_HI), col(COL_HG),
                  pl.BlockSpec((2, HG_DK), lambda h: (0, h)), pl.BlockSpec((1, HG_DK), lambda h: (0, 0)),
                  col(0), pl.BlockSpec((None, n_chunks, HG_DK, HG_DK), lambda h: (h, 0, 0, 0)), col(0)],
        out_specs=[col(0), col(0), col(0), col(0), pl.BlockSpec((2, HG_DK), lambda h: (0, h)),
                   pl.BlockSpec((8, HG_DK), lambda h: (0, 0))],
        out_shape=[outb, outb, outb, outb, jax.ShapeDtypeStruct((2, HG_WIDTH), F32),
                   jax.ShapeDtypeStruct((8, HG_DK), F32)],
        scratch_shapes=[pltpu.VMEM((T, HG_DK), F32), pltpu.VMEM((HG_DK, HG_DK), F32), pltpu.VMEM((1, HG_DK), F32)]
        + [pltpu.VMEM((CHUNK, HG_DK), F32)] * (3 * BWD_CHUNKS_PER_TRIP),
        sem=("arbitrary",), after=after)


CONST_KEYS = PAD - REL_CLIP
VAR_KEYS = BAND - CONST_KEYS
REL_LO = 128
REL_SPAN = N_REL_PAD - REL_LO


def _rel_onehot(t):
    r = lax.broadcasted_iota(jnp.int32, (REL_SPAN, VAR_KEYS), 0)
    j = lax.broadcasted_iota(jnp.int32, (REL_SPAN, VAR_KEYS), 1)
    idx = jnp.clip(t + PAD - CONST_KEYS - j, -REL_CLIP, REL_CLIP) + REL_CLIP - REL_LO
    return jnp.where(r == idx, 1.0, 0.0).astype(BF16)


def _split3(x):
    hi = x.astype(BF16)
    r1 = x - hi.astype(F32)
    mid = r1.astype(BF16)
    return hi, mid, (r1 - mid.astype(F32)).astype(BF16)


def _bias_expand(rel, name):
    rows = 8

    def body(rel_ref, out_ref):
        tab = rel_ref[...]
        pieces = _split3(tab[:, REL_LO:N_REL_PAD])
        constant = jnp.broadcast_to(tab[:, 2 * REL_CLIP:2 * REL_CLIP + 1], (AT_HEADS, CONST_KEYS))
        for i in range(rows):
            onehot = _rel_onehot(pl.program_id(0) * rows + i)
            out_ref[i, :, 0:CONST_KEYS] = constant
            out_ref[i, :, CONST_KEYS:BAND] = sum(jnp.dot(piece, onehot, preferred_element_type=F32)
                                                 for piece in pieces)

    return pl.pallas_call(
        body, name=name, grid=(CHUNK // rows,),
        in_specs=[pl.BlockSpec((AT_HEADS, N_REL_PAD), lambda t: (0, 0))],
        out_specs=pl.BlockSpec((rows, AT_HEADS, BAND), lambda t: (t, 0, 0)),
        out_shape=jax.ShapeDtypeStruct((CHUNK, AT_HEADS, BAND), F32),
        compiler_params=_cparams(("parallel",)),
    )(rel)


def _bias_reduce(dbias_rows, name, after=()):
    def body(db_ref, out_ref):
        lane = lax.broadcasted_iota(jnp.int32, (AT_HEADS, N_REL_PAD), 1)
        varying = lane >= CONST_KEYS
        by_offset = jnp.zeros((AT_HEADS, N_REL_PAD), F32)
        constant = jnp.zeros((AT_HEADS, N_REL_PAD), F32)
        for t in range(CHUNK):
            row = db_ref[t]
            constant = constant + jnp.where(varying, 0.0, row)
            moved = jnp.where(varying, row, 0.0)
            by_offset = by_offset + (pltpu.roll(moved, N_REL_PAD - t, axis=1) if t else moved)
        offset = lax.broadcasted_iota(jnp.int32, (N_REL_PAD, N_REL_PAD), 0)
        entry = lax.broadcasted_iota(jnp.int32, (N_REL_PAD, N_REL_PAD), 1)
        onehot = jnp.where(entry == jnp.clip(PAD - offset, -REL_CLIP, REL_CLIP) + REL_CLIP, 1.0, 0.0).astype(BF16)
        acc = sum(jnp.dot(piece, onehot, preferred_element_type=F32) for piece in _split3(by_offset))
        last = jnp.sum(constant, axis=1, keepdims=True)
        out_ref[...] = acc + jnp.where(lane == 2 * REL_CLIP, last, 0.0)

    whole = pl.BlockSpec((CHUNK, AT_HEADS, N_REL_PAD), lambda i: (0, 0, 0))
    return _call(
        body, (dbias_rows,), name=name, grid=(1,), in_specs=[whole],
        out_specs=[pl.BlockSpec((AT_HEADS, N_REL_PAD), lambda i: (0, 0))],
        out_shape=[jax.ShapeDtypeStruct((AT_HEADS, N_REL_PAD), F32)],
        sem=("arbitrary",), after=after)[0]


def _pair_lanes():
    return lax.broadcasted_iota(jnp.int32, (CHUNK, 2 * AT_DH), 1) < AT_DH


def _block_diag(a):
    first = _pair_lanes()
    return jnp.concatenate([jnp.where(first, a, 0.0), jnp.where(first, 0.0, a)], axis=0).astype(BF16)


def _diag_blocks(a):
    return jnp.where(_pair_lanes(), a[:CHUNK], a[CHUNK:])


def _band_probs_t(kb, qbd, bias_t, c):
    s = lax.dot_general(kb, qbd, (NT, ((), ())), preferred_element_type=F32) * (AT_DH ** -0.5) + bias_t
    j = lax.broadcasted_iota(jnp.int32, (BAND, 2 * AT_DH), 0)
    s = jnp.where(j + c * CHUNK >= PAD, s, -jnp.inf)
    p = jnp.exp(s - jnp.max(s, axis=0, keepdims=True))
    return p / jnp.sum(p, axis=0, keepdims=True)


def _attn_fwd(z, bias_t, name, after=()):
    T = z.shape[0]
    n_chunks = T // CHUNK

    def body(q_ref, k_ref, v_ref, bias_ref, y_ref, p_ref, *scratch):
        for pr in range(2):
            lanes = slice(128 * pr, 128 * (pr + 1))
            for dst_ref, src_ref in zip(scratch[2 * pr:2 * pr + 2], (k_ref, v_ref)):
                dst_ref[0:PAD, :] = jnp.zeros((PAD, 128), BF16)
                dst_ref[PAD:PAD + T, :] = src_ref[:, lanes].astype(BF16)

        def chunk(c, carry):
            rows = pl.ds(pl.multiple_of(c * CHUNK, CHUNK), CHUNK)
            band = pl.ds(pl.multiple_of(c * CHUNK, CHUNK), BAND)
            for pr in range(2):
                kp_ref, vp_ref = scratch[2 * pr:2 * pr + 2]
                lanes = slice(128 * pr, 128 * (pr + 1))
                p = _band_probs_t(kp_ref[band, :], _block_diag(q_ref[rows, lanes]), bias_ref[pr], c).astype(BF16)
                p_ref[pr, c] = p
                o2 = lax.dot_general(p, vp_ref[band, :], (TN, ((), ())), preferred_element_type=F32)
                y_ref[rows, lanes] = _diag_blocks(o2).astype(BF16)
            return carry

        lax.fori_loop(0, n_chunks, chunk, 0, unroll=2)

    def col(base):
        return pl.BlockSpec((T, 256), lambda h: (0, base // 2 + h))

    return _call(
        body, (z, z, z, bias_t), name=name, grid=(AT_HEADS // 4,),
        in_specs=[col(COL_AQ), col(COL_AK), col(COL_AV), pl.BlockSpec((2, BAND, 128), lambda h: (h, 0, 0))],
        out_specs=[col(0), pl.BlockSpec((2, n_chunks, BAND, 128), lambda h: (h, 0, 0, 0))],
        out_shape=[jax.ShapeDtypeStruct((T, AT_WIDTH), BF16),
                   jax.ShapeDtypeStruct((AT_HEADS // 2, n_chunks, BAND, 128), BF16)],
        scratch_shapes=[pltpu.VMEM((PAD + T, 128), BF16)] * 4,
        sem=("parallel",), after=after)


def _attn_bwd(z, probs, dyb, name, after=()):
    T = z.shape[0]
    n_chunks = T // CHUNK

    def body(q_ref, k_ref, v_ref, p_ref, dy_ref, dq_ref, dk_ref, dv_ref, dbias_ref, *scratch):
        dbias_ref[...] = jnp.zeros_like(dbias_ref)
        for pr in range(2):
            kp_ref, vp_ref, dkp_ref, dvp_ref = scratch[4 * pr:4 * pr + 4]
            lanes = slice(128 * pr, 128 * (pr + 1))
            kp_ref[0:PAD, :] = jnp.zeros((PAD, 128), BF16)
            vp_ref[0:PAD, :] = jnp.zeros((PAD, 128), BF16)
            kp_ref[PAD:PAD + T, :] = k_ref[:, lanes].astype(BF16)
            vp_ref[PAD:PAD + T, :] = v_ref[:, lanes].astype(BF16)
            dkp_ref[...] = jnp.zeros_like(dkp_ref)
            dvp_ref[...] = jnp.zeros_like(dvp_ref)

        def chunk(c, carry):
            rows = pl.ds(pl.multiple_of(c * CHUNK, CHUNK), CHUNK)
            band = pl.ds(pl.multiple_of(c * CHUNK, CHUNK), BAND)
            for pr in range(2):
                kp_ref, vp_ref, dkp_ref, dvp_ref = scratch[4 * pr:4 * pr + 4]
                lanes = slice(128 * pr, 128 * (pr + 1))
                qbd = _block_diag(q_ref[rows, lanes])
                dobd = _block_diag(dy_ref[rows, lanes])
                pb = p_ref[pr, c]
                p = pb.astype(F32)
                dp = lax.dot_general(vp_ref[band, :], dobd, (NT, ((), ())), preferred_element_type=F32)
                ds = p * (dp - jnp.sum(dp * p, axis=0, keepdims=True))
                dbias_ref[pr] += ds
                dsb = ds.astype(BF16)
                dq2 = lax.dot_general(dsb, kp_ref[band, :], (TN, ((), ())), preferred_element_type=F32)
                dq_ref[rows, lanes] = (_diag_blocks(dq2) * (AT_DH ** -0.5)).astype(BF16)
                dkp_ref[band, :] += jnp.dot(dsb, qbd, preferred_element_type=F32) * (AT_DH ** -0.5)
                dvp_ref[band, :] += jnp.dot(pb, dobd, preferred_element_type=F32)
            return carry

        lax.fori_loop(0, n_chunks, chunk, 0)
        for pr in range(2):
            lanes = slice(128 * pr, 128 * (pr + 1))
            dk_ref[:, lanes] = scratch[4 * pr + 2][PAD:PAD + T, :].astype(BF16)
            dv_ref[:, lanes] = scratch[4 * pr + 3][PAD:PAD + T, :].astype(BF16)

    def col(base):
        return pl.BlockSpec((T, 256), lambda h: (0, base // 2 + h))

    outb = jax.ShapeDtypeStruct((T, AT_WIDTH), BF16)
    return _call(
        body, (z, z, z, probs, dyb), name=name, grid=(AT_HEADS // 4,),
        in_specs=[col(COL_AQ), col(COL_AK), col(COL_AV),
                  pl.BlockSpec((2, n_chunks, BAND, 128), lambda h: (h, 0, 0, 0)), col(0)],
        out_specs=[col(0), col(0), col(0), pl.BlockSpec((2, BAND, 128), lambda h: (h, 0, 0))],
        out_shape=[outb, outb, outb, jax.ShapeDtypeStruct((AT_HEADS // 2, BAND, 128), F32)],
        scratch_shapes=([pltpu.VMEM((PAD + T, 128), BF16)] * 2 + [pltpu.VMEM((PAD + T, 128), F32)] * 2) * 2,
        sem=("parallel",), after=after)


def _local_step(x, target, lb_logits, hg_norm_w, rel_bias, norm_mix_w, norm_mlp_w, norm_final_w,
                w_in, rest, exchanges=None):
    ex = exchanges
    rel = jnp.pad(rel_bias, ((0, 0), (0, N_REL_PAD - N_REL)))

    u = _rms_fwd(x, norm_mix_w, "rms_mix_fwd")
    if ex:
        z, w_in = _mm_gathered(u, w_in, ex.order, "mm_in_fwd")
        gather = _Gather(rest[:3], [w_in], "ag")
        mlp_shards, _ = lax.optimization_barrier((rest[3:], gather.token))
        gather_mlp = _Gather([s.astype(BF16) for s in mlp_shards], [gather.token], "ag_mlp")
        z = _mm_gathered_tail(u, w_in, z, ex.order, "mm_in_fwd_tail", after=[gather_mlp.token])
        tok = []
    else:
        z = _mm_nn(u, w_in, F32, "mm_in_fwd")
        w_a, w_b, w_out, w_up, w_down = rest
        tok = []
    o_raw, y_a, s_all = _hgrn2_fwd(z, lb_logits, hg_norm_w, "hgrn2_fwd", after=tok)
    if ex:
        tok = [gather.pass_on([0, 1, 2], [o_raw], "abo")]
    bias_rows = _bias_expand(rel, "bias_expand")
    bias_t = jnp.transpose(bias_rows.reshape(CHUNK, AT_HEADS // 2, 2, BAND), (1, 3, 2, 0)).reshape(
        AT_HEADS // 2, BAND, 2 * CHUNK)
    y_b, probs = _attn_fwd(z, bias_t, "attn_fwd", after=tok)
    if ex:
        tok = [gather_mlp.pass_on([0], [y_b], "up")]
        w_a, w_b, w_out = gather.finish([0, 1, 2], tok, "abo")
    pa = _mm_nn(y_a, w_a, F32, "mm_a_fwd")
    pb, merged = _mm_nn(y_b, w_b, None, "mm_b_fwd", epilogue=(
        (z, z, pa), (COL_GATE_A * GATE_TILE, COL_GATE_B * GATE_TILE, 0), (F32, BF16), _gated_merge))
    w_out1 = w_out.reshape(1, D_MODEL, D_MODEL)
    h1, u2 = _mm_rows(merged, w_out.reshape(D_MODEL, D_MODEL), [x], [norm_mlp_w], (F32, BF16),
                      _residual_rms_rows, "mm_out_fwd")
    if ex:
        w_up, = gather_mlp.finish([0], [u2], "up")
    act = ((), (), (F32, BF16), _squared_relu)
    a, r = _mm_nn(u2, w_up, None, "mm_up_fwd_first", epilogue=act, blocks=(0, N_DEV // 2))
    tok = [gather_mlp.pass_on([1], [r], "down")] if ex else []
    a, r = _mm_nn(u2, w_up, None, "mm_up_fwd_second", after=tok, epilogue=act, blocks=(N_DEV // 2, N_DEV // 2),
                  into=(a, r))
    if ex:
        w_down, = gather_mlp.finish([1], [r], "down")
    w_down1 = w_down.reshape(1, D_FF, D_MODEL)
    mlp = _mm_nn(r, w_down1, F32, "mm_down_fwd")
    loss, dh2, dh2b, g_nf = _loss_head(h1, mlp, norm_final_w, target, "loss_head")

    own = ex.parity if ex else jnp.zeros((1,), jnp.int32)

    def sibling_half(weights, name, after=()):
        others = [_mm_tn_half(a_, g_, 1 - own, on, None, nm + "_sibling", after, *cols)
                  for a_, g_, on, nm, *cols in weights]
        rs = _ReduceScatter(others, name) if ex else None
        return rs, others, ([rs.token] if ex else [])

    def own_half(rs, weights, others, after):
        landed = rs.from_sibling(after) if ex else [None] * len(weights)
        sums = [_mm_tn_half(a_, g_, own, on, l, nm + "_own", (), *cols)
                for (a_, g_, on, nm, *cols), l in zip(weights, landed)]
        if ex:
            return [rs.scatter(sums)], None
        return [], [jnp.stack([s_, o_], axis=1).reshape((N_DEV,) + s_.shape[1:]) for s_, o_ in zip(sums, others)]

    down = [(r, dh2b, "a", "mm_down_wgrad")]
    rs_down, others, tok = sibling_half(down, "rs_down")
    da, = _mm_nt(dh2b, w_down1, None, "mm_down_dgrad", after=tok, epilogue=(
        (a,), (0,), (BF16,), lambda dr, av: (dr * (2.0 * jnp.maximum(av, 0.0)),)))
    tok, g_down = own_half(rs_down, down, others, [da])
    up = [(u2, da, "g", "mm_up_wgrad")]
    rs_up, others, tok = sibling_half(up, "rs_up", tok)
    du2 = _mm_nt(da, w_up, F32, "mm_up_dgrad", after=tok)
    tok, g_up = own_half(rs_up, up, others, [du2])
    dh1, dh1b, g_nmlp = _rms_bwd(du2, h1, norm_mlp_w, dh2, (F32, BF16), "rms_mlp_bwd", after=tok)

    dpa, dpb, dga, dgb = _mm_nt(dh1b, w_out1, None, "mm_out_dgrad", epilogue=(
        (z, z, pa, pb), (COL_GATE_A * GATE_TILE, COL_GATE_B * GATE_TILE, 0, 0), (BF16,) * 4, _merge_grads))
    mix = [(y_a, dpa, "g", "mm_a_wgrad"), (y_b, dpb, "g", "mm_b_wgrad"), (merged, dh1b, "a", "mm_out_wgrad")]
    rs_mix, others, tok = sibling_half(mix, "rs_mix")
    dya = _mm_nt(dpa, w_a, F32, "mm_a_dgrad", after=tok)
    dyb = _mm_nt(dpb, w_b, F32, "mm_b_dgrad", after=tok)
    tok, g_mix = own_half(rs_mix, mix, others, [dya, dyb])
    daq, dak, dav, dbias_t = _attn_bwd(z, probs, dyb, "attn_bwd", after=tok)
    dhq, dhf, dhi, dhg, g_lbl, g_hgw = _hgrn2_bwd(z, lb_logits, hg_norm_w, o_raw, s_all, dya, "hgrn2_bwd",
                                                  after=tok)
    dbias_rows = jnp.pad(jnp.transpose(dbias_t.reshape(AT_HEADS // 2, BAND, 2, CHUNK), (3, 0, 2, 1)).reshape(
        CHUNK, AT_HEADS, BAND), ((0, 0), (0, 0), (0, N_REL_PAD - BAND)))
    dz = jnp.concatenate([dhq, dhf, dhi, dhg, daq, dak, dav, dga, dgb], axis=1)
    half = D_MODEL // 2
    lo = [(u, dz, "g", "mm_in_wgrad_lo", (0, half))]
    hi = [(u, dz, "g", "mm_in_wgrad_hi", (half, half))]
    rs_in_lo, others_lo, tok = sibling_half(lo, "rs_in_lo")
    rs_in_hi, others_hi, tok = sibling_half(hi, "rs_in_hi", tok)
    tok, g_in_lo = own_half(rs_in_lo, lo, others_lo, tok)
    du = _mm_nt(dz, w_in, F32, "mm_in_dgrad", after=tok)
    tok, g_in_hi = own_half(rs_in_hi, hi, others_hi, [du])
    grad_x, g_nmix = _rms_bwd(du, x, norm_mix_w, dh1, (F32,), "rms_mix_bwd", after=tok)
    g_rel = _bias_reduce(dbias_rows, "bias_reduce", after=tok)[:, :N_REL]

    small = dict(lb_logits=g_lbl, hg_norm_w=g_hgw[0:1], rel_bias=g_rel, norm_mix_w=g_nmix, norm_mlp_w=g_nmlp,
                 norm_final_w=g_nf)
    if ex:
        grads = [(rs_in_lo, rs_in_hi), rs_mix, rs_up, rs_down]
    else:
        grads = [jnp.concatenate([g_in_lo[0], g_in_hi[0]], axis=1)] + g_mix + [g_up[0], g_down[0]]
    return loss, grad_x, grads, small


def _mm_gathered(u, shard, order, name):
    T, K = u.shape
    _, Nb = shard.shape

    def body(order_ref, u_ref, shard_ref, z_ref, full_ref, wbuf, load_sem, send_sems, recv_sems, local_sem):
        s = pl.program_id(0)
        x, y, c = _position()
        me, sibling = (x, y, c), (x, y, 1 - c)
        chips = [(1 - x, y), (x, 1 - y), (1 - x, 1 - y)]

        def copy(k, block, to, src=None):
            dst = full_ref.at[4 * block[0] + 2 * block[1] + block[2]]
            return pltpu.make_async_remote_copy(
                src_ref=dst if src is None else src, dst_ref=dst,
                send_sem=send_sems.at[k], recv_sem=recv_sems.at[k], device_id=to, device_id_type=MESH)

        @pl.when(s == 0)
        def _():
            local = pltpu.make_async_copy(shard_ref, full_ref.at[4 * x + 2 * y + c], local_sem)
            local.start()
            copy(0, me, sibling, src=shard_ref).start()
            for j, chip in enumerate(chips):
                copy(1 + j, me, (*chip, c), src=shard_ref).start()
            local.wait()

        @pl.when(s == 1)
        def _():
            copy(0, sibling, me).wait_recv()

        for j, chip in enumerate(chips):
            direct, passed = ((2, 4), (3, 5), (6, 7))[j]

            @pl.when(s == direct)
            def _(j=j, chip=chip):
                copy(1 + j, (*chip, c), me).wait_recv()
                copy(4 + j, (*chip, c), sibling).start()

            @pl.when(s == passed)
            def _(j=j, chip=chip):
                copy(4 + j, (*chip, 1 - c), me).wait_recv()

        @pl.when(s < N_EARLY_BLOCKS)
        def _():
            load = pltpu.make_async_copy(full_ref.at[order_ref[s]], wbuf, load_sem)
            load.start()
            load.wait()
            z_ref[...] = jnp.dot(u_ref[...], wbuf[...], preferred_element_type=F32)

        @pl.when(s == N_DEV - 1)
        def _():
            for k in range(7):
                copy(k, me, sibling).wait_send()

    z, full = pl.pallas_call(
        body, name=name,
        grid_spec=pltpu.PrefetchScalarGridSpec(
            num_scalar_prefetch=1, grid=(N_DEV,),
            in_specs=[pl.BlockSpec((T, K), lambda s, order: (0, 0)), ANY],
            out_specs=[pl.BlockSpec((T, Nb), lambda s, order: (0, order[jnp.minimum(s, N_EARLY_BLOCKS - 1)])), ANY],
            scratch_shapes=[pltpu.VMEM((K, Nb), BF16), pltpu.SemaphoreType.DMA,
                            pltpu.SemaphoreType.DMA((7,)), pltpu.SemaphoreType.DMA((7,)), pltpu.SemaphoreType.DMA]),
        out_shape=[jax.ShapeDtypeStruct((T, N_DEV * Nb), F32), jax.ShapeDtypeStruct((N_DEV, K, Nb), BF16)],
        compiler_params=_cparams(("arbitrary",)),
    )(order, u, shard)
    return z, full


N_EARLY_BLOCKS = 6


def _mm_gathered_tail(u, full, z, order, name, after=()):
    T, K = u.shape
    _, _, Nb = full.shape
    n_after = len(after)

    def body(order_ref, u_ref, w_ref, z_in_ref, *rest):
        rest[n_after][...] = jnp.dot(u_ref[...], w_ref[...], preferred_element_type=F32)

    return pl.pallas_call(
        body, name=name,
        grid_spec=pltpu.PrefetchScalarGridSpec(
            num_scalar_prefetch=1, grid=(N_DEV - N_EARLY_BLOCKS,),
            in_specs=[pl.BlockSpec((T, K), lambda s, order: (0, 0)),
                      pl.BlockSpec((None, K, Nb), lambda s, order: (order[N_EARLY_BLOCKS + s], 0, 0)), ANY]
            + [ANY] * n_after,
            out_specs=pl.BlockSpec((T, Nb), lambda s, order: (0, order[N_EARLY_BLOCKS + s]))),
        out_shape=jax.ShapeDtypeStruct(z.shape, z.dtype),
        input_output_aliases={3: 0},
        compiler_params=_cparams(("arbitrary",)),
    )(order, u, full, z, *after)


def _gather_order():
    x, y, c = _position()
    chips = [(1 - x, y), (x, 1 - y), (1 - x, 1 - y)]
    ids = [4 * x + 2 * y + c, 4 * x + 2 * y + (1 - c)]
    ids += [4 * cx + 2 * cy + c for cx, cy in chips[:2]] + [4 * cx + 2 * cy + (1 - c) for cx, cy in chips[:2]]
    ids += [4 * chips[2][0] + 2 * chips[2][1] + c, 4 * chips[2][0] + 2 * chips[2][1] + (1 - c)]
    return jnp.stack(ids).astype(jnp.int32)


HBM = pl.BlockSpec(memory_space=pltpu.HBM)
SEM = pl.BlockSpec(memory_space=pltpu.SEMAPHORE)
DATAFLOW = pltpu.SideEffectType.DATAFLOW_SIDE_EFFECTING


def _split_call(name, bufs, waits=(), starts=None, after=()):
    nb = len(bufs)
    n_new = starts[1] if starts else 0
    wait_sems = [s for w in waits for s in (*w[1], *w[2])]

    def body(*refs):
        b, pos = refs[:nb], nb
        for plan, ss, _, send_idx, recv_idx in waits:
            k = len(ss)
            copies = plan(b, refs[pos:pos + k], refs[pos + k:pos + 2 * k])
            pos += 2 * k
            for i in recv_idx:
                copies[i].wait_recv()
            for i in send_idx:
                copies[i].wait_send()
        outs = refs[pos + len(after):]
        if starts:
            for cp in starts[0](b, outs[nb:nb + n_new], outs[nb + n_new:nb + 2 * n_new]):
                cp.start()
        outs[-1][...] = jnp.zeros_like(outs[-1])

    res = pl.pallas_call(
        body, name=name,
        out_shape=tuple(pltpu.HBM(a.shape, a.dtype) for a in bufs) + (pltpu.SemaphoreType.DMA(()),) * (2 * n_new)
        + (jax.ShapeDtypeStruct((8, 128), F32),),
        in_specs=[HBM] * nb + [SEM] * len(wait_sems) + [ANY] * len(after),
        out_specs=(HBM,) * nb + (SEM,) * (2 * n_new) + (pl.BlockSpec(memory_space=pltpu.VMEM),),
        input_output_aliases={i: i for i in range(nb)},
        compiler_params=pltpu.CompilerParams(has_side_effects=DATAFLOW),
    )(*bufs, *wait_sems, *after)
    return list(res[:nb]), list(res[nb:nb + n_new]), list(res[nb + n_new:nb + 2 * n_new]), res[-1]


def _in_hbm(a):
    return pltpu.with_memory_space_constraint(a, pltpu.HBM)


def _remote(src, dst, send_sem, recv_sem, to):
    return pltpu.make_async_remote_copy(src_ref=src, dst_ref=dst, send_sem=send_sem, recv_sem=recv_sem,
                                        device_id=to, device_id_type=MESH)


def _other_chips():
    x, y, _ = _position()
    return [(1 - x, y), (x, 1 - y), (1 - x, 1 - y)]


def _plan_gather_first(n):
    def plan(b, ss, rs):
        x, y, c = _position()
        to = [(x, y, 1 - c)] + [(*chip, c) for chip in _other_chips()]
        return [_remote(b[w], b[n + w].at[4 * x + 2 * y + c], ss[4 * w + k], rs[4 * w + k], to[k])
                for w in range(n) for k in range(4)]
    return plan, 4 * n


def _plan_gather_pass(n):
    def plan(b, ss, rs):
        x, y, c = _position()
        copies = []
        for w in range(n):
            for j, chip in enumerate(_other_chips()):
                blk = b[n + w].at[4 * chip[0] + 2 * chip[1] + c]
                copies.append(_remote(blk, blk, ss[3 * w + j], rs[3 * w + j], (x, y, 1 - c)))
        return copies
    return plan, 3 * n


def _plan_sibling(n):
    def plan(b, ss, rs):
        x, y, c = _position()
        return [_remote(b[w].at[s], b[n + w].at[s], ss[4 * w + s], rs[4 * w + s], (x, y, 1 - c))
                for w in range(n) for s in range(N_CHIP)]
    return plan, 4 * n


def _plan_scatter(n):
    def plan(b, ss, rs):
        x, y, c = _position()
        return [_remote(b[w].at[2 * chip[0] + chip[1]], b[n + w].at[2 * x + y], ss[3 * w + j], rs[3 * w + j],
                        (*chip, c))
                for w in range(n) for j, chip in enumerate(_other_chips())]
    return plan, 3 * n


class _Gather:
    def __init__(self, shards, after, name):
        self.n, self.name = len(shards), name
        x, y, c = _position()
        placed = [lax.dynamic_update_index_in_dim(lax.empty((N_DEV,) + s.shape, s.dtype), s, 4 * x + 2 * y + c, 0)
                  for s in shards]
        bufs, self.ss, self.rs, self.token = _split_call(
            name + "_start", [_in_hbm(a) for a in list(shards) + placed], starts=_plan_gather_first(self.n),
            after=after)
        self.shards, self.fulls = bufs[:self.n], bufs[self.n:]
        self.passed = {}

    def _sub(self, ids, sems, per):
        return [sems[per * w + k] for w in ids for k in range(per)]

    def pass_on(self, ids, after, tag):
        m = len(ids)
        first = (_plan_gather_first(m)[0], self._sub(ids, self.ss, 4), self._sub(ids, self.rs, 4),
                 [], [4 * i + k for i in range(m) for k in (1, 2, 3)])
        bufs, ss, rs, token = _split_call(
            "%s_pass_%s" % (self.name, tag), [self.shards[w] for w in ids] + [self.fulls[w] for w in ids],
            waits=[first], starts=_plan_gather_pass(m), after=after)
        for i, w in enumerate(ids):
            self.shards[w], self.fulls[w] = bufs[i], bufs[m + i]
        self.passed[tuple(ids)] = (ss, rs)
        return token

    def finish(self, ids, after, tag):
        m = len(ids)
        ss2, rs2 = self.passed[tuple(ids)]
        first = (_plan_gather_first(m)[0], self._sub(ids, self.ss, 4), self._sub(ids, self.rs, 4),
                 list(range(4 * m)), [4 * i for i in range(m)])
        passed = (_plan_gather_pass(m)[0], ss2, rs2, list(range(3 * m)), list(range(3 * m)))
        bufs, _, _, _ = _split_call(
            "%s_finish_%s" % (self.name, tag), [self.shards[w] for w in ids] + [self.fulls[w] for w in ids],
            waits=[first, passed], after=after)
        return bufs[m:]


class _ReduceScatter:
    def __init__(self, others, name):
        self.n, self.name = len(others), name
        lands = [lax.empty(g.shape, g.dtype) for g in others]
        self.bufs, self.ss, self.rs, self.token = _split_call(
            name + "_sibling_start", [_in_hbm(a) for a in list(others) + lands], starts=_plan_sibling(self.n))

    def from_sibling(self, after):
        n = self.n
        bufs, _, _, _ = _split_call(
            self.name + "_sibling_wait", self.bufs,
            waits=[(_plan_sibling(n)[0], self.ss, self.rs, list(range(4 * n)), list(range(4 * n)))], after=after)
        return bufs[n:]

    def scatter(self, sums):
        lands = [lax.empty(s.shape, s.dtype) for s in sums]
        self.bufs, self.ss, self.rs, token = _split_call(
            self.name + "_scatter_start", [_in_hbm(a) for a in list(sums) + lands], starts=_plan_scatter(self.n))
        return token

    def finish(self, after):
        n = self.n
        bufs, _, _, _ = _split_call(
            self.name + "_scatter_wait", self.bufs,
            waits=[(_plan_scatter(n)[0], self.ss, self.rs, list(range(3 * n)), list(range(3 * n)))], after=after)
        return bufs[:n], bufs[n:]


class _Exchanges:
    def __init__(self, parity, order):
        self.parity, self.order = parity, order


def _plan_everyone():
    def plan(b, ss, rs):
        x, y, c = _position()
        return [_remote(b[0], b[1].at[4 * x + 2 * y + c], ss[k - 1], rs[k - 1],
                        (x ^ ((k >> 2) & 1), y ^ ((k >> 1) & 1), c ^ (k & 1))) for k in range(1, N_DEV)]
    return plan, N_DEV - 1


class _GatherSmall:
    def __init__(self, packed, after, name):
        self.name = name
        x, y, c = _position()
        placed = lax.dynamic_update_index_in_dim(lax.empty((N_DEV,) + packed.shape, packed.dtype), packed,
                                                 4 * x + 2 * y + c, 0)
        self.bufs, self.ss, self.rs, self.token = _split_call(
            name + "_start", [_in_hbm(packed), _in_hbm(placed)], starts=_plan_everyone(), after=after)

    def finish(self, after):
        everyone = list(range(N_DEV - 1))
        bufs, _, _, _ = _split_call(
            self.name + "_wait", self.bufs, waits=[(_plan_everyone()[0], self.ss, self.rs, everyone, everyone)],
            after=after)
        return bufs[1]


def _adamw_math(w, g, m, v):
    m = ADAM_B1 * m + (1.0 - ADAM_B1) * g
    v = ADAM_B2 * v + (1.0 - ADAM_B2) * (g * g)
    m_hat = m / (1.0 - ADAM_B1 ** ADAM_STEP)
    v_hat = v / (1.0 - ADAM_B2 ** ADAM_STEP)
    delta = -ADAM_LR * (m_hat / (jnp.sqrt(v_hat) + ADAM_EPS) + ADAM_WD * w)
    return delta, m, v


def _adamw_big_landed(w, m, v, parts, lands, slot, name, row0=0, into=None):
    R, C = w.shape
    rows = parts.shape[1]
    tr = _pick(rows, (256,))
    first = row0 // tr
    n_into = len(into) if into else 0

    def body(slot_ref, w_ref, m_ref, v_ref, own_ref, l1_ref, l2_ref, l3_ref, *rest):
        g = own_ref[...].astype(F32)
        for ref in (l1_ref, l2_ref, l3_ref):
            g = g + ref[...].astype(F32)
        for o_ref, res in zip(rest[n_into:], (g,) + _adamw_math(w_ref[...], g, m_ref[...], v_ref[...])):
            o_ref[...] = res

    blk = pl.BlockSpec((tr, C), lambda i, slot: (first + i, 0))

    def chip(k):
        return pl.BlockSpec((None, tr, C), lambda i, slot: ((slot[0] + k) % N_CHIP, i, 0))

    out = jax.ShapeDtypeStruct((R, C), F32)
    return pl.pallas_call(
        body, name=name,
        grid_spec=pltpu.PrefetchScalarGridSpec(
            num_scalar_prefetch=1, grid=(rows // tr,),
            in_specs=[blk, blk, blk, chip(0), chip(1), chip(2), chip(3)] + [ANY] * n_into,
            out_specs=[blk, blk, blk, blk]),
        out_shape=[out, out, out, out],
        input_output_aliases={8 + j: j for j in range(n_into)},
        compiler_params=_cparams(("parallel",)),
    )(slot, w, m, v, parts, lands, lands, lands, *(into or ()))


def _adamw_small(w, m, v, gathered, name):
    R = w.shape[0]

    def body(w_ref, m_ref, v_ref, p_ref, g_ref, d_ref, nm_ref, nv_ref):
        g = p_ref[0]
        for s in range(1, N_DEV):
            g = g + p_ref[s]
        d, nm, nv = _adamw_math(w_ref[...], g, m_ref[...], v_ref[...])
        g_ref[...] = g
        d_ref[...] = d
        nm_ref[...] = nm
        nv_ref[...] = nv

    out = jax.ShapeDtypeStruct((R, 128), F32)
    return pl.pallas_call(
        body, name=name, out_shape=[out, out, out, out],
    )(w, m, v, gathered)


SMALL_NAMES = ("lb_logits", "hg_norm_w", "rel_bias", "norm_mix_w", "norm_mlp_w", "norm_final_w")
SMALL_SHAPES = {"lb_logits": (2, HG_WIDTH), "hg_norm_w": (1, HG_DK), "rel_bias": (AT_HEADS, N_REL_PAD),
                "norm_mix_w": (1, D_MODEL), "norm_mlp_w": (1, D_MODEL), "norm_final_w": (1, D_MODEL)}


LOSS_ROW = sum(r * c for r, c in SMALL_SHAPES.values()) // 128


def _pack_small(parts, loss_row=None):
    rows = []
    for nme in SMALL_NAMES:
        p = parts[nme]
        if nme == "rel_bias":
            p = jnp.pad(p, ((0, 0), (0, N_REL_PAD - N_REL)))
        rows.append(p.reshape(-1, 128))
    if loss_row is not None:
        rows.append(loss_row)
    flat = jnp.concatenate(rows, axis=0)
    return jnp.pad(flat, ((0, SMALL_ROWS - flat.shape[0]), (0, 0)))


def _unpack_small(packed):
    out, at = {}, 0
    for nme in SMALL_NAMES:
        shp = SMALL_SHAPES[nme]
        nrow = shp[0] * shp[1] // 128
        p = packed[at:at + nrow].reshape(shp)
        at += nrow
        out[nme] = p[:, :N_REL] if nme == "rel_bias" else p
    return out


BIG_NAMES = ("w_in", "w_branch_a", "w_branch_b", "w_out", "w_up", "w_down")


def kernel(x, w_in, lb_logits, hg_norm_w, rel_bias, w_branch_a, w_branch_b, w_out, norm_mix_w, norm_mlp_w, w_up, w_down, norm_final_w, loss_target, m_w_in, m_lb_logits, m_hg_norm_w, m_rel_bias, m_w_branch_a, m_w_branch_b, m_w_out, m_norm_mix_w, m_norm_mlp_w, m_w_up, m_w_down, m_norm_final_w, v_w_in, v_lb_logits, v_hg_norm_w, v_rel_bias, v_w_branch_a, v_w_branch_b, v_w_out, v_norm_mix_w, v_norm_mlp_w, v_w_up, v_w_down, v_norm_final_w):
    big_w = [w_in[0], w_branch_a[0], w_branch_b[0], w_out[0], w_up[0], w_down[0]]
    big_m = [m_w_in[0], m_w_branch_a[0], m_w_branch_b[0], m_w_out[0], m_w_up[0], m_w_down[0]]
    big_v = [v_w_in[0], v_w_branch_a[0], v_w_branch_b[0], v_w_out[0], v_w_up[0], v_w_down[0]]

    shards = [w.astype(BF16) for w in big_w[:4]] + big_w[4:]
    parity = lax.axis_index("c").astype(jnp.int32).reshape(1)
    loss_part, grad_x, chip_parts, small = _local_step(
        x[0], loss_target[0], lb_logits, hg_norm_w, rel_bias[0], norm_mix_w, norm_mlp_w,
        norm_final_w.reshape(1, D_MODEL), shards[0], shards[1:], _Exchanges(parity, _gather_order()))
    (rs_in_lo, rs_in_hi), rs_mix, rs_up, rs_down = chip_parts
    slot =(2 * lax.axis_index("x") + lax.axis_index("y")).astype(jnp.int32).reshape(1)
    big = {}

    def finish(rs, names, after):
        sums, lands = rs.finish(after)
        for nme, own, land in zip(names, sums, lands):
            i = BIG_NAMES.index(nme)
            big[nme] = _adamw_big_landed(big_w[i], big_m[i], big_v[i], own, land, slot, "adamw_" + nme)
        return [big[nme][1] for nme in names]

    gather_small = _GatherSmall(_pack_small(small, loss_part[0:1]), [grad_x], "gather_small")
    done = finish(rs_down, ["w_down"], [grad_x, gather_small.token])
    done = finish(rs_up, ["w_up"], done)
    done = finish(rs_mix, ["w_branch_a", "w_branch_b", "w_out"], done)

    sw = dict(lb_logits=lb_logits, hg_norm_w=hg_norm_w, rel_bias=rel_bias[0], norm_mix_w=norm_mix_w,
              norm_mlp_w=norm_mlp_w, norm_final_w=norm_final_w.reshape(1, D_MODEL))
    sm = dict(lb_logits=m_lb_logits, hg_norm_w=m_hg_norm_w, rel_bias=m_rel_bias[0], norm_mix_w=m_norm_mix_w,
              norm_mlp_w=m_norm_mlp_w, norm_final_w=m_norm_final_w.reshape(1, D_MODEL))
    sv = dict(lb_logits=v_lb_logits, hg_norm_w=v_hg_norm_w, rel_bias=v_rel_bias[0], norm_mix_w=v_norm_mix_w,
              norm_mlp_w=v_norm_mlp_w, norm_final_w=v_norm_final_w.reshape(1, D_MODEL))

    (own,), (land,) = rs_in_lo.finish(done)
    lo = _adamw_big_landed(big_w[0], big_m[0], big_v[0], own, land, slot, "adamw_w_in_lo")
    gathered = gather_small.finish([lo[1]])
    small_packed = _adamw_small(_pack_small(sw), _pack_small(sm), _pack_small(sv), gathered, "adamw_small")
    loss = small_packed[0][LOSS_ROW, 0]
    small_out = [_unpack_small(p) for p in small_packed]
    (own,), (land,) = rs_in_hi.finish([small_packed[0]])
    big["w_in"] = _adamw_big_landed(big_w[0], big_m[0], big_v[0], own, land, slot, "adamw_w_in_hi",
                                    row0=D_MODEL // 2, into=lo)

    def leaf(kind, nme):
        if nme in BIG_NAMES:
            return big[nme][kind][None]
        p = small_out[kind][nme]
        if nme == "rel_bias":
            return p[None]
        if nme == "norm_final_w":
            return p.reshape(D_MODEL)
        return p

    order = ("w_in", "lb_logits", "hg_norm_w", "rel_bias", "w_branch_a", "w_branch_b", "w_out", "norm_mix_w",
             "norm_mlp_w", "w_up", "w_down", "norm_final_w")
    outs = [loss, grad_x[None]]
    for kind in range(4):
        outs += [leaf(kind, nme) for nme in order]
    return tuple(outs)
```

```python
import jax
import jax.numpy as jnp
from jax import lax
from jax.experimental import pallas as pl
from jax.experimental.pallas import tpu as pltpu

F32 = jnp.float32
BF16 = jnp.bfloat16
HIGHEST = lax.Precision.HIGHEST
MESH = pl.DeviceIdType.MESH

D_MODEL = 2048
HG_HEADS = 8
HG_DK = 128
HG_WIDTH = 1024
AT_HEADS = 16
AT_DH = 64
AT_WIDTH = 1024
CHUNK = 64
LEFT_CHUNKS = 8
BAND = (LEFT_CHUNKS + 1) * CHUNK
PAD = LEFT_CHUNKS * CHUNK
REL_CLIP = 256
N_REL = 2 * REL_CLIP + 1
N_REL_PAD = 640
D_FF = 4 * D_MODEL
EPS = 1e-6
N_DEV = 8
N_CHIP = 4

ADAM_LR = 0.001
ADAM_B1 = 0.9
ADAM_B2 = 0.999
ADAM_EPS = 1e-08
ADAM_WD = 0.01
ADAM_STEP = 10

COL_HQ, COL_HF, COL_HI, COL_HG = 0, 8, 16, 24
COL_AQ, COL_AK, COL_AV = 32, 40, 48
COL_GATE_A, COL_GATE_B = 7, 9

VMEM_LIMIT = 56 * 1024 * 1024
SMALL_ROWS = 152


def _cparams(sem=None, **kw):
    if sem is not None:
        kw["dimension_semantics"] = sem
    return pltpu.CompilerParams(vmem_limit_bytes=VMEM_LIMIT, **kw)


def _pick(n, cands):
    for c in cands:
        if n % c == 0:
            return c
    return n


def _sigmoid(x):
    return 1.0 / (1.0 + jnp.exp(-x))


ANY = pl.BlockSpec(memory_space=pl.ANY)


def _position():
    return lax.axis_index("x"), lax.axis_index("y"), lax.axis_index("c")


def _call(body, args, *, name, grid, in_specs, out_specs, out_shape, scratch_shapes=(), sem=None, after=(),
          aliases=None):
    n_in = len(args)

    def ordered(*refs):
        body(*refs[:n_in], *refs[n_in + len(after):])

    return list(pl.pallas_call(
        ordered if after else body, name=name, grid=grid, in_specs=list(in_specs) + [ANY] * len(after),
        out_specs=out_specs, out_shape=out_shape, scratch_shapes=list(scratch_shapes),
        input_output_aliases=aliases or {}, compiler_params=_cparams(sem))(*args, *after))


MAX_CONTRACTION_TILE = 4096


def _accumulate(part, acc_ref, step, n_steps, finish):
    if n_steps == 1:
        finish(part)
        return

    @pl.when(step == 0)
    def _():
        acc_ref[...] = part

    @pl.when(step > 0)
    def _():
        acc_ref[...] += part

    @pl.when(step == n_steps - 1)
    def _():
        finish(acc_ref[...])


def _mm_nn(a, wb, out_dtype, name, after=(), epilogue=None, blocks=None, into=()):
    M, K = a.shape
    NB, K2, Nb = wb.shape
    assert K == K2
    j0, nj = blocks or (0, NB)
    n_into = len(into)
    tm = min(M, 1024)
    tk = min(K, MAX_CONTRACTION_TILE)
    tn = _pick(Nb, (512, 1408, 256))
    nk = K // tk
    nn = Nb // tn
    extra, first_cols, out_dtypes, fn = epilogue or ((), (), (out_dtype,), lambda total: (total,))
    n_extra, n_out = len(extra), len(out_dtypes)

    def body(a_ref, b_ref, *rest):
        def finish(total):
            results = fn(total, *[r[...] for r in rest[:n_extra]])
            for o_ref, res, dt in zip(rest[n_extra + n_into:n_extra + n_into + n_out], results, out_dtypes):
                o_ref[...] = res.astype(dt)

        part = jnp.dot(a_ref[...], b_ref[...], preferred_element_type=F32)
        _accumulate(part, rest[-1], pl.program_id(3), nk, finish)

    def tile(first):
        return pl.BlockSpec((tm, tn), lambda m, j, n, k: (m, first + (j0 + j) * nn + n))

    outs = _call(
        body, (a, wb) + tuple(extra) + tuple(into), name=name, grid=(M // tm, nj, nn, nk),
        in_specs=[pl.BlockSpec((tm, tk), lambda m, j, n, k: (m, k)),
                  pl.BlockSpec((None, tk, tn), lambda m, j, n, k: (j0 + j, k, n))]
        + [tile(col // tn) for col in first_cols] + [ANY] * n_into,
        out_specs=[tile(0)] * n_out,
        out_shape=[jax.ShapeDtypeStruct((M, NB * Nb), dt) for dt in out_dtypes],
        scratch_shapes=[] if nk == 1 else [pltpu.VMEM((tm, tn), F32)],
        sem=("parallel", "parallel", "parallel", "arbitrary"), after=after,
        aliases={2 + n_extra + i: i for i in range(n_into)})
    return outs if epilogue else outs[0]


def _squared_relu(a):
    ra = jnp.maximum(a, 0.0)
    return a, ra * ra


def _gated_merge(pb, za, zb, pa):
    return pb, _sigmoid(za) * pa + _sigmoid(zb) * pb


def _mm_nt(a, wb, out_dtype, name, after=(), epilogue=None):
    M, N = a.shape
    NB, K, Nb = wb.shape
    assert N == NB * Nb
    tm = min(M, 1024)
    n_tiles_live = 1 + (len(epilogue[0]) + len(epilogue[2]) if epilogue else 0)
    tko = _pick(K, (1024,)) if n_tiles_live <= 3 else _pick(K, (512,))
    tc = _pick(Nb, (2048, 1024, 1408, 256))
    nc = Nb // tc
    jb = max([d for d in (8, 4, 2, 1) if NB % d == 0 and d * tc <= MAX_CONTRACTION_TILE]) if nc == 1 else 1
    nsteps = (NB // jb) * nc
    extra, first_cols, out_dtypes, fn = epilogue or ((), (), (out_dtype,), lambda total: (total,))
    n_extra, n_out = len(extra), len(out_dtypes)

    def body(a_ref, b_ref, *rest):
        def finish(total):
            results = fn(total, *[r[...] for r in rest[:n_extra]])
            for o_ref, res, dt in zip(rest[n_extra:n_extra + n_out], results, out_dtypes):
                o_ref[...] = res.astype(dt)

        part = sum(lax.dot_general(a_ref[:, i * tc:(i + 1) * tc], b_ref[i], (((1,), (1,)), ((), ())),
                                   preferred_element_type=F32) for i in range(jb))
        _accumulate(part, rest[-1], pl.program_id(2) * nc + pl.program_id(3), nsteps, finish)

    def tile(first):
        return pl.BlockSpec((tm, tko), lambda m, ko, j, c: (m, first + ko))

    outs = _call(
        body, (a, wb) + tuple(extra), name=name,
        grid=(M // tm, K // tko, NB // jb, nc),
        in_specs=[pl.BlockSpec((tm, jb * tc), lambda m, ko, j, c: (m, j * nc + c)),
                  pl.BlockSpec((jb, tko, tc), lambda m, ko, j, c: (j, ko, c))] + [tile(col // tko) for col in first_cols],
        out_specs=[tile(0)] * n_out,
        out_shape=[jax.ShapeDtypeStruct((M, K), dt) for dt in out_dtypes],
        scratch_shapes=[] if nsteps == 1 else [pltpu.VMEM((tm, tko), F32)],
        sem=("parallel", "parallel", "arbitrary", "arbitrary"), after=after)
    return outs if epilogue else outs[0]


ROWS_TILE = 512
ROWS_PIECE = 128


def _mm_rows(a, w, extras, vectors, row_dtypes, fn, name):
    M, K = a.shape
    N = w.shape[1]
    tm = min(M, ROWS_TILE)
    n_e, n_v = len(extras), len(vectors)

    def body(a_ref, w_ref, *rest):
        tiles, vecs, outs, product_ref = rest[:n_e], rest[n_e:n_e + n_v], rest[n_e + n_v:-1], rest[-1]
        product_ref[...] = jnp.dot(a_ref[...], w_ref[...], preferred_element_type=F32)
        for i in range(tm // ROWS_PIECE):
            piece = slice(i * ROWS_PIECE, (i + 1) * ROWS_PIECE)
            results = fn(product_ref[piece, :], *[t[piece, :] for t in tiles], *[v[...] for v in vecs])
            for o_ref, res, dt in zip(outs, results, row_dtypes):
                o_ref[piece, :] = res.astype(dt)

    row = pl.BlockSpec((tm, N), lambda m: (m, 0))
    return _call(
        body, (a, w) + tuple(extras) + tuple(vectors), name=name, grid=(M // tm,),
        in_specs=[pl.BlockSpec((tm, K), lambda m: (m, 0)), pl.BlockSpec((K, N), lambda m: (0, 0))]
        + [row] * n_e + [pl.BlockSpec((1, N), lambda m: (0, 0))] * n_v,
        out_specs=[row] * len(row_dtypes),
        out_shape=[jax.ShapeDtypeStruct((M, N), dt) for dt in row_dtypes],
        scratch_shapes=[pltpu.VMEM((tm, N), F32)], sem=("parallel",))


def _rms(h, w):
    return h * lax.rsqrt(jnp.mean(h * h, axis=-1, keepdims=True) + EPS) * w


def _residual_rms_rows(mix, x, w):
    h = x + mix
    return h, _rms(h, w)


def _mm_tn_half(a, g, which, blocks_on, add, name, after=(), a_cols=None):
    M, Ka = a.shape
    N = g.shape[1]
    first_col = 0
    if a_cols is not None:
        first_col, Ka = a_cols
    if blocks_on == "g":
        rows, cols = _pick(Ka, (1024,)), N // N_DEV
        tn = _pick(cols, (512, 1408, 256))
        nn = cols // tn
        grid = (Ka // rows, N_CHIP, nn)
        a_spec = pl.BlockSpec((M, rows), lambda ka, s, n, w: (0, first_col // rows + ka))
        g_spec = pl.BlockSpec((M, tn), lambda ka, s, n, w: (0, (2 * s + w[0]) * nn + n))
        out_rows = Ka
    else:
        rows, cols = Ka // N_DEV, N
        tn = _pick(cols, (2048, 512))
        nn = cols // tn
        grid = (1, N_CHIP, nn)
        a_spec = pl.BlockSpec((M, rows), lambda ka, s, n, w: (0, 2 * s + w[0]))
        g_spec = pl.BlockSpec((M, tn), lambda ka, s, n, w: (0, n))
        out_rows = rows
    o_spec = pl.BlockSpec((None, rows, tn), lambda ka, s, n, w: (s, ka, n))
    n_add = 0 if add is None else 1

    def body(which_ref, a_ref, g_ref, *rest):
        acc = lax.dot_general(a_ref[...], g_ref[...], (((0,), (0,)), ((), ())), preferred_element_type=F32)
        if n_add:
            acc = acc + rest[0][...].astype(F32)
        rest[-1][...] = acc.astype(BF16)

    return pl.pallas_call(
        body, name=name,
        grid_spec=pltpu.PrefetchScalarGridSpec(
            num_scalar_prefetch=1, grid=grid,
            in_specs=[a_spec, g_spec] + [o_spec] * n_add + [ANY] * len(after),
            out_specs=o_spec),
        out_shape=jax.ShapeDtypeStruct((N_CHIP, out_rows, cols), BF16),
        compiler_params=_cparams(("parallel", "parallel", "parallel")),
    )(which, a, g, *(() if add is None else (add,)), *after)


ROW_TILE = 256


def _rms_fwd(x, w, name):
    T, Dm = x.shape

    def body(x_ref, w_ref, u_ref):
        xv = x_ref[...]
        r = lax.rsqrt(jnp.mean(xv * xv, axis=-1, keepdims=True) + EPS)
        u_ref[...] = (xv * r * w_ref[...]).astype(BF16)

    return pl.pallas_call(
        body, name=name, grid=(T // ROW_TILE,),
        in_specs=[pl.BlockSpec((ROW_TILE, Dm), lambda i: (i, 0)), pl.BlockSpec((1, Dm), lambda i: (0, 0))],
        out_specs=pl.BlockSpec((ROW_TILE, Dm), lambda i: (i, 0)),
        out_shape=jax.ShapeDtypeStruct((T, Dm), BF16),
        compiler_params=_cparams(("parallel",)),
    )(x, w)


def _loss_head(h1, mlp, wf, target, name):
    T, Dm = h1.shape

    def body(h_ref, m_ref, w_ref, t_ref, loss_ref, dh_ref, dhb_ref, dw_ref):
        i = pl.program_id(0)
        h = h_ref[...] + m_ref[...]
        r = lax.rsqrt(jnp.mean(h * h, axis=-1, keepdims=True) + EPS)
        xh = h * r
        wv = w_ref[...]
        e = xh * wv - t_ref[...]
        part = 0.5 * jnp.sum(jnp.mean(e * e, axis=-1, keepdims=True), axis=0, keepdims=True)
        dy = e * (1.0 / Dm)
        dw = jnp.sum(dy * xh, axis=0, keepdims=True)
        gy = dy * wv
        dh = r * (gy - xh * jnp.mean(gy * xh, axis=-1, keepdims=True))
        dh_ref[...] = dh
        dhb_ref[...] = dh.astype(BF16)

        @pl.when(i == 0)
        def _():
            loss_ref[...] = jnp.zeros_like(loss_ref)
            dw_ref[...] = jnp.zeros_like(dw_ref)

        loss_ref[...] += jnp.broadcast_to(part, loss_ref.shape)
        dw_ref[...] += dw

    row = pl.BlockSpec((ROW_TILE, Dm), lambda i: (i, 0))
    vec = pl.BlockSpec((1, Dm), lambda i: (0, 0))
    return pl.pallas_call(
        body, name=name, grid=(T // ROW_TILE,),
        in_specs=[row, row, vec, row],
        out_specs=[pl.BlockSpec((8, 128), lambda i: (0, 0)), row, row, vec],
        out_shape=[jax.ShapeDtypeStruct((8, 128), F32), jax.ShapeDtypeStruct((T, Dm), F32),
                   jax.ShapeDtypeStruct((T, Dm), BF16), jax.ShapeDtypeStruct((1, Dm), F32)],
        compiler_params=_cparams(("arbitrary",)),
    )(h1, mlp, wf, target)


def _rms_bwd(dyn, x, w, dres, dx_dtypes, name, after=()):
    T, Dm = x.shape
    n_dx = len(dx_dtypes)

    def body(g_ref, x_ref, w_ref, r_ref, *outs):
        i = pl.program_id(0)
        xv = x_ref[...]
        r = lax.rsqrt(jnp.mean(xv * xv, axis=-1, keepdims=True) + EPS)
        xh = xv * r
        g = g_ref[...]
        dw = jnp.sum(g * xh, axis=0, keepdims=True)
        gy = g * w_ref[...]
        dx = r_ref[...] + r * (gy - xh * jnp.mean(gy * xh, axis=-1, keepdims=True))
        for dx_ref, dt in zip(outs, dx_dtypes):
            dx_ref[...] = dx.astype(dt)
        dw_ref = outs[n_dx]

        @pl.when(i == 0)
        def _():
            dw_ref[...] = jnp.zeros_like(dw_ref)

        dw_ref[...] += dw

    row = pl.BlockSpec((ROW_TILE, Dm), lambda i: (i, 0))
    vec = pl.BlockSpec((1, Dm), lambda i: (0, 0))
    return _call(
        body, (dyn, x, w, dres), name=name, grid=(T // ROW_TILE,),
        in_specs=[row, row, vec, row],
        out_specs=[row] * n_dx + [vec],
        out_shape=[jax.ShapeDtypeStruct((T, Dm), dt) for dt in dx_dtypes] + [jax.ShapeDtypeStruct((1, Dm), F32)],
        sem=("arbitrary",), after=after)


GATE_TILE = 1024


def _merge_grads(d, za, zb, pa, pb):
    ga = _sigmoid(za)
    gb = _sigmoid(zb)
    return d * ga, d * gb, d * pa * ga * (1.0 - ga), d * pb * gb * (1.0 - gb)


def _dot_hi(a, b, dims):
    return lax.dot_general(a, b, (dims, ((), ())), precision=HIGHEST, preferred_element_type=F32)


NN = ((1,), (0,))
NT = ((1,), (1,))
TN = ((0,), (0,))


def _hg_gates(hq, hf, lb):
    sq = _sigmoid(hq)
    q = hq * sq * (HG_DK ** -0.5)
    f = _sigmoid(hf)
    g = lb + (1.0 - lb) * f
    return q, sq, f, g, jnp.log(g), 1.0 - g


def _tri(lower):
    r = lax.broadcasted_iota(jnp.int32, (CHUNK, CHUNK), 0)
    c = lax.broadcasted_iota(jnp.int32, (CHUNK, CHUNK), 1)
    return jnp.where((r >= c) if lower else (r <= c), 1.0, 0.0).astype(BF16)


def _running_sum(tri, x):
    return sum(jnp.dot(tri, piece, preferred_element_type=F32) for piece in _split3(x))


GROUP = 16
N_GROUPS = CHUNK // GROUP
BWD_CHUNKS_PER_TRIP = 4


def _dot_bf16(a, b, dims):
    return lax.dot_general(a.astype(BF16), b.astype(BF16), (dims, ((), ())), preferred_element_type=F32)


def _rows_iota():
    return lax.broadcasted_iota(jnp.int32, (CHUNK, HG_DK), 0)


def _by_query_group(q, kk, b, g):
    r0 = GROUP * g
    b0 = b[r0:r0 + 1]
    decay = jnp.exp(b[r0:r0 + GROUP] - b0)
    ks = jnp.where(_rows_iota() < r0, kk * jnp.exp(jnp.minimum(b0 - b, 0.0)), 0.0)
    return q[r0:r0 + GROUP] * decay, ks, decay


def _by_key_group(q, kk, b, j):
    r1 = GROUP * (j + 1)
    b1 = b[r1 - 1:r1]
    decay = jnp.exp(b1 - b[r1 - GROUP:r1])
    qs = jnp.where(_rows_iota() >= r1, q * jnp.exp(jnp.minimum(b - b1, 0.0)), 0.0)
    return qs, kk[r1 - GROUP:r1] * decay, decay


def _scores_between_groups(q, kk, b):
    blocks = [jnp.zeros((GROUP, CHUNK), F32)]
    for g in range(1, N_GROUPS):
        qs, ks, _ = _by_query_group(q, kk, b, g)
        blocks.append(_dot_bf16(qs, ks, NT))
    return jnp.concatenate(blocks, axis=0)


def _hgrn2_fwd(z, lb_logits, hg_norm_w, name, after=()):
    T = z.shape[0]
    n_chunks = T // CHUNK

    def body(hq_ref, hf_ref, hi_ref, hg_ref, lbl_ref, nw_ref, o_ref, ya_ref, sall_ref, st_ref):
        lbl = lbl_ref[...]
        lb = 1.0 / (1.0 + jnp.exp(lbl[1:2, :] - lbl[0:1, :]))
        st_ref[...] = jnp.zeros_like(st_ref)
        tri = _tri(True)
        row8 = lax.broadcasted_iota(jnp.int32, (8, HG_DK), 0)

        def chunk(c, carry):
            rows = pl.ds(pl.multiple_of(c * CHUNK, CHUNK), CHUNK)
            q, _, _, _, lg, kk = _hg_gates(hq_ref[rows, :], hf_ref[rows, :], lb)
            v = hi_ref[rows, :]
            b = _running_sum(tri, lg)
            st = st_ref[...]
            sall_ref[c] = st
            for grp in range(N_GROUPS):
                r0 = GROUP * grp
                for h8 in range(GROUP // 8):
                    n = 8 * (h8 + 1)
                    bs, ks, vs = b[r0:r0 + n], kk[r0:r0 + n], v[r0:r0 + n]
                    sidx = lax.broadcasted_iota(jnp.int32, (n, HG_DK), 0)
                    blk = jnp.zeros((8, HG_DK), F32)
                    for i in range(8):
                        t = r0 + 8 * h8 + i
                        e = jnp.where(sidx <= 8 * h8 + i, jnp.exp(b[t:t + 1] - bs), 0.0)
                        p = jnp.sum(e * ks * q[t:t + 1], axis=1, keepdims=True)
                        ot = jnp.sum(p * vs, axis=0, keepdims=True)
                        blk = blk + jnp.where(row8 == i, ot, 0.0)
                    o_ref[pl.ds(pl.multiple_of(c * CHUNK + r0 + 8 * h8, 8), 8), :] = blk
            o_ref[rows, :] += _dot_hi(q * jnp.exp(b), st, NT) + _dot_bf16(_scores_between_groups(q, kk, b), v, NN)
            bl = b[CHUNK - 1:CHUNK]
            ke = kk * jnp.exp(bl - b)
            st_ref[...] = st * jnp.exp(bl) + _dot_hi(v, ke, TN)
            return carry

        lax.fori_loop(0, n_chunks, chunk, 0, unroll=2)
        o = o_ref[...]
        r = lax.rsqrt(jnp.mean(o * o, axis=-1, keepdims=True) + EPS)
        hg = hg_ref[...]
        ya_ref[...] = (o * r * nw_ref[...] * (hg * _sigmoid(hg))).astype(BF16)

    def col(base):
        return pl.BlockSpec((T, HG_DK), lambda h: (0, base + h))

    return _call(
        body, (z, z, z, z, lb_logits, hg_norm_w), name=name, grid=(HG_HEADS,),
        in_specs=[col(COL_HQ), col(COL_HF), col(COL_HI), col(COL_HG),
                  pl.BlockSpec((2, HG_DK), lambda h: (0, h)), pl.BlockSpec((1, HG_DK), lambda h: (0, 0))],
        out_specs=[col(0), col(0), pl.BlockSpec((None, n_chunks, HG_DK, HG_DK), lambda h: (h, 0, 0, 0))],
        out_shape=[jax.ShapeDtypeStruct((T, HG_WIDTH), F32), jax.ShapeDtypeStruct((T, HG_WIDTH), BF16),
                   jax.ShapeDtypeStruct((HG_HEADS, n_chunks, HG_DK, HG_DK), F32)],
        scratch_shapes=[pltpu.VMEM((HG_DK, HG_DK), F32)],
        sem=("parallel",), after=after)


def _hgrn2_bwd(z, lb_logits, hg_norm_w, o_raw, s_all, dya, name, after=()):
    T = z.shape[0]
    n_chunks = T // CHUNK

    def body(hq_ref, hf_ref, hi_ref, hg_ref, lbl_ref, nw_ref, o_ref, sall_ref, dya_ref,
             dhq_ref, dhf_ref, dhi_ref, dhg_ref, dlbl_ref, dnw_ref,
             do_ref, dst_ref, dlb_ref, *per_chunk):
        h = pl.program_id(0)
        lbl = lbl_ref[...]
        lb = 1.0 / (1.0 + jnp.exp(lbl[1:2, :] - lbl[0:1, :]))

        o = o_ref[...]
        r = lax.rsqrt(jnp.mean(o * o, axis=-1, keepdims=True) + EPS)
        oh = o * r
        nw = nw_ref[...]
        hg = hg_ref[...]
        sg = _sigmoid(hg)
        dy = dya_ref[...]
        d_on = dy * (hg * sg)
        dhg_ref[...] = (dy * (oh * nw) * (sg * (1.0 + hg * (1.0 - sg)))).astype(BF16)
        dnw = jnp.sum(d_on * oh, axis=0, keepdims=True)
        gy = d_on * nw
        do_ref[...] = r * (gy - oh * jnp.mean(gy * oh, axis=-1, keepdims=True))

        @pl.when(h == 0)
        def _():
            dnw_ref[...] = jnp.zeros_like(dnw_ref)

        dnw_ref[...] += jnp.broadcast_to(dnw, dnw_ref.shape)

        dst_ref[...] = jnp.zeros_like(dst_ref)
        dlb_ref[...] = jnp.zeros_like(dlb_ref)
        tri = _tri(True)
        tri_t = _tri(False)
        row8 = lax.broadcasted_iota(jnp.int32, (8, HG_DK), 0)
        row_group = lax.broadcasted_iota(jnp.int32, (CHUNK, CHUNK), 0) // GROUP
        col_group = lax.broadcasted_iota(jnp.int32, (CHUNK, CHUNK), 1) // GROUP
        earlier_group = col_group < row_group
        later_group = col_group > row_group

        def chunk(c, dq_ref, dk_ref, dv_ref, q_vm, k_vm, b_vm):
            first = pl.multiple_of(c * CHUNK, CHUNK)
            rows = pl.ds(first, CHUNK)
            hq = hq_ref[rows, :]
            q, sq, f, g, lg, kk = _hg_gates(hq, hf_ref[rows, :], lb)
            v = hi_ref[rows, :]
            do = do_ref[rows, :]
            b = _running_sum(tri, lg)
            eb = jnp.exp(b)
            bl = b[CHUNK - 1:CHUNK]
            ebl = jnp.exp(bl)
            ekb = jnp.exp(bl - b)
            qe = q * eb
            ke = kk * ekb
            st = sall_ref[c]
            dst = dst_ref[...]
            dqe = _dot_bf16(do, st, NN)
            dke = _dot_bf16(v, dst, NN)
            dv_inter = _dot_bf16(ke, dst, NT)
            d_ebl = jnp.sum(st * dst, axis=0, keepdims=True)
            dst_ref[...] = dst * ebl + _dot_bf16(do, qe, TN)

            q_vm[...] = q
            k_vm[...] = kk
            b_vm[...] = b
            dk_ref[...] = jnp.zeros_like(dk_ref)
            dv_ref[...] = jnp.zeros_like(dv_ref)
            for grp in range(N_GROUPS):
                r0 = GROUP * grp
                for h8 in range(GROUP // 8):
                    n = 8 * (h8 + 1)
                    bs, ks, vs = b_vm[r0:r0 + n, :], k_vm[r0:r0 + n, :], hi_ref[pl.ds(first + r0, n), :]
                    sidx = lax.broadcasted_iota(jnp.int32, (n, HG_DK), 0)
                    blk = jnp.zeros((8, HG_DK), F32)
                    for i in range(8):
                        t = r0 + 8 * h8 + i
                        qt = q_vm[t:t + 1, :]
                        dot_ = do_ref[pl.ds(first + t, 1), :]
                        e = jnp.where(sidx <= 8 * h8 + i, jnp.exp(b_vm[t:t + 1, :] - bs), 0.0)
                        w = e * ks
                        p = jnp.sum(w * qt, axis=1, keepdims=True)
                        dsc = jnp.sum(vs * dot_, axis=1, keepdims=True)
                        dqt = jnp.sum(dsc * w, axis=0, keepdims=True)
                        blk = blk + jnp.where(row8 == i, dqt, 0.0)
                        dk_ref[r0:r0 + n, :] += dsc * e * qt
                        dv_ref[r0:r0 + n, :] += p * dot_
                    dq_ref[r0 + 8 * h8:r0 + n, :] = blk
            ds_far = jnp.where(earlier_group, _dot_bf16(do, v, NT), 0.0)
            ds_far_t = jnp.where(later_group, _dot_bf16(v, do, NT), 0.0)
            dq_far, dk_far = [jnp.zeros((GROUP, HG_DK), F32)], []
            for grp in range(1, N_GROUPS):
                r0 = GROUP * grp
                _, ks, decay = _by_query_group(q, kk, b, grp)
                dq_far.append(decay * _dot_hi(ds_far[r0:r0 + GROUP], ks, NN))
                qs, _, decay = _by_key_group(q, kk, b, grp - 1)
                dk_far.append(decay * _dot_hi(ds_far_t[r0 - GROUP:r0], qs, NN))
            dk_far.append(jnp.zeros((GROUP, HG_DK), F32))
            dv_far = _dot_bf16(_scores_between_groups(q, kk, b), do, TN)
            dq_i = dq_ref[...] + jnp.concatenate(dq_far, axis=0)
            dk_i = dk_ref[...] + jnp.concatenate(dk_far, axis=0)
            dke_ke = dke * ke
            db = q * dq_i - kk * dk_i + dqe * qe - dke_ke
            db_last = jnp.sum(dke_ke, axis=0, keepdims=True) + d_ebl * ebl
            dlg = _running_sum(tri_t, db) + db_last
            dq = dq_i + dqe * eb
            dkk = dk_i + dke * ekb
            dg = dlg / g - dkk
            dhq_ref[rows, :] = (dq * (HG_DK ** -0.5) * (sq * (1.0 + hq * (1.0 - sq)))).astype(BF16)
            dhf_ref[rows, :] = (dg * (1.0 - lb) * f * (1.0 - f)).astype(BF16)
            dhi_ref[rows, :] = (dv_ref[...] + dv_far + dv_inter).astype(BF16)
            dlb_ref[...] += jnp.sum(dg * (1.0 - f), axis=0, keepdims=True)

        def trip(i, carry):
            for k in range(BWD_CHUNKS_PER_TRIP):
                chunk(n_chunks - 1 - k - BWD_CHUNKS_PER_TRIP * i, *per_chunk[6 * k:6 * k + 6])
            return carry

        lax.fori_loop(0, n_chunks // BWD_CHUNKS_PER_TRIP, trip, 0)
        dl0 = dlb_ref[...] * lb * (1.0 - lb)
        dlbl_ref[0:1, :] = dl0
        dlbl_ref[1:2, :] = -dl0

    def col(base):
        return pl.BlockSpec((T, HG_DK), lambda h: (0, base + h))

    outb = jax.ShapeDtypeStruct((T, HG_WIDTH), BF16)
    return _call(
        body, (z, z, z, z, lb_logits, hg_norm_w, o_raw, s_all, dya), name=name, grid=(HG_HEADS,),
        in_specs=[col(COL_HQ), col(COL_HF), col(COL_HI), col(COL_HG),
                  pl.BlockSpec((2, HG_DK), lambda h: (0, h)), pl.BlockSpec((1, HG_DK), lambda h: (0, 0)),
                  col(0), pl.BlockSpec((None, n_chunks, HG_DK, HG_DK), lambda h: (h, 0, 0, 0)), col(0)],
        out_specs=[col(0), col(0), col(0), col(0), pl.BlockSpec((2, HG_DK), lambda h: (0, h)),
                   pl.BlockSpec((8, HG_DK), lambda h: (0, 0))],
        out_shape=[outb, outb, outb, outb, jax.ShapeDtypeStruct((2, HG_WIDTH), F32),
                   jax.ShapeDtypeStruct((8, HG_DK), F32)],
        scratch_shapes=[pltpu.VMEM((T, HG_DK), F32), pltpu.VMEM((HG_DK, HG_DK), F32), pltpu.VMEM((1, HG_DK), F32)]
        + [pltpu.VMEM((CHUNK, HG_DK), F32)] * (6 * BWD_CHUNKS_PER_TRIP),
        sem=("arbitrary",), after=after)


CONST_KEYS = PAD - REL_CLIP
VAR_KEYS = BAND - CONST_KEYS
REL_LO = 128
REL_SPAN = N_REL_PAD - REL_LO


def _rel_onehot(t):
    r = lax.broadcasted_iota(jnp.int32, (REL_SPAN, VAR_KEYS), 0)
    j = lax.broadcasted_iota(jnp.int32, (REL_SPAN, VAR_KEYS), 1)
    idx = jnp.clip(t + PAD - CONST_KEYS - j, -REL_CLIP, REL_CLIP) + REL_CLIP - REL_LO
    return jnp.where(r == idx, 1.0, 0.0).astype(BF16)


def _split3(x):
    hi = x.astype(BF16)
    r1 = x - hi.astype(F32)
    mid = r1.astype(BF16)
    return hi, mid, (r1 - mid.astype(F32)).astype(BF16)


def _bias_expand(rel, name):
    rows = 8

    def body(rel_ref, out_ref):
        tab = rel_ref[...]
        pieces = _split3(tab[:, REL_LO:N_REL_PAD])
        constant = jnp.broadcast_to(tab[:, 2 * REL_CLIP:2 * REL_CLIP + 1], (AT_HEADS, CONST_KEYS))
        for i in range(rows):
            onehot = _rel_onehot(pl.program_id(0) * rows + i)
            out_ref[i, :, 0:CONST_KEYS] = constant
            out_ref[i, :, CONST_KEYS:BAND] = sum(jnp.dot(piece, onehot, preferred_element_type=F32)
                                                 for piece in pieces)

    return pl.pallas_call(
        body, name=name, grid=(CHUNK // rows,),
        in_specs=[pl.BlockSpec((AT_HEADS, N_REL_PAD), lambda t: (0, 0))],
        out_specs=pl.BlockSpec((rows, AT_HEADS, BAND), lambda t: (t, 0, 0)),
        out_shape=jax.ShapeDtypeStruct((CHUNK, AT_HEADS, BAND), F32),
        compiler_params=_cparams(("parallel",)),
    )(rel)


def _bias_reduce(dbias_rows, name, after=()):
    def body(db_ref, out_ref):
        lane = lax.broadcasted_iota(jnp.int32, (AT_HEADS, N_REL_PAD), 1)
        varying = lane >= CONST_KEYS
        by_offset = jnp.zeros((AT_HEADS, N_REL_PAD), F32)
        constant = jnp.zeros((AT_HEADS, N_REL_PAD), F32)
        for t in range(CHUNK):
            row = db_ref[t]
            constant = constant + jnp.where(varying, 0.0, row)
            moved = jnp.where(varying, row, 0.0)
            by_offset = by_offset + (pltpu.roll(moved, N_REL_PAD - t, axis=1) if t else moved)
        offset = lax.broadcasted_iota(jnp.int32, (N_REL_PAD, N_REL_PAD), 0)
        entry = lax.broadcasted_iota(jnp.int32, (N_REL_PAD, N_REL_PAD), 1)
        onehot = jnp.where(entry == jnp.clip(PAD - offset, -REL_CLIP, REL_CLIP) + REL_CLIP, 1.0, 0.0).astype(BF16)
        acc = sum(jnp.dot(piece, onehot, preferred_element_type=F32) for piece in _split3(by_offset))
        last = jnp.sum(constant, axis=1, keepdims=True)
        out_ref[...] = acc + jnp.where(lane == 2 * REL_CLIP, last, 0.0)

    whole = pl.BlockSpec((CHUNK, AT_HEADS, N_REL_PAD), lambda i: (0, 0, 0))
    return _call(
        body, (dbias_rows,), name=name, grid=(1,), in_specs=[whole],
        out_specs=[pl.BlockSpec((AT_HEADS, N_REL_PAD), lambda i: (0, 0))],
        out_shape=[jax.ShapeDtypeStruct((AT_HEADS, N_REL_PAD), F32)],
        sem=("arbitrary",), after=after)[0]


def _pair_lanes():
    return lax.broadcasted_iota(jnp.int32, (CHUNK, 2 * AT_DH), 1) < AT_DH


def _block_diag(a):
    first = _pair_lanes()
    return jnp.concatenate([jnp.where(first, a, 0.0), jnp.where(first, 0.0, a)], axis=0).astype(BF16)


def _diag_blocks(a):
    return jnp.where(_pair_lanes(), a[:CHUNK], a[CHUNK:])


def _band_probs_t(kb, qbd, bias_t, c):
    s = lax.dot_general(kb, qbd, (NT, ((), ())), preferred_element_type=F32) * (AT_DH ** -0.5) + bias_t
    j = lax.broadcasted_iota(jnp.int32, (BAND, 2 * AT_DH), 0)
    s = jnp.where(j + c * CHUNK >= PAD, s, -jnp.inf)
    p = jnp.exp(s - jnp.max(s, axis=0, keepdims=True))
    return p / jnp.sum(p, axis=0, keepdims=True)


def _attn_fwd(z, bias_t, name, after=()):
    T = z.shape[0]
    n_chunks = T // CHUNK

    def body(q_ref, k_ref, v_ref, bias_ref, y_ref, p_ref, *scratch):
        for pr in range(2):
            lanes = slice(128 * pr, 128 * (pr + 1))
            for dst_ref, src_ref in zip(scratch[2 * pr:2 * pr + 2], (k_ref, v_ref)):
                dst_ref[0:PAD, :] = jnp.zeros((PAD, 128), BF16)
                dst_ref[PAD:PAD + T, :] = src_ref[:, lanes].astype(BF16)

        def chunk(c, carry):
            rows = pl.ds(pl.multiple_of(c * CHUNK, CHUNK), CHUNK)
            band = pl.ds(pl.multiple_of(c * CHUNK, CHUNK), BAND)
            for pr in range(2):
                kp_ref, vp_ref = scratch[2 * pr:2 * pr + 2]
                lanes = slice(128 * pr, 128 * (pr + 1))
                p = _band_probs_t(kp_ref[band, :], _block_diag(q_ref[rows, lanes]), bias_ref[pr], c).astype(BF16)
                p_ref[pr, c] = p
                o2 = lax.dot_general(p, vp_ref[band, :], (TN, ((), ())), preferred_element_type=F32)
                y_ref[rows, lanes] = _diag_blocks(o2).astype(BF16)
            return carry

        lax.fori_loop(0, n_chunks, chunk, 0, unroll=2)

    def col(base):
        return pl.BlockSpec((T, 256), lambda h: (0, base // 2 + h))

    return _call(
        body, (z, z, z, bias_t), name=name, grid=(AT_HEADS // 4,),
        in_specs=[col(COL_AQ), col(COL_AK), col(COL_AV), pl.BlockSpec((2, BAND, 128), lambda h: (h, 0, 0))],
        out_specs=[col(0), pl.BlockSpec((2, n_chunks, BAND, 128), lambda h: (h, 0, 0, 0))],
        out_shape=[jax.ShapeDtypeStruct((T, AT_WIDTH), BF16),
                   jax.ShapeDtypeStruct((AT_HEADS // 2, n_chunks, BAND, 128), BF16)],
        scratch_shapes=[pltpu.VMEM((PAD + T, 128), BF16)] * 4,
        sem=("parallel",), after=after)


def _attn_bwd(z, probs, dyb, name, after=()):
    T = z.shape[0]
    n_chunks = T // CHUNK

    def body(q_ref, k_ref, v_ref, p_ref, dy_ref, dq_ref, dk_ref, dv_ref, dbias_ref, *scratch):
        dbias_ref[...] = jnp.zeros_like(dbias_ref)
        for pr in range(2):
            kp_ref, vp_ref, dkp_ref, dvp_ref = scratch[4 * pr:4 * pr + 4]
            lanes = slice(128 * pr, 128 * (pr + 1))
            kp_ref[0:PAD, :] = jnp.zeros((PAD, 128), BF16)
            vp_ref[0:PAD, :] = jnp.zeros((PAD, 128), BF16)
            kp_ref[PAD:PAD + T, :] = k_ref[:, lanes].astype(BF16)
            vp_ref[PAD:PAD + T, :] = v_ref[:, lanes].astype(BF16)
            dkp_ref[...] = jnp.zeros_like(dkp_ref)
            dvp_ref[...] = jnp.zeros_like(dvp_ref)

        def chunk(c, carry):
            rows = pl.ds(pl.multiple_of(c * CHUNK, CHUNK), CHUNK)
            band = pl.ds(pl.multiple_of(c * CHUNK, CHUNK), BAND)
            for pr in range(2):
                kp_ref, vp_ref, dkp_ref, dvp_ref = scratch[4 * pr:4 * pr + 4]
                lanes = slice(128 * pr, 128 * (pr + 1))
                qbd = _block_diag(q_ref[rows, lanes])
                dobd = _block_diag(dy_ref[rows, lanes])
                pb = p_ref[pr, c]
                p = pb.astype(F32)
                dp = lax.dot_general(vp_ref[band, :], dobd, (NT, ((), ())), preferred_element_type=F32)
                ds = p * (dp - jnp.sum(dp * p, axis=0, keepdims=True))
                dbias_ref[pr] += ds
                dsb = ds.astype(BF16)
                dq2 = lax.dot_general(dsb, kp_ref[band, :], (TN, ((), ())), preferred_element_type=F32)
                dq_ref[rows, lanes] = (_diag_blocks(dq2) * (AT_DH ** -0.5)).astype(BF16)
                dkp_ref[band, :] += jnp.dot(dsb, qbd, preferred_element_type=F32) * (AT_DH ** -0.5)
                dvp_ref[band, :] += jnp.dot(pb, dobd, preferred_element_type=F32)
            return carry

        lax.fori_loop(0, n_chunks, chunk, 0)
        for pr in range(2):
            lanes = slice(128 * pr, 128 * (pr + 1))
            dk_ref[:, lanes] = scratch[4 * pr + 2][PAD:PAD + T, :].astype(BF16)
            dv_ref[:, lanes] = scratch[4 * pr + 3][PAD:PAD + T, :].astype(BF16)

    def col(base):
        return pl.BlockSpec((T, 256), lambda h: (0, base // 2 + h))

    outb = jax.ShapeDtypeStruct((T, AT_WIDTH), BF16)
    return _call(
        body, (z, z, z, probs, dyb), name=name, grid=(AT_HEADS // 4,),
        in_specs=[col(COL_AQ), col(COL_AK), col(COL_AV),
                  pl.BlockSpec((2, n_chunks, BAND, 128), lambda h: (h, 0, 0, 0)), col(0)],
        out_specs=[col(0), col(0), col(0), pl.BlockSpec((2, BAND, 128), lambda h: (h, 0, 0))],
        out_shape=[outb, outb, outb, jax.ShapeDtypeStruct((AT_HEADS // 2, BAND, 128), F32)],
        scratch_shapes=([pltpu.VMEM((PAD + T, 128), BF16)] * 2 + [pltpu.VMEM((PAD + T, 128), F32)] * 2) * 2,
        sem=("parallel",), after=after)


def _local_step(x, target, lb_logits, hg_norm_w, rel_bias, norm_mix_w, norm_mlp_w, norm_final_w,
                w_in, rest, exchanges=None):
    ex = exchanges
    rel = jnp.pad(rel_bias, ((0, 0), (0, N_REL_PAD - N_REL)))

    u = _rms_fwd(x, norm_mix_w, "rms_mix_fwd")
    if ex:
        z, w_in = _mm_gathered(u, w_in, ex.order, "mm_in_fwd")
        gather = _Gather(rest[:3], [w_in], "ag")
        mlp_shards, _ = lax.optimization_barrier((rest[3:], gather.token))
        gather_mlp = _Gather([s.astype(BF16) for s in mlp_shards], [gather.token], "ag_mlp")
        z = _mm_gathered_tail(u, w_in, z, ex.order, "mm_in_fwd_tail", after=[gather_mlp.token])
        tok = []
    else:
        z = _mm_nn(u, w_in, F32, "mm_in_fwd")
        w_a, w_b, w_out, w_up, w_down = rest
        tok = []
    o_raw, y_a, s_all = _hgrn2_fwd(z, lb_logits, hg_norm_w, "hgrn2_fwd", after=tok)
    if ex:
        tok = [gather.pass_on([0, 1, 2], [o_raw], "abo")]
    bias_rows = _bias_expand(rel, "bias_expand")
    bias_t = jnp.transpose(bias_rows.reshape(CHUNK, AT_HEADS // 2, 2, BAND), (1, 3, 2, 0)).reshape(
        AT_HEADS // 2, BAND, 2 * CHUNK)
    y_b, probs = _attn_fwd(z, bias_t, "attn_fwd", after=tok)
    if ex:
        tok = [gather_mlp.pass_on([0], [y_b], "up")]
        w_a, w_b, w_out = gather.finish([0, 1, 2], tok, "abo")
    pa = _mm_nn(y_a, w_a, F32, "mm_a_fwd")
    pb, merged = _mm_nn(y_b, w_b, None, "mm_b_fwd", epilogue=(
        (z, z, pa), (COL_GATE_A * GATE_TILE, COL_GATE_B * GATE_TILE, 0), (F32, BF16), _gated_merge))
    w_out1 = w_out.reshape(1, D_MODEL, D_MODEL)
    h1, u2 = _mm_rows(merged, w_out.reshape(D_MODEL, D_MODEL), [x], [norm_mlp_w], (F32, BF16),
                      _residual_rms_rows, "mm_out_fwd")
    if ex:
        w_up, = gather_mlp.finish([0], [u2], "up")
    act = ((), (), (F32, BF16), _squared_relu)
    a, r = _mm_nn(u2, w_up, None, "mm_up_fwd_first", epilogue=act, blocks=(0, N_DEV // 2))
    tok = [gather_mlp.pass_on([1], [r], "down")] if ex else []
    a, r = _mm_nn(u2, w_up, None, "mm_up_fwd_second", after=tok, epilogue=act, blocks=(N_DEV // 2, N_DEV // 2),
                  into=(a, r))
    if ex:
        w_down, = gather_mlp.finish([1], [r], "down")
    w_down1 = w_down.reshape(1, D_FF, D_MODEL)
    mlp = _mm_nn(r, w_down1, F32, "mm_down_fwd")
    loss, dh2, dh2b, g_nf = _loss_head(h1, mlp, norm_final_w, target, "loss_head")

    own = ex.parity if ex else jnp.zeros((1,), jnp.int32)

    def sibling_half(weights, name, after=()):
        others = [_mm_tn_half(a_, g_, 1 - own, on, None, nm + "_sibling", after, *cols)
                  for a_, g_, on, nm, *cols in weights]
        rs = _ReduceScatter(others, name) if ex else None
        return rs, others, ([rs.token] if ex else [])

    def own_half(rs, weights, others, after):
        landed = rs.from_sibling(after) if ex else [None] * len(weights)
        sums = [_mm_tn_half(a_, g_, own, on, l, nm + "_own", (), *cols)
                for (a_, g_, on, nm, *cols), l in zip(weights, landed)]
        if ex:
            return [rs.scatter(sums)], None
        return [], [jnp.stack([s_, o_], axis=1).reshape((N_DEV,) + s_.shape[1:]) for s_, o_ in zip(sums, others)]

    down = [(r, dh2b, "a", "mm_down_wgrad")]
    rs_down, others, tok = sibling_half(down, "rs_down")
    da, = _mm_nt(dh2b, w_down1, None, "mm_down_dgrad", after=tok, epilogue=(
        (a,), (0,), (BF16,), lambda dr, av: (dr * (2.0 * jnp.maximum(av, 0.0)),)))
    tok, g_down = own_half(rs_down, down, others, [da])
    up = [(u2, da, "g", "mm_up_wgrad")]
    rs_up, others, tok = sibling_half(up, "rs_up", tok)
    du2 = _mm_nt(da, w_up, F32, "mm_up_dgrad", after=tok)
    tok, g_up = own_half(rs_up, up, others, [du2])
    dh1, dh1b, g_nmlp = _rms_bwd(du2, h1, norm_mlp_w, dh2, (F32, BF16), "rms_mlp_bwd", after=tok)

    dpa, dpb, dga, dgb = _mm_nt(dh1b, w_out1, None, "mm_out_dgrad", epilogue=(
        (z, z, pa, pb), (COL_GATE_A * GATE_TILE, COL_GATE_B * GATE_TILE, 0, 0), (BF16,) * 4, _merge_grads))
    mix = [(y_a, dpa, "g", "mm_a_wgrad"), (y_b, dpb, "g", "mm_b_wgrad"), (merged, dh1b, "a", "mm_out_wgrad")]
    rs_mix, others, tok = sibling_half(mix, "rs_mix")
    dya = _mm_nt(dpa, w_a, F32, "mm_a_dgrad", after=tok)
    dyb = _mm_nt(dpb, w_b, F32, "mm_b_dgrad", after=tok)
    tok, g_mix = own_half(rs_mix, mix, others, [dya, dyb])
    daq, dak, dav, dbias_t = _attn_bwd(z, probs, dyb, "attn_bwd", after=tok)
    dhq, dhf, dhi, dhg, g_lbl, g_hgw = _hgrn2_bwd(z, lb_logits, hg_norm_w, o_raw, s_all, dya, "hgrn2_bwd",
                                                  after=tok)
    dbias_rows = jnp.pad(jnp.transpose(dbias_t.reshape(AT_HEADS // 2, BAND, 2, CHUNK), (3, 0, 2, 1)).reshape(
        CHUNK, AT_HEADS, BAND), ((0, 0), (0, 0), (0, N_REL_PAD - BAND)))
    dz = jnp.concatenate([dhq, dhf, dhi, dhg, daq, dak, dav, dga, dgb], axis=1)
    half = D_MODEL // 2
    lo = [(u, dz, "g", "mm_in_wgrad_lo", (0, half))]
    hi = [(u, dz, "g", "mm_in_wgrad_hi", (half, half))]
    rs_in_lo, others_lo, tok = sibling_half(lo, "rs_in_lo")
    rs_in_hi, others_hi, tok = sibling_half(hi, "rs_in_hi", tok)
    tok, g_in_lo = own_half(rs_in_lo, lo, others_lo, tok)
    du = _mm_nt(dz, w_in, F32, "mm_in_dgrad", after=tok)
    tok, g_in_hi = own_half(rs_in_hi, hi, others_hi, [du])
    grad_x, g_nmix = _rms_bwd(du, x, norm_mix_w, dh1, (F32,), "rms_mix_bwd", after=tok)
    g_rel = _bias_reduce(dbias_rows, "bias_reduce", after=tok)[:, :N_REL]

    small = dict(lb_logits=g_lbl, hg_norm_w=g_hgw[0:1], rel_bias=g_rel, norm_mix_w=g_nmix, norm_mlp_w=g_nmlp,
                 norm_final_w=g_nf)
    if ex:
        grads = [(rs_in_lo, rs_in_hi), rs_mix, rs_up, rs_down]
    else:
        grads = [jnp.concatenate([g_in_lo[0], g_in_hi[0]], axis=1)] + g_mix + [g_up[0], g_down[0]]
    return loss, grad_x, grads, small


def _mm_gathered(u, shard, order, name):
    T, K = u.shape
    _, Nb = shard.shape

    def body(order_ref, u_ref, shard_ref, z_ref, full_ref, wbuf, load_sem, send_sems, recv_sems, local_sem):
        s = pl.program_id(0)
        x, y, c = _position()
        me, sibling = (x, y, c), (x, y, 1 - c)
        chips = [(1 - x, y), (x, 1 - y), (1 - x, 1 - y)]

        def copy(k, block, to, src=None):
            dst = full_ref.at[4 * block[0] + 2 * block[1] + block[2]]
            return pltpu.make_async_remote_copy(
                src_ref=dst if src is None else src, dst_ref=dst,
                send_sem=send_sems.at[k], recv_sem=recv_sems.at[k], device_id=to, device_id_type=MESH)

        @pl.when(s == 0)
        def _():
            local = pltpu.make_async_copy(shard_ref, full_ref.at[4 * x + 2 * y + c], local_sem)
            local.start()
            copy(0, me, sibling, src=shard_ref).start()
            for j, chip in enumerate(chips):
                copy(1 + j, me, (*chip, c), src=shard_ref).start()
            local.wait()

        @pl.when(s == 1)
        def _():
            copy(0, sibling, me).wait_recv()

        for j, chip in enumerate(chips):
            direct, passed = ((2, 4), (3, 5), (6, 7))[j]

            @pl.when(s == direct)
            def _(j=j, chip=chip):
                copy(1 + j, (*chip, c), me).wait_recv()
                copy(4 + j, (*chip, c), sibling).start()

            @pl.when(s == passed)
            def _(j=j, chip=chip):
                copy(4 + j, (*chip, 1 - c), me).wait_recv()

        @pl.when(s < N_EARLY_BLOCKS)
        def _():
            load = pltpu.make_async_copy(full_ref.at[order_ref[s]], wbuf, load_sem)
            load.start()
            load.wait()
            z_ref[...] = jnp.dot(u_ref[...], wbuf[...], preferred_element_type=F32)

        @pl.when(s == N_DEV - 1)
        def _():
            for k in range(7):
                copy(k, me, sibling).wait_send()

    z, full = pl.pallas_call(
        body, name=name,
        grid_spec=pltpu.PrefetchScalarGridSpec(
            num_scalar_prefetch=1, grid=(N_DEV,),
            in_specs=[pl.BlockSpec((T, K), lambda s, order: (0, 0)), ANY],
            out_specs=[pl.BlockSpec((T, Nb), lambda s, order: (0, order[jnp.minimum(s, N_EARLY_BLOCKS - 1)])), ANY],
            scratch_shapes=[pltpu.VMEM((K, Nb), BF16), pltpu.SemaphoreType.DMA,
                            pltpu.SemaphoreType.DMA((7,)), pltpu.SemaphoreType.DMA((7,)), pltpu.SemaphoreType.DMA]),
        out_shape=[jax.ShapeDtypeStruct((T, N_DEV * Nb), F32), jax.ShapeDtypeStruct((N_DEV, K, Nb), BF16)],
        compiler_params=_cparams(("arbitrary",)),
    )(order, u, shard)
    return z, full


N_EARLY_BLOCKS = 6


def _mm_gathered_tail(u, full, z, order, name, after=()):
    T, K = u.shape
    _, _, Nb = full.shape
    n_after = len(after)

    def body(order_ref, u_ref, w_ref, z_in_ref, *rest):
        rest[n_after][...] = jnp.dot(u_ref[...], w_ref[...], preferred_element_type=F32)

    return pl.pallas_call(
        body, name=name,
        grid_spec=pltpu.PrefetchScalarGridSpec(
            num_scalar_prefetch=1, grid=(N_DEV - N_EARLY_BLOCKS,),
            in_specs=[pl.BlockSpec((T, K), lambda s, order: (0, 0)),
                      pl.BlockSpec((None, K, Nb), lambda s, order: (order[N_EARLY_BLOCKS + s], 0, 0)), ANY]
            + [ANY] * n_after,
            out_specs=pl.BlockSpec((T, Nb), lambda s, order: (0, order[N_EARLY_BLOCKS + s]))),
        out_shape=jax.ShapeDtypeStruct(z.shape, z.dtype),
        input_output_aliases={3: 0},
        compiler_params=_cparams(("arbitrary",)),
    )(order, u, full, z, *after)


def _gather_order():
    x, y, c = _position()
    chips = [(1 - x, y), (x, 1 - y), (1 - x, 1 - y)]
    ids = [4 * x + 2 * y + c, 4 * x + 2 * y + (1 - c)]
    ids += [4 * cx + 2 * cy + c for cx, cy in chips[:2]] + [4 * cx + 2 * cy + (1 - c) for cx, cy in chips[:2]]
    ids += [4 * chips[2][0] + 2 * chips[2][1] + c, 4 * chips[2][0] + 2 * chips[2][1] + (1 - c)]
    return jnp.stack(ids).astype(jnp.int32)


HBM = pl.BlockSpec(memory_space=pltpu.HBM)
SEM = pl.BlockSpec(memory_space=pltpu.SEMAPHORE)
DATAFLOW = pltpu.SideEffectType.DATAFLOW_SIDE_EFFECTING


def _split_call(name, bufs, waits=(), starts=None, after=()):
    nb = len(bufs)
    n_new = starts[1] if starts else 0
    wait_sems = [s for w in waits for s in (*w[1], *w[2])]

    def body(*refs):
        b, pos = refs[:nb], nb
        for plan, ss, _, send_idx, recv_idx in waits:
            k = len(ss)
            copies = plan(b, refs[pos:pos + k], refs[pos + k:pos + 2 * k])
            pos += 2 * k
            for i in recv_idx:
                copies[i].wait_recv()
            for i in send_idx:
                copies[i].wait_send()
        outs = refs[pos + len(after):]
        if starts:
            for cp in starts[0](b, outs[nb:nb + n_new], outs[nb + n_new:nb + 2 * n_new]):
                cp.start()
        outs[-1][...] = jnp.zeros_like(outs[-1])

    res = pl.pallas_call(
        body, name=name,
        out_shape=tuple(pltpu.HBM(a.shape, a.dtype) for a in bufs) + (pltpu.SemaphoreType.DMA(()),) * (2 * n_new)
        + (jax.ShapeDtypeStruct((8, 128), F32),),
        in_specs=[HBM] * nb + [SEM] * len(wait_sems) + [ANY] * len(after),
        out_specs=(HBM,) * nb + (SEM,) * (2 * n_new) + (pl.BlockSpec(memory_space=pltpu.VMEM),),
        input_output_aliases={i: i for i in range(nb)},
        compiler_params=pltpu.CompilerParams(has_side_effects=DATAFLOW),
    )(*bufs, *wait_sems, *after)
    return list(res[:nb]), list(res[nb:nb + n_new]), list(res[nb + n_new:nb + 2 * n_new]), res[-1]


def _in_hbm(a):
    return pltpu.with_memory_space_constraint(a, pltpu.HBM)


def _remote(src, dst, send_sem, recv_sem, to):
    return pltpu.make_async_remote_copy(src_ref=src, dst_ref=dst, send_sem=send_sem, recv_sem=recv_sem,
                                        device_id=to, device_id_type=MESH)


def _other_chips():
    x, y, _ = _position()
    return [(1 - x, y), (x, 1 - y), (1 - x, 1 - y)]


def _plan_gather_first(n):
    def plan(b, ss, rs):
        x, y, c = _position()
        to = [(x, y, 1 - c)] + [(*chip, c) for chip in _other_chips()]
        return [_remote(b[w], b[n + w].at[4 * x + 2 * y + c], ss[4 * w + k], rs[4 * w + k], to[k])
                for w in range(n) for k in range(4)]
    return plan, 4 * n


def _plan_gather_pass(n):
    def plan(b, ss, rs):
        x, y, c = _position()
        copies = []
        for w in range(n):
            for j, chip in enumerate(_other_chips()):
                blk = b[n + w].at[4 * chip[0] + 2 * chip[1] + c]
                copies.append(_remote(blk, blk, ss[3 * w + j], rs[3 * w + j], (x, y, 1 - c)))
        return copies
    return plan, 3 * n


def _plan_sibling(n):
    def plan(b, ss, rs):
        x, y, c = _position()
        return [_remote(b[w].at[s], b[n + w].at[s], ss[4 * w + s], rs[4 * w + s], (x, y, 1 - c))
                for w in range(n) for s in range(N_CHIP)]
    return plan, 4 * n


def _plan_scatter(n):
    def plan(b, ss, rs):
        x, y, c = _position()
        return [_remote(b[w].at[2 * chip[0] + chip[1]], b[n + w].at[2 * x + y], ss[3 * w + j], rs[3 * w + j],
                        (*chip, c))
                for w in range(n) for j, chip in enumerate(_other_chips())]
    return plan, 3 * n


class _Gather:
    def __init__(self, shards, after, name):
        self.n, self.name = len(shards), name
        x, y, c = _position()
        placed = [lax.dynamic_update_index_in_dim(lax.empty((N_DEV,) + s.shape, s.dtype), s, 4 * x + 2 * y + c, 0)
                  for s in shards]
        bufs, self.ss, self.rs, self.token = _split_call(
            name + "_start", [_in_hbm(a) for a in list(shards) + placed], starts=_plan_gather_first(self.n),
            after=after)
        self.shards, self.fulls = bufs[:self.n], bufs[self.n:]
        self.passed = {}

    def _sub(self, ids, sems, per):
        return [sems[per * w + k] for w in ids for k in range(per)]

    def pass_on(self, ids, after, tag):
        m = len(ids)
        first = (_plan_gather_first(m)[0], self._sub(ids, self.ss, 4), self._sub(ids, self.rs, 4),
                 [], [4 * i + k for i in range(m) for k in (1, 2, 3)])
        bufs, ss, rs, token = _split_call(
            "%s_pass_%s" % (self.name, tag), [self.shards[w] for w in ids] + [self.fulls[w] for w in ids],
            waits=[first], starts=_plan_gather_pass(m), after=after)
        for i, w in enumerate(ids):
            self.shards[w], self.fulls[w] = bufs[i], bufs[m + i]
        self.passed[tuple(ids)] = (ss, rs)
        return token

    def finish(self, ids, after, tag):
        m = len(ids)
        ss2, rs2 = self.passed[tuple(ids)]
        first = (_plan_gather_first(m)[0], self._sub(ids, self.ss, 4), self._sub(ids, self.rs, 4),
                 list(range(4 * m)), [4 * i for i in range(m)])
        passed = (_plan_gather_pass(m)[0], ss2, rs2, list(range(3 * m)), list(range(3 * m)))
        bufs, _, _, _ = _split_call(
            "%s_finish_%s" % (self.name, tag), [self.shards[w] for w in ids] + [self.fulls[w] for w in ids],
            waits=[first, passed], after=after)
        return bufs[m:]


class _ReduceScatter:
    def __init__(self, others, name):
        self.n, self.name = len(others), name
        lands = [lax.empty(g.shape, g.dtype) for g in others]
        self.bufs, self.ss, self.rs, self.token = _split_call(
            name + "_sibling_start", [_in_hbm(a) for a in list(others) + lands], starts=_plan_sibling(self.n))

    def from_sibling(self, after):
        n = self.n
        bufs, _, _, _ = _split_call(
            self.name + "_sibling_wait", self.bufs,
            waits=[(_plan_sibling(n)[0], self.ss, self.rs, list(range(4 * n)), list(range(4 * n)))], after=after)
        return bufs[n:]

    def scatter(self, sums):
        lands = [lax.empty(s.shape, s.dtype) for s in sums]
        self.bufs, self.ss, self.rs, token = _split_call(
            self.name + "_scatter_start", [_in_hbm(a) for a in list(sums) + lands], starts=_plan_scatter(self.n))
        return token

    def finish(self, after):
        n = self.n
        bufs, _, _, _ = _split_call(
            self.name + "_scatter_wait", self.bufs,
            waits=[(_plan_scatter(n)[0], self.ss, self.rs, list(range(3 * n)), list(range(3 * n)))], after=after)
        return bufs[:n], bufs[n:]


class _Exchanges:
    def __init__(self, parity, order):
        self.parity, self.order = parity, order


def _plan_everyone():
    def plan(b, ss, rs):
        x, y, c = _position()
        return [_remote(b[0], b[1].at[4 * x + 2 * y + c], ss[k - 1], rs[k - 1],
                        (x ^ ((k >> 2) & 1), y ^ ((k >> 1) & 1), c ^ (k & 1))) for k in range(1, N_DEV)]
    return plan, N_DEV - 1


class _GatherSmall:
    def __init__(self, packed, after, name):
        self.name = name
        x, y, c = _position()
        placed = lax.dynamic_update_index_in_dim(lax.empty((N_DEV,) + packed.shape, packed.dtype), packed,
                                                 4 * x + 2 * y + c, 0)
        self.bufs, self.ss, self.rs, self.token = _split_call(
            name + "_start", [_in_hbm(packed), _in_hbm(placed)], starts=_plan_everyone(), after=after)

    def finish(self, after):
        everyone = list(range(N_DEV - 1))
        bufs, _, _, _ = _split_call(
            self.name + "_wait", self.bufs, waits=[(_plan_everyone()[0], self.ss, self.rs, everyone, everyone)],
            after=after)
        return bufs[1]


def _adamw_math(w, g, m, v):
    m = ADAM_B1 * m + (1.0 - ADAM_B1) * g
    v = ADAM_B2 * v + (1.0 - ADAM_B2) * (g * g)
    m_hat = m / (1.0 - ADAM_B1 ** ADAM_STEP)
    v_hat = v / (1.0 - ADAM_B2 ** ADAM_STEP)
    delta = -ADAM_LR * (m_hat / (jnp.sqrt(v_hat) + ADAM_EPS) + ADAM_WD * w)
    return delta, m, v


def _adamw_big_landed(w, m, v, parts, lands, slot, name, row0=0, into=None):
    R, C = w.shape
    rows = parts.shape[1]
    tr = _pick(rows, (256,))
    first = row0 // tr
    n_into = len(into) if into else 0

    def body(slot_ref, w_ref, m_ref, v_ref, own_ref, l1_ref, l2_ref, l3_ref, *rest):
        g = own_ref[...].astype(F32)
        for ref in (l1_ref, l2_ref, l3_ref):
            g = g + ref[...].astype(F32)
        for o_ref, res in zip(rest[n_into:], (g,) + _adamw_math(w_ref[...], g, m_ref[...], v_ref[...])):
            o_ref[...] = res

    blk = pl.BlockSpec((tr, C), lambda i, slot: (first + i, 0))

    def chip(k):
        return pl.BlockSpec((None, tr, C), lambda i, slot: ((slot[0] + k) % N_CHIP, i, 0))

    out = jax.ShapeDtypeStruct((R, C), F32)
    return pl.pallas_call(
        body, name=name,
        grid_spec=pltpu.PrefetchScalarGridSpec(
            num_scalar_prefetch=1, grid=(rows // tr,),
            in_specs=[blk, blk, blk, chip(0), chip(1), chip(2), chip(3)] + [ANY] * n_into,
            out_specs=[blk, blk, blk, blk]),
        out_shape=[out, out, out, out],
        input_output_aliases={8 + j: j for j in range(n_into)},
        compiler_params=_cparams(("parallel",)),
    )(slot, w, m, v, parts, lands, lands, lands, *(into or ()))


def _adamw_small(w, m, v, gathered, name):
    R = w.shape[0]

    def body(w_ref, m_ref, v_ref, p_ref, g_ref, d_ref, nm_ref, nv_ref):
        g = p_ref[0]
        for s in range(1, N_DEV):
            g = g + p_ref[s]
        d, nm, nv = _adamw_math(w_ref[...], g, m_ref[...], v_ref[...])
        g_ref[...] = g
        d_ref[...] = d
        nm_ref[...] = nm
        nv_ref[...] = nv

    out = jax.ShapeDtypeStruct((R, 128), F32)
    return pl.pallas_call(
        body, name=name, out_shape=[out, out, out, out],
    )(w, m, v, gathered)


SMALL_NAMES = ("lb_logits", "hg_norm_w", "rel_bias", "norm_mix_w", "norm_mlp_w", "norm_final_w")
SMALL_SHAPES = {"lb_logits": (2, HG_WIDTH), "hg_norm_w": (1, HG_DK), "rel_bias": (AT_HEADS, N_REL_PAD),
                "norm_mix_w": (1, D_MODEL), "norm_mlp_w": (1, D_MODEL), "norm_final_w": (1, D_MODEL)}


LOSS_ROW = sum(r * c for r, c in SMALL_SHAPES.values()) // 128


def _pack_small(parts, loss_row=None):
    rows = []
    for nme in SMALL_NAMES:
        p = parts[nme]
        if nme == "rel_bias":
            p = jnp.pad(p, ((0, 0), (0, N_REL_PAD - N_REL)))
        rows.append(p.reshape(-1, 128))
    if loss_row is not None:
        rows.append(loss_row)
    flat = jnp.concatenate(rows, axis=0)
    return jnp.pad(flat, ((0, SMALL_ROWS - flat.shape[0]), (0, 0)))


def _unpack_small(packed):
    out, at = {}, 0
    for nme in SMALL_NAMES:
        shp = SMALL_SHAPES[nme]
        nrow = shp[0] * shp[1] // 128
        p = packed[at:at + nrow].reshape(shp)
        at += nrow
        out[nme] = p[:, :N_REL] if nme == "rel_bias" else p
    return out


BIG_NAMES = ("w_in", "w_branch_a", "w_branch_b", "w_out", "w_up", "w_down")


def kernel(x, w_in, lb_logits, hg_norm_w, rel_bias, w_branch_a, w_branch_b, w_out, norm_mix_w, norm_mlp_w, w_up, w_down, norm_final_w, loss_target, m_w_in, m_lb_logits, m_hg_norm_w, m_rel_bias, m_w_branch_a, m_w_branch_b, m_w_out, m_norm_mix_w, m_norm_mlp_w, m_w_up, m_w_down, m_norm_final_w, v_w_in, v_lb_logits, v_hg_norm_w, v_rel_bias, v_w_branch_a, v_w_branch_b, v_w_out, v_norm_mix_w, v_norm_mlp_w, v_w_up, v_w_down, v_norm_final_w):
    big_w = [w_in[0], w_branch_a[0], w_branch_b[0], w_out[0], w_up[0], w_down[0]]
    big_m = [m_w_in[0], m_w_branch_a[0], m_w_branch_b[0], m_w_out[0], m_w_up[0], m_w_down[0]]
    big_v = [v_w_in[0], v_w_branch_a[0], v_w_branch_b[0], v_w_out[0], v_w_up[0], v_w_down[0]]

    shards = [w.astype(BF16) for w in big_w[:4]] + big_w[4:]
    parity = lax.axis_index("c").astype(jnp.int32).reshape(1)
    loss_part, grad_x, chip_parts, small = _local_step(
        x[0], loss_target[0], lb_logits, hg_norm_w, rel_bias[0], norm_mix_w, norm_mlp_w,
        norm_final_w.reshape(1, D_MODEL), shards[0], shards[1:], _Exchanges(parity, _gather_order()))
    (rs_in_lo, rs_in_hi), rs_mix, rs_up, rs_down = chip_parts
    slot =(2 * lax.axis_index("x") + lax.axis_index("y")).astype(jnp.int32).reshape(1)
    big = {}

    def finish(rs, names, after):
        sums, lands = rs.finish(after)
        for nme, own, land in zip(names, sums, lands):
            i = BIG_NAMES.index(nme)
            big[nme] = _adamw_big_landed(big_w[i], big_m[i], big_v[i], own, land, slot, "adamw_" + nme)
        return [big[nme][1] for nme in names]

    gather_small = _GatherSmall(_pack_small(small, loss_part[0:1]), [grad_x], "gather_small")
    done = finish(rs_down, ["w_down"], [grad_x, gather_small.token])
    done = finish(rs_up, ["w_up"], done)
    done = finish(rs_mix, ["w_branch_a", "w_branch_b", "w_out"], done)

    sw = dict(lb_logits=lb_logits, hg_norm_w=hg_norm_w, rel_bias=rel_bias[0], norm_mix_w=norm_mix_w,
              norm_mlp_w=norm_mlp_w, norm_final_w=norm_final_w.reshape(1, D_MODEL))
    sm = dict(lb_logits=m_lb_logits, hg_norm_w=m_hg_norm_w, rel_bias=m_rel_bias[0], norm_mix_w=m_norm_mix_w,
              norm_mlp_w=m_norm_mlp_w, norm_final_w=m_norm_final_w.reshape(1, D_MODEL))
    sv = dict(lb_logits=v_lb_logits, hg_norm_w=v_hg_norm_w, rel_bias=v_rel_bias[0], norm_mix_w=v_norm_mix_w,
              norm_mlp_w=v_norm_mlp_w, norm_final_w=v_norm_final_w.reshape(1, D_MODEL))

    (own,), (land,) = rs_in_lo.finish(done)
    lo = _adamw_big_landed(big_w[0], big_m[0], big_v[0], own, land, slot, "adamw_w_in_lo")
    gathered = gather_small.finish([lo[1]])
    small_packed = _adamw_small(_pack_small(sw), _pack_small(sm), _pack_small(sv), gathered, "adamw_small")
    loss = small_packed[0][LOSS_ROW, 0]
    small_out = [_unpack_small(p) for p in small_packed]
    (own,), (land,) = rs_in_hi.finish([small_packed[0]])
    big["w_in"] = _adamw_big_landed(big_w[0], big_m[0], big_v[0], own, land, slot, "adamw_w_in_hi",
                                    row0=D_MODEL // 2, into=lo)

    def leaf(kind, nme):
        if nme in BIG_NAMES:
            return big[nme][kind][None]
        p = small_out[kind][nme]
        if nme == "rel_bias":
            return p[None]
        if nme == "norm_final_w":
            return p.reshape(D_MODEL)
        return p

    order = ("w_in", "lb_logits", "hg_norm_w", "rel_bias", "w_branch_a", "w_branch_b", "w_out", "norm_mix_w",
             "norm_mlp_w", "w_up", "w_down", "norm_final_w")
    outs = [loss, grad_x[None]]
    for kind in range(4):
        outs += [leaf(kind, nme) for nme in order]
    return tuple(outs)
```
